```python
import jax, jax.numpy as jnp
from jax import lax
import numpy as np

D_MODEL = 1024
BATCH = 16
SEQ = 2048
DEPTH = 2

CHUNK = 64
N_MIXERS = 2
EXPAND = 2
D_INNER = EXPAND * D_MODEL
SG_BLOCK = 128
SG_GROUPS = 8
SG_GROUP_DIM = D_INNER // SG_GROUPS
HG_HEADS = 16
HG_HEAD_DIM = D_INNER // HG_HEADS
N_LAYERS_A = (DEPTH + 1) // 2
N_LAYERS_B = DEPTH // 2
EPS = 1e-6

kernel_name = "hybrid_gmlp_hgrn2_adaln_trunk"


def rms_norm(x, gain):
    xf = x.astype(jnp.float32)
    y = xf * lax.rsqrt(jnp.mean(xf * xf, axis=-1, keepdims=True) + EPS)
    return (y * gain.astype(jnp.float32)).astype(x.dtype)


def layer_norm(x, gain, bias):
    xf = x.astype(jnp.float32)
    mu = jnp.mean(xf, axis=-1, keepdims=True)
    var = jnp.mean(jnp.square(xf - mu), axis=-1, keepdims=True)
    y = (xf - mu) * lax.rsqrt(var + EPS) * gain.astype(jnp.float32) + bias.astype(jnp.float32)
    return y.astype(x.dtype)


def spatial_gating_mixer(h, w_in, ln_gain, ln_bias, w_s, b_s, w_out):
    bsz, seq, _ = h.shape
    proj = h @ w_in
    uv, g = proj[..., : 2 * D_INNER], proj[..., 2 * D_INNER:]
    uv = jax.nn.gelu(uv)
    u, v = uv[..., :D_INNER], uv[..., D_INNER:]
    v = layer_norm(v, ln_gain, ln_bias)
    nb = seq // SG_BLOCK
    v = v.reshape(bsz, nb, SG_BLOCK, SG_GROUPS, SG_GROUP_DIM)
    pos = jnp.arange(SG_BLOCK)
    mask = (pos[None, :] // CHUNK) <= (pos[:, None] // CHUNK)
    ws = jnp.where(mask[None], w_s, jnp.zeros((), w_s.dtype))
    s = jnp.einsum('gts,bnsgd->bntgd', ws, v) + b_s.T[None, None, :, :, None]
    s = s.reshape(bsz, seq, D_INNER)
    y = u * s * jax.nn.silu(g)
    return y @ w_out


def hgrn2_mixer(h, w_in, lower_bound, gn_gain, w_out):
    bsz, seq, _ = h.shape
    f32 = jnp.float32
    proj = h @ w_in
    q = proj[..., :D_INNER]
    f = proj[..., D_INNER: 2 * D_INNER]
    i = proj[..., 2 * D_INNER: 3 * D_INNER]
    g = proj[..., 3 * D_INNER:]
    q = jax.nn.silu(q.astype(f32))
    lb = lower_bound.astype(f32)
    f = lb + (1.0 - lb) * jax.nn.sigmoid(f.astype(f32))
    k = 1.0 - f
    log_f = jnp.log(f)
    nc = seq // CHUNK

    def heads(z):
        return z.reshape(bsz, nc, CHUNK, HG_HEADS, HG_HEAD_DIM).transpose(0, 3, 1, 2, 4)

    q, k, v, log_f = heads(q), heads(k), heads(i.astype(f32)), heads(log_f)
    a = jnp.cumsum(log_f, axis=3)
    a_ref = a[:, :, :, CHUNK // 2 - 1: CHUNK // 2, :]
    a_last = a[:, :, :, CHUNK - 1:, :]
    q_in = q * jnp.exp(a - a_ref)
    k_in = k * jnp.exp(a_ref - a)
    scores = jnp.einsum('bhnck,bhnsk->bhncs', q_in, k_in)
    causal = jnp.tril(jnp.ones((CHUNK, CHUNK), dtype=bool))
    scores = jnp.where(causal, scores, jnp.zeros((), f32))
    o_intra = jnp.einsum('bhncs,bhnsv->bhncv', scores, v)
    q_out = q * jnp.exp(a)
    k_out = k * jnp.exp(a_last - a)
    decay = jnp.exp(a_last[:, :, :, 0, :])

    def step(state, xs):
        q_c, k_c, v_c, d_c = xs
        o_c = jnp.einsum('bhck,bhkv->bhcv', q_c, state)
        state = d_c[..., None] * state + jnp.einsum('bhck,bhcv->bhkv', k_c, v_c)
        return state, o_c

    xs = (jnp.moveaxis(q_out, 2, 0), jnp.moveaxis(k_out, 2, 0),
          jnp.moveaxis(v, 2, 0), jnp.moveaxis(decay, 2, 0))
    init = jnp.zeros((bsz, HG_HEADS, HG_HEAD_DIM, HG_HEAD_DIM), f32)
    _, o_inter = lax.scan(step, init, xs)
    o = o_intra + jnp.moveaxis(o_inter, 0, 2)
    o = o.transpose(0, 2, 3, 1, 4)
    o = rms_norm(o, gn_gain)
    o = o.reshape(bsz, seq, D_INNER).astype(h.dtype)
    return (o * jax.nn.silu(g)) @ w_out


def _fwd_setup_inputs(seed: int = 0) -> dict:
    key = jax.random.key(seed)
    ks = jax.random.split(key, 20)
    nrm = jax.random.normal
    f32 = jnp.float32
    D, DI = D_MODEL, D_INNER
    return {
        "x": nrm(ks[0], (BATCH, SEQ, D), f32),
        "c": nrm(ks[1], (BATCH, D), f32),
        "norm_gain": 1.0 + 0.02 * nrm(ks[2], (DEPTH, D), f32),
        "w_ada": 0.5 * D ** -0.5 * nrm(ks[3], (DEPTH, D, 3 * D), f32),
        "b_ada": 0.02 * nrm(ks[4], (DEPTH, 3 * D), f32),
        "a_w_in": D ** -0.5 * nrm(ks[5], (N_LAYERS_A, D, 3 * DI), f32),
        "a_ln_gain": 1.0 + 0.02 * nrm(ks[6], (N_LAYERS_A, DI), f32),
        "a_ln_bias": 0.02 * nrm(ks[7], (N_LAYERS_A, DI), f32),
        "a_w_s": SG_BLOCK ** -0.5 * nrm(ks[8], (N_LAYERS_A, SG_GROUPS, SG_BLOCK, SG_BLOCK), f32),
        "a_b_s": 1.0 + 0.02 * nrm(ks[9], (N_LAYERS_A, SG_GROUPS, SG_BLOCK), f32),
        "a_w_out": DI ** -0.5 * nrm(ks[10], (N_LAYERS_A, DI, D), f32),
        "b_w_in": D ** -0.5 * nrm(ks[11], (N_LAYERS_B, D, 4 * DI), f32),
        "b_lower_bounds": 0.1 * nrm(ks[12], (DEPTH, DI), f32),
        "b_gn_gain": 1.0 + 0.02 * nrm(ks[13], (N_LAYERS_B, HG_HEAD_DIM), f32),
        "b_w_out": DI ** -0.5 * nrm(ks[14], (N_LAYERS_B, DI, D), f32),
        "final_gain": 1.0 + 0.02 * nrm(ks[15], (D,), f32),
    }


def _fwd_reference(x, c, norm_gain, w_ada, b_ada, a_w_in, a_ln_gain, a_ln_bias, a_w_s, a_b_s,
              a_w_out, b_w_in, b_lower_bounds, b_gn_gain, b_w_out, final_gain):
    p = jax.nn.softmax(b_lower_bounds.astype(jnp.float32), axis=0)
    cum = jnp.cumsum(p, axis=0)
    lower_bounds = cum - cum[0:1]
    c_act = jax.nn.silu(c)
    for layer in range(DEPTH):
        mod = c_act @ w_ada[layer] + b_ada[layer]
        shift = mod[:, None, :D_MODEL]
        scale = mod[:, None, D_MODEL: 2 * D_MODEL]
        gate = mod[:, None, 2 * D_MODEL:]
        h = rms_norm(x, norm_gain[layer]) * (1.0 + scale) + shift
        j = layer // N_MIXERS
        if layer % N_MIXERS == 0:
            y = spatial_gating_mixer(h, a_w_in[j], a_ln_gain[j], a_ln_bias[j],
                                     a_w_s[j], a_b_s[j], a_w_out[j])
        else:
            y = hgrn2_mixer(h, b_w_in[j], lower_bounds[layer], b_gn_gain[j], b_w_out[j])
        x = x + gate * y
    return rms_norm(x, final_gain)


import jax as _jax
import jax.numpy as _jnp

TWIN_FORMAT = 'train_step'
FWD_PARAMS = ['x', 'c', 'norm_gain', 'w_ada', 'b_ada', 'a_w_in', 'a_ln_gain', 'a_ln_bias', 'a_w_s', 'a_b_s', 'a_w_out', 'b_w_in', 'b_lower_bounds', 'b_gn_gain', 'b_w_out', 'final_gain']
TWIN_WEIGHTS = ['norm_gain', 'w_ada', 'b_ada', 'a_w_in', 'a_ln_gain', 'a_ln_bias', 'a_w_s', 'a_b_s', 'a_w_out', 'b_w_in', 'b_lower_bounds', 'b_gn_gain', 'b_w_out', 'final_gain']
TWIN_DIFF_INPUT = 'x'
TWIN_INPUTS = ['x', 'c', 'norm_gain', 'w_ada', 'b_ada', 'a_w_in', 'a_ln_gain', 'a_ln_bias', 'a_w_s', 'a_b_s', 'a_w_out', 'b_w_in', 'b_lower_bounds', 'b_gn_gain', 'b_w_out', 'final_gain', 'loss_target', 'm_norm_gain', 'm_w_ada', 'm_b_ada', 'm_a_w_in', 'm_a_ln_gain', 'm_a_ln_bias', 'm_a_w_s', 'm_a_b_s', 'm_a_w_out', 'm_b_w_in', 'm_b_lower_bounds', 'm_b_gn_gain', 'm_b_w_out', 'm_final_gain', 'v_norm_gain', 'v_w_ada', 'v_b_ada', 'v_a_w_in', 'v_a_ln_gain', 'v_a_ln_bias', 'v_a_w_s', 'v_a_b_s', 'v_a_w_out', 'v_b_w_in', 'v_b_lower_bounds', 'v_b_gn_gain', 'v_b_w_out', 'v_final_gain']
TWIN_OUTPUTS = ['loss', 'grad_x', 'grad_norm_gain', 'grad_w_ada', 'grad_b_ada', 'grad_a_w_in', 'grad_a_ln_gain', 'grad_a_ln_bias', 'grad_a_w_s', 'grad_a_b_s', 'grad_a_w_out', 'grad_b_w_in', 'grad_b_lower_bounds', 'grad_b_gn_gain', 'grad_b_w_out', 'grad_final_gain', 'delta_norm_gain', 'delta_w_ada', 'delta_b_ada', 'delta_a_w_in', 'delta_a_ln_gain', 'delta_a_ln_bias', 'delta_a_w_s', 'delta_a_b_s', 'delta_a_w_out', 'delta_b_w_in', 'delta_b_lower_bounds', 'delta_b_gn_gain', 'delta_b_w_out', 'delta_final_gain', 'new_m_norm_gain', 'new_m_w_ada', 'new_m_b_ada', 'new_m_a_w_in', 'new_m_a_ln_gain', 'new_m_a_ln_bias', 'new_m_a_w_s', 'new_m_a_b_s', 'new_m_a_w_out', 'new_m_b_w_in', 'new_m_b_lower_bounds', 'new_m_b_gn_gain', 'new_m_b_w_out', 'new_m_final_gain', 'new_v_norm_gain', 'new_v_w_ada', 'new_v_b_ada', 'new_v_a_w_in', 'new_v_a_ln_gain', 'new_v_a_ln_bias', 'new_v_a_w_s', 'new_v_a_b_s', 'new_v_a_w_out', 'new_v_b_w_in', 'new_v_b_lower_bounds', 'new_v_b_gn_gain', 'new_v_b_w_out', 'new_v_final_gain']
TWIN_LEAF_KINDS = {'loss': 'loss', 'grad_x': 'grad_x', 'grad_norm_gain': 'grad_w', 'grad_w_ada': 'grad_w', 'grad_b_ada': 'grad_w', 'grad_a_w_in': 'grad_w', 'grad_a_ln_gain': 'grad_w', 'grad_a_ln_bias': 'grad_w', 'grad_a_w_s': 'grad_w', 'grad_a_b_s': 'grad_w', 'grad_a_w_out': 'grad_w', 'grad_b_w_in': 'grad_w', 'grad_b_lower_bounds': 'grad_w', 'grad_b_gn_gain': 'grad_w', 'grad_b_w_out': 'grad_w', 'grad_final_gain': 'grad_w', 'delta_norm_gain': 'delta_w', 'delta_w_ada': 'delta_w', 'delta_b_ada': 'delta_w', 'delta_a_w_in': 'delta_w', 'delta_a_ln_gain': 'delta_w', 'delta_a_ln_bias': 'delta_w', 'delta_a_w_s': 'delta_w', 'delta_a_b_s': 'delta_w', 'delta_a_w_out': 'delta_w', 'delta_b_w_in': 'delta_w', 'delta_b_lower_bounds': 'delta_w', 'delta_b_gn_gain': 'delta_w', 'delta_b_w_out': 'delta_w', 'delta_final_gain': 'delta_w', 'new_m_norm_gain': 'new_m', 'new_m_w_ada': 'new_m', 'new_m_b_ada': 'new_m', 'new_m_a_w_in': 'new_m', 'new_m_a_ln_gain': 'new_m', 'new_m_a_ln_bias': 'new_m', 'new_m_a_w_s': 'new_m', 'new_m_a_b_s': 'new_m', 'new_m_a_w_out': 'new_m', 'new_m_b_w_in': 'new_m', 'new_m_b_lower_bounds': 'new_m', 'new_m_b_gn_gain': 'new_m', 'new_m_b_w_out': 'new_m', 'new_m_final_gain': 'new_m', 'new_v_norm_gain': 'new_v', 'new_v_w_ada': 'new_v', 'new_v_b_ada': 'new_v', 'new_v_a_w_in': 'new_v', 'new_v_a_ln_gain': 'new_v', 'new_v_a_ln_bias': 'new_v', 'new_v_a_w_s': 'new_v', 'new_v_a_b_s': 'new_v', 'new_v_a_w_out': 'new_v', 'new_v_b_w_in': 'new_v', 'new_v_b_lower_bounds': 'new_v', 'new_v_b_gn_gain': 'new_v', 'new_v_b_w_out': 'new_v', 'new_v_final_gain': 'new_v'}


def _forward(args):
    return _fwd_reference(*[args[k] for k in FWD_PARAMS])


def _output_shape():
    out = _jax.eval_shape(lambda: _forward(_fwd_setup_inputs(0)))
    return out.shape, out.dtype

N_MICROBATCH = 1
ADAM_LR = 0.001
ADAM_B1 = 0.9
ADAM_B2 = 0.999
ADAM_EPS = 1e-08
ADAM_WD = 0.01
ADAM_STEP = 10
PER_EXAMPLE_BATCH_AXIS = {'x': 0, 'c': 0, 'loss_target': 0}
SHARED_INPUTS = []
_WEIGHT_DTYPES = {'norm_gain': _jnp.float32, 'w_ada': _jnp.float32, 'b_ada': _jnp.float32, 'a_w_in': _jnp.float32, 'a_ln_gain': _jnp.float32, 'a_ln_bias': _jnp.float32, 'a_w_s': _jnp.float32, 'a_b_s': _jnp.float32, 'a_w_out': _jnp.float32, 'b_w_in': _jnp.float32, 'b_lower_bounds': _jnp.float32, 'b_gn_gain': _jnp.float32, 'b_w_out': _jnp.float32, 'final_gain': _jnp.float32}
MOMENT_SCALE = {'norm_gain': 4.983153e-02, 'w_ada': 5.123898e-02, 'b_ada': 8.404101e-02, 'a_w_in': 2.296712e-02, 'a_ln_gain': 1.682496e-02, 'a_ln_bias': 1.611198e-02, 'a_w_s': 2.236159e-02, 'a_b_s': 2.627610e-02, 'a_w_out': 3.418074e-02, 'b_w_in': 1.789024e-02, 'b_lower_bounds': 1.991411e-03, 'b_gn_gain': 1.085465e-01, 'b_w_out': 3.475366e-02, 'final_gain': 3.197523e+01}


def _to_microbatches(a, axis):
    t = _jnp.moveaxis(a, axis, 0)
    t = t.reshape((N_MICROBATCH, t.shape[0] // N_MICROBATCH) + t.shape[1:])
    return _jnp.moveaxis(t, 1, axis + 1)


def setup_inputs(seed: int = 0) -> dict:
    inp = _fwd_setup_inputs(seed)
    key = _jax.random.fold_in(_jax.random.key(seed), 7919)
    shape, _ = _output_shape()
    out = dict(inp)
    out["loss_target"] = _jax.random.normal(_jax.random.fold_in(key, 0), shape, _jnp.float32)
    for i, name in enumerate(TWIN_WEIGHTS):
        w = inp[name].astype(_jnp.float32)
        if MOMENT_SCALE is None:
            s = _jnp.sqrt(_jnp.mean(_jnp.square(w)) + 1e-30)
        else:
            s = MOMENT_SCALE[name]
        km, kv = _jax.random.split(_jax.random.fold_in(key, i + 1))
        out[name] = w
        out["m_" + name] = s * _jax.random.normal(km, w.shape, _jnp.float32)
        out["v_" + name] = (s * s) * _jax.random.uniform(kv, w.shape, _jnp.float32, 0.5, 1.5)
    if N_MICROBATCH > 1:
        for name, axis in PER_EXAMPLE_BATCH_AXIS.items():
            out[name] = _to_microbatches(out[name], axis)
    return {'x': out['x'], 'c': out['c'], 'norm_gain': out['norm_gain'], 'w_ada': out['w_ada'], 'b_ada': out['b_ada'], 'a_w_in': out['a_w_in'], 'a_ln_gain': out['a_ln_gain'], 'a_ln_bias': out['a_ln_bias'], 'a_w_s': out['a_w_s'], 'a_b_s': out['a_b_s'], 'a_w_out': out['a_w_out'], 'b_w_in': out['b_w_in'], 'b_lower_bounds': out['b_lower_bounds'], 'b_gn_gain': out['b_gn_gain'], 'b_w_out': out['b_w_out'], 'final_gain': out['final_gain'], 'loss_target': out['loss_target'], 'm_norm_gain': out['m_norm_gain'], 'm_w_ada': out['m_w_ada'], 'm_b_ada': out['m_b_ada'], 'm_a_w_in': out['m_a_w_in'], 'm_a_ln_gain': out['m_a_ln_gain'], 'm_a_ln_bias': out['m_a_ln_bias'], 'm_a_w_s': out['m_a_w_s'], 'm_a_b_s': out['m_a_b_s'], 'm_a_w_out': out['m_a_w_out'], 'm_b_w_in': out['m_b_w_in'], 'm_b_lower_bounds': out['m_b_lower_bounds'], 'm_b_gn_gain': out['m_b_gn_gain'], 'm_b_w_out': out['m_b_w_out'], 'm_final_gain': out['m_final_gain'], 'v_norm_gain': out['v_norm_gain'], 'v_w_ada': out['v_w_ada'], 'v_b_ada': out['v_b_ada'], 'v_a_w_in': out['v_a_w_in'], 'v_a_ln_gain': out['v_a_ln_gain'], 'v_a_ln_bias': out['v_a_ln_bias'], 'v_a_w_s': out['v_a_w_s'], 'v_a_b_s': out['v_a_b_s'], 'v_a_w_out': out['v_a_w_out'], 'v_b_w_in': out['v_b_w_in'], 'v_b_lower_bounds': out['v_b_lower_bounds'], 'v_b_gn_gain': out['v_b_gn_gain'], 'v_b_w_out': out['v_b_w_out'], 'v_final_gain': out['v_final_gain']}


def _loss(weights, diff, rest, loss_target):
    with _jax.named_scope("forward"):
        args = {**rest, TWIN_DIFF_INPUT: diff, **{k: w.astype(_WEIGHT_DTYPES[k]) for k, w in weights.items()}}
        y = _forward(args)
    with _jax.named_scope("loss_head"):
        err = _jnp.square(y.astype(_jnp.float32) - loss_target)
        return 0.5 * _jnp.sum(_jnp.mean(err, axis=-1)) if err.ndim else 0.5 * err


def _adamw(w, g, m, v):
    m = ADAM_B1 * m + (1.0 - ADAM_B1) * g
    v = ADAM_B2 * v + (1.0 - ADAM_B2) * _jnp.square(g)
    m_hat = m / (1.0 - ADAM_B1 ** ADAM_STEP)
    v_hat = v / (1.0 - ADAM_B2 ** ADAM_STEP)
    delta = -ADAM_LR * (m_hat / (_jnp.sqrt(v_hat) + ADAM_EPS) + ADAM_WD * w)
    return delta, m, v


def reference(x, c, norm_gain, w_ada, b_ada, a_w_in, a_ln_gain, a_ln_bias, a_w_s, a_b_s, a_w_out, b_w_in, b_lower_bounds, b_gn_gain, b_w_out, final_gain, loss_target, m_norm_gain, m_w_ada, m_b_ada, m_a_w_in, m_a_ln_gain, m_a_ln_bias, m_a_w_s, m_a_b_s, m_a_w_out, m_b_w_in, m_b_lower_bounds, m_b_gn_gain, m_b_w_out, m_final_gain, v_norm_gain, v_w_ada, v_b_ada, v_a_w_in, v_a_ln_gain, v_a_ln_bias, v_a_w_s, v_a_b_s, v_a_w_out, v_b_w_in, v_b_lower_bounds, v_b_gn_gain, v_b_w_out, v_final_gain):
    given = dict(x=x, c=c, norm_gain=norm_gain, w_ada=w_ada, b_ada=b_ada, a_w_in=a_w_in, a_ln_gain=a_ln_gain, a_ln_bias=a_ln_bias, a_w_s=a_w_s, a_b_s=a_b_s, a_w_out=a_w_out, b_w_in=b_w_in, b_lower_bounds=b_lower_bounds, b_gn_gain=b_gn_gain, b_w_out=b_w_out, final_gain=final_gain, loss_target=loss_target, m_norm_gain=m_norm_gain, m_w_ada=m_w_ada, m_b_ada=m_b_ada, m_a_w_in=m_a_w_in, m_a_ln_gain=m_a_ln_gain, m_a_ln_bias=m_a_ln_bias, m_a_w_s=m_a_w_s, m_a_b_s=m_a_b_s, m_a_w_out=m_a_w_out, m_b_w_in=m_b_w_in, m_b_lower_bounds=m_b_lower_bounds, m_b_gn_gain=m_b_gn_gain, m_b_w_out=m_b_w_out, m_final_gain=m_final_gain, v_norm_gain=v_norm_gain, v_w_ada=v_w_ada, v_b_ada=v_b_ada, v_a_w_in=v_a_w_in, v_a_ln_gain=v_a_ln_gain, v_a_ln_bias=v_a_ln_bias, v_a_w_s=v_a_w_s, v_a_b_s=v_a_b_s, v_a_w_out=v_a_w_out, v_b_w_in=v_b_w_in, v_b_lower_bounds=v_b_lower_bounds, v_b_gn_gain=v_b_gn_gain, v_b_w_out=v_b_w_out, v_final_gain=v_final_gain)
    weights = {n: given[n] for n in TWIN_WEIGHTS}
    shared = {n: given[n] for n in SHARED_INPUTS}
    per_example = {n: given[n] for n in ['x', 'c']}
    grad_fn = _jax.value_and_grad(_loss, argnums=(0, 1))

    def one_microbatch(ex, loss_target):
        ex = dict(ex)
        diff = ex.pop(TWIN_DIFF_INPUT)
        return grad_fn(weights, diff, {**shared, **ex}, loss_target)

    if N_MICROBATCH == 1:
        loss, (grad_w, grad_x) = one_microbatch(per_example, given["loss_target"])
    else:
        def body(carry, xs):
            loss_sum, grad_sum = carry
            l_k, (gw_k, gx_k) = one_microbatch(xs[0], xs[1])
            with _jax.named_scope("update"):
                return (loss_sum + l_k, _jax.tree.map(_jnp.add, grad_sum, gw_k)), gx_k

        init = (_jnp.zeros((), _jnp.float32), _jax.tree.map(_jnp.zeros_like, weights))
        (loss, grad_w), grad_x = _jax.lax.scan(body, init, (per_example, given["loss_target"]))
    with _jax.named_scope("update"):
        delta_w, new_m, new_v = {}, {}, {}
        for n in TWIN_WEIGHTS:
            delta_w[n], new_m[n], new_v[n] = _adamw(weights[n], grad_w[n], given["m_" + n], given["v_" + n])
    return (loss, grad_x, *[grad_w[n] for n in TWIN_WEIGHTS], *[delta_w[n] for n in TWIN_WEIGHTS],
            *[new_m[n] for n in TWIN_WEIGHTS], *[new_v[n] for n in TWIN_WEIGHTS])
```

```python
import functools
import math

import jax
import jax.numpy as jnp
from jax import lax
from jax.experimental import pallas as pl
from jax.experimental.pallas import tpu as pltpu

F32 = jnp.float32
BF16 = jnp.bfloat16
MESH = pl.DeviceIdType.MESH
NDEV = 8
EPS = 1e-6
CHUNK = 64
SG_BLOCK = 128
SG_GROUPS = 8
HEAD_DIM = 128
CUM_ROWS = 256
ADAM_LR, ADAM_B1, ADAM_B2, ADAM_EPS, ADAM_WD, ADAM_STEP = 0.001, 0.9, 0.999, 1e-08, 0.01, 10
VMEM_LIMIT = 56 * 1024 * 1024
ANY = pl.BlockSpec(memory_space=pl.ANY)


def _pc(body, *, name, out_shape, grid=None, in_specs=None, out_specs=None, scratch=(), sem=None,
        grid_spec=None):
    cp = dict(vmem_limit_bytes=VMEM_LIMIT)
    if sem is not None:
        cp["dimension_semantics"] = sem
    kw = {}
    if grid_spec is not None:
        kw["grid_spec"] = grid_spec
    else:
        if grid is not None:
            kw["grid"] = grid
        if in_specs is not None:
            kw["in_specs"] = in_specs
        if out_specs is not None:
            kw["out_specs"] = out_specs
        kw["scratch_shapes"] = list(scratch)
    return pl.pallas_call(functools.partial(body), name=name, out_shape=out_shape,
                          compiler_params=pltpu.CompilerParams(**cp), **kw)


def _tile(n, pref):
    return pref if n % pref == 0 else n


def _sigmoid(x):
    return 1.0 / (1.0 + jnp.exp(-x))


def _gelu(x):
    c = math.sqrt(2.0 / math.pi)
    return 0.5 * x * (1.0 + jnp.tanh(c * (x + 0.044715 * (x * x * x))))


def _dgelu(x):
    c = math.sqrt(2.0 / math.pi)
    t = jnp.tanh(c * (x + 0.044715 * (x * x * x)))
    return 0.5 * (1.0 + t) + 0.5 * x * (1.0 - t * t) * (c * (1.0 + 3.0 * 0.044715 * (x * x)))


def _dot(a, b):
    return jnp.dot(a, b, preferred_element_type=F32)


def _dot_nt(a, b):
    return lax.dot_general(a, b, (((1,), (1,)), ((), ())), preferred_element_type=F32)


def _dot_tn(a, b):
    return lax.dot_general(a, b, (((0,), (0,)), ((), ())), preferred_element_type=F32)


def _tri_mask(n, reverse):
    r = lax.broadcasted_iota(jnp.int32, (n, n), 0)
    c = lax.broadcasted_iota(jnp.int32, (n, n), 1)
    same = (r // CHUNK) == (c // CHUNK)
    tri = (c >= r) if reverse else (c <= r)
    return jnp.where(same & tri, 1.0, 0.0).astype(BF16)


def _tri_apply(tri, x):
    hi = x.astype(BF16)
    r1 = x - hi.astype(F32)
    mid = r1.astype(BF16)
    lo = (r1 - mid.astype(F32)).astype(BF16)
    return _dot(tri, hi) + (_dot(tri, mid) + _dot(tri, lo))


def _all_gather(arrs, name):
    n = len(arrs)

    def body(*refs):
        ins, outs = refs[:n], refs[n:2 * n]
        send_sems, recv_sems, local_sems = refs[2 * n:]
        x, y, c = lax.axis_index("x"), lax.axis_index("y"), lax.axis_index("c")
        me, sibling = (x, y, c), (x, y, 1 - c)
        chips = [(1 - x, y), (x, 1 - y), (1 - x, 1 - y)]

        def blk(a, p):
            return outs[a].at[4 * p[0] + 2 * p[1] + p[2]]

        def copy(a, k, block, to, src=None):
            return pltpu.make_async_remote_copy(
                src_ref=blk(a, block) if src is None else src, dst_ref=blk(a, block),
                send_sem=send_sems.at[7 * a + k], recv_sem=recv_sems.at[7 * a + k],
                device_id=to, device_id_type=MESH)

        mine = [pltpu.make_async_copy(ins[a], blk(a, me), local_sems.at[a]) for a in range(n)]
        for m in mine:
            m.start()
        first = []
        for a in range(n):
            first.append(copy(a, 0, me, sibling, src=ins[a]))
            for j, chip in enumerate(chips):
                first.append(copy(a, 1 + j, me, (*chip, c), src=ins[a]))
        for cp in first:
            cp.start()
        passed = []
        for j, chip in enumerate(chips):
            for a in range(n):
                copy(a, 1 + j, (*chip, c), me).wait_recv()
                p = copy(a, 4 + j, (*chip, c), sibling)
                p.start()
                passed.append(p)
        for a in range(n):
            copy(a, 0, sibling, me).wait_recv()
            for j, chip in enumerate(chips):
                copy(a, 4 + j, (*chip, 1 - c), me).wait_recv()
        for cp in first + passed:
            cp.wait_send()
        for m in mine:
            m.wait()

    out_shape = [jax.ShapeDtypeStruct((NDEV,) + a.shape, a.dtype) for a in arrs]
    return _pc(body, name=name, out_shape=out_shape, in_specs=[ANY] * n, out_specs=[ANY] * n,
               scratch=[pltpu.SemaphoreType.DMA((7 * n,)), pltpu.SemaphoreType.DMA((7 * n,)),
                        pltpu.SemaphoreType.DMA((n,))])(*arrs)


def _exchange(srcs, nblk, ids_fn, partner_fn, name):
    n = len(srcs)

    def body(*refs):
        ins, outs = refs[:n], refs[n:2 * n]
        send_sems, recv_sems = refs[2 * n:]
        x, y, c = lax.axis_index("x"), lax.axis_index("y"), lax.axis_index("c")
        ids = ids_fn(x, y, c)
        partner = partner_fn(x, y, c)
        copies = []
        for a in range(n):
            for k in range(nblk):
                copies.append(pltpu.make_async_remote_copy(
                    src_ref=ins[a].at[ids[k]], dst_ref=outs[a].at[k],
                    send_sem=send_sems.at[a * nblk + k], recv_sem=recv_sems.at[a * nblk + k],
                    device_id=partner, device_id_type=MESH))
        for cp in copies:
            cp.start()
        for cp in copies:
            cp.wait()

    out_shape = [jax.ShapeDtypeStruct((nblk,) + a.shape[1:], a.dtype) for a in srcs]
    return _pc(body, name=name, out_shape=out_shape, in_specs=[ANY] * n, out_specs=[ANY] * n,
               scratch=[pltpu.SemaphoreType.DMA((n * nblk,)), pltpu.SemaphoreType.DMA((n * nblk,))])(*srcs)


def _add_sel(a, sel, b, name):
    n, r, c = b.shape
    tr = _tile(r, 256)

    def body(sel_ref, a_ref, b_ref, o_ref):
        o_ref[...] = a_ref[...] + b_ref[...]

    gs = pltpu.PrefetchScalarGridSpec(
        num_scalar_prefetch=1, grid=(n, r // tr),
        in_specs=[pl.BlockSpec((None, tr, c), lambda k, i, s: (s[k], i, 0)),
                  pl.BlockSpec((None, tr, c), lambda k, i, s: (k, i, 0))],
        out_specs=pl.BlockSpec((None, tr, c), lambda k, i, s: (k, i, 0)))
    return _pc(body, name=name, out_shape=jax.ShapeDtypeStruct(b.shape, F32), grid_spec=gs,
               sem=("arbitrary", "arbitrary"))(sel, a, b)


def _ada_fwd(c_all, w_ada, b_cols, b_lb):
    nl, d, ncol = w_ada.shape
    nseq = c_all.shape[0]
    di = b_lb.shape[1]

    def body(c_ref, w_ref, b_ref, lb_ref, mod_ref, lbj_ref):
        cv = c_ref[...]
        cact = (cv * _sigmoid(cv)).astype(BF16)
        for l in range(nl):
            mod_ref[l] = _dot(cact, w_ref[l].astype(BF16)) + b_ref[l]
        b0, b1 = lb_ref[0:1, :], lb_ref[1:2, :]
        mx = jnp.maximum(b0, b1)
        e0, e1 = jnp.exp(b0 - mx), jnp.exp(b1 - mx)
        s = e0 + e1
        p0, p1 = e0 / s, e1 / s
        lbj_ref[0:1, :] = (p0 + p1) - p0
        lbj_ref[1:2, :] = p0 * p1

    return _pc(body, name="ada_fwd",
               out_shape=[jax.ShapeDtypeStruct((nl, nseq, ncol), F32), jax.ShapeDtypeStruct((2, di), F32)]
               )(c_all, w_ada, b_cols, b_lb)


def _ada_bwd(c_all, dmod_cols, dmod_full):
    nl, nseq, ncol = dmod_cols.shape
    d = c_all.shape[1]
    d3 = dmod_full.shape[2]

    def body(c_ref, dc_ref, df_ref, gw_ref, gb_ref):
        cv = c_ref[...]
        cact = (cv * _sigmoid(cv)).astype(BF16)
        for l in range(nl):
            gw_ref[l] = _dot_tn(cact, dc_ref[l].astype(BF16))
            gb_ref[l:l + 1, :] = jnp.sum(df_ref[l], axis=0, keepdims=True)

    return _pc(body, name="ada_bwd",
               out_shape=[jax.ShapeDtypeStruct((nl, d, ncol), F32), jax.ShapeDtypeStruct((nl, d3), F32)]
               )(c_all, dmod_cols, dmod_full)


def _prenorm(x, gain, mod, t_seq, name):
    m, d = x.shape
    tm = _tile(t_seq, 512)
    per = t_seq // tm

    def body(x_ref, g_ref, mod_ref, h_ref):
        xv = x_ref[...]
        rstd = lax.rsqrt(jnp.mean(xv * xv, axis=-1, keepdims=True) + EPS)
        r = xv * rstd * g_ref[...]
        h_ref[...] = (r * (1.0 + mod_ref[0, 1:2, :]) + mod_ref[0, 0:1, :]).astype(BF16)

    return _pc(body, name=name, out_shape=jax.ShapeDtypeStruct((m, d), BF16), grid=(m // tm,),
               in_specs=[pl.BlockSpec((tm, d), lambda i: (i, 0)), pl.BlockSpec((1, d), lambda i: (0, 0)),
                         pl.BlockSpec((1, 3, d), lambda i: (i // per, 0, 0))],
               out_specs=pl.BlockSpec((tm, d), lambda i: (i, 0)), sem=("parallel",))(x, gain, mod)


def _prenorm_bwd(dh, x, gain, mod, dxn, t_seq, name):
    m, d = x.shape
    nb = m // t_seq
    tm = _tile(t_seq, 512)
    per = t_seq // tm

    def body(dh_ref, x_ref, g_ref, mod_ref, dxn_ref, dx_ref, dss_ref, dg_ref):
        i = pl.program_id(0)
        xv, dhv, g = x_ref[...], dh_ref[...], g_ref[...]
        rstd = lax.rsqrt(jnp.mean(xv * xv, axis=-1, keepdims=True) + EPS)
        xhat = xv * rstd
        dr = dhv * (1.0 + mod_ref[0, 1:2, :])
        dxhat = dr * g
        dx_ref[...] = dxn_ref[...] + rstd * (dxhat - xhat * jnp.mean(dxhat * xhat, axis=-1, keepdims=True))

        @pl.when(i % per == 0)
        def _():
            dss_ref[...] = jnp.zeros_like(dss_ref)

        @pl.when(i == 0)
        def _():
            dg_ref[...] = jnp.zeros_like(dg_ref)

        dss_ref[0, 0:1, :] += jnp.sum(dhv, axis=0, keepdims=True)
        dss_ref[0, 1:2, :] += jnp.sum(dhv * (xhat * g), axis=0, keepdims=True)
        dg_ref[...] += jnp.sum(dr * xhat, axis=0, keepdims=True)

    row = pl.BlockSpec((tm, d), lambda i: (i, 0))
    return _pc(body, name=name,
               out_shape=[jax.ShapeDtypeStruct((m, d), F32), jax.ShapeDtypeStruct((nb, 2, d), F32),
                          jax.ShapeDtypeStruct((1, d), F32)],
               grid=(m // tm,),
               in_specs=[row, row, pl.BlockSpec((1, d), lambda i: (0, 0)),
                         pl.BlockSpec((1, 3, d), lambda i: (i // per, 0, 0)), row],
               out_specs=[row, pl.BlockSpec((1, 2, d), lambda i: (i // per, 0, 0)),
                          pl.BlockSpec((1, d), lambda i: (0, 0))],
               sem=("arbitrary",))(dh, x, gain, mod, dxn)


def _mm_in(h, w_g, sections, name):
    m, k = h.shape
    nc = w_g.shape[2]
    per = NDEV // sections if sections > 1 else NDEV
    tm = _tile(m, 512)

    def body(h_ref, w_ref, o_ref):
        o_ref[...] = _dot(h_ref[...], w_ref[...])

    if sections > 1:
        out_shape = jax.ShapeDtypeStruct((sections, m, per * nc), F32)
        out_spec = pl.BlockSpec((None, tm, nc), lambda j, i: (j // per, i, j % per))
    else:
        out_shape = jax.ShapeDtypeStruct((m, NDEV * nc), F32)
        out_spec = pl.BlockSpec((tm, nc), lambda j, i: (i, j))
    return _pc(body, name=name, out_shape=out_shape, grid=(NDEV, m // tm),
               in_specs=[pl.BlockSpec((tm, k), lambda j, i: (i, 0)),
                         pl.BlockSpec((None, k, nc), lambda j, i: (j, 0, 0))],
               out_specs=out_spec, sem=("parallel", "parallel"))(h, w_g)


def _dspec(sections, tm, nc, m_axis_first):
    per = NDEV // sections if sections > 1 else NDEV
    if sections > 1:
        if m_axis_first:
            return pl.BlockSpec((None, tm, nc), lambda i, j: (j // per, i, j % per))
        return pl.BlockSpec((None, tm, nc), lambda j, i: (j // per, i, j % per))
    if m_axis_first:
        return pl.BlockSpec((tm, nc), lambda i, j: (i, j))
    return pl.BlockSpec((tm, nc), lambda j, i: (i, j))


def _mm_din(dproj, w_g, sections, name):
    k, nc = w_g.shape[1], w_g.shape[2]
    m = dproj.shape[-2]
    tm = _tile(m, 512)

    def body(d_ref, w_ref, o_ref):
        j = pl.program_id(1)
        acc = _dot_nt(d_ref[...], w_ref[...])

        @pl.when(j == 0)
        def _():
            o_ref[...] = acc

        @pl.when(j > 0)
        def _():
            o_ref[...] += acc

    return _pc(body, name=name, out_shape=jax.ShapeDtypeStruct((m, k), F32), grid=(m // tm, NDEV),
               in_specs=[_dspec(sections, tm, nc, True), pl.BlockSpec((None, k, nc), lambda i, j: (j, 0, 0))],
               out_specs=pl.BlockSpec((tm, k), lambda i, j: (i, 0)), sem=("parallel", "arbitrary"))(dproj, w_g)


def _mm_dw_in(h, dproj, nc, sections, name):
    m, k = h.shape
    tk = _tile(m, 512)

    def body(h_ref, d_ref, o_ref):
        kk = pl.program_id(1)
        acc = _dot_tn(h_ref[...], d_ref[...])

        @pl.when(kk == 0)
        def _():
            o_ref[...] = acc

        @pl.when(kk > 0)
        def _():
            o_ref[...] += acc

    return _pc(body, name=name, out_shape=jax.ShapeDtypeStruct((NDEV, k, nc), F32), grid=(NDEV, m // tk),
               in_specs=[pl.BlockSpec((tk, k), lambda j, i: (i, 0)), _dspec(sections, tk, nc, False)],
               out_specs=pl.BlockSpec((None, k, nc), lambda j, i: (j, 0, 0)),
               sem=("parallel", "arbitrary"))(h, dproj)


def _out_proj(ybr, w_out, x, mod, t_seq, name):
    m, di = ybr.shape
    d = w_out.shape[1]
    tm = _tile(t_seq, 512)
    per = t_seq // tm

    def body(y_ref, w_ref, x_ref, mod_ref, yo_ref, xn_ref):
        yo = _dot(y_ref[...], w_ref[...])
        yo_ref[...] = yo
        xn_ref[...] = x_ref[...] + mod_ref[0, 2:3, :] * yo

    row = pl.BlockSpec((tm, d), lambda i: (i, 0))
    return _pc(body, name=name,
               out_shape=[jax.ShapeDtypeStruct((m, d), F32), jax.ShapeDtypeStruct((m, d), F32)],
               grid=(m // tm,),
               in_specs=[pl.BlockSpec((tm, di), lambda i: (i, 0)), pl.BlockSpec((di, d), lambda i: (0, 0)), row,
                         pl.BlockSpec((1, 3, d), lambda i: (i // per, 0, 0))],
               out_specs=[row, row], sem=("parallel",))(ybr, w_out, x, mod)


def _gate_bwd(dxn, yout, mod, t_seq, name):
    m, d = dxn.shape
    nb = m // t_seq
    tm = _tile(t_seq, 512)
    per = t_seq // tm

    def body(dxn_ref, yo_ref, mod_ref, dy_ref, dgate_ref):
        i = pl.program_id(0)
        dv = dxn_ref[...]
        dy_ref[...] = (mod_ref[0, 2:3, :] * dv).astype(BF16)

        @pl.when(i % per == 0)
        def _():
            dgate_ref[...] = jnp.zeros_like(dgate_ref)

        dgate_ref[0] += jnp.sum(dv * yo_ref[...], axis=0, keepdims=True)

    row = pl.BlockSpec((tm, d), lambda i: (i, 0))
    return _pc(body, name=name,
               out_shape=[jax.ShapeDtypeStruct((m, d), BF16), jax.ShapeDtypeStruct((nb, 1, d), F32)],
               grid=(m // tm,),
               in_specs=[row, row, pl.BlockSpec((1, 3, d), lambda i: (i // per, 0, 0))],
               out_specs=[row, pl.BlockSpec((1, 1, d), lambda i: (i // per, 0, 0))],
               sem=("arbitrary",))(dxn, yout, mod)


def _mm_dybr(dy, w_out, name):
    m, d = dy.shape
    di = w_out.shape[0]
    tm = _tile(m, 512)

    def body(dy_ref, w_ref, o_ref):
        o_ref[...] = _dot_nt(dy_ref[...], w_ref[...])

    return _pc(body, name=name, out_shape=jax.ShapeDtypeStruct((m, di), F32), grid=(m // tm,),
               in_specs=[pl.BlockSpec((tm, d), lambda i: (i, 0)), pl.BlockSpec((di, d), lambda i: (0, 0))],
               out_specs=pl.BlockSpec((tm, di), lambda i: (i, 0)), sem=("parallel",))(dy, w_out)


def _mm_dw_out(ybr, dy, name):
    m, di = ybr.shape
    d = dy.shape[1]
    tk = _tile(m, 512)
    tn = _tile(di, 1024)

    def body(y_ref, dy_ref, o_ref):
        kk = pl.program_id(1)
        acc = _dot_tn(y_ref[...], dy_ref[...])

        @pl.when(kk == 0)
        def _():
            o_ref[...] = acc

        @pl.when(kk > 0)
        def _():
            o_ref[...] += acc

    return _pc(body, name=name, out_shape=jax.ShapeDtypeStruct((di, d), F32), grid=(di // tn, m // tk),
               in_specs=[pl.BlockSpec((tk, tn), lambda n, k: (k, n)), pl.BlockSpec((tk, d), lambda n, k: (k, 0))],
               out_specs=pl.BlockSpec((tn, d), lambda n, k: (n, 0)), sem=("parallel", "arbitrary"))(ybr, dy)


def _sgu_mask():
    t = lax.broadcasted_iota(jnp.int32, (SG_BLOCK, SG_BLOCK), 0)
    s = lax.broadcasted_iota(jnp.int32, (SG_BLOCK, SG_BLOCK), 1)
    return (s // CHUNK) <= (t // CHUNK)


def _a_mid_fwd(proj, ln_g, ln_b, w_s, bs_t, t_seq):
    m, n3 = proj.shape
    di = n3 // 3
    gd = di // SG_GROUPS
    r = _tile(t_seq, 256)
    nblk = r // SG_BLOCK

    def body(p_ref, lg_ref, lb_ref, ws_ref, bs_ref, ybr_ref, s_scr):
        v = _gelu(p_ref[:, di:2 * di])
        mu = jnp.mean(v, axis=-1, keepdims=True)
        vc = v - mu
        rstd = lax.rsqrt(jnp.mean(vc * vc, axis=-1, keepdims=True) + EPS)
        vb = (vc * rstd * lg_ref[...] + lb_ref[...]).astype(BF16)
        mask = _sgu_mask()
        for gi in range(SG_GROUPS):
            ws = jnp.where(mask, ws_ref[gi], 0.0).astype(BF16)
            bcol = bs_ref[:, gi:gi + 1]
            for b in range(nblk):
                rows = slice(b * SG_BLOCK, (b + 1) * SG_BLOCK)
                cols = slice(gi * gd, (gi + 1) * gd)
                s_scr[rows, cols] = _dot(ws, vb[rows, cols]) + bcol
        gg = p_ref[:, 2 * di:]
        ybr_ref[...] = (_gelu(p_ref[:, :di]) * s_scr[...] * (gg * _sigmoid(gg))).astype(BF16)

    vec = pl.BlockSpec((1, di), lambda i: (0, 0))
    return _pc(body, name="a_mid_fwd", out_shape=jax.ShapeDtypeStruct((m, di), BF16), grid=(m // r,),
               in_specs=[pl.BlockSpec((r, n3), lambda i: (i, 0)), vec, vec,
                         pl.BlockSpec((SG_GROUPS, SG_BLOCK, SG_BLOCK), lambda i: (0, 0, 0)),
                         pl.BlockSpec((SG_BLOCK, 128), lambda i: (0, 0))],
               out_specs=pl.BlockSpec((r, di), lambda i: (i, 0)),
               scratch=[pltpu.VMEM((r, di), F32)], sem=("parallel",))(proj, ln_g, ln_b, w_s, bs_t)


def _a_mid_bwd(proj, dybr, ln_g, ln_b, w_s, bs_t, t_seq):
    m, n3 = proj.shape
    di = n3 // 3
    gd = di // SG_GROUPS
    r = _tile(t_seq, 256)
    nblk = r // SG_BLOCK

    def body(p_ref, dy_ref, lg_ref, lb_ref, ws_ref, bs_ref,
             dp_ref, dlg_ref, dlb_ref, dws_ref, dbs_ref, s_scr, dvl_scr):
        i = pl.program_id(0)

        @pl.when(i == 0)
        def _():
            dlg_ref[...] = jnp.zeros_like(dlg_ref)
            dlb_ref[...] = jnp.zeros_like(dlb_ref)
            dws_ref[...] = jnp.zeros_like(dws_ref)
            dbs_ref[...] = jnp.zeros_like(dbs_ref)

        v_pre = p_ref[:, di:2 * di]
        v = _gelu(v_pre)
        mu = jnp.mean(v, axis=-1, keepdims=True)
        vc = v - mu
        rstd = lax.rsqrt(jnp.mean(vc * vc, axis=-1, keepdims=True) + EPS)
        vhat = vc * rstd
        lg = lg_ref[...]
        vb = (vhat * lg + lb_ref[...]).astype(BF16)
        u_pre = p_ref[:, :di]
        u = _gelu(u_pre)
        gg = p_ref[:, 2 * di:]
        sg = _sigmoid(gg)
        dyv = dy_ref[...]
        dus = dyv * (gg * sg)
        dsb = (dus * u).astype(BF16)
        ds32 = dus * u
        mask = _sgu_mask()
        lane = lax.broadcasted_iota(jnp.int32, (SG_BLOCK, 128), 1)
        dbs_acc = jnp.zeros((SG_BLOCK, 128), F32)
        for gi in range(SG_GROUPS):
            ws = jnp.where(mask, ws_ref[gi], 0.0).astype(BF16)
            bcol = bs_ref[:, gi:gi + 1]
            cols = slice(gi * gd, (gi + 1) * gd)
            dws_acc = jnp.zeros((SG_BLOCK, SG_BLOCK), F32)
            dbs_col = jnp.zeros((SG_BLOCK, 1), F32)
            for b in range(nblk):
                rows = slice(b * SG_BLOCK, (b + 1) * SG_BLOCK)
                s_scr[rows, cols] = _dot(ws, vb[rows, cols]) + bcol
                dvl_scr[rows, cols] = _dot_tn(ws, dsb[rows, cols])
                dws_acc += _dot_nt(dsb[rows, cols], vb[rows, cols])
                dbs_col += jnp.sum(ds32[rows, cols], axis=-1, keepdims=True)
            dws_ref[gi] += jnp.where(mask, dws_acc, 0.0)
            dbs_acc += jnp.where(lane == gi, dbs_col, 0.0)
        dbs_ref[...] += dbs_acc
        s = s_scr[...]
        dp_ref[:, :di] = (dyv * s * (gg * sg) * _dgelu(u_pre)).astype(BF16)
        dp_ref[:, 2 * di:] = (dyv * u * s * (sg * (1.0 + gg * (1.0 - sg)))).astype(BF16)
        dvl = dvl_scr[...]
        dlg_ref[...] += jnp.sum(dvl * vhat, axis=0, keepdims=True)
        dlb_ref[...] += jnp.sum(dvl, axis=0, keepdims=True)
        dvh = dvl * lg
        dv = rstd * (dvh - jnp.mean(dvh, axis=-1, keepdims=True)
                     - vhat * jnp.mean(dvh * vhat, axis=-1, keepdims=True))
        dp_ref[:, di:2 * di] = (dv * _dgelu(v_pre)).astype(BF16)

    vec = pl.BlockSpec((1, di), lambda i: (0, 0))
    wsb = pl.BlockSpec((SG_GROUPS, SG_BLOCK, SG_BLOCK), lambda i: (0, 0, 0))
    bsb = pl.BlockSpec((SG_BLOCK, 128), lambda i: (0, 0))
    return _pc(body, name="a_mid_bwd",
               out_shape=[jax.ShapeDtypeStruct((m, n3), BF16), jax.ShapeDtypeStruct((1, di), F32),
                          jax.ShapeDtypeStruct((1, di), F32),
                          jax.ShapeDtypeStruct((SG_GROUPS, SG_BLOCK, SG_BLOCK), F32),
                          jax.ShapeDtypeStruct((SG_BLOCK, 128), F32)],
               grid=(m // r,),
               in_specs=[pl.BlockSpec((r, n3), lambda i: (i, 0)), pl.BlockSpec((r, di), lambda i: (i, 0)),
                         vec, vec, wsb, bsb],
               out_specs=[pl.BlockSpec((r, n3), lambda i: (i, 0)), vec, vec, wsb, bsb],
               scratch=[pltpu.VMEM((r, di), F32), pltpu.VMEM((r, di), F32)],
               sem=("arbitrary",))(proj, dybr, ln_g, ln_b, w_s, bs_t)


def _hgrn_dims(t_seq, di):
    tr = _tile(t_seq, 512)
    hc = _tile(di, 512)
    return tr, hc, hc // HEAD_DIM


def _hgrn_gates(f_ref, lb, a_scr, k_scr, tr):
    sig = _sigmoid(f_ref[...])
    fg = lb + (1.0 - lb) * sig
    k_scr[...] = 1.0 - fg
    logf = jnp.log(fg)
    g = min(CUM_ROWS, tr)
    tri = _tri_mask(g, reverse=False)
    for rg in range(tr // g):
        a_scr[rg * g:(rg + 1) * g, :] = _tri_apply(tri, logf[rg * g:(rg + 1) * g, :])
    return sig, fg


def _hgrn_fwd(proj, lbj, gn, nb, t_seq):
    _, m, di = proj.shape
    tr, hc, hpg = _hgrn_dims(t_seq, di)
    nt, nhg, ncl = t_seq // tr, di // hc, tr // CHUNK
    nheads = di // HEAD_DIM

    def body(q_ref, f_ref, i_ref, g_ref, lb_ref, gn_ref, o_ref, ybr_ref, st_ref, st_scr, a_scr, k_scr):
        t = pl.program_id(2)

        @pl.when(t == 0)
        def _():
            st_scr[...] = jnp.zeros_like(st_scr)

        _hgrn_gates(f_ref, lb_ref[0:1, :], a_scr, k_scr, tr)
        gnv = gn_ref[...]
        rr = lax.broadcasted_iota(jnp.int32, (CHUNK, CHUNK), 0)
        cc = lax.broadcasted_iota(jnp.int32, (CHUNK, CHUNK), 1)
        causal = cc <= rr

        def chunk(n, carry):
            rows = pl.ds(pl.multiple_of(n * CHUNK, CHUNK), CHUNK)
            for hd in range(hpg):
                ls = slice(hd * HEAD_DIM, (hd + 1) * HEAD_DIM)
                ah, kh = a_scr[rows, ls], k_scr[rows, ls]
                qp = q_ref[rows, ls]
                qh = qp * _sigmoid(qp)
                vb = i_ref[rows, ls].astype(BF16)
                aref, alast = ah[CHUNK // 2 - 1:CHUNK // 2, :], ah[CHUNK - 1:CHUNK, :]
                q_in = (qh * jnp.exp(ah - aref)).astype(BF16)
                k_in = (kh * jnp.exp(aref - ah)).astype(BF16)
                scores = jnp.where(causal, _dot_nt(q_in, k_in), 0.0).astype(BF16)
                q_out = (qh * jnp.exp(ah)).astype(BF16)
                k_out = (kh * jnp.exp(alast - ah)).astype(BF16)
                st = st_scr[hd]
                st_ref[n, hd] = st
                o = _dot(scores, vb) + _dot_nt(q_out, st.astype(BF16))
                st_scr[hd] = st * jnp.exp(alast) + _dot_tn(vb, k_out)
                o_ref[rows, ls] = o
                rstd = lax.rsqrt(jnp.mean(o * o, axis=-1, keepdims=True) + EPS)
                gg = g_ref[rows, ls]
                ybr_ref[rows, ls] = ((o * rstd * gnv) * (gg * _sigmoid(gg))).astype(BF16)
            return carry

        lax.fori_loop(0, ncl, chunk, 0)

    def sec(s):
        return pl.BlockSpec((None, tr, hc), lambda hg, b, t: (s, b * nt + t, hg))

    blk = pl.BlockSpec((tr, hc), lambda hg, b, t: (b * nt + t, hg))
    return _pc(body, name="hgrn_fwd",
               out_shape=[jax.ShapeDtypeStruct((m, di), F32), jax.ShapeDtypeStruct((m, di), BF16),
                          jax.ShapeDtypeStruct((m // CHUNK, nheads, HEAD_DIM, HEAD_DIM), F32)],
               grid=(nhg, nb, nt),
               in_specs=[sec(0), sec(1), sec(2), sec(3), pl.BlockSpec((2, hc), lambda hg, b, t: (0, hg)),
                         pl.BlockSpec((1, HEAD_DIM), lambda hg, b, t: (0, 0))],
               out_specs=[blk, blk, pl.BlockSpec((ncl, hpg, HEAD_DIM, HEAD_DIM),
                                                 lambda hg, b, t: (b * nt + t, hg, 0, 0))],
               scratch=[pltpu.VMEM((hpg, HEAD_DIM, HEAD_DIM), F32), pltpu.VMEM((tr, hc), F32),
                        pltpu.VMEM((tr, hc), F32)],
               sem=("parallel", "arbitrary", "arbitrary"))(proj, proj, proj, proj, lbj, gn)


def _hgrn_bwd(proj, o_all, dybr, states, lbj, gn, nb, t_seq):
    _, m, di = proj.shape
    tr, hc, hpg = _hgrn_dims(t_seq, di)
    nt, nhg, ncl = t_seq // tr, di // hc, tr // CHUNK

    def body(q_ref, f_ref, i_ref, g_ref, o_ref, dy_ref, st_ref, lb_ref, gn_ref,
             dp_ref, dlb_ref, dgn_ref, dst_scr, a_scr, k_scr, da_scr, dk_scr):
        hg, b, t = pl.program_id(0), pl.program_id(1), pl.program_id(2)

        @pl.when(t == 0)
        def _():
            dst_scr[...] = jnp.zeros_like(dst_scr)

        @pl.when((b == 0) & (t == 0))
        def _():
            dlb_ref[...] = jnp.zeros_like(dlb_ref)

        @pl.when((hg == 0) & (b == 0) & (t == 0))
        def _():
            dgn_ref[...] = jnp.zeros_like(dgn_ref)

        lb = lb_ref[0:1, :]
        sig, fg = _hgrn_gates(f_ref, lb, a_scr, k_scr, tr)
        gnv = gn_ref[...]
        rr = lax.broadcasted_iota(jnp.int32, (CHUNK, CHUNK), 0)
        cc = lax.broadcasted_iota(jnp.int32, (CHUNK, CHUNK), 1)
        causal = cc <= rr
        rowi = lax.broadcasted_iota(jnp.int32, (CHUNK, HEAD_DIM), 0)

        def chunk(it, carry):
            n = ncl - 1 - it
            rows = pl.ds(pl.multiple_of(n * CHUNK, CHUNK), CHUNK)
            for hd in range(hpg):
                ls = slice(hd * HEAD_DIM, (hd + 1) * HEAD_DIM)
                ah, kh = a_scr[rows, ls], k_scr[rows, ls]
                qp = q_ref[rows, ls]
                sq = _sigmoid(qp)
                qh = qp * sq
                vb = i_ref[rows, ls].astype(BF16)
                aref, alast = ah[CHUNK // 2 - 1:CHUNK // 2, :], ah[CHUNK - 1:CHUNK, :]
                e1, e2, e3, e4 = jnp.exp(ah - aref), jnp.exp(aref - ah), jnp.exp(ah), jnp.exp(alast - ah)
                dec = jnp.exp(alast)
                q_in, k_in, q_out, k_out = qh * e1, kh * e2, qh * e3, kh * e4
                q_in_b, k_in_b, q_out_b, k_out_b = (z.astype(BF16) for z in (q_in, k_in, q_out, k_out))
                scores = jnp.where(causal, _dot_nt(q_in_b, k_in_b), 0.0).astype(BF16)
                o = o_ref[rows, ls]
                rstd = lax.rsqrt(jnp.mean(o * o, axis=-1, keepdims=True) + EPS)
                ohat = o * rstd
                gg = g_ref[rows, ls]
                sg = _sigmoid(gg)
                dyv = dy_ref[rows, ls]
                d_on = dyv * (gg * sg)
                dp_ref[3, rows, ls] = (dyv * (ohat * gnv) * (sg * (1.0 + gg * (1.0 - sg)))).astype(BF16)
                dgn_ref[...] += jnp.sum(d_on * ohat, axis=0, keepdims=True)
                dohat = d_on * gnv
                do = rstd * (dohat - ohat * jnp.mean(dohat * ohat, axis=-1, keepdims=True))
                do_b = do.astype(BF16)
                st_prev = st_ref[n, hd]
                dst = dst_scr[hd]
                dst_b = dst.astype(BF16)
                dscores = jnp.where(causal, _dot_nt(do_b, vb), 0.0).astype(BF16)
                dv = _dot_tn(scores, do_b) + _dot_nt(k_out_b, dst_b)
                dq_in = _dot(dscores, k_in_b)
                dk_in = _dot_tn(dscores, q_in_b)
                dq_out = _dot(do_b, st_prev.astype(BF16))
                dk_out = _dot(vb, dst_b)
                ddec = jnp.sum(dst * st_prev, axis=0, keepdims=True)
                dst_scr[hd] = dst * dec + _dot_tn(do_b, q_out_b)
                dp_ref[2, rows, ls] = dv.astype(BF16)
                dq = dq_in * e1 + dq_out * e3
                dp_ref[0, rows, ls] = (dq * (sq * (1.0 + qp * (1.0 - sq)))).astype(BF16)
                dk_scr[rows, ls] = dk_in * e2 + dk_out * e4
                t_in = dq_in * q_in - dk_in * k_in
                t_out = dk_out * k_out
                da = t_in + dq_out * q_out - t_out
                da_ref_row = -jnp.sum(t_in, axis=0, keepdims=True)
                da_last_row = jnp.sum(t_out, axis=0, keepdims=True) + ddec * dec
                da = da + jnp.where(rowi == CHUNK // 2 - 1, da_ref_row, 0.0) \
                        + jnp.where(rowi == CHUNK - 1, da_last_row, 0.0)
                da_scr[rows, ls] = da
            return carry

        lax.fori_loop(0, ncl, chunk, 0)
        g = min(CUM_ROWS, tr)
        tri = _tri_mask(g, reverse=True)
        for rg in range(tr // g):
            rs = slice(rg * g, (rg + 1) * g)
            dlogf = _tri_apply(tri, da_scr[rs, :])
            df = dlogf / fg[rs, :] - dk_scr[rs, :]
            sgr = sig[rs, :]
            dp_ref[1, rs, :] = (df * (1.0 - lb) * (sgr * (1.0 - sgr))).astype(BF16)
            dlb_ref[...] += jnp.sum(df * (1.0 - sgr), axis=0, keepdims=True) * lb_ref[1:2, :]

    def sec(s):
        return pl.BlockSpec((None, tr, hc), lambda hg, b, t: (s, b * nt + (nt - 1 - t), hg))

    blk = pl.BlockSpec((tr, hc), lambda hg, b, t: (b * nt + (nt - 1 - t), hg))
    return _pc(body, name="hgrn_bwd",
               out_shape=[jax.ShapeDtypeStruct((4, m, di), BF16), jax.ShapeDtypeStruct((1, di), F32),
                          jax.ShapeDtypeStruct((1, HEAD_DIM), F32)],
               grid=(nhg, nb, nt),
               in_specs=[sec(0), sec(1), sec(2), sec(3), blk, blk,
                         pl.BlockSpec((ncl, hpg, HEAD_DIM, HEAD_DIM),
                                      lambda hg, b, t: (b * nt + (nt - 1 - t), hg, 0, 0)),
                         pl.BlockSpec((2, hc), lambda hg, b, t: (0, hg)),
                         pl.BlockSpec((1, HEAD_DIM), lambda hg, b, t: (0, 0))],
               out_specs=[pl.BlockSpec((4, tr, hc), lambda hg, b, t: (0, b * nt + (nt - 1 - t), hg)),
                          pl.BlockSpec((1, hc), lambda hg, b, t: (0, hg)),
                          pl.BlockSpec((1, HEAD_DIM), lambda hg, b, t: (0, 0))],
               scratch=[pltpu.VMEM((hpg, HEAD_DIM, HEAD_DIM), F32)] + [pltpu.VMEM((tr, hc), F32)] * 4,
               sem=("arbitrary", "arbitrary", "arbitrary"))(proj, proj, proj, proj, o_all, dybr, states, lbj, gn)


def _final_loss(x, gain, target):
    m, d = x.shape
    tm = _tile(m, 512)

    def body(x_ref, g_ref, t_ref, dx_ref, loss_ref, dg_ref):
        i = pl.program_id(0)
        xv, g = x_ref[...], g_ref[...]
        rstd = lax.rsqrt(jnp.mean(xv * xv, axis=-1, keepdims=True) + EPS)
        xhat = xv * rstd
        err = xhat * g - t_ref[...]
        dy = err * (1.0 / d)
        dxhat = dy * g
        dx_ref[...] = rstd * (dxhat - xhat * jnp.mean(dxhat * xhat, axis=-1, keepdims=True))

        @pl.when(i == 0)
        def _():
            loss_ref[...] = jnp.zeros_like(loss_ref)
            dg_ref[...] = jnp.zeros_like(dg_ref)

        loss_ref[...] += 0.5 * jnp.sum(jnp.mean(err * err, axis=-1, keepdims=True), axis=0, keepdims=True)
        dg_ref[...] += jnp.sum(dy * xhat, axis=0, keepdims=True)

    row = pl.BlockSpec((tm, d), lambda i: (i, 0))
    return _pc(body, name="final_loss",
               out_shape=[jax.ShapeDtypeStruct((m, d), F32), jax.ShapeDtypeStruct((1, 1), F32),
                          jax.ShapeDtypeStruct((1, d), F32)],
               grid=(m // tm,),
               in_specs=[row, pl.BlockSpec((1, d), lambda i: (0, 0)), row],
               out_specs=[row, pl.BlockSpec((1, 1), lambda i: (0, 0)), pl.BlockSpec((1, d), lambda i: (0, 0))],
               sem=("arbitrary",))(x, gain, target)


def _adamw(parts, w, m, v, name):
    r, c = w.shape
    tr = _tile(r, 256)
    npart = len(parts)
    c1 = 1.0 - ADAM_B1 ** ADAM_STEP
    c2 = 1.0 - ADAM_B2 ** ADAM_STEP

    def body(*refs):
        p_refs = refs[:npart]
        w_ref, m_ref, v_ref, g_ref, d_ref, nm_ref, nv_ref = refs[npart:]
        g = p_refs[0][...]
        for p in p_refs[1:]:
            g = g + p[...]
        nm = ADAM_B1 * m_ref[...] + (1.0 - ADAM_B1) * g
        nv = ADAM_B2 * v_ref[...] + (1.0 - ADAM_B2) * (g * g)
        g_ref[...] = g
        nm_ref[...] = nm
        nv_ref[...] = nv
        d_ref[...] = -ADAM_LR * ((nm / c1) / (jnp.sqrt(nv / c2) + ADAM_EPS) + ADAM_WD * w_ref[...])

    blk = pl.BlockSpec((tr, c), lambda i: (i, 0))
    return _pc(body, name=name, out_shape=[jax.ShapeDtypeStruct((r, c), F32)] * 4, grid=(r // tr,),
               in_specs=[blk] * (npart + 3), out_specs=[blk] * 4, sem=("parallel",))(*parts, w, m, v)


_SMALL = ["norm_gain", "a_ln_gain", "a_ln_bias", "a_w_s", "a_b_s", "b_lower_bounds", "b_gn_gain", "final_gain"]


def _pack(arrs):
    flat = jnp.concatenate([a.reshape(-1) for a in arrs])
    rows = -(-flat.shape[0] // 1024) * 8
    return jnp.pad(flat, (0, rows * 128 - flat.shape[0])).reshape(rows, 128)


def _unpack(buf, like):
    flat = buf.reshape(-1)
    out, off = [], 0
    for a in like:
        out.append(flat[off:off + a.size].reshape(a.shape))
        off += a.size
    return out


def _reduce_scatter_adamw(grads, ws, ms, vs):
    x, y, c = lax.axis_index("x"), lax.axis_index("y"), lax.axis_index("c")
    n = len(grads)
    r1 = _exchange(grads, 4, lambda x, y, c: [1 - c, 3 - c, 5 - c, 7 - c], lambda x, y, c: (x, y, 1 - c), "rs_core")
    keep1 = jnp.stack([c, 2 + c, 4 + c, 6 + c]).astype(jnp.int32)
    h1 = [_add_sel(grads[a], keep1, r1[a], "rs_add_core_%d" % a) for a in range(n)]
    r2 = _exchange(h1, 2, lambda x, y, c: [2 * (1 - x), 2 * (1 - x) + 1], lambda x, y, c: (1 - x, y, c), "rs_x")
    keep2 = jnp.stack([2 * x, 2 * x + 1]).astype(jnp.int32)
    h2 = [_add_sel(h1[a], keep2, r2[a], "rs_add_x_%d" % a) for a in range(n)]
    r3 = _exchange(h2, 1, lambda x, y, c: [1 - y], lambda x, y, c: (x, 1 - y, c), "rs_y")
    out = []
    for a in range(n):
        mine = lax.dynamic_index_in_dim(h2[a], y, 0, keepdims=False)
        out.append(_adamw([mine, r3[a][0]], ws[a], ms[a], vs[a], "adamw_big_%d" % a))
    return out


def kernel(x, c, norm_gain, w_ada, b_ada, a_w_in, a_ln_gain, a_ln_bias, a_w_s, a_b_s, a_w_out, b_w_in, b_lower_bounds, b_gn_gain, b_w_out, final_gain, loss_target, m_norm_gain, m_w_ada, m_b_ada, m_a_w_in, m_a_ln_gain, m_a_ln_bias, m_a_w_s, m_a_b_s, m_a_w_out, m_b_w_in, m_b_lower_bounds, m_b_gn_gain, m_b_w_out, m_final_gain, v_norm_gain, v_w_ada, v_b_ada, v_a_w_in, v_a_ln_gain, v_a_ln_bias, v_a_w_s, v_a_b_s, v_a_w_out, v_b_w_in, v_b_lower_bounds, v_b_gn_gain, v_b_w_out, v_final_gain):
    w = dict(norm_gain=norm_gain, w_ada=w_ada, b_ada=b_ada, a_w_in=a_w_in, a_ln_gain=a_ln_gain,
             a_ln_bias=a_ln_bias, a_w_s=a_w_s, a_b_s=a_b_s, a_w_out=a_w_out, b_w_in=b_w_in,
             b_lower_bounds=b_lower_bounds, b_gn_gain=b_gn_gain, b_w_out=b_w_out, final_gain=final_gain)
    mo = dict(norm_gain=m_norm_gain, w_ada=m_w_ada, b_ada=m_b_ada, a_w_in=m_a_w_in, a_ln_gain=m_a_ln_gain,
              a_ln_bias=m_a_ln_bias, a_w_s=m_a_w_s, a_b_s=m_a_b_s, a_w_out=m_a_w_out, b_w_in=m_b_w_in,
              b_lower_bounds=m_b_lower_bounds, b_gn_gain=m_b_gn_gain, b_w_out=m_b_w_out, final_gain=m_final_gain)
    vo = dict(norm_gain=v_norm_gain, w_ada=v_w_ada, b_ada=v_b_ada, a_w_in=v_a_w_in, a_ln_gain=v_a_ln_gain,
              a_ln_bias=v_a_ln_bias, a_w_s=v_a_w_s, a_b_s=v_a_b_s, a_w_out=v_a_w_out, b_w_in=v_b_w_in,
              b_lower_bounds=v_b_lower_bounds, b_gn_gain=v_b_gn_gain, b_w_out=v_b_w_out, final_gain=v_final_gain)

    nb, t_seq, d = x.shape
    m = nb * t_seq
    ncol_ada = w_ada.shape[2]
    xi, yi, ci = lax.axis_index("x"), lax.axis_index("y"), lax.axis_index("c")
    me = 4 * xi + 2 * yi + ci

    c_all = _all_gather([c], "gather_c")[0].reshape(NDEV * nb, d)
    b_cols = lax.dynamic_slice(b_ada, (0, me * ncol_ada), (2, ncol_ada)).reshape(2, 1, ncol_ada)
    mod_part, lbj = _ada_fwd(c_all, w_ada, b_cols, b_lower_bounds)
    mod_all = _all_gather([mod_part], "gather_mod")[0]
    mod_mine = lax.dynamic_slice_in_dim(mod_all, me * nb, nb, axis=2)
    mod_mine = mod_mine.transpose(1, 2, 0, 3).reshape(2, nb, 3, d)
    mod0, mod1 = mod_mine[0], mod_mine[1]

    wa_in_g, wa_out_g, wb_in_g, wb_out_g = _all_gather(
        [a_w_in[0].astype(BF16), a_w_out[0].astype(BF16), b_w_in[0].astype(BF16), b_w_out[0].astype(BF16)],
        "gather_weights")
    di = wa_out_g.shape[0] * wa_out_g.shape[1]
    wa_out = wa_out_g.reshape(di, d)
    wb_out = wb_out_g.reshape(di, d)

    xf = x.reshape(m, d)
    tgt = loss_target.reshape(m, d)
    ng0, ng1 = norm_gain[0:1], norm_gain[1:2]
    h0 = _prenorm(xf, ng0, mod0, t_seq, "prenorm_a")
    proj_a = _mm_in(h0, wa_in_g, 1, "in_proj_a")
    bs_t = jnp.pad(a_b_s[0].T, ((0, 0), (0, 128 - SG_GROUPS)))
    ybr_a = _a_mid_fwd(proj_a, a_ln_gain, a_ln_bias, a_w_s[0], bs_t, t_seq)
    yout_a, x1 = _out_proj(ybr_a, wa_out, xf, mod0, t_seq, "out_proj_a")
    h1 = _prenorm(x1, ng1, mod1, t_seq, "prenorm_b")
    proj_b = _mm_in(h1, wb_in_g, 4, "in_proj_b")
    o_b, ybr_b, states = _hgrn_fwd(proj_b, lbj, b_gn_gain, nb, t_seq)
    yout_b, x2 = _out_proj(ybr_b, wb_out, x1, mod1, t_seq, "out_proj_b")
    dx2, loss_part, d_final_gain = _final_loss(x2, final_gain.reshape(1, d), tgt)
    loss = lax.psum(loss_part[0, 0], ("x", "y", "c"))

    dy_b, dgate1 = _gate_bwd(dx2, yout_b, mod1, t_seq, "gate_bwd_b")
    dybr_b = _mm_dybr(dy_b, wb_out, "dybr_b")
    g_wb_out = _mm_dw_out(ybr_b, dy_b, "dw_out_b")
    dproj_b, d_lb, d_gn = _hgrn_bwd(proj_b, o_b, dybr_b, states, lbj, b_gn_gain, nb, t_seq)
    dh1 = _mm_din(dproj_b, wb_in_g, 4, "dh_b")
    dx1, dss1, dgain1 = _prenorm_bwd(dh1, x1, ng1, mod1, dx2, t_seq, "prenorm_bwd_b")
    g_wb_in = _mm_dw_in(h1, dproj_b, wb_in_g.shape[2], 4, "dw_in_b")

    dy_a, dgate0 = _gate_bwd(dx1, yout_a, mod0, t_seq, "gate_bwd_a")
    dybr_a = _mm_dybr(dy_a, wa_out, "dybr_a")
    g_wa_out = _mm_dw_out(ybr_a, dy_a, "dw_out_a")
    dproj_a, d_lng, d_lnb, d_ws, d_bs_t = _a_mid_bwd(proj_a, dybr_a, a_ln_gain, a_ln_bias, a_w_s[0], bs_t, t_seq)
    dh0 = _mm_din(dproj_a, wa_in_g, 1, "dh_a")
    dx0, dss0, dgain0 = _prenorm_bwd(dh0, xf, ng0, mod0, dx1, t_seq, "prenorm_bwd_a")
    g_wa_in = _mm_dw_in(h0, dproj_a, wa_in_g.shape[2], 1, "dw_in_a")
    grad_x = dx0.reshape(nb, t_seq, d)

    dmod = jnp.stack([jnp.concatenate([dss0, dgate0], axis=1), jnp.concatenate([dss1, dgate1], axis=1)])
    dmod_all = _all_gather([dmod.reshape(2, nb, 3 * d)], "gather_dmod")[0]
    dmod_all = dmod_all.transpose(1, 0, 2, 3).reshape(2, NDEV * nb, 3 * d)
    dmod_cols = lax.dynamic_slice_in_dim(dmod_all, me * ncol_ada, ncol_ada, axis=2)
    g_w_ada, g_b_ada = _ada_bwd(c_all, dmod_cols, dmod_all)

    part = dict(norm_gain=jnp.concatenate([dgain0, dgain1], axis=0), a_ln_gain=d_lng, a_ln_bias=d_lnb,
                a_w_s=d_ws[None], a_b_s=d_bs_t[:, :SG_GROUPS].T[None],
                b_lower_bounds=jnp.concatenate([-d_lb, d_lb], axis=0), b_gn_gain=d_gn, final_gain=d_final_gain[0])
    small_like = [w[k] for k in _SMALL]
    parts_all = _all_gather([_pack([part[k].reshape(w[k].shape) for k in _SMALL])], "gather_small")[0]
    sm = _adamw([parts_all[k] for k in range(NDEV)], _pack(small_like), _pack([mo[k] for k in _SMALL]),
                _pack([vo[k] for k in _SMALL]), "adamw_small")
    sm = [dict(zip(_SMALL, _unpack(buf, small_like))) for buf in sm]

    res = {}
    for k in _SMALL:
        res[k] = tuple(s[k] for s in sm)
    rb = _adamw([g_b_ada], b_ada, mo["b_ada"], vo["b_ada"], "adamw_b_ada")
    res["b_ada"] = tuple(rb)
    sh = w_ada.shape
    ra = _adamw([g_w_ada.reshape(sh[0] * sh[1], sh[2])], w_ada.reshape(sh[0] * sh[1], sh[2]),
                mo["w_ada"].reshape(sh[0] * sh[1], sh[2]), vo["w_ada"].reshape(sh[0] * sh[1], sh[2]), "adamw_w_ada")
    res["w_ada"] = tuple(z.reshape(sh) for z in ra)

    big = ["a_w_in", "a_w_out", "b_w_in", "b_w_out"]
    rows_out = a_w_out.shape[1]
    grads = [g_wa_in, g_wa_out.reshape(NDEV, rows_out, d), g_wb_in, g_wb_out.reshape(NDEV, rows_out, d)]
    rbig = _reduce_scatter_adamw(grads, [w[k][0] for k in big], [mo[k][0] for k in big], [vo[k][0] for k in big])
    for k, r in zip(big, rbig):
        res[k] = tuple(z[None] for z in r)

    order = ["norm_gain", "w_ada", "b_ada", "a_w_in", "a_ln_gain", "a_ln_bias", "a_w_s", "a_b_s", "a_w_out",
             "b_w_in", "b_lower_bounds", "b_gn_gain", "b_w_out", "final_gain"]
    return (loss, grad_x, *[res[k][0] for k in order], *[res[k][1] for k in order],
            *[res[k][2] for k in order], *[res[k][3] for k in order])
```

```python
import functools
import math

import jax
import jax.numpy as jnp
from jax import lax
from jax.experimental import pallas as pl
from jax.experimental.pallas import tpu as pltpu

F32 = jnp.float32
BF16 = jnp.bfloat16
MESH = pl.DeviceIdType.MESH
NDEV = 8
EPS = 1e-6
CHUNK = 64
SG_BLOCK = 128
SG_GROUPS = 8
HEAD_DIM = 128
CUM_ROWS = 256
ADAM_LR, ADAM_B1, ADAM_B2, ADAM_EPS, ADAM_WD, ADAM_STEP = 0.001, 0.9, 0.999, 1e-08, 0.01, 10
VMEM_LIMIT = 56 * 1024 * 1024
ANY = pl.BlockSpec(memory_space=pl.ANY)


class _Hosted:
    def __init__(self, arrays, out_shapes, nsem, start, finish, aliases=None):
        self.arrays, self.out_shapes, self.nsem = list(arrays), list(out_shapes), nsem
        self.start, self.finish = start, finish
        self.aliases = dict(aliases or {})


def _join(*comms):
    arrays, outs, aliases, offs, nsem = [], [], {}, [], 0
    for cm in comms:
        offs.append((len(arrays), len(outs), nsem))
        for i, o in cm.aliases.items():
            aliases[len(arrays) + i] = len(outs) + o
        arrays += cm.arrays
        outs += cm.out_shapes
        nsem += cm.nsem

    def run(which):
        def f(ins, outs_, ss, rs, base):
            for cm, (ia, io, isem) in zip(comms, offs):
                getattr(cm, which)(ins[ia:ia + len(cm.arrays)], outs_[io:io + len(cm.out_shapes)], ss, rs, base + isem)
        return f

    return _Hosted(arrays, outs, nsem, run("start"), run("finish"), aliases)


def _pc(body, *, name, out_shape, grid=None, in_specs=None, out_specs=None, scratch=(), sem=None,
        grid_spec=None, comm=None):
    cp = dict(vmem_limit_bytes=VMEM_LIMIT)
    if comm is None:
        if sem is not None:
            cp["dimension_semantics"] = sem
        kw = {}
        if grid_spec is not None:
            kw["grid_spec"] = grid_spec
        else:
            if grid is not None:
                kw["grid"] = grid
            if in_specs is not None:
                kw["in_specs"] = in_specs
            if out_specs is not None:
                kw["out_specs"] = out_specs
            kw["scratch_shapes"] = list(scratch)
        return pl.pallas_call(functools.partial(body), name=name, out_shape=out_shape,
                              compiler_params=pltpu.CompilerParams(**cp), **kw)

    single = not isinstance(out_shape, (list, tuple))
    outs_list = [out_shape] if single else list(out_shape)
    ospecs = [out_specs] if single else list(out_specs)
    n_in, n_out, n_ci, n_co, n_scr = len(in_specs), len(outs_list), len(comm.arrays), len(comm.out_shapes), len(scratch)
    cp["dimension_semantics"] = ("arbitrary",) * len(grid)

    def hosted(*refs):
        cin, hin = refs[:n_in], refs[n_in:n_in + n_ci]
        cout = refs[n_in + n_ci:n_in + n_ci + n_out]
        hout = refs[n_in + n_ci + n_out:n_in + n_ci + n_out + n_co]
        scr = refs[n_in + n_ci + n_out + n_co:n_in + n_ci + n_out + n_co + n_scr]
        ssem, rsem = refs[-2], refs[-1]
        first = functools.reduce(lambda p, q: p & q, [pl.program_id(a) == 0 for a in range(len(grid))])
        last = functools.reduce(lambda p, q: p & q, [pl.program_id(a) == grid[a] - 1 for a in range(len(grid))])

        @pl.when(first)
        def _():
            comm.start(hin, hout, ssem, rsem, 0)

        body(*cin, *cout, *scr)

        @pl.when(last)
        def _():
            comm.finish(hin, hout, ssem, rsem, 0)

    call = pl.pallas_call(
        hosted, name=name, grid=grid, in_specs=list(in_specs) + [ANY] * n_ci, out_specs=ospecs + [ANY] * n_co,
        out_shape=outs_list + comm.out_shapes,
        scratch_shapes=list(scratch) + [pltpu.SemaphoreType.DMA((comm.nsem,)), pltpu.SemaphoreType.DMA((comm.nsem,))],
        input_output_aliases={n_in + i: n_out + o for i, o in comm.aliases.items()},
        compiler_params=pltpu.CompilerParams(**cp))

    def run(*args):
        res = call(*args, *comm.arrays)
        comp = res[:n_out]
        return (comp[0] if single else comp), list(res[n_out:])

    return run


def _tile(n, pref):
    return pref if n % pref == 0 else n


def _sigmoid(x):
    return 1.0 / (1.0 + jnp.exp(-x))


def _gelu(x):
    c = math.sqrt(2.0 / math.pi)
    return 0.5 * x * (1.0 + jnp.tanh(c * (x + 0.044715 * (x * x * x))))


def _dgelu(x):
    c = math.sqrt(2.0 / math.pi)
    t = jnp.tanh(c * (x + 0.044715 * (x * x * x)))
    return 0.5 * (1.0 + t) + 0.5 * x * (1.0 - t * t) * (c * (1.0 + 3.0 * 0.044715 * (x * x)))


def _dot(a, b):
    return jnp.dot(a, b, preferred_element_type=F32)


def _dot_nt(a, b):
    return lax.dot_general(a, b, (((1,), (1,)), ((), ())), preferred_element_type=F32)


def _dot_tn(a, b):
    return lax.dot_general(a, b, (((0,), (0,)), ((), ())), preferred_element_type=F32)


def _tri_mask(n, reverse):
    r = lax.broadcasted_iota(jnp.int32, (n, n), 0)
    c = lax.broadcasted_iota(jnp.int32, (n, n), 1)
    same = (r // CHUNK) == (c // CHUNK)
    tri = (c >= r) if reverse else (c <= r)
    return jnp.where(same & tri, 1.0, 0.0).astype(BF16)


def _tri_apply(tri, x):
    hi = x.astype(BF16)
    r1 = x - hi.astype(F32)
    mid = r1.astype(BF16)
    lo = (r1 - mid.astype(F32)).astype(BF16)
    return _dot(tri, hi) + (_dot(tri, mid) + _dot(tri, lo))


def _all_gather(arrs, name):
    n = len(arrs)

    def body(*refs):
        ins, outs = refs[:n], refs[n:2 * n]
        send_sems, recv_sems, local_sems = refs[2 * n:]
        x, y, c = lax.axis_index("x"), lax.axis_index("y"), lax.axis_index("c")
        me, sibling = (x, y, c), (x, y, 1 - c)
        chips = [(1 - x, y), (x, 1 - y), (1 - x, 1 - y)]

        def blk(a, p):
            return outs[a].at[4 * p[0] + 2 * p[1] + p[2]]

        def copy(a, k, block, to, src=None):
            return pltpu.make_async_remote_copy(
                src_ref=blk(a, block) if src is None else src, dst_ref=blk(a, block),
                send_sem=send_sems.at[7 * a + k], recv_sem=recv_sems.at[7 * a + k],
                device_id=to, device_id_type=MESH)

        mine = [pltpu.make_async_copy(ins[a], blk(a, me), local_sems.at[a]) for a in range(n)]
        for m in mine:
            m.start()
        first = []
        for a in range(n):
            first.append(copy(a, 0, me, sibling, src=ins[a]))
            for j, chip in enumerate(chips):
                first.append(copy(a, 1 + j, me, (*chip, c), src=ins[a]))
        for cp in first:
            cp.start()
        passed = []
        for j, chip in enumerate(chips):
            for a in range(n):
                copy(a, 1 + j, (*chip, c), me).wait_recv()
                p = copy(a, 4 + j, (*chip, c), sibling)
                p.start()
                passed.append(p)
        for a in range(n):
            copy(a, 0, sibling, me).wait_recv()
            for j, chip in enumerate(chips):
                copy(a, 4 + j, (*chip, 1 - c), me).wait_recv()
        for cp in first + passed:
            cp.wait_send()
        for m in mine:
            m.wait()

    out_shape = [jax.ShapeDtypeStruct((NDEV,) + a.shape, a.dtype) for a in arrs]
    return _pc(body, name=name, out_shape=out_shape, in_specs=[ANY] * n, out_specs=[ANY] * n,
               scratch=[pltpu.SemaphoreType.DMA((7 * n,)), pltpu.SemaphoreType.DMA((7 * n,)),
                        pltpu.SemaphoreType.DMA((n,))])(*arrs)


def _gather_first(arrs):
    n = len(arrs)

    def parts(ins, outs, ss, rs, base):
        x, y, c = lax.axis_index("x"), lax.axis_index("y"), lax.axis_index("c")
        me, sibling = (x, y, c), (x, y, 1 - c)
        chips = [(1 - x, y), (x, 1 - y), (1 - x, 1 - y)]

        def blk(a, p):
            return outs[a].at[4 * p[0] + 2 * p[1] + p[2]]

        def copy(a, k, block, to):
            return pltpu.make_async_remote_copy(
                src_ref=ins[a], dst_ref=blk(a, block), send_sem=ss.at[base + 4 * a + k],
                recv_sem=rs.at[base + 4 * a + k], device_id=to, device_id_type=MESH)

        local = [pltpu.make_async_copy(ins[a], blk(a, me), ss.at[base + 4 * n + a]) for a in range(n)]
        sends, recvs = [], []
        for a in range(n):
            sends.append(copy(a, 0, me, sibling))
            recvs.append(copy(a, 0, sibling, me))
            for j, chip in enumerate(chips):
                sends.append(copy(a, 1 + j, me, (*chip, c)))
                recvs.append(copy(a, 1 + j, (*chip, c), me))
        return local, sends, recvs

    def start(ins, outs, ss, rs, base):
        local, sends, _ = parts(ins, outs, ss, rs, base)
        for cp in local + sends:
            cp.start()

    def finish(ins, outs, ss, rs, base):
        local, sends, recvs = parts(ins, outs, ss, rs, base)
        for cp in recvs:
            cp.wait_recv()
        for cp in sends:
            cp.wait_send()
        for cp in local:
            cp.wait()

    return _Hosted(arrs, [jax.ShapeDtypeStruct((NDEV,) + a.shape, a.dtype) for a in arrs], 5 * n, start, finish)


def _gather_second(bufs):
    n = len(bufs)

    def parts(ins, outs, ss, rs, base):
        x, y, c = lax.axis_index("x"), lax.axis_index("y"), lax.axis_index("c")
        sibling = (x, y, 1 - c)
        chips = [(1 - x, y), (x, 1 - y), (1 - x, 1 - y)]
        sends, recvs = [], []
        for a in range(n):
            for j, chip in enumerate(chips):
                mine = 4 * chip[0] + 2 * chip[1] + c
                theirs = 4 * chip[0] + 2 * chip[1] + (1 - c)
                sends.append(pltpu.make_async_remote_copy(
                    src_ref=ins[a].at[mine], dst_ref=outs[a].at[mine], send_sem=ss.at[base + 3 * a + j],
                    recv_sem=rs.at[base + 3 * a + j], device_id=sibling, device_id_type=MESH))
                recvs.append(pltpu.make_async_remote_copy(
                    src_ref=ins[a].at[theirs], dst_ref=outs[a].at[theirs], send_sem=ss.at[base + 3 * a + j],
                    recv_sem=rs.at[base + 3 * a + j], device_id=sibling, device_id_type=MESH))
        return sends, recvs

    def start(ins, outs, ss, rs, base):
        for cp in parts(ins, outs, ss, rs, base)[0]:
            cp.start()

    def finish(ins, outs, ss, rs, base):
        sends, recvs = parts(ins, outs, ss, rs, base)
        for cp in recvs:
            cp.wait_recv()
        for cp in sends:
            cp.wait_send()

    return _Hosted(bufs, [jax.ShapeDtypeStruct(b.shape, b.dtype) for b in bufs], 3 * n, start, finish,
                   aliases={a: a for a in range(n)})


def _swap(src, nblk, ids_fn, partner_fn):
    def copies(ins, outs, ss, rs, base):
        x, y, c = lax.axis_index("x"), lax.axis_index("y"), lax.axis_index("c")
        ids = ids_fn(x, y, c)
        partner = partner_fn(x, y, c)
        return [pltpu.make_async_remote_copy(
            src_ref=ins[0].at[ids[k]], dst_ref=outs[0].at[k], send_sem=ss.at[base + k], recv_sem=rs.at[base + k],
            device_id=partner, device_id_type=MESH) for k in range(nblk)]

    def start(ins, outs, ss, rs, base):
        for cp in copies(ins, outs, ss, rs, base):
            cp.start()

    def finish(ins, outs, ss, rs, base):
        for cp in copies(ins, outs, ss, rs, base):
            cp.wait()

    return _Hosted([src], [jax.ShapeDtypeStruct((nblk,) + src.shape[1:], src.dtype)], nblk, start, finish)


def _blocking(comm, name):
    n_i, n_o = len(comm.arrays), len(comm.out_shapes)

    def body(*refs):
        ins, outs = refs[:n_i], refs[n_i:n_i + n_o]
        comm.start(ins, outs, refs[-2], refs[-1], 0)
        comm.finish(ins, outs, refs[-2], refs[-1], 0)

    return pl.pallas_call(
        body, name=name, out_shape=comm.out_shapes, in_specs=[ANY] * n_i, out_specs=[ANY] * n_o,
        scratch_shapes=[pltpu.SemaphoreType.DMA((comm.nsem,)), pltpu.SemaphoreType.DMA((comm.nsem,))],
        input_output_aliases=comm.aliases)(*comm.arrays)


def _add_split(a, b, idx, nh, name):
    _, r, c = a.shape
    tr = _tile(r, 256)

    def body(idx_ref, ak_ref, bk_ref, as_ref, bs_ref, keep_ref, send_ref):
        keep_ref[...] = ak_ref[...] + bk_ref[...].astype(F32)
        send_ref[...] = (as_ref[...] + bs_ref[...].astype(F32)).astype(BF16)

    def sel(off):
        return pl.BlockSpec((None, tr, c), lambda k, i, s: (s[off + k], i, 0))

    out = pl.BlockSpec((None, tr, c), lambda k, i, s: (k, i, 0))
    gs = pltpu.PrefetchScalarGridSpec(num_scalar_prefetch=1, grid=(nh, r // tr),
                                      in_specs=[sel(0), sel(nh), sel(2 * nh), sel(3 * nh)], out_specs=[out, out])
    return _pc(body, name=name, grid_spec=gs, sem=("arbitrary", "arbitrary"),
               out_shape=[jax.ShapeDtypeStruct((nh, r, c), F32), jax.ShapeDtypeStruct((nh, r, c), BF16)])(
                   idx, a, b, a, b)


class _ReduceScatter:
    def __init__(self, g, tag):
        self.g, self.tag = g, tag
        self.x, self.y, self.c = lax.axis_index("x"), lax.axis_index("y"), lax.axis_index("c")

    def swap_core(self):
        return _swap(self.g, 4, lambda x, y, c: [1 - c, 3 - c, 5 - c, 7 - c], lambda x, y, c: (x, y, 1 - c))

    def after_core(self, recv):
        x, c = self.x, self.c
        idx = jnp.stack([4 * x + c, 4 * x + 2 + c, 2 * x, 2 * x + 1,
                         4 * (1 - x) + c, 4 * (1 - x) + 2 + c, 2 * (1 - x), 2 * (1 - x) + 1]).astype(jnp.int32)
        self.keep_x, self.send_x = _add_split(self.g, recv, idx, 2, "rs_add_core_" + self.tag)

    def swap_x(self):
        return _swap(self.send_x, 2, lambda x, y, c: [0, 1], lambda x, y, c: (1 - x, y, c))

    def after_x(self, recv):
        y = self.y
        idx = jnp.stack([y, y, 1 - y, 1 - y]).astype(jnp.int32)
        self.keep_y, self.send_y = _add_split(self.keep_x, recv, idx, 1, "rs_add_x_" + self.tag)

    def swap_y(self):
        return _swap(self.send_y, 1, lambda x, y, c: [0], lambda x, y, c: (x, 1 - y, c))

    def after_y(self, recv):
        self.parts = [self.keep_y[0], recv[0]]


def _ada_fwd(c_all, w_ada, b_cols, b_lb):
    nl, d, ncol = w_ada.shape
    nseq = c_all.shape[0]
    di = b_lb.shape[1]

    def body(c_ref, w_ref, b_ref, lb_ref, mod_ref, lbj_ref):
        cv = c_ref[...]
        cact = (cv * _sigmoid(cv)).astype(BF16)
        for l in range(nl):
            mod_ref[l] = _dot(cact, w_ref[l].astype(BF16)) + b_ref[l]
        b0, b1 = lb_ref[0:1, :], lb_ref[1:2, :]
        mx = jnp.maximum(b0, b1)
        e0, e1 = jnp.exp(b0 - mx), jnp.exp(b1 - mx)
        s = e0 + e1
        p0, p1 = e0 / s, e1 / s
        lbj_ref[0:1, :] = (p0 + p1) - p0
        lbj_ref[1:2, :] = p0 * p1

    return _pc(body, name="ada_fwd",
               out_shape=[jax.ShapeDtypeStruct((nl, nseq, ncol), F32), jax.ShapeDtypeStruct((2, di), F32)]
               )(c_all, w_ada, b_cols, b_lb)


def _ada_bwd(c_all, dmod_cols, dmod_full):
    nl, nseq, ncol = dmod_cols.shape
    d = c_all.shape[1]
    d3 = dmod_full.shape[2]

    def body(c_ref, dc_ref, df_ref, gw_ref, gb_ref):
        cv = c_ref[...]
        cact = (cv * _sigmoid(cv)).astype(BF16)
        for l in range(nl):
            gw_ref[l] = _dot_tn(cact, dc_ref[l].astype(BF16))
            gb_ref[l:l + 1, :] = jnp.sum(df_ref[l], axis=0, keepdims=True)

    return _pc(body, name="ada_bwd",
               out_shape=[jax.ShapeDtypeStruct((nl, d, ncol), F32), jax.ShapeDtypeStruct((nl, d3), F32)]
               )(c_all, dmod_cols, dmod_full)


def _prenorm(x, gain, mod, t_seq, name):
    m, d = x.shape
    tm = _tile(t_seq, 512)
    per = t_seq // tm

    def body(x_ref, g_ref, mod_ref, h_ref):
        xv = x_ref[...]
        rstd = lax.rsqrt(jnp.mean(xv * xv, axis=-1, keepdims=True) + EPS)
        r = xv * rstd * g_ref[...]
        h_ref[...] = (r * (1.0 + mod_ref[0, 1:2, :]) + mod_ref[0, 0:1, :]).astype(BF16)

    return _pc(body, name=name, out_shape=jax.ShapeDtypeStruct((m, d), BF16), grid=(m // tm,),
               in_specs=[pl.BlockSpec((tm, d), lambda i: (i, 0)), pl.BlockSpec((1, d), lambda i: (0, 0)),
                         pl.BlockSpec((1, 3, d), lambda i: (i // per, 0, 0))],
               out_specs=pl.BlockSpec((tm, d), lambda i: (i, 0)), sem=("parallel",))(x, gain, mod)


def _prenorm_bwd(dh, x, gain, mod, dxn, t_seq, name, comm=None):
    m, d = x.shape
    nb = m // t_seq
    tm = _tile(t_seq, 512)
    per = t_seq // tm

    def body(dh_ref, x_ref, g_ref, mod_ref, dxn_ref, dx_ref, dss_ref, dg_ref):
        i = pl.program_id(0)
        xv, dhv, g = x_ref[...], dh_ref[...], g_ref[...]
        rstd = lax.rsqrt(jnp.mean(xv * xv, axis=-1, keepdims=True) + EPS)
        xhat = xv * rstd
        dr = dhv * (1.0 + mod_ref[0, 1:2, :])
        dxhat = dr * g
        dx_ref[...] = dxn_ref[...] + rstd * (dxhat - xhat * jnp.mean(dxhat * xhat, axis=-1, keepdims=True))

        @pl.when(i % per == 0)
        def _():
            dss_ref[...] = jnp.zeros_like(dss_ref)

        @pl.when(i == 0)
        def _():
            dg_ref[...] = jnp.zeros_like(dg_ref)

        dss_ref[0, 0:1, :] += jnp.sum(dhv, axis=0, keepdims=True)
        dss_ref[0, 1:2, :] += jnp.sum(dhv * (xhat * g), axis=0, keepdims=True)
        dg_ref[...] += jnp.sum(dr * xhat, axis=0, keepdims=True)

    row = pl.BlockSpec((tm, d), lambda i: (i, 0))
    return _pc(body, name=name,
               out_shape=[jax.ShapeDtypeStruct((m, d), F32), jax.ShapeDtypeStruct((nb, 2, d), F32),
                          jax.ShapeDtypeStruct((1, d), F32)],
               grid=(m // tm,),
               in_specs=[row, row, pl.BlockSpec((1, d), lambda i: (0, 0)),
                         pl.BlockSpec((1, 3, d), lambda i: (i // per, 0, 0)), row],
               out_specs=[row, pl.BlockSpec((1, 2, d), lambda i: (i // per, 0, 0)),
                          pl.BlockSpec((1, d), lambda i: (0, 0))],
               sem=("arbitrary",), comm=comm)(dh, x, gain, mod, dxn)


def _mm_in(h, w_g, sections, name, comm=None):
    m, k = h.shape
    nc = w_g.shape[2]
    per = NDEV // sections if sections > 1 else NDEV
    tm = _tile(m, 512)

    def body(h_ref, w_ref, o_ref):
        o_ref[...] = _dot(h_ref[...], w_ref[...])

    if sections > 1:
        out_shape = jax.ShapeDtypeStruct((sections, m, per * nc), F32)
        out_spec = pl.BlockSpec((None, tm, nc), lambda j, i: (j // per, i, j % per))
    else:
        out_shape = jax.ShapeDtypeStruct((m, NDEV * nc), F32)
        out_spec = pl.BlockSpec((tm, nc), lambda j, i: (i, j))
    return _pc(body, name=name, out_shape=out_shape, grid=(NDEV, m // tm),
               in_specs=[pl.BlockSpec((tm, k), lambda j, i: (i, 0)),
                         pl.BlockSpec((None, k, nc), lambda j, i: (j, 0, 0))],
               out_specs=out_spec, sem=("parallel", "parallel"), comm=comm)(h, w_g)


def _dspec(sections, tm, nc, m_axis_first):
    per = NDEV // sections if sections > 1 else NDEV
    if sections > 1:
        if m_axis_first:
            return pl.BlockSpec((None, tm, nc), lambda i, j: (j // per, i, j % per))
        return pl.BlockSpec((None, tm, nc), lambda j, i: (j // per, i, j % per))
    if m_axis_first:
        return pl.BlockSpec((tm, nc), lambda i, j: (i, j))
    return pl.BlockSpec((tm, nc), lambda j, i: (i, j))


def _mm_din(dproj, w_g, sections, name, comm=None):
    k, nc = w_g.shape[1], w_g.shape[2]
    m = dproj.shape[-2]
    tm = _tile(m, 512)

    def body(d_ref, w_ref, o_ref):
        j = pl.program_id(1)
        acc = _dot_nt(d_ref[...], w_ref[...])

        @pl.when(j == 0)
        def _():
            o_ref[...] = acc

        @pl.when(j > 0)
        def _():
            o_ref[...] += acc

    return _pc(body, name=name, out_shape=jax.ShapeDtypeStruct((m, k), F32), grid=(m // tm, NDEV),
               in_specs=[_dspec(sections, tm, nc, True), pl.BlockSpec((None, k, nc), lambda i, j: (j, 0, 0))],
               out_specs=pl.BlockSpec((tm, k), lambda i, j: (i, 0)), sem=("parallel", "arbitrary"),
               comm=comm)(dproj, w_g)


def _mm_dw_in(h, dproj, nc, sections, name, comm=None):
    m, k = h.shape
    tk = _tile(m, 512)

    def body(h_ref, d_ref, o_ref):
        kk = pl.program_id(1)
        acc = _dot_tn(h_ref[...], d_ref[...])

        @pl.when(kk == 0)
        def _():
            o_ref[...] = acc

        @pl.when(kk > 0)
        def _():
            o_ref[...] += acc

    return _pc(body, name=name, out_shape=jax.ShapeDtypeStruct((NDEV, k, nc), F32), grid=(NDEV, m // tk),
               in_specs=[pl.BlockSpec((tk, k), lambda j, i: (i, 0)), _dspec(sections, tk, nc, False)],
               out_specs=pl.BlockSpec((None, k, nc), lambda j, i: (j, 0, 0)),
               sem=("parallel", "arbitrary"), comm=comm)(h, dproj)


def _out_proj(ybr, w_out, x, mod, t_seq, name):
    m, di = ybr.shape
    d = w_out.shape[1]
    tm = _tile(t_seq, 512)
    per = t_seq // tm

    def body(y_ref, w_ref, x_ref, mod_ref, yo_ref, xn_ref):
        yo = _dot(y_ref[...], w_ref[...])
        yo_ref[...] = yo
        xn_ref[...] = x_ref[...] + mod_ref[0, 2:3, :] * yo

    row = pl.BlockSpec((tm, d), lambda i: (i, 0))
    return _pc(body, name=name,
               out_shape=[jax.ShapeDtypeStruct((m, d), F32), jax.ShapeDtypeStruct((m, d), F32)],
               grid=(m // tm,),
               in_specs=[pl.BlockSpec((tm, di), lambda i: (i, 0)), pl.BlockSpec((di, d), lambda i: (0, 0)), row,
                         pl.BlockSpec((1, 3, d), lambda i: (i // per, 0, 0))],
               out_specs=[row, row], sem=("parallel",))(ybr, w_out, x, mod)


def _gate_bwd(dxn, yout, mod, t_seq, name):
    m, d = dxn.shape
    nb = m // t_seq
    tm = _tile(t_seq, 512)
    per = t_seq // tm

    def body(dxn_ref, yo_ref, mod_ref, dy_ref, dgate_ref):
        i = pl.program_id(0)
        dv = dxn_ref[...]
        dy_ref[...] = (mod_ref[0, 2:3, :] * dv).astype(BF16)

        @pl.when(i % per == 0)
        def _():
            dgate_ref[...] = jnp.zeros_like(dgate_ref)

        dgate_ref[0] += jnp.sum(dv * yo_ref[...], axis=0, keepdims=True)

    row = pl.BlockSpec((tm, d), lambda i: (i, 0))
    return _pc(body, name=name,
               out_shape=[jax.ShapeDtypeStruct((m, d), BF16), jax.ShapeDtypeStruct((nb, 1, d), F32)],
               grid=(m // tm,),
               in_specs=[row, row, pl.BlockSpec((1, 3, d), lambda i: (i // per, 0, 0))],
               out_specs=[row, pl.BlockSpec((1, 1, d), lambda i: (i // per, 0, 0))],
               sem=("arbitrary",))(dxn, yout, mod)


def _mm_dybr(dy, w_out, name, comm=None):
    m, d = dy.shape
    di = w_out.shape[0]
    tm = _tile(m, 512)

    def body(dy_ref, w_ref, o_ref):
        o_ref[...] = _dot_nt(dy_ref[...], w_ref[...])

    return _pc(body, name=name, out_shape=jax.ShapeDtypeStruct((m, di), F32), grid=(m // tm,),
               in_specs=[pl.BlockSpec((tm, d), lambda i: (i, 0)), pl.BlockSpec((di, d), lambda i: (0, 0))],
               out_specs=pl.BlockSpec((tm, di), lambda i: (i, 0)), sem=("parallel",), comm=comm)(dy, w_out)


def _mm_dw_out(ybr, dy, name):
    m, di = ybr.shape
    d = dy.shape[1]
    tk = _tile(m, 512)
    tn = _tile(di, 1024)

    def body(y_ref, dy_ref, o_ref):
        kk = pl.program_id(1)
        acc = _dot_tn(y_ref[...], dy_ref[...])

        @pl.when(kk == 0)
        def _():
            o_ref[...] = acc

        @pl.when(kk > 0)
        def _():
            o_ref[...] += acc

    return _pc(body, name=name, out_shape=jax.ShapeDtypeStruct((di, d), F32), grid=(di // tn, m // tk),
               in_specs=[pl.BlockSpec((tk, tn), lambda n, k: (k, n)), pl.BlockSpec((tk, d), lambda n, k: (k, 0))],
               out_specs=pl.BlockSpec((tn, d), lambda n, k: (n, 0)), sem=("parallel", "arbitrary"))(ybr, dy)


def _sgu_mask():
    t = lax.broadcasted_iota(jnp.int32, (SG_BLOCK, SG_BLOCK), 0)
    s = lax.broadcasted_iota(jnp.int32, (SG_BLOCK, SG_BLOCK), 1)
    return (s // CHUNK) <= (t // CHUNK)


def _a_mid_fwd(proj, ln_g, ln_b, w_s, bs_t, t_seq, comm=None):
    m, n3 = proj.shape
    di = n3 // 3
    gd = di // SG_GROUPS
    r = _tile(t_seq, 256)
    nblk = r // SG_BLOCK

    def body(p_ref, lg_ref, lb_ref, ws_ref, bs_ref, ybr_ref, s_scr):
        v = _gelu(p_ref[:, di:2 * di])
        mu = jnp.mean(v, axis=-1, keepdims=True)
        vc = v - mu
        rstd = lax.rsqrt(jnp.mean(vc * vc, axis=-1, keepdims=True) + EPS)
        vb = (vc * rstd * lg_ref[...] + lb_ref[...]).astype(BF16)
        mask = _sgu_mask()
        for gi in range(SG_GROUPS):
            ws = jnp.where(mask, ws_ref[gi], 0.0).astype(BF16)
            bcol = bs_ref[:, gi:gi + 1]
            for b in range(nblk):
                rows = slice(b * SG_BLOCK, (b + 1) * SG_BLOCK)
                cols = slice(gi * gd, (gi + 1) * gd)
                s_scr[rows, cols] = _dot(ws, vb[rows, cols]) + bcol
        gg = p_ref[:, 2 * di:]
        ybr_ref[...] = (_gelu(p_ref[:, :di]) * s_scr[...] * (gg * _sigmoid(gg))).astype(BF16)

    vec = pl.BlockSpec((1, di), lambda i: (0, 0))
    return _pc(body, name="a_mid_fwd", out_shape=jax.ShapeDtypeStruct((m, di), BF16), grid=(m // r,),
               in_specs=[pl.BlockSpec((r, n3), lambda i: (i, 0)), vec, vec,
                         pl.BlockSpec((SG_GROUPS, SG_BLOCK, SG_BLOCK), lambda i: (0, 0, 0)),
                         pl.BlockSpec((SG_BLOCK, 128), lambda i: (0, 0))],
               out_specs=pl.BlockSpec((r, di), lambda i: (i, 0)),
               scratch=[pltpu.VMEM((r, di), F32)], sem=("parallel",), comm=comm)(proj, ln_g, ln_b, w_s, bs_t)


def _a_mid_bwd(proj, dybr, ln_g, ln_b, w_s, bs_t, t_seq, comm=None):
    m, n3 = proj.shape
    di = n3 // 3
    gd = di // SG_GROUPS
    r = _tile(t_seq, 256)
    nblk = r // SG_BLOCK

    def body(p_ref, dy_ref, lg_ref, lb_ref, ws_ref, bs_ref,
             dp_ref, dlg_ref, dlb_ref, dws_ref, dbs_ref, s_scr, dvl_scr):
        i = pl.program_id(0)

        @pl.when(i == 0)
        def _():
            dlg_ref[...] = jnp.zeros_like(dlg_ref)
            dlb_ref[...] = jnp.zeros_like(dlb_ref)
            dws_ref[...] = jnp.zeros_like(dws_ref)
            dbs_ref[...] = jnp.zeros_like(dbs_ref)

        v_pre = p_ref[:, di:2 * di]
        v = _gelu(v_pre)
        mu = jnp.mean(v, axis=-1, keepdims=True)
        vc = v - mu
        rstd = lax.rsqrt(jnp.mean(vc * vc, axis=-1, keepdims=True) + EPS)
        vhat = vc * rstd
        lg = lg_ref[...]
        vb = (vhat * lg + lb_ref[...]).astype(BF16)
        u_pre = p_ref[:, :di]
        u = _gelu(u_pre)
        gg = p_ref[:, 2 * di:]
        sg = _sigmoid(gg)
        dyv = dy_ref[...]
        dus = dyv * (gg * sg)
        dsb = (dus * u).astype(BF16)
        ds32 = dus * u
        mask = _sgu_mask()
        lane = lax.broadcasted_iota(jnp.int32, (SG_BLOCK, 128), 1)
        dbs_acc = jnp.zeros((SG_BLOCK, 128), F32)
        for gi in range(SG_GROUPS):
            ws = jnp.where(mask, ws_ref[gi], 0.0).astype(BF16)
            bcol = bs_ref[:, gi:gi + 1]
            cols = slice(gi * gd, (gi + 1) * gd)
            dws_acc = jnp.zeros((SG_BLOCK, SG_BLOCK), F32)
            dbs_col = jnp.zeros((SG_BLOCK, 1), F32)
            for b in range(nblk):
                rows = slice(b * SG_BLOCK, (b + 1) * SG_BLOCK)
                s_scr[rows, cols] = _dot(ws, vb[rows, cols]) + bcol
                dvl_scr[rows, cols] = _dot_tn(ws, dsb[rows, cols])
                dws_acc += _dot_nt(dsb[rows, cols], vb[rows, cols])
                dbs_col += jnp.sum(ds32[rows, cols], axis=-1, keepdims=True)
            dws_ref[gi] += jnp.where(mask, dws_acc, 0.0)
            dbs_acc += jnp.where(lane == gi, dbs_col, 0.0)
        dbs_ref[...] += dbs_acc
        s = s_scr[...]
        dp_ref[:, :di] = (dyv * s * (gg * sg) * _dgelu(u_pre)).astype(BF16)
        dp_ref[:, 2 * di:] = (dyv * u * s * (sg * (1.0 + gg * (1.0 - sg)))).astype(BF16)
        dvl = dvl_scr[...]
        dlg_ref[...] += jnp.sum(dvl * vhat, axis=0, keepdims=True)
        dlb_ref[...] += jnp.sum(dvl, axis=0, keepdims=True)
        dvh = dvl * lg
        dv = rstd * (dvh - jnp.mean(dvh, axis=-1, keepdims=True)
                     - vhat * jnp.mean(dvh * vhat, axis=-1, keepdims=True))
        dp_ref[:, di:2 * di] = (dv * _dgelu(v_pre)).astype(BF16)

    vec = pl.BlockSpec((1, di), lambda i: (0, 0))
    wsb = pl.BlockSpec((SG_GROUPS, SG_BLOCK, SG_BLOCK), lambda i: (0, 0, 0))
    bsb = pl.BlockSpec((SG_BLOCK, 128), lambda i: (0, 0))
    return _pc(body, name="a_mid_bwd",
               out_shape=[jax.ShapeDtypeStruct((m, n3), BF16), jax.ShapeDtypeStruct((1, di), F32),
                          jax.ShapeDtypeStruct((1, di), F32),
                          jax.ShapeDtypeStruct((SG_GROUPS, SG_BLOCK, SG_BLOCK), F32),
                          jax.ShapeDtypeStruct((SG_BLOCK, 128), F32)],
               grid=(m // r,),
               in_specs=[pl.BlockSpec((r, n3), lambda i: (i, 0)), pl.BlockSpec((r, di), lambda i: (i, 0)),
                         vec, vec, wsb, bsb],
               out_specs=[pl.BlockSpec((r, n3), lambda i: (i, 0)), vec, vec, wsb, bsb],
               scratch=[pltpu.VMEM((r, di), F32), pltpu.VMEM((r, di), F32)],
               sem=("arbitrary",), comm=comm)(proj, dybr, ln_g, ln_b, w_s, bs_t)


def _hgrn_dims(t_seq, di):
    tr = _tile(t_seq, 512)
    hc = _tile(di, 512)
    return tr, hc, hc // HEAD_DIM


def _hgrn_gates(f_ref, lb, a_scr, k_scr, tr):
    sig = _sigmoid(f_ref[...])
    fg = lb + (1.0 - lb) * sig
    k_scr[...] = 1.0 - fg
    logf = jnp.log(fg)
    g = min(CUM_ROWS, tr)
    tri = _tri_mask(g, reverse=False)
    for rg in range(tr // g):
        a_scr[rg * g:(rg + 1) * g, :] = _tri_apply(tri, logf[rg * g:(rg + 1) * g, :])
    return sig, fg


def _hgrn_fwd(proj, lbj, gn, nb, t_seq):
    _, m, di = proj.shape
    tr, hc, hpg = _hgrn_dims(t_seq, di)
    nt, nhg, ncl = t_seq // tr, di // hc, tr // CHUNK
    nheads = di // HEAD_DIM

    def body(q_ref, f_ref, i_ref, g_ref, lb_ref, gn_ref, o_ref, ybr_ref, st_ref, st_scr, a_scr, k_scr):
        t = pl.program_id(2)

        @pl.when(t == 0)
        def _():
            st_scr[...] = jnp.zeros_like(st_scr)

        _hgrn_gates(f_ref, lb_ref[0:1, :], a_scr, k_scr, tr)
        gnv = gn_ref[...]
        rr = lax.broadcasted_iota(jnp.int32, (CHUNK, CHUNK), 0)
        cc = lax.broadcasted_iota(jnp.int32, (CHUNK, CHUNK), 1)
        causal = cc <= rr

        def chunk(n, carry):
            rows = pl.ds(pl.multiple_of(n * CHUNK, CHUNK), CHUNK)
            for hd in range(hpg):
                ls = slice(hd * HEAD_DIM, (hd + 1) * HEAD_DIM)
                ah, kh = a_scr[rows, ls], k_scr[rows, ls]
                qp = q_ref[rows, ls]
                qh = qp * _sigmoid(qp)
                vb = i_ref[rows, ls].astype(BF16)
                aref, alast = ah[CHUNK // 2 - 1:CHUNK // 2, :], ah[CHUNK - 1:CHUNK, :]
                q_in = (qh * jnp.exp(ah - aref)).astype(BF16)
                k_in = (kh * jnp.exp(aref - ah)).astype(BF16)
                scores = jnp.where(causal, _dot_nt(q_in, k_in), 0.0).astype(BF16)
                q_out = (qh * jnp.exp(ah)).astype(BF16)
                k_out = (kh * jnp.exp(alast - ah)).astype(BF16)
                st = st_scr[hd]
                st_ref[n, hd] = st
                o = _dot(scores, vb) + _dot_nt(q_out, st.astype(BF16))
                st_scr[hd] = st * jnp.exp(alast) + _dot_tn(vb, k_out)
                o_ref[rows, ls] = o
                rstd = lax.rsqrt(jnp.mean(o * o, axis=-1, keepdims=True) + EPS)
                gg = g_ref[rows, ls]
                ybr_ref[rows, ls] = ((o * rstd * gnv) * (gg * _sigmoid(gg))).astype(BF16)
            return carry

        lax.fori_loop(0, ncl, chunk, 0)

    def sec(s):
        return pl.BlockSpec((None, tr, hc), lambda hg, b, t: (s, b * nt + t, hg))

    blk = pl.BlockSpec((tr, hc), lambda hg, b, t: (b * nt + t, hg))
    return _pc(body, name="hgrn_fwd",
               out_shape=[jax.ShapeDtypeStruct((m, di), F32), jax.ShapeDtypeStruct((m, di), BF16),
                          jax.ShapeDtypeStruct((m // CHUNK, nheads, HEAD_DIM, HEAD_DIM), F32)],
               grid=(nhg, nb, nt),
               in_specs=[sec(0), sec(1), sec(2), sec(3), pl.BlockSpec((2, hc), lambda hg, b, t: (0, hg)),
                         pl.BlockSpec((1, HEAD_DIM), lambda hg, b, t: (0, 0))],
               out_specs=[blk, blk, pl.BlockSpec((ncl, hpg, HEAD_DIM, HEAD_DIM),
                                                 lambda hg, b, t: (b * nt + t, hg, 0, 0))],
               scratch=[pltpu.VMEM((hpg, HEAD_DIM, HEAD_DIM), F32), pltpu.VMEM((tr, hc), F32),
                        pltpu.VMEM((tr, hc), F32)],
               sem=("parallel", "arbitrary", "arbitrary"))(proj, proj, proj, proj, lbj, gn)


def _hgrn_bwd(proj, o_all, dybr, states, lbj, gn, nb, t_seq, comm=None):
    _, m, di = proj.shape
    tr, hc, hpg = _hgrn_dims(t_seq, di)
    nt, nhg, ncl = t_seq // tr, di // hc, tr // CHUNK

    def body(q_ref, f_ref, i_ref, g_ref, o_ref, dy_ref, st_ref, lb_ref, gn_ref,
             dp_ref, dlb_ref, dgn_ref, dst_scr, a_scr, k_scr, da_scr, dk_scr):
        hg, b, t = pl.program_id(0), pl.program_id(1), pl.program_id(2)

        @pl.when(t == 0)
        def _():
            dst_scr[...] = jnp.zeros_like(dst_scr)

        @pl.when((b == 0) & (t == 0))
        def _():
            dlb_ref[...] = jnp.zeros_like(dlb_ref)

        @pl.when((hg == 0) & (b == 0) & (t == 0))
        def _():
            dgn_ref[...] = jnp.zeros_like(dgn_ref)

        lb = lb_ref[0:1, :]
        sig, fg = _hgrn_gates(f_ref, lb, a_scr, k_scr, tr)
        gnv = gn_ref[...]
        rr = lax.broadcasted_iota(jnp.int32, (CHUNK, CHUNK), 0)
        cc = lax.broadcasted_iota(jnp.int32, (CHUNK, CHUNK), 1)
        causal = cc <= rr
        rowi = lax.broadcasted_iota(jnp.int32, (CHUNK, HEAD_DIM), 0)

        def chunk(it, carry):
            n = ncl - 1 - it
            rows = pl.ds(pl.multiple_of(n * CHUNK, CHUNK), CHUNK)
            for hd in range(hpg):
                ls = slice(hd * HEAD_DIM, (hd + 1) * HEAD_DIM)
                ah, kh = a_scr[rows, ls], k_scr[rows, ls]
                qp = q_ref[rows, ls]
                sq = _sigmoid(qp)
                qh = qp * sq
                vb = i_ref[rows, ls].astype(BF16)
                aref, alast = ah[CHUNK // 2 - 1:CHUNK // 2, :], ah[CHUNK - 1:CHUNK, :]
                e1, e2, e3, e4 = jnp.exp(ah - aref), jnp.exp(aref - ah), jnp.exp(ah), jnp.exp(alast - ah)
                dec = jnp.exp(alast)
                q_in, k_in, q_out, k_out = qh * e1, kh * e2, qh * e3, kh * e4
                q_in_b, k_in_b, q_out_b, k_out_b = (z.astype(BF16) for z in (q_in, k_in, q_out, k_out))
                scores = jnp.where(causal, _dot_nt(q_in_b, k_in_b), 0.0).astype(BF16)
                o = o_ref[rows, ls]
                rstd = lax.rsqrt(jnp.mean(o * o, axis=-1, keepdims=True) + EPS)
                ohat = o * rstd
                gg = g_ref[rows, ls]
                sg = _sigmoid(gg)
                dyv = dy_ref[rows, ls]
                d_on = dyv * (gg * sg)
                dp_ref[3, rows, ls] = (dyv * (ohat * gnv) * (sg * (1.0 + gg * (1.0 - sg)))).astype(BF16)
                dgn_ref[...] += jnp.sum(d_on * ohat, axis=0, keepdims=True)
                dohat = d_on * gnv
                do = rstd * (dohat - ohat * jnp.mean(dohat * ohat, axis=-1, keepdims=True))
                do_b = do.astype(BF16)
                st_prev = st_ref[n, hd]
                dst = dst_scr[hd]
                dst_b = dst.astype(BF16)
                dscores = jnp.where(causal, _dot_nt(do_b, vb), 0.0).astype(BF16)
                dv = _dot_tn(scores, do_b) + _dot_nt(k_out_b, dst_b)
                dq_in = _dot(dscores, k_in_b)
                dk_in = _dot_tn(dscores, q_in_b)
                dq_out = _dot(do_b, st_prev.astype(BF16))
                dk_out = _dot(vb, dst_b)
                ddec = jnp.sum(dst * st_prev, axis=0, keepdims=True)
                dst_scr[hd] = dst * dec + _dot_tn(do_b, q_out_b)
                dp_ref[2, rows, ls] = dv.astype(BF16)
                dq = dq_in * e1 + dq_out * e3
                dp_ref[0, rows, ls] = (dq * (sq * (1.0 + qp * (1.0 - sq)))).astype(BF16)
                dk_scr[rows, ls] = dk_in * e2 + dk_out * e4
                t_in = dq_in * q_in - dk_in * k_in
                t_out = dk_out * k_out
                da = t_in + dq_out * q_out - t_out
                da_ref_row = -jnp.sum(t_in, axis=0, keepdims=True)
                da_last_row = jnp.sum(t_out, axis=0, keepdims=True) + ddec * dec
                da = da + jnp.where(rowi == CHUNK // 2 - 1, da_ref_row, 0.0) \
                        + jnp.where(rowi == CHUNK - 1, da_last_row, 0.0)
                da_scr[rows, ls] = da
            return carry

        lax.fori_loop(0, ncl, chunk, 0)
        g = min(CUM_ROWS, tr)
        tri = _tri_mask(g, reverse=True)
        for rg in range(tr // g):
            rs = slice(rg * g, (rg + 1) * g)
            dlogf = _tri_apply(tri, da_scr[rs, :])
            df = dlogf / fg[rs, :] - dk_scr[rs, :]
            sgr = sig[rs, :]
            dp_ref[1, rs, :] = (df * (1.0 - lb) * (sgr * (1.0 - sgr))).astype(BF16)
            dlb_ref[...] += jnp.sum(df * (1.0 - sgr), axis=0, keepdims=True) * lb_ref[1:2, :]

    def sec(s):
        return pl.BlockSpec((None, tr, hc), lambda hg, b, t: (s, b * nt + (nt - 1 - t), hg))

    blk = pl.BlockSpec((tr, hc), lambda hg, b, t: (b * nt + (nt - 1 - t), hg))
    return _pc(body, name="hgrn_bwd",
               out_shape=[jax.ShapeDtypeStruct((4, m, di), BF16), jax.ShapeDtypeStruct((1, di), F32),
                          jax.ShapeDtypeStruct((1, HEAD_DIM), F32)],
               grid=(nhg, nb, nt),
               in_specs=[sec(0), sec(1), sec(2), sec(3), blk, blk,
                         pl.BlockSpec((ncl, hpg, HEAD_DIM, HEAD_DIM),
                                      lambda hg, b, t: (b * nt + (nt - 1 - t), hg, 0, 0)),
                         pl.BlockSpec((2, hc), lambda hg, b, t: (0, hg)),
                         pl.BlockSpec((1, HEAD_DIM), lambda hg, b, t: (0, 0))],
               out_specs=[pl.BlockSpec((4, tr, hc), lambda hg, b, t: (0, b * nt + (nt - 1 - t), hg)),
                          pl.BlockSpec((1, hc), lambda hg, b, t: (0, hg)),
                          pl.BlockSpec((1, HEAD_DIM), lambda hg, b, t: (0, 0))],
               scratch=[pltpu.VMEM((hpg, HEAD_DIM, HEAD_DIM), F32)] + [pltpu.VMEM((tr, hc), F32)] * 4,
               sem=("arbitrary", "arbitrary", "arbitrary"), comm=comm)(
                   proj, proj, proj, proj, o_all, dybr, states, lbj, gn)


def _final_loss(x, gain, target):
    m, d = x.shape
    tm = _tile(m, 512)

    def body(x_ref, g_ref, t_ref, dx_ref, loss_ref, dg_ref):
        i = pl.program_id(0)
        xv, g = x_ref[...], g_ref[...]
        rstd = lax.rsqrt(jnp.mean(xv * xv, axis=-1, keepdims=True) + EPS)
        xhat = xv * rstd
        err = xhat * g - t_ref[...]
        dy = err * (1.0 / d)
        dxhat = dy * g
        dx_ref[...] = rstd * (dxhat - xhat * jnp.mean(dxhat * xhat, axis=-1, keepdims=True))

        @pl.when(i == 0)
        def _():
            loss_ref[...] = jnp.zeros_like(loss_ref)
            dg_ref[...] = jnp.zeros_like(dg_ref)

        loss_ref[...] += 0.5 * jnp.sum(jnp.mean(err * err, axis=-1, keepdims=True), axis=0, keepdims=True)
        dg_ref[...] += jnp.sum(dy * xhat, axis=0, keepdims=True)

    row = pl.BlockSpec((tm, d), lambda i: (i, 0))
    return _pc(body, name="final_loss",
               out_shape=[jax.ShapeDtypeStruct((m, d), F32), jax.ShapeDtypeStruct((1, 1), F32),
                          jax.ShapeDtypeStruct((1, d), F32)],
               grid=(m // tm,),
               in_specs=[row, pl.BlockSpec((1, d), lambda i: (0, 0)), row],
               out_specs=[row, pl.BlockSpec((1, 1), lambda i: (0, 0)), pl.BlockSpec((1, d), lambda i: (0, 0))],
               sem=("arbitrary",))(x, gain, target)


def _adamw(parts, w, m, v, name, comm=None):
    r, c = w.shape
    tr = _tile(r, 256)
    npart = len(parts)
    c1 = 1.0 - ADAM_B1 ** ADAM_STEP
    c2 = 1.0 - ADAM_B2 ** ADAM_STEP

    def body(*refs):
        p_refs = refs[:npart]
        w_ref, m_ref, v_ref, g_ref, d_ref, nm_ref, nv_ref = refs[npart:]
        g = p_refs[0][...].astype(F32)
        for p in p_refs[1:]:
            g = g + p[...].astype(F32)
        nm = ADAM_B1 * m_ref[...] + (1.0 - ADAM_B1) * g
        nv = ADAM_B2 * v_ref[...] + (1.0 - ADAM_B2) * (g * g)
        g_ref[...] = g
        nm_ref[...] = nm
        nv_ref[...] = nv
        d_ref[...] = -ADAM_LR * ((nm / c1) / (jnp.sqrt(nv / c2) + ADAM_EPS) + ADAM_WD * w_ref[...])

    blk = pl.BlockSpec((tr, c), lambda i: (i, 0))
    return _pc(body, name=name, out_shape=[jax.ShapeDtypeStruct((r, c), F32)] * 4, grid=(r // tr,),
               in_specs=[blk] * (npart + 3), out_specs=[blk] * 4, sem=("parallel",), comm=comm)(*parts, w, m, v)


_SMALL = ["norm_gain", "a_ln_gain", "a_ln_bias", "a_w_s", "a_b_s", "b_lower_bounds", "b_gn_gain", "final_gain"]


def _pack(arrs):
    flat = jnp.concatenate([a.reshape(-1) for a in arrs])
    rows = -(-flat.shape[0] // 1024) * 8
    return jnp.pad(flat, (0, rows * 128 - flat.shape[0])).reshape(rows, 128)


def _unpack(buf, like):
    flat = buf.reshape(-1)
    out, off = [], 0
    for a in like:
        out.append(flat[off:off + a.size].reshape(a.shape))
        off += a.size
    return out


def kernel(x, c, norm_gain, w_ada, b_ada, a_w_in, a_ln_gain, a_ln_bias, a_w_s, a_b_s, a_w_out, b_w_in, b_lower_bounds, b_gn_gain, b_w_out, final_gain, loss_target, m_norm_gain, m_w_ada, m_b_ada, m_a_w_in, m_a_ln_gain, m_a_ln_bias, m_a_w_s, m_a_b_s, m_a_w_out, m_b_w_in, m_b_lower_bounds, m_b_gn_gain, m_b_w_out, m_final_gain, v_norm_gain, v_w_ada, v_b_ada, v_a_w_in, v_a_ln_gain, v_a_ln_bias, v_a_w_s, v_a_b_s, v_a_w_out, v_b_w_in, v_b_lower_bounds, v_b_gn_gain, v_b_w_out, v_final_gain):
    w = dict(norm_gain=norm_gain, w_ada=w_ada, b_ada=b_ada, a_w_in=a_w_in, a_ln_gain=a_ln_gain,
             a_ln_bias=a_ln_bias, a_w_s=a_w_s, a_b_s=a_b_s, a_w_out=a_w_out, b_w_in=b_w_in,
             b_lower_bounds=b_lower_bounds, b_gn_gain=b_gn_gain, b_w_out=b_w_out, final_gain=final_gain)
    mo = dict(norm_gain=m_norm_gain, w_ada=m_w_ada, b_ada=m_b_ada, a_w_in=m_a_w_in, a_ln_gain=m_a_ln_gain,
              a_ln_bias=m_a_ln_bias, a_w_s=m_a_w_s, a_b_s=m_a_b_s, a_w_out=m_a_w_out, b_w_in=m_b_w_in,
              b_lower_bounds=m_b_lower_bounds, b_gn_gain=m_b_gn_gain, b_w_out=m_b_w_out, final_gain=m_final_gain)
    vo = dict(norm_gain=v_norm_gain, w_ada=v_w_ada, b_ada=v_b_ada, a_w_in=v_a_w_in, a_ln_gain=v_a_ln_gain,
              a_ln_bias=v_a_ln_bias, a_w_s=v_a_w_s, a_b_s=v_a_b_s, a_w_out=v_a_w_out, b_w_in=v_b_w_in,
              b_lower_bounds=v_b_lower_bounds, b_gn_gain=v_b_gn_gain, b_w_out=v_b_w_out, final_gain=v_final_gain)

    nb, t_seq, d = x.shape
    m = nb * t_seq
    ncol_ada = w_ada.shape[2]
    xi, yi, ci = lax.axis_index("x"), lax.axis_index("y"), lax.axis_index("c")
    me = 4 * xi + 2 * yi + ci

    c_g, wa_in_g, wa_out_g = _all_gather([c, a_w_in[0].astype(BF16), a_w_out[0].astype(BF16)], "gather_c_wa")

    c_all = c_g.reshape(NDEV * nb, d)
    b_cols = lax.dynamic_slice(b_ada, (0, me * ncol_ada), (2, ncol_ada)).reshape(2, 1, ncol_ada)
    mod_part, lbj = _ada_fwd(c_all, w_ada, b_cols, b_lower_bounds)
    mod_all = _all_gather([mod_part], "gather_mod")[0]
    mod_mine = lax.dynamic_slice_in_dim(mod_all, me * nb, nb, axis=2)
    mod_mine = mod_mine.transpose(1, 2, 0, 3).reshape(2, nb, 3, d)
    mod0, mod1 = mod_mine[0], mod_mine[1]

    di = wa_out_g.shape[0] * wa_out_g.shape[1]
    wa_out = wa_out_g.reshape(di, d)

    xf = x.reshape(m, d)
    tgt = loss_target.reshape(m, d)
    ng0, ng1 = norm_gain[0:1], norm_gain[1:2]
    h0 = _prenorm(xf, ng0, mod0, t_seq, "prenorm_a")
    proj_a, wb_half = _mm_in(h0, wa_in_g, 1, "in_proj_a",
                             comm=_gather_first([b_w_in[0].astype(BF16), b_w_out[0].astype(BF16)]))
    bs_t = jnp.pad(a_b_s[0].T, ((0, 0), (0, 128 - SG_GROUPS)))
    ybr_a, (wb_in_g, wb_out_g) = _a_mid_fwd(proj_a, a_ln_gain, a_ln_bias, a_w_s[0], bs_t, t_seq,
                                            comm=_gather_second(wb_half))
    wb_out = wb_out_g.reshape(di, d)
    yout_a, x1 = _out_proj(ybr_a, wa_out, xf, mod0, t_seq, "out_proj_a")
    h1 = _prenorm(x1, ng1, mod1, t_seq, "prenorm_b")
    proj_b = _mm_in(h1, wb_in_g, 4, "in_proj_b")
    o_b, ybr_b, states = _hgrn_fwd(proj_b, lbj, b_gn_gain, nb, t_seq)
    yout_b, x2 = _out_proj(ybr_b, wb_out, x1, mod1, t_seq, "out_proj_b")
    dx2, loss_part, d_final_gain = _final_loss(x2, final_gain.reshape(1, d), tgt)
    loss = lax.psum(loss_part[0, 0], ("x", "y", "c"))

    rows_out = a_w_out.shape[1]
    dy_b, dgate1 = _gate_bwd(dx2, yout_b, mod1, t_seq, "gate_bwd_b")
    dybr_b = _mm_dybr(dy_b, wb_out, "dybr_b")
    rs_wb_out = _ReduceScatter(_mm_dw_out(ybr_b, dy_b, "dw_out_b").reshape(NDEV, rows_out, d), "b_w_out")
    (dproj_b, d_lb, d_gn), got = _hgrn_bwd(proj_b, o_b, dybr_b, states, lbj, b_gn_gain, nb, t_seq,
                                           comm=rs_wb_out.swap_core())
    rs_wb_out.after_core(got[0])
    dh1, got = _mm_din(dproj_b, wb_in_g, 4, "dh_b", comm=rs_wb_out.swap_x())
    rs_wb_out.after_x(got[0])
    (dx1, dss1, dgain1), got = _prenorm_bwd(dh1, x1, ng1, mod1, dx2, t_seq, "prenorm_bwd_b", comm=rs_wb_out.swap_y())
    rs_wb_out.after_y(got[0])
    rs_wb_in = _ReduceScatter(_mm_dw_in(h1, dproj_b, wb_in_g.shape[2], 4, "dw_in_b"), "b_w_in")

    dy_a, dgate0 = _gate_bwd(dx1, yout_a, mod0, t_seq, "gate_bwd_a")
    dybr_a, got = _mm_dybr(dy_a, wa_out, "dybr_a", comm=rs_wb_in.swap_core())
    rs_wb_in.after_core(got[0])
    rs_wa_out = _ReduceScatter(_mm_dw_out(ybr_a, dy_a, "dw_out_a").reshape(NDEV, rows_out, d), "a_w_out")
    (dproj_a, d_lng, d_lnb, d_ws, d_bs_t), got = _a_mid_bwd(
        proj_a, dybr_a, a_ln_gain, a_ln_bias, a_w_s[0], bs_t, t_seq,
        comm=_join(rs_wb_in.swap_x(), rs_wa_out.swap_core()))
    rs_wb_in.after_x(got[0])
    rs_wa_out.after_core(got[1])
    g_wa_in, got = _mm_dw_in(h0, dproj_a, wa_in_g.shape[2], 1, "dw_in_a",
                             comm=_join(rs_wb_in.swap_y(), rs_wa_out.swap_x()))
    rs_wb_in.after_y(got[0])
    rs_wa_out.after_x(got[1])
    rs_wa_in = _ReduceScatter(g_wa_in, "a_w_in")
    dh0, got = _mm_din(dproj_a, wa_in_g, 1, "dh_a", comm=_join(rs_wa_out.swap_y(), rs_wa_in.swap_core()))
    rs_wa_out.after_y(got[0])
    rs_wa_in.after_core(got[1])
    (dx0, dss0, dgain0), got = _prenorm_bwd(dh0, xf, ng0, mod0, dx1, t_seq, "prenorm_bwd_a", comm=rs_wa_in.swap_x())
    rs_wa_in.after_x(got[0])
    grad_x = dx0.reshape(nb, t_seq, d)

    dmod = jnp.stack([jnp.concatenate([dss0, dgate0], axis=1), jnp.concatenate([dss1, dgate1], axis=1)])
    dmod_all = _all_gather([dmod.reshape(2, nb, 3 * d)], "gather_dmod")[0]
    dmod_all = dmod_all.transpose(1, 0, 2, 3).reshape(2, NDEV * nb, 3 * d)
    dmod_cols = lax.dynamic_slice_in_dim(dmod_all, me * ncol_ada, ncol_ada, axis=2)
    g_w_ada, g_b_ada = _ada_bwd(c_all, dmod_cols, dmod_all)

    part = dict(norm_gain=jnp.concatenate([dgain0, dgain1], axis=0), a_ln_gain=d_lng, a_ln_bias=d_lnb,
                a_w_s=d_ws[None], a_b_s=d_bs_t[:, :SG_GROUPS].T[None],
                b_lower_bounds=jnp.concatenate([-d_lb, d_lb], axis=0), b_gn_gain=d_gn, final_gain=d_final_gain[0])
    small_like = [w[k] for k in _SMALL]
    parts_all = _all_gather([_pack([part[k].reshape(w[k].shape) for k in _SMALL])], "gather_small")[0]
    sm = _adamw([parts_all[k] for k in range(NDEV)], _pack(small_like), _pack([mo[k] for k in _SMALL]),
                _pack([vo[k] for k in _SMALL]), "adamw_small")
    sm = [dict(zip(_SMALL, _unpack(buf, small_like))) for buf in sm]

    res = {}
    for k in _SMALL:
        res[k] = tuple(s[k] for s in sm)
    rb = _adamw([g_b_ada], b_ada, mo["b_ada"], vo["b_ada"], "adamw_b_ada")
    res["b_ada"] = tuple(rb)
    sh = w_ada.shape
    ra = _adamw([g_w_ada.reshape(sh[0] * sh[1], sh[2])], w_ada.reshape(sh[0] * sh[1], sh[2]),
                mo["w_ada"].reshape(sh[0] * sh[1], sh[2]), vo["w_ada"].reshape(sh[0] * sh[1], sh[2]), "adamw_w_ada")
    res["w_ada"] = tuple(z.reshape(sh) for z in ra)

    def big(k, rs, comm=None):
        return _adamw(rs.parts, w[k][0], mo[k][0], vo[k][0], "adamw_" + k, comm=comm)

    rbig = {"b_w_out": big("b_w_out", rs_wb_out), "a_w_out": big("a_w_out", rs_wa_out)}
    rbig["b_w_in"], got = big("b_w_in", rs_wb_in, comm=rs_wa_in.swap_y())
    rs_wa_in.after_y(got[0])
    rbig["a_w_in"] = big("a_w_in", rs_wa_in)
    for k, r in rbig.items():
        res[k] = tuple(z[None] for z in r)

    order = ["norm_gain", "w_ada", "b_ada", "a_w_in", "a_ln_gain", "a_ln_bias", "a_w_s", "a_b_s", "a_w_out",
             "b_w_in", "b_lower_bounds", "b_gn_gain", "b_w_out", "final_gain"]
    return (loss, grad_x, *[res[k][0] for k in order], *[res[k][1] for k in order],
            *[res[k][2] for k in order], *[res[k][3] for k in order])
```

```python
import functools
import math

import jax
import jax.numpy as jnp
from jax import lax
from jax.experimental import pallas as pl
from jax.experimental.pallas import tpu as pltpu

F32 = jnp.float32
BF16 = jnp.bfloat16
MESH = pl.DeviceIdType.MESH
NDEV = 8
EPS = 1e-6
CHUNK = 64
SG_BLOCK = 128
SG_GROUPS = 8
HEAD_DIM = 128
CUM_ROWS = 256
ADAM_LR, ADAM_B1, ADAM_B2, ADAM_EPS, ADAM_WD, ADAM_STEP = 0.001, 0.9, 0.999, 1e-08, 0.01, 10
VMEM_LIMIT = 56 * 1024 * 1024
ANY = pl.BlockSpec(memory_space=pl.ANY)


class _Hosted:
    def __init__(self, arrays, out_shapes, nsem, start, finish, aliases=None):
        self.arrays, self.out_shapes, self.nsem = list(arrays), list(out_shapes), nsem
        self.start, self.finish = start, finish
        self.aliases = dict(aliases or {})


def _join(*comms):
    arrays, outs, aliases, offs, nsem = [], [], {}, [], 0
    for cm in comms:
        offs.append((len(arrays), len(outs), nsem))
        for i, o in cm.aliases.items():
            aliases[len(arrays) + i] = len(outs) + o
        arrays += cm.arrays
        outs += cm.out_shapes
        nsem += cm.nsem

    def run(which):
        def f(ins, outs_, ss, rs, base):
            for cm, (ia, io, isem) in zip(comms, offs):
                getattr(cm, which)(ins[ia:ia + len(cm.arrays)], outs_[io:io + len(cm.out_shapes)], ss, rs, base + isem)
        return f

    return _Hosted(arrays, outs, nsem, run("start"), run("finish"), aliases)


def _pc(body, *, name, out_shape, grid=None, in_specs=None, out_specs=None, scratch=(), sem=None,
        grid_spec=None, comm=None):
    cp = dict(vmem_limit_bytes=VMEM_LIMIT)
    if comm is None:
        if sem is not None:
            cp["dimension_semantics"] = sem
        kw = {}
        if grid_spec is not None:
            kw["grid_spec"] = grid_spec
        else:
            if grid is not None:
                kw["grid"] = grid
            if in_specs is not None:
                kw["in_specs"] = in_specs
            if out_specs is not None:
                kw["out_specs"] = out_specs
            kw["scratch_shapes"] = list(scratch)
        return pl.pallas_call(functools.partial(body), name=name, out_shape=out_shape,
                              compiler_params=pltpu.CompilerParams(**cp), **kw)

    single = not isinstance(out_shape, (list, tuple))
    outs_list = [out_shape] if single else list(out_shape)
    ospecs = [out_specs] if single else list(out_specs)
    n_in, n_out, n_ci, n_co, n_scr = len(in_specs), len(outs_list), len(comm.arrays), len(comm.out_shapes), len(scratch)
    cp["dimension_semantics"] = ("arbitrary",) * len(grid)

    def hosted(*refs):
        cin, hin = refs[:n_in], refs[n_in:n_in + n_ci]
        cout = refs[n_in + n_ci:n_in + n_ci + n_out]
        hout = refs[n_in + n_ci + n_out:n_in + n_ci + n_out + n_co]
        scr = refs[n_in + n_ci + n_out + n_co:n_in + n_ci + n_out + n_co + n_scr]
        ssem, rsem = refs[-2], refs[-1]
        first = functools.reduce(lambda p, q: p & q, [pl.program_id(a) == 0 for a in range(len(grid))])
        last = functools.reduce(lambda p, q: p & q, [pl.program_id(a) == grid[a] - 1 for a in range(len(grid))])

        @pl.when(first)
        def _():
            comm.start(hin, hout, ssem, rsem, 0)

        body(*cin, *cout, *scr)

        @pl.when(last)
        def _():
            comm.finish(hin, hout, ssem, rsem, 0)

    call = pl.pallas_call(
        hosted, name=name, grid=grid, in_specs=list(in_specs) + [ANY] * n_ci, out_specs=ospecs + [ANY] * n_co,
        out_shape=outs_list + comm.out_shapes,
        scratch_shapes=list(scratch) + [pltpu.SemaphoreType.DMA((comm.nsem,)), pltpu.SemaphoreType.DMA((comm.nsem,))],
        input_output_aliases={n_in + i: n_out + o for i, o in comm.aliases.items()},
        compiler_params=pltpu.CompilerParams(**cp))

    def run(*args):
        res = call(*args, *comm.arrays)
        comp = res[:n_out]
        return (comp[0] if single else comp), list(res[n_out:])

    return run


def _tile(n, pref):
    return pref if n % pref == 0 else n


def _sigmoid(x):
    return 1.0 / (1.0 + jnp.exp(-x))


def _gelu(x):
    c = math.sqrt(2.0 / math.pi)
    return 0.5 * x * (1.0 + jnp.tanh(c * (x + 0.044715 * (x * x * x))))


def _dgelu(x):
    c = math.sqrt(2.0 / math.pi)
    t = jnp.tanh(c * (x + 0.044715 * (x * x * x)))
    return 0.5 * (1.0 + t) + 0.5 * x * (1.0 - t * t) * (c * (1.0 + 3.0 * 0.044715 * (x * x)))


def _dot(a, b):
    return jnp.dot(a, b, preferred_element_type=F32)


def _dot_nt(a, b):
    return lax.dot_general(a, b, (((1,), (1,)), ((), ())), preferred_element_type=F32)


def _dot_tn(a, b):
    return lax.dot_general(a, b, (((0,), (0,)), ((), ())), preferred_element_type=F32)


def _tri_mask(n, reverse):
    r = lax.broadcasted_iota(jnp.int32, (n, n), 0)
    c = lax.broadcasted_iota(jnp.int32, (n, n), 1)
    same = (r // CHUNK) == (c // CHUNK)
    tri = (c >= r) if reverse else (c <= r)
    return jnp.where(same & tri, 1.0, 0.0).astype(BF16)


def _tri_apply(tri, x):
    hi = x.astype(BF16)
    r1 = x - hi.astype(F32)
    mid = r1.astype(BF16)
    lo = (r1 - mid.astype(F32)).astype(BF16)
    return _dot(tri, hi) + (_dot(tri, mid) + _dot(tri, lo))


def _all_gather(arrs, name):
    n = len(arrs)

    def body(*refs):
        ins, outs = refs[:n], refs[n:2 * n]
        send_sems, recv_sems, local_sems = refs[2 * n:]
        x, y, c = lax.axis_index("x"), lax.axis_index("y"), lax.axis_index("c")
        me, sibling = (x, y, c), (x, y, 1 - c)
        chips = [(1 - x, y), (x, 1 - y), (1 - x, 1 - y)]

        def blk(a, p):
            return outs[a].at[4 * p[0] + 2 * p[1] + p[2]]

        def copy(a, k, block, to, src=None):
            return pltpu.make_async_remote_copy(
                src_ref=blk(a, block) if src is None else src, dst_ref=blk(a, block),
                send_sem=send_sems.at[7 * a + k], recv_sem=recv_sems.at[7 * a + k],
                device_id=to, device_id_type=MESH)

        mine = [pltpu.make_async_copy(ins[a], blk(a, me), local_sems.at[a]) for a in range(n)]
        for m in mine:
            m.start()
        first = []
        for a in range(n):
            first.append(copy(a, 0, me, sibling, src=ins[a]))
            for j, chip in enumerate(chips):
                first.append(copy(a, 1 + j, me, (*chip, c), src=ins[a]))
        for cp in first:
            cp.start()
        passed = []
        for j, chip in enumerate(chips):
            for a in range(n):
                copy(a, 1 + j, (*chip, c), me).wait_recv()
                p = copy(a, 4 + j, (*chip, c), sibling)
                p.start()
                passed.append(p)
        for a in range(n):
            copy(a, 0, sibling, me).wait_recv()
            for j, chip in enumerate(chips):
                copy(a, 4 + j, (*chip, 1 - c), me).wait_recv()
        for cp in first + passed:
            cp.wait_send()
        for m in mine:
            m.wait()

    out_shape = [jax.ShapeDtypeStruct((NDEV,) + a.shape, a.dtype) for a in arrs]
    return _pc(body, name=name, out_shape=out_shape, in_specs=[ANY] * n, out_specs=[ANY] * n,
               scratch=[pltpu.SemaphoreType.DMA((7 * n,)), pltpu.SemaphoreType.DMA((7 * n,)),
                        pltpu.SemaphoreType.DMA((n,))])(*arrs)


def _gather_first(arrs):
    n = len(arrs)

    def parts(ins, outs, ss, rs, base):
        x, y, c = lax.axis_index("x"), lax.axis_index("y"), lax.axis_index("c")
        me, sibling = (x, y, c), (x, y, 1 - c)
        chips = [(1 - x, y), (x, 1 - y), (1 - x, 1 - y)]

        def blk(a, p):
            return outs[a].at[4 * p[0] + 2 * p[1] + p[2]]

        def copy(a, k, block, to):
            return pltpu.make_async_remote_copy(
                src_ref=ins[a], dst_ref=blk(a, block), send_sem=ss.at[base + 4 * a + k],
                recv_sem=rs.at[base + 4 * a + k], device_id=to, device_id_type=MESH)

        local = [pltpu.make_async_copy(ins[a], blk(a, me), ss.at[base + 4 * n + a]) for a in range(n)]
        sends, recvs = [], []
        for a in range(n):
            sends.append(copy(a, 0, me, sibling))
            recvs.append(copy(a, 0, sibling, me))
            for j, chip in enumerate(chips):
                sends.append(copy(a, 1 + j, me, (*chip, c)))
                recvs.append(copy(a, 1 + j, (*chip, c), me))
        return local, sends, recvs

    def start(ins, outs, ss, rs, base):
        local, sends, _ = parts(ins, outs, ss, rs, base)
        for cp in local + sends:
            cp.start()

    def finish(ins, outs, ss, rs, base):
        local, sends, recvs = parts(ins, outs, ss, rs, base)
        for cp in recvs:
            cp.wait_recv()
        for cp in sends:
            cp.wait_send()
        for cp in local:
            cp.wait()

    return _Hosted(arrs, [jax.ShapeDtypeStruct((NDEV,) + a.shape, a.dtype) for a in arrs], 5 * n, start, finish)


def _gather_second(bufs):
    n = len(bufs)

    def parts(ins, outs, ss, rs, base):
        x, y, c = lax.axis_index("x"), lax.axis_index("y"), lax.axis_index("c")
        sibling = (x, y, 1 - c)
        chips = [(1 - x, y), (x, 1 - y), (1 - x, 1 - y)]
        sends, recvs = [], []
        for a in range(n):
            for j, chip in enumerate(chips):
                mine = 4 * chip[0] + 2 * chip[1] + c
                theirs = 4 * chip[0] + 2 * chip[1] + (1 - c)
                sends.append(pltpu.make_async_remote_copy(
                    src_ref=ins[a].at[mine], dst_ref=outs[a].at[mine], send_sem=ss.at[base + 3 * a + j],
                    recv_sem=rs.at[base + 3 * a + j], device_id=sibling, device_id_type=MESH))
                recvs.append(pltpu.make_async_remote_copy(
                    src_ref=ins[a].at[theirs], dst_ref=outs[a].at[theirs], send_sem=ss.at[base + 3 * a + j],
                    recv_sem=rs.at[base + 3 * a + j], device_id=sibling, device_id_type=MESH))
        return sends, recvs

    def start(ins, outs, ss, rs, base):
        for cp in parts(ins, outs, ss, rs, base)[0]:
            cp.start()

    def finish(ins, outs, ss, rs, base):
        sends, recvs = parts(ins, outs, ss, rs, base)
        for cp in recvs:
            cp.wait_recv()
        for cp in sends:
            cp.wait_send()

    return _Hosted(bufs, [jax.ShapeDtypeStruct(b.shape, b.dtype) for b in bufs], 3 * n, start, finish,
                   aliases={a: a for a in range(n)})


def _swap(src, nblk, ids_fn, partner_fn):
    def copies(ins, outs, ss, rs, base):
        x, y, c = lax.axis_index("x"), lax.axis_index("y"), lax.axis_index("c")
        ids = ids_fn(x, y, c)
        partner = partner_fn(x, y, c)
        return [pltpu.make_async_remote_copy(
            src_ref=ins[0].at[ids[k]], dst_ref=outs[0].at[k], send_sem=ss.at[base + k], recv_sem=rs.at[base + k],
            device_id=partner, device_id_type=MESH) for k in range(nblk)]

    def start(ins, outs, ss, rs, base):
        for cp in copies(ins, outs, ss, rs, base):
            cp.start()

    def finish(ins, outs, ss, rs, base):
        for cp in copies(ins, outs, ss, rs, base):
            cp.wait()

    return _Hosted([src], [jax.ShapeDtypeStruct((nblk,) + src.shape[1:], src.dtype)], nblk, start, finish)


def _blocking(comm, name):
    n_i, n_o = len(comm.arrays), len(comm.out_shapes)

    def body(*refs):
        ins, outs = refs[:n_i], refs[n_i:n_i + n_o]
        comm.start(ins, outs, refs[-2], refs[-1], 0)
        comm.finish(ins, outs, refs[-2], refs[-1], 0)

    return pl.pallas_call(
        body, name=name, out_shape=comm.out_shapes, in_specs=[ANY] * n_i, out_specs=[ANY] * n_o,
        scratch_shapes=[pltpu.SemaphoreType.DMA((comm.nsem,)), pltpu.SemaphoreType.DMA((comm.nsem,))],
        input_output_aliases=comm.aliases)(*comm.arrays)


def _add_split(a, b, idx, nh, name):
    _, r, c = a.shape
    tr = _tile(r, 256)

    def body(idx_ref, ak_ref, bk_ref, as_ref, bs_ref, keep_ref, send_ref):
        keep_ref[...] = ak_ref[...] + bk_ref[...].astype(F32)
        send_ref[...] = (as_ref[...] + bs_ref[...].astype(F32)).astype(BF16)

    def sel(off):
        return pl.BlockSpec((None, tr, c), lambda k, i, s: (s[off + k], i, 0))

    out = pl.BlockSpec((None, tr, c), lambda k, i, s: (k, i, 0))
    gs = pltpu.PrefetchScalarGridSpec(num_scalar_prefetch=1, grid=(nh, r // tr),
                                      in_specs=[sel(0), sel(nh), sel(2 * nh), sel(3 * nh)], out_specs=[out, out])
    return _pc(body, name=name, grid_spec=gs, sem=("arbitrary", "arbitrary"),
               out_shape=[jax.ShapeDtypeStruct((nh, r, c), F32), jax.ShapeDtypeStruct((nh, r, c), BF16)])(
                   idx, a, b, a, b)


class _ReduceScatter:
    def __init__(self, g, tag):
        self.g, self.tag = g, tag
        self.x, self.y, self.c = lax.axis_index("x"), lax.axis_index("y"), lax.axis_index("c")

    def swap_core(self):
        return _swap(self.g, 4, lambda x, y, c: [1 - c, 3 - c, 5 - c, 7 - c], lambda x, y, c: (x, y, 1 - c))

    def after_core(self, recv):
        x, c = self.x, self.c
        idx = jnp.stack([4 * x + c, 4 * x + 2 + c, 2 * x, 2 * x + 1,
                         4 * (1 - x) + c, 4 * (1 - x) + 2 + c, 2 * (1 - x), 2 * (1 - x) + 1]).astype(jnp.int32)
        self.keep_x, self.send_x = _add_split(self.g, recv, idx, 2, "rs_add_core_" + self.tag)

    def swap_x(self):
        return _swap(self.send_x, 2, lambda x, y, c: [0, 1], lambda x, y, c: (1 - x, y, c))

    def after_x(self, recv):
        y = self.y
        idx = jnp.stack([y, y, 1 - y, 1 - y]).astype(jnp.int32)
        self.keep_y, self.send_y = _add_split(self.keep_x, recv, idx, 1, "rs_add_x_" + self.tag)

    def swap_y(self):
        return _swap(self.send_y, 1, lambda x, y, c: [0], lambda x, y, c: (x, 1 - y, c))

    def after_y(self, recv):
        self.parts = [self.keep_y[0], recv[0]]


def _ada_fwd(c_all, w_ada, b_cols, b_lb):
    nl, d, ncol = w_ada.shape
    nseq = c_all.shape[0]
    di = b_lb.shape[1]

    def body(c_ref, w_ref, b_ref, lb_ref, mod_ref, lbj_ref):
        cv = c_ref[...]
        cact = (cv * _sigmoid(cv)).astype(BF16)
        for l in range(nl):
            mod_ref[l] = _dot(cact, w_ref[l].astype(BF16)) + b_ref[l]
        b0, b1 = lb_ref[0:1, :], lb_ref[1:2, :]
        mx = jnp.maximum(b0, b1)
        e0, e1 = jnp.exp(b0 - mx), jnp.exp(b1 - mx)
        s = e0 + e1
        p0, p1 = e0 / s, e1 / s
        lbj_ref[0:1, :] = (p0 + p1) - p0
        lbj_ref[1:2, :] = p0 * p1

    return _pc(body, name="ada_fwd",
               out_shape=[jax.ShapeDtypeStruct((nl, nseq, ncol), F32), jax.ShapeDtypeStruct((2, di), F32)]
               )(c_all, w_ada, b_cols, b_lb)


def _ada_bwd(c_all, dmod_cols, dmod_full):
    nl, nseq, ncol = dmod_cols.shape
    d = c_all.shape[1]
    d3 = dmod_full.shape[2]

    def body(c_ref, dc_ref, df_ref, gw_ref, gb_ref):
        cv = c_ref[...]
        cact = (cv * _sigmoid(cv)).astype(BF16)
        for l in range(nl):
            gw_ref[l] = _dot_tn(cact, dc_ref[l].astype(BF16))
            gb_ref[l:l + 1, :] = jnp.sum(df_ref[l], axis=0, keepdims=True)

    return _pc(body, name="ada_bwd",
               out_shape=[jax.ShapeDtypeStruct((nl, d, ncol), F32), jax.ShapeDtypeStruct((nl, d3), F32)]
               )(c_all, dmod_cols, dmod_full)


def _prenorm(x, gain, mod, t_seq, name):
    m, d = x.shape
    tm = _tile(t_seq, 512)
    per = t_seq // tm

    def body(x_ref, g_ref, mod_ref, h_ref):
        xv = x_ref[...]
        rstd = lax.rsqrt(jnp.mean(xv * xv, axis=-1, keepdims=True) + EPS)
        r = xv * rstd * g_ref[...]
        h_ref[...] = (r * (1.0 + mod_ref[0, 1:2, :]) + mod_ref[0, 0:1, :]).astype(BF16)

    return _pc(body, name=name, out_shape=jax.ShapeDtypeStruct((m, d), BF16), grid=(m // tm,),
               in_specs=[pl.BlockSpec((tm, d), lambda i: (i, 0)), pl.BlockSpec((1, d), lambda i: (0, 0)),
                         pl.BlockSpec((1, 3, d), lambda i: (i // per, 0, 0))],
               out_specs=pl.BlockSpec((tm, d), lambda i: (i, 0)), sem=("parallel",))(x, gain, mod)


def _prenorm_bwd(dh, x, gain, mod, dxn, t_seq, name, comm=None):
    m, d = x.shape
    nb = m // t_seq
    tm = _tile(t_seq, 512)
    per = t_seq // tm

    def body(dh_ref, x_ref, g_ref, mod_ref, dxn_ref, dx_ref, dss_ref, dg_ref):
        i = pl.program_id(0)
        xv, dhv, g = x_ref[...], dh_ref[...], g_ref[...]
        rstd = lax.rsqrt(jnp.mean(xv * xv, axis=-1, keepdims=True) + EPS)
        xhat = xv * rstd
        dr = dhv * (1.0 + mod_ref[0, 1:2, :])
        dxhat = dr * g
        dx_ref[...] = dxn_ref[...] + rstd * (dxhat - xhat * jnp.mean(dxhat * xhat, axis=-1, keepdims=True))

        @pl.when(i % per == 0)
        def _():
            dss_ref[...] = jnp.zeros_like(dss_ref)

        @pl.when(i == 0)
        def _():
            dg_ref[...] = jnp.zeros_like(dg_ref)

        dss_ref[0, 0:1, :] += jnp.sum(dhv, axis=0, keepdims=True)
        dss_ref[0, 1:2, :] += jnp.sum(dhv * (xhat * g), axis=0, keepdims=True)
        dg_ref[...] += jnp.sum(dr * xhat, axis=0, keepdims=True)

    row = pl.BlockSpec((tm, d), lambda i: (i, 0))
    return _pc(body, name=name,
               out_shape=[jax.ShapeDtypeStruct((m, d), F32), jax.ShapeDtypeStruct((nb, 2, d), F32),
                          jax.ShapeDtypeStruct((1, d), F32)],
               grid=(m // tm,),
               in_specs=[row, row, pl.BlockSpec((1, d), lambda i: (0, 0)),
                         pl.BlockSpec((1, 3, d), lambda i: (i // per, 0, 0)), row],
               out_specs=[row, pl.BlockSpec((1, 2, d), lambda i: (i // per, 0, 0)),
                          pl.BlockSpec((1, d), lambda i: (0, 0))],
               sem=("arbitrary",), comm=comm)(dh, x, gain, mod, dxn)


def _mm_in(h, w_g, sections, name, comm=None):
    m, k = h.shape
    nc = w_g.shape[2]
    per = NDEV // sections if sections > 1 else NDEV
    tm = _tile(m, 512)

    def body(h_ref, w_ref, o_ref):
        o_ref[...] = _dot(h_ref[...], w_ref[...])

    if sections > 1:
        out_shape = jax.ShapeDtypeStruct((sections, m, per * nc), F32)
        out_spec = pl.BlockSpec((None, tm, nc), lambda j, i: (j // per, i, j % per))
    else:
        out_shape = jax.ShapeDtypeStruct((m, NDEV * nc), F32)
        out_spec = pl.BlockSpec((tm, nc), lambda j, i: (i, j))
    return _pc(body, name=name, out_shape=out_shape, grid=(NDEV, m // tm),
               in_specs=[pl.BlockSpec((tm, k), lambda j, i: (i, 0)),
                         pl.BlockSpec((None, k, nc), lambda j, i: (j, 0, 0))],
               out_specs=out_spec, sem=("parallel", "parallel"), comm=comm)(h, w_g)


def _dspec(sections, tm, nc, m_axis_first):
    per = NDEV // sections if sections > 1 else NDEV
    if sections > 1:
        if m_axis_first:
            return pl.BlockSpec((None, tm, nc), lambda i, j: (j // per, i, j % per))
        return pl.BlockSpec((None, tm, nc), lambda j, i: (j // per, i, j % per))
    if m_axis_first:
        return pl.BlockSpec((tm, nc), lambda i, j: (i, j))
    return pl.BlockSpec((tm, nc), lambda j, i: (i, j))


def _mm_din(dproj, w_g, sections, name, comm=None):
    k, nc = w_g.shape[1], w_g.shape[2]
    m = dproj.shape[-2]
    tm = _tile(m, 512)

    def body(d_ref, w_ref, o_ref):
        j = pl.program_id(1)
        acc = _dot_nt(d_ref[...], w_ref[...])

        @pl.when(j == 0)
        def _():
            o_ref[...] = acc

        @pl.when(j > 0)
        def _():
            o_ref[...] += acc

    return _pc(body, name=name, out_shape=jax.ShapeDtypeStruct((m, k), F32), grid=(m // tm, NDEV),
               in_specs=[_dspec(sections, tm, nc, True), pl.BlockSpec((None, k, nc), lambda i, j: (j, 0, 0))],
               out_specs=pl.BlockSpec((tm, k), lambda i, j: (i, 0)), sem=("parallel", "arbitrary"),
               comm=comm)(dproj, w_g)


def _mm_dw_in(h, dproj, nc, sections, name, comm=None):
    m, k = h.shape
    tk = _tile(m, 512)

    def body(h_ref, d_ref, o_ref):
        kk = pl.program_id(1)
        acc = _dot_tn(h_ref[...], d_ref[...])

        @pl.when(kk == 0)
        def _():
            o_ref[...] = acc

        @pl.when(kk > 0)
        def _():
            o_ref[...] += acc

    return _pc(body, name=name, out_shape=jax.ShapeDtypeStruct((NDEV, k, nc), F32), grid=(NDEV, m // tk),
               in_specs=[pl.BlockSpec((tk, k), lambda j, i: (i, 0)), _dspec(sections, tk, nc, False)],
               out_specs=pl.BlockSpec((None, k, nc), lambda j, i: (j, 0, 0)),
               sem=("parallel", "arbitrary"), comm=comm)(h, dproj)


def _out_proj(ybr, w_out, x, mod, t_seq, name):
    m, di = ybr.shape
    d = w_out.shape[1]
    tm = _tile(t_seq, 512)
    per = t_seq // tm

    def body(y_ref, w_ref, x_ref, mod_ref, yo_ref, xn_ref):
        yo = _dot(y_ref[...], w_ref[...])
        yo_ref[...] = yo
        xn_ref[...] = x_ref[...] + mod_ref[0, 2:3, :] * yo

    row = pl.BlockSpec((tm, d), lambda i: (i, 0))
    return _pc(body, name=name,
               out_shape=[jax.ShapeDtypeStruct((m, d), F32), jax.ShapeDtypeStruct((m, d), F32)],
               grid=(m // tm,),
               in_specs=[pl.BlockSpec((tm, di), lambda i: (i, 0)), pl.BlockSpec((di, d), lambda i: (0, 0)), row,
                         pl.BlockSpec((1, 3, d), lambda i: (i // per, 0, 0))],
               out_specs=[row, row], sem=("parallel",))(ybr, w_out, x, mod)


def _gate_bwd(dxn, yout, mod, t_seq, name):
    m, d = dxn.shape
    nb = m // t_seq
    tm = _tile(t_seq, 512)
    per = t_seq // tm

    def body(dxn_ref, yo_ref, mod_ref, dy_ref, dgate_ref):
        i = pl.program_id(0)
        dv = dxn_ref[...]
        dy_ref[...] = (mod_ref[0, 2:3, :] * dv).astype(BF16)

        @pl.when(i % per == 0)
        def _():
            dgate_ref[...] = jnp.zeros_like(dgate_ref)

        dgate_ref[0] += jnp.sum(dv * yo_ref[...], axis=0, keepdims=True)

    row = pl.BlockSpec((tm, d), lambda i: (i, 0))
    return _pc(body, name=name,
               out_shape=[jax.ShapeDtypeStruct((m, d), BF16), jax.ShapeDtypeStruct((nb, 1, d), F32)],
               grid=(m // tm,),
               in_specs=[row, row, pl.BlockSpec((1, 3, d), lambda i: (i // per, 0, 0))],
               out_specs=[row, pl.BlockSpec((1, 1, d), lambda i: (i // per, 0, 0))],
               sem=("arbitrary",))(dxn, yout, mod)


def _mm_dybr(dy, w_out, name, comm=None):
    m, d = dy.shape
    di = w_out.shape[0]
    tm = _tile(m, 512)

    def body(dy_ref, w_ref, o_ref):
        o_ref[...] = _dot_nt(dy_ref[...], w_ref[...])

    return _pc(body, name=name, out_shape=jax.ShapeDtypeStruct((m, di), F32), grid=(m // tm,),
               in_specs=[pl.BlockSpec((tm, d), lambda i: (i, 0)), pl.BlockSpec((di, d), lambda i: (0, 0))],
               out_specs=pl.BlockSpec((tm, di), lambda i: (i, 0)), sem=("parallel",), comm=comm)(dy, w_out)


def _mm_dw_out(ybr, dy, name):
    m, di = ybr.shape
    d = dy.shape[1]
    tk = _tile(m, 512)
    tn = _tile(di, 1024)

    def body(y_ref, dy_ref, o_ref):
        kk = pl.program_id(1)
        acc = _dot_tn(y_ref[...], dy_ref[...])

        @pl.when(kk == 0)
        def _():
            o_ref[...] = acc

        @pl.when(kk > 0)
        def _():
            o_ref[...] += acc

    return _pc(body, name=name, out_shape=jax.ShapeDtypeStruct((di, d), F32), grid=(di // tn, m // tk),
               in_specs=[pl.BlockSpec((tk, tn), lambda n, k: (k, n)), pl.BlockSpec((tk, d), lambda n, k: (k, 0))],
               out_specs=pl.BlockSpec((tn, d), lambda n, k: (n, 0)), sem=("parallel", "arbitrary"))(ybr, dy)


def _sgu_mask():
    t = lax.broadcasted_iota(jnp.int32, (SG_BLOCK, SG_BLOCK), 0)
    s = lax.broadcasted_iota(jnp.int32, (SG_BLOCK, SG_BLOCK), 1)
    return (s // CHUNK) <= (t // CHUNK)


def _a_mid_fwd(proj, ln_g, ln_b, w_s, bs_t, t_seq, comm=None):
    m, n3 = proj.shape
    di = n3 // 3
    gd = di // SG_GROUPS
    r = _tile(t_seq, 256)
    nblk = r // SG_BLOCK

    def body(p_ref, lg_ref, lb_ref, ws_ref, bs_ref, ybr_ref, s_scr):
        v = _gelu(p_ref[:, di:2 * di])
        mu = jnp.mean(v, axis=-1, keepdims=True)
        vc = v - mu
        rstd = lax.rsqrt(jnp.mean(vc * vc, axis=-1, keepdims=True) + EPS)
        vb = (vc * rstd * lg_ref[...] + lb_ref[...]).astype(BF16)
        mask = _sgu_mask()
        for gi in range(SG_GROUPS):
            ws = jnp.where(mask, ws_ref[gi], 0.0).astype(BF16)
            bcol = bs_ref[:, gi:gi + 1]
            for b in range(nblk):
                rows = slice(b * SG_BLOCK, (b + 1) * SG_BLOCK)
                cols = slice(gi * gd, (gi + 1) * gd)
                s_scr[rows, cols] = _dot(ws, vb[rows, cols]) + bcol
        gg = p_ref[:, 2 * di:]
        ybr_ref[...] = (_gelu(p_ref[:, :di]) * s_scr[...] * (gg * _sigmoid(gg))).astype(BF16)

    vec = pl.BlockSpec((1, di), lambda i: (0, 0))
    return _pc(body, name="a_mid_fwd", out_shape=jax.ShapeDtypeStruct((m, di), BF16), grid=(m // r,),
               in_specs=[pl.BlockSpec((r, n3), lambda i: (i, 0)), vec, vec,
                         pl.BlockSpec((SG_GROUPS, SG_BLOCK, SG_BLOCK), lambda i: (0, 0, 0)),
                         pl.BlockSpec((SG_BLOCK, 128), lambda i: (0, 0))],
               out_specs=pl.BlockSpec((r, di), lambda i: (i, 0)),
               scratch=[pltpu.VMEM((r, di), F32)], sem=("parallel",), comm=comm)(proj, ln_g, ln_b, w_s, bs_t)


def _a_mid_bwd(proj, dybr, ln_g, ln_b, w_s, bs_t, t_seq, comm=None):
    m, n3 = proj.shape
    di = n3 // 3
    gd = di // SG_GROUPS
    r = _tile(t_seq, 256)
    nblk = r // SG_BLOCK

    def body(p_ref, dy_ref, lg_ref, lb_ref, ws_ref, bs_ref,
             dp_ref, dlg_ref, dlb_ref, dws_ref, dbs_ref, s_scr, dvl_scr):
        i = pl.program_id(0)

        @pl.when(i == 0)
        def _():
            dlg_ref[...] = jnp.zeros_like(dlg_ref)
            dlb_ref[...] = jnp.zeros_like(dlb_ref)
            dws_ref[...] = jnp.zeros_like(dws_ref)
            dbs_ref[...] = jnp.zeros_like(dbs_ref)

        v_pre = p_ref[:, di:2 * di]
        v = _gelu(v_pre)
        mu = jnp.mean(v, axis=-1, keepdims=True)
        vc = v - mu
        rstd = lax.rsqrt(jnp.mean(vc * vc, axis=-1, keepdims=True) + EPS)
        vhat = vc * rstd
        lg = lg_ref[...]
        vb = (vhat * lg + lb_ref[...]).astype(BF16)
        u_pre = p_ref[:, :di]
        u = _gelu(u_pre)
        gg = p_ref[:, 2 * di:]
        sg = _sigmoid(gg)
        dyv = dy_ref[...]
        dus = dyv * (gg * sg)
        dsb = (dus * u).astype(BF16)
        ds32 = dus * u
        mask = _sgu_mask()
        lane = lax.broadcasted_iota(jnp.int32, (SG_BLOCK, 128), 1)
        dbs_acc = jnp.zeros((SG_BLOCK, 128), F32)
        for gi in range(SG_GROUPS):
            ws = jnp.where(mask, ws_ref[gi], 0.0).astype(BF16)
            bcol = bs_ref[:, gi:gi + 1]
            cols = slice(gi * gd, (gi + 1) * gd)
            dws_acc = jnp.zeros((SG_BLOCK, SG_BLOCK), F32)
            dbs_col = jnp.zeros((SG_BLOCK, 1), F32)
            for b in range(nblk):
                rows = slice(b * SG_BLOCK, (b + 1) * SG_BLOCK)
                s_scr[rows, cols] = _dot(ws, vb[rows, cols]) + bcol
                dvl_scr[rows, cols] = _dot_tn(ws, dsb[rows, cols])
                dws_acc += _dot_nt(dsb[rows, cols], vb[rows, cols])
                dbs_col += jnp.sum(ds32[rows, cols], axis=-1, keepdims=True)
            dws_ref[gi] += jnp.where(mask, dws_acc, 0.0)
            dbs_acc += jnp.where(lane == gi, dbs_col, 0.0)
        dbs_ref[...] += dbs_acc
        s = s_scr[...]
        dp_ref[:, :di] = (dyv * s * (gg * sg) * _dgelu(u_pre)).astype(BF16)
        dp_ref[:, 2 * di:] = (dyv * u * s * (sg * (1.0 + gg * (1.0 - sg)))).astype(BF16)
        dvl = dvl_scr[...]
        dlg_ref[...] += jnp.sum(dvl * vhat, axis=0, keepdims=True)
        dlb_ref[...] += jnp.sum(dvl, axis=0, keepdims=True)
        dvh = dvl * lg
        dv = rstd * (dvh - jnp.mean(dvh, axis=-1, keepdims=True)
                     - vhat * jnp.mean(dvh * vhat, axis=-1, keepdims=True))
        dp_ref[:, di:2 * di] = (dv * _dgelu(v_pre)).astype(BF16)

    vec = pl.BlockSpec((1, di), lambda i: (0, 0))
    wsb = pl.BlockSpec((SG_GROUPS, SG_BLOCK, SG_BLOCK), lambda i: (0, 0, 0))
    bsb = pl.BlockSpec((SG_BLOCK, 128), lambda i: (0, 0))
    return _pc(body, name="a_mid_bwd",
               out_shape=[jax.ShapeDtypeStruct((m, n3), BF16), jax.ShapeDtypeStruct((1, di), F32),
                          jax.ShapeDtypeStruct((1, di), F32),
                          jax.ShapeDtypeStruct((SG_GROUPS, SG_BLOCK, SG_BLOCK), F32),
                          jax.ShapeDtypeStruct((SG_BLOCK, 128), F32)],
               grid=(m // r,),
               in_specs=[pl.BlockSpec((r, n3), lambda i: (i, 0)), pl.BlockSpec((r, di), lambda i: (i, 0)),
                         vec, vec, wsb, bsb],
               out_specs=[pl.BlockSpec((r, n3), lambda i: (i, 0)), vec, vec, wsb, bsb],
               scratch=[pltpu.VMEM((r, di), F32), pltpu.VMEM((r, di), F32)],
               sem=("arbitrary",), comm=comm)(proj, dybr, ln_g, ln_b, w_s, bs_t)


def _hgrn_dims(t_seq, di):
    tr = _tile(t_seq, 256)
    hc = _tile(di, 1024)
    return tr, hc, hc // HEAD_DIM


def _hgrn_gates(f_ref, lb, a_scr, k_scr, tr):
    sig = _sigmoid(f_ref[...])
    fg = lb + (1.0 - lb) * sig
    k_scr[...] = 1.0 - fg
    logf = jnp.log(fg)
    g = min(CUM_ROWS, tr)
    tri = _tri_mask(g, reverse=False)
    for rg in range(tr // g):
        a_scr[rg * g:(rg + 1) * g, :] = _tri_apply(tri, logf[rg * g:(rg + 1) * g, :])
    return sig, fg


def _hgrn_fwd(proj, lbj, gn, nb, t_seq):
    _, m, di = proj.shape
    tr, hc, hpg = _hgrn_dims(t_seq, di)
    nt, nhg, ncl = t_seq // tr, di // hc, tr // CHUNK
    nheads = di // HEAD_DIM

    def body(q_ref, f_ref, i_ref, g_ref, lb_ref, gn_ref, o_ref, ybr_ref, st_ref, st_scr, a_scr, k_scr):
        t = pl.program_id(2)

        @pl.when(t == 0)
        def _():
            st_scr[...] = jnp.zeros_like(st_scr)

        _hgrn_gates(f_ref, lb_ref[0:1, :], a_scr, k_scr, tr)
        gnv = gn_ref[...]
        rr = lax.broadcasted_iota(jnp.int32, (CHUNK, CHUNK), 0)
        cc = lax.broadcasted_iota(jnp.int32, (CHUNK, CHUNK), 1)
        causal = cc <= rr

        def chunk(n, carry):
            rows = pl.ds(pl.multiple_of(n * CHUNK, CHUNK), CHUNK)
            lanes = [slice(hd * HEAD_DIM, (hd + 1) * HEAD_DIM) for hd in range(hpg)]
            hs = []
            for hd, ls in enumerate(lanes):
                h = {}
                ah, kh = a_scr[rows, ls], k_scr[rows, ls]
                qp = q_ref[rows, ls]
                qh = qp * _sigmoid(qp)
                h["vb"] = i_ref[rows, ls].astype(BF16)
                aref, alast = ah[CHUNK // 2 - 1:CHUNK // 2, :], ah[CHUNK - 1:CHUNK, :]
                h["q_in"] = (qh * jnp.exp(ah - aref)).astype(BF16)
                h["k_in"] = (kh * jnp.exp(aref - ah)).astype(BF16)
                h["q_out"] = (qh * jnp.exp(ah)).astype(BF16)
                h["k_out"] = (kh * jnp.exp(alast - ah)).astype(BF16)
                h["dec"] = jnp.exp(alast)
                st = st_scr[hd]
                st_ref[n, hd] = st
                h["st"] = st
                hs.append(h)
            for h in hs:
                h["scores"] = _dot_nt(h["q_in"], h["k_in"])
                h["o_inter"] = _dot_nt(h["q_out"], h["st"].astype(BF16))
                h["st_mm"] = _dot_tn(h["vb"], h["k_out"])
            for h in hs:
                h["o"] = _dot(jnp.where(causal, h["scores"], 0.0).astype(BF16), h["vb"]) + h["o_inter"]
            for hd, (h, ls) in enumerate(zip(hs, lanes)):
                st_scr[hd] = h["st"] * h["dec"] + h["st_mm"]
                o = h["o"]
                o_ref[rows, ls] = o
                rstd = lax.rsqrt(jnp.mean(o * o, axis=-1, keepdims=True) + EPS)
                gg = g_ref[rows, ls]
                ybr_ref[rows, ls] = ((o * rstd * gnv) * (gg * _sigmoid(gg))).astype(BF16)
            return carry

        lax.fori_loop(0, ncl, chunk, 0)

    def sec(s):
        return pl.BlockSpec((None, tr, hc), lambda hg, b, t: (s, b * nt + t, hg))

    blk = pl.BlockSpec((tr, hc), lambda hg, b, t: (b * nt + t, hg))
    return _pc(body, name="hgrn_fwd",
               out_shape=[jax.ShapeDtypeStruct((m, di), F32), jax.ShapeDtypeStruct((m, di), BF16),
                          jax.ShapeDtypeStruct((m // CHUNK, nheads, HEAD_DIM, HEAD_DIM), F32)],
               grid=(nhg, nb, nt),
               in_specs=[sec(0), sec(1), sec(2), sec(3), pl.BlockSpec((2, hc), lambda hg, b, t: (0, hg)),
                         pl.BlockSpec((1, HEAD_DIM), lambda hg, b, t: (0, 0))],
               out_specs=[blk, blk, pl.BlockSpec((ncl, hpg, HEAD_DIM, HEAD_DIM),
                                                 lambda hg, b, t: (b * nt + t, hg, 0, 0))],
               scratch=[pltpu.VMEM((hpg, HEAD_DIM, HEAD_DIM), F32), pltpu.VMEM((tr, hc), F32),
                        pltpu.VMEM((tr, hc), F32)],
               sem=("parallel", "arbitrary", "arbitrary"))(proj, proj, proj, proj, lbj, gn)


def _hgrn_bwd(proj, o_all, dybr, states, lbj, gn, nb, t_seq, comm=None):
    _, m, di = proj.shape
    tr, hc, hpg = _hgrn_dims(t_seq, di)
    nt, nhg, ncl = t_seq // tr, di // hc, tr // CHUNK

    def body(q_ref, f_ref, i_ref, g_ref, o_ref, dy_ref, st_ref, lb_ref, gn_ref,
             dp_ref, dlb_ref, dgn_ref, dst_scr, a_scr, k_scr, da_scr, dk_scr):
        hg, b, t = pl.program_id(0), pl.program_id(1), pl.program_id(2)

        @pl.when(t == 0)
        def _():
            dst_scr[...] = jnp.zeros_like(dst_scr)

        @pl.when((b == 0) & (t == 0))
        def _():
            dlb_ref[...] = jnp.zeros_like(dlb_ref)

        @pl.when((hg == 0) & (b == 0) & (t == 0))
        def _():
            dgn_ref[...] = jnp.zeros_like(dgn_ref)

        lb = lb_ref[0:1, :]
        sig, fg = _hgrn_gates(f_ref, lb, a_scr, k_scr, tr)
        gnv = gn_ref[...]
        rr = lax.broadcasted_iota(jnp.int32, (CHUNK, CHUNK), 0)
        cc = lax.broadcasted_iota(jnp.int32, (CHUNK, CHUNK), 1)
        causal = cc <= rr
        rowi = lax.broadcasted_iota(jnp.int32, (CHUNK, HEAD_DIM), 0)

        def chunk(it, carry):
            n = ncl - 1 - it
            rows = pl.ds(pl.multiple_of(n * CHUNK, CHUNK), CHUNK)
            lanes = [slice(hd * HEAD_DIM, (hd + 1) * HEAD_DIM) for hd in range(hpg)]
            hs = []
            for hd, ls in enumerate(lanes):
                h = {}
                ah, kh = a_scr[rows, ls], k_scr[rows, ls]
                qp = q_ref[rows, ls]
                sq = _sigmoid(qp)
                qh = qp * sq
                h["dsilu_q"] = sq * (1.0 + qp * (1.0 - sq))
                h["vb"] = i_ref[rows, ls].astype(BF16)
                aref, alast = ah[CHUNK // 2 - 1:CHUNK // 2, :], ah[CHUNK - 1:CHUNK, :]
                h["e1"], h["e2"] = jnp.exp(ah - aref), jnp.exp(aref - ah)
                h["e3"], h["e4"] = jnp.exp(ah), jnp.exp(alast - ah)
                h["dec"] = jnp.exp(alast)
                h["q_in"], h["k_in"], h["q_out"], h["k_out"] = qh * h["e1"], kh * h["e2"], qh * h["e3"], kh * h["e4"]
                for nm in ("q_in", "k_in", "q_out", "k_out"):
                    h[nm + "_b"] = h[nm].astype(BF16)
                o = o_ref[rows, ls]
                rstd = lax.rsqrt(jnp.mean(o * o, axis=-1, keepdims=True) + EPS)
                ohat = o * rstd
                gg = g_ref[rows, ls]
                sg = _sigmoid(gg)
                dyv = dy_ref[rows, ls]
                d_on = dyv * (gg * sg)
                dp_ref[3, rows, ls] = (dyv * (ohat * gnv) * (sg * (1.0 + gg * (1.0 - sg)))).astype(BF16)
                h["dgn"] = jnp.sum(d_on * ohat, axis=0, keepdims=True)
                dohat = d_on * gnv
                do = rstd * (dohat - ohat * jnp.mean(dohat * ohat, axis=-1, keepdims=True))
                h["do_b"] = do.astype(BF16)
                h["st_prev"] = st_ref[n, hd]
                h["dst"] = dst_scr[hd]
                hs.append(h)
            for h in hs:
                dst_b = h["dst"].astype(BF16)
                h["scores"] = _dot_nt(h["q_in_b"], h["k_in_b"])
                h["dscores"] = _dot_nt(h["do_b"], h["vb"])
                h["dv_inter"] = _dot_nt(h["k_out_b"], dst_b)
                h["dq_out"] = _dot(h["do_b"], h["st_prev"].astype(BF16))
                h["dk_out"] = _dot(h["vb"], dst_b)
                h["dst_mm"] = _dot_tn(h["do_b"], h["q_out_b"])
            for h in hs:
                scores = jnp.where(causal, h["scores"], 0.0).astype(BF16)
                dscores = jnp.where(causal, h["dscores"], 0.0).astype(BF16)
                h["dv"] = _dot_tn(scores, h["do_b"]) + h["dv_inter"]
                h["dq_in"] = _dot(dscores, h["k_in_b"])
                h["dk_in"] = _dot_tn(dscores, h["q_in_b"])
            dgn = hs[0]["dgn"]
            for h in hs[1:]:
                dgn = dgn + h["dgn"]
            dgn_ref[...] += dgn
            for hd, (h, ls) in enumerate(zip(hs, lanes)):
                ddec = jnp.sum(h["dst"] * h["st_prev"], axis=0, keepdims=True)
                dst_scr[hd] = h["dst"] * h["dec"] + h["dst_mm"]
                dp_ref[2, rows, ls] = h["dv"].astype(BF16)
                dq = h["dq_in"] * h["e1"] + h["dq_out"] * h["e3"]
                dp_ref[0, rows, ls] = (dq * h["dsilu_q"]).astype(BF16)
                dk_scr[rows, ls] = h["dk_in"] * h["e2"] + h["dk_out"] * h["e4"]
                t_in = h["dq_in"] * h["q_in"] - h["dk_in"] * h["k_in"]
                t_out = h["dk_out"] * h["k_out"]
                da = t_in + h["dq_out"] * h["q_out"] - t_out
                da_ref_row = -jnp.sum(t_in, axis=0, keepdims=True)
                da_last_row = jnp.sum(t_out, axis=0, keepdims=True) + ddec * h["dec"]
                da = da + jnp.where(rowi == CHUNK // 2 - 1, da_ref_row, 0.0) \
                        + jnp.where(rowi == CHUNK - 1, da_last_row, 0.0)
                da_scr[rows, ls] = da
            return carry

        lax.fori_loop(0, ncl, chunk, 0)
        g = min(CUM_ROWS, tr)
        tri = _tri_mask(g, reverse=True)
        for rg in range(tr // g):
            rs = slice(rg * g, (rg + 1) * g)
            dlogf = _tri_apply(tri, da_scr[rs, :])
            df = dlogf / fg[rs, :] - dk_scr[rs, :]
            sgr = sig[rs, :]
            dp_ref[1, rs, :] = (df * (1.0 - lb) * (sgr * (1.0 - sgr))).astype(BF16)
            dlb_ref[...] += jnp.sum(df * (1.0 - sgr), axis=0, keepdims=True) * lb_ref[1:2, :]

    def sec(s):
        return pl.BlockSpec((None, tr, hc), lambda hg, b, t: (s, b * nt + (nt - 1 - t), hg))

    blk = pl.BlockSpec((tr, hc), lambda hg, b, t: (b * nt + (nt - 1 - t), hg))
    return _pc(body, name="hgrn_bwd",
               out_shape=[jax.ShapeDtypeStruct((4, m, di), BF16), jax.ShapeDtypeStruct((1, di), F32),
                          jax.ShapeDtypeStruct((1, HEAD_DIM), F32)],
               grid=(nhg, nb, nt),
               in_specs=[sec(0), sec(1), sec(2), sec(3), blk, blk,
                         pl.BlockSpec((ncl, hpg, HEAD_DIM, HEAD_DIM),
                                      lambda hg, b, t: (b * nt + (nt - 1 - t), hg, 0, 0)),
                         pl.BlockSpec((2, hc), lambda hg, b, t: (0, hg)),
                         pl.BlockSpec((1, HEAD_DIM), lambda hg, b, t: (0, 0))],
               out_specs=[pl.BlockSpec((4, tr, hc), lambda hg, b, t: (0, b * nt + (nt - 1 - t), hg)),
                          pl.BlockSpec((1, hc), lambda hg, b, t: (0, hg)),
                          pl.BlockSpec((1, HEAD_DIM), lambda hg, b, t: (0, 0))],
               scratch=[pltpu.VMEM((hpg, HEAD_DIM, HEAD_DIM), F32)] + [pltpu.VMEM((tr, hc), F32)] * 4,
               sem=("arbitrary", "arbitrary", "arbitrary"), comm=comm)(
                   proj, proj, proj, proj, o_all, dybr, states, lbj, gn)


def _final_loss(x, gain, target):
    m, d = x.shape
    tm = _tile(m, 512)

    def body(x_ref, g_ref, t_ref, dx_ref, loss_ref, dg_ref):
        i = pl.program_id(0)
        xv, g = x_ref[...], g_ref[...]
        rstd = lax.rsqrt(jnp.mean(xv * xv, axis=-1, keepdims=True) + EPS)
        xhat = xv * rstd
        err = xhat * g - t_ref[...]
        dy = err * (1.0 / d)
        dxhat = dy * g
        dx_ref[...] = rstd * (dxhat - xhat * jnp.mean(dxhat * xhat, axis=-1, keepdims=True))

        @pl.when(i == 0)
        def _():
            loss_ref[...] = jnp.zeros_like(loss_ref)
            dg_ref[...] = jnp.zeros_like(dg_ref)

        loss_ref[...] += 0.5 * jnp.sum(jnp.mean(err * err, axis=-1, keepdims=True), axis=0, keepdims=True)
        dg_ref[...] += jnp.sum(dy * xhat, axis=0, keepdims=True)

    row = pl.BlockSpec((tm, d), lambda i: (i, 0))
    return _pc(body, name="final_loss",
               out_shape=[jax.ShapeDtypeStruct((m, d), F32), jax.ShapeDtypeStruct((1, 1), F32),
                          jax.ShapeDtypeStruct((1, d), F32)],
               grid=(m // tm,),
               in_specs=[row, pl.BlockSpec((1, d), lambda i: (0, 0)), row],
               out_specs=[row, pl.BlockSpec((1, 1), lambda i: (0, 0)), pl.BlockSpec((1, d), lambda i: (0, 0))],
               sem=("arbitrary",))(x, gain, target)


def _adamw(parts, w, m, v, name, comm=None):
    r, c = w.shape
    tr = _tile(r, 256)
    npart = len(parts)
    c1 = 1.0 - ADAM_B1 ** ADAM_STEP
    c2 = 1.0 - ADAM_B2 ** ADAM_STEP

    def body(*refs):
        p_refs = refs[:npart]
        w_ref, m_ref, v_ref, g_ref, d_ref, nm_ref, nv_ref = refs[npart:]
        g = p_refs[0][...].astype(F32)
        for p in p_refs[1:]:
            g = g + p[...].astype(F32)
        nm = ADAM_B1 * m_ref[...] + (1.0 - ADAM_B1) * g
        nv = ADAM_B2 * v_ref[...] + (1.0 - ADAM_B2) * (g * g)
        g_ref[...] = g
        nm_ref[...] = nm
        nv_ref[...] = nv
        d_ref[...] = -ADAM_LR * ((nm / c1) / (jnp.sqrt(nv / c2) + ADAM_EPS) + ADAM_WD * w_ref[...])

    blk = pl.BlockSpec((tr, c), lambda i: (i, 0))
    return _pc(body, name=name, out_shape=[jax.ShapeDtypeStruct((r, c), F32)] * 4, grid=(r // tr,),
               in_specs=[blk] * (npart + 3), out_specs=[blk] * 4, sem=("parallel",), comm=comm)(*parts, w, m, v)


_SMALL = ["norm_gain", "a_ln_gain", "a_ln_bias", "a_w_s", "a_b_s", "b_lower_bounds", "b_gn_gain", "final_gain"]


def _pack(arrs):
    flat = jnp.concatenate([a.reshape(-1) for a in arrs])
    rows = -(-flat.shape[0] // 1024) * 8
    return jnp.pad(flat, (0, rows * 128 - flat.shape[0])).reshape(rows, 128)


def _unpack(buf, like):
    flat = buf.reshape(-1)
    out, off = [], 0
    for a in like:
        out.append(flat[off:off + a.size].reshape(a.shape))
        off += a.size
    return out


def kernel(x, c, norm_gain, w_ada, b_ada, a_w_in, a_ln_gain, a_ln_bias, a_w_s, a_b_s, a_w_out, b_w_in, b_lower_bounds, b_gn_gain, b_w_out, final_gain, loss_target, m_norm_gain, m_w_ada, m_b_ada, m_a_w_in, m_a_ln_gain, m_a_ln_bias, m_a_w_s, m_a_b_s, m_a_w_out, m_b_w_in, m_b_lower_bounds, m_b_gn_gain, m_b_w_out, m_final_gain, v_norm_gain, v_w_ada, v_b_ada, v_a_w_in, v_a_ln_gain, v_a_ln_bias, v_a_w_s, v_a_b_s, v_a_w_out, v_b_w_in, v_b_lower_bounds, v_b_gn_gain, v_b_w_out, v_final_gain):
    w = dict(norm_gain=norm_gain, w_ada=w_ada, b_ada=b_ada, a_w_in=a_w_in, a_ln_gain=a_ln_gain,
             a_ln_bias=a_ln_bias, a_w_s=a_w_s, a_b_s=a_b_s, a_w_out=a_w_out, b_w_in=b_w_in,
             b_lower_bounds=b_lower_bounds, b_gn_gain=b_gn_gain, b_w_out=b_w_out, final_gain=final_gain)
    mo = dict(norm_gain=m_norm_gain, w_ada=m_w_ada, b_ada=m_b_ada, a_w_in=m_a_w_in, a_ln_gain=m_a_ln_gain,
              a_ln_bias=m_a_ln_bias, a_w_s=m_a_w_s, a_b_s=m_a_b_s, a_w_out=m_a_w_out, b_w_in=m_b_w_in,
              b_lower_bounds=m_b_lower_bounds, b_gn_gain=m_b_gn_gain, b_w_out=m_b_w_out, final_gain=m_final_gain)
    vo = dict(norm_gain=v_norm_gain, w_ada=v_w_ada, b_ada=v_b_ada, a_w_in=v_a_w_in, a_ln_gain=v_a_ln_gain,
              a_ln_bias=v_a_ln_bias, a_w_s=v_a_w_s, a_b_s=v_a_b_s, a_w_out=v_a_w_out, b_w_in=v_b_w_in,
              b_lower_bounds=v_b_lower_bounds, b_gn_gain=v_b_gn_gain, b_w_out=v_b_w_out, final_gain=v_final_gain)

    nb, t_seq, d = x.shape
    m = nb * t_seq
    ncol_ada = w_ada.shape[2]
    xi, yi, ci = lax.axis_index("x"), lax.axis_index("y"), lax.axis_index("c")
    me = 4 * xi + 2 * yi + ci

    c_g, wa_in_g, wa_out_g = _all_gather([c, a_w_in[0].astype(BF16), a_w_out[0].astype(BF16)], "gather_c_wa")

    c_all = c_g.reshape(NDEV * nb, d)
    b_cols = lax.dynamic_slice(b_ada, (0, me * ncol_ada), (2, ncol_ada)).reshape(2, 1, ncol_ada)
    mod_part, lbj = _ada_fwd(c_all, w_ada, b_cols, b_lower_bounds)
    mod_all = _all_gather([mod_part], "gather_mod")[0]
    mod_mine = lax.dynamic_slice_in_dim(mod_all, me * nb, nb, axis=2)
    mod_mine = mod_mine.transpose(1, 2, 0, 3).reshape(2, nb, 3, d)
    mod0, mod1 = mod_mine[0], mod_mine[1]

    di = wa_out_g.shape[0] * wa_out_g.shape[1]
    wa_out = wa_out_g.reshape(di, d)

    xf = x.reshape(m, d)
    tgt = loss_target.reshape(m, d)
    ng0, ng1 = norm_gain[0:1], norm_gain[1:2]
    h0 = _prenorm(xf, ng0, mod0, t_seq, "prenorm_a")
    proj_a, wb_half = _mm_in(h0, wa_in_g, 1, "in_proj_a",
                             comm=_gather_first([b_w_in[0].astype(BF16), b_w_out[0].astype(BF16)]))
    bs_t = jnp.pad(a_b_s[0].T, ((0, 0), (0, 128 - SG_GROUPS)))
    ybr_a, (wb_in_g, wb_out_g) = _a_mid_fwd(proj_a, a_ln_gain, a_ln_bias, a_w_s[0], bs_t, t_seq,
                                            comm=_gather_second(wb_half))
    wb_out = wb_out_g.reshape(di, d)
    yout_a, x1 = _out_proj(ybr_a, wa_out, xf, mod0, t_seq, "out_proj_a")
    h1 = _prenorm(x1, ng1, mod1, t_seq, "prenorm_b")
    proj_b = _mm_in(h1, wb_in_g, 4, "in_proj_b")
    o_b, ybr_b, states = _hgrn_fwd(proj_b, lbj, b_gn_gain, nb, t_seq)
    yout_b, x2 = _out_proj(ybr_b, wb_out, x1, mod1, t_seq, "out_proj_b")
    dx2, loss_part, d_final_gain = _final_loss(x2, final_gain.reshape(1, d), tgt)
    loss = lax.psum(loss_part[0, 0], ("x", "y", "c"))

    rows_out = a_w_out.shape[1]
    dy_b, dgate1 = _gate_bwd(dx2, yout_b, mod1, t_seq, "gate_bwd_b")
    dybr_b = _mm_dybr(dy_b, wb_out, "dybr_b")
    rs_wb_out = _ReduceScatter(_mm_dw_out(ybr_b, dy_b, "dw_out_b").reshape(NDEV, rows_out, d), "b_w_out")
    (dproj_b, d_lb, d_gn), got = _hgrn_bwd(proj_b, o_b, dybr_b, states, lbj, b_gn_gain, nb, t_seq,
                                           comm=rs_wb_out.swap_core())
    rs_wb_out.after_core(got[0])
    dh1, got = _mm_din(dproj_b, wb_in_g, 4, "dh_b", comm=rs_wb_out.swap_x())
    rs_wb_out.after_x(got[0])
    (dx1, dss1, dgain1), got = _prenorm_bwd(dh1, x1, ng1, mod1, dx2, t_seq, "prenorm_bwd_b", comm=rs_wb_out.swap_y())
    rs_wb_out.after_y(got[0])
    rs_wb_in = _ReduceScatter(_mm_dw_in(h1, dproj_b, wb_in_g.shape[2], 4, "dw_in_b"), "b_w_in")

    dy_a, dgate0 = _gate_bwd(dx1, yout_a, mod0, t_seq, "gate_bwd_a")
    dybr_a, got = _mm_dybr(dy_a, wa_out, "dybr_a", comm=rs_wb_in.swap_core())
    rs_wb_in.after_core(got[0])
    rs_wa_out = _ReduceScatter(_mm_dw_out(ybr_a, dy_a, "dw_out_a").reshape(NDEV, rows_out, d), "a_w_out")
    (dproj_a, d_lng, d_lnb, d_ws, d_bs_t), got = _a_mid_bwd(
        proj_a, dybr_a, a_ln_gain, a_ln_bias, a_w_s[0], bs_t, t_seq,
        comm=_join(rs_wb_in.swap_x(), rs_wa_out.swap_core()))
    rs_wb_in.after_x(got[0])
    rs_wa_out.after_core(got[1])
    g_wa_in, got = _mm_dw_in(h0, dproj_a, wa_in_g.shape[2], 1, "dw_in_a",
                             comm=_join(rs_wb_in.swap_y(), rs_wa_out.swap_x()))
    rs_wb_in.after_y(got[0])
    rs_wa_out.after_x(got[1])
    rs_wa_in = _ReduceScatter(g_wa_in, "a_w_in")
    dh0, got = _mm_din(dproj_a, wa_in_g, 1, "dh_a", comm=_join(rs_wa_out.swap_y(), rs_wa_in.swap_core()))
    rs_wa_out.after_y(got[0])
    rs_wa_in.after_core(got[1])
    (dx0, dss0, dgain0), got = _prenorm_bwd(dh0, xf, ng0, mod0, dx1, t_seq, "prenorm_bwd_a", comm=rs_wa_in.swap_x())
    rs_wa_in.after_x(got[0])
    grad_x = dx0.reshape(nb, t_seq, d)

    dmod = jnp.stack([jnp.concatenate([dss0, dgate0], axis=1), jnp.concatenate([dss1, dgate1], axis=1)])
    dmod_all = _all_gather([dmod.reshape(2, nb, 3 * d)], "gather_dmod")[0]
    dmod_all = dmod_all.transpose(1, 0, 2, 3).reshape(2, NDEV * nb, 3 * d)
    dmod_cols = lax.dynamic_slice_in_dim(dmod_all, me * ncol_ada, ncol_ada, axis=2)
    g_w_ada, g_b_ada = _ada_bwd(c_all, dmod_cols, dmod_all)

    part = dict(norm_gain=jnp.concatenate([dgain0, dgain1], axis=0), a_ln_gain=d_lng, a_ln_bias=d_lnb,
                a_w_s=d_ws[None], a_b_s=d_bs_t[:, :SG_GROUPS].T[None],
                b_lower_bounds=jnp.concatenate([-d_lb, d_lb], axis=0), b_gn_gain=d_gn, final_gain=d_final_gain[0])
    small_like = [w[k] for k in _SMALL]
    parts_all = _all_gather([_pack([part[k].reshape(w[k].shape) for k in _SMALL])], "gather_small")[0]
    sm = _adamw([parts_all[k] for k in range(NDEV)], _pack(small_like), _pack([mo[k] for k in _SMALL]),
                _pack([vo[k] for k in _SMALL]), "adamw_small")
    sm = [dict(zip(_SMALL, _unpack(buf, small_like))) for buf in sm]

    res = {}
    for k in _SMALL:
        res[k] = tuple(s[k] for s in sm)
    rb = _adamw([g_b_ada], b_ada, mo["b_ada"], vo["b_ada"], "adamw_b_ada")
    res["b_ada"] = tuple(rb)
    sh = w_ada.shape
    ra = _adamw([g_w_ada.reshape(sh[0] * sh[1], sh[2])], w_ada.reshape(sh[0] * sh[1], sh[2]),
                mo["w_ada"].reshape(sh[0] * sh[1], sh[2]), vo["w_ada"].reshape(sh[0] * sh[1], sh[2]), "adamw_w_ada")
    res["w_ada"] = tuple(z.reshape(sh) for z in ra)

    def big(k, rs, comm=None):
        return _adamw(rs.parts, w[k][0], mo[k][0], vo[k][0], "adamw_" + k, comm=comm)

    rbig = {"b_w_out": big("b_w_out", rs_wb_out), "a_w_out": big("a_w_out", rs_wa_out)}
    rbig["b_w_in"], got = big("b_w_in", rs_wb_in, comm=rs_wa_in.swap_y())
    rs_wa_in.after_y(got[0])
    rbig["a_w_in"] = big("a_w_in", rs_wa_in)
    for k, r in rbig.items():
        res[k] = tuple(z[None] for z in r)

    order = ["norm_gain", "w_ada", "b_ada", "a_w_in", "a_ln_gain", "a_ln_bias", "a_w_s", "a_b_s", "a_w_out",
             "b_w_in", "b_lower_bounds", "b_gn_gain", "b_w_out", "final_gain"]
    return (loss, grad_x, *[res[k][0] for k in order], *[res[k][1] for k in order],
            *[res[k][2] for k in order], *[res[k][3] for k in order])
```

```python
import functools
import math

import jax
import jax.numpy as jnp
from jax import lax
from jax.experimental import pallas as pl
from jax.experimental.pallas import tpu as pltpu

F32 = jnp.float32
BF16 = jnp.bfloat16
MESH = pl.DeviceIdType.MESH
NDEV = 8
EPS = 1e-6
CHUNK = 64
SG_BLOCK = 128
SG_GROUPS = 8
HEAD_DIM = 128
CUM_ROWS = 256
ADAM_LR, ADAM_B1, ADAM_B2, ADAM_EPS, ADAM_WD, ADAM_STEP = 0.001, 0.9, 0.999, 1e-08, 0.01, 10
VMEM_LIMIT = 56 * 1024 * 1024
ANY = pl.BlockSpec(memory_space=pl.ANY)


class _Hosted:
    def __init__(self, arrays, out_shapes, nsem, start, finish, aliases=None):
        self.arrays, self.out_shapes, self.nsem = list(arrays), list(out_shapes), nsem
        self.start, self.finish = start, finish
        self.aliases = dict(aliases or {})


def _join(*comms):
    arrays, outs, aliases, offs, nsem = [], [], {}, [], 0
    for cm in comms:
        offs.append((len(arrays), len(outs), nsem))
        for i, o in cm.aliases.items():
            aliases[len(arrays) + i] = len(outs) + o
        arrays += cm.arrays
        outs += cm.out_shapes
        nsem += cm.nsem

    def run(which):
        def f(ins, outs_, ss, rs, base):
            for cm, (ia, io, isem) in zip(comms, offs):
                getattr(cm, which)(ins[ia:ia + len(cm.arrays)], outs_[io:io + len(cm.out_shapes)], ss, rs, base + isem)
        return f

    return _Hosted(arrays, outs, nsem, run("start"), run("finish"), aliases)


def _pc(body, *, name, out_shape, grid=None, in_specs=None, out_specs=None, scratch=(), sem=None,
        grid_spec=None, comm=None, aliases=None):
    cp = dict(vmem_limit_bytes=VMEM_LIMIT)
    aliases = dict(aliases or {})
    if comm is None:
        if sem is not None:
            cp["dimension_semantics"] = sem
        kw = {"input_output_aliases": aliases}
        if grid_spec is not None:
            kw["grid_spec"] = grid_spec
        else:
            if grid is not None:
                kw["grid"] = grid
            if in_specs is not None:
                kw["in_specs"] = in_specs
            if out_specs is not None:
                kw["out_specs"] = out_specs
            kw["scratch_shapes"] = list(scratch)
        return pl.pallas_call(functools.partial(body), name=name, out_shape=out_shape,
                              compiler_params=pltpu.CompilerParams(**cp), **kw)

    single = not isinstance(out_shape, (list, tuple))
    outs_list = [out_shape] if single else list(out_shape)
    ospecs = [out_specs] if single else list(out_specs)
    n_in, n_out, n_ci, n_co, n_scr = len(in_specs), len(outs_list), len(comm.arrays), len(comm.out_shapes), len(scratch)
    cp["dimension_semantics"] = ("arbitrary",) * len(grid)

    def hosted(*refs):
        cin, hin = refs[:n_in], refs[n_in:n_in + n_ci]
        cout = refs[n_in + n_ci:n_in + n_ci + n_out]
        hout = refs[n_in + n_ci + n_out:n_in + n_ci + n_out + n_co]
        scr = refs[n_in + n_ci + n_out + n_co:n_in + n_ci + n_out + n_co + n_scr]
        ssem, rsem = refs[-2], refs[-1]
        first = functools.reduce(lambda p, q: p & q, [pl.program_id(a) == 0 for a in range(len(grid))])
        last = functools.reduce(lambda p, q: p & q, [pl.program_id(a) == grid[a] - 1 for a in range(len(grid))])

        @pl.when(first)
        def _():
            comm.start(hin, hout, ssem, rsem, 0)

        body(*cin, *cout, *scr)

        @pl.when(last)
        def _():
            comm.finish(hin, hout, ssem, rsem, 0)

    call = pl.pallas_call(
        hosted, name=name, grid=grid, in_specs=list(in_specs) + [ANY] * n_ci, out_specs=ospecs + [ANY] * n_co,
        out_shape=outs_list + comm.out_shapes,
        scratch_shapes=list(scratch) + [pltpu.SemaphoreType.DMA((comm.nsem,)), pltpu.SemaphoreType.DMA((comm.nsem,))],
        input_output_aliases={**aliases, **{n_in + i: n_out + o for i, o in comm.aliases.items()}},
        compiler_params=pltpu.CompilerParams(**cp))

    def run(*args):
        res = call(*args, *comm.arrays)
        comp = res[:n_out]
        return (comp[0] if single else comp), list(res[n_out:])

    return run


def _tile(n, pref):
    return pref if n % pref == 0 else n


def _sigmoid(x):
    return 1.0 / (1.0 + jnp.exp(-x))


def _gelu(x):
    c = math.sqrt(2.0 / math.pi)
    return 0.5 * x * (1.0 + jnp.tanh(c * (x + 0.044715 * (x * x * x))))


def _dgelu(x):
    c = math.sqrt(2.0 / math.pi)
    t = jnp.tanh(c * (x + 0.044715 * (x * x * x)))
    return 0.5 * (1.0 + t) + 0.5 * x * (1.0 - t * t) * (c * (1.0 + 3.0 * 0.044715 * (x * x)))


def _dot(a, b):
    return jnp.dot(a, b, preferred_element_type=F32)


def _dot_nt(a, b):
    return lax.dot_general(a, b, (((1,), (1,)), ((), ())), preferred_element_type=F32)


def _dot_tn(a, b):
    return lax.dot_general(a, b, (((0,), (0,)), ((), ())), preferred_element_type=F32)


def _tri_mask(n, reverse):
    r = lax.broadcasted_iota(jnp.int32, (n, n), 0)
    c = lax.broadcasted_iota(jnp.int32, (n, n), 1)
    same = (r // CHUNK) == (c // CHUNK)
    tri = (c >= r) if reverse else (c <= r)
    return jnp.where(same & tri, 1.0, 0.0).astype(BF16)


def _tri_apply(tri, x):
    hi = x.astype(BF16)
    r1 = x - hi.astype(F32)
    mid = r1.astype(BF16)
    lo = (r1 - mid.astype(F32)).astype(BF16)
    return _dot(tri, hi) + (_dot(tri, mid) + _dot(tri, lo))


def _all_gather(arrs, name):
    n = len(arrs)

    def body(*refs):
        ins, outs = refs[:n], refs[n:2 * n]
        send_sems, recv_sems, local_sems = refs[2 * n:]
        x, y, c = lax.axis_index("x"), lax.axis_index("y"), lax.axis_index("c")
        me, sibling = (x, y, c), (x, y, 1 - c)
        chips = [(1 - x, y), (x, 1 - y), (1 - x, 1 - y)]

        def blk(a, p):
            return outs[a].at[4 * p[0] + 2 * p[1] + p[2]]

        def copy(a, k, block, to, src=None):
            return pltpu.make_async_remote_copy(
                src_ref=blk(a, block) if src is None else src, dst_ref=blk(a, block),
                send_sem=send_sems.at[7 * a + k], recv_sem=recv_sems.at[7 * a + k],
                device_id=to, device_id_type=MESH)

        mine = [pltpu.make_async_copy(ins[a], blk(a, me), local_sems.at[a]) for a in range(n)]
        for m in mine:
            m.start()
        first = []
        for a in range(n):
            first.append(copy(a, 0, me, sibling, src=ins[a]))
            for j, chip in enumerate(chips):
                first.append(copy(a, 1 + j, me, (*chip, c), src=ins[a]))
        for cp in first:
            cp.start()
        passed = []
        for j, chip in enumerate(chips):
            for a in range(n):
                copy(a, 1 + j, (*chip, c), me).wait_recv()
                p = copy(a, 4 + j, (*chip, c), sibling)
                p.start()
                passed.append(p)
        for a in range(n):
            copy(a, 0, sibling, me).wait_recv()
            for j, chip in enumerate(chips):
                copy(a, 4 + j, (*chip, 1 - c), me).wait_recv()
        for cp in first + passed:
            cp.wait_send()
        for m in mine:
            m.wait()

    out_shape = [jax.ShapeDtypeStruct((NDEV,) + a.shape, a.dtype) for a in arrs]
    return _pc(body, name=name, out_shape=out_shape, in_specs=[ANY] * n, out_specs=[ANY] * n,
               scratch=[pltpu.SemaphoreType.DMA((7 * n,)), pltpu.SemaphoreType.DMA((7 * n,)),
                        pltpu.SemaphoreType.DMA((n,))])(*arrs)


def _gather_first(arrs):
    n = len(arrs)

    def parts(ins, outs, ss, rs, base):
        x, y, c = lax.axis_index("x"), lax.axis_index("y"), lax.axis_index("c")
        me, sibling = (x, y, c), (x, y, 1 - c)
        chips = [(1 - x, y), (x, 1 - y), (1 - x, 1 - y)]

        def blk(a, p):
            return outs[a].at[4 * p[0] + 2 * p[1] + p[2]]

        def copy(a, k, block, to):
            return pltpu.make_async_remote_copy(
                src_ref=ins[a], dst_ref=blk(a, block), send_sem=ss.at[base + 4 * a + k],
                recv_sem=rs.at[base + 4 * a + k], device_id=to, device_id_type=MESH)

        local = [pltpu.make_async_copy(ins[a], blk(a, me), ss.at[base + 4 * n + a]) for a in range(n)]
        sends, recvs = [], []
        for a in range(n):
            sends.append(copy(a, 0, me, sibling))
            recvs.append(copy(a, 0, sibling, me))
            for j, chip in enumerate(chips):
                sends.append(copy(a, 1 + j, me, (*chip, c)))
                recvs.append(copy(a, 1 + j, (*chip, c), me))
        return local, sends, recvs

    def start(ins, outs, ss, rs, base):
        local, sends, _ = parts(ins, outs, ss, rs, base)
        for cp in local + sends:
            cp.start()

    def finish(ins, outs, ss, rs, base):
        local, sends, recvs = parts(ins, outs, ss, rs, base)
        for cp in recvs:
            cp.wait_recv()
        for cp in sends:
            cp.wait_send()
        for cp in local:
            cp.wait()

    return _Hosted(arrs, [jax.ShapeDtypeStruct((NDEV,) + a.shape, a.dtype) for a in arrs], 5 * n, start, finish)


def _gather_second(bufs):
    n = len(bufs)

    def parts(ins, outs, ss, rs, base):
        x, y, c = lax.axis_index("x"), lax.axis_index("y"), lax.axis_index("c")
        sibling = (x, y, 1 - c)
        chips = [(1 - x, y), (x, 1 - y), (1 - x, 1 - y)]
        sends, recvs = [], []
        for a in range(n):
            for j, chip in enumerate(chips):
                mine = 4 * chip[0] + 2 * chip[1] + c
                theirs = 4 * chip[0] + 2 * chip[1] + (1 - c)
                sends.append(pltpu.make_async_remote_copy(
                    src_ref=ins[a].at[mine], dst_ref=outs[a].at[mine], send_sem=ss.at[base + 3 * a + j],
                    recv_sem=rs.at[base + 3 * a + j], device_id=sibling, device_id_type=MESH))
                recvs.append(pltpu.make_async_remote_copy(
                    src_ref=ins[a].at[theirs], dst_ref=outs[a].at[theirs], send_sem=ss.at[base + 3 * a + j],
                    recv_sem=rs.at[base + 3 * a + j], device_id=sibling, device_id_type=MESH))
        return sends, recvs

    def start(ins, outs, ss, rs, base):
        for cp in parts(ins, outs, ss, rs, base)[0]:
            cp.start()

    def finish(ins, outs, ss, rs, base):
        sends, recvs = parts(ins, outs, ss, rs, base)
        for cp in recvs:
            cp.wait_recv()
        for cp in sends:
            cp.wait_send()

    return _Hosted(bufs, [jax.ShapeDtypeStruct(b.shape, b.dtype) for b in bufs], 3 * n, start, finish,
                   aliases={a: a for a in range(n)})


def _swap(src, nblk, ids_fn, partner_fn):
    def copies(ins, outs, ss, rs, base):
        x, y, c = lax.axis_index("x"), lax.axis_index("y"), lax.axis_index("c")
        ids = ids_fn(x, y, c)
        partner = partner_fn(x, y, c)
        return [pltpu.make_async_remote_copy(
            src_ref=ins[0].at[ids[k]], dst_ref=outs[0].at[k], send_sem=ss.at[base + k], recv_sem=rs.at[base + k],
            device_id=partner, device_id_type=MESH) for k in range(nblk)]

    def start(ins, outs, ss, rs, base):
        for cp in copies(ins, outs, ss, rs, base):
            cp.start()

    def finish(ins, outs, ss, rs, base):
        for cp in copies(ins, outs, ss, rs, base):
            cp.wait()

    return _Hosted([src], [jax.ShapeDtypeStruct((nblk,) + src.shape[1:], src.dtype)], nblk, start, finish)


def _blocking(comm, name):
    n_i, n_o = len(comm.arrays), len(comm.out_shapes)

    def body(*refs):
        ins, outs = refs[:n_i], refs[n_i:n_i + n_o]
        comm.start(ins, outs, refs[-2], refs[-1], 0)
        comm.finish(ins, outs, refs[-2], refs[-1], 0)

    return pl.pallas_call(
        body, name=name, out_shape=comm.out_shapes, in_specs=[ANY] * n_i, out_specs=[ANY] * n_o,
        scratch_shapes=[pltpu.SemaphoreType.DMA((comm.nsem,)), pltpu.SemaphoreType.DMA((comm.nsem,))],
        input_output_aliases=comm.aliases)(*comm.arrays)


def _swap_chips(send):
    def copies(ins, outs, ss, rs, base):
        x, y, c = lax.axis_index("x"), lax.axis_index("y"), lax.axis_index("c")
        chips = [(1 - x, y), (x, 1 - y), (1 - x, 1 - y)]
        return [pltpu.make_async_remote_copy(
            src_ref=ins[0].at[j], dst_ref=outs[0].at[j], send_sem=ss.at[base + j], recv_sem=rs.at[base + j],
            device_id=(*chip, c), device_id_type=MESH) for j, chip in enumerate(chips)]

    def start(ins, outs, ss, rs, base):
        for cp in copies(ins, outs, ss, rs, base):
            cp.start()

    def finish(ins, outs, ss, rs, base):
        for cp in copies(ins, outs, ss, rs, base):
            cp.wait()

    return _Hosted([send], [jax.ShapeDtypeStruct(send.shape, send.dtype)], 3, start, finish)


def _add_send(a, b, idx, ns, name):
    _, r, c = a.shape
    tr = _tile(r, 256)

    def body(idx_ref, a_ref, b_ref, send_ref):
        send_ref[...] = (a_ref[...] + b_ref[...]).astype(BF16)

    def sel(off):
        return pl.BlockSpec((None, tr, c), lambda k, i, s: (s[off + k], i, 0))

    gs = pltpu.PrefetchScalarGridSpec(num_scalar_prefetch=1, grid=(ns, r // tr), in_specs=[sel(0), sel(ns)],
                                      out_specs=pl.BlockSpec((None, tr, c), lambda k, i, s: (k, i, 0)))
    return _pc(body, name=name, grid_spec=gs, sem=("arbitrary", "arbitrary"),
               out_shape=jax.ShapeDtypeStruct((ns, r, c), BF16))(idx, a, b)


class _ReduceScatter:
    def __init__(self, g, tag):
        self.g, self.tag = g, tag

    def swap_core(self):
        return _swap(self.g, 4, lambda x, y, c: [1 - c, 3 - c, 5 - c, 7 - c], lambda x, y, c: (x, y, 1 - c))

    def after_core(self, recv):
        x, y, c = lax.axis_index("x"), lax.axis_index("y"), lax.axis_index("c")
        chips = [(1 - x, y), (x, 1 - y), (1 - x, 1 - y)]
        idx = jnp.stack([4 * p + 2 * q + c for p, q in chips] + [2 * p + q for p, q in chips]).astype(jnp.int32)
        self.send = _add_send(self.g, recv, idx, 3, "rs_add_" + self.tag)
        self.mine = [lax.dynamic_index_in_dim(self.g, 4 * x + 2 * y + c, 0, keepdims=False),
                     lax.dynamic_index_in_dim(recv, 2 * x + y, 0, keepdims=False)]

    def swap_chips(self):
        return _swap_chips(self.send)

    def after_chips(self, recv):
        self.parts = self.mine + [recv[0], recv[1], recv[2]]


def _ada_fwd(c_all, w_ada, b_cols, b_lb):
    nl, d, ncol = w_ada.shape
    nseq = c_all.shape[0]
    di = b_lb.shape[1]

    def body(c_ref, w_ref, b_ref, lb_ref, mod_ref, lbj_ref):
        cv = c_ref[...]
        cact = (cv * _sigmoid(cv)).astype(BF16)
        for l in range(nl):
            mod_ref[l] = _dot(cact, w_ref[l].astype(BF16)) + b_ref[l]
        b0, b1 = lb_ref[0:1, :], lb_ref[1:2, :]
        mx = jnp.maximum(b0, b1)
        e0, e1 = jnp.exp(b0 - mx), jnp.exp(b1 - mx)
        s = e0 + e1
        p0, p1 = e0 / s, e1 / s
        lbj_ref[0:1, :] = (p0 + p1) - p0
        lbj_ref[1:2, :] = p0 * p1

    return _pc(body, name="ada_fwd",
               out_shape=[jax.ShapeDtypeStruct((nl, nseq, ncol), F32), jax.ShapeDtypeStruct((2, di), F32)]
               )(c_all, w_ada, b_cols, b_lb)


def _ada_bwd(c_all, dmod_cols, dmod_full):
    nl, nseq, ncol = dmod_cols.shape
    d = c_all.shape[1]
    d3 = dmod_full.shape[2]

    def body(c_ref, dc_ref, df_ref, gw_ref, gb_ref):
        cv = c_ref[...]
        cact = (cv * _sigmoid(cv)).astype(BF16)
        for l in range(nl):
            gw_ref[l] = _dot_tn(cact, dc_ref[l].astype(BF16))
            gb_ref[l:l + 1, :] = jnp.sum(df_ref[l], axis=0, keepdims=True)

    return _pc(body, name="ada_bwd",
               out_shape=[jax.ShapeDtypeStruct((nl, d, ncol), F32), jax.ShapeDtypeStruct((nl, d3), F32)]
               )(c_all, dmod_cols, dmod_full)


def _prenorm(x, gain, mod, t_seq, name):
    m, d = x.shape
    tm = _tile(t_seq, 512)
    per = t_seq // tm

    def body(x_ref, g_ref, mod_ref, h_ref):
        xv = x_ref[...]
        rstd = lax.rsqrt(jnp.mean(xv * xv, axis=-1, keepdims=True) + EPS)
        r = xv * rstd * g_ref[...]
        h_ref[...] = (r * (1.0 + mod_ref[0, 1:2, :]) + mod_ref[0, 0:1, :]).astype(BF16)

    return _pc(body, name=name, out_shape=jax.ShapeDtypeStruct((m, d), BF16), grid=(m // tm,),
               in_specs=[pl.BlockSpec((tm, d), lambda i: (i, 0)), pl.BlockSpec((1, d), lambda i: (0, 0)),
                         pl.BlockSpec((1, 3, d), lambda i: (i // per, 0, 0))],
               out_specs=pl.BlockSpec((tm, d), lambda i: (i, 0)), sem=("parallel",))(x, gain, mod)


def _prenorm_bwd(dh, x, gain, mod, dxn, t_seq, name, comm=None):
    m, d = x.shape
    nb = m // t_seq
    tm = _tile(t_seq, 512)
    per = t_seq // tm

    def body(dh_ref, x_ref, g_ref, mod_ref, dxn_ref, dx_ref, dss_ref, dg_ref):
        i = pl.program_id(0)
        xv, dhv, g = x_ref[...], dh_ref[...], g_ref[...]
        rstd = lax.rsqrt(jnp.mean(xv * xv, axis=-1, keepdims=True) + EPS)
        xhat = xv * rstd
        dr = dhv * (1.0 + mod_ref[0, 1:2, :])
        dxhat = dr * g
        dx_ref[...] = dxn_ref[...] + rstd * (dxhat - xhat * jnp.mean(dxhat * xhat, axis=-1, keepdims=True))

        @pl.when(i % per == 0)
        def _():
            dss_ref[...] = jnp.zeros_like(dss_ref)

        @pl.when(i == 0)
        def _():
            dg_ref[...] = jnp.zeros_like(dg_ref)

        dss_ref[0, 0:1, :] += jnp.sum(dhv, axis=0, keepdims=True)
        dss_ref[0, 1:2, :] += jnp.sum(dhv * (xhat * g), axis=0, keepdims=True)
        dg_ref[...] += jnp.sum(dr * xhat, axis=0, keepdims=True)

    row = pl.BlockSpec((tm, d), lambda i: (i, 0))
    return _pc(body, name=name,
               out_shape=[jax.ShapeDtypeStruct((m, d), F32), jax.ShapeDtypeStruct((nb, 2, d), F32),
                          jax.ShapeDtypeStruct((1, d), F32)],
               grid=(m // tm,),
               in_specs=[row, row, pl.BlockSpec((1, d), lambda i: (0, 0)),
                         pl.BlockSpec((1, 3, d), lambda i: (i // per, 0, 0)), row],
               out_specs=[row, pl.BlockSpec((1, 2, d), lambda i: (i // per, 0, 0)),
                          pl.BlockSpec((1, d), lambda i: (0, 0))],
               sem=("arbitrary",), comm=comm)(dh, x, gain, mod, dxn)


def _mm_in(h, w_g, sections, name, comm=None):
    m, k = h.shape
    nc = w_g.shape[2]
    per = NDEV // sections if sections > 1 else NDEV
    tm = _tile(m, 512)

    def body(h_ref, w_ref, o_ref):
        o_ref[...] = _dot(h_ref[...], w_ref[...])

    if sections > 1:
        out_shape = jax.ShapeDtypeStruct((sections, m, per * nc), F32)
        out_spec = pl.BlockSpec((None, tm, nc), lambda j, i: (j // per, i, j % per))
    else:
        out_shape = jax.ShapeDtypeStruct((m, NDEV * nc), F32)
        out_spec = pl.BlockSpec((tm, nc), lambda j, i: (i, j))
    return _pc(body, name=name, out_shape=out_shape, grid=(NDEV, m // tm),
               in_specs=[pl.BlockSpec((tm, k), lambda j, i: (i, 0)),
                         pl.BlockSpec((None, k, nc), lambda j, i: (j, 0, 0))],
               out_specs=out_spec, sem=("parallel", "parallel"), comm=comm)(h, w_g)


def _dspec(sections, tm, nc, m_axis_first):
    per = NDEV // sections if sections > 1 else NDEV
    if sections > 1:
        if m_axis_first:
            return pl.BlockSpec((None, tm, nc), lambda i, j: (j // per, i, j % per))
        return pl.BlockSpec((None, tm, nc), lambda j, i: (j // per, i, j % per))
    if m_axis_first:
        return pl.BlockSpec((tm, nc), lambda i, j: (i, j))
    return pl.BlockSpec((tm, nc), lambda j, i: (i, j))


def _mm_din(dproj, w_g, sections, name, comm=None, tiles=None, prev=None):
    k, nc = w_g.shape[1], w_g.shape[2]
    m = dproj.shape[-2]
    tm = _tile(m, 512)
    t0, nt = tiles if tiles is not None else (0, m // tm)
    per = NDEV // sections if sections > 1 else NDEV

    def body(*refs):
        d_ref, w_ref, o_ref = refs[0], refs[1], refs[-1]
        j = pl.program_id(1)
        acc = _dot_nt(d_ref[...], w_ref[...])

        @pl.when(j == 0)
        def _():
            o_ref[...] = acc

        @pl.when(j > 0)
        def _():
            o_ref[...] += acc

    if sections > 1:
        dspec = pl.BlockSpec((None, tm, nc), lambda i, j: (j // per, i + t0, j % per))
    else:
        dspec = pl.BlockSpec((tm, nc), lambda i, j: (i + t0, j))
    in_specs = [dspec, pl.BlockSpec((None, k, nc), lambda i, j: (j, 0, 0))]
    args = [dproj, w_g]
    if prev is not None:
        in_specs.append(ANY)
        args.append(prev)
    return _pc(body, name=name, out_shape=jax.ShapeDtypeStruct((m, k), F32), grid=(nt, NDEV), in_specs=in_specs,
               out_specs=pl.BlockSpec((tm, k), lambda i, j: (i + t0, 0)), sem=("parallel", "arbitrary"),
               comm=comm, aliases={2: 0} if prev is not None else None)(*args)


def _mm_dw_in(h, dproj, nc, sections, name, comm=None):
    m, k = h.shape
    tk = _tile(m, 512)

    def body(h_ref, d_ref, o_ref):
        kk = pl.program_id(1)
        acc = _dot_tn(h_ref[...], d_ref[...])

        @pl.when(kk == 0)
        def _():
            o_ref[...] = acc

        @pl.when(kk > 0)
        def _():
            o_ref[...] += acc

    return _pc(body, name=name, out_shape=jax.ShapeDtypeStruct((NDEV, k, nc), F32), grid=(NDEV, m // tk),
               in_specs=[pl.BlockSpec((tk, k), lambda j, i: (i, 0)), _dspec(sections, tk, nc, False)],
               out_specs=pl.BlockSpec((None, k, nc), lambda j, i: (j, 0, 0)),
               sem=("parallel", "arbitrary"), comm=comm)(h, dproj)


def _out_proj(ybr, w_out, x, mod, t_seq, name, comm=None):
    m, di = ybr.shape
    d = w_out.shape[1]
    tm = _tile(t_seq, 512)
    per = t_seq // tm

    def body(y_ref, w_ref, x_ref, mod_ref, yo_ref, xn_ref):
        yo = _dot(y_ref[...], w_ref[...])
        yo_ref[...] = yo
        xn_ref[...] = x_ref[...] + mod_ref[0, 2:3, :] * yo

    row = pl.BlockSpec((tm, d), lambda i: (i, 0))
    return _pc(body, name=name,
               out_shape=[jax.ShapeDtypeStruct((m, d), F32), jax.ShapeDtypeStruct((m, d), F32)],
               grid=(m // tm,),
               in_specs=[pl.BlockSpec((tm, di), lambda i: (i, 0)), pl.BlockSpec((di, d), lambda i: (0, 0)), row,
                         pl.BlockSpec((1, 3, d), lambda i: (i // per, 0, 0))],
               out_specs=[row, row], sem=("parallel",), comm=comm)(ybr, w_out, x, mod)


def _gate_bwd(dxn, yout, mod, t_seq, name):
    m, d = dxn.shape
    nb = m // t_seq
    tm = _tile(t_seq, 512)
    per = t_seq // tm

    def body(dxn_ref, yo_ref, mod_ref, dy_ref, dgate_ref):
        i = pl.program_id(0)
        dv = dxn_ref[...]
        dy_ref[...] = (mod_ref[0, 2:3, :] * dv).astype(BF16)

        @pl.when(i % per == 0)
        def _():
            dgate_ref[...] = jnp.zeros_like(dgate_ref)

        dgate_ref[0] += jnp.sum(dv * yo_ref[...], axis=0, keepdims=True)

    row = pl.BlockSpec((tm, d), lambda i: (i, 0))
    return _pc(body, name=name,
               out_shape=[jax.ShapeDtypeStruct((m, d), BF16), jax.ShapeDtypeStruct((nb, 1, d), F32)],
               grid=(m // tm,),
               in_specs=[row, row, pl.BlockSpec((1, 3, d), lambda i: (i // per, 0, 0))],
               out_specs=[row, pl.BlockSpec((1, 1, d), lambda i: (i // per, 0, 0))],
               sem=("arbitrary",))(dxn, yout, mod)


def _mm_dybr(dy, w_out, name, comm=None):
    m, d = dy.shape
    di = w_out.shape[0]
    tm = _tile(m, 512)

    def body(dy_ref, w_ref, o_ref):
        o_ref[...] = _dot_nt(dy_ref[...], w_ref[...])

    return _pc(body, name=name, out_shape=jax.ShapeDtypeStruct((m, di), F32), grid=(m // tm,),
               in_specs=[pl.BlockSpec((tm, d), lambda i: (i, 0)), pl.BlockSpec((di, d), lambda i: (0, 0))],
               out_specs=pl.BlockSpec((tm, di), lambda i: (i, 0)), sem=("parallel",), comm=comm)(dy, w_out)


def _mm_dw_out(ybr, dy, name, comm=None):
    m, di = ybr.shape
    d = dy.shape[1]
    tk = _tile(m, 512)
    tn = _tile(di, 1024)

    def body(y_ref, dy_ref, o_ref):
        kk = pl.program_id(1)
        acc = _dot_tn(y_ref[...], dy_ref[...])

        @pl.when(kk == 0)
        def _():
            o_ref[...] = acc

        @pl.when(kk > 0)
        def _():
            o_ref[...] += acc

    return _pc(body, name=name, out_shape=jax.ShapeDtypeStruct((di, d), F32), grid=(di // tn, m // tk),
               in_specs=[pl.BlockSpec((tk, tn), lambda n, k: (k, n)), pl.BlockSpec((tk, d), lambda n, k: (k, 0))],
               out_specs=pl.BlockSpec((tn, d), lambda n, k: (n, 0)), sem=("parallel", "arbitrary"),
               comm=comm)(ybr, dy)


def _sgu_mask():
    t = lax.broadcasted_iota(jnp.int32, (SG_BLOCK, SG_BLOCK), 0)
    s = lax.broadcasted_iota(jnp.int32, (SG_BLOCK, SG_BLOCK), 1)
    return (s // CHUNK) <= (t // CHUNK)


def _a_mid_fwd(proj, ln_g, ln_b, w_s, bs_t, t_seq, comm=None):
    m, n3 = proj.shape
    di = n3 // 3
    gd = di // SG_GROUPS
    r = _tile(t_seq, 256)
    nblk = r // SG_BLOCK

    def body(p_ref, lg_ref, lb_ref, ws_ref, bs_ref, ybr_ref, s_scr):
        v = _gelu(p_ref[:, di:2 * di])
        mu = jnp.mean(v, axis=-1, keepdims=True)
        vc = v - mu
        rstd = lax.rsqrt(jnp.mean(vc * vc, axis=-1, keepdims=True) + EPS)
        vb = (vc * rstd * lg_ref[...] + lb_ref[...]).astype(BF16)
        mask = _sgu_mask()
        for gi in range(SG_GROUPS):
            ws = jnp.where(mask, ws_ref[gi], 0.0).astype(BF16)
            bcol = bs_ref[:, gi:gi + 1]
            for b in range(nblk):
                rows = slice(b * SG_BLOCK, (b + 1) * SG_BLOCK)
                cols = slice(gi * gd, (gi + 1) * gd)
                s_scr[rows, cols] = _dot(ws, vb[rows, cols]) + bcol
        gg = p_ref[:, 2 * di:]
        ybr_ref[...] = (_gelu(p_ref[:, :di]) * s_scr[...] * (gg * _sigmoid(gg))).astype(BF16)

    vec = pl.BlockSpec((1, di), lambda i: (0, 0))
    return _pc(body, name="a_mid_fwd", out_shape=jax.ShapeDtypeStruct((m, di), BF16), grid=(m // r,),
               in_specs=[pl.BlockSpec((r, n3), lambda i: (i, 0)), vec, vec,
                         pl.BlockSpec((SG_GROUPS, SG_BLOCK, SG_BLOCK), lambda i: (0, 0, 0)),
                         pl.BlockSpec((SG_BLOCK, 128), lambda i: (0, 0))],
               out_specs=pl.BlockSpec((r, di), lambda i: (i, 0)),
               scratch=[pltpu.VMEM((r, di), F32)], sem=("parallel",), comm=comm)(proj, ln_g, ln_b, w_s, bs_t)


def _a_mid_bwd(proj, dybr, ln_g, ln_b, w_s, bs_t, t_seq, comm=None):
    m, n3 = proj.shape
    di = n3 // 3
    gd = di // SG_GROUPS
    r = _tile(t_seq, 256)
    nblk = r // SG_BLOCK

    def body(p_ref, dy_ref, lg_ref, lb_ref, ws_ref, bs_ref,
             dp_ref, dlg_ref, dlb_ref, dws_ref, dbs_ref, s_scr, dvl_scr):
        i = pl.program_id(0)

        @pl.when(i == 0)
        def _():
            dlg_ref[...] = jnp.zeros_like(dlg_ref)
            dlb_ref[...] = jnp.zeros_like(dlb_ref)
            dws_ref[...] = jnp.zeros_like(dws_ref)
            dbs_ref[...] = jnp.zeros_like(dbs_ref)

        v_pre = p_ref[:, di:2 * di]
        v = _gelu(v_pre)
        mu = jnp.mean(v, axis=-1, keepdims=True)
        vc = v - mu
        rstd = lax.rsqrt(jnp.mean(vc * vc, axis=-1, keepdims=True) + EPS)
        vhat = vc * rstd
        lg = lg_ref[...]
        vb = (vhat * lg + lb_ref[...]).astype(BF16)
        u_pre = p_ref[:, :di]
        u = _gelu(u_pre)
        gg = p_ref[:, 2 * di:]
        sg = _sigmoid(gg)
        dyv = dy_ref[...]
        dus = dyv * (gg * sg)
        dsb = (dus * u).astype(BF16)
        ds32 = dus * u
        mask = _sgu_mask()
        lane = lax.broadcasted_iota(jnp.int32, (SG_BLOCK, 128), 1)
        dbs_acc = jnp.zeros((SG_BLOCK, 128), F32)
        for gi in range(SG_GROUPS):
            ws = jnp.where(mask, ws_ref[gi], 0.0).astype(BF16)
            bcol = bs_ref[:, gi:gi + 1]
            cols = slice(gi * gd, (gi + 1) * gd)
            dws_acc = jnp.zeros((SG_BLOCK, SG_BLOCK), F32)
            dbs_col = jnp.zeros((SG_BLOCK, 1), F32)
            for b in range(nblk):
                rows = slice(b * SG_BLOCK, (b + 1) * SG_BLOCK)
                s_scr[rows, cols] = _dot(ws, vb[rows, cols]) + bcol
                dvl_scr[rows, cols] = _dot_tn(ws, dsb[rows, cols])
                dws_acc += _dot_nt(dsb[rows, cols], vb[rows, cols])
                dbs_col += jnp.sum(ds32[rows, cols], axis=-1, keepdims=True)
            dws_ref[gi] += jnp.where(mask, dws_acc, 0.0)
            dbs_acc += jnp.where(lane == gi, dbs_col, 0.0)
        dbs_ref[...] += dbs_acc
        s = s_scr[...]
        dp_ref[:, :di] = (dyv * s * (gg * sg) * _dgelu(u_pre)).astype(BF16)
        dp_ref[:, 2 * di:] = (dyv * u * s * (sg * (1.0 + gg * (1.0 - sg)))).astype(BF16)
        dvl = dvl_scr[...]
        dlg_ref[...] += jnp.sum(dvl * vhat, axis=0, keepdims=True)
        dlb_ref[...] += jnp.sum(dvl, axis=0, keepdims=True)
        dvh = dvl * lg
        dv = rstd * (dvh - jnp.mean(dvh, axis=-1, keepdims=True)
                     - vhat * jnp.mean(dvh * vhat, axis=-1, keepdims=True))
        dp_ref[:, di:2 * di] = (dv * _dgelu(v_pre)).astype(BF16)

    vec = pl.BlockSpec((1, di), lambda i: (0, 0))
    wsb = pl.BlockSpec((SG_GROUPS, SG_BLOCK, SG_BLOCK), lambda i: (0, 0, 0))
    bsb = pl.BlockSpec((SG_BLOCK, 128), lambda i: (0, 0))
    return _pc(body, name="a_mid_bwd",
               out_shape=[jax.ShapeDtypeStruct((m, n3), BF16), jax.ShapeDtypeStruct((1, di), F32),
                          jax.ShapeDtypeStruct((1, di), F32),
                          jax.ShapeDtypeStruct((SG_GROUPS, SG_BLOCK, SG_BLOCK), F32),
                          jax.ShapeDtypeStruct((SG_BLOCK, 128), F32)],
               grid=(m // r,),
               in_specs=[pl.BlockSpec((r, n3), lambda i: (i, 0)), pl.BlockSpec((r, di), lambda i: (i, 0)),
                         vec, vec, wsb, bsb],
               out_specs=[pl.BlockSpec((r, n3), lambda i: (i, 0)), vec, vec, wsb, bsb],
               scratch=[pltpu.VMEM((r, di), F32), pltpu.VMEM((r, di), F32)],
               sem=("arbitrary",), comm=comm)(proj, dybr, ln_g, ln_b, w_s, bs_t)


def _hgrn_dims(t_seq, di):
    tr = _tile(t_seq, 256)
    hc = _tile(di, 1024)
    return tr, hc, hc // HEAD_DIM


def _hgrn_gates(f_ref, lb, a_scr, k_scr, tr):
    sig = _sigmoid(f_ref[...])
    fg = lb + (1.0 - lb) * sig
    k_scr[...] = 1.0 - fg
    logf = jnp.log(fg)
    g = min(CUM_ROWS, tr)
    tri = _tri_mask(g, reverse=False)
    for rg in range(tr // g):
        a_scr[rg * g:(rg + 1) * g, :] = _tri_apply(tri, logf[rg * g:(rg + 1) * g, :])
    return sig, fg


def _hgrn_fwd(proj, lbj, gn, nb, t_seq):
    _, m, di = proj.shape
    tr, hc, hpg = _hgrn_dims(t_seq, di)
    nt, nhg, ncl = t_seq // tr, di // hc, tr // CHUNK
    nheads = di // HEAD_DIM

    def body(q_ref, f_ref, i_ref, g_ref, lb_ref, gn_ref, o_ref, ybr_ref, st_ref, st_scr, a_scr, k_scr):
        t = pl.program_id(2)

        @pl.when(t == 0)
        def _():
            st_scr[...] = jnp.zeros_like(st_scr)

        _hgrn_gates(f_ref, lb_ref[0:1, :], a_scr, k_scr, tr)
        gnv = gn_ref[...]
        rr = lax.broadcasted_iota(jnp.int32, (CHUNK, CHUNK), 0)
        cc = lax.broadcasted_iota(jnp.int32, (CHUNK, CHUNK), 1)
        causal = cc <= rr

        def chunk(n, carry):
            rows = pl.ds(pl.multiple_of(n * CHUNK, CHUNK), CHUNK)
            lanes = [slice(hd * HEAD_DIM, (hd + 1) * HEAD_DIM) for hd in range(hpg)]
            hs = []
            for hd, ls in enumerate(lanes):
                h = {}
                ah, kh = a_scr[rows, ls], k_scr[rows, ls]
                qp = q_ref[rows, ls]
                qh = qp * _sigmoid(qp)
                h["vb"] = i_ref[rows, ls].astype(BF16)
                aref, alast = ah[CHUNK // 2 - 1:CHUNK // 2, :], ah[CHUNK - 1:CHUNK, :]
                h["q_in"] = (qh * jnp.exp(ah - aref)).astype(BF16)
                h["k_in"] = (kh * jnp.exp(aref - ah)).astype(BF16)
                h["q_out"] = (qh * jnp.exp(ah)).astype(BF16)
                h["k_out"] = (kh * jnp.exp(alast - ah)).astype(BF16)
                h["dec"] = jnp.exp(alast)
                st = st_scr[hd]
                st_ref[n, hd] = st
                h["st"] = st
                hs.append(h)
            for h in hs:
                h["scores"] = _dot_nt(h["q_in"], h["k_in"])
                h["o_inter"] = _dot_nt(h["q_out"], h["st"].astype(BF16))
                h["st_mm"] = _dot_tn(h["vb"], h["k_out"])
            for h in hs:
                h["o"] = _dot(jnp.where(causal, h["scores"], 0.0).astype(BF16), h["vb"]) + h["o_inter"]
            for hd, (h, ls) in enumerate(zip(hs, lanes)):
                st_scr[hd] = h["st"] * h["dec"] + h["st_mm"]
                o = h["o"]
                o_ref[rows, ls] = o
                rstd = lax.rsqrt(jnp.mean(o * o, axis=-1, keepdims=True) + EPS)
                gg = g_ref[rows, ls]
                ybr_ref[rows, ls] = ((o * rstd * gnv) * (gg * _sigmoid(gg))).astype(BF16)
            return carry

        lax.fori_loop(0, ncl, chunk, 0)

    def sec(s):
        return pl.BlockSpec((None, tr, hc), lambda hg, b, t: (s, b * nt + t, hg))

    blk = pl.BlockSpec((tr, hc), lambda hg, b, t: (b * nt + t, hg))
    return _pc(body, name="hgrn_fwd",
               out_shape=[jax.ShapeDtypeStruct((m, di), F32), jax.ShapeDtypeStruct((m, di), BF16),
                          jax.ShapeDtypeStruct((m // CHUNK, nheads, HEAD_DIM, HEAD_DIM), F32)],
               grid=(nhg, nb, nt),
               in_specs=[sec(0), sec(1), sec(2), sec(3), pl.BlockSpec((2, hc), lambda hg, b, t: (0, hg)),
                         pl.BlockSpec((1, HEAD_DIM), lambda hg, b, t: (0, 0))],
               out_specs=[blk, blk, pl.BlockSpec((ncl, hpg, HEAD_DIM, HEAD_DIM),
                                                 lambda hg, b, t: (b * nt + t, hg, 0, 0))],
               scratch=[pltpu.VMEM((hpg, HEAD_DIM, HEAD_DIM), F32), pltpu.VMEM((tr, hc), F32),
                        pltpu.VMEM((tr, hc), F32)],
               sem=("parallel", "arbitrary", "arbitrary"))(proj, proj, proj, proj, lbj, gn)


def _hgrn_bwd(proj, o_all, dybr, states, lbj, gn, nb, t_seq, comm=None):
    _, m, di = proj.shape
    tr, hc, hpg = _hgrn_dims(t_seq, di)
    nt, nhg, ncl = t_seq // tr, di // hc, tr // CHUNK

    def body(q_ref, f_ref, i_ref, g_ref, o_ref, dy_ref, st_ref, lb_ref, gn_ref,
             dp_ref, dlb_ref, dgn_ref, dst_scr, a_scr, k_scr, da_scr, dk_scr):
        hg, b, t = pl.program_id(0), pl.program_id(1), pl.program_id(2)

        @pl.when(t == 0)
        def _():
            dst_scr[...] = jnp.zeros_like(dst_scr)

        @pl.when((b == 0) & (t == 0))
        def _():
            dlb_ref[...] = jnp.zeros_like(dlb_ref)

        @pl.when((hg == 0) & (b == 0) & (t == 0))
        def _():
            dgn_ref[...] = jnp.zeros_like(dgn_ref)

        lb = lb_ref[0:1, :]
        sig, fg = _hgrn_gates(f_ref, lb, a_scr, k_scr, tr)
        gnv = gn_ref[...]
        rr = lax.broadcasted_iota(jnp.int32, (CHUNK, CHUNK), 0)
        cc = lax.broadcasted_iota(jnp.int32, (CHUNK, CHUNK), 1)
        causal = cc <= rr
        rowi = lax.broadcasted_iota(jnp.int32, (CHUNK, HEAD_DIM), 0)

        def chunk(it, carry):
            n = ncl - 1 - it
            rows = pl.ds(pl.multiple_of(n * CHUNK, CHUNK), CHUNK)
            lanes = [slice(hd * HEAD_DIM, (hd + 1) * HEAD_DIM) for hd in range(hpg)]
            hs = []
            for hd, ls in enumerate(lanes):
                h = {}
                ah, kh = a_scr[rows, ls], k_scr[rows, ls]
                qp = q_ref[rows, ls]
                sq = _sigmoid(qp)
                qh = qp * sq
                h["dsilu_q"] = sq * (1.0 + qp * (1.0 - sq))
                h["vb"] = i_ref[rows, ls].astype(BF16)
                aref, alast = ah[CHUNK // 2 - 1:CHUNK // 2, :], ah[CHUNK - 1:CHUNK, :]
                h["e1"], h["e2"] = jnp.exp(ah - aref), jnp.exp(aref - ah)
                h["e3"], h["e4"] = jnp.exp(ah), jnp.exp(alast - ah)
                h["dec"] = jnp.exp(alast)
                h["q_in"], h["k_in"], h["q_out"], h["k_out"] = qh * h["e1"], kh * h["e2"], qh * h["e3"], kh * h["e4"]
                for nm in ("q_in", "k_in", "q_out", "k_out"):
                    h[nm + "_b"] = h[nm].astype(BF16)
                o = o_ref[rows, ls]
                rstd = lax.rsqrt(jnp.mean(o * o, axis=-1, keepdims=True) + EPS)
                ohat = o * rstd
                gg = g_ref[rows, ls]
                sg = _sigmoid(gg)
                dyv = dy_ref[rows, ls]
                d_on = dyv * (gg * sg)
                dp_ref[3, rows, ls] = (dyv * (ohat * gnv) * (sg * (1.0 + gg * (1.0 - sg)))).astype(BF16)
                h["dgn"] = jnp.sum(d_on * ohat, axis=0, keepdims=True)
                dohat = d_on * gnv
                do = rstd * (dohat - ohat * jnp.mean(dohat * ohat, axis=-1, keepdims=True))
                h["do_b"] = do.astype(BF16)
                h["st_prev"] = st_ref[n, hd]
                h["dst"] = dst_scr[hd]
                hs.append(h)
            for h in hs:
                dst_b = h["dst"].astype(BF16)
                h["scores"] = _dot_nt(h["q_in_b"], h["k_in_b"])
                h["dscores"] = _dot_nt(h["do_b"], h["vb"])
                h["dv_inter"] = _dot_nt(h["k_out_b"], dst_b)
                h["dq_out"] = _dot(h["do_b"], h["st_prev"].astype(BF16))
                h["dk_out"] = _dot(h["vb"], dst_b)
                h["dst_mm"] = _dot_tn(h["do_b"], h["q_out_b"])
            for h in hs:
                scores = jnp.where(causal, h["scores"], 0.0).astype(BF16)
                dscores = jnp.where(causal, h["dscores"], 0.0).astype(BF16)
                h["dv"] = _dot_tn(scores, h["do_b"]) + h["dv_inter"]
                h["dq_in"] = _dot(dscores, h["k_in_b"])
                h["dk_in"] = _dot_tn(dscores, h["q_in_b"])
            dgn = hs[0]["dgn"]
            for h in hs[1:]:
                dgn = dgn + h["dgn"]
            dgn_ref[...] += dgn
            for hd, (h, ls) in enumerate(zip(hs, lanes)):
                ddec = jnp.sum(h["dst"] * h["st_prev"], axis=0, keepdims=True)
                dst_scr[hd] = h["dst"] * h["dec"] + h["dst_mm"]
                dp_ref[2, rows, ls] = h["dv"].astype(BF16)
                dq = h["dq_in"] * h["e1"] + h["dq_out"] * h["e3"]
                dp_ref[0, rows, ls] = (dq * h["dsilu_q"]).astype(BF16)
                dk_scr[rows, ls] = h["dk_in"] * h["e2"] + h["dk_out"] * h["e4"]
                t_in = h["dq_in"] * h["q_in"] - h["dk_in"] * h["k_in"]
                t_out = h["dk_out"] * h["k_out"]
                da = t_in + h["dq_out"] * h["q_out"] - t_out
                da_ref_row = -jnp.sum(t_in, axis=0, keepdims=True)
                da_last_row = jnp.sum(t_out, axis=0, keepdims=True) + ddec * h["dec"]
                da = da + jnp.where(rowi == CHUNK // 2 - 1, da_ref_row, 0.0) \
                        + jnp.where(rowi == CHUNK - 1, da_last_row, 0.0)
                da_scr[rows, ls] = da
            return carry

        lax.fori_loop(0, ncl, chunk, 0)
        g = min(CUM_ROWS, tr)
        tri = _tri_mask(g, reverse=True)
        for rg in range(tr // g):
            rs = slice(rg * g, (rg + 1) * g)
            dlogf = _tri_apply(tri, da_scr[rs, :])
            df = dlogf / fg[rs, :] - dk_scr[rs, :]
            sgr = sig[rs, :]
            dp_ref[1, rs, :] = (df * (1.0 - lb) * (sgr * (1.0 - sgr))).astype(BF16)
            dlb_ref[...] += jnp.sum(df * (1.0 - sgr), axis=0, keepdims=True) * lb_ref[1:2, :]

    def sec(s):
        return pl.BlockSpec((None, tr, hc), lambda hg, b, t: (s, b * nt + (nt - 1 - t), hg))

    blk = pl.BlockSpec((tr, hc), lambda hg, b, t: (b * nt + (nt - 1 - t), hg))
    return _pc(body, name="hgrn_bwd",
               out_shape=[jax.ShapeDtypeStruct((4, m, di), BF16), jax.ShapeDtypeStruct((1, di), F32),
                          jax.ShapeDtypeStruct((1, HEAD_DIM), F32)],
               grid=(nhg, nb, nt),
               in_specs=[sec(0), sec(1), sec(2), sec(3), blk, blk,
                         pl.BlockSpec((ncl, hpg, HEAD_DIM, HEAD_DIM),
                                      lambda hg, b, t: (b * nt + (nt - 1 - t), hg, 0, 0)),
                         pl.BlockSpec((2, hc), lambda hg, b, t: (0, hg)),
                         pl.BlockSpec((1, HEAD_DIM), lambda hg, b, t: (0, 0))],
               out_specs=[pl.BlockSpec((4, tr, hc), lambda hg, b, t: (0, b * nt + (nt - 1 - t), hg)),
                          pl.BlockSpec((1, hc), lambda hg, b, t: (0, hg)),
                          pl.BlockSpec((1, HEAD_DIM), lambda hg, b, t: (0, 0))],
               scratch=[pltpu.VMEM((hpg, HEAD_DIM, HEAD_DIM), F32)] + [pltpu.VMEM((tr, hc), F32)] * 4,
               sem=("arbitrary", "arbitrary", "arbitrary"), comm=comm)(
                   proj, proj, proj, proj, o_all, dybr, states, lbj, gn)


def _final_loss(x, gain, target):
    m, d = x.shape
    tm = _tile(m, 512)

    def body(x_ref, g_ref, t_ref, dx_ref, loss_ref, dg_ref):
        i = pl.program_id(0)
        xv, g = x_ref[...], g_ref[...]
        rstd = lax.rsqrt(jnp.mean(xv * xv, axis=-1, keepdims=True) + EPS)
        xhat = xv * rstd
        err = xhat * g - t_ref[...]
        dy = err * (1.0 / d)
        dxhat = dy * g
        dx_ref[...] = rstd * (dxhat - xhat * jnp.mean(dxhat * xhat, axis=-1, keepdims=True))

        @pl.when(i == 0)
        def _():
            loss_ref[...] = jnp.zeros_like(loss_ref)
            dg_ref[...] = jnp.zeros_like(dg_ref)

        loss_ref[...] += 0.5 * jnp.sum(jnp.mean(err * err, axis=-1, keepdims=True), axis=0, keepdims=True)
        dg_ref[...] += jnp.sum(dy * xhat, axis=0, keepdims=True)

    row = pl.BlockSpec((tm, d), lambda i: (i, 0))
    return _pc(body, name="final_loss",
               out_shape=[jax.ShapeDtypeStruct((m, d), F32), jax.ShapeDtypeStruct((1, 1), F32),
                          jax.ShapeDtypeStruct((1, d), F32)],
               grid=(m // tm,),
               in_specs=[row, pl.BlockSpec((1, d), lambda i: (0, 0)), row],
               out_specs=[row, pl.BlockSpec((1, 1), lambda i: (0, 0)), pl.BlockSpec((1, d), lambda i: (0, 0))],
               sem=("arbitrary",))(x, gain, target)


def _adamw(parts, w, m, v, name, comm=None):
    r, c = w.shape
    tr = _tile(r, 256)
    npart = len(parts)
    c1 = 1.0 - ADAM_B1 ** ADAM_STEP
    c2 = 1.0 - ADAM_B2 ** ADAM_STEP

    def body(*refs):
        p_refs = refs[:npart]
        w_ref, m_ref, v_ref, g_ref, d_ref, nm_ref, nv_ref = refs[npart:]
        g = p_refs[0][...].astype(F32)
        for p in p_refs[1:]:
            g = g + p[...].astype(F32)
        nm = ADAM_B1 * m_ref[...] + (1.0 - ADAM_B1) * g
        nv = ADAM_B2 * v_ref[...] + (1.0 - ADAM_B2) * (g * g)
        g_ref[...] = g
        nm_ref[...] = nm
        nv_ref[...] = nv
        d_ref[...] = -ADAM_LR * ((nm / c1) / (jnp.sqrt(nv / c2) + ADAM_EPS) + ADAM_WD * w_ref[...])

    blk = pl.BlockSpec((tr, c), lambda i: (i, 0))
    return _pc(body, name=name, out_shape=[jax.ShapeDtypeStruct((r, c), F32)] * 4, grid=(r // tr,),
               in_specs=[blk] * (npart + 3), out_specs=[blk] * 4, sem=("parallel",), comm=comm)(*parts, w, m, v)


_SMALL = ["norm_gain", "a_ln_gain", "a_ln_bias", "a_w_s", "a_b_s", "b_lower_bounds", "b_gn_gain", "final_gain"]


def _pack(arrs):
    flat = jnp.concatenate([a.reshape(-1) for a in arrs])
    rows = -(-flat.shape[0] // 1024) * 8
    return jnp.pad(flat, (0, rows * 128 - flat.shape[0])).reshape(rows, 128)


def _unpack(buf, like):
    flat = buf.reshape(-1)
    out, off = [], 0
    for a in like:
        out.append(flat[off:off + a.size].reshape(a.shape))
        off += a.size
    return out


def kernel(x, c, norm_gain, w_ada, b_ada, a_w_in, a_ln_gain, a_ln_bias, a_w_s, a_b_s, a_w_out, b_w_in, b_lower_bounds, b_gn_gain, b_w_out, final_gain, loss_target, m_norm_gain, m_w_ada, m_b_ada, m_a_w_in, m_a_ln_gain, m_a_ln_bias, m_a_w_s, m_a_b_s, m_a_w_out, m_b_w_in, m_b_lower_bounds, m_b_gn_gain, m_b_w_out, m_final_gain, v_norm_gain, v_w_ada, v_b_ada, v_a_w_in, v_a_ln_gain, v_a_ln_bias, v_a_w_s, v_a_b_s, v_a_w_out, v_b_w_in, v_b_lower_bounds, v_b_gn_gain, v_b_w_out, v_final_gain):
    w = dict(norm_gain=norm_gain, w_ada=w_ada, b_ada=b_ada, a_w_in=a_w_in, a_ln_gain=a_ln_gain,
             a_ln_bias=a_ln_bias, a_w_s=a_w_s, a_b_s=a_b_s, a_w_out=a_w_out, b_w_in=b_w_in,
             b_lower_bounds=b_lower_bounds, b_gn_gain=b_gn_gain, b_w_out=b_w_out, final_gain=final_gain)
    mo = dict(norm_gain=m_norm_gain, w_ada=m_w_ada, b_ada=m_b_ada, a_w_in=m_a_w_in, a_ln_gain=m_a_ln_gain,
              a_ln_bias=m_a_ln_bias, a_w_s=m_a_w_s, a_b_s=m_a_b_s, a_w_out=m_a_w_out, b_w_in=m_b_w_in,
              b_lower_bounds=m_b_lower_bounds, b_gn_gain=m_b_gn_gain, b_w_out=m_b_w_out, final_gain=m_final_gain)
    vo = dict(norm_gain=v_norm_gain, w_ada=v_w_ada, b_ada=v_b_ada, a_w_in=v_a_w_in, a_ln_gain=v_a_ln_gain,
              a_ln_bias=v_a_ln_bias, a_w_s=v_a_w_s, a_b_s=v_a_b_s, a_w_out=v_a_w_out, b_w_in=v_b_w_in,
              b_lower_bounds=v_b_lower_bounds, b_gn_gain=v_b_gn_gain, b_w_out=v_b_w_out, final_gain=v_final_gain)

    nb, t_seq, d = x.shape
    m = nb * t_seq
    ncol_ada = w_ada.shape[2]
    xi, yi, ci = lax.axis_index("x"), lax.axis_index("y"), lax.axis_index("c")
    me = 4 * xi + 2 * yi + ci

    c_g, wa_in_g = _all_gather([c, a_w_in[0].astype(BF16)], "gather_c_wa")

    c_all = c_g.reshape(NDEV * nb, d)
    b_cols = lax.dynamic_slice(b_ada, (0, me * ncol_ada), (2, ncol_ada)).reshape(2, 1, ncol_ada)
    mod_part, lbj = _ada_fwd(c_all, w_ada, b_cols, b_lower_bounds)
    mod_all = _all_gather([mod_part], "gather_mod")[0]
    mod_mine = lax.dynamic_slice_in_dim(mod_all, me * nb, nb, axis=2)
    mod_mine = mod_mine.transpose(1, 2, 0, 3).reshape(2, nb, 3, d)
    mod0, mod1 = mod_mine[0], mod_mine[1]

    di = a_w_out.shape[1] * NDEV

    xf = x.reshape(m, d)
    tgt = loss_target.reshape(m, d)
    ng0, ng1 = norm_gain[0:1], norm_gain[1:2]
    h0 = _prenorm(xf, ng0, mod0, t_seq, "prenorm_a")
    proj_a, half = _mm_in(h0, wa_in_g, 1, "in_proj_a",
                          comm=_gather_first([a_w_out[0].astype(BF16), b_w_in[0].astype(BF16)]))
    bs_t = jnp.pad(a_b_s[0].T, ((0, 0), (0, 128 - SG_GROUPS)))
    ybr_a, (wa_out_g, wb_in_g, wb_out_half) = _a_mid_fwd(
        proj_a, a_ln_gain, a_ln_bias, a_w_s[0], bs_t, t_seq,
        comm=_join(_gather_second(half), _gather_first([b_w_out[0].astype(BF16)])))
    wa_out = wa_out_g.reshape(di, d)
    (yout_a, x1), (wb_out_g,) = _out_proj(ybr_a, wa_out, xf, mod0, t_seq, "out_proj_a",
                                         comm=_gather_second([wb_out_half]))
    wb_out = wb_out_g.reshape(di, d)
    h1 = _prenorm(x1, ng1, mod1, t_seq, "prenorm_b")
    proj_b = _mm_in(h1, wb_in_g, 4, "in_proj_b")
    o_b, ybr_b, states = _hgrn_fwd(proj_b, lbj, b_gn_gain, nb, t_seq)
    yout_b, x2 = _out_proj(ybr_b, wb_out, x1, mod1, t_seq, "out_proj_b")
    dx2, loss_part, d_final_gain = _final_loss(x2, final_gain.reshape(1, d), tgt)
    loss = lax.psum(loss_part[0, 0], ("x", "y", "c"))

    rows_out = a_w_out.shape[1]
    dy_b, dgate1 = _gate_bwd(dx2, yout_b, mod1, t_seq, "gate_bwd_b")
    dybr_b = _mm_dybr(dy_b, wb_out, "dybr_b")
    rs_wb_out = _ReduceScatter(_mm_dw_out(ybr_b, dy_b, "dw_out_b").reshape(NDEV, rows_out, d), "b_w_out")
    (dproj_b, d_lb, d_gn), got = _hgrn_bwd(proj_b, o_b, dybr_b, states, lbj, b_gn_gain, nb, t_seq,
                                           comm=rs_wb_out.swap_core())
    rs_wb_out.after_core(got[0])
    dh1, got = _mm_din(dproj_b, wb_in_g, 4, "dh_b", comm=rs_wb_out.swap_chips())
    rs_wb_out.after_chips(got[0])
    dx1, dss1, dgain1 = _prenorm_bwd(dh1, x1, ng1, mod1, dx2, t_seq, "prenorm_bwd_b")
    rs_wb_in = _ReduceScatter(_mm_dw_in(h1, dproj_b, wb_in_g.shape[2], 4, "dw_in_b"), "b_w_in")

    dy_a, dgate0 = _gate_bwd(dx1, yout_a, mod0, t_seq, "gate_bwd_a")
    dybr_a = _mm_dybr(dy_a, wa_out, "dybr_a")
    g_wa_out, got = _mm_dw_out(ybr_a, dy_a, "dw_out_a", comm=rs_wb_in.swap_core())
    rs_wb_in.after_core(got[0])
    rs_wa_out = _ReduceScatter(g_wa_out.reshape(NDEV, rows_out, d), "a_w_out")
    (dproj_a, d_lng, d_lnb, d_ws, d_bs_t), got = _a_mid_bwd(
        proj_a, dybr_a, a_ln_gain, a_ln_bias, a_w_s[0], bs_t, t_seq,
        comm=_join(rs_wb_in.swap_chips(), rs_wa_out.swap_core()))
    rs_wb_in.after_chips(got[0])
    rs_wa_out.after_core(got[1])
    g_wa_in, got = _mm_dw_in(h0, dproj_a, wa_in_g.shape[2], 1, "dw_in_a", comm=rs_wa_out.swap_chips())
    rs_wa_out.after_chips(got[0])
    rs_wa_in = _ReduceScatter(g_wa_in, "a_w_in")
    n_tiles = m // _tile(m, 512)
    assert n_tiles >= 2
    first_tiles = max(1, (3 * n_tiles) // 8)
    dh0, got = _mm_din(dproj_a, wa_in_g, 1, "dh_a_first", comm=rs_wa_in.swap_core(), tiles=(0, first_tiles))
    rs_wa_in.after_core(got[0])
    dh0, got = _mm_din(dproj_a, wa_in_g, 1, "dh_a_rest", comm=rs_wa_in.swap_chips(),
                       tiles=(first_tiles, n_tiles - first_tiles), prev=dh0)
    rs_wa_in.after_chips(got[0])
    dx0, dss0, dgain0 = _prenorm_bwd(dh0, xf, ng0, mod0, dx1, t_seq, "prenorm_bwd_a")
    grad_x = dx0.reshape(nb, t_seq, d)

    dmod = jnp.stack([jnp.concatenate([dss0, dgate0], axis=1), jnp.concatenate([dss1, dgate1], axis=1)])
    dmod_all = _all_gather([dmod.reshape(2, nb, 3 * d)], "gather_dmod")[0]
    dmod_all = dmod_all.transpose(1, 0, 2, 3).reshape(2, NDEV * nb, 3 * d)
    dmod_cols = lax.dynamic_slice_in_dim(dmod_all, me * ncol_ada, ncol_ada, axis=2)
    g_w_ada, g_b_ada = _ada_bwd(c_all, dmod_cols, dmod_all)

    part = dict(norm_gain=jnp.concatenate([dgain0, dgain1], axis=0), a_ln_gain=d_lng, a_ln_bias=d_lnb,
                a_w_s=d_ws[None], a_b_s=d_bs_t[:, :SG_GROUPS].T[None],
                b_lower_bounds=jnp.concatenate([-d_lb, d_lb], axis=0), b_gn_gain=d_gn, final_gain=d_final_gain[0])
    small_like = [w[k] for k in _SMALL]
    parts_all = _all_gather([_pack([part[k].reshape(w[k].shape) for k in _SMALL])], "gather_small")[0]
    sm = _adamw([parts_all[k] for k in range(NDEV)], _pack(small_like), _pack([mo[k] for k in _SMALL]),
                _pack([vo[k] for k in _SMALL]), "adamw_small")
    sm = [dict(zip(_SMALL, _unpack(buf, small_like))) for buf in sm]

    res = {}
    for k in _SMALL:
        res[k] = tuple(s[k] for s in sm)
    rb = _adamw([g_b_ada], b_ada, mo["b_ada"], vo["b_ada"], "adamw_b_ada")
    res["b_ada"] = tuple(rb)
    sh = w_ada.shape
    ra = _adamw([g_w_ada.reshape(sh[0] * sh[1], sh[2])], w_ada.reshape(sh[0] * sh[1], sh[2]),
                mo["w_ada"].reshape(sh[0] * sh[1], sh[2]), vo["w_ada"].reshape(sh[0] * sh[1], sh[2]), "adamw_w_ada")
    res["w_ada"] = tuple(z.reshape(sh) for z in ra)

    for k, rs in (("b_w_out", rs_wb_out), ("b_w_in", rs_wb_in), ("a_w_out", rs_wa_out), ("a_w_in", rs_wa_in)):
        res[k] = tuple(z[None] for z in _adamw(rs.parts, w[k][0], mo[k][0], vo[k][0], "adamw_" + k))

    order = ["norm_gain", "w_ada", "b_ada", "a_w_in", "a_ln_gain", "a_ln_bias", "a_w_s", "a_b_s", "a_w_out",
             "b_w_in", "b_lower_bounds", "b_gn_gain", "b_w_out", "final_gain"]
    return (loss, grad_x, *[res[k][0] for k in order], *[res[k][1] for k in order],
            *[res[k][2] for k in order], *[res[k][3] for k in order])
```

```python
import functools
import math

import jax
import jax.numpy as jnp
from jax import lax
from jax.experimental import pallas as pl
from jax.experimental.pallas import tpu as pltpu

F32 = jnp.float32
BF16 = jnp.bfloat16
MESH = pl.DeviceIdType.MESH
NDEV = 8
EPS = 1e-6
CHUNK = 64
SG_BLOCK = 128
SG_GROUPS = 8
HEAD_DIM = 128
CUM_ROWS = 256
ADAM_LR, ADAM_B1, ADAM_B2, ADAM_EPS, ADAM_WD, ADAM_STEP = 0.001, 0.9, 0.999, 1e-08, 0.01, 10
VMEM_LIMIT = 56 * 1024 * 1024
ANY = pl.BlockSpec(memory_space=pl.ANY)


class _Hosted:
    def __init__(self, arrays, out_shapes, nsem, start, finish, aliases=None):
        self.arrays, self.out_shapes, self.nsem = list(arrays), list(out_shapes), nsem
        self.start, self.finish = start, finish
        self.aliases = dict(aliases or {})


def _join(*comms):
    arrays, outs, aliases, offs, nsem = [], [], {}, [], 0
    for cm in comms:
        offs.append((len(arrays), len(outs), nsem))
        for i, o in cm.aliases.items():
            aliases[len(arrays) + i] = len(outs) + o
        arrays += cm.arrays
        outs += cm.out_shapes
        nsem += cm.nsem

    def run(which):
        def f(ins, outs_, ss, rs, base):
            for cm, (ia, io, isem) in zip(comms, offs):
                getattr(cm, which)(ins[ia:ia + len(cm.arrays)], outs_[io:io + len(cm.out_shapes)], ss, rs, base + isem)
        return f

    return _Hosted(arrays, outs, nsem, run("start"), run("finish"), aliases)


def _pc(body, *, name, out_shape, grid=None, in_specs=None, out_specs=None, scratch=(), sem=None,
        grid_spec=None, comm=None, aliases=None):
    cp = dict(vmem_limit_bytes=VMEM_LIMIT)
    aliases = dict(aliases or {})
    if comm is None:
        if sem is not None:
            cp["dimension_semantics"] = sem
        kw = {"input_output_aliases": aliases}
        if grid_spec is not None:
            kw["grid_spec"] = grid_spec
        else:
            if grid is not None:
                kw["grid"] = grid
            if in_specs is not None:
                kw["in_specs"] = in_specs
            if out_specs is not None:
                kw["out_specs"] = out_specs
            kw["scratch_shapes"] = list(scratch)
        return pl.pallas_call(functools.partial(body), name=name, out_shape=out_shape,
                              compiler_params=pltpu.CompilerParams(**cp), **kw)

    single = not isinstance(out_shape, (list, tuple))
    outs_list = [out_shape] if single else list(out_shape)
    ospecs = [out_specs] if single else list(out_specs)
    n_in, n_out, n_ci, n_co, n_scr = len(in_specs), len(outs_list), len(comm.arrays), len(comm.out_shapes), len(scratch)
    cp["dimension_semantics"] = ("arbitrary",) * len(grid)

    def hosted(*refs):
        cin, hin = refs[:n_in], refs[n_in:n_in + n_ci]
        cout = refs[n_in + n_ci:n_in + n_ci + n_out]
        hout = refs[n_in + n_ci + n_out:n_in + n_ci + n_out + n_co]
        scr = refs[n_in + n_ci + n_out + n_co:n_in + n_ci + n_out + n_co + n_scr]
        ssem, rsem = refs[-2], refs[-1]
        first = functools.reduce(lambda p, q: p & q, [pl.program_id(a) == 0 for a in range(len(grid))])
        last = functools.reduce(lambda p, q: p & q, [pl.program_id(a) == grid[a] - 1 for a in range(len(grid))])

        @pl.when(first)
        def _():
            comm.start(hin, hout, ssem, rsem, 0)

        body(*cin, *cout, *scr)

        @pl.when(last)
        def _():
            comm.finish(hin, hout, ssem, rsem, 0)

    call = pl.pallas_call(
        hosted, name=name, grid=grid, in_specs=list(in_specs) + [ANY] * n_ci, out_specs=ospecs + [ANY] * n_co,
        out_shape=outs_list + comm.out_shapes,
        scratch_shapes=list(scratch) + [pltpu.SemaphoreType.DMA((comm.nsem,)), pltpu.SemaphoreType.DMA((comm.nsem,))],
        input_output_aliases={**aliases, **{n_in + i: n_out + o for i, o in comm.aliases.items()}},
        compiler_params=pltpu.CompilerParams(**cp))

    def run(*args):
        res = call(*args, *comm.arrays)
        comp = res[:n_out]
        return (comp[0] if single else comp), list(res[n_out:])

    return run


def _tile(n, pref):
    return pref if n % pref == 0 else n


def _sigmoid(x):
    return 1.0 / (1.0 + jnp.exp(-x))


def _gelu(x):
    c = math.sqrt(2.0 / math.pi)
    return 0.5 * x * (1.0 + jnp.tanh(c * (x + 0.044715 * (x * x * x))))


def _dgelu(x):
    c = math.sqrt(2.0 / math.pi)
    t = jnp.tanh(c * (x + 0.044715 * (x * x * x)))
    return 0.5 * (1.0 + t) + 0.5 * x * (1.0 - t * t) * (c * (1.0 + 3.0 * 0.044715 * (x * x)))


def _dot(a, b):
    return jnp.dot(a, b, preferred_element_type=F32)


def _dot_nt(a, b):
    return lax.dot_general(a, b, (((1,), (1,)), ((), ())), preferred_element_type=F32)


def _dot_tn(a, b):
    return lax.dot_general(a, b, (((0,), (0,)), ((), ())), preferred_element_type=F32)


def _tri_mask(n, reverse):
    r = lax.broadcasted_iota(jnp.int32, (n, n), 0)
    c = lax.broadcasted_iota(jnp.int32, (n, n), 1)
    same = (r // CHUNK) == (c // CHUNK)
    tri = (c >= r) if reverse else (c <= r)
    return jnp.where(same & tri, 1.0, 0.0).astype(BF16)


def _tri_apply(tri, x):
    hi = x.astype(BF16)
    r1 = x - hi.astype(F32)
    mid = r1.astype(BF16)
    lo = (r1 - mid.astype(F32)).astype(BF16)
    return _dot(tri, hi) + (_dot(tri, mid) + _dot(tri, lo))


def _all_gather(arrs, name):
    n = len(arrs)

    def body(*refs):
        ins, outs = refs[:n], refs[n:2 * n]
        send_sems, recv_sems, local_sems = refs[2 * n:]
        x, y, c = lax.axis_index("x"), lax.axis_index("y"), lax.axis_index("c")
        me, sibling = (x, y, c), (x, y, 1 - c)
        chips = [(1 - x, y), (x, 1 - y), (1 - x, 1 - y)]

        def blk(a, p):
            return outs[a].at[4 * p[0] + 2 * p[1] + p[2]]

        def copy(a, k, block, to, src=None):
            return pltpu.make_async_remote_copy(
                src_ref=blk(a, block) if src is None else src, dst_ref=blk(a, block),
                send_sem=send_sems.at[7 * a + k], recv_sem=recv_sems.at[7 * a + k],
                device_id=to, device_id_type=MESH)

        mine = [pltpu.make_async_copy(ins[a], blk(a, me), local_sems.at[a]) for a in range(n)]
        for m in mine:
            m.start()
        first = []
        for a in range(n):
            first.append(copy(a, 0, me, sibling, src=ins[a]))
            for j, chip in enumerate(chips):
                first.append(copy(a, 1 + j, me, (*chip, c), src=ins[a]))
        for cp in first:
            cp.start()
        passed = []
        for j, chip in enumerate(chips):
            for a in range(n):
                copy(a, 1 + j, (*chip, c), me).wait_recv()
                p = copy(a, 4 + j, (*chip, c), sibling)
                p.start()
                passed.append(p)
        for a in range(n):
            copy(a, 0, sibling, me).wait_recv()
            for j, chip in enumerate(chips):
                copy(a, 4 + j, (*chip, 1 - c), me).wait_recv()
        for cp in first + passed:
            cp.wait_send()
        for m in mine:
            m.wait()

    out_shape = [jax.ShapeDtypeStruct((NDEV,) + a.shape, a.dtype) for a in arrs]
    return _pc(body, name=name, out_shape=out_shape, in_specs=[ANY] * n, out_specs=[ANY] * n,
               scratch=[pltpu.SemaphoreType.DMA((7 * n,)), pltpu.SemaphoreType.DMA((7 * n,)),
                        pltpu.SemaphoreType.DMA((n,))])(*arrs)


def _gather_first(arrs):
    n = len(arrs)

    def parts(ins, outs, ss, rs, base):
        x, y, c = lax.axis_index("x"), lax.axis_index("y"), lax.axis_index("c")
        me, sibling = (x, y, c), (x, y, 1 - c)
        chips = [(1 - x, y), (x, 1 - y), (1 - x, 1 - y)]

        def blk(a, p):
            return outs[a].at[4 * p[0] + 2 * p[1] + p[2]]

        def copy(a, k, block, to):
            return pltpu.make_async_remote_copy(
                src_ref=ins[a], dst_ref=blk(a, block), send_sem=ss.at[base + 4 * a + k],
                recv_sem=rs.at[base + 4 * a + k], device_id=to, device_id_type=MESH)

        local = [pltpu.make_async_copy(ins[a], blk(a, me), ss.at[base + 4 * n + a]) for a in range(n)]
        sends, recvs = [], []
        for a in range(n):
            sends.append(copy(a, 0, me, sibling))
            recvs.append(copy(a, 0, sibling, me))
            for j, chip in enumerate(chips):
                sends.append(copy(a, 1 + j, me, (*chip, c)))
                recvs.append(copy(a, 1 + j, (*chip, c), me))
        return local, sends, recvs

    def start(ins, outs, ss, rs, base):
        local, sends, _ = parts(ins, outs, ss, rs, base)
        for cp in local + sends:
            cp.start()

    def finish(ins, outs, ss, rs, base):
        local, sends, recvs = parts(ins, outs, ss, rs, base)
        for cp in recvs:
            cp.wait_recv()
        for cp in sends:
            cp.wait_send()
        for cp in local:
            cp.wait()

    return _Hosted(arrs, [jax.ShapeDtypeStruct((NDEV,) + a.shape, a.dtype) for a in arrs], 5 * n, start, finish)


def _gather_second(bufs):
    n = len(bufs)

    def parts(ins, outs, ss, rs, base):
        x, y, c = lax.axis_index("x"), lax.axis_index("y"), lax.axis_index("c")
        sibling = (x, y, 1 - c)
        chips = [(1 - x, y), (x, 1 - y), (1 - x, 1 - y)]
        sends, recvs = [], []
        for a in range(n):
            for j, chip in enumerate(chips):
                mine = 4 * chip[0] + 2 * chip[1] + c
                theirs = 4 * chip[0] + 2 * chip[1] + (1 - c)
                sends.append(pltpu.make_async_remote_copy(
                    src_ref=ins[a].at[mine], dst_ref=outs[a].at[mine], send_sem=ss.at[base + 3 * a + j],
                    recv_sem=rs.at[base + 3 * a + j], device_id=sibling, device_id_type=MESH))
                recvs.append(pltpu.make_async_remote_copy(
                    src_ref=ins[a].at[theirs], dst_ref=outs[a].at[theirs], send_sem=ss.at[base + 3 * a + j],
                    recv_sem=rs.at[base + 3 * a + j], device_id=sibling, device_id_type=MESH))
        return sends, recvs

    def start(ins, outs, ss, rs, base):
        for cp in parts(ins, outs, ss, rs, base)[0]:
            cp.start()

    def finish(ins, outs, ss, rs, base):
        sends, recvs = parts(ins, outs, ss, rs, base)
        for cp in recvs:
            cp.wait_recv()
        for cp in sends:
            cp.wait_send()

    return _Hosted(bufs, [jax.ShapeDtypeStruct(b.shape, b.dtype) for b in bufs], 3 * n, start, finish,
                   aliases={a: a for a in range(n)})


def _swap(src, nblk, ids_fn, partner_fn):
    def copies(ins, outs, ss, rs, base):
        x, y, c = lax.axis_index("x"), lax.axis_index("y"), lax.axis_index("c")
        ids = ids_fn(x, y, c)
        partner = partner_fn(x, y, c)
        return [pltpu.make_async_remote_copy(
            src_ref=ins[0].at[ids[k]], dst_ref=outs[0].at[k], send_sem=ss.at[base + k], recv_sem=rs.at[base + k],
            device_id=partner, device_id_type=MESH) for k in range(nblk)]

    def start(ins, outs, ss, rs, base):
        for cp in copies(ins, outs, ss, rs, base):
            cp.start()

    def finish(ins, outs, ss, rs, base):
        for cp in copies(ins, outs, ss, rs, base):
            cp.wait()

    return _Hosted([src], [jax.ShapeDtypeStruct((nblk,) + src.shape[1:], src.dtype)], nblk, start, finish)


def _blocking(comm, name):
    n_i, n_o = len(comm.arrays), len(comm.out_shapes)

    def body(*refs):
        ins, outs = refs[:n_i], refs[n_i:n_i + n_o]
        comm.start(ins, outs, refs[-2], refs[-1], 0)
        comm.finish(ins, outs, refs[-2], refs[-1], 0)

    return pl.pallas_call(
        body, name=name, out_shape=comm.out_shapes, in_specs=[ANY] * n_i, out_specs=[ANY] * n_o,
        scratch_shapes=[pltpu.SemaphoreType.DMA((comm.nsem,)), pltpu.SemaphoreType.DMA((comm.nsem,))],
        input_output_aliases=comm.aliases)(*comm.arrays)


def _swap_chips(send):
    def copies(ins, outs, ss, rs, base):
        x, y, c = lax.axis_index("x"), lax.axis_index("y"), lax.axis_index("c")
        chips = [(1 - x, y), (x, 1 - y), (1 - x, 1 - y)]
        return [pltpu.make_async_remote_copy(
            src_ref=ins[0].at[j], dst_ref=outs[0].at[j], send_sem=ss.at[base + j], recv_sem=rs.at[base + j],
            device_id=(*chip, c), device_id_type=MESH) for j, chip in enumerate(chips)]

    def start(ins, outs, ss, rs, base):
        for cp in copies(ins, outs, ss, rs, base):
            cp.start()

    def finish(ins, outs, ss, rs, base):
        for cp in copies(ins, outs, ss, rs, base):
            cp.wait()

    return _Hosted([send], [jax.ShapeDtypeStruct(send.shape, send.dtype)], 3, start, finish)


def _add_send(a, b, idx, ns, name):
    _, r, c = a.shape
    tr = _tile(r, 256)

    def body(idx_ref, a_ref, b_ref, send_ref):
        send_ref[...] = (a_ref[...] + b_ref[...]).astype(BF16)

    def sel(off):
        return pl.BlockSpec((None, tr, c), lambda k, i, s: (s[off + k], i, 0))

    gs = pltpu.PrefetchScalarGridSpec(num_scalar_prefetch=1, grid=(ns, r // tr), in_specs=[sel(0), sel(ns)],
                                      out_specs=pl.BlockSpec((None, tr, c), lambda k, i, s: (k, i, 0)))
    return _pc(body, name=name, grid_spec=gs, sem=("arbitrary", "arbitrary"),
               out_shape=jax.ShapeDtypeStruct((ns, r, c), BF16))(idx, a, b)


class _ReduceScatter:
    def __init__(self, g, tag):
        self.g, self.tag = g, tag

    def swap_core(self):
        return _swap(self.g, 4, lambda x, y, c: [1 - c, 3 - c, 5 - c, 7 - c], lambda x, y, c: (x, y, 1 - c))

    def after_core(self, recv):
        x, y, c = lax.axis_index("x"), lax.axis_index("y"), lax.axis_index("c")
        chips = [(1 - x, y), (x, 1 - y), (1 - x, 1 - y)]
        idx = jnp.stack([4 * p + 2 * q + c for p, q in chips] + [2 * p + q for p, q in chips]).astype(jnp.int32)
        self.send = _add_send(self.g, recv, idx, 3, "rs_add_" + self.tag)
        self.mine = [lax.dynamic_index_in_dim(self.g, 4 * x + 2 * y + c, 0, keepdims=False),
                     lax.dynamic_index_in_dim(recv, 2 * x + y, 0, keepdims=False)]

    def swap_chips(self):
        return _swap_chips(self.send)

    def after_chips(self, recv):
        self.parts = self.mine + [recv[0], recv[1], recv[2]]


def _ada_fwd(c_all, w_ada, b_cols, b_lb):
    nl, d, ncol = w_ada.shape
    nseq = c_all.shape[0]
    di = b_lb.shape[1]

    def body(c_ref, w_ref, b_ref, lb_ref, mod_ref, lbj_ref):
        cv = c_ref[...]
        cact = (cv * _sigmoid(cv)).astype(BF16)
        for l in range(nl):
            mod_ref[l] = _dot(cact, w_ref[l].astype(BF16)) + b_ref[l]
        b0, b1 = lb_ref[0:1, :], lb_ref[1:2, :]
        mx = jnp.maximum(b0, b1)
        e0, e1 = jnp.exp(b0 - mx), jnp.exp(b1 - mx)
        s = e0 + e1
        p0, p1 = e0 / s, e1 / s
        lbj_ref[0:1, :] = (p0 + p1) - p0
        lbj_ref[1:2, :] = p0 * p1

    return _pc(body, name="ada_fwd",
               out_shape=[jax.ShapeDtypeStruct((nl, nseq, ncol), F32), jax.ShapeDtypeStruct((2, di), F32)]
               )(c_all, w_ada, b_cols, b_lb)


def _ada_bwd(c_all, dmod_cols, dmod_full):
    nl, nseq, ncol = dmod_cols.shape
    d = c_all.shape[1]
    d3 = dmod_full.shape[2]

    def body(c_ref, dc_ref, df_ref, gw_ref, gb_ref):
        cv = c_ref[...]
        cact = (cv * _sigmoid(cv)).astype(BF16)
        for l in range(nl):
            gw_ref[l] = _dot_tn(cact, dc_ref[l].astype(BF16))
            gb_ref[l:l + 1, :] = jnp.sum(df_ref[l], axis=0, keepdims=True)

    return _pc(body, name="ada_bwd",
               out_shape=[jax.ShapeDtypeStruct((nl, d, ncol), F32), jax.ShapeDtypeStruct((nl, d3), F32)]
               )(c_all, dmod_cols, dmod_full)


def _prenorm(x, gain, mod, t_seq, name):
    m, d = x.shape
    tm = _tile(t_seq, 512)
    per = t_seq // tm

    def body(x_ref, g_ref, mod_ref, h_ref, ht_ref):
        xv = x_ref[...]
        rstd = lax.rsqrt(jnp.mean(xv * xv, axis=-1, keepdims=True) + EPS)
        r = xv * rstd * g_ref[...]
        h = r * (1.0 + mod_ref[0, 1:2, :]) + mod_ref[0, 0:1, :]
        h_ref[...] = h.astype(BF16)
        ht_ref[...] = h.T.astype(BF16)

    return _pc(body, name=name, out_shape=[jax.ShapeDtypeStruct((m, d), BF16), jax.ShapeDtypeStruct((d, m), BF16)],
               grid=(m // tm,),
               in_specs=[pl.BlockSpec((tm, d), lambda i: (i, 0)), pl.BlockSpec((1, d), lambda i: (0, 0)),
                         pl.BlockSpec((1, 3, d), lambda i: (i // per, 0, 0))],
               out_specs=[pl.BlockSpec((tm, d), lambda i: (i, 0)), pl.BlockSpec((d, tm), lambda i: (0, i))],
               sem=("parallel",))(x, gain, mod)


def _prenorm_bwd(dh, x, gain, mod, dxn, t_seq, name, comm=None):
    m, d = x.shape
    nb = m // t_seq
    tm = _tile(t_seq, 512)
    per = t_seq // tm

    def body(dh_ref, x_ref, g_ref, mod_ref, dxn_ref, dx_ref, dss_ref, dg_ref):
        i = pl.program_id(0)
        xv, dhv, g = x_ref[...], dh_ref[...], g_ref[...]
        rstd = lax.rsqrt(jnp.mean(xv * xv, axis=-1, keepdims=True) + EPS)
        xhat = xv * rstd
        dr = dhv * (1.0 + mod_ref[0, 1:2, :])
        dxhat = dr * g
        dx_ref[...] = dxn_ref[...] + rstd * (dxhat - xhat * jnp.mean(dxhat * xhat, axis=-1, keepdims=True))

        @pl.when(i % per == 0)
        def _():
            dss_ref[...] = jnp.zeros_like(dss_ref)

        @pl.when(i == 0)
        def _():
            dg_ref[...] = jnp.zeros_like(dg_ref)

        dss_ref[0, 0:1, :] += jnp.sum(dhv, axis=0, keepdims=True)
        dss_ref[0, 1:2, :] += jnp.sum(dhv * (xhat * g), axis=0, keepdims=True)
        dg_ref[...] += jnp.sum(dr * xhat, axis=0, keepdims=True)

    row = pl.BlockSpec((tm, d), lambda i: (i, 0))
    return _pc(body, name=name,
               out_shape=[jax.ShapeDtypeStruct((m, d), F32), jax.ShapeDtypeStruct((nb, 2, d), F32),
                          jax.ShapeDtypeStruct((1, d), F32)],
               grid=(m // tm,),
               in_specs=[row, row, pl.BlockSpec((1, d), lambda i: (0, 0)),
                         pl.BlockSpec((1, 3, d), lambda i: (i // per, 0, 0)), row],
               out_specs=[row, pl.BlockSpec((1, 2, d), lambda i: (i // per, 0, 0)),
                          pl.BlockSpec((1, d), lambda i: (0, 0))],
               sem=("arbitrary",), comm=comm)(dh, x, gain, mod, dxn)


def _mm_in(h, w_g, sections, name, comm=None):
    m, k = h.shape
    nc = w_g.shape[2]
    per = NDEV // sections if sections > 1 else NDEV
    tm = _tile(m, 512)
    assert per % 2 == 0

    def body(h_ref, w_ref, o_ref):
        hv = h_ref[...]
        o_ref[:, :nc] = _dot(hv, w_ref[0])
        o_ref[:, nc:] = _dot(hv, w_ref[1])

    if sections > 1:
        out_shape = jax.ShapeDtypeStruct((sections, m, per * nc), F32)
        out_spec = pl.BlockSpec((None, tm, 2 * nc), lambda j, i: ((2 * j) // per, i, ((2 * j) % per) // 2))
    else:
        out_shape = jax.ShapeDtypeStruct((m, NDEV * nc), F32)
        out_spec = pl.BlockSpec((tm, 2 * nc), lambda j, i: (i, j))
    return _pc(body, name=name, out_shape=out_shape, grid=(NDEV // 2, m // tm),
               in_specs=[pl.BlockSpec((tm, k), lambda j, i: (i, 0)),
                         pl.BlockSpec((2, k, nc), lambda j, i: (j, 0, 0))],
               out_specs=out_spec, sem=("parallel", "parallel"), comm=comm)(h, w_g)


def _din_tile(m):
    return 1024 if m % 1024 == 0 and m >= 2048 else _tile(m, 512)


def _mm_din(dproj, w_g, sections, name, comm=None, tiles=None, prev=None):
    k, nc = w_g.shape[1], w_g.shape[2]
    m = dproj.shape[-2]
    tm = _din_tile(m)
    t0, nt = tiles if tiles is not None else (0, m // tm)
    per = NDEV // sections if sections > 1 else NDEV

    def body(*refs):
        d_ref, w_ref, o_ref = refs[0], refs[1], refs[-1]
        j = pl.program_id(1)
        acc = _dot_nt(d_ref[...], w_ref[...])

        @pl.when(j == 0)
        def _():
            o_ref[...] = acc

        @pl.when(j > 0)
        def _():
            o_ref[...] += acc

    if sections > 1:
        dspec = pl.BlockSpec((None, tm, nc), lambda i, j: (j // per, i + t0, j % per))
    else:
        dspec = pl.BlockSpec((tm, nc), lambda i, j: (i + t0, j))
    in_specs = [dspec, pl.BlockSpec((None, k, nc), lambda i, j: (j, 0, 0))]
    args = [dproj, w_g]
    if prev is not None:
        in_specs.append(ANY)
        args.append(prev)
    return _pc(body, name=name, out_shape=jax.ShapeDtypeStruct((m, k), F32), grid=(nt, NDEV), in_specs=in_specs,
               out_specs=pl.BlockSpec((tm, k), lambda i, j: (i + t0, 0)), sem=("parallel", "arbitrary"),
               comm=comm, aliases={2: 0} if prev is not None else None)(*args)


def _mm_dw_in(ht, dproj, nc, sections, name, comm=None):
    k, m = ht.shape
    tk = _din_tile(m)
    per = NDEV // sections if sections > 1 else NDEV

    def body(h_ref, d_ref, o_ref):
        kk = pl.program_id(1)
        acc = _dot(h_ref[...], d_ref[...])

        @pl.when(kk == 0)
        def _():
            o_ref[...] = acc

        @pl.when(kk > 0)
        def _():
            o_ref[...] += acc

    if sections > 1:
        dspec = pl.BlockSpec((None, tk, nc), lambda j, i: (j // per, i, j % per))
    else:
        dspec = pl.BlockSpec((tk, nc), lambda j, i: (i, j))
    return _pc(body, name=name, out_shape=jax.ShapeDtypeStruct((NDEV, k, nc), F32), grid=(NDEV, m // tk),
               in_specs=[pl.BlockSpec((k, tk), lambda j, i: (0, i)), dspec],
               out_specs=pl.BlockSpec((None, k, nc), lambda j, i: (j, 0, 0)),
               sem=("parallel", "arbitrary"), comm=comm)(ht, dproj)


def _out_proj(ybr, w_out, x, mod, t_seq, name, comm=None):
    m, di = ybr.shape
    d = w_out.shape[1]
    tm = _tile(t_seq, 512)
    per = t_seq // tm

    def body(y_ref, w_ref, x_ref, mod_ref, yo_ref, xn_ref):
        yo = _dot(y_ref[...], w_ref[...])
        yo_ref[...] = yo
        xn_ref[...] = x_ref[...] + mod_ref[0, 2:3, :] * yo

    row = pl.BlockSpec((tm, d), lambda i: (i, 0))
    return _pc(body, name=name,
               out_shape=[jax.ShapeDtypeStruct((m, d), F32), jax.ShapeDtypeStruct((m, d), F32)],
               grid=(m // tm,),
               in_specs=[pl.BlockSpec((tm, di), lambda i: (i, 0)), pl.BlockSpec((di, d), lambda i: (0, 0)), row,
                         pl.BlockSpec((1, 3, d), lambda i: (i // per, 0, 0))],
               out_specs=[row, row], sem=("parallel",), comm=comm)(ybr, w_out, x, mod)


def _gate_bwd(dxn, yout, mod, t_seq, name):
    m, d = dxn.shape
    nb = m // t_seq
    tm = _tile(t_seq, 512)
    per = t_seq // tm

    def body(dxn_ref, yo_ref, mod_ref, dy_ref, dgate_ref):
        i = pl.program_id(0)
        dv = dxn_ref[...]
        dy_ref[...] = (mod_ref[0, 2:3, :] * dv).astype(BF16)

        @pl.when(i % per == 0)
        def _():
            dgate_ref[...] = jnp.zeros_like(dgate_ref)

        dgate_ref[0] += jnp.sum(dv * yo_ref[...], axis=0, keepdims=True)

    row = pl.BlockSpec((tm, d), lambda i: (i, 0))
    return _pc(body, name=name,
               out_shape=[jax.ShapeDtypeStruct((m, d), BF16), jax.ShapeDtypeStruct((nb, 1, d), F32)],
               grid=(m // tm,),
               in_specs=[row, row, pl.BlockSpec((1, 3, d), lambda i: (i // per, 0, 0))],
               out_specs=[row, pl.BlockSpec((1, 1, d), lambda i: (i // per, 0, 0))],
               sem=("arbitrary",))(dxn, yout, mod)


def _mm_dybr(dy, w_out, name, comm=None):
    m, d = dy.shape
    di = w_out.shape[0]
    tm = _tile(m, 512)

    def body(dy_ref, w_ref, o_ref):
        o_ref[...] = _dot_nt(dy_ref[...], w_ref[...])

    return _pc(body, name=name, out_shape=jax.ShapeDtypeStruct((m, di), F32), grid=(m // tm,),
               in_specs=[pl.BlockSpec((tm, d), lambda i: (i, 0)), pl.BlockSpec((di, d), lambda i: (0, 0))],
               out_specs=pl.BlockSpec((tm, di), lambda i: (i, 0)), sem=("parallel",), comm=comm)(dy, w_out)


def _mm_dw_out(ybr, dy, name, comm=None):
    m, di = ybr.shape
    d = dy.shape[1]
    tk = _tile(m, 512)
    tn = _tile(di, 1024)

    def body(y_ref, dy_ref, o_ref):
        kk = pl.program_id(1)
        acc = _dot_tn(y_ref[...], dy_ref[...])

        @pl.when(kk == 0)
        def _():
            o_ref[...] = acc

        @pl.when(kk > 0)
        def _():
            o_ref[...] += acc

    return _pc(body, name=name, out_shape=jax.ShapeDtypeStruct((di, d), F32), grid=(di // tn, m // tk),
               in_specs=[pl.BlockSpec((tk, tn), lambda n, k: (k, n)), pl.BlockSpec((tk, d), lambda n, k: (k, 0))],
               out_specs=pl.BlockSpec((tn, d), lambda n, k: (n, 0)), sem=("parallel", "arbitrary"),
               comm=comm)(ybr, dy)


def _sgu_mask():
    t = lax.broadcasted_iota(jnp.int32, (SG_BLOCK, SG_BLOCK), 0)
    s = lax.broadcasted_iota(jnp.int32, (SG_BLOCK, SG_BLOCK), 1)
    return (s // CHUNK) <= (t // CHUNK)


def _a_mid_fwd(proj, ln_g, ln_b, w_s, bs_t, t_seq, comm=None):
    m, n3 = proj.shape
    di = n3 // 3
    gd = di // SG_GROUPS
    r = _tile(t_seq, 256)
    nblk = r // SG_BLOCK

    def body(p_ref, lg_ref, lb_ref, ws_ref, bs_ref, ybr_ref, s_scr):
        v = _gelu(p_ref[:, di:2 * di])
        mu = jnp.mean(v, axis=-1, keepdims=True)
        vc = v - mu
        rstd = lax.rsqrt(jnp.mean(vc * vc, axis=-1, keepdims=True) + EPS)
        vb = (vc * rstd * lg_ref[...] + lb_ref[...]).astype(BF16)
        mask = _sgu_mask()
        for gi in range(SG_GROUPS):
            ws = jnp.where(mask, ws_ref[gi], 0.0).astype(BF16)
            bcol = bs_ref[:, gi:gi + 1]
            for b in range(nblk):
                rows = slice(b * SG_BLOCK, (b + 1) * SG_BLOCK)
                cols = slice(gi * gd, (gi + 1) * gd)
                s_scr[rows, cols] = _dot(ws, vb[rows, cols]) + bcol
        gg = p_ref[:, 2 * di:]
        ybr_ref[...] = (_gelu(p_ref[:, :di]) * s_scr[...] * (gg * _sigmoid(gg))).astype(BF16)

    vec = pl.BlockSpec((1, di), lambda i: (0, 0))
    return _pc(body, name="a_mid_fwd", out_shape=jax.ShapeDtypeStruct((m, di), BF16), grid=(m // r,),
               in_specs=[pl.BlockSpec((r, n3), lambda i: (i, 0)), vec, vec,
                         pl.BlockSpec((SG_GROUPS, SG_BLOCK, SG_BLOCK), lambda i: (0, 0, 0)),
                         pl.BlockSpec((SG_BLOCK, 128), lambda i: (0, 0))],
               out_specs=pl.BlockSpec((r, di), lambda i: (i, 0)),
               scratch=[pltpu.VMEM((r, di), F32)], sem=("parallel",), comm=comm)(proj, ln_g, ln_b, w_s, bs_t)


def _a_mid_bwd(proj, dybr, ln_g, ln_b, w_s, bs_t, t_seq, comm=None):
    m, n3 = proj.shape
    di = n3 // 3
    gd = di // SG_GROUPS
    r = _tile(t_seq, 256)
    nblk = r // SG_BLOCK

    def body(p_ref, dy_ref, lg_ref, lb_ref, ws_ref, bs_ref,
             dp_ref, dlg_ref, dlb_ref, dws_ref, dbs_ref, s_scr, dvl_scr):
        i = pl.program_id(0)

        @pl.when(i == 0)
        def _():
            dlg_ref[...] = jnp.zeros_like(dlg_ref)
            dlb_ref[...] = jnp.zeros_like(dlb_ref)
            dws_ref[...] = jnp.zeros_like(dws_ref)
            dbs_ref[...] = jnp.zeros_like(dbs_ref)

        v_pre = p_ref[:, di:2 * di]
        v = _gelu(v_pre)
        mu = jnp.mean(v, axis=-1, keepdims=True)
        vc = v - mu
        rstd = lax.rsqrt(jnp.mean(vc * vc, axis=-1, keepdims=True) + EPS)
        vhat = vc * rstd
        lg = lg_ref[...]
        vb = (vhat * lg + lb_ref[...]).astype(BF16)
        u_pre = p_ref[:, :di]
        u = _gelu(u_pre)
        gg = p_ref[:, 2 * di:]
        sg = _sigmoid(gg)
        dyv = dy_ref[...]
        dus = dyv * (gg * sg)
        dsb = (dus * u).astype(BF16)
        ds32 = dus * u
        mask = _sgu_mask()
        lane = lax.broadcasted_iota(jnp.int32, (SG_BLOCK, 128), 1)
        dbs_acc = jnp.zeros((SG_BLOCK, 128), F32)
        for gi in range(SG_GROUPS):
            ws = jnp.where(mask, ws_ref[gi], 0.0).astype(BF16)
            bcol = bs_ref[:, gi:gi + 1]
            cols = slice(gi * gd, (gi + 1) * gd)
            dws_acc = jnp.zeros((SG_BLOCK, SG_BLOCK), F32)
            dbs_col = jnp.zeros((SG_BLOCK, 1), F32)
            for b in range(nblk):
                rows = slice(b * SG_BLOCK, (b + 1) * SG_BLOCK)
                s_scr[rows, cols] = _dot(ws, vb[rows, cols]) + bcol
                dvl_scr[rows, cols] = _dot_tn(ws, dsb[rows, cols])
                dws_acc += _dot_nt(dsb[rows, cols], vb[rows, cols])
                dbs_col += jnp.sum(ds32[rows, cols], axis=-1, keepdims=True)
            dws_ref[gi] += jnp.where(mask, dws_acc, 0.0)
            dbs_acc += jnp.where(lane == gi, dbs_col, 0.0)
        dbs_ref[...] += dbs_acc
        s = s_scr[...]
        dp_ref[:, :di] = (dyv * s * (gg * sg) * _dgelu(u_pre)).astype(BF16)
        dp_ref[:, 2 * di:] = (dyv * u * s * (sg * (1.0 + gg * (1.0 - sg)))).astype(BF16)
        dvl = dvl_scr[...]
        dlg_ref[...] += jnp.sum(dvl * vhat, axis=0, keepdims=True)
        dlb_ref[...] += jnp.sum(dvl, axis=0, keepdims=True)
        dvh = dvl * lg
        dv = rstd * (dvh - jnp.mean(dvh, axis=-1, keepdims=True)
                     - vhat * jnp.mean(dvh * vhat, axis=-1, keepdims=True))
        dp_ref[:, di:2 * di] = (dv * _dgelu(v_pre)).astype(BF16)

    vec = pl.BlockSpec((1, di), lambda i: (0, 0))
    wsb = pl.BlockSpec((SG_GROUPS, SG_BLOCK, SG_BLOCK), lambda i: (0, 0, 0))
    bsb = pl.BlockSpec((SG_BLOCK, 128), lambda i: (0, 0))
    return _pc(body, name="a_mid_bwd",
               out_shape=[jax.ShapeDtypeStruct((m, n3), BF16), jax.ShapeDtypeStruct((1, di), F32),
                          jax.ShapeDtypeStruct((1, di), F32),
                          jax.ShapeDtypeStruct((SG_GROUPS, SG_BLOCK, SG_BLOCK), F32),
                          jax.ShapeDtypeStruct((SG_BLOCK, 128), F32)],
               grid=(m // r,),
               in_specs=[pl.BlockSpec((r, n3), lambda i: (i, 0)), pl.BlockSpec((r, di), lambda i: (i, 0)),
                         vec, vec, wsb, bsb],
               out_specs=[pl.BlockSpec((r, n3), lambda i: (i, 0)), vec, vec, wsb, bsb],
               scratch=[pltpu.VMEM((r, di), F32), pltpu.VMEM((r, di), F32)],
               sem=("arbitrary",), comm=comm)(proj, dybr, ln_g, ln_b, w_s, bs_t)


def _hgrn_dims(t_seq, di):
    tr = _tile(t_seq, 256)
    hc = _tile(di, 1024)
    return tr, hc, hc // HEAD_DIM


def _hgrn_gates(f_ref, lb, a_scr, k_scr, tr):
    sig = _sigmoid(f_ref[...])
    fg = lb + (1.0 - lb) * sig
    k_scr[...] = 1.0 - fg
    logf = jnp.log(fg)
    g = min(CUM_ROWS, tr)
    tri = _tri_mask(g, reverse=False)
    for rg in range(tr // g):
        a_scr[rg * g:(rg + 1) * g, :] = _tri_apply(tri, logf[rg * g:(rg + 1) * g, :])
    return sig, fg


def _hgrn_fwd(proj, lbj, gn, nb, t_seq):
    _, m, di = proj.shape
    tr, hc, hpg = _hgrn_dims(t_seq, di)
    nt, nhg, ncl = t_seq // tr, di // hc, tr // CHUNK
    nheads = di // HEAD_DIM

    def body(q_ref, f_ref, i_ref, g_ref, lb_ref, gn_ref, o_ref, ybr_ref, st_ref, st_scr, a_scr, k_scr):
        t = pl.program_id(2)

        @pl.when(t == 0)
        def _():
            st_scr[...] = jnp.zeros_like(st_scr)

        _hgrn_gates(f_ref, lb_ref[0:1, :], a_scr, k_scr, tr)
        gnv = gn_ref[...]
        rr = lax.broadcasted_iota(jnp.int32, (CHUNK, CHUNK), 0)
        cc = lax.broadcasted_iota(jnp.int32, (CHUNK, CHUNK), 1)
        causal = cc <= rr

        def chunk(n, carry):
            rows = pl.ds(pl.multiple_of(n * CHUNK, CHUNK), CHUNK)
            lanes = [slice(hd * HEAD_DIM, (hd + 1) * HEAD_DIM) for hd in range(hpg)]
            hs = []
            for hd, ls in enumerate(lanes):
                h = {}
                ah, kh = a_scr[rows, ls], k_scr[rows, ls]
                qp = q_ref[rows, ls]
                qh = qp * _sigmoid(qp)
                h["vb"] = i_ref[rows, ls].astype(BF16)
                aref, alast = ah[CHUNK // 2 - 1:CHUNK // 2, :], ah[CHUNK - 1:CHUNK, :]
                h["q_in"] = (qh * jnp.exp(ah - aref)).astype(BF16)
                h["k_in"] = (kh * jnp.exp(aref - ah)).astype(BF16)
                h["q_out"] = (qh * jnp.exp(ah)).astype(BF16)
                h["k_out"] = (kh * jnp.exp(alast - ah)).astype(BF16)
                h["dec"] = jnp.exp(alast)
                st = st_scr[hd]
                st_ref[n, hd] = st
                h["st"] = st
                hs.append(h)
            for h in hs:
                h["scores"] = _dot_nt(h["q_in"], h["k_in"])
                h["o_inter"] = _dot_nt(h["q_out"], h["st"].astype(BF16))
                h["st_mm"] = _dot_tn(h["vb"], h["k_out"])
            for h in hs:
                h["o"] = _dot(jnp.where(causal, h["scores"], 0.0).astype(BF16), h["vb"]) + h["o_inter"]
            for hd, (h, ls) in enumerate(zip(hs, lanes)):
                st_scr[hd] = h["st"] * h["dec"] + h["st_mm"]
                o = h["o"]
                o_ref[rows, ls] = o
                rstd = lax.rsqrt(jnp.mean(o * o, axis=-1, keepdims=True) + EPS)
                gg = g_ref[rows, ls]
                ybr_ref[rows, ls] = ((o * rstd * gnv) * (gg * _sigmoid(gg))).astype(BF16)
            return carry

        lax.fori_loop(0, ncl, chunk, 0)

    def sec(s):
        return pl.BlockSpec((None, tr, hc), lambda hg, b, t: (s, b * nt + t, hg))

    blk = pl.BlockSpec((tr, hc), lambda hg, b, t: (b * nt + t, hg))
    return _pc(body, name="hgrn_fwd",
               out_shape=[jax.ShapeDtypeStruct((m, di), F32), jax.ShapeDtypeStruct((m, di), BF16),
                          jax.ShapeDtypeStruct((m // CHUNK, nheads, HEAD_DIM, HEAD_DIM), F32)],
               grid=(nhg, nb, nt),
               in_specs=[sec(0), sec(1), sec(2), sec(3), pl.BlockSpec((2, hc), lambda hg, b, t: (0, hg)),
                         pl.BlockSpec((1, HEAD_DIM), lambda hg, b, t: (0, 0))],
               out_specs=[blk, blk, pl.BlockSpec((ncl, hpg, HEAD_DIM, HEAD_DIM),
                                                 lambda hg, b, t: (b * nt + t, hg, 0, 0))],
               scratch=[pltpu.VMEM((hpg, HEAD_DIM, HEAD_DIM), F32), pltpu.VMEM((tr, hc), F32),
                        pltpu.VMEM((tr, hc), F32)],
               sem=("parallel", "arbitrary", "arbitrary"))(proj, proj, proj, proj, lbj, gn)


def _hgrn_bwd(proj, o_all, dybr, states, lbj, gn, nb, t_seq, comm=None):
    _, m, di = proj.shape
    tr, hc, hpg = _hgrn_dims(t_seq, di)
    nt, nhg, ncl = t_seq // tr, di // hc, tr // CHUNK

    def body(q_ref, f_ref, i_ref, g_ref, o_ref, dy_ref, st_ref, lb_ref, gn_ref,
             dp_ref, dlb_ref, dgn_ref, dst_scr, a_scr, k_scr, da_scr, dk_scr):
        hg, b, t = pl.program_id(0), pl.program_id(1), pl.program_id(2)

        @pl.when(t == 0)
        def _():
            dst_scr[...] = jnp.zeros_like(dst_scr)

        @pl.when((b == 0) & (t == 0))
        def _():
            dlb_ref[...] = jnp.zeros_like(dlb_ref)

        @pl.when((hg == 0) & (b == 0) & (t == 0))
        def _():
            dgn_ref[...] = jnp.zeros_like(dgn_ref)

        lb = lb_ref[0:1, :]
        sig, fg = _hgrn_gates(f_ref, lb, a_scr, k_scr, tr)
        gnv = gn_ref[...]
        rr = lax.broadcasted_iota(jnp.int32, (CHUNK, CHUNK), 0)
        cc = lax.broadcasted_iota(jnp.int32, (CHUNK, CHUNK), 1)
        causal = cc <= rr
        rowi = lax.broadcasted_iota(jnp.int32, (CHUNK, HEAD_DIM), 0)

        def chunk(it, carry):
            n = ncl - 1 - it
            rows = pl.ds(pl.multiple_of(n * CHUNK, CHUNK), CHUNK)
            lanes = [slice(hd * HEAD_DIM, (hd + 1) * HEAD_DIM) for hd in range(hpg)]
            hs = []
            for hd, ls in enumerate(lanes):
                h = {}
                ah, kh = a_scr[rows, ls], k_scr[rows, ls]
                qp = q_ref[rows, ls]
                sq = _sigmoid(qp)
                qh = qp * sq
                h["dsilu_q"] = sq * (1.0 + qp * (1.0 - sq))
                h["vb"] = i_ref[rows, ls].astype(BF16)
                aref, alast = ah[CHUNK // 2 - 1:CHUNK // 2, :], ah[CHUNK - 1:CHUNK, :]
                h["e1"], h["e2"] = jnp.exp(ah - aref), jnp.exp(aref - ah)
                h["e3"], h["e4"] = jnp.exp(ah), jnp.exp(alast - ah)
                h["dec"] = jnp.exp(alast)
                h["q_in"], h["k_in"], h["q_out"], h["k_out"] = qh * h["e1"], kh * h["e2"], qh * h["e3"], kh * h["e4"]
                for nm in ("q_in", "k_in", "q_out", "k_out"):
                    h[nm + "_b"] = h[nm].astype(BF16)
                o = o_ref[rows, ls]
                rstd = lax.rsqrt(jnp.mean(o * o, axis=-1, keepdims=True) + EPS)
                ohat = o * rstd
                gg = g_ref[rows, ls]
                sg = _sigmoid(gg)
                dyv = dy_ref[rows, ls]
                d_on = dyv * (gg * sg)
                dp_ref[3, rows, ls] = (dyv * (ohat * gnv) * (sg * (1.0 + gg * (1.0 - sg)))).astype(BF16)
                h["dgn"] = jnp.sum(d_on * ohat, axis=0, keepdims=True)
                dohat = d_on * gnv
                do = rstd * (dohat - ohat * jnp.mean(dohat * ohat, axis=-1, keepdims=True))
                h["do_b"] = do.astype(BF16)
                h["st_prev"] = st_ref[n, hd]
                h["dst"] = dst_scr[hd]
                hs.append(h)
            for h in hs:
                dst_b = h["dst"].astype(BF16)
                h["scores"] = _dot_nt(h["q_in_b"], h["k_in_b"])
                h["dscores"] = _dot_nt(h["do_b"], h["vb"])
                h["dv_inter"] = _dot_nt(h["k_out_b"], dst_b)
                h["dq_out"] = _dot(h["do_b"], h["st_prev"].astype(BF16))
                h["dk_out"] = _dot(h["vb"], dst_b)
                h["dst_mm"] = _dot_tn(h["do_b"], h["q_out_b"])
            for h in hs:
                scores = jnp.where(causal, h["scores"], 0.0).astype(BF16)
                dscores = jnp.where(causal, h["dscores"], 0.0).astype(BF16)
                h["dv"] = _dot_tn(scores, h["do_b"]) + h["dv_inter"]
                h["dq_in"] = _dot(dscores, h["k_in_b"])
                h["dk_in"] = _dot_tn(dscores, h["q_in_b"])
            dgn = hs[0]["dgn"]
            for h in hs[1:]:
                dgn = dgn + h["dgn"]
            dgn_ref[...] += dgn
            for hd, (h, ls) in enumerate(zip(hs, lanes)):
                ddec = jnp.sum(h["dst"] * h["st_prev"], axis=0, keepdims=True)
                dst_scr[hd] = h["dst"] * h["dec"] + h["dst_mm"]
                dp_ref[2, rows, ls] = h["dv"].astype(BF16)
                dq = h["dq_in"] * h["e1"] + h["dq_out"] * h["e3"]
                dp_ref[0, rows, ls] = (dq * h["dsilu_q"]).astype(BF16)
                dk_scr[rows, ls] = h["dk_in"] * h["e2"] + h["dk_out"] * h["e4"]
                t_in = h["dq_in"] * h["q_in"] - h["dk_in"] * h["k_in"]
                t_out = h["dk_out"] * h["k_out"]
                da = t_in + h["dq_out"] * h["q_out"] - t_out
                da_ref_row = -jnp.sum(t_in, axis=0, keepdims=True)
                da_last_row = jnp.sum(t_out, axis=0, keepdims=True) + ddec * h["dec"]
                da = da + jnp.where(rowi == CHUNK // 2 - 1, da_ref_row, 0.0) \
                        + jnp.where(rowi == CHUNK - 1, da_last_row, 0.0)
                da_scr[rows, ls] = da
            return carry

        lax.fori_loop(0, ncl, chunk, 0)
        g = min(CUM_ROWS, tr)
        tri = _tri_mask(g, reverse=True)
        for rg in range(tr // g):
            rs = slice(rg * g, (rg + 1) * g)
            dlogf = _tri_apply(tri, da_scr[rs, :])
            df = dlogf / fg[rs, :] - dk_scr[rs, :]
            sgr = sig[rs, :]
            dp_ref[1, rs, :] = (df * (1.0 - lb) * (sgr * (1.0 - sgr))).astype(BF16)
            dlb_ref[...] += jnp.sum(df * (1.0 - sgr), axis=0, keepdims=True) * lb_ref[1:2, :]

    def sec(s):
        return pl.BlockSpec((None, tr, hc), lambda hg, b, t: (s, b * nt + (nt - 1 - t), hg))

    blk = pl.BlockSpec((tr, hc), lambda hg, b, t: (b * nt + (nt - 1 - t), hg))
    return _pc(body, name="hgrn_bwd",
               out_shape=[jax.ShapeDtypeStruct((4, m, di), BF16), jax.ShapeDtypeStruct((1, di), F32),
                          jax.ShapeDtypeStruct((1, HEAD_DIM), F32)],
               grid=(nhg, nb, nt),
               in_specs=[sec(0), sec(1), sec(2), sec(3), blk, blk,
                         pl.BlockSpec((ncl, hpg, HEAD_DIM, HEAD_DIM),
                                      lambda hg, b, t: (b * nt + (nt - 1 - t), hg, 0, 0)),
                         pl.BlockSpec((2, hc), lambda hg, b, t: (0, hg)),
                         pl.BlockSpec((1, HEAD_DIM), lambda hg, b, t: (0, 0))],
               out_specs=[pl.BlockSpec((4, tr, hc), lambda hg, b, t: (0, b * nt + (nt - 1 - t), hg)),
                          pl.BlockSpec((1, hc), lambda hg, b, t: (0, hg)),
                          pl.BlockSpec((1, HEAD_DIM), lambda hg, b, t: (0, 0))],
               scratch=[pltpu.VMEM((hpg, HEAD_DIM, HEAD_DIM), F32)] + [pltpu.VMEM((tr, hc), F32)] * 4,
               sem=("arbitrary", "arbitrary", "arbitrary"), comm=comm)(
                   proj, proj, proj, proj, o_all, dybr, states, lbj, gn)


def _final_loss(x, gain, target):
    m, d = x.shape
    tm = _tile(m, 512)

    def body(x_ref, g_ref, t_ref, dx_ref, loss_ref, dg_ref):
        i = pl.program_id(0)
        xv, g = x_ref[...], g_ref[...]
        rstd = lax.rsqrt(jnp.mean(xv * xv, axis=-1, keepdims=True) + EPS)
        xhat = xv * rstd
        err = xhat * g - t_ref[...]
        dy = err * (1.0 / d)
        dxhat = dy * g
        dx_ref[...] = rstd * (dxhat - xhat * jnp.mean(dxhat * xhat, axis=-1, keepdims=True))

        @pl.when(i == 0)
        def _():
            loss_ref[...] = jnp.zeros_like(loss_ref)
            dg_ref[...] = jnp.zeros_like(dg_ref)

        loss_ref[...] += 0.5 * jnp.sum(jnp.mean(err * err, axis=-1, keepdims=True), axis=0, keepdims=True)
        dg_ref[...] += jnp.sum(dy * xhat, axis=0, keepdims=True)

    row = pl.BlockSpec((tm, d), lambda i: (i, 0))
    return _pc(body, name="final_loss",
               out_shape=[jax.ShapeDtypeStruct((m, d), F32), jax.ShapeDtypeStruct((1, 1), F32),
                          jax.ShapeDtypeStruct((1, d), F32)],
               grid=(m // tm,),
               in_specs=[row, pl.BlockSpec((1, d), lambda i: (0, 0)), row],
               out_specs=[row, pl.BlockSpec((1, 1), lambda i: (0, 0)), pl.BlockSpec((1, d), lambda i: (0, 0))],
               sem=("arbitrary",))(x, gain, target)


def _adamw(parts, w, m, v, name, comm=None):
    r, c = w.shape
    tr = _tile(r, 256)
    npart = len(parts)
    c1 = 1.0 - ADAM_B1 ** ADAM_STEP
    c2 = 1.0 - ADAM_B2 ** ADAM_STEP

    def body(*refs):
        p_refs = refs[:npart]
        w_ref, m_ref, v_ref, g_ref, d_ref, nm_ref, nv_ref = refs[npart:]
        g = p_refs[0][...].astype(F32)
        for p in p_refs[1:]:
            g = g + p[...].astype(F32)
        nm = ADAM_B1 * m_ref[...] + (1.0 - ADAM_B1) * g
        nv = ADAM_B2 * v_ref[...] + (1.0 - ADAM_B2) * (g * g)
        g_ref[...] = g
        nm_ref[...] = nm
        nv_ref[...] = nv
        d_ref[...] = -ADAM_LR * ((nm / c1) / (jnp.sqrt(nv / c2) + ADAM_EPS) + ADAM_WD * w_ref[...])

    blk = pl.BlockSpec((tr, c), lambda i: (i, 0))
    return _pc(body, name=name, out_shape=[jax.ShapeDtypeStruct((r, c), F32)] * 4, grid=(r // tr,),
               in_specs=[blk] * (npart + 3), out_specs=[blk] * 4, sem=("parallel",), comm=comm)(*parts, w, m, v)


_SMALL = ["norm_gain", "a_ln_gain", "a_ln_bias", "a_w_s", "a_b_s", "b_lower_bounds", "b_gn_gain", "final_gain"]


def _pack(arrs):
    flat = jnp.concatenate([a.reshape(-1) for a in arrs])
    rows = -(-flat.shape[0] // 1024) * 8
    return jnp.pad(flat, (0, rows * 128 - flat.shape[0])).reshape(rows, 128)


def _unpack(buf, like):
    flat = buf.reshape(-1)
    out, off = [], 0
    for a in like:
        out.append(flat[off:off + a.size].reshape(a.shape))
        off += a.size
    return out


def kernel(x, c, norm_gain, w_ada, b_ada, a_w_in, a_ln_gain, a_ln_bias, a_w_s, a_b_s, a_w_out, b_w_in, b_lower_bounds, b_gn_gain, b_w_out, final_gain, loss_target, m_norm_gain, m_w_ada, m_b_ada, m_a_w_in, m_a_ln_gain, m_a_ln_bias, m_a_w_s, m_a_b_s, m_a_w_out, m_b_w_in, m_b_lower_bounds, m_b_gn_gain, m_b_w_out, m_final_gain, v_norm_gain, v_w_ada, v_b_ada, v_a_w_in, v_a_ln_gain, v_a_ln_bias, v_a_w_s, v_a_b_s, v_a_w_out, v_b_w_in, v_b_lower_bounds, v_b_gn_gain, v_b_w_out, v_final_gain):
    w = dict(norm_gain=norm_gain, w_ada=w_ada, b_ada=b_ada, a_w_in=a_w_in, a_ln_gain=a_ln_gain,
             a_ln_bias=a_ln_bias, a_w_s=a_w_s, a_b_s=a_b_s, a_w_out=a_w_out, b_w_in=b_w_in,
             b_lower_bounds=b_lower_bounds, b_gn_gain=b_gn_gain, b_w_out=b_w_out, final_gain=final_gain)
    mo = dict(norm_gain=m_norm_gain, w_ada=m_w_ada, b_ada=m_b_ada, a_w_in=m_a_w_in, a_ln_gain=m_a_ln_gain,
              a_ln_bias=m_a_ln_bias, a_w_s=m_a_w_s, a_b_s=m_a_b_s, a_w_out=m_a_w_out, b_w_in=m_b_w_in,
              b_lower_bounds=m_b_lower_bounds, b_gn_gain=m_b_gn_gain, b_w_out=m_b_w_out, final_gain=m_final_gain)
    vo = dict(norm_gain=v_norm_gain, w_ada=v_w_ada, b_ada=v_b_ada, a_w_in=v_a_w_in, a_ln_gain=v_a_ln_gain,
              a_ln_bias=v_a_ln_bias, a_w_s=v_a_w_s, a_b_s=v_a_b_s, a_w_out=v_a_w_out, b_w_in=v_b_w_in,
              b_lower_bounds=v_b_lower_bounds, b_gn_gain=v_b_gn_gain, b_w_out=v_b_w_out, final_gain=v_final_gain)

    nb, t_seq, d = x.shape
    m = nb * t_seq
    ncol_ada = w_ada.shape[2]
    xi, yi, ci = lax.axis_index("x"), lax.axis_index("y"), lax.axis_index("c")
    me = 4 * xi + 2 * yi + ci

    c_g, wa_in_g = _all_gather([c, a_w_in[0].astype(BF16)], "gather_c_wa")

    c_all = c_g.reshape(NDEV * nb, d)
    b_cols = lax.dynamic_slice(b_ada, (0, me * ncol_ada), (2, ncol_ada)).reshape(2, 1, ncol_ada)
    mod_part, lbj = _ada_fwd(c_all, w_ada, b_cols, b_lower_bounds)
    mod_all = _all_gather([mod_part], "gather_mod")[0]
    mod_mine = lax.dynamic_slice_in_dim(mod_all, me * nb, nb, axis=2)
    mod_mine = mod_mine.transpose(1, 2, 0, 3).reshape(2, nb, 3, d)
    mod0, mod1 = mod_mine[0], mod_mine[1]

    di = a_w_out.shape[1] * NDEV

    xf = x.reshape(m, d)
    tgt = loss_target.reshape(m, d)
    ng0, ng1 = norm_gain[0:1], norm_gain[1:2]
    h0, h0_t = _prenorm(xf, ng0, mod0, t_seq, "prenorm_a")
    proj_a, half = _mm_in(h0, wa_in_g, 1, "in_proj_a",
                          comm=_gather_first([a_w_out[0].astype(BF16), b_w_in[0].astype(BF16)]))
    bs_t = jnp.pad(a_b_s[0].T, ((0, 0), (0, 128 - SG_GROUPS)))
    ybr_a, (wa_out_g, wb_in_g, wb_out_half) = _a_mid_fwd(
        proj_a, a_ln_gain, a_ln_bias, a_w_s[0], bs_t, t_seq,
        comm=_join(_gather_second(half), _gather_first([b_w_out[0].astype(BF16)])))
    wa_out = wa_out_g.reshape(di, d)
    (yout_a, x1), (wb_out_g,) = _out_proj(ybr_a, wa_out, xf, mod0, t_seq, "out_proj_a",
                                         comm=_gather_second([wb_out_half]))
    wb_out = wb_out_g.reshape(di, d)
    h1, h1_t = _prenorm(x1, ng1, mod1, t_seq, "prenorm_b")
    proj_b = _mm_in(h1, wb_in_g, 4, "in_proj_b")
    o_b, ybr_b, states = _hgrn_fwd(proj_b, lbj, b_gn_gain, nb, t_seq)
    yout_b, x2 = _out_proj(ybr_b, wb_out, x1, mod1, t_seq, "out_proj_b")
    dx2, loss_part, d_final_gain = _final_loss(x2, final_gain.reshape(1, d), tgt)
    loss = lax.psum(loss_part[0, 0], ("x", "y", "c"))

    rows_out = a_w_out.shape[1]
    dy_b, dgate1 = _gate_bwd(dx2, yout_b, mod1, t_seq, "gate_bwd_b")
    dybr_b = _mm_dybr(dy_b, wb_out, "dybr_b")
    rs_wb_out = _ReduceScatter(_mm_dw_out(ybr_b, dy_b, "dw_out_b").reshape(NDEV, rows_out, d), "b_w_out")
    (dproj_b, d_lb, d_gn), got = _hgrn_bwd(proj_b, o_b, dybr_b, states, lbj, b_gn_gain, nb, t_seq,
                                           comm=rs_wb_out.swap_core())
    rs_wb_out.after_core(got[0])
    dh1, got = _mm_din(dproj_b, wb_in_g, 4, "dh_b", comm=rs_wb_out.swap_chips())
    rs_wb_out.after_chips(got[0])
    dx1, dss1, dgain1 = _prenorm_bwd(dh1, x1, ng1, mod1, dx2, t_seq, "prenorm_bwd_b")
    rs_wb_in = _ReduceScatter(_mm_dw_in(h1_t, dproj_b, wb_in_g.shape[2], 4, "dw_in_b"), "b_w_in")

    dy_a, dgate0 = _gate_bwd(dx1, yout_a, mod0, t_seq, "gate_bwd_a")
    dybr_a = _mm_dybr(dy_a, wa_out, "dybr_a")
    g_wa_out, got = _mm_dw_out(ybr_a, dy_a, "dw_out_a", comm=rs_wb_in.swap_core())
    rs_wb_in.after_core(got[0])
    rs_wa_out = _ReduceScatter(g_wa_out.reshape(NDEV, rows_out, d), "a_w_out")
    (dproj_a, d_lng, d_lnb, d_ws, d_bs_t), got = _a_mid_bwd(
        proj_a, dybr_a, a_ln_gain, a_ln_bias, a_w_s[0], bs_t, t_seq,
        comm=_join(rs_wb_in.swap_chips(), rs_wa_out.swap_core()))
    rs_wb_in.after_chips(got[0])
    rs_wa_out.after_core(got[1])
    g_wa_in, got = _mm_dw_in(h0_t, dproj_a, wa_in_g.shape[2], 1, "dw_in_a", comm=rs_wa_out.swap_chips())
    rs_wa_out.after_chips(got[0])
    rs_wa_in = _ReduceScatter(g_wa_in, "a_w_in")
    n_tiles = m // _din_tile(m)
    assert n_tiles >= 2
    first_tiles = max(1, (3 * n_tiles) // 8)
    dh0, got = _mm_din(dproj_a, wa_in_g, 1, "dh_a_first", comm=rs_wa_in.swap_core(), tiles=(0, first_tiles))
    rs_wa_in.after_core(got[0])
    dh0, got = _mm_din(dproj_a, wa_in_g, 1, "dh_a_rest", comm=rs_wa_in.swap_chips(),
                       tiles=(first_tiles, n_tiles - first_tiles), prev=dh0)
    rs_wa_in.after_chips(got[0])
    dx0, dss0, dgain0 = _prenorm_bwd(dh0, xf, ng0, mod0, dx1, t_seq, "prenorm_bwd_a")
    grad_x = dx0.reshape(nb, t_seq, d)

    dmod = jnp.stack([jnp.concatenate([dss0, dgate0], axis=1), jnp.concatenate([dss1, dgate1], axis=1)])
    dmod_all = _all_gather([dmod.reshape(2, nb, 3 * d)], "gather_dmod")[0]
    dmod_all = dmod_all.transpose(1, 0, 2, 3).reshape(2, NDEV * nb, 3 * d)
    dmod_cols = lax.dynamic_slice_in_dim(dmod_all, me * ncol_ada, ncol_ada, axis=2)
    g_w_ada, g_b_ada = _ada_bwd(c_all, dmod_cols, dmod_all)

    part = dict(norm_gain=jnp.concatenate([dgain0, dgain1], axis=0), a_ln_gain=d_lng, a_ln_bias=d_lnb,
                a_w_s=d_ws[None], a_b_s=d_bs_t[:, :SG_GROUPS].T[None],
                b_lower_bounds=jnp.concatenate([-d_lb, d_lb], axis=0), b_gn_gain=d_gn, final_gain=d_final_gain[0])
    small_like = [w[k] for k in _SMALL]
    parts_all = _all_gather([_pack([part[k].reshape(w[k].shape) for k in _SMALL])], "gather_small")[0]
    sm = _adamw([parts_all[k] for k in range(NDEV)], _pack(small_like), _pack([mo[k] for k in _SMALL]),
                _pack([vo[k] for k in _SMALL]), "adamw_small")
    sm = [dict(zip(_SMALL, _unpack(buf, small_like))) for buf in sm]

    res = {}
    for k in _SMALL:
        res[k] = tuple(s[k] for s in sm)
    rb = _adamw([g_b_ada], b_ada, mo["b_ada"], vo["b_ada"], "adamw_b_ada")
    res["b_ada"] = tuple(rb)
    sh = w_ada.shape
    ra = _adamw([g_w_ada.reshape(sh[0] * sh[1], sh[2])], w_ada.reshape(sh[0] * sh[1], sh[2]),
                mo["w_ada"].reshape(sh[0] * sh[1], sh[2]), vo["w_ada"].reshape(sh[0] * sh[1], sh[2]), "adamw_w_ada")
    res["w_ada"] = tuple(z.reshape(sh) for z in ra)

    for k, rs in (("b_w_out", rs_wb_out), ("b_w_in", rs_wb_in), ("a_w_out", rs_wa_out), ("a_w_in", rs_wa_in)):
        res[k] = tuple(z[None] for z in _adamw(rs.parts, w[k][0], mo[k][0], vo[k][0], "adamw_" + k))

    order = ["norm_gain", "w_ada", "b_ada", "a_w_in", "a_ln_gain", "a_ln_bias", "a_w_s", "a_b_s", "a_w_out",
             "b_w_in", "b_lower_bounds", "b_gn_gain", "b_w_out", "final_gain"]
    return (loss, grad_x, *[res[k][0] for k in order], *[res[k][1] for k in order],
            *[res[k][2] for k in order], *[res[k][3] for k in order])
```

```python
import functools
import math

import jax
import jax.numpy as jnp
from jax import lax
from jax.experimental import pallas as pl
from jax.experimental.pallas import tpu as pltpu

F32 = jnp.float32
BF16 = jnp.bfloat16
MESH = pl.DeviceIdType.MESH
NDEV = 8
EPS = 1e-6
CHUNK = 64
SG_BLOCK = 128
SG_GROUPS = 8
HEAD_DIM = 128
CUM_ROWS = 256
ADAM_LR, ADAM_B1, ADAM_B2, ADAM_EPS, ADAM_WD, ADAM_STEP = 0.001, 0.9, 0.999, 1e-08, 0.01, 10
VMEM_LIMIT = 56 * 1024 * 1024
ANY = pl.BlockSpec(memory_space=pl.ANY)


class _Hosted:
    def __init__(self, arrays, out_shapes, nsem, start, finish, aliases=None):
        self.arrays, self.out_shapes, self.nsem = list(arrays), list(out_shapes), nsem
        self.start, self.finish = start, finish
        self.aliases = dict(aliases or {})


def _join(*comms):
    arrays, outs, aliases, offs, nsem = [], [], {}, [], 0
    for cm in comms:
        offs.append((len(arrays), len(outs), nsem))
        for i, o in cm.aliases.items():
            aliases[len(arrays) + i] = len(outs) + o
        arrays += cm.arrays
        outs += cm.out_shapes
        nsem += cm.nsem

    def run(which):
        def f(ins, outs_, ss, rs, base):
            for cm, (ia, io, isem) in zip(comms, offs):
                getattr(cm, which)(ins[ia:ia + len(cm.arrays)], outs_[io:io + len(cm.out_shapes)], ss, rs, base + isem)
        return f

    return _Hosted(arrays, outs, nsem, run("start"), run("finish"), aliases)


def _pc(body, *, name, out_shape, grid=None, in_specs=None, out_specs=None, scratch=(), sem=None,
        grid_spec=None, comm=None, aliases=None):
    cp = dict(vmem_limit_bytes=VMEM_LIMIT)
    aliases = dict(aliases or {})
    if comm is None:
        if sem is not None:
            cp["dimension_semantics"] = sem
        kw = {"input_output_aliases": aliases}
        if grid_spec is not None:
            kw["grid_spec"] = grid_spec
        else:
            if grid is not None:
                kw["grid"] = grid
            if in_specs is not None:
                kw["in_specs"] = in_specs
            if out_specs is not None:
                kw["out_specs"] = out_specs
            kw["scratch_shapes"] = list(scratch)
        return pl.pallas_call(functools.partial(body), name=name, out_shape=out_shape,
                              compiler_params=pltpu.CompilerParams(**cp), **kw)

    single = not isinstance(out_shape, (list, tuple))
    outs_list = [out_shape] if single else list(out_shape)
    ospecs = [out_specs] if single else list(out_specs)
    n_in, n_out, n_ci, n_co, n_scr = len(in_specs), len(outs_list), len(comm.arrays), len(comm.out_shapes), len(scratch)
    cp["dimension_semantics"] = ("arbitrary",) * len(grid)

    def hosted(*refs):
        cin, hin = refs[:n_in], refs[n_in:n_in + n_ci]
        cout = refs[n_in + n_ci:n_in + n_ci + n_out]
        hout = refs[n_in + n_ci + n_out:n_in + n_ci + n_out + n_co]
        scr = refs[n_in + n_ci + n_out + n_co:n_in + n_ci + n_out + n_co + n_scr]
        ssem, rsem = refs[-2], refs[-1]
        first = functools.reduce(lambda p, q: p & q, [pl.program_id(a) == 0 for a in range(len(grid))])
        last = functools.reduce(lambda p, q: p & q, [pl.program_id(a) == grid[a] - 1 for a in range(len(grid))])

        @pl.when(first)
        def _():
            comm.start(hin, hout, ssem, rsem, 0)

        body(*cin, *cout, *scr)

        @pl.when(last)
        def _():
            comm.finish(hin, hout, ssem, rsem, 0)

    call = pl.pallas_call(
        hosted, name=name, grid=grid, in_specs=list(in_specs) + [ANY] * n_ci, out_specs=ospecs + [ANY] * n_co,
        out_shape=outs_list + comm.out_shapes,
        scratch_shapes=list(scratch) + [pltpu.SemaphoreType.DMA((comm.nsem,)), pltpu.SemaphoreType.DMA((comm.nsem,))],
        input_output_aliases={**aliases, **{n_in + i: n_out + o for i, o in comm.aliases.items()}},
        compiler_params=pltpu.CompilerParams(**cp))

    def run(*args):
        res = call(*args, *comm.arrays)
        comp = res[:n_out]
        return (comp[0] if single else comp), list(res[n_out:])

    return run


def _tile(n, pref):
    return pref if n % pref == 0 else n


def _sigmoid(x):
    return 1.0 / (1.0 + jnp.exp(-x))


def _gelu(x):
    c = math.sqrt(2.0 / math.pi)
    return 0.5 * x * (1.0 + jnp.tanh(c * (x + 0.044715 * (x * x * x))))


def _dgelu(x):
    c = math.sqrt(2.0 / math.pi)
    t = jnp.tanh(c * (x + 0.044715 * (x * x * x)))
    return 0.5 * (1.0 + t) + 0.5 * x * (1.0 - t * t) * (c * (1.0 + 3.0 * 0.044715 * (x * x)))


def _dot(a, b):
    return jnp.dot(a, b, preferred_element_type=F32)


def _dot_nt(a, b):
    return lax.dot_general(a, b, (((1,), (1,)), ((), ())), preferred_element_type=F32)


def _dot_tn(a, b):
    return lax.dot_general(a, b, (((0,), (0,)), ((), ())), preferred_element_type=F32)


def _tri_mask(n, reverse):
    r = lax.broadcasted_iota(jnp.int32, (n, n), 0)
    c = lax.broadcasted_iota(jnp.int32, (n, n), 1)
    same = (r // CHUNK) == (c // CHUNK)
    tri = (c >= r) if reverse else (c <= r)
    return jnp.where(same & tri, 1.0, 0.0).astype(BF16)


def _tri_apply(tri, x):
    hi = x.astype(BF16)
    r1 = x - hi.astype(F32)
    mid = r1.astype(BF16)
    lo = (r1 - mid.astype(F32)).astype(BF16)
    return _dot(tri, hi) + (_dot(tri, mid) + _dot(tri, lo))


def _all_gather(arrs, name):
    n = len(arrs)

    def body(*refs):
        ins, outs = refs[:n], refs[n:2 * n]
        send_sems, recv_sems, local_sems = refs[2 * n:]
        x, y, c = lax.axis_index("x"), lax.axis_index("y"), lax.axis_index("c")
        me, sibling = (x, y, c), (x, y, 1 - c)
        chips = [(1 - x, y), (x, 1 - y), (1 - x, 1 - y)]

        def blk(a, p):
            return outs[a].at[4 * p[0] + 2 * p[1] + p[2]]

        def copy(a, k, block, to, src=None):
            return pltpu.make_async_remote_copy(
                src_ref=blk(a, block) if src is None else src, dst_ref=blk(a, block),
                send_sem=send_sems.at[7 * a + k], recv_sem=recv_sems.at[7 * a + k],
                device_id=to, device_id_type=MESH)

        mine = [pltpu.make_async_copy(ins[a], blk(a, me), local_sems.at[a]) for a in range(n)]
        for m in mine:
            m.start()
        first = []
        for a in range(n):
            first.append(copy(a, 0, me, sibling, src=ins[a]))
            for j, chip in enumerate(chips):
                first.append(copy(a, 1 + j, me, (*chip, c), src=ins[a]))
        for cp in first:
            cp.start()
        passed = []
        for j, chip in enumerate(chips):
            for a in range(n):
                copy(a, 1 + j, (*chip, c), me).wait_recv()
                p = copy(a, 4 + j, (*chip, c), sibling)
                p.start()
                passed.append(p)
        for a in range(n):
            copy(a, 0, sibling, me).wait_recv()
            for j, chip in enumerate(chips):
                copy(a, 4 + j, (*chip, 1 - c), me).wait_recv()
        for cp in first + passed:
            cp.wait_send()
        for m in mine:
            m.wait()

    out_shape = [jax.ShapeDtypeStruct((NDEV,) + a.shape, a.dtype) for a in arrs]
    return _pc(body, name=name, out_shape=out_shape, in_specs=[ANY] * n, out_specs=[ANY] * n,
               scratch=[pltpu.SemaphoreType.DMA((7 * n,)), pltpu.SemaphoreType.DMA((7 * n,)),
                        pltpu.SemaphoreType.DMA((n,))])(*arrs)


def _gather_first(arrs):
    n = len(arrs)

    def parts(ins, outs, ss, rs, base):
        x, y, c = lax.axis_index("x"), lax.axis_index("y"), lax.axis_index("c")
        me, sibling = (x, y, c), (x, y, 1 - c)
        chips = [(1 - x, y), (x, 1 - y), (1 - x, 1 - y)]

        def blk(a, p):
            return outs[a].at[4 * p[0] + 2 * p[1] + p[2]]

        def copy(a, k, block, to):
            return pltpu.make_async_remote_copy(
                src_ref=ins[a], dst_ref=blk(a, block), send_sem=ss.at[base + 4 * a + k],
                recv_sem=rs.at[base + 4 * a + k], device_id=to, device_id_type=MESH)

        local = [pltpu.make_async_copy(ins[a], blk(a, me), ss.at[base + 4 * n + a]) for a in range(n)]
        sends, recvs = [], []
        for a in range(n):
            sends.append(copy(a, 0, me, sibling))
            recvs.append(copy(a, 0, sibling, me))
            for j, chip in enumerate(chips):
                sends.append(copy(a, 1 + j, me, (*chip, c)))
                recvs.append(copy(a, 1 + j, (*chip, c), me))
        return local, sends, recvs

    def start(ins, outs, ss, rs, base):
        local, sends, _ = parts(ins, outs, ss, rs, base)
        for cp in local + sends:
            cp.start()

    def finish(ins, outs, ss, rs, base):
        local, sends, recvs = parts(ins, outs, ss, rs, base)
        for cp in recvs:
            cp.wait_recv()
        for cp in sends:
            cp.wait_send()
        for cp in local:
            cp.wait()

    return _Hosted(arrs, [jax.ShapeDtypeStruct((NDEV,) + a.shape, a.dtype) for a in arrs], 5 * n, start, finish)


def _gather_second(bufs):
    n = len(bufs)

    def parts(ins, outs, ss, rs, base):
        x, y, c = lax.axis_index("x"), lax.axis_index("y"), lax.axis_index("c")
        sibling = (x, y, 1 - c)
        chips = [(1 - x, y), (x, 1 - y), (1 - x, 1 - y)]
        sends, recvs = [], []
        for a in range(n):
            for j, chip in enumerate(chips):
                mine = 4 * chip[0] + 2 * chip[1] + c
                theirs = 4 * chip[0] + 2 * chip[1] + (1 - c)
                sends.append(pltpu.make_async_remote_copy(
                    src_ref=ins[a].at[mine], dst_ref=outs[a].at[mine], send_sem=ss.at[base + 3 * a + j],
                    recv_sem=rs.at[base + 3 * a + j], device_id=sibling, device_id_type=MESH))
                recvs.append(pltpu.make_async_remote_copy(
                    src_ref=ins[a].at[theirs], dst_ref=outs[a].at[theirs], send_sem=ss.at[base + 3 * a + j],
                    recv_sem=rs.at[base + 3 * a + j], device_id=sibling, device_id_type=MESH))
        return sends, recvs

    def start(ins, outs, ss, rs, base):
        for cp in parts(ins, outs, ss, rs, base)[0]:
            cp.start()

    def finish(ins, outs, ss, rs, base):
        sends, recvs = parts(ins, outs, ss, rs, base)
        for cp in recvs:
            cp.wait_recv()
        for cp in sends:
            cp.wait_send()

    return _Hosted(bufs, [jax.ShapeDtypeStruct(b.shape, b.dtype) for b in bufs], 3 * n, start, finish,
                   aliases={a: a for a in range(n)})


def _swap(src, nblk, ids_fn, partner_fn):
    def copies(ins, outs, ss, rs, base):
        x, y, c = lax.axis_index("x"), lax.axis_index("y"), lax.axis_index("c")
        ids = ids_fn(x, y, c)
        partner = partner_fn(x, y, c)
        return [pltpu.make_async_remote_copy(
            src_ref=ins[0].at[ids[k]], dst_ref=outs[0].at[k], send_sem=ss.at[base + k], recv_sem=rs.at[base + k],
            device_id=partner, device_id_type=MESH) for k in range(nblk)]

    def start(ins, outs, ss, rs, base):
        for cp in copies(ins, outs, ss, rs, base):
            cp.start()

    def finish(ins, outs, ss, rs, base):
        for cp in copies(ins, outs, ss, rs, base):
            cp.wait()

    return _Hosted([src], [jax.ShapeDtypeStruct((nblk,) + src.shape[1:], src.dtype)], nblk, start, finish)


def _blocking(comm, name):
    n_i, n_o = len(comm.arrays), len(comm.out_shapes)

    def body(*refs):
        ins, outs = refs[:n_i], refs[n_i:n_i + n_o]
        comm.start(ins, outs, refs[-2], refs[-1], 0)
        comm.finish(ins, outs, refs[-2], refs[-1], 0)

    return pl.pallas_call(
        body, name=name, out_shape=comm.out_shapes, in_specs=[ANY] * n_i, out_specs=[ANY] * n_o,
        scratch_shapes=[pltpu.SemaphoreType.DMA((comm.nsem,)), pltpu.SemaphoreType.DMA((comm.nsem,))],
        input_output_aliases=comm.aliases)(*comm.arrays)


def _swap_chips(send):
    def copies(ins, outs, ss, rs, base):
        x, y, c = lax.axis_index("x"), lax.axis_index("y"), lax.axis_index("c")
        chips = [(1 - x, y), (x, 1 - y), (1 - x, 1 - y)]
        return [pltpu.make_async_remote_copy(
            src_ref=ins[0].at[j], dst_ref=outs[0].at[j], send_sem=ss.at[base + j], recv_sem=rs.at[base + j],
            device_id=(*chip, c), device_id_type=MESH) for j, chip in enumerate(chips)]

    def start(ins, outs, ss, rs, base):
        for cp in copies(ins, outs, ss, rs, base):
            cp.start()

    def finish(ins, outs, ss, rs, base):
        for cp in copies(ins, outs, ss, rs, base):
            cp.wait()

    return _Hosted([send], [jax.ShapeDtypeStruct(send.shape, send.dtype)], 3, start, finish)


def _add_send(a, b, idx, ns, name):
    _, r, c = a.shape
    tr = _tile(r, 256)

    def body(idx_ref, a_ref, b_ref, send_ref):
        send_ref[...] = (a_ref[...] + b_ref[...]).astype(BF16)

    def sel(off):
        return pl.BlockSpec((None, tr, c), lambda k, i, s: (s[off + k], i, 0))

    gs = pltpu.PrefetchScalarGridSpec(num_scalar_prefetch=1, grid=(ns, r // tr), in_specs=[sel(0), sel(ns)],
                                      out_specs=pl.BlockSpec((None, tr, c), lambda k, i, s: (k, i, 0)))
    return _pc(body, name=name, grid_spec=gs, sem=("arbitrary", "arbitrary"),
               out_shape=jax.ShapeDtypeStruct((ns, r, c), BF16))(idx, a, b)


class _ReduceScatter:
    def __init__(self, g, tag):
        self.g, self.tag = g, tag

    def swap_core(self):
        return _swap(self.g, 4, lambda x, y, c: [1 - c, 3 - c, 5 - c, 7 - c], lambda x, y, c: (x, y, 1 - c))

    def after_core(self, recv):
        x, y, c = lax.axis_index("x"), lax.axis_index("y"), lax.axis_index("c")
        chips = [(1 - x, y), (x, 1 - y), (1 - x, 1 - y)]
        idx = jnp.stack([4 * p + 2 * q + c for p, q in chips] + [2 * p + q for p, q in chips]).astype(jnp.int32)
        self.send = _add_send(self.g, recv, idx, 3, "rs_add_" + self.tag)
        self.mine = [lax.dynamic_index_in_dim(self.g, 4 * x + 2 * y + c, 0, keepdims=False),
                     lax.dynamic_index_in_dim(recv, 2 * x + y, 0, keepdims=False)]

    def swap_chips(self):
        return _swap_chips(self.send)

    def after_chips(self, recv):
        self.parts = self.mine + [recv[0], recv[1], recv[2]]


def _ada_fwd(c_all, w_ada, b_cols, b_lb):
    nl, d, ncol = w_ada.shape
    nseq = c_all.shape[0]
    di = b_lb.shape[1]

    def body(c_ref, w_ref, b_ref, lb_ref, mod_ref, lbj_ref):
        cv = c_ref[...]
        cact = (cv * _sigmoid(cv)).astype(BF16)
        for l in range(nl):
            mod_ref[l] = _dot(cact, w_ref[l].astype(BF16)) + b_ref[l]
        b0, b1 = lb_ref[0:1, :], lb_ref[1:2, :]
        mx = jnp.maximum(b0, b1)
        e0, e1 = jnp.exp(b0 - mx), jnp.exp(b1 - mx)
        s = e0 + e1
        p0, p1 = e0 / s, e1 / s
        lbj_ref[0:1, :] = (p0 + p1) - p0
        lbj_ref[1:2, :] = p0 * p1

    return _pc(body, name="ada_fwd",
               out_shape=[jax.ShapeDtypeStruct((nl, nseq, ncol), F32), jax.ShapeDtypeStruct((2, di), F32)]
               )(c_all, w_ada, b_cols, b_lb)


def _ada_bwd(c_all, dmod_cols, dmod_full):
    nl, nseq, ncol = dmod_cols.shape
    d = c_all.shape[1]
    d3 = dmod_full.shape[2]

    def body(c_ref, dc_ref, df_ref, gw_ref, gb_ref):
        cv = c_ref[...]
        cact = (cv * _sigmoid(cv)).astype(BF16)
        for l in range(nl):
            gw_ref[l] = _dot_tn(cact, dc_ref[l].astype(BF16))
            gb_ref[l:l + 1, :] = jnp.sum(df_ref[l], axis=0, keepdims=True)

    return _pc(body, name="ada_bwd",
               out_shape=[jax.ShapeDtypeStruct((nl, d, ncol), F32), jax.ShapeDtypeStruct((nl, d3), F32)]
               )(c_all, dmod_cols, dmod_full)


def _prenorm(x, gain, mod, t_seq, name):
    m, d = x.shape
    tm = _tile(t_seq, 512)
    per = t_seq // tm

    def body(x_ref, g_ref, mod_ref, h_ref, ht_ref):
        xv = x_ref[...]
        rstd = lax.rsqrt(jnp.mean(xv * xv, axis=-1, keepdims=True) + EPS)
        r = xv * rstd * g_ref[...]
        h = r * (1.0 + mod_ref[0, 1:2, :]) + mod_ref[0, 0:1, :]
        h_ref[...] = h.astype(BF16)
        ht_ref[...] = h.T.astype(BF16)

    return _pc(body, name=name, out_shape=[jax.ShapeDtypeStruct((m, d), BF16), jax.ShapeDtypeStruct((d, m), BF16)],
               grid=(m // tm,),
               in_specs=[pl.BlockSpec((tm, d), lambda i: (i, 0)), pl.BlockSpec((1, d), lambda i: (0, 0)),
                         pl.BlockSpec((1, 3, d), lambda i: (i // per, 0, 0))],
               out_specs=[pl.BlockSpec((tm, d), lambda i: (i, 0)), pl.BlockSpec((d, tm), lambda i: (0, i))],
               sem=("parallel",))(x, gain, mod)


def _prenorm_bwd(dh, x, gain, mod, dxn, t_seq, name, comm=None):
    m, d = x.shape
    nb = m // t_seq
    tm = _tile(t_seq, 512)
    per = t_seq // tm

    def body(dh_ref, x_ref, g_ref, mod_ref, dxn_ref, dx_ref, dss_ref, dg_ref):
        i = pl.program_id(0)
        xv, dhv, g = x_ref[...], dh_ref[...], g_ref[...]
        rstd = lax.rsqrt(jnp.mean(xv * xv, axis=-1, keepdims=True) + EPS)
        xhat = xv * rstd
        dr = dhv * (1.0 + mod_ref[0, 1:2, :])
        dxhat = dr * g
        dx_ref[...] = dxn_ref[...] + rstd * (dxhat - xhat * jnp.mean(dxhat * xhat, axis=-1, keepdims=True))

        @pl.when(i % per == 0)
        def _():
            dss_ref[...] = jnp.zeros_like(dss_ref)

        @pl.when(i == 0)
        def _():
            dg_ref[...] = jnp.zeros_like(dg_ref)

        dss_ref[0, 0:1, :] += jnp.sum(dhv, axis=0, keepdims=True)
        dss_ref[0, 1:2, :] += jnp.sum(dhv * (xhat * g), axis=0, keepdims=True)
        dg_ref[...] += jnp.sum(dr * xhat, axis=0, keepdims=True)

    row = pl.BlockSpec((tm, d), lambda i: (i, 0))
    return _pc(body, name=name,
               out_shape=[jax.ShapeDtypeStruct((m, d), F32), jax.ShapeDtypeStruct((nb, 2, d), F32),
                          jax.ShapeDtypeStruct((1, d), F32)],
               grid=(m // tm,),
               in_specs=[row, row, pl.BlockSpec((1, d), lambda i: (0, 0)),
                         pl.BlockSpec((1, 3, d), lambda i: (i // per, 0, 0)), row],
               out_specs=[row, pl.BlockSpec((1, 2, d), lambda i: (i // per, 0, 0)),
                          pl.BlockSpec((1, d), lambda i: (0, 0))],
               sem=("arbitrary",), comm=comm)(dh, x, gain, mod, dxn)


def _mm_in(h, ws, sections, name, comm=None):
    m, k = h.shape
    nw, ncp = len(ws), ws[0].shape[2]
    nc = nw * ncp
    per = NDEV // sections if sections > 1 else NDEV
    tm = _tile(m, 512)
    assert per % 2 == 0

    def body(*refs):
        hv = refs[0][...]
        o_ref = refs[1 + nw]
        for b in range(2):
            for a in range(nw):
                lo = b * nc + a * ncp
                o_ref[:, lo:lo + ncp] = _dot(hv, refs[1 + a][b])

    w_spec = pl.BlockSpec((2, k, ncp), lambda j, i: (j, 0, 0))
    if sections > 1:
        out_shape = jax.ShapeDtypeStruct((sections, m, per * nc), F32)
        out_spec = pl.BlockSpec((None, tm, 2 * nc), lambda j, i: ((2 * j) // per, i, ((2 * j) % per) // 2))
    else:
        out_shape = jax.ShapeDtypeStruct((m, NDEV * nc), F32)
        out_spec = pl.BlockSpec((tm, 2 * nc), lambda j, i: (i, j))
    return _pc(body, name=name, out_shape=out_shape, grid=(NDEV // 2, m // tm),
               in_specs=[pl.BlockSpec((tm, k), lambda j, i: (i, 0))] + [w_spec] * nw,
               out_specs=out_spec, sem=("parallel", "parallel"), comm=comm)(h, *ws)


def _din_tile(m):
    return 1024 if m % 1024 == 0 and m >= 2048 else _tile(m, 512)


def _mm_din(dproj, ws, sections, name, comm=None, tiles=None, prev=None):
    nw, k, ncp = len(ws), ws[0].shape[1], ws[0].shape[2]
    nc = nw * ncp
    m = dproj.shape[-2]
    tm = _din_tile(m)
    t0, nt = tiles if tiles is not None else (0, m // tm)
    per = NDEV // sections if sections > 1 else NDEV

    def body(*refs):
        d_ref, o_ref = refs[0], refs[-1]
        j = pl.program_id(1)
        acc = _dot_nt(d_ref[:, :ncp], refs[1][...])
        for a in range(1, nw):
            acc = acc + _dot_nt(d_ref[:, a * ncp:(a + 1) * ncp], refs[1 + a][...])

        @pl.when(j == 0)
        def _():
            o_ref[...] = acc

        @pl.when(j > 0)
        def _():
            o_ref[...] += acc

    if sections > 1:
        dspec = pl.BlockSpec((None, tm, nc), lambda i, j: (j // per, i + t0, j % per))
    else:
        dspec = pl.BlockSpec((tm, nc), lambda i, j: (i + t0, j))
    in_specs = [dspec] + [pl.BlockSpec((None, k, ncp), lambda i, j: (j, 0, 0))] * nw
    args = [dproj, *ws]
    if prev is not None:
        in_specs.append(ANY)
        args.append(prev)
    return _pc(body, name=name, out_shape=jax.ShapeDtypeStruct((m, k), F32), grid=(nt, NDEV), in_specs=in_specs,
               out_specs=pl.BlockSpec((tm, k), lambda i, j: (i + t0, 0)), sem=("parallel", "arbitrary"),
               comm=comm, aliases={1 + nw: 0} if prev is not None else None)(*args)


def _mm_dw_in(ht, dproj, nc, sections, name, comm=None):
    k, m = ht.shape
    tk = _din_tile(m)
    per = NDEV // sections if sections > 1 else NDEV

    def body(h_ref, d_ref, o_ref):
        kk = pl.program_id(1)
        acc = _dot(h_ref[...], d_ref[...])

        @pl.when(kk == 0)
        def _():
            o_ref[...] = acc

        @pl.when(kk > 0)
        def _():
            o_ref[...] += acc

    if sections > 1:
        dspec = pl.BlockSpec((None, tk, nc), lambda j, i: (j // per, i, j % per))
    else:
        dspec = pl.BlockSpec((tk, nc), lambda j, i: (i, j))
    return _pc(body, name=name, out_shape=jax.ShapeDtypeStruct((NDEV, k, nc), F32), grid=(NDEV, m // tk),
               in_specs=[pl.BlockSpec((k, tk), lambda j, i: (0, i)), dspec],
               out_specs=pl.BlockSpec((None, k, nc), lambda j, i: (j, 0, 0)),
               sem=("parallel", "arbitrary"), comm=comm)(ht, dproj)


def _out_proj(ybr, w_out, x, mod, t_seq, name, comm=None):
    m, di = ybr.shape
    d = w_out.shape[1]
    tm = _tile(t_seq, 512)
    per = t_seq // tm

    def body(y_ref, w_ref, x_ref, mod_ref, yo_ref, xn_ref):
        yo = _dot(y_ref[...], w_ref[...])
        yo_ref[...] = yo
        xn_ref[...] = x_ref[...] + mod_ref[0, 2:3, :] * yo

    row = pl.BlockSpec((tm, d), lambda i: (i, 0))
    return _pc(body, name=name,
               out_shape=[jax.ShapeDtypeStruct((m, d), F32), jax.ShapeDtypeStruct((m, d), F32)],
               grid=(m // tm,),
               in_specs=[pl.BlockSpec((tm, di), lambda i: (i, 0)), pl.BlockSpec((di, d), lambda i: (0, 0)), row,
                         pl.BlockSpec((1, 3, d), lambda i: (i // per, 0, 0))],
               out_specs=[row, row], sem=("parallel",), comm=comm)(ybr, w_out, x, mod)


def _gate_bwd(dxn, yout, mod, t_seq, name):
    m, d = dxn.shape
    nb = m // t_seq
    tm = _tile(t_seq, 512)
    per = t_seq // tm

    def body(dxn_ref, yo_ref, mod_ref, dy_ref, dgate_ref):
        i = pl.program_id(0)
        dv = dxn_ref[...]
        dy_ref[...] = (mod_ref[0, 2:3, :] * dv).astype(BF16)

        @pl.when(i % per == 0)
        def _():
            dgate_ref[...] = jnp.zeros_like(dgate_ref)

        dgate_ref[0] += jnp.sum(dv * yo_ref[...], axis=0, keepdims=True)

    row = pl.BlockSpec((tm, d), lambda i: (i, 0))
    return _pc(body, name=name,
               out_shape=[jax.ShapeDtypeStruct((m, d), BF16), jax.ShapeDtypeStruct((nb, 1, d), F32)],
               grid=(m // tm,),
               in_specs=[row, row, pl.BlockSpec((1, 3, d), lambda i: (i // per, 0, 0))],
               out_specs=[row, pl.BlockSpec((1, 1, d), lambda i: (i // per, 0, 0))],
               sem=("arbitrary",))(dxn, yout, mod)


def _mm_dybr(dy, w_out, name, comm=None):
    m, d = dy.shape
    di = w_out.shape[0]
    tm = _tile(m, 512)

    def body(dy_ref, w_ref, o_ref):
        o_ref[...] = _dot_nt(dy_ref[...], w_ref[...])

    return _pc(body, name=name, out_shape=jax.ShapeDtypeStruct((m, di), F32), grid=(m // tm,),
               in_specs=[pl.BlockSpec((tm, d), lambda i: (i, 0)), pl.BlockSpec((di, d), lambda i: (0, 0))],
               out_specs=pl.BlockSpec((tm, di), lambda i: (i, 0)), sem=("parallel",), comm=comm)(dy, w_out)


def _mm_dw_out(ybr, dy, name, comm=None):
    m, di = ybr.shape
    d = dy.shape[1]
    tk = _tile(m, 512)
    tn = _tile(di, 1024)

    def body(y_ref, dy_ref, o_ref):
        kk = pl.program_id(1)
        acc = _dot_tn(y_ref[...], dy_ref[...])

        @pl.when(kk == 0)
        def _():
            o_ref[...] = acc

        @pl.when(kk > 0)
        def _():
            o_ref[...] += acc

    return _pc(body, name=name, out_shape=jax.ShapeDtypeStruct((di, d), F32), grid=(di // tn, m // tk),
               in_specs=[pl.BlockSpec((tk, tn), lambda n, k: (k, n)), pl.BlockSpec((tk, d), lambda n, k: (k, 0))],
               out_specs=pl.BlockSpec((tn, d), lambda n, k: (n, 0)), sem=("parallel", "arbitrary"),
               comm=comm)(ybr, dy)


def _sgu_mask():
    t = lax.broadcasted_iota(jnp.int32, (SG_BLOCK, SG_BLOCK), 0)
    s = lax.broadcasted_iota(jnp.int32, (SG_BLOCK, SG_BLOCK), 1)
    return (s // CHUNK) <= (t // CHUNK)


def _a_mid_fwd(proj, ln_g, ln_b, w_s, bs_t, t_seq, comm=None):
    m, n3 = proj.shape
    di = n3 // 3
    gd = di // SG_GROUPS
    r = _tile(t_seq, 256)
    nblk = r // SG_BLOCK

    def body(p_ref, lg_ref, lb_ref, ws_ref, bs_ref, ybr_ref, s_scr):
        v = _gelu(p_ref[:, di:2 * di])
        mu = jnp.mean(v, axis=-1, keepdims=True)
        vc = v - mu
        rstd = lax.rsqrt(jnp.mean(vc * vc, axis=-1, keepdims=True) + EPS)
        vb = (vc * rstd * lg_ref[...] + lb_ref[...]).astype(BF16)
        mask = _sgu_mask()
        for gi in range(SG_GROUPS):
            ws = jnp.where(mask, ws_ref[gi], 0.0).astype(BF16)
            bcol = bs_ref[:, gi:gi + 1]
            for b in range(nblk):
                rows = slice(b * SG_BLOCK, (b + 1) * SG_BLOCK)
                cols = slice(gi * gd, (gi + 1) * gd)
                s_scr[rows, cols] = _dot(ws, vb[rows, cols]) + bcol
        gg = p_ref[:, 2 * di:]
        ybr_ref[...] = (_gelu(p_ref[:, :di]) * s_scr[...] * (gg * _sigmoid(gg))).astype(BF16)

    vec = pl.BlockSpec((1, di), lambda i: (0, 0))
    return _pc(body, name="a_mid_fwd", out_shape=jax.ShapeDtypeStruct((m, di), BF16), grid=(m // r,),
               in_specs=[pl.BlockSpec((r, n3), lambda i: (i, 0)), vec, vec,
                         pl.BlockSpec((SG_GROUPS, SG_BLOCK, SG_BLOCK), lambda i: (0, 0, 0)),
                         pl.BlockSpec((SG_BLOCK, 128), lambda i: (0, 0))],
               out_specs=pl.BlockSpec((r, di), lambda i: (i, 0)),
               scratch=[pltpu.VMEM((r, di), F32)], sem=("parallel",), comm=comm)(proj, ln_g, ln_b, w_s, bs_t)


def _a_mid_bwd(proj, dybr, ln_g, ln_b, w_s, bs_t, t_seq, comm=None):
    m, n3 = proj.shape
    di = n3 // 3
    gd = di // SG_GROUPS
    r = _tile(t_seq, 256)
    nblk = r // SG_BLOCK

    def body(p_ref, dy_ref, lg_ref, lb_ref, ws_ref, bs_ref,
             dp_ref, dlg_ref, dlb_ref, dws_ref, dbs_ref, s_scr, dvl_scr):
        i = pl.program_id(0)

        @pl.when(i == 0)
        def _():
            dlg_ref[...] = jnp.zeros_like(dlg_ref)
            dlb_ref[...] = jnp.zeros_like(dlb_ref)
            dws_ref[...] = jnp.zeros_like(dws_ref)
            dbs_ref[...] = jnp.zeros_like(dbs_ref)

        v_pre = p_ref[:, di:2 * di]
        v = _gelu(v_pre)
        mu = jnp.mean(v, axis=-1, keepdims=True)
        vc = v - mu
        rstd = lax.rsqrt(jnp.mean(vc * vc, axis=-1, keepdims=True) + EPS)
        vhat = vc * rstd
        lg = lg_ref[...]
        vb = (vhat * lg + lb_ref[...]).astype(BF16)
        u_pre = p_ref[:, :di]
        u = _gelu(u_pre)
        gg = p_ref[:, 2 * di:]
        sg = _sigmoid(gg)
        dyv = dy_ref[...]
        dus = dyv * (gg * sg)
        dsb = (dus * u).astype(BF16)
        ds32 = dus * u
        mask = _sgu_mask()
        lane = lax.broadcasted_iota(jnp.int32, (SG_BLOCK, 128), 1)
        dbs_acc = jnp.zeros((SG_BLOCK, 128), F32)
        for gi in range(SG_GROUPS):
            ws = jnp.where(mask, ws_ref[gi], 0.0).astype(BF16)
            bcol = bs_ref[:, gi:gi + 1]
            cols = slice(gi * gd, (gi + 1) * gd)
            dws_acc = jnp.zeros((SG_BLOCK, SG_BLOCK), F32)
            dbs_col = jnp.zeros((SG_BLOCK, 1), F32)
            for b in range(nblk):
                rows = slice(b * SG_BLOCK, (b + 1) * SG_BLOCK)
                s_scr[rows, cols] = _dot(ws, vb[rows, cols]) + bcol
                dvl_scr[rows, cols] = _dot_tn(ws, dsb[rows, cols])
                dws_acc += _dot_nt(dsb[rows, cols], vb[rows, cols])
                dbs_col += jnp.sum(ds32[rows, cols], axis=-1, keepdims=True)
            dws_ref[gi] += jnp.where(mask, dws_acc, 0.0)
            dbs_acc += jnp.where(lane == gi, dbs_col, 0.0)
        dbs_ref[...] += dbs_acc
        s = s_scr[...]
        dp_ref[:, :di] = (dyv * s * (gg * sg) * _dgelu(u_pre)).astype(BF16)
        dp_ref[:, 2 * di:] = (dyv * u * s * (sg * (1.0 + gg * (1.0 - sg)))).astype(BF16)
        dvl = dvl_scr[...]
        dlg_ref[...] += jnp.sum(dvl * vhat, axis=0, keepdims=True)
        dlb_ref[...] += jnp.sum(dvl, axis=0, keepdims=True)
        dvh = dvl * lg
        dv = rstd * (dvh - jnp.mean(dvh, axis=-1, keepdims=True)
                     - vhat * jnp.mean(dvh * vhat, axis=-1, keepdims=True))
        dp_ref[:, di:2 * di] = (dv * _dgelu(v_pre)).astype(BF16)

    vec = pl.BlockSpec((1, di), lambda i: (0, 0))
    wsb = pl.BlockSpec((SG_GROUPS, SG_BLOCK, SG_BLOCK), lambda i: (0, 0, 0))
    bsb = pl.BlockSpec((SG_BLOCK, 128), lambda i: (0, 0))
    return _pc(body, name="a_mid_bwd",
               out_shape=[jax.ShapeDtypeStruct((m, n3), BF16), jax.ShapeDtypeStruct((1, di), F32),
                          jax.ShapeDtypeStruct((1, di), F32),
                          jax.ShapeDtypeStruct((SG_GROUPS, SG_BLOCK, SG_BLOCK), F32),
                          jax.ShapeDtypeStruct((SG_BLOCK, 128), F32)],
               grid=(m // r,),
               in_specs=[pl.BlockSpec((r, n3), lambda i: (i, 0)), pl.BlockSpec((r, di), lambda i: (i, 0)),
                         vec, vec, wsb, bsb],
               out_specs=[pl.BlockSpec((r, n3), lambda i: (i, 0)), vec, vec, wsb, bsb],
               scratch=[pltpu.VMEM((r, di), F32), pltpu.VMEM((r, di), F32)],
               sem=("arbitrary",), comm=comm)(proj, dybr, ln_g, ln_b, w_s, bs_t)


def _hgrn_dims(t_seq, di):
    tr = _tile(t_seq, 256)
    hc = _tile(di, 1024)
    return tr, hc, hc // HEAD_DIM


def _hgrn_gates(f_ref, lb, a_scr, k_scr, tr):
    sig = _sigmoid(f_ref[...])
    fg = lb + (1.0 - lb) * sig
    k_scr[...] = 1.0 - fg
    logf = jnp.log(fg)
    g = min(CUM_ROWS, tr)
    tri = _tri_mask(g, reverse=False)
    for rg in range(tr // g):
        a_scr[rg * g:(rg + 1) * g, :] = _tri_apply(tri, logf[rg * g:(rg + 1) * g, :])
    return sig, fg


def _hgrn_fwd(proj, lbj, gn, nb, t_seq):
    _, m, di = proj.shape
    tr, hc, hpg = _hgrn_dims(t_seq, di)
    nt, nhg, ncl = t_seq // tr, di // hc, tr // CHUNK
    nheads = di // HEAD_DIM

    def body(q_ref, f_ref, i_ref, g_ref, lb_ref, gn_ref, o_ref, ybr_ref, st_ref, st_scr, a_scr, k_scr):
        t = pl.program_id(2)

        @pl.when(t == 0)
        def _():
            st_scr[...] = jnp.zeros_like(st_scr)

        _hgrn_gates(f_ref, lb_ref[0:1, :], a_scr, k_scr, tr)
        gnv = gn_ref[...]
        rr = lax.broadcasted_iota(jnp.int32, (CHUNK, CHUNK), 0)
        cc = lax.broadcasted_iota(jnp.int32, (CHUNK, CHUNK), 1)
        causal = cc <= rr

        def chunk(n, carry):
            rows = pl.ds(pl.multiple_of(n * CHUNK, CHUNK), CHUNK)
            lanes = [slice(hd * HEAD_DIM, (hd + 1) * HEAD_DIM) for hd in range(hpg)]
            hs = []
            for hd, ls in enumerate(lanes):
                h = {}
                ah, kh = a_scr[rows, ls], k_scr[rows, ls]
                qp = q_ref[rows, ls]
                qh = qp * _sigmoid(qp)
                h["vb"] = i_ref[rows, ls].astype(BF16)
                aref, alast = ah[CHUNK // 2 - 1:CHUNK // 2, :], ah[CHUNK - 1:CHUNK, :]
                h["q_in"] = (qh * jnp.exp(ah - aref)).astype(BF16)
                h["k_in"] = (kh * jnp.exp(aref - ah)).astype(BF16)
                h["q_out"] = (qh * jnp.exp(ah)).astype(BF16)
                h["k_out"] = (kh * jnp.exp(alast - ah)).astype(BF16)
                h["dec"] = jnp.exp(alast)
                st = st_scr[hd]
                st_ref[n, hd] = st
                h["st"] = st
                hs.append(h)
            for h in hs:
                h["scores"] = _dot_nt(h["q_in"], h["k_in"])
                h["o_inter"] = _dot_nt(h["q_out"], h["st"].astype(BF16))
                h["st_mm"] = _dot_tn(h["vb"], h["k_out"])
            for h in hs:
                h["o"] = _dot(jnp.where(causal, h["scores"], 0.0).astype(BF16), h["vb"]) + h["o_inter"]
            for hd, (h, ls) in enumerate(zip(hs, lanes)):
                st_scr[hd] = h["st"] * h["dec"] + h["st_mm"]
                o = h["o"]
                o_ref[rows, ls] = o
                rstd = lax.rsqrt(jnp.mean(o * o, axis=-1, keepdims=True) + EPS)
                gg = g_ref[rows, ls]
                ybr_ref[rows, ls] = ((o * rstd * gnv) * (gg * _sigmoid(gg))).astype(BF16)
            return carry

        lax.fori_loop(0, ncl, chunk, 0)

    def sec(s):
        return pl.BlockSpec((None, tr, hc), lambda hg, b, t: (s, b * nt + t, hg))

    blk = pl.BlockSpec((tr, hc), lambda hg, b, t: (b * nt + t, hg))
    return _pc(body, name="hgrn_fwd",
               out_shape=[jax.ShapeDtypeStruct((m, di), F32), jax.ShapeDtypeStruct((m, di), BF16),
                          jax.ShapeDtypeStruct((m // CHUNK, nheads, HEAD_DIM, HEAD_DIM), F32)],
               grid=(nhg, nb, nt),
               in_specs=[sec(0), sec(1), sec(2), sec(3), pl.BlockSpec((2, hc), lambda hg, b, t: (0, hg)),
                         pl.BlockSpec((1, HEAD_DIM), lambda hg, b, t: (0, 0))],
               out_specs=[blk, blk, pl.BlockSpec((ncl, hpg, HEAD_DIM, HEAD_DIM),
                                                 lambda hg, b, t: (b * nt + t, hg, 0, 0))],
               scratch=[pltpu.VMEM((hpg, HEAD_DIM, HEAD_DIM), F32), pltpu.VMEM((tr, hc), F32),
                        pltpu.VMEM((tr, hc), F32)],
               sem=("parallel", "arbitrary", "arbitrary"))(proj, proj, proj, proj, lbj, gn)


def _hgrn_bwd(proj, o_all, dybr, states, lbj, gn, nb, t_seq, comm=None):
    _, m, di = proj.shape
    tr, hc, hpg = _hgrn_dims(t_seq, di)
    nt, nhg, ncl = t_seq // tr, di // hc, tr // CHUNK

    def body(q_ref, f_ref, i_ref, g_ref, o_ref, dy_ref, st_ref, lb_ref, gn_ref,
             dp_ref, dlb_ref, dgn_ref, dst_scr, a_scr, k_scr, da_scr, dk_scr):
        hg, b, t = pl.program_id(0), pl.program_id(1), pl.program_id(2)

        @pl.when(t == 0)
        def _():
            dst_scr[...] = jnp.zeros_like(dst_scr)

        @pl.when((b == 0) & (t == 0))
        def _():
            dlb_ref[...] = jnp.zeros_like(dlb_ref)

        @pl.when((hg == 0) & (b == 0) & (t == 0))
        def _():
            dgn_ref[...] = jnp.zeros_like(dgn_ref)

        lb = lb_ref[0:1, :]
        sig, fg = _hgrn_gates(f_ref, lb, a_scr, k_scr, tr)
        gnv = gn_ref[...]
        rr = lax.broadcasted_iota(jnp.int32, (CHUNK, CHUNK), 0)
        cc = lax.broadcasted_iota(jnp.int32, (CHUNK, CHUNK), 1)
        causal = cc <= rr
        rowi = lax.broadcasted_iota(jnp.int32, (CHUNK, HEAD_DIM), 0)

        def chunk(it, carry):
            n = ncl - 1 - it
            rows = pl.ds(pl.multiple_of(n * CHUNK, CHUNK), CHUNK)
            lanes = [slice(hd * HEAD_DIM, (hd + 1) * HEAD_DIM) for hd in range(hpg)]
            hs = []
            for hd, ls in enumerate(lanes):
                h = {}
                ah, kh = a_scr[rows, ls], k_scr[rows, ls]
                qp = q_ref[rows, ls]
                sq = _sigmoid(qp)
                qh = qp * sq
                h["dsilu_q"] = sq * (1.0 + qp * (1.0 - sq))
                h["vb"] = i_ref[rows, ls].astype(BF16)
                aref, alast = ah[CHUNK // 2 - 1:CHUNK // 2, :], ah[CHUNK - 1:CHUNK, :]
                h["e1"], h["e2"] = jnp.exp(ah - aref), jnp.exp(aref - ah)
                h["e3"], h["e4"] = jnp.exp(ah), jnp.exp(alast - ah)
                h["dec"] = jnp.exp(alast)
                h["q_in"], h["k_in"], h["q_out"], h["k_out"] = qh * h["e1"], kh * h["e2"], qh * h["e3"], kh * h["e4"]
                for nm in ("q_in", "k_in", "q_out", "k_out"):
                    h[nm + "_b"] = h[nm].astype(BF16)
                o = o_ref[rows, ls]
                rstd = lax.rsqrt(jnp.mean(o * o, axis=-1, keepdims=True) + EPS)
                ohat = o * rstd
                gg = g_ref[rows, ls]
                sg = _sigmoid(gg)
                dyv = dy_ref[rows, ls]
                d_on = dyv * (gg * sg)
                dp_ref[3, rows, ls] = (dyv * (ohat * gnv) * (sg * (1.0 + gg * (1.0 - sg)))).astype(BF16)
                h["dgn"] = jnp.sum(d_on * ohat, axis=0, keepdims=True)
                dohat = d_on * gnv
                do = rstd * (dohat - ohat * jnp.mean(dohat * ohat, axis=-1, keepdims=True))
                h["do_b"] = do.astype(BF16)
                h["st_prev"] = st_ref[n, hd]
                h["dst"] = dst_scr[hd]
                hs.append(h)
            for h in hs:
                dst_b = h["dst"].astype(BF16)
                h["scores"] = _dot_nt(h["q_in_b"], h["k_in_b"])
                h["dscores"] = _dot_nt(h["do_b"], h["vb"])
                h["dv_inter"] = _dot_nt(h["k_out_b"], dst_b)
                h["dq_out"] = _dot(h["do_b"], h["st_prev"].astype(BF16))
                h["dk_out"] = _dot(h["vb"], dst_b)
                h["dst_mm"] = _dot_tn(h["do_b"], h["q_out_b"])
            for h in hs:
                scores = jnp.where(causal, h["scores"], 0.0).astype(BF16)
                dscores = jnp.where(causal, h["dscores"], 0.0).astype(BF16)
                h["dv"] = _dot_tn(scores, h["do_b"]) + h["dv_inter"]
                h["dq_in"] = _dot(dscores, h["k_in_b"])
                h["dk_in"] = _dot_tn(dscores, h["q_in_b"])
            dgn = hs[0]["dgn"]
            for h in hs[1:]:
                dgn = dgn + h["dgn"]
            dgn_ref[...] += dgn
            for hd, (h, ls) in enumerate(zip(hs, lanes)):
                ddec = jnp.sum(h["dst"] * h["st_prev"], axis=0, keepdims=True)
                dst_scr[hd] = h["dst"] * h["dec"] + h["dst_mm"]
                dp_ref[2, rows, ls] = h["dv"].astype(BF16)
                dq = h["dq_in"] * h["e1"] + h["dq_out"] * h["e3"]
                dp_ref[0, rows, ls] = (dq * h["dsilu_q"]).astype(BF16)
                dk_scr[rows, ls] = h["dk_in"] * h["e2"] + h["dk_out"] * h["e4"]
                t_in = h["dq_in"] * h["q_in"] - h["dk_in"] * h["k_in"]
                t_out = h["dk_out"] * h["k_out"]
                da = t_in + h["dq_out"] * h["q_out"] - t_out
                da_ref_row = -jnp.sum(t_in, axis=0, keepdims=True)
                da_last_row = jnp.sum(t_out, axis=0, keepdims=True) + ddec * h["dec"]
                da = da + jnp.where(rowi == CHUNK // 2 - 1, da_ref_row, 0.0) \
                        + jnp.where(rowi == CHUNK - 1, da_last_row, 0.0)
                da_scr[rows, ls] = da
            return carry

        lax.fori_loop(0, ncl, chunk, 0)
        g = min(CUM_ROWS, tr)
        tri = _tri_mask(g, reverse=True)
        for rg in range(tr // g):
            rs = slice(rg * g, (rg + 1) * g)
            dlogf = _tri_apply(tri, da_scr[rs, :])
            df = dlogf / fg[rs, :] - dk_scr[rs, :]
            sgr = sig[rs, :]
            dp_ref[1, rs, :] = (df * (1.0 - lb) * (sgr * (1.0 - sgr))).astype(BF16)
            dlb_ref[...] += jnp.sum(df * (1.0 - sgr), axis=0, keepdims=True) * lb_ref[1:2, :]

    def sec(s):
        return pl.BlockSpec((None, tr, hc), lambda hg, b, t: (s, b * nt + (nt - 1 - t), hg))

    blk = pl.BlockSpec((tr, hc), lambda hg, b, t: (b * nt + (nt - 1 - t), hg))
    return _pc(body, name="hgrn_bwd",
               out_shape=[jax.ShapeDtypeStruct((4, m, di), BF16), jax.ShapeDtypeStruct((1, di), F32),
                          jax.ShapeDtypeStruct((1, HEAD_DIM), F32)],
               grid=(nhg, nb, nt),
               in_specs=[sec(0), sec(1), sec(2), sec(3), blk, blk,
                         pl.BlockSpec((ncl, hpg, HEAD_DIM, HEAD_DIM),
                                      lambda hg, b, t: (b * nt + (nt - 1 - t), hg, 0, 0)),
                         pl.BlockSpec((2, hc), lambda hg, b, t: (0, hg)),
                         pl.BlockSpec((1, HEAD_DIM), lambda hg, b, t: (0, 0))],
               out_specs=[pl.BlockSpec((4, tr, hc), lambda hg, b, t: (0, b * nt + (nt - 1 - t), hg)),
                          pl.BlockSpec((1, hc), lambda hg, b, t: (0, hg)),
                          pl.BlockSpec((1, HEAD_DIM), lambda hg, b, t: (0, 0))],
               scratch=[pltpu.VMEM((hpg, HEAD_DIM, HEAD_DIM), F32)] + [pltpu.VMEM((tr, hc), F32)] * 4,
               sem=("arbitrary", "arbitrary", "arbitrary"), comm=comm)(
                   proj, proj, proj, proj, o_all, dybr, states, lbj, gn)


def _final_loss(x, gain, target):
    m, d = x.shape
    tm = _tile(m, 512)

    def body(x_ref, g_ref, t_ref, dx_ref, loss_ref, dg_ref):
        i = pl.program_id(0)
        xv, g = x_ref[...], g_ref[...]
        rstd = lax.rsqrt(jnp.mean(xv * xv, axis=-1, keepdims=True) + EPS)
        xhat = xv * rstd
        err = xhat * g - t_ref[...]
        dy = err * (1.0 / d)
        dxhat = dy * g
        dx_ref[...] = rstd * (dxhat - xhat * jnp.mean(dxhat * xhat, axis=-1, keepdims=True))

        @pl.when(i == 0)
        def _():
            loss_ref[...] = jnp.zeros_like(loss_ref)
            dg_ref[...] = jnp.zeros_like(dg_ref)

        loss_ref[...] += 0.5 * jnp.sum(jnp.mean(err * err, axis=-1, keepdims=True), axis=0, keepdims=True)
        dg_ref[...] += jnp.sum(dy * xhat, axis=0, keepdims=True)

    row = pl.BlockSpec((tm, d), lambda i: (i, 0))
    return _pc(body, name="final_loss",
               out_shape=[jax.ShapeDtypeStruct((m, d), F32), jax.ShapeDtypeStruct((1, 1), F32),
                          jax.ShapeDtypeStruct((1, d), F32)],
               grid=(m // tm,),
               in_specs=[row, pl.BlockSpec((1, d), lambda i: (0, 0)), row],
               out_specs=[row, pl.BlockSpec((1, 1), lambda i: (0, 0)), pl.BlockSpec((1, d), lambda i: (0, 0))],
               sem=("arbitrary",))(x, gain, target)


def _adamw(parts, w, m, v, name, comm=None):
    r, c = w.shape
    tr = _tile(r, 256)
    npart = len(parts)
    c1 = 1.0 - ADAM_B1 ** ADAM_STEP
    c2 = 1.0 - ADAM_B2 ** ADAM_STEP

    def body(*refs):
        p_refs = refs[:npart]
        w_ref, m_ref, v_ref, g_ref, d_ref, nm_ref, nv_ref = refs[npart:]
        g = p_refs[0][...].astype(F32)
        for p in p_refs[1:]:
            g = g + p[...].astype(F32)
        nm = ADAM_B1 * m_ref[...] + (1.0 - ADAM_B1) * g
        nv = ADAM_B2 * v_ref[...] + (1.0 - ADAM_B2) * (g * g)
        g_ref[...] = g
        nm_ref[...] = nm
        nv_ref[...] = nv
        d_ref[...] = -ADAM_LR * ((nm / c1) / (jnp.sqrt(nv / c2) + ADAM_EPS) + ADAM_WD * w_ref[...])

    blk = pl.BlockSpec((tr, c), lambda i: (i, 0))
    return _pc(body, name=name, out_shape=[jax.ShapeDtypeStruct((r, c), F32)] * 4, grid=(r // tr,),
               in_specs=[blk] * (npart + 3), out_specs=[blk] * 4, sem=("parallel",), comm=comm)(*parts, w, m, v)


_EARLY = ["a_ln_gain", "a_ln_bias", "a_w_s", "a_b_s", "b_lower_bounds", "b_gn_gain"]


def _pack(arrs):
    flat = jnp.concatenate([a.reshape(-1) for a in arrs])
    rows = -(-flat.shape[0] // 1024) * 8
    return jnp.pad(flat, (0, rows * 128 - flat.shape[0])).reshape(rows, 128)


def _unpack(buf, like):
    flat = buf.reshape(-1)
    out, off = [], 0
    for a in like:
        out.append(flat[off:off + a.size].reshape(a.shape))
        off += a.size
    return out


def kernel(x, c, norm_gain, w_ada, b_ada, a_w_in, a_ln_gain, a_ln_bias, a_w_s, a_b_s, a_w_out, b_w_in, b_lower_bounds, b_gn_gain, b_w_out, final_gain, loss_target, m_norm_gain, m_w_ada, m_b_ada, m_a_w_in, m_a_ln_gain, m_a_ln_bias, m_a_w_s, m_a_b_s, m_a_w_out, m_b_w_in, m_b_lower_bounds, m_b_gn_gain, m_b_w_out, m_final_gain, v_norm_gain, v_w_ada, v_b_ada, v_a_w_in, v_a_ln_gain, v_a_ln_bias, v_a_w_s, v_a_b_s, v_a_w_out, v_b_w_in, v_b_lower_bounds, v_b_gn_gain, v_b_w_out, v_final_gain):
    w = dict(norm_gain=norm_gain, w_ada=w_ada, b_ada=b_ada, a_w_in=a_w_in, a_ln_gain=a_ln_gain,
             a_ln_bias=a_ln_bias, a_w_s=a_w_s, a_b_s=a_b_s, a_w_out=a_w_out, b_w_in=b_w_in,
             b_lower_bounds=b_lower_bounds, b_gn_gain=b_gn_gain, b_w_out=b_w_out, final_gain=final_gain)
    mo = dict(norm_gain=m_norm_gain, w_ada=m_w_ada, b_ada=m_b_ada, a_w_in=m_a_w_in, a_ln_gain=m_a_ln_gain,
              a_ln_bias=m_a_ln_bias, a_w_s=m_a_w_s, a_b_s=m_a_b_s, a_w_out=m_a_w_out, b_w_in=m_b_w_in,
              b_lower_bounds=m_b_lower_bounds, b_gn_gain=m_b_gn_gain, b_w_out=m_b_w_out, final_gain=m_final_gain)
    vo = dict(norm_gain=v_norm_gain, w_ada=v_w_ada, b_ada=v_b_ada, a_w_in=v_a_w_in, a_ln_gain=v_a_ln_gain,
              a_ln_bias=v_a_ln_bias, a_w_s=v_a_w_s, a_b_s=v_a_b_s, a_w_out=v_a_w_out, b_w_in=v_b_w_in,
              b_lower_bounds=v_b_lower_bounds, b_gn_gain=v_b_gn_gain, b_w_out=v_b_w_out, final_gain=v_final_gain)

    nb, t_seq, d = x.shape
    m = nb * t_seq
    ncol_ada = w_ada.shape[2]
    xi, yi, ci = lax.axis_index("x"), lax.axis_index("y"), lax.axis_index("c")
    me = 4 * xi + 2 * yi + ci

    c_g, wa_in_g = _all_gather([c, a_w_in[0].astype(BF16)], "gather_c_wa")

    c_all = c_g.reshape(NDEV * nb, d)
    b_cols = lax.dynamic_slice(b_ada, (0, me * ncol_ada), (2, ncol_ada)).reshape(2, 1, ncol_ada)
    mod_part, lbj = _ada_fwd(c_all, w_ada, b_cols, b_lower_bounds)
    mod_all = _all_gather([mod_part], "gather_mod")[0]
    mod_mine = lax.dynamic_slice_in_dim(mod_all, me * nb, nb, axis=2)
    mod_mine = mod_mine.transpose(1, 2, 0, 3).reshape(2, nb, 3, d)
    mod0, mod1 = mod_mine[0], mod_mine[1]

    di = a_w_out.shape[1] * NDEV

    xf = x.reshape(m, d)
    tgt = loss_target.reshape(m, d)
    ng0, ng1 = norm_gain[0:1], norm_gain[1:2]
    ncb = b_w_in.shape[2]
    wb_lo, wb_hi = b_w_in[0][:, :ncb // 2].astype(BF16), b_w_in[0][:, ncb // 2:].astype(BF16)
    h0, h0_t = _prenorm(xf, ng0, mod0, t_seq, "prenorm_a")
    proj_a, half = _mm_in(h0, [wa_in_g], 1, "in_proj_a", comm=_gather_first([a_w_out[0].astype(BF16), wb_lo]))
    bs_t = jnp.pad(a_b_s[0].T, ((0, 0), (0, 128 - SG_GROUPS)))
    ybr_a, (wa_out_g, wb_lo_g, wb_hi_half) = _a_mid_fwd(
        proj_a, a_ln_gain, a_ln_bias, a_w_s[0], bs_t, t_seq, comm=_join(_gather_second(half), _gather_first([wb_hi])))
    wa_out = wa_out_g.reshape(di, d)
    (yout_a, x1), (wb_hi_g, wb_out_half) = _out_proj(
        ybr_a, wa_out, xf, mod0, t_seq, "out_proj_a",
        comm=_join(_gather_second([wb_hi_half]), _gather_first([b_w_out[0].astype(BF16)])))
    wb_in_g = [wb_lo_g, wb_hi_g]
    h1, h1_t = _prenorm(x1, ng1, mod1, t_seq, "prenorm_b")
    proj_b, (wb_out_g,) = _mm_in(h1, wb_in_g, 4, "in_proj_b", comm=_gather_second([wb_out_half]))
    wb_out = wb_out_g.reshape(di, d)
    o_b, ybr_b, states = _hgrn_fwd(proj_b, lbj, b_gn_gain, nb, t_seq)
    yout_b, x2 = _out_proj(ybr_b, wb_out, x1, mod1, t_seq, "out_proj_b")
    dx2, loss_part, d_final_gain = _final_loss(x2, final_gain.reshape(1, d), tgt)

    rows_out = a_w_out.shape[1]
    dy_b, dgate1 = _gate_bwd(dx2, yout_b, mod1, t_seq, "gate_bwd_b")
    dybr_b = _mm_dybr(dy_b, wb_out, "dybr_b")
    rs_wb_out = _ReduceScatter(_mm_dw_out(ybr_b, dy_b, "dw_out_b").reshape(NDEV, rows_out, d), "b_w_out")
    (dproj_b, d_lb, d_gn), got = _hgrn_bwd(proj_b, o_b, dybr_b, states, lbj, b_gn_gain, nb, t_seq,
                                           comm=rs_wb_out.swap_core())
    rs_wb_out.after_core(got[0])
    dh1, got = _mm_din(dproj_b, wb_in_g, 4, "dh_b", comm=rs_wb_out.swap_chips())
    rs_wb_out.after_chips(got[0])
    dx1, dss1, dgain1 = _prenorm_bwd(dh1, x1, ng1, mod1, dx2, t_seq, "prenorm_bwd_b")
    rs_wb_in = _ReduceScatter(_mm_dw_in(h1_t, dproj_b, ncb, 4, "dw_in_b"), "b_w_in")

    dy_a, dgate0 = _gate_bwd(dx1, yout_a, mod0, t_seq, "gate_bwd_a")
    dybr_a = _mm_dybr(dy_a, wa_out, "dybr_a")
    g_wa_out, got = _mm_dw_out(ybr_a, dy_a, "dw_out_a", comm=rs_wb_in.swap_core())
    rs_wb_in.after_core(got[0])
    rs_wa_out = _ReduceScatter(g_wa_out.reshape(NDEV, rows_out, d), "a_w_out")
    (dproj_a, d_lng, d_lnb, d_ws, d_bs_t), got = _a_mid_bwd(
        proj_a, dybr_a, a_ln_gain, a_ln_bias, a_w_s[0], bs_t, t_seq,
        comm=_join(rs_wb_in.swap_chips(), rs_wa_out.swap_core()))
    rs_wb_in.after_chips(got[0])
    rs_wa_out.after_core(got[1])
    part = dict(a_ln_gain=d_lng, a_ln_bias=d_lnb, a_w_s=d_ws[None], a_b_s=d_bs_t[:, :SG_GROUPS].T[None],
                b_lower_bounds=jnp.concatenate([-d_lb, d_lb], axis=0), b_gn_gain=d_gn)
    early_pack = _pack([part[k].reshape(w[k].shape) for k in _EARLY])
    g_wa_in, got = _mm_dw_in(h0_t, dproj_a, wa_in_g.shape[2], 1, "dw_in_a",
                             comm=_join(rs_wa_out.swap_chips(), _gather_first([early_pack])))
    rs_wa_out.after_chips(got[0])
    rs_wa_in = _ReduceScatter(g_wa_in, "a_w_in")
    n_tiles = m // _din_tile(m)
    assert n_tiles >= 2
    first_tiles = max(1, (3 * n_tiles) // 8)
    dh0, got2 = _mm_din(dproj_a, [wa_in_g], 1, "dh_a_first", tiles=(0, first_tiles),
                        comm=_join(rs_wa_in.swap_core(), _gather_second([got[1]])))
    rs_wa_in.after_core(got2[0])
    early_all = got2[1]
    dh0, got = _mm_din(dproj_a, [wa_in_g], 1, "dh_a_rest", comm=rs_wa_in.swap_chips(),
                       tiles=(first_tiles, n_tiles - first_tiles), prev=dh0)
    rs_wa_in.after_chips(got[0])
    dx0, dss0, dgain0 = _prenorm_bwd(dh0, xf, ng0, mod0, dx1, t_seq, "prenorm_bwd_a")
    grad_x = dx0.reshape(nb, t_seq, d)

    dmod = jnp.stack([jnp.concatenate([dss0, dgate0], axis=1), jnp.concatenate([dss1, dgate1], axis=1)])
    late_like = [norm_gain, final_gain, loss_part.reshape(1)]
    late_pack = _pack([jnp.concatenate([dgain0, dgain1], axis=0), d_final_gain[0], loss_part.reshape(1)])
    dmod_all, late_all = _all_gather([dmod.reshape(2, nb, 3 * d), late_pack], "gather_tail")
    dmod_all = dmod_all.transpose(1, 0, 2, 3).reshape(2, NDEV * nb, 3 * d)
    dmod_cols = lax.dynamic_slice_in_dim(dmod_all, me * ncol_ada, ncol_ada, axis=2)
    g_w_ada, g_b_ada = _ada_bwd(c_all, dmod_cols, dmod_all)

    res = {}
    early_like = [w[k] for k in _EARLY]
    sm = _adamw([early_all[k] for k in range(NDEV)], _pack(early_like), _pack([mo[k] for k in _EARLY]),
                _pack([vo[k] for k in _EARLY]), "adamw_small_early")
    sm = [dict(zip(_EARLY, _unpack(buf, early_like))) for buf in sm]
    for k in _EARLY:
        res[k] = tuple(s[k] for s in sm)
    zero = jnp.zeros((1,), F32)
    sm = _adamw([late_all[k] for k in range(NDEV)], _pack([norm_gain, final_gain, zero]),
                _pack([mo["norm_gain"], mo["final_gain"], zero]), _pack([vo["norm_gain"], vo["final_gain"], zero]),
                "adamw_small_late")
    sm = [_unpack(buf, late_like) for buf in sm]
    res["norm_gain"] = tuple(s[0] for s in sm)
    res["final_gain"] = tuple(s[1] for s in sm)
    loss = sm[0][2][0]
    rb = _adamw([g_b_ada], b_ada, mo["b_ada"], vo["b_ada"], "adamw_b_ada")
    res["b_ada"] = tuple(rb)
    sh = w_ada.shape
    ra = _adamw([g_w_ada.reshape(sh[0] * sh[1], sh[2])], w_ada.reshape(sh[0] * sh[1], sh[2]),
                mo["w_ada"].reshape(sh[0] * sh[1], sh[2]), vo["w_ada"].reshape(sh[0] * sh[1], sh[2]), "adamw_w_ada")
    res["w_ada"] = tuple(z.reshape(sh) for z in ra)

    for k, rs in (("b_w_out", rs_wb_out), ("b_w_in", rs_wb_in), ("a_w_out", rs_wa_out), ("a_w_in", rs_wa_in)):
        res[k] = tuple(z[None] for z in _adamw(rs.parts, w[k][0], mo[k][0], vo[k][0], "adamw_" + k))

    order = ["norm_gain", "w_ada", "b_ada", "a_w_in", "a_ln_gain", "a_ln_bias", "a_w_s", "a_b_s", "a_w_out",
             "b_w_in", "b_lower_bounds", "b_gn_gain", "b_w_out", "final_gain"]
    return (loss, grad_x, *[res[k][0] for k in order], *[res[k][1] for k in order],
            *[res[k][2] for k in order], *[res[k][3] for k in order])
```

```python
import functools
import math

import jax
import jax.numpy as jnp
from jax import lax
from jax.experimental import pallas as pl
from jax.experimental.pallas import tpu as pltpu

F32 = jnp.float32
BF16 = jnp.bfloat16
MESH = pl.DeviceIdType.MESH
NDEV = 8
EPS = 1e-6
CHUNK = 64
SG_BLOCK = 128
SG_GROUPS = 8
HEAD_DIM = 128
CUM_ROWS = 256
ADAM_LR, ADAM_B1, ADAM_B2, ADAM_EPS, ADAM_WD, ADAM_STEP = 0.001, 0.9, 0.999, 1e-08, 0.01, 10
VMEM_LIMIT = 56 * 1024 * 1024
ANY = pl.BlockSpec(memory_space=pl.ANY)


class _Hosted:
    def __init__(self, arrays, out_shapes, nsem, start, finish, aliases=None):
        self.arrays, self.out_shapes, self.nsem = list(arrays), list(out_shapes), nsem
        self.start, self.finish = start, finish
        self.aliases = dict(aliases or {})


def _join(*comms):
    arrays, outs, aliases, offs, nsem = [], [], {}, [], 0
    for cm in comms:
        offs.append((len(arrays), len(outs), nsem))
        for i, o in cm.aliases.items():
            aliases[len(arrays) + i] = len(outs) + o
        arrays += cm.arrays
        outs += cm.out_shapes
        nsem += cm.nsem

    def run(which):
        def f(ins, outs_, ss, rs, base):
            for cm, (ia, io, isem) in zip(comms, offs):
                getattr(cm, which)(ins[ia:ia + len(cm.arrays)], outs_[io:io + len(cm.out_shapes)], ss, rs, base + isem)
        return f

    return _Hosted(arrays, outs, nsem, run("start"), run("finish"), aliases)


def _pc(body, *, name, out_shape, grid=None, in_specs=None, out_specs=None, scratch=(), sem=None,
        grid_spec=None, comm=None, aliases=None):
    cp = dict(vmem_limit_bytes=VMEM_LIMIT)
    aliases = dict(aliases or {})
    if comm is None:
        if sem is not None:
            cp["dimension_semantics"] = sem
        kw = {"input_output_aliases": aliases}
        if grid_spec is not None:
            kw["grid_spec"] = grid_spec
        else:
            if grid is not None:
                kw["grid"] = grid
            if in_specs is not None:
                kw["in_specs"] = in_specs
            if out_specs is not None:
                kw["out_specs"] = out_specs
            kw["scratch_shapes"] = list(scratch)
        return pl.pallas_call(functools.partial(body), name=name, out_shape=out_shape,
                              compiler_params=pltpu.CompilerParams(**cp), **kw)

    single = not isinstance(out_shape, (list, tuple))
    outs_list = [out_shape] if single else list(out_shape)
    ospecs = [out_specs] if single else list(out_specs)
    n_in, n_out, n_ci, n_co, n_scr = len(in_specs), len(outs_list), len(comm.arrays), len(comm.out_shapes), len(scratch)
    cp["dimension_semantics"] = ("arbitrary",) * len(grid)

    def hosted(*refs):
        cin, hin = refs[:n_in], refs[n_in:n_in + n_ci]
        cout = refs[n_in + n_ci:n_in + n_ci + n_out]
        hout = refs[n_in + n_ci + n_out:n_in + n_ci + n_out + n_co]
        scr = refs[n_in + n_ci + n_out + n_co:n_in + n_ci + n_out + n_co + n_scr]
        ssem, rsem = refs[-2], refs[-1]
        first = functools.reduce(lambda p, q: p & q, [pl.program_id(a) == 0 for a in range(len(grid))])
        last = functools.reduce(lambda p, q: p & q, [pl.program_id(a) == grid[a] - 1 for a in range(len(grid))])

        @pl.when(first)
        def _():
            comm.start(hin, hout, ssem, rsem, 0)

        body(*cin, *cout, *scr)

        @pl.when(last)
        def _():
            comm.finish(hin, hout, ssem, rsem, 0)

    call = pl.pallas_call(
        hosted, name=name, grid=grid, in_specs=list(in_specs) + [ANY] * n_ci, out_specs=ospecs + [ANY] * n_co,
        out_shape=outs_list + comm.out_shapes,
        scratch_shapes=list(scratch) + [pltpu.SemaphoreType.DMA((comm.nsem,)), pltpu.SemaphoreType.DMA((comm.nsem,))],
        input_output_aliases={**aliases, **{n_in + i: n_out + o for i, o in comm.aliases.items()}},
        compiler_params=pltpu.CompilerParams(**cp))

    def run(*args):
        res = call(*args, *comm.arrays)
        comp = res[:n_out]
        return (comp[0] if single else comp), list(res[n_out:])

    return run


def _tile(n, pref):
    return pref if n % pref == 0 else n


def _sigmoid(x):
    return 1.0 / (1.0 + jnp.exp(-x))


def _gelu(x):
    c = math.sqrt(2.0 / math.pi)
    return 0.5 * x * (1.0 + jnp.tanh(c * (x + 0.044715 * (x * x * x))))


def _gelu_and_grad(x):
    c = math.sqrt(2.0 / math.pi)
    x2 = x * x
    t = jnp.tanh(c * (x + 0.044715 * (x2 * x)))
    half = 0.5 * (1.0 + t)
    return x * half, half + (0.5 * x) * (1.0 - t * t) * (c + (3.0 * 0.044715 * c) * x2)


def _dot(a, b):
    return jnp.dot(a, b, preferred_element_type=F32)


def _dot_nt(a, b):
    return lax.dot_general(a, b, (((1,), (1,)), ((), ())), preferred_element_type=F32)


def _dot_tn(a, b):
    return lax.dot_general(a, b, (((0,), (0,)), ((), ())), preferred_element_type=F32)


def _tri_mask(n, reverse):
    r = lax.broadcasted_iota(jnp.int32, (n, n), 0)
    c = lax.broadcasted_iota(jnp.int32, (n, n), 1)
    same = (r // CHUNK) == (c // CHUNK)
    tri = (c >= r) if reverse else (c <= r)
    return jnp.where(same & tri, 1.0, 0.0).astype(BF16)


def _tri_apply(tri, x):
    hi = x.astype(BF16)
    r1 = x - hi.astype(F32)
    mid = r1.astype(BF16)
    lo = (r1 - mid.astype(F32)).astype(BF16)
    return _dot(tri, hi) + (_dot(tri, mid) + _dot(tri, lo))


def _all_gather(arrs, name):
    n = len(arrs)

    def body(*refs):
        ins, outs = refs[:n], refs[n:2 * n]
        send_sems, recv_sems, local_sems = refs[2 * n:]
        x, y, c = lax.axis_index("x"), lax.axis_index("y"), lax.axis_index("c")
        me, sibling = (x, y, c), (x, y, 1 - c)
        chips = [(1 - x, y), (x, 1 - y), (1 - x, 1 - y)]

        def blk(a, p):
            return outs[a].at[4 * p[0] + 2 * p[1] + p[2]]

        def copy(a, k, block, to, src=None):
            return pltpu.make_async_remote_copy(
                src_ref=blk(a, block) if src is None else src, dst_ref=blk(a, block),
                send_sem=send_sems.at[7 * a + k], recv_sem=recv_sems.at[7 * a + k],
                device_id=to, device_id_type=MESH)

        mine = [pltpu.make_async_copy(ins[a], blk(a, me), local_sems.at[a]) for a in range(n)]
        for m in mine:
            m.start()
        first = []
        for a in range(n):
            first.append(copy(a, 0, me, sibling, src=ins[a]))
            for j, chip in enumerate(chips):
                first.append(copy(a, 1 + j, me, (*chip, c), src=ins[a]))
        for cp in first:
            cp.start()
        passed = []
        for j, chip in enumerate(chips):
            for a in range(n):
                copy(a, 1 + j, (*chip, c), me).wait_recv()
                p = copy(a, 4 + j, (*chip, c), sibling)
                p.start()
                passed.append(p)
        for a in range(n):
            copy(a, 0, sibling, me).wait_recv()
            for j, chip in enumerate(chips):
                copy(a, 4 + j, (*chip, 1 - c), me).wait_recv()
        for cp in first + passed:
            cp.wait_send()
        for m in mine:
            m.wait()

    out_shape = [jax.ShapeDtypeStruct((NDEV,) + a.shape, a.dtype) for a in arrs]
    return _pc(body, name=name, out_shape=out_shape, in_specs=[ANY] * n, out_specs=[ANY] * n,
               scratch=[pltpu.SemaphoreType.DMA((7 * n,)), pltpu.SemaphoreType.DMA((7 * n,)),
                        pltpu.SemaphoreType.DMA((n,))])(*arrs)


def _gather_first(arrs):
    n = len(arrs)

    def parts(ins, outs, ss, rs, base):
        x, y, c = lax.axis_index("x"), lax.axis_index("y"), lax.axis_index("c")
        me, sibling = (x, y, c), (x, y, 1 - c)
        chips = [(1 - x, y), (x, 1 - y), (1 - x, 1 - y)]

        def blk(a, p):
            return outs[a].at[4 * p[0] + 2 * p[1] + p[2]]

        def copy(a, k, block, to):
            return pltpu.make_async_remote_copy(
                src_ref=ins[a], dst_ref=blk(a, block), send_sem=ss.at[base + 4 * a + k],
                recv_sem=rs.at[base + 4 * a + k], device_id=to, device_id_type=MESH)

        local = [pltpu.make_async_copy(ins[a], blk(a, me), ss.at[base + 4 * n + a]) for a in range(n)]
        sends, recvs = [], []
        for a in range(n):
            sends.append(copy(a, 0, me, sibling))
            recvs.append(copy(a, 0, sibling, me))
            for j, chip in enumerate(chips):
                sends.append(copy(a, 1 + j, me, (*chip, c)))
                recvs.append(copy(a, 1 + j, (*chip, c), me))
        return local, sends, recvs

    def start(ins, outs, ss, rs, base):
        local, sends, _ = parts(ins, outs, ss, rs, base)
        for cp in local + sends:
            cp.start()

    def finish(ins, outs, ss, rs, base):
        local, sends, recvs = parts(ins, outs, ss, rs, base)
        for cp in recvs:
            cp.wait_recv()
        for cp in sends:
            cp.wait_send()
        for cp in local:
            cp.wait()

    return _Hosted(arrs, [jax.ShapeDtypeStruct((NDEV,) + a.shape, a.dtype) for a in arrs], 5 * n, start, finish)


def _gather_second(bufs):
    n = len(bufs)

    def parts(ins, outs, ss, rs, base):
        x, y, c = lax.axis_index("x"), lax.axis_index("y"), lax.axis_index("c")
        sibling = (x, y, 1 - c)
        chips = [(1 - x, y), (x, 1 - y), (1 - x, 1 - y)]
        sends, recvs = [], []
        for a in range(n):
            for j, chip in enumerate(chips):
                mine = 4 * chip[0] + 2 * chip[1] + c
                theirs = 4 * chip[0] + 2 * chip[1] + (1 - c)
                sends.append(pltpu.make_async_remote_copy(
                    src_ref=ins[a].at[mine], dst_ref=outs[a].at[mine], send_sem=ss.at[base + 3 * a + j],
                    recv_sem=rs.at[base + 3 * a + j], device_id=sibling, device_id_type=MESH))
                recvs.append(pltpu.make_async_remote_copy(
                    src_ref=ins[a].at[theirs], dst_ref=outs[a].at[theirs], send_sem=ss.at[base + 3 * a + j],
                    recv_sem=rs.at[base + 3 * a + j], device_id=sibling, device_id_type=MESH))
        return sends, recvs

    def start(ins, outs, ss, rs, base):
        for cp in parts(ins, outs, ss, rs, base)[0]:
            cp.start()

    def finish(ins, outs, ss, rs, base):
        sends, recvs = parts(ins, outs, ss, rs, base)
        for cp in recvs:
            cp.wait_recv()
        for cp in sends:
            cp.wait_send()

    return _Hosted(bufs, [jax.ShapeDtypeStruct(b.shape, b.dtype) for b in bufs], 3 * n, start, finish,
                   aliases={a: a for a in range(n)})


def _swap(src, nblk, ids_fn, partner_fn):
    def copies(ins, outs, ss, rs, base):
        x, y, c = lax.axis_index("x"), lax.axis_index("y"), lax.axis_index("c")
        ids = ids_fn(x, y, c)
        partner = partner_fn(x, y, c)
        return [pltpu.make_async_remote_copy(
            src_ref=ins[0].at[ids[k]], dst_ref=outs[0].at[k], send_sem=ss.at[base + k], recv_sem=rs.at[base + k],
            device_id=partner, device_id_type=MESH) for k in range(nblk)]

    def start(ins, outs, ss, rs, base):
        for cp in copies(ins, outs, ss, rs, base):
            cp.start()

    def finish(ins, outs, ss, rs, base):
        for cp in copies(ins, outs, ss, rs, base):
            cp.wait()

    return _Hosted([src], [jax.ShapeDtypeStruct((nblk,) + src.shape[1:], src.dtype)], nblk, start, finish)


def _blocking(comm, name):
    n_i, n_o = len(comm.arrays), len(comm.out_shapes)

    def body(*refs):
        ins, outs = refs[:n_i], refs[n_i:n_i + n_o]
        comm.start(ins, outs, refs[-2], refs[-1], 0)
        comm.finish(ins, outs, refs[-2], refs[-1], 0)

    return pl.pallas_call(
        body, name=name, out_shape=comm.out_shapes, in_specs=[ANY] * n_i, out_specs=[ANY] * n_o,
        scratch_shapes=[pltpu.SemaphoreType.DMA((comm.nsem,)), pltpu.SemaphoreType.DMA((comm.nsem,))],
        input_output_aliases=comm.aliases)(*comm.arrays)


def _swap_chips(send):
    def copies(ins, outs, ss, rs, base):
        x, y, c = lax.axis_index("x"), lax.axis_index("y"), lax.axis_index("c")
        chips = [(1 - x, y), (x, 1 - y), (1 - x, 1 - y)]
        return [pltpu.make_async_remote_copy(
            src_ref=ins[0].at[j], dst_ref=outs[0].at[j], send_sem=ss.at[base + j], recv_sem=rs.at[base + j],
            device_id=(*chip, c), device_id_type=MESH) for j, chip in enumerate(chips)]

    def start(ins, outs, ss, rs, base):
        for cp in copies(ins, outs, ss, rs, base):
            cp.start()

    def finish(ins, outs, ss, rs, base):
        for cp in copies(ins, outs, ss, rs, base):
            cp.wait()

    return _Hosted([send], [jax.ShapeDtypeStruct(send.shape, send.dtype)], 3, start, finish)


def _add_send(a, b, idx, ns, name):
    _, r, c = a.shape
    tr = _tile(r, 256)

    def body(idx_ref, a_ref, b_ref, send_ref):
        send_ref[...] = (a_ref[...] + b_ref[...]).astype(BF16)

    def sel(off):
        return pl.BlockSpec((None, tr, c), lambda k, i, s: (s[off + k], i, 0))

    gs = pltpu.PrefetchScalarGridSpec(num_scalar_prefetch=1, grid=(ns, r // tr), in_specs=[sel(0), sel(ns)],
                                      out_specs=pl.BlockSpec((None, tr, c), lambda k, i, s: (k, i, 0)))
    return _pc(body, name=name, grid_spec=gs, sem=("arbitrary", "arbitrary"),
               out_shape=jax.ShapeDtypeStruct((ns, r, c), BF16))(idx, a, b)


class _ReduceScatter:
    def __init__(self, g, tag):
        self.g, self.tag = g, tag

    def swap_core(self):
        return _swap(self.g, 4, lambda x, y, c: [1 - c, 3 - c, 5 - c, 7 - c], lambda x, y, c: (x, y, 1 - c))

    def after_core(self, recv):
        x, y, c = lax.axis_index("x"), lax.axis_index("y"), lax.axis_index("c")
        chips = [(1 - x, y), (x, 1 - y), (1 - x, 1 - y)]
        idx = jnp.stack([4 * p + 2 * q + c for p, q in chips] + [2 * p + q for p, q in chips]).astype(jnp.int32)
        self.send = _add_send(self.g, recv, idx, 3, "rs_add_" + self.tag)
        self.mine = [lax.dynamic_index_in_dim(self.g, 4 * x + 2 * y + c, 0, keepdims=False),
                     lax.dynamic_index_in_dim(recv, 2 * x + y, 0, keepdims=False)]

    def swap_chips(self):
        return _swap_chips(self.send)

    def after_chips(self, recv):
        self.parts = self.mine + [recv[0], recv[1], recv[2]]


def _ada_fwd(c_all, w_ada, b_cols, b_lb):
    nl, d, ncol = w_ada.shape
    nseq = c_all.shape[0]
    di = b_lb.shape[1]

    def body(c_ref, w_ref, b_ref, lb_ref, mod_ref, lbj_ref):
        cv = c_ref[...]
        cact = (cv * _sigmoid(cv)).astype(BF16)
        for l in range(nl):
            mod_ref[l] = _dot(cact, w_ref[l].astype(BF16)) + b_ref[l]
        b0, b1 = lb_ref[0:1, :], lb_ref[1:2, :]
        mx = jnp.maximum(b0, b1)
        e0, e1 = jnp.exp(b0 - mx), jnp.exp(b1 - mx)
        s = e0 + e1
        p0, p1 = e0 / s, e1 / s
        lbj_ref[0:1, :] = (p0 + p1) - p0
        lbj_ref[1:2, :] = p0 * p1

    return _pc(body, name="ada_fwd",
               out_shape=[jax.ShapeDtypeStruct((nl, nseq, ncol), F32), jax.ShapeDtypeStruct((2, di), F32)]
               )(c_all, w_ada, b_cols, b_lb)


def _ada_bwd(c_all, dmod_cols, dmod_full):
    nl, nseq, ncol = dmod_cols.shape
    d = c_all.shape[1]
    d3 = dmod_full.shape[2]

    def body(c_ref, dc_ref, df_ref, gw_ref, gb_ref):
        cv = c_ref[...]
        cact = (cv * _sigmoid(cv)).astype(BF16)
        for l in range(nl):
            gw_ref[l] = _dot_tn(cact, dc_ref[l].astype(BF16))
            gb_ref[l:l + 1, :] = jnp.sum(df_ref[l], axis=0, keepdims=True)

    return _pc(body, name="ada_bwd",
               out_shape=[jax.ShapeDtypeStruct((nl, d, ncol), F32), jax.ShapeDtypeStruct((nl, d3), F32)]
               )(c_all, dmod_cols, dmod_full)


def _prenorm(x, gain, mod, t_seq, name):
    m, d = x.shape
    tm = _tile(t_seq, 512)
    per = t_seq // tm

    def body(x_ref, g_ref, mod_ref, h_ref, ht_ref):
        xv = x_ref[...]
        rstd = lax.rsqrt(jnp.mean(xv * xv, axis=-1, keepdims=True) + EPS)
        r = xv * rstd * g_ref[...]
        h = r * (1.0 + mod_ref[0, 1:2, :]) + mod_ref[0, 0:1, :]
        h_ref[...] = h.astype(BF16)
        ht_ref[...] = h.T.astype(BF16)

    return _pc(body, name=name, out_shape=[jax.ShapeDtypeStruct((m, d), BF16), jax.ShapeDtypeStruct((d, m), BF16)],
               grid=(m // tm,),
               in_specs=[pl.BlockSpec((tm, d), lambda i: (i, 0)), pl.BlockSpec((1, d), lambda i: (0, 0)),
                         pl.BlockSpec((1, 3, d), lambda i: (i // per, 0, 0))],
               out_specs=[pl.BlockSpec((tm, d), lambda i: (i, 0)), pl.BlockSpec((d, tm), lambda i: (0, i))],
               sem=("parallel",))(x, gain, mod)


def _prenorm_bwd(dh, x, gain, mod, dxn, t_seq, name, comm=None):
    m, d = x.shape
    nb = m // t_seq
    tm = _tile(t_seq, 512)
    per = t_seq // tm

    def body(dh_ref, x_ref, g_ref, mod_ref, dxn_ref, dx_ref, dss_ref, dg_ref):
        i = pl.program_id(0)
        xv, dhv, g = x_ref[...], dh_ref[...], g_ref[...]
        rstd = lax.rsqrt(jnp.mean(xv * xv, axis=-1, keepdims=True) + EPS)
        xhat = xv * rstd
        dr = dhv * (1.0 + mod_ref[0, 1:2, :])
        dxhat = dr * g
        dx_ref[...] = dxn_ref[...] + rstd * (dxhat - xhat * jnp.mean(dxhat * xhat, axis=-1, keepdims=True))

        @pl.when(i % per == 0)
        def _():
            dss_ref[...] = jnp.zeros_like(dss_ref)

        @pl.when(i == 0)
        def _():
            dg_ref[...] = jnp.zeros_like(dg_ref)

        dss_ref[0, 0:1, :] += jnp.sum(dhv, axis=0, keepdims=True)
        dss_ref[0, 1:2, :] += jnp.sum(dhv * (xhat * g), axis=0, keepdims=True)
        dg_ref[...] += jnp.sum(dr * xhat, axis=0, keepdims=True)

    row = pl.BlockSpec((tm, d), lambda i: (i, 0))
    return _pc(body, name=name,
               out_shape=[jax.ShapeDtypeStruct((m, d), F32), jax.ShapeDtypeStruct((nb, 2, d), F32),
                          jax.ShapeDtypeStruct((1, d), F32)],
               grid=(m // tm,),
               in_specs=[row, row, pl.BlockSpec((1, d), lambda i: (0, 0)),
                         pl.BlockSpec((1, 3, d), lambda i: (i // per, 0, 0)), row],
               out_specs=[row, pl.BlockSpec((1, 2, d), lambda i: (i // per, 0, 0)),
                          pl.BlockSpec((1, d), lambda i: (0, 0))],
               sem=("arbitrary",), comm=comm)(dh, x, gain, mod, dxn)


def _mm_in(h, ws, sections, name, comm=None):
    m, k = h.shape
    nw, ncp = len(ws), ws[0].shape[2]
    nc = nw * ncp
    per = NDEV // sections if sections > 1 else NDEV
    tm = _tile(m, 512)
    assert per % 2 == 0

    def body(*refs):
        hv = refs[0][...]
        o_ref = refs[1 + nw]
        for b in range(2):
            for a in range(nw):
                lo = b * nc + a * ncp
                o_ref[:, lo:lo + ncp] = _dot(hv, refs[1 + a][b])

    w_spec = pl.BlockSpec((2, k, ncp), lambda j, i: (j, 0, 0))
    if sections > 1:
        out_shape = jax.ShapeDtypeStruct((sections, m, per * nc), F32)
        out_spec = pl.BlockSpec((None, tm, 2 * nc), lambda j, i: ((2 * j) // per, i, ((2 * j) % per) // 2))
    else:
        out_shape = jax.ShapeDtypeStruct((m, NDEV * nc), F32)
        out_spec = pl.BlockSpec((tm, 2 * nc), lambda j, i: (i, j))
    return _pc(body, name=name, out_shape=out_shape, grid=(NDEV // 2, m // tm),
               in_specs=[pl.BlockSpec((tm, k), lambda j, i: (i, 0))] + [w_spec] * nw,
               out_specs=out_spec, sem=("parallel", "parallel"), comm=comm)(h, *ws)


def _din_tile(m):
    return 1024 if m % 1024 == 0 and m >= 2048 else _tile(m, 512)


def _mm_din(dproj, ws, sections, name, comm=None, tiles=None, prev=None):
    nw, k, ncp = len(ws), ws[0].shape[1], ws[0].shape[2]
    nc = nw * ncp
    m = dproj.shape[-2]
    tm = _din_tile(m)
    t0, nt = tiles if tiles is not None else (0, m // tm)
    per = NDEV // sections if sections > 1 else NDEV

    def body(*refs):
        d_ref, o_ref = refs[0], refs[-1]
        j = pl.program_id(1)
        acc = _dot_nt(d_ref[:, :ncp], refs[1][...])
        for a in range(1, nw):
            acc = acc + _dot_nt(d_ref[:, a * ncp:(a + 1) * ncp], refs[1 + a][...])

        @pl.when(j == 0)
        def _():
            o_ref[...] = acc

        @pl.when(j > 0)
        def _():
            o_ref[...] += acc

    if sections > 1:
        dspec = pl.BlockSpec((None, tm, nc), lambda i, j: (j // per, i + t0, j % per))
    else:
        dspec = pl.BlockSpec((tm, nc), lambda i, j: (i + t0, j))
    in_specs = [dspec] + [pl.BlockSpec((None, k, ncp), lambda i, j: (j, 0, 0))] * nw
    args = [dproj, *ws]
    if prev is not None:
        in_specs.append(ANY)
        args.append(prev)
    return _pc(body, name=name, out_shape=jax.ShapeDtypeStruct((m, k), F32), grid=(nt, NDEV), in_specs=in_specs,
               out_specs=pl.BlockSpec((tm, k), lambda i, j: (i + t0, 0)), sem=("parallel", "arbitrary"),
               comm=comm, aliases={1 + nw: 0} if prev is not None else None)(*args)


def _mm_dw_in(ht, dproj, nc, sections, name, comm=None):
    k, m = ht.shape
    tk = _din_tile(m)
    per = NDEV // sections if sections > 1 else NDEV

    def body(h_ref, d_ref, o_ref):
        kk = pl.program_id(1)
        acc = _dot(h_ref[...], d_ref[...])

        @pl.when(kk == 0)
        def _():
            o_ref[...] = acc

        @pl.when(kk > 0)
        def _():
            o_ref[...] += acc

    if sections > 1:
        dspec = pl.BlockSpec((None, tk, nc), lambda j, i: (j // per, i, j % per))
    else:
        dspec = pl.BlockSpec((tk, nc), lambda j, i: (i, j))
    return _pc(body, name=name, out_shape=jax.ShapeDtypeStruct((NDEV, k, nc), F32), grid=(NDEV, m // tk),
               in_specs=[pl.BlockSpec((k, tk), lambda j, i: (0, i)), dspec],
               out_specs=pl.BlockSpec((None, k, nc), lambda j, i: (j, 0, 0)),
               sem=("parallel", "arbitrary"), comm=comm)(ht, dproj)


def _out_proj(ybr, w_out, x, mod, t_seq, name, comm=None):
    m, di = ybr.shape
    d = w_out.shape[1]
    tm = _tile(t_seq, 512)
    per = t_seq // tm

    def body(y_ref, w_ref, x_ref, mod_ref, yo_ref, xn_ref):
        yo = _dot(y_ref[...], w_ref[...])
        yo_ref[...] = yo
        xn_ref[...] = x_ref[...] + mod_ref[0, 2:3, :] * yo

    row = pl.BlockSpec((tm, d), lambda i: (i, 0))
    return _pc(body, name=name,
               out_shape=[jax.ShapeDtypeStruct((m, d), F32), jax.ShapeDtypeStruct((m, d), F32)],
               grid=(m // tm,),
               in_specs=[pl.BlockSpec((tm, di), lambda i: (i, 0)), pl.BlockSpec((di, d), lambda i: (0, 0)), row,
                         pl.BlockSpec((1, 3, d), lambda i: (i // per, 0, 0))],
               out_specs=[row, row], sem=("parallel",), comm=comm)(ybr, w_out, x, mod)


def _out_proj_loss(ybr, w_out, x, mod, gain, target, t_seq):
    m, di = ybr.shape
    d = w_out.shape[1]
    tm = _tile(t_seq, 512)
    per = t_seq // tm

    def body(y_ref, w_ref, x_ref, mod_ref, g_ref, t_ref, yo_ref, dx_ref, loss_ref, dg_ref):
        i = pl.program_id(0)
        yo = _dot(y_ref[...], w_ref[...])
        yo_ref[...] = yo
        xv = x_ref[...] + mod_ref[0, 2:3, :] * yo
        g = g_ref[...]
        rstd = lax.rsqrt(jnp.mean(xv * xv, axis=-1, keepdims=True) + EPS)
        xhat = xv * rstd
        err = xhat * g - t_ref[...]
        dy = err * (1.0 / d)
        dxhat = dy * g
        dx_ref[...] = rstd * (dxhat - xhat * jnp.mean(dxhat * xhat, axis=-1, keepdims=True))

        @pl.when(i == 0)
        def _():
            loss_ref[...] = jnp.zeros_like(loss_ref)
            dg_ref[...] = jnp.zeros_like(dg_ref)

        loss_ref[...] += 0.5 * jnp.sum(jnp.mean(err * err, axis=-1, keepdims=True), axis=0, keepdims=True)
        dg_ref[...] += jnp.sum(dy * xhat, axis=0, keepdims=True)

    row = pl.BlockSpec((tm, d), lambda i: (i, 0))
    vec = pl.BlockSpec((1, d), lambda i: (0, 0))
    return _pc(body, name="out_proj_loss",
               out_shape=[jax.ShapeDtypeStruct((m, d), F32), jax.ShapeDtypeStruct((m, d), F32),
                          jax.ShapeDtypeStruct((1, 1), F32), jax.ShapeDtypeStruct((1, d), F32)],
               grid=(m // tm,),
               in_specs=[pl.BlockSpec((tm, di), lambda i: (i, 0)), pl.BlockSpec((di, d), lambda i: (0, 0)), row,
                         pl.BlockSpec((1, 3, d), lambda i: (i // per, 0, 0)), vec, row],
               out_specs=[row, row, pl.BlockSpec((1, 1), lambda i: (0, 0)), vec],
               sem=("arbitrary",))(ybr, w_out, x, mod, gain, target)


def _gate_dybr(dxn, yout, mod, w_out, t_seq, name):
    m, d = dxn.shape
    di = w_out.shape[0]
    nb = m // t_seq
    tm = _tile(t_seq, 512)
    per = t_seq // tm

    def body(dxn_ref, yo_ref, mod_ref, w_ref, dy_ref, dgate_ref, o_ref):
        i = pl.program_id(0)
        dv = dxn_ref[...]
        dy = (mod_ref[0, 2:3, :] * dv).astype(BF16)
        dy_ref[...] = dy
        o_ref[...] = _dot_nt(dy, w_ref[...])

        @pl.when(i % per == 0)
        def _():
            dgate_ref[...] = jnp.zeros_like(dgate_ref)

        dgate_ref[0] += jnp.sum(dv * yo_ref[...], axis=0, keepdims=True)

    row = pl.BlockSpec((tm, d), lambda i: (i, 0))
    return _pc(body, name=name,
               out_shape=[jax.ShapeDtypeStruct((m, d), BF16), jax.ShapeDtypeStruct((nb, 1, d), F32),
                          jax.ShapeDtypeStruct((m, di), F32)],
               grid=(m // tm,),
               in_specs=[row, row, pl.BlockSpec((1, 3, d), lambda i: (i // per, 0, 0)),
                         pl.BlockSpec((di, d), lambda i: (0, 0))],
               out_specs=[row, pl.BlockSpec((1, 1, d), lambda i: (i // per, 0, 0)),
                          pl.BlockSpec((tm, di), lambda i: (i, 0))],
               sem=("arbitrary",))(dxn, yout, mod, w_out)


def _mm_dw_out(ybr, dy, name, comm=None):
    m, di = ybr.shape
    d = dy.shape[1]
    tk = _tile(m, 512)
    tn = _tile(di, 1024)

    def body(y_ref, dy_ref, o_ref):
        kk = pl.program_id(1)
        acc = _dot_tn(y_ref[...], dy_ref[...])

        @pl.when(kk == 0)
        def _():
            o_ref[...] = acc

        @pl.when(kk > 0)
        def _():
            o_ref[...] += acc

    return _pc(body, name=name, out_shape=jax.ShapeDtypeStruct((di, d), F32), grid=(di // tn, m // tk),
               in_specs=[pl.BlockSpec((tk, tn), lambda n, k: (k, n)), pl.BlockSpec((tk, d), lambda n, k: (k, 0))],
               out_specs=pl.BlockSpec((tn, d), lambda n, k: (n, 0)), sem=("parallel", "arbitrary"),
               comm=comm)(ybr, dy)


def _sgu_mask():
    t = lax.broadcasted_iota(jnp.int32, (SG_BLOCK, SG_BLOCK), 0)
    s = lax.broadcasted_iota(jnp.int32, (SG_BLOCK, SG_BLOCK), 1)
    return (s // CHUNK) <= (t // CHUNK)


def _a_mid_fwd(proj, ln_g, ln_b, w_s, bs_t, t_seq, comm=None):
    m, n3 = proj.shape
    di = n3 // 3
    gd = di // SG_GROUPS
    r = _tile(t_seq, 256)
    nblk = r // SG_BLOCK

    def body(p_ref, lg_ref, lb_ref, ws_ref, bs_ref, ybr_ref, s_scr):
        v = _gelu(p_ref[:, di:2 * di])
        mu = jnp.mean(v, axis=-1, keepdims=True)
        vc = v - mu
        rstd = lax.rsqrt(jnp.mean(vc * vc, axis=-1, keepdims=True) + EPS)
        vb = (vc * rstd * lg_ref[...] + lb_ref[...]).astype(BF16)
        mask = _sgu_mask()
        for gi in range(SG_GROUPS):
            ws = jnp.where(mask, ws_ref[gi], 0.0).astype(BF16)
            bcol = bs_ref[:, gi:gi + 1]
            for b in range(nblk):
                rows = slice(b * SG_BLOCK, (b + 1) * SG_BLOCK)
                cols = slice(gi * gd, (gi + 1) * gd)
                s_scr[rows, cols] = _dot(ws, vb[rows, cols]) + bcol
        gg = p_ref[:, 2 * di:]
        ybr_ref[...] = (_gelu(p_ref[:, :di]) * s_scr[...] * (gg * _sigmoid(gg))).astype(BF16)

    vec = pl.BlockSpec((1, di), lambda i: (0, 0))
    return _pc(body, name="a_mid_fwd", out_shape=jax.ShapeDtypeStruct((m, di), BF16), grid=(m // r,),
               in_specs=[pl.BlockSpec((r, n3), lambda i: (i, 0)), vec, vec,
                         pl.BlockSpec((SG_GROUPS, SG_BLOCK, SG_BLOCK), lambda i: (0, 0, 0)),
                         pl.BlockSpec((SG_BLOCK, 128), lambda i: (0, 0))],
               out_specs=pl.BlockSpec((r, di), lambda i: (i, 0)),
               scratch=[pltpu.VMEM((r, di), F32)], sem=("parallel",), comm=comm)(proj, ln_g, ln_b, w_s, bs_t)


def _a_mid_bwd(proj, dybr, ln_g, ln_b, w_s, bs_t, t_seq, comm=None):
    m, n3 = proj.shape
    di = n3 // 3
    gd = di // SG_GROUPS
    r = _tile(t_seq, 256)
    nblk = r // SG_BLOCK

    def body(p_ref, dy_ref, lg_ref, lb_ref, ws_ref, bs_ref,
             dp_ref, dlg_ref, dlb_ref, dws_ref, dbs_ref, s_scr, dvl_scr):
        i = pl.program_id(0)

        @pl.when(i == 0)
        def _():
            dlg_ref[...] = jnp.zeros_like(dlg_ref)
            dlb_ref[...] = jnp.zeros_like(dlb_ref)
            dws_ref[...] = jnp.zeros_like(dws_ref)
            dbs_ref[...] = jnp.zeros_like(dbs_ref)

        v, dgelu_v = _gelu_and_grad(p_ref[:, di:2 * di])
        mu = jnp.mean(v, axis=-1, keepdims=True)
        vc = v - mu
        rstd = lax.rsqrt(jnp.mean(vc * vc, axis=-1, keepdims=True) + EPS)
        vhat = vc * rstd
        lg = lg_ref[...]
        vb = (vhat * lg + lb_ref[...]).astype(BF16)
        u, dgelu_u = _gelu_and_grad(p_ref[:, :di])
        gg = p_ref[:, 2 * di:]
        sg = _sigmoid(gg)
        dyv = dy_ref[...]
        dus = dyv * (gg * sg)
        dsb = (dus * u).astype(BF16)
        ds32 = dus * u
        mask = _sgu_mask()
        lane = lax.broadcasted_iota(jnp.int32, (SG_BLOCK, 128), 1)
        dbs_acc = jnp.zeros((SG_BLOCK, 128), F32)
        for gi in range(SG_GROUPS):
            ws = jnp.where(mask, ws_ref[gi], 0.0).astype(BF16)
            bcol = bs_ref[:, gi:gi + 1]
            cols = slice(gi * gd, (gi + 1) * gd)
            dws_acc = jnp.zeros((SG_BLOCK, SG_BLOCK), F32)
            dbs_col = jnp.zeros((SG_BLOCK, 1), F32)
            for b in range(nblk):
                rows = slice(b * SG_BLOCK, (b + 1) * SG_BLOCK)
                s_scr[rows, cols] = _dot(ws, vb[rows, cols]) + bcol
                dvl_scr[rows, cols] = _dot_tn(ws, dsb[rows, cols])
                dws_acc += _dot_nt(dsb[rows, cols], vb[rows, cols])
                dbs_col += jnp.sum(ds32[rows, cols], axis=-1, keepdims=True)
            dws_ref[gi] += jnp.where(mask, dws_acc, 0.0)
            dbs_acc += jnp.where(lane == gi, dbs_col, 0.0)
        dbs_ref[...] += dbs_acc
        s = s_scr[...]
        dp_ref[:, :di] = (dus * s * dgelu_u).astype(BF16)
        dp_ref[:, 2 * di:] = (dyv * u * s * (sg * (1.0 + gg * (1.0 - sg)))).astype(BF16)
        dvl = dvl_scr[...]
        dlg_ref[...] += jnp.sum(dvl * vhat, axis=0, keepdims=True)
        dlb_ref[...] += jnp.sum(dvl, axis=0, keepdims=True)
        dvh = dvl * lg
        dv = rstd * (dvh - jnp.mean(dvh, axis=-1, keepdims=True)
                     - vhat * jnp.mean(dvh * vhat, axis=-1, keepdims=True))
        dp_ref[:, di:2 * di] = (dv * dgelu_v).astype(BF16)

    vec = pl.BlockSpec((1, di), lambda i: (0, 0))
    wsb = pl.BlockSpec((SG_GROUPS, SG_BLOCK, SG_BLOCK), lambda i: (0, 0, 0))
    bsb = pl.BlockSpec((SG_BLOCK, 128), lambda i: (0, 0))
    return _pc(body, name="a_mid_bwd",
               out_shape=[jax.ShapeDtypeStruct((m, n3), BF16), jax.ShapeDtypeStruct((1, di), F32),
                          jax.ShapeDtypeStruct((1, di), F32),
                          jax.ShapeDtypeStruct((SG_GROUPS, SG_BLOCK, SG_BLOCK), F32),
                          jax.ShapeDtypeStruct((SG_BLOCK, 128), F32)],
               grid=(m // r,),
               in_specs=[pl.BlockSpec((r, n3), lambda i: (i, 0)), pl.BlockSpec((r, di), lambda i: (i, 0)),
                         vec, vec, wsb, bsb],
               out_specs=[pl.BlockSpec((r, n3), lambda i: (i, 0)), vec, vec, wsb, bsb],
               scratch=[pltpu.VMEM((r, di), F32), pltpu.VMEM((r, di), F32)],
               sem=("arbitrary",), comm=comm)(proj, dybr, ln_g, ln_b, w_s, bs_t)


def _hgrn_dims(t_seq, di):
    tr = _tile(t_seq, 256)
    hc = _tile(di, 1024)
    return tr, hc, hc // HEAD_DIM


def _hgrn_gates(f_ref, lb, a_scr, k_scr, tr):
    sig = _sigmoid(f_ref[...])
    fg = lb + (1.0 - lb) * sig
    k_scr[...] = 1.0 - fg
    logf = jnp.log(fg)
    g = min(CUM_ROWS, tr)
    tri = _tri_mask(g, reverse=False)
    for rg in range(tr // g):
        a_scr[rg * g:(rg + 1) * g, :] = _tri_apply(tri, logf[rg * g:(rg + 1) * g, :])
    return sig, fg


def _hgrn_fwd(proj, lbj, gn, nb, t_seq):
    _, m, di = proj.shape
    tr, hc, hpg = _hgrn_dims(t_seq, di)
    nt, nhg, ncl = t_seq // tr, di // hc, tr // CHUNK
    nheads = di // HEAD_DIM

    def body(q_ref, f_ref, i_ref, g_ref, lb_ref, gn_ref, o_ref, ybr_ref, st_ref, st_scr, a_scr, k_scr):
        t = pl.program_id(2)

        @pl.when(t == 0)
        def _():
            st_scr[...] = jnp.zeros_like(st_scr)

        _hgrn_gates(f_ref, lb_ref[0:1, :], a_scr, k_scr, tr)
        gnv = gn_ref[...]
        rr = lax.broadcasted_iota(jnp.int32, (CHUNK, CHUNK), 0)
        cc = lax.broadcasted_iota(jnp.int32, (CHUNK, CHUNK), 1)
        causal = cc <= rr

        def chunk(n, carry):
            rows = pl.ds(pl.multiple_of(n * CHUNK, CHUNK), CHUNK)
            lanes = [slice(hd * HEAD_DIM, (hd + 1) * HEAD_DIM) for hd in range(hpg)]
            hs = []
            for hd, ls in enumerate(lanes):
                h = {}
                ah, kh = a_scr[rows, ls], k_scr[rows, ls]
                qp = q_ref[rows, ls]
                qh = qp * _sigmoid(qp)
                h["vb"] = i_ref[rows, ls].astype(BF16)
                aref, alast = ah[CHUNK // 2 - 1:CHUNK // 2, :], ah[CHUNK - 1:CHUNK, :]
                h["q_in"] = (qh * jnp.exp(ah - aref)).astype(BF16)
                h["k_in"] = (kh * jnp.exp(aref - ah)).astype(BF16)
                h["q_out"] = (qh * jnp.exp(ah)).astype(BF16)
                h["k_out"] = (kh * jnp.exp(alast - ah)).astype(BF16)
                h["dec"] = jnp.exp(alast)
                st = st_scr[hd]
                st_ref[n, hd] = st
                h["st"] = st
                hs.append(h)
            for h in hs:
                h["scores"] = _dot_nt(h["q_in"], h["k_in"])
                h["o_inter"] = _dot_nt(h["q_out"], h["st"].astype(BF16))
                h["st_mm"] = _dot_tn(h["vb"], h["k_out"])
            for h in hs:
                h["o"] = _dot(jnp.where(causal, h["scores"], 0.0).astype(BF16), h["vb"]) + h["o_inter"]
            for hd, (h, ls) in enumerate(zip(hs, lanes)):
                st_scr[hd] = h["st"] * h["dec"] + h["st_mm"]
                o = h["o"]
                o_ref[rows, ls] = o
                rstd = lax.rsqrt(jnp.mean(o * o, axis=-1, keepdims=True) + EPS)
                gg = g_ref[rows, ls]
                ybr_ref[rows, ls] = ((o * rstd * gnv) * (gg * _sigmoid(gg))).astype(BF16)
            return carry

        lax.fori_loop(0, ncl, chunk, 0)

    def sec(s):
        return pl.BlockSpec((None, tr, hc), lambda hg, b, t: (s, b * nt + t, hg))

    blk = pl.BlockSpec((tr, hc), lambda hg, b, t: (b * nt + t, hg))
    return _pc(body, name="hgrn_fwd",
               out_shape=[jax.ShapeDtypeStruct((m, di), F32), jax.ShapeDtypeStruct((m, di), BF16),
                          jax.ShapeDtypeStruct((m // CHUNK, nheads, HEAD_DIM, HEAD_DIM), F32)],
               grid=(nhg, nb, nt),
               in_specs=[sec(0), sec(1), sec(2), sec(3), pl.BlockSpec((2, hc), lambda hg, b, t: (0, hg)),
                         pl.BlockSpec((1, HEAD_DIM), lambda hg, b, t: (0, 0))],
               out_specs=[blk, blk, pl.BlockSpec((ncl, hpg, HEAD_DIM, HEAD_DIM),
                                                 lambda hg, b, t: (b * nt + t, hg, 0, 0))],
               scratch=[pltpu.VMEM((hpg, HEAD_DIM, HEAD_DIM), F32), pltpu.VMEM((tr, hc), F32),
                        pltpu.VMEM((tr, hc), F32)],
               sem=("parallel", "arbitrary", "arbitrary"))(proj, proj, proj, proj, lbj, gn)


def _hgrn_bwd(proj, o_all, dybr, states, lbj, gn, nb, t_seq, comm=None):
    _, m, di = proj.shape
    tr, hc, hpg = _hgrn_dims(t_seq, di)
    nt, nhg, ncl = t_seq // tr, di // hc, tr // CHUNK

    def body(q_ref, f_ref, i_ref, g_ref, o_ref, dy_ref, st_ref, lb_ref, gn_ref,
             dp_ref, dlb_ref, dgn_ref, dst_scr, a_scr, k_scr, da_scr, dk_scr):
        hg, b, t = pl.program_id(0), pl.program_id(1), pl.program_id(2)

        @pl.when(t == 0)
        def _():
            dst_scr[...] = jnp.zeros_like(dst_scr)

        @pl.when((b == 0) & (t == 0))
        def _():
            dlb_ref[...] = jnp.zeros_like(dlb_ref)

        @pl.when((hg == 0) & (b == 0) & (t == 0))
        def _():
            dgn_ref[...] = jnp.zeros_like(dgn_ref)

        lb = lb_ref[0:1, :]
        sig, fg = _hgrn_gates(f_ref, lb, a_scr, k_scr, tr)
        gnv = gn_ref[...]
        rr = lax.broadcasted_iota(jnp.int32, (CHUNK, CHUNK), 0)
        cc = lax.broadcasted_iota(jnp.int32, (CHUNK, CHUNK), 1)
        causal = cc <= rr
        rowi = lax.broadcasted_iota(jnp.int32, (CHUNK, HEAD_DIM), 0)

        def chunk(it, carry):
            n = ncl - 1 - it
            rows = pl.ds(pl.multiple_of(n * CHUNK, CHUNK), CHUNK)
            lanes = [slice(hd * HEAD_DIM, (hd + 1) * HEAD_DIM) for hd in range(hpg)]
            hs = []
            for hd, ls in enumerate(lanes):
                h = {}
                ah, kh = a_scr[rows, ls], k_scr[rows, ls]
                qp = q_ref[rows, ls]
                sq = _sigmoid(qp)
                qh = qp * sq
                h["dsilu_q"] = sq * (1.0 + qp * (1.0 - sq))
                h["vb"] = i_ref[rows, ls].astype(BF16)
                aref, alast = ah[CHUNK // 2 - 1:CHUNK // 2, :], ah[CHUNK - 1:CHUNK, :]
                h["e1"], h["e2"] = jnp.exp(ah - aref), jnp.exp(aref - ah)
                h["e3"], h["e4"] = jnp.exp(ah), jnp.exp(alast - ah)
                h["dec"] = jnp.exp(alast)
                h["q_in"], h["k_in"], h["q_out"], h["k_out"] = qh * h["e1"], kh * h["e2"], qh * h["e3"], kh * h["e4"]
                for nm in ("q_in", "k_in", "q_out", "k_out"):
                    h[nm + "_b"] = h[nm].astype(BF16)
                o = o_ref[rows, ls]
                rstd = lax.rsqrt(jnp.mean(o * o, axis=-1, keepdims=True) + EPS)
                ohat = o * rstd
                gg = g_ref[rows, ls]
                sg = _sigmoid(gg)
                dyv = dy_ref[rows, ls]
                d_on = dyv * (gg * sg)
                dp_ref[3, rows, ls] = (dyv * (ohat * gnv) * (sg * (1.0 + gg * (1.0 - sg)))).astype(BF16)
                h["dgn"] = jnp.sum(d_on * ohat, axis=0, keepdims=True)
                dohat = d_on * gnv
                do = rstd * (dohat - ohat * jnp.mean(dohat * ohat, axis=-1, keepdims=True))
                h["do_b"] = do.astype(BF16)
                h["st_prev"] = st_ref[n, hd]
                h["dst"] = dst_scr[hd]
                hs.append(h)
            for h in hs:
                dst_b = h["dst"].astype(BF16)
                h["scores"] = _dot_nt(h["q_in_b"], h["k_in_b"])
                h["dscores"] = _dot_nt(h["do_b"], h["vb"])
                h["dv_inter"] = _dot_nt(h["k_out_b"], dst_b)
                h["dq_out"] = _dot(h["do_b"], h["st_prev"].astype(BF16))
                h["dk_out"] = _dot(h["vb"], dst_b)
                h["dst_mm"] = _dot_tn(h["do_b"], h["q_out_b"])
            for h in hs:
                scores = jnp.where(causal, h["scores"], 0.0).astype(BF16)
                dscores = jnp.where(causal, h["dscores"], 0.0).astype(BF16)
                h["dv"] = _dot_tn(scores, h["do_b"]) + h["dv_inter"]
                h["dq_in"] = _dot(dscores, h["k_in_b"])
                h["dk_in"] = _dot_tn(dscores, h["q_in_b"])
            dgn = hs[0]["dgn"]
            for h in hs[1:]:
                dgn = dgn + h["dgn"]
            dgn_ref[...] += dgn
            for hd, (h, ls) in enumerate(zip(hs, lanes)):
                ddec = jnp.sum(h["dst"] * h["st_prev"], axis=0, keepdims=True)
                dst_scr[hd] = h["dst"] * h["dec"] + h["dst_mm"]
                dp_ref[2, rows, ls] = h["dv"].astype(BF16)
                dq = h["dq_in"] * h["e1"] + h["dq_out"] * h["e3"]
                dp_ref[0, rows, ls] = (dq * h["dsilu_q"]).astype(BF16)
                dk_scr[rows, ls] = h["dk_in"] * h["e2"] + h["dk_out"] * h["e4"]
                t_in = h["dq_in"] * h["q_in"] - h["dk_in"] * h["k_in"]
                t_out = h["dk_out"] * h["k_out"]
                da = t_in + h["dq_out"] * h["q_out"] - t_out
                da_ref_row = -jnp.sum(t_in, axis=0, keepdims=True)
                da_last_row = jnp.sum(t_out, axis=0, keepdims=True) + ddec * h["dec"]
                da = da + jnp.where(rowi == CHUNK // 2 - 1, da_ref_row, 0.0) \
                        + jnp.where(rowi == CHUNK - 1, da_last_row, 0.0)
                da_scr[rows, ls] = da
            return carry

        lax.fori_loop(0, ncl, chunk, 0)
        g = min(CUM_ROWS, tr)
        tri = _tri_mask(g, reverse=True)
        for rg in range(tr // g):
            rs = slice(rg * g, (rg + 1) * g)
            dlogf = _tri_apply(tri, da_scr[rs, :])
            df = dlogf / fg[rs, :] - dk_scr[rs, :]
            sgr = sig[rs, :]
            dp_ref[1, rs, :] = (df * (1.0 - lb) * (sgr * (1.0 - sgr))).astype(BF16)
            dlb_ref[...] += jnp.sum(df * (1.0 - sgr), axis=0, keepdims=True) * lb_ref[1:2, :]

    def sec(s):
        return pl.BlockSpec((None, tr, hc), lambda hg, b, t: (s, b * nt + (nt - 1 - t), hg))

    blk = pl.BlockSpec((tr, hc), lambda hg, b, t: (b * nt + (nt - 1 - t), hg))
    return _pc(body, name="hgrn_bwd",
               out_shape=[jax.ShapeDtypeStruct((4, m, di), BF16), jax.ShapeDtypeStruct((1, di), F32),
                          jax.ShapeDtypeStruct((1, HEAD_DIM), F32)],
               grid=(nhg, nb, nt),
               in_specs=[sec(0), sec(1), sec(2), sec(3), blk, blk,
                         pl.BlockSpec((ncl, hpg, HEAD_DIM, HEAD_DIM),
                                      lambda hg, b, t: (b * nt + (nt - 1 - t), hg, 0, 0)),
                         pl.BlockSpec((2, hc), lambda hg, b, t: (0, hg)),
                         pl.BlockSpec((1, HEAD_DIM), lambda hg, b, t: (0, 0))],
               out_specs=[pl.BlockSpec((4, tr, hc), lambda hg, b, t: (0, b * nt + (nt - 1 - t), hg)),
                          pl.BlockSpec((1, hc), lambda hg, b, t: (0, hg)),
                          pl.BlockSpec((1, HEAD_DIM), lambda hg, b, t: (0, 0))],
               scratch=[pltpu.VMEM((hpg, HEAD_DIM, HEAD_DIM), F32)] + [pltpu.VMEM((tr, hc), F32)] * 4,
               sem=("arbitrary", "arbitrary", "arbitrary"), comm=comm)(
                   proj, proj, proj, proj, o_all, dybr, states, lbj, gn)


def _adamw(parts, w, m, v, name, comm=None):
    r, c = w.shape
    tr = _tile(r, 256)
    npart = len(parts)
    c1 = 1.0 - ADAM_B1 ** ADAM_STEP
    c2 = 1.0 - ADAM_B2 ** ADAM_STEP

    def body(*refs):
        p_refs = refs[:npart]
        w_ref, m_ref, v_ref, g_ref, d_ref, nm_ref, nv_ref = refs[npart:]
        g = p_refs[0][...].astype(F32)
        for p in p_refs[1:]:
            g = g + p[...].astype(F32)
        nm = ADAM_B1 * m_ref[...] + (1.0 - ADAM_B1) * g
        nv = ADAM_B2 * v_ref[...] + (1.0 - ADAM_B2) * (g * g)
        g_ref[...] = g
        nm_ref[...] = nm
        nv_ref[...] = nv
        d_ref[...] = -ADAM_LR * ((nm / c1) / (jnp.sqrt(nv / c2) + ADAM_EPS) + ADAM_WD * w_ref[...])

    blk = pl.BlockSpec((tr, c), lambda i: (i, 0))
    return _pc(body, name=name, out_shape=[jax.ShapeDtypeStruct((r, c), F32)] * 4, grid=(r // tr,),
               in_specs=[blk] * (npart + 3), out_specs=[blk] * 4, sem=("parallel",), comm=comm)(*parts, w, m, v)


_EARLY = ["a_ln_gain", "a_ln_bias", "a_w_s", "a_b_s", "b_lower_bounds", "b_gn_gain"]


def _pack(arrs):
    flat = jnp.concatenate([a.reshape(-1) for a in arrs])
    rows = -(-flat.shape[0] // 1024) * 8
    return jnp.pad(flat, (0, rows * 128 - flat.shape[0])).reshape(rows, 128)


def _unpack(buf, like):
    flat = buf.reshape(-1)
    out, off = [], 0
    for a in like:
        out.append(flat[off:off + a.size].reshape(a.shape))
        off += a.size
    return out


def kernel(x, c, norm_gain, w_ada, b_ada, a_w_in, a_ln_gain, a_ln_bias, a_w_s, a_b_s, a_w_out, b_w_in, b_lower_bounds, b_gn_gain, b_w_out, final_gain, loss_target, m_norm_gain, m_w_ada, m_b_ada, m_a_w_in, m_a_ln_gain, m_a_ln_bias, m_a_w_s, m_a_b_s, m_a_w_out, m_b_w_in, m_b_lower_bounds, m_b_gn_gain, m_b_w_out, m_final_gain, v_norm_gain, v_w_ada, v_b_ada, v_a_w_in, v_a_ln_gain, v_a_ln_bias, v_a_w_s, v_a_b_s, v_a_w_out, v_b_w_in, v_b_lower_bounds, v_b_gn_gain, v_b_w_out, v_final_gain):
    w = dict(norm_gain=norm_gain, w_ada=w_ada, b_ada=b_ada, a_w_in=a_w_in, a_ln_gain=a_ln_gain,
             a_ln_bias=a_ln_bias, a_w_s=a_w_s, a_b_s=a_b_s, a_w_out=a_w_out, b_w_in=b_w_in,
             b_lower_bounds=b_lower_bounds, b_gn_gain=b_gn_gain, b_w_out=b_w_out, final_gain=final_gain)
    mo = dict(norm_gain=m_norm_gain, w_ada=m_w_ada, b_ada=m_b_ada, a_w_in=m_a_w_in, a_ln_gain=m_a_ln_gain,
              a_ln_bias=m_a_ln_bias, a_w_s=m_a_w_s, a_b_s=m_a_b_s, a_w_out=m_a_w_out, b_w_in=m_b_w_in,
              b_lower_bounds=m_b_lower_bounds, b_gn_gain=m_b_gn_gain, b_w_out=m_b_w_out, final_gain=m_final_gain)
    vo = dict(norm_gain=v_norm_gain, w_ada=v_w_ada, b_ada=v_b_ada, a_w_in=v_a_w_in, a_ln_gain=v_a_ln_gain,
              a_ln_bias=v_a_ln_bias, a_w_s=v_a_w_s, a_b_s=v_a_b_s, a_w_out=v_a_w_out, b_w_in=v_b_w_in,
              b_lower_bounds=v_b_lower_bounds, b_gn_gain=v_b_gn_gain, b_w_out=v_b_w_out, final_gain=v_final_gain)

    nb, t_seq, d = x.shape
    m = nb * t_seq
    ncol_ada = w_ada.shape[2]
    xi, yi, ci = lax.axis_index("x"), lax.axis_index("y"), lax.axis_index("c")
    me = 4 * xi + 2 * yi + ci

    c_g, wa_in_g = _all_gather([c, a_w_in[0].astype(BF16)], "gather_c_wa")

    c_all = c_g.reshape(NDEV * nb, d)
    b_cols = lax.dynamic_slice(b_ada, (0, me * ncol_ada), (2, ncol_ada)).reshape(2, 1, ncol_ada)
    mod_part, lbj = _ada_fwd(c_all, w_ada, b_cols, b_lower_bounds)
    mod_all = _all_gather([mod_part], "gather_mod")[0]
    mod_mine = lax.dynamic_slice_in_dim(mod_all, me * nb, nb, axis=2)
    mod_mine = mod_mine.transpose(1, 2, 0, 3).reshape(2, nb, 3, d)
    mod0, mod1 = mod_mine[0], mod_mine[1]

    di = a_w_out.shape[1] * NDEV

    xf = x.reshape(m, d)
    tgt = loss_target.reshape(m, d)
    ng0, ng1 = norm_gain[0:1], norm_gain[1:2]
    ncb = b_w_in.shape[2]
    wb_lo, wb_hi = b_w_in[0][:, :ncb // 2].astype(BF16), b_w_in[0][:, ncb // 2:].astype(BF16)
    h0, h0_t = _prenorm(xf, ng0, mod0, t_seq, "prenorm_a")
    proj_a, half = _mm_in(h0, [wa_in_g], 1, "in_proj_a", comm=_gather_first([a_w_out[0].astype(BF16), wb_lo]))
    bs_t = jnp.pad(a_b_s[0].T, ((0, 0), (0, 128 - SG_GROUPS)))
    ybr_a, (wa_out_g, wb_lo_g, wb_hi_half) = _a_mid_fwd(
        proj_a, a_ln_gain, a_ln_bias, a_w_s[0], bs_t, t_seq, comm=_join(_gather_second(half), _gather_first([wb_hi])))
    wa_out = wa_out_g.reshape(di, d)
    (yout_a, x1), (wb_hi_g, wb_out_half) = _out_proj(
        ybr_a, wa_out, xf, mod0, t_seq, "out_proj_a",
        comm=_join(_gather_second([wb_hi_half]), _gather_first([b_w_out[0].astype(BF16)])))
    wb_in_g = [wb_lo_g, wb_hi_g]
    h1, h1_t = _prenorm(x1, ng1, mod1, t_seq, "prenorm_b")
    proj_b, (wb_out_g,) = _mm_in(h1, wb_in_g, 4, "in_proj_b", comm=_gather_second([wb_out_half]))
    wb_out = wb_out_g.reshape(di, d)
    o_b, ybr_b, states = _hgrn_fwd(proj_b, lbj, b_gn_gain, nb, t_seq)
    yout_b, dx2, loss_part, d_final_gain = _out_proj_loss(ybr_b, wb_out, x1, mod1, final_gain.reshape(1, d), tgt, t_seq)

    rows_out = a_w_out.shape[1]
    dy_b, dgate1, dybr_b = _gate_dybr(dx2, yout_b, mod1, wb_out, t_seq, "dybr_b")
    rs_wb_out = _ReduceScatter(_mm_dw_out(ybr_b, dy_b, "dw_out_b").reshape(NDEV, rows_out, d), "b_w_out")
    (dproj_b, d_lb, d_gn), got = _hgrn_bwd(proj_b, o_b, dybr_b, states, lbj, b_gn_gain, nb, t_seq,
                                           comm=rs_wb_out.swap_core())
    rs_wb_out.after_core(got[0])
    dh1, got = _mm_din(dproj_b, wb_in_g, 4, "dh_b", comm=rs_wb_out.swap_chips())
    rs_wb_out.after_chips(got[0])
    dx1, dss1, dgain1 = _prenorm_bwd(dh1, x1, ng1, mod1, dx2, t_seq, "prenorm_bwd_b")
    rs_wb_in = _ReduceScatter(_mm_dw_in(h1_t, dproj_b, ncb, 4, "dw_in_b"), "b_w_in")

    dy_a, dgate0, dybr_a = _gate_dybr(dx1, yout_a, mod0, wa_out, t_seq, "dybr_a")
    g_wa_out, got = _mm_dw_out(ybr_a, dy_a, "dw_out_a", comm=rs_wb_in.swap_core())
    rs_wb_in.after_core(got[0])
    rs_wa_out = _ReduceScatter(g_wa_out.reshape(NDEV, rows_out, d), "a_w_out")
    (dproj_a, d_lng, d_lnb, d_ws, d_bs_t), got = _a_mid_bwd(
        proj_a, dybr_a, a_ln_gain, a_ln_bias, a_w_s[0], bs_t, t_seq,
        comm=_join(rs_wb_in.swap_chips(), rs_wa_out.swap_core()))
    rs_wb_in.after_chips(got[0])
    rs_wa_out.after_core(got[1])
    part = dict(a_ln_gain=d_lng, a_ln_bias=d_lnb, a_w_s=d_ws[None], a_b_s=d_bs_t[:, :SG_GROUPS].T[None],
                b_lower_bounds=jnp.concatenate([-d_lb, d_lb], axis=0), b_gn_gain=d_gn)
    early_pack = _pack([part[k].reshape(w[k].shape) for k in _EARLY])
    g_wa_in, got = _mm_dw_in(h0_t, dproj_a, wa_in_g.shape[2], 1, "dw_in_a",
                             comm=_join(rs_wa_out.swap_chips(), _gather_first([early_pack])))
    rs_wa_out.after_chips(got[0])
    rs_wa_in = _ReduceScatter(g_wa_in, "a_w_in")
    n_tiles = m // _din_tile(m)
    assert n_tiles >= 2
    first_tiles = max(1, (3 * n_tiles) // 8)
    dh0, got2 = _mm_din(dproj_a, [wa_in_g], 1, "dh_a_first", tiles=(0, first_tiles),
                        comm=_join(rs_wa_in.swap_core(), _gather_second([got[1]])))
    rs_wa_in.after_core(got2[0])
    early_all = got2[1]
    dh0, got = _mm_din(dproj_a, [wa_in_g], 1, "dh_a_rest", comm=rs_wa_in.swap_chips(),
                       tiles=(first_tiles, n_tiles - first_tiles), prev=dh0)
    rs_wa_in.after_chips(got[0])
    dx0, dss0, dgain0 = _prenorm_bwd(dh0, xf, ng0, mod0, dx1, t_seq, "prenorm_bwd_a")
    grad_x = dx0.reshape(nb, t_seq, d)

    dmod = jnp.stack([jnp.concatenate([dss0, dgate0], axis=1), jnp.concatenate([dss1, dgate1], axis=1)])
    late_like = [norm_gain, final_gain, loss_part.reshape(1)]
    late_pack = _pack([jnp.concatenate([dgain0, dgain1], axis=0), d_final_gain[0], loss_part.reshape(1)])
    dmod_all, late_all = _all_gather([dmod.reshape(2, nb, 3 * d), late_pack], "gather_tail")
    dmod_all = dmod_all.transpose(1, 0, 2, 3).reshape(2, NDEV * nb, 3 * d)
    dmod_cols = lax.dynamic_slice_in_dim(dmod_all, me * ncol_ada, ncol_ada, axis=2)
    g_w_ada, g_b_ada = _ada_bwd(c_all, dmod_cols, dmod_all)

    res = {}
    early_like = [w[k] for k in _EARLY]
    sm = _adamw([early_all[k] for k in range(NDEV)], _pack(early_like), _pack([mo[k] for k in _EARLY]),
                _pack([vo[k] for k in _EARLY]), "adamw_small_early")
    sm = [dict(zip(_EARLY, _unpack(buf, early_like))) for buf in sm]
    for k in _EARLY:
        res[k] = tuple(s[k] for s in sm)
    zero = jnp.zeros((1,), F32)
    sm = _adamw([late_all[k] for k in range(NDEV)], _pack([norm_gain, final_gain, zero]),
                _pack([mo["norm_gain"], mo["final_gain"], zero]), _pack([vo["norm_gain"], vo["final_gain"], zero]),
                "adamw_small_late")
    sm = [_unpack(buf, late_like) for buf in sm]
    res["norm_gain"] = tuple(s[0] for s in sm)
    res["final_gain"] = tuple(s[1] for s in sm)
    loss = sm[0][2][0]
    rb = _adamw([g_b_ada], b_ada, mo["b_ada"], vo["b_ada"], "adamw_b_ada")
    res["b_ada"] = tuple(rb)
    sh = w_ada.shape
    ra = _adamw([g_w_ada.reshape(sh[0] * sh[1], sh[2])], w_ada.reshape(sh[0] * sh[1], sh[2]),
                mo["w_ada"].reshape(sh[0] * sh[1], sh[2]), vo["w_ada"].reshape(sh[0] * sh[1], sh[2]), "adamw_w_ada")
    res["w_ada"] = tuple(z.reshape(sh) for z in ra)

    for k, rs in (("b_w_out", rs_wb_out), ("b_w_in", rs_wb_in), ("a_w_out", rs_wa_out), ("a_w_in", rs_wa_in)):
        res[k] = tuple(z[None] for z in _adamw(rs.parts, w[k][0], mo[k][0], vo[k][0], "adamw_" + k))

    order = ["norm_gain", "w_ada", "b_ada", "a_w_in", "a_ln_gain", "a_ln_bias", "a_w_s", "a_b_s", "a_w_out",
             "b_w_in", "b_lower_bounds", "b_gn_gain", "b_w_out", "final_gain"]
    return (loss, grad_x, *[res[k][0] for k in order], *[res[k][1] for k in order],
            *[res[k][2] for k in order], *[res[k][3] for k in order])
```

```python
import functools
import math

import jax
import jax.numpy as jnp
from jax import lax
from jax.experimental import pallas as pl
from jax.experimental.pallas import tpu as pltpu

F32 = jnp.float32
BF16 = jnp.bfloat16
MESH = pl.DeviceIdType.MESH
NDEV = 8
EPS = 1e-6
CHUNK = 64
SG_BLOCK = 128
SG_GROUPS = 8
HEAD_DIM = 128
CUM_ROWS = 256
ADAM_LR, ADAM_B1, ADAM_B2, ADAM_EPS, ADAM_WD, ADAM_STEP = 0.001, 0.9, 0.999, 1e-08, 0.01, 10
VMEM_LIMIT = 56 * 1024 * 1024
ANY = pl.BlockSpec(memory_space=pl.ANY)


class _Hosted:
    def __init__(self, arrays, out_shapes, nsem, start, finish, aliases=None):
        self.arrays, self.out_shapes, self.nsem = list(arrays), list(out_shapes), nsem
        self.start, self.finish = start, finish
        self.aliases = dict(aliases or {})


def _join(*comms):
    arrays, outs, aliases, offs, nsem = [], [], {}, [], 0
    for cm in comms:
        offs.append((len(arrays), len(outs), nsem))
        for i, o in cm.aliases.items():
            aliases[len(arrays) + i] = len(outs) + o
        arrays += cm.arrays
        outs += cm.out_shapes
        nsem += cm.nsem

    def run(which):
        def f(ins, outs_, ss, rs, base):
            for cm, (ia, io, isem) in zip(comms, offs):
                getattr(cm, which)(ins[ia:ia + len(cm.arrays)], outs_[io:io + len(cm.out_shapes)], ss, rs, base + isem)
        return f

    return _Hosted(arrays, outs, nsem, run("start"), run("finish"), aliases)


def _pc(body, *, name, out_shape, grid=None, in_specs=None, out_specs=None, scratch=(), sem=None,
        grid_spec=None, comm=None, aliases=None):
    cp = dict(vmem_limit_bytes=VMEM_LIMIT)
    aliases = dict(aliases or {})
    if comm is None:
        if sem is not None:
            cp["dimension_semantics"] = sem
        kw = {"input_output_aliases": aliases}
        if grid_spec is not None:
            kw["grid_spec"] = grid_spec
        else:
            if grid is not None:
                kw["grid"] = grid
            if in_specs is not None:
                kw["in_specs"] = in_specs
            if out_specs is not None:
                kw["out_specs"] = out_specs
            kw["scratch_shapes"] = list(scratch)
        return pl.pallas_call(functools.partial(body), name=name, out_shape=out_shape,
                              compiler_params=pltpu.CompilerParams(**cp), **kw)

    single = not isinstance(out_shape, (list, tuple))
    outs_list = [out_shape] if single else list(out_shape)
    ospecs = [out_specs] if single else list(out_specs)
    n_in, n_out, n_ci, n_co, n_scr = len(in_specs), len(outs_list), len(comm.arrays), len(comm.out_shapes), len(scratch)
    cp["dimension_semantics"] = ("arbitrary",) * len(grid)

    def hosted(*refs):
        cin, hin = refs[:n_in], refs[n_in:n_in + n_ci]
        cout = refs[n_in + n_ci:n_in + n_ci + n_out]
        hout = refs[n_in + n_ci + n_out:n_in + n_ci + n_out + n_co]
        scr = refs[n_in + n_ci + n_out + n_co:n_in + n_ci + n_out + n_co + n_scr]
        ssem, rsem = refs[-2], refs[-1]
        first = functools.reduce(lambda p, q: p & q, [pl.program_id(a) == 0 for a in range(len(grid))])
        last = functools.reduce(lambda p, q: p & q, [pl.program_id(a) == grid[a] - 1 for a in range(len(grid))])

        @pl.when(first)
        def _():
            comm.start(hin, hout, ssem, rsem, 0)

        body(*cin, *cout, *scr)

        @pl.when(last)
        def _():
            comm.finish(hin, hout, ssem, rsem, 0)

    call = pl.pallas_call(
        hosted, name=name, grid=grid, in_specs=list(in_specs) + [ANY] * n_ci, out_specs=ospecs + [ANY] * n_co,
        out_shape=outs_list + comm.out_shapes,
        scratch_shapes=list(scratch) + [pltpu.SemaphoreType.DMA((comm.nsem,)), pltpu.SemaphoreType.DMA((comm.nsem,))],
        input_output_aliases={**aliases, **{n_in + i: n_out + o for i, o in comm.aliases.items()}},
        compiler_params=pltpu.CompilerParams(**cp))

    def run(*args):
        res = call(*args, *comm.arrays)
        comp = res[:n_out]
        return (comp[0] if single else comp), list(res[n_out:])

    return run


def _tile(n, pref):
    return pref if n % pref == 0 else n


def _sigmoid(x):
    return 1.0 / (1.0 + jnp.exp(-x))


def _gelu(x):
    c = math.sqrt(2.0 / math.pi)
    return 0.5 * x * (1.0 + jnp.tanh(c * (x + 0.044715 * (x * x * x))))


def _gelu_and_grad(x):
    c = math.sqrt(2.0 / math.pi)
    x2 = x * x
    t = jnp.tanh(c * (x + 0.044715 * (x2 * x)))
    half = 0.5 * (1.0 + t)
    return x * half, half + (0.5 * x) * (1.0 - t * t) * (c + (3.0 * 0.044715 * c) * x2)


def _dot(a, b):
    return jnp.dot(a, b, preferred_element_type=F32)


def _dot_nt(a, b):
    return lax.dot_general(a, b, (((1,), (1,)), ((), ())), preferred_element_type=F32)


def _dot_tn(a, b):
    return lax.dot_general(a, b, (((0,), (0,)), ((), ())), preferred_element_type=F32)


def _tri_mask(n, reverse):
    r = lax.broadcasted_iota(jnp.int32, (n, n), 0)
    c = lax.broadcasted_iota(jnp.int32, (n, n), 1)
    same = (r // CHUNK) == (c // CHUNK)
    tri = (c >= r) if reverse else (c <= r)
    return jnp.where(same & tri, 1.0, 0.0).astype(BF16)


def _tri_apply(tri, x):
    hi = x.astype(BF16)
    r1 = x - hi.astype(F32)
    mid = r1.astype(BF16)
    lo = (r1 - mid.astype(F32)).astype(BF16)
    return _dot(tri, hi) + (_dot(tri, mid) + _dot(tri, lo))


def _all_gather(arrs, name):
    n = len(arrs)

    def body(*refs):
        ins, outs = refs[:n], refs[n:2 * n]
        send_sems, recv_sems, local_sems = refs[2 * n:]
        x, y, c = lax.axis_index("x"), lax.axis_index("y"), lax.axis_index("c")
        me, sibling = (x, y, c), (x, y, 1 - c)
        chips = [(1 - x, y), (x, 1 - y), (1 - x, 1 - y)]

        def blk(a, p):
            return outs[a].at[4 * p[0] + 2 * p[1] + p[2]]

        def copy(a, k, block, to, src=None):
            return pltpu.make_async_remote_copy(
                src_ref=blk(a, block) if src is None else src, dst_ref=blk(a, block),
                send_sem=send_sems.at[7 * a + k], recv_sem=recv_sems.at[7 * a + k],
                device_id=to, device_id_type=MESH)

        mine = [pltpu.make_async_copy(ins[a], blk(a, me), local_sems.at[a]) for a in range(n)]
        for m in mine:
            m.start()
        first = []
        for a in range(n):
            first.append(copy(a, 0, me, sibling, src=ins[a]))
            for j, chip in enumerate(chips):
                first.append(copy(a, 1 + j, me, (*chip, c), src=ins[a]))
        for cp in first:
            cp.start()
        passed = []
        for j, chip in enumerate(chips):
            for a in range(n):
                copy(a, 1 + j, (*chip, c), me).wait_recv()
                p = copy(a, 4 + j, (*chip, c), sibling)
                p.start()
                passed.append(p)
        for a in range(n):
            copy(a, 0, sibling, me).wait_recv()
            for j, chip in enumerate(chips):
                copy(a, 4 + j, (*chip, 1 - c), me).wait_recv()
        for cp in first + passed:
            cp.wait_send()
        for m in mine:
            m.wait()

    out_shape = [jax.ShapeDtypeStruct((NDEV,) + a.shape, a.dtype) for a in arrs]
    return _pc(body, name=name, out_shape=out_shape, in_specs=[ANY] * n, out_specs=[ANY] * n,
               scratch=[pltpu.SemaphoreType.DMA((7 * n,)), pltpu.SemaphoreType.DMA((7 * n,)),
                        pltpu.SemaphoreType.DMA((n,))])(*arrs)


def _gather_first(arrs):
    n = len(arrs)

    def parts(ins, outs, ss, rs, base):
        x, y, c = lax.axis_index("x"), lax.axis_index("y"), lax.axis_index("c")
        me, sibling = (x, y, c), (x, y, 1 - c)
        chips = [(1 - x, y), (x, 1 - y), (1 - x, 1 - y)]

        def blk(a, p):
            return outs[a].at[4 * p[0] + 2 * p[1] + p[2]]

        def copy(a, k, block, to):
            return pltpu.make_async_remote_copy(
                src_ref=ins[a], dst_ref=blk(a, block), send_sem=ss.at[base + 4 * a + k],
                recv_sem=rs.at[base + 4 * a + k], device_id=to, device_id_type=MESH)

        local = [pltpu.make_async_copy(ins[a], blk(a, me), ss.at[base + 4 * n + a]) for a in range(n)]
        sends, recvs = [], []
        for a in range(n):
            sends.append(copy(a, 0, me, sibling))
            recvs.append(copy(a, 0, sibling, me))
            for j, chip in enumerate(chips):
                sends.append(copy(a, 1 + j, me, (*chip, c)))
                recvs.append(copy(a, 1 + j, (*chip, c), me))
        return local, sends, recvs

    def start(ins, outs, ss, rs, base):
        local, sends, _ = parts(ins, outs, ss, rs, base)
        for cp in local + sends:
            cp.start()

    def finish(ins, outs, ss, rs, base):
        local, sends, recvs = parts(ins, outs, ss, rs, base)
        for cp in recvs:
            cp.wait_recv()
        for cp in sends:
            cp.wait_send()
        for cp in local:
            cp.wait()

    return _Hosted(arrs, [jax.ShapeDtypeStruct((NDEV,) + a.shape, a.dtype) for a in arrs], 5 * n, start, finish)


def _gather_second(bufs):
    n = len(bufs)

    def parts(ins, outs, ss, rs, base):
        x, y, c = lax.axis_index("x"), lax.axis_index("y"), lax.axis_index("c")
        sibling = (x, y, 1 - c)
        chips = [(1 - x, y), (x, 1 - y), (1 - x, 1 - y)]
        sends, recvs = [], []
        for a in range(n):
            for j, chip in enumerate(chips):
                mine = 4 * chip[0] + 2 * chip[1] + c
                theirs = 4 * chip[0] + 2 * chip[1] + (1 - c)
                sends.append(pltpu.make_async_remote_copy(
                    src_ref=ins[a].at[mine], dst_ref=outs[a].at[mine], send_sem=ss.at[base + 3 * a + j],
                    recv_sem=rs.at[base + 3 * a + j], device_id=sibling, device_id_type=MESH))
                recvs.append(pltpu.make_async_remote_copy(
                    src_ref=ins[a].at[theirs], dst_ref=outs[a].at[theirs], send_sem=ss.at[base + 3 * a + j],
                    recv_sem=rs.at[base + 3 * a + j], device_id=sibling, device_id_type=MESH))
        return sends, recvs

    def start(ins, outs, ss, rs, base):
        for cp in parts(ins, outs, ss, rs, base)[0]:
            cp.start()

    def finish(ins, outs, ss, rs, base):
        sends, recvs = parts(ins, outs, ss, rs, base)
        for cp in recvs:
            cp.wait_recv()
        for cp in sends:
            cp.wait_send()

    return _Hosted(bufs, [jax.ShapeDtypeStruct(b.shape, b.dtype) for b in bufs], 3 * n, start, finish,
                   aliases={a: a for a in range(n)})


def _swap(src, nblk, ids_fn, partner_fn):
    def copies(ins, outs, ss, rs, base):
        x, y, c = lax.axis_index("x"), lax.axis_index("y"), lax.axis_index("c")
        ids = ids_fn(x, y, c)
        partner = partner_fn(x, y, c)
        return [pltpu.make_async_remote_copy(
            src_ref=ins[0].at[ids[k]], dst_ref=outs[0].at[k], send_sem=ss.at[base + k], recv_sem=rs.at[base + k],
            device_id=partner, device_id_type=MESH) for k in range(nblk)]

    def start(ins, outs, ss, rs, base):
        for cp in copies(ins, outs, ss, rs, base):
            cp.start()

    def finish(ins, outs, ss, rs, base):
        for cp in copies(ins, outs, ss, rs, base):
            cp.wait()

    return _Hosted([src], [jax.ShapeDtypeStruct((nblk,) + src.shape[1:], src.dtype)], nblk, start, finish)


def _blocking(comm, name):
    n_i, n_o = len(comm.arrays), len(comm.out_shapes)

    def body(*refs):
        ins, outs = refs[:n_i], refs[n_i:n_i + n_o]
        comm.start(ins, outs, refs[-2], refs[-1], 0)
        comm.finish(ins, outs, refs[-2], refs[-1], 0)

    return pl.pallas_call(
        body, name=name, out_shape=comm.out_shapes, in_specs=[ANY] * n_i, out_specs=[ANY] * n_o,
        scratch_shapes=[pltpu.SemaphoreType.DMA((comm.nsem,)), pltpu.SemaphoreType.DMA((comm.nsem,))],
        input_output_aliases=comm.aliases)(*comm.arrays)


def _swap_chips(send):
    def copies(ins, outs, ss, rs, base):
        x, y, c = lax.axis_index("x"), lax.axis_index("y"), lax.axis_index("c")
        chips = [(1 - x, y), (x, 1 - y), (1 - x, 1 - y)]
        return [pltpu.make_async_remote_copy(
            src_ref=ins[0].at[j], dst_ref=outs[0].at[j], send_sem=ss.at[base + j], recv_sem=rs.at[base + j],
            device_id=(*chip, c), device_id_type=MESH) for j, chip in enumerate(chips)]

    def start(ins, outs, ss, rs, base):
        for cp in copies(ins, outs, ss, rs, base):
            cp.start()

    def finish(ins, outs, ss, rs, base):
        for cp in copies(ins, outs, ss, rs, base):
            cp.wait()

    return _Hosted([send], [jax.ShapeDtypeStruct(send.shape, send.dtype)], 3, start, finish)


def _add_send(a, b, idx, ns, name):
    _, r, c = a.shape
    tr = _tile(r, 256)

    def body(idx_ref, a_ref, b_ref, send_ref):
        send_ref[...] = (a_ref[...] + b_ref[...]).astype(BF16)

    def sel(off):
        return pl.BlockSpec((None, tr, c), lambda k, i, s: (s[off + k], i, 0))

    gs = pltpu.PrefetchScalarGridSpec(num_scalar_prefetch=1, grid=(ns, r // tr), in_specs=[sel(0), sel(ns)],
                                      out_specs=pl.BlockSpec((None, tr, c), lambda k, i, s: (k, i, 0)))
    return _pc(body, name=name, grid_spec=gs, sem=("arbitrary", "arbitrary"),
               out_shape=jax.ShapeDtypeStruct((ns, r, c), BF16))(idx, a, b)


class _ReduceScatter:
    def __init__(self, g, tag):
        self.g, self.tag = g, tag

    def swap_core(self):
        return _swap(self.g, 4, lambda x, y, c: [1 - c, 3 - c, 5 - c, 7 - c], lambda x, y, c: (x, y, 1 - c))

    def after_core(self, recv):
        x, y, c = lax.axis_index("x"), lax.axis_index("y"), lax.axis_index("c")
        chips = [(1 - x, y), (x, 1 - y), (1 - x, 1 - y)]
        idx = jnp.stack([4 * p + 2 * q + c for p, q in chips] + [2 * p + q for p, q in chips]).astype(jnp.int32)
        self.send = _add_send(self.g, recv, idx, 3, "rs_add_" + self.tag)
        self.mine = [lax.dynamic_index_in_dim(self.g, 4 * x + 2 * y + c, 0, keepdims=False),
                     lax.dynamic_index_in_dim(recv, 2 * x + y, 0, keepdims=False)]

    def swap_chips(self):
        return _swap_chips(self.send)

    def after_chips(self, recv):
        self.parts = self.mine + [recv[0], recv[1], recv[2]]


def _ada_fwd(c_all, w_ada, b_cols, b_lb):
    nl, d, ncol = w_ada.shape
    nseq = c_all.shape[0]
    di = b_lb.shape[1]

    def body(c_ref, w_ref, b_ref, lb_ref, mod_ref, lbj_ref):
        cv = c_ref[...]
        cact = (cv * _sigmoid(cv)).astype(BF16)
        for l in range(nl):
            mod_ref[l] = _dot(cact, w_ref[l].astype(BF16)) + b_ref[l]
        b0, b1 = lb_ref[0:1, :], lb_ref[1:2, :]
        mx = jnp.maximum(b0, b1)
        e0, e1 = jnp.exp(b0 - mx), jnp.exp(b1 - mx)
        s = e0 + e1
        p0, p1 = e0 / s, e1 / s
        lbj_ref[0:1, :] = (p0 + p1) - p0
        lbj_ref[1:2, :] = p0 * p1

    return _pc(body, name="ada_fwd",
               out_shape=[jax.ShapeDtypeStruct((nl, nseq, ncol), F32), jax.ShapeDtypeStruct((2, di), F32)]
               )(c_all, w_ada, b_cols, b_lb)


def _ada_bwd(c_all, dmod_cols, dmod_full):
    nl, nseq, ncol = dmod_cols.shape
    d = c_all.shape[1]
    d3 = dmod_full.shape[2]

    def body(c_ref, dc_ref, df_ref, gw_ref, gb_ref):
        cv = c_ref[...]
        cact = (cv * _sigmoid(cv)).astype(BF16)
        for l in range(nl):
            gw_ref[l] = _dot_tn(cact, dc_ref[l].astype(BF16))
            gb_ref[l:l + 1, :] = jnp.sum(df_ref[l], axis=0, keepdims=True)

    return _pc(body, name="ada_bwd",
               out_shape=[jax.ShapeDtypeStruct((nl, d, ncol), F32), jax.ShapeDtypeStruct((nl, d3), F32)]
               )(c_all, dmod_cols, dmod_full)


def _prenorm(x, gain, mod, t_seq, name):
    m, d = x.shape
    tm = _tile(t_seq, 512)
    per = t_seq // tm

    def body(x_ref, g_ref, mod_ref, h_ref, ht_ref):
        xv = x_ref[...]
        rstd = lax.rsqrt(jnp.mean(xv * xv, axis=-1, keepdims=True) + EPS)
        r = xv * rstd * g_ref[...]
        h = r * (1.0 + mod_ref[0, 1:2, :]) + mod_ref[0, 0:1, :]
        h_ref[...] = h.astype(BF16)
        ht_ref[...] = h.T.astype(BF16)

    return _pc(body, name=name, out_shape=[jax.ShapeDtypeStruct((m, d), BF16), jax.ShapeDtypeStruct((d, m), BF16)],
               grid=(m // tm,),
               in_specs=[pl.BlockSpec((tm, d), lambda i: (i, 0)), pl.BlockSpec((1, d), lambda i: (0, 0)),
                         pl.BlockSpec((1, 3, d), lambda i: (i // per, 0, 0))],
               out_specs=[pl.BlockSpec((tm, d), lambda i: (i, 0)), pl.BlockSpec((d, tm), lambda i: (0, i))],
               sem=("parallel",))(x, gain, mod)


def _prenorm_bwd(dh, x, gain, mod, dxn, t_seq, name, comm=None):
    m, d = x.shape
    nb = m // t_seq
    tm = _tile(t_seq, 512)
    per = t_seq // tm

    def body(dh_ref, x_ref, g_ref, mod_ref, dxn_ref, dx_ref, dss_ref, dg_ref):
        i = pl.program_id(0)
        xv, dhv, g = x_ref[...], dh_ref[...], g_ref[...]
        rstd = lax.rsqrt(jnp.mean(xv * xv, axis=-1, keepdims=True) + EPS)
        xhat = xv * rstd
        dr = dhv * (1.0 + mod_ref[0, 1:2, :])
        dxhat = dr * g
        dx_ref[...] = dxn_ref[...] + rstd * (dxhat - xhat * jnp.mean(dxhat * xhat, axis=-1, keepdims=True))

        @pl.when(i % per == 0)
        def _():
            dss_ref[...] = jnp.zeros_like(dss_ref)

        @pl.when(i == 0)
        def _():
            dg_ref[...] = jnp.zeros_like(dg_ref)

        dss_ref[0, 0:1, :] += jnp.sum(dhv, axis=0, keepdims=True)
        dss_ref[0, 1:2, :] += jnp.sum(dhv * (xhat * g), axis=0, keepdims=True)
        dg_ref[...] += jnp.sum(dr * xhat, axis=0, keepdims=True)

    row = pl.BlockSpec((tm, d), lambda i: (i, 0))
    return _pc(body, name=name,
               out_shape=[jax.ShapeDtypeStruct((m, d), F32), jax.ShapeDtypeStruct((nb, 2, d), F32),
                          jax.ShapeDtypeStruct((1, d), F32)],
               grid=(m // tm,),
               in_specs=[row, row, pl.BlockSpec((1, d), lambda i: (0, 0)),
                         pl.BlockSpec((1, 3, d), lambda i: (i // per, 0, 0)), row],
               out_specs=[row, pl.BlockSpec((1, 2, d), lambda i: (i // per, 0, 0)),
                          pl.BlockSpec((1, d), lambda i: (0, 0))],
               sem=("arbitrary",), comm=comm)(dh, x, gain, mod, dxn)


def _mm_in(h, ws, sections, name, comm=None):
    m, k = h.shape
    nw, ncp = len(ws), ws[0].shape[2]
    nc = nw * ncp
    per = NDEV // sections if sections > 1 else NDEV
    tm = _tile(m, 512)
    assert per % 2 == 0

    def body(*refs):
        hv = refs[0][...]
        o_ref = refs[1 + nw]
        for b in range(2):
            for a in range(nw):
                lo = b * nc + a * ncp
                o_ref[:, lo:lo + ncp] = _dot(hv, refs[1 + a][b])

    w_spec = pl.BlockSpec((2, k, ncp), lambda j, i: (j, 0, 0))
    if sections > 1:
        out_shape = jax.ShapeDtypeStruct((sections, m, per * nc), F32)
        out_spec = pl.BlockSpec((None, tm, 2 * nc), lambda j, i: ((2 * j) // per, i, ((2 * j) % per) // 2))
    else:
        out_shape = jax.ShapeDtypeStruct((m, NDEV * nc), F32)
        out_spec = pl.BlockSpec((tm, 2 * nc), lambda j, i: (i, j))
    return _pc(body, name=name, out_shape=out_shape, grid=(NDEV // 2, m // tm),
               in_specs=[pl.BlockSpec((tm, k), lambda j, i: (i, 0))] + [w_spec] * nw,
               out_specs=out_spec, sem=("parallel", "parallel"), comm=comm)(h, *ws)


def _din_tile(m):
    return 1024 if m % 1024 == 0 and m >= 2048 else _tile(m, 512)


def _mm_din(dproj, ws, sections, name, comm=None, tiles=None, prev=None):
    nw, k, ncp = len(ws), ws[0].shape[1], ws[0].shape[2]
    nc = nw * ncp
    m = dproj.shape[-2]
    tm = _din_tile(m)
    t0, nt = tiles if tiles is not None else (0, m // tm)
    per = NDEV // sections if sections > 1 else NDEV
    assert per % 2 == 0

    def body(*refs):
        d_ref, o_ref = refs[0], refs[-1]
        j = pl.program_id(1)
        acc = None
        for b in range(2):
            for a in range(nw):
                lo = b * nc + a * ncp
                term = _dot_nt(d_ref[:, lo:lo + ncp], refs[1 + a][b])
                acc = term if acc is None else acc + term

        @pl.when(j == 0)
        def _():
            o_ref[...] = acc

        @pl.when(j > 0)
        def _():
            o_ref[...] += acc

    if sections > 1:
        dspec = pl.BlockSpec((None, tm, 2 * nc), lambda i, j: ((2 * j) // per, i + t0, ((2 * j) % per) // 2))
    else:
        dspec = pl.BlockSpec((tm, 2 * nc), lambda i, j: (i + t0, j))
    in_specs = [dspec] + [pl.BlockSpec((2, k, ncp), lambda i, j: (j, 0, 0))] * nw
    args = [dproj, *ws]
    if prev is not None:
        in_specs.append(ANY)
        args.append(prev)
    return _pc(body, name=name, out_shape=jax.ShapeDtypeStruct((m, k), F32), grid=(nt, NDEV // 2), in_specs=in_specs,
               out_specs=pl.BlockSpec((tm, k), lambda i, j: (i + t0, 0)), sem=("parallel", "arbitrary"),
               comm=comm, aliases={1 + nw: 0} if prev is not None else None)(*args)


def _mm_dw_in(ht, dproj, nc, sections, name, comm=None):
    k, m = ht.shape
    tk = 2048 if m % 2048 == 0 else _din_tile(m)
    per = NDEV // sections if sections > 1 else NDEV

    def body(h_ref, d_ref, o_ref):
        kk = pl.program_id(1)
        acc = _dot(h_ref[...], d_ref[...])

        @pl.when(kk == 0)
        def _():
            o_ref[...] = acc

        @pl.when(kk > 0)
        def _():
            o_ref[...] += acc

    if sections > 1:
        dspec = pl.BlockSpec((None, tk, nc), lambda j, i: (j // per, i, j % per))
    else:
        dspec = pl.BlockSpec((tk, nc), lambda j, i: (i, j))
    return _pc(body, name=name, out_shape=jax.ShapeDtypeStruct((NDEV, k, nc), F32), grid=(NDEV, m // tk),
               in_specs=[pl.BlockSpec((k, tk), lambda j, i: (0, i)), dspec],
               out_specs=pl.BlockSpec((None, k, nc), lambda j, i: (j, 0, 0)),
               sem=("parallel", "arbitrary"), comm=comm)(ht, dproj)


def _out_proj(ybr, w_out, x, mod, t_seq, name, comm=None):
    m, di = ybr.shape
    d = w_out.shape[1]
    tm = _tile(t_seq, 512)
    per = t_seq // tm

    def body(y_ref, w_ref, x_ref, mod_ref, yo_ref, xn_ref):
        yo = _dot(y_ref[...], w_ref[...])
        yo_ref[...] = yo
        xn_ref[...] = x_ref[...] + mod_ref[0, 2:3, :] * yo

    row = pl.BlockSpec((tm, d), lambda i: (i, 0))
    return _pc(body, name=name,
               out_shape=[jax.ShapeDtypeStruct((m, d), F32), jax.ShapeDtypeStruct((m, d), F32)],
               grid=(m // tm,),
               in_specs=[pl.BlockSpec((tm, di), lambda i: (i, 0)), pl.BlockSpec((di, d), lambda i: (0, 0)), row,
                         pl.BlockSpec((1, 3, d), lambda i: (i // per, 0, 0))],
               out_specs=[row, row], sem=("parallel",), comm=comm)(ybr, w_out, x, mod)


def _out_proj_loss(ybr, w_out, x, mod, gain, target, t_seq):
    m, di = ybr.shape
    d = w_out.shape[1]
    tm = _tile(t_seq, 512)
    per = t_seq // tm

    def body(y_ref, w_ref, x_ref, mod_ref, g_ref, t_ref, yo_ref, dx_ref, loss_ref, dg_ref):
        i = pl.program_id(0)
        yo = _dot(y_ref[...], w_ref[...])
        yo_ref[...] = yo
        xv = x_ref[...] + mod_ref[0, 2:3, :] * yo
        g = g_ref[...]
        rstd = lax.rsqrt(jnp.mean(xv * xv, axis=-1, keepdims=True) + EPS)
        xhat = xv * rstd
        err = xhat * g - t_ref[...]
        dy = err * (1.0 / d)
        dxhat = dy * g
        dx_ref[...] = rstd * (dxhat - xhat * jnp.mean(dxhat * xhat, axis=-1, keepdims=True))

        @pl.when(i == 0)
        def _():
            loss_ref[...] = jnp.zeros_like(loss_ref)
            dg_ref[...] = jnp.zeros_like(dg_ref)

        loss_ref[...] += 0.5 * jnp.sum(jnp.mean(err * err, axis=-1, keepdims=True), axis=0, keepdims=True)
        dg_ref[...] += jnp.sum(dy * xhat, axis=0, keepdims=True)

    row = pl.BlockSpec((tm, d), lambda i: (i, 0))
    vec = pl.BlockSpec((1, d), lambda i: (0, 0))
    return _pc(body, name="out_proj_loss",
               out_shape=[jax.ShapeDtypeStruct((m, d), F32), jax.ShapeDtypeStruct((m, d), F32),
                          jax.ShapeDtypeStruct((1, 1), F32), jax.ShapeDtypeStruct((1, d), F32)],
               grid=(m // tm,),
               in_specs=[pl.BlockSpec((tm, di), lambda i: (i, 0)), pl.BlockSpec((di, d), lambda i: (0, 0)), row,
                         pl.BlockSpec((1, 3, d), lambda i: (i // per, 0, 0)), vec, row],
               out_specs=[row, row, pl.BlockSpec((1, 1), lambda i: (0, 0)), vec],
               sem=("arbitrary",))(ybr, w_out, x, mod, gain, target)


def _gate_dybr(dxn, yout, mod, w_out, t_seq, name):
    m, d = dxn.shape
    di = w_out.shape[0]
    nb = m // t_seq
    tm = _tile(t_seq, 512)
    per = t_seq // tm

    def body(dxn_ref, yo_ref, mod_ref, w_ref, dy_ref, dgate_ref, o_ref):
        i = pl.program_id(0)
        dv = dxn_ref[...]
        dy = (mod_ref[0, 2:3, :] * dv).astype(BF16)
        dy_ref[...] = dy
        o_ref[...] = _dot_nt(dy, w_ref[...])

        @pl.when(i % per == 0)
        def _():
            dgate_ref[...] = jnp.zeros_like(dgate_ref)

        dgate_ref[0] += jnp.sum(dv * yo_ref[...], axis=0, keepdims=True)

    row = pl.BlockSpec((tm, d), lambda i: (i, 0))
    return _pc(body, name=name,
               out_shape=[jax.ShapeDtypeStruct((m, d), BF16), jax.ShapeDtypeStruct((nb, 1, d), F32),
                          jax.ShapeDtypeStruct((m, di), F32)],
               grid=(m // tm,),
               in_specs=[row, row, pl.BlockSpec((1, 3, d), lambda i: (i // per, 0, 0)),
                         pl.BlockSpec((di, d), lambda i: (0, 0))],
               out_specs=[row, pl.BlockSpec((1, 1, d), lambda i: (i // per, 0, 0)),
                          pl.BlockSpec((tm, di), lambda i: (i, 0))],
               sem=("arbitrary",))(dxn, yout, mod, w_out)


def _mm_dw_out(ybr, dy, name, comm=None):
    m, di = ybr.shape
    d = dy.shape[1]
    tk = _tile(m, 512)
    tn = _tile(di, 1024)

    def body(y_ref, dy_ref, o_ref):
        kk = pl.program_id(1)
        acc = _dot_tn(y_ref[...], dy_ref[...])

        @pl.when(kk == 0)
        def _():
            o_ref[...] = acc

        @pl.when(kk > 0)
        def _():
            o_ref[...] += acc

    return _pc(body, name=name, out_shape=jax.ShapeDtypeStruct((di, d), F32), grid=(di // tn, m // tk),
               in_specs=[pl.BlockSpec((tk, tn), lambda n, k: (k, n)), pl.BlockSpec((tk, d), lambda n, k: (k, 0))],
               out_specs=pl.BlockSpec((tn, d), lambda n, k: (n, 0)), sem=("parallel", "arbitrary"),
               comm=comm)(ybr, dy)


def _sgu_mask():
    t = lax.broadcasted_iota(jnp.int32, (SG_BLOCK, SG_BLOCK), 0)
    s = lax.broadcasted_iota(jnp.int32, (SG_BLOCK, SG_BLOCK), 1)
    return (s // CHUNK) <= (t // CHUNK)


def _a_mid_fwd(proj, ln_g, ln_b, w_s, bs_t, t_seq, comm=None):
    m, n3 = proj.shape
    di = n3 // 3
    gd = di // SG_GROUPS
    r = _tile(t_seq, 256)
    nblk = r // SG_BLOCK

    def body(p_ref, lg_ref, lb_ref, ws_ref, bs_ref, ybr_ref, s_scr):
        v = _gelu(p_ref[:, di:2 * di])
        mu = jnp.mean(v, axis=-1, keepdims=True)
        vc = v - mu
        rstd = lax.rsqrt(jnp.mean(vc * vc, axis=-1, keepdims=True) + EPS)
        vb = (vc * rstd * lg_ref[...] + lb_ref[...]).astype(BF16)
        mask = _sgu_mask()
        for gi in range(SG_GROUPS):
            ws = jnp.where(mask, ws_ref[gi], 0.0).astype(BF16)
            bcol = bs_ref[:, gi:gi + 1]
            for b in range(nblk):
                rows = slice(b * SG_BLOCK, (b + 1) * SG_BLOCK)
                cols = slice(gi * gd, (gi + 1) * gd)
                s_scr[rows, cols] = _dot(ws, vb[rows, cols]) + bcol
        gg = p_ref[:, 2 * di:]
        ybr_ref[...] = (_gelu(p_ref[:, :di]) * s_scr[...] * (gg * _sigmoid(gg))).astype(BF16)

    vec = pl.BlockSpec((1, di), lambda i: (0, 0))
    return _pc(body, name="a_mid_fwd", out_shape=jax.ShapeDtypeStruct((m, di), BF16), grid=(m // r,),
               in_specs=[pl.BlockSpec((r, n3), lambda i: (i, 0)), vec, vec,
                         pl.BlockSpec((SG_GROUPS, SG_BLOCK, SG_BLOCK), lambda i: (0, 0, 0)),
                         pl.BlockSpec((SG_BLOCK, 128), lambda i: (0, 0))],
               out_specs=pl.BlockSpec((r, di), lambda i: (i, 0)),
               scratch=[pltpu.VMEM((r, di), F32)], sem=("parallel",), comm=comm)(proj, ln_g, ln_b, w_s, bs_t)


def _a_mid_bwd(proj, dybr, ln_g, ln_b, w_s, bs_t, t_seq, comm=None):
    m, n3 = proj.shape
    di = n3 // 3
    gd = di // SG_GROUPS
    r = _tile(t_seq, 256)
    nblk = r // SG_BLOCK

    def body(p_ref, dy_ref, lg_ref, lb_ref, ws_ref, bs_ref,
             dp_ref, dlg_ref, dlb_ref, dws_ref, dbs_ref, s_scr, dvl_scr):
        i = pl.program_id(0)

        @pl.when(i == 0)
        def _():
            dlg_ref[...] = jnp.zeros_like(dlg_ref)
            dlb_ref[...] = jnp.zeros_like(dlb_ref)
            dws_ref[...] = jnp.zeros_like(dws_ref)
            dbs_ref[...] = jnp.zeros_like(dbs_ref)

        v, dgelu_v = _gelu_and_grad(p_ref[:, di:2 * di])
        mu = jnp.mean(v, axis=-1, keepdims=True)
        vc = v - mu
        rstd = lax.rsqrt(jnp.mean(vc * vc, axis=-1, keepdims=True) + EPS)
        vhat = vc * rstd
        lg = lg_ref[...]
        vb = (vhat * lg + lb_ref[...]).astype(BF16)
        u, dgelu_u = _gelu_and_grad(p_ref[:, :di])
        gg = p_ref[:, 2 * di:]
        sg = _sigmoid(gg)
        dyv = dy_ref[...]
        dus = dyv * (gg * sg)
        dsb = (dus * u).astype(BF16)
        ds32 = dus * u
        mask = _sgu_mask()
        lane = lax.broadcasted_iota(jnp.int32, (SG_BLOCK, 128), 1)
        dbs_acc = jnp.zeros((SG_BLOCK, 128), F32)
        for gi in range(SG_GROUPS):
            ws = jnp.where(mask, ws_ref[gi], 0.0).astype(BF16)
            bcol = bs_ref[:, gi:gi + 1]
            cols = slice(gi * gd, (gi + 1) * gd)
            dws_acc = jnp.zeros((SG_BLOCK, SG_BLOCK), F32)
            dbs_col = jnp.zeros((SG_BLOCK, 1), F32)
            for b in range(nblk):
                rows = slice(b * SG_BLOCK, (b + 1) * SG_BLOCK)
                s_scr[rows, cols] = _dot(ws, vb[rows, cols]) + bcol
                dvl_scr[rows, cols] = _dot_tn(ws, dsb[rows, cols])
                dws_acc += _dot_nt(dsb[rows, cols], vb[rows, cols])
                dbs_col += jnp.sum(ds32[rows, cols], axis=-1, keepdims=True)
            dws_ref[gi] += jnp.where(mask, dws_acc, 0.0)
            dbs_acc += jnp.where(lane == gi, dbs_col, 0.0)
        dbs_ref[...] += dbs_acc
        s = s_scr[...]
        dp_ref[:, :di] = (dus * s * dgelu_u).astype(BF16)
        dp_ref[:, 2 * di:] = (dyv * u * s * (sg * (1.0 + gg * (1.0 - sg)))).astype(BF16)
        dvl = dvl_scr[...]
        dlg_ref[...] += jnp.sum(dvl * vhat, axis=0, keepdims=True)
        dlb_ref[...] += jnp.sum(dvl, axis=0, keepdims=True)
        dvh = dvl * lg
        dv = rstd * (dvh - jnp.mean(dvh, axis=-1, keepdims=True)
                     - vhat * jnp.mean(dvh * vhat, axis=-1, keepdims=True))
        dp_ref[:, di:2 * di] = (dv * dgelu_v).astype(BF16)

    vec = pl.BlockSpec((1, di), lambda i: (0, 0))
    wsb = pl.BlockSpec((SG_GROUPS, SG_BLOCK, SG_BLOCK), lambda i: (0, 0, 0))
    bsb = pl.BlockSpec((SG_BLOCK, 128), lambda i: (0, 0))
    return _pc(body, name="a_mid_bwd",
               out_shape=[jax.ShapeDtypeStruct((m, n3), BF16), jax.ShapeDtypeStruct((1, di), F32),
                          jax.ShapeDtypeStruct((1, di), F32),
                          jax.ShapeDtypeStruct((SG_GROUPS, SG_BLOCK, SG_BLOCK), F32),
                          jax.ShapeDtypeStruct((SG_BLOCK, 128), F32)],
               grid=(m // r,),
               in_specs=[pl.BlockSpec((r, n3), lambda i: (i, 0)), pl.BlockSpec((r, di), lambda i: (i, 0)),
                         vec, vec, wsb, bsb],
               out_specs=[pl.BlockSpec((r, n3), lambda i: (i, 0)), vec, vec, wsb, bsb],
               scratch=[pltpu.VMEM((r, di), F32), pltpu.VMEM((r, di), F32)],
               sem=("arbitrary",), comm=comm)(proj, dybr, ln_g, ln_b, w_s, bs_t)


def _hgrn_dims(t_seq, di):
    tr = _tile(t_seq, 256)
    hc = _tile(di, 1024)
    return tr, hc, hc // HEAD_DIM


def _hgrn_gates(f_ref, lb, a_scr, k_scr, tr):
    sig = _sigmoid(f_ref[...])
    fg = lb + (1.0 - lb) * sig
    k_scr[...] = 1.0 - fg
    logf = jnp.log(fg)
    g = min(CUM_ROWS, tr)
    tri = _tri_mask(g, reverse=False)
    for rg in range(tr // g):
        a_scr[rg * g:(rg + 1) * g, :] = _tri_apply(tri, logf[rg * g:(rg + 1) * g, :])
    return sig, fg


def _hgrn_fwd(proj, lbj, gn, nb, t_seq):
    _, m, di = proj.shape
    tr, hc, hpg = _hgrn_dims(t_seq, di)
    nt, nhg, ncl = t_seq // tr, di // hc, tr // CHUNK
    nheads = di // HEAD_DIM

    def body(q_ref, f_ref, i_ref, g_ref, lb_ref, gn_ref, o_ref, ybr_ref, st_ref, st_scr, a_scr, k_scr):
        t = pl.program_id(2)

        @pl.when(t == 0)
        def _():
            st_scr[...] = jnp.zeros_like(st_scr)

        _hgrn_gates(f_ref, lb_ref[0:1, :], a_scr, k_scr, tr)
        gnv = gn_ref[...]
        rr = lax.broadcasted_iota(jnp.int32, (CHUNK, CHUNK), 0)
        cc = lax.broadcasted_iota(jnp.int32, (CHUNK, CHUNK), 1)
        causal = cc <= rr

        def chunk(n, carry):
            rows = pl.ds(pl.multiple_of(n * CHUNK, CHUNK), CHUNK)
            lanes = [slice(hd * HEAD_DIM, (hd + 1) * HEAD_DIM) for hd in range(hpg)]
            hs = []
            for hd, ls in enumerate(lanes):
                h = {}
                ah, kh = a_scr[rows, ls], k_scr[rows, ls]
                qp = q_ref[rows, ls]
                qh = qp * _sigmoid(qp)
                h["vb"] = i_ref[rows, ls].astype(BF16)
                aref, alast = ah[CHUNK // 2 - 1:CHUNK // 2, :], ah[CHUNK - 1:CHUNK, :]
                h["q_in"] = (qh * jnp.exp(ah - aref)).astype(BF16)
                h["k_in"] = (kh * jnp.exp(aref - ah)).astype(BF16)
                h["q_out"] = (qh * jnp.exp(ah)).astype(BF16)
                h["k_out"] = (kh * jnp.exp(alast - ah)).astype(BF16)
                h["dec"] = jnp.exp(alast)
                st = st_scr[hd]
                st_ref[n, hd] = st
                h["st"] = st
                hs.append(h)
            for h in hs:
                h["scores"] = _dot_nt(h["q_in"], h["k_in"])
                h["o_inter"] = _dot_nt(h["q_out"], h["st"].astype(BF16))
                h["st_mm"] = _dot_tn(h["vb"], h["k_out"])
            for h in hs:
                h["o"] = _dot(jnp.where(causal, h["scores"], 0.0).astype(BF16), h["vb"]) + h["o_inter"]
            for hd, (h, ls) in enumerate(zip(hs, lanes)):
                st_scr[hd] = h["st"] * h["dec"] + h["st_mm"]
                o = h["o"]
                o_ref[rows, ls] = o
                rstd = lax.rsqrt(jnp.mean(o * o, axis=-1, keepdims=True) + EPS)
                gg = g_ref[rows, ls]
                ybr_ref[rows, ls] = ((o * rstd * gnv) * (gg * _sigmoid(gg))).astype(BF16)
            return carry

        lax.fori_loop(0, ncl, chunk, 0)

    def sec(s):
        return pl.BlockSpec((None, tr, hc), lambda hg, b, t: (s, b * nt + t, hg))

    blk = pl.BlockSpec((tr, hc), lambda hg, b, t: (b * nt + t, hg))
    return _pc(body, name="hgrn_fwd",
               out_shape=[jax.ShapeDtypeStruct((m, di), F32), jax.ShapeDtypeStruct((m, di), BF16),
                          jax.ShapeDtypeStruct((m // CHUNK, nheads, HEAD_DIM, HEAD_DIM), F32)],
               grid=(nhg, nb, nt),
               in_specs=[sec(0), sec(1), sec(2), sec(3), pl.BlockSpec((2, hc), lambda hg, b, t: (0, hg)),
                         pl.BlockSpec((1, HEAD_DIM), lambda hg, b, t: (0, 0))],
               out_specs=[blk, blk, pl.BlockSpec((ncl, hpg, HEAD_DIM, HEAD_DIM),
                                                 lambda hg, b, t: (b * nt + t, hg, 0, 0))],
               scratch=[pltpu.VMEM((hpg, HEAD_DIM, HEAD_DIM), F32), pltpu.VMEM((tr, hc), F32),
                        pltpu.VMEM((tr, hc), F32)],
               sem=("parallel", "arbitrary", "arbitrary"))(proj, proj, proj, proj, lbj, gn)


def _hgrn_bwd(proj, o_all, dybr, states, lbj, gn, nb, t_seq, comm=None):
    _, m, di = proj.shape
    tr, hc, hpg = _hgrn_dims(t_seq, di)
    nt, nhg, ncl = t_seq // tr, di // hc, tr // CHUNK

    def body(q_ref, f_ref, i_ref, g_ref, o_ref, dy_ref, st_ref, lb_ref, gn_ref,
             dp_ref, dlb_ref, dgn_ref, dst_scr, a_scr, k_scr, da_scr, dk_scr):
        hg, b, t = pl.program_id(0), pl.program_id(1), pl.program_id(2)

        @pl.when(t == 0)
        def _():
            dst_scr[...] = jnp.zeros_like(dst_scr)

        @pl.when((b == 0) & (t == 0))
        def _():
            dlb_ref[...] = jnp.zeros_like(dlb_ref)

        @pl.when((hg == 0) & (b == 0) & (t == 0))
        def _():
            dgn_ref[...] = jnp.zeros_like(dgn_ref)

        lb = lb_ref[0:1, :]
        sig, fg = _hgrn_gates(f_ref, lb, a_scr, k_scr, tr)
        gnv = gn_ref[...]
        rr = lax.broadcasted_iota(jnp.int32, (CHUNK, CHUNK), 0)
        cc = lax.broadcasted_iota(jnp.int32, (CHUNK, CHUNK), 1)
        causal = cc <= rr
        rowi = lax.broadcasted_iota(jnp.int32, (CHUNK, HEAD_DIM), 0)

        def chunk(it, carry):
            n = ncl - 1 - it
            rows = pl.ds(pl.multiple_of(n * CHUNK, CHUNK), CHUNK)
            lanes = [slice(hd * HEAD_DIM, (hd + 1) * HEAD_DIM) for hd in range(hpg)]
            hs = []
            for hd, ls in enumerate(lanes):
                h = {}
                ah, kh = a_scr[rows, ls], k_scr[rows, ls]
                qp = q_ref[rows, ls]
                sq = _sigmoid(qp)
                qh = qp * sq
                h["dsilu_q"] = sq * (1.0 + qp * (1.0 - sq))
                h["vb"] = i_ref[rows, ls].astype(BF16)
                aref, alast = ah[CHUNK // 2 - 1:CHUNK // 2, :], ah[CHUNK - 1:CHUNK, :]
                h["e1"], h["e2"] = jnp.exp(ah - aref), jnp.exp(aref - ah)
                h["e3"], h["e4"] = jnp.exp(ah), jnp.exp(alast - ah)
                h["dec"] = jnp.exp(alast)
                h["q_in"], h["k_in"], h["q_out"], h["k_out"] = qh * h["e1"], kh * h["e2"], qh * h["e3"], kh * h["e4"]
                for nm in ("q_in", "k_in", "q_out", "k_out"):
                    h[nm + "_b"] = h[nm].astype(BF16)
                o = o_ref[rows, ls]
                rstd = lax.rsqrt(jnp.mean(o * o, axis=-1, keepdims=True) + EPS)
                ohat = o * rstd
                gg = g_ref[rows, ls]
                sg = _sigmoid(gg)
                dyv = dy_ref[rows, ls]
                d_on = dyv * (gg * sg)
                dp_ref[3, rows, ls] = (dyv * (ohat * gnv) * (sg * (1.0 + gg * (1.0 - sg)))).astype(BF16)
                h["dgn"] = jnp.sum(d_on * ohat, axis=0, keepdims=True)
                dohat = d_on * gnv
                do = rstd * (dohat - ohat * jnp.mean(dohat * ohat, axis=-1, keepdims=True))
                h["do_b"] = do.astype(BF16)
                h["st_prev"] = st_ref[n, hd]
                h["dst"] = dst_scr[hd]
                hs.append(h)
            for h in hs:
                dst_b = h["dst"].astype(BF16)
                h["scores"] = _dot_nt(h["q_in_b"], h["k_in_b"])
                h["dscores"] = _dot_nt(h["do_b"], h["vb"])
                h["dv_inter"] = _dot_nt(h["k_out_b"], dst_b)
                h["dq_out"] = _dot(h["do_b"], h["st_prev"].astype(BF16))
                h["dk_out"] = _dot(h["vb"], dst_b)
                h["dst_mm"] = _dot_tn(h["do_b"], h["q_out_b"])
            for h in hs:
                scores = jnp.where(causal, h["scores"], 0.0).astype(BF16)
                dscores = jnp.where(causal, h["dscores"], 0.0).astype(BF16)
                h["dv"] = _dot_tn(scores, h["do_b"]) + h["dv_inter"]
                h["dq_in"] = _dot(dscores, h["k_in_b"])
                h["dk_in"] = _dot_tn(dscores, h["q_in_b"])
            dgn = hs[0]["dgn"]
            for h in hs[1:]:
                dgn = dgn + h["dgn"]
            dgn_ref[...] += dgn
            for hd, (h, ls) in enumerate(zip(hs, lanes)):
                ddec = jnp.sum(h["dst"] * h["st_prev"], axis=0, keepdims=True)
                dst_scr[hd] = h["dst"] * h["dec"] + h["dst_mm"]
                dp_ref[2, rows, ls] = h["dv"].astype(BF16)
                dq = h["dq_in"] * h["e1"] + h["dq_out"] * h["e3"]
                dp_ref[0, rows, ls] = (dq * h["dsilu_q"]).astype(BF16)
                dk_scr[rows, ls] = h["dk_in"] * h["e2"] + h["dk_out"] * h["e4"]
                t_in = h["dq_in"] * h["q_in"] - h["dk_in"] * h["k_in"]
                t_out = h["dk_out"] * h["k_out"]
                da = t_in + h["dq_out"] * h["q_out"] - t_out
                da_ref_row = -jnp.sum(t_in, axis=0, keepdims=True)
                da_last_row = jnp.sum(t_out, axis=0, keepdims=True) + ddec * h["dec"]
                da = da + jnp.where(rowi == CHUNK // 2 - 1, da_ref_row, 0.0) \
                        + jnp.where(rowi == CHUNK - 1, da_last_row, 0.0)
                da_scr[rows, ls] = da
            return carry

        lax.fori_loop(0, ncl, chunk, 0)
        g = min(CUM_ROWS, tr)
        tri = _tri_mask(g, reverse=True)
        for rg in range(tr // g):
            rs = slice(rg * g, (rg + 1) * g)
            dlogf = _tri_apply(tri, da_scr[rs, :])
            df = dlogf / fg[rs, :] - dk_scr[rs, :]
            sgr = sig[rs, :]
            dp_ref[1, rs, :] = (df * (1.0 - lb) * (sgr * (1.0 - sgr))).astype(BF16)
            dlb_ref[...] += jnp.sum(df * (1.0 - sgr), axis=0, keepdims=True) * lb_ref[1:2, :]

    def sec(s):
        return pl.BlockSpec((None, tr, hc), lambda hg, b, t: (s, b * nt + (nt - 1 - t), hg))

    blk = pl.BlockSpec((tr, hc), lambda hg, b, t: (b * nt + (nt - 1 - t), hg))
    return _pc(body, name="hgrn_bwd",
               out_shape=[jax.ShapeDtypeStruct((4, m, di), BF16), jax.ShapeDtypeStruct((1, di), F32),
                          jax.ShapeDtypeStruct((1, HEAD_DIM), F32)],
               grid=(nhg, nb, nt),
               in_specs=[sec(0), sec(1), sec(2), sec(3), blk, blk,
                         pl.BlockSpec((ncl, hpg, HEAD_DIM, HEAD_DIM),
                                      lambda hg, b, t: (b * nt + (nt - 1 - t), hg, 0, 0)),
                         pl.BlockSpec((2, hc), lambda hg, b, t: (0, hg)),
                         pl.BlockSpec((1, HEAD_DIM), lambda hg, b, t: (0, 0))],
               out_specs=[pl.BlockSpec((4, tr, hc), lambda hg, b, t: (0, b * nt + (nt - 1 - t), hg)),
                          pl.BlockSpec((1, hc), lambda hg, b, t: (0, hg)),
                          pl.BlockSpec((1, HEAD_DIM), lambda hg, b, t: (0, 0))],
               scratch=[pltpu.VMEM((hpg, HEAD_DIM, HEAD_DIM), F32)] + [pltpu.VMEM((tr, hc), F32)] * 4,
               sem=("arbitrary", "arbitrary", "arbitrary"), comm=comm)(
                   proj, proj, proj, proj, o_all, dybr, states, lbj, gn)


def _adamw(parts, w, m, v, name, comm=None):
    r, c = w.shape
    tr = _tile(r, 256)
    npart = len(parts)
    c1 = 1.0 - ADAM_B1 ** ADAM_STEP
    c2 = 1.0 - ADAM_B2 ** ADAM_STEP

    def body(*refs):
        p_refs = refs[:npart]
        w_ref, m_ref, v_ref, g_ref, d_ref, nm_ref, nv_ref = refs[npart:]
        g = p_refs[0][...].astype(F32)
        for p in p_refs[1:]:
            g = g + p[...].astype(F32)
        nm = ADAM_B1 * m_ref[...] + (1.0 - ADAM_B1) * g
        nv = ADAM_B2 * v_ref[...] + (1.0 - ADAM_B2) * (g * g)
        g_ref[...] = g
        nm_ref[...] = nm
        nv_ref[...] = nv
        d_ref[...] = -ADAM_LR * ((nm / c1) / (jnp.sqrt(nv / c2) + ADAM_EPS) + ADAM_WD * w_ref[...])

    blk = pl.BlockSpec((tr, c), lambda i: (i, 0))
    return _pc(body, name=name, out_shape=[jax.ShapeDtypeStruct((r, c), F32)] * 4, grid=(r // tr,),
               in_specs=[blk] * (npart + 3), out_specs=[blk] * 4, sem=("parallel",), comm=comm)(*parts, w, m, v)


_EARLY = ["a_ln_gain", "a_ln_bias", "a_w_s", "a_b_s", "b_lower_bounds", "b_gn_gain"]


def _pack(arrs):
    flat = jnp.concatenate([a.reshape(-1) for a in arrs])
    rows = -(-flat.shape[0] // 1024) * 8
    return jnp.pad(flat, (0, rows * 128 - flat.shape[0])).reshape(rows, 128)


def _unpack(buf, like):
    flat = buf.reshape(-1)
    out, off = [], 0
    for a in like:
        out.append(flat[off:off + a.size].reshape(a.shape))
        off += a.size
    return out


def kernel(x, c, norm_gain, w_ada, b_ada, a_w_in, a_ln_gain, a_ln_bias, a_w_s, a_b_s, a_w_out, b_w_in, b_lower_bounds, b_gn_gain, b_w_out, final_gain, loss_target, m_norm_gain, m_w_ada, m_b_ada, m_a_w_in, m_a_ln_gain, m_a_ln_bias, m_a_w_s, m_a_b_s, m_a_w_out, m_b_w_in, m_b_lower_bounds, m_b_gn_gain, m_b_w_out, m_final_gain, v_norm_gain, v_w_ada, v_b_ada, v_a_w_in, v_a_ln_gain, v_a_ln_bias, v_a_w_s, v_a_b_s, v_a_w_out, v_b_w_in, v_b_lower_bounds, v_b_gn_gain, v_b_w_out, v_final_gain):
    w = dict(norm_gain=norm_gain, w_ada=w_ada, b_ada=b_ada, a_w_in=a_w_in, a_ln_gain=a_ln_gain,
             a_ln_bias=a_ln_bias, a_w_s=a_w_s, a_b_s=a_b_s, a_w_out=a_w_out, b_w_in=b_w_in,
             b_lower_bounds=b_lower_bounds, b_gn_gain=b_gn_gain, b_w_out=b_w_out, final_gain=final_gain)
    mo = dict(norm_gain=m_norm_gain, w_ada=m_w_ada, b_ada=m_b_ada, a_w_in=m_a_w_in, a_ln_gain=m_a_ln_gain,
              a_ln_bias=m_a_ln_bias, a_w_s=m_a_w_s, a_b_s=m_a_b_s, a_w_out=m_a_w_out, b_w_in=m_b_w_in,
              b_lower_bounds=m_b_lower_bounds, b_gn_gain=m_b_gn_gain, b_w_out=m_b_w_out, final_gain=m_final_gain)
    vo = dict(norm_gain=v_norm_gain, w_ada=v_w_ada, b_ada=v_b_ada, a_w_in=v_a_w_in, a_ln_gain=v_a_ln_gain,
              a_ln_bias=v_a_ln_bias, a_w_s=v_a_w_s, a_b_s=v_a_b_s, a_w_out=v_a_w_out, b_w_in=v_b_w_in,
              b_lower_bounds=v_b_lower_bounds, b_gn_gain=v_b_gn_gain, b_w_out=v_b_w_out, final_gain=v_final_gain)

    nb, t_seq, d = x.shape
    m = nb * t_seq
    ncol_ada = w_ada.shape[2]
    xi, yi, ci = lax.axis_index("x"), lax.axis_index("y"), lax.axis_index("c")
    me = 4 * xi + 2 * yi + ci

    c_g, wa_in_g = _all_gather([c, a_w_in[0].astype(BF16)], "gather_c_wa")

    c_all = c_g.reshape(NDEV * nb, d)
    b_cols = lax.dynamic_slice(b_ada, (0, me * ncol_ada), (2, ncol_ada)).reshape(2, 1, ncol_ada)
    mod_part, lbj = _ada_fwd(c_all, w_ada, b_cols, b_lower_bounds)
    mod_all = _all_gather([mod_part], "gather_mod")[0]
    mod_mine = lax.dynamic_slice_in_dim(mod_all, me * nb, nb, axis=2)
    mod_mine = mod_mine.transpose(1, 2, 0, 3).reshape(2, nb, 3, d)
    mod0, mod1 = mod_mine[0], mod_mine[1]

    di = a_w_out.shape[1] * NDEV

    xf = x.reshape(m, d)
    tgt = loss_target.reshape(m, d)
    ng0, ng1 = norm_gain[0:1], norm_gain[1:2]
    ncb = b_w_in.shape[2]
    wb_lo, wb_hi = b_w_in[0][:, :ncb // 2].astype(BF16), b_w_in[0][:, ncb // 2:].astype(BF16)
    h0, h0_t = _prenorm(xf, ng0, mod0, t_seq, "prenorm_a")
    proj_a, half = _mm_in(h0, [wa_in_g], 1, "in_proj_a", comm=_gather_first([a_w_out[0].astype(BF16), wb_lo]))
    bs_t = jnp.pad(a_b_s[0].T, ((0, 0), (0, 128 - SG_GROUPS)))
    ybr_a, (wa_out_g, wb_lo_g, wb_hi_half) = _a_mid_fwd(
        proj_a, a_ln_gain, a_ln_bias, a_w_s[0], bs_t, t_seq, comm=_join(_gather_second(half), _gather_first([wb_hi])))
    wa_out = wa_out_g.reshape(di, d)
    (yout_a, x1), (wb_hi_g, wb_out_half) = _out_proj(
        ybr_a, wa_out, xf, mod0, t_seq, "out_proj_a",
        comm=_join(_gather_second([wb_hi_half]), _gather_first([b_w_out[0].astype(BF16)])))
    wb_in_g = [wb_lo_g, wb_hi_g]
    h1, h1_t = _prenorm(x1, ng1, mod1, t_seq, "prenorm_b")
    proj_b, (wb_out_g,) = _mm_in(h1, wb_in_g, 4, "in_proj_b", comm=_gather_second([wb_out_half]))
    wb_out = wb_out_g.reshape(di, d)
    o_b, ybr_b, states = _hgrn_fwd(proj_b, lbj, b_gn_gain, nb, t_seq)
    yout_b, dx2, loss_part, d_final_gain = _out_proj_loss(ybr_b, wb_out, x1, mod1, final_gain.reshape(1, d), tgt, t_seq)

    rows_out = a_w_out.shape[1]
    dy_b, dgate1, dybr_b = _gate_dybr(dx2, yout_b, mod1, wb_out, t_seq, "dybr_b")
    rs_wb_out = _ReduceScatter(_mm_dw_out(ybr_b, dy_b, "dw_out_b").reshape(NDEV, rows_out, d), "b_w_out")
    (dproj_b, d_lb, d_gn), got = _hgrn_bwd(proj_b, o_b, dybr_b, states, lbj, b_gn_gain, nb, t_seq,
                                           comm=rs_wb_out.swap_core())
    rs_wb_out.after_core(got[0])
    dh1, got = _mm_din(dproj_b, wb_in_g, 4, "dh_b", comm=rs_wb_out.swap_chips())
    rs_wb_out.after_chips(got[0])
    dx1, dss1, dgain1 = _prenorm_bwd(dh1, x1, ng1, mod1, dx2, t_seq, "prenorm_bwd_b")
    rs_wb_in = _ReduceScatter(_mm_dw_in(h1_t, dproj_b, ncb, 4, "dw_in_b"), "b_w_in")

    dy_a, dgate0, dybr_a = _gate_dybr(dx1, yout_a, mod0, wa_out, t_seq, "dybr_a")
    g_wa_out, got = _mm_dw_out(ybr_a, dy_a, "dw_out_a", comm=rs_wb_in.swap_core())
    rs_wb_in.after_core(got[0])
    rs_wa_out = _ReduceScatter(g_wa_out.reshape(NDEV, rows_out, d), "a_w_out")
    (dproj_a, d_lng, d_lnb, d_ws, d_bs_t), got = _a_mid_bwd(
        proj_a, dybr_a, a_ln_gain, a_ln_bias, a_w_s[0], bs_t, t_seq,
        comm=_join(rs_wb_in.swap_chips(), rs_wa_out.swap_core()))
    rs_wb_in.after_chips(got[0])
    rs_wa_out.after_core(got[1])
    part = dict(a_ln_gain=d_lng, a_ln_bias=d_lnb, a_w_s=d_ws[None], a_b_s=d_bs_t[:, :SG_GROUPS].T[None],
                b_lower_bounds=jnp.concatenate([-d_lb, d_lb], axis=0), b_gn_gain=d_gn)
    early_pack = _pack([part[k].reshape(w[k].shape) for k in _EARLY])
    g_wa_in, got = _mm_dw_in(h0_t, dproj_a, wa_in_g.shape[2], 1, "dw_in_a",
                             comm=_join(rs_wa_out.swap_chips(), _gather_first([early_pack])))
    rs_wa_out.after_chips(got[0])
    rs_wa_in = _ReduceScatter(g_wa_in, "a_w_in")
    n_tiles = m // _din_tile(m)
    assert n_tiles >= 2
    first_tiles = max(1, (3 * n_tiles) // 8)
    dh0, got2 = _mm_din(dproj_a, [wa_in_g], 1, "dh_a_first", tiles=(0, first_tiles),
                        comm=_join(rs_wa_in.swap_core(), _gather_second([got[1]])))
    rs_wa_in.after_core(got2[0])
    early_all = got2[1]
    dh0, got = _mm_din(dproj_a, [wa_in_g], 1, "dh_a_rest", comm=rs_wa_in.swap_chips(),
                       tiles=(first_tiles, n_tiles - first_tiles), prev=dh0)
    rs_wa_in.after_chips(got[0])
    dx0, dss0, dgain0 = _prenorm_bwd(dh0, xf, ng0, mod0, dx1, t_seq, "prenorm_bwd_a")
    grad_x = dx0.reshape(nb, t_seq, d)

    dmod = jnp.stack([jnp.concatenate([dss0, dgate0], axis=1), jnp.concatenate([dss1, dgate1], axis=1)])
    late_like = [norm_gain, final_gain, loss_part.reshape(1)]
    late_pack = _pack([jnp.concatenate([dgain0, dgain1], axis=0), d_final_gain[0], loss_part.reshape(1)])
    dmod_all, late_all = _all_gather([dmod.reshape(2, nb, 3 * d), late_pack], "gather_tail")
    dmod_all = dmod_all.transpose(1, 0, 2, 3).reshape(2, NDEV * nb, 3 * d)
    dmod_cols = lax.dynamic_slice_in_dim(dmod_all, me * ncol_ada, ncol_ada, axis=2)
    g_w_ada, g_b_ada = _ada_bwd(c_all, dmod_cols, dmod_all)

    res = {}
    early_like = [w[k] for k in _EARLY]
    sm = _adamw([early_all[k] for k in range(NDEV)], _pack(early_like), _pack([mo[k] for k in _EARLY]),
                _pack([vo[k] for k in _EARLY]), "adamw_small_early")
    sm = [dict(zip(_EARLY, _unpack(buf, early_like))) for buf in sm]
    for k in _EARLY:
        res[k] = tuple(s[k] for s in sm)
    zero = jnp.zeros((1,), F32)
    sm = _adamw([late_all[k] for k in range(NDEV)], _pack([norm_gain, final_gain, zero]),
                _pack([mo["norm_gain"], mo["final_gain"], zero]), _pack([vo["norm_gain"], vo["final_gain"], zero]),
                "adamw_small_late")
    sm = [_unpack(buf, late_like) for buf in sm]
    res["norm_gain"] = tuple(s[0] for s in sm)
    res["final_gain"] = tuple(s[1] for s in sm)
    loss = sm[0][2][0]
    rb = _adamw([g_b_ada], b_ada, mo["b_ada"], vo["b_ada"], "adamw_b_ada")
    res["b_ada"] = tuple(rb)
    sh = w_ada.shape
    ra = _adamw([g_w_ada.reshape(sh[0] * sh[1], sh[2])], w_ada.reshape(sh[0] * sh[1], sh[2]),
                mo["w_ada"].reshape(sh[0] * sh[1], sh[2]), vo["w_ada"].reshape(sh[0] * sh[1], sh[2]), "adamw_w_ada")
    res["w_ada"] = tuple(z.reshape(sh) for z in ra)

    for k, rs in (("b_w_out", rs_wb_out), ("b_w_in", rs_wb_in), ("a_w_out", rs_wa_out), ("a_w_in", rs_wa_in)):
        res[k] = tuple(z[None] for z in _adamw(rs.parts, w[k][0], mo[k][0], vo[k][0], "adamw_" + k))

    order = ["norm_gain", "w_ada", "b_ada", "a_w_in", "a_ln_gain", "a_ln_bias", "a_w_s", "a_b_s", "a_w_out",
             "b_w_in", "b_lower_bounds", "b_gn_gain", "b_w_out", "final_gain"]
    return (loss, grad_x, *[res[k][0] for k in order], *[res[k][1] for k in order],
            *[res[k][2] for k in order], *[res[k][3] for k in order])
```

```python
import functools
import math

import jax
import jax.numpy as jnp
from jax import lax
from jax.experimental import pallas as pl
from jax.experimental.pallas import tpu as pltpu

F32 = jnp.float32
BF16 = jnp.bfloat16
MESH = pl.DeviceIdType.MESH
NDEV = 8
EPS = 1e-6
CHUNK = 64
SG_BLOCK = 128
SG_GROUPS = 8
HEAD_DIM = 128
CUM_ROWS = 256
ADAM_LR, ADAM_B1, ADAM_B2, ADAM_EPS, ADAM_WD, ADAM_STEP = 0.001, 0.9, 0.999, 1e-08, 0.01, 10
VMEM_LIMIT = 56 * 1024 * 1024
ANY = pl.BlockSpec(memory_space=pl.ANY)


class _Hosted:
    def __init__(self, arrays, out_shapes, nsem, start, finish, aliases=None):
        self.arrays, self.out_shapes, self.nsem = list(arrays), list(out_shapes), nsem
        self.start, self.finish = start, finish
        self.aliases = dict(aliases or {})


def _join(*comms):
    arrays, outs, aliases, offs, nsem = [], [], {}, [], 0
    for cm in comms:
        offs.append((len(arrays), len(outs), nsem))
        for i, o in cm.aliases.items():
            aliases[len(arrays) + i] = len(outs) + o
        arrays += cm.arrays
        outs += cm.out_shapes
        nsem += cm.nsem

    def run(which):
        def f(ins, outs_, ss, rs, base):
            for cm, (ia, io, isem) in zip(comms, offs):
                getattr(cm, which)(ins[ia:ia + len(cm.arrays)], outs_[io:io + len(cm.out_shapes)], ss, rs, base + isem)
        return f

    return _Hosted(arrays, outs, nsem, run("start"), run("finish"), aliases)


def _pc(body, *, name, out_shape, grid=None, in_specs=None, out_specs=None, scratch=(), sem=None,
        grid_spec=None, comm=None, aliases=None):
    cp = dict(vmem_limit_bytes=VMEM_LIMIT)
    aliases = dict(aliases or {})
    if comm is None:
        if sem is not None:
            cp["dimension_semantics"] = sem
        kw = {"input_output_aliases": aliases}
        if grid_spec is not None:
            kw["grid_spec"] = grid_spec
        else:
            if grid is not None:
                kw["grid"] = grid
            if in_specs is not None:
                kw["in_specs"] = in_specs
            if out_specs is not None:
                kw["out_specs"] = out_specs
            kw["scratch_shapes"] = list(scratch)
        return pl.pallas_call(functools.partial(body), name=name, out_shape=out_shape,
                              compiler_params=pltpu.CompilerParams(**cp), **kw)

    single = not isinstance(out_shape, (list, tuple))
    outs_list = [out_shape] if single else list(out_shape)
    ospecs = [out_specs] if single else list(out_specs)
    n_in, n_out, n_ci, n_co, n_scr = len(in_specs), len(outs_list), len(comm.arrays), len(comm.out_shapes), len(scratch)
    cp["dimension_semantics"] = ("arbitrary",) * len(grid)

    def hosted(*refs):
        cin, hin = refs[:n_in], refs[n_in:n_in + n_ci]
        cout = refs[n_in + n_ci:n_in + n_ci + n_out]
        hout = refs[n_in + n_ci + n_out:n_in + n_ci + n_out + n_co]
        scr = refs[n_in + n_ci + n_out + n_co:n_in + n_ci + n_out + n_co + n_scr]
        ssem, rsem = refs[-2], refs[-1]
        first = functools.reduce(lambda p, q: p & q, [pl.program_id(a) == 0 for a in range(len(grid))])
        last = functools.reduce(lambda p, q: p & q, [pl.program_id(a) == grid[a] - 1 for a in range(len(grid))])

        @pl.when(first)
        def _():
            comm.start(hin, hout, ssem, rsem, 0)

        body(*cin, *cout, *scr)

        @pl.when(last)
        def _():
            comm.finish(hin, hout, ssem, rsem, 0)

    call = pl.pallas_call(
        hosted, name=name, grid=grid, in_specs=list(in_specs) + [ANY] * n_ci, out_specs=ospecs + [ANY] * n_co,
        out_shape=outs_list + comm.out_shapes,
        scratch_shapes=list(scratch) + [pltpu.SemaphoreType.DMA((comm.nsem,)), pltpu.SemaphoreType.DMA((comm.nsem,))],
        input_output_aliases={**aliases, **{n_in + i: n_out + o for i, o in comm.aliases.items()}},
        compiler_params=pltpu.CompilerParams(**cp))

    def run(*args):
        res = call(*args, *comm.arrays)
        comp = res[:n_out]
        return (comp[0] if single else comp), list(res[n_out:])

    return run


def _tile(n, pref):
    return pref if n % pref == 0 else n


def _sigmoid(x):
    return 1.0 / (1.0 + jnp.exp(-x))


def _gelu(x):
    c = math.sqrt(2.0 / math.pi)
    return 0.5 * x * (1.0 + jnp.tanh(c * (x + 0.044715 * (x * x * x))))


def _gelu_and_grad(x):
    c = math.sqrt(2.0 / math.pi)
    x2 = x * x
    t = jnp.tanh(c * (x + 0.044715 * (x2 * x)))
    half = 0.5 * (1.0 + t)
    return x * half, half + (0.5 * x) * (1.0 - t * t) * (c + (3.0 * 0.044715 * c) * x2)


def _dot(a, b):
    return jnp.dot(a, b, preferred_element_type=F32)


def _dot_nt(a, b):
    return lax.dot_general(a, b, (((1,), (1,)), ((), ())), preferred_element_type=F32)


def _dot_tn(a, b):
    return lax.dot_general(a, b, (((0,), (0,)), ((), ())), preferred_element_type=F32)


def _tri_mask(n, reverse):
    r = lax.broadcasted_iota(jnp.int32, (n, n), 0)
    c = lax.broadcasted_iota(jnp.int32, (n, n), 1)
    same = (r // CHUNK) == (c // CHUNK)
    tri = (c >= r) if reverse else (c <= r)
    return jnp.where(same & tri, 1.0, 0.0).astype(BF16)


def _tri_apply(tri, x):
    hi = x.astype(BF16)
    r1 = x - hi.astype(F32)
    mid = r1.astype(BF16)
    lo = (r1 - mid.astype(F32)).astype(BF16)
    return _dot(tri, hi) + (_dot(tri, mid) + _dot(tri, lo))


def _all_gather(arrs, name):
    n = len(arrs)

    def body(*refs):
        ins, outs = refs[:n], refs[n:2 * n]
        send_sems, recv_sems, local_sems = refs[2 * n:]
        x, y, c = lax.axis_index("x"), lax.axis_index("y"), lax.axis_index("c")
        me, sibling = (x, y, c), (x, y, 1 - c)
        chips = [(1 - x, y), (x, 1 - y), (1 - x, 1 - y)]

        def blk(a, p):
            return outs[a].at[4 * p[0] + 2 * p[1] + p[2]]

        def copy(a, k, block, to, src=None):
            return pltpu.make_async_remote_copy(
                src_ref=blk(a, block) if src is None else src, dst_ref=blk(a, block),
                send_sem=send_sems.at[7 * a + k], recv_sem=recv_sems.at[7 * a + k],
                device_id=to, device_id_type=MESH)

        mine = [pltpu.make_async_copy(ins[a], blk(a, me), local_sems.at[a]) for a in range(n)]
        for m in mine:
            m.start()
        first = []
        for a in range(n):
            first.append(copy(a, 0, me, sibling, src=ins[a]))
            for j, chip in enumerate(chips):
                first.append(copy(a, 1 + j, me, (*chip, c), src=ins[a]))
        for cp in first:
            cp.start()
        passed = []
        for j, chip in enumerate(chips):
            for a in range(n):
                copy(a, 1 + j, (*chip, c), me).wait_recv()
                p = copy(a, 4 + j, (*chip, c), sibling)
                p.start()
                passed.append(p)
        for a in range(n):
            copy(a, 0, sibling, me).wait_recv()
            for j, chip in enumerate(chips):
                copy(a, 4 + j, (*chip, 1 - c), me).wait_recv()
        for cp in first + passed:
            cp.wait_send()
        for m in mine:
            m.wait()

    out_shape = [jax.ShapeDtypeStruct((NDEV,) + a.shape, a.dtype) for a in arrs]
    return _pc(body, name=name, out_shape=out_shape, in_specs=[ANY] * n, out_specs=[ANY] * n,
               scratch=[pltpu.SemaphoreType.DMA((7 * n,)), pltpu.SemaphoreType.DMA((7 * n,)),
                        pltpu.SemaphoreType.DMA((n,))])(*arrs)


def _gather_first(arrs):
    n = len(arrs)

    def parts(ins, outs, ss, rs, base):
        x, y, c = lax.axis_index("x"), lax.axis_index("y"), lax.axis_index("c")
        me, sibling = (x, y, c), (x, y, 1 - c)
        chips = [(1 - x, y), (x, 1 - y), (1 - x, 1 - y)]

        def blk(a, p):
            return outs[a].at[4 * p[0] + 2 * p[1] + p[2]]

        def copy(a, k, block, to):
            return pltpu.make_async_remote_copy(
                src_ref=ins[a], dst_ref=blk(a, block), send_sem=ss.at[base + 4 * a + k],
                recv_sem=rs.at[base + 4 * a + k], device_id=to, device_id_type=MESH)

        local = [pltpu.make_async_copy(ins[a], blk(a, me), ss.at[base + 4 * n + a]) for a in range(n)]
        sends, recvs = [], []
        for a in range(n):
            sends.append(copy(a, 0, me, sibling))
            recvs.append(copy(a, 0, sibling, me))
            for j, chip in enumerate(chips):
                sends.append(copy(a, 1 + j, me, (*chip, c)))
                recvs.append(copy(a, 1 + j, (*chip, c), me))
        return local, sends, recvs

    def start(ins, outs, ss, rs, base):
        local, sends, _ = parts(ins, outs, ss, rs, base)
        for cp in local + sends:
            cp.start()

    def finish(ins, outs, ss, rs, base):
        local, sends, recvs = parts(ins, outs, ss, rs, base)
        for cp in recvs:
            cp.wait_recv()
        for cp in sends:
            cp.wait_send()
        for cp in local:
            cp.wait()

    return _Hosted(arrs, [jax.ShapeDtypeStruct((NDEV,) + a.shape, a.dtype) for a in arrs], 5 * n, start, finish)


def _gather_second(bufs):
    n = len(bufs)

    def parts(ins, outs, ss, rs, base):
        x, y, c = lax.axis_index("x"), lax.axis_index("y"), lax.axis_index("c")
        sibling = (x, y, 1 - c)
        chips = [(1 - x, y), (x, 1 - y), (1 - x, 1 - y)]
        sends, recvs = [], []
        for a in range(n):
            for j, chip in enumerate(chips):
                mine = 4 * chip[0] + 2 * chip[1] + c
                theirs = 4 * chip[0] + 2 * chip[1] + (1 - c)
                sends.append(pltpu.make_async_remote_copy(
                    src_ref=ins[a].at[mine], dst_ref=outs[a].at[mine], send_sem=ss.at[base + 3 * a + j],
                    recv_sem=rs.at[base + 3 * a + j], device_id=sibling, device_id_type=MESH))
                recvs.append(pltpu.make_async_remote_copy(
                    src_ref=ins[a].at[theirs], dst_ref=outs[a].at[theirs], send_sem=ss.at[base + 3 * a + j],
                    recv_sem=rs.at[base + 3 * a + j], device_id=sibling, device_id_type=MESH))
        return sends, recvs

    def start(ins, outs, ss, rs, base):
        for cp in parts(ins, outs, ss, rs, base)[0]:
            cp.start()

    def finish(ins, outs, ss, rs, base):
        sends, recvs = parts(ins, outs, ss, rs, base)
        for cp in recvs:
            cp.wait_recv()
        for cp in sends:
            cp.wait_send()

    return _Hosted(bufs, [jax.ShapeDtypeStruct(b.shape, b.dtype) for b in bufs], 3 * n, start, finish,
                   aliases={a: a for a in range(n)})


def _swap(src, nblk, ids_fn, partner_fn):
    def copies(ins, outs, ss, rs, base):
        x, y, c = lax.axis_index("x"), lax.axis_index("y"), lax.axis_index("c")
        ids = ids_fn(x, y, c)
        partner = partner_fn(x, y, c)
        return [pltpu.make_async_remote_copy(
            src_ref=ins[0].at[ids[k]], dst_ref=outs[0].at[k], send_sem=ss.at[base + k], recv_sem=rs.at[base + k],
            device_id=partner, device_id_type=MESH) for k in range(nblk)]

    def start(ins, outs, ss, rs, base):
        for cp in copies(ins, outs, ss, rs, base):
            cp.start()

    def finish(ins, outs, ss, rs, base):
        for cp in copies(ins, outs, ss, rs, base):
            cp.wait()

    return _Hosted([src], [jax.ShapeDtypeStruct((nblk,) + src.shape[1:], src.dtype)], nblk, start, finish)


def _blocking(comm, name):
    n_i, n_o = len(comm.arrays), len(comm.out_shapes)

    def body(*refs):
        ins, outs = refs[:n_i], refs[n_i:n_i + n_o]
        comm.start(ins, outs, refs[-2], refs[-1], 0)
        comm.finish(ins, outs, refs[-2], refs[-1], 0)

    return pl.pallas_call(
        body, name=name, out_shape=comm.out_shapes, in_specs=[ANY] * n_i, out_specs=[ANY] * n_o,
        scratch_shapes=[pltpu.SemaphoreType.DMA((comm.nsem,)), pltpu.SemaphoreType.DMA((comm.nsem,))],
        input_output_aliases=comm.aliases)(*comm.arrays)


def _swap_chips(send):
    def copies(ins, outs, ss, rs, base):
        x, y, c = lax.axis_index("x"), lax.axis_index("y"), lax.axis_index("c")
        chips = [(1 - x, y), (x, 1 - y), (1 - x, 1 - y)]
        return [pltpu.make_async_remote_copy(
            src_ref=ins[0].at[j], dst_ref=outs[0].at[j], send_sem=ss.at[base + j], recv_sem=rs.at[base + j],
            device_id=(*chip, c), device_id_type=MESH) for j, chip in enumerate(chips)]

    def start(ins, outs, ss, rs, base):
        for cp in copies(ins, outs, ss, rs, base):
            cp.start()

    def finish(ins, outs, ss, rs, base):
        for cp in copies(ins, outs, ss, rs, base):
            cp.wait()

    return _Hosted([send], [jax.ShapeDtypeStruct(send.shape, send.dtype)], 3, start, finish)


def _add_send(a, b, idx, ns, name):
    _, r, c = a.shape
    tr = _tile(r, 256)

    def body(idx_ref, a_ref, b_ref, send_ref):
        send_ref[...] = (a_ref[...] + b_ref[...]).astype(BF16)

    def sel(off):
        return pl.BlockSpec((None, tr, c), lambda k, i, s: (s[off + k], i, 0))

    gs = pltpu.PrefetchScalarGridSpec(num_scalar_prefetch=1, grid=(ns, r // tr), in_specs=[sel(0), sel(ns)],
                                      out_specs=pl.BlockSpec((None, tr, c), lambda k, i, s: (k, i, 0)))
    return _pc(body, name=name, grid_spec=gs, sem=("arbitrary", "arbitrary"),
               out_shape=jax.ShapeDtypeStruct((ns, r, c), BF16))(idx, a, b)


class _ReduceScatter:
    def __init__(self, g, tag):
        self.g, self.tag = g, tag

    def swap_core(self):
        return _swap(self.g, 4, lambda x, y, c: [1 - c, 3 - c, 5 - c, 7 - c], lambda x, y, c: (x, y, 1 - c))

    def after_core(self, recv):
        x, y, c = lax.axis_index("x"), lax.axis_index("y"), lax.axis_index("c")
        chips = [(1 - x, y), (x, 1 - y), (1 - x, 1 - y)]
        idx = jnp.stack([4 * p + 2 * q + c for p, q in chips] + [2 * p + q for p, q in chips]).astype(jnp.int32)
        self.send = _add_send(self.g, recv, idx, 3, "rs_add_" + self.tag)
        self.recv_core = recv
        zero = jnp.zeros((), jnp.int32)
        self.idx = jnp.stack([4 * x + 2 * y + c, 2 * x + y, zero, zero + 1, zero + 2]).astype(jnp.int32)

    def swap_chips(self):
        return _swap_chips(self.send)

    def after_chips(self, recv):
        self.parts = [self.g, self.recv_core, recv, recv, recv]


def _ada_fwd(c_all, w_ada, b_cols, b_lb):
    nl, d, ncol = w_ada.shape
    nseq = c_all.shape[0]
    di = b_lb.shape[1]

    def body(c_ref, w_ref, b_ref, lb_ref, mod_ref, lbj_ref):
        cv = c_ref[...]
        cact = (cv * _sigmoid(cv)).astype(BF16)
        for l in range(nl):
            mod_ref[l] = _dot(cact, w_ref[l].astype(BF16)) + b_ref[l]
        b0, b1 = lb_ref[0:1, :], lb_ref[1:2, :]
        mx = jnp.maximum(b0, b1)
        e0, e1 = jnp.exp(b0 - mx), jnp.exp(b1 - mx)
        s = e0 + e1
        p0, p1 = e0 / s, e1 / s
        lbj_ref[0:1, :] = (p0 + p1) - p0
        lbj_ref[1:2, :] = p0 * p1

    return _pc(body, name="ada_fwd",
               out_shape=[jax.ShapeDtypeStruct((nl, nseq, ncol), F32), jax.ShapeDtypeStruct((2, di), F32)]
               )(c_all, w_ada, b_cols, b_lb)


def _ada_bwd(c_all, dmod_cols, dmod_full):
    nl, nseq, ncol = dmod_cols.shape
    d = c_all.shape[1]
    d3 = dmod_full.shape[2]

    def body(c_ref, dc_ref, df_ref, gw_ref, gb_ref):
        cv = c_ref[...]
        cact = (cv * _sigmoid(cv)).astype(BF16)
        for l in range(nl):
            gw_ref[l] = _dot_tn(cact, dc_ref[l].astype(BF16))
            gb_ref[l:l + 1, :] = jnp.sum(df_ref[l], axis=0, keepdims=True)

    return _pc(body, name="ada_bwd",
               out_shape=[jax.ShapeDtypeStruct((nl, d, ncol), F32), jax.ShapeDtypeStruct((nl, d3), F32)]
               )(c_all, dmod_cols, dmod_full)


def _prenorm(x, gain, mod, t_seq, name):
    m, d = x.shape
    tm = _tile(t_seq, 512)
    per = t_seq // tm

    def body(x_ref, g_ref, mod_ref, h_ref, ht_ref):
        xv = x_ref[...]
        rstd = lax.rsqrt(jnp.mean(xv * xv, axis=-1, keepdims=True) + EPS)
        r = xv * rstd * g_ref[...]
        h = r * (1.0 + mod_ref[0, 1:2, :]) + mod_ref[0, 0:1, :]
        h_ref[...] = h.astype(BF16)
        ht_ref[...] = h.T.astype(BF16)

    return _pc(body, name=name, out_shape=[jax.ShapeDtypeStruct((m, d), BF16), jax.ShapeDtypeStruct((d, m), BF16)],
               grid=(m // tm,),
               in_specs=[pl.BlockSpec((tm, d), lambda i: (i, 0)), pl.BlockSpec((1, d), lambda i: (0, 0)),
                         pl.BlockSpec((1, 3, d), lambda i: (i // per, 0, 0))],
               out_specs=[pl.BlockSpec((tm, d), lambda i: (i, 0)), pl.BlockSpec((d, tm), lambda i: (0, i))],
               sem=("parallel",))(x, gain, mod)


def _prenorm_bwd(dh, x, gain, mod, dxn, t_seq, name, comm=None):
    m, d = x.shape
    nb = m // t_seq
    tm = _tile(t_seq, 512)
    per = t_seq // tm

    def body(dh_ref, x_ref, g_ref, mod_ref, dxn_ref, dx_ref, dss_ref, dg_ref):
        i = pl.program_id(0)
        xv, dhv, g = x_ref[...], dh_ref[...], g_ref[...]
        rstd = lax.rsqrt(jnp.mean(xv * xv, axis=-1, keepdims=True) + EPS)
        xhat = xv * rstd
        dr = dhv * (1.0 + mod_ref[0, 1:2, :])
        dxhat = dr * g
        dx_ref[...] = dxn_ref[...] + rstd * (dxhat - xhat * jnp.mean(dxhat * xhat, axis=-1, keepdims=True))

        @pl.when(i % per == 0)
        def _():
            dss_ref[...] = jnp.zeros_like(dss_ref)

        @pl.when(i == 0)
        def _():
            dg_ref[...] = jnp.zeros_like(dg_ref)

        dss_ref[0, 0:1, :] += jnp.sum(dhv, axis=0, keepdims=True)
        dss_ref[0, 1:2, :] += jnp.sum(dhv * (xhat * g), axis=0, keepdims=True)
        dg_ref[...] += jnp.sum(dr * xhat, axis=0, keepdims=True)

    row = pl.BlockSpec((tm, d), lambda i: (i, 0))
    return _pc(body, name=name,
               out_shape=[jax.ShapeDtypeStruct((m, d), F32), jax.ShapeDtypeStruct((nb, 2, d), F32),
                          jax.ShapeDtypeStruct((1, d), F32)],
               grid=(m // tm,),
               in_specs=[row, row, pl.BlockSpec((1, d), lambda i: (0, 0)),
                         pl.BlockSpec((1, 3, d), lambda i: (i // per, 0, 0)), row],
               out_specs=[row, pl.BlockSpec((1, 2, d), lambda i: (i // per, 0, 0)),
                          pl.BlockSpec((1, d), lambda i: (0, 0))],
               sem=("arbitrary",), comm=comm)(dh, x, gain, mod, dxn)


def _mm_in(h, ws, sections, name, comm=None):
    m, k = h.shape
    nw, ncp = len(ws), ws[0].shape[2]
    nc = nw * ncp
    per = NDEV // sections if sections > 1 else NDEV
    tm = _tile(m, 512)
    assert per % 2 == 0

    def body(*refs):
        hv = refs[0][...]
        o_ref = refs[1 + nw]
        for b in range(2):
            for a in range(nw):
                lo = b * nc + a * ncp
                o_ref[:, lo:lo + ncp] = _dot(hv, refs[1 + a][b])

    w_spec = pl.BlockSpec((2, k, ncp), lambda j, i: (j, 0, 0))
    if sections > 1:
        out_shape = jax.ShapeDtypeStruct((sections, m, per * nc), F32)
        out_spec = pl.BlockSpec((None, tm, 2 * nc), lambda j, i: ((2 * j) // per, i, ((2 * j) % per) // 2))
    else:
        out_shape = jax.ShapeDtypeStruct((m, NDEV * nc), F32)
        out_spec = pl.BlockSpec((tm, 2 * nc), lambda j, i: (i, j))
    return _pc(body, name=name, out_shape=out_shape, grid=(NDEV // 2, m // tm),
               in_specs=[pl.BlockSpec((tm, k), lambda j, i: (i, 0))] + [w_spec] * nw,
               out_specs=out_spec, sem=("parallel", "parallel"), comm=comm)(h, *ws)


def _din_tile(m):
    return 1024 if m % 1024 == 0 and m >= 2048 else _tile(m, 512)


def _mm_din(dproj, ws, sections, name, comm=None, tiles=None, prev=None):
    nw, k, ncp = len(ws), ws[0].shape[1], ws[0].shape[2]
    nc = nw * ncp
    m = dproj.shape[-2]
    tm = _din_tile(m)
    t0, nt = tiles if tiles is not None else (0, m // tm)
    per = NDEV // sections if sections > 1 else NDEV
    assert per % 2 == 0

    def body(*refs):
        d_ref, o_ref = refs[0], refs[-1]
        j = pl.program_id(1)
        acc = None
        for b in range(2):
            for a in range(nw):
                lo = b * nc + a * ncp
                term = _dot_nt(d_ref[:, lo:lo + ncp], refs[1 + a][b])
                acc = term if acc is None else acc + term

        @pl.when(j == 0)
        def _():
            o_ref[...] = acc

        @pl.when(j > 0)
        def _():
            o_ref[...] += acc

    if sections > 1:
        dspec = pl.BlockSpec((None, tm, 2 * nc), lambda i, j: ((2 * j) // per, i + t0, ((2 * j) % per) // 2))
    else:
        dspec = pl.BlockSpec((tm, 2 * nc), lambda i, j: (i + t0, j))
    in_specs = [dspec] + [pl.BlockSpec((2, k, ncp), lambda i, j: (j, 0, 0))] * nw
    args = [dproj, *ws]
    if prev is not None:
        in_specs.append(ANY)
        args.append(prev)
    return _pc(body, name=name, out_shape=jax.ShapeDtypeStruct((m, k), F32), grid=(nt, NDEV // 2), in_specs=in_specs,
               out_specs=pl.BlockSpec((tm, k), lambda i, j: (i + t0, 0)), sem=("parallel", "arbitrary"),
               comm=comm, aliases={1 + nw: 0} if prev is not None else None)(*args)


def _mm_dw_in(ht, dproj, nc, sections, name, comm=None):
    k, m = ht.shape
    tk = 2048 if m % 2048 == 0 else _din_tile(m)
    per = NDEV // sections if sections > 1 else NDEV

    def body(h_ref, d_ref, o_ref):
        kk = pl.program_id(1)
        acc = _dot(h_ref[...], d_ref[...])

        @pl.when(kk == 0)
        def _():
            o_ref[...] = acc

        @pl.when(kk > 0)
        def _():
            o_ref[...] += acc

    if sections > 1:
        dspec = pl.BlockSpec((None, tk, nc), lambda j, i: (j // per, i, j % per))
    else:
        dspec = pl.BlockSpec((tk, nc), lambda j, i: (i, j))
    return _pc(body, name=name, out_shape=jax.ShapeDtypeStruct((NDEV, k, nc), F32), grid=(NDEV, m // tk),
               in_specs=[pl.BlockSpec((k, tk), lambda j, i: (0, i)), dspec],
               out_specs=pl.BlockSpec((None, k, nc), lambda j, i: (j, 0, 0)),
               sem=("parallel", "arbitrary"), comm=comm)(ht, dproj)


def _out_proj(ybr, w_out, x, mod, t_seq, name, comm=None):
    m, di = ybr.shape
    d = w_out.shape[1]
    tm = _tile(t_seq, 512)
    per = t_seq // tm

    def body(y_ref, w_ref, x_ref, mod_ref, yo_ref, xn_ref):
        yo = _dot(y_ref[...], w_ref[...])
        yo_ref[...] = yo
        xn_ref[...] = x_ref[...] + mod_ref[0, 2:3, :] * yo

    row = pl.BlockSpec((tm, d), lambda i: (i, 0))
    return _pc(body, name=name,
               out_shape=[jax.ShapeDtypeStruct((m, d), F32), jax.ShapeDtypeStruct((m, d), F32)],
               grid=(m // tm,),
               in_specs=[pl.BlockSpec((tm, di), lambda i: (i, 0)), pl.BlockSpec((di, d), lambda i: (0, 0)), row,
                         pl.BlockSpec((1, 3, d), lambda i: (i // per, 0, 0))],
               out_specs=[row, row], sem=("parallel",), comm=comm)(ybr, w_out, x, mod)


def _out_proj_loss(ybr, w_out, x, mod, gain, target, t_seq):
    m, di = ybr.shape
    d = w_out.shape[1]
    tm = _tile(t_seq, 512)
    per = t_seq // tm

    def body(y_ref, w_ref, x_ref, mod_ref, g_ref, t_ref, yo_ref, dx_ref, loss_ref, dg_ref):
        i = pl.program_id(0)
        yo = _dot(y_ref[...], w_ref[...])
        yo_ref[...] = yo
        xv = x_ref[...] + mod_ref[0, 2:3, :] * yo
        g = g_ref[...]
        rstd = lax.rsqrt(jnp.mean(xv * xv, axis=-1, keepdims=True) + EPS)
        xhat = xv * rstd
        err = xhat * g - t_ref[...]
        dy = err * (1.0 / d)
        dxhat = dy * g
        dx_ref[...] = rstd * (dxhat - xhat * jnp.mean(dxhat * xhat, axis=-1, keepdims=True))

        @pl.when(i == 0)
        def _():
            loss_ref[...] = jnp.zeros_like(loss_ref)
            dg_ref[...] = jnp.zeros_like(dg_ref)

        loss_ref[...] += 0.5 * jnp.sum(jnp.mean(err * err, axis=-1, keepdims=True), axis=0, keepdims=True)
        dg_ref[...] += jnp.sum(dy * xhat, axis=0, keepdims=True)

    row = pl.BlockSpec((tm, d), lambda i: (i, 0))
    vec = pl.BlockSpec((1, d), lambda i: (0, 0))
    return _pc(body, name="out_proj_loss",
               out_shape=[jax.ShapeDtypeStruct((m, d), F32), jax.ShapeDtypeStruct((m, d), F32),
                          jax.ShapeDtypeStruct((1, 1), F32), jax.ShapeDtypeStruct((1, d), F32)],
               grid=(m // tm,),
               in_specs=[pl.BlockSpec((tm, di), lambda i: (i, 0)), pl.BlockSpec((di, d), lambda i: (0, 0)), row,
                         pl.BlockSpec((1, 3, d), lambda i: (i // per, 0, 0)), vec, row],
               out_specs=[row, row, pl.BlockSpec((1, 1), lambda i: (0, 0)), vec],
               sem=("arbitrary",))(ybr, w_out, x, mod, gain, target)


def _gate_dybr(dxn, yout, mod, w_out, t_seq, name):
    m, d = dxn.shape
    di = w_out.shape[0]
    nb = m // t_seq
    tm = _tile(t_seq, 512)
    per = t_seq // tm

    def body(dxn_ref, yo_ref, mod_ref, w_ref, dy_ref, dgate_ref, o_ref):
        i = pl.program_id(0)
        dv = dxn_ref[...]
        dy = (mod_ref[0, 2:3, :] * dv).astype(BF16)
        dy_ref[...] = dy
        o_ref[...] = _dot_nt(dy, w_ref[...])

        @pl.when(i % per == 0)
        def _():
            dgate_ref[...] = jnp.zeros_like(dgate_ref)

        dgate_ref[0] += jnp.sum(dv * yo_ref[...], axis=0, keepdims=True)

    row = pl.BlockSpec((tm, d), lambda i: (i, 0))
    return _pc(body, name=name,
               out_shape=[jax.ShapeDtypeStruct((m, d), BF16), jax.ShapeDtypeStruct((nb, 1, d), F32),
                          jax.ShapeDtypeStruct((m, di), F32)],
               grid=(m // tm,),
               in_specs=[row, row, pl.BlockSpec((1, 3, d), lambda i: (i // per, 0, 0)),
                         pl.BlockSpec((di, d), lambda i: (0, 0))],
               out_specs=[row, pl.BlockSpec((1, 1, d), lambda i: (i // per, 0, 0)),
                          pl.BlockSpec((tm, di), lambda i: (i, 0))],
               sem=("arbitrary",))(dxn, yout, mod, w_out)


def _mm_dw_out(ybr, dy, name, comm=None):
    m, di = ybr.shape
    d = dy.shape[1]
    tk = _tile(m, 512)
    tn = _tile(di, 1024)

    def body(y_ref, dy_ref, o_ref):
        kk = pl.program_id(1)
        acc = _dot_tn(y_ref[...], dy_ref[...])

        @pl.when(kk == 0)
        def _():
            o_ref[...] = acc

        @pl.when(kk > 0)
        def _():
            o_ref[...] += acc

    return _pc(body, name=name, out_shape=jax.ShapeDtypeStruct((di, d), F32), grid=(di // tn, m // tk),
               in_specs=[pl.BlockSpec((tk, tn), lambda n, k: (k, n)), pl.BlockSpec((tk, d), lambda n, k: (k, 0))],
               out_specs=pl.BlockSpec((tn, d), lambda n, k: (n, 0)), sem=("parallel", "arbitrary"),
               comm=comm)(ybr, dy)


def _sgu_mask():
    t = lax.broadcasted_iota(jnp.int32, (SG_BLOCK, SG_BLOCK), 0)
    s = lax.broadcasted_iota(jnp.int32, (SG_BLOCK, SG_BLOCK), 1)
    return (s // CHUNK) <= (t // CHUNK)


def _a_mid_fwd(proj, ln_g, ln_b, w_s, bs_t, t_seq, comm=None):
    m, n3 = proj.shape
    di = n3 // 3
    gd = di // SG_GROUPS
    r = _tile(t_seq, 256)
    nblk = r // SG_BLOCK

    def body(p_ref, lg_ref, lb_ref, ws_ref, bs_ref, ybr_ref, s_scr):
        v = _gelu(p_ref[:, di:2 * di])
        mu = jnp.mean(v, axis=-1, keepdims=True)
        vc = v - mu
        rstd = lax.rsqrt(jnp.mean(vc * vc, axis=-1, keepdims=True) + EPS)
        vb = (vc * rstd * lg_ref[...] + lb_ref[...]).astype(BF16)
        mask = _sgu_mask()
        for gi in range(SG_GROUPS):
            ws = jnp.where(mask, ws_ref[gi], 0.0).astype(BF16)
            bcol = bs_ref[:, gi:gi + 1]
            for b in range(nblk):
                rows = slice(b * SG_BLOCK, (b + 1) * SG_BLOCK)
                cols = slice(gi * gd, (gi + 1) * gd)
                s_scr[rows, cols] = _dot(ws, vb[rows, cols]) + bcol
        gg = p_ref[:, 2 * di:]
        ybr_ref[...] = (_gelu(p_ref[:, :di]) * s_scr[...] * (gg * _sigmoid(gg))).astype(BF16)

    vec = pl.BlockSpec((1, di), lambda i: (0, 0))
    return _pc(body, name="a_mid_fwd", out_shape=jax.ShapeDtypeStruct((m, di), BF16), grid=(m // r,),
               in_specs=[pl.BlockSpec((r, n3), lambda i: (i, 0)), vec, vec,
                         pl.BlockSpec((SG_GROUPS, SG_BLOCK, SG_BLOCK), lambda i: (0, 0, 0)),
                         pl.BlockSpec((SG_BLOCK, 128), lambda i: (0, 0))],
               out_specs=pl.BlockSpec((r, di), lambda i: (i, 0)),
               scratch=[pltpu.VMEM((r, di), F32)], sem=("parallel",), comm=comm)(proj, ln_g, ln_b, w_s, bs_t)


def _a_mid_bwd(proj, dybr, ln_g, ln_b, w_s, bs_t, t_seq, comm=None):
    m, n3 = proj.shape
    di = n3 // 3
    gd = di // SG_GROUPS
    r = _tile(t_seq, 256)
    nblk = r // SG_BLOCK

    def body(p_ref, dy_ref, lg_ref, lb_ref, ws_ref, bs_ref,
             dp_ref, dlg_ref, dlb_ref, dws_ref, dbs_ref, s_scr, dvl_scr):
        i = pl.program_id(0)

        @pl.when(i == 0)
        def _():
            dlg_ref[...] = jnp.zeros_like(dlg_ref)
            dlb_ref[...] = jnp.zeros_like(dlb_ref)
            dws_ref[...] = jnp.zeros_like(dws_ref)
            dbs_ref[...] = jnp.zeros_like(dbs_ref)

        v, dgelu_v = _gelu_and_grad(p_ref[:, di:2 * di])
        mu = jnp.mean(v, axis=-1, keepdims=True)
        vc = v - mu
        rstd = lax.rsqrt(jnp.mean(vc * vc, axis=-1, keepdims=True) + EPS)
        vhat = vc * rstd
        lg = lg_ref[...]
        vb = (vhat * lg + lb_ref[...]).astype(BF16)
        u, dgelu_u = _gelu_and_grad(p_ref[:, :di])
        gg = p_ref[:, 2 * di:]
        sg = _sigmoid(gg)
        dyv = dy_ref[...]
        dus = dyv * (gg * sg)
        dsb = (dus * u).astype(BF16)
        ds32 = dus * u
        mask = _sgu_mask()
        lane = lax.broadcasted_iota(jnp.int32, (SG_BLOCK, 128), 1)
        dbs_acc = jnp.zeros((SG_BLOCK, 128), F32)
        for gi in range(SG_GROUPS):
            ws = jnp.where(mask, ws_ref[gi], 0.0).astype(BF16)
            bcol = bs_ref[:, gi:gi + 1]
            cols = slice(gi * gd, (gi + 1) * gd)
            dws_acc = jnp.zeros((SG_BLOCK, SG_BLOCK), F32)
            dbs_col = jnp.zeros((SG_BLOCK, 1), F32)
            for b in range(nblk):
                rows = slice(b * SG_BLOCK, (b + 1) * SG_BLOCK)
                s_scr[rows, cols] = _dot(ws, vb[rows, cols]) + bcol
                dvl_scr[rows, cols] = _dot_tn(ws, dsb[rows, cols])
                dws_acc += _dot_nt(dsb[rows, cols], vb[rows, cols])
                dbs_col += jnp.sum(ds32[rows, cols], axis=-1, keepdims=True)
            dws_ref[gi] += jnp.where(mask, dws_acc, 0.0)
            dbs_acc += jnp.where(lane == gi, dbs_col, 0.0)
        dbs_ref[...] += dbs_acc
        s = s_scr[...]
        dp_ref[:, :di] = (dus * s * dgelu_u).astype(BF16)
        dp_ref[:, 2 * di:] = (dyv * u * s * (sg * (1.0 + gg * (1.0 - sg)))).astype(BF16)
        dvl = dvl_scr[...]
        dlg_ref[...] += jnp.sum(dvl * vhat, axis=0, keepdims=True)
        dlb_ref[...] += jnp.sum(dvl, axis=0, keepdims=True)
        dvh = dvl * lg
        dv = rstd * (dvh - jnp.mean(dvh, axis=-1, keepdims=True)
                     - vhat * jnp.mean(dvh * vhat, axis=-1, keepdims=True))
        dp_ref[:, di:2 * di] = (dv * dgelu_v).astype(BF16)

    vec = pl.BlockSpec((1, di), lambda i: (0, 0))
    wsb = pl.BlockSpec((SG_GROUPS, SG_BLOCK, SG_BLOCK), lambda i: (0, 0, 0))
    bsb = pl.BlockSpec((SG_BLOCK, 128), lambda i: (0, 0))
    return _pc(body, name="a_mid_bwd",
               out_shape=[jax.ShapeDtypeStruct((m, n3), BF16), jax.ShapeDtypeStruct((1, di), F32),
                          jax.ShapeDtypeStruct((1, di), F32),
                          jax.ShapeDtypeStruct((SG_GROUPS, SG_BLOCK, SG_BLOCK), F32),
                          jax.ShapeDtypeStruct((SG_BLOCK, 128), F32)],
               grid=(m // r,),
               in_specs=[pl.BlockSpec((r, n3), lambda i: (i, 0)), pl.BlockSpec((r, di), lambda i: (i, 0)),
                         vec, vec, wsb, bsb],
               out_specs=[pl.BlockSpec((r, n3), lambda i: (i, 0)), vec, vec, wsb, bsb],
               scratch=[pltpu.VMEM((r, di), F32), pltpu.VMEM((r, di), F32)],
               sem=("arbitrary",), comm=comm)(proj, dybr, ln_g, ln_b, w_s, bs_t)


def _hgrn_dims(t_seq, di):
    tr = _tile(t_seq, 256)
    hc = _tile(di, 1024)
    return tr, hc, hc // HEAD_DIM


def _hgrn_gates(f_ref, lb, a_scr, k_scr, tr):
    sig = _sigmoid(f_ref[...])
    fg = lb + (1.0 - lb) * sig
    k_scr[...] = 1.0 - fg
    logf = jnp.log(fg)
    g = min(CUM_ROWS, tr)
    tri = _tri_mask(g, reverse=False)
    for rg in range(tr // g):
        a_scr[rg * g:(rg + 1) * g, :] = _tri_apply(tri, logf[rg * g:(rg + 1) * g, :])
    return sig, fg


def _hgrn_fwd(proj, lbj, gn, nb, t_seq):
    _, m, di = proj.shape
    tr, hc, hpg = _hgrn_dims(t_seq, di)
    nt, nhg, ncl = t_seq // tr, di // hc, tr // CHUNK
    nheads = di // HEAD_DIM

    def body(q_ref, f_ref, i_ref, g_ref, lb_ref, gn_ref, o_ref, ybr_ref, st_ref, st_scr, a_scr, k_scr):
        t = pl.program_id(2)

        @pl.when(t == 0)
        def _():
            st_scr[...] = jnp.zeros_like(st_scr)

        _hgrn_gates(f_ref, lb_ref[0:1, :], a_scr, k_scr, tr)
        gnv = gn_ref[...]
        rr = lax.broadcasted_iota(jnp.int32, (CHUNK, CHUNK), 0)
        cc = lax.broadcasted_iota(jnp.int32, (CHUNK, CHUNK), 1)
        causal = cc <= rr

        def chunk(n, carry):
            rows = pl.ds(pl.multiple_of(n * CHUNK, CHUNK), CHUNK)
            lanes = [slice(hd * HEAD_DIM, (hd + 1) * HEAD_DIM) for hd in range(hpg)]
            hs = []
            for hd, ls in enumerate(lanes):
                h = {}
                ah, kh = a_scr[rows, ls], k_scr[rows, ls]
                qp = q_ref[rows, ls]
                qh = qp * _sigmoid(qp)
                h["vb"] = i_ref[rows, ls].astype(BF16)
                aref, alast = ah[CHUNK // 2 - 1:CHUNK // 2, :], ah[CHUNK - 1:CHUNK, :]
                h["q_in"] = (qh * jnp.exp(ah - aref)).astype(BF16)
                h["k_in"] = (kh * jnp.exp(aref - ah)).astype(BF16)
                h["q_out"] = (qh * jnp.exp(ah)).astype(BF16)
                h["k_out"] = (kh * jnp.exp(alast - ah)).astype(BF16)
                h["dec"] = jnp.exp(alast)
                st = st_scr[hd]
                st_ref[n, hd] = st
                h["st"] = st
                hs.append(h)
            for h in hs:
                h["scores"] = _dot_nt(h["q_in"], h["k_in"])
                h["o_inter"] = _dot_nt(h["q_out"], h["st"].astype(BF16))
                h["st_mm"] = _dot_tn(h["vb"], h["k_out"])
            for h in hs:
                h["o"] = _dot(jnp.where(causal, h["scores"], 0.0).astype(BF16), h["vb"]) + h["o_inter"]
            for hd, (h, ls) in enumerate(zip(hs, lanes)):
                st_scr[hd] = h["st"] * h["dec"] + h["st_mm"]
                o = h["o"]
                o_ref[rows, ls] = o
                rstd = lax.rsqrt(jnp.mean(o * o, axis=-1, keepdims=True) + EPS)
                gg = g_ref[rows, ls]
                ybr_ref[rows, ls] = ((o * rstd * gnv) * (gg * _sigmoid(gg))).astype(BF16)
            return carry

        lax.fori_loop(0, ncl, chunk, 0)

    def sec(s):
        return pl.BlockSpec((None, tr, hc), lambda hg, b, t: (s, b * nt + t, hg))

    blk = pl.BlockSpec((tr, hc), lambda hg, b, t: (b * nt + t, hg))
    return _pc(body, name="hgrn_fwd",
               out_shape=[jax.ShapeDtypeStruct((m, di), F32), jax.ShapeDtypeStruct((m, di), BF16),
                          jax.ShapeDtypeStruct((m // CHUNK, nheads, HEAD_DIM, HEAD_DIM), F32)],
               grid=(nhg, nb, nt),
               in_specs=[sec(0), sec(1), sec(2), sec(3), pl.BlockSpec((2, hc), lambda hg, b, t: (0, hg)),
                         pl.BlockSpec((1, HEAD_DIM), lambda hg, b, t: (0, 0))],
               out_specs=[blk, blk, pl.BlockSpec((ncl, hpg, HEAD_DIM, HEAD_DIM),
                                                 lambda hg, b, t: (b * nt + t, hg, 0, 0))],
               scratch=[pltpu.VMEM((hpg, HEAD_DIM, HEAD_DIM), F32), pltpu.VMEM((tr, hc), F32),
                        pltpu.VMEM((tr, hc), F32)],
               sem=("parallel", "arbitrary", "arbitrary"))(proj, proj, proj, proj, lbj, gn)


def _hgrn_bwd(proj, o_all, dybr, states, lbj, gn, nb, t_seq, comm=None):
    _, m, di = proj.shape
    tr, hc, hpg = _hgrn_dims(t_seq, di)
    nt, nhg, ncl = t_seq // tr, di // hc, tr // CHUNK

    def body(q_ref, f_ref, i_ref, g_ref, o_ref, dy_ref, st_ref, lb_ref, gn_ref,
             dp_ref, dlb_ref, dgn_ref, dst_scr, a_scr, k_scr, da_scr, dk_scr):
        hg, b, t = pl.program_id(0), pl.program_id(1), pl.program_id(2)

        @pl.when(t == 0)
        def _():
            dst_scr[...] = jnp.zeros_like(dst_scr)

        @pl.when((b == 0) & (t == 0))
        def _():
            dlb_ref[...] = jnp.zeros_like(dlb_ref)

        @pl.when((hg == 0) & (b == 0) & (t == 0))
        def _():
            dgn_ref[...] = jnp.zeros_like(dgn_ref)

        lb = lb_ref[0:1, :]
        sig, fg = _hgrn_gates(f_ref, lb, a_scr, k_scr, tr)
        gnv = gn_ref[...]
        rr = lax.broadcasted_iota(jnp.int32, (CHUNK, CHUNK), 0)
        cc = lax.broadcasted_iota(jnp.int32, (CHUNK, CHUNK), 1)
        causal = cc <= rr
        rowi = lax.broadcasted_iota(jnp.int32, (CHUNK, HEAD_DIM), 0)

        def chunk(it, carry):
            n = ncl - 1 - it
            rows = pl.ds(pl.multiple_of(n * CHUNK, CHUNK), CHUNK)
            lanes = [slice(hd * HEAD_DIM, (hd + 1) * HEAD_DIM) for hd in range(hpg)]
            hs = []
            for hd, ls in enumerate(lanes):
                h = {}
                ah, kh = a_scr[rows, ls], k_scr[rows, ls]
                qp = q_ref[rows, ls]
                sq = _sigmoid(qp)
                qh = qp * sq
                h["dsilu_q"] = sq * (1.0 + qp * (1.0 - sq))
                h["vb"] = i_ref[rows, ls].astype(BF16)
                aref, alast = ah[CHUNK // 2 - 1:CHUNK // 2, :], ah[CHUNK - 1:CHUNK, :]
                h["e1"], h["e2"] = jnp.exp(ah - aref), jnp.exp(aref - ah)
                h["e3"], h["e4"] = jnp.exp(ah), jnp.exp(alast - ah)
                h["dec"] = jnp.exp(alast)
                h["q_in"], h["k_in"], h["q_out"], h["k_out"] = qh * h["e1"], kh * h["e2"], qh * h["e3"], kh * h["e4"]
                for nm in ("q_in", "k_in", "q_out", "k_out"):
                    h[nm + "_b"] = h[nm].astype(BF16)
                o = o_ref[rows, ls]
                rstd = lax.rsqrt(jnp.mean(o * o, axis=-1, keepdims=True) + EPS)
                ohat = o * rstd
                gg = g_ref[rows, ls]
                sg = _sigmoid(gg)
                dyv = dy_ref[rows, ls]
                d_on = dyv * (gg * sg)
                dp_ref[3, rows, ls] = (dyv * (ohat * gnv) * (sg * (1.0 + gg * (1.0 - sg)))).astype(BF16)
                h["dgn"] = jnp.sum(d_on * ohat, axis=0, keepdims=True)
                dohat = d_on * gnv
                do = rstd * (dohat - ohat * jnp.mean(dohat * ohat, axis=-1, keepdims=True))
                h["do_b"] = do.astype(BF16)
                h["st_prev"] = st_ref[n, hd]
                h["dst"] = dst_scr[hd]
                hs.append(h)
            for h in hs:
                dst_b = h["dst"].astype(BF16)
                h["scores"] = _dot_nt(h["q_in_b"], h["k_in_b"])
                h["dscores"] = _dot_nt(h["do_b"], h["vb"])
                h["dv_inter"] = _dot_nt(h["k_out_b"], dst_b)
                h["dq_out"] = _dot(h["do_b"], h["st_prev"].astype(BF16))
                h["dk_out"] = _dot(h["vb"], dst_b)
                h["dst_mm"] = _dot_tn(h["do_b"], h["q_out_b"])
            for h in hs:
                scores = jnp.where(causal, h["scores"], 0.0).astype(BF16)
                dscores = jnp.where(causal, h["dscores"], 0.0).astype(BF16)
                h["dv"] = _dot_tn(scores, h["do_b"]) + h["dv_inter"]
                h["dq_in"] = _dot(dscores, h["k_in_b"])
                h["dk_in"] = _dot_tn(dscores, h["q_in_b"])
            dgn = hs[0]["dgn"]
            for h in hs[1:]:
                dgn = dgn + h["dgn"]
            dgn_ref[...] += dgn
            for hd, (h, ls) in enumerate(zip(hs, lanes)):
                ddec = jnp.sum(h["dst"] * h["st_prev"], axis=0, keepdims=True)
                dst_scr[hd] = h["dst"] * h["dec"] + h["dst_mm"]
                dp_ref[2, rows, ls] = h["dv"].astype(BF16)
                dq = h["dq_in"] * h["e1"] + h["dq_out"] * h["e3"]
                dp_ref[0, rows, ls] = (dq * h["dsilu_q"]).astype(BF16)
                dk_scr[rows, ls] = h["dk_in"] * h["e2"] + h["dk_out"] * h["e4"]
                t_in = h["dq_in"] * h["q_in"] - h["dk_in"] * h["k_in"]
                t_out = h["dk_out"] * h["k_out"]
                da = t_in + h["dq_out"] * h["q_out"] - t_out
                da_ref_row = -jnp.sum(t_in, axis=0, keepdims=True)
                da_last_row = jnp.sum(t_out, axis=0, keepdims=True) + ddec * h["dec"]
                da = da + jnp.where(rowi == CHUNK // 2 - 1, da_ref_row, 0.0) \
                        + jnp.where(rowi == CHUNK - 1, da_last_row, 0.0)
                da_scr[rows, ls] = da
            return carry

        lax.fori_loop(0, ncl, chunk, 0)
        g = min(CUM_ROWS, tr)
        tri = _tri_mask(g, reverse=True)
        for rg in range(tr // g):
            rs = slice(rg * g, (rg + 1) * g)
            dlogf = _tri_apply(tri, da_scr[rs, :])
            df = dlogf / fg[rs, :] - dk_scr[rs, :]
            sgr = sig[rs, :]
            dp_ref[1, rs, :] = (df * (1.0 - lb) * (sgr * (1.0 - sgr))).astype(BF16)
            dlb_ref[...] += jnp.sum(df * (1.0 - sgr), axis=0, keepdims=True) * lb_ref[1:2, :]

    def sec(s):
        return pl.BlockSpec((None, tr, hc), lambda hg, b, t: (s, b * nt + (nt - 1 - t), hg))

    blk = pl.BlockSpec((tr, hc), lambda hg, b, t: (b * nt + (nt - 1 - t), hg))
    return _pc(body, name="hgrn_bwd",
               out_shape=[jax.ShapeDtypeStruct((4, m, di), BF16), jax.ShapeDtypeStruct((1, di), F32),
                          jax.ShapeDtypeStruct((1, HEAD_DIM), F32)],
               grid=(nhg, nb, nt),
               in_specs=[sec(0), sec(1), sec(2), sec(3), blk, blk,
                         pl.BlockSpec((ncl, hpg, HEAD_DIM, HEAD_DIM),
                                      lambda hg, b, t: (b * nt + (nt - 1 - t), hg, 0, 0)),
                         pl.BlockSpec((2, hc), lambda hg, b, t: (0, hg)),
                         pl.BlockSpec((1, HEAD_DIM), lambda hg, b, t: (0, 0))],
               out_specs=[pl.BlockSpec((4, tr, hc), lambda hg, b, t: (0, b * nt + (nt - 1 - t), hg)),
                          pl.BlockSpec((1, hc), lambda hg, b, t: (0, hg)),
                          pl.BlockSpec((1, HEAD_DIM), lambda hg, b, t: (0, 0))],
               scratch=[pltpu.VMEM((hpg, HEAD_DIM, HEAD_DIM), F32)] + [pltpu.VMEM((tr, hc), F32)] * 4,
               sem=("arbitrary", "arbitrary", "arbitrary"), comm=comm)(
                   proj, proj, proj, proj, o_all, dybr, states, lbj, gn)


def _adamw(parts, w, m, v, name, comm=None):
    r, c = w.shape
    tr = _tile(r, 256)
    npart = len(parts)
    c1 = 1.0 - ADAM_B1 ** ADAM_STEP
    c2 = 1.0 - ADAM_B2 ** ADAM_STEP

    def body(*refs):
        p_refs = refs[:npart]
        _adamw_math(p_refs, *refs[npart:], c1, c2)

    blk = pl.BlockSpec((tr, c), lambda i: (i, 0))
    return _pc(body, name=name, out_shape=[jax.ShapeDtypeStruct((r, c), F32)] * 4, grid=(r // tr,),
               in_specs=[blk] * (npart + 3), out_specs=[blk] * 4, sem=("parallel",), comm=comm)(*parts, w, m, v)


def _adamw_math(p_refs, w_ref, m_ref, v_ref, g_ref, d_ref, nm_ref, nv_ref, c1, c2):
    g = p_refs[0][...].astype(F32)
    for p in p_refs[1:]:
        g = g + p[...].astype(F32)
    nm = ADAM_B1 * m_ref[...] + (1.0 - ADAM_B1) * g
    nv = ADAM_B2 * v_ref[...] + (1.0 - ADAM_B2) * (g * g)
    g_ref[...] = g
    nm_ref[...] = nm
    nv_ref[...] = nv
    d_ref[...] = -ADAM_LR * ((nm / c1) / (jnp.sqrt(nv / c2) + ADAM_EPS) + ADAM_WD * w_ref[...])


def _adamw_blocks(parts, idx, w, m, v, name):
    r, c = w.shape
    tr = _tile(r, 256)
    npart = len(parts)
    c1 = 1.0 - ADAM_B1 ** ADAM_STEP
    c2 = 1.0 - ADAM_B2 ** ADAM_STEP

    def body(idx_ref, *refs):
        _adamw_math(refs[:npart], *refs[npart:], c1, c2)

    def sel(p):
        return pl.BlockSpec((None, tr, c), lambda i, s: (s[p], i, 0))

    blk = pl.BlockSpec((tr, c), lambda i, s: (i, 0))
    gs = pltpu.PrefetchScalarGridSpec(num_scalar_prefetch=1, grid=(r // tr,),
                                      in_specs=[sel(p) for p in range(npart)] + [blk] * 3, out_specs=[blk] * 4)
    return _pc(body, name=name, out_shape=[jax.ShapeDtypeStruct((r, c), F32)] * 4, grid_spec=gs,
               sem=("parallel",))(idx, *parts, w, m, v)


_EARLY = ["a_ln_gain", "a_ln_bias", "a_w_s", "a_b_s", "b_lower_bounds", "b_gn_gain"]


def _pack(arrs):
    flat = jnp.concatenate([a.reshape(-1) for a in arrs])
    rows = -(-flat.shape[0] // 1024) * 8
    return jnp.pad(flat, (0, rows * 128 - flat.shape[0])).reshape(rows, 128)


def _unpack(buf, like):
    flat = buf.reshape(-1)
    out, off = [], 0
    for a in like:
        out.append(flat[off:off + a.size].reshape(a.shape))
        off += a.size
    return out


def kernel(x, c, norm_gain, w_ada, b_ada, a_w_in, a_ln_gain, a_ln_bias, a_w_s, a_b_s, a_w_out, b_w_in, b_lower_bounds, b_gn_gain, b_w_out, final_gain, loss_target, m_norm_gain, m_w_ada, m_b_ada, m_a_w_in, m_a_ln_gain, m_a_ln_bias, m_a_w_s, m_a_b_s, m_a_w_out, m_b_w_in, m_b_lower_bounds, m_b_gn_gain, m_b_w_out, m_final_gain, v_norm_gain, v_w_ada, v_b_ada, v_a_w_in, v_a_ln_gain, v_a_ln_bias, v_a_w_s, v_a_b_s, v_a_w_out, v_b_w_in, v_b_lower_bounds, v_b_gn_gain, v_b_w_out, v_final_gain):
    w = dict(norm_gain=norm_gain, w_ada=w_ada, b_ada=b_ada, a_w_in=a_w_in, a_ln_gain=a_ln_gain,
             a_ln_bias=a_ln_bias, a_w_s=a_w_s, a_b_s=a_b_s, a_w_out=a_w_out, b_w_in=b_w_in,
             b_lower_bounds=b_lower_bounds, b_gn_gain=b_gn_gain, b_w_out=b_w_out, final_gain=final_gain)
    mo = dict(norm_gain=m_norm_gain, w_ada=m_w_ada, b_ada=m_b_ada, a_w_in=m_a_w_in, a_ln_gain=m_a_ln_gain,
              a_ln_bias=m_a_ln_bias, a_w_s=m_a_w_s, a_b_s=m_a_b_s, a_w_out=m_a_w_out, b_w_in=m_b_w_in,
              b_lower_bounds=m_b_lower_bounds, b_gn_gain=m_b_gn_gain, b_w_out=m_b_w_out, final_gain=m_final_gain)
    vo = dict(norm_gain=v_norm_gain, w_ada=v_w_ada, b_ada=v_b_ada, a_w_in=v_a_w_in, a_ln_gain=v_a_ln_gain,
              a_ln_bias=v_a_ln_bias, a_w_s=v_a_w_s, a_b_s=v_a_b_s, a_w_out=v_a_w_out, b_w_in=v_b_w_in,
              b_lower_bounds=v_b_lower_bounds, b_gn_gain=v_b_gn_gain, b_w_out=v_b_w_out, final_gain=v_final_gain)

    nb, t_seq, d = x.shape
    m = nb * t_seq
    ncol_ada = w_ada.shape[2]
    xi, yi, ci = lax.axis_index("x"), lax.axis_index("y"), lax.axis_index("c")
    me = 4 * xi + 2 * yi + ci

    c_g, wa_in_g = _all_gather([c, a_w_in[0].astype(BF16)], "gather_c_wa")

    c_all = c_g.reshape(NDEV * nb, d)
    b_cols = lax.dynamic_slice(b_ada, (0, me * ncol_ada), (2, ncol_ada)).reshape(2, 1, ncol_ada)
    mod_part, lbj = _ada_fwd(c_all, w_ada, b_cols, b_lower_bounds)
    mod_all = _all_gather([mod_part], "gather_mod")[0]
    mod_mine = lax.dynamic_slice_in_dim(mod_all, me * nb, nb, axis=2)
    mod_mine = mod_mine.transpose(1, 2, 0, 3).reshape(2, nb, 3, d)
    mod0, mod1 = mod_mine[0], mod_mine[1]

    di = a_w_out.shape[1] * NDEV

    xf = x.reshape(m, d)
    tgt = loss_target.reshape(m, d)
    ng0, ng1 = norm_gain[0:1], norm_gain[1:2]
    ncb = b_w_in.shape[2]
    wb_lo, wb_hi = b_w_in[0][:, :ncb // 2].astype(BF16), b_w_in[0][:, ncb // 2:].astype(BF16)
    h0, h0_t = _prenorm(xf, ng0, mod0, t_seq, "prenorm_a")
    proj_a, half = _mm_in(h0, [wa_in_g], 1, "in_proj_a", comm=_gather_first([a_w_out[0].astype(BF16), wb_lo]))
    bs_t = jnp.pad(a_b_s[0].T, ((0, 0), (0, 128 - SG_GROUPS)))
    ybr_a, (wa_out_g, wb_lo_g, wb_hi_half) = _a_mid_fwd(
        proj_a, a_ln_gain, a_ln_bias, a_w_s[0], bs_t, t_seq, comm=_join(_gather_second(half), _gather_first([wb_hi])))
    wa_out = wa_out_g.reshape(di, d)
    (yout_a, x1), (wb_hi_g, wb_out_half) = _out_proj(
        ybr_a, wa_out, xf, mod0, t_seq, "out_proj_a",
        comm=_join(_gather_second([wb_hi_half]), _gather_first([b_w_out[0].astype(BF16)])))
    wb_in_g = [wb_lo_g, wb_hi_g]
    h1, h1_t = _prenorm(x1, ng1, mod1, t_seq, "prenorm_b")
    proj_b, (wb_out_g,) = _mm_in(h1, wb_in_g, 4, "in_proj_b", comm=_gather_second([wb_out_half]))
    wb_out = wb_out_g.reshape(di, d)
    o_b, ybr_b, states = _hgrn_fwd(proj_b, lbj, b_gn_gain, nb, t_seq)
    yout_b, dx2, loss_part, d_final_gain = _out_proj_loss(ybr_b, wb_out, x1, mod1, final_gain.reshape(1, d), tgt, t_seq)

    rows_out = a_w_out.shape[1]
    dy_b, dgate1, dybr_b = _gate_dybr(dx2, yout_b, mod1, wb_out, t_seq, "dybr_b")
    rs_wb_out = _ReduceScatter(_mm_dw_out(ybr_b, dy_b, "dw_out_b").reshape(NDEV, rows_out, d), "b_w_out")
    (dproj_b, d_lb, d_gn), got = _hgrn_bwd(proj_b, o_b, dybr_b, states, lbj, b_gn_gain, nb, t_seq,
                                           comm=rs_wb_out.swap_core())
    rs_wb_out.after_core(got[0])
    dh1, got = _mm_din(dproj_b, wb_in_g, 4, "dh_b", comm=rs_wb_out.swap_chips())
    rs_wb_out.after_chips(got[0])
    dx1, dss1, dgain1 = _prenorm_bwd(dh1, x1, ng1, mod1, dx2, t_seq, "prenorm_bwd_b")
    rs_wb_in = _ReduceScatter(_mm_dw_in(h1_t, dproj_b, ncb, 4, "dw_in_b"), "b_w_in")

    dy_a, dgate0, dybr_a = _gate_dybr(dx1, yout_a, mod0, wa_out, t_seq, "dybr_a")
    g_wa_out, got = _mm_dw_out(ybr_a, dy_a, "dw_out_a", comm=rs_wb_in.swap_core())
    rs_wb_in.after_core(got[0])
    rs_wa_out = _ReduceScatter(g_wa_out.reshape(NDEV, rows_out, d), "a_w_out")
    (dproj_a, d_lng, d_lnb, d_ws, d_bs_t), got = _a_mid_bwd(
        proj_a, dybr_a, a_ln_gain, a_ln_bias, a_w_s[0], bs_t, t_seq,
        comm=_join(rs_wb_in.swap_chips(), rs_wa_out.swap_core()))
    rs_wb_in.after_chips(got[0])
    rs_wa_out.after_core(got[1])
    part = dict(a_ln_gain=d_lng, a_ln_bias=d_lnb, a_w_s=d_ws[None], a_b_s=d_bs_t[:, :SG_GROUPS].T[None],
                b_lower_bounds=jnp.concatenate([-d_lb, d_lb], axis=0), b_gn_gain=d_gn)
    early_pack = _pack([part[k].reshape(w[k].shape) for k in _EARLY])
    g_wa_in, got = _mm_dw_in(h0_t, dproj_a, wa_in_g.shape[2], 1, "dw_in_a",
                             comm=_join(rs_wa_out.swap_chips(), _gather_first([early_pack])))
    rs_wa_out.after_chips(got[0])
    rs_wa_in = _ReduceScatter(g_wa_in, "a_w_in")
    n_tiles = m // _din_tile(m)
    assert n_tiles >= 2
    first_tiles = max(1, (3 * n_tiles) // 8)
    dh0, got2 = _mm_din(dproj_a, [wa_in_g], 1, "dh_a_first", tiles=(0, first_tiles),
                        comm=_join(rs_wa_in.swap_core(), _gather_second([got[1]])))
    rs_wa_in.after_core(got2[0])
    early_all = got2[1]
    dh0, got = _mm_din(dproj_a, [wa_in_g], 1, "dh_a_rest", comm=rs_wa_in.swap_chips(),
                       tiles=(first_tiles, n_tiles - first_tiles), prev=dh0)
    rs_wa_in.after_chips(got[0])
    dx0, dss0, dgain0 = _prenorm_bwd(dh0, xf, ng0, mod0, dx1, t_seq, "prenorm_bwd_a")
    grad_x = dx0.reshape(nb, t_seq, d)

    dmod = jnp.stack([jnp.concatenate([dss0, dgate0], axis=1), jnp.concatenate([dss1, dgate1], axis=1)])
    late_like = [norm_gain, final_gain, loss_part.reshape(1)]
    late_pack = _pack([jnp.concatenate([dgain0, dgain1], axis=0), d_final_gain[0], loss_part.reshape(1)])
    dmod_all, late_all = _all_gather([dmod.reshape(2, nb, 3 * d), late_pack], "gather_tail")
    dmod_all = dmod_all.transpose(1, 0, 2, 3).reshape(2, NDEV * nb, 3 * d)
    dmod_cols = lax.dynamic_slice_in_dim(dmod_all, me * ncol_ada, ncol_ada, axis=2)
    g_w_ada, g_b_ada = _ada_bwd(c_all, dmod_cols, dmod_all)

    res = {}
    early_like = [w[k] for k in _EARLY]
    dev_order = jnp.arange(NDEV, dtype=jnp.int32)
    sm = _adamw_blocks([early_all] * NDEV, dev_order, _pack(early_like), _pack([mo[k] for k in _EARLY]),
                       _pack([vo[k] for k in _EARLY]), "adamw_small_early")
    sm = [dict(zip(_EARLY, _unpack(buf, early_like))) for buf in sm]
    for k in _EARLY:
        res[k] = tuple(s[k] for s in sm)
    zero = jnp.zeros((1,), F32)
    sm = _adamw_blocks([late_all] * NDEV, dev_order, _pack([norm_gain, final_gain, zero]),
                       _pack([mo["norm_gain"], mo["final_gain"], zero]),
                       _pack([vo["norm_gain"], vo["final_gain"], zero]), "adamw_small_late")
    sm = [_unpack(buf, late_like) for buf in sm]
    res["norm_gain"] = tuple(s[0] for s in sm)
    res["final_gain"] = tuple(s[1] for s in sm)
    loss = sm[0][2][0]
    rb = _adamw([g_b_ada], b_ada, mo["b_ada"], vo["b_ada"], "adamw_b_ada")
    res["b_ada"] = tuple(rb)
    sh = w_ada.shape
    ra = _adamw([g_w_ada.reshape(sh[0] * sh[1], sh[2])], w_ada.reshape(sh[0] * sh[1], sh[2]),
                mo["w_ada"].reshape(sh[0] * sh[1], sh[2]), vo["w_ada"].reshape(sh[0] * sh[1], sh[2]), "adamw_w_ada")
    res["w_ada"] = tuple(z.reshape(sh) for z in ra)

    for k, rs in (("b_w_out", rs_wb_out), ("b_w_in", rs_wb_in), ("a_w_out", rs_wa_out), ("a_w_in", rs_wa_in)):
        res[k] = tuple(z[None] for z in _adamw_blocks(rs.parts, rs.idx, w[k][0], mo[k][0], vo[k][0], "adamw_" + k))

    order = ["norm_gain", "w_ada", "b_ada", "a_w_in", "a_ln_gain", "a_ln_bias", "a_w_s", "a_b_s", "a_w_out",
             "b_w_in", "b_lower_bounds", "b_gn_gain", "b_w_out", "final_gain"]
    return (loss, grad_x, *[res[k][0] for k in order], *[res[k][1] for k in order],
            *[res[k][2] for k in order], *[res[k][3] for k in order])
```

```python
import functools
import math

import jax
import jax.numpy as jnp
from jax import lax
from jax.experimental import pallas as pl
from jax.experimental.pallas import tpu as pltpu

F32 = jnp.float32
BF16 = jnp.bfloat16
MESH = pl.DeviceIdType.MESH
NDEV = 8
EPS = 1e-6
CHUNK = 64
SG_BLOCK = 128
SG_GROUPS = 8
HEAD_DIM = 128
CUM_ROWS = 256
ADAM_LR, ADAM_B1, ADAM_B2, ADAM_EPS, ADAM_WD, ADAM_STEP = 0.001, 0.9, 0.999, 1e-08, 0.01, 10
VMEM_LIMIT = 56 * 1024 * 1024
ANY = pl.BlockSpec(memory_space=pl.ANY)


class _Hosted:
    def __init__(self, arrays, out_shapes, nsem, start, finish, aliases=None):
        self.arrays, self.out_shapes, self.nsem = list(arrays), list(out_shapes), nsem
        self.start, self.finish = start, finish
        self.aliases = dict(aliases or {})


def _join(*comms):
    arrays, outs, aliases, offs, nsem = [], [], {}, [], 0
    for cm in comms:
        offs.append((len(arrays), len(outs), nsem))
        for i, o in cm.aliases.items():
            aliases[len(arrays) + i] = len(outs) + o
        arrays += cm.arrays
        outs += cm.out_shapes
        nsem += cm.nsem

    def run(which):
        def f(ins, outs_, ss, rs, base):
            for cm, (ia, io, isem) in zip(comms, offs):
                getattr(cm, which)(ins[ia:ia + len(cm.arrays)], outs_[io:io + len(cm.out_shapes)], ss, rs, base + isem)
        return f

    return _Hosted(arrays, outs, nsem, run("start"), run("finish"), aliases)


def _pc(body, *, name, out_shape, grid=None, in_specs=None, out_specs=None, scratch=(), sem=None,
        grid_spec=None, comm=None, aliases=None):
    cp = dict(vmem_limit_bytes=VMEM_LIMIT)
    aliases = dict(aliases or {})
    if comm is None:
        if sem is not None:
            cp["dimension_semantics"] = sem
        kw = {"input_output_aliases": aliases}
        if grid_spec is not None:
            kw["grid_spec"] = grid_spec
        else:
            if grid is not None:
                kw["grid"] = grid
            if in_specs is not None:
                kw["in_specs"] = in_specs
            if out_specs is not None:
                kw["out_specs"] = out_specs
            kw["scratch_shapes"] = list(scratch)
        return pl.pallas_call(functools.partial(body), name=name, out_shape=out_shape,
                              compiler_params=pltpu.CompilerParams(**cp), **kw)

    single = not isinstance(out_shape, (list, tuple))
    outs_list = [out_shape] if single else list(out_shape)
    ospecs = [out_specs] if single else list(out_specs)
    n_in, n_out, n_ci, n_co, n_scr = len(in_specs), len(outs_list), len(comm.arrays), len(comm.out_shapes), len(scratch)
    cp["dimension_semantics"] = ("arbitrary",) * len(grid)

    def hosted(*refs):
        cin, hin = refs[:n_in], refs[n_in:n_in + n_ci]
        cout = refs[n_in + n_ci:n_in + n_ci + n_out]
        hout = refs[n_in + n_ci + n_out:n_in + n_ci + n_out + n_co]
        scr = refs[n_in + n_ci + n_out + n_co:n_in + n_ci + n_out + n_co + n_scr]
        ssem, rsem = refs[-2], refs[-1]
        first = functools.reduce(lambda p, q: p & q, [pl.program_id(a) == 0 for a in range(len(grid))])
        last = functools.reduce(lambda p, q: p & q, [pl.program_id(a) == grid[a] - 1 for a in range(len(grid))])

        @pl.when(first)
        def _():
            comm.start(hin, hout, ssem, rsem, 0)

        body(*cin, *cout, *scr)

        @pl.when(last)
        def _():
            comm.finish(hin, hout, ssem, rsem, 0)

    call = pl.pallas_call(
        hosted, name=name, grid=grid, in_specs=list(in_specs) + [ANY] * n_ci, out_specs=ospecs + [ANY] * n_co,
        out_shape=outs_list + comm.out_shapes,
        scratch_shapes=list(scratch) + [pltpu.SemaphoreType.DMA((comm.nsem,)), pltpu.SemaphoreType.DMA((comm.nsem,))],
        input_output_aliases={**aliases, **{n_in + i: n_out + o for i, o in comm.aliases.items()}},
        compiler_params=pltpu.CompilerParams(**cp))

    def run(*args):
        res = call(*args, *comm.arrays)
        comp = res[:n_out]
        return (comp[0] if single else comp), list(res[n_out:])

    return run


def _tile(n, pref):
    return pref if n % pref == 0 else n


def _sigmoid(x):
    return 1.0 / (1.0 + jnp.exp(-x))


def _gelu(x):
    c = math.sqrt(2.0 / math.pi)
    return 0.5 * x * (1.0 + jnp.tanh(c * (x + 0.044715 * (x * x * x))))


def _gelu_and_grad(x):
    c = math.sqrt(2.0 / math.pi)
    x2 = x * x
    t = jnp.tanh(c * (x + 0.044715 * (x2 * x)))
    half = 0.5 * (1.0 + t)
    return x * half, half + (0.5 * x) * (1.0 - t * t) * (c + (3.0 * 0.044715 * c) * x2)


def _dot(a, b):
    return jnp.dot(a, b, preferred_element_type=F32)


def _dot_nt(a, b):
    return lax.dot_general(a, b, (((1,), (1,)), ((), ())), preferred_element_type=F32)


def _dot_tn(a, b):
    return lax.dot_general(a, b, (((0,), (0,)), ((), ())), preferred_element_type=F32)


def _tri_mask(n, reverse):
    r = lax.broadcasted_iota(jnp.int32, (n, n), 0)
    c = lax.broadcasted_iota(jnp.int32, (n, n), 1)
    same = (r // CHUNK) == (c // CHUNK)
    tri = (c >= r) if reverse else (c <= r)
    return jnp.where(same & tri, 1.0, 0.0).astype(BF16)


def _tri_apply(tri, x):
    hi = x.astype(BF16)
    r1 = x - hi.astype(F32)
    mid = r1.astype(BF16)
    lo = (r1 - mid.astype(F32)).astype(BF16)
    return _dot(tri, hi) + (_dot(tri, mid) + _dot(tri, lo))


def _all_gather(arrs, name):
    n = len(arrs)

    def body(*refs):
        ins, outs = refs[:n], refs[n:2 * n]
        send_sems, recv_sems, local_sems = refs[2 * n:]
        x, y, c = lax.axis_index("x"), lax.axis_index("y"), lax.axis_index("c")
        me, sibling = (x, y, c), (x, y, 1 - c)
        chips = [(1 - x, y), (x, 1 - y), (1 - x, 1 - y)]

        def blk(a, p):
            return outs[a].at[4 * p[0] + 2 * p[1] + p[2]]

        def copy(a, k, block, to, src=None):
            return pltpu.make_async_remote_copy(
                src_ref=blk(a, block) if src is None else src, dst_ref=blk(a, block),
                send_sem=send_sems.at[7 * a + k], recv_sem=recv_sems.at[7 * a + k],
                device_id=to, device_id_type=MESH)

        mine = [pltpu.make_async_copy(ins[a], blk(a, me), local_sems.at[a]) for a in range(n)]
        for m in mine:
            m.start()
        first = []
        for a in range(n):
            first.append(copy(a, 0, me, sibling, src=ins[a]))
            for j, chip in enumerate(chips):
                first.append(copy(a, 1 + j, me, (*chip, c), src=ins[a]))
        for cp in first:
            cp.start()
        passed = []
        for j, chip in enumerate(chips):
            for a in range(n):
                copy(a, 1 + j, (*chip, c), me).wait_recv()
                p = copy(a, 4 + j, (*chip, c), sibling)
                p.start()
                passed.append(p)
        for a in range(n):
            copy(a, 0, sibling, me).wait_recv()
            for j, chip in enumerate(chips):
                copy(a, 4 + j, (*chip, 1 - c), me).wait_recv()
        for cp in first + passed:
            cp.wait_send()
        for m in mine:
            m.wait()

    out_shape = [jax.ShapeDtypeStruct((NDEV,) + a.shape, a.dtype) for a in arrs]
    return _pc(body, name=name, out_shape=out_shape, in_specs=[ANY] * n, out_specs=[ANY] * n,
               scratch=[pltpu.SemaphoreType.DMA((7 * n,)), pltpu.SemaphoreType.DMA((7 * n,)),
                        pltpu.SemaphoreType.DMA((n,))])(*arrs)


def _gather_first(arrs):
    n = len(arrs)

    def parts(ins, outs, ss, rs, base):
        x, y, c = lax.axis_index("x"), lax.axis_index("y"), lax.axis_index("c")
        me, sibling = (x, y, c), (x, y, 1 - c)
        chips = [(1 - x, y), (x, 1 - y), (1 - x, 1 - y)]

        def blk(a, p):
            return outs[a].at[4 * p[0] + 2 * p[1] + p[2]]

        def copy(a, k, block, to):
            return pltpu.make_async_remote_copy(
                src_ref=ins[a], dst_ref=blk(a, block), send_sem=ss.at[base + 4 * a + k],
                recv_sem=rs.at[base + 4 * a + k], device_id=to, device_id_type=MESH)

        local = [pltpu.make_async_copy(ins[a], blk(a, me), ss.at[base + 4 * n + a]) for a in range(n)]
        sends, recvs = [], []
        for a in range(n):
            sends.append(copy(a, 0, me, sibling))
            recvs.append(copy(a, 0, sibling, me))
            for j, chip in enumerate(chips):
                sends.append(copy(a, 1 + j, me, (*chip, c)))
                recvs.append(copy(a, 1 + j, (*chip, c), me))
        return local, sends, recvs

    def start(ins, outs, ss, rs, base):
        local, sends, _ = parts(ins, outs, ss, rs, base)
        for cp in local + sends:
            cp.start()

    def finish(ins, outs, ss, rs, base):
        local, sends, recvs = parts(ins, outs, ss, rs, base)
        for cp in recvs:
            cp.wait_recv()
        for cp in sends:
            cp.wait_send()
        for cp in local:
            cp.wait()

    return _Hosted(arrs, [jax.ShapeDtypeStruct((NDEV,) + a.shape, a.dtype) for a in arrs], 5 * n, start, finish)


def _gather_second(bufs):
    n = len(bufs)

    def parts(ins, outs, ss, rs, base):
        x, y, c = lax.axis_index("x"), lax.axis_index("y"), lax.axis_index("c")
        sibling = (x, y, 1 - c)
        chips = [(1 - x, y), (x, 1 - y), (1 - x, 1 - y)]
        sends, recvs = [], []
        for a in range(n):
            for j, chip in enumerate(chips):
                mine = 4 * chip[0] + 2 * chip[1] + c
                theirs = 4 * chip[0] + 2 * chip[1] + (1 - c)
                sends.append(pltpu.make_async_remote_copy(
                    src_ref=ins[a].at[mine], dst_ref=outs[a].at[mine], send_sem=ss.at[base + 3 * a + j],
                    recv_sem=rs.at[base + 3 * a + j], device_id=sibling, device_id_type=MESH))
                recvs.append(pltpu.make_async_remote_copy(
                    src_ref=ins[a].at[theirs], dst_ref=outs[a].at[theirs], send_sem=ss.at[base + 3 * a + j],
                    recv_sem=rs.at[base + 3 * a + j], device_id=sibling, device_id_type=MESH))
        return sends, recvs

    def start(ins, outs, ss, rs, base):
        for cp in parts(ins, outs, ss, rs, base)[0]:
            cp.start()

    def finish(ins, outs, ss, rs, base):
        sends, recvs = parts(ins, outs, ss, rs, base)
        for cp in recvs:
            cp.wait_recv()
        for cp in sends:
            cp.wait_send()

    return _Hosted(bufs, [jax.ShapeDtypeStruct(b.shape, b.dtype) for b in bufs], 3 * n, start, finish,
                   aliases={a: a for a in range(n)})


def _swap(src, nblk, ids_fn, partner_fn):
    def copies(ins, outs, ss, rs, base):
        x, y, c = lax.axis_index("x"), lax.axis_index("y"), lax.axis_index("c")
        ids = ids_fn(x, y, c)
        partner = partner_fn(x, y, c)
        return [pltpu.make_async_remote_copy(
            src_ref=ins[0].at[ids[k]], dst_ref=outs[0].at[k], send_sem=ss.at[base + k], recv_sem=rs.at[base + k],
            device_id=partner, device_id_type=MESH) for k in range(nblk)]

    def start(ins, outs, ss, rs, base):
        for cp in copies(ins, outs, ss, rs, base):
            cp.start()

    def finish(ins, outs, ss, rs, base):
        for cp in copies(ins, outs, ss, rs, base):
            cp.wait()

    return _Hosted([src], [jax.ShapeDtypeStruct((nblk,) + src.shape[1:], src.dtype)], nblk, start, finish)


def _blocking(comm, name):
    n_i, n_o = len(comm.arrays), len(comm.out_shapes)

    def body(*refs):
        ins, outs = refs[:n_i], refs[n_i:n_i + n_o]
        comm.start(ins, outs, refs[-2], refs[-1], 0)
        comm.finish(ins, outs, refs[-2], refs[-1], 0)

    return pl.pallas_call(
        body, name=name, out_shape=comm.out_shapes, in_specs=[ANY] * n_i, out_specs=[ANY] * n_o,
        scratch_shapes=[pltpu.SemaphoreType.DMA((comm.nsem,)), pltpu.SemaphoreType.DMA((comm.nsem,))],
        input_output_aliases=comm.aliases)(*comm.arrays)


def _swap_chips(send):
    def copies(ins, outs, ss, rs, base):
        x, y, c = lax.axis_index("x"), lax.axis_index("y"), lax.axis_index("c")
        chips = [(1 - x, y), (x, 1 - y), (1 - x, 1 - y)]
        return [pltpu.make_async_remote_copy(
            src_ref=ins[0].at[j], dst_ref=outs[0].at[j], send_sem=ss.at[base + j], recv_sem=rs.at[base + j],
            device_id=(*chip, c), device_id_type=MESH) for j, chip in enumerate(chips)]

    def start(ins, outs, ss, rs, base):
        for cp in copies(ins, outs, ss, rs, base):
            cp.start()

    def finish(ins, outs, ss, rs, base):
        for cp in copies(ins, outs, ss, rs, base):
            cp.wait()

    return _Hosted([send], [jax.ShapeDtypeStruct(send.shape, send.dtype)], 3, start, finish)


def _add_send(a, b, idx, ns, name):
    _, r, c = a.shape
    tr = _tile(r, 256)

    def body(idx_ref, a_ref, b_ref, send_ref):
        send_ref[...] = (a_ref[...] + b_ref[...]).astype(BF16)

    def sel(off):
        return pl.BlockSpec((None, tr, c), lambda k, i, s: (s[off + k], i, 0))

    gs = pltpu.PrefetchScalarGridSpec(num_scalar_prefetch=1, grid=(ns, r // tr), in_specs=[sel(0), sel(ns)],
                                      out_specs=pl.BlockSpec((None, tr, c), lambda k, i, s: (k, i, 0)))
    return _pc(body, name=name, grid_spec=gs, sem=("arbitrary", "arbitrary"),
               out_shape=jax.ShapeDtypeStruct((ns, r, c), BF16))(idx, a, b)


class _ReduceScatter:
    def __init__(self, g, tag):
        self.g, self.tag = g, tag

    def swap_core(self):
        return _swap(self.g, 4, lambda x, y, c: [1 - c, 3 - c, 5 - c, 7 - c], lambda x, y, c: (x, y, 1 - c))

    def after_core(self, recv):
        x, y, c = lax.axis_index("x"), lax.axis_index("y"), lax.axis_index("c")
        chips = [(1 - x, y), (x, 1 - y), (1 - x, 1 - y)]
        idx = jnp.stack([4 * p + 2 * q + c for p, q in chips] + [2 * p + q for p, q in chips]).astype(jnp.int32)
        self.send = _add_send(self.g, recv, idx, 3, "rs_add_" + self.tag)
        self.recv_core = recv
        zero = jnp.zeros((), jnp.int32)
        self.idx = jnp.stack([4 * x + 2 * y + c, 2 * x + y, zero, zero + 1, zero + 2]).astype(jnp.int32)

    def swap_chips(self):
        return _swap_chips(self.send)

    def after_chips(self, recv):
        self.parts = [self.g, self.recv_core, recv, recv, recv]


def _ada_fwd(c_all, w_ada, b_cols, b_lb):
    nl, d, ncol = w_ada.shape
    nseq = c_all.shape[0]
    di = b_lb.shape[1]

    def body(c_ref, w_ref, b_ref, lb_ref, mod_ref, lbj_ref):
        cv = c_ref[...]
        cact = (cv * _sigmoid(cv)).astype(BF16)
        for l in range(nl):
            mod_ref[l] = _dot(cact, w_ref[l].astype(BF16)) + b_ref[l]
        b0, b1 = lb_ref[0:1, :], lb_ref[1:2, :]
        mx = jnp.maximum(b0, b1)
        e0, e1 = jnp.exp(b0 - mx), jnp.exp(b1 - mx)
        s = e0 + e1
        p0, p1 = e0 / s, e1 / s
        lbj_ref[0:1, :] = (p0 + p1) - p0
        lbj_ref[1:2, :] = p0 * p1

    return _pc(body, name="ada_fwd",
               out_shape=[jax.ShapeDtypeStruct((nl, nseq, ncol), F32), jax.ShapeDtypeStruct((2, di), F32)]
               )(c_all, w_ada, b_cols, b_lb)


def _ada_bwd(c_all, dmod_cols, dmod_full):
    nl, nseq, ncol = dmod_cols.shape
    d = c_all.shape[1]
    d3 = dmod_full.shape[2]

    def body(c_ref, dc_ref, df_ref, gw_ref, gb_ref):
        cv = c_ref[...]
        cact = (cv * _sigmoid(cv)).astype(BF16)
        for l in range(nl):
            gw_ref[l] = _dot_tn(cact, dc_ref[l].astype(BF16))
            gb_ref[l:l + 1, :] = jnp.sum(df_ref[l], axis=0, keepdims=True)

    return _pc(body, name="ada_bwd",
               out_shape=[jax.ShapeDtypeStruct((nl, d, ncol), F32), jax.ShapeDtypeStruct((nl, d3), F32)]
               )(c_all, dmod_cols, dmod_full)


def _prenorm(x, gain, mod, t_seq, name, comm=None):
    m, d = x.shape
    tm = _tile(t_seq, 512)
    per = t_seq // tm

    def body(x_ref, g_ref, mod_ref, h_ref, ht_ref):
        xv = x_ref[...]
        rstd = lax.rsqrt(jnp.mean(xv * xv, axis=-1, keepdims=True) + EPS)
        r = xv * rstd * g_ref[...]
        h = r * (1.0 + mod_ref[0, 1:2, :]) + mod_ref[0, 0:1, :]
        h_ref[...] = h.astype(BF16)
        ht_ref[...] = h.T.astype(BF16)

    return _pc(body, name=name, out_shape=[jax.ShapeDtypeStruct((m, d), BF16), jax.ShapeDtypeStruct((d, m), BF16)],
               grid=(m // tm,),
               in_specs=[pl.BlockSpec((tm, d), lambda i: (i, 0)), pl.BlockSpec((1, d), lambda i: (0, 0)),
                         pl.BlockSpec((1, 3, d), lambda i: (i // per, 0, 0))],
               out_specs=[pl.BlockSpec((tm, d), lambda i: (i, 0)), pl.BlockSpec((d, tm), lambda i: (0, i))],
               sem=("parallel",), comm=comm)(x, gain, mod)


def _prenorm_bwd(dh, x, gain, mod, dxn, t_seq, name, comm=None):
    m, d = x.shape
    nb = m // t_seq
    tm = _tile(t_seq, 512)
    per = t_seq // tm

    def body(dh_ref, x_ref, g_ref, mod_ref, dxn_ref, dx_ref, dss_ref, dg_ref):
        i = pl.program_id(0)
        xv, dhv, g = x_ref[...], dh_ref[...], g_ref[...]
        rstd = lax.rsqrt(jnp.mean(xv * xv, axis=-1, keepdims=True) + EPS)
        xhat = xv * rstd
        dr = dhv * (1.0 + mod_ref[0, 1:2, :])
        dxhat = dr * g
        dx_ref[...] = dxn_ref[...] + rstd * (dxhat - xhat * jnp.mean(dxhat * xhat, axis=-1, keepdims=True))

        @pl.when(i % per == 0)
        def _():
            dss_ref[...] = jnp.zeros_like(dss_ref)

        @pl.when(i == 0)
        def _():
            dg_ref[...] = jnp.zeros_like(dg_ref)

        dss_ref[0, 0:1, :] += jnp.sum(dhv, axis=0, keepdims=True)
        dss_ref[0, 1:2, :] += jnp.sum(dhv * (xhat * g), axis=0, keepdims=True)
        dg_ref[...] += jnp.sum(dr * xhat, axis=0, keepdims=True)

    row = pl.BlockSpec((tm, d), lambda i: (i, 0))
    return _pc(body, name=name,
               out_shape=[jax.ShapeDtypeStruct((m, d), F32), jax.ShapeDtypeStruct((nb, 2, d), F32),
                          jax.ShapeDtypeStruct((1, d), F32)],
               grid=(m // tm,),
               in_specs=[row, row, pl.BlockSpec((1, d), lambda i: (0, 0)),
                         pl.BlockSpec((1, 3, d), lambda i: (i // per, 0, 0)), row],
               out_specs=[row, pl.BlockSpec((1, 2, d), lambda i: (i // per, 0, 0)),
                          pl.BlockSpec((1, d), lambda i: (0, 0))],
               sem=("arbitrary",), comm=comm)(dh, x, gain, mod, dxn)


def _in_proj_gather(h, w_own, extra, name):
    m, k = h.shape
    nc = w_own.shape[1]
    tm = _tile(m, 512)
    nt = m // tm
    n_ei, n_eo = len(extra.arrays), len(extra.out_shapes)
    n_own = 7

    def body(perm_ref, h_ref, w_ref, *rest):
        e_in, proj_ref, wg_ref = rest[:n_ei], rest[n_ei], rest[n_ei + 1]
        e_out = rest[n_ei + 2:n_ei + 2 + n_eo]
        wbuf, ssem, rsem, lsem = rest[n_ei + 2 + n_eo:]
        jj, i = pl.program_id(0), pl.program_id(1)
        x, y, c = lax.axis_index("x"), lax.axis_index("y"), lax.axis_index("c")
        sibling = (x, y, 1 - c)
        chips = [(1 - x, y), (x, 1 - y), (1 - x, 1 - y)]

        def blk(p, q, cc):
            return wg_ref.at[4 * p + 2 * q + cc]

        def rcopy(kk, src, dst, to):
            return pltpu.make_async_remote_copy(src_ref=src, dst_ref=dst, send_sem=ssem.at[kk], recv_sem=rsem.at[kk],
                                                device_id=to, device_id_type=MESH)

        def load(pair, slot):
            return pltpu.make_async_copy(wg_ref.at[pl.ds(2 * pair, 2)], wbuf.at[slot], lsem.at[slot])

        own_local = pltpu.make_async_copy(w_ref, blk(x, y, c), lsem.at[2])
        first = [rcopy(0, w_ref, blk(x, y, c), sibling)]
        first += [rcopy(1 + s, w_ref, blk(x, y, c), (*chip, c)) for s, chip in enumerate(chips)]
        passed = [rcopy(4 + s, blk(*chip, c), blk(*chip, c), sibling) for s, chip in enumerate(chips)]

        @pl.when((jj == 0) & (i == 0))
        def _():
            own_local.start()
            for cp in first:
                cp.start()
            extra.start(e_in, e_out, ssem, rsem, n_own)
            own_local.wait()
            rcopy(0, w_ref, blk(x, y, 1 - c), sibling).wait_recv()
            ld = load(perm_ref[0], 0)
            ld.start()
            ld.wait()

        for s in range(1, 4):
            @pl.when((jj == s) & (i == 0))
            def _(s=s):
                load(perm_ref[s], s % 2).wait()

        hv = h_ref[...]
        slot = jj % 2
        proj_ref[:, :nc] = _dot(hv, wbuf[slot, 0])
        proj_ref[:, nc:] = _dot(hv, wbuf[slot, 1])

        for s, chip in enumerate(chips):
            @pl.when((jj == s) & (i == nt - 1))
            def _(s=s, chip=chip):
                rcopy(1 + s, w_ref, blk(*chip, c), (*chip, c)).wait_recv()
                passed[s].start()
                rcopy(4 + s, blk(*chip, 1 - c), blk(*chip, 1 - c), sibling).wait_recv()
                load(perm_ref[s + 1], (s + 1) % 2).start()

        @pl.when((jj == 3) & (i == nt - 1))
        def _():
            for cp in first + passed:
                cp.wait_send()
            extra.finish(e_in, e_out, ssem, rsem, n_own)

    gs = pltpu.PrefetchScalarGridSpec(
        num_scalar_prefetch=1, grid=(4, nt),
        in_specs=[pl.BlockSpec((tm, k), lambda jj, i, p: (i, 0)), ANY] + [ANY] * n_ei,
        out_specs=[pl.BlockSpec((tm, 2 * nc), lambda jj, i, p: (i, p[jj])), ANY] + [ANY] * n_eo,
        scratch_shapes=[pltpu.VMEM((2, 2, k, nc), BF16), pltpu.SemaphoreType.DMA((n_own + extra.nsem,)),
                        pltpu.SemaphoreType.DMA((n_own + extra.nsem,)), pltpu.SemaphoreType.DMA((3,))])
    xi, yi = lax.axis_index("x"), lax.axis_index("y")
    perm = jnp.stack([2 * xi + yi, 2 * (1 - xi) + yi, 2 * xi + (1 - yi), 2 * (1 - xi) + (1 - yi)]).astype(jnp.int32)
    res = _pc(body, name=name, grid_spec=gs, sem=("arbitrary", "arbitrary"),
              out_shape=[jax.ShapeDtypeStruct((m, NDEV * nc), F32), jax.ShapeDtypeStruct((NDEV, k, nc), BF16)]
              + extra.out_shapes)(perm, h, w_own, *extra.arrays)
    return res[0], res[1], list(res[2:])


def _mm_in(h, ws, sections, name, comm=None):
    m, k = h.shape
    nw, ncp = len(ws), ws[0].shape[2]
    nc = nw * ncp
    per = NDEV // sections if sections > 1 else NDEV
    tm = _tile(m, 512)
    assert per % 2 == 0

    def body(*refs):
        hv = refs[0][...]
        o_ref = refs[1 + nw]
        for b in range(2):
            for a in range(nw):
                lo = b * nc + a * ncp
                o_ref[:, lo:lo + ncp] = _dot(hv, refs[1 + a][b])

    w_spec = pl.BlockSpec((2, k, ncp), lambda j, i: (j, 0, 0))
    if sections > 1:
        out_shape = jax.ShapeDtypeStruct((sections, m, per * nc), F32)
        out_spec = pl.BlockSpec((None, tm, 2 * nc), lambda j, i: ((2 * j) // per, i, ((2 * j) % per) // 2))
    else:
        out_shape = jax.ShapeDtypeStruct((m, NDEV * nc), F32)
        out_spec = pl.BlockSpec((tm, 2 * nc), lambda j, i: (i, j))
    return _pc(body, name=name, out_shape=out_shape, grid=(NDEV // 2, m // tm),
               in_specs=[pl.BlockSpec((tm, k), lambda j, i: (i, 0))] + [w_spec] * nw,
               out_specs=out_spec, sem=("parallel", "parallel"), comm=comm)(h, *ws)


def _din_tile(m):
    return 1024 if m % 1024 == 0 and m >= 2048 else _tile(m, 512)


def _mm_din(dproj, ws, sections, name, comm=None, tiles=None, prev=None):
    nw, k, ncp = len(ws), ws[0].shape[1], ws[0].shape[2]
    nc = nw * ncp
    m = dproj.shape[-2]
    tm = _din_tile(m)
    t0, nt = tiles if tiles is not None else (0, m // tm)
    per = NDEV // sections if sections > 1 else NDEV
    assert per % 2 == 0

    def body(*refs):
        d_ref, o_ref = refs[0], refs[-1]
        j = pl.program_id(1)
        acc = None
        for b in range(2):
            for a in range(nw):
                lo = b * nc + a * ncp
                term = _dot_nt(d_ref[:, lo:lo + ncp], refs[1 + a][b])
                acc = term if acc is None else acc + term

        @pl.when(j == 0)
        def _():
            o_ref[...] = acc

        @pl.when(j > 0)
        def _():
            o_ref[...] += acc

    if sections > 1:
        dspec = pl.BlockSpec((None, tm, 2 * nc), lambda i, j: ((2 * j) // per, i + t0, ((2 * j) % per) // 2))
    else:
        dspec = pl.BlockSpec((tm, 2 * nc), lambda i, j: (i + t0, j))
    in_specs = [dspec] + [pl.BlockSpec((2, k, ncp), lambda i, j: (j, 0, 0))] * nw
    args = [dproj, *ws]
    if prev is not None:
        in_specs.append(ANY)
        args.append(prev)
    return _pc(body, name=name, out_shape=jax.ShapeDtypeStruct((m, k), F32), grid=(nt, NDEV // 2), in_specs=in_specs,
               out_specs=pl.BlockSpec((tm, k), lambda i, j: (i + t0, 0)), sem=("parallel", "arbitrary"),
               comm=comm, aliases={1 + nw: 0} if prev is not None else None)(*args)


def _mm_dw_in(ht, dproj, nc, sections, name, comm=None):
    k, m = ht.shape
    tk = 2048 if m % 2048 == 0 else _din_tile(m)
    per = NDEV // sections if sections > 1 else NDEV

    def body(h_ref, d_ref, o_ref):
        kk = pl.program_id(1)
        acc = _dot(h_ref[...], d_ref[...])

        @pl.when(kk == 0)
        def _():
            o_ref[...] = acc

        @pl.when(kk > 0)
        def _():
            o_ref[...] += acc

    if sections > 1:
        dspec = pl.BlockSpec((None, tk, nc), lambda j, i: (j // per, i, j % per))
    else:
        dspec = pl.BlockSpec((tk, nc), lambda j, i: (i, j))
    return _pc(body, name=name, out_shape=jax.ShapeDtypeStruct((NDEV, k, nc), F32), grid=(NDEV, m // tk),
               in_specs=[pl.BlockSpec((k, tk), lambda j, i: (0, i)), dspec],
               out_specs=pl.BlockSpec((None, k, nc), lambda j, i: (j, 0, 0)),
               sem=("parallel", "arbitrary"), comm=comm)(ht, dproj)


def _out_proj(ybr, w_out, x, mod, t_seq, name, comm=None):
    m, di = ybr.shape
    d = w_out.shape[1]
    tm = _tile(t_seq, 512)
    per = t_seq // tm

    def body(y_ref, w_ref, x_ref, mod_ref, yo_ref, xn_ref):
        yo = _dot(y_ref[...], w_ref[...])
        yo_ref[...] = yo
        xn_ref[...] = x_ref[...] + mod_ref[0, 2:3, :] * yo

    row = pl.BlockSpec((tm, d), lambda i: (i, 0))
    return _pc(body, name=name,
               out_shape=[jax.ShapeDtypeStruct((m, d), F32), jax.ShapeDtypeStruct((m, d), F32)],
               grid=(m // tm,),
               in_specs=[pl.BlockSpec((tm, di), lambda i: (i, 0)), pl.BlockSpec((di, d), lambda i: (0, 0)), row,
                         pl.BlockSpec((1, 3, d), lambda i: (i // per, 0, 0))],
               out_specs=[row, row], sem=("parallel",), comm=comm)(ybr, w_out, x, mod)


def _out_proj_loss(ybr, w_out, x, mod, gain, target, t_seq):
    m, di = ybr.shape
    d = w_out.shape[1]
    tm = _tile(t_seq, 512)
    per = t_seq // tm

    def body(y_ref, w_ref, x_ref, mod_ref, g_ref, t_ref, yo_ref, dx_ref, loss_ref, dg_ref):
        i = pl.program_id(0)
        yo = _dot(y_ref[...], w_ref[...])
        yo_ref[...] = yo
        xv = x_ref[...] + mod_ref[0, 2:3, :] * yo
        g = g_ref[...]
        rstd = lax.rsqrt(jnp.mean(xv * xv, axis=-1, keepdims=True) + EPS)
        xhat = xv * rstd
        err = xhat * g - t_ref[...]
        dy = err * (1.0 / d)
        dxhat = dy * g
        dx_ref[...] = rstd * (dxhat - xhat * jnp.mean(dxhat * xhat, axis=-1, keepdims=True))

        @pl.when(i == 0)
        def _():
            loss_ref[...] = jnp.zeros_like(loss_ref)
            dg_ref[...] = jnp.zeros_like(dg_ref)

        loss_ref[...] += 0.5 * jnp.sum(jnp.mean(err * err, axis=-1, keepdims=True), axis=0, keepdims=True)
        dg_ref[...] += jnp.sum(dy * xhat, axis=0, keepdims=True)

    row = pl.BlockSpec((tm, d), lambda i: (i, 0))
    vec = pl.BlockSpec((1, d), lambda i: (0, 0))
    return _pc(body, name="out_proj_loss",
               out_shape=[jax.ShapeDtypeStruct((m, d), F32), jax.ShapeDtypeStruct((m, d), F32),
                          jax.ShapeDtypeStruct((1, 1), F32), jax.ShapeDtypeStruct((1, d), F32)],
               grid=(m // tm,),
               in_specs=[pl.BlockSpec((tm, di), lambda i: (i, 0)), pl.BlockSpec((di, d), lambda i: (0, 0)), row,
                         pl.BlockSpec((1, 3, d), lambda i: (i // per, 0, 0)), vec, row],
               out_specs=[row, row, pl.BlockSpec((1, 1), lambda i: (0, 0)), vec],
               sem=("arbitrary",))(ybr, w_out, x, mod, gain, target)


def _gate_dybr(dxn, yout, mod, w_out, t_seq, name):
    m, d = dxn.shape
    di = w_out.shape[0]
    nb = m // t_seq
    tm = _tile(t_seq, 512)
    per = t_seq // tm

    def body(dxn_ref, yo_ref, mod_ref, w_ref, dy_ref, dgate_ref, o_ref):
        i = pl.program_id(0)
        dv = dxn_ref[...]
        dy = (mod_ref[0, 2:3, :] * dv).astype(BF16)
        dy_ref[...] = dy
        o_ref[...] = _dot_nt(dy, w_ref[...])

        @pl.when(i % per == 0)
        def _():
            dgate_ref[...] = jnp.zeros_like(dgate_ref)

        dgate_ref[0] += jnp.sum(dv * yo_ref[...], axis=0, keepdims=True)

    row = pl.BlockSpec((tm, d), lambda i: (i, 0))
    return _pc(body, name=name,
               out_shape=[jax.ShapeDtypeStruct((m, d), BF16), jax.ShapeDtypeStruct((nb, 1, d), F32),
                          jax.ShapeDtypeStruct((m, di), F32)],
               grid=(m // tm,),
               in_specs=[row, row, pl.BlockSpec((1, 3, d), lambda i: (i // per, 0, 0)),
                         pl.BlockSpec((di, d), lambda i: (0, 0))],
               out_specs=[row, pl.BlockSpec((1, 1, d), lambda i: (i // per, 0, 0)),
                          pl.BlockSpec((tm, di), lambda i: (i, 0))],
               sem=("arbitrary",))(dxn, yout, mod, w_out)


def _mm_dw_out(ybr, dy, name, comm=None):
    m, di = ybr.shape
    d = dy.shape[1]
    tk = _tile(m, 512)
    tn = _tile(di, 1024)

    def body(y_ref, dy_ref, o_ref):
        kk = pl.program_id(1)
        acc = _dot_tn(y_ref[...], dy_ref[...])

        @pl.when(kk == 0)
        def _():
            o_ref[...] = acc

        @pl.when(kk > 0)
        def _():
            o_ref[...] += acc

    return _pc(body, name=name, out_shape=jax.ShapeDtypeStruct((di, d), F32), grid=(di // tn, m // tk),
               in_specs=[pl.BlockSpec((tk, tn), lambda n, k: (k, n)), pl.BlockSpec((tk, d), lambda n, k: (k, 0))],
               out_specs=pl.BlockSpec((tn, d), lambda n, k: (n, 0)), sem=("parallel", "arbitrary"),
               comm=comm)(ybr, dy)


def _sgu_mask():
    t = lax.broadcasted_iota(jnp.int32, (SG_BLOCK, SG_BLOCK), 0)
    s = lax.broadcasted_iota(jnp.int32, (SG_BLOCK, SG_BLOCK), 1)
    return (s // CHUNK) <= (t // CHUNK)


def _a_mid_fwd(proj, ln_g, ln_b, w_s, bs_t, t_seq, comm=None):
    m, n3 = proj.shape
    di = n3 // 3
    gd = di // SG_GROUPS
    r = _tile(t_seq, 256)
    nblk = r // SG_BLOCK

    def body(p_ref, lg_ref, lb_ref, ws_ref, bs_ref, ybr_ref, s_scr):
        v = _gelu(p_ref[:, di:2 * di])
        mu = jnp.mean(v, axis=-1, keepdims=True)
        vc = v - mu
        rstd = lax.rsqrt(jnp.mean(vc * vc, axis=-1, keepdims=True) + EPS)
        vb = (vc * rstd * lg_ref[...] + lb_ref[...]).astype(BF16)
        mask = _sgu_mask()
        for gi in range(SG_GROUPS):
            ws = jnp.where(mask, ws_ref[gi], 0.0).astype(BF16)
            bcol = bs_ref[:, gi:gi + 1]
            for b in range(nblk):
                rows = slice(b * SG_BLOCK, (b + 1) * SG_BLOCK)
                cols = slice(gi * gd, (gi + 1) * gd)
                s_scr[rows, cols] = _dot(ws, vb[rows, cols]) + bcol
        gg = p_ref[:, 2 * di:]
        ybr_ref[...] = (_gelu(p_ref[:, :di]) * s_scr[...] * (gg * _sigmoid(gg))).astype(BF16)

    vec = pl.BlockSpec((1, di), lambda i: (0, 0))
    return _pc(body, name="a_mid_fwd", out_shape=jax.ShapeDtypeStruct((m, di), BF16), grid=(m // r,),
               in_specs=[pl.BlockSpec((r, n3), lambda i: (i, 0)), vec, vec,
                         pl.BlockSpec((SG_GROUPS, SG_BLOCK, SG_BLOCK), lambda i: (0, 0, 0)),
                         pl.BlockSpec((SG_BLOCK, 128), lambda i: (0, 0))],
               out_specs=pl.BlockSpec((r, di), lambda i: (i, 0)),
               scratch=[pltpu.VMEM((r, di), F32)], sem=("parallel",), comm=comm)(proj, ln_g, ln_b, w_s, bs_t)


def _a_mid_bwd(proj, dybr, ln_g, ln_b, w_s, bs_t, t_seq, comm=None):
    m, n3 = proj.shape
    di = n3 // 3
    gd = di // SG_GROUPS
    r = _tile(t_seq, 256)
    nblk = r // SG_BLOCK

    def body(p_ref, dy_ref, lg_ref, lb_ref, ws_ref, bs_ref,
             dp_ref, dlg_ref, dlb_ref, dws_ref, dbs_ref, s_scr, dvl_scr):
        i = pl.program_id(0)

        @pl.when(i == 0)
        def _():
            dlg_ref[...] = jnp.zeros_like(dlg_ref)
            dlb_ref[...] = jnp.zeros_like(dlb_ref)
            dws_ref[...] = jnp.zeros_like(dws_ref)
            dbs_ref[...] = jnp.zeros_like(dbs_ref)

        v, dgelu_v = _gelu_and_grad(p_ref[:, di:2 * di])
        mu = jnp.mean(v, axis=-1, keepdims=True)
        vc = v - mu
        rstd = lax.rsqrt(jnp.mean(vc * vc, axis=-1, keepdims=True) + EPS)
        vhat = vc * rstd
        lg = lg_ref[...]
        vb = (vhat * lg + lb_ref[...]).astype(BF16)
        u, dgelu_u = _gelu_and_grad(p_ref[:, :di])
        gg = p_ref[:, 2 * di:]
        sg = _sigmoid(gg)
        dyv = dy_ref[...]
        dus = dyv * (gg * sg)
        dsb = (dus * u).astype(BF16)
        ds32 = dus * u
        mask = _sgu_mask()
        lane = lax.broadcasted_iota(jnp.int32, (SG_BLOCK, 128), 1)
        dbs_acc = jnp.zeros((SG_BLOCK, 128), F32)
        for gi in range(SG_GROUPS):
            ws = jnp.where(mask, ws_ref[gi], 0.0).astype(BF16)
            bcol = bs_ref[:, gi:gi + 1]
            cols = slice(gi * gd, (gi + 1) * gd)
            dws_acc = jnp.zeros((SG_BLOCK, SG_BLOCK), F32)
            dbs_col = jnp.zeros((SG_BLOCK, 1), F32)
            for b in range(nblk):
                rows = slice(b * SG_BLOCK, (b + 1) * SG_BLOCK)
                s_scr[rows, cols] = _dot(ws, vb[rows, cols]) + bcol
                dvl_scr[rows, cols] = _dot_tn(ws, dsb[rows, cols])
                dws_acc += _dot_nt(dsb[rows, cols], vb[rows, cols])
                dbs_col += jnp.sum(ds32[rows, cols], axis=-1, keepdims=True)
            dws_ref[gi] += jnp.where(mask, dws_acc, 0.0)
            dbs_acc += jnp.where(lane == gi, dbs_col, 0.0)
        dbs_ref[...] += dbs_acc
        s = s_scr[...]
        dp_ref[:, :di] = (dus * s * dgelu_u).astype(BF16)
        dp_ref[:, 2 * di:] = (dyv * u * s * (sg * (1.0 + gg * (1.0 - sg)))).astype(BF16)
        dvl = dvl_scr[...]
        dlg_ref[...] += jnp.sum(dvl * vhat, axis=0, keepdims=True)
        dlb_ref[...] += jnp.sum(dvl, axis=0, keepdims=True)
        dvh = dvl * lg
        dv = rstd * (dvh - jnp.mean(dvh, axis=-1, keepdims=True)
                     - vhat * jnp.mean(dvh * vhat, axis=-1, keepdims=True))
        dp_ref[:, di:2 * di] = (dv * dgelu_v).astype(BF16)

    vec = pl.BlockSpec((1, di), lambda i: (0, 0))
    wsb = pl.BlockSpec((SG_GROUPS, SG_BLOCK, SG_BLOCK), lambda i: (0, 0, 0))
    bsb = pl.BlockSpec((SG_BLOCK, 128), lambda i: (0, 0))
    return _pc(body, name="a_mid_bwd",
               out_shape=[jax.ShapeDtypeStruct((m, n3), BF16), jax.ShapeDtypeStruct((1, di), F32),
                          jax.ShapeDtypeStruct((1, di), F32),
                          jax.ShapeDtypeStruct((SG_GROUPS, SG_BLOCK, SG_BLOCK), F32),
                          jax.ShapeDtypeStruct((SG_BLOCK, 128), F32)],
               grid=(m // r,),
               in_specs=[pl.BlockSpec((r, n3), lambda i: (i, 0)), pl.BlockSpec((r, di), lambda i: (i, 0)),
                         vec, vec, wsb, bsb],
               out_specs=[pl.BlockSpec((r, n3), lambda i: (i, 0)), vec, vec, wsb, bsb],
               scratch=[pltpu.VMEM((r, di), F32), pltpu.VMEM((r, di), F32)],
               sem=("arbitrary",), comm=comm)(proj, dybr, ln_g, ln_b, w_s, bs_t)


def _hgrn_dims(t_seq, di):
    tr = _tile(t_seq, 256)
    hc = _tile(di, 1024)
    return tr, hc, hc // HEAD_DIM


def _hgrn_gates(f_ref, lb, a_scr, k_scr, tr):
    sig = _sigmoid(f_ref[...])
    fg = lb + (1.0 - lb) * sig
    k_scr[...] = 1.0 - fg
    logf = jnp.log(fg)
    g = min(CUM_ROWS, tr)
    tri = _tri_mask(g, reverse=False)
    for rg in range(tr // g):
        a_scr[rg * g:(rg + 1) * g, :] = _tri_apply(tri, logf[rg * g:(rg + 1) * g, :])
    return sig, fg


def _hgrn_fwd(proj, lbj, gn, nb, t_seq, comm=None):
    _, m, di = proj.shape
    tr, hc, hpg = _hgrn_dims(t_seq, di)
    nt, nhg, ncl = t_seq // tr, di // hc, tr // CHUNK
    nheads = di // HEAD_DIM

    def body(q_ref, f_ref, i_ref, g_ref, lb_ref, gn_ref, o_ref, ybr_ref, st_ref, st_scr, a_scr, k_scr):
        t = pl.program_id(2)

        @pl.when(t == 0)
        def _():
            st_scr[...] = jnp.zeros_like(st_scr)

        _hgrn_gates(f_ref, lb_ref[0:1, :], a_scr, k_scr, tr)
        gnv = gn_ref[...]
        rr = lax.broadcasted_iota(jnp.int32, (CHUNK, CHUNK), 0)
        cc = lax.broadcasted_iota(jnp.int32, (CHUNK, CHUNK), 1)
        causal = cc <= rr

        def chunk(n, carry):
            rows = pl.ds(pl.multiple_of(n * CHUNK, CHUNK), CHUNK)
            lanes = [slice(hd * HEAD_DIM, (hd + 1) * HEAD_DIM) for hd in range(hpg)]
            hs = []
            for hd, ls in enumerate(lanes):
                h = {}
                ah, kh = a_scr[rows, ls], k_scr[rows, ls]
                qp = q_ref[rows, ls]
                qh = qp * _sigmoid(qp)
                h["vb"] = i_ref[rows, ls].astype(BF16)
                aref, alast = ah[CHUNK // 2 - 1:CHUNK // 2, :], ah[CHUNK - 1:CHUNK, :]
                h["q_in"] = (qh * jnp.exp(ah - aref)).astype(BF16)
                h["k_in"] = (kh * jnp.exp(aref - ah)).astype(BF16)
                h["q_out"] = (qh * jnp.exp(ah)).astype(BF16)
                h["k_out"] = (kh * jnp.exp(alast - ah)).astype(BF16)
                h["dec"] = jnp.exp(alast)
                st = st_scr[hd]
                st_ref[n, hd] = st
                h["st"] = st
                hs.append(h)
            for h in hs:
                h["scores"] = _dot_nt(h["q_in"], h["k_in"])
                h["o_inter"] = _dot_nt(h["q_out"], h["st"].astype(BF16))
                h["st_mm"] = _dot_tn(h["vb"], h["k_out"])
            for h in hs:
                h["o"] = _dot(jnp.where(causal, h["scores"], 0.0).astype(BF16), h["vb"]) + h["o_inter"]
            for hd, (h, ls) in enumerate(zip(hs, lanes)):
                st_scr[hd] = h["st"] * h["dec"] + h["st_mm"]
                o = h["o"]
                o_ref[rows, ls] = o
                rstd = lax.rsqrt(jnp.mean(o * o, axis=-1, keepdims=True) + EPS)
                gg = g_ref[rows, ls]
                ybr_ref[rows, ls] = ((o * rstd * gnv) * (gg * _sigmoid(gg))).astype(BF16)
            return carry

        lax.fori_loop(0, ncl, chunk, 0)

    def sec(s):
        return pl.BlockSpec((None, tr, hc), lambda hg, b, t: (s, b * nt + t, hg))

    blk = pl.BlockSpec((tr, hc), lambda hg, b, t: (b * nt + t, hg))
    return _pc(body, name="hgrn_fwd",
               out_shape=[jax.ShapeDtypeStruct((m, di), F32), jax.ShapeDtypeStruct((m, di), BF16),
                          jax.ShapeDtypeStruct((m // CHUNK, nheads, HEAD_DIM, HEAD_DIM), F32)],
               grid=(nhg, nb, nt),
               in_specs=[sec(0), sec(1), sec(2), sec(3), pl.BlockSpec((2, hc), lambda hg, b, t: (0, hg)),
                         pl.BlockSpec((1, HEAD_DIM), lambda hg, b, t: (0, 0))],
               out_specs=[blk, blk, pl.BlockSpec((ncl, hpg, HEAD_DIM, HEAD_DIM),
                                                 lambda hg, b, t: (b * nt + t, hg, 0, 0))],
               scratch=[pltpu.VMEM((hpg, HEAD_DIM, HEAD_DIM), F32), pltpu.VMEM((tr, hc), F32),
                        pltpu.VMEM((tr, hc), F32)],
               sem=("parallel", "arbitrary", "arbitrary"), comm=comm)(proj, proj, proj, proj, lbj, gn)


def _hgrn_bwd(proj, o_all, dybr, states, lbj, gn, nb, t_seq, comm=None):
    _, m, di = proj.shape
    tr, hc, hpg = _hgrn_dims(t_seq, di)
    nt, nhg, ncl = t_seq // tr, di // hc, tr // CHUNK

    def body(q_ref, f_ref, i_ref, g_ref, o_ref, dy_ref, st_ref, lb_ref, gn_ref,
             dp_ref, dlb_ref, dgn_ref, dst_scr, a_scr, k_scr, da_scr, dk_scr):
        hg, b, t = pl.program_id(0), pl.program_id(1), pl.program_id(2)

        @pl.when(t == 0)
        def _():
            dst_scr[...] = jnp.zeros_like(dst_scr)

        @pl.when((b == 0) & (t == 0))
        def _():
            dlb_ref[...] = jnp.zeros_like(dlb_ref)

        @pl.when((hg == 0) & (b == 0) & (t == 0))
        def _():
            dgn_ref[...] = jnp.zeros_like(dgn_ref)

        lb = lb_ref[0:1, :]
        sig, fg = _hgrn_gates(f_ref, lb, a_scr, k_scr, tr)
        gnv = gn_ref[...]
        rr = lax.broadcasted_iota(jnp.int32, (CHUNK, CHUNK), 0)
        cc = lax.broadcasted_iota(jnp.int32, (CHUNK, CHUNK), 1)
        causal = cc <= rr
        rowi = lax.broadcasted_iota(jnp.int32, (CHUNK, HEAD_DIM), 0)

        def chunk(it, carry):
            n = ncl - 1 - it
            rows = pl.ds(pl.multiple_of(n * CHUNK, CHUNK), CHUNK)
            lanes = [slice(hd * HEAD_DIM, (hd + 1) * HEAD_DIM) for hd in range(hpg)]
            hs = []
            for hd, ls in enumerate(lanes):
                h = {}
                ah, kh = a_scr[rows, ls], k_scr[rows, ls]
                qp = q_ref[rows, ls]
                sq = _sigmoid(qp)
                qh = qp * sq
                h["dsilu_q"] = sq * (1.0 + qp * (1.0 - sq))
                h["vb"] = i_ref[rows, ls].astype(BF16)
                aref, alast = ah[CHUNK // 2 - 1:CHUNK // 2, :], ah[CHUNK - 1:CHUNK, :]
                h["e1"], h["e2"] = jnp.exp(ah - aref), jnp.exp(aref - ah)
                h["e3"], h["e4"] = jnp.exp(ah), jnp.exp(alast - ah)
                h["dec"] = jnp.exp(alast)
                h["q_in"], h["k_in"], h["q_out"], h["k_out"] = qh * h["e1"], kh * h["e2"], qh * h["e3"], kh * h["e4"]
                for nm in ("q_in", "k_in", "q_out", "k_out"):
                    h[nm + "_b"] = h[nm].astype(BF16)
                o = o_ref[rows, ls]
                rstd = lax.rsqrt(jnp.mean(o * o, axis=-1, keepdims=True) + EPS)
                ohat = o * rstd
                gg = g_ref[rows, ls]
                sg = _sigmoid(gg)
                dyv = dy_ref[rows, ls]
                d_on = dyv * (gg * sg)
                dp_ref[3, rows, ls] = (dyv * (ohat * gnv) * (sg * (1.0 + gg * (1.0 - sg)))).astype(BF16)
                h["dgn"] = jnp.sum(d_on * ohat, axis=0, keepdims=True)
                dohat = d_on * gnv
                do = rstd * (dohat - ohat * jnp.mean(dohat * ohat, axis=-1, keepdims=True))
                h["do_b"] = do.astype(BF16)
                h["st_prev"] = st_ref[n, hd]
                h["dst"] = dst_scr[hd]
                hs.append(h)
            for h in hs:
                dst_b = h["dst"].astype(BF16)
                h["scores"] = _dot_nt(h["q_in_b"], h["k_in_b"])
                h["dscores"] = _dot_nt(h["do_b"], h["vb"])
                h["dv_inter"] = _dot_nt(h["k_out_b"], dst_b)
                h["dq_out"] = _dot(h["do_b"], h["st_prev"].astype(BF16))
                h["dk_out"] = _dot(h["vb"], dst_b)
                h["dst_mm"] = _dot_tn(h["do_b"], h["q_out_b"])
            for h in hs:
                scores = jnp.where(causal, h["scores"], 0.0).astype(BF16)
                dscores = jnp.where(causal, h["dscores"], 0.0).astype(BF16)
                h["dv"] = _dot_tn(scores, h["do_b"]) + h["dv_inter"]
                h["dq_in"] = _dot(dscores, h["k_in_b"])
                h["dk_in"] = _dot_tn(dscores, h["q_in_b"])
            dgn = hs[0]["dgn"]
            for h in hs[1:]:
                dgn = dgn + h["dgn"]
            dgn_ref[...] += dgn
            for hd, (h, ls) in enumerate(zip(hs, lanes)):
                ddec = jnp.sum(h["dst"] * h["st_prev"], axis=0, keepdims=True)
                dst_scr[hd] = h["dst"] * h["dec"] + h["dst_mm"]
                dp_ref[2, rows, ls] = h["dv"].astype(BF16)
                dq = h["dq_in"] * h["e1"] + h["dq_out"] * h["e3"]
                dp_ref[0, rows, ls] = (dq * h["dsilu_q"]).astype(BF16)
                dk_scr[rows, ls] = h["dk_in"] * h["e2"] + h["dk_out"] * h["e4"]
                t_in = h["dq_in"] * h["q_in"] - h["dk_in"] * h["k_in"]
                t_out = h["dk_out"] * h["k_out"]
                da = t_in + h["dq_out"] * h["q_out"] - t_out
                da_ref_row = -jnp.sum(t_in, axis=0, keepdims=True)
                da_last_row = jnp.sum(t_out, axis=0, keepdims=True) + ddec * h["dec"]
                da = da + jnp.where(rowi == CHUNK // 2 - 1, da_ref_row, 0.0) \
                        + jnp.where(rowi == CHUNK - 1, da_last_row, 0.0)
                da_scr[rows, ls] = da
            return carry

        lax.fori_loop(0, ncl, chunk, 0)
        g = min(CUM_ROWS, tr)
        tri = _tri_mask(g, reverse=True)
        for rg in range(tr // g):
            rs = slice(rg * g, (rg + 1) * g)
            dlogf = _tri_apply(tri, da_scr[rs, :])
            df = dlogf / fg[rs, :] - dk_scr[rs, :]
            sgr = sig[rs, :]
            dp_ref[1, rs, :] = (df * (1.0 - lb) * (sgr * (1.0 - sgr))).astype(BF16)
            dlb_ref[...] += jnp.sum(df * (1.0 - sgr), axis=0, keepdims=True) * lb_ref[1:2, :]

    def sec(s):
        return pl.BlockSpec((None, tr, hc), lambda hg, b, t: (s, b * nt + (nt - 1 - t), hg))

    blk = pl.BlockSpec((tr, hc), lambda hg, b, t: (b * nt + (nt - 1 - t), hg))
    return _pc(body, name="hgrn_bwd",
               out_shape=[jax.ShapeDtypeStruct((4, m, di), BF16), jax.ShapeDtypeStruct((1, di), F32),
                          jax.ShapeDtypeStruct((1, HEAD_DIM), F32)],
               grid=(nhg, nb, nt),
               in_specs=[sec(0), sec(1), sec(2), sec(3), blk, blk,
                         pl.BlockSpec((ncl, hpg, HEAD_DIM, HEAD_DIM),
                                      lambda hg, b, t: (b * nt + (nt - 1 - t), hg, 0, 0)),
                         pl.BlockSpec((2, hc), lambda hg, b, t: (0, hg)),
                         pl.BlockSpec((1, HEAD_DIM), lambda hg, b, t: (0, 0))],
               out_specs=[pl.BlockSpec((4, tr, hc), lambda hg, b, t: (0, b * nt + (nt - 1 - t), hg)),
                          pl.BlockSpec((1, hc), lambda hg, b, t: (0, hg)),
                          pl.BlockSpec((1, HEAD_DIM), lambda hg, b, t: (0, 0))],
               scratch=[pltpu.VMEM((hpg, HEAD_DIM, HEAD_DIM), F32)] + [pltpu.VMEM((tr, hc), F32)] * 4,
               sem=("arbitrary", "arbitrary", "arbitrary"), comm=comm)(
                   proj, proj, proj, proj, o_all, dybr, states, lbj, gn)


def _adamw(parts, w, m, v, name, comm=None):
    r, c = w.shape
    tr = _tile(r, 256)
    npart = len(parts)
    c1 = 1.0 - ADAM_B1 ** ADAM_STEP
    c2 = 1.0 - ADAM_B2 ** ADAM_STEP

    def body(*refs):
        p_refs = refs[:npart]
        _adamw_math(p_refs, *refs[npart:], c1, c2)

    blk = pl.BlockSpec((tr, c), lambda i: (i, 0))
    return _pc(body, name=name, out_shape=[jax.ShapeDtypeStruct((r, c), F32)] * 4, grid=(r // tr,),
               in_specs=[blk] * (npart + 3), out_specs=[blk] * 4, sem=("parallel",), comm=comm)(*parts, w, m, v)


def _adamw_math(p_refs, w_ref, m_ref, v_ref, g_ref, d_ref, nm_ref, nv_ref, c1, c2):
    g = p_refs[0][...].astype(F32)
    for p in p_refs[1:]:
        g = g + p[...].astype(F32)
    nm = ADAM_B1 * m_ref[...] + (1.0 - ADAM_B1) * g
    nv = ADAM_B2 * v_ref[...] + (1.0 - ADAM_B2) * (g * g)
    g_ref[...] = g
    nm_ref[...] = nm
    nv_ref[...] = nv
    d_ref[...] = -ADAM_LR * ((nm / c1) / (jnp.sqrt(nv / c2) + ADAM_EPS) + ADAM_WD * w_ref[...])


def _adamw_blocks(parts, idx, w, m, v, name):
    r, c = w.shape
    tr = _tile(r, 256)
    npart = len(parts)
    c1 = 1.0 - ADAM_B1 ** ADAM_STEP
    c2 = 1.0 - ADAM_B2 ** ADAM_STEP

    def body(idx_ref, *refs):
        _adamw_math(refs[:npart], *refs[npart:], c1, c2)

    def sel(p):
        return pl.BlockSpec((None, tr, c), lambda i, s: (s[p], i, 0))

    blk = pl.BlockSpec((tr, c), lambda i, s: (i, 0))
    gs = pltpu.PrefetchScalarGridSpec(num_scalar_prefetch=1, grid=(r // tr,),
                                      in_specs=[sel(p) for p in range(npart)] + [blk] * 3, out_specs=[blk] * 4)
    return _pc(body, name=name, out_shape=[jax.ShapeDtypeStruct((r, c), F32)] * 4, grid_spec=gs,
               sem=("parallel",))(idx, *parts, w, m, v)


_EARLY = ["a_ln_gain", "a_ln_bias", "a_w_s", "a_b_s", "b_lower_bounds", "b_gn_gain"]


def _pack(arrs):
    flat = jnp.concatenate([a.reshape(-1) for a in arrs])
    rows = -(-flat.shape[0] // 1024) * 8
    return jnp.pad(flat, (0, rows * 128 - flat.shape[0])).reshape(rows, 128)


def _unpack(buf, like):
    flat = buf.reshape(-1)
    out, off = [], 0
    for a in like:
        out.append(flat[off:off + a.size].reshape(a.shape))
        off += a.size
    return out


def kernel(x, c, norm_gain, w_ada, b_ada, a_w_in, a_ln_gain, a_ln_bias, a_w_s, a_b_s, a_w_out, b_w_in, b_lower_bounds, b_gn_gain, b_w_out, final_gain, loss_target, m_norm_gain, m_w_ada, m_b_ada, m_a_w_in, m_a_ln_gain, m_a_ln_bias, m_a_w_s, m_a_b_s, m_a_w_out, m_b_w_in, m_b_lower_bounds, m_b_gn_gain, m_b_w_out, m_final_gain, v_norm_gain, v_w_ada, v_b_ada, v_a_w_in, v_a_ln_gain, v_a_ln_bias, v_a_w_s, v_a_b_s, v_a_w_out, v_b_w_in, v_b_lower_bounds, v_b_gn_gain, v_b_w_out, v_final_gain):
    w = dict(norm_gain=norm_gain, w_ada=w_ada, b_ada=b_ada, a_w_in=a_w_in, a_ln_gain=a_ln_gain,
             a_ln_bias=a_ln_bias, a_w_s=a_w_s, a_b_s=a_b_s, a_w_out=a_w_out, b_w_in=b_w_in,
             b_lower_bounds=b_lower_bounds, b_gn_gain=b_gn_gain, b_w_out=b_w_out, final_gain=final_gain)
    mo = dict(norm_gain=m_norm_gain, w_ada=m_w_ada, b_ada=m_b_ada, a_w_in=m_a_w_in, a_ln_gain=m_a_ln_gain,
              a_ln_bias=m_a_ln_bias, a_w_s=m_a_w_s, a_b_s=m_a_b_s, a_w_out=m_a_w_out, b_w_in=m_b_w_in,
              b_lower_bounds=m_b_lower_bounds, b_gn_gain=m_b_gn_gain, b_w_out=m_b_w_out, final_gain=m_final_gain)
    vo = dict(norm_gain=v_norm_gain, w_ada=v_w_ada, b_ada=v_b_ada, a_w_in=v_a_w_in, a_ln_gain=v_a_ln_gain,
              a_ln_bias=v_a_ln_bias, a_w_s=v_a_w_s, a_b_s=v_a_b_s, a_w_out=v_a_w_out, b_w_in=v_b_w_in,
              b_lower_bounds=v_b_lower_bounds, b_gn_gain=v_b_gn_gain, b_w_out=v_b_w_out, final_gain=v_final_gain)

    nb, t_seq, d = x.shape
    m = nb * t_seq
    ncol_ada = w_ada.shape[2]
    xi, yi, ci = lax.axis_index("x"), lax.axis_index("y"), lax.axis_index("c")
    me = 4 * xi + 2 * yi + ci

    c_g = _all_gather([c], "gather_c")[0]

    c_all = c_g.reshape(NDEV * nb, d)
    b_cols = lax.dynamic_slice(b_ada, (0, me * ncol_ada), (2, ncol_ada)).reshape(2, 1, ncol_ada)
    mod_part, lbj = _ada_fwd(c_all, w_ada, b_cols, b_lower_bounds)
    mod_all = _all_gather([mod_part], "gather_mod")[0]
    mod_mine = lax.dynamic_slice_in_dim(mod_all, me * nb, nb, axis=2)
    mod_mine = mod_mine.transpose(1, 2, 0, 3).reshape(2, nb, 3, d)
    mod0, mod1 = mod_mine[0], mod_mine[1]

    di = a_w_out.shape[1] * NDEV

    xf = x.reshape(m, d)
    tgt = loss_target.reshape(m, d)
    ng0, ng1 = norm_gain[0:1], norm_gain[1:2]
    ncb = b_w_in.shape[2]
    wb_lo, wb_hi = b_w_in[0][:, :ncb // 2].astype(BF16), b_w_in[0][:, ncb // 2:].astype(BF16)
    h0, h0_t = _prenorm(xf, ng0, mod0, t_seq, "prenorm_a")
    proj_a, wa_in_g, (wa_out_half,) = _in_proj_gather(h0, a_w_in[0].astype(BF16),
                                                      _gather_first([a_w_out[0].astype(BF16)]), "in_proj_a")
    bs_t = jnp.pad(a_b_s[0].T, ((0, 0), (0, 128 - SG_GROUPS)))
    ybr_a, (wa_out_g, wb_lo_half) = _a_mid_fwd(
        proj_a, a_ln_gain, a_ln_bias, a_w_s[0], bs_t, t_seq,
        comm=_join(_gather_second([wa_out_half]), _gather_first([wb_lo])))
    wa_out = wa_out_g.reshape(di, d)
    (yout_a, x1), (wb_lo_g, wb_hi_half) = _out_proj(
        ybr_a, wa_out, xf, mod0, t_seq, "out_proj_a", comm=_join(_gather_second([wb_lo_half]), _gather_first([wb_hi])))
    (h1, h1_t), (wb_hi_g,) = _prenorm(x1, ng1, mod1, t_seq, "prenorm_b", comm=_gather_second([wb_hi_half]))
    wb_in_g = [wb_lo_g, wb_hi_g]
    proj_b, (wb_out_half,) = _mm_in(h1, wb_in_g, 4, "in_proj_b", comm=_gather_first([b_w_out[0].astype(BF16)]))
    (o_b, ybr_b, states), (wb_out_g,) = _hgrn_fwd(proj_b, lbj, b_gn_gain, nb, t_seq,
                                                  comm=_gather_second([wb_out_half]))
    wb_out = wb_out_g.reshape(di, d)
    yout_b, dx2, loss_part, d_final_gain = _out_proj_loss(ybr_b, wb_out, x1, mod1, final_gain.reshape(1, d), tgt, t_seq)

    rows_out = a_w_out.shape[1]
    dy_b, dgate1, dybr_b = _gate_dybr(dx2, yout_b, mod1, wb_out, t_seq, "dybr_b")
    rs_wb_out = _ReduceScatter(_mm_dw_out(ybr_b, dy_b, "dw_out_b").reshape(NDEV, rows_out, d), "b_w_out")
    (dproj_b, d_lb, d_gn), got = _hgrn_bwd(proj_b, o_b, dybr_b, states, lbj, b_gn_gain, nb, t_seq,
                                           comm=rs_wb_out.swap_core())
    rs_wb_out.after_core(got[0])
    dh1, got = _mm_din(dproj_b, wb_in_g, 4, "dh_b", comm=rs_wb_out.swap_chips())
    rs_wb_out.after_chips(got[0])
    dx1, dss1, dgain1 = _prenorm_bwd(dh1, x1, ng1, mod1, dx2, t_seq, "prenorm_bwd_b")
    rs_wb_in = _ReduceScatter(_mm_dw_in(h1_t, dproj_b, ncb, 4, "dw_in_b"), "b_w_in")

    dy_a, dgate0, dybr_a = _gate_dybr(dx1, yout_a, mod0, wa_out, t_seq, "dybr_a")
    g_wa_out, got = _mm_dw_out(ybr_a, dy_a, "dw_out_a", comm=rs_wb_in.swap_core())
    rs_wb_in.after_core(got[0])
    rs_wa_out = _ReduceScatter(g_wa_out.reshape(NDEV, rows_out, d), "a_w_out")
    (dproj_a, d_lng, d_lnb, d_ws, d_bs_t), got = _a_mid_bwd(
        proj_a, dybr_a, a_ln_gain, a_ln_bias, a_w_s[0], bs_t, t_seq,
        comm=_join(rs_wb_in.swap_chips(), rs_wa_out.swap_core()))
    rs_wb_in.after_chips(got[0])
    rs_wa_out.after_core(got[1])
    part = dict(a_ln_gain=d_lng, a_ln_bias=d_lnb, a_w_s=d_ws[None], a_b_s=d_bs_t[:, :SG_GROUPS].T[None],
                b_lower_bounds=jnp.concatenate([-d_lb, d_lb], axis=0), b_gn_gain=d_gn)
    early_pack = _pack([part[k].reshape(w[k].shape) for k in _EARLY])
    g_wa_in, got = _mm_dw_in(h0_t, dproj_a, wa_in_g.shape[2], 1, "dw_in_a",
                             comm=_join(rs_wa_out.swap_chips(), _gather_first([early_pack])))
    rs_wa_out.after_chips(got[0])
    rs_wa_in = _ReduceScatter(g_wa_in, "a_w_in")
    n_tiles = m // _din_tile(m)
    assert n_tiles >= 2
    first_tiles = max(1, (3 * n_tiles) // 8)
    dh0, got2 = _mm_din(dproj_a, [wa_in_g], 1, "dh_a_first", tiles=(0, first_tiles),
                        comm=_join(rs_wa_in.swap_core(), _gather_second([got[1]])))
    rs_wa_in.after_core(got2[0])
    early_all = got2[1]
    dh0, got = _mm_din(dproj_a, [wa_in_g], 1, "dh_a_rest", comm=rs_wa_in.swap_chips(),
                       tiles=(first_tiles, n_tiles - first_tiles), prev=dh0)
    rs_wa_in.after_chips(got[0])
    dx0, dss0, dgain0 = _prenorm_bwd(dh0, xf, ng0, mod0, dx1, t_seq, "prenorm_bwd_a")
    grad_x = dx0.reshape(nb, t_seq, d)

    dmod = jnp.stack([jnp.concatenate([dss0, dgate0], axis=1), jnp.concatenate([dss1, dgate1], axis=1)])
    late_like = [norm_gain, final_gain, loss_part.reshape(1)]
    late_pack = _pack([jnp.concatenate([dgain0, dgain1], axis=0), d_final_gain[0], loss_part.reshape(1)])
    dmod_all, late_all = _all_gather([dmod.reshape(2, nb, 3 * d), late_pack], "gather_tail")
    dmod_all = dmod_all.transpose(1, 0, 2, 3).reshape(2, NDEV * nb, 3 * d)
    dmod_cols = lax.dynamic_slice_in_dim(dmod_all, me * ncol_ada, ncol_ada, axis=2)
    g_w_ada, g_b_ada = _ada_bwd(c_all, dmod_cols, dmod_all)

    res = {}
    early_like = [w[k] for k in _EARLY]
    dev_order = jnp.arange(NDEV, dtype=jnp.int32)
    sm = _adamw_blocks([early_all] * NDEV, dev_order, _pack(early_like), _pack([mo[k] for k in _EARLY]),
                       _pack([vo[k] for k in _EARLY]), "adamw_small_early")
    sm = [dict(zip(_EARLY, _unpack(buf, early_like))) for buf in sm]
    for k in _EARLY:
        res[k] = tuple(s[k] for s in sm)
    zero = jnp.zeros((1,), F32)
    sm = _adamw_blocks([late_all] * NDEV, dev_order, _pack([norm_gain, final_gain, zero]),
                       _pack([mo["norm_gain"], mo["final_gain"], zero]),
                       _pack([vo["norm_gain"], vo["final_gain"], zero]), "adamw_small_late")
    sm = [_unpack(buf, late_like) for buf in sm]
    res["norm_gain"] = tuple(s[0] for s in sm)
    res["final_gain"] = tuple(s[1] for s in sm)
    loss = sm[0][2][0]
    rb = _adamw([g_b_ada], b_ada, mo["b_ada"], vo["b_ada"], "adamw_b_ada")
    res["b_ada"] = tuple(rb)
    sh = w_ada.shape
    ra = _adamw([g_w_ada.reshape(sh[0] * sh[1], sh[2])], w_ada.reshape(sh[0] * sh[1], sh[2]),
                mo["w_ada"].reshape(sh[0] * sh[1], sh[2]), vo["w_ada"].reshape(sh[0] * sh[1], sh[2]), "adamw_w_ada")
    res["w_ada"] = tuple(z.reshape(sh) for z in ra)

    for k, rs in (("b_w_out", rs_wb_out), ("b_w_in", rs_wb_in), ("a_w_out", rs_wa_out), ("a_w_in", rs_wa_in)):
        res[k] = tuple(z[None] for z in _adamw_blocks(rs.parts, rs.idx, w[k][0], mo[k][0], vo[k][0], "adamw_" + k))

    order = ["norm_gain", "w_ada", "b_ada", "a_w_in", "a_ln_gain", "a_ln_bias", "a_w_s", "a_b_s", "a_w_out",
             "b_w_in", "b_lower_bounds", "b_gn_gain", "b_w_out", "final_gain"]
    return (loss, grad_x, *[res[k][0] for k in order], *[res[k][1] for k in order],
            *[res[k][2] for k in order], *[res[k][3] for k in order])
```

```python
import functools
import math

import jax
import jax.numpy as jnp
from jax import lax
from jax.experimental import pallas as pl
from jax.experimental.pallas import tpu as pltpu

F32 = jnp.float32
BF16 = jnp.bfloat16
MESH = pl.DeviceIdType.MESH
NDEV = 8
EPS = 1e-6
CHUNK = 64
SG_BLOCK = 128
SG_GROUPS = 8
HEAD_DIM = 128
CUM_ROWS = 256
ADAM_LR, ADAM_B1, ADAM_B2, ADAM_EPS, ADAM_WD, ADAM_STEP = 0.001, 0.9, 0.999, 1e-08, 0.01, 10
VMEM_LIMIT = 56 * 1024 * 1024
ANY = pl.BlockSpec(memory_space=pl.ANY)


class _Hosted:
    def __init__(self, arrays, out_shapes, nsem, start, finish, aliases=None):
        self.arrays, self.out_shapes, self.nsem = list(arrays), list(out_shapes), nsem
        self.start, self.finish = start, finish
        self.aliases = dict(aliases or {})


def _join(*comms):
    arrays, outs, aliases, offs, nsem = [], [], {}, [], 0
    for cm in comms:
        offs.append((len(arrays), len(outs), nsem))
        for i, o in cm.aliases.items():
            aliases[len(arrays) + i] = len(outs) + o
        arrays += cm.arrays
        outs += cm.out_shapes
        nsem += cm.nsem

    def run(which):
        def f(ins, outs_, ss, rs, base):
            for cm, (ia, io, isem) in zip(comms, offs):
                getattr(cm, which)(ins[ia:ia + len(cm.arrays)], outs_[io:io + len(cm.out_shapes)], ss, rs, base + isem)
        return f

    return _Hosted(arrays, outs, nsem, run("start"), run("finish"), aliases)


def _pc(body, *, name, out_shape, grid=None, in_specs=None, out_specs=None, scratch=(), sem=None,
        grid_spec=None, comm=None, aliases=None):
    cp = dict(vmem_limit_bytes=VMEM_LIMIT)
    aliases = dict(aliases or {})
    if comm is None:
        if sem is not None:
            cp["dimension_semantics"] = sem
        kw = {"input_output_aliases": aliases}
        if grid_spec is not None:
            kw["grid_spec"] = grid_spec
        else:
            if grid is not None:
                kw["grid"] = grid
            if in_specs is not None:
                kw["in_specs"] = in_specs
            if out_specs is not None:
                kw["out_specs"] = out_specs
            kw["scratch_shapes"] = list(scratch)
        return pl.pallas_call(functools.partial(body), name=name, out_shape=out_shape,
                              compiler_params=pltpu.CompilerParams(**cp), **kw)

    single = not isinstance(out_shape, (list, tuple))
    outs_list = [out_shape] if single else list(out_shape)
    ospecs = [out_specs] if single else list(out_specs)
    n_in, n_out, n_ci, n_co, n_scr = len(in_specs), len(outs_list), len(comm.arrays), len(comm.out_shapes), len(scratch)
    cp["dimension_semantics"] = ("arbitrary",) * len(grid)

    def hosted(*refs):
        cin, hin = refs[:n_in], refs[n_in:n_in + n_ci]
        cout = refs[n_in + n_ci:n_in + n_ci + n_out]
        hout = refs[n_in + n_ci + n_out:n_in + n_ci + n_out + n_co]
        scr = refs[n_in + n_ci + n_out + n_co:n_in + n_ci + n_out + n_co + n_scr]
        ssem, rsem = refs[-2], refs[-1]
        first = functools.reduce(lambda p, q: p & q, [pl.program_id(a) == 0 for a in range(len(grid))])
        last = functools.reduce(lambda p, q: p & q, [pl.program_id(a) == grid[a] - 1 for a in range(len(grid))])

        @pl.when(first)
        def _():
            comm.start(hin, hout, ssem, rsem, 0)

        body(*cin, *cout, *scr)

        @pl.when(last)
        def _():
            comm.finish(hin, hout, ssem, rsem, 0)

    call = pl.pallas_call(
        hosted, name=name, grid=grid, in_specs=list(in_specs) + [ANY] * n_ci, out_specs=ospecs + [ANY] * n_co,
        out_shape=outs_list + comm.out_shapes,
        scratch_shapes=list(scratch) + [pltpu.SemaphoreType.DMA((comm.nsem,)), pltpu.SemaphoreType.DMA((comm.nsem,))],
        input_output_aliases={**aliases, **{n_in + i: n_out + o for i, o in comm.aliases.items()}},
        compiler_params=pltpu.CompilerParams(**cp))

    def run(*args):
        res = call(*args, *comm.arrays)
        comp = res[:n_out]
        return (comp[0] if single else comp), list(res[n_out:])

    return run


def _tile(n, pref):
    return pref if n % pref == 0 else n


def _sigmoid(x):
    return 1.0 / (1.0 + jnp.exp(-x))


def _gelu(x):
    c = math.sqrt(2.0 / math.pi)
    return 0.5 * x * (1.0 + jnp.tanh(c * (x + 0.044715 * (x * x * x))))


def _gelu_and_grad(x):
    c = math.sqrt(2.0 / math.pi)
    x2 = x * x
    t = jnp.tanh(c * (x + 0.044715 * (x2 * x)))
    half = 0.5 * (1.0 + t)
    return x * half, half + (0.5 * x) * (1.0 - t * t) * (c + (3.0 * 0.044715 * c) * x2)


def _dot(a, b):
    return jnp.dot(a, b, preferred_element_type=F32)


def _dot_nt(a, b):
    return lax.dot_general(a, b, (((1,), (1,)), ((), ())), preferred_element_type=F32)


def _dot_tn(a, b):
    return lax.dot_general(a, b, (((0,), (0,)), ((), ())), preferred_element_type=F32)


def _tri_mask(n, reverse):
    r = lax.broadcasted_iota(jnp.int32, (n, n), 0)
    c = lax.broadcasted_iota(jnp.int32, (n, n), 1)
    same = (r // CHUNK) == (c // CHUNK)
    tri = (c >= r) if reverse else (c <= r)
    return jnp.where(same & tri, 1.0, 0.0).astype(BF16)


def _tri_apply(tri, x):
    hi = x.astype(BF16)
    r1 = x - hi.astype(F32)
    mid = r1.astype(BF16)
    lo = (r1 - mid.astype(F32)).astype(BF16)
    return _dot(tri, hi) + (_dot(tri, mid) + _dot(tri, lo))


def _all_gather(arrs, name):
    n = len(arrs)

    def body(*refs):
        ins, outs = refs[:n], refs[n:2 * n]
        send_sems, recv_sems, local_sems = refs[2 * n:]
        x, y, c = lax.axis_index("x"), lax.axis_index("y"), lax.axis_index("c")
        me, sibling = (x, y, c), (x, y, 1 - c)
        chips = [(1 - x, y), (x, 1 - y), (1 - x, 1 - y)]

        def blk(a, p):
            return outs[a].at[4 * p[0] + 2 * p[1] + p[2]]

        def copy(a, k, block, to, src=None):
            return pltpu.make_async_remote_copy(
                src_ref=blk(a, block) if src is None else src, dst_ref=blk(a, block),
                send_sem=send_sems.at[7 * a + k], recv_sem=recv_sems.at[7 * a + k],
                device_id=to, device_id_type=MESH)

        mine = [pltpu.make_async_copy(ins[a], blk(a, me), local_sems.at[a]) for a in range(n)]
        for m in mine:
            m.start()
        first = []
        for a in range(n):
            first.append(copy(a, 0, me, sibling, src=ins[a]))
            for j, chip in enumerate(chips):
                first.append(copy(a, 1 + j, me, (*chip, c), src=ins[a]))
        for cp in first:
            cp.start()
        passed = []
        for j, chip in enumerate(chips):
            for a in range(n):
                copy(a, 1 + j, (*chip, c), me).wait_recv()
                p = copy(a, 4 + j, (*chip, c), sibling)
                p.start()
                passed.append(p)
        for a in range(n):
            copy(a, 0, sibling, me).wait_recv()
            for j, chip in enumerate(chips):
                copy(a, 4 + j, (*chip, 1 - c), me).wait_recv()
        for cp in first + passed:
            cp.wait_send()
        for m in mine:
            m.wait()

    out_shape = [jax.ShapeDtypeStruct((NDEV,) + a.shape, a.dtype) for a in arrs]
    return _pc(body, name=name, out_shape=out_shape, in_specs=[ANY] * n, out_specs=[ANY] * n,
               scratch=[pltpu.SemaphoreType.DMA((7 * n,)), pltpu.SemaphoreType.DMA((7 * n,)),
                        pltpu.SemaphoreType.DMA((n,))])(*arrs)


def _gather_first(arrs):
    n = len(arrs)

    def parts(ins, outs, ss, rs, base):
        x, y, c = lax.axis_index("x"), lax.axis_index("y"), lax.axis_index("c")
        me, sibling = (x, y, c), (x, y, 1 - c)
        chips = [(1 - x, y), (x, 1 - y), (1 - x, 1 - y)]

        def blk(a, p):
            return outs[a].at[4 * p[0] + 2 * p[1] + p[2]]

        def copy(a, k, block, to):
            return pltpu.make_async_remote_copy(
                src_ref=ins[a], dst_ref=blk(a, block), send_sem=ss.at[base + 4 * a + k],
                recv_sem=rs.at[base + 4 * a + k], device_id=to, device_id_type=MESH)

        local = [pltpu.make_async_copy(ins[a], blk(a, me), ss.at[base + 4 * n + a]) for a in range(n)]
        sends, recvs = [], []
        for a in range(n):
            sends.append(copy(a, 0, me, sibling))
            recvs.append(copy(a, 0, sibling, me))
            for j, chip in enumerate(chips):
                sends.append(copy(a, 1 + j, me, (*chip, c)))
                recvs.append(copy(a, 1 + j, (*chip, c), me))
        return local, sends, recvs

    def start(ins, outs, ss, rs, base):
        local, sends, _ = parts(ins, outs, ss, rs, base)
        for cp in local + sends:
            cp.start()

    def finish(ins, outs, ss, rs, base):
        local, sends, recvs = parts(ins, outs, ss, rs, base)
        for cp in recvs:
            cp.wait_recv()
        for cp in sends:
            cp.wait_send()
        for cp in local:
            cp.wait()

    return _Hosted(arrs, [jax.ShapeDtypeStruct((NDEV,) + a.shape, a.dtype) for a in arrs], 5 * n, start, finish)


def _gather_second(bufs):
    n = len(bufs)

    def parts(ins, outs, ss, rs, base):
        x, y, c = lax.axis_index("x"), lax.axis_index("y"), lax.axis_index("c")
        sibling = (x, y, 1 - c)
        chips = [(1 - x, y), (x, 1 - y), (1 - x, 1 - y)]
        sends, recvs = [], []
        for a in range(n):
            for j, chip in enumerate(chips):
                mine = 4 * chip[0] + 2 * chip[1] + c
                theirs = 4 * chip[0] + 2 * chip[1] + (1 - c)
                sends.append(pltpu.make_async_remote_copy(
                    src_ref=ins[a].at[mine], dst_ref=outs[a].at[mine], send_sem=ss.at[base + 3 * a + j],
                    recv_sem=rs.at[base + 3 * a + j], device_id=sibling, device_id_type=MESH))
                recvs.append(pltpu.make_async_remote_copy(
                    src_ref=ins[a].at[theirs], dst_ref=outs[a].at[theirs], send_sem=ss.at[base + 3 * a + j],
                    recv_sem=rs.at[base + 3 * a + j], device_id=sibling, device_id_type=MESH))
        return sends, recvs

    def start(ins, outs, ss, rs, base):
        for cp in parts(ins, outs, ss, rs, base)[0]:
            cp.start()

    def finish(ins, outs, ss, rs, base):
        sends, recvs = parts(ins, outs, ss, rs, base)
        for cp in recvs:
            cp.wait_recv()
        for cp in sends:
            cp.wait_send()

    return _Hosted(bufs, [jax.ShapeDtypeStruct(b.shape, b.dtype) for b in bufs], 3 * n, start, finish,
                   aliases={a: a for a in range(n)})


def _swap(src, nblk, ids_fn, partner_fn):
    def copies(ins, outs, ss, rs, base):
        x, y, c = lax.axis_index("x"), lax.axis_index("y"), lax.axis_index("c")
        ids = ids_fn(x, y, c)
        partner = partner_fn(x, y, c)
        return [pltpu.make_async_remote_copy(
            src_ref=ins[0].at[ids[k]], dst_ref=outs[0].at[k], send_sem=ss.at[base + k], recv_sem=rs.at[base + k],
            device_id=partner, device_id_type=MESH) for k in range(nblk)]

    def start(ins, outs, ss, rs, base):
        for cp in copies(ins, outs, ss, rs, base):
            cp.start()

    def finish(ins, outs, ss, rs, base):
        for cp in copies(ins, outs, ss, rs, base):
            cp.wait()

    return _Hosted([src], [jax.ShapeDtypeStruct((nblk,) + src.shape[1:], src.dtype)], nblk, start, finish)


def _blocking(comm, name):
    n_i, n_o = len(comm.arrays), len(comm.out_shapes)

    def body(*refs):
        ins, outs = refs[:n_i], refs[n_i:n_i + n_o]
        comm.start(ins, outs, refs[-2], refs[-1], 0)
        comm.finish(ins, outs, refs[-2], refs[-1], 0)

    return pl.pallas_call(
        body, name=name, out_shape=comm.out_shapes, in_specs=[ANY] * n_i, out_specs=[ANY] * n_o,
        scratch_shapes=[pltpu.SemaphoreType.DMA((comm.nsem,)), pltpu.SemaphoreType.DMA((comm.nsem,))],
        input_output_aliases=comm.aliases)(*comm.arrays)


def _swap_chips(send):
    def copies(ins, outs, ss, rs, base):
        x, y, c = lax.axis_index("x"), lax.axis_index("y"), lax.axis_index("c")
        chips = [(1 - x, y), (x, 1 - y), (1 - x, 1 - y)]
        return [pltpu.make_async_remote_copy(
            src_ref=ins[0].at[j], dst_ref=outs[0].at[j], send_sem=ss.at[base + j], recv_sem=rs.at[base + j],
            device_id=(*chip, c), device_id_type=MESH) for j, chip in enumerate(chips)]

    def start(ins, outs, ss, rs, base):
        for cp in copies(ins, outs, ss, rs, base):
            cp.start()

    def finish(ins, outs, ss, rs, base):
        for cp in copies(ins, outs, ss, rs, base):
            cp.wait()

    return _Hosted([send], [jax.ShapeDtypeStruct(send.shape, send.dtype)], 3, start, finish)


def _add_send(a, b, idx, ns, name):
    _, r, c = a.shape
    tr = _tile(r, 256)

    def body(idx_ref, a_ref, b_ref, send_ref):
        send_ref[...] = (a_ref[...] + b_ref[...]).astype(BF16)

    def sel(off):
        return pl.BlockSpec((None, tr, c), lambda k, i, s: (s[off + k], i, 0))

    gs = pltpu.PrefetchScalarGridSpec(num_scalar_prefetch=1, grid=(ns, r // tr), in_specs=[sel(0), sel(ns)],
                                      out_specs=pl.BlockSpec((None, tr, c), lambda k, i, s: (k, i, 0)))
    return _pc(body, name=name, grid_spec=gs, sem=("arbitrary", "arbitrary"),
               out_shape=jax.ShapeDtypeStruct((ns, r, c), BF16))(idx, a, b)


class _ReduceScatter:
    def __init__(self, g, tag):
        self.g, self.tag = g, tag

    def swap_core(self):
        return _swap(self.g, 4, lambda x, y, c: [1 - c, 3 - c, 5 - c, 7 - c], lambda x, y, c: (x, y, 1 - c))

    def after_core(self, recv):
        x, y, c = lax.axis_index("x"), lax.axis_index("y"), lax.axis_index("c")
        chips = [(1 - x, y), (x, 1 - y), (1 - x, 1 - y)]
        idx = jnp.stack([4 * p + 2 * q + c for p, q in chips] + [2 * p + q for p, q in chips]).astype(jnp.int32)
        self.send = _add_send(self.g, recv, idx, 3, "rs_add_" + self.tag)
        self.recv_core = recv
        zero = jnp.zeros((), jnp.int32)
        self.idx = jnp.stack([4 * x + 2 * y + c, 2 * x + y, zero, zero + 1, zero + 2]).astype(jnp.int32)

    def swap_chips(self):
        return _swap_chips(self.send)

    def after_chips(self, recv):
        self.parts = [self.g, self.recv_core, recv, recv, recv]


def _ada_fwd(c_all, w_ada, b_cols, b_lb):
    nl, d, ncol = w_ada.shape
    nseq = c_all.shape[0]
    di = b_lb.shape[1]

    def body(c_ref, w_ref, b_ref, lb_ref, mod_ref, lbj_ref):
        cv = c_ref[...]
        cact = (cv * _sigmoid(cv)).astype(BF16)
        for l in range(nl):
            mod_ref[l] = _dot(cact, w_ref[l].astype(BF16)) + b_ref[l]
        b0, b1 = lb_ref[0:1, :], lb_ref[1:2, :]
        mx = jnp.maximum(b0, b1)
        e0, e1 = jnp.exp(b0 - mx), jnp.exp(b1 - mx)
        s = e0 + e1
        p0, p1 = e0 / s, e1 / s
        lbj_ref[0:1, :] = (p0 + p1) - p0
        lbj_ref[1:2, :] = p0 * p1

    return _pc(body, name="ada_fwd",
               out_shape=[jax.ShapeDtypeStruct((nl, nseq, ncol), F32), jax.ShapeDtypeStruct((2, di), F32)]
               )(c_all, w_ada, b_cols, b_lb)


def _ada_bwd(c_all, dmod_cols, dmod_full):
    nl, nseq, ncol = dmod_cols.shape
    d = c_all.shape[1]
    d3 = dmod_full.shape[2]

    def body(c_ref, dc_ref, df_ref, gw_ref, gb_ref):
        cv = c_ref[...]
        cact = (cv * _sigmoid(cv)).astype(BF16)
        for l in range(nl):
            gw_ref[l] = _dot_tn(cact, dc_ref[l].astype(BF16))
            gb_ref[l:l + 1, :] = jnp.sum(df_ref[l], axis=0, keepdims=True)

    return _pc(body, name="ada_bwd",
               out_shape=[jax.ShapeDtypeStruct((nl, d, ncol), F32), jax.ShapeDtypeStruct((nl, d3), F32)]
               )(c_all, dmod_cols, dmod_full)


def _prenorm(x, gain, mod, t_seq, name, comm=None):
    m, d = x.shape
    tm = _tile(t_seq, 512)
    per = t_seq // tm

    def body(x_ref, g_ref, mod_ref, h_ref, ht_ref):
        xv = x_ref[...]
        rstd = lax.rsqrt(jnp.mean(xv * xv, axis=-1, keepdims=True) + EPS)
        r = xv * rstd * g_ref[...]
        h = r * (1.0 + mod_ref[0, 1:2, :]) + mod_ref[0, 0:1, :]
        h_ref[...] = h.astype(BF16)
        ht_ref[...] = h.T.astype(BF16)

    return _pc(body, name=name, out_shape=[jax.ShapeDtypeStruct((m, d), BF16), jax.ShapeDtypeStruct((d, m), BF16)],
               grid=(m // tm,),
               in_specs=[pl.BlockSpec((tm, d), lambda i: (i, 0)), pl.BlockSpec((1, d), lambda i: (0, 0)),
                         pl.BlockSpec((1, 3, d), lambda i: (i // per, 0, 0))],
               out_specs=[pl.BlockSpec((tm, d), lambda i: (i, 0)), pl.BlockSpec((d, tm), lambda i: (0, i))],
               sem=("parallel",), comm=comm)(x, gain, mod)


def _prenorm_bwd(dh, x, gain, mod, dxn, t_seq, name, comm=None):
    m, d = x.shape
    nb = m // t_seq
    tm = _tile(t_seq, 512)
    per = t_seq // tm

    def body(dh_ref, x_ref, g_ref, mod_ref, dxn_ref, dx_ref, dss_ref, dg_ref):
        i = pl.program_id(0)
        xv, dhv, g = x_ref[...], dh_ref[...], g_ref[...]
        rstd = lax.rsqrt(jnp.mean(xv * xv, axis=-1, keepdims=True) + EPS)
        xhat = xv * rstd
        dr = dhv * (1.0 + mod_ref[0, 1:2, :])
        dxhat = dr * g
        dx_ref[...] = dxn_ref[...] + rstd * (dxhat - xhat * jnp.mean(dxhat * xhat, axis=-1, keepdims=True))

        @pl.when(i % per == 0)
        def _():
            dss_ref[...] = jnp.zeros_like(dss_ref)

        @pl.when(i == 0)
        def _():
            dg_ref[...] = jnp.zeros_like(dg_ref)

        dss_ref[0, 0:1, :] += jnp.sum(dhv, axis=0, keepdims=True)
        dss_ref[0, 1:2, :] += jnp.sum(dhv * (xhat * g), axis=0, keepdims=True)
        dg_ref[...] += jnp.sum(dr * xhat, axis=0, keepdims=True)

    row = pl.BlockSpec((tm, d), lambda i: (i, 0))
    return _pc(body, name=name,
               out_shape=[jax.ShapeDtypeStruct((m, d), F32), jax.ShapeDtypeStruct((nb, 2, d), F32),
                          jax.ShapeDtypeStruct((1, d), F32)],
               grid=(m // tm,),
               in_specs=[row, row, pl.BlockSpec((1, d), lambda i: (0, 0)),
                         pl.BlockSpec((1, 3, d), lambda i: (i // per, 0, 0)), row],
               out_specs=[row, pl.BlockSpec((1, 2, d), lambda i: (i // per, 0, 0)),
                          pl.BlockSpec((1, d), lambda i: (0, 0))],
               sem=("arbitrary",), comm=comm)(dh, x, gain, mod, dxn)


def _in_proj_gather(h, w_own, extra, name):
    m, k = h.shape
    nc = w_own.shape[1]
    tm = _tile(m, 512)
    nt = m // tm
    n_ei, n_eo = len(extra.arrays), len(extra.out_shapes)
    n_own = 7

    def body(perm_ref, h_ref, w_ref, *rest):
        e_in, proj_ref, wg_ref = rest[:n_ei], rest[n_ei], rest[n_ei + 1]
        e_out = rest[n_ei + 2:n_ei + 2 + n_eo]
        wbuf, ssem, rsem, lsem = rest[n_ei + 2 + n_eo:]
        jj, i = pl.program_id(0), pl.program_id(1)
        x, y, c = lax.axis_index("x"), lax.axis_index("y"), lax.axis_index("c")
        sibling = (x, y, 1 - c)
        chips = [(1 - x, y), (x, 1 - y), (1 - x, 1 - y)]

        def blk(p, q, cc):
            return wg_ref.at[4 * p + 2 * q + cc]

        def rcopy(kk, src, dst, to):
            return pltpu.make_async_remote_copy(src_ref=src, dst_ref=dst, send_sem=ssem.at[kk], recv_sem=rsem.at[kk],
                                                device_id=to, device_id_type=MESH)

        def load(pair, slot):
            return pltpu.make_async_copy(wg_ref.at[pl.ds(2 * pair, 2)], wbuf.at[slot], lsem.at[slot])

        own_local = pltpu.make_async_copy(w_ref, blk(x, y, c), lsem.at[2])
        first = [rcopy(0, w_ref, blk(x, y, c), sibling)]
        first += [rcopy(1 + s, w_ref, blk(x, y, c), (*chip, c)) for s, chip in enumerate(chips)]
        passed = [rcopy(4 + s, blk(*chip, c), blk(*chip, c), sibling) for s, chip in enumerate(chips)]

        def landed(s):
            return rcopy(1 + s, w_ref, blk(*chips[s], c), (*chips[s], c))

        def handed(s):
            return rcopy(4 + s, blk(*chips[s], 1 - c), blk(*chips[s], 1 - c), sibling)

        @pl.when((jj == 0) & (i == 0))
        def _():
            own_local.start()
            for cp in first[:3]:
                cp.start()
            extra.start(e_in, e_out, ssem, rsem, n_own)
            own_local.wait()
            rcopy(0, w_ref, blk(x, y, 1 - c), sibling).wait_recv()
            ld = load(perm_ref[0], 0)
            ld.start()
            ld.wait()

        for s in range(1, 4):
            @pl.when((jj == s) & (i == 0))
            def _(s=s):
                load(perm_ref[s], s % 2).wait()

        hv = h_ref[...]
        slot = jj % 2
        proj_ref[:, :nc] = _dot(hv, wbuf[slot, 0])
        proj_ref[:, nc:] = _dot(hv, wbuf[slot, 1])

        @pl.when((jj == 0) & (i == nt - 1))
        def _():
            landed(0).wait_recv()
            landed(1).wait_recv()
            passed[0].start()
            passed[1].start()
            first[3].start()
            handed(0).wait_recv()
            load(perm_ref[1], 1).start()

        @pl.when((jj == 1) & (i == nt - 1))
        def _():
            handed(1).wait_recv()
            load(perm_ref[2], 0).start()

        @pl.when((jj == 2) & (i == nt - 1))
        def _():
            landed(2).wait_recv()
            passed[2].start()
            handed(2).wait_recv()
            load(perm_ref[3], 1).start()

        @pl.when((jj == 3) & (i == nt - 1))
        def _():
            for cp in first + passed:
                cp.wait_send()
            extra.finish(e_in, e_out, ssem, rsem, n_own)

    gs = pltpu.PrefetchScalarGridSpec(
        num_scalar_prefetch=1, grid=(4, nt),
        in_specs=[pl.BlockSpec((tm, k), lambda jj, i, p: (i, 0)), ANY] + [ANY] * n_ei,
        out_specs=[pl.BlockSpec((tm, 2 * nc), lambda jj, i, p: (i, p[jj])), ANY] + [ANY] * n_eo,
        scratch_shapes=[pltpu.VMEM((2, 2, k, nc), BF16), pltpu.SemaphoreType.DMA((n_own + extra.nsem,)),
                        pltpu.SemaphoreType.DMA((n_own + extra.nsem,)), pltpu.SemaphoreType.DMA((3,))])
    xi, yi = lax.axis_index("x"), lax.axis_index("y")
    perm = jnp.stack([2 * xi + yi, 2 * (1 - xi) + yi, 2 * xi + (1 - yi), 2 * (1 - xi) + (1 - yi)]).astype(jnp.int32)
    res = _pc(body, name=name, grid_spec=gs, sem=("arbitrary", "arbitrary"),
              aliases={3 + a: 2 + o for a, o in extra.aliases.items()},
              out_shape=[jax.ShapeDtypeStruct((m, NDEV * nc), F32), jax.ShapeDtypeStruct((NDEV, k, nc), BF16)]
              + extra.out_shapes)(perm, h, w_own, *extra.arrays)
    return res[0], res[1], list(res[2:])


def _mm_in(h, ws, sections, name, comm=None):
    m, k = h.shape
    nw = len(ws)
    widths = [w.shape[2] for w in ws]
    offs = [sum(widths[:a]) for a in range(nw)]
    nc = sum(widths)
    per = NDEV // sections if sections > 1 else NDEV
    tm = _tile(m, 512)
    assert per % 2 == 0

    def body(*refs):
        hv = refs[0][...]
        o_ref = refs[1 + nw]
        for b in range(2):
            for a in range(nw):
                lo = b * nc + offs[a]
                o_ref[:, lo:lo + widths[a]] = _dot(hv, refs[1 + a][b])

    w_specs = [pl.BlockSpec((2, k, wd), lambda j, i: (j, 0, 0)) for wd in widths]
    if sections > 1:
        out_shape = jax.ShapeDtypeStruct((sections, m, per * nc), F32)
        out_spec = pl.BlockSpec((None, tm, 2 * nc), lambda j, i: ((2 * j) // per, i, ((2 * j) % per) // 2))
    else:
        out_shape = jax.ShapeDtypeStruct((m, NDEV * nc), F32)
        out_spec = pl.BlockSpec((tm, 2 * nc), lambda j, i: (i, j))
    return _pc(body, name=name, out_shape=out_shape, grid=(NDEV // 2, m // tm),
               in_specs=[pl.BlockSpec((tm, k), lambda j, i: (i, 0))] + w_specs,
               out_specs=out_spec, sem=("parallel", "parallel"), comm=comm)(h, *ws)


def _din_tile(m):
    return 1024 if m % 1024 == 0 and m >= 2048 else _tile(m, 512)


def _mm_din(dproj, ws, sections, name, comm=None, tiles=None, prev=None):
    nw, k = len(ws), ws[0].shape[1]
    widths = [w.shape[2] for w in ws]
    offs = [sum(widths[:a]) for a in range(nw)]
    nc = sum(widths)
    m = dproj.shape[-2]
    tm = _din_tile(m)
    t0, nt = tiles if tiles is not None else (0, m // tm)
    per = NDEV // sections if sections > 1 else NDEV
    assert per % 2 == 0

    def body(*refs):
        d_ref, o_ref = refs[0], refs[-1]
        j = pl.program_id(1)
        acc = None
        for b in range(2):
            for a in range(nw):
                lo = b * nc + offs[a]
                term = _dot_nt(d_ref[:, lo:lo + widths[a]], refs[1 + a][b])
                acc = term if acc is None else acc + term

        @pl.when(j == 0)
        def _():
            o_ref[...] = acc

        @pl.when(j > 0)
        def _():
            o_ref[...] += acc

    if sections > 1:
        dspec = pl.BlockSpec((None, tm, 2 * nc), lambda i, j: ((2 * j) // per, i + t0, ((2 * j) % per) // 2))
    else:
        dspec = pl.BlockSpec((tm, 2 * nc), lambda i, j: (i + t0, j))
    in_specs = [dspec] + [pl.BlockSpec((2, k, wd), lambda i, j: (j, 0, 0)) for wd in widths]
    args = [dproj, *ws]
    if prev is not None:
        in_specs.append(ANY)
        args.append(prev)
    return _pc(body, name=name, out_shape=jax.ShapeDtypeStruct((m, k), F32), grid=(nt, NDEV // 2), in_specs=in_specs,
               out_specs=pl.BlockSpec((tm, k), lambda i, j: (i + t0, 0)), sem=("parallel", "arbitrary"),
               comm=comm, aliases={1 + nw: 0} if prev is not None else None)(*args)


def _mm_dw_in(ht, dproj, nc, sections, name, comm=None):
    k, m = ht.shape
    tk = 2048 if m % 2048 == 0 else _din_tile(m)
    per = NDEV // sections if sections > 1 else NDEV

    def body(h_ref, d_ref, o_ref):
        kk = pl.program_id(1)
        acc = _dot(h_ref[...], d_ref[...])

        @pl.when(kk == 0)
        def _():
            o_ref[...] = acc

        @pl.when(kk > 0)
        def _():
            o_ref[...] += acc

    if sections > 1:
        dspec = pl.BlockSpec((None, tk, nc), lambda j, i: (j // per, i, j % per))
    else:
        dspec = pl.BlockSpec((tk, nc), lambda j, i: (i, j))
    return _pc(body, name=name, out_shape=jax.ShapeDtypeStruct((NDEV, k, nc), F32), grid=(NDEV, m // tk),
               in_specs=[pl.BlockSpec((k, tk), lambda j, i: (0, i)), dspec],
               out_specs=pl.BlockSpec((None, k, nc), lambda j, i: (j, 0, 0)),
               sem=("parallel", "arbitrary"), comm=comm)(ht, dproj)


def _out_proj(ybr, w_out, x, mod, t_seq, name, comm=None):
    m, di = ybr.shape
    d = w_out.shape[1]
    tm = _tile(t_seq, 512)
    per = t_seq // tm

    def body(y_ref, w_ref, x_ref, mod_ref, yo_ref, xn_ref):
        yo = _dot(y_ref[...], w_ref[...])
        yo_ref[...] = yo
        xn_ref[...] = x_ref[...] + mod_ref[0, 2:3, :] * yo

    row = pl.BlockSpec((tm, d), lambda i: (i, 0))
    return _pc(body, name=name,
               out_shape=[jax.ShapeDtypeStruct((m, d), F32), jax.ShapeDtypeStruct((m, d), F32)],
               grid=(m // tm,),
               in_specs=[pl.BlockSpec((tm, di), lambda i: (i, 0)), pl.BlockSpec((di, d), lambda i: (0, 0)), row,
                         pl.BlockSpec((1, 3, d), lambda i: (i // per, 0, 0))],
               out_specs=[row, row], sem=("parallel",), comm=comm)(ybr, w_out, x, mod)


def _out_proj_loss(ybr, w_out, x, mod, gain, target, t_seq):
    m, di = ybr.shape
    d = w_out.shape[1]
    tm = _tile(t_seq, 512)
    per = t_seq // tm

    def body(y_ref, w_ref, x_ref, mod_ref, g_ref, t_ref, yo_ref, dx_ref, loss_ref, dg_ref):
        i = pl.program_id(0)
        yo = _dot(y_ref[...], w_ref[...])
        yo_ref[...] = yo
        xv = x_ref[...] + mod_ref[0, 2:3, :] * yo
        g = g_ref[...]
        rstd = lax.rsqrt(jnp.mean(xv * xv, axis=-1, keepdims=True) + EPS)
        xhat = xv * rstd
        err = xhat * g - t_ref[...]
        dy = err * (1.0 / d)
        dxhat = dy * g
        dx_ref[...] = rstd * (dxhat - xhat * jnp.mean(dxhat * xhat, axis=-1, keepdims=True))

        @pl.when(i == 0)
        def _():
            loss_ref[...] = jnp.zeros_like(loss_ref)
            dg_ref[...] = jnp.zeros_like(dg_ref)

        loss_ref[...] += 0.5 * jnp.sum(jnp.mean(err * err, axis=-1, keepdims=True), axis=0, keepdims=True)
        dg_ref[...] += jnp.sum(dy * xhat, axis=0, keepdims=True)

    row = pl.BlockSpec((tm, d), lambda i: (i, 0))
    vec = pl.BlockSpec((1, d), lambda i: (0, 0))
    return _pc(body, name="out_proj_loss",
               out_shape=[jax.ShapeDtypeStruct((m, d), F32), jax.ShapeDtypeStruct((m, d), F32),
                          jax.ShapeDtypeStruct((1, 1), F32), jax.ShapeDtypeStruct((1, d), F32)],
               grid=(m // tm,),
               in_specs=[pl.BlockSpec((tm, di), lambda i: (i, 0)), pl.BlockSpec((di, d), lambda i: (0, 0)), row,
                         pl.BlockSpec((1, 3, d), lambda i: (i // per, 0, 0)), vec, row],
               out_specs=[row, row, pl.BlockSpec((1, 1), lambda i: (0, 0)), vec],
               sem=("arbitrary",))(ybr, w_out, x, mod, gain, target)


def _gate_dybr(dxn, yout, mod, w_out, t_seq, name):
    m, d = dxn.shape
    di = w_out.shape[0]
    nb = m // t_seq
    tm = _tile(t_seq, 512)
    per = t_seq // tm

    def body(dxn_ref, yo_ref, mod_ref, w_ref, dy_ref, dgate_ref, o_ref):
        i = pl.program_id(0)
        dv = dxn_ref[...]
        dy = (mod_ref[0, 2:3, :] * dv).astype(BF16)
        dy_ref[...] = dy
        o_ref[...] = _dot_nt(dy, w_ref[...])

        @pl.when(i % per == 0)
        def _():
            dgate_ref[...] = jnp.zeros_like(dgate_ref)

        dgate_ref[0] += jnp.sum(dv * yo_ref[...], axis=0, keepdims=True)

    row = pl.BlockSpec((tm, d), lambda i: (i, 0))
    return _pc(body, name=name,
               out_shape=[jax.ShapeDtypeStruct((m, d), BF16), jax.ShapeDtypeStruct((nb, 1, d), F32),
                          jax.ShapeDtypeStruct((m, di), F32)],
               grid=(m // tm,),
               in_specs=[row, row, pl.BlockSpec((1, 3, d), lambda i: (i // per, 0, 0)),
                         pl.BlockSpec((di, d), lambda i: (0, 0))],
               out_specs=[row, pl.BlockSpec((1, 1, d), lambda i: (i // per, 0, 0)),
                          pl.BlockSpec((tm, di), lambda i: (i, 0))],
               sem=("arbitrary",))(dxn, yout, mod, w_out)


def _mm_dw_out(ybr, dy, name, comm=None):
    m, di = ybr.shape
    d = dy.shape[1]
    tk = _tile(m, 512)
    tn = _tile(di, 1024)

    def body(y_ref, dy_ref, o_ref):
        kk = pl.program_id(1)
        acc = _dot_tn(y_ref[...], dy_ref[...])

        @pl.when(kk == 0)
        def _():
            o_ref[...] = acc

        @pl.when(kk > 0)
        def _():
            o_ref[...] += acc

    return _pc(body, name=name, out_shape=jax.ShapeDtypeStruct((di, d), F32), grid=(di // tn, m // tk),
               in_specs=[pl.BlockSpec((tk, tn), lambda n, k: (k, n)), pl.BlockSpec((tk, d), lambda n, k: (k, 0))],
               out_specs=pl.BlockSpec((tn, d), lambda n, k: (n, 0)), sem=("parallel", "arbitrary"),
               comm=comm)(ybr, dy)


def _sgu_mask():
    t = lax.broadcasted_iota(jnp.int32, (SG_BLOCK, SG_BLOCK), 0)
    s = lax.broadcasted_iota(jnp.int32, (SG_BLOCK, SG_BLOCK), 1)
    return (s // CHUNK) <= (t // CHUNK)


def _a_mid_fwd(proj, ln_g, ln_b, w_s, bs_t, t_seq, comm=None):
    m, n3 = proj.shape
    di = n3 // 3
    gd = di // SG_GROUPS
    r = _tile(t_seq, 256)
    nblk = r // SG_BLOCK

    def body(p_ref, lg_ref, lb_ref, ws_ref, bs_ref, ybr_ref, s_scr):
        v = _gelu(p_ref[:, di:2 * di])
        mu = jnp.mean(v, axis=-1, keepdims=True)
        vc = v - mu
        rstd = lax.rsqrt(jnp.mean(vc * vc, axis=-1, keepdims=True) + EPS)
        vb = (vc * rstd * lg_ref[...] + lb_ref[...]).astype(BF16)
        mask = _sgu_mask()
        for gi in range(SG_GROUPS):
            ws = jnp.where(mask, ws_ref[gi], 0.0).astype(BF16)
            bcol = bs_ref[:, gi:gi + 1]
            for b in range(nblk):
                rows = slice(b * SG_BLOCK, (b + 1) * SG_BLOCK)
                cols = slice(gi * gd, (gi + 1) * gd)
                s_scr[rows, cols] = _dot(ws, vb[rows, cols]) + bcol
        gg = p_ref[:, 2 * di:]
        ybr_ref[...] = (_gelu(p_ref[:, :di]) * s_scr[...] * (gg * _sigmoid(gg))).astype(BF16)

    vec = pl.BlockSpec((1, di), lambda i: (0, 0))
    return _pc(body, name="a_mid_fwd", out_shape=jax.ShapeDtypeStruct((m, di), BF16), grid=(m // r,),
               in_specs=[pl.BlockSpec((r, n3), lambda i: (i, 0)), vec, vec,
                         pl.BlockSpec((SG_GROUPS, SG_BLOCK, SG_BLOCK), lambda i: (0, 0, 0)),
                         pl.BlockSpec((SG_BLOCK, 128), lambda i: (0, 0))],
               out_specs=pl.BlockSpec((r, di), lambda i: (i, 0)),
               scratch=[pltpu.VMEM((r, di), F32)], sem=("parallel",), comm=comm)(proj, ln_g, ln_b, w_s, bs_t)


def _a_mid_bwd(proj, dybr, ln_g, ln_b, w_s, bs_t, t_seq, comm=None):
    m, n3 = proj.shape
    di = n3 // 3
    gd = di // SG_GROUPS
    r = _tile(t_seq, 256)
    nblk = r // SG_BLOCK

    def body(p_ref, dy_ref, lg_ref, lb_ref, ws_ref, bs_ref,
             dp_ref, dlg_ref, dlb_ref, dws_ref, dbs_ref, s_scr, dvl_scr):
        i = pl.program_id(0)

        @pl.when(i == 0)
        def _():
            dlg_ref[...] = jnp.zeros_like(dlg_ref)
            dlb_ref[...] = jnp.zeros_like(dlb_ref)
            dws_ref[...] = jnp.zeros_like(dws_ref)
            dbs_ref[...] = jnp.zeros_like(dbs_ref)

        v, dgelu_v = _gelu_and_grad(p_ref[:, di:2 * di])
        mu = jnp.mean(v, axis=-1, keepdims=True)
        vc = v - mu
        rstd = lax.rsqrt(jnp.mean(vc * vc, axis=-1, keepdims=True) + EPS)
        vhat = vc * rstd
        lg = lg_ref[...]
        vb = (vhat * lg + lb_ref[...]).astype(BF16)
        u, dgelu_u = _gelu_and_grad(p_ref[:, :di])
        gg = p_ref[:, 2 * di:]
        sg = _sigmoid(gg)
        dyv = dy_ref[...]
        dus = dyv * (gg * sg)
        dsb = (dus * u).astype(BF16)
        ds32 = dus * u
        mask = _sgu_mask()
        lane = lax.broadcasted_iota(jnp.int32, (SG_BLOCK, 128), 1)
        dbs_acc = jnp.zeros((SG_BLOCK, 128), F32)
        for gi in range(SG_GROUPS):
            ws = jnp.where(mask, ws_ref[gi], 0.0).astype(BF16)
            bcol = bs_ref[:, gi:gi + 1]
            cols = slice(gi * gd, (gi + 1) * gd)
            dws_acc = jnp.zeros((SG_BLOCK, SG_BLOCK), F32)
            dbs_col = jnp.zeros((SG_BLOCK, 1), F32)
            for b in range(nblk):
                rows = slice(b * SG_BLOCK, (b + 1) * SG_BLOCK)
                s_scr[rows, cols] = _dot(ws, vb[rows, cols]) + bcol
                dvl_scr[rows, cols] = _dot_tn(ws, dsb[rows, cols])
                dws_acc += _dot_nt(dsb[rows, cols], vb[rows, cols])
                dbs_col += jnp.sum(ds32[rows, cols], axis=-1, keepdims=True)
            dws_ref[gi] += jnp.where(mask, dws_acc, 0.0)
            dbs_acc += jnp.where(lane == gi, dbs_col, 0.0)
        dbs_ref[...] += dbs_acc
        s = s_scr[...]
        dp_ref[:, :di] = (dus * s * dgelu_u).astype(BF16)
        dp_ref[:, 2 * di:] = (dyv * u * s * (sg * (1.0 + gg * (1.0 - sg)))).astype(BF16)
        dvl = dvl_scr[...]
        dlg_ref[...] += jnp.sum(dvl * vhat, axis=0, keepdims=True)
        dlb_ref[...] += jnp.sum(dvl, axis=0, keepdims=True)
        dvh = dvl * lg
        dv = rstd * (dvh - jnp.mean(dvh, axis=-1, keepdims=True)
                     - vhat * jnp.mean(dvh * vhat, axis=-1, keepdims=True))
        dp_ref[:, di:2 * di] = (dv * dgelu_v).astype(BF16)

    vec = pl.BlockSpec((1, di), lambda i: (0, 0))
    wsb = pl.BlockSpec((SG_GROUPS, SG_BLOCK, SG_BLOCK), lambda i: (0, 0, 0))
    bsb = pl.BlockSpec((SG_BLOCK, 128), lambda i: (0, 0))
    return _pc(body, name="a_mid_bwd",
               out_shape=[jax.ShapeDtypeStruct((m, n3), BF16), jax.ShapeDtypeStruct((1, di), F32),
                          jax.ShapeDtypeStruct((1, di), F32),
                          jax.ShapeDtypeStruct((SG_GROUPS, SG_BLOCK, SG_BLOCK), F32),
                          jax.ShapeDtypeStruct((SG_BLOCK, 128), F32)],
               grid=(m // r,),
               in_specs=[pl.BlockSpec((r, n3), lambda i: (i, 0)), pl.BlockSpec((r, di), lambda i: (i, 0)),
                         vec, vec, wsb, bsb],
               out_specs=[pl.BlockSpec((r, n3), lambda i: (i, 0)), vec, vec, wsb, bsb],
               scratch=[pltpu.VMEM((r, di), F32), pltpu.VMEM((r, di), F32)],
               sem=("arbitrary",), comm=comm)(proj, dybr, ln_g, ln_b, w_s, bs_t)


def _hgrn_dims(t_seq, di):
    tr = _tile(t_seq, 256)
    hc = _tile(di, 1024)
    return tr, hc, hc // HEAD_DIM


def _hgrn_gates(f_ref, lb, a_scr, k_scr, tr):
    sig = _sigmoid(f_ref[...])
    fg = lb + (1.0 - lb) * sig
    k_scr[...] = 1.0 - fg
    logf = jnp.log(fg)
    g = min(CUM_ROWS, tr)
    tri = _tri_mask(g, reverse=False)
    for rg in range(tr // g):
        a_scr[rg * g:(rg + 1) * g, :] = _tri_apply(tri, logf[rg * g:(rg + 1) * g, :])
    return sig, fg


def _hgrn_fwd(proj, lbj, gn, nb, t_seq, comm=None):
    _, m, di = proj.shape
    tr, hc, hpg = _hgrn_dims(t_seq, di)
    nt, nhg, ncl = t_seq // tr, di // hc, tr // CHUNK
    nheads = di // HEAD_DIM

    def body(q_ref, f_ref, i_ref, g_ref, lb_ref, gn_ref, o_ref, ybr_ref, st_ref, st_scr, a_scr, k_scr):
        t = pl.program_id(2)

        @pl.when(t == 0)
        def _():
            st_scr[...] = jnp.zeros_like(st_scr)

        _hgrn_gates(f_ref, lb_ref[0:1, :], a_scr, k_scr, tr)
        gnv = gn_ref[...]
        rr = lax.broadcasted_iota(jnp.int32, (CHUNK, CHUNK), 0)
        cc = lax.broadcasted_iota(jnp.int32, (CHUNK, CHUNK), 1)
        causal = cc <= rr

        def chunk(n, carry):
            rows = pl.ds(pl.multiple_of(n * CHUNK, CHUNK), CHUNK)
            lanes = [slice(hd * HEAD_DIM, (hd + 1) * HEAD_DIM) for hd in range(hpg)]
            hs = []
            for hd, ls in enumerate(lanes):
                h = {}
                ah, kh = a_scr[rows, ls], k_scr[rows, ls]
                qp = q_ref[rows, ls]
                qh = qp * _sigmoid(qp)
                h["vb"] = i_ref[rows, ls].astype(BF16)
                aref, alast = ah[CHUNK // 2 - 1:CHUNK // 2, :], ah[CHUNK - 1:CHUNK, :]
                h["q_in"] = (qh * jnp.exp(ah - aref)).astype(BF16)
                h["k_in"] = (kh * jnp.exp(aref - ah)).astype(BF16)
                h["q_out"] = (qh * jnp.exp(ah)).astype(BF16)
                h["k_out"] = (kh * jnp.exp(alast - ah)).astype(BF16)
                h["dec"] = jnp.exp(alast)
                st = st_scr[hd]
                st_ref[n, hd] = st
                h["st"] = st
                hs.append(h)
            for h in hs:
                h["scores"] = _dot_nt(h["q_in"], h["k_in"])
                h["o_inter"] = _dot_nt(h["q_out"], h["st"].astype(BF16))
                h["st_mm"] = _dot_tn(h["vb"], h["k_out"])
            for h in hs:
                h["o"] = _dot(jnp.where(causal, h["scores"], 0.0).astype(BF16), h["vb"]) + h["o_inter"]
            for hd, (h, ls) in enumerate(zip(hs, lanes)):
                st_scr[hd] = h["st"] * h["dec"] + h["st_mm"]
                o = h["o"]
                o_ref[rows, ls] = o
                rstd = lax.rsqrt(jnp.mean(o * o, axis=-1, keepdims=True) + EPS)
                gg = g_ref[rows, ls]
                ybr_ref[rows, ls] = ((o * rstd * gnv) * (gg * _sigmoid(gg))).astype(BF16)
            return carry

        lax.fori_loop(0, ncl, chunk, 0)

    def sec(s):
        return pl.BlockSpec((None, tr, hc), lambda hg, b, t: (s, b * nt + t, hg))

    blk = pl.BlockSpec((tr, hc), lambda hg, b, t: (b * nt + t, hg))
    return _pc(body, name="hgrn_fwd",
               out_shape=[jax.ShapeDtypeStruct((m, di), F32), jax.ShapeDtypeStruct((m, di), BF16),
                          jax.ShapeDtypeStruct((m // CHUNK, nheads, HEAD_DIM, HEAD_DIM), F32)],
               grid=(nhg, nb, nt),
               in_specs=[sec(0), sec(1), sec(2), sec(3), pl.BlockSpec((2, hc), lambda hg, b, t: (0, hg)),
                         pl.BlockSpec((1, HEAD_DIM), lambda hg, b, t: (0, 0))],
               out_specs=[blk, blk, pl.BlockSpec((ncl, hpg, HEAD_DIM, HEAD_DIM),
                                                 lambda hg, b, t: (b * nt + t, hg, 0, 0))],
               scratch=[pltpu.VMEM((hpg, HEAD_DIM, HEAD_DIM), F32), pltpu.VMEM((tr, hc), F32),
                        pltpu.VMEM((tr, hc), F32)],
               sem=("parallel", "arbitrary", "arbitrary"), comm=comm)(proj, proj, proj, proj, lbj, gn)


def _hgrn_bwd(proj, o_all, dybr, states, lbj, gn, nb, t_seq, comm=None):
    _, m, di = proj.shape
    tr, hc, hpg = _hgrn_dims(t_seq, di)
    nt, nhg, ncl = t_seq // tr, di // hc, tr // CHUNK

    def body(q_ref, f_ref, i_ref, g_ref, o_ref, dy_ref, st_ref, lb_ref, gn_ref,
             dp_ref, dlb_ref, dgn_ref, dst_scr, a_scr, k_scr, da_scr, dk_scr):
        hg, b, t = pl.program_id(0), pl.program_id(1), pl.program_id(2)

        @pl.when(t == 0)
        def _():
            dst_scr[...] = jnp.zeros_like(dst_scr)

        @pl.when((b == 0) & (t == 0))
        def _():
            dlb_ref[...] = jnp.zeros_like(dlb_ref)

        @pl.when((hg == 0) & (b == 0) & (t == 0))
        def _():
            dgn_ref[...] = jnp.zeros_like(dgn_ref)

        lb = lb_ref[0:1, :]
        sig, fg = _hgrn_gates(f_ref, lb, a_scr, k_scr, tr)
        gnv = gn_ref[...]
        rr = lax.broadcasted_iota(jnp.int32, (CHUNK, CHUNK), 0)
        cc = lax.broadcasted_iota(jnp.int32, (CHUNK, CHUNK), 1)
        causal = cc <= rr
        rowi = lax.broadcasted_iota(jnp.int32, (CHUNK, HEAD_DIM), 0)

        def chunk(it, carry):
            n = ncl - 1 - it
            rows = pl.ds(pl.multiple_of(n * CHUNK, CHUNK), CHUNK)
            lanes = [slice(hd * HEAD_DIM, (hd + 1) * HEAD_DIM) for hd in range(hpg)]
            hs = []
            for hd, ls in enumerate(lanes):
                h = {}
                ah, kh = a_scr[rows, ls], k_scr[rows, ls]
                qp = q_ref[rows, ls]
                sq = _sigmoid(qp)
                qh = qp * sq
                h["dsilu_q"] = sq * (1.0 + qp * (1.0 - sq))
                h["vb"] = i_ref[rows, ls].astype(BF16)
                aref, alast = ah[CHUNK // 2 - 1:CHUNK // 2, :], ah[CHUNK - 1:CHUNK, :]
                h["e1"], h["e2"] = jnp.exp(ah - aref), jnp.exp(aref - ah)
                h["e3"], h["e4"] = jnp.exp(ah), jnp.exp(alast - ah)
                h["dec"] = jnp.exp(alast)
                h["q_in"], h["k_in"], h["q_out"], h["k_out"] = qh * h["e1"], kh * h["e2"], qh * h["e3"], kh * h["e4"]
                for nm in ("q_in", "k_in", "q_out", "k_out"):
                    h[nm + "_b"] = h[nm].astype(BF16)
                o = o_ref[rows, ls]
                rstd = lax.rsqrt(jnp.mean(o * o, axis=-1, keepdims=True) + EPS)
                ohat = o * rstd
                gg = g_ref[rows, ls]
                sg = _sigmoid(gg)
                dyv = dy_ref[rows, ls]
                d_on = dyv * (gg * sg)
                dp_ref[3, rows, ls] = (dyv * (ohat * gnv) * (sg * (1.0 + gg * (1.0 - sg)))).astype(BF16)
                h["dgn"] = jnp.sum(d_on * ohat, axis=0, keepdims=True)
                dohat = d_on * gnv
                do = rstd * (dohat - ohat * jnp.mean(dohat * ohat, axis=-1, keepdims=True))
                h["do_b"] = do.astype(BF16)
                h["st_prev"] = st_ref[n, hd]
                h["dst"] = dst_scr[hd]
                hs.append(h)
            for h in hs:
                dst_b = h["dst"].astype(BF16)
                h["scores"] = _dot_nt(h["q_in_b"], h["k_in_b"])
                h["dscores"] = _dot_nt(h["do_b"], h["vb"])
                h["dv_inter"] = _dot_nt(h["k_out_b"], dst_b)
                h["dq_out"] = _dot(h["do_b"], h["st_prev"].astype(BF16))
                h["dk_out"] = _dot(h["vb"], dst_b)
                h["dst_mm"] = _dot_tn(h["do_b"], h["q_out_b"])
            for h in hs:
                scores = jnp.where(causal, h["scores"], 0.0).astype(BF16)
                dscores = jnp.where(causal, h["dscores"], 0.0).astype(BF16)
                h["dv"] = _dot_tn(scores, h["do_b"]) + h["dv_inter"]
                h["dq_in"] = _dot(dscores, h["k_in_b"])
                h["dk_in"] = _dot_tn(dscores, h["q_in_b"])
            dgn = hs[0]["dgn"]
            for h in hs[1:]:
                dgn = dgn + h["dgn"]
            dgn_ref[...] += dgn
            for hd, (h, ls) in enumerate(zip(hs, lanes)):
                ddec = jnp.sum(h["dst"] * h["st_prev"], axis=0, keepdims=True)
                dst_scr[hd] = h["dst"] * h["dec"] + h["dst_mm"]
                dp_ref[2, rows, ls] = h["dv"].astype(BF16)
                dq = h["dq_in"] * h["e1"] + h["dq_out"] * h["e3"]
                dp_ref[0, rows, ls] = (dq * h["dsilu_q"]).astype(BF16)
                dk_scr[rows, ls] = h["dk_in"] * h["e2"] + h["dk_out"] * h["e4"]
                t_in = h["dq_in"] * h["q_in"] - h["dk_in"] * h["k_in"]
                t_out = h["dk_out"] * h["k_out"]
                da = t_in + h["dq_out"] * h["q_out"] - t_out
                da_ref_row = -jnp.sum(t_in, axis=0, keepdims=True)
                da_last_row = jnp.sum(t_out, axis=0, keepdims=True) + ddec * h["dec"]
                da = da + jnp.where(rowi == CHUNK // 2 - 1, da_ref_row, 0.0) \
                        + jnp.where(rowi == CHUNK - 1, da_last_row, 0.0)
                da_scr[rows, ls] = da
            return carry

        lax.fori_loop(0, ncl, chunk, 0)
        g = min(CUM_ROWS, tr)
        tri = _tri_mask(g, reverse=True)
        for rg in range(tr // g):
            rs = slice(rg * g, (rg + 1) * g)
            dlogf = _tri_apply(tri, da_scr[rs, :])
            df = dlogf / fg[rs, :] - dk_scr[rs, :]
            sgr = sig[rs, :]
            dp_ref[1, rs, :] = (df * (1.0 - lb) * (sgr * (1.0 - sgr))).astype(BF16)
            dlb_ref[...] += jnp.sum(df * (1.0 - sgr), axis=0, keepdims=True) * lb_ref[1:2, :]

    def sec(s):
        return pl.BlockSpec((None, tr, hc), lambda hg, b, t: (s, b * nt + (nt - 1 - t), hg))

    blk = pl.BlockSpec((tr, hc), lambda hg, b, t: (b * nt + (nt - 1 - t), hg))
    return _pc(body, name="hgrn_bwd",
               out_shape=[jax.ShapeDtypeStruct((4, m, di), BF16), jax.ShapeDtypeStruct((1, di), F32),
                          jax.ShapeDtypeStruct((1, HEAD_DIM), F32)],
               grid=(nhg, nb, nt),
               in_specs=[sec(0), sec(1), sec(2), sec(3), blk, blk,
                         pl.BlockSpec((ncl, hpg, HEAD_DIM, HEAD_DIM),
                                      lambda hg, b, t: (b * nt + (nt - 1 - t), hg, 0, 0)),
                         pl.BlockSpec((2, hc), lambda hg, b, t: (0, hg)),
                         pl.BlockSpec((1, HEAD_DIM), lambda hg, b, t: (0, 0))],
               out_specs=[pl.BlockSpec((4, tr, hc), lambda hg, b, t: (0, b * nt + (nt - 1 - t), hg)),
                          pl.BlockSpec((1, hc), lambda hg, b, t: (0, hg)),
                          pl.BlockSpec((1, HEAD_DIM), lambda hg, b, t: (0, 0))],
               scratch=[pltpu.VMEM((hpg, HEAD_DIM, HEAD_DIM), F32)] + [pltpu.VMEM((tr, hc), F32)] * 4,
               sem=("arbitrary", "arbitrary", "arbitrary"), comm=comm)(
                   proj, proj, proj, proj, o_all, dybr, states, lbj, gn)


def _adamw(parts, w, m, v, name, comm=None):
    r, c = w.shape
    tr = _tile(r, 256)
    npart = len(parts)
    c1 = 1.0 - ADAM_B1 ** ADAM_STEP
    c2 = 1.0 - ADAM_B2 ** ADAM_STEP

    def body(*refs):
        p_refs = refs[:npart]
        _adamw_math(p_refs, *refs[npart:], c1, c2)

    blk = pl.BlockSpec((tr, c), lambda i: (i, 0))
    return _pc(body, name=name, out_shape=[jax.ShapeDtypeStruct((r, c), F32)] * 4, grid=(r // tr,),
               in_specs=[blk] * (npart + 3), out_specs=[blk] * 4, sem=("parallel",), comm=comm)(*parts, w, m, v)


def _adamw_math(p_refs, w_ref, m_ref, v_ref, g_ref, d_ref, nm_ref, nv_ref, c1, c2):
    g = p_refs[0][...].astype(F32)
    for p in p_refs[1:]:
        g = g + p[...].astype(F32)
    nm = ADAM_B1 * m_ref[...] + (1.0 - ADAM_B1) * g
    nv = ADAM_B2 * v_ref[...] + (1.0 - ADAM_B2) * (g * g)
    g_ref[...] = g
    nm_ref[...] = nm
    nv_ref[...] = nv
    d_ref[...] = -ADAM_LR * ((nm / c1) / (jnp.sqrt(nv / c2) + ADAM_EPS) + ADAM_WD * w_ref[...])


def _adamw_blocks(parts, idx, w, m, v, name):
    r, c = w.shape
    tr = _tile(r, 256)
    npart = len(parts)
    c1 = 1.0 - ADAM_B1 ** ADAM_STEP
    c2 = 1.0 - ADAM_B2 ** ADAM_STEP

    def body(idx_ref, *refs):
        _adamw_math(refs[:npart], *refs[npart:], c1, c2)

    def sel(p):
        return pl.BlockSpec((None, tr, c), lambda i, s: (s[p], i, 0))

    blk = pl.BlockSpec((tr, c), lambda i, s: (i, 0))
    gs = pltpu.PrefetchScalarGridSpec(num_scalar_prefetch=1, grid=(r // tr,),
                                      in_specs=[sel(p) for p in range(npart)] + [blk] * 3, out_specs=[blk] * 4)
    return _pc(body, name=name, out_shape=[jax.ShapeDtypeStruct((r, c), F32)] * 4, grid_spec=gs,
               sem=("parallel",))(idx, *parts, w, m, v)


_EARLY = ["a_ln_gain", "a_ln_bias", "a_w_s", "a_b_s", "b_lower_bounds", "b_gn_gain"]


def _pack(arrs):
    flat = jnp.concatenate([a.reshape(-1) for a in arrs])
    rows = -(-flat.shape[0] // 1024) * 8
    return jnp.pad(flat, (0, rows * 128 - flat.shape[0])).reshape(rows, 128)


def _unpack(buf, like):
    flat = buf.reshape(-1)
    out, off = [], 0
    for a in like:
        out.append(flat[off:off + a.size].reshape(a.shape))
        off += a.size
    return out


def kernel(x, c, norm_gain, w_ada, b_ada, a_w_in, a_ln_gain, a_ln_bias, a_w_s, a_b_s, a_w_out, b_w_in, b_lower_bounds, b_gn_gain, b_w_out, final_gain, loss_target, m_norm_gain, m_w_ada, m_b_ada, m_a_w_in, m_a_ln_gain, m_a_ln_bias, m_a_w_s, m_a_b_s, m_a_w_out, m_b_w_in, m_b_lower_bounds, m_b_gn_gain, m_b_w_out, m_final_gain, v_norm_gain, v_w_ada, v_b_ada, v_a_w_in, v_a_ln_gain, v_a_ln_bias, v_a_w_s, v_a_b_s, v_a_w_out, v_b_w_in, v_b_lower_bounds, v_b_gn_gain, v_b_w_out, v_final_gain):
    w = dict(norm_gain=norm_gain, w_ada=w_ada, b_ada=b_ada, a_w_in=a_w_in, a_ln_gain=a_ln_gain,
             a_ln_bias=a_ln_bias, a_w_s=a_w_s, a_b_s=a_b_s, a_w_out=a_w_out, b_w_in=b_w_in,
             b_lower_bounds=b_lower_bounds, b_gn_gain=b_gn_gain, b_w_out=b_w_out, final_gain=final_gain)
    mo = dict(norm_gain=m_norm_gain, w_ada=m_w_ada, b_ada=m_b_ada, a_w_in=m_a_w_in, a_ln_gain=m_a_ln_gain,
              a_ln_bias=m_a_ln_bias, a_w_s=m_a_w_s, a_b_s=m_a_b_s, a_w_out=m_a_w_out, b_w_in=m_b_w_in,
              b_lower_bounds=m_b_lower_bounds, b_gn_gain=m_b_gn_gain, b_w_out=m_b_w_out, final_gain=m_final_gain)
    vo = dict(norm_gain=v_norm_gain, w_ada=v_w_ada, b_ada=v_b_ada, a_w_in=v_a_w_in, a_ln_gain=v_a_ln_gain,
              a_ln_bias=v_a_ln_bias, a_w_s=v_a_w_s, a_b_s=v_a_b_s, a_w_out=v_a_w_out, b_w_in=v_b_w_in,
              b_lower_bounds=v_b_lower_bounds, b_gn_gain=v_b_gn_gain, b_w_out=v_b_w_out, final_gain=v_final_gain)

    nb, t_seq, d = x.shape
    m = nb * t_seq
    ncol_ada = w_ada.shape[2]
    xi, yi, ci = lax.axis_index("x"), lax.axis_index("y"), lax.axis_index("c")
    me = 4 * xi + 2 * yi + ci

    c_g = _all_gather([c], "gather_c")[0]

    c_all = c_g.reshape(NDEV * nb, d)
    b_cols = lax.dynamic_slice(b_ada, (0, me * ncol_ada), (2, ncol_ada)).reshape(2, 1, ncol_ada)
    mod_part, lbj = _ada_fwd(c_all, w_ada, b_cols, b_lower_bounds)
    mod_all = _all_gather([mod_part], "gather_mod")[0]
    mod_mine = lax.dynamic_slice_in_dim(mod_all, me * nb, nb, axis=2)
    mod_mine = mod_mine.transpose(1, 2, 0, 3).reshape(2, nb, 3, d)
    mod0, mod1 = mod_mine[0], mod_mine[1]

    di = a_w_out.shape[1] * NDEV

    xf = x.reshape(m, d)
    tgt = loss_target.reshape(m, d)
    ng0, ng1 = norm_gain[0:1], norm_gain[1:2]
    ncb = b_w_in.shape[2]
    n_lo = max(128, (5 * ncb // 8) // 128 * 128)
    wb_lo, wb_hi = b_w_in[0][:, :n_lo].astype(BF16), b_w_in[0][:, n_lo:].astype(BF16)
    (h0, h0_t), (wa_out_half,) = _prenorm(xf, ng0, mod0, t_seq, "prenorm_a",
                                         comm=_gather_first([a_w_out[0].astype(BF16)]))
    proj_a, wa_in_g, (wa_out_g,) = _in_proj_gather(h0, a_w_in[0].astype(BF16), _gather_second([wa_out_half]),
                                                   "in_proj_a")
    bs_t = jnp.pad(a_b_s[0].T, ((0, 0), (0, 128 - SG_GROUPS)))
    ybr_a, (wb_lo_half,) = _a_mid_fwd(proj_a, a_ln_gain, a_ln_bias, a_w_s[0], bs_t, t_seq,
                                      comm=_gather_first([wb_lo]))
    wa_out = wa_out_g.reshape(di, d)
    (yout_a, x1), (wb_lo_g, wb_hi_half) = _out_proj(
        ybr_a, wa_out, xf, mod0, t_seq, "out_proj_a", comm=_join(_gather_second([wb_lo_half]), _gather_first([wb_hi])))
    (h1, h1_t), (wb_hi_g,) = _prenorm(x1, ng1, mod1, t_seq, "prenorm_b", comm=_gather_second([wb_hi_half]))
    wb_in_g = [wb_lo_g, wb_hi_g]
    proj_b, (wb_out_half,) = _mm_in(h1, wb_in_g, 4, "in_proj_b", comm=_gather_first([b_w_out[0].astype(BF16)]))
    (o_b, ybr_b, states), (wb_out_g,) = _hgrn_fwd(proj_b, lbj, b_gn_gain, nb, t_seq,
                                                  comm=_gather_second([wb_out_half]))
    wb_out = wb_out_g.reshape(di, d)
    yout_b, dx2, loss_part, d_final_gain = _out_proj_loss(ybr_b, wb_out, x1, mod1, final_gain.reshape(1, d), tgt, t_seq)

    rows_out = a_w_out.shape[1]
    dy_b, dgate1, dybr_b = _gate_dybr(dx2, yout_b, mod1, wb_out, t_seq, "dybr_b")
    rs_wb_out = _ReduceScatter(_mm_dw_out(ybr_b, dy_b, "dw_out_b").reshape(NDEV, rows_out, d), "b_w_out")
    (dproj_b, d_lb, d_gn), got = _hgrn_bwd(proj_b, o_b, dybr_b, states, lbj, b_gn_gain, nb, t_seq,
                                           comm=rs_wb_out.swap_core())
    rs_wb_out.after_core(got[0])
    dh1, got = _mm_din(dproj_b, wb_in_g, 4, "dh_b", comm=rs_wb_out.swap_chips())
    rs_wb_out.after_chips(got[0])
    dx1, dss1, dgain1 = _prenorm_bwd(dh1, x1, ng1, mod1, dx2, t_seq, "prenorm_bwd_b")
    rs_wb_in = _ReduceScatter(_mm_dw_in(h1_t, dproj_b, ncb, 4, "dw_in_b"), "b_w_in")

    dy_a, dgate0, dybr_a = _gate_dybr(dx1, yout_a, mod0, wa_out, t_seq, "dybr_a")
    g_wa_out, got = _mm_dw_out(ybr_a, dy_a, "dw_out_a", comm=rs_wb_in.swap_core())
    rs_wb_in.after_core(got[0])
    rs_wa_out = _ReduceScatter(g_wa_out.reshape(NDEV, rows_out, d), "a_w_out")
    (dproj_a, d_lng, d_lnb, d_ws, d_bs_t), got = _a_mid_bwd(
        proj_a, dybr_a, a_ln_gain, a_ln_bias, a_w_s[0], bs_t, t_seq,
        comm=_join(rs_wb_in.swap_chips(), rs_wa_out.swap_core()))
    rs_wb_in.after_chips(got[0])
    rs_wa_out.after_core(got[1])
    part = dict(a_ln_gain=d_lng, a_ln_bias=d_lnb, a_w_s=d_ws[None], a_b_s=d_bs_t[:, :SG_GROUPS].T[None],
                b_lower_bounds=jnp.concatenate([-d_lb, d_lb], axis=0), b_gn_gain=d_gn)
    early_pack = _pack([part[k].reshape(w[k].shape) for k in _EARLY])
    g_wa_in, got = _mm_dw_in(h0_t, dproj_a, wa_in_g.shape[2], 1, "dw_in_a",
                             comm=_join(rs_wa_out.swap_chips(), _gather_first([early_pack])))
    rs_wa_out.after_chips(got[0])
    rs_wa_in = _ReduceScatter(g_wa_in, "a_w_in")
    n_tiles = m // _din_tile(m)
    assert n_tiles >= 2
    first_tiles = max(1, (3 * n_tiles) // 8)
    dh0, got2 = _mm_din(dproj_a, [wa_in_g], 1, "dh_a_first", tiles=(0, first_tiles),
                        comm=_join(rs_wa_in.swap_core(), _gather_second([got[1]])))
    rs_wa_in.after_core(got2[0])
    early_all = got2[1]
    dh0, got = _mm_din(dproj_a, [wa_in_g], 1, "dh_a_rest", comm=rs_wa_in.swap_chips(),
                       tiles=(first_tiles, n_tiles - first_tiles), prev=dh0)
    rs_wa_in.after_chips(got[0])
    dx0, dss0, dgain0 = _prenorm_bwd(dh0, xf, ng0, mod0, dx1, t_seq, "prenorm_bwd_a")
    grad_x = dx0.reshape(nb, t_seq, d)

    dmod = jnp.stack([jnp.concatenate([dss0, dgate0], axis=1), jnp.concatenate([dss1, dgate1], axis=1)])
    late_like = [norm_gain, final_gain, loss_part.reshape(1)]
    late_pack = _pack([jnp.concatenate([dgain0, dgain1], axis=0), d_final_gain[0], loss_part.reshape(1)])
    dmod_all, late_all = _all_gather([dmod.reshape(2, nb, 3 * d), late_pack], "gather_tail")
    dmod_all = dmod_all.transpose(1, 0, 2, 3).reshape(2, NDEV * nb, 3 * d)
    dmod_cols = lax.dynamic_slice_in_dim(dmod_all, me * ncol_ada, ncol_ada, axis=2)
    g_w_ada, g_b_ada = _ada_bwd(c_all, dmod_cols, dmod_all)

    res = {}
    early_like = [w[k] for k in _EARLY]
    dev_order = jnp.arange(NDEV, dtype=jnp.int32)
    sm = _adamw_blocks([early_all] * NDEV, dev_order, _pack(early_like), _pack([mo[k] for k in _EARLY]),
                       _pack([vo[k] for k in _EARLY]), "adamw_small_early")
    sm = [dict(zip(_EARLY, _unpack(buf, early_like))) for buf in sm]
    for k in _EARLY:
        res[k] = tuple(s[k] for s in sm)
    zero = jnp.zeros((1,), F32)
    sm = _adamw_blocks([late_all] * NDEV, dev_order, _pack([norm_gain, final_gain, zero]),
                       _pack([mo["norm_gain"], mo["final_gain"], zero]),
                       _pack([vo["norm_gain"], vo["final_gain"], zero]), "adamw_small_late")
    sm = [_unpack(buf, late_like) for buf in sm]
    res["norm_gain"] = tuple(s[0] for s in sm)
    res["final_gain"] = tuple(s[1] for s in sm)
    loss = sm[0][2][0]
    rb = _adamw([g_b_ada], b_ada, mo["b_ada"], vo["b_ada"], "adamw_b_ada")
    res["b_ada"] = tuple(rb)
    sh = w_ada.shape
    ra = _adamw([g_w_ada.reshape(sh[0] * sh[1], sh[2])], w_ada.reshape(sh[0] * sh[1], sh[2]),
                mo["w_ada"].reshape(sh[0] * sh[1], sh[2]), vo["w_ada"].reshape(sh[0] * sh[1], sh[2]), "adamw_w_ada")
    res["w_ada"] = tuple(z.reshape(sh) for z in ra)

    for k, rs in (("b_w_out", rs_wb_out), ("b_w_in", rs_wb_in), ("a_w_out", rs_wa_out), ("a_w_in", rs_wa_in)):
        res[k] = tuple(z[None] for z in _adamw_blocks(rs.parts, rs.idx, w[k][0], mo[k][0], vo[k][0], "adamw_" + k))

    order = ["norm_gain", "w_ada", "b_ada", "a_w_in", "a_ln_gain", "a_ln_bias", "a_w_s", "a_b_s", "a_w_out",
             "b_w_in", "b_lower_bounds", "b_gn_gain", "b_w_out", "final_gain"]
    return (loss, grad_x, *[res[k][0] for k in order], *[res[k][1] for k in order],
            *[res[k][2] for k in order], *[res[k][3] for k in order])
```

```python
import functools
import math

import jax
import jax.numpy as jnp
from jax import lax
from jax.experimental import pallas as pl
from jax.experimental.pallas import tpu as pltpu

F32 = jnp.float32
BF16 = jnp.bfloat16
MESH = pl.DeviceIdType.MESH
NDEV = 8
EPS = 1e-6
CHUNK = 64
SG_BLOCK = 128
SG_GROUPS = 8
HEAD_DIM = 128
CUM_ROWS = 256
ADAM_LR, ADAM_B1, ADAM_B2, ADAM_EPS, ADAM_WD, ADAM_STEP = 0.001, 0.9, 0.999, 1e-08, 0.01, 10
VMEM_LIMIT = 56 * 1024 * 1024
ANY = pl.BlockSpec(memory_space=pl.ANY)


class _Hosted:
    def __init__(self, arrays, out_shapes, nsem, start, finish, aliases=None):
        self.arrays, self.out_shapes, self.nsem = list(arrays), list(out_shapes), nsem
        self.start, self.finish = start, finish
        self.aliases = dict(aliases or {})


def _join(*comms):
    arrays, outs, aliases, offs, nsem = [], [], {}, [], 0
    for cm in comms:
        offs.append((len(arrays), len(outs), nsem))
        for i, o in cm.aliases.items():
            aliases[len(arrays) + i] = len(outs) + o
        arrays += cm.arrays
        outs += cm.out_shapes
        nsem += cm.nsem

    def run(which):
        def f(ins, outs_, ss, rs, base):
            for cm, (ia, io, isem) in zip(comms, offs):
                getattr(cm, which)(ins[ia:ia + len(cm.arrays)], outs_[io:io + len(cm.out_shapes)], ss, rs, base + isem)
        return f

    return _Hosted(arrays, outs, nsem, run("start"), run("finish"), aliases)


def _pc(body, *, name, out_shape, grid=None, in_specs=None, out_specs=None, scratch=(), sem=None,
        grid_spec=None, comm=None, aliases=None):
    cp = dict(vmem_limit_bytes=VMEM_LIMIT)
    aliases = dict(aliases or {})
    if comm is None:
        if sem is not None:
            cp["dimension_semantics"] = sem
        kw = {"input_output_aliases": aliases}
        if grid_spec is not None:
            kw["grid_spec"] = grid_spec
        else:
            if grid is not None:
                kw["grid"] = grid
            if in_specs is not None:
                kw["in_specs"] = in_specs
            if out_specs is not None:
                kw["out_specs"] = out_specs
            kw["scratch_shapes"] = list(scratch)
        return pl.pallas_call(functools.partial(body), name=name, out_shape=out_shape,
                              compiler_params=pltpu.CompilerParams(**cp), **kw)

    single = not isinstance(out_shape, (list, tuple))
    outs_list = [out_shape] if single else list(out_shape)
    ospecs = [out_specs] if single else list(out_specs)
    n_in, n_out, n_ci, n_co, n_scr = len(in_specs), len(outs_list), len(comm.arrays), len(comm.out_shapes), len(scratch)
    cp["dimension_semantics"] = ("arbitrary",) * len(grid)

    def hosted(*refs):
        cin, hin = refs[:n_in], refs[n_in:n_in + n_ci]
        cout = refs[n_in + n_ci:n_in + n_ci + n_out]
        hout = refs[n_in + n_ci + n_out:n_in + n_ci + n_out + n_co]
        scr = refs[n_in + n_ci + n_out + n_co:n_in + n_ci + n_out + n_co + n_scr]
        ssem, rsem = refs[-2], refs[-1]
        first = functools.reduce(lambda p, q: p & q, [pl.program_id(a) == 0 for a in range(len(grid))])
        last = functools.reduce(lambda p, q: p & q, [pl.program_id(a) == grid[a] - 1 for a in range(len(grid))])

        @pl.when(first)
        def _():
            comm.start(hin, hout, ssem, rsem, 0)

        body(*cin, *cout, *scr)

        @pl.when(last)
        def _():
            comm.finish(hin, hout, ssem, rsem, 0)

    call = pl.pallas_call(
        hosted, name=name, grid=grid, in_specs=list(in_specs) + [ANY] * n_ci, out_specs=ospecs + [ANY] * n_co,
        out_shape=outs_list + comm.out_shapes,
        scratch_shapes=list(scratch) + [pltpu.SemaphoreType.DMA((comm.nsem,)), pltpu.SemaphoreType.DMA((comm.nsem,))],
        input_output_aliases={**aliases, **{n_in + i: n_out + o for i, o in comm.aliases.items()}},
        compiler_params=pltpu.CompilerParams(**cp))

    def run(*args):
        res = call(*args, *comm.arrays)
        comp = res[:n_out]
        return (comp[0] if single else comp), list(res[n_out:])

    return run


def _tile(n, pref):
    return pref if n % pref == 0 else n


def _sigmoid(x):
    return 1.0 / (1.0 + jnp.exp(-x))


def _gelu(x):
    c = math.sqrt(2.0 / math.pi)
    return 0.5 * x * (1.0 + jnp.tanh(c * (x + 0.044715 * (x * x * x))))


def _gelu_and_grad(x):
    c = math.sqrt(2.0 / math.pi)
    x2 = x * x
    t = jnp.tanh(c * (x + 0.044715 * (x2 * x)))
    half = 0.5 * (1.0 + t)
    return x * half, half + (0.5 * x) * (1.0 - t * t) * (c + (3.0 * 0.044715 * c) * x2)


def _dot(a, b):
    return jnp.dot(a, b, preferred_element_type=F32)


def _dot_nt(a, b):
    return lax.dot_general(a, b, (((1,), (1,)), ((), ())), preferred_element_type=F32)


def _dot_tn(a, b):
    return lax.dot_general(a, b, (((0,), (0,)), ((), ())), preferred_element_type=F32)


def _tri_mask(n, reverse):
    r = lax.broadcasted_iota(jnp.int32, (n, n), 0)
    c = lax.broadcasted_iota(jnp.int32, (n, n), 1)
    same = (r // CHUNK) == (c // CHUNK)
    tri = (c >= r) if reverse else (c <= r)
    return jnp.where(same & tri, 1.0, 0.0).astype(BF16)


def _tri_apply(tri, x):
    hi = x.astype(BF16)
    r1 = x - hi.astype(F32)
    mid = r1.astype(BF16)
    lo = (r1 - mid.astype(F32)).astype(BF16)
    return _dot(tri, hi) + (_dot(tri, mid) + _dot(tri, lo))


def _all_gather(arrs, name):
    n = len(arrs)

    def body(*refs):
        ins, outs = refs[:n], refs[n:2 * n]
        send_sems, recv_sems, local_sems = refs[2 * n:]
        x, y, c = lax.axis_index("x"), lax.axis_index("y"), lax.axis_index("c")
        me, sibling = (x, y, c), (x, y, 1 - c)
        chips = [(1 - x, y), (x, 1 - y), (1 - x, 1 - y)]

        def blk(a, p):
            return outs[a].at[4 * p[0] + 2 * p[1] + p[2]]

        def copy(a, k, block, to, src=None):
            return pltpu.make_async_remote_copy(
                src_ref=blk(a, block) if src is None else src, dst_ref=blk(a, block),
                send_sem=send_sems.at[7 * a + k], recv_sem=recv_sems.at[7 * a + k],
                device_id=to, device_id_type=MESH)

        mine = [pltpu.make_async_copy(ins[a], blk(a, me), local_sems.at[a]) for a in range(n)]
        for m in mine:
            m.start()
        first = []
        for a in range(n):
            first.append(copy(a, 0, me, sibling, src=ins[a]))
            for j, chip in enumerate(chips):
                first.append(copy(a, 1 + j, me, (*chip, c), src=ins[a]))
        for cp in first:
            cp.start()
        passed = []
        for j, chip in enumerate(chips):
            for a in range(n):
                copy(a, 1 + j, (*chip, c), me).wait_recv()
                p = copy(a, 4 + j, (*chip, c), sibling)
                p.start()
                passed.append(p)
        for a in range(n):
            copy(a, 0, sibling, me).wait_recv()
            for j, chip in enumerate(chips):
                copy(a, 4 + j, (*chip, 1 - c), me).wait_recv()
        for cp in first + passed:
            cp.wait_send()
        for m in mine:
            m.wait()

    out_shape = [jax.ShapeDtypeStruct((NDEV,) + a.shape, a.dtype) for a in arrs]
    return _pc(body, name=name, out_shape=out_shape, in_specs=[ANY] * n, out_specs=[ANY] * n,
               scratch=[pltpu.SemaphoreType.DMA((7 * n,)), pltpu.SemaphoreType.DMA((7 * n,)),
                        pltpu.SemaphoreType.DMA((n,))])(*arrs)


def _gather_first(arrs):
    n = len(arrs)

    def parts(ins, outs, ss, rs, base):
        x, y, c = lax.axis_index("x"), lax.axis_index("y"), lax.axis_index("c")
        me, sibling = (x, y, c), (x, y, 1 - c)
        chips = [(1 - x, y), (x, 1 - y), (1 - x, 1 - y)]

        def blk(a, p):
            return outs[a].at[4 * p[0] + 2 * p[1] + p[2]]

        def copy(a, k, block, to):
            return pltpu.make_async_remote_copy(
                src_ref=ins[a], dst_ref=blk(a, block), send_sem=ss.at[base + 4 * a + k],
                recv_sem=rs.at[base + 4 * a + k], device_id=to, device_id_type=MESH)

        local = [pltpu.make_async_copy(ins[a], blk(a, me), ss.at[base + 4 * n + a]) for a in range(n)]
        sends, recvs = [], []
        for a in range(n):
            sends.append(copy(a, 0, me, sibling))
            recvs.append(copy(a, 0, sibling, me))
            for j, chip in enumerate(chips):
                sends.append(copy(a, 1 + j, me, (*chip, c)))
                recvs.append(copy(a, 1 + j, (*chip, c), me))
        return local, sends, recvs

    def start(ins, outs, ss, rs, base):
        local, sends, _ = parts(ins, outs, ss, rs, base)
        for cp in local + sends:
            cp.start()

    def finish(ins, outs, ss, rs, base):
        local, sends, recvs = parts(ins, outs, ss, rs, base)
        for cp in recvs:
            cp.wait_recv()
        for cp in sends:
            cp.wait_send()
        for cp in local:
            cp.wait()

    return _Hosted(arrs, [jax.ShapeDtypeStruct((NDEV,) + a.shape, a.dtype) for a in arrs], 5 * n, start, finish)


def _gather_second(bufs):
    n = len(bufs)

    def parts(ins, outs, ss, rs, base):
        x, y, c = lax.axis_index("x"), lax.axis_index("y"), lax.axis_index("c")
        sibling = (x, y, 1 - c)
        chips = [(1 - x, y), (x, 1 - y), (1 - x, 1 - y)]
        sends, recvs = [], []
        for a in range(n):
            for j, chip in enumerate(chips):
                mine = 4 * chip[0] + 2 * chip[1] + c
                theirs = 4 * chip[0] + 2 * chip[1] + (1 - c)
                sends.append(pltpu.make_async_remote_copy(
                    src_ref=ins[a].at[mine], dst_ref=outs[a].at[mine], send_sem=ss.at[base + 3 * a + j],
                    recv_sem=rs.at[base + 3 * a + j], device_id=sibling, device_id_type=MESH))
                recvs.append(pltpu.make_async_remote_copy(
                    src_ref=ins[a].at[theirs], dst_ref=outs[a].at[theirs], send_sem=ss.at[base + 3 * a + j],
                    recv_sem=rs.at[base + 3 * a + j], device_id=sibling, device_id_type=MESH))
        return sends, recvs

    def start(ins, outs, ss, rs, base):
        for cp in parts(ins, outs, ss, rs, base)[0]:
            cp.start()

    def finish(ins, outs, ss, rs, base):
        sends, recvs = parts(ins, outs, ss, rs, base)
        for cp in recvs:
            cp.wait_recv()
        for cp in sends:
            cp.wait_send()

    return _Hosted(bufs, [jax.ShapeDtypeStruct(b.shape, b.dtype) for b in bufs], 3 * n, start, finish,
                   aliases={a: a for a in range(n)})


def _swap(src, nblk, ids_fn, partner_fn):
    def copies(ins, outs, ss, rs, base):
        x, y, c = lax.axis_index("x"), lax.axis_index("y"), lax.axis_index("c")
        ids = ids_fn(x, y, c)
        partner = partner_fn(x, y, c)
        return [pltpu.make_async_remote_copy(
            src_ref=ins[0].at[ids[k]], dst_ref=outs[0].at[k], send_sem=ss.at[base + k], recv_sem=rs.at[base + k],
            device_id=partner, device_id_type=MESH) for k in range(nblk)]

    def start(ins, outs, ss, rs, base):
        for cp in copies(ins, outs, ss, rs, base):
            cp.start()

    def finish(ins, outs, ss, rs, base):
        for cp in copies(ins, outs, ss, rs, base):
            cp.wait()

    return _Hosted([src], [jax.ShapeDtypeStruct((nblk,) + src.shape[1:], src.dtype)], nblk, start, finish)


def _blocking(comm, name):
    n_i, n_o = len(comm.arrays), len(comm.out_shapes)

    def body(*refs):
        ins, outs = refs[:n_i], refs[n_i:n_i + n_o]
        comm.start(ins, outs, refs[-2], refs[-1], 0)
        comm.finish(ins, outs, refs[-2], refs[-1], 0)

    return pl.pallas_call(
        body, name=name, out_shape=comm.out_shapes, in_specs=[ANY] * n_i, out_specs=[ANY] * n_o,
        scratch_shapes=[pltpu.SemaphoreType.DMA((comm.nsem,)), pltpu.SemaphoreType.DMA((comm.nsem,))],
        input_output_aliases=comm.aliases)(*comm.arrays)


def _swap_chips(send):
    def copies(ins, outs, ss, rs, base):
        x, y, c = lax.axis_index("x"), lax.axis_index("y"), lax.axis_index("c")
        chips = [(1 - x, y), (x, 1 - y), (1 - x, 1 - y)]
        return [pltpu.make_async_remote_copy(
            src_ref=ins[0].at[j], dst_ref=outs[0].at[j], send_sem=ss.at[base + j], recv_sem=rs.at[base + j],
            device_id=(*chip, c), device_id_type=MESH) for j, chip in enumerate(chips)]

    def start(ins, outs, ss, rs, base):
        for cp in copies(ins, outs, ss, rs, base):
            cp.start()

    def finish(ins, outs, ss, rs, base):
        for cp in copies(ins, outs, ss, rs, base):
            cp.wait()

    return _Hosted([send], [jax.ShapeDtypeStruct(send.shape, send.dtype)], 3, start, finish)


def _add_send(a, b, idx, ns, name):
    _, r, c = a.shape
    tr = _tile(r, 256)

    def body(idx_ref, a_ref, b_ref, send_ref):
        send_ref[...] = (a_ref[...] + b_ref[...]).astype(BF16)

    def sel(off):
        return pl.BlockSpec((None, tr, c), lambda k, i, s: (s[off + k], i, 0))

    gs = pltpu.PrefetchScalarGridSpec(num_scalar_prefetch=1, grid=(ns, r // tr), in_specs=[sel(0), sel(ns)],
                                      out_specs=pl.BlockSpec((None, tr, c), lambda k, i, s: (k, i, 0)))
    return _pc(body, name=name, grid_spec=gs, sem=("arbitrary", "arbitrary"),
               out_shape=jax.ShapeDtypeStruct((ns, r, c), BF16))(idx, a, b)


class _ReduceScatter:
    def __init__(self, g, tag):
        self.g, self.tag = g, tag

    def swap_core(self):
        return _swap(self.g, 4, lambda x, y, c: [1 - c, 3 - c, 5 - c, 7 - c], lambda x, y, c: (x, y, 1 - c))

    def after_core(self, recv):
        x, y, c = lax.axis_index("x"), lax.axis_index("y"), lax.axis_index("c")
        chips = [(1 - x, y), (x, 1 - y), (1 - x, 1 - y)]
        idx = jnp.stack([4 * p + 2 * q + c for p, q in chips] + [2 * p + q for p, q in chips]).astype(jnp.int32)
        self.send = _add_send(self.g, recv, idx, 3, "rs_add_" + self.tag)
        self.recv_core = recv
        zero = jnp.zeros((), jnp.int32)
        self.idx = jnp.stack([4 * x + 2 * y + c, 2 * x + y, zero, zero + 1, zero + 2]).astype(jnp.int32)

    def swap_chips(self):
        return _swap_chips(self.send)

    def after_chips(self, recv):
        self.parts = [self.g, self.recv_core, recv, recv, recv]


def _ada_fwd(c_all, w_ada, b_cols, b_lb):
    nl, d, ncol = w_ada.shape
    nseq = c_all.shape[0]
    di = b_lb.shape[1]

    def body(c_ref, w_ref, b_ref, lb_ref, mod_ref, lbj_ref):
        cv = c_ref[...]
        cact = (cv * _sigmoid(cv)).astype(BF16)
        for l in range(nl):
            mod_ref[l] = _dot(cact, w_ref[l].astype(BF16)) + b_ref[l]
        b0, b1 = lb_ref[0:1, :], lb_ref[1:2, :]
        mx = jnp.maximum(b0, b1)
        e0, e1 = jnp.exp(b0 - mx), jnp.exp(b1 - mx)
        s = e0 + e1
        p0, p1 = e0 / s, e1 / s
        lbj_ref[0:1, :] = (p0 + p1) - p0
        lbj_ref[1:2, :] = p0 * p1

    return _pc(body, name="ada_fwd",
               out_shape=[jax.ShapeDtypeStruct((nl, nseq, ncol), F32), jax.ShapeDtypeStruct((2, di), F32)]
               )(c_all, w_ada, b_cols, b_lb)


def _ada_bwd(c_all, dmod_cols, dmod_full):
    nl, nseq, ncol = dmod_cols.shape
    d = c_all.shape[1]
    d3 = dmod_full.shape[2]

    def body(c_ref, dc_ref, df_ref, gw_ref, gb_ref):
        cv = c_ref[...]
        cact = (cv * _sigmoid(cv)).astype(BF16)
        for l in range(nl):
            gw_ref[l] = _dot_tn(cact, dc_ref[l].astype(BF16))
            gb_ref[l:l + 1, :] = jnp.sum(df_ref[l], axis=0, keepdims=True)

    return _pc(body, name="ada_bwd",
               out_shape=[jax.ShapeDtypeStruct((nl, d, ncol), F32), jax.ShapeDtypeStruct((nl, d3), F32)]
               )(c_all, dmod_cols, dmod_full)


def _prenorm(x, gain, mod, t_seq, name, comm=None):
    m, d = x.shape
    tm = _tile(t_seq, 512)
    per = t_seq // tm

    def body(x_ref, g_ref, mod_ref, h_ref, ht_ref):
        xv = x_ref[...]
        rstd = lax.rsqrt(jnp.mean(xv * xv, axis=-1, keepdims=True) + EPS)
        r = xv * rstd * g_ref[...]
        h = r * (1.0 + mod_ref[0, 1:2, :]) + mod_ref[0, 0:1, :]
        h_ref[...] = h.astype(BF16)
        ht_ref[...] = h.T.astype(BF16)

    return _pc(body, name=name, out_shape=[jax.ShapeDtypeStruct((m, d), BF16), jax.ShapeDtypeStruct((d, m), BF16)],
               grid=(m // tm,),
               in_specs=[pl.BlockSpec((tm, d), lambda i: (i, 0)), pl.BlockSpec((1, d), lambda i: (0, 0)),
                         pl.BlockSpec((1, 3, d), lambda i: (i // per, 0, 0))],
               out_specs=[pl.BlockSpec((tm, d), lambda i: (i, 0)), pl.BlockSpec((d, tm), lambda i: (0, i))],
               sem=("parallel",), comm=comm)(x, gain, mod)


def _prenorm_bwd(dh, x, gain, mod, dxn, t_seq, name, comm=None):
    m, d = x.shape
    nb = m // t_seq
    tm = _tile(t_seq, 512)
    per = t_seq // tm

    def body(dh_ref, x_ref, g_ref, mod_ref, dxn_ref, dx_ref, dss_ref, dg_ref):
        i = pl.program_id(0)
        xv, dhv, g = x_ref[...], dh_ref[...], g_ref[...]
        rstd = lax.rsqrt(jnp.mean(xv * xv, axis=-1, keepdims=True) + EPS)
        xhat = xv * rstd
        dr = dhv * (1.0 + mod_ref[0, 1:2, :])
        dxhat = dr * g
        dx_ref[...] = dxn_ref[...] + rstd * (dxhat - xhat * jnp.mean(dxhat * xhat, axis=-1, keepdims=True))

        @pl.when(i % per == 0)
        def _():
            dss_ref[...] = jnp.zeros_like(dss_ref)

        @pl.when(i == 0)
        def _():
            dg_ref[...] = jnp.zeros_like(dg_ref)

        dss_ref[0, 0:1, :] += jnp.sum(dhv, axis=0, keepdims=True)
        dss_ref[0, 1:2, :] += jnp.sum(dhv * (xhat * g), axis=0, keepdims=True)
        dg_ref[...] += jnp.sum(dr * xhat, axis=0, keepdims=True)

    row = pl.BlockSpec((tm, d), lambda i: (i, 0))
    return _pc(body, name=name,
               out_shape=[jax.ShapeDtypeStruct((m, d), F32), jax.ShapeDtypeStruct((nb, 2, d), F32),
                          jax.ShapeDtypeStruct((1, d), F32)],
               grid=(m // tm,),
               in_specs=[row, row, pl.BlockSpec((1, d), lambda i: (0, 0)),
                         pl.BlockSpec((1, 3, d), lambda i: (i // per, 0, 0)), row],
               out_specs=[row, pl.BlockSpec((1, 2, d), lambda i: (i // per, 0, 0)),
                          pl.BlockSpec((1, d), lambda i: (0, 0))],
               sem=("arbitrary",), comm=comm)(dh, x, gain, mod, dxn)


def _mm_in(h, ws, sections, name, comm=None):
    m, k = h.shape
    nw = len(ws)
    widths = [w.shape[2] for w in ws]
    offs = [sum(widths[:a]) for a in range(nw)]
    nc = sum(widths)
    per = NDEV // sections if sections > 1 else NDEV
    tm = _tile(m, 512)
    assert per % 2 == 0

    def body(*refs):
        hv = refs[0][...]
        o_ref = refs[1 + nw]
        for b in range(2):
            for a in range(nw):
                lo = b * nc + offs[a]
                o_ref[:, lo:lo + widths[a]] = _dot(hv, refs[1 + a][b]).astype(BF16)

    w_specs = [pl.BlockSpec((2, k, wd), lambda j, i: (j, 0, 0)) for wd in widths]
    if sections > 1:
        out_shape = jax.ShapeDtypeStruct((sections, m, per * nc), BF16)
        out_spec = pl.BlockSpec((None, tm, 2 * nc), lambda j, i: ((2 * j) // per, i, ((2 * j) % per) // 2))
    else:
        out_shape = jax.ShapeDtypeStruct((m, NDEV * nc), BF16)
        out_spec = pl.BlockSpec((tm, 2 * nc), lambda j, i: (i, j))
    return _pc(body, name=name, out_shape=out_shape, grid=(NDEV // 2, m // tm),
               in_specs=[pl.BlockSpec((tm, k), lambda j, i: (i, 0))] + w_specs,
               out_specs=out_spec, sem=("parallel", "parallel"), comm=comm)(h, *ws)


def _din_tile(m):
    return 1024 if m % 1024 == 0 and m >= 2048 else _tile(m, 512)


def _mm_din(dproj, ws, sections, name, comm=None, tiles=None, prev=None):
    nw, k = len(ws), ws[0].shape[1]
    widths = [w.shape[2] for w in ws]
    offs = [sum(widths[:a]) for a in range(nw)]
    nc = sum(widths)
    m = dproj.shape[-2]
    tm = _din_tile(m)
    t0, nt = tiles if tiles is not None else (0, m // tm)
    per = NDEV // sections if sections > 1 else NDEV
    assert per % 2 == 0

    def body(*refs):
        d_ref, o_ref = refs[0], refs[-1]
        j = pl.program_id(1)
        acc = None
        for b in range(2):
            for a in range(nw):
                lo = b * nc + offs[a]
                term = _dot_nt(d_ref[:, lo:lo + widths[a]], refs[1 + a][b])
                acc = term if acc is None else acc + term

        @pl.when(j == 0)
        def _():
            o_ref[...] = acc

        @pl.when(j > 0)
        def _():
            o_ref[...] += acc

    if sections > 1:
        dspec = pl.BlockSpec((None, tm, 2 * nc), lambda i, j: ((2 * j) // per, i + t0, ((2 * j) % per) // 2))
    else:
        dspec = pl.BlockSpec((tm, 2 * nc), lambda i, j: (i + t0, j))
    in_specs = [dspec] + [pl.BlockSpec((2, k, wd), lambda i, j: (j, 0, 0)) for wd in widths]
    args = [dproj, *ws]
    if prev is not None:
        in_specs.append(ANY)
        args.append(prev)
    return _pc(body, name=name, out_shape=jax.ShapeDtypeStruct((m, k), F32), grid=(nt, NDEV // 2), in_specs=in_specs,
               out_specs=pl.BlockSpec((tm, k), lambda i, j: (i + t0, 0)), sem=("parallel", "arbitrary"),
               comm=comm, aliases={1 + nw: 0} if prev is not None else None)(*args)


def _mm_dw_in(ht, dproj, nc, sections, name, comm=None):
    k, m = ht.shape
    tk = 2048 if m % 2048 == 0 else _din_tile(m)
    per = NDEV // sections if sections > 1 else NDEV

    def body(h_ref, d_ref, o_ref):
        kk = pl.program_id(1)
        acc = _dot(h_ref[...], d_ref[...])

        @pl.when(kk == 0)
        def _():
            o_ref[...] = acc

        @pl.when(kk > 0)
        def _():
            o_ref[...] += acc

    if sections > 1:
        dspec = pl.BlockSpec((None, tk, nc), lambda j, i: (j // per, i, j % per))
    else:
        dspec = pl.BlockSpec((tk, nc), lambda j, i: (i, j))
    return _pc(body, name=name, out_shape=jax.ShapeDtypeStruct((NDEV, k, nc), F32), grid=(NDEV, m // tk),
               in_specs=[pl.BlockSpec((k, tk), lambda j, i: (0, i)), dspec],
               out_specs=pl.BlockSpec((None, k, nc), lambda j, i: (j, 0, 0)),
               sem=("parallel", "arbitrary"), comm=comm)(ht, dproj)


def _out_proj(ybr, w_out, x, mod, t_seq, name, comm=None):
    m, di = ybr.shape
    d = w_out.shape[1]
    tm = _tile(t_seq, 512)
    per = t_seq // tm

    def body(y_ref, w_ref, x_ref, mod_ref, yo_ref, xn_ref):
        yo = _dot(y_ref[...], w_ref[...])
        yo_ref[...] = yo
        xn_ref[...] = x_ref[...] + mod_ref[0, 2:3, :] * yo

    row = pl.BlockSpec((tm, d), lambda i: (i, 0))
    return _pc(body, name=name,
               out_shape=[jax.ShapeDtypeStruct((m, d), F32), jax.ShapeDtypeStruct((m, d), F32)],
               grid=(m // tm,),
               in_specs=[pl.BlockSpec((tm, di), lambda i: (i, 0)), pl.BlockSpec((di, d), lambda i: (0, 0)), row,
                         pl.BlockSpec((1, 3, d), lambda i: (i // per, 0, 0))],
               out_specs=[row, row], sem=("parallel",), comm=comm)(ybr, w_out, x, mod)


def _out_proj_loss(ybr, w_out, x, mod, gain, target, t_seq):
    m, di = ybr.shape
    d = w_out.shape[1]
    tm = _tile(t_seq, 512)
    per = t_seq // tm

    def body(y_ref, w_ref, x_ref, mod_ref, g_ref, t_ref, yo_ref, dx_ref, loss_ref, dg_ref):
        i = pl.program_id(0)
        yo = _dot(y_ref[...], w_ref[...])
        yo_ref[...] = yo
        xv = x_ref[...] + mod_ref[0, 2:3, :] * yo
        g = g_ref[...]
        rstd = lax.rsqrt(jnp.mean(xv * xv, axis=-1, keepdims=True) + EPS)
        xhat = xv * rstd
        err = xhat * g - t_ref[...]
        dy = err * (1.0 / d)
        dxhat = dy * g
        dx_ref[...] = rstd * (dxhat - xhat * jnp.mean(dxhat * xhat, axis=-1, keepdims=True))

        @pl.when(i == 0)
        def _():
            loss_ref[...] = jnp.zeros_like(loss_ref)
            dg_ref[...] = jnp.zeros_like(dg_ref)

        loss_ref[...] += 0.5 * jnp.sum(jnp.mean(err * err, axis=-1, keepdims=True), axis=0, keepdims=True)
        dg_ref[...] += jnp.sum(dy * xhat, axis=0, keepdims=True)

    row = pl.BlockSpec((tm, d), lambda i: (i, 0))
    vec = pl.BlockSpec((1, d), lambda i: (0, 0))
    return _pc(body, name="out_proj_loss",
               out_shape=[jax.ShapeDtypeStruct((m, d), F32), jax.ShapeDtypeStruct((m, d), F32),
                          jax.ShapeDtypeStruct((1, 1), F32), jax.ShapeDtypeStruct((1, d), F32)],
               grid=(m // tm,),
               in_specs=[pl.BlockSpec((tm, di), lambda i: (i, 0)), pl.BlockSpec((di, d), lambda i: (0, 0)), row,
                         pl.BlockSpec((1, 3, d), lambda i: (i // per, 0, 0)), vec, row],
               out_specs=[row, row, pl.BlockSpec((1, 1), lambda i: (0, 0)), vec],
               sem=("arbitrary",))(ybr, w_out, x, mod, gain, target)


def _gate_dybr(dxn, yout, mod, w_out, t_seq, name):
    m, d = dxn.shape
    di = w_out.shape[0]
    nb = m // t_seq
    tm = _tile(t_seq, 512)
    per = t_seq // tm

    def body(dxn_ref, yo_ref, mod_ref, w_ref, dy_ref, dgate_ref, o_ref):
        i = pl.program_id(0)
        dv = dxn_ref[...]
        dy = (mod_ref[0, 2:3, :] * dv).astype(BF16)
        dy_ref[...] = dy
        o_ref[...] = _dot_nt(dy, w_ref[...])

        @pl.when(i % per == 0)
        def _():
            dgate_ref[...] = jnp.zeros_like(dgate_ref)

        dgate_ref[0] += jnp.sum(dv * yo_ref[...], axis=0, keepdims=True)

    row = pl.BlockSpec((tm, d), lambda i: (i, 0))
    return _pc(body, name=name,
               out_shape=[jax.ShapeDtypeStruct((m, d), BF16), jax.ShapeDtypeStruct((nb, 1, d), F32),
                          jax.ShapeDtypeStruct((m, di), F32)],
               grid=(m // tm,),
               in_specs=[row, row, pl.BlockSpec((1, 3, d), lambda i: (i // per, 0, 0)),
                         pl.BlockSpec((di, d), lambda i: (0, 0))],
               out_specs=[row, pl.BlockSpec((1, 1, d), lambda i: (i // per, 0, 0)),
                          pl.BlockSpec((tm, di), lambda i: (i, 0))],
               sem=("arbitrary",))(dxn, yout, mod, w_out)


def _mm_dw_out(ybr, dy, name, comm=None):
    m, di = ybr.shape
    d = dy.shape[1]
    tk = _tile(m, 512)
    tn = _tile(di, 1024)

    def body(y_ref, dy_ref, o_ref):
        kk = pl.program_id(1)
        acc = _dot_tn(y_ref[...], dy_ref[...])

        @pl.when(kk == 0)
        def _():
            o_ref[...] = acc

        @pl.when(kk > 0)
        def _():
            o_ref[...] += acc

    return _pc(body, name=name, out_shape=jax.ShapeDtypeStruct((di, d), F32), grid=(di // tn, m // tk),
               in_specs=[pl.BlockSpec((tk, tn), lambda n, k: (k, n)), pl.BlockSpec((tk, d), lambda n, k: (k, 0))],
               out_specs=pl.BlockSpec((tn, d), lambda n, k: (n, 0)), sem=("parallel", "arbitrary"),
               comm=comm)(ybr, dy)


def _sgu_mask():
    t = lax.broadcasted_iota(jnp.int32, (SG_BLOCK, SG_BLOCK), 0)
    s = lax.broadcasted_iota(jnp.int32, (SG_BLOCK, SG_BLOCK), 1)
    return (s // CHUNK) <= (t // CHUNK)


def _a_mid_fwd(proj, ln_g, ln_b, w_s, bs_t, t_seq, comm=None):
    m, n3 = proj.shape
    di = n3 // 3
    gd = di // SG_GROUPS
    r = _tile(t_seq, 256)
    nblk = r // SG_BLOCK

    def body(p_ref, lg_ref, lb_ref, ws_ref, bs_ref, ybr_ref, s_scr):
        v = _gelu(p_ref[:, di:2 * di].astype(F32))
        mu = jnp.mean(v, axis=-1, keepdims=True)
        vc = v - mu
        rstd = lax.rsqrt(jnp.mean(vc * vc, axis=-1, keepdims=True) + EPS)
        vb = (vc * rstd * lg_ref[...] + lb_ref[...]).astype(BF16)
        mask = _sgu_mask()
        for gi in range(SG_GROUPS):
            ws = jnp.where(mask, ws_ref[gi], 0.0).astype(BF16)
            bcol = bs_ref[:, gi:gi + 1]
            for b in range(nblk):
                rows = slice(b * SG_BLOCK, (b + 1) * SG_BLOCK)
                cols = slice(gi * gd, (gi + 1) * gd)
                s_scr[rows, cols] = _dot(ws, vb[rows, cols]) + bcol
        gg = p_ref[:, 2 * di:].astype(F32)
        ybr_ref[...] = (_gelu(p_ref[:, :di].astype(F32)) * s_scr[...] * (gg * _sigmoid(gg))).astype(BF16)

    vec = pl.BlockSpec((1, di), lambda i: (0, 0))
    return _pc(body, name="a_mid_fwd", out_shape=jax.ShapeDtypeStruct((m, di), BF16), grid=(m // r,),
               in_specs=[pl.BlockSpec((r, n3), lambda i: (i, 0)), vec, vec,
                         pl.BlockSpec((SG_GROUPS, SG_BLOCK, SG_BLOCK), lambda i: (0, 0, 0)),
                         pl.BlockSpec((SG_BLOCK, 128), lambda i: (0, 0))],
               out_specs=pl.BlockSpec((r, di), lambda i: (i, 0)),
               scratch=[pltpu.VMEM((r, di), F32)], sem=("parallel",), comm=comm)(proj, ln_g, ln_b, w_s, bs_t)


def _a_mid_bwd(proj, dybr, ln_g, ln_b, w_s, bs_t, t_seq, comm=None):
    m, n3 = proj.shape
    di = n3 // 3
    gd = di // SG_GROUPS
    r = _tile(t_seq, 256)
    nblk = r // SG_BLOCK

    def body(p_ref, dy_ref, lg_ref, lb_ref, ws_ref, bs_ref,
             dp_ref, dlg_ref, dlb_ref, dws_ref, dbs_ref, s_scr, dvl_scr):
        i = pl.program_id(0)

        @pl.when(i == 0)
        def _():
            dlg_ref[...] = jnp.zeros_like(dlg_ref)
            dlb_ref[...] = jnp.zeros_like(dlb_ref)
            dws_ref[...] = jnp.zeros_like(dws_ref)
            dbs_ref[...] = jnp.zeros_like(dbs_ref)

        v, dgelu_v = _gelu_and_grad(p_ref[:, di:2 * di].astype(F32))
        mu = jnp.mean(v, axis=-1, keepdims=True)
        vc = v - mu
        rstd = lax.rsqrt(jnp.mean(vc * vc, axis=-1, keepdims=True) + EPS)
        vhat = vc * rstd
        lg = lg_ref[...]
        vb = (vhat * lg + lb_ref[...]).astype(BF16)
        u, dgelu_u = _gelu_and_grad(p_ref[:, :di].astype(F32))
        gg = p_ref[:, 2 * di:].astype(F32)
        sg = _sigmoid(gg)
        dyv = dy_ref[...]
        dus = dyv * (gg * sg)
        dsb = (dus * u).astype(BF16)
        ds32 = dus * u
        mask = _sgu_mask()
        lane = lax.broadcasted_iota(jnp.int32, (SG_BLOCK, 128), 1)
        dbs_acc = jnp.zeros((SG_BLOCK, 128), F32)
        for gi in range(SG_GROUPS):
            ws = jnp.where(mask, ws_ref[gi], 0.0).astype(BF16)
            bcol = bs_ref[:, gi:gi + 1]
            cols = slice(gi * gd, (gi + 1) * gd)
            dws_acc = jnp.zeros((SG_BLOCK, SG_BLOCK), F32)
            dbs_col = jnp.zeros((SG_BLOCK, 1), F32)
            for b in range(nblk):
                rows = slice(b * SG_BLOCK, (b + 1) * SG_BLOCK)
                s_scr[rows, cols] = _dot(ws, vb[rows, cols]) + bcol
                dvl_scr[rows, cols] = _dot_tn(ws, dsb[rows, cols])
                dws_acc += _dot_nt(dsb[rows, cols], vb[rows, cols])
                dbs_col += jnp.sum(ds32[rows, cols], axis=-1, keepdims=True)
            dws_ref[gi] += jnp.where(mask, dws_acc, 0.0)
            dbs_acc += jnp.where(lane == gi, dbs_col, 0.0)
        dbs_ref[...] += dbs_acc
        s = s_scr[...]
        dp_ref[:, :di] = (dus * s * dgelu_u).astype(BF16)
        dp_ref[:, 2 * di:] = (dyv * u * s * (sg * (1.0 + gg * (1.0 - sg)))).astype(BF16)
        dvl = dvl_scr[...]
        dlg_ref[...] += jnp.sum(dvl * vhat, axis=0, keepdims=True)
        dlb_ref[...] += jnp.sum(dvl, axis=0, keepdims=True)
        dvh = dvl * lg
        dv = rstd * (dvh - jnp.mean(dvh, axis=-1, keepdims=True)
                     - vhat * jnp.mean(dvh * vhat, axis=-1, keepdims=True))
        dp_ref[:, di:2 * di] = (dv * dgelu_v).astype(BF16)

    vec = pl.BlockSpec((1, di), lambda i: (0, 0))
    wsb = pl.BlockSpec((SG_GROUPS, SG_BLOCK, SG_BLOCK), lambda i: (0, 0, 0))
    bsb = pl.BlockSpec((SG_BLOCK, 128), lambda i: (0, 0))
    return _pc(body, name="a_mid_bwd",
               out_shape=[jax.ShapeDtypeStruct((m, n3), BF16), jax.ShapeDtypeStruct((1, di), F32),
                          jax.ShapeDtypeStruct((1, di), F32),
                          jax.ShapeDtypeStruct((SG_GROUPS, SG_BLOCK, SG_BLOCK), F32),
                          jax.ShapeDtypeStruct((SG_BLOCK, 128), F32)],
               grid=(m // r,),
               in_specs=[pl.BlockSpec((r, n3), lambda i: (i, 0)), pl.BlockSpec((r, di), lambda i: (i, 0)),
                         vec, vec, wsb, bsb],
               out_specs=[pl.BlockSpec((r, n3), lambda i: (i, 0)), vec, vec, wsb, bsb],
               scratch=[pltpu.VMEM((r, di), F32), pltpu.VMEM((r, di), F32)],
               sem=("arbitrary",), comm=comm)(proj, dybr, ln_g, ln_b, w_s, bs_t)


def _hgrn_dims(t_seq, di):
    tr = _tile(t_seq, 256)
    hc = _tile(di, 1024)
    return tr, hc, hc // HEAD_DIM


def _hgrn_gates(f_ref, lb, a_scr, k_scr, tr):
    sig = _sigmoid(f_ref[...].astype(F32))
    fg = lb + (1.0 - lb) * sig
    k_scr[...] = 1.0 - fg
    logf = jnp.log(fg)
    g = min(CUM_ROWS, tr)
    tri = _tri_mask(g, reverse=False)
    for rg in range(tr // g):
        a_scr[rg * g:(rg + 1) * g, :] = _tri_apply(tri, logf[rg * g:(rg + 1) * g, :])
    return sig, fg


def _hgrn_fwd(proj, lbj, gn, nb, t_seq, comm=None):
    _, m, di = proj.shape
    tr, hc, hpg = _hgrn_dims(t_seq, di)
    nt, nhg, ncl = t_seq // tr, di // hc, tr // CHUNK
    nheads = di // HEAD_DIM

    def body(q_ref, f_ref, i_ref, g_ref, lb_ref, gn_ref, o_ref, ybr_ref, st_ref, st_scr, a_scr, k_scr):
        t = pl.program_id(2)

        @pl.when(t == 0)
        def _():
            st_scr[...] = jnp.zeros_like(st_scr)

        _hgrn_gates(f_ref, lb_ref[0:1, :], a_scr, k_scr, tr)
        gnv = gn_ref[...]
        rr = lax.broadcasted_iota(jnp.int32, (CHUNK, CHUNK), 0)
        cc = lax.broadcasted_iota(jnp.int32, (CHUNK, CHUNK), 1)
        causal = cc <= rr

        def chunk(n, carry):
            rows = pl.ds(pl.multiple_of(n * CHUNK, CHUNK), CHUNK)
            lanes = [slice(hd * HEAD_DIM, (hd + 1) * HEAD_DIM) for hd in range(hpg)]
            hs = []
            for hd, ls in enumerate(lanes):
                h = {}
                ah, kh = a_scr[rows, ls], k_scr[rows, ls]
                qp = q_ref[rows, ls].astype(F32)
                qh = qp * _sigmoid(qp)
                h["vb"] = i_ref[rows, ls]
                aref, alast = ah[CHUNK // 2 - 1:CHUNK // 2, :], ah[CHUNK - 1:CHUNK, :]
                h["q_in"] = (qh * jnp.exp(ah - aref)).astype(BF16)
                h["k_in"] = (kh * jnp.exp(aref - ah)).astype(BF16)
                h["q_out"] = (qh * jnp.exp(ah)).astype(BF16)
                h["k_out"] = (kh * jnp.exp(alast - ah)).astype(BF16)
                h["dec"] = jnp.exp(alast)
                st = st_scr[hd]
                st_ref[n, hd] = st
                h["st"] = st
                hs.append(h)
            for h in hs:
                h["scores"] = _dot_nt(h["q_in"], h["k_in"])
                h["o_inter"] = _dot_nt(h["q_out"], h["st"].astype(BF16))
                h["st_mm"] = _dot_tn(h["vb"], h["k_out"])
            for h in hs:
                h["o"] = _dot(jnp.where(causal, h["scores"], 0.0).astype(BF16), h["vb"]) + h["o_inter"]
            for hd, (h, ls) in enumerate(zip(hs, lanes)):
                st_scr[hd] = h["st"] * h["dec"] + h["st_mm"]
                o = h["o"]
                o_ref[rows, ls] = o
                rstd = lax.rsqrt(jnp.mean(o * o, axis=-1, keepdims=True) + EPS)
                gg = g_ref[rows, ls].astype(F32)
                ybr_ref[rows, ls] = ((o * rstd * gnv) * (gg * _sigmoid(gg))).astype(BF16)
            return carry

        lax.fori_loop(0, ncl, chunk, 0)

    def sec(s):
        return pl.BlockSpec((None, tr, hc), lambda hg, b, t: (s, b * nt + t, hg))

    blk = pl.BlockSpec((tr, hc), lambda hg, b, t: (b * nt + t, hg))
    return _pc(body, name="hgrn_fwd",
               out_shape=[jax.ShapeDtypeStruct((m, di), F32), jax.ShapeDtypeStruct((m, di), BF16),
                          jax.ShapeDtypeStruct((m // CHUNK, nheads, HEAD_DIM, HEAD_DIM), F32)],
               grid=(nhg, nb, nt),
               in_specs=[sec(0), sec(1), sec(2), sec(3), pl.BlockSpec((2, hc), lambda hg, b, t: (0, hg)),
                         pl.BlockSpec((1, HEAD_DIM), lambda hg, b, t: (0, 0))],
               out_specs=[blk, blk, pl.BlockSpec((ncl, hpg, HEAD_DIM, HEAD_DIM),
                                                 lambda hg, b, t: (b * nt + t, hg, 0, 0))],
               scratch=[pltpu.VMEM((hpg, HEAD_DIM, HEAD_DIM), F32), pltpu.VMEM((tr, hc), F32),
                        pltpu.VMEM((tr, hc), F32)],
               sem=("parallel", "arbitrary", "arbitrary"), comm=comm)(proj, proj, proj, proj, lbj, gn)


def _hgrn_bwd(proj, o_all, dybr, states, lbj, gn, nb, t_seq, comm=None):
    _, m, di = proj.shape
    tr, hc, hpg = _hgrn_dims(t_seq, di)
    nt, nhg, ncl = t_seq // tr, di // hc, tr // CHUNK

    def body(q_ref, f_ref, i_ref, g_ref, o_ref, dy_ref, st_ref, lb_ref, gn_ref,
             dp_ref, dlb_ref, dgn_ref, dst_scr, a_scr, k_scr, da_scr, dk_scr):
        hg, b, t = pl.program_id(0), pl.program_id(1), pl.program_id(2)

        @pl.when(t == 0)
        def _():
            dst_scr[...] = jnp.zeros_like(dst_scr)

        @pl.when((b == 0) & (t == 0))
        def _():
            dlb_ref[...] = jnp.zeros_like(dlb_ref)

        @pl.when((hg == 0) & (b == 0) & (t == 0))
        def _():
            dgn_ref[...] = jnp.zeros_like(dgn_ref)

        lb = lb_ref[0:1, :]
        sig, fg = _hgrn_gates(f_ref, lb, a_scr, k_scr, tr)
        gnv = gn_ref[...]
        rr = lax.broadcasted_iota(jnp.int32, (CHUNK, CHUNK), 0)
        cc = lax.broadcasted_iota(jnp.int32, (CHUNK, CHUNK), 1)
        causal = cc <= rr
        rowi = lax.broadcasted_iota(jnp.int32, (CHUNK, HEAD_DIM), 0)

        def chunk(it, carry):
            n = ncl - 1 - it
            rows = pl.ds(pl.multiple_of(n * CHUNK, CHUNK), CHUNK)
            lanes = [slice(hd * HEAD_DIM, (hd + 1) * HEAD_DIM) for hd in range(hpg)]
            hs = []
            for hd, ls in enumerate(lanes):
                h = {}
                ah, kh = a_scr[rows, ls], k_scr[rows, ls]
                qp = q_ref[rows, ls].astype(F32)
                sq = _sigmoid(qp)
                qh = qp * sq
                h["dsilu_q"] = sq * (1.0 + qp * (1.0 - sq))
                h["vb"] = i_ref[rows, ls]
                aref, alast = ah[CHUNK // 2 - 1:CHUNK // 2, :], ah[CHUNK - 1:CHUNK, :]
                h["e1"], h["e2"] = jnp.exp(ah - aref), jnp.exp(aref - ah)
                h["e3"], h["e4"] = jnp.exp(ah), jnp.exp(alast - ah)
                h["dec"] = jnp.exp(alast)
                h["q_in"], h["k_in"], h["q_out"], h["k_out"] = qh * h["e1"], kh * h["e2"], qh * h["e3"], kh * h["e4"]
                for nm in ("q_in", "k_in", "q_out", "k_out"):
                    h[nm + "_b"] = h[nm].astype(BF16)
                o = o_ref[rows, ls]
                rstd = lax.rsqrt(jnp.mean(o * o, axis=-1, keepdims=True) + EPS)
                ohat = o * rstd
                gg = g_ref[rows, ls].astype(F32)
                sg = _sigmoid(gg)
                dyv = dy_ref[rows, ls]
                d_on = dyv * (gg * sg)
                dp_ref[3, rows, ls] = (dyv * (ohat * gnv) * (sg * (1.0 + gg * (1.0 - sg)))).astype(BF16)
                h["dgn"] = jnp.sum(d_on * ohat, axis=0, keepdims=True)
                dohat = d_on * gnv
                do = rstd * (dohat - ohat * jnp.mean(dohat * ohat, axis=-1, keepdims=True))
                h["do_b"] = do.astype(BF16)
                h["st_prev"] = st_ref[n, hd]
                h["dst"] = dst_scr[hd]
                hs.append(h)
            for h in hs:
                dst_b = h["dst"].astype(BF16)
                h["scores"] = _dot_nt(h["q_in_b"], h["k_in_b"])
                h["dscores"] = _dot_nt(h["do_b"], h["vb"])
                h["dv_inter"] = _dot_nt(h["k_out_b"], dst_b)
                h["dq_out"] = _dot(h["do_b"], h["st_prev"].astype(BF16))
                h["dk_out"] = _dot(h["vb"], dst_b)
                h["dst_mm"] = _dot_tn(h["do_b"], h["q_out_b"])
            for h in hs:
                scores = jnp.where(causal, h["scores"], 0.0).astype(BF16)
                dscores = jnp.where(causal, h["dscores"], 0.0).astype(BF16)
                h["dv"] = _dot_tn(scores, h["do_b"]) + h["dv_inter"]
                h["dq_in"] = _dot(dscores, h["k_in_b"])
                h["dk_in"] = _dot_tn(dscores, h["q_in_b"])
            dgn = hs[0]["dgn"]
            for h in hs[1:]:
                dgn = dgn + h["dgn"]
            dgn_ref[...] += dgn
            for hd, (h, ls) in enumerate(zip(hs, lanes)):
                ddec = jnp.sum(h["dst"] * h["st_prev"], axis=0, keepdims=True)
                dst_scr[hd] = h["dst"] * h["dec"] + h["dst_mm"]
                dp_ref[2, rows, ls] = h["dv"].astype(BF16)
                dq = h["dq_in"] * h["e1"] + h["dq_out"] * h["e3"]
                dp_ref[0, rows, ls] = (dq * h["dsilu_q"]).astype(BF16)
                dk_scr[rows, ls] = h["dk_in"] * h["e2"] + h["dk_out"] * h["e4"]
                t_in = h["dq_in"] * h["q_in"] - h["dk_in"] * h["k_in"]
                t_out = h["dk_out"] * h["k_out"]
                da = t_in + h["dq_out"] * h["q_out"] - t_out
                da_ref_row = -jnp.sum(t_in, axis=0, keepdims=True)
                da_last_row = jnp.sum(t_out, axis=0, keepdims=True) + ddec * h["dec"]
                da = da + jnp.where(rowi == CHUNK // 2 - 1, da_ref_row, 0.0) \
                        + jnp.where(rowi == CHUNK - 1, da_last_row, 0.0)
                da_scr[rows, ls] = da
            return carry

        lax.fori_loop(0, ncl, chunk, 0)
        g = min(CUM_ROWS, tr)
        tri = _tri_mask(g, reverse=True)
        for rg in range(tr // g):
            rs = slice(rg * g, (rg + 1) * g)
            dlogf = _tri_apply(tri, da_scr[rs, :])
            df = dlogf / fg[rs, :] - dk_scr[rs, :]
            sgr = sig[rs, :]
            dp_ref[1, rs, :] = (df * (1.0 - lb) * (sgr * (1.0 - sgr))).astype(BF16)
            dlb_ref[...] += jnp.sum(df * (1.0 - sgr), axis=0, keepdims=True) * lb_ref[1:2, :]

    def sec(s):
        return pl.BlockSpec((None, tr, hc), lambda hg, b, t: (s, b * nt + (nt - 1 - t), hg))

    blk = pl.BlockSpec((tr, hc), lambda hg, b, t: (b * nt + (nt - 1 - t), hg))
    return _pc(body, name="hgrn_bwd",
               out_shape=[jax.ShapeDtypeStruct((4, m, di), BF16), jax.ShapeDtypeStruct((1, di), F32),
                          jax.ShapeDtypeStruct((1, HEAD_DIM), F32)],
               grid=(nhg, nb, nt),
               in_specs=[sec(0), sec(1), sec(2), sec(3), blk, blk,
                         pl.BlockSpec((ncl, hpg, HEAD_DIM, HEAD_DIM),
                                      lambda hg, b, t: (b * nt + (nt - 1 - t), hg, 0, 0)),
                         pl.BlockSpec((2, hc), lambda hg, b, t: (0, hg)),
                         pl.BlockSpec((1, HEAD_DIM), lambda hg, b, t: (0, 0))],
               out_specs=[pl.BlockSpec((4, tr, hc), lambda hg, b, t: (0, b * nt + (nt - 1 - t), hg)),
                          pl.BlockSpec((1, hc), lambda hg, b, t: (0, hg)),
                          pl.BlockSpec((1, HEAD_DIM), lambda hg, b, t: (0, 0))],
               scratch=[pltpu.VMEM((hpg, HEAD_DIM, HEAD_DIM), F32)] + [pltpu.VMEM((tr, hc), F32)] * 4,
               sem=("arbitrary", "arbitrary", "arbitrary"), comm=comm)(
                   proj, proj, proj, proj, o_all, dybr, states, lbj, gn)


def _adamw(parts, w, m, v, name, comm=None):
    r, c = w.shape
    tr = _tile(r, 256)
    npart = len(parts)
    c1 = 1.0 - ADAM_B1 ** ADAM_STEP
    c2 = 1.0 - ADAM_B2 ** ADAM_STEP

    def body(*refs):
        p_refs = refs[:npart]
        _adamw_math(p_refs, *refs[npart:], c1, c2)

    blk = pl.BlockSpec((tr, c), lambda i: (i, 0))
    return _pc(body, name=name, out_shape=[jax.ShapeDtypeStruct((r, c), F32)] * 4, grid=(r // tr,),
               in_specs=[blk] * (npart + 3), out_specs=[blk] * 4, sem=("parallel",), comm=comm)(*parts, w, m, v)


def _adamw_math(p_refs, w_ref, m_ref, v_ref, g_ref, d_ref, nm_ref, nv_ref, c1, c2):
    g = p_refs[0][...].astype(F32)
    for p in p_refs[1:]:
        g = g + p[...].astype(F32)
    nm = ADAM_B1 * m_ref[...] + (1.0 - ADAM_B1) * g
    nv = ADAM_B2 * v_ref[...] + (1.0 - ADAM_B2) * (g * g)
    g_ref[...] = g
    nm_ref[...] = nm
    nv_ref[...] = nv
    d_ref[...] = -ADAM_LR * ((nm / c1) / (jnp.sqrt(nv / c2) + ADAM_EPS) + ADAM_WD * w_ref[...])


def _adamw_blocks(parts, idx, w, m, v, name):
    r, c = w.shape
    tr = _tile(r, 256)
    npart = len(parts)
    c1 = 1.0 - ADAM_B1 ** ADAM_STEP
    c2 = 1.0 - ADAM_B2 ** ADAM_STEP

    def body(idx_ref, *refs):
        _adamw_math(refs[:npart], *refs[npart:], c1, c2)

    def sel(p):
        return pl.BlockSpec((None, tr, c), lambda i, s: (s[p], i, 0))

    blk = pl.BlockSpec((tr, c), lambda i, s: (i, 0))
    gs = pltpu.PrefetchScalarGridSpec(num_scalar_prefetch=1, grid=(r // tr,),
                                      in_specs=[sel(p) for p in range(npart)] + [blk] * 3, out_specs=[blk] * 4)
    return _pc(body, name=name, out_shape=[jax.ShapeDtypeStruct((r, c), F32)] * 4, grid_spec=gs,
               sem=("parallel",))(idx, *parts, w, m, v)


_EARLY = ["a_ln_gain", "a_ln_bias", "a_w_s", "a_b_s", "b_lower_bounds", "b_gn_gain"]


def _pack(arrs):
    flat = jnp.concatenate([a.reshape(-1) for a in arrs])
    rows = -(-flat.shape[0] // 1024) * 8
    return jnp.pad(flat, (0, rows * 128 - flat.shape[0])).reshape(rows, 128)


def _unpack(buf, like):
    flat = buf.reshape(-1)
    out, off = [], 0
    for a in like:
        out.append(flat[off:off + a.size].reshape(a.shape))
        off += a.size
    return out


def kernel(x, c, norm_gain, w_ada, b_ada, a_w_in, a_ln_gain, a_ln_bias, a_w_s, a_b_s, a_w_out, b_w_in, b_lower_bounds, b_gn_gain, b_w_out, final_gain, loss_target, m_norm_gain, m_w_ada, m_b_ada, m_a_w_in, m_a_ln_gain, m_a_ln_bias, m_a_w_s, m_a_b_s, m_a_w_out, m_b_w_in, m_b_lower_bounds, m_b_gn_gain, m_b_w_out, m_final_gain, v_norm_gain, v_w_ada, v_b_ada, v_a_w_in, v_a_ln_gain, v_a_ln_bias, v_a_w_s, v_a_b_s, v_a_w_out, v_b_w_in, v_b_lower_bounds, v_b_gn_gain, v_b_w_out, v_final_gain):
    w = dict(norm_gain=norm_gain, w_ada=w_ada, b_ada=b_ada, a_w_in=a_w_in, a_ln_gain=a_ln_gain,
             a_ln_bias=a_ln_bias, a_w_s=a_w_s, a_b_s=a_b_s, a_w_out=a_w_out, b_w_in=b_w_in,
             b_lower_bounds=b_lower_bounds, b_gn_gain=b_gn_gain, b_w_out=b_w_out, final_gain=final_gain)
    mo = dict(norm_gain=m_norm_gain, w_ada=m_w_ada, b_ada=m_b_ada, a_w_in=m_a_w_in, a_ln_gain=m_a_ln_gain,
              a_ln_bias=m_a_ln_bias, a_w_s=m_a_w_s, a_b_s=m_a_b_s, a_w_out=m_a_w_out, b_w_in=m_b_w_in,
              b_lower_bounds=m_b_lower_bounds, b_gn_gain=m_b_gn_gain, b_w_out=m_b_w_out, final_gain=m_final_gain)
    vo = dict(norm_gain=v_norm_gain, w_ada=v_w_ada, b_ada=v_b_ada, a_w_in=v_a_w_in, a_ln_gain=v_a_ln_gain,
              a_ln_bias=v_a_ln_bias, a_w_s=v_a_w_s, a_b_s=v_a_b_s, a_w_out=v_a_w_out, b_w_in=v_b_w_in,
              b_lower_bounds=v_b_lower_bounds, b_gn_gain=v_b_gn_gain, b_w_out=v_b_w_out, final_gain=v_final_gain)

    nb, t_seq, d = x.shape
    m = nb * t_seq
    ncol_ada = w_ada.shape[2]
    xi, yi, ci = lax.axis_index("x"), lax.axis_index("y"), lax.axis_index("c")
    me = 4 * xi + 2 * yi + ci

    c_g, wa_in_g = _all_gather([c, a_w_in[0].astype(BF16)], "gather_c_wa")

    c_all = c_g.reshape(NDEV * nb, d)
    b_cols = lax.dynamic_slice(b_ada, (0, me * ncol_ada), (2, ncol_ada)).reshape(2, 1, ncol_ada)
    mod_part, lbj = _ada_fwd(c_all, w_ada, b_cols, b_lower_bounds)
    mod_all = _all_gather([mod_part], "gather_mod")[0]
    mod_mine = lax.dynamic_slice_in_dim(mod_all, me * nb, nb, axis=2)
    mod_mine = mod_mine.transpose(1, 2, 0, 3).reshape(2, nb, 3, d)
    mod0, mod1 = mod_mine[0], mod_mine[1]

    di = a_w_out.shape[1] * NDEV

    xf = x.reshape(m, d)
    tgt = loss_target.reshape(m, d)
    ng0, ng1 = norm_gain[0:1], norm_gain[1:2]
    ncb = b_w_in.shape[2]
    wb_lo, wb_hi = b_w_in[0][:, :ncb // 2].astype(BF16), b_w_in[0][:, ncb // 2:].astype(BF16)
    h0, h0_t = _prenorm(xf, ng0, mod0, t_seq, "prenorm_a")
    proj_a, half = _mm_in(h0, [wa_in_g], 1, "in_proj_a", comm=_gather_first([a_w_out[0].astype(BF16), wb_lo]))
    bs_t = jnp.pad(a_b_s[0].T, ((0, 0), (0, 128 - SG_GROUPS)))
    ybr_a, (wa_out_g, wb_lo_g, wb_hi_half) = _a_mid_fwd(
        proj_a, a_ln_gain, a_ln_bias, a_w_s[0], bs_t, t_seq, comm=_join(_gather_second(half), _gather_first([wb_hi])))
    wa_out = wa_out_g.reshape(di, d)
    (yout_a, x1), (wb_hi_g, wb_out_half) = _out_proj(
        ybr_a, wa_out, xf, mod0, t_seq, "out_proj_a",
        comm=_join(_gather_second([wb_hi_half]), _gather_first([b_w_out[0].astype(BF16)])))
    wb_in_g = [wb_lo_g, wb_hi_g]
    h1, h1_t = _prenorm(x1, ng1, mod1, t_seq, "prenorm_b")
    proj_b, (wb_out_g,) = _mm_in(h1, wb_in_g, 4, "in_proj_b", comm=_gather_second([wb_out_half]))
    wb_out = wb_out_g.reshape(di, d)
    o_b, ybr_b, states = _hgrn_fwd(proj_b, lbj, b_gn_gain, nb, t_seq)
    yout_b, dx2, loss_part, d_final_gain = _out_proj_loss(ybr_b, wb_out, x1, mod1, final_gain.reshape(1, d), tgt, t_seq)

    rows_out = a_w_out.shape[1]
    dy_b, dgate1, dybr_b = _gate_dybr(dx2, yout_b, mod1, wb_out, t_seq, "dybr_b")
    rs_wb_out = _ReduceScatter(_mm_dw_out(ybr_b, dy_b, "dw_out_b").reshape(NDEV, rows_out, d), "b_w_out")
    (dproj_b, d_lb, d_gn), got = _hgrn_bwd(proj_b, o_b, dybr_b, states, lbj, b_gn_gain, nb, t_seq,
                                           comm=rs_wb_out.swap_core())
    rs_wb_out.after_core(got[0])
    dh1, got = _mm_din(dproj_b, wb_in_g, 4, "dh_b", comm=rs_wb_out.swap_chips())
    rs_wb_out.after_chips(got[0])
    dx1, dss1, dgain1 = _prenorm_bwd(dh1, x1, ng1, mod1, dx2, t_seq, "prenorm_bwd_b")
    rs_wb_in = _ReduceScatter(_mm_dw_in(h1_t, dproj_b, ncb, 4, "dw_in_b"), "b_w_in")

    dy_a, dgate0, dybr_a = _gate_dybr(dx1, yout_a, mod0, wa_out, t_seq, "dybr_a")
    g_wa_out, got = _mm_dw_out(ybr_a, dy_a, "dw_out_a", comm=rs_wb_in.swap_core())
    rs_wb_in.after_core(got[0])
    rs_wa_out = _ReduceScatter(g_wa_out.reshape(NDEV, rows_out, d), "a_w_out")
    (dproj_a, d_lng, d_lnb, d_ws, d_bs_t), got = _a_mid_bwd(
        proj_a, dybr_a, a_ln_gain, a_ln_bias, a_w_s[0], bs_t, t_seq,
        comm=_join(rs_wb_in.swap_chips(), rs_wa_out.swap_core()))
    rs_wb_in.after_chips(got[0])
    rs_wa_out.after_core(got[1])
    part = dict(a_ln_gain=d_lng, a_ln_bias=d_lnb, a_w_s=d_ws[None], a_b_s=d_bs_t[:, :SG_GROUPS].T[None],
                b_lower_bounds=jnp.concatenate([-d_lb, d_lb], axis=0), b_gn_gain=d_gn)
    early_pack = _pack([part[k].reshape(w[k].shape) for k in _EARLY])
    g_wa_in, got = _mm_dw_in(h0_t, dproj_a, wa_in_g.shape[2], 1, "dw_in_a",
                             comm=_join(rs_wa_out.swap_chips(), _gather_first([early_pack])))
    rs_wa_out.after_chips(got[0])
    rs_wa_in = _ReduceScatter(g_wa_in, "a_w_in")
    n_tiles = m // _din_tile(m)
    assert n_tiles >= 2
    first_tiles = max(1, (3 * n_tiles) // 8)
    dh0, got2 = _mm_din(dproj_a, [wa_in_g], 1, "dh_a_first", tiles=(0, first_tiles),
                        comm=_join(rs_wa_in.swap_core(), _gather_second([got[1]])))
    rs_wa_in.after_core(got2[0])
    early_all = got2[1]
    dh0, got = _mm_din(dproj_a, [wa_in_g], 1, "dh_a_rest", comm=rs_wa_in.swap_chips(),
                       tiles=(first_tiles, n_tiles - first_tiles), prev=dh0)
    rs_wa_in.after_chips(got[0])
    dx0, dss0, dgain0 = _prenorm_bwd(dh0, xf, ng0, mod0, dx1, t_seq, "prenorm_bwd_a")
    grad_x = dx0.reshape(nb, t_seq, d)

    dmod = jnp.stack([jnp.concatenate([dss0, dgate0], axis=1), jnp.concatenate([dss1, dgate1], axis=1)])
    late_like = [norm_gain, final_gain, loss_part.reshape(1)]
    late_pack = _pack([jnp.concatenate([dgain0, dgain1], axis=0), d_final_gain[0], loss_part.reshape(1)])
    dmod_all, late_all = _all_gather([dmod.reshape(2, nb, 3 * d), late_pack], "gather_tail")
    dmod_all = dmod_all.transpose(1, 0, 2, 3).reshape(2, NDEV * nb, 3 * d)
    dmod_cols = lax.dynamic_slice_in_dim(dmod_all, me * ncol_ada, ncol_ada, axis=2)
    g_w_ada, g_b_ada = _ada_bwd(c_all, dmod_cols, dmod_all)

    res = {}
    early_like = [w[k] for k in _EARLY]
    dev_order = jnp.arange(NDEV, dtype=jnp.int32)
    sm = _adamw_blocks([early_all] * NDEV, dev_order, _pack(early_like), _pack([mo[k] for k in _EARLY]),
                       _pack([vo[k] for k in _EARLY]), "adamw_small_early")
    sm = [dict(zip(_EARLY, _unpack(buf, early_like))) for buf in sm]
    for k in _EARLY:
        res[k] = tuple(s[k] for s in sm)
    zero = jnp.zeros((1,), F32)
    sm = _adamw_blocks([late_all] * NDEV, dev_order, _pack([norm_gain, final_gain, zero]),
                       _pack([mo["norm_gain"], mo["final_gain"], zero]),
                       _pack([vo["norm_gain"], vo["final_gain"], zero]), "adamw_small_late")
    sm = [_unpack(buf, late_like) for buf in sm]
    res["norm_gain"] = tuple(s[0] for s in sm)
    res["final_gain"] = tuple(s[1] for s in sm)
    loss = sm[0][2][0]
    rb = _adamw([g_b_ada], b_ada, mo["b_ada"], vo["b_ada"], "adamw_b_ada")
    res["b_ada"] = tuple(rb)
    sh = w_ada.shape
    ra = _adamw([g_w_ada.reshape(sh[0] * sh[1], sh[2])], w_ada.reshape(sh[0] * sh[1], sh[2]),
                mo["w_ada"].reshape(sh[0] * sh[1], sh[2]), vo["w_ada"].reshape(sh[0] * sh[1], sh[2]), "adamw_w_ada")
    res["w_ada"] = tuple(z.reshape(sh) for z in ra)

    for k, rs in (("b_w_out", rs_wb_out), ("b_w_in", rs_wb_in), ("a_w_out", rs_wa_out), ("a_w_in", rs_wa_in)):
        res[k] = tuple(z[None] for z in _adamw_blocks(rs.parts, rs.idx, w[k][0], mo[k][0], vo[k][0], "adamw_" + k))

    order = ["norm_gain", "w_ada", "b_ada", "a_w_in", "a_ln_gain", "a_ln_bias", "a_w_s", "a_b_s", "a_w_out",
             "b_w_in", "b_lower_bounds", "b_gn_gain", "b_w_out", "final_gain"]
    return (loss, grad_x, *[res[k][0] for k in order], *[res[k][1] for k in order],
            *[res[k][2] for k in order], *[res[k][3] for k in order])
```

```python
import functools
import math

import jax
import jax.numpy as jnp
from jax import lax
from jax.experimental import pallas as pl
from jax.experimental.pallas import tpu as pltpu

F32 = jnp.float32
BF16 = jnp.bfloat16
MESH = pl.DeviceIdType.MESH
NDEV = 8
EPS = 1e-6
CHUNK = 64
SG_BLOCK = 128
SG_GROUPS = 8
HEAD_DIM = 128
CUM_ROWS = 256
ADAM_LR, ADAM_B1, ADAM_B2, ADAM_EPS, ADAM_WD, ADAM_STEP = 0.001, 0.9, 0.999, 1e-08, 0.01, 10
VMEM_LIMIT = 56 * 1024 * 1024
ANY = pl.BlockSpec(memory_space=pl.ANY)


class _Hosted:
    def __init__(self, arrays, out_shapes, nsem, start, finish, aliases=None):
        self.arrays, self.out_shapes, self.nsem = list(arrays), list(out_shapes), nsem
        self.start, self.finish = start, finish
        self.aliases = dict(aliases or {})


def _join(*comms):
    arrays, outs, aliases, offs, nsem = [], [], {}, [], 0
    for cm in comms:
        offs.append((len(arrays), len(outs), nsem))
        for i, o in cm.aliases.items():
            aliases[len(arrays) + i] = len(outs) + o
        arrays += cm.arrays
        outs += cm.out_shapes
        nsem += cm.nsem

    def run(which):
        def f(ins, outs_, ss, rs, base):
            for cm, (ia, io, isem) in zip(comms, offs):
                getattr(cm, which)(ins[ia:ia + len(cm.arrays)], outs_[io:io + len(cm.out_shapes)], ss, rs, base + isem)
        return f

    return _Hosted(arrays, outs, nsem, run("start"), run("finish"), aliases)


def _pc(body, *, name, out_shape, grid=None, in_specs=None, out_specs=None, scratch=(), sem=None,
        grid_spec=None, comm=None, aliases=None):
    cp = dict(vmem_limit_bytes=VMEM_LIMIT)
    aliases = dict(aliases or {})
    if comm is None:
        if sem is not None:
            cp["dimension_semantics"] = sem
        kw = {"input_output_aliases": aliases}
        if grid_spec is not None:
            kw["grid_spec"] = grid_spec
        else:
            if grid is not None:
                kw["grid"] = grid
            if in_specs is not None:
                kw["in_specs"] = in_specs
            if out_specs is not None:
                kw["out_specs"] = out_specs
            kw["scratch_shapes"] = list(scratch)
        return pl.pallas_call(functools.partial(body), name=name, out_shape=out_shape,
                              compiler_params=pltpu.CompilerParams(**cp), **kw)

    single = not isinstance(out_shape, (list, tuple))
    outs_list = [out_shape] if single else list(out_shape)
    ospecs = [out_specs] if single else list(out_specs)
    n_in, n_out, n_ci, n_co, n_scr = len(in_specs), len(outs_list), len(comm.arrays), len(comm.out_shapes), len(scratch)
    cp["dimension_semantics"] = ("arbitrary",) * len(grid)

    def hosted(*refs):
        cin, hin = refs[:n_in], refs[n_in:n_in + n_ci]
        cout = refs[n_in + n_ci:n_in + n_ci + n_out]
        hout = refs[n_in + n_ci + n_out:n_in + n_ci + n_out + n_co]
        scr = refs[n_in + n_ci + n_out + n_co:n_in + n_ci + n_out + n_co + n_scr]
        ssem, rsem = refs[-2], refs[-1]
        first = functools.reduce(lambda p, q: p & q, [pl.program_id(a) == 0 for a in range(len(grid))])
        last = functools.reduce(lambda p, q: p & q, [pl.program_id(a) == grid[a] - 1 for a in range(len(grid))])

        @pl.when(first)
        def _():
            comm.start(hin, hout, ssem, rsem, 0)

        body(*cin, *cout, *scr)

        @pl.when(last)
        def _():
            comm.finish(hin, hout, ssem, rsem, 0)

    call = pl.pallas_call(
        hosted, name=name, grid=grid, in_specs=list(in_specs) + [ANY] * n_ci, out_specs=ospecs + [ANY] * n_co,
        out_shape=outs_list + comm.out_shapes,
        scratch_shapes=list(scratch) + [pltpu.SemaphoreType.DMA((comm.nsem,)), pltpu.SemaphoreType.DMA((comm.nsem,))],
        input_output_aliases={**aliases, **{n_in + i: n_out + o for i, o in comm.aliases.items()}},
        compiler_params=pltpu.CompilerParams(**cp))

    def run(*args):
        res = call(*args, *comm.arrays)
        comp = res[:n_out]
        return (comp[0] if single else comp), list(res[n_out:])

    return run


def _tile(n, pref):
    return pref if n % pref == 0 else n


def _sigmoid(x):
    return 1.0 / (1.0 + jnp.exp(-x))


def _gelu(x):
    c = math.sqrt(2.0 / math.pi)
    return 0.5 * x * (1.0 + jnp.tanh(c * (x + 0.044715 * (x * x * x))))


def _gelu_and_grad(x):
    c = math.sqrt(2.0 / math.pi)
    x2 = x * x
    t = jnp.tanh(c * (x + 0.044715 * (x2 * x)))
    half = 0.5 * (1.0 + t)
    return x * half, half + (0.5 * x) * (1.0 - t * t) * (c + (3.0 * 0.044715 * c) * x2)


def _dot(a, b):
    return jnp.dot(a, b, preferred_element_type=F32)


def _dot_nt(a, b):
    return lax.dot_general(a, b, (((1,), (1,)), ((), ())), preferred_element_type=F32)


def _dot_tn(a, b):
    return lax.dot_general(a, b, (((0,), (0,)), ((), ())), preferred_element_type=F32)


def _tri_mask(n, reverse):
    r = lax.broadcasted_iota(jnp.int32, (n, n), 0)
    c = lax.broadcasted_iota(jnp.int32, (n, n), 1)
    same = (r // CHUNK) == (c // CHUNK)
    tri = (c >= r) if reverse else (c <= r)
    return jnp.where(same & tri, 1.0, 0.0).astype(BF16)


def _tri_apply(tri, x):
    hi = x.astype(BF16)
    r1 = x - hi.astype(F32)
    mid = r1.astype(BF16)
    lo = (r1 - mid.astype(F32)).astype(BF16)
    return _dot(tri, hi) + (_dot(tri, mid) + _dot(tri, lo))


def _all_gather(arrs, name):
    n = len(arrs)

    def body(*refs):
        ins, outs = refs[:n], refs[n:2 * n]
        send_sems, recv_sems, local_sems = refs[2 * n:]
        x, y, c = lax.axis_index("x"), lax.axis_index("y"), lax.axis_index("c")
        me, sibling = (x, y, c), (x, y, 1 - c)
        near = (x + c - 2 * x * c, y + (1 - c) - 2 * y * (1 - c))
        far = (x + (1 - c) - 2 * x * (1 - c), y + c - 2 * y * c)
        diag = (1 - x, 1 - y)

        def blk(a, p):
            return outs[a].at[4 * p[0] + 2 * p[1] + p[2]]

        def copy(a, k, block, to, src=None):
            return pltpu.make_async_remote_copy(
                src_ref=blk(a, block) if src is None else src, dst_ref=blk(a, block),
                send_sem=send_sems.at[7 * a + k], recv_sem=recv_sems.at[7 * a + k],
                device_id=to, device_id_type=MESH)

        mine = [pltpu.make_async_copy(ins[a], blk(a, me), local_sems.at[a]) for a in range(n)]
        for m in mine:
            m.start()
        sends = []
        for a in range(n):
            sends += [copy(a, 0, me, sibling, src=ins[a]), copy(a, 1, me, (*near, c), src=ins[a]),
                      copy(a, 2, me, (*far, c), src=ins[a])]
        for cp in sends:
            cp.start()
        for a in range(n):
            copy(a, 1, (*near, c), me).wait_recv()
            sends.append(copy(a, 3, (*near, c), (*far, c)))
            sends[-1].start()
        for a in range(n):
            sends.append(copy(a, 4, (*near, c), sibling))
            sends[-1].start()
            copy(a, 2, (*far, c), me).wait_recv()
            sends.append(copy(a, 5, (*far, c), sibling))
            sends[-1].start()
        for a in range(n):
            copy(a, 3, (*diag, c), me).wait_recv()
            sends.append(copy(a, 6, (*diag, c), sibling))
            sends[-1].start()
        for a in range(n):
            copy(a, 0, sibling, me).wait_recv()
            copy(a, 4, (*far, 1 - c), me).wait_recv()
            copy(a, 5, (*near, 1 - c), me).wait_recv()
            copy(a, 6, (*diag, 1 - c), me).wait_recv()
        for cp in sends:
            cp.wait_send()
        for m in mine:
            m.wait()

    out_shape = [jax.ShapeDtypeStruct((NDEV,) + a.shape, a.dtype) for a in arrs]
    return _pc(body, name=name, out_shape=out_shape, in_specs=[ANY] * n, out_specs=[ANY] * n,
               scratch=[pltpu.SemaphoreType.DMA((7 * n,)), pltpu.SemaphoreType.DMA((7 * n,)),
                        pltpu.SemaphoreType.DMA((n,))])(*arrs)


def _gather_first(arrs):
    n = len(arrs)

    def parts(ins, outs, ss, rs, base):
        x, y, c = lax.axis_index("x"), lax.axis_index("y"), lax.axis_index("c")
        me, sibling = (x, y, c), (x, y, 1 - c)
        chips = [(1 - x, y), (x, 1 - y), (1 - x, 1 - y)]

        def blk(a, p):
            return outs[a].at[4 * p[0] + 2 * p[1] + p[2]]

        def copy(a, k, block, to):
            return pltpu.make_async_remote_copy(
                src_ref=ins[a], dst_ref=blk(a, block), send_sem=ss.at[base + 4 * a + k],
                recv_sem=rs.at[base + 4 * a + k], device_id=to, device_id_type=MESH)

        local = [pltpu.make_async_copy(ins[a], blk(a, me), ss.at[base + 4 * n + a]) for a in range(n)]
        sends, recvs = [], []
        for a in range(n):
            sends.append(copy(a, 0, me, sibling))
            recvs.append(copy(a, 0, sibling, me))
            for j, chip in enumerate(chips):
                sends.append(copy(a, 1 + j, me, (*chip, c)))
                recvs.append(copy(a, 1 + j, (*chip, c), me))
        return local, sends, recvs

    def start(ins, outs, ss, rs, base):
        local, sends, _ = parts(ins, outs, ss, rs, base)
        for cp in local + sends:
            cp.start()

    def finish(ins, outs, ss, rs, base):
        local, sends, recvs = parts(ins, outs, ss, rs, base)
        for cp in recvs:
            cp.wait_recv()
        for cp in sends:
            cp.wait_send()
        for cp in local:
            cp.wait()

    return _Hosted(arrs, [jax.ShapeDtypeStruct((NDEV,) + a.shape, a.dtype) for a in arrs], 5 * n, start, finish)


def _gather_second(bufs):
    n = len(bufs)

    def parts(ins, outs, ss, rs, base):
        x, y, c = lax.axis_index("x"), lax.axis_index("y"), lax.axis_index("c")
        sibling = (x, y, 1 - c)
        chips = [(1 - x, y), (x, 1 - y), (1 - x, 1 - y)]
        sends, recvs = [], []
        for a in range(n):
            for j, chip in enumerate(chips):
                mine = 4 * chip[0] + 2 * chip[1] + c
                theirs = 4 * chip[0] + 2 * chip[1] + (1 - c)
                sends.append(pltpu.make_async_remote_copy(
                    src_ref=ins[a].at[mine], dst_ref=outs[a].at[mine], send_sem=ss.at[base + 3 * a + j],
                    recv_sem=rs.at[base + 3 * a + j], device_id=sibling, device_id_type=MESH))
                recvs.append(pltpu.make_async_remote_copy(
                    src_ref=ins[a].at[theirs], dst_ref=outs[a].at[theirs], send_sem=ss.at[base + 3 * a + j],
                    recv_sem=rs.at[base + 3 * a + j], device_id=sibling, device_id_type=MESH))
        return sends, recvs

    def start(ins, outs, ss, rs, base):
        for cp in parts(ins, outs, ss, rs, base)[0]:
            cp.start()

    def finish(ins, outs, ss, rs, base):
        sends, recvs = parts(ins, outs, ss, rs, base)
        for cp in recvs:
            cp.wait_recv()
        for cp in sends:
            cp.wait_send()

    return _Hosted(bufs, [jax.ShapeDtypeStruct(b.shape, b.dtype) for b in bufs], 3 * n, start, finish,
                   aliases={a: a for a in range(n)})


def _swap(src, nblk, ids_fn, partner_fn):
    def copies(ins, outs, ss, rs, base):
        x, y, c = lax.axis_index("x"), lax.axis_index("y"), lax.axis_index("c")
        ids = ids_fn(x, y, c)
        partner = partner_fn(x, y, c)
        return [pltpu.make_async_remote_copy(
            src_ref=ins[0].at[ids[k]], dst_ref=outs[0].at[k], send_sem=ss.at[base + k], recv_sem=rs.at[base + k],
            device_id=partner, device_id_type=MESH) for k in range(nblk)]

    def start(ins, outs, ss, rs, base):
        for cp in copies(ins, outs, ss, rs, base):
            cp.start()

    def finish(ins, outs, ss, rs, base):
        for cp in copies(ins, outs, ss, rs, base):
            cp.wait()

    return _Hosted([src], [jax.ShapeDtypeStruct((nblk,) + src.shape[1:], src.dtype)], nblk, start, finish)


def _blocking(comm, name):
    n_i, n_o = len(comm.arrays), len(comm.out_shapes)

    def body(*refs):
        ins, outs = refs[:n_i], refs[n_i:n_i + n_o]
        comm.start(ins, outs, refs[-2], refs[-1], 0)
        comm.finish(ins, outs, refs[-2], refs[-1], 0)

    return pl.pallas_call(
        body, name=name, out_shape=comm.out_shapes, in_specs=[ANY] * n_i, out_specs=[ANY] * n_o,
        scratch_shapes=[pltpu.SemaphoreType.DMA((comm.nsem,)), pltpu.SemaphoreType.DMA((comm.nsem,))],
        input_output_aliases=comm.aliases)(*comm.arrays)


def _swap_chips(send):
    def copies(ins, outs, ss, rs, base):
        x, y, c = lax.axis_index("x"), lax.axis_index("y"), lax.axis_index("c")
        chips = [(1 - x, y), (x, 1 - y), (1 - x, 1 - y)]
        return [pltpu.make_async_remote_copy(
            src_ref=ins[0].at[j], dst_ref=outs[0].at[j], send_sem=ss.at[base + j], recv_sem=rs.at[base + j],
            device_id=(*chip, c), device_id_type=MESH) for j, chip in enumerate(chips)]

    def start(ins, outs, ss, rs, base):
        for cp in copies(ins, outs, ss, rs, base):
            cp.start()

    def finish(ins, outs, ss, rs, base):
        for cp in copies(ins, outs, ss, rs, base):
            cp.wait()

    return _Hosted([send], [jax.ShapeDtypeStruct(send.shape, send.dtype)], 3, start, finish)


def _add_send(a, b, idx, ns, name):
    _, r, c = a.shape
    tr = _tile(r, 256)

    def body(idx_ref, a_ref, b_ref, send_ref):
        send_ref[...] = (a_ref[...] + b_ref[...]).astype(BF16)

    def sel(off):
        return pl.BlockSpec((None, tr, c), lambda k, i, s: (s[off + k], i, 0))

    gs = pltpu.PrefetchScalarGridSpec(num_scalar_prefetch=1, grid=(ns, r // tr), in_specs=[sel(0), sel(ns)],
                                      out_specs=pl.BlockSpec((None, tr, c), lambda k, i, s: (k, i, 0)))
    return _pc(body, name=name, grid_spec=gs, sem=("arbitrary", "arbitrary"),
               out_shape=jax.ShapeDtypeStruct((ns, r, c), BF16))(idx, a, b)


class _ReduceScatter:
    def __init__(self, g, tag):
        self.g, self.tag = g, tag

    def swap_core(self):
        return _swap(self.g, 4, lambda x, y, c: [1 - c, 3 - c, 5 - c, 7 - c], lambda x, y, c: (x, y, 1 - c))

    def after_core(self, recv):
        x, y, c = lax.axis_index("x"), lax.axis_index("y"), lax.axis_index("c")
        chips = [(1 - x, y), (x, 1 - y), (1 - x, 1 - y)]
        idx = jnp.stack([4 * p + 2 * q + c for p, q in chips] + [2 * p + q for p, q in chips]).astype(jnp.int32)
        self.send = _add_send(self.g, recv, idx, 3, "rs_add_" + self.tag)
        self.recv_core = recv
        zero = jnp.zeros((), jnp.int32)
        self.idx = jnp.stack([4 * x + 2 * y + c, 2 * x + y, zero, zero + 1, zero + 2]).astype(jnp.int32)

    def swap_chips(self):
        return _swap_chips(self.send)

    def after_chips(self, recv):
        self.parts = [self.g, self.recv_core, recv, recv, recv]


def _ada_fwd(c_all, w_ada, b_cols, b_lb):
    nl, d, ncol = w_ada.shape
    nseq = c_all.shape[0]
    di = b_lb.shape[1]

    def body(c_ref, w_ref, b_ref, lb_ref, mod_ref, lbj_ref):
        cv = c_ref[...]
        cact = (cv * _sigmoid(cv)).astype(BF16)
        for l in range(nl):
            mod_ref[l] = _dot(cact, w_ref[l].astype(BF16)) + b_ref[l]
        b0, b1 = lb_ref[0:1, :], lb_ref[1:2, :]
        mx = jnp.maximum(b0, b1)
        e0, e1 = jnp.exp(b0 - mx), jnp.exp(b1 - mx)
        s = e0 + e1
        p0, p1 = e0 / s, e1 / s
        lbj_ref[0:1, :] = (p0 + p1) - p0
        lbj_ref[1:2, :] = p0 * p1

    return _pc(body, name="ada_fwd",
               out_shape=[jax.ShapeDtypeStruct((nl, nseq, ncol), F32), jax.ShapeDtypeStruct((2, di), F32)]
               )(c_all, w_ada, b_cols, b_lb)


def _ada_bwd(c_all, dmod_cols, dmod_full):
    nl, nseq, ncol = dmod_cols.shape
    d = c_all.shape[1]
    d3 = dmod_full.shape[2]

    def body(c_ref, dc_ref, df_ref, gw_ref, gb_ref):
        cv = c_ref[...]
        cact = (cv * _sigmoid(cv)).astype(BF16)
        for l in range(nl):
            gw_ref[l] = _dot_tn(cact, dc_ref[l].astype(BF16))
            gb_ref[l:l + 1, :] = jnp.sum(df_ref[l], axis=0, keepdims=True)

    return _pc(body, name="ada_bwd",
               out_shape=[jax.ShapeDtypeStruct((nl, d, ncol), F32), jax.ShapeDtypeStruct((nl, d3), F32)]
               )(c_all, dmod_cols, dmod_full)


def _prenorm(x, gain, mod, t_seq, name, comm=None):
    m, d = x.shape
    tm = _tile(t_seq, 1024)
    per = t_seq // tm

    def body(x_ref, g_ref, mod_ref, h_ref, ht_ref):
        xv = x_ref[...]
        rstd = lax.rsqrt(jnp.mean(xv * xv, axis=-1, keepdims=True) + EPS)
        r = xv * rstd * g_ref[...]
        h = r * (1.0 + mod_ref[0, 1:2, :]) + mod_ref[0, 0:1, :]
        h_ref[...] = h.astype(BF16)
        ht_ref[...] = h.T.astype(BF16)

    return _pc(body, name=name, out_shape=[jax.ShapeDtypeStruct((m, d), BF16), jax.ShapeDtypeStruct((d, m), BF16)],
               grid=(m // tm,),
               in_specs=[pl.BlockSpec((tm, d), lambda i: (i, 0)), pl.BlockSpec((1, d), lambda i: (0, 0)),
                         pl.BlockSpec((1, 3, d), lambda i: (i // per, 0, 0))],
               out_specs=[pl.BlockSpec((tm, d), lambda i: (i, 0)), pl.BlockSpec((d, tm), lambda i: (0, i))],
               sem=("parallel",), comm=comm)(x, gain, mod)


def _prenorm_bwd(dh, x, gain, mod, dxn, t_seq, name, comm=None):
    m, d = x.shape
    nb = m // t_seq
    tm = _tile(t_seq, 1024)
    per = t_seq // tm

    def body(dh_ref, x_ref, g_ref, mod_ref, dxn_ref, dx_ref, dss_ref, dg_ref):
        i = pl.program_id(0)
        xv, dhv, g = x_ref[...], dh_ref[...], g_ref[...]
        rstd = lax.rsqrt(jnp.mean(xv * xv, axis=-1, keepdims=True) + EPS)
        xhat = xv * rstd
        dr = dhv * (1.0 + mod_ref[0, 1:2, :])
        dxhat = dr * g
        dx_ref[...] = dxn_ref[...] + rstd * (dxhat - xhat * jnp.mean(dxhat * xhat, axis=-1, keepdims=True))

        @pl.when(i % per == 0)
        def _():
            dss_ref[...] = jnp.zeros_like(dss_ref)

        @pl.when(i == 0)
        def _():
            dg_ref[...] = jnp.zeros_like(dg_ref)

        dss_ref[0, 0:1, :] += jnp.sum(dhv, axis=0, keepdims=True)
        dss_ref[0, 1:2, :] += jnp.sum(dhv * (xhat * g), axis=0, keepdims=True)
        dg_ref[...] += jnp.sum(dr * xhat, axis=0, keepdims=True)

    row = pl.BlockSpec((tm, d), lambda i: (i, 0))
    return _pc(body, name=name,
               out_shape=[jax.ShapeDtypeStruct((m, d), F32), jax.ShapeDtypeStruct((nb, 2, d), F32),
                          jax.ShapeDtypeStruct((1, d), F32)],
               grid=(m // tm,),
               in_specs=[row, row, pl.BlockSpec((1, d), lambda i: (0, 0)),
                         pl.BlockSpec((1, 3, d), lambda i: (i // per, 0, 0)), row],
               out_specs=[row, pl.BlockSpec((1, 2, d), lambda i: (i // per, 0, 0)),
                          pl.BlockSpec((1, d), lambda i: (0, 0))],
               sem=("arbitrary",), comm=comm)(dh, x, gain, mod, dxn)


def _mm_in(h, ws, sections, name, comm=None):
    m, k = h.shape
    nw = len(ws)
    widths = [w.shape[2] for w in ws]
    offs = [sum(widths[:a]) for a in range(nw)]
    nc = sum(widths)
    per = NDEV // sections if sections > 1 else NDEV
    tm = _din_tile(m)
    assert per % 2 == 0

    def body(*refs):
        hv = refs[0][...]
        o_ref = refs[1 + nw]
        for b in range(2):
            for a in range(nw):
                lo = b * nc + offs[a]
                o_ref[:, lo:lo + widths[a]] = _dot(hv, refs[1 + a][b])

    w_specs = [pl.BlockSpec((2, k, wd), lambda j, i: (j, 0, 0)) for wd in widths]
    if sections > 1:
        out_shape = jax.ShapeDtypeStruct((sections, m, per * nc), F32)
        out_spec = pl.BlockSpec((None, tm, 2 * nc), lambda j, i: ((2 * j) // per, i, ((2 * j) % per) // 2))
    else:
        out_shape = jax.ShapeDtypeStruct((m, NDEV * nc), F32)
        out_spec = pl.BlockSpec((tm, 2 * nc), lambda j, i: (i, j))
    return _pc(body, name=name, out_shape=out_shape, grid=(NDEV // 2, m // tm),
               in_specs=[pl.BlockSpec((tm, k), lambda j, i: (i, 0))] + w_specs,
               out_specs=out_spec, sem=("parallel", "parallel"), comm=comm)(h, *ws)


def _din_tile(m):
    return 1024 if m % 1024 == 0 and m >= 2048 else _tile(m, 512)


def _mm_din(dproj, ws, sections, name, comm=None, tiles=None, prev=None):
    nw, k = len(ws), ws[0].shape[1]
    widths = [w.shape[2] for w in ws]
    offs = [sum(widths[:a]) for a in range(nw)]
    nc = sum(widths)
    m = dproj.shape[-2]
    tm = _din_tile(m)
    t0, nt = tiles if tiles is not None else (0, m // tm)
    per = NDEV // sections if sections > 1 else NDEV
    assert per % 2 == 0

    def body(*refs):
        d_ref, o_ref = refs[0], refs[-1]
        j = pl.program_id(1)
        acc = None
        for b in range(2):
            for a in range(nw):
                lo = b * nc + offs[a]
                term = _dot_nt(d_ref[:, lo:lo + widths[a]], refs[1 + a][b])
                acc = term if acc is None else acc + term

        @pl.when(j == 0)
        def _():
            o_ref[...] = acc

        @pl.when(j > 0)
        def _():
            o_ref[...] += acc

    if sections > 1:
        dspec = pl.BlockSpec((None, tm, 2 * nc), lambda i, j: ((2 * j) // per, i + t0, ((2 * j) % per) // 2))
    else:
        dspec = pl.BlockSpec((tm, 2 * nc), lambda i, j: (i + t0, j))
    in_specs = [dspec] + [pl.BlockSpec((2, k, wd), lambda i, j: (j, 0, 0)) for wd in widths]
    args = [dproj, *ws]
    if prev is not None:
        in_specs.append(ANY)
        args.append(prev)
    return _pc(body, name=name, out_shape=jax.ShapeDtypeStruct((m, k), F32), grid=(nt, NDEV // 2), in_specs=in_specs,
               out_specs=pl.BlockSpec((tm, k), lambda i, j: (i + t0, 0)), sem=("parallel", "arbitrary"),
               comm=comm, aliases={1 + nw: 0} if prev is not None else None)(*args)


def _mm_dw_in(ht, dproj, nc, sections, name, comm=None):
    k, m = ht.shape
    tk = 2048 if m % 2048 == 0 else _din_tile(m)
    per = NDEV // sections if sections > 1 else NDEV

    def body(h_ref, d_ref, o_ref):
        kk = pl.program_id(1)
        acc = _dot(h_ref[...], d_ref[...])

        @pl.when(kk == 0)
        def _():
            o_ref[...] = acc

        @pl.when(kk > 0)
        def _():
            o_ref[...] += acc

    if sections > 1:
        dspec = pl.BlockSpec((None, tk, nc), lambda j, i: (j // per, i, j % per))
    else:
        dspec = pl.BlockSpec((tk, nc), lambda j, i: (i, j))
    return _pc(body, name=name, out_shape=jax.ShapeDtypeStruct((NDEV, k, nc), F32), grid=(NDEV, m // tk),
               in_specs=[pl.BlockSpec((k, tk), lambda j, i: (0, i)), dspec],
               out_specs=pl.BlockSpec((None, k, nc), lambda j, i: (j, 0, 0)),
               sem=("parallel", "arbitrary"), comm=comm)(ht, dproj)


def _out_proj(ybr, w_out, x, mod, t_seq, name, comm=None):
    m, di = ybr.shape
    d = w_out.shape[1]
    tm = _tile(t_seq, 512)
    per = t_seq // tm

    def body(y_ref, w_ref, x_ref, mod_ref, yo_ref, xn_ref):
        yo = _dot(y_ref[...], w_ref[...])
        yo_ref[...] = yo
        xn_ref[...] = x_ref[...] + mod_ref[0, 2:3, :] * yo

    row = pl.BlockSpec((tm, d), lambda i: (i, 0))
    return _pc(body, name=name,
               out_shape=[jax.ShapeDtypeStruct((m, d), F32), jax.ShapeDtypeStruct((m, d), F32)],
               grid=(m // tm,),
               in_specs=[pl.BlockSpec((tm, di), lambda i: (i, 0)), pl.BlockSpec((di, d), lambda i: (0, 0)), row,
                         pl.BlockSpec((1, 3, d), lambda i: (i // per, 0, 0))],
               out_specs=[row, row], sem=("parallel",), comm=comm)(ybr, w_out, x, mod)


def _out_proj_loss(ybr, w_out, x, mod, gain, target, t_seq):
    m, di = ybr.shape
    d = w_out.shape[1]
    tm = _tile(t_seq, 512)
    per = t_seq // tm

    def body(y_ref, w_ref, x_ref, mod_ref, g_ref, t_ref, yo_ref, dx_ref, loss_ref, dg_ref):
        i = pl.program_id(0)
        yo = _dot(y_ref[...], w_ref[...])
        yo_ref[...] = yo
        xv = x_ref[...] + mod_ref[0, 2:3, :] * yo
        g = g_ref[...]
        rstd = lax.rsqrt(jnp.mean(xv * xv, axis=-1, keepdims=True) + EPS)
        xhat = xv * rstd
        err = xhat * g - t_ref[...]
        dy = err * (1.0 / d)
        dxhat = dy * g
        dx_ref[...] = rstd * (dxhat - xhat * jnp.mean(dxhat * xhat, axis=-1, keepdims=True))

        @pl.when(i == 0)
        def _():
            loss_ref[...] = jnp.zeros_like(loss_ref)
            dg_ref[...] = jnp.zeros_like(dg_ref)

        loss_ref[...] += 0.5 * jnp.sum(jnp.mean(err * err, axis=-1, keepdims=True), axis=0, keepdims=True)
        dg_ref[...] += jnp.sum(dy * xhat, axis=0, keepdims=True)

    row = pl.BlockSpec((tm, d), lambda i: (i, 0))
    vec = pl.BlockSpec((1, d), lambda i: (0, 0))
    return _pc(body, name="out_proj_loss",
               out_shape=[jax.ShapeDtypeStruct((m, d), F32), jax.ShapeDtypeStruct((m, d), F32),
                          jax.ShapeDtypeStruct((1, 1), F32), jax.ShapeDtypeStruct((1, d), F32)],
               grid=(m // tm,),
               in_specs=[pl.BlockSpec((tm, di), lambda i: (i, 0)), pl.BlockSpec((di, d), lambda i: (0, 0)), row,
                         pl.BlockSpec((1, 3, d), lambda i: (i // per, 0, 0)), vec, row],
               out_specs=[row, row, pl.BlockSpec((1, 1), lambda i: (0, 0)), vec],
               sem=("arbitrary",))(ybr, w_out, x, mod, gain, target)


def _gate_dybr(dxn, yout, mod, w_out, t_seq, name):
    m, d = dxn.shape
    di = w_out.shape[0]
    nb = m // t_seq
    tm = _tile(t_seq, 512)
    per = t_seq // tm

    def body(dxn_ref, yo_ref, mod_ref, w_ref, dy_ref, dgate_ref, o_ref):
        i = pl.program_id(0)
        dv = dxn_ref[...]
        dy = (mod_ref[0, 2:3, :] * dv).astype(BF16)
        dy_ref[...] = dy
        o_ref[...] = _dot_nt(dy, w_ref[...])

        @pl.when(i % per == 0)
        def _():
            dgate_ref[...] = jnp.zeros_like(dgate_ref)

        dgate_ref[0] += jnp.sum(dv * yo_ref[...], axis=0, keepdims=True)

    row = pl.BlockSpec((tm, d), lambda i: (i, 0))
    return _pc(body, name=name,
               out_shape=[jax.ShapeDtypeStruct((m, d), BF16), jax.ShapeDtypeStruct((nb, 1, d), F32),
                          jax.ShapeDtypeStruct((m, di), F32)],
               grid=(m // tm,),
               in_specs=[row, row, pl.BlockSpec((1, 3, d), lambda i: (i // per, 0, 0)),
                         pl.BlockSpec((di, d), lambda i: (0, 0))],
               out_specs=[row, pl.BlockSpec((1, 1, d), lambda i: (i // per, 0, 0)),
                          pl.BlockSpec((tm, di), lambda i: (i, 0))],
               sem=("arbitrary",))(dxn, yout, mod, w_out)


def _mm_dw_out(ybr, dy, name, comm=None):
    m, di = ybr.shape
    d = dy.shape[1]
    tk = _tile(m, 512)
    tn = _tile(di, 1024)

    def body(y_ref, dy_ref, o_ref):
        kk = pl.program_id(1)
        acc = _dot_tn(y_ref[...], dy_ref[...])

        @pl.when(kk == 0)
        def _():
            o_ref[...] = acc

        @pl.when(kk > 0)
        def _():
            o_ref[...] += acc

    return _pc(body, name=name, out_shape=jax.ShapeDtypeStruct((di, d), F32), grid=(di // tn, m // tk),
               in_specs=[pl.BlockSpec((tk, tn), lambda n, k: (k, n)), pl.BlockSpec((tk, d), lambda n, k: (k, 0))],
               out_specs=pl.BlockSpec((tn, d), lambda n, k: (n, 0)), sem=("parallel", "arbitrary"),
               comm=comm)(ybr, dy)


def _sgu_mask():
    t = lax.broadcasted_iota(jnp.int32, (SG_BLOCK, SG_BLOCK), 0)
    s = lax.broadcasted_iota(jnp.int32, (SG_BLOCK, SG_BLOCK), 1)
    return (s // CHUNK) <= (t // CHUNK)


def _a_mid_fwd(proj, ln_g, ln_b, w_s, bs_t, t_seq, comm=None):
    m, n3 = proj.shape
    di = n3 // 3
    gd = di // SG_GROUPS
    r = _tile(t_seq, 256)
    nblk = r // SG_BLOCK

    def body(p_ref, lg_ref, lb_ref, ws_ref, bs_ref, ybr_ref, s_scr):
        v = _gelu(p_ref[:, di:2 * di])
        mu = jnp.mean(v, axis=-1, keepdims=True)
        vc = v - mu
        rstd = lax.rsqrt(jnp.mean(vc * vc, axis=-1, keepdims=True) + EPS)
        vb = (vc * rstd * lg_ref[...] + lb_ref[...]).astype(BF16)
        mask = _sgu_mask()
        for gi in range(SG_GROUPS):
            ws = jnp.where(mask, ws_ref[gi], 0.0).astype(BF16)
            bcol = bs_ref[:, gi:gi + 1]
            for b in range(nblk):
                rows = slice(b * SG_BLOCK, (b + 1) * SG_BLOCK)
                cols = slice(gi * gd, (gi + 1) * gd)
                s_scr[rows, cols] = _dot(ws, vb[rows, cols]) + bcol
        gg = p_ref[:, 2 * di:]
        ybr_ref[...] = (_gelu(p_ref[:, :di]) * s_scr[...] * (gg * _sigmoid(gg))).astype(BF16)

    vec = pl.BlockSpec((1, di), lambda i: (0, 0))
    return _pc(body, name="a_mid_fwd", out_shape=jax.ShapeDtypeStruct((m, di), BF16), grid=(m // r,),
               in_specs=[pl.BlockSpec((r, n3), lambda i: (i, 0)), vec, vec,
                         pl.BlockSpec((SG_GROUPS, SG_BLOCK, SG_BLOCK), lambda i: (0, 0, 0)),
                         pl.BlockSpec((SG_BLOCK, 128), lambda i: (0, 0))],
               out_specs=pl.BlockSpec((r, di), lambda i: (i, 0)),
               scratch=[pltpu.VMEM((r, di), F32)], sem=("parallel",), comm=comm)(proj, ln_g, ln_b, w_s, bs_t)


def _a_mid_bwd(proj, dybr, ln_g, ln_b, w_s, bs_t, t_seq, comm=None):
    m, n3 = proj.shape
    di = n3 // 3
    gd = di // SG_GROUPS
    r = _tile(t_seq, 256)
    nblk = r // SG_BLOCK

    def body(p_ref, dy_ref, lg_ref, lb_ref, ws_ref, bs_ref,
             dp_ref, dlg_ref, dlb_ref, dws_ref, dbs_ref, s_scr, dvl_scr):
        i = pl.program_id(0)

        @pl.when(i == 0)
        def _():
            dlg_ref[...] = jnp.zeros_like(dlg_ref)
            dlb_ref[...] = jnp.zeros_like(dlb_ref)
            dws_ref[...] = jnp.zeros_like(dws_ref)
            dbs_ref[...] = jnp.zeros_like(dbs_ref)

        v, dgelu_v = _gelu_and_grad(p_ref[:, di:2 * di])
        mu = jnp.mean(v, axis=-1, keepdims=True)
        vc = v - mu
        rstd = lax.rsqrt(jnp.mean(vc * vc, axis=-1, keepdims=True) + EPS)
        vhat = vc * rstd
        lg = lg_ref[...]
        vb = (vhat * lg + lb_ref[...]).astype(BF16)
        u, dgelu_u = _gelu_and_grad(p_ref[:, :di])
        gg = p_ref[:, 2 * di:]
        sg = _sigmoid(gg)
        dyv = dy_ref[...]
        dus = dyv * (gg * sg)
        dsb = (dus * u).astype(BF16)
        ds32 = dus * u
        mask = _sgu_mask()
        lane = lax.broadcasted_iota(jnp.int32, (SG_BLOCK, 128), 1)
        dbs_acc = jnp.zeros((SG_BLOCK, 128), F32)
        for gi in range(SG_GROUPS):
            ws = jnp.where(mask, ws_ref[gi], 0.0).astype(BF16)
            bcol = bs_ref[:, gi:gi + 1]
            cols = slice(gi * gd, (gi + 1) * gd)
            dws_acc = jnp.zeros((SG_BLOCK, SG_BLOCK), F32)
            dbs_col = jnp.zeros((SG_BLOCK, 1), F32)
            for b in range(nblk):
                rows = slice(b * SG_BLOCK, (b + 1) * SG_BLOCK)
                s_scr[rows, cols] = _dot(ws, vb[rows, cols]) + bcol
                dvl_scr[rows, cols] = _dot_tn(ws, dsb[rows, cols])
                dws_acc += _dot_nt(dsb[rows, cols], vb[rows, cols])
                dbs_col += jnp.sum(ds32[rows, cols], axis=-1, keepdims=True)
            dws_ref[gi] += jnp.where(mask, dws_acc, 0.0)
            dbs_acc += jnp.where(lane == gi, dbs_col, 0.0)
        dbs_ref[...] += dbs_acc
        s = s_scr[...]
        dp_ref[:, :di] = (dus * s * dgelu_u).astype(BF16)
        dp_ref[:, 2 * di:] = (dyv * u * s * (sg * (1.0 + gg * (1.0 - sg)))).astype(BF16)
        dvl = dvl_scr[...]
        dlg_ref[...] += jnp.sum(dvl * vhat, axis=0, keepdims=True)
        dlb_ref[...] += jnp.sum(dvl, axis=0, keepdims=True)
        dvh = dvl * lg
        dv = rstd * (dvh - jnp.mean(dvh, axis=-1, keepdims=True)
                     - vhat * jnp.mean(dvh * vhat, axis=-1, keepdims=True))
        dp_ref[:, di:2 * di] = (dv * dgelu_v).astype(BF16)

    vec = pl.BlockSpec((1, di), lambda i: (0, 0))
    wsb = pl.BlockSpec((SG_GROUPS, SG_BLOCK, SG_BLOCK), lambda i: (0, 0, 0))
    bsb = pl.BlockSpec((SG_BLOCK, 128), lambda i: (0, 0))
    return _pc(body, name="a_mid_bwd",
               out_shape=[jax.ShapeDtypeStruct((m, n3), BF16), jax.ShapeDtypeStruct((1, di), F32),
                          jax.ShapeDtypeStruct((1, di), F32),
                          jax.ShapeDtypeStruct((SG_GROUPS, SG_BLOCK, SG_BLOCK), F32),
                          jax.ShapeDtypeStruct((SG_BLOCK, 128), F32)],
               grid=(m // r,),
               in_specs=[pl.BlockSpec((r, n3), lambda i: (i, 0)), pl.BlockSpec((r, di), lambda i: (i, 0)),
                         vec, vec, wsb, bsb],
               out_specs=[pl.BlockSpec((r, n3), lambda i: (i, 0)), vec, vec, wsb, bsb],
               scratch=[pltpu.VMEM((r, di), F32), pltpu.VMEM((r, di), F32)],
               sem=("arbitrary",), comm=comm)(proj, dybr, ln_g, ln_b, w_s, bs_t)


def _hgrn_dims(t_seq, di):
    tr = _tile(t_seq, 256)
    hc = _tile(di, 1024)
    return tr, hc, hc // HEAD_DIM


def _hgrn_gates(f_ref, lb, a_scr, k_scr, tr):
    sig = _sigmoid(f_ref[...])
    fg = lb + (1.0 - lb) * sig
    k_scr[...] = 1.0 - fg
    logf = jnp.log(fg)
    g = min(CUM_ROWS, tr)
    tri = _tri_mask(g, reverse=False)
    for rg in range(tr // g):
        a_scr[rg * g:(rg + 1) * g, :] = _tri_apply(tri, logf[rg * g:(rg + 1) * g, :])
    return sig, fg


def _hgrn_fwd(proj, lbj, gn, nb, t_seq, comm=None):
    _, m, di = proj.shape
    tr, hc, hpg = _hgrn_dims(t_seq, di)
    nt, nhg, ncl = t_seq // tr, di // hc, tr // CHUNK
    nheads = di // HEAD_DIM

    def body(q_ref, f_ref, i_ref, g_ref, lb_ref, gn_ref, o_ref, ybr_ref, st_ref, st_scr, a_scr, k_scr):
        t = pl.program_id(2)

        @pl.when(t == 0)
        def _():
            st_scr[...] = jnp.zeros_like(st_scr)

        _hgrn_gates(f_ref, lb_ref[0:1, :], a_scr, k_scr, tr)
        gnv = gn_ref[...]
        rr = lax.broadcasted_iota(jnp.int32, (CHUNK, CHUNK), 0)
        cc = lax.broadcasted_iota(jnp.int32, (CHUNK, CHUNK), 1)
        causal = cc <= rr

        def chunk(n, carry):
            rows = pl.ds(pl.multiple_of(n * CHUNK, CHUNK), CHUNK)
            lanes = [slice(hd * HEAD_DIM, (hd + 1) * HEAD_DIM) for hd in range(hpg)]
            hs = []
            for hd, ls in enumerate(lanes):
                h = {}
                ah, kh = a_scr[rows, ls], k_scr[rows, ls]
                qp = q_ref[rows, ls]
                qh = qp * _sigmoid(qp)
                h["vb"] = i_ref[rows, ls].astype(BF16)
                aref, alast = ah[CHUNK // 2 - 1:CHUNK // 2, :], ah[CHUNK - 1:CHUNK, :]
                h["q_in"] = (qh * jnp.exp(ah - aref)).astype(BF16)
                h["k_in"] = (kh * jnp.exp(aref - ah)).astype(BF16)
                h["q_out"] = (qh * jnp.exp(ah)).astype(BF16)
                h["k_out"] = (kh * jnp.exp(alast - ah)).astype(BF16)
                h["dec"] = jnp.exp(alast)
                st = st_scr[hd]
                st_ref[n, hd] = st
                h["st"] = st
                hs.append(h)
            for h in hs:
                h["scores"] = _dot_nt(h["q_in"], h["k_in"])
                h["o_inter"] = _dot_nt(h["q_out"], h["st"].astype(BF16))
                h["st_mm"] = _dot_tn(h["vb"], h["k_out"])
            for h in hs:
                h["o"] = _dot(jnp.where(causal, h["scores"], 0.0).astype(BF16), h["vb"]) + h["o_inter"]
            for hd, (h, ls) in enumerate(zip(hs, lanes)):
                st_scr[hd] = h["st"] * h["dec"] + h["st_mm"]
                o = h["o"]
                o_ref[rows, ls] = o
                rstd = lax.rsqrt(jnp.mean(o * o, axis=-1, keepdims=True) + EPS)
                gg = g_ref[rows, ls]
                ybr_ref[rows, ls] = ((o * rstd * gnv) * (gg * _sigmoid(gg))).astype(BF16)
            return carry

        lax.fori_loop(0, ncl, chunk, 0)

    def sec(s):
        return pl.BlockSpec((None, tr, hc), lambda hg, b, t: (s, b * nt + t, hg))

    blk = pl.BlockSpec((tr, hc), lambda hg, b, t: (b * nt + t, hg))
    return _pc(body, name="hgrn_fwd",
               out_shape=[jax.ShapeDtypeStruct((m, di), F32), jax.ShapeDtypeStruct((m, di), BF16),
                          jax.ShapeDtypeStruct((m // CHUNK, nheads, HEAD_DIM, HEAD_DIM), F32)],
               grid=(nhg, nb, nt),
               in_specs=[sec(0), sec(1), sec(2), sec(3), pl.BlockSpec((2, hc), lambda hg, b, t: (0, hg)),
                         pl.BlockSpec((1, HEAD_DIM), lambda hg, b, t: (0, 0))],
               out_specs=[blk, blk, pl.BlockSpec((ncl, hpg, HEAD_DIM, HEAD_DIM),
                                                 lambda hg, b, t: (b * nt + t, hg, 0, 0))],
               scratch=[pltpu.VMEM((hpg, HEAD_DIM, HEAD_DIM), F32), pltpu.VMEM((tr, hc), F32),
                        pltpu.VMEM((tr, hc), F32)],
               sem=("parallel", "arbitrary", "arbitrary"), comm=comm)(proj, proj, proj, proj, lbj, gn)


def _hgrn_bwd(proj, o_all, dybr, states, lbj, gn, nb, t_seq, comm=None):
    _, m, di = proj.shape
    tr, hc, hpg = _hgrn_dims(t_seq, di)
    nt, nhg, ncl = t_seq // tr, di // hc, tr // CHUNK

    def body(q_ref, f_ref, i_ref, g_ref, o_ref, dy_ref, st_ref, lb_ref, gn_ref,
             dp_ref, dlb_ref, dgn_ref, dst_scr, a_scr, k_scr, da_scr, dk_scr):
        hg, b, t = pl.program_id(0), pl.program_id(1), pl.program_id(2)

        @pl.when(t == 0)
        def _():
            dst_scr[...] = jnp.zeros_like(dst_scr)

        @pl.when((b == 0) & (t == 0))
        def _():
            dlb_ref[...] = jnp.zeros_like(dlb_ref)

        @pl.when((hg == 0) & (b == 0) & (t == 0))
        def _():
            dgn_ref[...] = jnp.zeros_like(dgn_ref)

        lb = lb_ref[0:1, :]
        sig, fg = _hgrn_gates(f_ref, lb, a_scr, k_scr, tr)
        gnv = gn_ref[...]
        rr = lax.broadcasted_iota(jnp.int32, (CHUNK, CHUNK), 0)
        cc = lax.broadcasted_iota(jnp.int32, (CHUNK, CHUNK), 1)
        causal = cc <= rr
        rowi = lax.broadcasted_iota(jnp.int32, (CHUNK, HEAD_DIM), 0)

        def chunk(it, carry):
            n = ncl - 1 - it
            rows = pl.ds(pl.multiple_of(n * CHUNK, CHUNK), CHUNK)
            lanes = [slice(hd * HEAD_DIM, (hd + 1) * HEAD_DIM) for hd in range(hpg)]
            hs = []
            for hd, ls in enumerate(lanes):
                h = {}
                ah, kh = a_scr[rows, ls], k_scr[rows, ls]
                qp = q_ref[rows, ls]
                sq = _sigmoid(qp)
                qh = qp * sq
                h["dsilu_q"] = sq * (1.0 + qp * (1.0 - sq))
                h["vb"] = i_ref[rows, ls].astype(BF16)
                aref, alast = ah[CHUNK // 2 - 1:CHUNK // 2, :], ah[CHUNK - 1:CHUNK, :]
                h["e1"], h["e2"] = jnp.exp(ah - aref), jnp.exp(aref - ah)
                h["e3"], h["e4"] = jnp.exp(ah), jnp.exp(alast - ah)
                h["dec"] = jnp.exp(alast)
                h["q_in"], h["k_in"], h["q_out"], h["k_out"] = qh * h["e1"], kh * h["e2"], qh * h["e3"], kh * h["e4"]
                for nm in ("q_in", "k_in", "q_out", "k_out"):
                    h[nm + "_b"] = h[nm].astype(BF16)
                o = o_ref[rows, ls]
                rstd = lax.rsqrt(jnp.mean(o * o, axis=-1, keepdims=True) + EPS)
                ohat = o * rstd
                gg = g_ref[rows, ls]
                sg = _sigmoid(gg)
                dyv = dy_ref[rows, ls]
                d_on = dyv * (gg * sg)
                dp_ref[3, rows, ls] = (dyv * (ohat * gnv) * (sg * (1.0 + gg * (1.0 - sg)))).astype(BF16)
                h["dgn"] = jnp.sum(d_on * ohat, axis=0, keepdims=True)
                dohat = d_on * gnv
                do = rstd * (dohat - ohat * jnp.mean(dohat * ohat, axis=-1, keepdims=True))
                h["do_b"] = do.astype(BF16)
                h["st_prev"] = st_ref[n, hd]
                h["dst"] = dst_scr[hd]
                hs.append(h)
            for h in hs:
                dst_b = h["dst"].astype(BF16)
                h["scores"] = _dot_nt(h["q_in_b"], h["k_in_b"])
                h["dscores"] = _dot_nt(h["do_b"], h["vb"])
                h["dv_inter"] = _dot_nt(h["k_out_b"], dst_b)
                h["dq_out"] = _dot(h["do_b"], h["st_prev"].astype(BF16))
                h["dk_out"] = _dot(h["vb"], dst_b)
                h["dst_mm"] = _dot_tn(h["do_b"], h["q_out_b"])
            for h in hs:
                scores = jnp.where(causal, h["scores"], 0.0).astype(BF16)
                dscores = jnp.where(causal, h["dscores"], 0.0).astype(BF16)
                h["dv"] = _dot_tn(scores, h["do_b"]) + h["dv_inter"]
                h["dq_in"] = _dot(dscores, h["k_in_b"])
                h["dk_in"] = _dot_tn(dscores, h["q_in_b"])
            dgn = hs[0]["dgn"]
            for h in hs[1:]:
                dgn = dgn + h["dgn"]
            dgn_ref[...] += dgn
            for hd, (h, ls) in enumerate(zip(hs, lanes)):
                ddec = jnp.sum(h["dst"] * h["st_prev"], axis=0, keepdims=True)
                dst_scr[hd] = h["dst"] * h["dec"] + h["dst_mm"]
                dp_ref[2, rows, ls] = h["dv"].astype(BF16)
                dq = h["dq_in"] * h["e1"] + h["dq_out"] * h["e3"]
                dp_ref[0, rows, ls] = (dq * h["dsilu_q"]).astype(BF16)
                dk_scr[rows, ls] = h["dk_in"] * h["e2"] + h["dk_out"] * h["e4"]
                t_in = h["dq_in"] * h["q_in"] - h["dk_in"] * h["k_in"]
                t_out = h["dk_out"] * h["k_out"]
                da = t_in + h["dq_out"] * h["q_out"] - t_out
                da_ref_row = -jnp.sum(t_in, axis=0, keepdims=True)
                da_last_row = jnp.sum(t_out, axis=0, keepdims=True) + ddec * h["dec"]
                da = da + jnp.where(rowi == CHUNK // 2 - 1, da_ref_row, 0.0) \
                        + jnp.where(rowi == CHUNK - 1, da_last_row, 0.0)
                da_scr[rows, ls] = da
            return carry

        lax.fori_loop(0, ncl, chunk, 0)
        g = min(CUM_ROWS, tr)
        tri = _tri_mask(g, reverse=True)
        for rg in range(tr // g):
            rs = slice(rg * g, (rg + 1) * g)
            dlogf = _tri_apply(tri, da_scr[rs, :])
            df = dlogf / fg[rs, :] - dk_scr[rs, :]
            sgr = sig[rs, :]
            dp_ref[1, rs, :] = (df * (1.0 - lb) * (sgr * (1.0 - sgr))).astype(BF16)
            dlb_ref[...] += jnp.sum(df * (1.0 - sgr), axis=0, keepdims=True) * lb_ref[1:2, :]

    def sec(s):
        return pl.BlockSpec((None, tr, hc), lambda hg, b, t: (s, b * nt + (nt - 1 - t), hg))

    blk = pl.BlockSpec((tr, hc), lambda hg, b, t: (b * nt + (nt - 1 - t), hg))
    return _pc(body, name="hgrn_bwd",
               out_shape=[jax.ShapeDtypeStruct((4, m, di), BF16), jax.ShapeDtypeStruct((1, di), F32),
                          jax.ShapeDtypeStruct((1, HEAD_DIM), F32)],
               grid=(nhg, nb, nt),
               in_specs=[sec(0), sec(1), sec(2), sec(3), blk, blk,
                         pl.BlockSpec((ncl, hpg, HEAD_DIM, HEAD_DIM),
                                      lambda hg, b, t: (b * nt + (nt - 1 - t), hg, 0, 0)),
                         pl.BlockSpec((2, hc), lambda hg, b, t: (0, hg)),
                         pl.BlockSpec((1, HEAD_DIM), lambda hg, b, t: (0, 0))],
               out_specs=[pl.BlockSpec((4, tr, hc), lambda hg, b, t: (0, b * nt + (nt - 1 - t), hg)),
                          pl.BlockSpec((1, hc), lambda hg, b, t: (0, hg)),
                          pl.BlockSpec((1, HEAD_DIM), lambda hg, b, t: (0, 0))],
               scratch=[pltpu.VMEM((hpg, HEAD_DIM, HEAD_DIM), F32)] + [pltpu.VMEM((tr, hc), F32)] * 4,
               sem=("arbitrary", "arbitrary", "arbitrary"), comm=comm)(
                   proj, proj, proj, proj, o_all, dybr, states, lbj, gn)


def _adamw(parts, w, m, v, name, comm=None):
    r, c = w.shape
    tr = _tile(r, 256)
    npart = len(parts)
    c1 = 1.0 - ADAM_B1 ** ADAM_STEP
    c2 = 1.0 - ADAM_B2 ** ADAM_STEP

    def body(*refs):
        p_refs = refs[:npart]
        _adamw_math(p_refs, *refs[npart:], c1, c2)

    blk = pl.BlockSpec((tr, c), lambda i: (i, 0))
    return _pc(body, name=name, out_shape=[jax.ShapeDtypeStruct((r, c), F32)] * 4, grid=(r // tr,),
               in_specs=[blk] * (npart + 3), out_specs=[blk] * 4, sem=("parallel",), comm=comm)(*parts, w, m, v)


def _adamw_math(p_refs, w_ref, m_ref, v_ref, g_ref, d_ref, nm_ref, nv_ref, c1, c2):
    g = p_refs[0][...].astype(F32)
    for p in p_refs[1:]:
        g = g + p[...].astype(F32)
    nm = ADAM_B1 * m_ref[...] + (1.0 - ADAM_B1) * g
    nv = ADAM_B2 * v_ref[...] + (1.0 - ADAM_B2) * (g * g)
    g_ref[...] = g
    nm_ref[...] = nm
    nv_ref[...] = nv
    d_ref[...] = -ADAM_LR * ((nm / c1) / (jnp.sqrt(nv / c2) + ADAM_EPS) + ADAM_WD * w_ref[...])


def _adamw_blocks(parts, idx, w, m, v, name):
    r, c = w.shape
    tr = _tile(r, 256)
    npart = len(parts)
    c1 = 1.0 - ADAM_B1 ** ADAM_STEP
    c2 = 1.0 - ADAM_B2 ** ADAM_STEP

    def body(idx_ref, *refs):
        _adamw_math(refs[:npart], *refs[npart:], c1, c2)

    def sel(p):
        return pl.BlockSpec((None, tr, c), lambda i, s: (s[p], i, 0))

    blk = pl.BlockSpec((tr, c), lambda i, s: (i, 0))
    gs = pltpu.PrefetchScalarGridSpec(num_scalar_prefetch=1, grid=(r // tr,),
                                      in_specs=[sel(p) for p in range(npart)] + [blk] * 3, out_specs=[blk] * 4)
    return _pc(body, name=name, out_shape=[jax.ShapeDtypeStruct((r, c), F32)] * 4, grid_spec=gs,
               sem=("parallel",))(idx, *parts, w, m, v)


_EARLY = ["a_ln_gain", "a_ln_bias", "a_w_s", "a_b_s", "b_lower_bounds", "b_gn_gain"]


def _pack(arrs):
    flat = jnp.concatenate([a.reshape(-1) for a in arrs])
    rows = -(-flat.shape[0] // 1024) * 8
    return jnp.pad(flat, (0, rows * 128 - flat.shape[0])).reshape(rows, 128)


def _unpack(buf, like):
    flat = buf.reshape(-1)
    out, off = [], 0
    for a in like:
        out.append(flat[off:off + a.size].reshape(a.shape))
        off += a.size
    return out


def kernel(x, c, norm_gain, w_ada, b_ada, a_w_in, a_ln_gain, a_ln_bias, a_w_s, a_b_s, a_w_out, b_w_in, b_lower_bounds, b_gn_gain, b_w_out, final_gain, loss_target, m_norm_gain, m_w_ada, m_b_ada, m_a_w_in, m_a_ln_gain, m_a_ln_bias, m_a_w_s, m_a_b_s, m_a_w_out, m_b_w_in, m_b_lower_bounds, m_b_gn_gain, m_b_w_out, m_final_gain, v_norm_gain, v_w_ada, v_b_ada, v_a_w_in, v_a_ln_gain, v_a_ln_bias, v_a_w_s, v_a_b_s, v_a_w_out, v_b_w_in, v_b_lower_bounds, v_b_gn_gain, v_b_w_out, v_final_gain):
    w = dict(norm_gain=norm_gain, w_ada=w_ada, b_ada=b_ada, a_w_in=a_w_in, a_ln_gain=a_ln_gain,
             a_ln_bias=a_ln_bias, a_w_s=a_w_s, a_b_s=a_b_s, a_w_out=a_w_out, b_w_in=b_w_in,
             b_lower_bounds=b_lower_bounds, b_gn_gain=b_gn_gain, b_w_out=b_w_out, final_gain=final_gain)
    mo = dict(norm_gain=m_norm_gain, w_ada=m_w_ada, b_ada=m_b_ada, a_w_in=m_a_w_in, a_ln_gain=m_a_ln_gain,
              a_ln_bias=m_a_ln_bias, a_w_s=m_a_w_s, a_b_s=m_a_b_s, a_w_out=m_a_w_out, b_w_in=m_b_w_in,
              b_lower_bounds=m_b_lower_bounds, b_gn_gain=m_b_gn_gain, b_w_out=m_b_w_out, final_gain=m_final_gain)
    vo = dict(norm_gain=v_norm_gain, w_ada=v_w_ada, b_ada=v_b_ada, a_w_in=v_a_w_in, a_ln_gain=v_a_ln_gain,
              a_ln_bias=v_a_ln_bias, a_w_s=v_a_w_s, a_b_s=v_a_b_s, a_w_out=v_a_w_out, b_w_in=v_b_w_in,
              b_lower_bounds=v_b_lower_bounds, b_gn_gain=v_b_gn_gain, b_w_out=v_b_w_out, final_gain=v_final_gain)

    nb, t_seq, d = x.shape
    m = nb * t_seq
    ncol_ada = w_ada.shape[2]
    xi, yi, ci = lax.axis_index("x"), lax.axis_index("y"), lax.axis_index("c")
    me = 4 * xi + 2 * yi + ci

    c_g, wa_in_g = _all_gather([c, a_w_in[0].astype(BF16)], "gather_c_wa")

    c_all = c_g.reshape(NDEV * nb, d)
    b_cols = lax.dynamic_slice(b_ada, (0, me * ncol_ada), (2, ncol_ada)).reshape(2, 1, ncol_ada)
    mod_part, lbj = _ada_fwd(c_all, w_ada, b_cols, b_lower_bounds)
    mod_all = _all_gather([mod_part], "gather_mod")[0]
    mod_mine = lax.dynamic_slice_in_dim(mod_all, me * nb, nb, axis=2)
    mod_mine = mod_mine.transpose(1, 2, 0, 3).reshape(2, nb, 3, d)
    mod0, mod1 = mod_mine[0], mod_mine[1]

    di = a_w_out.shape[1] * NDEV

    xf = x.reshape(m, d)
    tgt = loss_target.reshape(m, d)
    ng0, ng1 = norm_gain[0:1], norm_gain[1:2]
    ncb = b_w_in.shape[2]
    wb_lo, wb_hi = b_w_in[0][:, :ncb // 2].astype(BF16), b_w_in[0][:, ncb // 2:].astype(BF16)
    h0, h0_t = _prenorm(xf, ng0, mod0, t_seq, "prenorm_a")
    proj_a, half = _mm_in(h0, [wa_in_g], 1, "in_proj_a", comm=_gather_first([a_w_out[0].astype(BF16), wb_lo]))
    bs_t = jnp.pad(a_b_s[0].T, ((0, 0), (0, 128 - SG_GROUPS)))
    ybr_a, (wa_out_g, wb_lo_g, wb_hi_half) = _a_mid_fwd(
        proj_a, a_ln_gain, a_ln_bias, a_w_s[0], bs_t, t_seq, comm=_join(_gather_second(half), _gather_first([wb_hi])))
    wa_out = wa_out_g.reshape(di, d)
    (yout_a, x1), (wb_hi_g, wb_out_half) = _out_proj(
        ybr_a, wa_out, xf, mod0, t_seq, "out_proj_a",
        comm=_join(_gather_second([wb_hi_half]), _gather_first([b_w_out[0].astype(BF16)])))
    wb_in_g = [wb_lo_g, wb_hi_g]
    h1, h1_t = _prenorm(x1, ng1, mod1, t_seq, "prenorm_b")
    proj_b, (wb_out_g,) = _mm_in(h1, wb_in_g, 4, "in_proj_b", comm=_gather_second([wb_out_half]))
    wb_out = wb_out_g.reshape(di, d)
    o_b, ybr_b, states = _hgrn_fwd(proj_b, lbj, b_gn_gain, nb, t_seq)
    yout_b, dx2, loss_part, d_final_gain = _out_proj_loss(ybr_b, wb_out, x1, mod1, final_gain.reshape(1, d), tgt, t_seq)

    rows_out = a_w_out.shape[1]
    dy_b, dgate1, dybr_b = _gate_dybr(dx2, yout_b, mod1, wb_out, t_seq, "dybr_b")
    rs_wb_out = _ReduceScatter(_mm_dw_out(ybr_b, dy_b, "dw_out_b").reshape(NDEV, rows_out, d), "b_w_out")
    (dproj_b, d_lb, d_gn), got = _hgrn_bwd(proj_b, o_b, dybr_b, states, lbj, b_gn_gain, nb, t_seq,
                                           comm=rs_wb_out.swap_core())
    rs_wb_out.after_core(got[0])
    dh1, got = _mm_din(dproj_b, wb_in_g, 4, "dh_b", comm=rs_wb_out.swap_chips())
    rs_wb_out.after_chips(got[0])
    dx1, dss1, dgain1 = _prenorm_bwd(dh1, x1, ng1, mod1, dx2, t_seq, "prenorm_bwd_b")
    rs_wb_in = _ReduceScatter(_mm_dw_in(h1_t, dproj_b, ncb, 4, "dw_in_b"), "b_w_in")

    dy_a, dgate0, dybr_a = _gate_dybr(dx1, yout_a, mod0, wa_out, t_seq, "dybr_a")
    g_wa_out, got = _mm_dw_out(ybr_a, dy_a, "dw_out_a", comm=rs_wb_in.swap_core())
    rs_wb_in.after_core(got[0])
    rs_wa_out = _ReduceScatter(g_wa_out.reshape(NDEV, rows_out, d), "a_w_out")
    (dproj_a, d_lng, d_lnb, d_ws, d_bs_t), got = _a_mid_bwd(
        proj_a, dybr_a, a_ln_gain, a_ln_bias, a_w_s[0], bs_t, t_seq,
        comm=_join(rs_wb_in.swap_chips(), rs_wa_out.swap_core()))
    rs_wb_in.after_chips(got[0])
    rs_wa_out.after_core(got[1])
    part = dict(a_ln_gain=d_lng, a_ln_bias=d_lnb, a_w_s=d_ws[None], a_b_s=d_bs_t[:, :SG_GROUPS].T[None],
                b_lower_bounds=jnp.concatenate([-d_lb, d_lb], axis=0), b_gn_gain=d_gn)
    early_pack = _pack([part[k].reshape(w[k].shape) for k in _EARLY])
    g_wa_in, got = _mm_dw_in(h0_t, dproj_a, wa_in_g.shape[2], 1, "dw_in_a",
                             comm=_join(rs_wa_out.swap_chips(), _gather_first([early_pack])))
    rs_wa_out.after_chips(got[0])
    rs_wa_in = _ReduceScatter(g_wa_in, "a_w_in")
    n_tiles = m // _din_tile(m)
    assert n_tiles >= 2
    first_tiles = max(1, (3 * n_tiles) // 8)
    dh0, got2 = _mm_din(dproj_a, [wa_in_g], 1, "dh_a_first", tiles=(0, first_tiles),
                        comm=_join(rs_wa_in.swap_core(), _gather_second([got[1]])))
    rs_wa_in.after_core(got2[0])
    early_all = got2[1]
    dh0, got = _mm_din(dproj_a, [wa_in_g], 1, "dh_a_rest", comm=rs_wa_in.swap_chips(),
                       tiles=(first_tiles, n_tiles - first_tiles), prev=dh0)
    rs_wa_in.after_chips(got[0])
    dx0, dss0, dgain0 = _prenorm_bwd(dh0, xf, ng0, mod0, dx1, t_seq, "prenorm_bwd_a")
    grad_x = dx0.reshape(nb, t_seq, d)

    dmod = jnp.stack([jnp.concatenate([dss0, dgate0], axis=1), jnp.concatenate([dss1, dgate1], axis=1)])
    late_like = [norm_gain, final_gain, loss_part.reshape(1)]
    late_pack = _pack([jnp.concatenate([dgain0, dgain1], axis=0), d_final_gain[0], loss_part.reshape(1)])
    dmod_all, late_all = _all_gather([dmod.reshape(2, nb, 3 * d), late_pack], "gather_tail")
    dmod_all = dmod_all.transpose(1, 0, 2, 3).reshape(2, NDEV * nb, 3 * d)
    dmod_cols = lax.dynamic_slice_in_dim(dmod_all, me * ncol_ada, ncol_ada, axis=2)
    g_w_ada, g_b_ada = _ada_bwd(c_all, dmod_cols, dmod_all)

    res = {}
    early_like = [w[k] for k in _EARLY]
    dev_order = jnp.arange(NDEV, dtype=jnp.int32)
    sm = _adamw_blocks([early_all] * NDEV, dev_order, _pack(early_like), _pack([mo[k] for k in _EARLY]),
                       _pack([vo[k] for k in _EARLY]), "adamw_small_early")
    sm = [dict(zip(_EARLY, _unpack(buf, early_like))) for buf in sm]
    for k in _EARLY:
        res[k] = tuple(s[k] for s in sm)
    zero = jnp.zeros((1,), F32)
    sm = _adamw_blocks([late_all] * NDEV, dev_order, _pack([norm_gain, final_gain, zero]),
                       _pack([mo["norm_gain"], mo["final_gain"], zero]),
                       _pack([vo["norm_gain"], vo["final_gain"], zero]), "adamw_small_late")
    sm = [_unpack(buf, late_like) for buf in sm]
    res["norm_gain"] = tuple(s[0] for s in sm)
    res["final_gain"] = tuple(s[1] for s in sm)
    loss = sm[0][2][0]
    rb = _adamw([g_b_ada], b_ada, mo["b_ada"], vo["b_ada"], "adamw_b_ada")
    res["b_ada"] = tuple(rb)
    sh = w_ada.shape
    ra = _adamw([g_w_ada.reshape(sh[0] * sh[1], sh[2])], w_ada.reshape(sh[0] * sh[1], sh[2]),
                mo["w_ada"].reshape(sh[0] * sh[1], sh[2]), vo["w_ada"].reshape(sh[0] * sh[1], sh[2]), "adamw_w_ada")
    res["w_ada"] = tuple(z.reshape(sh) for z in ra)

    for k, rs in (("b_w_out", rs_wb_out), ("b_w_in", rs_wb_in), ("a_w_out", rs_wa_out), ("a_w_in", rs_wa_in)):
        res[k] = tuple(z[None] for z in _adamw_blocks(rs.parts, rs.idx, w[k][0], mo[k][0], vo[k][0], "adamw_" + k))

    order = ["norm_gain", "w_ada", "b_ada", "a_w_in", "a_ln_gain", "a_ln_bias", "a_w_s", "a_b_s", "a_w_out",
             "b_w_in", "b_lower_bounds", "b_gn_gain", "b_w_out", "final_gain"]
    return (loss, grad_x, *[res[k][0] for k in order], *[res[k][1] for k in order],
            *[res[k][2] for k in order], *[res[k][3] for k in order])
```

```python
import functools
import math

import jax
import jax.numpy as jnp
from jax import lax
from jax.experimental import pallas as pl
from jax.experimental.pallas import tpu as pltpu

F32 = jnp.float32
BF16 = jnp.bfloat16
MESH = pl.DeviceIdType.MESH
NDEV = 8
EPS = 1e-6
CHUNK = 64
SG_BLOCK = 128
SG_GROUPS = 8
HEAD_DIM = 128
CUM_ROWS = 256
ADAM_LR, ADAM_B1, ADAM_B2, ADAM_EPS, ADAM_WD, ADAM_STEP = 0.001, 0.9, 0.999, 1e-08, 0.01, 10
VMEM_LIMIT = 56 * 1024 * 1024
ANY = pl.BlockSpec(memory_space=pl.ANY)


class _Hosted:
    def __init__(self, arrays, out_shapes, nsem, start, finish, aliases=None):
        self.arrays, self.out_shapes, self.nsem = list(arrays), list(out_shapes), nsem
        self.start, self.finish = start, finish
        self.aliases = dict(aliases or {})


def _join(*comms):
    arrays, outs, aliases, offs, nsem = [], [], {}, [], 0
    for cm in comms:
        offs.append((len(arrays), len(outs), nsem))
        for i, o in cm.aliases.items():
            aliases[len(arrays) + i] = len(outs) + o
        arrays += cm.arrays
        outs += cm.out_shapes
        nsem += cm.nsem

    def run(which):
        def f(ins, outs_, ss, rs, base):
            for cm, (ia, io, isem) in zip(comms, offs):
                getattr(cm, which)(ins[ia:ia + len(cm.arrays)], outs_[io:io + len(cm.out_shapes)], ss, rs, base + isem)
        return f

    return _Hosted(arrays, outs, nsem, run("start"), run("finish"), aliases)


def _pc(body, *, name, out_shape, grid=None, in_specs=None, out_specs=None, scratch=(), sem=None,
        grid_spec=None, comm=None, aliases=None):
    cp = dict(vmem_limit_bytes=VMEM_LIMIT)
    aliases = dict(aliases or {})
    if comm is None:
        if sem is not None:
            cp["dimension_semantics"] = sem
        kw = {"input_output_aliases": aliases}
        if grid_spec is not None:
            kw["grid_spec"] = grid_spec
        else:
            if grid is not None:
                kw["grid"] = grid
            if in_specs is not None:
                kw["in_specs"] = in_specs
            if out_specs is not None:
                kw["out_specs"] = out_specs
            kw["scratch_shapes"] = list(scratch)
        return pl.pallas_call(functools.partial(body), name=name, out_shape=out_shape,
                              compiler_params=pltpu.CompilerParams(**cp), **kw)

    single = not isinstance(out_shape, (list, tuple))
    outs_list = [out_shape] if single else list(out_shape)
    ospecs = [out_specs] if single else list(out_specs)
    n_in, n_out, n_ci, n_co, n_scr = len(in_specs), len(outs_list), len(comm.arrays), len(comm.out_shapes), len(scratch)
    cp["dimension_semantics"] = ("arbitrary",) * len(grid)

    def hosted(*refs):
        cin, hin = refs[:n_in], refs[n_in:n_in + n_ci]
        cout = refs[n_in + n_ci:n_in + n_ci + n_out]
        hout = refs[n_in + n_ci + n_out:n_in + n_ci + n_out + n_co]
        scr = refs[n_in + n_ci + n_out + n_co:n_in + n_ci + n_out + n_co + n_scr]
        ssem, rsem = refs[-2], refs[-1]
        first = functools.reduce(lambda p, q: p & q, [pl.program_id(a) == 0 for a in range(len(grid))])
        last = functools.reduce(lambda p, q: p & q, [pl.program_id(a) == grid[a] - 1 for a in range(len(grid))])

        @pl.when(first)
        def _():
            comm.start(hin, hout, ssem, rsem, 0)

        body(*cin, *cout, *scr)

        @pl.when(last)
        def _():
            comm.finish(hin, hout, ssem, rsem, 0)

    call = pl.pallas_call(
        hosted, name=name, grid=grid, in_specs=list(in_specs) + [ANY] * n_ci, out_specs=ospecs + [ANY] * n_co,
        out_shape=outs_list + comm.out_shapes,
        scratch_shapes=list(scratch) + [pltpu.SemaphoreType.DMA((comm.nsem,)), pltpu.SemaphoreType.DMA((comm.nsem,))],
        input_output_aliases={**aliases, **{n_in + i: n_out + o for i, o in comm.aliases.items()}},
        compiler_params=pltpu.CompilerParams(**cp))

    def run(*args):
        res = call(*args, *comm.arrays)
        comp = res[:n_out]
        return (comp[0] if single else comp), list(res[n_out:])

    return run


def _tile(n, pref):
    return pref if n % pref == 0 else n


def _sigmoid(x):
    return 1.0 / (1.0 + jnp.exp(-x))


def _gelu(x):
    c = math.sqrt(2.0 / math.pi)
    return 0.5 * x * (1.0 + jnp.tanh(c * (x + 0.044715 * (x * x * x))))


def _gelu_and_grad(x):
    c = math.sqrt(2.0 / math.pi)
    x2 = x * x
    t = jnp.tanh(c * (x + 0.044715 * (x2 * x)))
    half = 0.5 * (1.0 + t)
    return x * half, half + (0.5 * x) * (1.0 - t * t) * (c + (3.0 * 0.044715 * c) * x2)


def _dot(a, b):
    return jnp.dot(a, b, preferred_element_type=F32)


def _dot_nt(a, b):
    return lax.dot_general(a, b, (((1,), (1,)), ((), ())), preferred_element_type=F32)


def _dot_tn(a, b):
    return lax.dot_general(a, b, (((0,), (0,)), ((), ())), preferred_element_type=F32)


def _tri_mask(n, reverse):
    r = lax.broadcasted_iota(jnp.int32, (n, n), 0)
    c = lax.broadcasted_iota(jnp.int32, (n, n), 1)
    same = (r // CHUNK) == (c // CHUNK)
    tri = (c >= r) if reverse else (c <= r)
    return jnp.where(same & tri, 1.0, 0.0).astype(BF16)


def _tri_apply(tri, x):
    hi = x.astype(BF16)
    r1 = x - hi.astype(F32)
    mid = r1.astype(BF16)
    lo = (r1 - mid.astype(F32)).astype(BF16)
    return _dot(tri, hi) + (_dot(tri, mid) + _dot(tri, lo))


def _all_gather(arrs, name):
    n = len(arrs)

    def body(*refs):
        ins, outs = refs[:n], refs[n:2 * n]
        send_sems, recv_sems, local_sems = refs[2 * n:]
        x, y, c = lax.axis_index("x"), lax.axis_index("y"), lax.axis_index("c")
        me, sibling = (x, y, c), (x, y, 1 - c)
        near = (x + c - 2 * x * c, y + (1 - c) - 2 * y * (1 - c))
        far = (x + (1 - c) - 2 * x * (1 - c), y + c - 2 * y * c)
        diag = (1 - x, 1 - y)

        def blk(a, p):
            return outs[a].at[4 * p[0] + 2 * p[1] + p[2]]

        def copy(a, k, block, to, src=None):
            return pltpu.make_async_remote_copy(
                src_ref=blk(a, block) if src is None else src, dst_ref=blk(a, block),
                send_sem=send_sems.at[7 * a + k], recv_sem=recv_sems.at[7 * a + k],
                device_id=to, device_id_type=MESH)

        mine = [pltpu.make_async_copy(ins[a], blk(a, me), local_sems.at[a]) for a in range(n)]
        for m in mine:
            m.start()
        sends = []
        for a in range(n):
            sends += [copy(a, 0, me, sibling, src=ins[a]), copy(a, 1, me, (*near, c), src=ins[a]),
                      copy(a, 2, me, (*far, c), src=ins[a])]
        for cp in sends:
            cp.start()
        for a in range(n):
            copy(a, 1, (*near, c), me).wait_recv()
            sends.append(copy(a, 3, (*near, c), (*far, c)))
            sends[-1].start()
        for a in range(n):
            sends.append(copy(a, 4, (*near, c), sibling))
            sends[-1].start()
            copy(a, 2, (*far, c), me).wait_recv()
            sends.append(copy(a, 5, (*far, c), sibling))
            sends[-1].start()
        for a in range(n):
            copy(a, 3, (*diag, c), me).wait_recv()
            sends.append(copy(a, 6, (*diag, c), sibling))
            sends[-1].start()
        for a in range(n):
            copy(a, 0, sibling, me).wait_recv()
            copy(a, 4, (*far, 1 - c), me).wait_recv()
            copy(a, 5, (*near, 1 - c), me).wait_recv()
            copy(a, 6, (*diag, 1 - c), me).wait_recv()
        for cp in sends:
            cp.wait_send()
        for m in mine:
            m.wait()

    out_shape = [jax.ShapeDtypeStruct((NDEV,) + a.shape, a.dtype) for a in arrs]
    return _pc(body, name=name, out_shape=out_shape, in_specs=[ANY] * n, out_specs=[ANY] * n,
               scratch=[pltpu.SemaphoreType.DMA((7 * n,)), pltpu.SemaphoreType.DMA((7 * n,)),
                        pltpu.SemaphoreType.DMA((n,))])(*arrs)


def _gather_first(arrs):
    n = len(arrs)

    def parts(ins, outs, ss, rs, base):
        x, y, c = lax.axis_index("x"), lax.axis_index("y"), lax.axis_index("c")
        me, sibling = (x, y, c), (x, y, 1 - c)
        chips = [(1 - x, y), (x, 1 - y), (1 - x, 1 - y)]

        def blk(a, p):
            return outs[a].at[4 * p[0] + 2 * p[1] + p[2]]

        def copy(a, k, block, to):
            return pltpu.make_async_remote_copy(
                src_ref=ins[a], dst_ref=blk(a, block), send_sem=ss.at[base + 4 * a + k],
                recv_sem=rs.at[base + 4 * a + k], device_id=to, device_id_type=MESH)

        local = [pltpu.make_async_copy(ins[a], blk(a, me), ss.at[base + 4 * n + a]) for a in range(n)]
        sends, recvs = [], []
        for a in range(n):
            sends.append(copy(a, 0, me, sibling))
            recvs.append(copy(a, 0, sibling, me))
            for j, chip in enumerate(chips):
                sends.append(copy(a, 1 + j, me, (*chip, c)))
                recvs.append(copy(a, 1 + j, (*chip, c), me))
        return local, sends, recvs

    def start(ins, outs, ss, rs, base):
        local, sends, _ = parts(ins, outs, ss, rs, base)
        for cp in local + sends:
            cp.start()

    def finish(ins, outs, ss, rs, base):
        local, sends, recvs = parts(ins, outs, ss, rs, base)
        for cp in recvs:
            cp.wait_recv()
        for cp in sends:
            cp.wait_send()
        for cp in local:
            cp.wait()

    return _Hosted(arrs, [jax.ShapeDtypeStruct((NDEV,) + a.shape, a.dtype) for a in arrs], 5 * n, start, finish)


def _gather_second(bufs):
    n = len(bufs)

    def parts(ins, outs, ss, rs, base):
        x, y, c = lax.axis_index("x"), lax.axis_index("y"), lax.axis_index("c")
        sibling = (x, y, 1 - c)
        chips = [(1 - x, y), (x, 1 - y), (1 - x, 1 - y)]
        sends, recvs = [], []
        for a in range(n):
            for j, chip in enumerate(chips):
                mine = 4 * chip[0] + 2 * chip[1] + c
                theirs = 4 * chip[0] + 2 * chip[1] + (1 - c)
                sends.append(pltpu.make_async_remote_copy(
                    src_ref=ins[a].at[mine], dst_ref=outs[a].at[mine], send_sem=ss.at[base + 3 * a + j],
                    recv_sem=rs.at[base + 3 * a + j], device_id=sibling, device_id_type=MESH))
                recvs.append(pltpu.make_async_remote_copy(
                    src_ref=ins[a].at[theirs], dst_ref=outs[a].at[theirs], send_sem=ss.at[base + 3 * a + j],
                    recv_sem=rs.at[base + 3 * a + j], device_id=sibling, device_id_type=MESH))
        return sends, recvs

    def start(ins, outs, ss, rs, base):
        for cp in parts(ins, outs, ss, rs, base)[0]:
            cp.start()

    def finish(ins, outs, ss, rs, base):
        sends, recvs = parts(ins, outs, ss, rs, base)
        for cp in recvs:
            cp.wait_recv()
        for cp in sends:
            cp.wait_send()

    return _Hosted(bufs, [jax.ShapeDtypeStruct(b.shape, b.dtype) for b in bufs], 3 * n, start, finish,
                   aliases={a: a for a in range(n)})


def _swap(src, nblk, ids_fn, partner_fn):
    def copies(ins, outs, ss, rs, base):
        x, y, c = lax.axis_index("x"), lax.axis_index("y"), lax.axis_index("c")
        ids = ids_fn(x, y, c)
        partner = partner_fn(x, y, c)
        return [pltpu.make_async_remote_copy(
            src_ref=ins[0].at[ids[k]], dst_ref=outs[0].at[k], send_sem=ss.at[base + k], recv_sem=rs.at[base + k],
            device_id=partner, device_id_type=MESH) for k in range(nblk)]

    def start(ins, outs, ss, rs, base):
        for cp in copies(ins, outs, ss, rs, base):
            cp.start()

    def finish(ins, outs, ss, rs, base):
        for cp in copies(ins, outs, ss, rs, base):
            cp.wait()

    return _Hosted([src], [jax.ShapeDtypeStruct((nblk,) + src.shape[1:], src.dtype)], nblk, start, finish)


def _blocking(comm, name):
    n_i, n_o = len(comm.arrays), len(comm.out_shapes)

    def body(*refs):
        ins, outs = refs[:n_i], refs[n_i:n_i + n_o]
        comm.start(ins, outs, refs[-2], refs[-1], 0)
        comm.finish(ins, outs, refs[-2], refs[-1], 0)

    return pl.pallas_call(
        body, name=name, out_shape=comm.out_shapes, in_specs=[ANY] * n_i, out_specs=[ANY] * n_o,
        scratch_shapes=[pltpu.SemaphoreType.DMA((comm.nsem,)), pltpu.SemaphoreType.DMA((comm.nsem,))],
        input_output_aliases=comm.aliases)(*comm.arrays)


def _swap_chips(send):
    def copies(ins, outs, ss, rs, base):
        x, y, c = lax.axis_index("x"), lax.axis_index("y"), lax.axis_index("c")
        chips = [(1 - x, y), (x, 1 - y), (1 - x, 1 - y)]
        return [pltpu.make_async_remote_copy(
            src_ref=ins[0].at[j], dst_ref=outs[0].at[j], send_sem=ss.at[base + j], recv_sem=rs.at[base + j],
            device_id=(*chip, c), device_id_type=MESH) for j, chip in enumerate(chips)]

    def start(ins, outs, ss, rs, base):
        for cp in copies(ins, outs, ss, rs, base):
            cp.start()

    def finish(ins, outs, ss, rs, base):
        for cp in copies(ins, outs, ss, rs, base):
            cp.wait()

    return _Hosted([send], [jax.ShapeDtypeStruct(send.shape, send.dtype)], 3, start, finish)


def _add_send(a, b, idx, ns, name):
    _, r, c = a.shape
    tr = _tile(r, 256)

    def body(idx_ref, a_ref, b_ref, send_ref):
        send_ref[...] = (a_ref[...] + b_ref[...]).astype(BF16)

    def sel(off):
        return pl.BlockSpec((None, tr, c), lambda k, i, s: (s[off + k], i, 0))

    gs = pltpu.PrefetchScalarGridSpec(num_scalar_prefetch=1, grid=(ns, r // tr), in_specs=[sel(0), sel(ns)],
                                      out_specs=pl.BlockSpec((None, tr, c), lambda k, i, s: (k, i, 0)))
    return _pc(body, name=name, grid_spec=gs, sem=("arbitrary", "arbitrary"),
               out_shape=jax.ShapeDtypeStruct((ns, r, c), BF16))(idx, a, b)


class _ReduceScatter:
    def __init__(self, g, tag):
        self.g, self.tag = g, tag

    def swap_core(self):
        return _swap(self.g, 4, lambda x, y, c: [1 - c, 3 - c, 5 - c, 7 - c], lambda x, y, c: (x, y, 1 - c))

    def after_core(self, recv):
        x, y, c = lax.axis_index("x"), lax.axis_index("y"), lax.axis_index("c")
        chips = [(1 - x, y), (x, 1 - y), (1 - x, 1 - y)]
        idx = jnp.stack([4 * p + 2 * q + c for p, q in chips] + [2 * p + q for p, q in chips]).astype(jnp.int32)
        self.send = _add_send(self.g, recv, idx, 3, "rs_add_" + self.tag)
        self.recv_core = recv
        zero = jnp.zeros((), jnp.int32)
        self.idx = jnp.stack([4 * x + 2 * y + c, 2 * x + y, zero, zero + 1, zero + 2]).astype(jnp.int32)

    def swap_chips(self):
        return _swap_chips(self.send)

    def after_chips(self, recv):
        self.parts = [self.g, self.recv_core, recv, recv, recv]


def _ada_fwd(c_all, w_ada, b_cols, b_lb):
    nl, d, ncol = w_ada.shape
    nseq = c_all.shape[0]
    di = b_lb.shape[1]

    def body(c_ref, w_ref, b_ref, lb_ref, mod_ref, lbj_ref):
        cv = c_ref[...]
        cact = (cv * _sigmoid(cv)).astype(BF16)
        for l in range(nl):
            mod_ref[l] = _dot(cact, w_ref[l].astype(BF16)) + b_ref[l]
        b0, b1 = lb_ref[0:1, :], lb_ref[1:2, :]
        mx = jnp.maximum(b0, b1)
        e0, e1 = jnp.exp(b0 - mx), jnp.exp(b1 - mx)
        s = e0 + e1
        p0, p1 = e0 / s, e1 / s
        lbj_ref[0:1, :] = (p0 + p1) - p0
        lbj_ref[1:2, :] = p0 * p1

    return _pc(body, name="ada_fwd",
               out_shape=[jax.ShapeDtypeStruct((nl, nseq, ncol), F32), jax.ShapeDtypeStruct((2, di), F32)]
               )(c_all, w_ada, b_cols, b_lb)


def _ada_bwd(c_all, dmod_cols, dmod_full):
    nl, nseq, ncol = dmod_cols.shape
    d = c_all.shape[1]
    d3 = dmod_full.shape[2]

    def body(c_ref, dc_ref, df_ref, gw_ref, gb_ref):
        cv = c_ref[...]
        cact = (cv * _sigmoid(cv)).astype(BF16)
        for l in range(nl):
            gw_ref[l] = _dot_tn(cact, dc_ref[l].astype(BF16))
            gb_ref[l:l + 1, :] = jnp.sum(df_ref[l], axis=0, keepdims=True)

    return _pc(body, name="ada_bwd",
               out_shape=[jax.ShapeDtypeStruct((nl, d, ncol), F32), jax.ShapeDtypeStruct((nl, d3), F32)]
               )(c_all, dmod_cols, dmod_full)


def _prenorm(x, gain, mod, t_seq, name, comm=None):
    m, d = x.shape
    tm = _tile(t_seq, 1024)
    per = t_seq // tm

    def body(x_ref, g_ref, mod_ref, h_ref, ht_ref):
        xv = x_ref[...]
        rstd = lax.rsqrt(jnp.mean(xv * xv, axis=-1, keepdims=True) + EPS)
        r = xv * rstd * g_ref[...]
        h = r * (1.0 + mod_ref[0, 1:2, :]) + mod_ref[0, 0:1, :]
        h_ref[...] = h.astype(BF16)
        ht_ref[...] = h.T.astype(BF16)

    return _pc(body, name=name, out_shape=[jax.ShapeDtypeStruct((m, d), BF16), jax.ShapeDtypeStruct((d, m), BF16)],
               grid=(m // tm,),
               in_specs=[pl.BlockSpec((tm, d), lambda i: (i, 0)), pl.BlockSpec((1, d), lambda i: (0, 0)),
                         pl.BlockSpec((1, 3, d), lambda i: (i // per, 0, 0))],
               out_specs=[pl.BlockSpec((tm, d), lambda i: (i, 0)), pl.BlockSpec((d, tm), lambda i: (0, i))],
               sem=("parallel",), comm=comm)(x, gain, mod)


def _prenorm_bwd(dh, x, gain, mod, dxn, t_seq, name, comm=None):
    m, d = x.shape
    nb = m // t_seq
    tm = _tile(t_seq, 1024)
    per = t_seq // tm

    def body(dh_ref, x_ref, g_ref, mod_ref, dxn_ref, dx_ref, dss_ref, dg_ref):
        i = pl.program_id(0)
        xv, dhv, g = x_ref[...], dh_ref[...], g_ref[...]
        rstd = lax.rsqrt(jnp.mean(xv * xv, axis=-1, keepdims=True) + EPS)
        xhat = xv * rstd
        dr = dhv * (1.0 + mod_ref[0, 1:2, :])
        dxhat = dr * g
        dx_ref[...] = dxn_ref[...] + rstd * (dxhat - xhat * jnp.mean(dxhat * xhat, axis=-1, keepdims=True))

        @pl.when(i % per == 0)
        def _():
            dss_ref[...] = jnp.zeros_like(dss_ref)

        @pl.when(i == 0)
        def _():
            dg_ref[...] = jnp.zeros_like(dg_ref)

        dss_ref[0, 0:1, :] += jnp.sum(dhv, axis=0, keepdims=True)
        dss_ref[0, 1:2, :] += jnp.sum(dhv * (xhat * g), axis=0, keepdims=True)
        dg_ref[...] += jnp.sum(dr * xhat, axis=0, keepdims=True)

    row = pl.BlockSpec((tm, d), lambda i: (i, 0))
    return _pc(body, name=name,
               out_shape=[jax.ShapeDtypeStruct((m, d), F32), jax.ShapeDtypeStruct((nb, 2, d), F32),
                          jax.ShapeDtypeStruct((1, d), F32)],
               grid=(m // tm,),
               in_specs=[row, row, pl.BlockSpec((1, d), lambda i: (0, 0)),
                         pl.BlockSpec((1, 3, d), lambda i: (i // per, 0, 0)), row],
               out_specs=[row, pl.BlockSpec((1, 2, d), lambda i: (i // per, 0, 0)),
                          pl.BlockSpec((1, d), lambda i: (0, 0))],
               sem=("arbitrary",), comm=comm)(dh, x, gain, mod, dxn)


def _mm_in(h, ws, sections, name, comm=None):
    m, k = h.shape
    nw = len(ws)
    widths = [w.shape[2] for w in ws]
    offs = [sum(widths[:a]) for a in range(nw)]
    nc = sum(widths)
    per = NDEV // sections if sections > 1 else NDEV
    tm = _din_tile(m)
    assert per % 2 == 0

    def body(*refs):
        hv = refs[0][...]
        o_ref = refs[1 + nw]
        for b in range(2):
            for a in range(nw):
                lo = b * nc + offs[a]
                o_ref[:, lo:lo + widths[a]] = _dot(hv, refs[1 + a][b])

    w_specs = [pl.BlockSpec((2, k, wd), lambda j, i: (j, 0, 0)) for wd in widths]
    if sections > 1:
        out_shape = jax.ShapeDtypeStruct((sections, m, per * nc), F32)
        out_spec = pl.BlockSpec((None, tm, 2 * nc), lambda j, i: ((2 * j) // per, i, ((2 * j) % per) // 2))
    else:
        out_shape = jax.ShapeDtypeStruct((m, NDEV * nc), F32)
        out_spec = pl.BlockSpec((tm, 2 * nc), lambda j, i: (i, j))
    return _pc(body, name=name, out_shape=out_shape, grid=(NDEV // 2, m // tm),
               in_specs=[pl.BlockSpec((tm, k), lambda j, i: (i, 0))] + w_specs,
               out_specs=out_spec, sem=("parallel", "parallel"), comm=comm)(h, *ws)


def _din_tile(m):
    return 1024 if m % 1024 == 0 and m >= 2048 else _tile(m, 512)


def _mm_din(dproj, ws, sections, name, comm=None, tiles=None, prev=None):
    nw, k = len(ws), ws[0].shape[1]
    widths = [w.shape[2] for w in ws]
    offs = [sum(widths[:a]) for a in range(nw)]
    nc = sum(widths)
    m = dproj.shape[-2]
    tm = _din_tile(m)
    t0, nt = tiles if tiles is not None else (0, m // tm)
    per = NDEV // sections if sections > 1 else NDEV
    assert per % 2 == 0

    def body(*refs):
        d_ref, o_ref = refs[0], refs[-1]
        j = pl.program_id(1)
        acc = None
        for b in range(2):
            for a in range(nw):
                lo = b * nc + offs[a]
                term = _dot_nt(d_ref[:, lo:lo + widths[a]], refs[1 + a][b])
                acc = term if acc is None else acc + term

        @pl.when(j == 0)
        def _():
            o_ref[...] = acc

        @pl.when(j > 0)
        def _():
            o_ref[...] += acc

    if sections > 1:
        dspec = pl.BlockSpec((None, tm, 2 * nc), lambda i, j: ((2 * j) // per, i + t0, ((2 * j) % per) // 2))
    else:
        dspec = pl.BlockSpec((tm, 2 * nc), lambda i, j: (i + t0, j))
    in_specs = [dspec] + [pl.BlockSpec((2, k, wd), lambda i, j: (j, 0, 0)) for wd in widths]
    args = [dproj, *ws]
    if prev is not None:
        in_specs.append(ANY)
        args.append(prev)
    return _pc(body, name=name, out_shape=jax.ShapeDtypeStruct((m, k), F32), grid=(nt, NDEV // 2), in_specs=in_specs,
               out_specs=pl.BlockSpec((tm, k), lambda i, j: (i + t0, 0)), sem=("parallel", "arbitrary"),
               comm=comm, aliases={1 + nw: 0} if prev is not None else None)(*args)


def _mm_dw_in(ht, dproj, nc, sections, name, comm=None):
    k, m = ht.shape
    tk = 2048 if m % 2048 == 0 else _din_tile(m)
    per = NDEV // sections if sections > 1 else NDEV

    def body(h_ref, d_ref, o_ref):
        kk = pl.program_id(1)
        acc = _dot(h_ref[...], d_ref[...])

        @pl.when(kk == 0)
        def _():
            o_ref[...] = acc

        @pl.when(kk > 0)
        def _():
            o_ref[...] += acc

    if sections > 1:
        dspec = pl.BlockSpec((None, tk, nc), lambda j, i: (j // per, i, j % per))
    else:
        dspec = pl.BlockSpec((tk, nc), lambda j, i: (i, j))
    return _pc(body, name=name, out_shape=jax.ShapeDtypeStruct((NDEV, k, nc), F32), grid=(NDEV, m // tk),
               in_specs=[pl.BlockSpec((k, tk), lambda j, i: (0, i)), dspec],
               out_specs=pl.BlockSpec((None, k, nc), lambda j, i: (j, 0, 0)),
               sem=("parallel", "arbitrary"), comm=comm)(ht, dproj)


def _out_proj(ybr, w_out, x, mod, t_seq, name, comm=None):
    m, di = ybr.shape
    d = w_out.shape[1]
    tm = _tile(t_seq, 512)
    per = t_seq // tm

    def body(y_ref, w_ref, x_ref, mod_ref, yo_ref, xn_ref):
        yo = _dot(y_ref[...], w_ref[...])
        yo_ref[...] = yo
        xn_ref[...] = x_ref[...] + mod_ref[0, 2:3, :] * yo

    row = pl.BlockSpec((tm, d), lambda i: (i, 0))
    return _pc(body, name=name,
               out_shape=[jax.ShapeDtypeStruct((m, d), F32), jax.ShapeDtypeStruct((m, d), F32)],
               grid=(m // tm,),
               in_specs=[pl.BlockSpec((tm, di), lambda i: (i, 0)), pl.BlockSpec((di, d), lambda i: (0, 0)), row,
                         pl.BlockSpec((1, 3, d), lambda i: (i // per, 0, 0))],
               out_specs=[row, row], sem=("parallel",), comm=comm)(ybr, w_out, x, mod)


def _out_proj_loss(ybr, w_out, x, mod, gain, target, t_seq):
    m, di = ybr.shape
    d = w_out.shape[1]
    tm = _tile(t_seq, 512)
    per = t_seq // tm

    def body(y_ref, w_ref, x_ref, mod_ref, g_ref, t_ref, yo_ref, dx_ref, loss_ref, dg_ref):
        i = pl.program_id(0)
        yo = _dot(y_ref[...], w_ref[...])
        yo_ref[...] = yo
        xv = x_ref[...] + mod_ref[0, 2:3, :] * yo
        g = g_ref[...]
        rstd = lax.rsqrt(jnp.mean(xv * xv, axis=-1, keepdims=True) + EPS)
        xhat = xv * rstd
        err = xhat * g - t_ref[...]
        dy = err * (1.0 / d)
        dxhat = dy * g
        dx_ref[...] = rstd * (dxhat - xhat * jnp.mean(dxhat * xhat, axis=-1, keepdims=True))

        @pl.when(i == 0)
        def _():
            loss_ref[...] = jnp.zeros_like(loss_ref)
            dg_ref[...] = jnp.zeros_like(dg_ref)

        loss_ref[...] += 0.5 * jnp.sum(jnp.mean(err * err, axis=-1, keepdims=True), axis=0, keepdims=True)
        dg_ref[...] += jnp.sum(dy * xhat, axis=0, keepdims=True)

    row = pl.BlockSpec((tm, d), lambda i: (i, 0))
    vec = pl.BlockSpec((1, d), lambda i: (0, 0))
    return _pc(body, name="out_proj_loss",
               out_shape=[jax.ShapeDtypeStruct((m, d), F32), jax.ShapeDtypeStruct((m, d), F32),
                          jax.ShapeDtypeStruct((1, 1), F32), jax.ShapeDtypeStruct((1, d), F32)],
               grid=(m // tm,),
               in_specs=[pl.BlockSpec((tm, di), lambda i: (i, 0)), pl.BlockSpec((di, d), lambda i: (0, 0)), row,
                         pl.BlockSpec((1, 3, d), lambda i: (i // per, 0, 0)), vec, row],
               out_specs=[row, row, pl.BlockSpec((1, 1), lambda i: (0, 0)), vec],
               sem=("arbitrary",))(ybr, w_out, x, mod, gain, target)


def _gate_dybr(dxn, yout, mod, w_out, t_seq, name):
    m, d = dxn.shape
    di = w_out.shape[0]
    nb = m // t_seq
    tm = _tile(t_seq, 512)
    per = t_seq // tm

    def body(dxn_ref, yo_ref, mod_ref, w_ref, dy_ref, dgate_ref, o_ref):
        i = pl.program_id(0)
        dv = dxn_ref[...]
        dy = (mod_ref[0, 2:3, :] * dv).astype(BF16)
        dy_ref[...] = dy
        o_ref[...] = _dot_nt(dy, w_ref[...])

        @pl.when(i % per == 0)
        def _():
            dgate_ref[...] = jnp.zeros_like(dgate_ref)

        dgate_ref[0] += jnp.sum(dv * yo_ref[...], axis=0, keepdims=True)

    row = pl.BlockSpec((tm, d), lambda i: (i, 0))
    return _pc(body, name=name,
               out_shape=[jax.ShapeDtypeStruct((m, d), BF16), jax.ShapeDtypeStruct((nb, 1, d), F32),
                          jax.ShapeDtypeStruct((m, di), F32)],
               grid=(m // tm,),
               in_specs=[row, row, pl.BlockSpec((1, 3, d), lambda i: (i // per, 0, 0)),
                         pl.BlockSpec((di, d), lambda i: (0, 0))],
               out_specs=[row, pl.BlockSpec((1, 1, d), lambda i: (i // per, 0, 0)),
                          pl.BlockSpec((tm, di), lambda i: (i, 0))],
               sem=("arbitrary",))(dxn, yout, mod, w_out)


def _mm_dw_out(ybr, dy, name, comm=None):
    m, di = ybr.shape
    d = dy.shape[1]
    tk = _tile(m, 512)
    tn = _tile(di, 1024)

    def body(y_ref, dy_ref, o_ref):
        kk = pl.program_id(1)
        acc = _dot_tn(y_ref[...], dy_ref[...])

        @pl.when(kk == 0)
        def _():
            o_ref[...] = acc

        @pl.when(kk > 0)
        def _():
            o_ref[...] += acc

    return _pc(body, name=name, out_shape=jax.ShapeDtypeStruct((di, d), F32), grid=(di // tn, m // tk),
               in_specs=[pl.BlockSpec((tk, tn), lambda n, k: (k, n)), pl.BlockSpec((tk, d), lambda n, k: (k, 0))],
               out_specs=pl.BlockSpec((tn, d), lambda n, k: (n, 0)), sem=("parallel", "arbitrary"),
               comm=comm)(ybr, dy)


def _sgu_mask():
    t = lax.broadcasted_iota(jnp.int32, (SG_BLOCK, SG_BLOCK), 0)
    s = lax.broadcasted_iota(jnp.int32, (SG_BLOCK, SG_BLOCK), 1)
    return (s // CHUNK) <= (t // CHUNK)


def _a_mid_fwd(proj, ln_g, ln_b, w_s, bs_t, t_seq, comm=None):
    m, n3 = proj.shape
    di = n3 // 3
    gd = di // SG_GROUPS
    r = _tile(t_seq, 256)
    nblk = r // SG_BLOCK

    def body(p_ref, lg_ref, lb_ref, ws_ref, bs_ref, ybr_ref, s_scr):
        v = _gelu(p_ref[:, di:2 * di])
        mu = jnp.mean(v, axis=-1, keepdims=True)
        vc = v - mu
        rstd = lax.rsqrt(jnp.mean(vc * vc, axis=-1, keepdims=True) + EPS)
        vb = (vc * rstd * lg_ref[...] + lb_ref[...]).astype(BF16)
        mask = _sgu_mask()
        for gi in range(SG_GROUPS):
            ws = jnp.where(mask, ws_ref[gi], 0.0).astype(BF16)
            bcol = bs_ref[:, gi:gi + 1]
            for b in range(nblk):
                rows = slice(b * SG_BLOCK, (b + 1) * SG_BLOCK)
                cols = slice(gi * gd, (gi + 1) * gd)
                s_scr[rows, cols] = _dot(ws, vb[rows, cols]) + bcol
        gg = p_ref[:, 2 * di:]
        ybr_ref[...] = (_gelu(p_ref[:, :di]) * s_scr[...] * (gg * _sigmoid(gg))).astype(BF16)

    vec = pl.BlockSpec((1, di), lambda i: (0, 0))
    return _pc(body, name="a_mid_fwd", out_shape=jax.ShapeDtypeStruct((m, di), BF16), grid=(m // r,),
               in_specs=[pl.BlockSpec((r, n3), lambda i: (i, 0)), vec, vec,
                         pl.BlockSpec((SG_GROUPS, SG_BLOCK, SG_BLOCK), lambda i: (0, 0, 0)),
                         pl.BlockSpec((SG_BLOCK, 128), lambda i: (0, 0))],
               out_specs=pl.BlockSpec((r, di), lambda i: (i, 0)),
               scratch=[pltpu.VMEM((r, di), F32)], sem=("parallel",), comm=comm)(proj, ln_g, ln_b, w_s, bs_t)


def _a_mid_bwd(proj, dybr, ln_g, ln_b, w_s, bs_t, t_seq, comm=None):
    m, n3 = proj.shape
    di = n3 // 3
    gd = di // SG_GROUPS
    r = _tile(t_seq, 256)
    nblk = r // SG_BLOCK

    def body(p_ref, dy_ref, lg_ref, lb_ref, ws_ref, bs_ref,
             dp_ref, dlg_ref, dlb_ref, dws_ref, dbs_ref, s_scr, dvl_scr):
        i = pl.program_id(0)

        @pl.when(i == 0)
        def _():
            dlg_ref[...] = jnp.zeros_like(dlg_ref)
            dlb_ref[...] = jnp.zeros_like(dlb_ref)
            dws_ref[...] = jnp.zeros_like(dws_ref)
            dbs_ref[...] = jnp.zeros_like(dbs_ref)

        v, dgelu_v = _gelu_and_grad(p_ref[:, di:2 * di])
        mu = jnp.mean(v, axis=-1, keepdims=True)
        vc = v - mu
        rstd = lax.rsqrt(jnp.mean(vc * vc, axis=-1, keepdims=True) + EPS)
        vhat = vc * rstd
        lg = lg_ref[...]
        vb = (vhat * lg + lb_ref[...]).astype(BF16)
        u, dgelu_u = _gelu_and_grad(p_ref[:, :di])
        gg = p_ref[:, 2 * di:]
        sg = _sigmoid(gg)
        dyv = dy_ref[...]
        dus = dyv * (gg * sg)
        dsb = (dus * u).astype(BF16)
        ds32 = dus * u
        mask = _sgu_mask()
        lane = lax.broadcasted_iota(jnp.int32, (SG_BLOCK, 128), 1)
        dbs_acc = jnp.zeros((SG_BLOCK, 128), F32)
        for gi in range(SG_GROUPS):
            ws = jnp.where(mask, ws_ref[gi], 0.0).astype(BF16)
            bcol = bs_ref[:, gi:gi + 1]
            cols = slice(gi * gd, (gi + 1) * gd)
            dws_acc = jnp.zeros((SG_BLOCK, SG_BLOCK), F32)
            dbs_col = jnp.zeros((SG_BLOCK, 1), F32)
            for b in range(nblk):
                rows = slice(b * SG_BLOCK, (b + 1) * SG_BLOCK)
                s_scr[rows, cols] = _dot(ws, vb[rows, cols]) + bcol
                dvl_scr[rows, cols] = _dot_tn(ws, dsb[rows, cols])
                dws_acc += _dot_nt(dsb[rows, cols], vb[rows, cols])
                dbs_col += jnp.sum(ds32[rows, cols], axis=-1, keepdims=True)
            dws_ref[gi] += jnp.where(mask, dws_acc, 0.0)
            dbs_acc += jnp.where(lane == gi, dbs_col, 0.0)
        dbs_ref[...] += dbs_acc
        s = s_scr[...]
        dp_ref[:, :di] = (dus * s * dgelu_u).astype(BF16)
        dp_ref[:, 2 * di:] = (dyv * u * s * (sg * (1.0 + gg * (1.0 - sg)))).astype(BF16)
        dvl = dvl_scr[...]
        dlg_ref[...] += jnp.sum(dvl * vhat, axis=0, keepdims=True)
        dlb_ref[...] += jnp.sum(dvl, axis=0, keepdims=True)
        dvh = dvl * lg
        dv = rstd * (dvh - jnp.mean(dvh, axis=-1, keepdims=True)
                     - vhat * jnp.mean(dvh * vhat, axis=-1, keepdims=True))
        dp_ref[:, di:2 * di] = (dv * dgelu_v).astype(BF16)

    vec = pl.BlockSpec((1, di), lambda i: (0, 0))
    wsb = pl.BlockSpec((SG_GROUPS, SG_BLOCK, SG_BLOCK), lambda i: (0, 0, 0))
    bsb = pl.BlockSpec((SG_BLOCK, 128), lambda i: (0, 0))
    return _pc(body, name="a_mid_bwd",
               out_shape=[jax.ShapeDtypeStruct((m, n3), BF16), jax.ShapeDtypeStruct((1, di), F32),
                          jax.ShapeDtypeStruct((1, di), F32),
                          jax.ShapeDtypeStruct((SG_GROUPS, SG_BLOCK, SG_BLOCK), F32),
                          jax.ShapeDtypeStruct((SG_BLOCK, 128), F32)],
               grid=(m // r,),
               in_specs=[pl.BlockSpec((r, n3), lambda i: (i, 0)), pl.BlockSpec((r, di), lambda i: (i, 0)),
                         vec, vec, wsb, bsb],
               out_specs=[pl.BlockSpec((r, n3), lambda i: (i, 0)), vec, vec, wsb, bsb],
               scratch=[pltpu.VMEM((r, di), F32), pltpu.VMEM((r, di), F32)],
               sem=("arbitrary",), comm=comm)(proj, dybr, ln_g, ln_b, w_s, bs_t)


def _hgrn_dims(t_seq, di):
    tr = _tile(t_seq, 256)
    hc = _tile(di, 1024)
    return tr, hc, hc // HEAD_DIM


def _hgrn_gates(f_ref, lb, a_scr, k_scr, tr):
    sig = _sigmoid(f_ref[...])
    fg = lb + (1.0 - lb) * sig
    k_scr[...] = 1.0 - fg
    logf = jnp.log(fg)
    g = min(CUM_ROWS, tr)
    tri = _tri_mask(g, reverse=False)
    for rg in range(tr // g):
        a_scr[rg * g:(rg + 1) * g, :] = _tri_apply(tri, logf[rg * g:(rg + 1) * g, :])
    return sig, fg


def _hgrn_fwd(proj, lbj, gn, nb, t_seq, comm=None):
    _, m, di = proj.shape
    tr, hc, hpg = _hgrn_dims(t_seq, di)
    nt, nhg, ncl = t_seq // tr, di // hc, tr // CHUNK
    nheads = di // HEAD_DIM

    def body(q_ref, f_ref, i_ref, g_ref, lb_ref, gn_ref, o_ref, ybr_ref, st_ref, st_scr, a_scr, k_scr):
        t = pl.program_id(2)

        @pl.when(t == 0)
        def _():
            st_scr[...] = jnp.zeros_like(st_scr)

        _hgrn_gates(f_ref, lb_ref[0:1, :], a_scr, k_scr, tr)
        gnv = gn_ref[...]
        rr = lax.broadcasted_iota(jnp.int32, (CHUNK, CHUNK), 0)
        cc = lax.broadcasted_iota(jnp.int32, (CHUNK, CHUNK), 1)
        causal = cc <= rr

        def chunk(n, carry):
            rows = pl.ds(pl.multiple_of(n * CHUNK, CHUNK), CHUNK)
            lanes = [slice(hd * HEAD_DIM, (hd + 1) * HEAD_DIM) for hd in range(hpg)]
            hs = []
            for hd, ls in enumerate(lanes):
                h = {}
                ah, kh = a_scr[rows, ls], k_scr[rows, ls]
                qp = q_ref[rows, ls]
                qh = qp * _sigmoid(qp)
                h["vb"] = i_ref[rows, ls].astype(BF16)
                aref, alast = ah[CHUNK // 2 - 1:CHUNK // 2, :], ah[CHUNK - 1:CHUNK, :]
                h["q_in"] = (qh * jnp.exp(ah - aref)).astype(BF16)
                h["k_in"] = (kh * jnp.exp(aref - ah)).astype(BF16)
                h["q_out"] = (qh * jnp.exp(ah)).astype(BF16)
                h["k_out"] = (kh * jnp.exp(alast - ah)).astype(BF16)
                h["dec"] = jnp.exp(alast)
                st = st_scr[hd]
                st_ref[n, hd] = st
                h["st"] = st
                hs.append(h)
            for h in hs:
                h["scores"] = _dot_nt(h["q_in"], h["k_in"])
                h["o_inter"] = _dot_nt(h["q_out"], h["st"].astype(BF16))
                h["st_mm"] = _dot_tn(h["vb"], h["k_out"])
            for h in hs:
                h["o"] = _dot(jnp.where(causal, h["scores"], 0.0).astype(BF16), h["vb"]) + h["o_inter"]
            for hd, (h, ls) in enumerate(zip(hs, lanes)):
                st_scr[hd] = h["st"] * h["dec"] + h["st_mm"]
                o = h["o"]
                o_ref[rows, ls] = o
                rstd = lax.rsqrt(jnp.mean(o * o, axis=-1, keepdims=True) + EPS)
                gg = g_ref[rows, ls]
                ybr_ref[rows, ls] = ((o * rstd * gnv) * (gg * _sigmoid(gg))).astype(BF16)
            return carry

        lax.fori_loop(0, ncl, chunk, 0)

    def sec(s):
        return pl.BlockSpec((None, tr, hc), lambda hg, b, t: (s, b * nt + t, hg))

    blk = pl.BlockSpec((tr, hc), lambda hg, b, t: (b * nt + t, hg))
    return _pc(body, name="hgrn_fwd",
               out_shape=[jax.ShapeDtypeStruct((m, di), F32), jax.ShapeDtypeStruct((m, di), BF16),
                          jax.ShapeDtypeStruct((m // CHUNK, nheads, HEAD_DIM, HEAD_DIM), F32)],
               grid=(nhg, nb, nt),
               in_specs=[sec(0), sec(1), sec(2), sec(3), pl.BlockSpec((2, hc), lambda hg, b, t: (0, hg)),
                         pl.BlockSpec((1, HEAD_DIM), lambda hg, b, t: (0, 0))],
               out_specs=[blk, blk, pl.BlockSpec((ncl, hpg, HEAD_DIM, HEAD_DIM),
                                                 lambda hg, b, t: (b * nt + t, hg, 0, 0))],
               scratch=[pltpu.VMEM((hpg, HEAD_DIM, HEAD_DIM), F32), pltpu.VMEM((tr, hc), F32),
                        pltpu.VMEM((tr, hc), F32)],
               sem=("parallel", "arbitrary", "arbitrary"), comm=comm)(proj, proj, proj, proj, lbj, gn)


def _hgrn_bwd(proj, o_all, dybr, states, lbj, gn, nb, t_seq, comm=None):
    _, m, di = proj.shape
    tr, hc, hpg = _hgrn_dims(t_seq, di)
    nt, nhg, ncl = t_seq // tr, di // hc, tr // CHUNK

    def body(q_ref, f_ref, i_ref, g_ref, o_ref, dy_ref, st_ref, lb_ref, gn_ref,
             dp_ref, dlb_ref, dgn_ref, dst_scr, a_scr, k_scr, da_scr, dk_scr):
        hg, b, t = pl.program_id(0), pl.program_id(1), pl.program_id(2)

        @pl.when(t == 0)
        def _():
            dst_scr[...] = jnp.zeros_like(dst_scr)

        @pl.when((b == 0) & (t == 0))
        def _():
            dlb_ref[...] = jnp.zeros_like(dlb_ref)

        @pl.when((hg == 0) & (b == 0) & (t == 0))
        def _():
            dgn_ref[...] = jnp.zeros_like(dgn_ref)

        lb = lb_ref[0:1, :]
        sig, fg = _hgrn_gates(f_ref, lb, a_scr, k_scr, tr)
        gnv = gn_ref[...]
        rr = lax.broadcasted_iota(jnp.int32, (CHUNK, CHUNK), 0)
        cc = lax.broadcasted_iota(jnp.int32, (CHUNK, CHUNK), 1)
        causal = cc <= rr
        rowi = lax.broadcasted_iota(jnp.int32, (CHUNK, HEAD_DIM), 0)

        def chunk(it, carry):
            n = ncl - 1 - it
            rows = pl.ds(pl.multiple_of(n * CHUNK, CHUNK), CHUNK)
            lanes = [slice(hd * HEAD_DIM, (hd + 1) * HEAD_DIM) for hd in range(hpg)]
            hs = []
            for hd, ls in enumerate(lanes):
                h = {}
                ah, kh = a_scr[rows, ls], k_scr[rows, ls]
                qp = q_ref[rows, ls]
                sq = _sigmoid(qp)
                qh = qp * sq
                h["dsilu_q"] = sq * (1.0 + qp * (1.0 - sq))
                h["vb"] = i_ref[rows, ls].astype(BF16)
                aref, alast = ah[CHUNK // 2 - 1:CHUNK // 2, :], ah[CHUNK - 1:CHUNK, :]
                h["e1"], h["e2"] = jnp.exp(ah - aref), jnp.exp(aref - ah)
                h["e3"], h["e4"] = jnp.exp(ah), jnp.exp(alast - ah)
                h["dec"] = jnp.exp(alast)
                h["q_in"], h["k_in"], h["q_out"], h["k_out"] = qh * h["e1"], kh * h["e2"], qh * h["e3"], kh * h["e4"]
                for nm in ("q_in", "k_in", "q_out", "k_out"):
                    h[nm + "_b"] = h[nm].astype(BF16)
                o = o_ref[rows, ls]
                rstd = lax.rsqrt(jnp.mean(o * o, axis=-1, keepdims=True) + EPS)
                ohat = o * rstd
                gg = g_ref[rows, ls]
                sg = _sigmoid(gg)
                dyv = dy_ref[rows, ls]
                d_on = dyv * (gg * sg)
                dp_ref[3, rows, ls] = (dyv * (ohat * gnv) * (sg * (1.0 + gg * (1.0 - sg)))).astype(BF16)
                h["dgn"] = jnp.sum(d_on * ohat, axis=0, keepdims=True)
                dohat = d_on * gnv
                do = rstd * (dohat - ohat * jnp.mean(dohat * ohat, axis=-1, keepdims=True))
                h["do_b"] = do.astype(BF16)
                h["st_prev"] = st_ref[n, hd]
                h["dst"] = dst_scr[hd]
                hs.append(h)
            for h in hs:
                dst_b = h["dst"].astype(BF16)
                h["scores"] = _dot_nt(h["q_in_b"], h["k_in_b"])
                h["dscores"] = _dot_nt(h["do_b"], h["vb"])
                h["dv_inter"] = _dot_nt(h["k_out_b"], dst_b)
                h["dq_out"] = _dot(h["do_b"], h["st_prev"].astype(BF16))
                h["dk_out"] = _dot(h["vb"], dst_b)
                h["dst_mm"] = _dot_tn(h["do_b"], h["q_out_b"])
            for h in hs:
                scores = jnp.where(causal, h["scores"], 0.0).astype(BF16)
                dscores = jnp.where(causal, h["dscores"], 0.0).astype(BF16)
                h["dv"] = _dot_tn(scores, h["do_b"]) + h["dv_inter"]
                h["dq_in"] = _dot(dscores, h["k_in_b"])
                h["dk_in"] = _dot_tn(dscores, h["q_in_b"])
            dgn = hs[0]["dgn"]
            for h in hs[1:]:
                dgn = dgn + h["dgn"]
            dgn_ref[...] += dgn
            for hd, (h, ls) in enumerate(zip(hs, lanes)):
                ddec = jnp.sum(h["dst"] * h["st_prev"], axis=0, keepdims=True)
                dst_scr[hd] = h["dst"] * h["dec"] + h["dst_mm"]
                dp_ref[2, rows, ls] = h["dv"].astype(BF16)
                dq = h["dq_in"] * h["e1"] + h["dq_out"] * h["e3"]
                dp_ref[0, rows, ls] = (dq * h["dsilu_q"]).astype(BF16)
                dk_scr[rows, ls] = h["dk_in"] * h["e2"] + h["dk_out"] * h["e4"]
                t_in = h["dq_in"] * h["q_in"] - h["dk_in"] * h["k_in"]
                t_out = h["dk_out"] * h["k_out"]
                da = t_in + h["dq_out"] * h["q_out"] - t_out
                da_ref_row = -jnp.sum(t_in, axis=0, keepdims=True)
                da_last_row = jnp.sum(t_out, axis=0, keepdims=True) + ddec * h["dec"]
                da = da + jnp.where(rowi == CHUNK // 2 - 1, da_ref_row, 0.0) \
                        + jnp.where(rowi == CHUNK - 1, da_last_row, 0.0)
                da_scr[rows, ls] = da
            return carry

        lax.fori_loop(0, ncl, chunk, 0)
        g = min(CUM_ROWS, tr)
        tri = _tri_mask(g, reverse=True)
        for rg in range(tr // g):
            rs = slice(rg * g, (rg + 1) * g)
            dlogf = _tri_apply(tri, da_scr[rs, :])
            df = dlogf / fg[rs, :] - dk_scr[rs, :]
            sgr = sig[rs, :]
            dp_ref[1, rs, :] = (df * (1.0 - lb) * (sgr * (1.0 - sgr))).astype(BF16)
            dlb_ref[...] += jnp.sum(df * (1.0 - sgr), axis=0, keepdims=True) * lb_ref[1:2, :]

    def sec(s):
        return pl.BlockSpec((None, tr, hc), lambda hg, b, t: (s, b * nt + (nt - 1 - t), hg))

    blk = pl.BlockSpec((tr, hc), lambda hg, b, t: (b * nt + (nt - 1 - t), hg))
    return _pc(body, name="hgrn_bwd",
               out_shape=[jax.ShapeDtypeStruct((4, m, di), BF16), jax.ShapeDtypeStruct((1, di), F32),
                          jax.ShapeDtypeStruct((1, HEAD_DIM), F32)],
               grid=(nhg, nb, nt),
               in_specs=[sec(0), sec(1), sec(2), sec(3), blk, blk,
                         pl.BlockSpec((ncl, hpg, HEAD_DIM, HEAD_DIM),
                                      lambda hg, b, t: (b * nt + (nt - 1 - t), hg, 0, 0)),
                         pl.BlockSpec((2, hc), lambda hg, b, t: (0, hg)),
                         pl.BlockSpec((1, HEAD_DIM), lambda hg, b, t: (0, 0))],
               out_specs=[pl.BlockSpec((4, tr, hc), lambda hg, b, t: (0, b * nt + (nt - 1 - t), hg)),
                          pl.BlockSpec((1, hc), lambda hg, b, t: (0, hg)),
                          pl.BlockSpec((1, HEAD_DIM), lambda hg, b, t: (0, 0))],
               scratch=[pltpu.VMEM((hpg, HEAD_DIM, HEAD_DIM), F32)] + [pltpu.VMEM((tr, hc), F32)] * 4,
               sem=("arbitrary", "arbitrary", "arbitrary"), comm=comm)(
                   proj, proj, proj, proj, o_all, dybr, states, lbj, gn)


def _adamw(parts, w, m, v, name, comm=None):
    r, c = w.shape
    tr = _tile(r, 256)
    npart = len(parts)
    c1 = 1.0 - ADAM_B1 ** ADAM_STEP
    c2 = 1.0 - ADAM_B2 ** ADAM_STEP

    def body(*refs):
        p_refs = refs[:npart]
        _adamw_math(p_refs, *refs[npart:], c1, c2)

    blk = pl.BlockSpec((tr, c), lambda i: (i, 0))
    return _pc(body, name=name, out_shape=[jax.ShapeDtypeStruct((r, c), F32)] * 4, grid=(r // tr,),
               in_specs=[blk] * (npart + 3), out_specs=[blk] * 4, sem=("parallel",), comm=comm)(*parts, w, m, v)


def _adamw_math(p_refs, w_ref, m_ref, v_ref, g_ref, d_ref, nm_ref, nv_ref, c1, c2):
    g = p_refs[0][...].astype(F32)
    for p in p_refs[1:]:
        g = g + p[...].astype(F32)
    nm = ADAM_B1 * m_ref[...] + (1.0 - ADAM_B1) * g
    nv = ADAM_B2 * v_ref[...] + (1.0 - ADAM_B2) * (g * g)
    g_ref[...] = g
    nm_ref[...] = nm
    nv_ref[...] = nv
    d_ref[...] = -ADAM_LR * ((nm / c1) / (jnp.sqrt(nv / c2) + ADAM_EPS) + ADAM_WD * w_ref[...])


def _adamw_blocks(parts, idx, w, m, v, name):
    r, c = w.shape
    tr = _tile(r, 256)
    npart = len(parts)
    c1 = 1.0 - ADAM_B1 ** ADAM_STEP
    c2 = 1.0 - ADAM_B2 ** ADAM_STEP

    def body(idx_ref, *refs):
        _adamw_math(refs[:npart], *refs[npart:], c1, c2)

    def sel(p):
        return pl.BlockSpec((None, tr, c), lambda i, s: (s[p], i, 0))

    blk = pl.BlockSpec((tr, c), lambda i, s: (i, 0))
    gs = pltpu.PrefetchScalarGridSpec(num_scalar_prefetch=1, grid=(r // tr,),
                                      in_specs=[sel(p) for p in range(npart)] + [blk] * 3, out_specs=[blk] * 4)
    return _pc(body, name=name, out_shape=[jax.ShapeDtypeStruct((r, c), F32)] * 4, grid_spec=gs,
               sem=("parallel",))(idx, *parts, w, m, v)


_EARLY = ["a_ln_gain", "a_ln_bias", "a_w_s", "a_b_s", "b_lower_bounds", "b_gn_gain"]


def _pack(arrs):
    flat = jnp.concatenate([a.reshape(-1) for a in arrs])
    rows = -(-flat.shape[0] // 1024) * 8
    return jnp.pad(flat, (0, rows * 128 - flat.shape[0])).reshape(rows, 128)


def _unpack(buf, like):
    flat = buf.reshape(-1)
    out, off = [], 0
    for a in like:
        out.append(flat[off:off + a.size].reshape(a.shape))
        off += a.size
    return out


def kernel(x, c, norm_gain, w_ada, b_ada, a_w_in, a_ln_gain, a_ln_bias, a_w_s, a_b_s, a_w_out, b_w_in, b_lower_bounds, b_gn_gain, b_w_out, final_gain, loss_target, m_norm_gain, m_w_ada, m_b_ada, m_a_w_in, m_a_ln_gain, m_a_ln_bias, m_a_w_s, m_a_b_s, m_a_w_out, m_b_w_in, m_b_lower_bounds, m_b_gn_gain, m_b_w_out, m_final_gain, v_norm_gain, v_w_ada, v_b_ada, v_a_w_in, v_a_ln_gain, v_a_ln_bias, v_a_w_s, v_a_b_s, v_a_w_out, v_b_w_in, v_b_lower_bounds, v_b_gn_gain, v_b_w_out, v_final_gain):
    w = dict(norm_gain=norm_gain, w_ada=w_ada, b_ada=b_ada, a_w_in=a_w_in, a_ln_gain=a_ln_gain,
             a_ln_bias=a_ln_bias, a_w_s=a_w_s, a_b_s=a_b_s, a_w_out=a_w_out, b_w_in=b_w_in,
             b_lower_bounds=b_lower_bounds, b_gn_gain=b_gn_gain, b_w_out=b_w_out, final_gain=final_gain)
    mo = dict(norm_gain=m_norm_gain, w_ada=m_w_ada, b_ada=m_b_ada, a_w_in=m_a_w_in, a_ln_gain=m_a_ln_gain,
              a_ln_bias=m_a_ln_bias, a_w_s=m_a_w_s, a_b_s=m_a_b_s, a_w_out=m_a_w_out, b_w_in=m_b_w_in,
              b_lower_bounds=m_b_lower_bounds, b_gn_gain=m_b_gn_gain, b_w_out=m_b_w_out, final_gain=m_final_gain)
    vo = dict(norm_gain=v_norm_gain, w_ada=v_w_ada, b_ada=v_b_ada, a_w_in=v_a_w_in, a_ln_gain=v_a_ln_gain,
              a_ln_bias=v_a_ln_bias, a_w_s=v_a_w_s, a_b_s=v_a_b_s, a_w_out=v_a_w_out, b_w_in=v_b_w_in,
              b_lower_bounds=v_b_lower_bounds, b_gn_gain=v_b_gn_gain, b_w_out=v_b_w_out, final_gain=v_final_gain)

    nb, t_seq, d = x.shape
    m = nb * t_seq
    ncol_ada = w_ada.shape[2]
    xi, yi, ci = lax.axis_index("x"), lax.axis_index("y"), lax.axis_index("c")
    me = 4 * xi + 2 * yi + ci

    c_g, wa_in_g = _all_gather([c, a_w_in[0].astype(BF16)], "gather_c_wa")

    c_all = c_g.reshape(NDEV * nb, d)
    b_cols = lax.dynamic_slice(b_ada, (0, me * ncol_ada), (2, ncol_ada)).reshape(2, 1, ncol_ada)
    mod_part, lbj = _ada_fwd(c_all, w_ada, b_cols, b_lower_bounds)
    mod_all = _all_gather([mod_part], "gather_mod")[0]
    mod_mine = lax.dynamic_slice_in_dim(mod_all, me * nb, nb, axis=2)
    mod_mine = mod_mine.transpose(1, 2, 0, 3).reshape(2, nb, 3, d)
    mod0, mod1 = mod_mine[0], mod_mine[1]

    di = a_w_out.shape[1] * NDEV

    xf = x.reshape(m, d)
    tgt = loss_target.reshape(m, d)
    ng0, ng1 = norm_gain[0:1], norm_gain[1:2]
    ncb = b_w_in.shape[2]
    wb1, wb2, wb3 = (b_w_in[0][:, lo:hi].astype(BF16)
                     for lo, hi in ((0, ncb // 4), (ncb // 4, 3 * ncb // 4), (3 * ncb // 4, ncb)))
    h0, h0_t = _prenorm(xf, ng0, mod0, t_seq, "prenorm_a")
    proj_a, half = _mm_in(h0, [wa_in_g], 1, "in_proj_a", comm=_gather_first([a_w_out[0].astype(BF16), wb1]))
    bs_t = jnp.pad(a_b_s[0].T, ((0, 0), (0, 128 - SG_GROUPS)))
    ybr_a, (wa_out_g, wb1_g, wb2_half) = _a_mid_fwd(
        proj_a, a_ln_gain, a_ln_bias, a_w_s[0], bs_t, t_seq, comm=_join(_gather_second(half), _gather_first([wb2])))
    wa_out = wa_out_g.reshape(di, d)
    (yout_a, x1), (wb2_g, wb3_half) = _out_proj(
        ybr_a, wa_out, xf, mod0, t_seq, "out_proj_a", comm=_join(_gather_second([wb2_half]), _gather_first([wb3])))
    (h1, h1_t), (wb3_g,) = _prenorm(x1, ng1, mod1, t_seq, "prenorm_b", comm=_gather_second([wb3_half]))
    wb_in_g = [wb1_g, wb2_g, wb3_g]
    proj_b, (wb_out_half,) = _mm_in(h1, wb_in_g, 4, "in_proj_b", comm=_gather_first([b_w_out[0].astype(BF16)]))
    (o_b, ybr_b, states), (wb_out_g,) = _hgrn_fwd(proj_b, lbj, b_gn_gain, nb, t_seq,
                                                  comm=_gather_second([wb_out_half]))
    wb_out = wb_out_g.reshape(di, d)
    yout_b, dx2, loss_part, d_final_gain = _out_proj_loss(ybr_b, wb_out, x1, mod1, final_gain.reshape(1, d), tgt, t_seq)

    rows_out = a_w_out.shape[1]
    dy_b, dgate1, dybr_b = _gate_dybr(dx2, yout_b, mod1, wb_out, t_seq, "dybr_b")
    rs_wb_out = _ReduceScatter(_mm_dw_out(ybr_b, dy_b, "dw_out_b").reshape(NDEV, rows_out, d), "b_w_out")
    (dproj_b, d_lb, d_gn), got = _hgrn_bwd(proj_b, o_b, dybr_b, states, lbj, b_gn_gain, nb, t_seq,
                                           comm=rs_wb_out.swap_core())
    rs_wb_out.after_core(got[0])
    dh1, got = _mm_din(dproj_b, wb_in_g, 4, "dh_b", comm=rs_wb_out.swap_chips())
    rs_wb_out.after_chips(got[0])
    dx1, dss1, dgain1 = _prenorm_bwd(dh1, x1, ng1, mod1, dx2, t_seq, "prenorm_bwd_b")
    rs_wb_in = _ReduceScatter(_mm_dw_in(h1_t, dproj_b, ncb, 4, "dw_in_b"), "b_w_in")

    dy_a, dgate0, dybr_a = _gate_dybr(dx1, yout_a, mod0, wa_out, t_seq, "dybr_a")
    g_wa_out, got = _mm_dw_out(ybr_a, dy_a, "dw_out_a", comm=rs_wb_in.swap_core())
    rs_wb_in.after_core(got[0])
    rs_wa_out = _ReduceScatter(g_wa_out.reshape(NDEV, rows_out, d), "a_w_out")
    (dproj_a, d_lng, d_lnb, d_ws, d_bs_t), got = _a_mid_bwd(
        proj_a, dybr_a, a_ln_gain, a_ln_bias, a_w_s[0], bs_t, t_seq,
        comm=_join(rs_wb_in.swap_chips(), rs_wa_out.swap_core()))
    rs_wb_in.after_chips(got[0])
    rs_wa_out.after_core(got[1])
    part = dict(a_ln_gain=d_lng, a_ln_bias=d_lnb, a_w_s=d_ws[None], a_b_s=d_bs_t[:, :SG_GROUPS].T[None],
                b_lower_bounds=jnp.concatenate([-d_lb, d_lb], axis=0), b_gn_gain=d_gn)
    early_pack = _pack([part[k].reshape(w[k].shape) for k in _EARLY])
    g_wa_in, got = _mm_dw_in(h0_t, dproj_a, wa_in_g.shape[2], 1, "dw_in_a",
                             comm=_join(rs_wa_out.swap_chips(), _gather_first([early_pack])))
    rs_wa_out.after_chips(got[0])
    rs_wa_in = _ReduceScatter(g_wa_in, "a_w_in")
    n_tiles = m // _din_tile(m)
    assert n_tiles >= 2
    first_tiles = max(1, (3 * n_tiles) // 8)
    dh0, got2 = _mm_din(dproj_a, [wa_in_g], 1, "dh_a_first", tiles=(0, first_tiles),
                        comm=_join(rs_wa_in.swap_core(), _gather_second([got[1]])))
    rs_wa_in.after_core(got2[0])
    early_all = got2[1]
    dh0, got = _mm_din(dproj_a, [wa_in_g], 1, "dh_a_rest", comm=rs_wa_in.swap_chips(),
                       tiles=(first_tiles, n_tiles - first_tiles), prev=dh0)
    rs_wa_in.after_chips(got[0])
    dx0, dss0, dgain0 = _prenorm_bwd(dh0, xf, ng0, mod0, dx1, t_seq, "prenorm_bwd_a")
    grad_x = dx0.reshape(nb, t_seq, d)

    dmod = jnp.stack([jnp.concatenate([dss0, dgate0], axis=1), jnp.concatenate([dss1, dgate1], axis=1)])
    late_like = [norm_gain, final_gain, loss_part.reshape(1)]
    late_pack = _pack([jnp.concatenate([dgain0, dgain1], axis=0), d_final_gain[0], loss_part.reshape(1)])
    dmod_all, late_all = _all_gather([dmod.reshape(2, nb, 3 * d), late_pack], "gather_tail")
    dmod_all = dmod_all.transpose(1, 0, 2, 3).reshape(2, NDEV * nb, 3 * d)
    dmod_cols = lax.dynamic_slice_in_dim(dmod_all, me * ncol_ada, ncol_ada, axis=2)
    g_w_ada, g_b_ada = _ada_bwd(c_all, dmod_cols, dmod_all)

    res = {}
    early_like = [w[k] for k in _EARLY]
    dev_order = jnp.arange(NDEV, dtype=jnp.int32)
    sm = _adamw_blocks([early_all] * NDEV, dev_order, _pack(early_like), _pack([mo[k] for k in _EARLY]),
                       _pack([vo[k] for k in _EARLY]), "adamw_small_early")
    sm = [dict(zip(_EARLY, _unpack(buf, early_like))) for buf in sm]
    for k in _EARLY:
        res[k] = tuple(s[k] for s in sm)
    zero = jnp.zeros((1,), F32)
    sm = _adamw_blocks([late_all] * NDEV, dev_order, _pack([norm_gain, final_gain, zero]),
                       _pack([mo["norm_gain"], mo["final_gain"], zero]),
                       _pack([vo["norm_gain"], vo["final_gain"], zero]), "adamw_small_late")
    sm = [_unpack(buf, late_like) for buf in sm]
    res["norm_gain"] = tuple(s[0] for s in sm)
    res["final_gain"] = tuple(s[1] for s in sm)
    loss = sm[0][2][0]
    rb = _adamw([g_b_ada], b_ada, mo["b_ada"], vo["b_ada"], "adamw_b_ada")
    res["b_ada"] = tuple(rb)
    sh = w_ada.shape
    ra = _adamw([g_w_ada.reshape(sh[0] * sh[1], sh[2])], w_ada.reshape(sh[0] * sh[1], sh[2]),
                mo["w_ada"].reshape(sh[0] * sh[1], sh[2]), vo["w_ada"].reshape(sh[0] * sh[1], sh[2]), "adamw_w_ada")
    res["w_ada"] = tuple(z.reshape(sh) for z in ra)

    for k, rs in (("b_w_out", rs_wb_out), ("b_w_in", rs_wb_in), ("a_w_out", rs_wa_out), ("a_w_in", rs_wa_in)):
        res[k] = tuple(z[None] for z in _adamw_blocks(rs.parts, rs.idx, w[k][0], mo[k][0], vo[k][0], "adamw_" + k))

    order = ["norm_gain", "w_ada", "b_ada", "a_w_in", "a_ln_gain", "a_ln_bias", "a_w_s", "a_b_s", "a_w_out",
             "b_w_in", "b_lower_bounds", "b_gn_gain", "b_w_out", "final_gain"]
    return (loss, grad_x, *[res[k][0] for k in order], *[res[k][1] for k in order],
            *[res[k][2] for k in order], *[res[k][3] for k in order])
```

```python
import functools
import math

import jax
import jax.numpy as jnp
from jax import lax
from jax.experimental import pallas as pl
from jax.experimental.pallas import tpu as pltpu

F32 = jnp.float32
BF16 = jnp.bfloat16
MESH = pl.DeviceIdType.MESH
NDEV = 8
EPS = 1e-6
CHUNK = 64
SG_BLOCK = 128
SG_GROUPS = 8
HEAD_DIM = 128
CUM_ROWS = 256
ADAM_LR, ADAM_B1, ADAM_B2, ADAM_EPS, ADAM_WD, ADAM_STEP = 0.001, 0.9, 0.999, 1e-08, 0.01, 10
VMEM_LIMIT = 56 * 1024 * 1024
ANY = pl.BlockSpec(memory_space=pl.ANY)


class _Hosted:
    def __init__(self, arrays, out_shapes, nsem, start, finish, aliases=None):
        self.arrays, self.out_shapes, self.nsem = list(arrays), list(out_shapes), nsem
        self.start, self.finish = start, finish
        self.aliases = dict(aliases or {})


def _join(*comms):
    arrays, outs, aliases, offs, nsem = [], [], {}, [], 0
    for cm in comms:
        offs.append((len(arrays), len(outs), nsem))
        for i, o in cm.aliases.items():
            aliases[len(arrays) + i] = len(outs) + o
        arrays += cm.arrays
        outs += cm.out_shapes
        nsem += cm.nsem

    def run(which):
        def f(ins, outs_, ss, rs, base):
            for cm, (ia, io, isem) in zip(comms, offs):
                getattr(cm, which)(ins[ia:ia + len(cm.arrays)], outs_[io:io + len(cm.out_shapes)], ss, rs, base + isem)
        return f

    return _Hosted(arrays, outs, nsem, run("start"), run("finish"), aliases)


def _pc(body, *, name, out_shape, grid=None, in_specs=None, out_specs=None, scratch=(), sem=None,
        grid_spec=None, comm=None, aliases=None):
    cp = dict(vmem_limit_bytes=VMEM_LIMIT)
    aliases = dict(aliases or {})
    if comm is None:
        if sem is not None:
            cp["dimension_semantics"] = sem
        kw = {"input_output_aliases": aliases}
        if grid_spec is not None:
            kw["grid_spec"] = grid_spec
        else:
            if grid is not None:
                kw["grid"] = grid
            if in_specs is not None:
                kw["in_specs"] = in_specs
            if out_specs is not None:
                kw["out_specs"] = out_specs
            kw["scratch_shapes"] = list(scratch)
        return pl.pallas_call(functools.partial(body), name=name, out_shape=out_shape,
                              compiler_params=pltpu.CompilerParams(**cp), **kw)

    single = not isinstance(out_shape, (list, tuple))
    outs_list = [out_shape] if single else list(out_shape)
    ospecs = [out_specs] if single else list(out_specs)
    n_in, n_out, n_ci, n_co, n_scr = len(in_specs), len(outs_list), len(comm.arrays), len(comm.out_shapes), len(scratch)
    cp["dimension_semantics"] = ("arbitrary",) * len(grid)

    def hosted(*refs):
        cin, hin = refs[:n_in], refs[n_in:n_in + n_ci]
        cout = refs[n_in + n_ci:n_in + n_ci + n_out]
        hout = refs[n_in + n_ci + n_out:n_in + n_ci + n_out + n_co]
        scr = refs[n_in + n_ci + n_out + n_co:n_in + n_ci + n_out + n_co + n_scr]
        ssem, rsem = refs[-2], refs[-1]
        first = functools.reduce(lambda p, q: p & q, [pl.program_id(a) == 0 for a in range(len(grid))])
        last = functools.reduce(lambda p, q: p & q, [pl.program_id(a) == grid[a] - 1 for a in range(len(grid))])

        @pl.when(first)
        def _():
            comm.start(hin, hout, ssem, rsem, 0)

        body(*cin, *cout, *scr)

        @pl.when(last)
        def _():
            comm.finish(hin, hout, ssem, rsem, 0)

    call = pl.pallas_call(
        hosted, name=name, grid=grid, in_specs=list(in_specs) + [ANY] * n_ci, out_specs=ospecs + [ANY] * n_co,
        out_shape=outs_list + comm.out_shapes,
        scratch_shapes=list(scratch) + [pltpu.SemaphoreType.DMA((comm.nsem,)), pltpu.SemaphoreType.DMA((comm.nsem,))],
        input_output_aliases={**aliases, **{n_in + i: n_out + o for i, o in comm.aliases.items()}},
        compiler_params=pltpu.CompilerParams(**cp))

    def run(*args):
        res = call(*args, *comm.arrays)
        comp = res[:n_out]
        return (comp[0] if single else comp), list(res[n_out:])

    return run


def _tile(n, pref):
    return pref if n % pref == 0 else n


def _sigmoid(x):
    return 1.0 / (1.0 + jnp.exp(-x))


def _gelu(x):
    c = math.sqrt(2.0 / math.pi)
    return 0.5 * x * (1.0 + jnp.tanh(c * (x + 0.044715 * (x * x * x))))


def _gelu_and_grad(x):
    c = math.sqrt(2.0 / math.pi)
    x2 = x * x
    t = jnp.tanh(c * (x + 0.044715 * (x2 * x)))
    half = 0.5 * (1.0 + t)
    return x * half, half + (0.5 * x) * (1.0 - t * t) * (c + (3.0 * 0.044715 * c) * x2)


def _dot(a, b):
    return jnp.dot(a, b, preferred_element_type=F32)


def _dot_nt(a, b):
    return lax.dot_general(a, b, (((1,), (1,)), ((), ())), preferred_element_type=F32)


def _dot_tn(a, b):
    return lax.dot_general(a, b, (((0,), (0,)), ((), ())), preferred_element_type=F32)


def _tri_mask(n, reverse):
    r = lax.broadcasted_iota(jnp.int32, (n, n), 0)
    c = lax.broadcasted_iota(jnp.int32, (n, n), 1)
    same = (r // CHUNK) == (c // CHUNK)
    tri = (c >= r) if reverse else (c <= r)
    return jnp.where(same & tri, 1.0, 0.0).astype(BF16)


def _tri_apply(tri, x):
    hi = x.astype(BF16)
    r1 = x - hi.astype(F32)
    mid = r1.astype(BF16)
    lo = (r1 - mid.astype(F32)).astype(BF16)
    return _dot(tri, hi) + (_dot(tri, mid) + _dot(tri, lo))


def _all_gather(arrs, name):
    n = len(arrs)

    def body(*refs):
        ins, outs = refs[:n], refs[n:2 * n]
        send_sems, recv_sems, local_sems = refs[2 * n:]
        x, y, c = lax.axis_index("x"), lax.axis_index("y"), lax.axis_index("c")
        me, sibling = (x, y, c), (x, y, 1 - c)
        near = (x + c - 2 * x * c, y + (1 - c) - 2 * y * (1 - c))
        far = (x + (1 - c) - 2 * x * (1 - c), y + c - 2 * y * c)
        diag = (1 - x, 1 - y)

        def blk(a, p):
            return outs[a].at[4 * p[0] + 2 * p[1] + p[2]]

        def copy(a, k, block, to, src=None):
            return pltpu.make_async_remote_copy(
                src_ref=blk(a, block) if src is None else src, dst_ref=blk(a, block),
                send_sem=send_sems.at[7 * a + k], recv_sem=recv_sems.at[7 * a + k],
                device_id=to, device_id_type=MESH)

        mine = [pltpu.make_async_copy(ins[a], blk(a, me), local_sems.at[a]) for a in range(n)]
        for m in mine:
            m.start()
        sends = []
        for a in range(n):
            sends += [copy(a, 0, me, sibling, src=ins[a]), copy(a, 1, me, (*near, c), src=ins[a]),
                      copy(a, 2, me, (*far, c), src=ins[a])]
        for cp in sends:
            cp.start()
        for a in range(n):
            copy(a, 1, (*near, c), me).wait_recv()
            sends.append(copy(a, 3, (*near, c), (*far, c)))
            sends[-1].start()
        for a in range(n):
            sends.append(copy(a, 4, (*near, c), sibling))
            sends[-1].start()
            copy(a, 2, (*far, c), me).wait_recv()
            sends.append(copy(a, 5, (*far, c), sibling))
            sends[-1].start()
        for a in range(n):
            copy(a, 3, (*diag, c), me).wait_recv()
            sends.append(copy(a, 6, (*diag, c), sibling))
            sends[-1].start()
        for a in range(n):
            copy(a, 0, sibling, me).wait_recv()
            copy(a, 4, (*far, 1 - c), me).wait_recv()
            copy(a, 5, (*near, 1 - c), me).wait_recv()
            copy(a, 6, (*diag, 1 - c), me).wait_recv()
        for cp in sends:
            cp.wait_send()
        for m in mine:
            m.wait()

    out_shape = [jax.ShapeDtypeStruct((NDEV,) + a.shape, a.dtype) for a in arrs]
    return _pc(body, name=name, out_shape=out_shape, in_specs=[ANY] * n, out_specs=[ANY] * n,
               scratch=[pltpu.SemaphoreType.DMA((7 * n,)), pltpu.SemaphoreType.DMA((7 * n,)),
                        pltpu.SemaphoreType.DMA((n,))])(*arrs)


def _gather_first(arrs):
    n = len(arrs)

    def parts(ins, outs, ss, rs, base):
        x, y, c = lax.axis_index("x"), lax.axis_index("y"), lax.axis_index("c")
        me, sibling = (x, y, c), (x, y, 1 - c)
        chips = [(1 - x, y), (x, 1 - y), (1 - x, 1 - y)]

        def blk(a, p):
            return outs[a].at[4 * p[0] + 2 * p[1] + p[2]]

        def copy(a, k, block, to):
            return pltpu.make_async_remote_copy(
                src_ref=ins[a], dst_ref=blk(a, block), send_sem=ss.at[base + 4 * a + k],
                recv_sem=rs.at[base + 4 * a + k], device_id=to, device_id_type=MESH)

        local = [pltpu.make_async_copy(ins[a], blk(a, me), ss.at[base + 4 * n + a]) for a in range(n)]
        sends, recvs = [], []
        for a in range(n):
            sends.append(copy(a, 0, me, sibling))
            recvs.append(copy(a, 0, sibling, me))
            for j, chip in enumerate(chips):
                sends.append(copy(a, 1 + j, me, (*chip, c)))
                recvs.append(copy(a, 1 + j, (*chip, c), me))
        return local, sends, recvs

    def start(ins, outs, ss, rs, base):
        local, sends, _ = parts(ins, outs, ss, rs, base)
        for cp in local + sends:
            cp.start()

    def finish(ins, outs, ss, rs, base):
        local, sends, recvs = parts(ins, outs, ss, rs, base)
        for cp in recvs:
            cp.wait_recv()
        for cp in sends:
            cp.wait_send()
        for cp in local:
            cp.wait()

    return _Hosted(arrs, [jax.ShapeDtypeStruct((NDEV,) + a.shape, a.dtype) for a in arrs], 5 * n, start, finish)


def _gather_second(bufs):
    n = len(bufs)

    def parts(ins, outs, ss, rs, base):
        x, y, c = lax.axis_index("x"), lax.axis_index("y"), lax.axis_index("c")
        sibling = (x, y, 1 - c)
        chips = [(1 - x, y), (x, 1 - y), (1 - x, 1 - y)]
        sends, recvs = [], []
        for a in range(n):
            for j, chip in enumerate(chips):
                mine = 4 * chip[0] + 2 * chip[1] + c
                theirs = 4 * chip[0] + 2 * chip[1] + (1 - c)
                sends.append(pltpu.make_async_remote_copy(
                    src_ref=ins[a].at[mine], dst_ref=outs[a].at[mine], send_sem=ss.at[base + 3 * a + j],
                    recv_sem=rs.at[base + 3 * a + j], device_id=sibling, device_id_type=MESH))
                recvs.append(pltpu.make_async_remote_copy(
                    src_ref=ins[a].at[theirs], dst_ref=outs[a].at[theirs], send_sem=ss.at[base + 3 * a + j],
                    recv_sem=rs.at[base + 3 * a + j], device_id=sibling, device_id_type=MESH))
        return sends, recvs

    def start(ins, outs, ss, rs, base):
        for cp in parts(ins, outs, ss, rs, base)[0]:
            cp.start()

    def finish(ins, outs, ss, rs, base):
        sends, recvs = parts(ins, outs, ss, rs, base)
        for cp in recvs:
            cp.wait_recv()
        for cp in sends:
            cp.wait_send()

    return _Hosted(bufs, [jax.ShapeDtypeStruct(b.shape, b.dtype) for b in bufs], 3 * n, start, finish,
                   aliases={a: a for a in range(n)})


def _swap(src, nblk, ids_fn, partner_fn):
    def copies(ins, outs, ss, rs, base):
        x, y, c = lax.axis_index("x"), lax.axis_index("y"), lax.axis_index("c")
        ids = ids_fn(x, y, c)
        partner = partner_fn(x, y, c)
        return [pltpu.make_async_remote_copy(
            src_ref=ins[0].at[ids[k]], dst_ref=outs[0].at[k], send_sem=ss.at[base + k], recv_sem=rs.at[base + k],
            device_id=partner, device_id_type=MESH) for k in range(nblk)]

    def start(ins, outs, ss, rs, base):
        for cp in copies(ins, outs, ss, rs, base):
            cp.start()

    def finish(ins, outs, ss, rs, base):
        for cp in copies(ins, outs, ss, rs, base):
            cp.wait()

    return _Hosted([src], [jax.ShapeDtypeStruct((nblk,) + src.shape[1:], src.dtype)], nblk, start, finish)


def _blocking(comm, name):
    n_i, n_o = len(comm.arrays), len(comm.out_shapes)

    def body(*refs):
        ins, outs = refs[:n_i], refs[n_i:n_i + n_o]
        comm.start(ins, outs, refs[-2], refs[-1], 0)
        comm.finish(ins, outs, refs[-2], refs[-1], 0)

    return pl.pallas_call(
        body, name=name, out_shape=comm.out_shapes, in_specs=[ANY] * n_i, out_specs=[ANY] * n_o,
        scratch_shapes=[pltpu.SemaphoreType.DMA((comm.nsem,)), pltpu.SemaphoreType.DMA((comm.nsem,))],
        input_output_aliases=comm.aliases)(*comm.arrays)


def _swap_chips(send):
    def copies(ins, outs, ss, rs, base):
        x, y, c = lax.axis_index("x"), lax.axis_index("y"), lax.axis_index("c")
        chips = [(1 - x, y), (x, 1 - y), (1 - x, 1 - y)]
        return [pltpu.make_async_remote_copy(
            src_ref=ins[0].at[j], dst_ref=outs[0].at[j], send_sem=ss.at[base + j], recv_sem=rs.at[base + j],
            device_id=(*chip, c), device_id_type=MESH) for j, chip in enumerate(chips)]

    def start(ins, outs, ss, rs, base):
        for cp in copies(ins, outs, ss, rs, base):
            cp.start()

    def finish(ins, outs, ss, rs, base):
        for cp in copies(ins, outs, ss, rs, base):
            cp.wait()

    return _Hosted([send], [jax.ShapeDtypeStruct(send.shape, send.dtype)], 3, start, finish)


def _add_send(a, b, idx, ns, name):
    _, r, c = a.shape
    tr = _tile(r, 256)

    def body(idx_ref, a_ref, b_ref, send_ref):
        send_ref[...] = (a_ref[...] + b_ref[...]).astype(BF16)

    def sel(off):
        return pl.BlockSpec((None, tr, c), lambda k, i, s: (s[off + k], i, 0))

    gs = pltpu.PrefetchScalarGridSpec(num_scalar_prefetch=1, grid=(ns, r // tr), in_specs=[sel(0), sel(ns)],
                                      out_specs=pl.BlockSpec((None, tr, c), lambda k, i, s: (k, i, 0)))
    return _pc(body, name=name, grid_spec=gs, sem=("arbitrary", "arbitrary"),
               out_shape=jax.ShapeDtypeStruct((ns, r, c), BF16))(idx, a, b)


class _ReduceScatter:
    def __init__(self, g, tag):
        self.g, self.tag = g, tag

    def swap_core(self):
        return _swap(self.g, 4, lambda x, y, c: [1 - c, 3 - c, 5 - c, 7 - c], lambda x, y, c: (x, y, 1 - c))

    def after_core(self, recv):
        x, y, c = lax.axis_index("x"), lax.axis_index("y"), lax.axis_index("c")
        chips = [(1 - x, y), (x, 1 - y), (1 - x, 1 - y)]
        idx = jnp.stack([4 * p + 2 * q + c for p, q in chips] + [2 * p + q for p, q in chips]).astype(jnp.int32)
        self.send = _add_send(self.g, recv, idx, 3, "rs_add_" + self.tag)
        self.recv_core = recv
        zero = jnp.zeros((), jnp.int32)
        self.idx = jnp.stack([4 * x + 2 * y + c, 2 * x + y, zero, zero + 1, zero + 2]).astype(jnp.int32)

    def swap_chips(self):
        return _swap_chips(self.send)

    def after_chips(self, recv):
        self.parts = [self.g, self.recv_core, recv, recv, recv]


def _ada_fwd(c_all, w_ada, b_cols, b_lb):
    nl, d, ncol = w_ada.shape
    nseq = c_all.shape[0]
    di = b_lb.shape[1]

    def body(c_ref, w_ref, b_ref, lb_ref, mod_ref, lbj_ref):
        cv = c_ref[...]
        cact = (cv * _sigmoid(cv)).astype(BF16)
        for l in range(nl):
            mod_ref[l] = _dot(cact, w_ref[l].astype(BF16)) + b_ref[l]
        b0, b1 = lb_ref[0:1, :], lb_ref[1:2, :]
        mx = jnp.maximum(b0, b1)
        e0, e1 = jnp.exp(b0 - mx), jnp.exp(b1 - mx)
        s = e0 + e1
        p0, p1 = e0 / s, e1 / s
        lbj_ref[0:1, :] = (p0 + p1) - p0
        lbj_ref[1:2, :] = p0 * p1

    return _pc(body, name="ada_fwd",
               out_shape=[jax.ShapeDtypeStruct((nl, nseq, ncol), F32), jax.ShapeDtypeStruct((2, di), F32)]
               )(c_all, w_ada, b_cols, b_lb)


def _ada_bwd(c_all, dmod_cols, dmod_full):
    nl, nseq, ncol = dmod_cols.shape
    d = c_all.shape[1]
    d3 = dmod_full.shape[2]

    def body(c_ref, dc_ref, df_ref, gw_ref, gb_ref):
        cv = c_ref[...]
        cact = (cv * _sigmoid(cv)).astype(BF16)
        for l in range(nl):
            gw_ref[l] = _dot_tn(cact, dc_ref[l].astype(BF16))
            gb_ref[l:l + 1, :] = jnp.sum(df_ref[l], axis=0, keepdims=True)

    return _pc(body, name="ada_bwd",
               out_shape=[jax.ShapeDtypeStruct((nl, d, ncol), F32), jax.ShapeDtypeStruct((nl, d3), F32)]
               )(c_all, dmod_cols, dmod_full)


def _prenorm(x, gain, mod, t_seq, name, comm=None):
    m, d = x.shape
    tm = _tile(t_seq, 1024)
    per = t_seq // tm

    def body(x_ref, g_ref, mod_ref, h_ref, ht_ref):
        xv = x_ref[...]
        rstd = lax.rsqrt(jnp.mean(xv * xv, axis=-1, keepdims=True) + EPS)
        r = xv * rstd * g_ref[...]
        h = r * (1.0 + mod_ref[0, 1:2, :]) + mod_ref[0, 0:1, :]
        h_ref[...] = h.astype(BF16)
        ht_ref[...] = h.T.astype(BF16)

    return _pc(body, name=name, out_shape=[jax.ShapeDtypeStruct((m, d), BF16), jax.ShapeDtypeStruct((d, m), BF16)],
               grid=(m // tm,),
               in_specs=[pl.BlockSpec((tm, d), lambda i: (i, 0)), pl.BlockSpec((1, d), lambda i: (0, 0)),
                         pl.BlockSpec((1, 3, d), lambda i: (i // per, 0, 0))],
               out_specs=[pl.BlockSpec((tm, d), lambda i: (i, 0)), pl.BlockSpec((d, tm), lambda i: (0, i))],
               sem=("parallel",), comm=comm)(x, gain, mod)


def _prenorm_bwd(dh, x, gain, mod, dxn, t_seq, name, comm=None):
    m, d = x.shape
    nb = m // t_seq
    tm = _tile(t_seq, 1024)
    per = t_seq // tm

    def body(dh_ref, x_ref, g_ref, mod_ref, dxn_ref, dx_ref, dss_ref, dg_ref):
        i = pl.program_id(0)
        xv, dhv, g = x_ref[...], dh_ref[...], g_ref[...]
        rstd = lax.rsqrt(jnp.mean(xv * xv, axis=-1, keepdims=True) + EPS)
        xhat = xv * rstd
        dr = dhv * (1.0 + mod_ref[0, 1:2, :])
        dxhat = dr * g
        dx_ref[...] = dxn_ref[...] + rstd * (dxhat - xhat * jnp.mean(dxhat * xhat, axis=-1, keepdims=True))

        @pl.when(i % per == 0)
        def _():
            dss_ref[...] = jnp.zeros_like(dss_ref)

        @pl.when(i == 0)
        def _():
            dg_ref[...] = jnp.zeros_like(dg_ref)

        dss_ref[0, 0:1, :] += jnp.sum(dhv, axis=0, keepdims=True)
        dss_ref[0, 1:2, :] += jnp.sum(dhv * (xhat * g), axis=0, keepdims=True)
        dg_ref[...] += jnp.sum(dr * xhat, axis=0, keepdims=True)

    row = pl.BlockSpec((tm, d), lambda i: (i, 0))
    return _pc(body, name=name,
               out_shape=[jax.ShapeDtypeStruct((m, d), F32), jax.ShapeDtypeStruct((nb, 2, d), F32),
                          jax.ShapeDtypeStruct((1, d), F32)],
               grid=(m // tm,),
               in_specs=[row, row, pl.BlockSpec((1, d), lambda i: (0, 0)),
                         pl.BlockSpec((1, 3, d), lambda i: (i // per, 0, 0)), row],
               out_specs=[row, pl.BlockSpec((1, 2, d), lambda i: (i // per, 0, 0)),
                          pl.BlockSpec((1, d), lambda i: (0, 0))],
               sem=("arbitrary",), comm=comm)(dh, x, gain, mod, dxn)


def _mm_in(h, ws, sections, name, comm=None):
    m, k = h.shape
    nw = len(ws)
    widths = [w.shape[2] for w in ws]
    offs = [sum(widths[:a]) for a in range(nw)]
    nc = sum(widths)
    per = NDEV // sections if sections > 1 else NDEV
    tm = _din_tile(m)
    assert per % 2 == 0

    def body(*refs):
        hv = refs[0][...]
        o_ref = refs[1 + nw]
        for b in range(2):
            for a in range(nw):
                lo = b * nc + offs[a]
                o_ref[:, lo:lo + widths[a]] = _dot(hv, refs[1 + a][b])

    w_specs = [pl.BlockSpec((2, k, wd), lambda j, i: (j, 0, 0)) for wd in widths]
    if sections > 1:
        out_shape = jax.ShapeDtypeStruct((sections, m, per * nc), F32)
        out_spec = pl.BlockSpec((None, tm, 2 * nc), lambda j, i: ((2 * j) // per, i, ((2 * j) % per) // 2))
    else:
        out_shape = jax.ShapeDtypeStruct((m, NDEV * nc), F32)
        out_spec = pl.BlockSpec((tm, 2 * nc), lambda j, i: (i, j))
    return _pc(body, name=name, out_shape=out_shape, grid=(NDEV // 2, m // tm),
               in_specs=[pl.BlockSpec((tm, k), lambda j, i: (i, 0))] + w_specs,
               out_specs=out_spec, sem=("parallel", "parallel"), comm=comm)(h, *ws)


def _din_tile(m):
    return 1024 if m % 1024 == 0 and m >= 2048 else _tile(m, 512)


def _mm_din(dproj, ws, sections, name, comm=None, tiles=None, prev=None):
    nw, k = len(ws), ws[0].shape[1]
    widths = [w.shape[2] for w in ws]
    offs = [sum(widths[:a]) for a in range(nw)]
    nc = sum(widths)
    m = dproj.shape[-2]
    tm = _din_tile(m)
    t0, nt = tiles if tiles is not None else (0, m // tm)
    per = NDEV // sections if sections > 1 else NDEV
    assert per % 2 == 0

    def body(*refs):
        d_ref, o_ref = refs[0], refs[-1]
        j = pl.program_id(1)
        acc = None
        for b in range(2):
            for a in range(nw):
                lo = b * nc + offs[a]
                term = _dot_nt(d_ref[:, lo:lo + widths[a]], refs[1 + a][b])
                acc = term if acc is None else acc + term

        @pl.when(j == 0)
        def _():
            o_ref[...] = acc

        @pl.when(j > 0)
        def _():
            o_ref[...] += acc

    if sections > 1:
        dspec = pl.BlockSpec((None, tm, 2 * nc), lambda i, j: ((2 * j) // per, i + t0, ((2 * j) % per) // 2))
    else:
        dspec = pl.BlockSpec((tm, 2 * nc), lambda i, j: (i + t0, j))
    in_specs = [dspec] + [pl.BlockSpec((2, k, wd), lambda i, j: (j, 0, 0)) for wd in widths]
    args = [dproj, *ws]
    if prev is not None:
        in_specs.append(ANY)
        args.append(prev)
    return _pc(body, name=name, out_shape=jax.ShapeDtypeStruct((m, k), F32), grid=(nt, NDEV // 2), in_specs=in_specs,
               out_specs=pl.BlockSpec((tm, k), lambda i, j: (i + t0, 0)), sem=("parallel", "arbitrary"),
               comm=comm, aliases={1 + nw: 0} if prev is not None else None)(*args)


def _mm_dw_in(ht, dproj, nc, sections, name, comm=None):
    k, m = ht.shape
    tk = 2048 if m % 2048 == 0 else _din_tile(m)
    per = NDEV // sections if sections > 1 else NDEV

    def body(h_ref, d_ref, o_ref):
        kk = pl.program_id(1)
        acc = _dot(h_ref[...], d_ref[...])

        @pl.when(kk == 0)
        def _():
            o_ref[...] = acc

        @pl.when(kk > 0)
        def _():
            o_ref[...] += acc

    if sections > 1:
        dspec = pl.BlockSpec((None, tk, nc), lambda j, i: (j // per, i, j % per))
    else:
        dspec = pl.BlockSpec((tk, nc), lambda j, i: (i, j))
    return _pc(body, name=name, out_shape=jax.ShapeDtypeStruct((NDEV, k, nc), F32), grid=(NDEV, m // tk),
               in_specs=[pl.BlockSpec((k, tk), lambda j, i: (0, i)), dspec],
               out_specs=pl.BlockSpec((None, k, nc), lambda j, i: (j, 0, 0)),
               sem=("parallel", "arbitrary"), comm=comm)(ht, dproj)


def _out_proj(ybr, w_out, x, mod, t_seq, name, comm=None):
    m, di = ybr.shape
    d = w_out.shape[1]
    tm = _tile(t_seq, 512)
    per = t_seq // tm

    def body(y_ref, w_ref, x_ref, mod_ref, yo_ref, xn_ref):
        yo = _dot(y_ref[...], w_ref[...])
        yo_ref[...] = yo
        xn_ref[...] = x_ref[...] + mod_ref[0, 2:3, :] * yo

    row = pl.BlockSpec((tm, d), lambda i: (i, 0))
    return _pc(body, name=name,
               out_shape=[jax.ShapeDtypeStruct((m, d), F32), jax.ShapeDtypeStruct((m, d), F32)],
               grid=(m // tm,),
               in_specs=[pl.BlockSpec((tm, di), lambda i: (i, 0)), pl.BlockSpec((di, d), lambda i: (0, 0)), row,
                         pl.BlockSpec((1, 3, d), lambda i: (i // per, 0, 0))],
               out_specs=[row, row], sem=("parallel",), comm=comm)(ybr, w_out, x, mod)


def _out_proj_loss(ybr, w_out, x, mod, gain, target, t_seq):
    m, di = ybr.shape
    d = w_out.shape[1]
    tm = _tile(t_seq, 512)
    per = t_seq // tm

    def body(y_ref, w_ref, x_ref, mod_ref, g_ref, t_ref, yo_ref, dx_ref, loss_ref, dg_ref):
        i = pl.program_id(0)
        yo = _dot(y_ref[...], w_ref[...])
        yo_ref[...] = yo
        xv = x_ref[...] + mod_ref[0, 2:3, :] * yo
        g = g_ref[...]
        rstd = lax.rsqrt(jnp.mean(xv * xv, axis=-1, keepdims=True) + EPS)
        xhat = xv * rstd
        err = xhat * g - t_ref[...]
        dy = err * (1.0 / d)
        dxhat = dy * g
        dx_ref[...] = rstd * (dxhat - xhat * jnp.mean(dxhat * xhat, axis=-1, keepdims=True))

        @pl.when(i == 0)
        def _():
            loss_ref[...] = jnp.zeros_like(loss_ref)
            dg_ref[...] = jnp.zeros_like(dg_ref)

        loss_ref[...] += 0.5 * jnp.sum(jnp.mean(err * err, axis=-1, keepdims=True), axis=0, keepdims=True)
        dg_ref[...] += jnp.sum(dy * xhat, axis=0, keepdims=True)

    row = pl.BlockSpec((tm, d), lambda i: (i, 0))
    vec = pl.BlockSpec((1, d), lambda i: (0, 0))
    return _pc(body, name="out_proj_loss",
               out_shape=[jax.ShapeDtypeStruct((m, d), F32), jax.ShapeDtypeStruct((m, d), F32),
                          jax.ShapeDtypeStruct((1, 1), F32), jax.ShapeDtypeStruct((1, d), F32)],
               grid=(m // tm,),
               in_specs=[pl.BlockSpec((tm, di), lambda i: (i, 0)), pl.BlockSpec((di, d), lambda i: (0, 0)), row,
                         pl.BlockSpec((1, 3, d), lambda i: (i // per, 0, 0)), vec, row],
               out_specs=[row, row, pl.BlockSpec((1, 1), lambda i: (0, 0)), vec],
               sem=("arbitrary",))(ybr, w_out, x, mod, gain, target)


def _gate_dybr(dxn, yout, mod, w_out, t_seq, name):
    m, d = dxn.shape
    di = w_out.shape[0]
    nb = m // t_seq
    tm = _tile(t_seq, 512)
    per = t_seq // tm

    def body(dxn_ref, yo_ref, mod_ref, w_ref, dy_ref, dgate_ref, o_ref):
        i = pl.program_id(0)
        dv = dxn_ref[...]
        dy = (mod_ref[0, 2:3, :] * dv).astype(BF16)
        dy_ref[...] = dy
        o_ref[...] = _dot_nt(dy, w_ref[...])

        @pl.when(i % per == 0)
        def _():
            dgate_ref[...] = jnp.zeros_like(dgate_ref)

        dgate_ref[0] += jnp.sum(dv * yo_ref[...], axis=0, keepdims=True)

    row = pl.BlockSpec((tm, d), lambda i: (i, 0))
    return _pc(body, name=name,
               out_shape=[jax.ShapeDtypeStruct((m, d), BF16), jax.ShapeDtypeStruct((nb, 1, d), F32),
                          jax.ShapeDtypeStruct((m, di), F32)],
               grid=(m // tm,),
               in_specs=[row, row, pl.BlockSpec((1, 3, d), lambda i: (i // per, 0, 0)),
                         pl.BlockSpec((di, d), lambda i: (0, 0))],
               out_specs=[row, pl.BlockSpec((1, 1, d), lambda i: (i // per, 0, 0)),
                          pl.BlockSpec((tm, di), lambda i: (i, 0))],
               sem=("arbitrary",))(dxn, yout, mod, w_out)


def _mm_dw_out(ybr, dy, name, comm=None):
    m, di = ybr.shape
    d = dy.shape[1]
    tk = 2048 if m % 2048 == 0 else _tile(m, 512)
    tn = _tile(di, 1024)

    def body(y_ref, dy_ref, o_ref):
        kk = pl.program_id(1)
        acc = _dot_tn(y_ref[...], dy_ref[...])

        @pl.when(kk == 0)
        def _():
            o_ref[...] = acc

        @pl.when(kk > 0)
        def _():
            o_ref[...] += acc

    return _pc(body, name=name, out_shape=jax.ShapeDtypeStruct((di, d), F32), grid=(di // tn, m // tk),
               in_specs=[pl.BlockSpec((tk, tn), lambda n, k: (k, n)), pl.BlockSpec((tk, d), lambda n, k: (k, 0))],
               out_specs=pl.BlockSpec((tn, d), lambda n, k: (n, 0)), sem=("parallel", "arbitrary"),
               comm=comm)(ybr, dy)


def _sgu_mask():
    t = lax.broadcasted_iota(jnp.int32, (SG_BLOCK, SG_BLOCK), 0)
    s = lax.broadcasted_iota(jnp.int32, (SG_BLOCK, SG_BLOCK), 1)
    return (s // CHUNK) <= (t // CHUNK)


def _a_mid_fwd(proj, ln_g, ln_b, w_s, bs_t, t_seq, comm=None):
    m, n3 = proj.shape
    di = n3 // 3
    gd = di // SG_GROUPS
    r = _tile(t_seq, 256)
    nblk = r // SG_BLOCK

    def body(p_ref, lg_ref, lb_ref, ws_ref, bs_ref, ybr_ref, s_scr):
        v = _gelu(p_ref[:, di:2 * di])
        mu = jnp.mean(v, axis=-1, keepdims=True)
        vc = v - mu
        rstd = lax.rsqrt(jnp.mean(vc * vc, axis=-1, keepdims=True) + EPS)
        vb = (vc * rstd * lg_ref[...] + lb_ref[...]).astype(BF16)
        mask = _sgu_mask()
        for gi in range(SG_GROUPS):
            ws = jnp.where(mask, ws_ref[gi], 0.0).astype(BF16)
            bcol = bs_ref[:, gi:gi + 1]
            for b in range(nblk):
                rows = slice(b * SG_BLOCK, (b + 1) * SG_BLOCK)
                cols = slice(gi * gd, (gi + 1) * gd)
                s_scr[rows, cols] = _dot(ws, vb[rows, cols]) + bcol
        gg = p_ref[:, 2 * di:]
        ybr_ref[...] = (_gelu(p_ref[:, :di]) * s_scr[...] * (gg * _sigmoid(gg))).astype(BF16)

    vec = pl.BlockSpec((1, di), lambda i: (0, 0))
    return _pc(body, name="a_mid_fwd", out_shape=jax.ShapeDtypeStruct((m, di), BF16), grid=(m // r,),
               in_specs=[pl.BlockSpec((r, n3), lambda i: (i, 0)), vec, vec,
                         pl.BlockSpec((SG_GROUPS, SG_BLOCK, SG_BLOCK), lambda i: (0, 0, 0)),
                         pl.BlockSpec((SG_BLOCK, 128), lambda i: (0, 0))],
               out_specs=pl.BlockSpec((r, di), lambda i: (i, 0)),
               scratch=[pltpu.VMEM((r, di), F32)], sem=("parallel",), comm=comm)(proj, ln_g, ln_b, w_s, bs_t)


def _a_mid_bwd(proj, dybr, ln_g, ln_b, w_s, bs_t, t_seq, comm=None):
    m, n3 = proj.shape
    di = n3 // 3
    gd = di // SG_GROUPS
    r = _tile(t_seq, 256)
    nblk = r // SG_BLOCK

    def body(p_ref, dy_ref, lg_ref, lb_ref, ws_ref, bs_ref,
             dp_ref, dlg_ref, dlb_ref, dws_ref, dbs_ref, s_scr, dvl_scr):
        i = pl.program_id(0)

        @pl.when(i == 0)
        def _():
            dlg_ref[...] = jnp.zeros_like(dlg_ref)
            dlb_ref[...] = jnp.zeros_like(dlb_ref)
            dws_ref[...] = jnp.zeros_like(dws_ref)
            dbs_ref[...] = jnp.zeros_like(dbs_ref)

        v, dgelu_v = _gelu_and_grad(p_ref[:, di:2 * di])
        mu = jnp.mean(v, axis=-1, keepdims=True)
        vc = v - mu
        rstd = lax.rsqrt(jnp.mean(vc * vc, axis=-1, keepdims=True) + EPS)
        vhat = vc * rstd
        lg = lg_ref[...]
        vb = (vhat * lg + lb_ref[...]).astype(BF16)
        u, dgelu_u = _gelu_and_grad(p_ref[:, :di])
        gg = p_ref[:, 2 * di:]
        sg = _sigmoid(gg)
        dyv = dy_ref[...]
        dus = dyv * (gg * sg)
        dsb = (dus * u).astype(BF16)
        ds32 = dus * u
        mask = _sgu_mask()
        lane = lax.broadcasted_iota(jnp.int32, (SG_BLOCK, 128), 1)
        dbs_acc = jnp.zeros((SG_BLOCK, 128), F32)
        for gi in range(SG_GROUPS):
            ws = jnp.where(mask, ws_ref[gi], 0.0).astype(BF16)
            bcol = bs_ref[:, gi:gi + 1]
            cols = slice(gi * gd, (gi + 1) * gd)
            dws_acc = jnp.zeros((SG_BLOCK, SG_BLOCK), F32)
            dbs_col = jnp.zeros((SG_BLOCK, 1), F32)
            for b in range(nblk):
                rows = slice(b * SG_BLOCK, (b + 1) * SG_BLOCK)
                s_scr[rows, cols] = _dot(ws, vb[rows, cols]) + bcol
                dvl_scr[rows, cols] = _dot_tn(ws, dsb[rows, cols])
                dws_acc += _dot_nt(dsb[rows, cols], vb[rows, cols])
                dbs_col += jnp.sum(ds32[rows, cols], axis=-1, keepdims=True)
            dws_ref[gi] += jnp.where(mask, dws_acc, 0.0)
            dbs_acc += jnp.where(lane == gi, dbs_col, 0.0)
        dbs_ref[...] += dbs_acc
        s = s_scr[...]
        dp_ref[:, :di] = (dus * s * dgelu_u).astype(BF16)
        dp_ref[:, 2 * di:] = (dyv * u * s * (sg * (1.0 + gg * (1.0 - sg)))).astype(BF16)
        dvl = dvl_scr[...]
        dlg_ref[...] += jnp.sum(dvl * vhat, axis=0, keepdims=True)
        dlb_ref[...] += jnp.sum(dvl, axis=0, keepdims=True)
        dvh = dvl * lg
        dv = rstd * (dvh - jnp.mean(dvh, axis=-1, keepdims=True)
                     - vhat * jnp.mean(dvh * vhat, axis=-1, keepdims=True))
        dp_ref[:, di:2 * di] = (dv * dgelu_v).astype(BF16)

    vec = pl.BlockSpec((1, di), lambda i: (0, 0))
    wsb = pl.BlockSpec((SG_GROUPS, SG_BLOCK, SG_BLOCK), lambda i: (0, 0, 0))
    bsb = pl.BlockSpec((SG_BLOCK, 128), lambda i: (0, 0))
    return _pc(body, name="a_mid_bwd",
               out_shape=[jax.ShapeDtypeStruct((m, n3), BF16), jax.ShapeDtypeStruct((1, di), F32),
                          jax.ShapeDtypeStruct((1, di), F32),
                          jax.ShapeDtypeStruct((SG_GROUPS, SG_BLOCK, SG_BLOCK), F32),
                          jax.ShapeDtypeStruct((SG_BLOCK, 128), F32)],
               grid=(m // r,),
               in_specs=[pl.BlockSpec((r, n3), lambda i: (i, 0)), pl.BlockSpec((r, di), lambda i: (i, 0)),
                         vec, vec, wsb, bsb],
               out_specs=[pl.BlockSpec((r, n3), lambda i: (i, 0)), vec, vec, wsb, bsb],
               scratch=[pltpu.VMEM((r, di), F32), pltpu.VMEM((r, di), F32)],
               sem=("arbitrary",), comm=comm)(proj, dybr, ln_g, ln_b, w_s, bs_t)


def _hgrn_dims(t_seq, di):
    tr = _tile(t_seq, 256)
    hc = _tile(di, 1024)
    return tr, hc, hc // HEAD_DIM


def _hgrn_gates(f_ref, lb, a_scr, k_scr, tr):
    sig = _sigmoid(f_ref[...])
    fg = lb + (1.0 - lb) * sig
    k_scr[...] = 1.0 - fg
    logf = jnp.log(fg)
    g = min(CUM_ROWS, tr)
    tri = _tri_mask(g, reverse=False)
    for rg in range(tr // g):
        a_scr[rg * g:(rg + 1) * g, :] = _tri_apply(tri, logf[rg * g:(rg + 1) * g, :])
    return sig, fg


def _hgrn_fwd(proj, lbj, gn, nb, t_seq, comm=None):
    _, m, di = proj.shape
    tr, hc, hpg = _hgrn_dims(t_seq, di)
    nt, nhg, ncl = t_seq // tr, di // hc, tr // CHUNK
    nheads = di // HEAD_DIM

    def body(p_ref, lb_ref, gn_ref, o_ref, ybr_ref, st_ref, st_scr, a_scr, k_scr):
        q_ref, f_ref, i_ref, g_ref = (p_ref.at[s] for s in range(4))
        t = pl.program_id(2)

        @pl.when(t == 0)
        def _():
            st_scr[...] = jnp.zeros_like(st_scr)

        _hgrn_gates(f_ref, lb_ref[0:1, :], a_scr, k_scr, tr)
        gnv = gn_ref[...]
        rr = lax.broadcasted_iota(jnp.int32, (CHUNK, CHUNK), 0)
        cc = lax.broadcasted_iota(jnp.int32, (CHUNK, CHUNK), 1)
        causal = cc <= rr

        def chunk(n, carry):
            rows = pl.ds(pl.multiple_of(n * CHUNK, CHUNK), CHUNK)
            lanes = [slice(hd * HEAD_DIM, (hd + 1) * HEAD_DIM) for hd in range(hpg)]
            hs = []
            for hd, ls in enumerate(lanes):
                h = {}
                ah, kh = a_scr[rows, ls], k_scr[rows, ls]
                qp = q_ref[rows, ls]
                qh = qp * _sigmoid(qp)
                h["vb"] = i_ref[rows, ls].astype(BF16)
                aref, alast = ah[CHUNK // 2 - 1:CHUNK // 2, :], ah[CHUNK - 1:CHUNK, :]
                h["q_in"] = (qh * jnp.exp(ah - aref)).astype(BF16)
                h["k_in"] = (kh * jnp.exp(aref - ah)).astype(BF16)
                h["q_out"] = (qh * jnp.exp(ah)).astype(BF16)
                h["k_out"] = (kh * jnp.exp(alast - ah)).astype(BF16)
                h["dec"] = jnp.exp(alast)
                st = st_scr[hd]
                st_ref[n, hd] = st
                h["st"] = st
                hs.append(h)
            for h in hs:
                h["scores"] = _dot_nt(h["q_in"], h["k_in"])
                h["o_inter"] = _dot_nt(h["q_out"], h["st"].astype(BF16))
                h["st_mm"] = _dot_tn(h["vb"], h["k_out"])
            for h in hs:
                h["o"] = _dot(jnp.where(causal, h["scores"], 0.0).astype(BF16), h["vb"]) + h["o_inter"]
            for hd, (h, ls) in enumerate(zip(hs, lanes)):
                st_scr[hd] = h["st"] * h["dec"] + h["st_mm"]
                o = h["o"]
                o_ref[rows, ls] = o
                rstd = lax.rsqrt(jnp.mean(o * o, axis=-1, keepdims=True) + EPS)
                gg = g_ref[rows, ls]
                ybr_ref[rows, ls] = ((o * rstd * gnv) * (gg * _sigmoid(gg))).astype(BF16)
            return carry

        lax.fori_loop(0, ncl, chunk, 0)

    blk = pl.BlockSpec((tr, hc), lambda hg, b, t: (b * nt + t, hg))
    return _pc(body, name="hgrn_fwd",
               out_shape=[jax.ShapeDtypeStruct((m, di), F32), jax.ShapeDtypeStruct((m, di), BF16),
                          jax.ShapeDtypeStruct((m // CHUNK, nheads, HEAD_DIM, HEAD_DIM), F32)],
               grid=(nhg, nb, nt),
               in_specs=[pl.BlockSpec((4, tr, hc), lambda hg, b, t: (0, b * nt + t, hg)),
                         pl.BlockSpec((2, hc), lambda hg, b, t: (0, hg)),
                         pl.BlockSpec((1, HEAD_DIM), lambda hg, b, t: (0, 0))],
               out_specs=[blk, blk, pl.BlockSpec((ncl, hpg, HEAD_DIM, HEAD_DIM),
                                                 lambda hg, b, t: (b * nt + t, hg, 0, 0))],
               scratch=[pltpu.VMEM((hpg, HEAD_DIM, HEAD_DIM), F32), pltpu.VMEM((tr, hc), F32),
                        pltpu.VMEM((tr, hc), F32)],
               sem=("parallel", "arbitrary", "arbitrary"), comm=comm)(proj, lbj, gn)


def _hgrn_bwd(proj, o_all, dybr, states, lbj, gn, nb, t_seq, comm=None):
    _, m, di = proj.shape
    tr, hc, hpg = _hgrn_dims(t_seq, di)
    nt, nhg, ncl = t_seq // tr, di // hc, tr // CHUNK

    def body(p_ref, o_ref, dy_ref, st_ref, lb_ref, gn_ref,
             dp_ref, dlb_ref, dgn_ref, dst_scr, a_scr, k_scr, da_scr, dk_scr):
        q_ref, f_ref, i_ref, g_ref = (p_ref.at[s] for s in range(4))
        hg, b, t = pl.program_id(0), pl.program_id(1), pl.program_id(2)

        @pl.when(t == 0)
        def _():
            dst_scr[...] = jnp.zeros_like(dst_scr)

        @pl.when((b == 0) & (t == 0))
        def _():
            dlb_ref[...] = jnp.zeros_like(dlb_ref)

        @pl.when((hg == 0) & (b == 0) & (t == 0))
        def _():
            dgn_ref[...] = jnp.zeros_like(dgn_ref)

        lb = lb_ref[0:1, :]
        sig, fg = _hgrn_gates(f_ref, lb, a_scr, k_scr, tr)
        gnv = gn_ref[...]
        rr = lax.broadcasted_iota(jnp.int32, (CHUNK, CHUNK), 0)
        cc = lax.broadcasted_iota(jnp.int32, (CHUNK, CHUNK), 1)
        causal = cc <= rr
        rowi = lax.broadcasted_iota(jnp.int32, (CHUNK, HEAD_DIM), 0)

        def chunk(it, carry):
            n = ncl - 1 - it
            rows = pl.ds(pl.multiple_of(n * CHUNK, CHUNK), CHUNK)
            lanes = [slice(hd * HEAD_DIM, (hd + 1) * HEAD_DIM) for hd in range(hpg)]
            hs = []
            for hd, ls in enumerate(lanes):
                h = {}
                ah, kh = a_scr[rows, ls], k_scr[rows, ls]
                qp = q_ref[rows, ls]
                sq = _sigmoid(qp)
                qh = qp * sq
                h["dsilu_q"] = sq * (1.0 + qp * (1.0 - sq))
                h["vb"] = i_ref[rows, ls].astype(BF16)
                aref, alast = ah[CHUNK // 2 - 1:CHUNK // 2, :], ah[CHUNK - 1:CHUNK, :]
                h["e1"], h["e2"] = jnp.exp(ah - aref), jnp.exp(aref - ah)
                h["e3"], h["e4"] = jnp.exp(ah), jnp.exp(alast - ah)
                h["dec"] = jnp.exp(alast)
                h["q_in"], h["k_in"], h["q_out"], h["k_out"] = qh * h["e1"], kh * h["e2"], qh * h["e3"], kh * h["e4"]
                for nm in ("q_in", "k_in", "q_out", "k_out"):
                    h[nm + "_b"] = h[nm].astype(BF16)
                o = o_ref[rows, ls]
                rstd = lax.rsqrt(jnp.mean(o * o, axis=-1, keepdims=True) + EPS)
                ohat = o * rstd
                gg = g_ref[rows, ls]
                sg = _sigmoid(gg)
                dyv = dy_ref[rows, ls]
                d_on = dyv * (gg * sg)
                dp_ref[3, rows, ls] = (dyv * (ohat * gnv) * (sg * (1.0 + gg * (1.0 - sg)))).astype(BF16)
                h["dgn"] = jnp.sum(d_on * ohat, axis=0, keepdims=True)
                dohat = d_on * gnv
                do = rstd * (dohat - ohat * jnp.mean(dohat * ohat, axis=-1, keepdims=True))
                h["do_b"] = do.astype(BF16)
                h["st_prev"] = st_ref[n, hd]
                h["dst"] = dst_scr[hd]
                hs.append(h)
            for h in hs:
                dst_b = h["dst"].astype(BF16)
                h["scores"] = _dot_nt(h["q_in_b"], h["k_in_b"])
                h["dscores"] = _dot_nt(h["do_b"], h["vb"])
                h["dv_inter"] = _dot_nt(h["k_out_b"], dst_b)
                h["dq_out"] = _dot(h["do_b"], h["st_prev"].astype(BF16))
                h["dk_out"] = _dot(h["vb"], dst_b)
                h["dst_mm"] = _dot_tn(h["do_b"], h["q_out_b"])
            for h in hs:
                scores = jnp.where(causal, h["scores"], 0.0).astype(BF16)
                dscores = jnp.where(causal, h["dscores"], 0.0).astype(BF16)
                h["dv"] = _dot_tn(scores, h["do_b"]) + h["dv_inter"]
                h["dq_in"] = _dot(dscores, h["k_in_b"])
                h["dk_in"] = _dot_tn(dscores, h["q_in_b"])
            dgn = hs[0]["dgn"]
            for h in hs[1:]:
                dgn = dgn + h["dgn"]
            dgn_ref[...] += dgn
            for hd, (h, ls) in enumerate(zip(hs, lanes)):
                ddec = jnp.sum(h["dst"] * h["st_prev"], axis=0, keepdims=True)
                dst_scr[hd] = h["dst"] * h["dec"] + h["dst_mm"]
                dp_ref[2, rows, ls] = h["dv"].astype(BF16)
                dq = h["dq_in"] * h["e1"] + h["dq_out"] * h["e3"]
                dp_ref[0, rows, ls] = (dq * h["dsilu_q"]).astype(BF16)
                dk_scr[rows, ls] = h["dk_in"] * h["e2"] + h["dk_out"] * h["e4"]
                t_in = h["dq_in"] * h["q_in"] - h["dk_in"] * h["k_in"]
                t_out = h["dk_out"] * h["k_out"]
                da = t_in + h["dq_out"] * h["q_out"] - t_out
                da_ref_row = -jnp.sum(t_in, axis=0, keepdims=True)
                da_last_row = jnp.sum(t_out, axis=0, keepdims=True) + ddec * h["dec"]
                da = da + jnp.where(rowi == CHUNK // 2 - 1, da_ref_row, 0.0) \
                        + jnp.where(rowi == CHUNK - 1, da_last_row, 0.0)
                da_scr[rows, ls] = da
            return carry

        lax.fori_loop(0, ncl, chunk, 0)
        g = min(CUM_ROWS, tr)
        tri = _tri_mask(g, reverse=True)
        for rg in range(tr // g):
            rs = slice(rg * g, (rg + 1) * g)
            dlogf = _tri_apply(tri, da_scr[rs, :])
            df = dlogf / fg[rs, :] - dk_scr[rs, :]
            sgr = sig[rs, :]
            dp_ref[1, rs, :] = (df * (1.0 - lb) * (sgr * (1.0 - sgr))).astype(BF16)
            dlb_ref[...] += jnp.sum(df * (1.0 - sgr), axis=0, keepdims=True) * lb_ref[1:2, :]

    blk = pl.BlockSpec((tr, hc), lambda hg, b, t: (b * nt + (nt - 1 - t), hg))
    return _pc(body, name="hgrn_bwd",
               out_shape=[jax.ShapeDtypeStruct((4, m, di), BF16), jax.ShapeDtypeStruct((1, di), F32),
                          jax.ShapeDtypeStruct((1, HEAD_DIM), F32)],
               grid=(nhg, nb, nt),
               in_specs=[pl.BlockSpec((4, tr, hc), lambda hg, b, t: (0, b * nt + (nt - 1 - t), hg)), blk, blk,
                         pl.BlockSpec((ncl, hpg, HEAD_DIM, HEAD_DIM),
                                      lambda hg, b, t: (b * nt + (nt - 1 - t), hg, 0, 0)),
                         pl.BlockSpec((2, hc), lambda hg, b, t: (0, hg)),
                         pl.BlockSpec((1, HEAD_DIM), lambda hg, b, t: (0, 0))],
               out_specs=[pl.BlockSpec((4, tr, hc), lambda hg, b, t: (0, b * nt + (nt - 1 - t), hg)),
                          pl.BlockSpec((1, hc), lambda hg, b, t: (0, hg)),
                          pl.BlockSpec((1, HEAD_DIM), lambda hg, b, t: (0, 0))],
               scratch=[pltpu.VMEM((hpg, HEAD_DIM, HEAD_DIM), F32)] + [pltpu.VMEM((tr, hc), F32)] * 4,
               sem=("arbitrary", "arbitrary", "arbitrary"), comm=comm)(
                   proj, o_all, dybr, states, lbj, gn)


def _adamw(parts, w, m, v, name, comm=None):
    r, c = w.shape
    tr = _tile(r, 256)
    npart = len(parts)
    c1 = 1.0 - ADAM_B1 ** ADAM_STEP
    c2 = 1.0 - ADAM_B2 ** ADAM_STEP

    def body(*refs):
        p_refs = refs[:npart]
        _adamw_math(p_refs, *refs[npart:], c1, c2)

    blk = pl.BlockSpec((tr, c), lambda i: (i, 0))
    return _pc(body, name=name, out_shape=[jax.ShapeDtypeStruct((r, c), F32)] * 4, grid=(r // tr,),
               in_specs=[blk] * (npart + 3), out_specs=[blk] * 4, sem=("parallel",), comm=comm)(*parts, w, m, v)


def _adamw_math(p_refs, w_ref, m_ref, v_ref, g_ref, d_ref, nm_ref, nv_ref, c1, c2):
    g = p_refs[0][...].astype(F32)
    for p in p_refs[1:]:
        g = g + p[...].astype(F32)
    nm = ADAM_B1 * m_ref[...] + (1.0 - ADAM_B1) * g
    nv = ADAM_B2 * v_ref[...] + (1.0 - ADAM_B2) * (g * g)
    g_ref[...] = g
    nm_ref[...] = nm
    nv_ref[...] = nv
    d_ref[...] = -ADAM_LR * ((nm / c1) / (jnp.sqrt(nv / c2) + ADAM_EPS) + ADAM_WD * w_ref[...])


def _adamw_blocks(parts, idx, w, m, v, name):
    r, c = w.shape
    tr = _tile(r, 256)
    npart = len(parts)
    c1 = 1.0 - ADAM_B1 ** ADAM_STEP
    c2 = 1.0 - ADAM_B2 ** ADAM_STEP

    def body(idx_ref, *refs):
        _adamw_math(refs[:npart], *refs[npart:], c1, c2)

    def sel(p):
        return pl.BlockSpec((None, tr, c), lambda i, s: (s[p], i, 0))

    blk = pl.BlockSpec((tr, c), lambda i, s: (i, 0))
    gs = pltpu.PrefetchScalarGridSpec(num_scalar_prefetch=1, grid=(r // tr,),
                                      in_specs=[sel(p) for p in range(npart)] + [blk] * 3, out_specs=[blk] * 4)
    return _pc(body, name=name, out_shape=[jax.ShapeDtypeStruct((r, c), F32)] * 4, grid_spec=gs,
               sem=("parallel",))(idx, *parts, w, m, v)


_EARLY = ["a_ln_gain", "a_ln_bias", "a_w_s", "a_b_s", "b_lower_bounds", "b_gn_gain"]


def _pack(arrs):
    flat = jnp.concatenate([a.reshape(-1) for a in arrs])
    rows = -(-flat.shape[0] // 1024) * 8
    return jnp.pad(flat, (0, rows * 128 - flat.shape[0])).reshape(rows, 128)


def _unpack(buf, like):
    flat = buf.reshape(-1)
    out, off = [], 0
    for a in like:
        out.append(flat[off:off + a.size].reshape(a.shape))
        off += a.size
    return out


def kernel(x, c, norm_gain, w_ada, b_ada, a_w_in, a_ln_gain, a_ln_bias, a_w_s, a_b_s, a_w_out, b_w_in, b_lower_bounds, b_gn_gain, b_w_out, final_gain, loss_target, m_norm_gain, m_w_ada, m_b_ada, m_a_w_in, m_a_ln_gain, m_a_ln_bias, m_a_w_s, m_a_b_s, m_a_w_out, m_b_w_in, m_b_lower_bounds, m_b_gn_gain, m_b_w_out, m_final_gain, v_norm_gain, v_w_ada, v_b_ada, v_a_w_in, v_a_ln_gain, v_a_ln_bias, v_a_w_s, v_a_b_s, v_a_w_out, v_b_w_in, v_b_lower_bounds, v_b_gn_gain, v_b_w_out, v_final_gain):
    w = dict(norm_gain=norm_gain, w_ada=w_ada, b_ada=b_ada, a_w_in=a_w_in, a_ln_gain=a_ln_gain,
             a_ln_bias=a_ln_bias, a_w_s=a_w_s, a_b_s=a_b_s, a_w_out=a_w_out, b_w_in=b_w_in,
             b_lower_bounds=b_lower_bounds, b_gn_gain=b_gn_gain, b_w_out=b_w_out, final_gain=final_gain)
    mo = dict(norm_gain=m_norm_gain, w_ada=m_w_ada, b_ada=m_b_ada, a_w_in=m_a_w_in, a_ln_gain=m_a_ln_gain,
              a_ln_bias=m_a_ln_bias, a_w_s=m_a_w_s, a_b_s=m_a_b_s, a_w_out=m_a_w_out, b_w_in=m_b_w_in,
              b_lower_bounds=m_b_lower_bounds, b_gn_gain=m_b_gn_gain, b_w_out=m_b_w_out, final_gain=m_final_gain)
    vo = dict(norm_gain=v_norm_gain, w_ada=v_w_ada, b_ada=v_b_ada, a_w_in=v_a_w_in, a_ln_gain=v_a_ln_gain,
              a_ln_bias=v_a_ln_bias, a_w_s=v_a_w_s, a_b_s=v_a_b_s, a_w_out=v_a_w_out, b_w_in=v_b_w_in,
              b_lower_bounds=v_b_lower_bounds, b_gn_gain=v_b_gn_gain, b_w_out=v_b_w_out, final_gain=v_final_gain)

    nb, t_seq, d = x.shape
    m = nb * t_seq
    ncol_ada = w_ada.shape[2]
    xi, yi, ci = lax.axis_index("x"), lax.axis_index("y"), lax.axis_index("c")
    me = 4 * xi + 2 * yi + ci

    c_g, wa_in_g = _all_gather([c, a_w_in[0].astype(BF16)], "gather_c_wa")

    c_all = c_g.reshape(NDEV * nb, d)
    b_cols = lax.dynamic_slice(b_ada, (0, me * ncol_ada), (2, ncol_ada)).reshape(2, 1, ncol_ada)
    mod_part, lbj = _ada_fwd(c_all, w_ada, b_cols, b_lower_bounds)
    mod_all = _all_gather([mod_part], "gather_mod")[0]
    mod_mine = lax.dynamic_slice_in_dim(mod_all, me * nb, nb, axis=2)
    mod_mine = mod_mine.transpose(1, 2, 0, 3).reshape(2, nb, 3, d)
    mod0, mod1 = mod_mine[0], mod_mine[1]

    di = a_w_out.shape[1] * NDEV

    xf = x.reshape(m, d)
    tgt = loss_target.reshape(m, d)
    ng0, ng1 = norm_gain[0:1], norm_gain[1:2]
    ncb = b_w_in.shape[2]
    wb_lo, wb_hi = b_w_in[0][:, :ncb // 2].astype(BF16), b_w_in[0][:, ncb // 2:].astype(BF16)
    h0, h0_t = _prenorm(xf, ng0, mod0, t_seq, "prenorm_a")
    proj_a, half = _mm_in(h0, [wa_in_g], 1, "in_proj_a", comm=_gather_first([a_w_out[0].astype(BF16), wb_lo]))
    bs_t = jnp.pad(a_b_s[0].T, ((0, 0), (0, 128 - SG_GROUPS)))
    ybr_a, (wa_out_g, wb_lo_g, wb_hi_half) = _a_mid_fwd(
        proj_a, a_ln_gain, a_ln_bias, a_w_s[0], bs_t, t_seq, comm=_join(_gather_second(half), _gather_first([wb_hi])))
    wa_out = wa_out_g.reshape(di, d)
    (yout_a, x1), (wb_hi_g, wb_out_half) = _out_proj(
        ybr_a, wa_out, xf, mod0, t_seq, "out_proj_a",
        comm=_join(_gather_second([wb_hi_half]), _gather_first([b_w_out[0].astype(BF16)])))
    wb_in_g = [wb_lo_g, wb_hi_g]
    h1, h1_t = _prenorm(x1, ng1, mod1, t_seq, "prenorm_b")
    proj_b, (wb_out_g,) = _mm_in(h1, wb_in_g, 4, "in_proj_b", comm=_gather_second([wb_out_half]))
    wb_out = wb_out_g.reshape(di, d)
    o_b, ybr_b, states = _hgrn_fwd(proj_b, lbj, b_gn_gain, nb, t_seq)
    yout_b, dx2, loss_part, d_final_gain = _out_proj_loss(ybr_b, wb_out, x1, mod1, final_gain.reshape(1, d), tgt, t_seq)

    rows_out = a_w_out.shape[1]
    dy_b, dgate1, dybr_b = _gate_dybr(dx2, yout_b, mod1, wb_out, t_seq, "dybr_b")
    rs_wb_out = _ReduceScatter(_mm_dw_out(ybr_b, dy_b, "dw_out_b").reshape(NDEV, rows_out, d), "b_w_out")
    (dproj_b, d_lb, d_gn), got = _hgrn_bwd(proj_b, o_b, dybr_b, states, lbj, b_gn_gain, nb, t_seq,
                                           comm=rs_wb_out.swap_core())
    rs_wb_out.after_core(got[0])
    dh1, got = _mm_din(dproj_b, wb_in_g, 4, "dh_b", comm=rs_wb_out.swap_chips())
    rs_wb_out.after_chips(got[0])
    dx1, dss1, dgain1 = _prenorm_bwd(dh1, x1, ng1, mod1, dx2, t_seq, "prenorm_bwd_b")
    rs_wb_in = _ReduceScatter(_mm_dw_in(h1_t, dproj_b, ncb, 4, "dw_in_b"), "b_w_in")

    dy_a, dgate0, dybr_a = _gate_dybr(dx1, yout_a, mod0, wa_out, t_seq, "dybr_a")
    g_wa_out, got = _mm_dw_out(ybr_a, dy_a, "dw_out_a", comm=rs_wb_in.swap_core())
    rs_wb_in.after_core(got[0])
    rs_wa_out = _ReduceScatter(g_wa_out.reshape(NDEV, rows_out, d), "a_w_out")
    (dproj_a, d_lng, d_lnb, d_ws, d_bs_t), got = _a_mid_bwd(
        proj_a, dybr_a, a_ln_gain, a_ln_bias, a_w_s[0], bs_t, t_seq,
        comm=_join(rs_wb_in.swap_chips(), rs_wa_out.swap_core()))
    rs_wb_in.after_chips(got[0])
    rs_wa_out.after_core(got[1])
    part = dict(a_ln_gain=d_lng, a_ln_bias=d_lnb, a_w_s=d_ws[None], a_b_s=d_bs_t[:, :SG_GROUPS].T[None],
                b_lower_bounds=jnp.concatenate([-d_lb, d_lb], axis=0), b_gn_gain=d_gn)
    early_pack = _pack([part[k].reshape(w[k].shape) for k in _EARLY])
    g_wa_in, got = _mm_dw_in(h0_t, dproj_a, wa_in_g.shape[2], 1, "dw_in_a",
                             comm=_join(rs_wa_out.swap_chips(), _gather_first([early_pack])))
    rs_wa_out.after_chips(got[0])
    rs_wa_in = _ReduceScatter(g_wa_in, "a_w_in")
    n_tiles = m // _din_tile(m)
    assert n_tiles >= 2
    first_tiles = max(1, (3 * n_tiles) // 8)
    dh0, got2 = _mm_din(dproj_a, [wa_in_g], 1, "dh_a_first", tiles=(0, first_tiles),
                        comm=_join(rs_wa_in.swap_core(), _gather_second([got[1]])))
    rs_wa_in.after_core(got2[0])
    early_all = got2[1]
    dh0, got = _mm_din(dproj_a, [wa_in_g], 1, "dh_a_rest", comm=rs_wa_in.swap_chips(),
                       tiles=(first_tiles, n_tiles - first_tiles), prev=dh0)
    rs_wa_in.after_chips(got[0])
    dx0, dss0, dgain0 = _prenorm_bwd(dh0, xf, ng0, mod0, dx1, t_seq, "prenorm_bwd_a")
    grad_x = dx0.reshape(nb, t_seq, d)

    dmod = jnp.stack([jnp.concatenate([dss0, dgate0], axis=1), jnp.concatenate([dss1, dgate1], axis=1)])
    late_like = [norm_gain, final_gain, loss_part.reshape(1)]
    late_pack = _pack([jnp.concatenate([dgain0, dgain1], axis=0), d_final_gain[0], loss_part.reshape(1)])
    dmod_all, late_all = _all_gather([dmod.reshape(2, nb, 3 * d), late_pack], "gather_tail")
    dmod_all = dmod_all.transpose(1, 0, 2, 3).reshape(2, NDEV * nb, 3 * d)
    dmod_cols = lax.dynamic_slice_in_dim(dmod_all, me * ncol_ada, ncol_ada, axis=2)
    g_w_ada, g_b_ada = _ada_bwd(c_all, dmod_cols, dmod_all)

    res = {}
    early_like = [w[k] for k in _EARLY]
    dev_order = jnp.arange(NDEV, dtype=jnp.int32)
    sm = _adamw_blocks([early_all] * NDEV, dev_order, _pack(early_like), _pack([mo[k] for k in _EARLY]),
                       _pack([vo[k] for k in _EARLY]), "adamw_small_early")
    sm = [dict(zip(_EARLY, _unpack(buf, early_like))) for buf in sm]
    for k in _EARLY:
        res[k] = tuple(s[k] for s in sm)
    zero = jnp.zeros((1,), F32)
    sm = _adamw_blocks([late_all] * NDEV, dev_order, _pack([norm_gain, final_gain, zero]),
                       _pack([mo["norm_gain"], mo["final_gain"], zero]),
                       _pack([vo["norm_gain"], vo["final_gain"], zero]), "adamw_small_late")
    sm = [_unpack(buf, late_like) for buf in sm]
    res["norm_gain"] = tuple(s[0] for s in sm)
    res["final_gain"] = tuple(s[1] for s in sm)
    loss = sm[0][2][0]
    rb = _adamw([g_b_ada], b_ada, mo["b_ada"], vo["b_ada"], "adamw_b_ada")
    res["b_ada"] = tuple(rb)
    sh = w_ada.shape
    ra = _adamw([g_w_ada.reshape(sh[0] * sh[1], sh[2])], w_ada.reshape(sh[0] * sh[1], sh[2]),
                mo["w_ada"].reshape(sh[0] * sh[1], sh[2]), vo["w_ada"].reshape(sh[0] * sh[1], sh[2]), "adamw_w_ada")
    res["w_ada"] = tuple(z.reshape(sh) for z in ra)

    for k, rs in (("b_w_out", rs_wb_out), ("b_w_in", rs_wb_in), ("a_w_out", rs_wa_out), ("a_w_in", rs_wa_in)):
        res[k] = tuple(z[None] for z in _adamw_blocks(rs.parts, rs.idx, w[k][0], mo[k][0], vo[k][0], "adamw_" + k))

    order = ["norm_gain", "w_ada", "b_ada", "a_w_in", "a_ln_gain", "a_ln_bias", "a_w_s", "a_b_s", "a_w_out",
             "b_w_in", "b_lower_bounds", "b_gn_gain", "b_w_out", "final_gain"]
    return (loss, grad_x, *[res[k][0] for k in order], *[res[k][1] for k in order],
            *[res[k][2] for k in order], *[res[k][3] for k in order])
```

```python
import functools
import math

import jax
import jax.numpy as jnp
from jax import lax
from jax.experimental import pallas as pl
from jax.experimental.pallas import tpu as pltpu

F32 = jnp.float32
BF16 = jnp.bfloat16
MESH = pl.DeviceIdType.MESH
NDEV = 8
EPS = 1e-6
CHUNK = 64
SG_BLOCK = 128
SG_GROUPS = 8
HEAD_DIM = 128
CUM_ROWS = 256
ADAM_LR, ADAM_B1, ADAM_B2, ADAM_EPS, ADAM_WD, ADAM_STEP = 0.001, 0.9, 0.999, 1e-08, 0.01, 10
VMEM_LIMIT = 56 * 1024 * 1024
ANY = pl.BlockSpec(memory_space=pl.ANY)


class _Hosted:
    def __init__(self, arrays, out_shapes, nsem, start, finish, aliases=None):
        self.arrays, self.out_shapes, self.nsem = list(arrays), list(out_shapes), nsem
        self.start, self.finish = start, finish
        self.aliases = dict(aliases or {})


def _join(*comms):
    arrays, outs, aliases, offs, nsem = [], [], {}, [], 0
    for cm in comms:
        offs.append((len(arrays), len(outs), nsem))
        for i, o in cm.aliases.items():
            aliases[len(arrays) + i] = len(outs) + o
        arrays += cm.arrays
        outs += cm.out_shapes
        nsem += cm.nsem

    def run(which):
        def f(ins, outs_, ss, rs, base):
            for cm, (ia, io, isem) in zip(comms, offs):
                getattr(cm, which)(ins[ia:ia + len(cm.arrays)], outs_[io:io + len(cm.out_shapes)], ss, rs, base + isem)
        return f

    return _Hosted(arrays, outs, nsem, run("start"), run("finish"), aliases)


def _pc(body, *, name, out_shape, grid=None, in_specs=None, out_specs=None, scratch=(), sem=None,
        grid_spec=None, comm=None, aliases=None):
    cp = dict(vmem_limit_bytes=VMEM_LIMIT)
    aliases = dict(aliases or {})
    if comm is None:
        if sem is not None:
            cp["dimension_semantics"] = sem
        kw = {"input_output_aliases": aliases}
        if grid_spec is not None:
            kw["grid_spec"] = grid_spec
        else:
            if grid is not None:
                kw["grid"] = grid
            if in_specs is not None:
                kw["in_specs"] = in_specs
            if out_specs is not None:
                kw["out_specs"] = out_specs
            kw["scratch_shapes"] = list(scratch)
        return pl.pallas_call(functools.partial(body), name=name, out_shape=out_shape,
                              compiler_params=pltpu.CompilerParams(**cp), **kw)

    single = not isinstance(out_shape, (list, tuple))
    outs_list = [out_shape] if single else list(out_shape)
    ospecs = [out_specs] if single else list(out_specs)
    n_in, n_out, n_ci, n_co, n_scr = len(in_specs), len(outs_list), len(comm.arrays), len(comm.out_shapes), len(scratch)
    cp["dimension_semantics"] = ("arbitrary",) * len(grid)

    def hosted(*refs):
        cin, hin = refs[:n_in], refs[n_in:n_in + n_ci]
        cout = refs[n_in + n_ci:n_in + n_ci + n_out]
        hout = refs[n_in + n_ci + n_out:n_in + n_ci + n_out + n_co]
        scr = refs[n_in + n_ci + n_out + n_co:n_in + n_ci + n_out + n_co + n_scr]
        ssem, rsem = refs[-2], refs[-1]
        first = functools.reduce(lambda p, q: p & q, [pl.program_id(a) == 0 for a in range(len(grid))])
        last = functools.reduce(lambda p, q: p & q, [pl.program_id(a) == grid[a] - 1 for a in range(len(grid))])

        @pl.when(first)
        def _():
            comm.start(hin, hout, ssem, rsem, 0)

        body(*cin, *cout, *scr)

        @pl.when(last)
        def _():
            comm.finish(hin, hout, ssem, rsem, 0)

    call = pl.pallas_call(
        hosted, name=name, grid=grid, in_specs=list(in_specs) + [ANY] * n_ci, out_specs=ospecs + [ANY] * n_co,
        out_shape=outs_list + comm.out_shapes,
        scratch_shapes=list(scratch) + [pltpu.SemaphoreType.DMA((comm.nsem,)), pltpu.SemaphoreType.DMA((comm.nsem,))],
        input_output_aliases={**aliases, **{n_in + i: n_out + o for i, o in comm.aliases.items()}},
        compiler_params=pltpu.CompilerParams(**cp))

    def run(*args):
        res = call(*args, *comm.arrays)
        comp = res[:n_out]
        return (comp[0] if single else comp), list(res[n_out:])

    return run


def _tile(n, pref):
    return pref if n % pref == 0 else n


def _sigmoid(x):
    return 1.0 / (1.0 + jnp.exp(-x))


def _gelu(x):
    c = math.sqrt(2.0 / math.pi)
    return 0.5 * x * (1.0 + jnp.tanh(c * (x + 0.044715 * (x * x * x))))


def _gelu_and_grad(x):
    c = math.sqrt(2.0 / math.pi)
    x2 = x * x
    t = jnp.tanh(c * (x + 0.044715 * (x2 * x)))
    half = 0.5 * (1.0 + t)
    return x * half, half + (0.5 * x) * (1.0 - t * t) * (c + (3.0 * 0.044715 * c) * x2)


def _dot(a, b):
    return jnp.dot(a, b, preferred_element_type=F32)


def _dot_nt(a, b):
    return lax.dot_general(a, b, (((1,), (1,)), ((), ())), preferred_element_type=F32)


def _dot_tn(a, b):
    return lax.dot_general(a, b, (((0,), (0,)), ((), ())), preferred_element_type=F32)


def _tri_mask(n, reverse):
    r = lax.broadcasted_iota(jnp.int32, (n, n), 0)
    c = lax.broadcasted_iota(jnp.int32, (n, n), 1)
    same = (r // CHUNK) == (c // CHUNK)
    tri = (c >= r) if reverse else (c <= r)
    return jnp.where(same & tri, 1.0, 0.0).astype(BF16)


def _tri_apply(tri, x):
    hi = x.astype(BF16)
    r1 = x - hi.astype(F32)
    mid = r1.astype(BF16)
    lo = (r1 - mid.astype(F32)).astype(BF16)
    return _dot(tri, hi) + (_dot(tri, mid) + _dot(tri, lo))


def _all_gather(arrs, name):
    n = len(arrs)

    def body(*refs):
        ins, outs = refs[:n], refs[n:2 * n]
        send_sems, recv_sems, local_sems = refs[2 * n:]
        x, y, c = lax.axis_index("x"), lax.axis_index("y"), lax.axis_index("c")
        me, sibling = (x, y, c), (x, y, 1 - c)
        near = (x + c - 2 * x * c, y + (1 - c) - 2 * y * (1 - c))
        far = (x + (1 - c) - 2 * x * (1 - c), y + c - 2 * y * c)
        diag = (1 - x, 1 - y)

        def blk(a, p):
            return outs[a].at[4 * p[0] + 2 * p[1] + p[2]]

        def copy(a, k, block, to, src=None):
            return pltpu.make_async_remote_copy(
                src_ref=blk(a, block) if src is None else src, dst_ref=blk(a, block),
                send_sem=send_sems.at[7 * a + k], recv_sem=recv_sems.at[7 * a + k],
                device_id=to, device_id_type=MESH)

        mine = [pltpu.make_async_copy(ins[a], blk(a, me), local_sems.at[a]) for a in range(n)]
        for m in mine:
            m.start()
        sends = []
        for a in range(n):
            sends += [copy(a, 0, me, sibling, src=ins[a]), copy(a, 1, me, (*near, c), src=ins[a]),
                      copy(a, 2, me, (*far, c), src=ins[a])]
        for cp in sends:
            cp.start()
        for a in range(n):
            copy(a, 1, (*near, c), me).wait_recv()
            sends.append(copy(a, 3, (*near, c), (*far, c)))
            sends[-1].start()
        for a in range(n):
            sends.append(copy(a, 4, (*near, c), sibling))
            sends[-1].start()
            copy(a, 2, (*far, c), me).wait_recv()
            sends.append(copy(a, 5, (*far, c), sibling))
            sends[-1].start()
        for a in range(n):
            copy(a, 3, (*diag, c), me).wait_recv()
            sends.append(copy(a, 6, (*diag, c), sibling))
            sends[-1].start()
        for a in range(n):
            copy(a, 0, sibling, me).wait_recv()
            copy(a, 4, (*far, 1 - c), me).wait_recv()
            copy(a, 5, (*near, 1 - c), me).wait_recv()
            copy(a, 6, (*diag, 1 - c), me).wait_recv()
        for cp in sends:
            cp.wait_send()
        for m in mine:
            m.wait()

    out_shape = [jax.ShapeDtypeStruct((NDEV,) + a.shape, a.dtype) for a in arrs]
    return _pc(body, name=name, out_shape=out_shape, in_specs=[ANY] * n, out_specs=[ANY] * n,
               scratch=[pltpu.SemaphoreType.DMA((7 * n,)), pltpu.SemaphoreType.DMA((7 * n,)),
                        pltpu.SemaphoreType.DMA((n,))])(*arrs)


def _gather_first(arrs):
    n = len(arrs)

    def parts(ins, outs, ss, rs, base):
        x, y, c = lax.axis_index("x"), lax.axis_index("y"), lax.axis_index("c")
        me, sibling = (x, y, c), (x, y, 1 - c)
        chips = [(1 - x, y), (x, 1 - y), (1 - x, 1 - y)]

        def blk(a, p):
            return outs[a].at[4 * p[0] + 2 * p[1] + p[2]]

        def copy(a, k, block, to):
            return pltpu.make_async_remote_copy(
                src_ref=ins[a], dst_ref=blk(a, block), send_sem=ss.at[base + 4 * a + k],
                recv_sem=rs.at[base + 4 * a + k], device_id=to, device_id_type=MESH)

        local = [pltpu.make_async_copy(ins[a], blk(a, me), ss.at[base + 4 * n + a]) for a in range(n)]
        sends, recvs = [], []
        for a in range(n):
            sends.append(copy(a, 0, me, sibling))
            recvs.append(copy(a, 0, sibling, me))
            for j, chip in enumerate(chips):
                sends.append(copy(a, 1 + j, me, (*chip, c)))
                recvs.append(copy(a, 1 + j, (*chip, c), me))
        return local, sends, recvs

    def start(ins, outs, ss, rs, base):
        local, sends, _ = parts(ins, outs, ss, rs, base)
        for cp in local + sends:
            cp.start()

    def finish(ins, outs, ss, rs, base):
        local, sends, recvs = parts(ins, outs, ss, rs, base)
        for cp in recvs:
            cp.wait_recv()
        for cp in sends:
            cp.wait_send()
        for cp in local:
            cp.wait()

    return _Hosted(arrs, [jax.ShapeDtypeStruct((NDEV,) + a.shape, a.dtype) for a in arrs], 5 * n, start, finish)


def _gather_second(bufs):
    n = len(bufs)

    def parts(ins, outs, ss, rs, base):
        x, y, c = lax.axis_index("x"), lax.axis_index("y"), lax.axis_index("c")
        sibling = (x, y, 1 - c)
        chips = [(1 - x, y), (x, 1 - y), (1 - x, 1 - y)]
        sends, recvs = [], []
        for a in range(n):
            for j, chip in enumerate(chips):
                mine = 4 * chip[0] + 2 * chip[1] + c
                theirs = 4 * chip[0] + 2 * chip[1] + (1 - c)
                sends.append(pltpu.make_async_remote_copy(
                    src_ref=ins[a].at[mine], dst_ref=outs[a].at[mine], send_sem=ss.at[base + 3 * a + j],
                    recv_sem=rs.at[base + 3 * a + j], device_id=sibling, device_id_type=MESH))
                recvs.append(pltpu.make_async_remote_copy(
                    src_ref=ins[a].at[theirs], dst_ref=outs[a].at[theirs], send_sem=ss.at[base + 3 * a + j],
                    recv_sem=rs.at[base + 3 * a + j], device_id=sibling, device_id_type=MESH))
        return sends, recvs

    def start(ins, outs, ss, rs, base):
        for cp in parts(ins, outs, ss, rs, base)[0]:
            cp.start()

    def finish(ins, outs, ss, rs, base):
        sends, recvs = parts(ins, outs, ss, rs, base)
        for cp in recvs:
            cp.wait_recv()
        for cp in sends:
            cp.wait_send()

    return _Hosted(bufs, [jax.ShapeDtypeStruct(b.shape, b.dtype) for b in bufs], 3 * n, start, finish,
                   aliases={a: a for a in range(n)})


def _swap(src, nblk, ids_fn, partner_fn):
    def copies(ins, outs, ss, rs, base):
        x, y, c = lax.axis_index("x"), lax.axis_index("y"), lax.axis_index("c")
        ids = ids_fn(x, y, c)
        partner = partner_fn(x, y, c)
        return [pltpu.make_async_remote_copy(
            src_ref=ins[0].at[ids[k]], dst_ref=outs[0].at[k], send_sem=ss.at[base + k], recv_sem=rs.at[base + k],
            device_id=partner, device_id_type=MESH) for k in range(nblk)]

    def start(ins, outs, ss, rs, base):
        for cp in copies(ins, outs, ss, rs, base):
            cp.start()

    def finish(ins, outs, ss, rs, base):
        for cp in copies(ins, outs, ss, rs, base):
            cp.wait()

    return _Hosted([src], [jax.ShapeDtypeStruct((nblk,) + src.shape[1:], src.dtype)], nblk, start, finish)


def _blocking(comm, name):
    n_i, n_o = len(comm.arrays), len(comm.out_shapes)

    def body(*refs):
        ins, outs = refs[:n_i], refs[n_i:n_i + n_o]
        comm.start(ins, outs, refs[-2], refs[-1], 0)
        comm.finish(ins, outs, refs[-2], refs[-1], 0)

    return pl.pallas_call(
        body, name=name, out_shape=comm.out_shapes, in_specs=[ANY] * n_i, out_specs=[ANY] * n_o,
        scratch_shapes=[pltpu.SemaphoreType.DMA((comm.nsem,)), pltpu.SemaphoreType.DMA((comm.nsem,))],
        input_output_aliases=comm.aliases)(*comm.arrays)


def _swap_chips(send):
    def copies(ins, outs, ss, rs, base):
        x, y, c = lax.axis_index("x"), lax.axis_index("y"), lax.axis_index("c")
        chips = [(1 - x, y), (x, 1 - y), (1 - x, 1 - y)]
        return [pltpu.make_async_remote_copy(
            src_ref=ins[0].at[j], dst_ref=outs[0].at[j], send_sem=ss.at[base + j], recv_sem=rs.at[base + j],
            device_id=(*chip, c), device_id_type=MESH) for j, chip in enumerate(chips)]

    def start(ins, outs, ss, rs, base):
        for cp in copies(ins, outs, ss, rs, base):
            cp.start()

    def finish(ins, outs, ss, rs, base):
        for cp in copies(ins, outs, ss, rs, base):
            cp.wait()

    return _Hosted([send], [jax.ShapeDtypeStruct(send.shape, send.dtype)], 3, start, finish)


def _add_send(a, b, idx, ns, name):
    _, r, c = a.shape
    tr = _tile(r, 256)

    def body(idx_ref, a_ref, b_ref, send_ref):
        send_ref[...] = (a_ref[...] + b_ref[...]).astype(BF16)

    def sel(off):
        return pl.BlockSpec((None, tr, c), lambda k, i, s: (s[off + k], i, 0))

    gs = pltpu.PrefetchScalarGridSpec(num_scalar_prefetch=1, grid=(ns, r // tr), in_specs=[sel(0), sel(ns)],
                                      out_specs=pl.BlockSpec((None, tr, c), lambda k, i, s: (k, i, 0)))
    return _pc(body, name=name, grid_spec=gs, sem=("arbitrary", "arbitrary"),
               out_shape=jax.ShapeDtypeStruct((ns, r, c), BF16))(idx, a, b)


class _ReduceScatter:
    def __init__(self, g, tag):
        self.g, self.tag = g, tag

    def swap_core(self):
        return _swap(self.g, 4, lambda x, y, c: [1 - c, 3 - c, 5 - c, 7 - c], lambda x, y, c: (x, y, 1 - c))

    def after_core(self, recv):
        x, y, c = lax.axis_index("x"), lax.axis_index("y"), lax.axis_index("c")
        chips = [(1 - x, y), (x, 1 - y), (1 - x, 1 - y)]
        idx = jnp.stack([4 * p + 2 * q + c for p, q in chips] + [2 * p + q for p, q in chips]).astype(jnp.int32)
        self.send = _add_send(self.g, recv, idx, 3, "rs_add_" + self.tag)
        self.recv_core = recv
        zero = jnp.zeros((), jnp.int32)
        self.idx = jnp.stack([4 * x + 2 * y + c, 2 * x + y, zero, zero + 1, zero + 2]).astype(jnp.int32)

    def swap_chips(self):
        return _swap_chips(self.send)

    def after_chips(self, recv):
        self.parts = [self.g, self.recv_core, recv, recv, recv]


def _ada_fwd(c_all, w_ada, b_cols, b_lb):
    nl, d, ncol = w_ada.shape
    nseq = c_all.shape[0]
    di = b_lb.shape[1]

    def body(c_ref, w_ref, b_ref, lb_ref, mod_ref, lbj_ref):
        cv = c_ref[...]
        cact = (cv * _sigmoid(cv)).astype(BF16)
        for l in range(nl):
            mod_ref[l] = _dot(cact, w_ref[l].astype(BF16)) + b_ref[l]
        b0, b1 = lb_ref[0:1, :], lb_ref[1:2, :]
        mx = jnp.maximum(b0, b1)
        e0, e1 = jnp.exp(b0 - mx), jnp.exp(b1 - mx)
        s = e0 + e1
        p0, p1 = e0 / s, e1 / s
        lbj_ref[0:1, :] = (p0 + p1) - p0
        lbj_ref[1:2, :] = p0 * p1

    return _pc(body, name="ada_fwd",
               out_shape=[jax.ShapeDtypeStruct((nl, nseq, ncol), F32), jax.ShapeDtypeStruct((2, di), F32)]
               )(c_all, w_ada, b_cols, b_lb)


def _ada_bwd(c_all, dmod_cols, dmod_full):
    nl, nseq, ncol = dmod_cols.shape
    d = c_all.shape[1]
    d3 = dmod_full.shape[2]

    def body(c_ref, dc_ref, df_ref, gw_ref, gb_ref):
        cv = c_ref[...]
        cact = (cv * _sigmoid(cv)).astype(BF16)
        for l in range(nl):
            gw_ref[l] = _dot_tn(cact, dc_ref[l].astype(BF16))
            gb_ref[l:l + 1, :] = jnp.sum(df_ref[l], axis=0, keepdims=True)

    return _pc(body, name="ada_bwd",
               out_shape=[jax.ShapeDtypeStruct((nl, d, ncol), F32), jax.ShapeDtypeStruct((nl, d3), F32)]
               )(c_all, dmod_cols, dmod_full)


def _prenorm(x, gain, mod, t_seq, name, comm=None):
    m, d = x.shape
    tm = _tile(t_seq, 1024)
    per = t_seq // tm

    def body(x_ref, g_ref, mod_ref, h_ref, ht_ref):
        xv = x_ref[...]
        rstd = lax.rsqrt(jnp.mean(xv * xv, axis=-1, keepdims=True) + EPS)
        r = xv * rstd * g_ref[...]
        h = r * (1.0 + mod_ref[0, 1:2, :]) + mod_ref[0, 0:1, :]
        h_ref[...] = h.astype(BF16)
        ht_ref[...] = h.T.astype(BF16)

    return _pc(body, name=name, out_shape=[jax.ShapeDtypeStruct((m, d), BF16), jax.ShapeDtypeStruct((d, m), BF16)],
               grid=(m // tm,),
               in_specs=[pl.BlockSpec((tm, d), lambda i: (i, 0)), pl.BlockSpec((1, d), lambda i: (0, 0)),
                         pl.BlockSpec((1, 3, d), lambda i: (i // per, 0, 0))],
               out_specs=[pl.BlockSpec((tm, d), lambda i: (i, 0)), pl.BlockSpec((d, tm), lambda i: (0, i))],
               sem=("parallel",), comm=comm)(x, gain, mod)


def _prenorm_bwd(dh, x, gain, mod, dxn, t_seq, name, comm=None):
    m, d = x.shape
    nb = m // t_seq
    tm = _tile(t_seq, 1024)
    per = t_seq // tm

    def body(dh_ref, x_ref, g_ref, mod_ref, dxn_ref, dx_ref, dss_ref, dg_ref):
        i = pl.program_id(0)
        xv, dhv, g = x_ref[...], dh_ref[...], g_ref[...]
        rstd = lax.rsqrt(jnp.mean(xv * xv, axis=-1, keepdims=True) + EPS)
        xhat = xv * rstd
        dr = dhv * (1.0 + mod_ref[0, 1:2, :])
        dxhat = dr * g
        dx_ref[...] = dxn_ref[...] + rstd * (dxhat - xhat * jnp.mean(dxhat * xhat, axis=-1, keepdims=True))

        @pl.when(i % per == 0)
        def _():
            dss_ref[...] = jnp.zeros_like(dss_ref)

        @pl.when(i == 0)
        def _():
            dg_ref[...] = jnp.zeros_like(dg_ref)

        dss_ref[0, 0:1, :] += jnp.sum(dhv, axis=0, keepdims=True)
        dss_ref[0, 1:2, :] += jnp.sum(dhv * (xhat * g), axis=0, keepdims=True)
        dg_ref[...] += jnp.sum(dr * xhat, axis=0, keepdims=True)

    row = pl.BlockSpec((tm, d), lambda i: (i, 0))
    return _pc(body, name=name,
               out_shape=[jax.ShapeDtypeStruct((m, d), F32), jax.ShapeDtypeStruct((nb, 2, d), F32),
                          jax.ShapeDtypeStruct((1, d), F32)],
               grid=(m // tm,),
               in_specs=[row, row, pl.BlockSpec((1, d), lambda i: (0, 0)),
                         pl.BlockSpec((1, 3, d), lambda i: (i // per, 0, 0)), row],
               out_specs=[row, pl.BlockSpec((1, 2, d), lambda i: (i // per, 0, 0)),
                          pl.BlockSpec((1, d), lambda i: (0, 0))],
               sem=("arbitrary",), comm=comm)(dh, x, gain, mod, dxn)


def _mm_in(h, ws, sections, name, comm=None):
    m, k = h.shape
    nw = len(ws)
    widths = [w.shape[2] for w in ws]
    offs = [sum(widths[:a]) for a in range(nw)]
    nc = sum(widths)
    per = NDEV // sections if sections > 1 else NDEV
    tm = _din_tile(m)
    assert per % 2 == 0

    def body(*refs):
        hv = refs[0][...]
        o_ref = refs[1 + nw]
        for b in range(2):
            for a in range(nw):
                lo = b * nc + offs[a]
                o_ref[:, lo:lo + widths[a]] = _dot(hv, refs[1 + a][b])

    w_specs = [pl.BlockSpec((2, k, wd), lambda j, i: (j, 0, 0)) for wd in widths]
    if sections > 1:
        out_shape = jax.ShapeDtypeStruct((sections, m, per * nc), F32)
        out_spec = pl.BlockSpec((None, tm, 2 * nc), lambda j, i: ((2 * j) // per, i, ((2 * j) % per) // 2))
    else:
        out_shape = jax.ShapeDtypeStruct((m, NDEV * nc), F32)
        out_spec = pl.BlockSpec((tm, 2 * nc), lambda j, i: (i, j))
    return _pc(body, name=name, out_shape=out_shape, grid=(NDEV // 2, m // tm),
               in_specs=[pl.BlockSpec((tm, k), lambda j, i: (i, 0))] + w_specs,
               out_specs=out_spec, sem=("parallel", "parallel"), comm=comm)(h, *ws)


def _din_tile(m):
    return 1024 if m % 1024 == 0 and m >= 2048 else _tile(m, 512)


def _mm_din(dproj, ws, sections, name, comm=None, tiles=None, prev=None):
    nw, k = len(ws), ws[0].shape[1]
    widths = [w.shape[2] for w in ws]
    offs = [sum(widths[:a]) for a in range(nw)]
    nc = sum(widths)
    m = dproj.shape[-2]
    tm = _din_tile(m)
    t0, nt = tiles if tiles is not None else (0, m // tm)
    per = NDEV // sections if sections > 1 else NDEV
    assert per % 2 == 0

    def body(*refs):
        d_ref, o_ref = refs[0], refs[-1]
        j = pl.program_id(1)
        acc = None
        for b in range(2):
            for a in range(nw):
                lo = b * nc + offs[a]
                term = _dot_nt(d_ref[:, lo:lo + widths[a]], refs[1 + a][b])
                acc = term if acc is None else acc + term

        @pl.when(j == 0)
        def _():
            o_ref[...] = acc

        @pl.when(j > 0)
        def _():
            o_ref[...] += acc

    if sections > 1:
        dspec = pl.BlockSpec((None, tm, 2 * nc), lambda i, j: ((2 * j) // per, i + t0, ((2 * j) % per) // 2))
    else:
        dspec = pl.BlockSpec((tm, 2 * nc), lambda i, j: (i + t0, j))
    in_specs = [dspec] + [pl.BlockSpec((2, k, wd), lambda i, j: (j, 0, 0)) for wd in widths]
    args = [dproj, *ws]
    if prev is not None:
        in_specs.append(ANY)
        args.append(prev)
    return _pc(body, name=name, out_shape=jax.ShapeDtypeStruct((m, k), F32), grid=(nt, NDEV // 2), in_specs=in_specs,
               out_specs=pl.BlockSpec((tm, k), lambda i, j: (i + t0, 0)), sem=("parallel", "arbitrary"),
               comm=comm, aliases={1 + nw: 0} if prev is not None else None)(*args)


def _mm_dw_in(ht, dproj, nc, sections, name, comm=None):
    k, m = ht.shape
    tk = 2048 if m % 2048 == 0 else _din_tile(m)
    per = NDEV // sections if sections > 1 else NDEV

    def body(h_ref, d_ref, o_ref):
        kk = pl.program_id(1)
        acc = _dot(h_ref[...], d_ref[...])

        @pl.when(kk == 0)
        def _():
            o_ref[...] = acc

        @pl.when(kk > 0)
        def _():
            o_ref[...] += acc

    if sections > 1:
        dspec = pl.BlockSpec((None, tk, nc), lambda j, i: (j // per, i, j % per))
    else:
        dspec = pl.BlockSpec((tk, nc), lambda j, i: (i, j))
    return _pc(body, name=name, out_shape=jax.ShapeDtypeStruct((NDEV, k, nc), F32), grid=(NDEV, m // tk),
               in_specs=[pl.BlockSpec((k, tk), lambda j, i: (0, i)), dspec],
               out_specs=pl.BlockSpec((None, k, nc), lambda j, i: (j, 0, 0)),
               sem=("parallel", "arbitrary"), comm=comm)(ht, dproj)


def _out_proj(ybr, w_out, x, mod, t_seq, name, comm=None):
    m, di = ybr.shape
    d = w_out.shape[1]
    tm = _tile(t_seq, 512)
    per = t_seq // tm

    def body(y_ref, w_ref, x_ref, mod_ref, yo_ref, xn_ref):
        yo = _dot(y_ref[...], w_ref[...])
        yo_ref[...] = yo
        xn_ref[...] = x_ref[...] + mod_ref[0, 2:3, :] * yo

    row = pl.BlockSpec((tm, d), lambda i: (i, 0))
    return _pc(body, name=name,
               out_shape=[jax.ShapeDtypeStruct((m, d), F32), jax.ShapeDtypeStruct((m, d), F32)],
               grid=(m // tm,),
               in_specs=[pl.BlockSpec((tm, di), lambda i: (i, 0)), pl.BlockSpec((di, d), lambda i: (0, 0)), row,
                         pl.BlockSpec((1, 3, d), lambda i: (i // per, 0, 0))],
               out_specs=[row, row], sem=("parallel",), comm=comm)(ybr, w_out, x, mod)


def _out_proj_loss(ybr, w_out, x, mod, gain, target, t_seq):
    m, di = ybr.shape
    d = w_out.shape[1]
    tm = _tile(t_seq, 512)
    per = t_seq // tm

    def body(y_ref, w_ref, x_ref, mod_ref, g_ref, t_ref, yo_ref, dx_ref, loss_ref, dg_ref):
        i = pl.program_id(0)
        yo = _dot(y_ref[...], w_ref[...])
        yo_ref[...] = yo
        xv = x_ref[...] + mod_ref[0, 2:3, :] * yo
        g = g_ref[...]
        rstd = lax.rsqrt(jnp.mean(xv * xv, axis=-1, keepdims=True) + EPS)
        xhat = xv * rstd
        err = xhat * g - t_ref[...]
        dy = err * (1.0 / d)
        dxhat = dy * g
        dx_ref[...] = rstd * (dxhat - xhat * jnp.mean(dxhat * xhat, axis=-1, keepdims=True))

        @pl.when(i == 0)
        def _():
            loss_ref[...] = jnp.zeros_like(loss_ref)
            dg_ref[...] = jnp.zeros_like(dg_ref)

        loss_ref[...] += 0.5 * jnp.sum(jnp.mean(err * err, axis=-1, keepdims=True), axis=0, keepdims=True)
        dg_ref[...] += jnp.sum(dy * xhat, axis=0, keepdims=True)

    row = pl.BlockSpec((tm, d), lambda i: (i, 0))
    vec = pl.BlockSpec((1, d), lambda i: (0, 0))
    return _pc(body, name="out_proj_loss",
               out_shape=[jax.ShapeDtypeStruct((m, d), F32), jax.ShapeDtypeStruct((m, d), F32),
                          jax.ShapeDtypeStruct((1, 1), F32), jax.ShapeDtypeStruct((1, d), F32)],
               grid=(m // tm,),
               in_specs=[pl.BlockSpec((tm, di), lambda i: (i, 0)), pl.BlockSpec((di, d), lambda i: (0, 0)), row,
                         pl.BlockSpec((1, 3, d), lambda i: (i // per, 0, 0)), vec, row],
               out_specs=[row, row, pl.BlockSpec((1, 1), lambda i: (0, 0)), vec],
               sem=("arbitrary",))(ybr, w_out, x, mod, gain, target)


def _gate_dybr(dxn, yout, mod, w_out, t_seq, name):
    m, d = dxn.shape
    di = w_out.shape[0]
    nb = m // t_seq
    tm = _tile(t_seq, 512)
    per = t_seq // tm

    def body(dxn_ref, yo_ref, mod_ref, w_ref, dy_ref, dgate_ref, o_ref):
        i = pl.program_id(0)
        dv = dxn_ref[...]
        dy = (mod_ref[0, 2:3, :] * dv).astype(BF16)
        dy_ref[...] = dy
        o_ref[...] = _dot_nt(dy, w_ref[...])

        @pl.when(i % per == 0)
        def _():
            dgate_ref[...] = jnp.zeros_like(dgate_ref)

        dgate_ref[0] += jnp.sum(dv * yo_ref[...], axis=0, keepdims=True)

    row = pl.BlockSpec((tm, d), lambda i: (i, 0))
    return _pc(body, name=name,
               out_shape=[jax.ShapeDtypeStruct((m, d), BF16), jax.ShapeDtypeStruct((nb, 1, d), F32),
                          jax.ShapeDtypeStruct((m, di), F32)],
               grid=(m // tm,),
               in_specs=[row, row, pl.BlockSpec((1, 3, d), lambda i: (i // per, 0, 0)),
                         pl.BlockSpec((di, d), lambda i: (0, 0))],
               out_specs=[row, pl.BlockSpec((1, 1, d), lambda i: (i // per, 0, 0)),
                          pl.BlockSpec((tm, di), lambda i: (i, 0))],
               sem=("arbitrary",))(dxn, yout, mod, w_out)


def _mm_dw_out(ybr, dy, name, comm=None):
    m, di = ybr.shape
    d = dy.shape[1]
    tk = 2048 if m % 2048 == 0 else _tile(m, 512)
    tn = _tile(di, 1024)

    def body(y_ref, dy_ref, o_ref):
        kk = pl.program_id(1)
        acc = _dot_tn(y_ref[...], dy_ref[...])

        @pl.when(kk == 0)
        def _():
            o_ref[...] = acc

        @pl.when(kk > 0)
        def _():
            o_ref[...] += acc

    return _pc(body, name=name, out_shape=jax.ShapeDtypeStruct((di, d), F32), grid=(di // tn, m // tk),
               in_specs=[pl.BlockSpec((tk, tn), lambda n, k: (k, n)), pl.BlockSpec((tk, d), lambda n, k: (k, 0))],
               out_specs=pl.BlockSpec((tn, d), lambda n, k: (n, 0)), sem=("parallel", "arbitrary"),
               comm=comm)(ybr, dy)


def _sgu_mask():
    t = lax.broadcasted_iota(jnp.int32, (SG_BLOCK, SG_BLOCK), 0)
    s = lax.broadcasted_iota(jnp.int32, (SG_BLOCK, SG_BLOCK), 1)
    return (s // CHUNK) <= (t // CHUNK)


def _a_mid_fwd(proj, ln_g, ln_b, w_s, bs_t, t_seq, comm=None):
    m, n3 = proj.shape
    di = n3 // 3
    gd = di // SG_GROUPS
    r = _tile(t_seq, 256)
    nblk = r // SG_BLOCK

    def body(p_ref, lg_ref, lb_ref, ws_ref, bs_ref, ybr_ref, s_scr):
        v = _gelu(p_ref[:, di:2 * di])
        mu = jnp.mean(v, axis=-1, keepdims=True)
        vc = v - mu
        rstd = lax.rsqrt(jnp.mean(vc * vc, axis=-1, keepdims=True) + EPS)
        vb = (vc * rstd * lg_ref[...] + lb_ref[...]).astype(BF16)
        mask = _sgu_mask()
        for gi in range(SG_GROUPS):
            ws = jnp.where(mask, ws_ref[gi], 0.0).astype(BF16)
            bcol = bs_ref[:, gi:gi + 1]
            for b in range(nblk):
                rows = slice(b * SG_BLOCK, (b + 1) * SG_BLOCK)
                cols = slice(gi * gd, (gi + 1) * gd)
                s_scr[rows, cols] = _dot(ws, vb[rows, cols]) + bcol
        gg = p_ref[:, 2 * di:]
        ybr_ref[...] = (_gelu(p_ref[:, :di]) * s_scr[...] * (gg * _sigmoid(gg))).astype(BF16)

    vec = pl.BlockSpec((1, di), lambda i: (0, 0))
    return _pc(body, name="a_mid_fwd", out_shape=jax.ShapeDtypeStruct((m, di), BF16), grid=(m // r,),
               in_specs=[pl.BlockSpec((r, n3), lambda i: (i, 0)), vec, vec,
                         pl.BlockSpec((SG_GROUPS, SG_BLOCK, SG_BLOCK), lambda i: (0, 0, 0)),
                         pl.BlockSpec((SG_BLOCK, 128), lambda i: (0, 0))],
               out_specs=pl.BlockSpec((r, di), lambda i: (i, 0)),
               scratch=[pltpu.VMEM((r, di), F32)], sem=("parallel",), comm=comm)(proj, ln_g, ln_b, w_s, bs_t)


def _a_mid_bwd(proj, dybr, ln_g, ln_b, w_s, bs_t, t_seq, comm=None):
    m, n3 = proj.shape
    di = n3 // 3
    gd = di // SG_GROUPS
    r = _tile(t_seq, 256)
    nblk = r // SG_BLOCK

    def body(p_ref, dy_ref, lg_ref, lb_ref, ws_ref, bs_ref,
             dp_ref, dlg_ref, dlb_ref, dws_ref, dbs_ref, s_scr, dvl_scr):
        i = pl.program_id(0)

        @pl.when(i == 0)
        def _():
            dlg_ref[...] = jnp.zeros_like(dlg_ref)
            dlb_ref[...] = jnp.zeros_like(dlb_ref)
            dws_ref[...] = jnp.zeros_like(dws_ref)
            dbs_ref[...] = jnp.zeros_like(dbs_ref)

        v, dgelu_v = _gelu_and_grad(p_ref[:, di:2 * di])
        mu = jnp.mean(v, axis=-1, keepdims=True)
        vc = v - mu
        rstd = lax.rsqrt(jnp.mean(vc * vc, axis=-1, keepdims=True) + EPS)
        vhat = vc * rstd
        lg = lg_ref[...]
        vb = (vhat * lg + lb_ref[...]).astype(BF16)
        u, dgelu_u = _gelu_and_grad(p_ref[:, :di])
        gg = p_ref[:, 2 * di:]
        sg = _sigmoid(gg)
        dyv = dy_ref[...]
        dus = dyv * (gg * sg)
        dsb = (dus * u).astype(BF16)
        ds32 = dus * u
        mask = _sgu_mask()
        lane = lax.broadcasted_iota(jnp.int32, (SG_BLOCK, 128), 1)
        dbs_acc = jnp.zeros((SG_BLOCK, 128), F32)
        for gi in range(SG_GROUPS):
            ws = jnp.where(mask, ws_ref[gi], 0.0).astype(BF16)
            bcol = bs_ref[:, gi:gi + 1]
            cols = slice(gi * gd, (gi + 1) * gd)
            dws_acc = jnp.zeros((SG_BLOCK, SG_BLOCK), F32)
            dbs_col = jnp.zeros((SG_BLOCK, 1), F32)
            for b in range(nblk):
                rows = slice(b * SG_BLOCK, (b + 1) * SG_BLOCK)
                s_scr[rows, cols] = _dot(ws, vb[rows, cols]) + bcol
                dvl_scr[rows, cols] = _dot_tn(ws, dsb[rows, cols])
                dws_acc += _dot_nt(dsb[rows, cols], vb[rows, cols])
                dbs_col += jnp.sum(ds32[rows, cols], axis=-1, keepdims=True)
            dws_ref[gi] += jnp.where(mask, dws_acc, 0.0)
            dbs_acc += jnp.where(lane == gi, dbs_col, 0.0)
        dbs_ref[...] += dbs_acc
        s = s_scr[...]
        dp_ref[:, :di] = (dus * s * dgelu_u).astype(BF16)
        dp_ref[:, 2 * di:] = (dyv * u * s * (sg * (1.0 + gg * (1.0 - sg)))).astype(BF16)
        dvl = dvl_scr[...]
        dlg_ref[...] += jnp.sum(dvl * vhat, axis=0, keepdims=True)
        dlb_ref[...] += jnp.sum(dvl, axis=0, keepdims=True)
        dvh = dvl * lg
        dv = rstd * (dvh - jnp.mean(dvh, axis=-1, keepdims=True)
                     - vhat * jnp.mean(dvh * vhat, axis=-1, keepdims=True))
        dp_ref[:, di:2 * di] = (dv * dgelu_v).astype(BF16)

    vec = pl.BlockSpec((1, di), lambda i: (0, 0))
    wsb = pl.BlockSpec((SG_GROUPS, SG_BLOCK, SG_BLOCK), lambda i: (0, 0, 0))
    bsb = pl.BlockSpec((SG_BLOCK, 128), lambda i: (0, 0))
    return _pc(body, name="a_mid_bwd",
               out_shape=[jax.ShapeDtypeStruct((m, n3), BF16), jax.ShapeDtypeStruct((1, di), F32),
                          jax.ShapeDtypeStruct((1, di), F32),
                          jax.ShapeDtypeStruct((SG_GROUPS, SG_BLOCK, SG_BLOCK), F32),
                          jax.ShapeDtypeStruct((SG_BLOCK, 128), F32)],
               grid=(m // r,),
               in_specs=[pl.BlockSpec((r, n3), lambda i: (i, 0)), pl.BlockSpec((r, di), lambda i: (i, 0)),
                         vec, vec, wsb, bsb],
               out_specs=[pl.BlockSpec((r, n3), lambda i: (i, 0)), vec, vec, wsb, bsb],
               scratch=[pltpu.VMEM((r, di), F32), pltpu.VMEM((r, di), F32)],
               sem=("arbitrary",), comm=comm)(proj, dybr, ln_g, ln_b, w_s, bs_t)


def _hgrn_dims(t_seq, di):
    tr = _tile(t_seq, 128)
    hc = _tile(di, 2048)
    return tr, hc, hc // HEAD_DIM


def _hgrn_gates(f_ref, lb, a_scr, k_scr, tr):
    sig = _sigmoid(f_ref[...])
    fg = lb + (1.0 - lb) * sig
    k_scr[...] = 1.0 - fg
    logf = jnp.log(fg)
    g = min(CUM_ROWS, tr)
    tri = _tri_mask(g, reverse=False)
    for rg in range(tr // g):
        a_scr[rg * g:(rg + 1) * g, :] = _tri_apply(tri, logf[rg * g:(rg + 1) * g, :])
    return sig, fg


def _hgrn_fwd(proj, lbj, gn, nb, t_seq, comm=None):
    _, m, di = proj.shape
    tr, hc, hpg = _hgrn_dims(t_seq, di)
    nt, nhg, ncl = t_seq // tr, di // hc, tr // CHUNK
    nheads = di // HEAD_DIM

    def body(p_ref, lb_ref, gn_ref, o_ref, ybr_ref, st_ref, st_scr, a_scr, k_scr):
        q_ref, f_ref, i_ref, g_ref = (p_ref.at[s] for s in range(4))
        t = pl.program_id(2)

        @pl.when(t == 0)
        def _():
            st_scr[...] = jnp.zeros_like(st_scr)

        _hgrn_gates(f_ref, lb_ref[0:1, :], a_scr, k_scr, tr)
        gnv = gn_ref[...]
        rr = lax.broadcasted_iota(jnp.int32, (CHUNK, CHUNK), 0)
        cc = lax.broadcasted_iota(jnp.int32, (CHUNK, CHUNK), 1)
        causal = cc <= rr

        def chunk(n, carry):
            rows = pl.ds(pl.multiple_of(n * CHUNK, CHUNK), CHUNK)
            lanes = [slice(hd * HEAD_DIM, (hd + 1) * HEAD_DIM) for hd in range(hpg)]
            hs = []
            for hd, ls in enumerate(lanes):
                h = {}
                ah, kh = a_scr[rows, ls], k_scr[rows, ls]
                qp = q_ref[rows, ls]
                qh = qp * _sigmoid(qp)
                h["vb"] = i_ref[rows, ls].astype(BF16)
                aref, alast = ah[CHUNK // 2 - 1:CHUNK // 2, :], ah[CHUNK - 1:CHUNK, :]
                h["q_in"] = (qh * jnp.exp(ah - aref)).astype(BF16)
                h["k_in"] = (kh * jnp.exp(aref - ah)).astype(BF16)
                h["q_out"] = (qh * jnp.exp(ah)).astype(BF16)
                h["k_out"] = (kh * jnp.exp(alast - ah)).astype(BF16)
                h["dec"] = jnp.exp(alast)
                st = st_scr[hd]
                st_ref[n, hd] = st
                h["st"] = st
                hs.append(h)
            for h in hs:
                h["scores"] = _dot_nt(h["q_in"], h["k_in"])
                h["o_inter"] = _dot_nt(h["q_out"], h["st"].astype(BF16))
                h["st_mm"] = _dot_tn(h["vb"], h["k_out"])
            for h in hs:
                h["o"] = _dot(jnp.where(causal, h["scores"], 0.0).astype(BF16), h["vb"]) + h["o_inter"]
            for hd, (h, ls) in enumerate(zip(hs, lanes)):
                st_scr[hd] = h["st"] * h["dec"] + h["st_mm"]
                o = h["o"]
                o_ref[rows, ls] = o
                rstd = lax.rsqrt(jnp.mean(o * o, axis=-1, keepdims=True) + EPS)
                gg = g_ref[rows, ls]
                ybr_ref[rows, ls] = ((o * rstd * gnv) * (gg * _sigmoid(gg))).astype(BF16)
            return carry

        lax.fori_loop(0, ncl, chunk, 0)

    blk = pl.BlockSpec((tr, hc), lambda hg, b, t: (b * nt + t, hg))
    return _pc(body, name="hgrn_fwd",
               out_shape=[jax.ShapeDtypeStruct((m, di), F32), jax.ShapeDtypeStruct((m, di), BF16),
                          jax.ShapeDtypeStruct((m // CHUNK, nheads, HEAD_DIM, HEAD_DIM), F32)],
               grid=(nhg, nb, nt),
               in_specs=[pl.BlockSpec((4, tr, hc), lambda hg, b, t: (0, b * nt + t, hg)),
                         pl.BlockSpec((2, hc), lambda hg, b, t: (0, hg)),
                         pl.BlockSpec((1, HEAD_DIM), lambda hg, b, t: (0, 0))],
               out_specs=[blk, blk, pl.BlockSpec((ncl, hpg, HEAD_DIM, HEAD_DIM),
                                                 lambda hg, b, t: (b * nt + t, hg, 0, 0))],
               scratch=[pltpu.VMEM((hpg, HEAD_DIM, HEAD_DIM), F32), pltpu.VMEM((tr, hc), F32),
                        pltpu.VMEM((tr, hc), F32)],
               sem=("parallel", "arbitrary", "arbitrary"), comm=comm)(proj, lbj, gn)


def _hgrn_bwd(proj, o_all, dybr, states, lbj, gn, nb, t_seq, comm=None):
    _, m, di = proj.shape
    tr, hc, hpg = _hgrn_dims(t_seq, di)
    nt, nhg, ncl = t_seq // tr, di // hc, tr // CHUNK

    def body(p_ref, o_ref, dy_ref, st_ref, lb_ref, gn_ref,
             dp_ref, dlb_ref, dgn_ref, dst_scr, a_scr, k_scr, da_scr, dk_scr):
        q_ref, f_ref, i_ref, g_ref = (p_ref.at[s] for s in range(4))
        hg, b, t = pl.program_id(0), pl.program_id(1), pl.program_id(2)

        @pl.when(t == 0)
        def _():
            dst_scr[...] = jnp.zeros_like(dst_scr)

        @pl.when((b == 0) & (t == 0))
        def _():
            dlb_ref[...] = jnp.zeros_like(dlb_ref)

        @pl.when((hg == 0) & (b == 0) & (t == 0))
        def _():
            dgn_ref[...] = jnp.zeros_like(dgn_ref)

        lb = lb_ref[0:1, :]
        sig, fg = _hgrn_gates(f_ref, lb, a_scr, k_scr, tr)
        gnv = gn_ref[...]
        rr = lax.broadcasted_iota(jnp.int32, (CHUNK, CHUNK), 0)
        cc = lax.broadcasted_iota(jnp.int32, (CHUNK, CHUNK), 1)
        causal = cc <= rr
        rowi = lax.broadcasted_iota(jnp.int32, (CHUNK, HEAD_DIM), 0)

        def chunk(it, carry):
            n = ncl - 1 - it
            rows = pl.ds(pl.multiple_of(n * CHUNK, CHUNK), CHUNK)
            lanes = [slice(hd * HEAD_DIM, (hd + 1) * HEAD_DIM) for hd in range(hpg)]
            hs = []
            for hd, ls in enumerate(lanes):
                h = {}
                ah, kh = a_scr[rows, ls], k_scr[rows, ls]
                qp = q_ref[rows, ls]
                sq = _sigmoid(qp)
                qh = qp * sq
                h["dsilu_q"] = sq * (1.0 + qp * (1.0 - sq))
                h["vb"] = i_ref[rows, ls].astype(BF16)
                aref, alast = ah[CHUNK // 2 - 1:CHUNK // 2, :], ah[CHUNK - 1:CHUNK, :]
                h["e1"], h["e2"] = jnp.exp(ah - aref), jnp.exp(aref - ah)
                h["e3"], h["e4"] = jnp.exp(ah), jnp.exp(alast - ah)
                h["dec"] = jnp.exp(alast)
                h["q_in"], h["k_in"], h["q_out"], h["k_out"] = qh * h["e1"], kh * h["e2"], qh * h["e3"], kh * h["e4"]
                for nm in ("q_in", "k_in", "q_out", "k_out"):
                    h[nm + "_b"] = h[nm].astype(BF16)
                o = o_ref[rows, ls]
                rstd = lax.rsqrt(jnp.mean(o * o, axis=-1, keepdims=True) + EPS)
                ohat = o * rstd
                gg = g_ref[rows, ls]
                sg = _sigmoid(gg)
                dyv = dy_ref[rows, ls]
                d_on = dyv * (gg * sg)
                dp_ref[3, rows, ls] = (dyv * (ohat * gnv) * (sg * (1.0 + gg * (1.0 - sg)))).astype(BF16)
                h["dgn"] = jnp.sum(d_on * ohat, axis=0, keepdims=True)
                dohat = d_on * gnv
                do = rstd * (dohat - ohat * jnp.mean(dohat * ohat, axis=-1, keepdims=True))
                h["do_b"] = do.astype(BF16)
                h["st_prev"] = st_ref[n, hd]
                h["dst"] = dst_scr[hd]
                hs.append(h)
            for h in hs:
                dst_b = h["dst"].astype(BF16)
                h["scores"] = _dot_nt(h["q_in_b"], h["k_in_b"])
                h["dscores"] = _dot_nt(h["do_b"], h["vb"])
                h["dv_inter"] = _dot_nt(h["k_out_b"], dst_b)
                h["dq_out"] = _dot(h["do_b"], h["st_prev"].astype(BF16))
                h["dk_out"] = _dot(h["vb"], dst_b)
                h["dst_mm"] = _dot_tn(h["do_b"], h["q_out_b"])
            for h in hs:
                scores = jnp.where(causal, h["scores"], 0.0).astype(BF16)
                dscores = jnp.where(causal, h["dscores"], 0.0).astype(BF16)
                h["dv"] = _dot_tn(scores, h["do_b"]) + h["dv_inter"]
                h["dq_in"] = _dot(dscores, h["k_in_b"])
                h["dk_in"] = _dot_tn(dscores, h["q_in_b"])
            dgn = hs[0]["dgn"]
            for h in hs[1:]:
                dgn = dgn + h["dgn"]
            dgn_ref[...] += dgn
            for hd, (h, ls) in enumerate(zip(hs, lanes)):
                ddec = jnp.sum(h["dst"] * h["st_prev"], axis=0, keepdims=True)
                dst_scr[hd] = h["dst"] * h["dec"] + h["dst_mm"]
                dp_ref[2, rows, ls] = h["dv"].astype(BF16)
                dq = h["dq_in"] * h["e1"] + h["dq_out"] * h["e3"]
                dp_ref[0, rows, ls] = (dq * h["dsilu_q"]).astype(BF16)
                dk_scr[rows, ls] = h["dk_in"] * h["e2"] + h["dk_out"] * h["e4"]
                t_in = h["dq_in"] * h["q_in"] - h["dk_in"] * h["k_in"]
                t_out = h["dk_out"] * h["k_out"]
                da = t_in + h["dq_out"] * h["q_out"] - t_out
                da_ref_row = -jnp.sum(t_in, axis=0, keepdims=True)
                da_last_row = jnp.sum(t_out, axis=0, keepdims=True) + ddec * h["dec"]
                da = da + jnp.where(rowi == CHUNK // 2 - 1, da_ref_row, 0.0) \
                        + jnp.where(rowi == CHUNK - 1, da_last_row, 0.0)
                da_scr[rows, ls] = da
            return carry

        lax.fori_loop(0, ncl, chunk, 0)
        g = min(CUM_ROWS, tr)
        tri = _tri_mask(g, reverse=True)
        for rg in range(tr // g):
            rs = slice(rg * g, (rg + 1) * g)
            dlogf = _tri_apply(tri, da_scr[rs, :])
            df = dlogf / fg[rs, :] - dk_scr[rs, :]
            sgr = sig[rs, :]
            dp_ref[1, rs, :] = (df * (1.0 - lb) * (sgr * (1.0 - sgr))).astype(BF16)
            dlb_ref[...] += jnp.sum(df * (1.0 - sgr), axis=0, keepdims=True) * lb_ref[1:2, :]

    blk = pl.BlockSpec((tr, hc), lambda hg, b, t: (b * nt + (nt - 1 - t), hg))
    return _pc(body, name="hgrn_bwd",
               out_shape=[jax.ShapeDtypeStruct((4, m, di), BF16), jax.ShapeDtypeStruct((1, di), F32),
                          jax.ShapeDtypeStruct((1, HEAD_DIM), F32)],
               grid=(nhg, nb, nt),
               in_specs=[pl.BlockSpec((4, tr, hc), lambda hg, b, t: (0, b * nt + (nt - 1 - t), hg)), blk, blk,
                         pl.BlockSpec((ncl, hpg, HEAD_DIM, HEAD_DIM),
                                      lambda hg, b, t: (b * nt + (nt - 1 - t), hg, 0, 0)),
                         pl.BlockSpec((2, hc), lambda hg, b, t: (0, hg)),
                         pl.BlockSpec((1, HEAD_DIM), lambda hg, b, t: (0, 0))],
               out_specs=[pl.BlockSpec((4, tr, hc), lambda hg, b, t: (0, b * nt + (nt - 1 - t), hg)),
                          pl.BlockSpec((1, hc), lambda hg, b, t: (0, hg)),
                          pl.BlockSpec((1, HEAD_DIM), lambda hg, b, t: (0, 0))],
               scratch=[pltpu.VMEM((hpg, HEAD_DIM, HEAD_DIM), F32)] + [pltpu.VMEM((tr, hc), F32)] * 4,
               sem=("arbitrary", "arbitrary", "arbitrary"), comm=comm)(
                   proj, o_all, dybr, states, lbj, gn)


def _adamw(parts, w, m, v, name, comm=None):
    r, c = w.shape
    tr = _tile(r, 256)
    npart = len(parts)
    c1 = 1.0 - ADAM_B1 ** ADAM_STEP
    c2 = 1.0 - ADAM_B2 ** ADAM_STEP

    def body(*refs):
        p_refs = refs[:npart]
        _adamw_math(p_refs, *refs[npart:], c1, c2)

    blk = pl.BlockSpec((tr, c), lambda i: (i, 0))
    return _pc(body, name=name, out_shape=[jax.ShapeDtypeStruct((r, c), F32)] * 4, grid=(r // tr,),
               in_specs=[blk] * (npart + 3), out_specs=[blk] * 4, sem=("parallel",), comm=comm)(*parts, w, m, v)


def _adamw_math(p_refs, w_ref, m_ref, v_ref, g_ref, d_ref, nm_ref, nv_ref, c1, c2):
    g = p_refs[0][...].astype(F32)
    for p in p_refs[1:]:
        g = g + p[...].astype(F32)
    nm = ADAM_B1 * m_ref[...] + (1.0 - ADAM_B1) * g
    nv = ADAM_B2 * v_ref[...] + (1.0 - ADAM_B2) * (g * g)
    g_ref[...] = g
    nm_ref[...] = nm
    nv_ref[...] = nv
    d_ref[...] = -ADAM_LR * ((nm / c1) / (jnp.sqrt(nv / c2) + ADAM_EPS) + ADAM_WD * w_ref[...])


def _adamw_blocks(parts, idx, w, m, v, name):
    r, c = w.shape
    tr = _tile(r, 256)
    npart = len(parts)
    c1 = 1.0 - ADAM_B1 ** ADAM_STEP
    c2 = 1.0 - ADAM_B2 ** ADAM_STEP

    def body(idx_ref, *refs):
        _adamw_math(refs[:npart], *refs[npart:], c1, c2)

    def sel(p):
        return pl.BlockSpec((None, tr, c), lambda i, s: (s[p], i, 0))

    blk = pl.BlockSpec((tr, c), lambda i, s: (i, 0))
    gs = pltpu.PrefetchScalarGridSpec(num_scalar_prefetch=1, grid=(r // tr,),
                                      in_specs=[sel(p) for p in range(npart)] + [blk] * 3, out_specs=[blk] * 4)
    return _pc(body, name=name, out_shape=[jax.ShapeDtypeStruct((r, c), F32)] * 4, grid_spec=gs,
               sem=("parallel",))(idx, *parts, w, m, v)


_EARLY = ["a_ln_gain", "a_ln_bias", "a_w_s", "a_b_s", "b_lower_bounds", "b_gn_gain"]


def _pack(arrs):
    flat = jnp.concatenate([a.reshape(-1) for a in arrs])
    rows = -(-flat.shape[0] // 1024) * 8
    return jnp.pad(flat, (0, rows * 128 - flat.shape[0])).reshape(rows, 128)


def _unpack(buf, like):
    flat = buf.reshape(-1)
    out, off = [], 0
    for a in like:
        out.append(flat[off:off + a.size].reshape(a.shape))
        off += a.size
    return out


def kernel(x, c, norm_gain, w_ada, b_ada, a_w_in, a_ln_gain, a_ln_bias, a_w_s, a_b_s, a_w_out, b_w_in, b_lower_bounds, b_gn_gain, b_w_out, final_gain, loss_target, m_norm_gain, m_w_ada, m_b_ada, m_a_w_in, m_a_ln_gain, m_a_ln_bias, m_a_w_s, m_a_b_s, m_a_w_out, m_b_w_in, m_b_lower_bounds, m_b_gn_gain, m_b_w_out, m_final_gain, v_norm_gain, v_w_ada, v_b_ada, v_a_w_in, v_a_ln_gain, v_a_ln_bias, v_a_w_s, v_a_b_s, v_a_w_out, v_b_w_in, v_b_lower_bounds, v_b_gn_gain, v_b_w_out, v_final_gain):
    w = dict(norm_gain=norm_gain, w_ada=w_ada, b_ada=b_ada, a_w_in=a_w_in, a_ln_gain=a_ln_gain,
             a_ln_bias=a_ln_bias, a_w_s=a_w_s, a_b_s=a_b_s, a_w_out=a_w_out, b_w_in=b_w_in,
             b_lower_bounds=b_lower_bounds, b_gn_gain=b_gn_gain, b_w_out=b_w_out, final_gain=final_gain)
    mo = dict(norm_gain=m_norm_gain, w_ada=m_w_ada, b_ada=m_b_ada, a_w_in=m_a_w_in, a_ln_gain=m_a_ln_gain,
              a_ln_bias=m_a_ln_bias, a_w_s=m_a_w_s, a_b_s=m_a_b_s, a_w_out=m_a_w_out, b_w_in=m_b_w_in,
              b_lower_bounds=m_b_lower_bounds, b_gn_gain=m_b_gn_gain, b_w_out=m_b_w_out, final_gain=m_final_gain)
    vo = dict(norm_gain=v_norm_gain, w_ada=v_w_ada, b_ada=v_b_ada, a_w_in=v_a_w_in, a_ln_gain=v_a_ln_gain,
              a_ln_bias=v_a_ln_bias, a_w_s=v_a_w_s, a_b_s=v_a_b_s, a_w_out=v_a_w_out, b_w_in=v_b_w_in,
              b_lower_bounds=v_b_lower_bounds, b_gn_gain=v_b_gn_gain, b_w_out=v_b_w_out, final_gain=v_final_gain)

    nb, t_seq, d = x.shape
    m = nb * t_seq
    ncol_ada = w_ada.shape[2]
    xi, yi, ci = lax.axis_index("x"), lax.axis_index("y"), lax.axis_index("c")
    me = 4 * xi + 2 * yi + ci

    c_g, wa_in_g = _all_gather([c, a_w_in[0].astype(BF16)], "gather_c_wa")

    c_all = c_g.reshape(NDEV * nb, d)
    b_cols = lax.dynamic_slice(b_ada, (0, me * ncol_ada), (2, ncol_ada)).reshape(2, 1, ncol_ada)
    mod_part, lbj = _ada_fwd(c_all, w_ada, b_cols, b_lower_bounds)
    mod_all = _all_gather([mod_part], "gather_mod")[0]
    mod_mine = lax.dynamic_slice_in_dim(mod_all, me * nb, nb, axis=2)
    mod_mine = mod_mine.transpose(1, 2, 0, 3).reshape(2, nb, 3, d)
    mod0, mod1 = mod_mine[0], mod_mine[1]

    di = a_w_out.shape[1] * NDEV

    xf = x.reshape(m, d)
    tgt = loss_target.reshape(m, d)
    ng0, ng1 = norm_gain[0:1], norm_gain[1:2]
    ncb = b_w_in.shape[2]
    wb_lo, wb_hi = b_w_in[0][:, :ncb // 2].astype(BF16), b_w_in[0][:, ncb // 2:].astype(BF16)
    h0, h0_t = _prenorm(xf, ng0, mod0, t_seq, "prenorm_a")
    proj_a, half = _mm_in(h0, [wa_in_g], 1, "in_proj_a", comm=_gather_first([a_w_out[0].astype(BF16), wb_lo]))
    bs_t = jnp.pad(a_b_s[0].T, ((0, 0), (0, 128 - SG_GROUPS)))
    ybr_a, (wa_out_g, wb_lo_g, wb_hi_half) = _a_mid_fwd(
        proj_a, a_ln_gain, a_ln_bias, a_w_s[0], bs_t, t_seq, comm=_join(_gather_second(half), _gather_first([wb_hi])))
    wa_out = wa_out_g.reshape(di, d)
    (yout_a, x1), (wb_hi_g, wb_out_half) = _out_proj(
        ybr_a, wa_out, xf, mod0, t_seq, "out_proj_a",
        comm=_join(_gather_second([wb_hi_half]), _gather_first([b_w_out[0].astype(BF16)])))
    wb_in_g = [wb_lo_g, wb_hi_g]
    h1, h1_t = _prenorm(x1, ng1, mod1, t_seq, "prenorm_b")
    proj_b, (wb_out_g,) = _mm_in(h1, wb_in_g, 4, "in_proj_b", comm=_gather_second([wb_out_half]))
    wb_out = wb_out_g.reshape(di, d)
    o_b, ybr_b, states = _hgrn_fwd(proj_b, lbj, b_gn_gain, nb, t_seq)
    yout_b, dx2, loss_part, d_final_gain = _out_proj_loss(ybr_b, wb_out, x1, mod1, final_gain.reshape(1, d), tgt, t_seq)

    rows_out = a_w_out.shape[1]
    dy_b, dgate1, dybr_b = _gate_dybr(dx2, yout_b, mod1, wb_out, t_seq, "dybr_b")
    rs_wb_out = _ReduceScatter(_mm_dw_out(ybr_b, dy_b, "dw_out_b").reshape(NDEV, rows_out, d), "b_w_out")
    (dproj_b, d_lb, d_gn), got = _hgrn_bwd(proj_b, o_b, dybr_b, states, lbj, b_gn_gain, nb, t_seq,
                                           comm=rs_wb_out.swap_core())
    rs_wb_out.after_core(got[0])
    dh1, got = _mm_din(dproj_b, wb_in_g, 4, "dh_b", comm=rs_wb_out.swap_chips())
    rs_wb_out.after_chips(got[0])
    dx1, dss1, dgain1 = _prenorm_bwd(dh1, x1, ng1, mod1, dx2, t_seq, "prenorm_bwd_b")
    rs_wb_in = _ReduceScatter(_mm_dw_in(h1_t, dproj_b, ncb, 4, "dw_in_b"), "b_w_in")

    dy_a, dgate0, dybr_a = _gate_dybr(dx1, yout_a, mod0, wa_out, t_seq, "dybr_a")
    g_wa_out, got = _mm_dw_out(ybr_a, dy_a, "dw_out_a", comm=rs_wb_in.swap_core())
    rs_wb_in.after_core(got[0])
    rs_wa_out = _ReduceScatter(g_wa_out.reshape(NDEV, rows_out, d), "a_w_out")
    (dproj_a, d_lng, d_lnb, d_ws, d_bs_t), got = _a_mid_bwd(
        proj_a, dybr_a, a_ln_gain, a_ln_bias, a_w_s[0], bs_t, t_seq,
        comm=_join(rs_wb_in.swap_chips(), rs_wa_out.swap_core()))
    rs_wb_in.after_chips(got[0])
    rs_wa_out.after_core(got[1])
    part = dict(a_ln_gain=d_lng, a_ln_bias=d_lnb, a_w_s=d_ws[None], a_b_s=d_bs_t[:, :SG_GROUPS].T[None],
                b_lower_bounds=jnp.concatenate([-d_lb, d_lb], axis=0), b_gn_gain=d_gn)
    early_pack = _pack([part[k].reshape(w[k].shape) for k in _EARLY])
    g_wa_in, got = _mm_dw_in(h0_t, dproj_a, wa_in_g.shape[2], 1, "dw_in_a",
                             comm=_join(rs_wa_out.swap_chips(), _gather_first([early_pack])))
    rs_wa_out.after_chips(got[0])
    rs_wa_in = _ReduceScatter(g_wa_in, "a_w_in")
    n_tiles = m // _din_tile(m)
    assert n_tiles >= 2
    first_tiles = max(1, (3 * n_tiles) // 8)
    dh0, got2 = _mm_din(dproj_a, [wa_in_g], 1, "dh_a_first", tiles=(0, first_tiles),
                        comm=_join(rs_wa_in.swap_core(), _gather_second([got[1]])))
    rs_wa_in.after_core(got2[0])
    early_all = got2[1]
    dh0, got = _mm_din(dproj_a, [wa_in_g], 1, "dh_a_rest", comm=rs_wa_in.swap_chips(),
                       tiles=(first_tiles, n_tiles - first_tiles), prev=dh0)
    rs_wa_in.after_chips(got[0])
    dx0, dss0, dgain0 = _prenorm_bwd(dh0, xf, ng0, mod0, dx1, t_seq, "prenorm_bwd_a")
    grad_x = dx0.reshape(nb, t_seq, d)

    dmod = jnp.stack([jnp.concatenate([dss0, dgate0], axis=1), jnp.concatenate([dss1, dgate1], axis=1)])
    late_like = [norm_gain, final_gain, loss_part.reshape(1)]
    late_pack = _pack([jnp.concatenate([dgain0, dgain1], axis=0), d_final_gain[0], loss_part.reshape(1)])
    dmod_all, late_all = _all_gather([dmod.reshape(2, nb, 3 * d), late_pack], "gather_tail")
    dmod_all = dmod_all.transpose(1, 0, 2, 3).reshape(2, NDEV * nb, 3 * d)
    dmod_cols = lax.dynamic_slice_in_dim(dmod_all, me * ncol_ada, ncol_ada, axis=2)
    g_w_ada, g_b_ada = _ada_bwd(c_all, dmod_cols, dmod_all)

    res = {}
    early_like = [w[k] for k in _EARLY]
    dev_order = jnp.arange(NDEV, dtype=jnp.int32)
    sm = _adamw_blocks([early_all] * NDEV, dev_order, _pack(early_like), _pack([mo[k] for k in _EARLY]),
                       _pack([vo[k] for k in _EARLY]), "adamw_small_early")
    sm = [dict(zip(_EARLY, _unpack(buf, early_like))) for buf in sm]
    for k in _EARLY:
        res[k] = tuple(s[k] for s in sm)
    zero = jnp.zeros((1,), F32)
    sm = _adamw_blocks([late_all] * NDEV, dev_order, _pack([norm_gain, final_gain, zero]),
                       _pack([mo["norm_gain"], mo["final_gain"], zero]),
                       _pack([vo["norm_gain"], vo["final_gain"], zero]), "adamw_small_late")
    sm = [_unpack(buf, late_like) for buf in sm]
    res["norm_gain"] = tuple(s[0] for s in sm)
    res["final_gain"] = tuple(s[1] for s in sm)
    loss = sm[0][2][0]
    rb = _adamw([g_b_ada], b_ada, mo["b_ada"], vo["b_ada"], "adamw_b_ada")
    res["b_ada"] = tuple(rb)
    sh = w_ada.shape
    ra = _adamw([g_w_ada.reshape(sh[0] * sh[1], sh[2])], w_ada.reshape(sh[0] * sh[1], sh[2]),
                mo["w_ada"].reshape(sh[0] * sh[1], sh[2]), vo["w_ada"].reshape(sh[0] * sh[1], sh[2]), "adamw_w_ada")
    res["w_ada"] = tuple(z.reshape(sh) for z in ra)

    for k, rs in (("b_w_out", rs_wb_out), ("b_w_in", rs_wb_in), ("a_w_out", rs_wa_out), ("a_w_in", rs_wa_in)):
        res[k] = tuple(z[None] for z in _adamw_blocks(rs.parts, rs.idx, w[k][0], mo[k][0], vo[k][0], "adamw_" + k))

    order = ["norm_gain", "w_ada", "b_ada", "a_w_in", "a_ln_gain", "a_ln_bias", "a_w_s", "a_b_s", "a_w_out",
             "b_w_in", "b_lower_bounds", "b_gn_gain", "b_w_out", "final_gain"]
    return (loss, grad_x, *[res[k][0] for k in order], *[res[k][1] for k in order],
            *[res[k][2] for k in order], *[res[k][3] for k in order])
```

```python
import functools
import math

import jax
import jax.numpy as jnp
from jax import lax
from jax.experimental import pallas as pl
from jax.experimental.pallas import tpu as pltpu

F32 = jnp.float32
BF16 = jnp.bfloat16
MESH = pl.DeviceIdType.MESH
NDEV = 8
EPS = 1e-6
CHUNK = 64
SG_BLOCK = 128
SG_GROUPS = 8
HEAD_DIM = 128
CUM_ROWS = 256
ADAM_LR, ADAM_B1, ADAM_B2, ADAM_EPS, ADAM_WD, ADAM_STEP = 0.001, 0.9, 0.999, 1e-08, 0.01, 10
VMEM_LIMIT = 56 * 1024 * 1024
ANY = pl.BlockSpec(memory_space=pl.ANY)


class _Hosted:
    def __init__(self, arrays, out_shapes, nsem, start, finish, aliases=None):
        self.arrays, self.out_shapes, self.nsem = list(arrays), list(out_shapes), nsem
        self.start, self.finish = start, finish
        self.aliases = dict(aliases or {})


def _join(*comms):
    arrays, outs, aliases, offs, nsem = [], [], {}, [], 0
    for cm in comms:
        offs.append((len(arrays), len(outs), nsem))
        for i, o in cm.aliases.items():
            aliases[len(arrays) + i] = len(outs) + o
        arrays += cm.arrays
        outs += cm.out_shapes
        nsem += cm.nsem

    def run(which):
        def f(ins, outs_, ss, rs, base):
            for cm, (ia, io, isem) in zip(comms, offs):
                getattr(cm, which)(ins[ia:ia + len(cm.arrays)], outs_[io:io + len(cm.out_shapes)], ss, rs, base + isem)
        return f

    return _Hosted(arrays, outs, nsem, run("start"), run("finish"), aliases)


def _pc(body, *, name, out_shape, grid=None, in_specs=None, out_specs=None, scratch=(), sem=None,
        grid_spec=None, comm=None, aliases=None):
    cp = dict(vmem_limit_bytes=VMEM_LIMIT)
    aliases = dict(aliases or {})
    if comm is None:
        if sem is not None:
            cp["dimension_semantics"] = sem
        kw = {"input_output_aliases": aliases}
        if grid_spec is not None:
            kw["grid_spec"] = grid_spec
        else:
            if grid is not None:
                kw["grid"] = grid
            if in_specs is not None:
                kw["in_specs"] = in_specs
            if out_specs is not None:
                kw["out_specs"] = out_specs
            kw["scratch_shapes"] = list(scratch)
        return pl.pallas_call(functools.partial(body), name=name, out_shape=out_shape,
                              compiler_params=pltpu.CompilerParams(**cp), **kw)

    single = not isinstance(out_shape, (list, tuple))
    outs_list = [out_shape] if single else list(out_shape)
    ospecs = [out_specs] if single else list(out_specs)
    n_in, n_out, n_ci, n_co, n_scr = len(in_specs), len(outs_list), len(comm.arrays), len(comm.out_shapes), len(scratch)
    cp["dimension_semantics"] = ("arbitrary",) * len(grid)

    def hosted(*refs):
        cin, hin = refs[:n_in], refs[n_in:n_in + n_ci]
        cout = refs[n_in + n_ci:n_in + n_ci + n_out]
        hout = refs[n_in + n_ci + n_out:n_in + n_ci + n_out + n_co]
        scr = refs[n_in + n_ci + n_out + n_co:n_in + n_ci + n_out + n_co + n_scr]
        ssem, rsem = refs[-2], refs[-1]
        first = functools.reduce(lambda p, q: p & q, [pl.program_id(a) == 0 for a in range(len(grid))])
        last = functools.reduce(lambda p, q: p & q, [pl.program_id(a) == grid[a] - 1 for a in range(len(grid))])

        @pl.when(first)
        def _():
            comm.start(hin, hout, ssem, rsem, 0)

        body(*cin, *cout, *scr)

        @pl.when(last)
        def _():
            comm.finish(hin, hout, ssem, rsem, 0)

    call = pl.pallas_call(
        hosted, name=name, grid=grid, in_specs=list(in_specs) + [ANY] * n_ci, out_specs=ospecs + [ANY] * n_co,
        out_shape=outs_list + comm.out_shapes,
        scratch_shapes=list(scratch) + [pltpu.SemaphoreType.DMA((comm.nsem,)), pltpu.SemaphoreType.DMA((comm.nsem,))],
        input_output_aliases={**aliases, **{n_in + i: n_out + o for i, o in comm.aliases.items()}},
        compiler_params=pltpu.CompilerParams(**cp))

    def run(*args):
        res = call(*args, *comm.arrays)
        comp = res[:n_out]
        return (comp[0] if single else comp), list(res[n_out:])

    return run


def _tile(n, pref):
    return pref if n % pref == 0 else n


def _sigmoid(x):
    return 1.0 / (1.0 + jnp.exp(-x))


def _gelu(x):
    c = math.sqrt(2.0 / math.pi)
    return 0.5 * x * (1.0 + jnp.tanh(c * (x + 0.044715 * (x * x * x))))


def _gelu_and_grad(x):
    c = math.sqrt(2.0 / math.pi)
    x2 = x * x
    t = jnp.tanh(c * (x + 0.044715 * (x2 * x)))
    half = 0.5 * (1.0 + t)
    return x * half, half + (0.5 * x) * (1.0 - t * t) * (c + (3.0 * 0.044715 * c) * x2)


def _dot(a, b):
    return jnp.dot(a, b, preferred_element_type=F32)


def _dot_nt(a, b):
    return lax.dot_general(a, b, (((1,), (1,)), ((), ())), preferred_element_type=F32)


def _dot_tn(a, b):
    return lax.dot_general(a, b, (((0,), (0,)), ((), ())), preferred_element_type=F32)


def _tri_mask(n, reverse):
    r = lax.broadcasted_iota(jnp.int32, (n, n), 0)
    c = lax.broadcasted_iota(jnp.int32, (n, n), 1)
    same = (r // CHUNK) == (c // CHUNK)
    tri = (c >= r) if reverse else (c <= r)
    return jnp.where(same & tri, 1.0, 0.0).astype(BF16)


def _tri_apply(tri, x):
    hi = x.astype(BF16)
    r1 = x - hi.astype(F32)
    mid = r1.astype(BF16)
    lo = (r1 - mid.astype(F32)).astype(BF16)
    return _dot(tri, hi) + (_dot(tri, mid) + _dot(tri, lo))


def _all_gather(arrs, name):
    n = len(arrs)

    def body(*refs):
        ins, outs = refs[:n], refs[n:2 * n]
        send_sems, recv_sems, local_sems = refs[2 * n:]
        x, y, c = lax.axis_index("x"), lax.axis_index("y"), lax.axis_index("c")
        me, sibling = (x, y, c), (x, y, 1 - c)
        near = (x + c - 2 * x * c, y + (1 - c) - 2 * y * (1 - c))
        far = (x + (1 - c) - 2 * x * (1 - c), y + c - 2 * y * c)
        diag = (1 - x, 1 - y)

        def blk(a, p):
            return outs[a].at[4 * p[0] + 2 * p[1] + p[2]]

        def copy(a, k, block, to, src=None):
            return pltpu.make_async_remote_copy(
                src_ref=blk(a, block) if src is None else src, dst_ref=blk(a, block),
                send_sem=send_sems.at[7 * a + k], recv_sem=recv_sems.at[7 * a + k],
                device_id=to, device_id_type=MESH)

        mine = [pltpu.make_async_copy(ins[a], blk(a, me), local_sems.at[a]) for a in range(n)]
        for m in mine:
            m.start()
        sends = []
        for a in range(n):
            sends += [copy(a, 0, me, sibling, src=ins[a]), copy(a, 1, me, (*near, c), src=ins[a]),
                      copy(a, 2, me, (*far, c), src=ins[a])]
        for cp in sends:
            cp.start()
        for a in range(n):
            copy(a, 1, (*near, c), me).wait_recv()
            sends.append(copy(a, 3, (*near, c), (*far, c)))
            sends[-1].start()
        for a in range(n):
            sends.append(copy(a, 4, (*near, c), sibling))
            sends[-1].start()
            copy(a, 2, (*far, c), me).wait_recv()
            sends.append(copy(a, 5, (*far, c), sibling))
            sends[-1].start()
        for a in range(n):
            copy(a, 3, (*diag, c), me).wait_recv()
            sends.append(copy(a, 6, (*diag, c), sibling))
            sends[-1].start()
        for a in range(n):
            copy(a, 0, sibling, me).wait_recv()
            copy(a, 4, (*far, 1 - c), me).wait_recv()
            copy(a, 5, (*near, 1 - c), me).wait_recv()
            copy(a, 6, (*diag, 1 - c), me).wait_recv()
        for cp in sends:
            cp.wait_send()
        for m in mine:
            m.wait()

    out_shape = [jax.ShapeDtypeStruct((NDEV,) + a.shape, a.dtype) for a in arrs]
    return _pc(body, name=name, out_shape=out_shape, in_specs=[ANY] * n, out_specs=[ANY] * n,
               scratch=[pltpu.SemaphoreType.DMA((7 * n,)), pltpu.SemaphoreType.DMA((7 * n,)),
                        pltpu.SemaphoreType.DMA((n,))])(*arrs)


def _gather_first(arrs):
    n = len(arrs)

    def parts(ins, outs, ss, rs, base):
        x, y, c = lax.axis_index("x"), lax.axis_index("y"), lax.axis_index("c")
        me, sibling = (x, y, c), (x, y, 1 - c)
        chips = [(1 - x, y), (x, 1 - y), (1 - x, 1 - y)]

        def blk(a, p):
            return outs[a].at[4 * p[0] + 2 * p[1] + p[2]]

        def copy(a, k, block, to):
            return pltpu.make_async_remote_copy(
                src_ref=ins[a], dst_ref=blk(a, block), send_sem=ss.at[base + 4 * a + k],
                recv_sem=rs.at[base + 4 * a + k], device_id=to, device_id_type=MESH)

        local = [pltpu.make_async_copy(ins[a], blk(a, me), ss.at[base + 4 * n + a]) for a in range(n)]
        sends, recvs = [], []
        for a in range(n):
            sends.append(copy(a, 0, me, sibling))
            recvs.append(copy(a, 0, sibling, me))
            for j, chip in enumerate(chips):
                sends.append(copy(a, 1 + j, me, (*chip, c)))
                recvs.append(copy(a, 1 + j, (*chip, c), me))
        return local, sends, recvs

    def start(ins, outs, ss, rs, base):
        local, sends, _ = parts(ins, outs, ss, rs, base)
        for cp in local + sends:
            cp.start()

    def finish(ins, outs, ss, rs, base):
        local, sends, recvs = parts(ins, outs, ss, rs, base)
        for cp in recvs:
            cp.wait_recv()
        for cp in sends:
            cp.wait_send()
        for cp in local:
            cp.wait()

    return _Hosted(arrs, [jax.ShapeDtypeStruct((NDEV,) + a.shape, a.dtype) for a in arrs], 5 * n, start, finish)


def _gather_second(bufs):
    n = len(bufs)

    def parts(ins, outs, ss, rs, base):
        x, y, c = lax.axis_index("x"), lax.axis_index("y"), lax.axis_index("c")
        sibling = (x, y, 1 - c)
        chips = [(1 - x, y), (x, 1 - y), (1 - x, 1 - y)]
        sends, recvs = [], []
        for a in range(n):
            for j, chip in enumerate(chips):
                mine = 4 * chip[0] + 2 * chip[1] + c
                theirs = 4 * chip[0] + 2 * chip[1] + (1 - c)
                sends.append(pltpu.make_async_remote_copy(
                    src_ref=ins[a].at[mine], dst_ref=outs[a].at[mine], send_sem=ss.at[base + 3 * a + j],
                    recv_sem=rs.at[base + 3 * a + j], device_id=sibling, device_id_type=MESH))
                recvs.append(pltpu.make_async_remote_copy(
                    src_ref=ins[a].at[theirs], dst_ref=outs[a].at[theirs], send_sem=ss.at[base + 3 * a + j],
                    recv_sem=rs.at[base + 3 * a + j], device_id=sibling, device_id_type=MESH))
        return sends, recvs

    def start(ins, outs, ss, rs, base):
        for cp in parts(ins, outs, ss, rs, base)[0]:
            cp.start()

    def finish(ins, outs, ss, rs, base):
        sends, recvs = parts(ins, outs, ss, rs, base)
        for cp in recvs:
            cp.wait_recv()
        for cp in sends:
            cp.wait_send()

    return _Hosted(bufs, [jax.ShapeDtypeStruct(b.shape, b.dtype) for b in bufs], 3 * n, start, finish,
                   aliases={a: a for a in range(n)})


def _swap(src, nblk, ids_fn, partner_fn):
    def copies(ins, outs, ss, rs, base):
        x, y, c = lax.axis_index("x"), lax.axis_index("y"), lax.axis_index("c")
        ids = ids_fn(x, y, c)
        partner = partner_fn(x, y, c)
        return [pltpu.make_async_remote_copy(
            src_ref=ins[0].at[ids[k]], dst_ref=outs[0].at[k], send_sem=ss.at[base + k], recv_sem=rs.at[base + k],
            device_id=partner, device_id_type=MESH) for k in range(nblk)]

    def start(ins, outs, ss, rs, base):
        for cp in copies(ins, outs, ss, rs, base):
            cp.start()

    def finish(ins, outs, ss, rs, base):
        for cp in copies(ins, outs, ss, rs, base):
            cp.wait()

    return _Hosted([src], [jax.ShapeDtypeStruct((nblk,) + src.shape[1:], src.dtype)], nblk, start, finish)


def _blocking(comm, name):
    n_i, n_o = len(comm.arrays), len(comm.out_shapes)

    def body(*refs):
        ins, outs = refs[:n_i], refs[n_i:n_i + n_o]
        comm.start(ins, outs, refs[-2], refs[-1], 0)
        comm.finish(ins, outs, refs[-2], refs[-1], 0)

    return pl.pallas_call(
        body, name=name, out_shape=comm.out_shapes, in_specs=[ANY] * n_i, out_specs=[ANY] * n_o,
        scratch_shapes=[pltpu.SemaphoreType.DMA((comm.nsem,)), pltpu.SemaphoreType.DMA((comm.nsem,))],
        input_output_aliases=comm.aliases)(*comm.arrays)


def _swap_chips(send):
    def copies(ins, outs, ss, rs, base):
        x, y, c = lax.axis_index("x"), lax.axis_index("y"), lax.axis_index("c")
        chips = [(1 - x, y), (x, 1 - y), (1 - x, 1 - y)]
        return [pltpu.make_async_remote_copy(
            src_ref=ins[0].at[j], dst_ref=outs[0].at[j], send_sem=ss.at[base + j], recv_sem=rs.at[base + j],
            device_id=(*chip, c), device_id_type=MESH) for j, chip in enumerate(chips)]

    def start(ins, outs, ss, rs, base):
        for cp in copies(ins, outs, ss, rs, base):
            cp.start()

    def finish(ins, outs, ss, rs, base):
        for cp in copies(ins, outs, ss, rs, base):
            cp.wait()

    return _Hosted([send], [jax.ShapeDtypeStruct(send.shape, send.dtype)], 3, start, finish)


def _add_send(a, b, idx, ns, name):
    _, r, c = a.shape
    tr = _tile(r, 256)

    def body(idx_ref, a_ref, b_ref, send_ref):
        send_ref[...] = (a_ref[...] + b_ref[...]).astype(BF16)

    def sel(off):
        return pl.BlockSpec((None, tr, c), lambda k, i, s: (s[off + k], i, 0))

    gs = pltpu.PrefetchScalarGridSpec(num_scalar_prefetch=1, grid=(ns, r // tr), in_specs=[sel(0), sel(ns)],
                                      out_specs=pl.BlockSpec((None, tr, c), lambda k, i, s: (k, i, 0)))
    return _pc(body, name=name, grid_spec=gs, sem=("arbitrary", "arbitrary"),
               out_shape=jax.ShapeDtypeStruct((ns, r, c), BF16))(idx, a, b)


class _ReduceScatter:
    def __init__(self, g, tag):
        self.g, self.tag = g, tag

    def swap_core(self):
        return _swap(self.g, 4, lambda x, y, c: [1 - c, 3 - c, 5 - c, 7 - c], lambda x, y, c: (x, y, 1 - c))

    def after_core(self, recv):
        x, y, c = lax.axis_index("x"), lax.axis_index("y"), lax.axis_index("c")
        chips = [(1 - x, y), (x, 1 - y), (1 - x, 1 - y)]
        idx = jnp.stack([4 * p + 2 * q + c for p, q in chips] + [2 * p + q for p, q in chips]).astype(jnp.int32)
        self.send = _add_send(self.g, recv, idx, 3, "rs_add_" + self.tag)
        self.recv_core = recv
        zero = jnp.zeros((), jnp.int32)
        self.idx = jnp.stack([4 * x + 2 * y + c, 2 * x + y, zero, zero + 1, zero + 2]).astype(jnp.int32)

    def swap_chips(self):
        return _swap_chips(self.send)

    def after_chips(self, recv):
        self.parts = [self.g, self.recv_core, recv, recv, recv]


def _ada_fwd(c_all, w_ada, b_cols, b_lb):
    nl, d, ncol = w_ada.shape
    nseq = c_all.shape[0]
    di = b_lb.shape[1]

    def body(c_ref, w_ref, b_ref, lb_ref, mod_ref, lbj_ref):
        cv = c_ref[...]
        cact = (cv * _sigmoid(cv)).astype(BF16)
        for l in range(nl):
            mod_ref[l] = _dot(cact, w_ref[l].astype(BF16)) + b_ref[l]
        b0, b1 = lb_ref[0:1, :], lb_ref[1:2, :]
        mx = jnp.maximum(b0, b1)
        e0, e1 = jnp.exp(b0 - mx), jnp.exp(b1 - mx)
        s = e0 + e1
        p0, p1 = e0 / s, e1 / s
        lbj_ref[0:1, :] = (p0 + p1) - p0
        lbj_ref[1:2, :] = p0 * p1

    return _pc(body, name="ada_fwd",
               out_shape=[jax.ShapeDtypeStruct((nl, nseq, ncol), F32), jax.ShapeDtypeStruct((2, di), F32)]
               )(c_all, w_ada, b_cols, b_lb)


def _ada_bwd(c_all, dmod_cols, dmod_full):
    nl, nseq, ncol = dmod_cols.shape
    d = c_all.shape[1]
    d3 = dmod_full.shape[2]

    def body(c_ref, dc_ref, df_ref, gw_ref, gb_ref):
        cv = c_ref[...]
        cact = (cv * _sigmoid(cv)).astype(BF16)
        for l in range(nl):
            gw_ref[l] = _dot_tn(cact, dc_ref[l].astype(BF16))
            gb_ref[l:l + 1, :] = jnp.sum(df_ref[l], axis=0, keepdims=True)

    return _pc(body, name="ada_bwd",
               out_shape=[jax.ShapeDtypeStruct((nl, d, ncol), F32), jax.ShapeDtypeStruct((nl, d3), F32)]
               )(c_all, dmod_cols, dmod_full)


def _prenorm(x, gain, mod, t_seq, name, comm=None):
    m, d = x.shape
    tm = _tile(t_seq, 1024)
    per = t_seq // tm

    def body(x_ref, g_ref, mod_ref, h_ref, ht_ref):
        xv = x_ref[...]
        rstd = lax.rsqrt(jnp.mean(xv * xv, axis=-1, keepdims=True) + EPS)
        r = xv * rstd * g_ref[...]
        h = r * (1.0 + mod_ref[0, 1:2, :]) + mod_ref[0, 0:1, :]
        h_ref[...] = h.astype(BF16)
        ht_ref[...] = h.T.astype(BF16)

    return _pc(body, name=name, out_shape=[jax.ShapeDtypeStruct((m, d), BF16), jax.ShapeDtypeStruct((d, m), BF16)],
               grid=(m // tm,),
               in_specs=[pl.BlockSpec((tm, d), lambda i: (i, 0)), pl.BlockSpec((1, d), lambda i: (0, 0)),
                         pl.BlockSpec((1, 3, d), lambda i: (i // per, 0, 0))],
               out_specs=[pl.BlockSpec((tm, d), lambda i: (i, 0)), pl.BlockSpec((d, tm), lambda i: (0, i))],
               sem=("parallel",), comm=comm)(x, gain, mod)


def _prenorm_bwd(dh, x, gain, mod, dxn, t_seq, name, comm=None):
    m, d = x.shape
    nb = m // t_seq
    tm = _tile(t_seq, 1024)
    per = t_seq // tm

    def body(dh_ref, x_ref, g_ref, mod_ref, dxn_ref, dx_ref, dss_ref, dg_ref):
        i = pl.program_id(0)
        xv, dhv, g = x_ref[...], dh_ref[...], g_ref[...]
        rstd = lax.rsqrt(jnp.mean(xv * xv, axis=-1, keepdims=True) + EPS)
        xhat = xv * rstd
        dr = dhv * (1.0 + mod_ref[0, 1:2, :])
        dxhat = dr * g
        dx_ref[...] = dxn_ref[...] + rstd * (dxhat - xhat * jnp.mean(dxhat * xhat, axis=-1, keepdims=True))

        @pl.when(i % per == 0)
        def _():
            dss_ref[...] = jnp.zeros_like(dss_ref)

        @pl.when(i == 0)
        def _():
            dg_ref[...] = jnp.zeros_like(dg_ref)

        dss_ref[0, 0:1, :] += jnp.sum(dhv, axis=0, keepdims=True)
        dss_ref[0, 1:2, :] += jnp.sum(dhv * (xhat * g), axis=0, keepdims=True)
        dg_ref[...] += jnp.sum(dr * xhat, axis=0, keepdims=True)

    row = pl.BlockSpec((tm, d), lambda i: (i, 0))
    return _pc(body, name=name,
               out_shape=[jax.ShapeDtypeStruct((m, d), F32), jax.ShapeDtypeStruct((nb, 2, d), F32),
                          jax.ShapeDtypeStruct((1, d), F32)],
               grid=(m // tm,),
               in_specs=[row, row, pl.BlockSpec((1, d), lambda i: (0, 0)),
                         pl.BlockSpec((1, 3, d), lambda i: (i // per, 0, 0)), row],
               out_specs=[row, pl.BlockSpec((1, 2, d), lambda i: (i // per, 0, 0)),
                          pl.BlockSpec((1, d), lambda i: (0, 0))],
               sem=("arbitrary",), comm=comm)(dh, x, gain, mod, dxn)


def _mm_in(h, ws, sections, name, comm=None):
    m, k = h.shape
    nw = len(ws)
    widths = [w.shape[2] for w in ws]
    offs = [sum(widths[:a]) for a in range(nw)]
    nc = sum(widths)
    per = NDEV // sections if sections > 1 else NDEV
    tm = _din_tile(m)
    assert per % 2 == 0

    def body(*refs):
        hv = refs[0][...]
        o_ref = refs[1 + nw]
        for b in range(2):
            for a in range(nw):
                lo = b * nc + offs[a]
                o_ref[:, lo:lo + widths[a]] = _dot(hv, refs[1 + a][b])

    w_specs = [pl.BlockSpec((2, k, wd), lambda j, i: (j, 0, 0)) for wd in widths]
    if sections > 1:
        out_shape = jax.ShapeDtypeStruct((sections, m, per * nc), F32)
        out_spec = pl.BlockSpec((None, tm, 2 * nc), lambda j, i: ((2 * j) // per, i, ((2 * j) % per) // 2))
    else:
        out_shape = jax.ShapeDtypeStruct((m, NDEV * nc), F32)
        out_spec = pl.BlockSpec((tm, 2 * nc), lambda j, i: (i, j))
    return _pc(body, name=name, out_shape=out_shape, grid=(NDEV // 2, m // tm),
               in_specs=[pl.BlockSpec((tm, k), lambda j, i: (i, 0))] + w_specs,
               out_specs=out_spec, sem=("parallel", "parallel"), comm=comm)(h, *ws)


def _din_tile(m):
    return 1024 if m % 1024 == 0 and m >= 2048 else _tile(m, 512)


def _mm_din(dproj, ws, sections, name, comm=None, tiles=None, prev=None):
    nw, k = len(ws), ws[0].shape[1]
    widths = [w.shape[2] for w in ws]
    offs = [sum(widths[:a]) for a in range(nw)]
    nc = sum(widths)
    m = dproj.shape[-2]
    tm = _din_tile(m)
    t0, nt = tiles if tiles is not None else (0, m // tm)
    per = NDEV // sections if sections > 1 else NDEV
    assert per % 2 == 0

    def body(*refs):
        d_ref, o_ref = refs[0], refs[-1]
        j = pl.program_id(1)
        acc = None
        for b in range(2):
            for a in range(nw):
                lo = b * nc + offs[a]
                term = _dot_nt(d_ref[:, lo:lo + widths[a]], refs[1 + a][b])
                acc = term if acc is None else acc + term

        @pl.when(j == 0)
        def _():
            o_ref[...] = acc

        @pl.when(j > 0)
        def _():
            o_ref[...] += acc

    if sections > 1:
        dspec = pl.BlockSpec((None, tm, 2 * nc), lambda i, j: ((2 * j) // per, i + t0, ((2 * j) % per) // 2))
    else:
        dspec = pl.BlockSpec((tm, 2 * nc), lambda i, j: (i + t0, j))
    in_specs = [dspec] + [pl.BlockSpec((2, k, wd), lambda i, j: (j, 0, 0)) for wd in widths]
    args = [dproj, *ws]
    if prev is not None:
        in_specs.append(ANY)
        args.append(prev)
    return _pc(body, name=name, out_shape=jax.ShapeDtypeStruct((m, k), F32), grid=(nt, NDEV // 2), in_specs=in_specs,
               out_specs=pl.BlockSpec((tm, k), lambda i, j: (i + t0, 0)), sem=("parallel", "arbitrary"),
               comm=comm, aliases={1 + nw: 0} if prev is not None else None)(*args)


def _mm_dw_in(ht, dproj, nc, sections, name, comm=None):
    k, m = ht.shape
    tk = 2048 if m % 2048 == 0 else _din_tile(m)
    per = NDEV // sections if sections > 1 else NDEV

    def body(h_ref, d_ref, o_ref):
        kk = pl.program_id(1)
        acc = _dot(h_ref[...], d_ref[...])

        @pl.when(kk == 0)
        def _():
            o_ref[...] = acc

        @pl.when(kk > 0)
        def _():
            o_ref[...] += acc

    if sections > 1:
        dspec = pl.BlockSpec((None, tk, nc), lambda j, i: (j // per, i, j % per))
    else:
        dspec = pl.BlockSpec((tk, nc), lambda j, i: (i, j))
    return _pc(body, name=name, out_shape=jax.ShapeDtypeStruct((NDEV, k, nc), F32), grid=(NDEV, m // tk),
               in_specs=[pl.BlockSpec((k, tk), lambda j, i: (0, i)), dspec],
               out_specs=pl.BlockSpec((None, k, nc), lambda j, i: (j, 0, 0)),
               sem=("parallel", "arbitrary"), comm=comm)(ht, dproj)


def _out_proj(ybr, w_out, x, mod, t_seq, name, comm=None):
    m, di = ybr.shape
    d = w_out.shape[1]
    tm = _tile(t_seq, 512)
    per = t_seq // tm

    def body(y_ref, w_ref, x_ref, mod_ref, yo_ref, xn_ref):
        yo = _dot(y_ref[...], w_ref[...])
        yo_ref[...] = yo
        xn_ref[...] = x_ref[...] + mod_ref[0, 2:3, :] * yo

    row = pl.BlockSpec((tm, d), lambda i: (i, 0))
    return _pc(body, name=name,
               out_shape=[jax.ShapeDtypeStruct((m, d), F32), jax.ShapeDtypeStruct((m, d), F32)],
               grid=(m // tm,),
               in_specs=[pl.BlockSpec((tm, di), lambda i: (i, 0)), pl.BlockSpec((di, d), lambda i: (0, 0)), row,
                         pl.BlockSpec((1, 3, d), lambda i: (i // per, 0, 0))],
               out_specs=[row, row], sem=("parallel",), comm=comm)(ybr, w_out, x, mod)


def _out_proj_loss(ybr, w_out, x, mod, gain, target, t_seq):
    m, di = ybr.shape
    d = w_out.shape[1]
    tm = _tile(t_seq, 512)
    per = t_seq // tm

    def body(y_ref, w_ref, x_ref, mod_ref, g_ref, t_ref, yo_ref, dx_ref, loss_ref, dg_ref):
        i = pl.program_id(0)
        yo = _dot(y_ref[...], w_ref[...])
        yo_ref[...] = yo
        xv = x_ref[...] + mod_ref[0, 2:3, :] * yo
        g = g_ref[...]
        rstd = lax.rsqrt(jnp.mean(xv * xv, axis=-1, keepdims=True) + EPS)
        xhat = xv * rstd
        err = xhat * g - t_ref[...]
        dy = err * (1.0 / d)
        dxhat = dy * g
        dx_ref[...] = rstd * (dxhat - xhat * jnp.mean(dxhat * xhat, axis=-1, keepdims=True))

        @pl.when(i == 0)
        def _():
            loss_ref[...] = jnp.zeros_like(loss_ref)
            dg_ref[...] = jnp.zeros_like(dg_ref)

        loss_ref[...] += 0.5 * jnp.sum(jnp.mean(err * err, axis=-1, keepdims=True), axis=0, keepdims=True)
        dg_ref[...] += jnp.sum(dy * xhat, axis=0, keepdims=True)

    row = pl.BlockSpec((tm, d), lambda i: (i, 0))
    vec = pl.BlockSpec((1, d), lambda i: (0, 0))
    return _pc(body, name="out_proj_loss",
               out_shape=[jax.ShapeDtypeStruct((m, d), F32), jax.ShapeDtypeStruct((m, d), F32),
                          jax.ShapeDtypeStruct((1, 1), F32), jax.ShapeDtypeStruct((1, d), F32)],
               grid=(m // tm,),
               in_specs=[pl.BlockSpec((tm, di), lambda i: (i, 0)), pl.BlockSpec((di, d), lambda i: (0, 0)), row,
                         pl.BlockSpec((1, 3, d), lambda i: (i // per, 0, 0)), vec, row],
               out_specs=[row, row, pl.BlockSpec((1, 1), lambda i: (0, 0)), vec],
               sem=("arbitrary",))(ybr, w_out, x, mod, gain, target)


def _gate_dybr(dxn, yout, mod, w_out, t_seq, name):
    m, d = dxn.shape
    di = w_out.shape[0]
    nb = m // t_seq
    tm = _tile(t_seq, 512)
    per = t_seq // tm

    def body(dxn_ref, yo_ref, mod_ref, w_ref, dy_ref, dgate_ref, o_ref):
        i = pl.program_id(0)
        dv = dxn_ref[...]
        dy = (mod_ref[0, 2:3, :] * dv).astype(BF16)
        dy_ref[...] = dy
        o_ref[...] = _dot_nt(dy, w_ref[...])

        @pl.when(i % per == 0)
        def _():
            dgate_ref[...] = jnp.zeros_like(dgate_ref)

        dgate_ref[0] += jnp.sum(dv * yo_ref[...], axis=0, keepdims=True)

    row = pl.BlockSpec((tm, d), lambda i: (i, 0))
    return _pc(body, name=name,
               out_shape=[jax.ShapeDtypeStruct((m, d), BF16), jax.ShapeDtypeStruct((nb, 1, d), F32),
                          jax.ShapeDtypeStruct((m, di), F32)],
               grid=(m // tm,),
               in_specs=[row, row, pl.BlockSpec((1, 3, d), lambda i: (i // per, 0, 0)),
                         pl.BlockSpec((di, d), lambda i: (0, 0))],
               out_specs=[row, pl.BlockSpec((1, 1, d), lambda i: (i // per, 0, 0)),
                          pl.BlockSpec((tm, di), lambda i: (i, 0))],
               sem=("arbitrary",))(dxn, yout, mod, w_out)


def _mm_dw_out(ybr, dy, name, comm=None):
    m, di = ybr.shape
    d = dy.shape[1]
    tk = 2048 if m % 2048 == 0 else _tile(m, 512)
    tn = _tile(di, 1024)

    def body(y_ref, dy_ref, o_ref):
        kk = pl.program_id(1)
        acc = _dot_tn(y_ref[...], dy_ref[...])

        @pl.when(kk == 0)
        def _():
            o_ref[...] = acc

        @pl.when(kk > 0)
        def _():
            o_ref[...] += acc

    return _pc(body, name=name, out_shape=jax.ShapeDtypeStruct((di, d), F32), grid=(di // tn, m // tk),
               in_specs=[pl.BlockSpec((tk, tn), lambda n, k: (k, n)), pl.BlockSpec((tk, d), lambda n, k: (k, 0))],
               out_specs=pl.BlockSpec((tn, d), lambda n, k: (n, 0)), sem=("parallel", "arbitrary"),
               comm=comm)(ybr, dy)


def _sgu_mask():
    t = lax.broadcasted_iota(jnp.int32, (SG_BLOCK, SG_BLOCK), 0)
    s = lax.broadcasted_iota(jnp.int32, (SG_BLOCK, SG_BLOCK), 1)
    return (s // CHUNK) <= (t // CHUNK)


def _a_mid_fwd(proj, ln_g, ln_b, w_s, bs_t, t_seq, comm=None):
    m, n3 = proj.shape
    di = n3 // 3
    gd = di // SG_GROUPS
    r = _tile(t_seq, 256)
    nblk = r // SG_BLOCK

    def body(p_ref, lg_ref, lb_ref, ws_ref, bs_ref, ybr_ref, s_scr):
        v = _gelu(p_ref[:, di:2 * di])
        mu = jnp.mean(v, axis=-1, keepdims=True)
        vc = v - mu
        rstd = lax.rsqrt(jnp.mean(vc * vc, axis=-1, keepdims=True) + EPS)
        vb = (vc * rstd * lg_ref[...] + lb_ref[...]).astype(BF16)
        mask = _sgu_mask()
        for gi in range(SG_GROUPS):
            ws = jnp.where(mask, ws_ref[gi], 0.0).astype(BF16)
            bcol = bs_ref[:, gi:gi + 1]
            for b in range(nblk):
                rows = slice(b * SG_BLOCK, (b + 1) * SG_BLOCK)
                cols = slice(gi * gd, (gi + 1) * gd)
                s_scr[rows, cols] = _dot(ws, vb[rows, cols]) + bcol
        gg = p_ref[:, 2 * di:]
        ybr_ref[...] = (_gelu(p_ref[:, :di]) * s_scr[...] * (gg * _sigmoid(gg))).astype(BF16)

    vec = pl.BlockSpec((1, di), lambda i: (0, 0))
    return _pc(body, name="a_mid_fwd", out_shape=jax.ShapeDtypeStruct((m, di), BF16), grid=(m // r,),
               in_specs=[pl.BlockSpec((r, n3), lambda i: (i, 0)), vec, vec,
                         pl.BlockSpec((SG_GROUPS, SG_BLOCK, SG_BLOCK), lambda i: (0, 0, 0)),
                         pl.BlockSpec((SG_BLOCK, 128), lambda i: (0, 0))],
               out_specs=pl.BlockSpec((r, di), lambda i: (i, 0)),
               scratch=[pltpu.VMEM((r, di), F32)], sem=("parallel",), comm=comm)(proj, ln_g, ln_b, w_s, bs_t)


def _a_mid_bwd(proj, dybr, ln_g, ln_b, w_s, bs_t, t_seq, comm=None):
    m, n3 = proj.shape
    di = n3 // 3
    gd = di // SG_GROUPS
    r = _tile(t_seq, 256)
    nblk = r // SG_BLOCK

    def body(p_ref, dy_ref, lg_ref, lb_ref, ws_ref, bs_ref,
             dp_ref, dlg_ref, dlb_ref, dws_ref, dbs_ref, s_scr, dvl_scr):
        i = pl.program_id(0)

        @pl.when(i == 0)
        def _():
            dlg_ref[...] = jnp.zeros_like(dlg_ref)
            dlb_ref[...] = jnp.zeros_like(dlb_ref)
            dws_ref[...] = jnp.zeros_like(dws_ref)
            dbs_ref[...] = jnp.zeros_like(dbs_ref)

        v, dgelu_v = _gelu_and_grad(p_ref[:, di:2 * di])
        mu = jnp.mean(v, axis=-1, keepdims=True)
        vc = v - mu
        rstd = lax.rsqrt(jnp.mean(vc * vc, axis=-1, keepdims=True) + EPS)
        vhat = vc * rstd
        lg = lg_ref[...]
        vb = (vhat * lg + lb_ref[...]).astype(BF16)
        u, dgelu_u = _gelu_and_grad(p_ref[:, :di])
        gg = p_ref[:, 2 * di:]
        sg = _sigmoid(gg)
        dyv = dy_ref[...]
        dus = dyv * (gg * sg)
        dsb = (dus * u).astype(BF16)
        ds32 = dus * u
        mask = _sgu_mask()
        lane = lax.broadcasted_iota(jnp.int32, (SG_BLOCK, 128), 1)
        dbs_acc = jnp.zeros((SG_BLOCK, 128), F32)
        for gi in range(SG_GROUPS):
            ws = jnp.where(mask, ws_ref[gi], 0.0).astype(BF16)
            bcol = bs_ref[:, gi:gi + 1]
            cols = slice(gi * gd, (gi + 1) * gd)
            dws_acc = jnp.zeros((SG_BLOCK, SG_BLOCK), F32)
            dbs_col = jnp.zeros((SG_BLOCK, 1), F32)
            for b in range(nblk):
                rows = slice(b * SG_BLOCK, (b + 1) * SG_BLOCK)
                s_scr[rows, cols] = _dot(ws, vb[rows, cols]) + bcol
                dvl_scr[rows, cols] = _dot_tn(ws, dsb[rows, cols])
                dws_acc += _dot_nt(dsb[rows, cols], vb[rows, cols])
                dbs_col += jnp.sum(ds32[rows, cols], axis=-1, keepdims=True)
            dws_ref[gi] += jnp.where(mask, dws_acc, 0.0)
            dbs_acc += jnp.where(lane == gi, dbs_col, 0.0)
        dbs_ref[...] += dbs_acc
        s = s_scr[...]
        dp_ref[:, :di] = (dus * s * dgelu_u).astype(BF16)
        dp_ref[:, 2 * di:] = (dyv * u * s * (sg * (1.0 + gg * (1.0 - sg)))).astype(BF16)
        dvl = dvl_scr[...]
        dlg_ref[...] += jnp.sum(dvl * vhat, axis=0, keepdims=True)
        dlb_ref[...] += jnp.sum(dvl, axis=0, keepdims=True)
        dvh = dvl * lg
        dv = rstd * (dvh - jnp.mean(dvh, axis=-1, keepdims=True)
                     - vhat * jnp.mean(dvh * vhat, axis=-1, keepdims=True))
        dp_ref[:, di:2 * di] = (dv * dgelu_v).astype(BF16)

    vec = pl.BlockSpec((1, di), lambda i: (0, 0))
    wsb = pl.BlockSpec((SG_GROUPS, SG_BLOCK, SG_BLOCK), lambda i: (0, 0, 0))
    bsb = pl.BlockSpec((SG_BLOCK, 128), lambda i: (0, 0))
    return _pc(body, name="a_mid_bwd",
               out_shape=[jax.ShapeDtypeStruct((m, n3), BF16), jax.ShapeDtypeStruct((1, di), F32),
                          jax.ShapeDtypeStruct((1, di), F32),
                          jax.ShapeDtypeStruct((SG_GROUPS, SG_BLOCK, SG_BLOCK), F32),
                          jax.ShapeDtypeStruct((SG_BLOCK, 128), F32)],
               grid=(m // r,),
               in_specs=[pl.BlockSpec((r, n3), lambda i: (i, 0)), pl.BlockSpec((r, di), lambda i: (i, 0)),
                         vec, vec, wsb, bsb],
               out_specs=[pl.BlockSpec((r, n3), lambda i: (i, 0)), vec, vec, wsb, bsb],
               scratch=[pltpu.VMEM((r, di), F32), pltpu.VMEM((r, di), F32)],
               sem=("arbitrary",), comm=comm)(proj, dybr, ln_g, ln_b, w_s, bs_t)


def _chunk_rows(n):
    if isinstance(n, int):
        return pl.ds(n * CHUNK, CHUNK)
    return pl.ds(pl.multiple_of(n * CHUNK, CHUNK), CHUNK)


def _hgrn_dims(t_seq, di):
    tr = _tile(t_seq, 128)
    hc = _tile(di, 2048)
    return tr, hc, hc // HEAD_DIM


def _hgrn_gates(f_ref, lb, a_scr, k_scr, tr):
    sig = _sigmoid(f_ref[...])
    fg = lb + (1.0 - lb) * sig
    k_scr[...] = 1.0 - fg
    logf = jnp.log(fg)
    g = min(CUM_ROWS, tr)
    tri = _tri_mask(g, reverse=False)
    for rg in range(tr // g):
        a_scr[rg * g:(rg + 1) * g, :] = _tri_apply(tri, logf[rg * g:(rg + 1) * g, :])
    return sig, fg


def _hgrn_fwd(proj, lbj, gn, nb, t_seq, comm=None):
    _, m, di = proj.shape
    tr, hc, hpg = _hgrn_dims(t_seq, di)
    nt, nhg, ncl = t_seq // tr, di // hc, tr // CHUNK
    nheads = di // HEAD_DIM

    def body(p_ref, lb_ref, gn_ref, o_ref, ybr_ref, st_ref, st_scr, a_scr, k_scr):
        q_ref, f_ref, i_ref, g_ref = (p_ref.at[s] for s in range(4))
        t = pl.program_id(2)

        @pl.when(t == 0)
        def _():
            st_scr[...] = jnp.zeros_like(st_scr)

        _hgrn_gates(f_ref, lb_ref[0:1, :], a_scr, k_scr, tr)
        gnv = gn_ref[...]
        rr = lax.broadcasted_iota(jnp.int32, (CHUNK, CHUNK), 0)
        cc = lax.broadcasted_iota(jnp.int32, (CHUNK, CHUNK), 1)
        causal = cc <= rr

        def chunk(n, carry):
            rows = _chunk_rows(n)
            lanes = [slice(hd * HEAD_DIM, (hd + 1) * HEAD_DIM) for hd in range(hpg)]
            hs = []
            for hd, ls in enumerate(lanes):
                h = {}
                ah, kh = a_scr[rows, ls], k_scr[rows, ls]
                qp = q_ref[rows, ls]
                qh = qp * _sigmoid(qp)
                h["vb"] = i_ref[rows, ls].astype(BF16)
                aref, alast = ah[CHUNK // 2 - 1:CHUNK // 2, :], ah[CHUNK - 1:CHUNK, :]
                h["q_in"] = (qh * jnp.exp(ah - aref)).astype(BF16)
                h["k_in"] = (kh * jnp.exp(aref - ah)).astype(BF16)
                h["q_out"] = (qh * jnp.exp(ah)).astype(BF16)
                h["k_out"] = (kh * jnp.exp(alast - ah)).astype(BF16)
                h["dec"] = jnp.exp(alast)
                st = st_scr[hd]
                st_ref[n, hd] = st
                h["st"] = st
                hs.append(h)
            for h in hs:
                h["scores"] = _dot_nt(h["q_in"], h["k_in"])
                h["o_inter"] = _dot_nt(h["q_out"], h["st"].astype(BF16))
                h["st_mm"] = _dot_tn(h["vb"], h["k_out"])
            for h in hs:
                h["o"] = _dot(jnp.where(causal, h["scores"], 0.0).astype(BF16), h["vb"]) + h["o_inter"]
            for hd, (h, ls) in enumerate(zip(hs, lanes)):
                st_scr[hd] = h["st"] * h["dec"] + h["st_mm"]
                o = h["o"]
                o_ref[rows, ls] = o
                rstd = lax.rsqrt(jnp.mean(o * o, axis=-1, keepdims=True) + EPS)
                gg = g_ref[rows, ls]
                ybr_ref[rows, ls] = ((o * rstd * gnv) * (gg * _sigmoid(gg))).astype(BF16)
            return carry

        lax.fori_loop(0, ncl, chunk, 0)

    blk = pl.BlockSpec((tr, hc), lambda hg, b, t: (b * nt + t, hg))
    return _pc(body, name="hgrn_fwd",
               out_shape=[jax.ShapeDtypeStruct((m, di), F32), jax.ShapeDtypeStruct((m, di), BF16),
                          jax.ShapeDtypeStruct((m // CHUNK, nheads, HEAD_DIM, HEAD_DIM), F32)],
               grid=(nhg, nb, nt),
               in_specs=[pl.BlockSpec((4, tr, hc), lambda hg, b, t: (0, b * nt + t, hg)),
                         pl.BlockSpec((2, hc), lambda hg, b, t: (0, hg)),
                         pl.BlockSpec((1, HEAD_DIM), lambda hg, b, t: (0, 0))],
               out_specs=[blk, blk, pl.BlockSpec((ncl, hpg, HEAD_DIM, HEAD_DIM),
                                                 lambda hg, b, t: (b * nt + t, hg, 0, 0))],
               scratch=[pltpu.VMEM((hpg, HEAD_DIM, HEAD_DIM), F32), pltpu.VMEM((tr, hc), F32),
                        pltpu.VMEM((tr, hc), F32)],
               sem=("parallel", "arbitrary", "arbitrary"), comm=comm)(proj, lbj, gn)


def _hgrn_bwd(proj, o_all, dybr, states, lbj, gn, nb, t_seq, comm=None):
    _, m, di = proj.shape
    tr, hc, hpg = _hgrn_dims(t_seq, di)
    nt, nhg, ncl = t_seq // tr, di // hc, tr // CHUNK

    def body(p_ref, o_ref, dy_ref, st_ref, lb_ref, gn_ref,
             dp_ref, dlb_ref, dgn_ref, dst_scr, a_scr, k_scr, da_scr, dk_scr):
        q_ref, f_ref, i_ref, g_ref = (p_ref.at[s] for s in range(4))
        hg, b, t = pl.program_id(0), pl.program_id(1), pl.program_id(2)

        @pl.when(t == 0)
        def _():
            dst_scr[...] = jnp.zeros_like(dst_scr)

        @pl.when((b == 0) & (t == 0))
        def _():
            dlb_ref[...] = jnp.zeros_like(dlb_ref)

        @pl.when((hg == 0) & (b == 0) & (t == 0))
        def _():
            dgn_ref[...] = jnp.zeros_like(dgn_ref)

        lb = lb_ref[0:1, :]
        sig, fg = _hgrn_gates(f_ref, lb, a_scr, k_scr, tr)
        gnv = gn_ref[...]
        rr = lax.broadcasted_iota(jnp.int32, (CHUNK, CHUNK), 0)
        cc = lax.broadcasted_iota(jnp.int32, (CHUNK, CHUNK), 1)
        causal = cc <= rr
        rowi = lax.broadcasted_iota(jnp.int32, (CHUNK, HEAD_DIM), 0)

        def chunk(it, carry):
            n = ncl - 1 - it
            rows = _chunk_rows(n)
            lanes = [slice(hd * HEAD_DIM, (hd + 1) * HEAD_DIM) for hd in range(hpg)]
            hs = []
            for hd, ls in enumerate(lanes):
                h = {}
                ah, kh = a_scr[rows, ls], k_scr[rows, ls]
                qp = q_ref[rows, ls]
                sq = _sigmoid(qp)
                qh = qp * sq
                h["dsilu_q"] = sq * (1.0 + qp * (1.0 - sq))
                h["vb"] = i_ref[rows, ls].astype(BF16)
                aref, alast = ah[CHUNK // 2 - 1:CHUNK // 2, :], ah[CHUNK - 1:CHUNK, :]
                h["e1"], h["e2"] = jnp.exp(ah - aref), jnp.exp(aref - ah)
                h["e3"], h["e4"] = jnp.exp(ah), jnp.exp(alast - ah)
                h["dec"] = jnp.exp(alast)
                h["q_in"], h["k_in"], h["q_out"], h["k_out"] = qh * h["e1"], kh * h["e2"], qh * h["e3"], kh * h["e4"]
                for nm in ("q_in", "k_in", "q_out", "k_out"):
                    h[nm + "_b"] = h[nm].astype(BF16)
                o = o_ref[rows, ls]
                rstd = lax.rsqrt(jnp.mean(o * o, axis=-1, keepdims=True) + EPS)
                ohat = o * rstd
                gg = g_ref[rows, ls]
                sg = _sigmoid(gg)
                dyv = dy_ref[rows, ls]
                d_on = dyv * (gg * sg)
                dp_ref[3, rows, ls] = (dyv * (ohat * gnv) * (sg * (1.0 + gg * (1.0 - sg)))).astype(BF16)
                h["dgn"] = jnp.sum(d_on * ohat, axis=0, keepdims=True)
                dohat = d_on * gnv
                do = rstd * (dohat - ohat * jnp.mean(dohat * ohat, axis=-1, keepdims=True))
                h["do_b"] = do.astype(BF16)
                h["st_prev"] = st_ref[n, hd]
                h["dst"] = dst_scr[hd]
                hs.append(h)
            for h in hs:
                dst_b = h["dst"].astype(BF16)
                h["scores"] = _dot_nt(h["q_in_b"], h["k_in_b"])
                h["dscores"] = _dot_nt(h["do_b"], h["vb"])
                h["dv_inter"] = _dot_nt(h["k_out_b"], dst_b)
                h["dq_out"] = _dot(h["do_b"], h["st_prev"].astype(BF16))
                h["dk_out"] = _dot(h["vb"], dst_b)
                h["dst_mm"] = _dot_tn(h["do_b"], h["q_out_b"])
            for h in hs:
                scores = jnp.where(causal, h["scores"], 0.0).astype(BF16)
                dscores = jnp.where(causal, h["dscores"], 0.0).astype(BF16)
                h["dv"] = _dot_tn(scores, h["do_b"]) + h["dv_inter"]
                h["dq_in"] = _dot(dscores, h["k_in_b"])
                h["dk_in"] = _dot_tn(dscores, h["q_in_b"])
            dgn = hs[0]["dgn"]
            for h in hs[1:]:
                dgn = dgn + h["dgn"]
            dgn_ref[...] += dgn
            for hd, (h, ls) in enumerate(zip(hs, lanes)):
                ddec = jnp.sum(h["dst"] * h["st_prev"], axis=0, keepdims=True)
                dst_scr[hd] = h["dst"] * h["dec"] + h["dst_mm"]
                dp_ref[2, rows, ls] = h["dv"].astype(BF16)
                dq = h["dq_in"] * h["e1"] + h["dq_out"] * h["e3"]
                dp_ref[0, rows, ls] = (dq * h["dsilu_q"]).astype(BF16)
                dk_scr[rows, ls] = h["dk_in"] * h["e2"] + h["dk_out"] * h["e4"]
                t_in = h["dq_in"] * h["q_in"] - h["dk_in"] * h["k_in"]
                t_out = h["dk_out"] * h["k_out"]
                da = t_in + h["dq_out"] * h["q_out"] - t_out
                da_ref_row = -jnp.sum(t_in, axis=0, keepdims=True)
                da_last_row = jnp.sum(t_out, axis=0, keepdims=True) + ddec * h["dec"]
                da = da + jnp.where(rowi == CHUNK // 2 - 1, da_ref_row, 0.0) \
                        + jnp.where(rowi == CHUNK - 1, da_last_row, 0.0)
                da_scr[rows, ls] = da
            return carry

        if ncl <= 2:
            for it in range(ncl):
                chunk(it, 0)
        else:
            lax.fori_loop(0, ncl, chunk, 0)
        g = min(CUM_ROWS, tr)
        tri = _tri_mask(g, reverse=True)
        for rg in range(tr // g):
            rs = slice(rg * g, (rg + 1) * g)
            dlogf = _tri_apply(tri, da_scr[rs, :])
            df = dlogf / fg[rs, :] - dk_scr[rs, :]
            sgr = sig[rs, :]
            dp_ref[1, rs, :] = (df * (1.0 - lb) * (sgr * (1.0 - sgr))).astype(BF16)
            dlb_ref[...] += jnp.sum(df * (1.0 - sgr), axis=0, keepdims=True) * lb_ref[1:2, :]

    blk = pl.BlockSpec((tr, hc), lambda hg, b, t: (b * nt + (nt - 1 - t), hg))
    return _pc(body, name="hgrn_bwd",
               out_shape=[jax.ShapeDtypeStruct((4, m, di), BF16), jax.ShapeDtypeStruct((1, di), F32),
                          jax.ShapeDtypeStruct((1, HEAD_DIM), F32)],
               grid=(nhg, nb, nt),
               in_specs=[pl.BlockSpec((4, tr, hc), lambda hg, b, t: (0, b * nt + (nt - 1 - t), hg)), blk, blk,
                         pl.BlockSpec((ncl, hpg, HEAD_DIM, HEAD_DIM),
                                      lambda hg, b, t: (b * nt + (nt - 1 - t), hg, 0, 0)),
                         pl.BlockSpec((2, hc), lambda hg, b, t: (0, hg)),
                         pl.BlockSpec((1, HEAD_DIM), lambda hg, b, t: (0, 0))],
               out_specs=[pl.BlockSpec((4, tr, hc), lambda hg, b, t: (0, b * nt + (nt - 1 - t), hg)),
                          pl.BlockSpec((1, hc), lambda hg, b, t: (0, hg)),
                          pl.BlockSpec((1, HEAD_DIM), lambda hg, b, t: (0, 0))],
               scratch=[pltpu.VMEM((hpg, HEAD_DIM, HEAD_DIM), F32)] + [pltpu.VMEM((tr, hc), F32)] * 4,
               sem=("arbitrary", "arbitrary", "arbitrary"), comm=comm)(
                   proj, o_all, dybr, states, lbj, gn)


def _adamw(parts, w, m, v, name, comm=None):
    r, c = w.shape
    tr = _tile(r, 256)
    npart = len(parts)
    c1 = 1.0 - ADAM_B1 ** ADAM_STEP
    c2 = 1.0 - ADAM_B2 ** ADAM_STEP

    def body(*refs):
        p_refs = refs[:npart]
        _adamw_math(p_refs, *refs[npart:], c1, c2)

    blk = pl.BlockSpec((tr, c), lambda i: (i, 0))
    return _pc(body, name=name, out_shape=[jax.ShapeDtypeStruct((r, c), F32)] * 4, grid=(r // tr,),
               in_specs=[blk] * (npart + 3), out_specs=[blk] * 4, sem=("parallel",), comm=comm)(*parts, w, m, v)


def _adamw_math(p_refs, w_ref, m_ref, v_ref, g_ref, d_ref, nm_ref, nv_ref, c1, c2):
    g = p_refs[0][...].astype(F32)
    for p in p_refs[1:]:
        g = g + p[...].astype(F32)
    nm = ADAM_B1 * m_ref[...] + (1.0 - ADAM_B1) * g
    nv = ADAM_B2 * v_ref[...] + (1.0 - ADAM_B2) * (g * g)
    g_ref[...] = g
    nm_ref[...] = nm
    nv_ref[...] = nv
    d_ref[...] = -ADAM_LR * ((nm / c1) / (jnp.sqrt(nv / c2) + ADAM_EPS) + ADAM_WD * w_ref[...])


def _adamw_blocks(parts, idx, w, m, v, name):
    r, c = w.shape
    tr = _tile(r, 256)
    npart = len(parts)
    c1 = 1.0 - ADAM_B1 ** ADAM_STEP
    c2 = 1.0 - ADAM_B2 ** ADAM_STEP

    def body(idx_ref, *refs):
        _adamw_math(refs[:npart], *refs[npart:], c1, c2)

    def sel(p):
        return pl.BlockSpec((None, tr, c), lambda i, s: (s[p], i, 0))

    blk = pl.BlockSpec((tr, c), lambda i, s: (i, 0))
    gs = pltpu.PrefetchScalarGridSpec(num_scalar_prefetch=1, grid=(r // tr,),
                                      in_specs=[sel(p) for p in range(npart)] + [blk] * 3, out_specs=[blk] * 4)
    return _pc(body, name=name, out_shape=[jax.ShapeDtypeStruct((r, c), F32)] * 4, grid_spec=gs,
               sem=("parallel",))(idx, *parts, w, m, v)


_EARLY = ["a_ln_gain", "a_ln_bias", "a_w_s", "a_b_s", "b_lower_bounds", "b_gn_gain"]


def _pack(arrs):
    flat = jnp.concatenate([a.reshape(-1) for a in arrs])
    rows = -(-flat.shape[0] // 1024) * 8
    return jnp.pad(flat, (0, rows * 128 - flat.shape[0])).reshape(rows, 128)


def _unpack(buf, like):
    flat = buf.reshape(-1)
    out, off = [], 0
    for a in like:
        out.append(flat[off:off + a.size].reshape(a.shape))
        off += a.size
    return out


def kernel(x, c, norm_gain, w_ada, b_ada, a_w_in, a_ln_gain, a_ln_bias, a_w_s, a_b_s, a_w_out, b_w_in, b_lower_bounds, b_gn_gain, b_w_out, final_gain, loss_target, m_norm_gain, m_w_ada, m_b_ada, m_a_w_in, m_a_ln_gain, m_a_ln_bias, m_a_w_s, m_a_b_s, m_a_w_out, m_b_w_in, m_b_lower_bounds, m_b_gn_gain, m_b_w_out, m_final_gain, v_norm_gain, v_w_ada, v_b_ada, v_a_w_in, v_a_ln_gain, v_a_ln_bias, v_a_w_s, v_a_b_s, v_a_w_out, v_b_w_in, v_b_lower_bounds, v_b_gn_gain, v_b_w_out, v_final_gain):
    w = dict(norm_gain=norm_gain, w_ada=w_ada, b_ada=b_ada, a_w_in=a_w_in, a_ln_gain=a_ln_gain,
             a_ln_bias=a_ln_bias, a_w_s=a_w_s, a_b_s=a_b_s, a_w_out=a_w_out, b_w_in=b_w_in,
             b_lower_bounds=b_lower_bounds, b_gn_gain=b_gn_gain, b_w_out=b_w_out, final_gain=final_gain)
    mo = dict(norm_gain=m_norm_gain, w_ada=m_w_ada, b_ada=m_b_ada, a_w_in=m_a_w_in, a_ln_gain=m_a_ln_gain,
              a_ln_bias=m_a_ln_bias, a_w_s=m_a_w_s, a_b_s=m_a_b_s, a_w_out=m_a_w_out, b_w_in=m_b_w_in,
              b_lower_bounds=m_b_lower_bounds, b_gn_gain=m_b_gn_gain, b_w_out=m_b_w_out, final_gain=m_final_gain)
    vo = dict(norm_gain=v_norm_gain, w_ada=v_w_ada, b_ada=v_b_ada, a_w_in=v_a_w_in, a_ln_gain=v_a_ln_gain,
              a_ln_bias=v_a_ln_bias, a_w_s=v_a_w_s, a_b_s=v_a_b_s, a_w_out=v_a_w_out, b_w_in=v_b_w_in,
              b_lower_bounds=v_b_lower_bounds, b_gn_gain=v_b_gn_gain, b_w_out=v_b_w_out, final_gain=v_final_gain)

    nb, t_seq, d = x.shape
    m = nb * t_seq
    ncol_ada = w_ada.shape[2]
    xi, yi, ci = lax.axis_index("x"), lax.axis_index("y"), lax.axis_index("c")
    me = 4 * xi + 2 * yi + ci

    c_g, wa_in_g = _all_gather([c, a_w_in[0].astype(BF16)], "gather_c_wa")

    c_all = c_g.reshape(NDEV * nb, d)
    b_cols = lax.dynamic_slice(b_ada, (0, me * ncol_ada), (2, ncol_ada)).reshape(2, 1, ncol_ada)
    mod_part, lbj = _ada_fwd(c_all, w_ada, b_cols, b_lower_bounds)
    mod_all = _all_gather([mod_part], "gather_mod")[0]
    mod_mine = lax.dynamic_slice_in_dim(mod_all, me * nb, nb, axis=2)
    mod_mine = mod_mine.transpose(1, 2, 0, 3).reshape(2, nb, 3, d)
    mod0, mod1 = mod_mine[0], mod_mine[1]

    di = a_w_out.shape[1] * NDEV

    xf = x.reshape(m, d)
    tgt = loss_target.reshape(m, d)
    ng0, ng1 = norm_gain[0:1], norm_gain[1:2]
    ncb = b_w_in.shape[2]
    wb_lo, wb_hi = b_w_in[0][:, :ncb // 2].astype(BF16), b_w_in[0][:, ncb // 2:].astype(BF16)
    h0, h0_t = _prenorm(xf, ng0, mod0, t_seq, "prenorm_a")
    proj_a, half = _mm_in(h0, [wa_in_g], 1, "in_proj_a", comm=_gather_first([a_w_out[0].astype(BF16), wb_lo]))
    bs_t = jnp.pad(a_b_s[0].T, ((0, 0), (0, 128 - SG_GROUPS)))
    ybr_a, (wa_out_g, wb_lo_g, wb_hi_half) = _a_mid_fwd(
        proj_a, a_ln_gain, a_ln_bias, a_w_s[0], bs_t, t_seq, comm=_join(_gather_second(half), _gather_first([wb_hi])))
    wa_out = wa_out_g.reshape(di, d)
    (yout_a, x1), (wb_hi_g, wb_out_half) = _out_proj(
        ybr_a, wa_out, xf, mod0, t_seq, "out_proj_a",
        comm=_join(_gather_second([wb_hi_half]), _gather_first([b_w_out[0].astype(BF16)])))
    wb_in_g = [wb_lo_g, wb_hi_g]
    h1, h1_t = _prenorm(x1, ng1, mod1, t_seq, "prenorm_b")
    proj_b, (wb_out_g,) = _mm_in(h1, wb_in_g, 4, "in_proj_b", comm=_gather_second([wb_out_half]))
    wb_out = wb_out_g.reshape(di, d)
    o_b, ybr_b, states = _hgrn_fwd(proj_b, lbj, b_gn_gain, nb, t_seq)
    yout_b, dx2, loss_part, d_final_gain = _out_proj_loss(ybr_b, wb_out, x1, mod1, final_gain.reshape(1, d), tgt, t_seq)

    rows_out = a_w_out.shape[1]
    dy_b, dgate1, dybr_b = _gate_dybr(dx2, yout_b, mod1, wb_out, t_seq, "dybr_b")
    rs_wb_out = _ReduceScatter(_mm_dw_out(ybr_b, dy_b, "dw_out_b").reshape(NDEV, rows_out, d), "b_w_out")
    (dproj_b, d_lb, d_gn), got = _hgrn_bwd(proj_b, o_b, dybr_b, states, lbj, b_gn_gain, nb, t_seq,
                                           comm=rs_wb_out.swap_core())
    rs_wb_out.after_core(got[0])
    dh1, got = _mm_din(dproj_b, wb_in_g, 4, "dh_b", comm=rs_wb_out.swap_chips())
    rs_wb_out.after_chips(got[0])
    dx1, dss1, dgain1 = _prenorm_bwd(dh1, x1, ng1, mod1, dx2, t_seq, "prenorm_bwd_b")
    rs_wb_in = _ReduceScatter(_mm_dw_in(h1_t, dproj_b, ncb, 4, "dw_in_b"), "b_w_in")

    dy_a, dgate0, dybr_a = _gate_dybr(dx1, yout_a, mod0, wa_out, t_seq, "dybr_a")
    g_wa_out, got = _mm_dw_out(ybr_a, dy_a, "dw_out_a", comm=rs_wb_in.swap_core())
    rs_wb_in.after_core(got[0])
    rs_wa_out = _ReduceScatter(g_wa_out.reshape(NDEV, rows_out, d), "a_w_out")
    (dproj_a, d_lng, d_lnb, d_ws, d_bs_t), got = _a_mid_bwd(
        proj_a, dybr_a, a_ln_gain, a_ln_bias, a_w_s[0], bs_t, t_seq,
        comm=_join(rs_wb_in.swap_chips(), rs_wa_out.swap_core()))
    rs_wb_in.after_chips(got[0])
    rs_wa_out.after_core(got[1])
    part = dict(a_ln_gain=d_lng, a_ln_bias=d_lnb, a_w_s=d_ws[None], a_b_s=d_bs_t[:, :SG_GROUPS].T[None],
                b_lower_bounds=jnp.concatenate([-d_lb, d_lb], axis=0), b_gn_gain=d_gn)
    early_pack = _pack([part[k].reshape(w[k].shape) for k in _EARLY])
    g_wa_in, got = _mm_dw_in(h0_t, dproj_a, wa_in_g.shape[2], 1, "dw_in_a",
                             comm=_join(rs_wa_out.swap_chips(), _gather_first([early_pack])))
    rs_wa_out.after_chips(got[0])
    rs_wa_in = _ReduceScatter(g_wa_in, "a_w_in")
    n_tiles = m // _din_tile(m)
    assert n_tiles >= 2
    first_tiles = max(1, (3 * n_tiles) // 8)
    dh0, got2 = _mm_din(dproj_a, [wa_in_g], 1, "dh_a_first", tiles=(0, first_tiles),
                        comm=_join(rs_wa_in.swap_core(), _gather_second([got[1]])))
    rs_wa_in.after_core(got2[0])
    early_all = got2[1]
    dh0, got = _mm_din(dproj_a, [wa_in_g], 1, "dh_a_rest", comm=rs_wa_in.swap_chips(),
                       tiles=(first_tiles, n_tiles - first_tiles), prev=dh0)
    rs_wa_in.after_chips(got[0])
    dx0, dss0, dgain0 = _prenorm_bwd(dh0, xf, ng0, mod0, dx1, t_seq, "prenorm_bwd_a")
    grad_x = dx0.reshape(nb, t_seq, d)

    dmod = jnp.stack([jnp.concatenate([dss0, dgate0], axis=1), jnp.concatenate([dss1, dgate1], axis=1)])
    late_like = [norm_gain, final_gain, loss_part.reshape(1)]
    late_pack = _pack([jnp.concatenate([dgain0, dgain1], axis=0), d_final_gain[0], loss_part.reshape(1)])
    dmod_all, late_all = _all_gather([dmod.reshape(2, nb, 3 * d), late_pack], "gather_tail")
    dmod_all = dmod_all.transpose(1, 0, 2, 3).reshape(2, NDEV * nb, 3 * d)
    dmod_cols = lax.dynamic_slice_in_dim(dmod_all, me * ncol_ada, ncol_ada, axis=2)
    g_w_ada, g_b_ada = _ada_bwd(c_all, dmod_cols, dmod_all)

    res = {}
    early_like = [w[k] for k in _EARLY]
    dev_order = jnp.arange(NDEV, dtype=jnp.int32)
    sm = _adamw_blocks([early_all] * NDEV, dev_order, _pack(early_like), _pack([mo[k] for k in _EARLY]),
                       _pack([vo[k] for k in _EARLY]), "adamw_small_early")
    sm = [dict(zip(_EARLY, _unpack(buf, early_like))) for buf in sm]
    for k in _EARLY:
        res[k] = tuple(s[k] for s in sm)
    zero = jnp.zeros((1,), F32)
    sm = _adamw_blocks([late_all] * NDEV, dev_order, _pack([norm_gain, final_gain, zero]),
                       _pack([mo["norm_gain"], mo["final_gain"], zero]),
                       _pack([vo["norm_gain"], vo["final_gain"], zero]), "adamw_small_late")
    sm = [_unpack(buf, late_like) for buf in sm]
    res["norm_gain"] = tuple(s[0] for s in sm)
    res["final_gain"] = tuple(s[1] for s in sm)
    loss = sm[0][2][0]
    rb = _adamw([g_b_ada], b_ada, mo["b_ada"], vo["b_ada"], "adamw_b_ada")
    res["b_ada"] = tuple(rb)
    sh = w_ada.shape
    ra = _adamw([g_w_ada.reshape(sh[0] * sh[1], sh[2])], w_ada.reshape(sh[0] * sh[1], sh[2]),
                mo["w_ada"].reshape(sh[0] * sh[1], sh[2]), vo["w_ada"].reshape(sh[0] * sh[1], sh[2]), "adamw_w_ada")
    res["w_ada"] = tuple(z.reshape(sh) for z in ra)

    for k, rs in (("b_w_out", rs_wb_out), ("b_w_in", rs_wb_in), ("a_w_out", rs_wa_out), ("a_w_in", rs_wa_in)):
        res[k] = tuple(z[None] for z in _adamw_blocks(rs.parts, rs.idx, w[k][0], mo[k][0], vo[k][0], "adamw_" + k))

    order = ["norm_gain", "w_ada", "b_ada", "a_w_in", "a_ln_gain", "a_ln_bias", "a_w_s", "a_b_s", "a_w_out",
             "b_w_in", "b_lower_bounds", "b_gn_gain", "b_w_out", "final_gain"]
    return (loss, grad_x, *[res[k][0] for k in order], *[res[k][1] for k in order],
            *[res[k][2] for k in order], *[res[k][3] for k in order])
```

```python
import functools
import math

import jax
import jax.numpy as jnp
from jax import lax
from jax.experimental import pallas as pl
from jax.experimental.pallas import tpu as pltpu

F32 = jnp.float32
BF16 = jnp.bfloat16
MESH = pl.DeviceIdType.MESH
NDEV = 8
EPS = 1e-6
CHUNK = 64
SG_BLOCK = 128
SG_GROUPS = 8
HEAD_DIM = 128
CUM_ROWS = 256
PHASE_HEADS = 8
ADAM_LR, ADAM_B1, ADAM_B2, ADAM_EPS, ADAM_WD, ADAM_STEP = 0.001, 0.9, 0.999, 1e-08, 0.01, 10
VMEM_LIMIT = 56 * 1024 * 1024
ANY = pl.BlockSpec(memory_space=pl.ANY)


class _Hosted:
    def __init__(self, arrays, out_shapes, nsem, start, finish, aliases=None):
        self.arrays, self.out_shapes, self.nsem = list(arrays), list(out_shapes), nsem
        self.start, self.finish = start, finish
        self.aliases = dict(aliases or {})


def _join(*comms):
    arrays, outs, aliases, offs, nsem = [], [], {}, [], 0
    for cm in comms:
        offs.append((len(arrays), len(outs), nsem))
        for i, o in cm.aliases.items():
            aliases[len(arrays) + i] = len(outs) + o
        arrays += cm.arrays
        outs += cm.out_shapes
        nsem += cm.nsem

    def run(which):
        def f(ins, outs_, ss, rs, base):
            for cm, (ia, io, isem) in zip(comms, offs):
                getattr(cm, which)(ins[ia:ia + len(cm.arrays)], outs_[io:io + len(cm.out_shapes)], ss, rs, base + isem)
        return f

    return _Hosted(arrays, outs, nsem, run("start"), run("finish"), aliases)


def _pc(body, *, name, out_shape, grid=None, in_specs=None, out_specs=None, scratch=(), sem=None,
        grid_spec=None, comm=None, aliases=None):
    cp = dict(vmem_limit_bytes=VMEM_LIMIT)
    aliases = dict(aliases or {})
    if comm is None:
        if sem is not None:
            cp["dimension_semantics"] = sem
        kw = {"input_output_aliases": aliases}
        if grid_spec is not None:
            kw["grid_spec"] = grid_spec
        else:
            if grid is not None:
                kw["grid"] = grid
            if in_specs is not None:
                kw["in_specs"] = in_specs
            if out_specs is not None:
                kw["out_specs"] = out_specs
            kw["scratch_shapes"] = list(scratch)
        return pl.pallas_call(functools.partial(body), name=name, out_shape=out_shape,
                              compiler_params=pltpu.CompilerParams(**cp), **kw)

    single = not isinstance(out_shape, (list, tuple))
    outs_list = [out_shape] if single else list(out_shape)
    ospecs = [out_specs] if single else list(out_specs)
    n_in, n_out, n_ci, n_co, n_scr = len(in_specs), len(outs_list), len(comm.arrays), len(comm.out_shapes), len(scratch)
    cp["dimension_semantics"] = ("arbitrary",) * len(grid)

    def hosted(*refs):
        cin, hin = refs[:n_in], refs[n_in:n_in + n_ci]
        cout = refs[n_in + n_ci:n_in + n_ci + n_out]
        hout = refs[n_in + n_ci + n_out:n_in + n_ci + n_out + n_co]
        scr = refs[n_in + n_ci + n_out + n_co:n_in + n_ci + n_out + n_co + n_scr]
        ssem, rsem = refs[-2], refs[-1]
        first = functools.reduce(lambda p, q: p & q, [pl.program_id(a) == 0 for a in range(len(grid))])
        last = functools.reduce(lambda p, q: p & q, [pl.program_id(a) == grid[a] - 1 for a in range(len(grid))])

        @pl.when(first)
        def _():
            comm.start(hin, hout, ssem, rsem, 0)

        body(*cin, *cout, *scr)

        @pl.when(last)
        def _():
            comm.finish(hin, hout, ssem, rsem, 0)

    call = pl.pallas_call(
        hosted, name=name, grid=grid, in_specs=list(in_specs) + [ANY] * n_ci, out_specs=ospecs + [ANY] * n_co,
        out_shape=outs_list + comm.out_shapes,
        scratch_shapes=list(scratch) + [pltpu.SemaphoreType.DMA((comm.nsem,)), pltpu.SemaphoreType.DMA((comm.nsem,))],
        input_output_aliases={**aliases, **{n_in + i: n_out + o for i, o in comm.aliases.items()}},
        compiler_params=pltpu.CompilerParams(**cp))

    def run(*args):
        res = call(*args, *comm.arrays)
        comp = res[:n_out]
        return (comp[0] if single else comp), list(res[n_out:])

    return run


def _tile(n, pref):
    return pref if n % pref == 0 else n


def _sigmoid(x):
    return 1.0 / (1.0 + jnp.exp(-x))


def _gelu(x):
    c = math.sqrt(2.0 / math.pi)
    return 0.5 * x * (1.0 + jnp.tanh(c * (x + 0.044715 * (x * x * x))))


def _gelu_and_grad(x):
    c = math.sqrt(2.0 / math.pi)
    x2 = x * x
    t = jnp.tanh(c * (x + 0.044715 * (x2 * x)))
    half = 0.5 * (1.0 + t)
    return x * half, half + (0.5 * x) * (1.0 - t * t) * (c + (3.0 * 0.044715 * c) * x2)


def _dot(a, b):
    return jnp.dot(a, b, preferred_element_type=F32)


def _dot_nt(a, b):
    return lax.dot_general(a, b, (((1,), (1,)), ((), ())), preferred_element_type=F32)


def _dot_tn(a, b):
    return lax.dot_general(a, b, (((0,), (0,)), ((), ())), preferred_element_type=F32)


def _tri_mask(n, reverse):
    r = lax.broadcasted_iota(jnp.int32, (n, n), 0)
    c = lax.broadcasted_iota(jnp.int32, (n, n), 1)
    same = (r // CHUNK) == (c // CHUNK)
    tri = (c >= r) if reverse else (c <= r)
    return jnp.where(same & tri, 1.0, 0.0).astype(BF16)


def _tri_apply(tri, x):
    hi = x.astype(BF16)
    r1 = x - hi.astype(F32)
    mid = r1.astype(BF16)
    lo = (r1 - mid.astype(F32)).astype(BF16)
    return _dot(tri, hi) + (_dot(tri, mid) + _dot(tri, lo))


def _all_gather(arrs, name):
    n = len(arrs)

    def body(*refs):
        ins, outs = refs[:n], refs[n:2 * n]
        send_sems, recv_sems, local_sems = refs[2 * n:]
        x, y, c = lax.axis_index("x"), lax.axis_index("y"), lax.axis_index("c")
        me, sibling = (x, y, c), (x, y, 1 - c)
        near = (x + c - 2 * x * c, y + (1 - c) - 2 * y * (1 - c))
        far = (x + (1 - c) - 2 * x * (1 - c), y + c - 2 * y * c)
        diag = (1 - x, 1 - y)

        def blk(a, p):
            return outs[a].at[4 * p[0] + 2 * p[1] + p[2]]

        def copy(a, k, block, to, src=None):
            return pltpu.make_async_remote_copy(
                src_ref=blk(a, block) if src is None else src, dst_ref=blk(a, block),
                send_sem=send_sems.at[7 * a + k], recv_sem=recv_sems.at[7 * a + k],
                device_id=to, device_id_type=MESH)

        mine = [pltpu.make_async_copy(ins[a], blk(a, me), local_sems.at[a]) for a in range(n)]
        for m in mine:
            m.start()
        sends = []
        for a in range(n):
            sends += [copy(a, 0, me, sibling, src=ins[a]), copy(a, 1, me, (*near, c), src=ins[a]),
                      copy(a, 2, me, (*far, c), src=ins[a])]
        for cp in sends:
            cp.start()
        for a in range(n):
            copy(a, 1, (*near, c), me).wait_recv()
            sends.append(copy(a, 3, (*near, c), (*far, c)))
            sends[-1].start()
        for a in range(n):
            sends.append(copy(a, 4, (*near, c), sibling))
            sends[-1].start()
            copy(a, 2, (*far, c), me).wait_recv()
            sends.append(copy(a, 5, (*far, c), sibling))
            sends[-1].start()
        for a in range(n):
            copy(a, 3, (*diag, c), me).wait_recv()
            sends.append(copy(a, 6, (*diag, c), sibling))
            sends[-1].start()
        for a in range(n):
            copy(a, 0, sibling, me).wait_recv()
            copy(a, 4, (*far, 1 - c), me).wait_recv()
            copy(a, 5, (*near, 1 - c), me).wait_recv()
            copy(a, 6, (*diag, 1 - c), me).wait_recv()
        for cp in sends:
            cp.wait_send()
        for m in mine:
            m.wait()

    out_shape = [jax.ShapeDtypeStruct((NDEV,) + a.shape, a.dtype) for a in arrs]
    return _pc(body, name=name, out_shape=out_shape, in_specs=[ANY] * n, out_specs=[ANY] * n,
               scratch=[pltpu.SemaphoreType.DMA((7 * n,)), pltpu.SemaphoreType.DMA((7 * n,)),
                        pltpu.SemaphoreType.DMA((n,))])(*arrs)


def _gather_first(arrs):
    n = len(arrs)

    def parts(ins, outs, ss, rs, base):
        x, y, c = lax.axis_index("x"), lax.axis_index("y"), lax.axis_index("c")
        me, sibling = (x, y, c), (x, y, 1 - c)
        chips = [(1 - x, y), (x, 1 - y), (1 - x, 1 - y)]

        def blk(a, p):
            return outs[a].at[4 * p[0] + 2 * p[1] + p[2]]

        def copy(a, k, block, to):
            return pltpu.make_async_remote_copy(
                src_ref=ins[a], dst_ref=blk(a, block), send_sem=ss.at[base + 4 * a + k],
                recv_sem=rs.at[base + 4 * a + k], device_id=to, device_id_type=MESH)

        local = [pltpu.make_async_copy(ins[a], blk(a, me), ss.at[base + 4 * n + a]) for a in range(n)]
        sends, recvs = [], []
        for a in range(n):
            sends.append(copy(a, 0, me, sibling))
            recvs.append(copy(a, 0, sibling, me))
            for j, chip in enumerate(chips):
                sends.append(copy(a, 1 + j, me, (*chip, c)))
                recvs.append(copy(a, 1 + j, (*chip, c), me))
        return local, sends, recvs

    def start(ins, outs, ss, rs, base):
        local, sends, _ = parts(ins, outs, ss, rs, base)
        for cp in local + sends:
            cp.start()

    def finish(ins, outs, ss, rs, base):
        local, sends, recvs = parts(ins, outs, ss, rs, base)
        for cp in recvs:
            cp.wait_recv()
        for cp in sends:
            cp.wait_send()
        for cp in local:
            cp.wait()

    return _Hosted(arrs, [jax.ShapeDtypeStruct((NDEV,) + a.shape, a.dtype) for a in arrs], 5 * n, start, finish)


def _gather_second(bufs):
    n = len(bufs)

    def parts(ins, outs, ss, rs, base):
        x, y, c = lax.axis_index("x"), lax.axis_index("y"), lax.axis_index("c")
        sibling = (x, y, 1 - c)
        chips = [(1 - x, y), (x, 1 - y), (1 - x, 1 - y)]
        sends, recvs = [], []
        for a in range(n):
            for j, chip in enumerate(chips):
                mine = 4 * chip[0] + 2 * chip[1] + c
                theirs = 4 * chip[0] + 2 * chip[1] + (1 - c)
                sends.append(pltpu.make_async_remote_copy(
                    src_ref=ins[a].at[mine], dst_ref=outs[a].at[mine], send_sem=ss.at[base + 3 * a + j],
                    recv_sem=rs.at[base + 3 * a + j], device_id=sibling, device_id_type=MESH))
                recvs.append(pltpu.make_async_remote_copy(
                    src_ref=ins[a].at[theirs], dst_ref=outs[a].at[theirs], send_sem=ss.at[base + 3 * a + j],
                    recv_sem=rs.at[base + 3 * a + j], device_id=sibling, device_id_type=MESH))
        return sends, recvs

    def start(ins, outs, ss, rs, base):
        for cp in parts(ins, outs, ss, rs, base)[0]:
            cp.start()

    def finish(ins, outs, ss, rs, base):
        sends, recvs = parts(ins, outs, ss, rs, base)
        for cp in recvs:
            cp.wait_recv()
        for cp in sends:
            cp.wait_send()

    return _Hosted(bufs, [jax.ShapeDtypeStruct(b.shape, b.dtype) for b in bufs], 3 * n, start, finish,
                   aliases={a: a for a in range(n)})


def _swap(src, nblk, ids_fn, partner_fn):
    def copies(ins, outs, ss, rs, base):
        x, y, c = lax.axis_index("x"), lax.axis_index("y"), lax.axis_index("c")
        ids = ids_fn(x, y, c)
        partner = partner_fn(x, y, c)
        return [pltpu.make_async_remote_copy(
            src_ref=ins[0].at[ids[k]], dst_ref=outs[0].at[k], send_sem=ss.at[base + k], recv_sem=rs.at[base + k],
            device_id=partner, device_id_type=MESH) for k in range(nblk)]

    def start(ins, outs, ss, rs, base):
        for cp in copies(ins, outs, ss, rs, base):
            cp.start()

    def finish(ins, outs, ss, rs, base):
        for cp in copies(ins, outs, ss, rs, base):
            cp.wait()

    return _Hosted([src], [jax.ShapeDtypeStruct((nblk,) + src.shape[1:], src.dtype)], nblk, start, finish)


def _swap_chips(send):
    def copies(ins, outs, ss, rs, base):
        x, y, c = lax.axis_index("x"), lax.axis_index("y"), lax.axis_index("c")
        chips = [(1 - x, y), (x, 1 - y), (1 - x, 1 - y)]
        return [pltpu.make_async_remote_copy(
            src_ref=ins[0].at[j], dst_ref=outs[0].at[j], send_sem=ss.at[base + j], recv_sem=rs.at[base + j],
            device_id=(*chip, c), device_id_type=MESH) for j, chip in enumerate(chips)]

    def start(ins, outs, ss, rs, base):
        for cp in copies(ins, outs, ss, rs, base):
            cp.start()

    def finish(ins, outs, ss, rs, base):
        for cp in copies(ins, outs, ss, rs, base):
            cp.wait()

    return _Hosted([send], [jax.ShapeDtypeStruct(send.shape, send.dtype)], 3, start, finish)


def _add_send(a, b, idx, ns, name):
    _, r, c = a.shape
    tr = _tile(r, 256)

    def body(idx_ref, a_ref, b_ref, send_ref):
        send_ref[...] = (a_ref[...] + b_ref[...]).astype(BF16)

    def sel(off):
        return pl.BlockSpec((None, tr, c), lambda k, i, s: (s[off + k], i, 0))

    gs = pltpu.PrefetchScalarGridSpec(num_scalar_prefetch=1, grid=(ns, r // tr), in_specs=[sel(0), sel(ns)],
                                      out_specs=pl.BlockSpec((None, tr, c), lambda k, i, s: (k, i, 0)))
    return _pc(body, name=name, grid_spec=gs, sem=("arbitrary", "arbitrary"),
               out_shape=jax.ShapeDtypeStruct((ns, r, c), BF16))(idx, a, b)


class _ReduceScatter:
    def __init__(self, g, tag):
        self.g, self.tag = g, tag

    def swap_core(self):
        return _swap(self.g, 4, lambda x, y, c: [1 - c, 3 - c, 5 - c, 7 - c], lambda x, y, c: (x, y, 1 - c))

    def after_core(self, recv):
        x, y, c = lax.axis_index("x"), lax.axis_index("y"), lax.axis_index("c")
        chips = [(1 - x, y), (x, 1 - y), (1 - x, 1 - y)]
        idx = jnp.stack([4 * p + 2 * q + c for p, q in chips] + [2 * p + q for p, q in chips]).astype(jnp.int32)
        self.send = _add_send(self.g, recv, idx, 3, "rs_add_" + self.tag)
        self.recv_core = recv
        zero = jnp.zeros((), jnp.int32)
        self.idx = jnp.stack([4 * x + 2 * y + c, 2 * x + y, zero, zero + 1, zero + 2]).astype(jnp.int32)

    def swap_chips(self):
        return _swap_chips(self.send)

    def after_chips(self, recv):
        self.parts = [self.g, self.recv_core, recv, recv, recv]


def _ada_fwd(c_all, w_ada, b_cols, b_lb):
    nl, d, ncol = w_ada.shape
    nseq = c_all.shape[0]
    di = b_lb.shape[1]

    def body(c_ref, w_ref, b_ref, lb_ref, mod_ref, lbj_ref):
        cv = c_ref[...]
        cact = (cv * _sigmoid(cv)).astype(BF16)
        for l in range(nl):
            mod_ref[l] = _dot(cact, w_ref[l].astype(BF16)) + b_ref[l]
        b0, b1 = lb_ref[0:1, :], lb_ref[1:2, :]
        mx = jnp.maximum(b0, b1)
        e0, e1 = jnp.exp(b0 - mx), jnp.exp(b1 - mx)
        s = e0 + e1
        p0, p1 = e0 / s, e1 / s
        lbj_ref[0:1, :] = (p0 + p1) - p0
        lbj_ref[1:2, :] = p0 * p1

    return _pc(body, name="ada_fwd",
               out_shape=[jax.ShapeDtypeStruct((nl, nseq, ncol), F32), jax.ShapeDtypeStruct((2, di), F32)]
               )(c_all, w_ada, b_cols, b_lb)


def _ada_bwd(c_all, dmod_cols, dmod_full):
    nl, nseq, ncol = dmod_cols.shape
    d = c_all.shape[1]
    d3 = dmod_full.shape[2]

    def body(c_ref, dc_ref, df_ref, gw_ref, gb_ref):
        cv = c_ref[...]
        cact = (cv * _sigmoid(cv)).astype(BF16)
        for l in range(nl):
            gw_ref[l] = _dot_tn(cact, dc_ref[l].astype(BF16))
            gb_ref[l:l + 1, :] = jnp.sum(df_ref[l], axis=0, keepdims=True)

    return _pc(body, name="ada_bwd",
               out_shape=[jax.ShapeDtypeStruct((nl, d, ncol), F32), jax.ShapeDtypeStruct((nl, d3), F32)]
               )(c_all, dmod_cols, dmod_full)


def _prenorm(x, gain, mod, t_seq, name):
    m, d = x.shape
    tm = _tile(t_seq, 1024)
    per = t_seq // tm

    def body(x_ref, g_ref, mod_ref, h_ref, ht_ref):
        xv = x_ref[...]
        rstd = lax.rsqrt(jnp.mean(xv * xv, axis=-1, keepdims=True) + EPS)
        r = xv * rstd * g_ref[...]
        h = r * (1.0 + mod_ref[0, 1:2, :]) + mod_ref[0, 0:1, :]
        h_ref[...] = h.astype(BF16)
        ht_ref[...] = h.T.astype(BF16)

    return _pc(body, name=name, out_shape=[jax.ShapeDtypeStruct((m, d), BF16), jax.ShapeDtypeStruct((d, m), BF16)],
               grid=(m // tm,),
               in_specs=[pl.BlockSpec((tm, d), lambda i: (i, 0)), pl.BlockSpec((1, d), lambda i: (0, 0)),
                         pl.BlockSpec((1, 3, d), lambda i: (i // per, 0, 0))],
               out_specs=[pl.BlockSpec((tm, d), lambda i: (i, 0)), pl.BlockSpec((d, tm), lambda i: (0, i))],
               sem=("parallel",))(x, gain, mod)


def _prenorm_bwd(dh, x, gain, mod, dxn, t_seq, name):
    m, d = x.shape
    nb = m // t_seq
    tm = _tile(t_seq, 1024)
    per = t_seq // tm

    def body(dh_ref, x_ref, g_ref, mod_ref, dxn_ref, dx_ref, dss_ref, dg_ref):
        i = pl.program_id(0)
        xv, dhv, g = x_ref[...], dh_ref[...], g_ref[...]
        rstd = lax.rsqrt(jnp.mean(xv * xv, axis=-1, keepdims=True) + EPS)
        xhat = xv * rstd
        dr = dhv * (1.0 + mod_ref[0, 1:2, :])
        dxhat = dr * g
        dx_ref[...] = dxn_ref[...] + rstd * (dxhat - xhat * jnp.mean(dxhat * xhat, axis=-1, keepdims=True))

        @pl.when(i % per == 0)
        def _():
            dss_ref[...] = jnp.zeros_like(dss_ref)

        @pl.when(i == 0)
        def _():
            dg_ref[...] = jnp.zeros_like(dg_ref)

        dss_ref[0, 0:1, :] += jnp.sum(dhv, axis=0, keepdims=True)
        dss_ref[0, 1:2, :] += jnp.sum(dhv * (xhat * g), axis=0, keepdims=True)
        dg_ref[...] += jnp.sum(dr * xhat, axis=0, keepdims=True)

    row = pl.BlockSpec((tm, d), lambda i: (i, 0))
    return _pc(body, name=name,
               out_shape=[jax.ShapeDtypeStruct((m, d), F32), jax.ShapeDtypeStruct((nb, 2, d), F32),
                          jax.ShapeDtypeStruct((1, d), F32)],
               grid=(m // tm,),
               in_specs=[row, row, pl.BlockSpec((1, d), lambda i: (0, 0)),
                         pl.BlockSpec((1, 3, d), lambda i: (i // per, 0, 0)), row],
               out_specs=[row, pl.BlockSpec((1, 2, d), lambda i: (i // per, 0, 0)),
                          pl.BlockSpec((1, d), lambda i: (0, 0))],
               sem=("arbitrary",))(dh, x, gain, mod, dxn)


def _mm_in(h, ws, sections, name, comm=None):
    m, k = h.shape
    nw = len(ws)
    widths = [w.shape[2] for w in ws]
    offs = [sum(widths[:a]) for a in range(nw)]
    nc = sum(widths)
    per = NDEV // sections if sections > 1 else NDEV
    tm = _din_tile(m)
    assert per % 2 == 0

    def body(*refs):
        hv = refs[0][...]
        o_ref = refs[1 + nw]
        for b in range(2):
            for a in range(nw):
                lo = b * nc + offs[a]
                o_ref[:, lo:lo + widths[a]] = _dot(hv, refs[1 + a][b])

    w_specs = [pl.BlockSpec((2, k, wd), lambda j, i: (j, 0, 0)) for wd in widths]
    if sections > 1:
        out_shape = jax.ShapeDtypeStruct((sections, m, per * nc), F32)
        out_spec = pl.BlockSpec((None, tm, 2 * nc), lambda j, i: ((2 * j) // per, i, ((2 * j) % per) // 2))
    else:
        out_shape = jax.ShapeDtypeStruct((m, NDEV * nc), F32)
        out_spec = pl.BlockSpec((tm, 2 * nc), lambda j, i: (i, j))
    return _pc(body, name=name, out_shape=out_shape, grid=(NDEV // 2, m // tm),
               in_specs=[pl.BlockSpec((tm, k), lambda j, i: (i, 0))] + w_specs,
               out_specs=out_spec, sem=("parallel", "parallel"), comm=comm)(h, *ws)


def _din_tile(m):
    return 1024 if m % 1024 == 0 and m >= 2048 else _tile(m, 512)


def _mm_din(dproj, ws, sections, name, comm=None, tiles=None, prev=None):
    nw, k = len(ws), ws[0].shape[1]
    widths = [w.shape[2] for w in ws]
    offs = [sum(widths[:a]) for a in range(nw)]
    nc = sum(widths)
    m = dproj.shape[-2]
    tm = _din_tile(m)
    t0, nt = tiles if tiles is not None else (0, m // tm)
    per = NDEV // sections if sections > 1 else NDEV
    assert per % 2 == 0

    def body(*refs):
        d_ref, o_ref = refs[0], refs[-1]
        j = pl.program_id(1)
        acc = None
        for b in range(2):
            for a in range(nw):
                lo = b * nc + offs[a]
                term = _dot_nt(d_ref[:, lo:lo + widths[a]], refs[1 + a][b])
                acc = term if acc is None else acc + term

        @pl.when(j == 0)
        def _():
            o_ref[...] = acc

        @pl.when(j > 0)
        def _():
            o_ref[...] += acc

    if sections > 1:
        dspec = pl.BlockSpec((None, tm, 2 * nc), lambda i, j: ((2 * j) // per, i + t0, ((2 * j) % per) // 2))
    else:
        dspec = pl.BlockSpec((tm, 2 * nc), lambda i, j: (i + t0, j))
    in_specs = [dspec] + [pl.BlockSpec((2, k, wd), lambda i, j: (j, 0, 0)) for wd in widths]
    args = [dproj, *ws]
    if prev is not None:
        in_specs.append(ANY)
        args.append(prev)
    return _pc(body, name=name, out_shape=jax.ShapeDtypeStruct((m, k), F32), grid=(nt, NDEV // 2), in_specs=in_specs,
               out_specs=pl.BlockSpec((tm, k), lambda i, j: (i + t0, 0)), sem=("parallel", "arbitrary"),
               comm=comm, aliases={1 + nw: 0} if prev is not None else None)(*args)


def _mm_dw_in(ht, dproj, nc, sections, name, comm=None):
    k, m = ht.shape
    tk = 2048 if m % 2048 == 0 else _din_tile(m)
    per = NDEV // sections if sections > 1 else NDEV

    def body(h_ref, d_ref, o_ref):
        kk = pl.program_id(1)
        acc = _dot(h_ref[...], d_ref[...])

        @pl.when(kk == 0)
        def _():
            o_ref[...] = acc

        @pl.when(kk > 0)
        def _():
            o_ref[...] += acc

    if sections > 1:
        dspec = pl.BlockSpec((None, tk, nc), lambda j, i: (j // per, i, j % per))
    else:
        dspec = pl.BlockSpec((tk, nc), lambda j, i: (i, j))
    return _pc(body, name=name, out_shape=jax.ShapeDtypeStruct((NDEV, k, nc), F32), grid=(NDEV, m // tk),
               in_specs=[pl.BlockSpec((k, tk), lambda j, i: (0, i)), dspec],
               out_specs=pl.BlockSpec((None, k, nc), lambda j, i: (j, 0, 0)),
               sem=("parallel", "arbitrary"), comm=comm)(ht, dproj)


def _out_proj(ybr, w_out, x, mod, t_seq, name, comm=None):
    m, di = ybr.shape
    d = w_out.shape[1]
    tm = _tile(t_seq, 512)
    per = t_seq // tm

    def body(y_ref, w_ref, x_ref, mod_ref, yo_ref, xn_ref):
        yo = _dot(y_ref[...], w_ref[...])
        yo_ref[...] = yo
        xn_ref[...] = x_ref[...] + mod_ref[0, 2:3, :] * yo

    row = pl.BlockSpec((tm, d), lambda i: (i, 0))
    return _pc(body, name=name,
               out_shape=[jax.ShapeDtypeStruct((m, d), F32), jax.ShapeDtypeStruct((m, d), F32)],
               grid=(m // tm,),
               in_specs=[pl.BlockSpec((tm, di), lambda i: (i, 0)), pl.BlockSpec((di, d), lambda i: (0, 0)), row,
                         pl.BlockSpec((1, 3, d), lambda i: (i // per, 0, 0))],
               out_specs=[row, row], sem=("parallel",), comm=comm)(ybr, w_out, x, mod)


def _out_proj_loss(ybr, w_out, x, mod, gain, target, t_seq):
    m, di = ybr.shape
    d = w_out.shape[1]
    tm = _tile(t_seq, 512)
    per = t_seq // tm

    def body(y_ref, w_ref, x_ref, mod_ref, g_ref, t_ref, yo_ref, dx_ref, loss_ref, dg_ref):
        i = pl.program_id(0)
        yo = _dot(y_ref[...], w_ref[...])
        yo_ref[...] = yo
        xv = x_ref[...] + mod_ref[0, 2:3, :] * yo
        g = g_ref[...]
        rstd = lax.rsqrt(jnp.mean(xv * xv, axis=-1, keepdims=True) + EPS)
        xhat = xv * rstd
        err = xhat * g - t_ref[...]
        dy = err * (1.0 / d)
        dxhat = dy * g
        dx_ref[...] = rstd * (dxhat - xhat * jnp.mean(dxhat * xhat, axis=-1, keepdims=True))

        @pl.when(i == 0)
        def _():
            loss_ref[...] = jnp.zeros_like(loss_ref)
            dg_ref[...] = jnp.zeros_like(dg_ref)

        loss_ref[...] += 0.5 * jnp.sum(jnp.mean(err * err, axis=-1, keepdims=True), axis=0, keepdims=True)
        dg_ref[...] += jnp.sum(dy * xhat, axis=0, keepdims=True)

    row = pl.BlockSpec((tm, d), lambda i: (i, 0))
    vec = pl.BlockSpec((1, d), lambda i: (0, 0))
    return _pc(body, name="out_proj_loss",
               out_shape=[jax.ShapeDtypeStruct((m, d), F32), jax.ShapeDtypeStruct((m, d), F32),
                          jax.ShapeDtypeStruct((1, 1), F32), jax.ShapeDtypeStruct((1, d), F32)],
               grid=(m // tm,),
               in_specs=[pl.BlockSpec((tm, di), lambda i: (i, 0)), pl.BlockSpec((di, d), lambda i: (0, 0)), row,
                         pl.BlockSpec((1, 3, d), lambda i: (i // per, 0, 0)), vec, row],
               out_specs=[row, row, pl.BlockSpec((1, 1), lambda i: (0, 0)), vec],
               sem=("arbitrary",))(ybr, w_out, x, mod, gain, target)


def _gate_dybr(dxn, yout, mod, w_out, t_seq, name):
    m, d = dxn.shape
    di = w_out.shape[0]
    nb = m // t_seq
    tm = _tile(t_seq, 512)
    per = t_seq // tm

    def body(dxn_ref, yo_ref, mod_ref, w_ref, dy_ref, dgate_ref, o_ref):
        i = pl.program_id(0)
        dv = dxn_ref[...]
        dy = (mod_ref[0, 2:3, :] * dv).astype(BF16)
        dy_ref[...] = dy
        o_ref[...] = _dot_nt(dy, w_ref[...])

        @pl.when(i % per == 0)
        def _():
            dgate_ref[...] = jnp.zeros_like(dgate_ref)

        dgate_ref[0] += jnp.sum(dv * yo_ref[...], axis=0, keepdims=True)

    row = pl.BlockSpec((tm, d), lambda i: (i, 0))
    return _pc(body, name=name,
               out_shape=[jax.ShapeDtypeStruct((m, d), BF16), jax.ShapeDtypeStruct((nb, 1, d), F32),
                          jax.ShapeDtypeStruct((m, di), F32)],
               grid=(m // tm,),
               in_specs=[row, row, pl.BlockSpec((1, 3, d), lambda i: (i // per, 0, 0)),
                         pl.BlockSpec((di, d), lambda i: (0, 0))],
               out_specs=[row, pl.BlockSpec((1, 1, d), lambda i: (i // per, 0, 0)),
                          pl.BlockSpec((tm, di), lambda i: (i, 0))],
               sem=("arbitrary",))(dxn, yout, mod, w_out)


def _mm_dw_out(ybr, dy, name, comm=None):
    m, di = ybr.shape
    d = dy.shape[1]
    tk = 2048 if m % 2048 == 0 else _tile(m, 512)
    tn = _tile(di, 1024)

    def body(y_ref, dy_ref, o_ref):
        kk = pl.program_id(1)
        acc = _dot_tn(y_ref[...], dy_ref[...])

        @pl.when(kk == 0)
        def _():
            o_ref[...] = acc

        @pl.when(kk > 0)
        def _():
            o_ref[...] += acc

    return _pc(body, name=name, out_shape=jax.ShapeDtypeStruct((di, d), F32), grid=(di // tn, m // tk),
               in_specs=[pl.BlockSpec((tk, tn), lambda n, k: (k, n)), pl.BlockSpec((tk, d), lambda n, k: (k, 0))],
               out_specs=pl.BlockSpec((tn, d), lambda n, k: (n, 0)), sem=("parallel", "arbitrary"),
               comm=comm)(ybr, dy)


def _sgu_mask():
    t = lax.broadcasted_iota(jnp.int32, (SG_BLOCK, SG_BLOCK), 0)
    s = lax.broadcasted_iota(jnp.int32, (SG_BLOCK, SG_BLOCK), 1)
    return (s // CHUNK) <= (t // CHUNK)


def _a_mid_fwd(proj, ln_g, ln_b, w_s, bs_t, t_seq, comm=None):
    m, n3 = proj.shape
    di = n3 // 3
    gd = di // SG_GROUPS
    r = _tile(t_seq, 256)
    nblk = r // SG_BLOCK

    def body(p_ref, lg_ref, lb_ref, ws_ref, bs_ref, ybr_ref, s_scr):
        v = _gelu(p_ref[:, di:2 * di])
        mu = jnp.mean(v, axis=-1, keepdims=True)
        vc = v - mu
        rstd = lax.rsqrt(jnp.mean(vc * vc, axis=-1, keepdims=True) + EPS)
        vb = (vc * rstd * lg_ref[...] + lb_ref[...]).astype(BF16)
        mask = _sgu_mask()
        for gi in range(SG_GROUPS):
            ws = jnp.where(mask, ws_ref[gi], 0.0).astype(BF16)
            bcol = bs_ref[:, gi:gi + 1]
            for b in range(nblk):
                rows = slice(b * SG_BLOCK, (b + 1) * SG_BLOCK)
                cols = slice(gi * gd, (gi + 1) * gd)
                s_scr[rows, cols] = _dot(ws, vb[rows, cols]) + bcol
        gg = p_ref[:, 2 * di:]
        ybr_ref[...] = (_gelu(p_ref[:, :di]) * s_scr[...] * (gg * _sigmoid(gg))).astype(BF16)

    vec = pl.BlockSpec((1, di), lambda i: (0, 0))
    return _pc(body, name="a_mid_fwd", out_shape=jax.ShapeDtypeStruct((m, di), BF16), grid=(m // r,),
               in_specs=[pl.BlockSpec((r, n3), lambda i: (i, 0)), vec, vec,
                         pl.BlockSpec((SG_GROUPS, SG_BLOCK, SG_BLOCK), lambda i: (0, 0, 0)),
                         pl.BlockSpec((SG_BLOCK, 128), lambda i: (0, 0))],
               out_specs=pl.BlockSpec((r, di), lambda i: (i, 0)),
               scratch=[pltpu.VMEM((r, di), F32)], sem=("parallel",), comm=comm)(proj, ln_g, ln_b, w_s, bs_t)


def _a_mid_bwd(proj, dybr, ln_g, ln_b, w_s, bs_t, t_seq, comm=None):
    m, n3 = proj.shape
    di = n3 // 3
    gd = di // SG_GROUPS
    r = _tile(t_seq, 256)
    nblk = r // SG_BLOCK

    def body(p_ref, dy_ref, lg_ref, lb_ref, ws_ref, bs_ref,
             dp_ref, dlg_ref, dlb_ref, dws_ref, dbs_ref, s_scr, dvl_scr):
        i = pl.program_id(0)

        @pl.when(i == 0)
        def _():
            dlg_ref[...] = jnp.zeros_like(dlg_ref)
            dlb_ref[...] = jnp.zeros_like(dlb_ref)
            dws_ref[...] = jnp.zeros_like(dws_ref)
            dbs_ref[...] = jnp.zeros_like(dbs_ref)

        v, dgelu_v = _gelu_and_grad(p_ref[:, di:2 * di])
        mu = jnp.mean(v, axis=-1, keepdims=True)
        vc = v - mu
        rstd = lax.rsqrt(jnp.mean(vc * vc, axis=-1, keepdims=True) + EPS)
        vhat = vc * rstd
        lg = lg_ref[...]
        vb = (vhat * lg + lb_ref[...]).astype(BF16)
        u, dgelu_u = _gelu_and_grad(p_ref[:, :di])
        gg = p_ref[:, 2 * di:]
        sg = _sigmoid(gg)
        dyv = dy_ref[...]
        dus = dyv * (gg * sg)
        dsb = (dus * u).astype(BF16)
        ds32 = dus * u
        mask = _sgu_mask()
        lane = lax.broadcasted_iota(jnp.int32, (SG_BLOCK, 128), 1)
        dbs_acc = jnp.zeros((SG_BLOCK, 128), F32)
        for gi in range(SG_GROUPS):
            ws = jnp.where(mask, ws_ref[gi], 0.0).astype(BF16)
            bcol = bs_ref[:, gi:gi + 1]
            cols = slice(gi * gd, (gi + 1) * gd)
            dws_acc = jnp.zeros((SG_BLOCK, SG_BLOCK), F32)
            dbs_col = jnp.zeros((SG_BLOCK, 1), F32)
            for b in range(nblk):
                rows = slice(b * SG_BLOCK, (b + 1) * SG_BLOCK)
                s_scr[rows, cols] = _dot(ws, vb[rows, cols]) + bcol
                dvl_scr[rows, cols] = _dot_tn(ws, dsb[rows, cols])
                dws_acc += _dot_nt(dsb[rows, cols], vb[rows, cols])
                dbs_col += jnp.sum(ds32[rows, cols], axis=-1, keepdims=True)
            dws_ref[gi] += jnp.where(mask, dws_acc, 0.0)
            dbs_acc += jnp.where(lane == gi, dbs_col, 0.0)
        dbs_ref[...] += dbs_acc
        s = s_scr[...]
        dp_ref[:, :di] = (dus * s * dgelu_u).astype(BF16)
        dp_ref[:, 2 * di:] = (dyv * u * s * (sg * (1.0 + gg * (1.0 - sg)))).astype(BF16)
        dvl = dvl_scr[...]
        dlg_ref[...] += jnp.sum(dvl * vhat, axis=0, keepdims=True)
        dlb_ref[...] += jnp.sum(dvl, axis=0, keepdims=True)
        dvh = dvl * lg
        dv = rstd * (dvh - jnp.mean(dvh, axis=-1, keepdims=True)
                     - vhat * jnp.mean(dvh * vhat, axis=-1, keepdims=True))
        dp_ref[:, di:2 * di] = (dv * dgelu_v).astype(BF16)

    vec = pl.BlockSpec((1, di), lambda i: (0, 0))
    wsb = pl.BlockSpec((SG_GROUPS, SG_BLOCK, SG_BLOCK), lambda i: (0, 0, 0))
    bsb = pl.BlockSpec((SG_BLOCK, 128), lambda i: (0, 0))
    return _pc(body, name="a_mid_bwd",
               out_shape=[jax.ShapeDtypeStruct((m, n3), BF16), jax.ShapeDtypeStruct((1, di), F32),
                          jax.ShapeDtypeStruct((1, di), F32),
                          jax.ShapeDtypeStruct((SG_GROUPS, SG_BLOCK, SG_BLOCK), F32),
                          jax.ShapeDtypeStruct((SG_BLOCK, 128), F32)],
               grid=(m // r,),
               in_specs=[pl.BlockSpec((r, n3), lambda i: (i, 0)), pl.BlockSpec((r, di), lambda i: (i, 0)),
                         vec, vec, wsb, bsb],
               out_specs=[pl.BlockSpec((r, n3), lambda i: (i, 0)), vec, vec, wsb, bsb],
               scratch=[pltpu.VMEM((r, di), F32), pltpu.VMEM((r, di), F32)],
               sem=("arbitrary",), comm=comm)(proj, dybr, ln_g, ln_b, w_s, bs_t)


def _chunk_rows(n):
    if isinstance(n, int):
        return pl.ds(n * CHUNK, CHUNK)
    return pl.ds(pl.multiple_of(n * CHUNK, CHUNK), CHUNK)


def _hgrn_dims(t_seq, di):
    tr = _tile(t_seq, 128)
    hc = _tile(di, 2048)
    return tr, hc, hc // HEAD_DIM


def _hgrn_gates(f_ref, lb, a_scr, k_scr, tr):
    sig = _sigmoid(f_ref[...])
    fg = lb + (1.0 - lb) * sig
    k_scr[...] = 1.0 - fg
    logf = jnp.log(fg)
    g = min(CUM_ROWS, tr)
    tri = _tri_mask(g, reverse=False)
    for rg in range(tr // g):
        a_scr[rg * g:(rg + 1) * g, :] = _tri_apply(tri, logf[rg * g:(rg + 1) * g, :])
    return sig, fg


def _hgrn_fwd(proj, lbj, gn, nb, t_seq):
    _, m, di = proj.shape
    tr, hc, hpg = _hgrn_dims(t_seq, di)
    nt, nhg, ncl = t_seq // tr, di // hc, tr // CHUNK
    nheads = di // HEAD_DIM

    def body(p_ref, lb_ref, gn_ref, o_ref, ybr_ref, st_ref, st_scr, a_scr, k_scr):
        q_ref, f_ref, i_ref, g_ref = (p_ref.at[s] for s in range(4))
        t = pl.program_id(2)

        @pl.when(t == 0)
        def _():
            st_scr[...] = jnp.zeros_like(st_scr)

        _hgrn_gates(f_ref, lb_ref[0:1, :], a_scr, k_scr, tr)
        gnv = gn_ref[...]
        rr = lax.broadcasted_iota(jnp.int32, (CHUNK, CHUNK), 0)
        cc = lax.broadcasted_iota(jnp.int32, (CHUNK, CHUNK), 1)
        causal = cc <= rr

        def chunk(n, carry):
            rows = _chunk_rows(n)
            lanes = [slice(hd * HEAD_DIM, (hd + 1) * HEAD_DIM) for hd in range(hpg)]
            hs = []
            for hd, ls in enumerate(lanes):
                h = {}
                ah, kh = a_scr[rows, ls], k_scr[rows, ls]
                qp = q_ref[rows, ls]
                qh = qp * _sigmoid(qp)
                h["vb"] = i_ref[rows, ls].astype(BF16)
                aref, alast = ah[CHUNK // 2 - 1:CHUNK // 2, :], ah[CHUNK - 1:CHUNK, :]
                h["q_in"] = (qh * jnp.exp(ah - aref)).astype(BF16)
                h["k_in"] = (kh * jnp.exp(aref - ah)).astype(BF16)
                h["q_out"] = (qh * jnp.exp(ah)).astype(BF16)
                h["k_out"] = (kh * jnp.exp(alast - ah)).astype(BF16)
                h["dec"] = jnp.exp(alast)
                st = st_scr[hd]
                st_ref[n, hd] = st
                h["st"] = st
                hs.append(h)
            for h in hs:
                h["scores"] = _dot_nt(h["q_in"], h["k_in"])
                h["o_inter"] = _dot_nt(h["q_out"], h["st"].astype(BF16))
                h["st_mm"] = _dot_tn(h["vb"], h["k_out"])
            for h in hs:
                h["o"] = _dot(jnp.where(causal, h["scores"], 0.0).astype(BF16), h["vb"]) + h["o_inter"]
            for hd, (h, ls) in enumerate(zip(hs, lanes)):
                st_scr[hd] = h["st"] * h["dec"] + h["st_mm"]
                o = h["o"]
                o_ref[rows, ls] = o
                rstd = lax.rsqrt(jnp.mean(o * o, axis=-1, keepdims=True) + EPS)
                gg = g_ref[rows, ls]
                ybr_ref[rows, ls] = ((o * rstd * gnv) * (gg * _sigmoid(gg))).astype(BF16)
            return carry

        lax.fori_loop(0, ncl, chunk, 0)

    blk = pl.BlockSpec((tr, hc), lambda hg, b, t: (b * nt + t, hg))
    return _pc(body, name="hgrn_fwd",
               out_shape=[jax.ShapeDtypeStruct((m, di), F32), jax.ShapeDtypeStruct((m, di), BF16),
                          jax.ShapeDtypeStruct((m // CHUNK, nheads, HEAD_DIM, HEAD_DIM), F32)],
               grid=(nhg, nb, nt),
               in_specs=[pl.BlockSpec((4, tr, hc), lambda hg, b, t: (0, b * nt + t, hg)),
                         pl.BlockSpec((2, hc), lambda hg, b, t: (0, hg)),
                         pl.BlockSpec((1, HEAD_DIM), lambda hg, b, t: (0, 0))],
               out_specs=[blk, blk, pl.BlockSpec((ncl, hpg, HEAD_DIM, HEAD_DIM),
                                                 lambda hg, b, t: (b * nt + t, hg, 0, 0))],
               scratch=[pltpu.VMEM((hpg, HEAD_DIM, HEAD_DIM), F32), pltpu.VMEM((tr, hc), F32),
                        pltpu.VMEM((tr, hc), F32)],
               sem=("parallel", "arbitrary", "arbitrary"))(proj, lbj, gn)


def _hgrn_bwd(proj, o_all, dybr, states, lbj, gn, nb, t_seq, comm=None):
    _, m, di = proj.shape
    tr, hc, hpg = _hgrn_dims(t_seq, di)
    nt, nhg, ncl = t_seq // tr, di // hc, tr // CHUNK

    def body(p_ref, o_ref, dy_ref, st_ref, lb_ref, gn_ref,
             dp_ref, dlb_ref, dgn_ref, dst_scr, a_scr, k_scr, da_scr, dk_scr):
        q_ref, f_ref, i_ref, g_ref = (p_ref.at[s] for s in range(4))
        hg, b, t = pl.program_id(0), pl.program_id(1), pl.program_id(2)

        @pl.when(t == 0)
        def _():
            dst_scr[...] = jnp.zeros_like(dst_scr)

        @pl.when((b == 0) & (t == 0))
        def _():
            dlb_ref[...] = jnp.zeros_like(dlb_ref)

        @pl.when((hg == 0) & (b == 0) & (t == 0))
        def _():
            dgn_ref[...] = jnp.zeros_like(dgn_ref)

        lb = lb_ref[0:1, :]
        sig, fg = _hgrn_gates(f_ref, lb, a_scr, k_scr, tr)
        gnv = gn_ref[...]
        rr = lax.broadcasted_iota(jnp.int32, (CHUNK, CHUNK), 0)
        cc = lax.broadcasted_iota(jnp.int32, (CHUNK, CHUNK), 1)
        causal = cc <= rr
        rowi = lax.broadcasted_iota(jnp.int32, (CHUNK, HEAD_DIM), 0)

        def chunk(it, carry):
            n = ncl - 1 - it
            rows = _chunk_rows(n)
            for hd0 in range(0, hpg, PHASE_HEADS):
                heads(n, rows, range(hd0, min(hpg, hd0 + PHASE_HEADS)))
            return carry

        def heads(n, rows, ids):
            lanes = [slice(hd * HEAD_DIM, (hd + 1) * HEAD_DIM) for hd in ids]
            hs = []
            for hd, ls in zip(ids, lanes):
                h = {}
                ah, kh = a_scr[rows, ls], k_scr[rows, ls]
                qp = q_ref[rows, ls]
                sq = _sigmoid(qp)
                qh = qp * sq
                h["dsilu_q"] = sq * (1.0 + qp * (1.0 - sq))
                h["vb"] = i_ref[rows, ls].astype(BF16)
                aref, alast = ah[CHUNK // 2 - 1:CHUNK // 2, :], ah[CHUNK - 1:CHUNK, :]
                h["e1"], h["e2"] = jnp.exp(ah - aref), jnp.exp(aref - ah)
                h["e3"], h["e4"] = jnp.exp(ah), jnp.exp(alast - ah)
                h["dec"] = jnp.exp(alast)
                h["q_in"], h["k_in"], h["q_out"], h["k_out"] = qh * h["e1"], kh * h["e2"], qh * h["e3"], kh * h["e4"]
                for nm in ("q_in", "k_in", "q_out", "k_out"):
                    h[nm + "_b"] = h[nm].astype(BF16)
                o = o_ref[rows, ls]
                rstd = lax.rsqrt(jnp.mean(o * o, axis=-1, keepdims=True) + EPS)
                ohat = o * rstd
                gg = g_ref[rows, ls]
                sg = _sigmoid(gg)
                dyv = dy_ref[rows, ls]
                d_on = dyv * (gg * sg)
                dp_ref[3, rows, ls] = (dyv * (ohat * gnv) * (sg * (1.0 + gg * (1.0 - sg)))).astype(BF16)
                h["dgn"] = jnp.sum(d_on * ohat, axis=0, keepdims=True)
                dohat = d_on * gnv
                do = rstd * (dohat - ohat * jnp.mean(dohat * ohat, axis=-1, keepdims=True))
                h["do_b"] = do.astype(BF16)
                h["st_prev"] = st_ref[n, hd]
                h["dst"] = dst_scr[hd]
                hs.append(h)
            for h in hs:
                dst_b = h["dst"].astype(BF16)
                h["scores"] = _dot_nt(h["q_in_b"], h["k_in_b"])
                h["dscores"] = _dot_nt(h["do_b"], h["vb"])
                h["dv_inter"] = _dot_nt(h["k_out_b"], dst_b)
                h["dq_out"] = _dot(h["do_b"], h["st_prev"].astype(BF16))
                h["dk_out"] = _dot(h["vb"], dst_b)
                h["dst_mm"] = _dot_tn(h["do_b"], h["q_out_b"])
            for h in hs:
                scores = jnp.where(causal, h["scores"], 0.0).astype(BF16)
                dscores = jnp.where(causal, h["dscores"], 0.0).astype(BF16)
                h["dv"] = _dot_tn(scores, h["do_b"]) + h["dv_inter"]
                h["dq_in"] = _dot(dscores, h["k_in_b"])
                h["dk_in"] = _dot_tn(dscores, h["q_in_b"])
            dgn = hs[0]["dgn"]
            for h in hs[1:]:
                dgn = dgn + h["dgn"]
            dgn_ref[...] += dgn
            for hd, h, ls in zip(ids, hs, lanes):
                ddec = jnp.sum(h["dst"] * h["st_prev"], axis=0, keepdims=True)
                dst_scr[hd] = h["dst"] * h["dec"] + h["dst_mm"]
                dp_ref[2, rows, ls] = h["dv"].astype(BF16)
                dq = h["dq_in"] * h["e1"] + h["dq_out"] * h["e3"]
                dp_ref[0, rows, ls] = (dq * h["dsilu_q"]).astype(BF16)
                dk_scr[rows, ls] = h["dk_in"] * h["e2"] + h["dk_out"] * h["e4"]
                t_in = h["dq_in"] * h["q_in"] - h["dk_in"] * h["k_in"]
                t_out = h["dk_out"] * h["k_out"]
                da = t_in + h["dq_out"] * h["q_out"] - t_out
                da_ref_row = -jnp.sum(t_in, axis=0, keepdims=True)
                da_last_row = jnp.sum(t_out, axis=0, keepdims=True) + ddec * h["dec"]
                da = da + jnp.where(rowi == CHUNK // 2 - 1, da_ref_row, 0.0) \
                        + jnp.where(rowi == CHUNK - 1, da_last_row, 0.0)
                da_scr[rows, ls] = da

        if ncl <= 2:
            for it in range(ncl):
                chunk(it, 0)
        else:
            lax.fori_loop(0, ncl, chunk, 0)
        g = min(CUM_ROWS, tr)
        tri = _tri_mask(g, reverse=True)
        for rg in range(tr // g):
            rs = slice(rg * g, (rg + 1) * g)
            dlogf = _tri_apply(tri, da_scr[rs, :])
            df = dlogf / fg[rs, :] - dk_scr[rs, :]
            sgr = sig[rs, :]
            dp_ref[1, rs, :] = (df * (1.0 - lb) * (sgr * (1.0 - sgr))).astype(BF16)
            dlb_ref[...] += jnp.sum(df * (1.0 - sgr), axis=0, keepdims=True) * lb_ref[1:2, :]

    blk = pl.BlockSpec((tr, hc), lambda hg, b, t: (b * nt + (nt - 1 - t), hg))
    return _pc(body, name="hgrn_bwd",
               out_shape=[jax.ShapeDtypeStruct((4, m, di), BF16), jax.ShapeDtypeStruct((1, di), F32),
                          jax.ShapeDtypeStruct((1, HEAD_DIM), F32)],
               grid=(nhg, nb, nt),
               in_specs=[pl.BlockSpec((4, tr, hc), lambda hg, b, t: (0, b * nt + (nt - 1 - t), hg)), blk, blk,
                         pl.BlockSpec((ncl, hpg, HEAD_DIM, HEAD_DIM),
                                      lambda hg, b, t: (b * nt + (nt - 1 - t), hg, 0, 0)),
                         pl.BlockSpec((2, hc), lambda hg, b, t: (0, hg)),
                         pl.BlockSpec((1, HEAD_DIM), lambda hg, b, t: (0, 0))],
               out_specs=[pl.BlockSpec((4, tr, hc), lambda hg, b, t: (0, b * nt + (nt - 1 - t), hg)),
                          pl.BlockSpec((1, hc), lambda hg, b, t: (0, hg)),
                          pl.BlockSpec((1, HEAD_DIM), lambda hg, b, t: (0, 0))],
               scratch=[pltpu.VMEM((hpg, HEAD_DIM, HEAD_DIM), F32)] + [pltpu.VMEM((tr, hc), F32)] * 4,
               sem=("arbitrary", "arbitrary", "arbitrary"), comm=comm)(
                   proj, o_all, dybr, states, lbj, gn)


def _adamw(parts, w, m, v, name):
    r, c = w.shape
    tr = _tile(r, 256)
    npart = len(parts)
    c1 = 1.0 - ADAM_B1 ** ADAM_STEP
    c2 = 1.0 - ADAM_B2 ** ADAM_STEP

    def body(*refs):
        p_refs = refs[:npart]
        _adamw_math(p_refs, *refs[npart:], c1, c2)

    blk = pl.BlockSpec((tr, c), lambda i: (i, 0))
    return _pc(body, name=name, out_shape=[jax.ShapeDtypeStruct((r, c), F32)] * 4, grid=(r // tr,),
               in_specs=[blk] * (npart + 3), out_specs=[blk] * 4, sem=("parallel",))(*parts, w, m, v)


def _adamw_math(p_refs, w_ref, m_ref, v_ref, g_ref, d_ref, nm_ref, nv_ref, c1, c2):
    g = p_refs[0][...].astype(F32)
    for p in p_refs[1:]:
        g = g + p[...].astype(F32)
    nm = ADAM_B1 * m_ref[...] + (1.0 - ADAM_B1) * g
    nv = ADAM_B2 * v_ref[...] + (1.0 - ADAM_B2) * (g * g)
    g_ref[...] = g
    nm_ref[...] = nm
    nv_ref[...] = nv
    d_ref[...] = -ADAM_LR * ((nm / c1) / (jnp.sqrt(nv / c2) + ADAM_EPS) + ADAM_WD * w_ref[...])


def _adamw_blocks(parts, idx, w, m, v, name):
    r, c = w.shape
    tr = _tile(r, 256)
    npart = len(parts)
    c1 = 1.0 - ADAM_B1 ** ADAM_STEP
    c2 = 1.0 - ADAM_B2 ** ADAM_STEP

    def body(idx_ref, *refs):
        _adamw_math(refs[:npart], *refs[npart:], c1, c2)

    def sel(p):
        return pl.BlockSpec((None, tr, c), lambda i, s: (s[p], i, 0))

    blk = pl.BlockSpec((tr, c), lambda i, s: (i, 0))
    gs = pltpu.PrefetchScalarGridSpec(num_scalar_prefetch=1, grid=(r // tr,),
                                      in_specs=[sel(p) for p in range(npart)] + [blk] * 3, out_specs=[blk] * 4)
    return _pc(body, name=name, out_shape=[jax.ShapeDtypeStruct((r, c), F32)] * 4, grid_spec=gs,
               sem=("parallel",))(idx, *parts, w, m, v)


_EARLY = ["a_ln_gain", "a_ln_bias", "a_w_s", "a_b_s", "b_lower_bounds", "b_gn_gain"]


def _pack(arrs):
    flat = jnp.concatenate([a.reshape(-1) for a in arrs])
    rows = -(-flat.shape[0] // 1024) * 8
    return jnp.pad(flat, (0, rows * 128 - flat.shape[0])).reshape(rows, 128)


def _unpack(buf, like):
    flat = buf.reshape(-1)
    out, off = [], 0
    for a in like:
        out.append(flat[off:off + a.size].reshape(a.shape))
        off += a.size
    return out


def kernel(x, c, norm_gain, w_ada, b_ada, a_w_in, a_ln_gain, a_ln_bias, a_w_s, a_b_s, a_w_out, b_w_in, b_lower_bounds, b_gn_gain, b_w_out, final_gain, loss_target, m_norm_gain, m_w_ada, m_b_ada, m_a_w_in, m_a_ln_gain, m_a_ln_bias, m_a_w_s, m_a_b_s, m_a_w_out, m_b_w_in, m_b_lower_bounds, m_b_gn_gain, m_b_w_out, m_final_gain, v_norm_gain, v_w_ada, v_b_ada, v_a_w_in, v_a_ln_gain, v_a_ln_bias, v_a_w_s, v_a_b_s, v_a_w_out, v_b_w_in, v_b_lower_bounds, v_b_gn_gain, v_b_w_out, v_final_gain):
    w = dict(norm_gain=norm_gain, w_ada=w_ada, b_ada=b_ada, a_w_in=a_w_in, a_ln_gain=a_ln_gain,
             a_ln_bias=a_ln_bias, a_w_s=a_w_s, a_b_s=a_b_s, a_w_out=a_w_out, b_w_in=b_w_in,
             b_lower_bounds=b_lower_bounds, b_gn_gain=b_gn_gain, b_w_out=b_w_out, final_gain=final_gain)
    mo = dict(norm_gain=m_norm_gain, w_ada=m_w_ada, b_ada=m_b_ada, a_w_in=m_a_w_in, a_ln_gain=m_a_ln_gain,
              a_ln_bias=m_a_ln_bias, a_w_s=m_a_w_s, a_b_s=m_a_b_s, a_w_out=m_a_w_out, b_w_in=m_b_w_in,
              b_lower_bounds=m_b_lower_bounds, b_gn_gain=m_b_gn_gain, b_w_out=m_b_w_out, final_gain=m_final_gain)
    vo = dict(norm_gain=v_norm_gain, w_ada=v_w_ada, b_ada=v_b_ada, a_w_in=v_a_w_in, a_ln_gain=v_a_ln_gain,
              a_ln_bias=v_a_ln_bias, a_w_s=v_a_w_s, a_b_s=v_a_b_s, a_w_out=v_a_w_out, b_w_in=v_b_w_in,
              b_lower_bounds=v_b_lower_bounds, b_gn_gain=v_b_gn_gain, b_w_out=v_b_w_out, final_gain=v_final_gain)

    nb, t_seq, d = x.shape
    m = nb * t_seq
    ncol_ada = w_ada.shape[2]
    xi, yi, ci = lax.axis_index("x"), lax.axis_index("y"), lax.axis_index("c")
    me = 4 * xi + 2 * yi + ci

    c_g, wa_in_g = _all_gather([c, a_w_in[0].astype(BF16)], "gather_c_wa")

    c_all = c_g.reshape(NDEV * nb, d)
    b_cols = lax.dynamic_slice(b_ada, (0, me * ncol_ada), (2, ncol_ada)).reshape(2, 1, ncol_ada)
    mod_part, lbj = _ada_fwd(c_all, w_ada, b_cols, b_lower_bounds)
    mod_all = _all_gather([mod_part], "gather_mod")[0]
    mod_mine = lax.dynamic_slice_in_dim(mod_all, me * nb, nb, axis=2)
    mod_mine = mod_mine.transpose(1, 2, 0, 3).reshape(2, nb, 3, d)
    mod0, mod1 = mod_mine[0], mod_mine[1]

    di = a_w_out.shape[1] * NDEV

    xf = x.reshape(m, d)
    tgt = loss_target.reshape(m, d)
    ng0, ng1 = norm_gain[0:1], norm_gain[1:2]
    ncb = b_w_in.shape[2]
    wb_lo, wb_hi = b_w_in[0][:, :ncb // 2].astype(BF16), b_w_in[0][:, ncb // 2:].astype(BF16)
    h0, h0_t = _prenorm(xf, ng0, mod0, t_seq, "prenorm_a")
    proj_a, half = _mm_in(h0, [wa_in_g], 1, "in_proj_a", comm=_gather_first([a_w_out[0].astype(BF16), wb_lo]))
    bs_t = jnp.pad(a_b_s[0].T, ((0, 0), (0, 128 - SG_GROUPS)))
    ybr_a, (wa_out_g, wb_lo_g, wb_hi_half) = _a_mid_fwd(
        proj_a, a_ln_gain, a_ln_bias, a_w_s[0], bs_t, t_seq, comm=_join(_gather_second(half), _gather_first([wb_hi])))
    wa_out = wa_out_g.reshape(di, d)
    (yout_a, x1), (wb_hi_g, wb_out_half) = _out_proj(
        ybr_a, wa_out, xf, mod0, t_seq, "out_proj_a",
        comm=_join(_gather_second([wb_hi_half]), _gather_first([b_w_out[0].astype(BF16)])))
    wb_in_g = [wb_lo_g, wb_hi_g]
    h1, h1_t = _prenorm(x1, ng1, mod1, t_seq, "prenorm_b")
    proj_b, (wb_out_g,) = _mm_in(h1, wb_in_g, 4, "in_proj_b", comm=_gather_second([wb_out_half]))
    wb_out = wb_out_g.reshape(di, d)
    o_b, ybr_b, states = _hgrn_fwd(proj_b, lbj, b_gn_gain, nb, t_seq)
    yout_b, dx2, loss_part, d_final_gain = _out_proj_loss(ybr_b, wb_out, x1, mod1, final_gain.reshape(1, d), tgt, t_seq)

    rows_out = a_w_out.shape[1]
    dy_b, dgate1, dybr_b = _gate_dybr(dx2, yout_b, mod1, wb_out, t_seq, "dybr_b")
    rs_wb_out = _ReduceScatter(_mm_dw_out(ybr_b, dy_b, "dw_out_b").reshape(NDEV, rows_out, d), "b_w_out")
    (dproj_b, d_lb, d_gn), got = _hgrn_bwd(proj_b, o_b, dybr_b, states, lbj, b_gn_gain, nb, t_seq,
                                           comm=rs_wb_out.swap_core())
    rs_wb_out.after_core(got[0])
    dh1, got = _mm_din(dproj_b, wb_in_g, 4, "dh_b", comm=rs_wb_out.swap_chips())
    rs_wb_out.after_chips(got[0])
    dx1, dss1, dgain1 = _prenorm_bwd(dh1, x1, ng1, mod1, dx2, t_seq, "prenorm_bwd_b")
    rs_wb_in = _ReduceScatter(_mm_dw_in(h1_t, dproj_b, ncb, 4, "dw_in_b"), "b_w_in")

    dy_a, dgate0, dybr_a = _gate_dybr(dx1, yout_a, mod0, wa_out, t_seq, "dybr_a")
    g_wa_out, got = _mm_dw_out(ybr_a, dy_a, "dw_out_a", comm=rs_wb_in.swap_core())
    rs_wb_in.after_core(got[0])
    rs_wa_out = _ReduceScatter(g_wa_out.reshape(NDEV, rows_out, d), "a_w_out")
    (dproj_a, d_lng, d_lnb, d_ws, d_bs_t), got = _a_mid_bwd(
        proj_a, dybr_a, a_ln_gain, a_ln_bias, a_w_s[0], bs_t, t_seq,
        comm=_join(rs_wb_in.swap_chips(), rs_wa_out.swap_core()))
    rs_wb_in.after_chips(got[0])
    rs_wa_out.after_core(got[1])
    part = dict(a_ln_gain=d_lng, a_ln_bias=d_lnb, a_w_s=d_ws[None], a_b_s=d_bs_t[:, :SG_GROUPS].T[None],
                b_lower_bounds=jnp.concatenate([-d_lb, d_lb], axis=0), b_gn_gain=d_gn)
    early_pack = _pack([part[k].reshape(w[k].shape) for k in _EARLY])
    g_wa_in, got = _mm_dw_in(h0_t, dproj_a, wa_in_g.shape[2], 1, "dw_in_a",
                             comm=_join(rs_wa_out.swap_chips(), _gather_first([early_pack])))
    rs_wa_out.after_chips(got[0])
    rs_wa_in = _ReduceScatter(g_wa_in, "a_w_in")
    n_tiles = m // _din_tile(m)
    assert n_tiles >= 2
    first_tiles = max(1, (3 * n_tiles) // 8)
    dh0, got2 = _mm_din(dproj_a, [wa_in_g], 1, "dh_a_first", tiles=(0, first_tiles),
                        comm=_join(rs_wa_in.swap_core(), _gather_second([got[1]])))
    rs_wa_in.after_core(got2[0])
    early_all = got2[1]
    dh0, got = _mm_din(dproj_a, [wa_in_g], 1, "dh_a_rest", comm=rs_wa_in.swap_chips(),
                       tiles=(first_tiles, n_tiles - first_tiles), prev=dh0)
    rs_wa_in.after_chips(got[0])
    dx0, dss0, dgain0 = _prenorm_bwd(dh0, xf, ng0, mod0, dx1, t_seq, "prenorm_bwd_a")
    grad_x = dx0.reshape(nb, t_seq, d)

    dmod = jnp.stack([jnp.concatenate([dss0, dgate0], axis=1), jnp.concatenate([dss1, dgate1], axis=1)])
    late_like = [norm_gain, final_gain, loss_part.reshape(1)]
    late_pack = _pack([jnp.concatenate([dgain0, dgain1], axis=0), d_final_gain[0], loss_part.reshape(1)])
    dmod_all, late_all = _all_gather([dmod.reshape(2, nb, 3 * d), late_pack], "gather_tail")
    dmod_all = dmod_all.transpose(1, 0, 2, 3).reshape(2, NDEV * nb, 3 * d)
    dmod_cols = lax.dynamic_slice_in_dim(dmod_all, me * ncol_ada, ncol_ada, axis=2)
    g_w_ada, g_b_ada = _ada_bwd(c_all, dmod_cols, dmod_all)

    res = {}
    early_like = [w[k] for k in _EARLY]
    dev_order = jnp.arange(NDEV, dtype=jnp.int32)
    sm = _adamw_blocks([early_all] * NDEV, dev_order, _pack(early_like), _pack([mo[k] for k in _EARLY]),
                       _pack([vo[k] for k in _EARLY]), "adamw_small_early")
    sm = [dict(zip(_EARLY, _unpack(buf, early_like))) for buf in sm]
    for k in _EARLY:
        res[k] = tuple(s[k] for s in sm)
    zero = jnp.zeros((1,), F32)
    sm = _adamw_blocks([late_all] * NDEV, dev_order, _pack([norm_gain, final_gain, zero]),
                       _pack([mo["norm_gain"], mo["final_gain"], zero]),
                       _pack([vo["norm_gain"], vo["final_gain"], zero]), "adamw_small_late")
    sm = [_unpack(buf, late_like) for buf in sm]
    res["norm_gain"] = tuple(s[0] for s in sm)
    res["final_gain"] = tuple(s[1] for s in sm)
    loss = sm[0][2][0]
    rb = _adamw([g_b_ada], b_ada, mo["b_ada"], vo["b_ada"], "adamw_b_ada")
    res["b_ada"] = tuple(rb)
    sh = w_ada.shape
    ra = _adamw([g_w_ada.reshape(sh[0] * sh[1], sh[2])], w_ada.reshape(sh[0] * sh[1], sh[2]),
                mo["w_ada"].reshape(sh[0] * sh[1], sh[2]), vo["w_ada"].reshape(sh[0] * sh[1], sh[2]), "adamw_w_ada")
    res["w_ada"] = tuple(z.reshape(sh) for z in ra)

    for k, rs in (("b_w_out", rs_wb_out), ("b_w_in", rs_wb_in), ("a_w_out", rs_wa_out), ("a_w_in", rs_wa_in)):
        res[k] = tuple(z[None] for z in _adamw_blocks(rs.parts, rs.idx, w[k][0], mo[k][0], vo[k][0], "adamw_" + k))

    order = ["norm_gain", "w_ada", "b_ada", "a_w_in", "a_ln_gain", "a_ln_bias", "a_w_s", "a_b_s", "a_w_out",
             "b_w_in", "b_lower_bounds", "b_gn_gain", "b_w_out", "final_gain"]
    return (loss, grad_x, *[res[k][0] for k in order], *[res[k][1] for k in order],
            *[res[k][2] for k in order], *[res[k][3] for k in order])
```

```python
import functools
import math

import jax
import jax.numpy as jnp
from jax import lax
from jax.experimental import pallas as pl
from jax.experimental.pallas import tpu as pltpu

F32 = jnp.float32
BF16 = jnp.bfloat16
MESH = pl.DeviceIdType.MESH
NDEV = 8
EPS = 1e-6
CHUNK = 64
SG_BLOCK = 128
SG_GROUPS = 8
HEAD_DIM = 128
CUM_ROWS = 256
PHASE_HEADS = 8
ADAM_LR, ADAM_B1, ADAM_B2, ADAM_EPS, ADAM_WD, ADAM_STEP = 0.001, 0.9, 0.999, 1e-08, 0.01, 10
VMEM_LIMIT = 56 * 1024 * 1024
ANY = pl.BlockSpec(memory_space=pl.ANY)


class _Hosted:
    def __init__(self, arrays, out_shapes, nsem, start, finish, aliases=None):
        self.arrays, self.out_shapes, self.nsem = list(arrays), list(out_shapes), nsem
        self.start, self.finish = start, finish
        self.aliases = dict(aliases or {})


def _join(*comms):
    arrays, outs, aliases, offs, nsem = [], [], {}, [], 0
    for cm in comms:
        offs.append((len(arrays), len(outs), nsem))
        for i, o in cm.aliases.items():
            aliases[len(arrays) + i] = len(outs) + o
        arrays += cm.arrays
        outs += cm.out_shapes
        nsem += cm.nsem

    def run(which):
        def f(ins, outs_, ss, rs, base):
            for cm, (ia, io, isem) in zip(comms, offs):
                getattr(cm, which)(ins[ia:ia + len(cm.arrays)], outs_[io:io + len(cm.out_shapes)], ss, rs, base + isem)
        return f

    return _Hosted(arrays, outs, nsem, run("start"), run("finish"), aliases)


def _pc(body, *, name, out_shape, grid=None, in_specs=None, out_specs=None, scratch=(), sem=None,
        grid_spec=None, comm=None, aliases=None):
    cp = dict(vmem_limit_bytes=VMEM_LIMIT)
    aliases = dict(aliases or {})
    if comm is None:
        if sem is not None:
            cp["dimension_semantics"] = sem
        kw = {"input_output_aliases": aliases}
        if grid_spec is not None:
            kw["grid_spec"] = grid_spec
        else:
            if grid is not None:
                kw["grid"] = grid
            if in_specs is not None:
                kw["in_specs"] = in_specs
            if out_specs is not None:
                kw["out_specs"] = out_specs
            kw["scratch_shapes"] = list(scratch)
        return pl.pallas_call(functools.partial(body), name=name, out_shape=out_shape,
                              compiler_params=pltpu.CompilerParams(**cp), **kw)

    single = not isinstance(out_shape, (list, tuple))
    outs_list = [out_shape] if single else list(out_shape)
    ospecs = [out_specs] if single else list(out_specs)
    n_in, n_out, n_ci, n_co, n_scr = len(in_specs), len(outs_list), len(comm.arrays), len(comm.out_shapes), len(scratch)
    cp["dimension_semantics"] = ("arbitrary",) * len(grid)

    def hosted(*refs):
        cin, hin = refs[:n_in], refs[n_in:n_in + n_ci]
        cout = refs[n_in + n_ci:n_in + n_ci + n_out]
        hout = refs[n_in + n_ci + n_out:n_in + n_ci + n_out + n_co]
        scr = refs[n_in + n_ci + n_out + n_co:n_in + n_ci + n_out + n_co + n_scr]
        ssem, rsem = refs[-2], refs[-1]
        first = functools.reduce(lambda p, q: p & q, [pl.program_id(a) == 0 for a in range(len(grid))])
        last = functools.reduce(lambda p, q: p & q, [pl.program_id(a) == grid[a] - 1 for a in range(len(grid))])

        @pl.when(first)
        def _():
            comm.start(hin, hout, ssem, rsem, 0)

        body(*cin, *cout, *scr)

        @pl.when(last)
        def _():
            comm.finish(hin, hout, ssem, rsem, 0)

    call = pl.pallas_call(
        hosted, name=name, grid=grid, in_specs=list(in_specs) + [ANY] * n_ci, out_specs=ospecs + [ANY] * n_co,
        out_shape=outs_list + comm.out_shapes,
        scratch_shapes=list(scratch) + [pltpu.SemaphoreType.DMA((comm.nsem,)), pltpu.SemaphoreType.DMA((comm.nsem,))],
        input_output_aliases={**aliases, **{n_in + i: n_out + o for i, o in comm.aliases.items()}},
        compiler_params=pltpu.CompilerParams(**cp))

    def run(*args):
        res = call(*args, *comm.arrays)
        comp = res[:n_out]
        return (comp[0] if single else comp), list(res[n_out:])

    return run


def _tile(n, pref):
    return pref if n % pref == 0 else n


def _sigmoid(x):
    return 1.0 / (1.0 + jnp.exp(-x))


def _gelu(x):
    c = math.sqrt(2.0 / math.pi)
    return 0.5 * x * (1.0 + jnp.tanh(c * (x + 0.044715 * (x * x * x))))


def _gelu_and_grad(x):
    c = math.sqrt(2.0 / math.pi)
    x2 = x * x
    t = jnp.tanh(c * (x + 0.044715 * (x2 * x)))
    half = 0.5 * (1.0 + t)
    return x * half, half + (0.5 * x) * (1.0 - t * t) * (c + (3.0 * 0.044715 * c) * x2)


def _dot(a, b):
    return jnp.dot(a, b, preferred_element_type=F32)


def _dot_nt(a, b):
    return lax.dot_general(a, b, (((1,), (1,)), ((), ())), preferred_element_type=F32)


def _dot_tn(a, b):
    return lax.dot_general(a, b, (((0,), (0,)), ((), ())), preferred_element_type=F32)


def _tri_mask(n, reverse):
    r = lax.broadcasted_iota(jnp.int32, (n, n), 0)
    c = lax.broadcasted_iota(jnp.int32, (n, n), 1)
    same = (r // CHUNK) == (c // CHUNK)
    tri = (c >= r) if reverse else (c <= r)
    return jnp.where(same & tri, 1.0, 0.0).astype(BF16)


def _tri_apply(tri, x):
    hi = x.astype(BF16)
    r1 = x - hi.astype(F32)
    mid = r1.astype(BF16)
    lo = (r1 - mid.astype(F32)).astype(BF16)
    return _dot(tri, hi) + (_dot(tri, mid) + _dot(tri, lo))


def _all_gather(arrs, name):
    n = len(arrs)

    def body(*refs):
        ins, outs = refs[:n], refs[n:2 * n]
        send_sems, recv_sems, local_sems = refs[2 * n:]
        x, y, c = lax.axis_index("x"), lax.axis_index("y"), lax.axis_index("c")
        me, sibling = (x, y, c), (x, y, 1 - c)
        near = (x + c - 2 * x * c, y + (1 - c) - 2 * y * (1 - c))
        far = (x + (1 - c) - 2 * x * (1 - c), y + c - 2 * y * c)
        diag = (1 - x, 1 - y)

        def blk(a, p):
            return outs[a].at[4 * p[0] + 2 * p[1] + p[2]]

        def copy(a, k, block, to, src=None):
            return pltpu.make_async_remote_copy(
                src_ref=blk(a, block) if src is None else src, dst_ref=blk(a, block),
                send_sem=send_sems.at[7 * a + k], recv_sem=recv_sems.at[7 * a + k],
                device_id=to, device_id_type=MESH)

        mine = [pltpu.make_async_copy(ins[a], blk(a, me), local_sems.at[a]) for a in range(n)]
        for m in mine:
            m.start()
        sends = []
        for a in range(n):
            sends += [copy(a, 0, me, sibling, src=ins[a]), copy(a, 1, me, (*near, c), src=ins[a]),
                      copy(a, 2, me, (*far, c), src=ins[a])]
        for cp in sends:
            cp.start()
        for a in range(n):
            copy(a, 1, (*near, c), me).wait_recv()
            sends.append(copy(a, 3, (*near, c), (*far, c)))
            sends[-1].start()
        for a in range(n):
            sends.append(copy(a, 4, (*near, c), sibling))
            sends[-1].start()
            copy(a, 2, (*far, c), me).wait_recv()
            sends.append(copy(a, 5, (*far, c), sibling))
            sends[-1].start()
        for a in range(n):
            copy(a, 3, (*diag, c), me).wait_recv()
            sends.append(copy(a, 6, (*diag, c), sibling))
            sends[-1].start()
        for a in range(n):
            copy(a, 0, sibling, me).wait_recv()
            copy(a, 4, (*far, 1 - c), me).wait_recv()
            copy(a, 5, (*near, 1 - c), me).wait_recv()
            copy(a, 6, (*diag, 1 - c), me).wait_recv()
        for cp in sends:
            cp.wait_send()
        for m in mine:
            m.wait()

    out_shape = [jax.ShapeDtypeStruct((NDEV,) + a.shape, a.dtype) for a in arrs]
    return _pc(body, name=name, out_shape=out_shape, in_specs=[ANY] * n, out_specs=[ANY] * n,
               scratch=[pltpu.SemaphoreType.DMA((7 * n,)), pltpu.SemaphoreType.DMA((7 * n,)),
                        pltpu.SemaphoreType.DMA((n,))])(*arrs)


def _gather_first(arrs):
    n = len(arrs)

    def parts(ins, outs, ss, rs, base):
        x, y, c = lax.axis_index("x"), lax.axis_index("y"), lax.axis_index("c")
        me, sibling = (x, y, c), (x, y, 1 - c)
        chips = [(1 - x, y), (x, 1 - y), (1 - x, 1 - y)]

        def blk(a, p):
            return outs[a].at[4 * p[0] + 2 * p[1] + p[2]]

        def copy(a, k, block, to):
            return pltpu.make_async_remote_copy(
                src_ref=ins[a], dst_ref=blk(a, block), send_sem=ss.at[base + 4 * a + k],
                recv_sem=rs.at[base + 4 * a + k], device_id=to, device_id_type=MESH)

        local = [pltpu.make_async_copy(ins[a], blk(a, me), ss.at[base + 4 * n + a]) for a in range(n)]
        sends, recvs = [], []
        for a in range(n):
            sends.append(copy(a, 0, me, sibling))
            recvs.append(copy(a, 0, sibling, me))
            for j, chip in enumerate(chips):
                sends.append(copy(a, 1 + j, me, (*chip, c)))
                recvs.append(copy(a, 1 + j, (*chip, c), me))
        return local, sends, recvs

    def start(ins, outs, ss, rs, base):
        local, sends, _ = parts(ins, outs, ss, rs, base)
        for cp in local + sends:
            cp.start()

    def finish(ins, outs, ss, rs, base):
        local, sends, recvs = parts(ins, outs, ss, rs, base)
        for cp in recvs:
            cp.wait_recv()
        for cp in sends:
            cp.wait_send()
        for cp in local:
            cp.wait()

    return _Hosted(arrs, [jax.ShapeDtypeStruct((NDEV,) + a.shape, a.dtype) for a in arrs], 5 * n, start, finish)


def _gather_second(bufs):
    n = len(bufs)

    def parts(ins, outs, ss, rs, base):
        x, y, c = lax.axis_index("x"), lax.axis_index("y"), lax.axis_index("c")
        sibling = (x, y, 1 - c)
        chips = [(1 - x, y), (x, 1 - y), (1 - x, 1 - y)]
        sends, recvs = [], []
        for a in range(n):
            for j, chip in enumerate(chips):
                mine = 4 * chip[0] + 2 * chip[1] + c
                theirs = 4 * chip[0] + 2 * chip[1] + (1 - c)
                sends.append(pltpu.make_async_remote_copy(
                    src_ref=ins[a].at[mine], dst_ref=outs[a].at[mine], send_sem=ss.at[base + 3 * a + j],
                    recv_sem=rs.at[base + 3 * a + j], device_id=sibling, device_id_type=MESH))
                recvs.append(pltpu.make_async_remote_copy(
                    src_ref=ins[a].at[theirs], dst_ref=outs[a].at[theirs], send_sem=ss.at[base + 3 * a + j],
                    recv_sem=rs.at[base + 3 * a + j], device_id=sibling, device_id_type=MESH))
        return sends, recvs

    def start(ins, outs, ss, rs, base):
        for cp in parts(ins, outs, ss, rs, base)[0]:
            cp.start()

    def finish(ins, outs, ss, rs, base):
        sends, recvs = parts(ins, outs, ss, rs, base)
        for cp in recvs:
            cp.wait_recv()
        for cp in sends:
            cp.wait_send()

    return _Hosted(bufs, [jax.ShapeDtypeStruct(b.shape, b.dtype) for b in bufs], 3 * n, start, finish,
                   aliases={a: a for a in range(n)})


def _swap(src, nblk, ids_fn, partner_fn):
    def copies(ins, outs, ss, rs, base):
        x, y, c = lax.axis_index("x"), lax.axis_index("y"), lax.axis_index("c")
        ids = ids_fn(x, y, c)
        partner = partner_fn(x, y, c)
        return [pltpu.make_async_remote_copy(
            src_ref=ins[0].at[ids[k]], dst_ref=outs[0].at[k], send_sem=ss.at[base + k], recv_sem=rs.at[base + k],
            device_id=partner, device_id_type=MESH) for k in range(nblk)]

    def start(ins, outs, ss, rs, base):
        for cp in copies(ins, outs, ss, rs, base):
            cp.start()

    def finish(ins, outs, ss, rs, base):
        for cp in copies(ins, outs, ss, rs, base):
            cp.wait()

    return _Hosted([src], [jax.ShapeDtypeStruct((nblk,) + src.shape[1:], src.dtype)], nblk, start, finish)


def _swap_chips(send):
    def copies(ins, outs, ss, rs, base):
        x, y, c = lax.axis_index("x"), lax.axis_index("y"), lax.axis_index("c")
        chips = [(1 - x, y), (x, 1 - y), (1 - x, 1 - y)]
        return [pltpu.make_async_remote_copy(
            src_ref=ins[0].at[j], dst_ref=outs[0].at[j], send_sem=ss.at[base + j], recv_sem=rs.at[base + j],
            device_id=(*chip, c), device_id_type=MESH) for j, chip in enumerate(chips)]

    def start(ins, outs, ss, rs, base):
        for cp in copies(ins, outs, ss, rs, base):
            cp.start()

    def finish(ins, outs, ss, rs, base):
        for cp in copies(ins, outs, ss, rs, base):
            cp.wait()

    return _Hosted([send], [jax.ShapeDtypeStruct(send.shape, send.dtype)], 3, start, finish)


def _add_send(a, b, idx, ns, name):
    _, r, c = a.shape
    tr = _tile(r, 256)

    def body(idx_ref, a_ref, b_ref, send_ref):
        send_ref[...] = (a_ref[...] + b_ref[...]).astype(BF16)

    def sel(off):
        return pl.BlockSpec((None, tr, c), lambda k, i, s: (s[off + k], i, 0))

    gs = pltpu.PrefetchScalarGridSpec(num_scalar_prefetch=1, grid=(ns, r // tr), in_specs=[sel(0), sel(ns)],
                                      out_specs=pl.BlockSpec((None, tr, c), lambda k, i, s: (k, i, 0)))
    return _pc(body, name=name, grid_spec=gs, sem=("arbitrary", "arbitrary"),
               out_shape=jax.ShapeDtypeStruct((ns, r, c), BF16))(idx, a, b)


class _ReduceScatter:
    def __init__(self, g, tag):
        self.g, self.tag = g, tag

    def swap_core(self):
        return _swap(self.g, 4, lambda x, y, c: [1 - c, 3 - c, 5 - c, 7 - c], lambda x, y, c: (x, y, 1 - c))

    def after_core(self, recv):
        x, y, c = lax.axis_index("x"), lax.axis_index("y"), lax.axis_index("c")
        chips = [(1 - x, y), (x, 1 - y), (1 - x, 1 - y)]
        idx = jnp.stack([4 * p + 2 * q + c for p, q in chips] + [2 * p + q for p, q in chips]).astype(jnp.int32)
        self.send = _add_send(self.g, recv, idx, 3, "rs_add_" + self.tag)
        self.recv_core = recv
        zero = jnp.zeros((), jnp.int32)
        self.idx = jnp.stack([4 * x + 2 * y + c, 2 * x + y, zero, zero + 1, zero + 2]).astype(jnp.int32)

    def swap_chips(self):
        return _swap_chips(self.send)

    def after_chips(self, recv):
        self.parts = [self.g, self.recv_core, recv, recv, recv]


def _ada_fwd(c_all, w_ada, b_cols, b_lb):
    nl, d, ncol = w_ada.shape
    nseq = c_all.shape[0]
    di = b_lb.shape[1]

    def body(c_ref, w_ref, b_ref, lb_ref, mod_ref, lbj_ref):
        cv = c_ref[...]
        cact = (cv * _sigmoid(cv)).astype(BF16)
        for l in range(nl):
            mod_ref[l] = _dot(cact, w_ref[l].astype(BF16)) + b_ref[l]
        b0, b1 = lb_ref[0:1, :], lb_ref[1:2, :]
        mx = jnp.maximum(b0, b1)
        e0, e1 = jnp.exp(b0 - mx), jnp.exp(b1 - mx)
        s = e0 + e1
        p0, p1 = e0 / s, e1 / s
        lbj_ref[0:1, :] = (p0 + p1) - p0
        lbj_ref[1:2, :] = p0 * p1

    return _pc(body, name="ada_fwd",
               out_shape=[jax.ShapeDtypeStruct((nl, nseq, ncol), F32), jax.ShapeDtypeStruct((2, di), F32)]
               )(c_all, w_ada, b_cols, b_lb)


def _ada_bwd(c_all, dmod_cols, dmod_full):
    nl, nseq, ncol = dmod_cols.shape
    d = c_all.shape[1]
    d3 = dmod_full.shape[2]

    def body(c_ref, dc_ref, df_ref, gw_ref, gb_ref):
        cv = c_ref[...]
        cact = (cv * _sigmoid(cv)).astype(BF16)
        for l in range(nl):
            gw_ref[l] = _dot_tn(cact, dc_ref[l].astype(BF16))
            gb_ref[l:l + 1, :] = jnp.sum(df_ref[l], axis=0, keepdims=True)

    return _pc(body, name="ada_bwd",
               out_shape=[jax.ShapeDtypeStruct((nl, d, ncol), F32), jax.ShapeDtypeStruct((nl, d3), F32)]
               )(c_all, dmod_cols, dmod_full)


def _prenorm(x, gain, mod, t_seq, name):
    m, d = x.shape
    tm = _tile(t_seq, 1024)
    per = t_seq // tm

    def body(x_ref, g_ref, mod_ref, h_ref, ht_ref):
        xv = x_ref[...]
        rstd = lax.rsqrt(jnp.mean(xv * xv, axis=-1, keepdims=True) + EPS)
        r = xv * rstd * g_ref[...]
        h = r * (1.0 + mod_ref[0, 1:2, :]) + mod_ref[0, 0:1, :]
        h_ref[...] = h.astype(BF16)
        ht_ref[...] = h.T.astype(BF16)

    return _pc(body, name=name, out_shape=[jax.ShapeDtypeStruct((m, d), BF16), jax.ShapeDtypeStruct((d, m), BF16)],
               grid=(m // tm,),
               in_specs=[pl.BlockSpec((tm, d), lambda i: (i, 0)), pl.BlockSpec((1, d), lambda i: (0, 0)),
                         pl.BlockSpec((1, 3, d), lambda i: (i // per, 0, 0))],
               out_specs=[pl.BlockSpec((tm, d), lambda i: (i, 0)), pl.BlockSpec((d, tm), lambda i: (0, i))],
               sem=("parallel",))(x, gain, mod)


def _prenorm_bwd(dh, x, gain, mod, dxn, t_seq, name):
    m, d = x.shape
    nb = m // t_seq
    tm = _tile(t_seq, 1024)
    per = t_seq // tm

    def body(dh_ref, x_ref, g_ref, mod_ref, dxn_ref, dx_ref, dss_ref, dg_ref):
        i = pl.program_id(0)
        xv, dhv, g = x_ref[...], dh_ref[...], g_ref[...]
        rstd = lax.rsqrt(jnp.mean(xv * xv, axis=-1, keepdims=True) + EPS)
        xhat = xv * rstd
        dr = dhv * (1.0 + mod_ref[0, 1:2, :])
        dxhat = dr * g
        dx_ref[...] = dxn_ref[...] + rstd * (dxhat - xhat * jnp.mean(dxhat * xhat, axis=-1, keepdims=True))

        @pl.when(i % per == 0)
        def _():
            dss_ref[...] = jnp.zeros_like(dss_ref)

        @pl.when(i == 0)
        def _():
            dg_ref[...] = jnp.zeros_like(dg_ref)

        dss_ref[0, 0:1, :] += jnp.sum(dhv, axis=0, keepdims=True)
        dss_ref[0, 1:2, :] += jnp.sum(dhv * (xhat * g), axis=0, keepdims=True)
        dg_ref[...] += jnp.sum(dr * xhat, axis=0, keepdims=True)

    row = pl.BlockSpec((tm, d), lambda i: (i, 0))
    return _pc(body, name=name,
               out_shape=[jax.ShapeDtypeStruct((m, d), F32), jax.ShapeDtypeStruct((nb, 2, d), F32),
                          jax.ShapeDtypeStruct((1, d), F32)],
               grid=(m // tm,),
               in_specs=[row, row, pl.BlockSpec((1, d), lambda i: (0, 0)),
                         pl.BlockSpec((1, 3, d), lambda i: (i // per, 0, 0)), row],
               out_specs=[row, pl.BlockSpec((1, 2, d), lambda i: (i // per, 0, 0)),
                          pl.BlockSpec((1, d), lambda i: (0, 0))],
               sem=("arbitrary",))(dh, x, gain, mod, dxn)


def _mm_in(h, ws, sections, name, comm=None):
    m, k = h.shape
    nw = len(ws)
    widths = [w.shape[2] for w in ws]
    offs = [sum(widths[:a]) for a in range(nw)]
    nc = sum(widths)
    per = NDEV // sections if sections > 1 else NDEV
    tm = _din_tile(m)
    assert per % 2 == 0

    def body(*refs):
        hv = refs[0][...]
        o_ref = refs[1 + nw]
        for b in range(2):
            for a in range(nw):
                lo = b * nc + offs[a]
                o_ref[:, lo:lo + widths[a]] = _dot(hv, refs[1 + a][b])

    w_specs = [pl.BlockSpec((2, k, wd), lambda j, i: (j, 0, 0)) for wd in widths]
    if sections > 1:
        out_shape = jax.ShapeDtypeStruct((sections, m, per * nc), F32)
        out_spec = pl.BlockSpec((None, tm, 2 * nc), lambda j, i: ((2 * j) // per, i, ((2 * j) % per) // 2))
    else:
        out_shape = jax.ShapeDtypeStruct((m, NDEV * nc), F32)
        out_spec = pl.BlockSpec((tm, 2 * nc), lambda j, i: (i, j))
    return _pc(body, name=name, out_shape=out_shape, grid=(NDEV // 2, m // tm),
               in_specs=[pl.BlockSpec((tm, k), lambda j, i: (i, 0))] + w_specs,
               out_specs=out_spec, sem=("parallel", "parallel"), comm=comm)(h, *ws)


def _din_tile(m):
    return 1024 if m % 1024 == 0 and m >= 2048 else _tile(m, 512)


def _mm_din(dproj, ws, sections, name, comm=None, tiles=None, prev=None):
    nw, k = len(ws), ws[0].shape[1]
    widths = [w.shape[2] for w in ws]
    offs = [sum(widths[:a]) for a in range(nw)]
    nc = sum(widths)
    m = dproj.shape[-2]
    tm = _din_tile(m)
    t0, nt = tiles if tiles is not None else (0, m // tm)
    per = NDEV // sections if sections > 1 else NDEV
    assert per % 2 == 0

    def body(*refs):
        d_ref, o_ref = refs[0], refs[-1]
        j = pl.program_id(1)
        acc = None
        for b in range(2):
            for a in range(nw):
                lo = b * nc + offs[a]
                term = _dot_nt(d_ref[:, lo:lo + widths[a]], refs[1 + a][b])
                acc = term if acc is None else acc + term

        @pl.when(j == 0)
        def _():
            o_ref[...] = acc

        @pl.when(j > 0)
        def _():
            o_ref[...] += acc

    if sections > 1:
        dspec = pl.BlockSpec((None, tm, 2 * nc), lambda i, j: ((2 * j) // per, i + t0, ((2 * j) % per) // 2))
    else:
        dspec = pl.BlockSpec((tm, 2 * nc), lambda i, j: (i + t0, j))
    in_specs = [dspec] + [pl.BlockSpec((2, k, wd), lambda i, j: (j, 0, 0)) for wd in widths]
    args = [dproj, *ws]
    if prev is not None:
        in_specs.append(ANY)
        args.append(prev)
    return _pc(body, name=name, out_shape=jax.ShapeDtypeStruct((m, k), F32), grid=(nt, NDEV // 2), in_specs=in_specs,
               out_specs=pl.BlockSpec((tm, k), lambda i, j: (i + t0, 0)), sem=("parallel", "arbitrary"),
               comm=comm, aliases={1 + nw: 0} if prev is not None else None)(*args)


def _mm_dw_in(ht, dproj, nc, sections, name, comm=None):
    k, m = ht.shape
    per = NDEV // sections if sections > 1 else NDEV

    def body(h_ref, d_ref, o_ref):
        o_ref[...] = _dot(h_ref[...], d_ref[...])

    if sections > 1:
        dspec = pl.BlockSpec((None, m, nc), lambda j: (j // per, 0, j % per))
    else:
        dspec = pl.BlockSpec((m, nc), lambda j: (0, j))
    return _pc(body, name=name, out_shape=jax.ShapeDtypeStruct((NDEV, k, nc), F32), grid=(NDEV,),
               in_specs=[pl.BlockSpec((k, m), lambda j: (0, 0)), dspec],
               out_specs=pl.BlockSpec((None, k, nc), lambda j: (j, 0, 0)),
               sem=("parallel",), comm=comm)(ht, dproj)


def _mm_dw_in_cols(ht, dproj, nc, unit, first, count, name, comm=None):
    k, m = ht.shape
    per_blk = nc // unit

    def body(h_ref, *refs):
        d_refs, o_ref = refs[:count], refs[count]
        hv = h_ref[...]
        for s in range(count):
            o_ref[:, s * unit:(s + 1) * unit] = _dot(hv, d_refs[s][...])

    def dspec(s):
        return pl.BlockSpec((m, unit), lambda j: (0, j * per_blk + first + s))

    return _pc(body, name=name, out_shape=jax.ShapeDtypeStruct((NDEV, k, count * unit), F32), grid=(NDEV,),
               in_specs=[pl.BlockSpec((k, m), lambda j: (0, 0))] + [dspec(s) for s in range(count)],
               out_specs=pl.BlockSpec((None, k, count * unit), lambda j: (j, 0, 0)),
               sem=("parallel",), comm=comm)(ht, *([dproj] * count))


def _out_proj(ybr, w_out, x, mod, t_seq, name, comm=None):
    m, di = ybr.shape
    d = w_out.shape[1]
    tm = _tile(t_seq, 512)
    per = t_seq // tm

    def body(y_ref, w_ref, x_ref, mod_ref, yo_ref, xn_ref):
        yo = _dot(y_ref[...], w_ref[...])
        yo_ref[...] = yo
        xn_ref[...] = x_ref[...] + mod_ref[0, 2:3, :] * yo

    row = pl.BlockSpec((tm, d), lambda i: (i, 0))
    return _pc(body, name=name,
               out_shape=[jax.ShapeDtypeStruct((m, d), F32), jax.ShapeDtypeStruct((m, d), F32)],
               grid=(m // tm,),
               in_specs=[pl.BlockSpec((tm, di), lambda i: (i, 0)), pl.BlockSpec((di, d), lambda i: (0, 0)), row,
                         pl.BlockSpec((1, 3, d), lambda i: (i // per, 0, 0))],
               out_specs=[row, row], sem=("parallel",), comm=comm)(ybr, w_out, x, mod)


def _out_proj_loss(ybr, w_out, x, mod, gain, target, t_seq):
    m, di = ybr.shape
    d = w_out.shape[1]
    tm = _tile(t_seq, 512)
    per = t_seq // tm

    def body(y_ref, w_ref, x_ref, mod_ref, g_ref, t_ref, yo_ref, dx_ref, loss_ref, dg_ref):
        i = pl.program_id(0)
        yo = _dot(y_ref[...], w_ref[...])
        yo_ref[...] = yo
        xv = x_ref[...] + mod_ref[0, 2:3, :] * yo
        g = g_ref[...]
        rstd = lax.rsqrt(jnp.mean(xv * xv, axis=-1, keepdims=True) + EPS)
        xhat = xv * rstd
        err = xhat * g - t_ref[...]
        dy = err * (1.0 / d)
        dxhat = dy * g
        dx_ref[...] = rstd * (dxhat - xhat * jnp.mean(dxhat * xhat, axis=-1, keepdims=True))

        @pl.when(i == 0)
        def _():
            loss_ref[...] = jnp.zeros_like(loss_ref)
            dg_ref[...] = jnp.zeros_like(dg_ref)

        loss_ref[...] += 0.5 * jnp.sum(jnp.mean(err * err, axis=-1, keepdims=True), axis=0, keepdims=True)
        dg_ref[...] += jnp.sum(dy * xhat, axis=0, keepdims=True)

    row = pl.BlockSpec((tm, d), lambda i: (i, 0))
    vec = pl.BlockSpec((1, d), lambda i: (0, 0))
    return _pc(body, name="out_proj_loss",
               out_shape=[jax.ShapeDtypeStruct((m, d), F32), jax.ShapeDtypeStruct((m, d), F32),
                          jax.ShapeDtypeStruct((1, 1), F32), jax.ShapeDtypeStruct((1, d), F32)],
               grid=(m // tm,),
               in_specs=[pl.BlockSpec((tm, di), lambda i: (i, 0)), pl.BlockSpec((di, d), lambda i: (0, 0)), row,
                         pl.BlockSpec((1, 3, d), lambda i: (i // per, 0, 0)), vec, row],
               out_specs=[row, row, pl.BlockSpec((1, 1), lambda i: (0, 0)), vec],
               sem=("arbitrary",))(ybr, w_out, x, mod, gain, target)


def _gate_dybr(dxn, yout, mod, w_out, t_seq, name):
    m, d = dxn.shape
    di = w_out.shape[0]
    nb = m // t_seq
    tm = _tile(t_seq, 512)
    per = t_seq // tm

    def body(dxn_ref, yo_ref, mod_ref, w_ref, dy_ref, dgate_ref, o_ref):
        i = pl.program_id(0)
        dv = dxn_ref[...]
        dy = (mod_ref[0, 2:3, :] * dv).astype(BF16)
        dy_ref[...] = dy
        o_ref[...] = _dot_nt(dy, w_ref[...])

        @pl.when(i % per == 0)
        def _():
            dgate_ref[...] = jnp.zeros_like(dgate_ref)

        dgate_ref[0] += jnp.sum(dv * yo_ref[...], axis=0, keepdims=True)

    row = pl.BlockSpec((tm, d), lambda i: (i, 0))
    return _pc(body, name=name,
               out_shape=[jax.ShapeDtypeStruct((m, d), BF16), jax.ShapeDtypeStruct((nb, 1, d), F32),
                          jax.ShapeDtypeStruct((m, di), F32)],
               grid=(m // tm,),
               in_specs=[row, row, pl.BlockSpec((1, 3, d), lambda i: (i // per, 0, 0)),
                         pl.BlockSpec((di, d), lambda i: (0, 0))],
               out_specs=[row, pl.BlockSpec((1, 1, d), lambda i: (i // per, 0, 0)),
                          pl.BlockSpec((tm, di), lambda i: (i, 0))],
               sem=("arbitrary",))(dxn, yout, mod, w_out)


def _mm_dw_out(ybr, dy, name, comm=None):
    m, di = ybr.shape
    d = dy.shape[1]
    tk = 2048 if m % 2048 == 0 else _tile(m, 512)
    tn = _tile(di, 1024)

    def body(y_ref, dy_ref, o_ref):
        kk = pl.program_id(1)
        acc = _dot_tn(y_ref[...], dy_ref[...])

        @pl.when(kk == 0)
        def _():
            o_ref[...] = acc

        @pl.when(kk > 0)
        def _():
            o_ref[...] += acc

    return _pc(body, name=name, out_shape=jax.ShapeDtypeStruct((di, d), F32), grid=(di // tn, m // tk),
               in_specs=[pl.BlockSpec((tk, tn), lambda n, k: (k, n)), pl.BlockSpec((tk, d), lambda n, k: (k, 0))],
               out_specs=pl.BlockSpec((tn, d), lambda n, k: (n, 0)), sem=("parallel", "arbitrary"),
               comm=comm)(ybr, dy)


def _sgu_mask():
    t = lax.broadcasted_iota(jnp.int32, (SG_BLOCK, SG_BLOCK), 0)
    s = lax.broadcasted_iota(jnp.int32, (SG_BLOCK, SG_BLOCK), 1)
    return (s // CHUNK) <= (t // CHUNK)


def _a_mid_fwd(proj, ln_g, ln_b, w_s, bs_t, t_seq, comm=None):
    m, n3 = proj.shape
    di = n3 // 3
    gd = di // SG_GROUPS
    r = _tile(t_seq, 256)
    nblk = r // SG_BLOCK

    def body(p_ref, lg_ref, lb_ref, ws_ref, bs_ref, ybr_ref, s_scr):
        v = _gelu(p_ref[:, di:2 * di])
        mu = jnp.mean(v, axis=-1, keepdims=True)
        vc = v - mu
        rstd = lax.rsqrt(jnp.mean(vc * vc, axis=-1, keepdims=True) + EPS)
        vb = (vc * rstd * lg_ref[...] + lb_ref[...]).astype(BF16)
        mask = _sgu_mask()
        for gi in range(SG_GROUPS):
            ws = jnp.where(mask, ws_ref[gi], 0.0).astype(BF16)
            bcol = bs_ref[:, gi:gi + 1]
            for b in range(nblk):
                rows = slice(b * SG_BLOCK, (b + 1) * SG_BLOCK)
                cols = slice(gi * gd, (gi + 1) * gd)
                s_scr[rows, cols] = _dot(ws, vb[rows, cols]) + bcol
        gg = p_ref[:, 2 * di:]
        ybr_ref[...] = (_gelu(p_ref[:, :di]) * s_scr[...] * (gg * _sigmoid(gg))).astype(BF16)

    vec = pl.BlockSpec((1, di), lambda i: (0, 0))
    return _pc(body, name="a_mid_fwd", out_shape=jax.ShapeDtypeStruct((m, di), BF16), grid=(m // r,),
               in_specs=[pl.BlockSpec((r, n3), lambda i: (i, 0)), vec, vec,
                         pl.BlockSpec((SG_GROUPS, SG_BLOCK, SG_BLOCK), lambda i: (0, 0, 0)),
                         pl.BlockSpec((SG_BLOCK, 128), lambda i: (0, 0))],
               out_specs=pl.BlockSpec((r, di), lambda i: (i, 0)),
               scratch=[pltpu.VMEM((r, di), F32)], sem=("parallel",), comm=comm)(proj, ln_g, ln_b, w_s, bs_t)


def _a_mid_bwd(proj, dybr, ln_g, ln_b, w_s, bs_t, t_seq, comm=None):
    m, n3 = proj.shape
    di = n3 // 3
    gd = di // SG_GROUPS
    r = _tile(t_seq, 256)
    nblk = r // SG_BLOCK

    def body(p_ref, dy_ref, lg_ref, lb_ref, ws_ref, bs_ref,
             dp_ref, dlg_ref, dlb_ref, dws_ref, dbs_ref, s_scr, dvl_scr):
        i = pl.program_id(0)

        @pl.when(i == 0)
        def _():
            dlg_ref[...] = jnp.zeros_like(dlg_ref)
            dlb_ref[...] = jnp.zeros_like(dlb_ref)
            dws_ref[...] = jnp.zeros_like(dws_ref)
            dbs_ref[...] = jnp.zeros_like(dbs_ref)

        v, dgelu_v = _gelu_and_grad(p_ref[:, di:2 * di])
        mu = jnp.mean(v, axis=-1, keepdims=True)
        vc = v - mu
        rstd = lax.rsqrt(jnp.mean(vc * vc, axis=-1, keepdims=True) + EPS)
        vhat = vc * rstd
        lg = lg_ref[...]
        vb = (vhat * lg + lb_ref[...]).astype(BF16)
        u, dgelu_u = _gelu_and_grad(p_ref[:, :di])
        gg = p_ref[:, 2 * di:]
        sg = _sigmoid(gg)
        dyv = dy_ref[...]
        dus = dyv * (gg * sg)
        dsb = (dus * u).astype(BF16)
        ds32 = dus * u
        mask = _sgu_mask()
        lane = lax.broadcasted_iota(jnp.int32, (SG_BLOCK, 128), 1)
        dbs_acc = jnp.zeros((SG_BLOCK, 128), F32)
        for gi in range(SG_GROUPS):
            ws = jnp.where(mask, ws_ref[gi], 0.0).astype(BF16)
            bcol = bs_ref[:, gi:gi + 1]
            cols = slice(gi * gd, (gi + 1) * gd)
            dws_acc = jnp.zeros((SG_BLOCK, SG_BLOCK), F32)
            dbs_col = jnp.zeros((SG_BLOCK, 1), F32)
            for b in range(nblk):
                rows = slice(b * SG_BLOCK, (b + 1) * SG_BLOCK)
                s_scr[rows, cols] = _dot(ws, vb[rows, cols]) + bcol
                dvl_scr[rows, cols] = _dot_tn(ws, dsb[rows, cols])
                dws_acc += _dot_nt(dsb[rows, cols], vb[rows, cols])
                dbs_col += jnp.sum(ds32[rows, cols], axis=-1, keepdims=True)
            dws_ref[gi] += jnp.where(mask, dws_acc, 0.0)
            dbs_acc += jnp.where(lane == gi, dbs_col, 0.0)
        dbs_ref[...] += dbs_acc
        s = s_scr[...]
        dp_ref[:, :di] = (dus * s * dgelu_u).astype(BF16)
        dp_ref[:, 2 * di:] = (dyv * u * s * (sg * (1.0 + gg * (1.0 - sg)))).astype(BF16)
        dvl = dvl_scr[...]
        dlg_ref[...] += jnp.sum(dvl * vhat, axis=0, keepdims=True)
        dlb_ref[...] += jnp.sum(dvl, axis=0, keepdims=True)
        dvh = dvl * lg
        dv = rstd * (dvh - jnp.mean(dvh, axis=-1, keepdims=True)
                     - vhat * jnp.mean(dvh * vhat, axis=-1, keepdims=True))
        dp_ref[:, di:2 * di] = (dv * dgelu_v).astype(BF16)

    vec = pl.BlockSpec((1, di), lambda i: (0, 0))
    wsb = pl.BlockSpec((SG_GROUPS, SG_BLOCK, SG_BLOCK), lambda i: (0, 0, 0))
    bsb = pl.BlockSpec((SG_BLOCK, 128), lambda i: (0, 0))
    return _pc(body, name="a_mid_bwd",
               out_shape=[jax.ShapeDtypeStruct((m, n3), BF16), jax.ShapeDtypeStruct((1, di), F32),
                          jax.ShapeDtypeStruct((1, di), F32),
                          jax.ShapeDtypeStruct((SG_GROUPS, SG_BLOCK, SG_BLOCK), F32),
                          jax.ShapeDtypeStruct((SG_BLOCK, 128), F32)],
               grid=(m // r,),
               in_specs=[pl.BlockSpec((r, n3), lambda i: (i, 0)), pl.BlockSpec((r, di), lambda i: (i, 0)),
                         vec, vec, wsb, bsb],
               out_specs=[pl.BlockSpec((r, n3), lambda i: (i, 0)), vec, vec, wsb, bsb],
               scratch=[pltpu.VMEM((r, di), F32), pltpu.VMEM((r, di), F32)],
               sem=("arbitrary",), comm=comm)(proj, dybr, ln_g, ln_b, w_s, bs_t)


def _chunk_rows(n):
    if isinstance(n, int):
        return pl.ds(n * CHUNK, CHUNK)
    return pl.ds(pl.multiple_of(n * CHUNK, CHUNK), CHUNK)


def _hgrn_dims(t_seq, di):
    tr = _tile(t_seq, 128)
    hc = _tile(di, 2048)
    return tr, hc, hc // HEAD_DIM


def _hgrn_gates(f_ref, lb, a_scr, k_scr, tr):
    sig = _sigmoid(f_ref[...])
    fg = lb + (1.0 - lb) * sig
    k_scr[...] = 1.0 - fg
    logf = jnp.log(fg)
    g = min(CUM_ROWS, tr)
    tri = _tri_mask(g, reverse=False)
    for rg in range(tr // g):
        a_scr[rg * g:(rg + 1) * g, :] = _tri_apply(tri, logf[rg * g:(rg + 1) * g, :])
    return sig, fg


def _hgrn_fwd(proj, lbj, gn, nb, t_seq):
    _, m, di = proj.shape
    tr, hc, hpg = _hgrn_dims(t_seq, di)
    nt, nhg, ncl = t_seq // tr, di // hc, tr // CHUNK
    nheads = di // HEAD_DIM

    def body(p_ref, lb_ref, gn_ref, o_ref, ybr_ref, st_ref, st_scr, a_scr, k_scr):
        q_ref, f_ref, i_ref, g_ref = (p_ref.at[s] for s in range(4))
        t = pl.program_id(2)

        @pl.when(t == 0)
        def _():
            st_scr[...] = jnp.zeros_like(st_scr)

        _hgrn_gates(f_ref, lb_ref[0:1, :], a_scr, k_scr, tr)
        gnv = gn_ref[...]
        rr = lax.broadcasted_iota(jnp.int32, (CHUNK, CHUNK), 0)
        cc = lax.broadcasted_iota(jnp.int32, (CHUNK, CHUNK), 1)
        causal = cc <= rr

        def chunk(n, carry):
            rows = _chunk_rows(n)
            lanes = [slice(hd * HEAD_DIM, (hd + 1) * HEAD_DIM) for hd in range(hpg)]
            hs = []
            for hd, ls in enumerate(lanes):
                h = {}
                ah, kh = a_scr[rows, ls], k_scr[rows, ls]
                qp = q_ref[rows, ls]
                qh = qp * _sigmoid(qp)
                h["vb"] = i_ref[rows, ls].astype(BF16)
                aref, alast = ah[CHUNK // 2 - 1:CHUNK // 2, :], ah[CHUNK - 1:CHUNK, :]
                h["q_in"] = (qh * jnp.exp(ah - aref)).astype(BF16)
                h["k_in"] = (kh * jnp.exp(aref - ah)).astype(BF16)
                h["q_out"] = (qh * jnp.exp(ah)).astype(BF16)
                h["k_out"] = (kh * jnp.exp(alast - ah)).astype(BF16)
                h["dec"] = jnp.exp(alast)
                st = st_scr[hd]
                st_ref[n, hd] = st
                h["st"] = st
                hs.append(h)
            for h in hs:
                h["scores"] = _dot_nt(h["q_in"], h["k_in"])
                h["o_inter"] = _dot_nt(h["q_out"], h["st"].astype(BF16))
                h["st_mm"] = _dot_tn(h["vb"], h["k_out"])
            for h in hs:
                h["o"] = _dot(jnp.where(causal, h["scores"], 0.0).astype(BF16), h["vb"]) + h["o_inter"]
            for hd, (h, ls) in enumerate(zip(hs, lanes)):
                st_scr[hd] = h["st"] * h["dec"] + h["st_mm"]
                o = h["o"]
                o_ref[rows, ls] = o
                rstd = lax.rsqrt(jnp.mean(o * o, axis=-1, keepdims=True) + EPS)
                gg = g_ref[rows, ls]
                ybr_ref[rows, ls] = ((o * rstd * gnv) * (gg * _sigmoid(gg))).astype(BF16)
            return carry

        lax.fori_loop(0, ncl, chunk, 0)

    blk = pl.BlockSpec((tr, hc), lambda hg, b, t: (b * nt + t, hg))
    return _pc(body, name="hgrn_fwd",
               out_shape=[jax.ShapeDtypeStruct((m, di), F32), jax.ShapeDtypeStruct((m, di), BF16),
                          jax.ShapeDtypeStruct((m // CHUNK, nheads, HEAD_DIM, HEAD_DIM), F32)],
               grid=(nhg, nb, nt),
               in_specs=[pl.BlockSpec((4, tr, hc), lambda hg, b, t: (0, b * nt + t, hg)),
                         pl.BlockSpec((2, hc), lambda hg, b, t: (0, hg)),
                         pl.BlockSpec((1, HEAD_DIM), lambda hg, b, t: (0, 0))],
               out_specs=[blk, blk, pl.BlockSpec((ncl, hpg, HEAD_DIM, HEAD_DIM),
                                                 lambda hg, b, t: (b * nt + t, hg, 0, 0))],
               scratch=[pltpu.VMEM((hpg, HEAD_DIM, HEAD_DIM), F32), pltpu.VMEM((tr, hc), F32),
                        pltpu.VMEM((tr, hc), F32)],
               sem=("parallel", "arbitrary", "arbitrary"))(proj, lbj, gn)


def _hgrn_bwd(proj, o_all, dybr, states, lbj, gn, nb, t_seq, comm=None):
    _, m, di = proj.shape
    tr, hc, hpg = _hgrn_dims(t_seq, di)
    nt, nhg, ncl = t_seq // tr, di // hc, tr // CHUNK

    def body(p_ref, o_ref, dy_ref, st_ref, lb_ref, gn_ref,
             dp_ref, dlb_ref, dgn_ref, dst_scr, a_scr, k_scr, da_scr, dk_scr):
        q_ref, f_ref, i_ref, g_ref = (p_ref.at[s] for s in range(4))
        hg, b, t = pl.program_id(0), pl.program_id(1), pl.program_id(2)

        @pl.when(t == 0)
        def _():
            dst_scr[...] = jnp.zeros_like(dst_scr)

        @pl.when((b == 0) & (t == 0))
        def _():
            dlb_ref[...] = jnp.zeros_like(dlb_ref)

        @pl.when((hg == 0) & (b == 0) & (t == 0))
        def _():
            dgn_ref[...] = jnp.zeros_like(dgn_ref)

        lb = lb_ref[0:1, :]
        sig, fg = _hgrn_gates(f_ref, lb, a_scr, k_scr, tr)
        gnv = gn_ref[...]
        rr = lax.broadcasted_iota(jnp.int32, (CHUNK, CHUNK), 0)
        cc = lax.broadcasted_iota(jnp.int32, (CHUNK, CHUNK), 1)
        causal = cc <= rr
        rowi = lax.broadcasted_iota(jnp.int32, (CHUNK, HEAD_DIM), 0)

        def chunk(it, carry):
            n = ncl - 1 - it
            rows = _chunk_rows(n)
            for hd0 in range(0, hpg, PHASE_HEADS):
                heads(n, rows, range(hd0, min(hpg, hd0 + PHASE_HEADS)))
            return carry

        def heads(n, rows, ids):
            lanes = [slice(hd * HEAD_DIM, (hd + 1) * HEAD_DIM) for hd in ids]
            hs = []
            for hd, ls in zip(ids, lanes):
                h = {}
                ah, kh = a_scr[rows, ls], k_scr[rows, ls]
                qp = q_ref[rows, ls]
                sq = _sigmoid(qp)
                qh = qp * sq
                h["dsilu_q"] = sq * (1.0 + qp * (1.0 - sq))
                h["vb"] = i_ref[rows, ls].astype(BF16)
                aref, alast = ah[CHUNK // 2 - 1:CHUNK // 2, :], ah[CHUNK - 1:CHUNK, :]
                h["e1"], h["e2"] = jnp.exp(ah - aref), jnp.exp(aref - ah)
                h["e3"], h["e4"] = jnp.exp(ah), jnp.exp(alast - ah)
                h["dec"] = jnp.exp(alast)
                h["q_in"], h["k_in"], h["q_out"], h["k_out"] = qh * h["e1"], kh * h["e2"], qh * h["e3"], kh * h["e4"]
                for nm in ("q_in", "k_in", "q_out", "k_out"):
                    h[nm + "_b"] = h[nm].astype(BF16)
                o = o_ref[rows, ls]
                rstd = lax.rsqrt(jnp.mean(o * o, axis=-1, keepdims=True) + EPS)
                ohat = o * rstd
                gg = g_ref[rows, ls]
                sg = _sigmoid(gg)
                dyv = dy_ref[rows, ls]
                d_on = dyv * (gg * sg)
                dp_ref[3, rows, ls] = (dyv * (ohat * gnv) * (sg * (1.0 + gg * (1.0 - sg)))).astype(BF16)
                h["dgn"] = jnp.sum(d_on * ohat, axis=0, keepdims=True)
                dohat = d_on * gnv
                do = rstd * (dohat - ohat * jnp.mean(dohat * ohat, axis=-1, keepdims=True))
                h["do_b"] = do.astype(BF16)
                h["st_prev"] = st_ref[n, hd]
                h["dst"] = dst_scr[hd]
                hs.append(h)
            for h in hs:
                dst_b = h["dst"].astype(BF16)
                h["scores"] = _dot_nt(h["q_in_b"], h["k_in_b"])
                h["dscores"] = _dot_nt(h["do_b"], h["vb"])
                h["dv_inter"] = _dot_nt(h["k_out_b"], dst_b)
                h["dq_out"] = _dot(h["do_b"], h["st_prev"].astype(BF16))
                h["dk_out"] = _dot(h["vb"], dst_b)
                h["dst_mm"] = _dot_tn(h["do_b"], h["q_out_b"])
            for h in hs:
                scores = jnp.where(causal, h["scores"], 0.0).astype(BF16)
                dscores = jnp.where(causal, h["dscores"], 0.0).astype(BF16)
                h["dv"] = _dot_tn(scores, h["do_b"]) + h["dv_inter"]
                h["dq_in"] = _dot(dscores, h["k_in_b"])
                h["dk_in"] = _dot_tn(dscores, h["q_in_b"])
            dgn = hs[0]["dgn"]
            for h in hs[1:]:
                dgn = dgn + h["dgn"]
            dgn_ref[...] += dgn
            for hd, h, ls in zip(ids, hs, lanes):
                ddec = jnp.sum(h["dst"] * h["st_prev"], axis=0, keepdims=True)
                dst_scr[hd] = h["dst"] * h["dec"] + h["dst_mm"]
                dp_ref[2, rows, ls] = h["dv"].astype(BF16)
                dq = h["dq_in"] * h["e1"] + h["dq_out"] * h["e3"]
                dp_ref[0, rows, ls] = (dq * h["dsilu_q"]).astype(BF16)
                dk_scr[rows, ls] = h["dk_in"] * h["e2"] + h["dk_out"] * h["e4"]
                t_in = h["dq_in"] * h["q_in"] - h["dk_in"] * h["k_in"]
                t_out = h["dk_out"] * h["k_out"]
                da = t_in + h["dq_out"] * h["q_out"] - t_out
                da_ref_row = -jnp.sum(t_in, axis=0, keepdims=True)
                da_last_row = jnp.sum(t_out, axis=0, keepdims=True) + ddec * h["dec"]
                da = da + jnp.where(rowi == CHUNK // 2 - 1, da_ref_row, 0.0) \
                        + jnp.where(rowi == CHUNK - 1, da_last_row, 0.0)
                da_scr[rows, ls] = da

        if ncl <= 2:
            for it in range(ncl):
                chunk(it, 0)
        else:
            lax.fori_loop(0, ncl, chunk, 0)
        g = min(CUM_ROWS, tr)
        tri = _tri_mask(g, reverse=True)
        for rg in range(tr // g):
            rs = slice(rg * g, (rg + 1) * g)
            dlogf = _tri_apply(tri, da_scr[rs, :])
            df = dlogf / fg[rs, :] - dk_scr[rs, :]
            sgr = sig[rs, :]
            dp_ref[1, rs, :] = (df * (1.0 - lb) * (sgr * (1.0 - sgr))).astype(BF16)
            dlb_ref[...] += jnp.sum(df * (1.0 - sgr), axis=0, keepdims=True) * lb_ref[1:2, :]

    blk = pl.BlockSpec((tr, hc), lambda hg, b, t: (b * nt + (nt - 1 - t), hg))
    return _pc(body, name="hgrn_bwd",
               out_shape=[jax.ShapeDtypeStruct((4, m, di), BF16), jax.ShapeDtypeStruct((1, di), F32),
                          jax.ShapeDtypeStruct((1, HEAD_DIM), F32)],
               grid=(nhg, nb, nt),
               in_specs=[pl.BlockSpec((4, tr, hc), lambda hg, b, t: (0, b * nt + (nt - 1 - t), hg)), blk, blk,
                         pl.BlockSpec((ncl, hpg, HEAD_DIM, HEAD_DIM),
                                      lambda hg, b, t: (b * nt + (nt - 1 - t), hg, 0, 0)),
                         pl.BlockSpec((2, hc), lambda hg, b, t: (0, hg)),
                         pl.BlockSpec((1, HEAD_DIM), lambda hg, b, t: (0, 0))],
               out_specs=[pl.BlockSpec((4, tr, hc), lambda hg, b, t: (0, b * nt + (nt - 1 - t), hg)),
                          pl.BlockSpec((1, hc), lambda hg, b, t: (0, hg)),
                          pl.BlockSpec((1, HEAD_DIM), lambda hg, b, t: (0, 0))],
               scratch=[pltpu.VMEM((hpg, HEAD_DIM, HEAD_DIM), F32)] + [pltpu.VMEM((tr, hc), F32)] * 4,
               sem=("arbitrary", "arbitrary", "arbitrary"), comm=comm)(
                   proj, o_all, dybr, states, lbj, gn)


def _adamw(parts, w, m, v, name):
    r, c = w.shape
    tr = _tile(r, 256)
    npart = len(parts)
    c1 = 1.0 - ADAM_B1 ** ADAM_STEP
    c2 = 1.0 - ADAM_B2 ** ADAM_STEP

    def body(*refs):
        p_refs = refs[:npart]
        _adamw_math(p_refs, *refs[npart:], c1, c2)

    blk = pl.BlockSpec((tr, c), lambda i: (i, 0))
    return _pc(body, name=name, out_shape=[jax.ShapeDtypeStruct((r, c), F32)] * 4, grid=(r // tr,),
               in_specs=[blk] * (npart + 3), out_specs=[blk] * 4, sem=("parallel",))(*parts, w, m, v)


def _adamw_math(p_refs, w_ref, m_ref, v_ref, g_ref, d_ref, nm_ref, nv_ref, c1, c2):
    g = p_refs[0][...].astype(F32)
    for p in p_refs[1:]:
        g = g + p[...].astype(F32)
    nm = ADAM_B1 * m_ref[...] + (1.0 - ADAM_B1) * g
    nv = ADAM_B2 * v_ref[...] + (1.0 - ADAM_B2) * (g * g)
    g_ref[...] = g
    nm_ref[...] = nm
    nv_ref[...] = nv
    d_ref[...] = -ADAM_LR * ((nm / c1) / (jnp.sqrt(nv / c2) + ADAM_EPS) + ADAM_WD * w_ref[...])


def _adamw_blocks(parts, idx, w, m, v, name, cols=None, prev=None):
    r, c = w.shape
    tr = _tile(r, 256)
    npart = len(parts)
    cw, cj = cols if cols is not None else (c, 0)
    c1 = 1.0 - ADAM_B1 ** ADAM_STEP
    c2 = 1.0 - ADAM_B2 ** ADAM_STEP

    def body(idx_ref, *refs):
        _adamw_math(refs[:npart], *refs[npart:npart + 3], *refs[-4:], c1, c2)

    def sel(p):
        return pl.BlockSpec((None, tr, cw), lambda i, s: (s[p], i, 0))

    blk = pl.BlockSpec((tr, cw), lambda i, s: (i, cj))
    args = [idx, *parts, w, m, v]
    in_specs = [sel(p) for p in range(npart)] + [blk] * 3
    aliases = None
    if prev is not None:
        args += list(prev)
        in_specs += [ANY] * 4
        aliases = {1 + npart + 3 + o: o for o in range(4)}
    gs = pltpu.PrefetchScalarGridSpec(num_scalar_prefetch=1, grid=(r // tr,), in_specs=in_specs, out_specs=[blk] * 4)
    return _pc(body, name=name, out_shape=[jax.ShapeDtypeStruct((r, c), F32)] * 4, grid_spec=gs,
               sem=("parallel",), aliases=aliases)(*args)


_EARLY = ["a_ln_gain", "a_ln_bias", "a_w_s", "a_b_s", "b_lower_bounds", "b_gn_gain"]


def _pack(arrs):
    flat = jnp.concatenate([a.reshape(-1) for a in arrs])
    rows = -(-flat.shape[0] // 1024) * 8
    return jnp.pad(flat, (0, rows * 128 - flat.shape[0])).reshape(rows, 128)


def _unpack(buf, like):
    flat = buf.reshape(-1)
    out, off = [], 0
    for a in like:
        out.append(flat[off:off + a.size].reshape(a.shape))
        off += a.size
    return out


def kernel(x, c, norm_gain, w_ada, b_ada, a_w_in, a_ln_gain, a_ln_bias, a_w_s, a_b_s, a_w_out, b_w_in, b_lower_bounds, b_gn_gain, b_w_out, final_gain, loss_target, m_norm_gain, m_w_ada, m_b_ada, m_a_w_in, m_a_ln_gain, m_a_ln_bias, m_a_w_s, m_a_b_s, m_a_w_out, m_b_w_in, m_b_lower_bounds, m_b_gn_gain, m_b_w_out, m_final_gain, v_norm_gain, v_w_ada, v_b_ada, v_a_w_in, v_a_ln_gain, v_a_ln_bias, v_a_w_s, v_a_b_s, v_a_w_out, v_b_w_in, v_b_lower_bounds, v_b_gn_gain, v_b_w_out, v_final_gain):
    w = dict(norm_gain=norm_gain, w_ada=w_ada, b_ada=b_ada, a_w_in=a_w_in, a_ln_gain=a_ln_gain,
             a_ln_bias=a_ln_bias, a_w_s=a_w_s, a_b_s=a_b_s, a_w_out=a_w_out, b_w_in=b_w_in,
             b_lower_bounds=b_lower_bounds, b_gn_gain=b_gn_gain, b_w_out=b_w_out, final_gain=final_gain)
    mo = dict(norm_gain=m_norm_gain, w_ada=m_w_ada, b_ada=m_b_ada, a_w_in=m_a_w_in, a_ln_gain=m_a_ln_gain,
              a_ln_bias=m_a_ln_bias, a_w_s=m_a_w_s, a_b_s=m_a_b_s, a_w_out=m_a_w_out, b_w_in=m_b_w_in,
              b_lower_bounds=m_b_lower_bounds, b_gn_gain=m_b_gn_gain, b_w_out=m_b_w_out, final_gain=m_final_gain)
    vo = dict(norm_gain=v_norm_gain, w_ada=v_w_ada, b_ada=v_b_ada, a_w_in=v_a_w_in, a_ln_gain=v_a_ln_gain,
              a_ln_bias=v_a_ln_bias, a_w_s=v_a_w_s, a_b_s=v_a_b_s, a_w_out=v_a_w_out, b_w_in=v_b_w_in,
              b_lower_bounds=v_b_lower_bounds, b_gn_gain=v_b_gn_gain, b_w_out=v_b_w_out, final_gain=v_final_gain)

    nb, t_seq, d = x.shape
    m = nb * t_seq
    ncol_ada = w_ada.shape[2]
    xi, yi, ci = lax.axis_index("x"), lax.axis_index("y"), lax.axis_index("c")
    me = 4 * xi + 2 * yi + ci

    c_g, wa_in_g = _all_gather([c, a_w_in[0].astype(BF16)], "gather_c_wa")

    c_all = c_g.reshape(NDEV * nb, d)
    b_cols = lax.dynamic_slice(b_ada, (0, me * ncol_ada), (2, ncol_ada)).reshape(2, 1, ncol_ada)
    mod_part, lbj = _ada_fwd(c_all, w_ada, b_cols, b_lower_bounds)
    mod_all = _all_gather([mod_part], "gather_mod")[0]
    mod_mine = lax.dynamic_slice_in_dim(mod_all, me * nb, nb, axis=2)
    mod_mine = mod_mine.transpose(1, 2, 0, 3).reshape(2, nb, 3, d)
    mod0, mod1 = mod_mine[0], mod_mine[1]

    di = a_w_out.shape[1] * NDEV

    xf = x.reshape(m, d)
    tgt = loss_target.reshape(m, d)
    ng0, ng1 = norm_gain[0:1], norm_gain[1:2]
    ncb = b_w_in.shape[2]
    wb_lo, wb_hi = b_w_in[0][:, :ncb // 2].astype(BF16), b_w_in[0][:, ncb // 2:].astype(BF16)
    h0, h0_t = _prenorm(xf, ng0, mod0, t_seq, "prenorm_a")
    proj_a, half = _mm_in(h0, [wa_in_g], 1, "in_proj_a", comm=_gather_first([a_w_out[0].astype(BF16), wb_lo]))
    bs_t = jnp.pad(a_b_s[0].T, ((0, 0), (0, 128 - SG_GROUPS)))
    ybr_a, (wa_out_g, wb_lo_g, wb_hi_half) = _a_mid_fwd(
        proj_a, a_ln_gain, a_ln_bias, a_w_s[0], bs_t, t_seq, comm=_join(_gather_second(half), _gather_first([wb_hi])))
    wa_out = wa_out_g.reshape(di, d)
    (yout_a, x1), (wb_hi_g, wb_out_half) = _out_proj(
        ybr_a, wa_out, xf, mod0, t_seq, "out_proj_a",
        comm=_join(_gather_second([wb_hi_half]), _gather_first([b_w_out[0].astype(BF16)])))
    wb_in_g = [wb_lo_g, wb_hi_g]
    h1, h1_t = _prenorm(x1, ng1, mod1, t_seq, "prenorm_b")
    proj_b, (wb_out_g,) = _mm_in(h1, wb_in_g, 4, "in_proj_b", comm=_gather_second([wb_out_half]))
    wb_out = wb_out_g.reshape(di, d)
    o_b, ybr_b, states = _hgrn_fwd(proj_b, lbj, b_gn_gain, nb, t_seq)
    yout_b, dx2, loss_part, d_final_gain = _out_proj_loss(ybr_b, wb_out, x1, mod1, final_gain.reshape(1, d), tgt, t_seq)

    rows_out = a_w_out.shape[1]
    dy_b, dgate1, dybr_b = _gate_dybr(dx2, yout_b, mod1, wb_out, t_seq, "dybr_b")
    rs_wb_out = _ReduceScatter(_mm_dw_out(ybr_b, dy_b, "dw_out_b").reshape(NDEV, rows_out, d), "b_w_out")
    (dproj_b, d_lb, d_gn), got = _hgrn_bwd(proj_b, o_b, dybr_b, states, lbj, b_gn_gain, nb, t_seq,
                                           comm=rs_wb_out.swap_core())
    rs_wb_out.after_core(got[0])
    dh1, got = _mm_din(dproj_b, wb_in_g, 4, "dh_b", comm=rs_wb_out.swap_chips())
    rs_wb_out.after_chips(got[0])
    dx1, dss1, dgain1 = _prenorm_bwd(dh1, x1, ng1, mod1, dx2, t_seq, "prenorm_bwd_b")
    rs_wb_in = _ReduceScatter(_mm_dw_in(h1_t, dproj_b, ncb, 4, "dw_in_b"), "b_w_in")

    dy_a, dgate0, dybr_a = _gate_dybr(dx1, yout_a, mod0, wa_out, t_seq, "dybr_a")
    g_wa_out, got = _mm_dw_out(ybr_a, dy_a, "dw_out_a", comm=rs_wb_in.swap_core())
    rs_wb_in.after_core(got[0])
    rs_wa_out = _ReduceScatter(g_wa_out.reshape(NDEV, rows_out, d), "a_w_out")
    (dproj_a, d_lng, d_lnb, d_ws, d_bs_t), got = _a_mid_bwd(
        proj_a, dybr_a, a_ln_gain, a_ln_bias, a_w_s[0], bs_t, t_seq,
        comm=_join(rs_wb_in.swap_chips(), rs_wa_out.swap_core()))
    rs_wb_in.after_chips(got[0])
    rs_wa_out.after_core(got[1])
    part = dict(a_ln_gain=d_lng, a_ln_bias=d_lnb, a_w_s=d_ws[None], a_b_s=d_bs_t[:, :SG_GROUPS].T[None],
                b_lower_bounds=jnp.concatenate([-d_lb, d_lb], axis=0), b_gn_gain=d_gn)
    early_pack = _pack([part[k].reshape(w[k].shape) for k in _EARLY])
    nca = wa_in_g.shape[2]
    unit = nca // 3
    g_lo, got = _mm_dw_in_cols(h0_t, dproj_a, nca, unit, 0, 2, "dw_in_a_lo",
                               comm=_join(rs_wa_out.swap_chips(), _gather_first([early_pack])))
    rs_wa_out.after_chips(got[0])
    rs_lo = _ReduceScatter(g_lo, "a_w_in_lo")
    g_hi, got = _mm_dw_in_cols(h0_t, dproj_a, nca, unit, 2, 1, "dw_in_a_hi",
                               comm=_join(rs_lo.swap_core(), _gather_second([got[1]])))
    rs_lo.after_core(got[0])
    early_all = got[1]
    rs_hi = _ReduceScatter(g_hi, "a_w_in_hi")
    n_tiles = m // _din_tile(m)
    assert n_tiles >= 2
    first_tiles = n_tiles // 2
    dh0, got = _mm_din(dproj_a, [wa_in_g], 1, "dh_a_first", tiles=(0, first_tiles),
                       comm=_join(rs_hi.swap_core(), rs_lo.swap_chips()))
    rs_hi.after_core(got[0])
    rs_lo.after_chips(got[1])
    dh0, got = _mm_din(dproj_a, [wa_in_g], 1, "dh_a_rest", comm=rs_hi.swap_chips(),
                       tiles=(first_tiles, n_tiles - first_tiles), prev=dh0)
    rs_hi.after_chips(got[0])
    dx0, dss0, dgain0 = _prenorm_bwd(dh0, xf, ng0, mod0, dx1, t_seq, "prenorm_bwd_a")
    grad_x = dx0.reshape(nb, t_seq, d)

    dmod = jnp.stack([jnp.concatenate([dss0, dgate0], axis=1), jnp.concatenate([dss1, dgate1], axis=1)])
    late_like = [norm_gain, final_gain, loss_part.reshape(1)]
    late_pack = _pack([jnp.concatenate([dgain0, dgain1], axis=0), d_final_gain[0], loss_part.reshape(1)])
    dmod_all, late_all = _all_gather([dmod.reshape(2, nb, 3 * d), late_pack], "gather_tail")
    dmod_all = dmod_all.transpose(1, 0, 2, 3).reshape(2, NDEV * nb, 3 * d)
    dmod_cols = lax.dynamic_slice_in_dim(dmod_all, me * ncol_ada, ncol_ada, axis=2)
    g_w_ada, g_b_ada = _ada_bwd(c_all, dmod_cols, dmod_all)

    res = {}
    early_like = [w[k] for k in _EARLY]
    dev_order = jnp.arange(NDEV, dtype=jnp.int32)
    sm = _adamw_blocks([early_all] * NDEV, dev_order, _pack(early_like), _pack([mo[k] for k in _EARLY]),
                       _pack([vo[k] for k in _EARLY]), "adamw_small_early")
    sm = [dict(zip(_EARLY, _unpack(buf, early_like))) for buf in sm]
    for k in _EARLY:
        res[k] = tuple(s[k] for s in sm)
    zero = jnp.zeros((1,), F32)
    sm = _adamw_blocks([late_all] * NDEV, dev_order, _pack([norm_gain, final_gain, zero]),
                       _pack([mo["norm_gain"], mo["final_gain"], zero]),
                       _pack([vo["norm_gain"], vo["final_gain"], zero]), "adamw_small_late")
    sm = [_unpack(buf, late_like) for buf in sm]
    res["norm_gain"] = tuple(s[0] for s in sm)
    res["final_gain"] = tuple(s[1] for s in sm)
    loss = sm[0][2][0]
    rb = _adamw([g_b_ada], b_ada, mo["b_ada"], vo["b_ada"], "adamw_b_ada")
    res["b_ada"] = tuple(rb)
    sh = w_ada.shape
    ra = _adamw([g_w_ada.reshape(sh[0] * sh[1], sh[2])], w_ada.reshape(sh[0] * sh[1], sh[2]),
                mo["w_ada"].reshape(sh[0] * sh[1], sh[2]), vo["w_ada"].reshape(sh[0] * sh[1], sh[2]), "adamw_w_ada")
    res["w_ada"] = tuple(z.reshape(sh) for z in ra)

    for k, rs in (("b_w_out", rs_wb_out), ("b_w_in", rs_wb_in), ("a_w_out", rs_wa_out)):
        res[k] = tuple(z[None] for z in _adamw_blocks(rs.parts, rs.idx, w[k][0], mo[k][0], vo[k][0], "adamw_" + k))
    wa, ma, va = w["a_w_in"][0], mo["a_w_in"][0], vo["a_w_in"][0]
    r_lo = _adamw_blocks(rs_lo.parts, rs_lo.idx, wa, ma, va, "adamw_a_w_in_lo", cols=(2 * unit, 0))
    r_all = _adamw_blocks(rs_hi.parts, rs_hi.idx, wa, ma, va, "adamw_a_w_in_hi", cols=(unit, 2), prev=r_lo)
    res["a_w_in"] = tuple(z[None] for z in r_all)

    order = ["norm_gain", "w_ada", "b_ada", "a_w_in", "a_ln_gain", "a_ln_bias", "a_w_s", "a_b_s", "a_w_out",
             "b_w_in", "b_lower_bounds", "b_gn_gain", "b_w_out", "final_gain"]
    return (loss, grad_x, *[res[k][0] for k in order], *[res[k][1] for k in order],
            *[res[k][2] for k in order], *[res[k][3] for k in order])
```

```python
import functools
import math

import jax
import jax.numpy as jnp
from jax import lax
from jax.experimental import pallas as pl
from jax.experimental.pallas import tpu as pltpu

F32 = jnp.float32
BF16 = jnp.bfloat16
MESH = pl.DeviceIdType.MESH
NDEV = 8
EPS = 1e-6
CHUNK = 64
SG_BLOCK = 128
SG_GROUPS = 8
HEAD_DIM = 128
CUM_ROWS = 256
PHASE_HEADS = 8
ADAM_LR, ADAM_B1, ADAM_B2, ADAM_EPS, ADAM_WD, ADAM_STEP = 0.001, 0.9, 0.999, 1e-08, 0.01, 10
VMEM_LIMIT = 56 * 1024 * 1024
ANY = pl.BlockSpec(memory_space=pl.ANY)


class _Hosted:
    def __init__(self, arrays, out_shapes, nsem, start, finish, aliases=None):
        self.arrays, self.out_shapes, self.nsem = list(arrays), list(out_shapes), nsem
        self.start, self.finish = start, finish
        self.aliases = dict(aliases or {})


def _join(*comms):
    arrays, outs, aliases, offs, nsem = [], [], {}, [], 0
    for cm in comms:
        offs.append((len(arrays), len(outs), nsem))
        for i, o in cm.aliases.items():
            aliases[len(arrays) + i] = len(outs) + o
        arrays += cm.arrays
        outs += cm.out_shapes
        nsem += cm.nsem

    def run(which):
        def f(ins, outs_, ss, rs, base):
            for cm, (ia, io, isem) in zip(comms, offs):
                getattr(cm, which)(ins[ia:ia + len(cm.arrays)], outs_[io:io + len(cm.out_shapes)], ss, rs, base + isem)
        return f

    return _Hosted(arrays, outs, nsem, run("start"), run("finish"), aliases)


def _pc(body, *, name, out_shape, grid=None, in_specs=None, out_specs=None, scratch=(), sem=None,
        grid_spec=None, comm=None, aliases=None):
    cp = dict(vmem_limit_bytes=VMEM_LIMIT)
    aliases = dict(aliases or {})
    if comm is None:
        if sem is not None:
            cp["dimension_semantics"] = sem
        kw = {"input_output_aliases": aliases}
        if grid_spec is not None:
            kw["grid_spec"] = grid_spec
        else:
            if grid is not None:
                kw["grid"] = grid
            if in_specs is not None:
                kw["in_specs"] = in_specs
            if out_specs is not None:
                kw["out_specs"] = out_specs
            kw["scratch_shapes"] = list(scratch)
        return pl.pallas_call(functools.partial(body), name=name, out_shape=out_shape,
                              compiler_params=pltpu.CompilerParams(**cp), **kw)

    single = not isinstance(out_shape, (list, tuple))
    outs_list = [out_shape] if single else list(out_shape)
    ospecs = [out_specs] if single else list(out_specs)
    n_in, n_out, n_ci, n_co, n_scr = len(in_specs), len(outs_list), len(comm.arrays), len(comm.out_shapes), len(scratch)
    cp["dimension_semantics"] = ("arbitrary",) * len(grid)

    def hosted(*refs):
        cin, hin = refs[:n_in], refs[n_in:n_in + n_ci]
        cout = refs[n_in + n_ci:n_in + n_ci + n_out]
        hout = refs[n_in + n_ci + n_out:n_in + n_ci + n_out + n_co]
        scr = refs[n_in + n_ci + n_out + n_co:n_in + n_ci + n_out + n_co + n_scr]
        ssem, rsem = refs[-2], refs[-1]
        first = functools.reduce(lambda p, q: p & q, [pl.program_id(a) == 0 for a in range(len(grid))])
        last = functools.reduce(lambda p, q: p & q, [pl.program_id(a) == grid[a] - 1 for a in range(len(grid))])

        @pl.when(first)
        def _():
            comm.start(hin, hout, ssem, rsem, 0)

        body(*cin, *cout, *scr)

        @pl.when(last)
        def _():
            comm.finish(hin, hout, ssem, rsem, 0)

    call = pl.pallas_call(
        hosted, name=name, grid=grid, in_specs=list(in_specs) + [ANY] * n_ci, out_specs=ospecs + [ANY] * n_co,
        out_shape=outs_list + comm.out_shapes,
        scratch_shapes=list(scratch) + [pltpu.SemaphoreType.DMA((comm.nsem,)), pltpu.SemaphoreType.DMA((comm.nsem,))],
        input_output_aliases={**aliases, **{n_in + i: n_out + o for i, o in comm.aliases.items()}},
        compiler_params=pltpu.CompilerParams(**cp))

    def run(*args):
        res = call(*args, *comm.arrays)
        comp = res[:n_out]
        return (comp[0] if single else comp), list(res[n_out:])

    return run


def _tile(n, pref):
    return pref if n % pref == 0 else n


def _sigmoid(x):
    return 1.0 / (1.0 + jnp.exp(-x))


def _gelu(x):
    c = math.sqrt(2.0 / math.pi)
    return 0.5 * x * (1.0 + jnp.tanh(c * (x + 0.044715 * (x * x * x))))


def _gelu_and_grad(x):
    c = math.sqrt(2.0 / math.pi)
    x2 = x * x
    t = jnp.tanh(c * (x + 0.044715 * (x2 * x)))
    half = 0.5 * (1.0 + t)
    return x * half, half + (0.5 * x) * (1.0 - t * t) * (c + (3.0 * 0.044715 * c) * x2)


def _dot(a, b):
    return jnp.dot(a, b, preferred_element_type=F32)


def _dot_nt(a, b):
    return lax.dot_general(a, b, (((1,), (1,)), ((), ())), preferred_element_type=F32)


def _dot_tn(a, b):
    return lax.dot_general(a, b, (((0,), (0,)), ((), ())), preferred_element_type=F32)


def _tri_mask(n, reverse):
    r = lax.broadcasted_iota(jnp.int32, (n, n), 0)
    c = lax.broadcasted_iota(jnp.int32, (n, n), 1)
    same = (r // CHUNK) == (c // CHUNK)
    tri = (c >= r) if reverse else (c <= r)
    return jnp.where(same & tri, 1.0, 0.0).astype(BF16)


def _tri_apply(tri, x):
    hi = x.astype(BF16)
    r1 = x - hi.astype(F32)
    mid = r1.astype(BF16)
    lo = (r1 - mid.astype(F32)).astype(BF16)
    return _dot(tri, hi) + (_dot(tri, mid) + _dot(tri, lo))


def _all_gather(arrs, name):
    n = len(arrs)

    def body(*refs):
        ins, outs = refs[:n], refs[n:2 * n]
        send_sems, recv_sems, local_sems = refs[2 * n:]
        x, y, c = lax.axis_index("x"), lax.axis_index("y"), lax.axis_index("c")
        me, sibling = (x, y, c), (x, y, 1 - c)
        near = (x + c - 2 * x * c, y + (1 - c) - 2 * y * (1 - c))
        far = (x + (1 - c) - 2 * x * (1 - c), y + c - 2 * y * c)
        diag = (1 - x, 1 - y)

        def blk(a, p):
            return outs[a].at[4 * p[0] + 2 * p[1] + p[2]]

        def copy(a, k, block, to, src=None):
            return pltpu.make_async_remote_copy(
                src_ref=blk(a, block) if src is None else src, dst_ref=blk(a, block),
                send_sem=send_sems.at[7 * a + k], recv_sem=recv_sems.at[7 * a + k],
                device_id=to, device_id_type=MESH)

        mine = [pltpu.make_async_copy(ins[a], blk(a, me), local_sems.at[a]) for a in range(n)]
        for m in mine:
            m.start()
        sends = []
        for a in range(n):
            sends += [copy(a, 0, me, sibling, src=ins[a]), copy(a, 1, me, (*near, c), src=ins[a]),
                      copy(a, 2, me, (*far, c), src=ins[a])]
        for cp in sends:
            cp.start()
        for a in range(n):
            copy(a, 1, (*near, c), me).wait_recv()
            sends.append(copy(a, 3, (*near, c), (*far, c)))
            sends[-1].start()
        for a in range(n):
            sends.append(copy(a, 4, (*near, c), sibling))
            sends[-1].start()
            copy(a, 2, (*far, c), me).wait_recv()
            sends.append(copy(a, 5, (*far, c), sibling))
            sends[-1].start()
        for a in range(n):
            copy(a, 3, (*diag, c), me).wait_recv()
            sends.append(copy(a, 6, (*diag, c), sibling))
            sends[-1].start()
        for a in range(n):
            copy(a, 0, sibling, me).wait_recv()
            copy(a, 4, (*far, 1 - c), me).wait_recv()
            copy(a, 5, (*near, 1 - c), me).wait_recv()
            copy(a, 6, (*diag, 1 - c), me).wait_recv()
        for cp in sends:
            cp.wait_send()
        for m in mine:
            m.wait()

    out_shape = [jax.ShapeDtypeStruct((NDEV,) + a.shape, a.dtype) for a in arrs]
    return _pc(body, name=name, out_shape=out_shape, in_specs=[ANY] * n, out_specs=[ANY] * n,
               scratch=[pltpu.SemaphoreType.DMA((7 * n,)), pltpu.SemaphoreType.DMA((7 * n,)),
                        pltpu.SemaphoreType.DMA((n,))])(*arrs)


def _gather_first(arrs):
    n = len(arrs)

    def parts(ins, outs, ss, rs, base):
        x, y, c = lax.axis_index("x"), lax.axis_index("y"), lax.axis_index("c")
        me, sibling = (x, y, c), (x, y, 1 - c)
        chips = [(1 - x, y), (x, 1 - y), (1 - x, 1 - y)]

        def blk(a, p):
            return outs[a].at[4 * p[0] + 2 * p[1] + p[2]]

        def copy(a, k, block, to):
            return pltpu.make_async_remote_copy(
                src_ref=ins[a], dst_ref=blk(a, block), send_sem=ss.at[base + 4 * a + k],
                recv_sem=rs.at[base + 4 * a + k], device_id=to, device_id_type=MESH)

        local = [pltpu.make_async_copy(ins[a], blk(a, me), ss.at[base + 4 * n + a]) for a in range(n)]
        sends, recvs = [], []
        for a in range(n):
            sends.append(copy(a, 0, me, sibling))
            recvs.append(copy(a, 0, sibling, me))
            for j, chip in enumerate(chips):
                sends.append(copy(a, 1 + j, me, (*chip, c)))
                recvs.append(copy(a, 1 + j, (*chip, c), me))
        return local, sends, recvs

    def start(ins, outs, ss, rs, base):
        local, sends, _ = parts(ins, outs, ss, rs, base)
        for cp in local + sends:
            cp.start()

    def finish(ins, outs, ss, rs, base):
        local, sends, recvs = parts(ins, outs, ss, rs, base)
        for cp in recvs:
            cp.wait_recv()
        for cp in sends:
            cp.wait_send()
        for cp in local:
            cp.wait()

    return _Hosted(arrs, [jax.ShapeDtypeStruct((NDEV,) + a.shape, a.dtype) for a in arrs], 5 * n, start, finish)


def _gather_second(bufs):
    n = len(bufs)

    def parts(ins, outs, ss, rs, base):
        x, y, c = lax.axis_index("x"), lax.axis_index("y"), lax.axis_index("c")
        sibling = (x, y, 1 - c)
        chips = [(1 - x, y), (x, 1 - y), (1 - x, 1 - y)]
        sends, recvs = [], []
        for a in range(n):
            for j, chip in enumerate(chips):
                mine = 4 * chip[0] + 2 * chip[1] + c
                theirs = 4 * chip[0] + 2 * chip[1] + (1 - c)
                sends.append(pltpu.make_async_remote_copy(
                    src_ref=ins[a].at[mine], dst_ref=outs[a].at[mine], send_sem=ss.at[base + 3 * a + j],
                    recv_sem=rs.at[base + 3 * a + j], device_id=sibling, device_id_type=MESH))
                recvs.append(pltpu.make_async_remote_copy(
                    src_ref=ins[a].at[theirs], dst_ref=outs[a].at[theirs], send_sem=ss.at[base + 3 * a + j],
                    recv_sem=rs.at[base + 3 * a + j], device_id=sibling, device_id_type=MESH))
        return sends, recvs

    def start(ins, outs, ss, rs, base):
        for cp in parts(ins, outs, ss, rs, base)[0]:
            cp.start()

    def finish(ins, outs, ss, rs, base):
        sends, recvs = parts(ins, outs, ss, rs, base)
        for cp in recvs:
            cp.wait_recv()
        for cp in sends:
            cp.wait_send()

    return _Hosted(bufs, [jax.ShapeDtypeStruct(b.shape, b.dtype) for b in bufs], 3 * n, start, finish,
                   aliases={a: a for a in range(n)})


def _swap(src, nblk, ids_fn, partner_fn):
    def copies(ins, outs, ss, rs, base):
        x, y, c = lax.axis_index("x"), lax.axis_index("y"), lax.axis_index("c")
        ids = ids_fn(x, y, c)
        partner = partner_fn(x, y, c)
        return [pltpu.make_async_remote_copy(
            src_ref=ins[0].at[ids[k]], dst_ref=outs[0].at[k], send_sem=ss.at[base + k], recv_sem=rs.at[base + k],
            device_id=partner, device_id_type=MESH) for k in range(nblk)]

    def start(ins, outs, ss, rs, base):
        for cp in copies(ins, outs, ss, rs, base):
            cp.start()

    def finish(ins, outs, ss, rs, base):
        for cp in copies(ins, outs, ss, rs, base):
            cp.wait()

    return _Hosted([src], [jax.ShapeDtypeStruct((nblk,) + src.shape[1:], src.dtype)], nblk, start, finish)


def _swap_chips(send):
    def copies(ins, outs, ss, rs, base):
        x, y, c = lax.axis_index("x"), lax.axis_index("y"), lax.axis_index("c")
        chips = [(1 - x, y), (x, 1 - y), (1 - x, 1 - y)]
        return [pltpu.make_async_remote_copy(
            src_ref=ins[0].at[j], dst_ref=outs[0].at[j], send_sem=ss.at[base + j], recv_sem=rs.at[base + j],
            device_id=(*chip, c), device_id_type=MESH) for j, chip in enumerate(chips)]

    def start(ins, outs, ss, rs, base):
        for cp in copies(ins, outs, ss, rs, base):
            cp.start()

    def finish(ins, outs, ss, rs, base):
        for cp in copies(ins, outs, ss, rs, base):
            cp.wait()

    return _Hosted([send], [jax.ShapeDtypeStruct(send.shape, send.dtype)], 3, start, finish)


def _add_send(a, b, idx, ns, name):
    _, r, c = a.shape
    tr = _tile(r, 256)

    def body(idx_ref, a_ref, b_ref, send_ref):
        send_ref[...] = (a_ref[...] + b_ref[...]).astype(BF16)

    def sel(off):
        return pl.BlockSpec((None, tr, c), lambda k, i, s: (s[off + k], i, 0))

    gs = pltpu.PrefetchScalarGridSpec(num_scalar_prefetch=1, grid=(ns, r // tr), in_specs=[sel(0), sel(ns)],
                                      out_specs=pl.BlockSpec((None, tr, c), lambda k, i, s: (k, i, 0)))
    return _pc(body, name=name, grid_spec=gs, sem=("arbitrary", "arbitrary"),
               out_shape=jax.ShapeDtypeStruct((ns, r, c), BF16))(idx, a, b)


class _ReduceScatter:
    def __init__(self, g, tag):
        self.g, self.tag = g, tag

    def swap_core(self):
        return _swap(self.g, 4, lambda x, y, c: [1 - c, 3 - c, 5 - c, 7 - c], lambda x, y, c: (x, y, 1 - c))

    def after_core(self, recv):
        x, y, c = lax.axis_index("x"), lax.axis_index("y"), lax.axis_index("c")
        chips = [(1 - x, y), (x, 1 - y), (1 - x, 1 - y)]
        idx = jnp.stack([4 * p + 2 * q + c for p, q in chips] + [2 * p + q for p, q in chips]).astype(jnp.int32)
        self.send = _add_send(self.g, recv, idx, 3, "rs_add_" + self.tag)
        self.recv_core = recv
        zero = jnp.zeros((), jnp.int32)
        self.idx = jnp.stack([4 * x + 2 * y + c, 2 * x + y, zero, zero + 1, zero + 2]).astype(jnp.int32)

    def swap_chips(self):
        return _swap_chips(self.send)

    def after_chips(self, recv):
        self.parts = [self.g, self.recv_core, recv, recv, recv]


def _ada_fwd(c_all, w_ada, b_cols, b_lb):
    nl, d, ncol = w_ada.shape
    nseq = c_all.shape[0]
    di = b_lb.shape[1]

    def body(c_ref, w_ref, b_ref, lb_ref, mod_ref, lbj_ref):
        cv = c_ref[...]
        cact = (cv * _sigmoid(cv)).astype(BF16)
        for l in range(nl):
            mod_ref[l] = _dot(cact, w_ref[l].astype(BF16)) + b_ref[l]
        b0, b1 = lb_ref[0:1, :], lb_ref[1:2, :]
        mx = jnp.maximum(b0, b1)
        e0, e1 = jnp.exp(b0 - mx), jnp.exp(b1 - mx)
        s = e0 + e1
        p0, p1 = e0 / s, e1 / s
        lbj_ref[0:1, :] = (p0 + p1) - p0
        lbj_ref[1:2, :] = p0 * p1

    return _pc(body, name="ada_fwd",
               out_shape=[jax.ShapeDtypeStruct((nl, nseq, ncol), F32), jax.ShapeDtypeStruct((2, di), F32)]
               )(c_all, w_ada, b_cols, b_lb)


def _ada_bwd(c_all, dmod_cols, dmod_full):
    nl, nseq, ncol = dmod_cols.shape
    d = c_all.shape[1]
    d3 = dmod_full.shape[2]

    def body(c_ref, dc_ref, df_ref, gw_ref, gb_ref):
        cv = c_ref[...]
        cact = (cv * _sigmoid(cv)).astype(BF16)
        for l in range(nl):
            gw_ref[l] = _dot_tn(cact, dc_ref[l].astype(BF16))
            gb_ref[l:l + 1, :] = jnp.sum(df_ref[l], axis=0, keepdims=True)

    return _pc(body, name="ada_bwd",
               out_shape=[jax.ShapeDtypeStruct((nl, d, ncol), F32), jax.ShapeDtypeStruct((nl, d3), F32)]
               )(c_all, dmod_cols, dmod_full)


def _prenorm(x, gain, mod, t_seq, name):
    m, d = x.shape
    tm = _tile(t_seq, 1024)
    per = t_seq // tm

    def body(x_ref, g_ref, mod_ref, h_ref, ht_ref):
        xv = x_ref[...]
        rstd = lax.rsqrt(jnp.mean(xv * xv, axis=-1, keepdims=True) + EPS)
        r = xv * rstd * g_ref[...]
        h = r * (1.0 + mod_ref[0, 1:2, :]) + mod_ref[0, 0:1, :]
        h_ref[...] = h.astype(BF16)
        ht_ref[...] = h.T.astype(BF16)

    return _pc(body, name=name, out_shape=[jax.ShapeDtypeStruct((m, d), BF16), jax.ShapeDtypeStruct((d, m), BF16)],
               grid=(m // tm,),
               in_specs=[pl.BlockSpec((tm, d), lambda i: (i, 0)), pl.BlockSpec((1, d), lambda i: (0, 0)),
                         pl.BlockSpec((1, 3, d), lambda i: (i // per, 0, 0))],
               out_specs=[pl.BlockSpec((tm, d), lambda i: (i, 0)), pl.BlockSpec((d, tm), lambda i: (0, i))],
               sem=("parallel",))(x, gain, mod)


def _prenorm_bwd(dh, x, gain, mod, dxn, t_seq, name):
    m, d = x.shape
    nb = m // t_seq
    tm = _tile(t_seq, 1024)
    per = t_seq // tm

    def body(dh_ref, x_ref, g_ref, mod_ref, dxn_ref, dx_ref, dss_ref, dg_ref):
        i = pl.program_id(0)
        xv, dhv, g = x_ref[...], dh_ref[...], g_ref[...]
        rstd = lax.rsqrt(jnp.mean(xv * xv, axis=-1, keepdims=True) + EPS)
        xhat = xv * rstd
        dr = dhv * (1.0 + mod_ref[0, 1:2, :])
        dxhat = dr * g
        dx_ref[...] = dxn_ref[...] + rstd * (dxhat - xhat * jnp.mean(dxhat * xhat, axis=-1, keepdims=True))

        @pl.when(i % per == 0)
        def _():
            dss_ref[...] = jnp.zeros_like(dss_ref)

        @pl.when(i == 0)
        def _():
            dg_ref[...] = jnp.zeros_like(dg_ref)

        dss_ref[0, 0:1, :] += jnp.sum(dhv, axis=0, keepdims=True)
        dss_ref[0, 1:2, :] += jnp.sum(dhv * (xhat * g), axis=0, keepdims=True)
        dg_ref[...] += jnp.sum(dr * xhat, axis=0, keepdims=True)

    row = pl.BlockSpec((tm, d), lambda i: (i, 0))
    return _pc(body, name=name,
               out_shape=[jax.ShapeDtypeStruct((m, d), F32), jax.ShapeDtypeStruct((nb, 2, d), F32),
                          jax.ShapeDtypeStruct((1, d), F32)],
               grid=(m // tm,),
               in_specs=[row, row, pl.BlockSpec((1, d), lambda i: (0, 0)),
                         pl.BlockSpec((1, 3, d), lambda i: (i // per, 0, 0)), row],
               out_specs=[row, pl.BlockSpec((1, 2, d), lambda i: (i // per, 0, 0)),
                          pl.BlockSpec((1, d), lambda i: (0, 0))],
               sem=("arbitrary",))(dh, x, gain, mod, dxn)


def _mm_in(h, ws, sections, name, comm=None):
    m, k = h.shape
    nw = len(ws)
    widths = [w.shape[2] for w in ws]
    offs = [sum(widths[:a]) for a in range(nw)]
    nc = sum(widths)
    per = NDEV // sections if sections > 1 else NDEV
    tm = _din_tile(m)
    assert per % 2 == 0

    def body(*refs):
        hv = refs[0][...]
        o_ref = refs[1 + nw]
        for b in range(2):
            for a in range(nw):
                lo = b * nc + offs[a]
                o_ref[:, lo:lo + widths[a]] = _dot(hv, refs[1 + a][b])

    w_specs = [pl.BlockSpec((2, k, wd), lambda j, i: (j, 0, 0)) for wd in widths]
    if sections > 1:
        out_shape = jax.ShapeDtypeStruct((sections, m, per * nc), F32)
        out_spec = pl.BlockSpec((None, tm, 2 * nc), lambda j, i: ((2 * j) // per, i, ((2 * j) % per) // 2))
    else:
        out_shape = jax.ShapeDtypeStruct((m, NDEV * nc), F32)
        out_spec = pl.BlockSpec((tm, 2 * nc), lambda j, i: (i, j))
    return _pc(body, name=name, out_shape=out_shape, grid=(NDEV // 2, m // tm),
               in_specs=[pl.BlockSpec((tm, k), lambda j, i: (i, 0))] + w_specs,
               out_specs=out_spec, sem=("parallel", "parallel"), comm=comm)(h, *ws)


def _din_tile(m):
    return 1024 if m % 1024 == 0 and m >= 2048 else _tile(m, 512)


def _mm_din(dproj, ws, sections, name, comm=None, tiles=None, prev=None):
    nw, k = len(ws), ws[0].shape[1]
    widths = [w.shape[2] for w in ws]
    offs = [sum(widths[:a]) for a in range(nw)]
    nc = sum(widths)
    m = dproj.shape[-2]
    tm = _din_tile(m)
    t0, nt = tiles if tiles is not None else (0, m // tm)
    per = NDEV // sections if sections > 1 else NDEV
    assert per % 2 == 0

    def body(*refs):
        d_ref, o_ref = refs[0], refs[-1]
        j = pl.program_id(1)
        acc = None
        for b in range(2):
            for a in range(nw):
                lo = b * nc + offs[a]
                term = _dot_nt(d_ref[:, lo:lo + widths[a]], refs[1 + a][b])
                acc = term if acc is None else acc + term

        @pl.when(j == 0)
        def _():
            o_ref[...] = acc

        @pl.when(j > 0)
        def _():
            o_ref[...] += acc

    if sections > 1:
        dspec = pl.BlockSpec((None, tm, 2 * nc), lambda i, j: ((2 * j) // per, i + t0, ((2 * j) % per) // 2))
    else:
        dspec = pl.BlockSpec((tm, 2 * nc), lambda i, j: (i + t0, j))
    in_specs = [dspec] + [pl.BlockSpec((2, k, wd), lambda i, j: (j, 0, 0)) for wd in widths]
    args = [dproj, *ws]
    if prev is not None:
        in_specs.append(ANY)
        args.append(prev)
    return _pc(body, name=name, out_shape=jax.ShapeDtypeStruct((m, k), F32), grid=(nt, NDEV // 2), in_specs=in_specs,
               out_specs=pl.BlockSpec((tm, k), lambda i, j: (i + t0, 0)), sem=("parallel", "arbitrary"),
               comm=comm, aliases={1 + nw: 0} if prev is not None else None)(*args)


def _mm_dw_in(ht, dproj, nc, sections, name, comm=None):
    k, m = ht.shape
    per = NDEV // sections if sections > 1 else NDEV

    def body(h_ref, d_ref, o_ref):
        o_ref[...] = _dot(h_ref[...], d_ref[...])

    if sections > 1:
        dspec = pl.BlockSpec((None, m, nc), lambda j: (j // per, 0, j % per))
    else:
        dspec = pl.BlockSpec((m, nc), lambda j: (0, j))
    return _pc(body, name=name, out_shape=jax.ShapeDtypeStruct((NDEV, k, nc), F32), grid=(NDEV,),
               in_specs=[pl.BlockSpec((k, m), lambda j: (0, 0)), dspec],
               out_specs=pl.BlockSpec((None, k, nc), lambda j: (j, 0, 0)),
               sem=("parallel",), comm=comm)(ht, dproj)


def _out_proj(ybr, w_out, x, mod, t_seq, name, comm=None):
    m, di = ybr.shape
    d = w_out.shape[1]
    tm = _tile(t_seq, 512)
    per = t_seq // tm

    def body(y_ref, w_ref, x_ref, mod_ref, yo_ref, xn_ref):
        yo = _dot(y_ref[...], w_ref[...])
        yo_ref[...] = yo
        xn_ref[...] = x_ref[...] + mod_ref[0, 2:3, :] * yo

    row = pl.BlockSpec((tm, d), lambda i: (i, 0))
    return _pc(body, name=name,
               out_shape=[jax.ShapeDtypeStruct((m, d), F32), jax.ShapeDtypeStruct((m, d), F32)],
               grid=(m // tm,),
               in_specs=[pl.BlockSpec((tm, di), lambda i: (i, 0)), pl.BlockSpec((di, d), lambda i: (0, 0)), row,
                         pl.BlockSpec((1, 3, d), lambda i: (i // per, 0, 0))],
               out_specs=[row, row], sem=("parallel",), comm=comm)(ybr, w_out, x, mod)


def _out_proj_loss(ybr, w_out, x, mod, gain, target, t_seq):
    m, di = ybr.shape
    d = w_out.shape[1]
    tm = _tile(t_seq, 512)
    per = t_seq // tm

    def body(y_ref, w_ref, x_ref, mod_ref, g_ref, t_ref, yo_ref, dx_ref, loss_ref, dg_ref):
        i = pl.program_id(0)
        yo = _dot(y_ref[...], w_ref[...])
        yo_ref[...] = yo
        xv = x_ref[...] + mod_ref[0, 2:3, :] * yo
        g = g_ref[...]
        rstd = lax.rsqrt(jnp.mean(xv * xv, axis=-1, keepdims=True) + EPS)
        xhat = xv * rstd
        err = xhat * g - t_ref[...]
        dy = err * (1.0 / d)
        dxhat = dy * g
        dx_ref[...] = rstd * (dxhat - xhat * jnp.mean(dxhat * xhat, axis=-1, keepdims=True))

        @pl.when(i == 0)
        def _():
            loss_ref[...] = jnp.zeros_like(loss_ref)
            dg_ref[...] = jnp.zeros_like(dg_ref)

        loss_ref[...] += 0.5 * jnp.sum(jnp.mean(err * err, axis=-1, keepdims=True), axis=0, keepdims=True)
        dg_ref[...] += jnp.sum(dy * xhat, axis=0, keepdims=True)

    row = pl.BlockSpec((tm, d), lambda i: (i, 0))
    vec = pl.BlockSpec((1, d), lambda i: (0, 0))
    return _pc(body, name="out_proj_loss",
               out_shape=[jax.ShapeDtypeStruct((m, d), F32), jax.ShapeDtypeStruct((m, d), F32),
                          jax.ShapeDtypeStruct((1, 1), F32), jax.ShapeDtypeStruct((1, d), F32)],
               grid=(m // tm,),
               in_specs=[pl.BlockSpec((tm, di), lambda i: (i, 0)), pl.BlockSpec((di, d), lambda i: (0, 0)), row,
                         pl.BlockSpec((1, 3, d), lambda i: (i // per, 0, 0)), vec, row],
               out_specs=[row, row, pl.BlockSpec((1, 1), lambda i: (0, 0)), vec],
               sem=("arbitrary",))(ybr, w_out, x, mod, gain, target)


def _gate_dybr(dxn, yout, mod, w_out, t_seq, name):
    m, d = dxn.shape
    di = w_out.shape[0]
    nb = m // t_seq
    tm = _tile(t_seq, 512)
    per = t_seq // tm

    def body(dxn_ref, yo_ref, mod_ref, w_ref, dy_ref, dgate_ref, o_ref):
        i = pl.program_id(0)
        dv = dxn_ref[...]
        dy = (mod_ref[0, 2:3, :] * dv).astype(BF16)
        dy_ref[...] = dy
        o_ref[...] = _dot_nt(dy, w_ref[...])

        @pl.when(i % per == 0)
        def _():
            dgate_ref[...] = jnp.zeros_like(dgate_ref)

        dgate_ref[0] += jnp.sum(dv * yo_ref[...], axis=0, keepdims=True)

    row = pl.BlockSpec((tm, d), lambda i: (i, 0))
    return _pc(body, name=name,
               out_shape=[jax.ShapeDtypeStruct((m, d), BF16), jax.ShapeDtypeStruct((nb, 1, d), F32),
                          jax.ShapeDtypeStruct((m, di), F32)],
               grid=(m // tm,),
               in_specs=[row, row, pl.BlockSpec((1, 3, d), lambda i: (i // per, 0, 0)),
                         pl.BlockSpec((di, d), lambda i: (0, 0))],
               out_specs=[row, pl.BlockSpec((1, 1, d), lambda i: (i // per, 0, 0)),
                          pl.BlockSpec((tm, di), lambda i: (i, 0))],
               sem=("arbitrary",))(dxn, yout, mod, w_out)


def _mm_dw_out(ybr, dy, name, comm=None):
    m, di = ybr.shape
    d = dy.shape[1]
    tn = _tile(di, 1024)

    def body(y_ref, dy_ref, o_ref):
        o_ref[...] = _dot_tn(y_ref[...], dy_ref[...])

    return _pc(body, name=name, out_shape=jax.ShapeDtypeStruct((di, d), F32), grid=(di // tn,),
               in_specs=[pl.BlockSpec((m, tn), lambda n: (0, n)), pl.BlockSpec((m, d), lambda n: (0, 0))],
               out_specs=pl.BlockSpec((tn, d), lambda n: (n, 0)), sem=("parallel",), comm=comm)(ybr, dy)


def _sgu_mask():
    t = lax.broadcasted_iota(jnp.int32, (SG_BLOCK, SG_BLOCK), 0)
    s = lax.broadcasted_iota(jnp.int32, (SG_BLOCK, SG_BLOCK), 1)
    return (s // CHUNK) <= (t // CHUNK)


def _a_mid_fwd(proj, ln_g, ln_b, w_s, bs_t, t_seq, comm=None):
    m, n3 = proj.shape
    di = n3 // 3
    gd = di // SG_GROUPS
    r = _tile(t_seq, 256)
    nblk = r // SG_BLOCK

    def body(p_ref, lg_ref, lb_ref, ws_ref, bs_ref, ybr_ref, s_scr):
        v = _gelu(p_ref[:, di:2 * di])
        mu = jnp.mean(v, axis=-1, keepdims=True)
        vc = v - mu
        rstd = lax.rsqrt(jnp.mean(vc * vc, axis=-1, keepdims=True) + EPS)
        vb = (vc * rstd * lg_ref[...] + lb_ref[...]).astype(BF16)
        mask = _sgu_mask()
        for gi in range(SG_GROUPS):
            ws = jnp.where(mask, ws_ref[gi], 0.0).astype(BF16)
            bcol = bs_ref[:, gi:gi + 1]
            for b in range(nblk):
                rows = slice(b * SG_BLOCK, (b + 1) * SG_BLOCK)
                cols = slice(gi * gd, (gi + 1) * gd)
                s_scr[rows, cols] = _dot(ws, vb[rows, cols]) + bcol
        gg = p_ref[:, 2 * di:]
        ybr_ref[...] = (_gelu(p_ref[:, :di]) * s_scr[...] * (gg * _sigmoid(gg))).astype(BF16)

    vec = pl.BlockSpec((1, di), lambda i: (0, 0))
    return _pc(body, name="a_mid_fwd", out_shape=jax.ShapeDtypeStruct((m, di), BF16), grid=(m // r,),
               in_specs=[pl.BlockSpec((r, n3), lambda i: (i, 0)), vec, vec,
                         pl.BlockSpec((SG_GROUPS, SG_BLOCK, SG_BLOCK), lambda i: (0, 0, 0)),
                         pl.BlockSpec((SG_BLOCK, 128), lambda i: (0, 0))],
               out_specs=pl.BlockSpec((r, di), lambda i: (i, 0)),
               scratch=[pltpu.VMEM((r, di), F32)], sem=("parallel",), comm=comm)(proj, ln_g, ln_b, w_s, bs_t)


def _a_mid_bwd(proj, dybr, ln_g, ln_b, w_s, bs_t, t_seq, comm=None):
    m, n3 = proj.shape
    di = n3 // 3
    gd = di // SG_GROUPS
    r = _tile(t_seq, 256)
    nblk = r // SG_BLOCK

    def body(p_ref, dy_ref, lg_ref, lb_ref, ws_ref, bs_ref,
             dp_ref, dlg_ref, dlb_ref, dws_ref, dbs_ref, s_scr, dvl_scr):
        i = pl.program_id(0)

        @pl.when(i == 0)
        def _():
            dlg_ref[...] = jnp.zeros_like(dlg_ref)
            dlb_ref[...] = jnp.zeros_like(dlb_ref)
            dws_ref[...] = jnp.zeros_like(dws_ref)
            dbs_ref[...] = jnp.zeros_like(dbs_ref)

        v, dgelu_v = _gelu_and_grad(p_ref[:, di:2 * di])
        mu = jnp.mean(v, axis=-1, keepdims=True)
        vc = v - mu
        rstd = lax.rsqrt(jnp.mean(vc * vc, axis=-1, keepdims=True) + EPS)
        vhat = vc * rstd
        lg = lg_ref[...]
        vb = (vhat * lg + lb_ref[...]).astype(BF16)
        u, dgelu_u = _gelu_and_grad(p_ref[:, :di])
        gg = p_ref[:, 2 * di:]
        sg = _sigmoid(gg)
        dyv = dy_ref[...]
        dus = dyv * (gg * sg)
        dsb = (dus * u).astype(BF16)
        ds32 = dus * u
        mask = _sgu_mask()
        lane = lax.broadcasted_iota(jnp.int32, (SG_BLOCK, 128), 1)
        dbs_acc = jnp.zeros((SG_BLOCK, 128), F32)
        for gi in range(SG_GROUPS):
            ws = jnp.where(mask, ws_ref[gi], 0.0).astype(BF16)
            bcol = bs_ref[:, gi:gi + 1]
            cols = slice(gi * gd, (gi + 1) * gd)
            dws_acc = jnp.zeros((SG_BLOCK, SG_BLOCK), F32)
            dbs_col = jnp.zeros((SG_BLOCK, 1), F32)
            for b in range(nblk):
                rows = slice(b * SG_BLOCK, (b + 1) * SG_BLOCK)
                s_scr[rows, cols] = _dot(ws, vb[rows, cols]) + bcol
                dvl_scr[rows, cols] = _dot_tn(ws, dsb[rows, cols])
                dws_acc += _dot_nt(dsb[rows, cols], vb[rows, cols])
                dbs_col += jnp.sum(ds32[rows, cols], axis=-1, keepdims=True)
            dws_ref[gi] += jnp.where(mask, dws_acc, 0.0)
            dbs_acc += jnp.where(lane == gi, dbs_col, 0.0)
        dbs_ref[...] += dbs_acc
        s = s_scr[...]
        dp_ref[:, :di] = (dus * s * dgelu_u).astype(BF16)
        dp_ref[:, 2 * di:] = (dyv * u * s * (sg * (1.0 + gg * (1.0 - sg)))).astype(BF16)
        dvl = dvl_scr[...]
        dlg_ref[...] += jnp.sum(dvl * vhat, axis=0, keepdims=True)
        dlb_ref[...] += jnp.sum(dvl, axis=0, keepdims=True)
        dvh = dvl * lg
        dv = rstd * (dvh - jnp.mean(dvh, axis=-1, keepdims=True)
                     - vhat * jnp.mean(dvh * vhat, axis=-1, keepdims=True))
        dp_ref[:, di:2 * di] = (dv * dgelu_v).astype(BF16)

    vec = pl.BlockSpec((1, di), lambda i: (0, 0))
    wsb = pl.BlockSpec((SG_GROUPS, SG_BLOCK, SG_BLOCK), lambda i: (0, 0, 0))
    bsb = pl.BlockSpec((SG_BLOCK, 128), lambda i: (0, 0))
    return _pc(body, name="a_mid_bwd",
               out_shape=[jax.ShapeDtypeStruct((m, n3), BF16), jax.ShapeDtypeStruct((1, di), F32),
                          jax.ShapeDtypeStruct((1, di), F32),
                          jax.ShapeDtypeStruct((SG_GROUPS, SG_BLOCK, SG_BLOCK), F32),
                          jax.ShapeDtypeStruct((SG_BLOCK, 128), F32)],
               grid=(m // r,),
               in_specs=[pl.BlockSpec((r, n3), lambda i: (i, 0)), pl.BlockSpec((r, di), lambda i: (i, 0)),
                         vec, vec, wsb, bsb],
               out_specs=[pl.BlockSpec((r, n3), lambda i: (i, 0)), vec, vec, wsb, bsb],
               scratch=[pltpu.VMEM((r, di), F32), pltpu.VMEM((r, di), F32)],
               sem=("arbitrary",), comm=comm)(proj, dybr, ln_g, ln_b, w_s, bs_t)


def _chunk_rows(n):
    if isinstance(n, int):
        return pl.ds(n * CHUNK, CHUNK)
    return pl.ds(pl.multiple_of(n * CHUNK, CHUNK), CHUNK)


def _hgrn_dims(t_seq, di):
    tr = _tile(t_seq, 128)
    hc = _tile(di, 2048)
    return tr, hc, hc // HEAD_DIM


def _hgrn_gates(f_ref, lb, a_scr, k_scr, tr):
    sig = _sigmoid(f_ref[...])
    fg = lb + (1.0 - lb) * sig
    k_scr[...] = 1.0 - fg
    logf = jnp.log(fg)
    g = min(CUM_ROWS, tr)
    tri = _tri_mask(g, reverse=False)
    for rg in range(tr // g):
        a_scr[rg * g:(rg + 1) * g, :] = _tri_apply(tri, logf[rg * g:(rg + 1) * g, :])
    return sig, fg


def _hgrn_fwd(proj, lbj, gn, nb, t_seq):
    _, m, di = proj.shape
    tr, hc, hpg = _hgrn_dims(t_seq, di)
    nt, nhg, ncl = t_seq // tr, di // hc, tr // CHUNK
    nheads = di // HEAD_DIM

    def body(p_ref, lb_ref, gn_ref, o_ref, ybr_ref, st_ref, st_scr, a_scr, k_scr):
        q_ref, f_ref, i_ref, g_ref = (p_ref.at[s] for s in range(4))
        t = pl.program_id(2)

        @pl.when(t == 0)
        def _():
            st_scr[...] = jnp.zeros_like(st_scr)

        _hgrn_gates(f_ref, lb_ref[0:1, :], a_scr, k_scr, tr)
        gnv = gn_ref[...]
        rr = lax.broadcasted_iota(jnp.int32, (CHUNK, CHUNK), 0)
        cc = lax.broadcasted_iota(jnp.int32, (CHUNK, CHUNK), 1)
        causal = cc <= rr

        def chunk(n, carry):
            rows = _chunk_rows(n)
            lanes = [slice(hd * HEAD_DIM, (hd + 1) * HEAD_DIM) for hd in range(hpg)]
            hs = []
            for hd, ls in enumerate(lanes):
                h = {}
                ah, kh = a_scr[rows, ls], k_scr[rows, ls]
                qp = q_ref[rows, ls]
                qh = qp * _sigmoid(qp)
                h["vb"] = i_ref[rows, ls].astype(BF16)
                aref, alast = ah[CHUNK // 2 - 1:CHUNK // 2, :], ah[CHUNK - 1:CHUNK, :]
                h["q_in"] = (qh * jnp.exp(ah - aref)).astype(BF16)
                h["k_in"] = (kh * jnp.exp(aref - ah)).astype(BF16)
                h["q_out"] = (qh * jnp.exp(ah)).astype(BF16)
                h["k_out"] = (kh * jnp.exp(alast - ah)).astype(BF16)
                h["dec"] = jnp.exp(alast)
                st = st_scr[hd]
                st_ref[n, hd] = st
                h["st"] = st
                hs.append(h)
            for h in hs:
                h["scores"] = _dot_nt(h["q_in"], h["k_in"])
                h["o_inter"] = _dot_nt(h["q_out"], h["st"].astype(BF16))
                h["st_mm"] = _dot_tn(h["vb"], h["k_out"])
            for h in hs:
                h["o"] = _dot(jnp.where(causal, h["scores"], 0.0).astype(BF16), h["vb"]) + h["o_inter"]
            for hd, (h, ls) in enumerate(zip(hs, lanes)):
                st_scr[hd] = h["st"] * h["dec"] + h["st_mm"]
                o = h["o"]
                o_ref[rows, ls] = o
                rstd = lax.rsqrt(jnp.mean(o * o, axis=-1, keepdims=True) + EPS)
                gg = g_ref[rows, ls]
                ybr_ref[rows, ls] = ((o * rstd * gnv) * (gg * _sigmoid(gg))).astype(BF16)
            return carry

        lax.fori_loop(0, ncl, chunk, 0)

    blk = pl.BlockSpec((tr, hc), lambda hg, b, t: (b * nt + t, hg))
    return _pc(body, name="hgrn_fwd",
               out_shape=[jax.ShapeDtypeStruct((m, di), F32), jax.ShapeDtypeStruct((m, di), BF16),
                          jax.ShapeDtypeStruct((m // CHUNK, nheads, HEAD_DIM, HEAD_DIM), F32)],
               grid=(nhg, nb, nt),
               in_specs=[pl.BlockSpec((4, tr, hc), lambda hg, b, t: (0, b * nt + t, hg)),
                         pl.BlockSpec((2, hc), lambda hg, b, t: (0, hg)),
                         pl.BlockSpec((1, HEAD_DIM), lambda hg, b, t: (0, 0))],
               out_specs=[blk, blk, pl.BlockSpec((ncl, hpg, HEAD_DIM, HEAD_DIM),
                                                 lambda hg, b, t: (b * nt + t, hg, 0, 0))],
               scratch=[pltpu.VMEM((hpg, HEAD_DIM, HEAD_DIM), F32), pltpu.VMEM((tr, hc), F32),
                        pltpu.VMEM((tr, hc), F32)],
               sem=("parallel", "arbitrary", "arbitrary"))(proj, lbj, gn)


def _hgrn_bwd(proj, o_all, dybr, states, lbj, gn, nb, t_seq, comm=None):
    _, m, di = proj.shape
    tr, hc, hpg = _hgrn_dims(t_seq, di)
    nt, nhg, ncl = t_seq // tr, di // hc, tr // CHUNK

    def body(p_ref, o_ref, dy_ref, st_ref, lb_ref, gn_ref,
             dp_ref, dlb_ref, dgn_ref, dst_scr, a_scr, k_scr, da_scr, dk_scr):
        q_ref, f_ref, i_ref, g_ref = (p_ref.at[s] for s in range(4))
        hg, b, t = pl.program_id(0), pl.program_id(1), pl.program_id(2)

        @pl.when(t == 0)
        def _():
            dst_scr[...] = jnp.zeros_like(dst_scr)

        @pl.when((b == 0) & (t == 0))
        def _():
            dlb_ref[...] = jnp.zeros_like(dlb_ref)

        @pl.when((hg == 0) & (b == 0) & (t == 0))
        def _():
            dgn_ref[...] = jnp.zeros_like(dgn_ref)

        lb = lb_ref[0:1, :]
        sig, fg = _hgrn_gates(f_ref, lb, a_scr, k_scr, tr)
        gnv = gn_ref[...]
        rr = lax.broadcasted_iota(jnp.int32, (CHUNK, CHUNK), 0)
        cc = lax.broadcasted_iota(jnp.int32, (CHUNK, CHUNK), 1)
        causal = cc <= rr
        rowi = lax.broadcasted_iota(jnp.int32, (CHUNK, HEAD_DIM), 0)

        def chunk(it, carry):
            n = ncl - 1 - it
            rows = _chunk_rows(n)
            for hd0 in range(0, hpg, PHASE_HEADS):
                heads(n, rows, range(hd0, min(hpg, hd0 + PHASE_HEADS)))
            return carry

        def heads(n, rows, ids):
            lanes = [slice(hd * HEAD_DIM, (hd + 1) * HEAD_DIM) for hd in ids]
            hs = []
            for hd, ls in zip(ids, lanes):
                h = {}
                ah, kh = a_scr[rows, ls], k_scr[rows, ls]
                qp = q_ref[rows, ls]
                sq = _sigmoid(qp)
                qh = qp * sq
                h["dsilu_q"] = sq * (1.0 + qp * (1.0 - sq))
                h["vb"] = i_ref[rows, ls].astype(BF16)
                aref, alast = ah[CHUNK // 2 - 1:CHUNK // 2, :], ah[CHUNK - 1:CHUNK, :]
                h["e1"], h["e2"] = jnp.exp(ah - aref), jnp.exp(aref - ah)
                h["e3"], h["e4"] = jnp.exp(ah), jnp.exp(alast - ah)
                h["dec"] = jnp.exp(alast)
                h["q_in"], h["k_in"], h["q_out"], h["k_out"] = qh * h["e1"], kh * h["e2"], qh * h["e3"], kh * h["e4"]
                for nm in ("q_in", "k_in", "q_out", "k_out"):
                    h[nm + "_b"] = h[nm].astype(BF16)
                o = o_ref[rows, ls]
                rstd = lax.rsqrt(jnp.mean(o * o, axis=-1, keepdims=True) + EPS)
                ohat = o * rstd
                gg = g_ref[rows, ls]
                sg = _sigmoid(gg)
                dyv = dy_ref[rows, ls]
                d_on = dyv * (gg * sg)
                dp_ref[3, rows, ls] = (dyv * (ohat * gnv) * (sg * (1.0 + gg * (1.0 - sg)))).astype(BF16)
                h["dgn"] = jnp.sum(d_on * ohat, axis=0, keepdims=True)
                dohat = d_on * gnv
                do = rstd * (dohat - ohat * jnp.mean(dohat * ohat, axis=-1, keepdims=True))
                h["do_b"] = do.astype(BF16)
                h["st_prev"] = st_ref[n, hd]
                h["dst"] = dst_scr[hd]
                hs.append(h)
            for h in hs:
                dst_b = h["dst"].astype(BF16)
                h["scores"] = _dot_nt(h["q_in_b"], h["k_in_b"])
                h["dscores"] = _dot_nt(h["do_b"], h["vb"])
                h["dv_inter"] = _dot_nt(h["k_out_b"], dst_b)
                h["dq_out"] = _dot(h["do_b"], h["st_prev"].astype(BF16))
                h["dk_out"] = _dot(h["vb"], dst_b)
                h["dst_mm"] = _dot_tn(h["do_b"], h["q_out_b"])
            for h in hs:
                scores = jnp.where(causal, h["scores"], 0.0).astype(BF16)
                dscores = jnp.where(causal, h["dscores"], 0.0).astype(BF16)
                h["dv"] = _dot_tn(scores, h["do_b"]) + h["dv_inter"]
                h["dq_in"] = _dot(dscores, h["k_in_b"])
                h["dk_in"] = _dot_tn(dscores, h["q_in_b"])
            dgn = hs[0]["dgn"]
            for h in hs[1:]:
                dgn = dgn + h["dgn"]
            dgn_ref[...] += dgn
            for hd, h, ls in zip(ids, hs, lanes):
                ddec = jnp.sum(h["dst"] * h["st_prev"], axis=0, keepdims=True)
                dst_scr[hd] = h["dst"] * h["dec"] + h["dst_mm"]
                dp_ref[2, rows, ls] = h["dv"].astype(BF16)
                dq = h["dq_in"] * h["e1"] + h["dq_out"] * h["e3"]
                dp_ref[0, rows, ls] = (dq * h["dsilu_q"]).astype(BF16)
                dk_scr[rows, ls] = h["dk_in"] * h["e2"] + h["dk_out"] * h["e4"]
                t_in = h["dq_in"] * h["q_in"] - h["dk_in"] * h["k_in"]
                t_out = h["dk_out"] * h["k_out"]
                da = t_in + h["dq_out"] * h["q_out"] - t_out
                da_ref_row = -jnp.sum(t_in, axis=0, keepdims=True)
                da_last_row = jnp.sum(t_out, axis=0, keepdims=True) + ddec * h["dec"]
                da = da + jnp.where(rowi == CHUNK // 2 - 1, da_ref_row, 0.0) \
                        + jnp.where(rowi == CHUNK - 1, da_last_row, 0.0)
                da_scr[rows, ls] = da

        if ncl <= 2:
            for it in range(ncl):
                chunk(it, 0)
        else:
            lax.fori_loop(0, ncl, chunk, 0)
        g = min(CUM_ROWS, tr)
        tri = _tri_mask(g, reverse=True)
        for rg in range(tr // g):
            rs = slice(rg * g, (rg + 1) * g)
            dlogf = _tri_apply(tri, da_scr[rs, :])
            df = dlogf / fg[rs, :] - dk_scr[rs, :]
            sgr = sig[rs, :]
            dp_ref[1, rs, :] = (df * (1.0 - lb) * (sgr * (1.0 - sgr))).astype(BF16)
            dlb_ref[...] += jnp.sum(df * (1.0 - sgr), axis=0, keepdims=True) * lb_ref[1:2, :]

    blk = pl.BlockSpec((tr, hc), lambda hg, b, t: (b * nt + (nt - 1 - t), hg))
    return _pc(body, name="hgrn_bwd",
               out_shape=[jax.ShapeDtypeStruct((4, m, di), BF16), jax.ShapeDtypeStruct((1, di), F32),
                          jax.ShapeDtypeStruct((1, HEAD_DIM), F32)],
               grid=(nhg, nb, nt),
               in_specs=[pl.BlockSpec((4, tr, hc), lambda hg, b, t: (0, b * nt + (nt - 1 - t), hg)), blk, blk,
                         pl.BlockSpec((ncl, hpg, HEAD_DIM, HEAD_DIM),
                                      lambda hg, b, t: (b * nt + (nt - 1 - t), hg, 0, 0)),
                         pl.BlockSpec((2, hc), lambda hg, b, t: (0, hg)),
                         pl.BlockSpec((1, HEAD_DIM), lambda hg, b, t: (0, 0))],
               out_specs=[pl.BlockSpec((4, tr, hc), lambda hg, b, t: (0, b * nt + (nt - 1 - t), hg)),
                          pl.BlockSpec((1, hc), lambda hg, b, t: (0, hg)),
                          pl.BlockSpec((1, HEAD_DIM), lambda hg, b, t: (0, 0))],
               scratch=[pltpu.VMEM((hpg, HEAD_DIM, HEAD_DIM), F32)] + [pltpu.VMEM((tr, hc), F32)] * 4,
               sem=("arbitrary", "arbitrary", "arbitrary"), comm=comm)(
                   proj, o_all, dybr, states, lbj, gn)


def _adamw(parts, w, m, v, name):
    r, c = w.shape
    tr = _tile(r, 256)
    npart = len(parts)
    c1 = 1.0 - ADAM_B1 ** ADAM_STEP
    c2 = 1.0 - ADAM_B2 ** ADAM_STEP

    def body(*refs):
        p_refs = refs[:npart]
        _adamw_math(p_refs, *refs[npart:], c1, c2)

    blk = pl.BlockSpec((tr, c), lambda i: (i, 0))
    return _pc(body, name=name, out_shape=[jax.ShapeDtypeStruct((r, c), F32)] * 4, grid=(r // tr,),
               in_specs=[blk] * (npart + 3), out_specs=[blk] * 4, sem=("parallel",))(*parts, w, m, v)


def _adamw_math(p_refs, w_ref, m_ref, v_ref, g_ref, d_ref, nm_ref, nv_ref, c1, c2):
    g = p_refs[0][...].astype(F32)
    for p in p_refs[1:]:
        g = g + p[...].astype(F32)
    nm = ADAM_B1 * m_ref[...] + (1.0 - ADAM_B1) * g
    nv = ADAM_B2 * v_ref[...] + (1.0 - ADAM_B2) * (g * g)
    g_ref[...] = g
    nm_ref[...] = nm
    nv_ref[...] = nv
    d_ref[...] = -ADAM_LR * ((nm / c1) / (jnp.sqrt(nv / c2) + ADAM_EPS) + ADAM_WD * w_ref[...])


def _adamw_blocks(parts, idx, w, m, v, name):
    r, c = w.shape
    tr = _tile(r, 256)
    npart = len(parts)
    c1 = 1.0 - ADAM_B1 ** ADAM_STEP
    c2 = 1.0 - ADAM_B2 ** ADAM_STEP

    def body(idx_ref, *refs):
        _adamw_math(refs[:npart], *refs[npart:], c1, c2)

    def sel(p):
        return pl.BlockSpec((None, tr, c), lambda i, s: (s[p], i, 0))

    blk = pl.BlockSpec((tr, c), lambda i, s: (i, 0))
    gs = pltpu.PrefetchScalarGridSpec(num_scalar_prefetch=1, grid=(r // tr,),
                                      in_specs=[sel(p) for p in range(npart)] + [blk] * 3, out_specs=[blk] * 4)
    return _pc(body, name=name, out_shape=[jax.ShapeDtypeStruct((r, c), F32)] * 4, grid_spec=gs,
               sem=("parallel",))(idx, *parts, w, m, v)


_EARLY = ["a_ln_gain", "a_ln_bias", "a_w_s", "a_b_s", "b_lower_bounds", "b_gn_gain"]


def _pack(arrs):
    flat = jnp.concatenate([a.reshape(-1) for a in arrs])
    rows = -(-flat.shape[0] // 1024) * 8
    return jnp.pad(flat, (0, rows * 128 - flat.shape[0])).reshape(rows, 128)


def _unpack(buf, like):
    flat = buf.reshape(-1)
    out, off = [], 0
    for a in like:
        out.append(flat[off:off + a.size].reshape(a.shape))
        off += a.size
    return out


def kernel(x, c, norm_gain, w_ada, b_ada, a_w_in, a_ln_gain, a_ln_bias, a_w_s, a_b_s, a_w_out, b_w_in, b_lower_bounds, b_gn_gain, b_w_out, final_gain, loss_target, m_norm_gain, m_w_ada, m_b_ada, m_a_w_in, m_a_ln_gain, m_a_ln_bias, m_a_w_s, m_a_b_s, m_a_w_out, m_b_w_in, m_b_lower_bounds, m_b_gn_gain, m_b_w_out, m_final_gain, v_norm_gain, v_w_ada, v_b_ada, v_a_w_in, v_a_ln_gain, v_a_ln_bias, v_a_w_s, v_a_b_s, v_a_w_out, v_b_w_in, v_b_lower_bounds, v_b_gn_gain, v_b_w_out, v_final_gain):
    w = dict(norm_gain=norm_gain, w_ada=w_ada, b_ada=b_ada, a_w_in=a_w_in, a_ln_gain=a_ln_gain,
             a_ln_bias=a_ln_bias, a_w_s=a_w_s, a_b_s=a_b_s, a_w_out=a_w_out, b_w_in=b_w_in,
             b_lower_bounds=b_lower_bounds, b_gn_gain=b_gn_gain, b_w_out=b_w_out, final_gain=final_gain)
    mo = dict(norm_gain=m_norm_gain, w_ada=m_w_ada, b_ada=m_b_ada, a_w_in=m_a_w_in, a_ln_gain=m_a_ln_gain,
              a_ln_bias=m_a_ln_bias, a_w_s=m_a_w_s, a_b_s=m_a_b_s, a_w_out=m_a_w_out, b_w_in=m_b_w_in,
              b_lower_bounds=m_b_lower_bounds, b_gn_gain=m_b_gn_gain, b_w_out=m_b_w_out, final_gain=m_final_gain)
    vo = dict(norm_gain=v_norm_gain, w_ada=v_w_ada, b_ada=v_b_ada, a_w_in=v_a_w_in, a_ln_gain=v_a_ln_gain,
              a_ln_bias=v_a_ln_bias, a_w_s=v_a_w_s, a_b_s=v_a_b_s, a_w_out=v_a_w_out, b_w_in=v_b_w_in,
              b_lower_bounds=v_b_lower_bounds, b_gn_gain=v_b_gn_gain, b_w_out=v_b_w_out, final_gain=v_final_gain)

    nb, t_seq, d = x.shape
    m = nb * t_seq
    ncol_ada = w_ada.shape[2]
    xi, yi, ci = lax.axis_index("x"), lax.axis_index("y"), lax.axis_index("c")
    me = 4 * xi + 2 * yi + ci

    c_g, wa_in_g = _all_gather([c, a_w_in[0].astype(BF16)], "gather_c_wa")

    c_all = c_g.reshape(NDEV * nb, d)
    b_cols = lax.dynamic_slice(b_ada, (0, me * ncol_ada), (2, ncol_ada)).reshape(2, 1, ncol_ada)
    mod_part, lbj = _ada_fwd(c_all, w_ada, b_cols, b_lower_bounds)
    mod_all = _all_gather([mod_part], "gather_mod")[0]
    mod_mine = lax.dynamic_slice_in_dim(mod_all, me * nb, nb, axis=2)
    mod_mine = mod_mine.transpose(1, 2, 0, 3).reshape(2, nb, 3, d)
    mod0, mod1 = mod_mine[0], mod_mine[1]

    di = a_w_out.shape[1] * NDEV

    xf = x.reshape(m, d)
    tgt = loss_target.reshape(m, d)
    ng0, ng1 = norm_gain[0:1], norm_gain[1:2]
    ncb = b_w_in.shape[2]
    wb_lo, wb_hi = b_w_in[0][:, :ncb // 2].astype(BF16), b_w_in[0][:, ncb // 2:].astype(BF16)
    h0, h0_t = _prenorm(xf, ng0, mod0, t_seq, "prenorm_a")
    proj_a, half = _mm_in(h0, [wa_in_g], 1, "in_proj_a", comm=_gather_first([a_w_out[0].astype(BF16), wb_lo]))
    bs_t = jnp.pad(a_b_s[0].T, ((0, 0), (0, 128 - SG_GROUPS)))
    ybr_a, (wa_out_g, wb_lo_g, wb_hi_half) = _a_mid_fwd(
        proj_a, a_ln_gain, a_ln_bias, a_w_s[0], bs_t, t_seq, comm=_join(_gather_second(half), _gather_first([wb_hi])))
    wa_out = wa_out_g.reshape(di, d)
    (yout_a, x1), (wb_hi_g, wb_out_half) = _out_proj(
        ybr_a, wa_out, xf, mod0, t_seq, "out_proj_a",
        comm=_join(_gather_second([wb_hi_half]), _gather_first([b_w_out[0].astype(BF16)])))
    wb_in_g = [wb_lo_g, wb_hi_g]
    h1, h1_t = _prenorm(x1, ng1, mod1, t_seq, "prenorm_b")
    proj_b, (wb_out_g,) = _mm_in(h1, wb_in_g, 4, "in_proj_b", comm=_gather_second([wb_out_half]))
    wb_out = wb_out_g.reshape(di, d)
    o_b, ybr_b, states = _hgrn_fwd(proj_b, lbj, b_gn_gain, nb, t_seq)
    yout_b, dx2, loss_part, d_final_gain = _out_proj_loss(ybr_b, wb_out, x1, mod1, final_gain.reshape(1, d), tgt, t_seq)

    rows_out = a_w_out.shape[1]
    dy_b, dgate1, dybr_b = _gate_dybr(dx2, yout_b, mod1, wb_out, t_seq, "dybr_b")
    rs_wb_out = _ReduceScatter(_mm_dw_out(ybr_b, dy_b, "dw_out_b").reshape(NDEV, rows_out, d), "b_w_out")
    (dproj_b, d_lb, d_gn), got = _hgrn_bwd(proj_b, o_b, dybr_b, states, lbj, b_gn_gain, nb, t_seq,
                                           comm=rs_wb_out.swap_core())
    rs_wb_out.after_core(got[0])
    dh1, got = _mm_din(dproj_b, wb_in_g, 4, "dh_b", comm=rs_wb_out.swap_chips())
    rs_wb_out.after_chips(got[0])
    dx1, dss1, dgain1 = _prenorm_bwd(dh1, x1, ng1, mod1, dx2, t_seq, "prenorm_bwd_b")
    rs_wb_in = _ReduceScatter(_mm_dw_in(h1_t, dproj_b, ncb, 4, "dw_in_b"), "b_w_in")

    dy_a, dgate0, dybr_a = _gate_dybr(dx1, yout_a, mod0, wa_out, t_seq, "dybr_a")
    g_wa_out, got = _mm_dw_out(ybr_a, dy_a, "dw_out_a", comm=rs_wb_in.swap_core())
    rs_wb_in.after_core(got[0])
    rs_wa_out = _ReduceScatter(g_wa_out.reshape(NDEV, rows_out, d), "a_w_out")
    (dproj_a, d_lng, d_lnb, d_ws, d_bs_t), got = _a_mid_bwd(
        proj_a, dybr_a, a_ln_gain, a_ln_bias, a_w_s[0], bs_t, t_seq,
        comm=_join(rs_wb_in.swap_chips(), rs_wa_out.swap_core()))
    rs_wb_in.after_chips(got[0])
    rs_wa_out.after_core(got[1])
    part = dict(a_ln_gain=d_lng, a_ln_bias=d_lnb, a_w_s=d_ws[None], a_b_s=d_bs_t[:, :SG_GROUPS].T[None],
                b_lower_bounds=jnp.concatenate([-d_lb, d_lb], axis=0), b_gn_gain=d_gn)
    early_pack = _pack([part[k].reshape(w[k].shape) for k in _EARLY])
    g_wa_in, got = _mm_dw_in(h0_t, dproj_a, wa_in_g.shape[2], 1, "dw_in_a",
                             comm=_join(rs_wa_out.swap_chips(), _gather_first([early_pack])))
    rs_wa_out.after_chips(got[0])
    rs_wa_in = _ReduceScatter(g_wa_in, "a_w_in")
    n_tiles = m // _din_tile(m)
    assert n_tiles >= 2
    first_tiles = max(1, (3 * n_tiles) // 8)
    dh0, got2 = _mm_din(dproj_a, [wa_in_g], 1, "dh_a_first", tiles=(0, first_tiles),
                        comm=_join(rs_wa_in.swap_core(), _gather_second([got[1]])))
    rs_wa_in.after_core(got2[0])
    early_all = got2[1]
    dh0, got = _mm_din(dproj_a, [wa_in_g], 1, "dh_a_rest", comm=rs_wa_in.swap_chips(),
                       tiles=(first_tiles, n_tiles - first_tiles), prev=dh0)
    rs_wa_in.after_chips(got[0])
    dx0, dss0, dgain0 = _prenorm_bwd(dh0, xf, ng0, mod0, dx1, t_seq, "prenorm_bwd_a")
    grad_x = dx0.reshape(nb, t_seq, d)

    dmod = jnp.stack([jnp.concatenate([dss0, dgate0], axis=1), jnp.concatenate([dss1, dgate1], axis=1)])
    late_like = [norm_gain, final_gain, loss_part.reshape(1)]
    late_pack = _pack([jnp.concatenate([dgain0, dgain1], axis=0), d_final_gain[0], loss_part.reshape(1)])
    dmod_all, late_all = _all_gather([dmod.reshape(2, nb, 3 * d), late_pack], "gather_tail")
    dmod_all = dmod_all.transpose(1, 0, 2, 3).reshape(2, NDEV * nb, 3 * d)
    dmod_cols = lax.dynamic_slice_in_dim(dmod_all, me * ncol_ada, ncol_ada, axis=2)
    g_w_ada, g_b_ada = _ada_bwd(c_all, dmod_cols, dmod_all)

    res = {}
    early_like = [w[k] for k in _EARLY]
    dev_order = jnp.arange(NDEV, dtype=jnp.int32)
    sm = _adamw_blocks([early_all] * NDEV, dev_order, _pack(early_like), _pack([mo[k] for k in _EARLY]),
                       _pack([vo[k] for k in _EARLY]), "adamw_small_early")
    sm = [dict(zip(_EARLY, _unpack(buf, early_like))) for buf in sm]
    for k in _EARLY:
        res[k] = tuple(s[k] for s in sm)
    zero = jnp.zeros((1,), F32)
    sm = _adamw_blocks([late_all] * NDEV, dev_order, _pack([norm_gain, final_gain, zero]),
                       _pack([mo["norm_gain"], mo["final_gain"], zero]),
                       _pack([vo["norm_gain"], vo["final_gain"], zero]), "adamw_small_late")
    sm = [_unpack(buf, late_like) for buf in sm]
    res["norm_gain"] = tuple(s[0] for s in sm)
    res["final_gain"] = tuple(s[1] for s in sm)
    loss = sm[0][2][0]
    rb = _adamw([g_b_ada], b_ada, mo["b_ada"], vo["b_ada"], "adamw_b_ada")
    res["b_ada"] = tuple(rb)
    sh = w_ada.shape
    ra = _adamw([g_w_ada.reshape(sh[0] * sh[1], sh[2])], w_ada.reshape(sh[0] * sh[1], sh[2]),
                mo["w_ada"].reshape(sh[0] * sh[1], sh[2]), vo["w_ada"].reshape(sh[0] * sh[1], sh[2]), "adamw_w_ada")
    res["w_ada"] = tuple(z.reshape(sh) for z in ra)

    for k, rs in (("b_w_out", rs_wb_out), ("b_w_in", rs_wb_in), ("a_w_out", rs_wa_out), ("a_w_in", rs_wa_in)):
        res[k] = tuple(z[None] for z in _adamw_blocks(rs.parts, rs.idx, w[k][0], mo[k][0], vo[k][0], "adamw_" + k))

    order = ["norm_gain", "w_ada", "b_ada", "a_w_in", "a_ln_gain", "a_ln_bias", "a_w_s", "a_b_s", "a_w_out",
             "b_w_in", "b_lower_bounds", "b_gn_gain", "b_w_out", "final_gain"]
    return (loss, grad_x, *[res[k][0] for k in order], *[res[k][1] for k in order],
            *[res[k][2] for k in order], *[res[k][3] for k in order])
```

```python
import functools
import math

import jax
import jax.numpy as jnp
from jax import lax
from jax.experimental import pallas as pl
from jax.experimental.pallas import tpu as pltpu

F32 = jnp.float32
BF16 = jnp.bfloat16
MESH = pl.DeviceIdType.MESH
NDEV = 8
EPS = 1e-6
CHUNK = 64
SG_BLOCK = 128
SG_GROUPS = 8
HEAD_DIM = 128
CUM_ROWS = 256
PHASE_HEADS = 8
ADAM_LR, ADAM_B1, ADAM_B2, ADAM_EPS, ADAM_WD, ADAM_STEP = 0.001, 0.9, 0.999, 1e-08, 0.01, 10
VMEM_LIMIT = 56 * 1024 * 1024
ANY = pl.BlockSpec(memory_space=pl.ANY)


class _Hosted:
    def __init__(self, arrays, out_shapes, nsem, start, finish, aliases=None):
        self.arrays, self.out_shapes, self.nsem = list(arrays), list(out_shapes), nsem
        self.start, self.finish = start, finish
        self.aliases = dict(aliases or {})


def _join(*comms):
    arrays, outs, aliases, offs, nsem = [], [], {}, [], 0
    for cm in comms:
        offs.append((len(arrays), len(outs), nsem))
        for i, o in cm.aliases.items():
            aliases[len(arrays) + i] = len(outs) + o
        arrays += cm.arrays
        outs += cm.out_shapes
        nsem += cm.nsem

    def run(which):
        def f(ins, outs_, ss, rs, base):
            for cm, (ia, io, isem) in zip(comms, offs):
                getattr(cm, which)(ins[ia:ia + len(cm.arrays)], outs_[io:io + len(cm.out_shapes)], ss, rs, base + isem)
        return f

    return _Hosted(arrays, outs, nsem, run("start"), run("finish"), aliases)


def _pc(body, *, name, out_shape, grid=None, in_specs=None, out_specs=None, scratch=(), sem=None,
        grid_spec=None, comm=None, aliases=None):
    cp = dict(vmem_limit_bytes=VMEM_LIMIT)
    aliases = dict(aliases or {})
    if comm is None:
        if sem is not None:
            cp["dimension_semantics"] = sem
        kw = {"input_output_aliases": aliases}
        if grid_spec is not None:
            kw["grid_spec"] = grid_spec
        else:
            if grid is not None:
                kw["grid"] = grid
            if in_specs is not None:
                kw["in_specs"] = in_specs
            if out_specs is not None:
                kw["out_specs"] = out_specs
            kw["scratch_shapes"] = list(scratch)
        return pl.pallas_call(functools.partial(body), name=name, out_shape=out_shape,
                              compiler_params=pltpu.CompilerParams(**cp), **kw)

    single = not isinstance(out_shape, (list, tuple))
    outs_list = [out_shape] if single else list(out_shape)
    ospecs = [out_specs] if single else list(out_specs)
    n_in, n_out, n_ci, n_co, n_scr = len(in_specs), len(outs_list), len(comm.arrays), len(comm.out_shapes), len(scratch)
    cp["dimension_semantics"] = ("arbitrary",) * len(grid)

    def hosted(*refs):
        cin, hin = refs[:n_in], refs[n_in:n_in + n_ci]
        cout = refs[n_in + n_ci:n_in + n_ci + n_out]
        hout = refs[n_in + n_ci + n_out:n_in + n_ci + n_out + n_co]
        scr = refs[n_in + n_ci + n_out + n_co:n_in + n_ci + n_out + n_co + n_scr]
        ssem, rsem = refs[-2], refs[-1]
        first = functools.reduce(lambda p, q: p & q, [pl.program_id(a) == 0 for a in range(len(grid))])
        last = functools.reduce(lambda p, q: p & q, [pl.program_id(a) == grid[a] - 1 for a in range(len(grid))])

        @pl.when(first)
        def _():
            comm.start(hin, hout, ssem, rsem, 0)

        body(*cin, *cout, *scr)

        @pl.when(last)
        def _():
            comm.finish(hin, hout, ssem, rsem, 0)

    call = pl.pallas_call(
        hosted, name=name, grid=grid, in_specs=list(in_specs) + [ANY] * n_ci, out_specs=ospecs + [ANY] * n_co,
        out_shape=outs_list + comm.out_shapes,
        scratch_shapes=list(scratch) + [pltpu.SemaphoreType.DMA((comm.nsem,)), pltpu.SemaphoreType.DMA((comm.nsem,))],
        input_output_aliases={**aliases, **{n_in + i: n_out + o for i, o in comm.aliases.items()}},
        compiler_params=pltpu.CompilerParams(**cp))

    def run(*args):
        res = call(*args, *comm.arrays)
        comp = res[:n_out]
        return (comp[0] if single else comp), list(res[n_out:])

    return run


def _tile(n, pref):
    return pref if n % pref == 0 else n


def _sigmoid(x):
    return 1.0 / (1.0 + jnp.exp(-x))


def _gelu(x):
    c = math.sqrt(2.0 / math.pi)
    return 0.5 * x * (1.0 + jnp.tanh(c * (x + 0.044715 * (x * x * x))))


def _gelu_and_grad(x):
    c = math.sqrt(2.0 / math.pi)
    x2 = x * x
    t = jnp.tanh(c * (x + 0.044715 * (x2 * x)))
    half = 0.5 * (1.0 + t)
    return x * half, half + (0.5 * x) * (1.0 - t * t) * (c + (3.0 * 0.044715 * c) * x2)


def _dot(a, b):
    return jnp.dot(a, b, preferred_element_type=F32)


def _dot_nt(a, b):
    return lax.dot_general(a, b, (((1,), (1,)), ((), ())), preferred_element_type=F32)


def _dot_tn(a, b):
    return lax.dot_general(a, b, (((0,), (0,)), ((), ())), preferred_element_type=F32)


def _tri_mask(n, reverse):
    r = lax.broadcasted_iota(jnp.int32, (n, n), 0)
    c = lax.broadcasted_iota(jnp.int32, (n, n), 1)
    same = (r // CHUNK) == (c // CHUNK)
    tri = (c >= r) if reverse else (c <= r)
    return jnp.where(same & tri, 1.0, 0.0).astype(BF16)


def _tri_apply(tri, x):
    hi = x.astype(BF16)
    r1 = x - hi.astype(F32)
    mid = r1.astype(BF16)
    lo = (r1 - mid.astype(F32)).astype(BF16)
    return _dot(tri, hi) + (_dot(tri, mid) + _dot(tri, lo))


def _all_gather(arrs, name):
    n = len(arrs)

    def body(*refs):
        ins, outs = refs[:n], refs[n:2 * n]
        send_sems, recv_sems, local_sems = refs[2 * n:]
        x, y, c = lax.axis_index("x"), lax.axis_index("y"), lax.axis_index("c")
        me, sibling = (x, y, c), (x, y, 1 - c)
        near = (x + c - 2 * x * c, y + (1 - c) - 2 * y * (1 - c))
        far = (x + (1 - c) - 2 * x * (1 - c), y + c - 2 * y * c)
        diag = (1 - x, 1 - y)

        def blk(a, p):
            return outs[a].at[4 * p[0] + 2 * p[1] + p[2]]

        def copy(a, k, block, to, src=None):
            return pltpu.make_async_remote_copy(
                src_ref=blk(a, block) if src is None else src, dst_ref=blk(a, block),
                send_sem=send_sems.at[7 * a + k], recv_sem=recv_sems.at[7 * a + k],
                device_id=to, device_id_type=MESH)

        mine = [pltpu.make_async_copy(ins[a], blk(a, me), local_sems.at[a]) for a in range(n)]
        for m in mine:
            m.start()
        sends = []
        for a in range(n):
            sends += [copy(a, 0, me, sibling, src=ins[a]), copy(a, 1, me, (*near, c), src=ins[a]),
                      copy(a, 2, me, (*far, c), src=ins[a])]
        for cp in sends:
            cp.start()
        for a in range(n):
            copy(a, 1, (*near, c), me).wait_recv()
            sends.append(copy(a, 3, (*near, c), (*far, c)))
            sends[-1].start()
        for a in range(n):
            sends.append(copy(a, 4, (*near, c), sibling))
            sends[-1].start()
            copy(a, 2, (*far, c), me).wait_recv()
            sends.append(copy(a, 5, (*far, c), sibling))
            sends[-1].start()
        for a in range(n):
            copy(a, 3, (*diag, c), me).wait_recv()
            sends.append(copy(a, 6, (*diag, c), sibling))
            sends[-1].start()
        for a in range(n):
            copy(a, 0, sibling, me).wait_recv()
            copy(a, 4, (*far, 1 - c), me).wait_recv()
            copy(a, 5, (*near, 1 - c), me).wait_recv()
            copy(a, 6, (*diag, 1 - c), me).wait_recv()
        for cp in sends:
            cp.wait_send()
        for m in mine:
            m.wait()

    out_shape = [jax.ShapeDtypeStruct((NDEV,) + a.shape, a.dtype) for a in arrs]
    return _pc(body, name=name, out_shape=out_shape, in_specs=[ANY] * n, out_specs=[ANY] * n,
               scratch=[pltpu.SemaphoreType.DMA((7 * n,)), pltpu.SemaphoreType.DMA((7 * n,)),
                        pltpu.SemaphoreType.DMA((n,))])(*arrs)


def _gather_first(arrs):
    n = len(arrs)

    def parts(ins, outs, ss, rs, base):
        x, y, c = lax.axis_index("x"), lax.axis_index("y"), lax.axis_index("c")
        me, sibling = (x, y, c), (x, y, 1 - c)
        chips = [(1 - x, y), (x, 1 - y), (1 - x, 1 - y)]

        def blk(a, p):
            return outs[a].at[4 * p[0] + 2 * p[1] + p[2]]

        def copy(a, k, block, to):
            return pltpu.make_async_remote_copy(
                src_ref=ins[a], dst_ref=blk(a, block), send_sem=ss.at[base + 4 * a + k],
                recv_sem=rs.at[base + 4 * a + k], device_id=to, device_id_type=MESH)

        local = [pltpu.make_async_copy(ins[a], blk(a, me), ss.at[base + 4 * n + a]) for a in range(n)]
        sends, recvs = [], []
        for a in range(n):
            sends.append(copy(a, 0, me, sibling))
            recvs.append(copy(a, 0, sibling, me))
            for j, chip in enumerate(chips):
                sends.append(copy(a, 1 + j, me, (*chip, c)))
                recvs.append(copy(a, 1 + j, (*chip, c), me))
        return local, sends, recvs

    def start(ins, outs, ss, rs, base):
        local, sends, _ = parts(ins, outs, ss, rs, base)
        for cp in local + sends:
            cp.start()

    def finish(ins, outs, ss, rs, base):
        local, sends, recvs = parts(ins, outs, ss, rs, base)
        for cp in recvs:
            cp.wait_recv()
        for cp in sends:
            cp.wait_send()
        for cp in local:
            cp.wait()

    return _Hosted(arrs, [jax.ShapeDtypeStruct((NDEV,) + a.shape, a.dtype) for a in arrs], 5 * n, start, finish)


def _gather_second(bufs):
    n = len(bufs)

    def parts(ins, outs, ss, rs, base):
        x, y, c = lax.axis_index("x"), lax.axis_index("y"), lax.axis_index("c")
        sibling = (x, y, 1 - c)
        chips = [(1 - x, y), (x, 1 - y), (1 - x, 1 - y)]
        sends, recvs = [], []
        for a in range(n):
            for j, chip in enumerate(chips):
                mine = 4 * chip[0] + 2 * chip[1] + c
                theirs = 4 * chip[0] + 2 * chip[1] + (1 - c)
                sends.append(pltpu.make_async_remote_copy(
                    src_ref=ins[a].at[mine], dst_ref=outs[a].at[mine], send_sem=ss.at[base + 3 * a + j],
                    recv_sem=rs.at[base + 3 * a + j], device_id=sibling, device_id_type=MESH))
                recvs.append(pltpu.make_async_remote_copy(
                    src_ref=ins[a].at[theirs], dst_ref=outs[a].at[theirs], send_sem=ss.at[base + 3 * a + j],
                    recv_sem=rs.at[base + 3 * a + j], device_id=sibling, device_id_type=MESH))
        return sends, recvs

    def start(ins, outs, ss, rs, base):
        for cp in parts(ins, outs, ss, rs, base)[0]:
            cp.start()

    def finish(ins, outs, ss, rs, base):
        sends, recvs = parts(ins, outs, ss, rs, base)
        for cp in recvs:
            cp.wait_recv()
        for cp in sends:
            cp.wait_send()

    return _Hosted(bufs, [jax.ShapeDtypeStruct(b.shape, b.dtype) for b in bufs], 3 * n, start, finish,
                   aliases={a: a for a in range(n)})


def _swap(src, nblk, ids_fn, partner_fn):
    def copies(ins, outs, ss, rs, base):
        x, y, c = lax.axis_index("x"), lax.axis_index("y"), lax.axis_index("c")
        ids = ids_fn(x, y, c)
        partner = partner_fn(x, y, c)
        return [pltpu.make_async_remote_copy(
            src_ref=ins[0].at[ids[k]], dst_ref=outs[0].at[k], send_sem=ss.at[base + k], recv_sem=rs.at[base + k],
            device_id=partner, device_id_type=MESH) for k in range(nblk)]

    def start(ins, outs, ss, rs, base):
        for cp in copies(ins, outs, ss, rs, base):
            cp.start()

    def finish(ins, outs, ss, rs, base):
        for cp in copies(ins, outs, ss, rs, base):
            cp.wait()

    return _Hosted([src], [jax.ShapeDtypeStruct((nblk,) + src.shape[1:], src.dtype)], nblk, start, finish)


def _swap_chips(send):
    def copies(ins, outs, ss, rs, base):
        x, y, c = lax.axis_index("x"), lax.axis_index("y"), lax.axis_index("c")
        chips = [(1 - x, y), (x, 1 - y), (1 - x, 1 - y)]
        return [pltpu.make_async_remote_copy(
            src_ref=ins[0].at[j], dst_ref=outs[0].at[j], send_sem=ss.at[base + j], recv_sem=rs.at[base + j],
            device_id=(*chip, c), device_id_type=MESH) for j, chip in enumerate(chips)]

    def start(ins, outs, ss, rs, base):
        for cp in copies(ins, outs, ss, rs, base):
            cp.start()

    def finish(ins, outs, ss, rs, base):
        for cp in copies(ins, outs, ss, rs, base):
            cp.wait()

    return _Hosted([send], [jax.ShapeDtypeStruct(send.shape, send.dtype)], 3, start, finish)


def _add_send(a, b, idx, ns, name):
    _, r, c = a.shape
    tr = _tile(r, 256)

    def body(idx_ref, a_ref, b_ref, send_ref):
        send_ref[...] = (a_ref[...] + b_ref[...]).astype(BF16)

    def sel(off):
        return pl.BlockSpec((None, tr, c), lambda k, i, s: (s[off + k], i, 0))

    gs = pltpu.PrefetchScalarGridSpec(num_scalar_prefetch=1, grid=(ns, r // tr), in_specs=[sel(0), sel(ns)],
                                      out_specs=pl.BlockSpec((None, tr, c), lambda k, i, s: (k, i, 0)))
    return _pc(body, name=name, grid_spec=gs, sem=("arbitrary", "arbitrary"),
               out_shape=jax.ShapeDtypeStruct((ns, r, c), BF16))(idx, a, b)


class _ReduceScatter:
    def __init__(self, g, tag):
        self.g, self.tag = g, tag

    def swap_core(self):
        return _swap(self.g, 4, lambda x, y, c: [1 - c, 3 - c, 5 - c, 7 - c], lambda x, y, c: (x, y, 1 - c))

    def after_core(self, recv):
        x, y, c = lax.axis_index("x"), lax.axis_index("y"), lax.axis_index("c")
        chips = [(1 - x, y), (x, 1 - y), (1 - x, 1 - y)]
        idx = jnp.stack([4 * p + 2 * q + c for p, q in chips] + [2 * p + q for p, q in chips]).astype(jnp.int32)
        self.send = _add_send(self.g, recv, idx, 3, "rs_add_" + self.tag)
        self.recv_core = recv
        zero = jnp.zeros((), jnp.int32)
        self.idx = jnp.stack([4 * x + 2 * y + c, 2 * x + y, zero, zero + 1, zero + 2]).astype(jnp.int32)

    def swap_chips(self):
        return _swap_chips(self.send)

    def after_chips(self, recv):
        self.parts = [self.g, self.recv_core, recv, recv, recv]


def _ada_fwd(c_all, w_ada, b_cols, b_lb):
    nl, d, ncol = w_ada.shape
    nseq = c_all.shape[0]
    di = b_lb.shape[1]

    def body(c_ref, w_ref, b_ref, lb_ref, mod_ref, lbj_ref):
        cv = c_ref[...]
        cact = (cv * _sigmoid(cv)).astype(BF16)
        for l in range(nl):
            mod_ref[l] = _dot(cact, w_ref[l].astype(BF16)) + b_ref[l]
        b0, b1 = lb_ref[0:1, :], lb_ref[1:2, :]
        mx = jnp.maximum(b0, b1)
        e0, e1 = jnp.exp(b0 - mx), jnp.exp(b1 - mx)
        s = e0 + e1
        p0, p1 = e0 / s, e1 / s
        lbj_ref[0:1, :] = (p0 + p1) - p0
        lbj_ref[1:2, :] = p0 * p1

    return _pc(body, name="ada_fwd",
               out_shape=[jax.ShapeDtypeStruct((nl, nseq, ncol), F32), jax.ShapeDtypeStruct((2, di), F32)]
               )(c_all, w_ada, b_cols, b_lb)


def _ada_bwd(c_all, dmod_cols, dmod_full):
    nl, nseq, ncol = dmod_cols.shape
    d = c_all.shape[1]
    d3 = dmod_full.shape[2]

    def body(c_ref, dc_ref, df_ref, gw_ref, gb_ref):
        cv = c_ref[...]
        cact = (cv * _sigmoid(cv)).astype(BF16)
        for l in range(nl):
            gw_ref[l] = _dot_tn(cact, dc_ref[l].astype(BF16))
            gb_ref[l:l + 1, :] = jnp.sum(df_ref[l], axis=0, keepdims=True)

    return _pc(body, name="ada_bwd",
               out_shape=[jax.ShapeDtypeStruct((nl, d, ncol), F32), jax.ShapeDtypeStruct((nl, d3), F32)]
               )(c_all, dmod_cols, dmod_full)


def _prenorm(x, gain, mod, t_seq, name):
    m, d = x.shape
    tm = _tile(t_seq, 1024)
    per = t_seq // tm

    def body(x_ref, g_ref, mod_ref, h_ref, ht_ref):
        xv = x_ref[...]
        rstd = lax.rsqrt(jnp.mean(xv * xv, axis=-1, keepdims=True) + EPS)
        r = xv * rstd * g_ref[...]
        h = r * (1.0 + mod_ref[0, 1:2, :]) + mod_ref[0, 0:1, :]
        h_ref[...] = h.astype(BF16)
        ht_ref[...] = h.T.astype(BF16)

    return _pc(body, name=name, out_shape=[jax.ShapeDtypeStruct((m, d), BF16), jax.ShapeDtypeStruct((d, m), BF16)],
               grid=(m // tm,),
               in_specs=[pl.BlockSpec((tm, d), lambda i: (i, 0)), pl.BlockSpec((1, d), lambda i: (0, 0)),
                         pl.BlockSpec((1, 3, d), lambda i: (i // per, 0, 0))],
               out_specs=[pl.BlockSpec((tm, d), lambda i: (i, 0)), pl.BlockSpec((d, tm), lambda i: (0, i))],
               sem=("parallel",))(x, gain, mod)


def _prenorm_bwd(dh, x, gain, mod, dxn, t_seq, name):
    m, d = x.shape
    nb = m // t_seq
    tm = _tile(t_seq, 1024)
    per = t_seq // tm

    def body(dh_ref, x_ref, g_ref, mod_ref, dxn_ref, dx_ref, dss_ref, dg_ref):
        i = pl.program_id(0)
        xv, dhv, g = x_ref[...], dh_ref[...], g_ref[...]
        rstd = lax.rsqrt(jnp.mean(xv * xv, axis=-1, keepdims=True) + EPS)
        xhat = xv * rstd
        dr = dhv * (1.0 + mod_ref[0, 1:2, :])
        dxhat = dr * g
        dx_ref[...] = dxn_ref[...] + rstd * (dxhat - xhat * jnp.mean(dxhat * xhat, axis=-1, keepdims=True))

        @pl.when(i % per == 0)
        def _():
            dss_ref[...] = jnp.zeros_like(dss_ref)

        @pl.when(i == 0)
        def _():
            dg_ref[...] = jnp.zeros_like(dg_ref)

        dss_ref[0, 0:1, :] += jnp.sum(dhv, axis=0, keepdims=True)
        dss_ref[0, 1:2, :] += jnp.sum(dhv * (xhat * g), axis=0, keepdims=True)
        dg_ref[...] += jnp.sum(dr * xhat, axis=0, keepdims=True)

    row = pl.BlockSpec((tm, d), lambda i: (i, 0))
    return _pc(body, name=name,
               out_shape=[jax.ShapeDtypeStruct((m, d), F32), jax.ShapeDtypeStruct((nb, 2, d), F32),
                          jax.ShapeDtypeStruct((1, d), F32)],
               grid=(m // tm,),
               in_specs=[row, row, pl.BlockSpec((1, d), lambda i: (0, 0)),
                         pl.BlockSpec((1, 3, d), lambda i: (i // per, 0, 0)), row],
               out_specs=[row, pl.BlockSpec((1, 2, d), lambda i: (i // per, 0, 0)),
                          pl.BlockSpec((1, d), lambda i: (0, 0))],
               sem=("arbitrary",))(dh, x, gain, mod, dxn)


def _mm_in(h, ws, sections, name, comm=None):
    m, k = h.shape
    nw = len(ws)
    widths = [w.shape[2] for w in ws]
    offs = [sum(widths[:a]) for a in range(nw)]
    nc = sum(widths)
    per = NDEV // sections if sections > 1 else NDEV
    tm = _din_tile(m)
    assert per % 2 == 0

    def body(*refs):
        hv = refs[0][...]
        o_ref = refs[1 + nw]
        for b in range(2):
            for a in range(nw):
                lo = b * nc + offs[a]
                o_ref[:, lo:lo + widths[a]] = _dot(hv, refs[1 + a][b])

    w_specs = [pl.BlockSpec((2, k, wd), lambda j, i: (j, 0, 0)) for wd in widths]
    if sections > 1:
        out_shape = jax.ShapeDtypeStruct((sections, m, per * nc), F32)
        out_spec = pl.BlockSpec((None, tm, 2 * nc), lambda j, i: ((2 * j) // per, i, ((2 * j) % per) // 2))
    else:
        out_shape = jax.ShapeDtypeStruct((m, NDEV * nc), F32)
        out_spec = pl.BlockSpec((tm, 2 * nc), lambda j, i: (i, j))
    return _pc(body, name=name, out_shape=out_shape, grid=(NDEV // 2, m // tm),
               in_specs=[pl.BlockSpec((tm, k), lambda j, i: (i, 0))] + w_specs,
               out_specs=out_spec, sem=("parallel", "parallel"), comm=comm)(h, *ws)


def _mm_in_hgrn(h, ws, name, comm=None):
    m, k = h.shape
    nw = len(ws)
    widths = [w.shape[2] for w in ws]
    offs = [sum(widths[:a]) for a in range(nw)]
    nc = sum(widths)
    assert NDEV // 4 == 2
    tm = _din_tile(m)
    last = m // tm - 1
    sec_i = 2

    def body(*refs):
        hv = refs[0][...]
        f_ref, v_ref = refs[1 + nw], refs[2 + nw]
        j = pl.program_id(0)
        for b in range(2):
            for a in range(nw):
                lo = b * nc + offs[a]
                acc = _dot(hv, refs[1 + a][b])

                @pl.when(j != sec_i)
                def _(acc=acc, lo=lo, wd=widths[a]):
                    f_ref[:, lo:lo + wd] = acc

                @pl.when(j == sec_i)
                def _(acc=acc, lo=lo, wd=widths[a]):
                    v_ref[:, lo:lo + wd] = acc.astype(BF16)

    def f_index(j, i):
        sec = jnp.where(j < sec_i, j, jnp.where(j == sec_i, sec_i - 1, j - 1))
        return sec, jnp.where(j == sec_i, last, i), 0

    def v_index(j, i):
        return jnp.where(j < sec_i, 0, jnp.where(j == sec_i, i, last)), 0

    w_specs = [pl.BlockSpec((2, k, wd), lambda j, i: (j, 0, 0)) for wd in widths]
    return _pc(body, name=name,
               out_shape=[jax.ShapeDtypeStruct((3, m, 2 * nc), F32), jax.ShapeDtypeStruct((m, 2 * nc), BF16)],
               grid=(4, m // tm),
               in_specs=[pl.BlockSpec((tm, k), lambda j, i: (i, 0))] + w_specs,
               out_specs=[pl.BlockSpec((None, tm, 2 * nc), f_index), pl.BlockSpec((tm, 2 * nc), v_index)],
               sem=("arbitrary", "arbitrary"), comm=comm)(h, *ws)


def _din_tile(m):
    return 1024 if m % 1024 == 0 and m >= 2048 else _tile(m, 512)


def _mm_din(dproj, ws, sections, name, comm=None, tiles=None, prev=None):
    nw, k = len(ws), ws[0].shape[1]
    widths = [w.shape[2] for w in ws]
    offs = [sum(widths[:a]) for a in range(nw)]
    nc = sum(widths)
    m = dproj.shape[-2]
    tm = _din_tile(m)
    t0, nt = tiles if tiles is not None else (0, m // tm)
    per = NDEV // sections if sections > 1 else NDEV
    assert per % 2 == 0

    def body(*refs):
        d_ref, o_ref = refs[0], refs[-1]
        j = pl.program_id(1)
        acc = None
        for b in range(2):
            for a in range(nw):
                lo = b * nc + offs[a]
                term = _dot_nt(d_ref[:, lo:lo + widths[a]], refs[1 + a][b])
                acc = term if acc is None else acc + term

        @pl.when(j == 0)
        def _():
            o_ref[...] = acc

        @pl.when(j > 0)
        def _():
            o_ref[...] += acc

    if sections > 1:
        dspec = pl.BlockSpec((None, tm, 2 * nc), lambda i, j: ((2 * j) // per, i + t0, ((2 * j) % per) // 2))
    else:
        dspec = pl.BlockSpec((tm, 2 * nc), lambda i, j: (i + t0, j))
    in_specs = [dspec] + [pl.BlockSpec((2, k, wd), lambda i, j: (j, 0, 0)) for wd in widths]
    args = [dproj, *ws]
    if prev is not None:
        in_specs.append(ANY)
        args.append(prev)
    return _pc(body, name=name, out_shape=jax.ShapeDtypeStruct((m, k), F32), grid=(nt, NDEV // 2), in_specs=in_specs,
               out_specs=pl.BlockSpec((tm, k), lambda i, j: (i + t0, 0)), sem=("parallel", "arbitrary"),
               comm=comm, aliases={1 + nw: 0} if prev is not None else None)(*args)


def _mm_dw_in(ht, dproj, nc, sections, name, comm=None):
    k, m = ht.shape
    per = NDEV // sections if sections > 1 else NDEV

    def body(h_ref, d_ref, o_ref):
        o_ref[...] = _dot(h_ref[...], d_ref[...])

    if sections > 1:
        dspec = pl.BlockSpec((None, m, nc), lambda j: (j // per, 0, j % per))
    else:
        dspec = pl.BlockSpec((m, nc), lambda j: (0, j))
    return _pc(body, name=name, out_shape=jax.ShapeDtypeStruct((NDEV, k, nc), F32), grid=(NDEV,),
               in_specs=[pl.BlockSpec((k, m), lambda j: (0, 0)), dspec],
               out_specs=pl.BlockSpec((None, k, nc), lambda j: (j, 0, 0)),
               sem=("parallel",), comm=comm)(ht, dproj)


def _out_proj(ybr, w_out, x, mod, t_seq, name, comm=None):
    m, di = ybr.shape
    d = w_out.shape[1]
    tm = _tile(t_seq, 512)
    per = t_seq // tm

    def body(y_ref, w_ref, x_ref, mod_ref, yo_ref, xn_ref):
        yo = _dot(y_ref[...], w_ref[...])
        yo_ref[...] = yo
        xn_ref[...] = x_ref[...] + mod_ref[0, 2:3, :] * yo

    row = pl.BlockSpec((tm, d), lambda i: (i, 0))
    return _pc(body, name=name,
               out_shape=[jax.ShapeDtypeStruct((m, d), F32), jax.ShapeDtypeStruct((m, d), F32)],
               grid=(m // tm,),
               in_specs=[pl.BlockSpec((tm, di), lambda i: (i, 0)), pl.BlockSpec((di, d), lambda i: (0, 0)), row,
                         pl.BlockSpec((1, 3, d), lambda i: (i // per, 0, 0))],
               out_specs=[row, row], sem=("parallel",), comm=comm)(ybr, w_out, x, mod)


def _out_proj_loss(ybr, w_out, x, mod, gain, target, t_seq):
    m, di = ybr.shape
    d = w_out.shape[1]
    tm = _tile(t_seq, 512)
    per = t_seq // tm

    def body(y_ref, w_ref, x_ref, mod_ref, g_ref, t_ref, yo_ref, dx_ref, loss_ref, dg_ref):
        i = pl.program_id(0)
        yo = _dot(y_ref[...], w_ref[...])
        yo_ref[...] = yo
        xv = x_ref[...] + mod_ref[0, 2:3, :] * yo
        g = g_ref[...]
        rstd = lax.rsqrt(jnp.mean(xv * xv, axis=-1, keepdims=True) + EPS)
        xhat = xv * rstd
        err = xhat * g - t_ref[...]
        dy = err * (1.0 / d)
        dxhat = dy * g
        dx_ref[...] = rstd * (dxhat - xhat * jnp.mean(dxhat * xhat, axis=-1, keepdims=True))

        @pl.when(i == 0)
        def _():
            loss_ref[...] = jnp.zeros_like(loss_ref)
            dg_ref[...] = jnp.zeros_like(dg_ref)

        loss_ref[...] += 0.5 * jnp.sum(jnp.mean(err * err, axis=-1, keepdims=True), axis=0, keepdims=True)
        dg_ref[...] += jnp.sum(dy * xhat, axis=0, keepdims=True)

    row = pl.BlockSpec((tm, d), lambda i: (i, 0))
    vec = pl.BlockSpec((1, d), lambda i: (0, 0))
    return _pc(body, name="out_proj_loss",
               out_shape=[jax.ShapeDtypeStruct((m, d), F32), jax.ShapeDtypeStruct((m, d), F32),
                          jax.ShapeDtypeStruct((1, 1), F32), jax.ShapeDtypeStruct((1, d), F32)],
               grid=(m // tm,),
               in_specs=[pl.BlockSpec((tm, di), lambda i: (i, 0)), pl.BlockSpec((di, d), lambda i: (0, 0)), row,
                         pl.BlockSpec((1, 3, d), lambda i: (i // per, 0, 0)), vec, row],
               out_specs=[row, row, pl.BlockSpec((1, 1), lambda i: (0, 0)), vec],
               sem=("arbitrary",))(ybr, w_out, x, mod, gain, target)


def _gate_dybr(dxn, yout, mod, w_out, t_seq, name):
    m, d = dxn.shape
    di = w_out.shape[0]
    nb = m // t_seq
    tm = _tile(t_seq, 512)
    per = t_seq // tm

    def body(dxn_ref, yo_ref, mod_ref, w_ref, dy_ref, dgate_ref, o_ref):
        i = pl.program_id(0)
        dv = dxn_ref[...]
        dy = (mod_ref[0, 2:3, :] * dv).astype(BF16)
        dy_ref[...] = dy
        o_ref[...] = _dot_nt(dy, w_ref[...])

        @pl.when(i % per == 0)
        def _():
            dgate_ref[...] = jnp.zeros_like(dgate_ref)

        dgate_ref[0] += jnp.sum(dv * yo_ref[...], axis=0, keepdims=True)

    row = pl.BlockSpec((tm, d), lambda i: (i, 0))
    return _pc(body, name=name,
               out_shape=[jax.ShapeDtypeStruct((m, d), BF16), jax.ShapeDtypeStruct((nb, 1, d), F32),
                          jax.ShapeDtypeStruct((m, di), F32)],
               grid=(m // tm,),
               in_specs=[row, row, pl.BlockSpec((1, 3, d), lambda i: (i // per, 0, 0)),
                         pl.BlockSpec((di, d), lambda i: (0, 0))],
               out_specs=[row, pl.BlockSpec((1, 1, d), lambda i: (i // per, 0, 0)),
                          pl.BlockSpec((tm, di), lambda i: (i, 0))],
               sem=("arbitrary",))(dxn, yout, mod, w_out)


def _mm_dw_out(ybr, dy, name, comm=None):
    m, di = ybr.shape
    d = dy.shape[1]
    tn = _tile(di, 1024)

    def body(y_ref, dy_ref, o_ref):
        o_ref[...] = _dot_tn(y_ref[...], dy_ref[...])

    return _pc(body, name=name, out_shape=jax.ShapeDtypeStruct((di, d), F32), grid=(di // tn,),
               in_specs=[pl.BlockSpec((m, tn), lambda n: (0, n)), pl.BlockSpec((m, d), lambda n: (0, 0))],
               out_specs=pl.BlockSpec((tn, d), lambda n: (n, 0)), sem=("parallel",), comm=comm)(ybr, dy)


def _sgu_mask():
    t = lax.broadcasted_iota(jnp.int32, (SG_BLOCK, SG_BLOCK), 0)
    s = lax.broadcasted_iota(jnp.int32, (SG_BLOCK, SG_BLOCK), 1)
    return (s // CHUNK) <= (t // CHUNK)


def _a_mid_fwd(proj, ln_g, ln_b, w_s, bs_t, t_seq, comm=None):
    m, n3 = proj.shape
    di = n3 // 3
    gd = di // SG_GROUPS
    r = _tile(t_seq, 256)
    nblk = r // SG_BLOCK

    def body(p_ref, lg_ref, lb_ref, ws_ref, bs_ref, ybr_ref, s_scr):
        v = _gelu(p_ref[:, di:2 * di])
        mu = jnp.mean(v, axis=-1, keepdims=True)
        vc = v - mu
        rstd = lax.rsqrt(jnp.mean(vc * vc, axis=-1, keepdims=True) + EPS)
        vb = (vc * rstd * lg_ref[...] + lb_ref[...]).astype(BF16)
        mask = _sgu_mask()
        for gi in range(SG_GROUPS):
            ws = jnp.where(mask, ws_ref[gi], 0.0).astype(BF16)
            bcol = bs_ref[:, gi:gi + 1]
            for b in range(nblk):
                rows = slice(b * SG_BLOCK, (b + 1) * SG_BLOCK)
                cols = slice(gi * gd, (gi + 1) * gd)
                s_scr[rows, cols] = _dot(ws, vb[rows, cols]) + bcol
        gg = p_ref[:, 2 * di:]
        ybr_ref[...] = (_gelu(p_ref[:, :di]) * s_scr[...] * (gg * _sigmoid(gg))).astype(BF16)

    vec = pl.BlockSpec((1, di), lambda i: (0, 0))
    return _pc(body, name="a_mid_fwd", out_shape=jax.ShapeDtypeStruct((m, di), BF16), grid=(m // r,),
               in_specs=[pl.BlockSpec((r, n3), lambda i: (i, 0)), vec, vec,
                         pl.BlockSpec((SG_GROUPS, SG_BLOCK, SG_BLOCK), lambda i: (0, 0, 0)),
                         pl.BlockSpec((SG_BLOCK, 128), lambda i: (0, 0))],
               out_specs=pl.BlockSpec((r, di), lambda i: (i, 0)),
               scratch=[pltpu.VMEM((r, di), F32)], sem=("parallel",), comm=comm)(proj, ln_g, ln_b, w_s, bs_t)


def _a_mid_bwd(proj, dybr, ln_g, ln_b, w_s, bs_t, t_seq, comm=None):
    m, n3 = proj.shape
    di = n3 // 3
    gd = di // SG_GROUPS
    r = _tile(t_seq, 256)
    nblk = r // SG_BLOCK

    def body(p_ref, dy_ref, lg_ref, lb_ref, ws_ref, bs_ref,
             dp_ref, dlg_ref, dlb_ref, dws_ref, dbs_ref, s_scr, dvl_scr):
        i = pl.program_id(0)

        @pl.when(i == 0)
        def _():
            dlg_ref[...] = jnp.zeros_like(dlg_ref)
            dlb_ref[...] = jnp.zeros_like(dlb_ref)
            dws_ref[...] = jnp.zeros_like(dws_ref)
            dbs_ref[...] = jnp.zeros_like(dbs_ref)

        v, dgelu_v = _gelu_and_grad(p_ref[:, di:2 * di])
        mu = jnp.mean(v, axis=-1, keepdims=True)
        vc = v - mu
        rstd = lax.rsqrt(jnp.mean(vc * vc, axis=-1, keepdims=True) + EPS)
        vhat = vc * rstd
        lg = lg_ref[...]
        vb = (vhat * lg + lb_ref[...]).astype(BF16)
        u, dgelu_u = _gelu_and_grad(p_ref[:, :di])
        gg = p_ref[:, 2 * di:]
        sg = _sigmoid(gg)
        dyv = dy_ref[...]
        dus = dyv * (gg * sg)
        dsb = (dus * u).astype(BF16)
        ds32 = dus * u
        mask = _sgu_mask()
        lane = lax.broadcasted_iota(jnp.int32, (SG_BLOCK, 128), 1)
        dbs_acc = jnp.zeros((SG_BLOCK, 128), F32)
        for gi in range(SG_GROUPS):
            ws = jnp.where(mask, ws_ref[gi], 0.0).astype(BF16)
            bcol = bs_ref[:, gi:gi + 1]
            cols = slice(gi * gd, (gi + 1) * gd)
            dws_acc = jnp.zeros((SG_BLOCK, SG_BLOCK), F32)
            dbs_col = jnp.zeros((SG_BLOCK, 1), F32)
            for b in range(nblk):
                rows = slice(b * SG_BLOCK, (b + 1) * SG_BLOCK)
                s_scr[rows, cols] = _dot(ws, vb[rows, cols]) + bcol
                dvl_scr[rows, cols] = _dot_tn(ws, dsb[rows, cols])
                dws_acc += _dot_nt(dsb[rows, cols], vb[rows, cols])
                dbs_col += jnp.sum(ds32[rows, cols], axis=-1, keepdims=True)
            dws_ref[gi] += jnp.where(mask, dws_acc, 0.0)
            dbs_acc += jnp.where(lane == gi, dbs_col, 0.0)
        dbs_ref[...] += dbs_acc
        s = s_scr[...]
        dp_ref[:, :di] = (dus * s * dgelu_u).astype(BF16)
        dp_ref[:, 2 * di:] = (dyv * u * s * (sg * (1.0 + gg * (1.0 - sg)))).astype(BF16)
        dvl = dvl_scr[...]
        dlg_ref[...] += jnp.sum(dvl * vhat, axis=0, keepdims=True)
        dlb_ref[...] += jnp.sum(dvl, axis=0, keepdims=True)
        dvh = dvl * lg
        dv = rstd * (dvh - jnp.mean(dvh, axis=-1, keepdims=True)
                     - vhat * jnp.mean(dvh * vhat, axis=-1, keepdims=True))
        dp_ref[:, di:2 * di] = (dv * dgelu_v).astype(BF16)

    vec = pl.BlockSpec((1, di), lambda i: (0, 0))
    wsb = pl.BlockSpec((SG_GROUPS, SG_BLOCK, SG_BLOCK), lambda i: (0, 0, 0))
    bsb = pl.BlockSpec((SG_BLOCK, 128), lambda i: (0, 0))
    return _pc(body, name="a_mid_bwd",
               out_shape=[jax.ShapeDtypeStruct((m, n3), BF16), jax.ShapeDtypeStruct((1, di), F32),
                          jax.ShapeDtypeStruct((1, di), F32),
                          jax.ShapeDtypeStruct((SG_GROUPS, SG_BLOCK, SG_BLOCK), F32),
                          jax.ShapeDtypeStruct((SG_BLOCK, 128), F32)],
               grid=(m // r,),
               in_specs=[pl.BlockSpec((r, n3), lambda i: (i, 0)), pl.BlockSpec((r, di), lambda i: (i, 0)),
                         vec, vec, wsb, bsb],
               out_specs=[pl.BlockSpec((r, n3), lambda i: (i, 0)), vec, vec, wsb, bsb],
               scratch=[pltpu.VMEM((r, di), F32), pltpu.VMEM((r, di), F32)],
               sem=("arbitrary",), comm=comm)(proj, dybr, ln_g, ln_b, w_s, bs_t)


def _chunk_rows(n):
    if isinstance(n, int):
        return pl.ds(n * CHUNK, CHUNK)
    return pl.ds(pl.multiple_of(n * CHUNK, CHUNK), CHUNK)


def _hgrn_dims(t_seq, di):
    tr = _tile(t_seq, 128)
    hc = _tile(di, 2048)
    return tr, hc, hc // HEAD_DIM


def _hgrn_gates(f_ref, lb, a_scr, k_scr, tr):
    sig = _sigmoid(f_ref[...])
    fg = lb + (1.0 - lb) * sig
    k_scr[...] = 1.0 - fg
    logf = jnp.log(fg)
    g = min(CUM_ROWS, tr)
    tri = _tri_mask(g, reverse=False)
    for rg in range(tr // g):
        a_scr[rg * g:(rg + 1) * g, :] = _tri_apply(tri, logf[rg * g:(rg + 1) * g, :])
    return sig, fg


def _hgrn_fwd(proj, proj_i, lbj, gn, nb, t_seq):
    _, m, di = proj.shape
    tr, hc, hpg = _hgrn_dims(t_seq, di)
    nt, nhg, ncl = t_seq // tr, di // hc, tr // CHUNK
    nheads = di // HEAD_DIM

    def body(p_ref, i_ref, lb_ref, gn_ref, o_ref, ybr_ref, st_ref, st_scr, a_scr, k_scr):
        q_ref, f_ref, g_ref = (p_ref.at[s] for s in range(3))
        t = pl.program_id(2)

        @pl.when(t == 0)
        def _():
            st_scr[...] = jnp.zeros_like(st_scr)

        _hgrn_gates(f_ref, lb_ref[0:1, :], a_scr, k_scr, tr)
        gnv = gn_ref[...]
        rr = lax.broadcasted_iota(jnp.int32, (CHUNK, CHUNK), 0)
        cc = lax.broadcasted_iota(jnp.int32, (CHUNK, CHUNK), 1)
        causal = cc <= rr

        def chunk(n, carry):
            rows = _chunk_rows(n)
            lanes = [slice(hd * HEAD_DIM, (hd + 1) * HEAD_DIM) for hd in range(hpg)]
            hs = []
            for hd, ls in enumerate(lanes):
                h = {}
                ah, kh = a_scr[rows, ls], k_scr[rows, ls]
                qp = q_ref[rows, ls]
                qh = qp * _sigmoid(qp)
                h["vb"] = i_ref[rows, ls]
                aref, alast = ah[CHUNK // 2 - 1:CHUNK // 2, :], ah[CHUNK - 1:CHUNK, :]
                h["q_in"] = (qh * jnp.exp(ah - aref)).astype(BF16)
                h["k_in"] = (kh * jnp.exp(aref - ah)).astype(BF16)
                h["q_out"] = (qh * jnp.exp(ah)).astype(BF16)
                h["k_out"] = (kh * jnp.exp(alast - ah)).astype(BF16)
                h["dec"] = jnp.exp(alast)
                st = st_scr[hd]
                st_ref[n, hd] = st
                h["st"] = st
                hs.append(h)
            for h in hs:
                h["scores"] = _dot_nt(h["q_in"], h["k_in"])
                h["o_inter"] = _dot_nt(h["q_out"], h["st"].astype(BF16))
                h["st_mm"] = _dot_tn(h["vb"], h["k_out"])
            for h in hs:
                h["o"] = _dot(jnp.where(causal, h["scores"], 0.0).astype(BF16), h["vb"]) + h["o_inter"]
            for hd, (h, ls) in enumerate(zip(hs, lanes)):
                st_scr[hd] = h["st"] * h["dec"] + h["st_mm"]
                o = h["o"]
                o_ref[rows, ls] = o
                rstd = lax.rsqrt(jnp.mean(o * o, axis=-1, keepdims=True) + EPS)
                gg = g_ref[rows, ls]
                ybr_ref[rows, ls] = ((o * rstd * gnv) * (gg * _sigmoid(gg))).astype(BF16)
            return carry

        lax.fori_loop(0, ncl, chunk, 0)

    blk = pl.BlockSpec((tr, hc), lambda hg, b, t: (b * nt + t, hg))
    return _pc(body, name="hgrn_fwd",
               out_shape=[jax.ShapeDtypeStruct((m, di), F32), jax.ShapeDtypeStruct((m, di), BF16),
                          jax.ShapeDtypeStruct((m // CHUNK, nheads, HEAD_DIM, HEAD_DIM), F32)],
               grid=(nhg, nb, nt),
               in_specs=[pl.BlockSpec((3, tr, hc), lambda hg, b, t: (0, b * nt + t, hg)), blk,
                         pl.BlockSpec((2, hc), lambda hg, b, t: (0, hg)),
                         pl.BlockSpec((1, HEAD_DIM), lambda hg, b, t: (0, 0))],
               out_specs=[blk, blk, pl.BlockSpec((ncl, hpg, HEAD_DIM, HEAD_DIM),
                                                 lambda hg, b, t: (b * nt + t, hg, 0, 0))],
               scratch=[pltpu.VMEM((hpg, HEAD_DIM, HEAD_DIM), F32), pltpu.VMEM((tr, hc), F32),
                        pltpu.VMEM((tr, hc), F32)],
               sem=("parallel", "arbitrary", "arbitrary"))(proj, proj_i, lbj, gn)


def _hgrn_bwd(proj, proj_i, o_all, dybr, states, lbj, gn, nb, t_seq, comm=None):
    _, m, di = proj.shape
    tr, hc, hpg = _hgrn_dims(t_seq, di)
    nt, nhg, ncl = t_seq // tr, di // hc, tr // CHUNK

    def body(p_ref, i_ref, o_ref, dy_ref, st_ref, lb_ref, gn_ref,
             dp_ref, dlb_ref, dgn_ref, dst_scr, a_scr, k_scr, da_scr, dk_scr):
        q_ref, f_ref, g_ref = (p_ref.at[s] for s in range(3))
        hg, b, t = pl.program_id(0), pl.program_id(1), pl.program_id(2)

        @pl.when(t == 0)
        def _():
            dst_scr[...] = jnp.zeros_like(dst_scr)

        @pl.when((b == 0) & (t == 0))
        def _():
            dlb_ref[...] = jnp.zeros_like(dlb_ref)

        @pl.when((hg == 0) & (b == 0) & (t == 0))
        def _():
            dgn_ref[...] = jnp.zeros_like(dgn_ref)

        lb = lb_ref[0:1, :]
        sig, fg = _hgrn_gates(f_ref, lb, a_scr, k_scr, tr)
        gnv = gn_ref[...]
        rr = lax.broadcasted_iota(jnp.int32, (CHUNK, CHUNK), 0)
        cc = lax.broadcasted_iota(jnp.int32, (CHUNK, CHUNK), 1)
        causal = cc <= rr
        rowi = lax.broadcasted_iota(jnp.int32, (CHUNK, HEAD_DIM), 0)

        def chunk(it, carry):
            n = ncl - 1 - it
            rows = _chunk_rows(n)
            for hd0 in range(0, hpg, PHASE_HEADS):
                heads(n, rows, range(hd0, min(hpg, hd0 + PHASE_HEADS)))
            return carry

        def heads(n, rows, ids):
            lanes = [slice(hd * HEAD_DIM, (hd + 1) * HEAD_DIM) for hd in ids]
            hs = []
            for hd, ls in zip(ids, lanes):
                h = {}
                ah, kh = a_scr[rows, ls], k_scr[rows, ls]
                qp = q_ref[rows, ls]
                sq = _sigmoid(qp)
                qh = qp * sq
                h["dsilu_q"] = sq * (1.0 + qp * (1.0 - sq))
                h["vb"] = i_ref[rows, ls]
                aref, alast = ah[CHUNK // 2 - 1:CHUNK // 2, :], ah[CHUNK - 1:CHUNK, :]
                h["e1"], h["e2"] = jnp.exp(ah - aref), jnp.exp(aref - ah)
                h["e3"], h["e4"] = jnp.exp(ah), jnp.exp(alast - ah)
                h["dec"] = jnp.exp(alast)
                h["q_in"], h["k_in"], h["q_out"], h["k_out"] = qh * h["e1"], kh * h["e2"], qh * h["e3"], kh * h["e4"]
                for nm in ("q_in", "k_in", "q_out", "k_out"):
                    h[nm + "_b"] = h[nm].astype(BF16)
                o = o_ref[rows, ls]
                rstd = lax.rsqrt(jnp.mean(o * o, axis=-1, keepdims=True) + EPS)
                ohat = o * rstd
                gg = g_ref[rows, ls]
                sg = _sigmoid(gg)
                dyv = dy_ref[rows, ls]
                d_on = dyv * (gg * sg)
                dp_ref[3, rows, ls] = (dyv * (ohat * gnv) * (sg * (1.0 + gg * (1.0 - sg)))).astype(BF16)
                h["dgn"] = jnp.sum(d_on * ohat, axis=0, keepdims=True)
                dohat = d_on * gnv
                do = rstd * (dohat - ohat * jnp.mean(dohat * ohat, axis=-1, keepdims=True))
                h["do_b"] = do.astype(BF16)
                h["st_prev"] = st_ref[n, hd]
                h["dst"] = dst_scr[hd]
                hs.append(h)
            for h in hs:
                dst_b = h["dst"].astype(BF16)
                h["scores"] = _dot_nt(h["q_in_b"], h["k_in_b"])
                h["dscores"] = _dot_nt(h["do_b"], h["vb"])
                h["dv_inter"] = _dot_nt(h["k_out_b"], dst_b)
                h["dq_out"] = _dot(h["do_b"], h["st_prev"].astype(BF16))
                h["dk_out"] = _dot(h["vb"], dst_b)
                h["dst_mm"] = _dot_tn(h["do_b"], h["q_out_b"])
            for h in hs:
                scores = jnp.where(causal, h["scores"], 0.0).astype(BF16)
                dscores = jnp.where(causal, h["dscores"], 0.0).astype(BF16)
                h["dv"] = _dot_tn(scores, h["do_b"]) + h["dv_inter"]
                h["dq_in"] = _dot(dscores, h["k_in_b"])
                h["dk_in"] = _dot_tn(dscores, h["q_in_b"])
            dgn = hs[0]["dgn"]
            for h in hs[1:]:
                dgn = dgn + h["dgn"]
            dgn_ref[...] += dgn
            for hd, h, ls in zip(ids, hs, lanes):
                ddec = jnp.sum(h["dst"] * h["st_prev"], axis=0, keepdims=True)
                dst_scr[hd] = h["dst"] * h["dec"] + h["dst_mm"]
                dp_ref[2, rows, ls] = h["dv"].astype(BF16)
                dq = h["dq_in"] * h["e1"] + h["dq_out"] * h["e3"]
                dp_ref[0, rows, ls] = (dq * h["dsilu_q"]).astype(BF16)
                dk_scr[rows, ls] = h["dk_in"] * h["e2"] + h["dk_out"] * h["e4"]
                t_in = h["dq_in"] * h["q_in"] - h["dk_in"] * h["k_in"]
                t_out = h["dk_out"] * h["k_out"]
                da = t_in + h["dq_out"] * h["q_out"] - t_out
                da_ref_row = -jnp.sum(t_in, axis=0, keepdims=True)
                da_last_row = jnp.sum(t_out, axis=0, keepdims=True) + ddec * h["dec"]
                da = da + jnp.where(rowi == CHUNK // 2 - 1, da_ref_row, 0.0) \
                        + jnp.where(rowi == CHUNK - 1, da_last_row, 0.0)
                da_scr[rows, ls] = da

        if ncl <= 2:
            for it in range(ncl):
                chunk(it, 0)
        else:
            lax.fori_loop(0, ncl, chunk, 0)
        g = min(CUM_ROWS, tr)
        tri = _tri_mask(g, reverse=True)
        for rg in range(tr // g):
            rs = slice(rg * g, (rg + 1) * g)
            dlogf = _tri_apply(tri, da_scr[rs, :])
            df = dlogf / fg[rs, :] - dk_scr[rs, :]
            sgr = sig[rs, :]
            dp_ref[1, rs, :] = (df * (1.0 - lb) * (sgr * (1.0 - sgr))).astype(BF16)
            dlb_ref[...] += jnp.sum(df * (1.0 - sgr), axis=0, keepdims=True) * lb_ref[1:2, :]

    blk = pl.BlockSpec((tr, hc), lambda hg, b, t: (b * nt + (nt - 1 - t), hg))
    return _pc(body, name="hgrn_bwd",
               out_shape=[jax.ShapeDtypeStruct((4, m, di), BF16), jax.ShapeDtypeStruct((1, di), F32),
                          jax.ShapeDtypeStruct((1, HEAD_DIM), F32)],
               grid=(nhg, nb, nt),
               in_specs=[pl.BlockSpec((3, tr, hc), lambda hg, b, t: (0, b * nt + (nt - 1 - t), hg)), blk, blk, blk,
                         pl.BlockSpec((ncl, hpg, HEAD_DIM, HEAD_DIM),
                                      lambda hg, b, t: (b * nt + (nt - 1 - t), hg, 0, 0)),
                         pl.BlockSpec((2, hc), lambda hg, b, t: (0, hg)),
                         pl.BlockSpec((1, HEAD_DIM), lambda hg, b, t: (0, 0))],
               out_specs=[pl.BlockSpec((4, tr, hc), lambda hg, b, t: (0, b * nt + (nt - 1 - t), hg)),
                          pl.BlockSpec((1, hc), lambda hg, b, t: (0, hg)),
                          pl.BlockSpec((1, HEAD_DIM), lambda hg, b, t: (0, 0))],
               scratch=[pltpu.VMEM((hpg, HEAD_DIM, HEAD_DIM), F32)] + [pltpu.VMEM((tr, hc), F32)] * 4,
               sem=("arbitrary", "arbitrary", "arbitrary"), comm=comm)(
                   proj, proj_i, o_all, dybr, states, lbj, gn)


def _adamw(parts, w, m, v, name):
    r, c = w.shape
    tr = _tile(r, 256)
    npart = len(parts)
    c1 = 1.0 - ADAM_B1 ** ADAM_STEP
    c2 = 1.0 - ADAM_B2 ** ADAM_STEP

    def body(*refs):
        p_refs = refs[:npart]
        _adamw_math(p_refs, *refs[npart:], c1, c2)

    blk = pl.BlockSpec((tr, c), lambda i: (i, 0))
    return _pc(body, name=name, out_shape=[jax.ShapeDtypeStruct((r, c), F32)] * 4, grid=(r // tr,),
               in_specs=[blk] * (npart + 3), out_specs=[blk] * 4, sem=("parallel",))(*parts, w, m, v)


def _adamw_math(p_refs, w_ref, m_ref, v_ref, g_ref, d_ref, nm_ref, nv_ref, c1, c2):
    g = p_refs[0][...].astype(F32)
    for p in p_refs[1:]:
        g = g + p[...].astype(F32)
    nm = ADAM_B1 * m_ref[...] + (1.0 - ADAM_B1) * g
    nv = ADAM_B2 * v_ref[...] + (1.0 - ADAM_B2) * (g * g)
    g_ref[...] = g
    nm_ref[...] = nm
    nv_ref[...] = nv
    d_ref[...] = -ADAM_LR * ((nm / c1) / (jnp.sqrt(nv / c2) + ADAM_EPS) + ADAM_WD * w_ref[...])


def _adamw_blocks(parts, idx, w, m, v, name):
    r, c = w.shape
    tr = _tile(r, 256)
    npart = len(parts)
    c1 = 1.0 - ADAM_B1 ** ADAM_STEP
    c2 = 1.0 - ADAM_B2 ** ADAM_STEP

    def body(idx_ref, *refs):
        _adamw_math(refs[:npart], *refs[npart:], c1, c2)

    def sel(p):
        return pl.BlockSpec((None, tr, c), lambda i, s: (s[p], i, 0))

    blk = pl.BlockSpec((tr, c), lambda i, s: (i, 0))
    gs = pltpu.PrefetchScalarGridSpec(num_scalar_prefetch=1, grid=(r // tr,),
                                      in_specs=[sel(p) for p in range(npart)] + [blk] * 3, out_specs=[blk] * 4)
    return _pc(body, name=name, out_shape=[jax.ShapeDtypeStruct((r, c), F32)] * 4, grid_spec=gs,
               sem=("parallel",))(idx, *parts, w, m, v)


_EARLY = ["a_ln_gain", "a_ln_bias", "a_w_s", "a_b_s", "b_lower_bounds", "b_gn_gain"]


def _pack(arrs):
    flat = jnp.concatenate([a.reshape(-1) for a in arrs])
    rows = -(-flat.shape[0] // 1024) * 8
    return jnp.pad(flat, (0, rows * 128 - flat.shape[0])).reshape(rows, 128)


def _unpack(buf, like):
    flat = buf.reshape(-1)
    out, off = [], 0
    for a in like:
        out.append(flat[off:off + a.size].reshape(a.shape))
        off += a.size
    return out


def kernel(x, c, norm_gain, w_ada, b_ada, a_w_in, a_ln_gain, a_ln_bias, a_w_s, a_b_s, a_w_out, b_w_in, b_lower_bounds, b_gn_gain, b_w_out, final_gain, loss_target, m_norm_gain, m_w_ada, m_b_ada, m_a_w_in, m_a_ln_gain, m_a_ln_bias, m_a_w_s, m_a_b_s, m_a_w_out, m_b_w_in, m_b_lower_bounds, m_b_gn_gain, m_b_w_out, m_final_gain, v_norm_gain, v_w_ada, v_b_ada, v_a_w_in, v_a_ln_gain, v_a_ln_bias, v_a_w_s, v_a_b_s, v_a_w_out, v_b_w_in, v_b_lower_bounds, v_b_gn_gain, v_b_w_out, v_final_gain):
    w = dict(norm_gain=norm_gain, w_ada=w_ada, b_ada=b_ada, a_w_in=a_w_in, a_ln_gain=a_ln_gain,
             a_ln_bias=a_ln_bias, a_w_s=a_w_s, a_b_s=a_b_s, a_w_out=a_w_out, b_w_in=b_w_in,
             b_lower_bounds=b_lower_bounds, b_gn_gain=b_gn_gain, b_w_out=b_w_out, final_gain=final_gain)
    mo = dict(norm_gain=m_norm_gain, w_ada=m_w_ada, b_ada=m_b_ada, a_w_in=m_a_w_in, a_ln_gain=m_a_ln_gain,
              a_ln_bias=m_a_ln_bias, a_w_s=m_a_w_s, a_b_s=m_a_b_s, a_w_out=m_a_w_out, b_w_in=m_b_w_in,
              b_lower_bounds=m_b_lower_bounds, b_gn_gain=m_b_gn_gain, b_w_out=m_b_w_out, final_gain=m_final_gain)
    vo = dict(norm_gain=v_norm_gain, w_ada=v_w_ada, b_ada=v_b_ada, a_w_in=v_a_w_in, a_ln_gain=v_a_ln_gain,
              a_ln_bias=v_a_ln_bias, a_w_s=v_a_w_s, a_b_s=v_a_b_s, a_w_out=v_a_w_out, b_w_in=v_b_w_in,
              b_lower_bounds=v_b_lower_bounds, b_gn_gain=v_b_gn_gain, b_w_out=v_b_w_out, final_gain=v_final_gain)

    nb, t_seq, d = x.shape
    m = nb * t_seq
    ncol_ada = w_ada.shape[2]
    xi, yi, ci = lax.axis_index("x"), lax.axis_index("y"), lax.axis_index("c")
    me = 4 * xi + 2 * yi + ci

    c_g, wa_in_g = _all_gather([c, a_w_in[0].astype(BF16)], "gather_c_wa")

    c_all = c_g.reshape(NDEV * nb, d)
    b_cols = lax.dynamic_slice(b_ada, (0, me * ncol_ada), (2, ncol_ada)).reshape(2, 1, ncol_ada)
    mod_part, lbj = _ada_fwd(c_all, w_ada, b_cols, b_lower_bounds)
    mod_all = _all_gather([mod_part], "gather_mod")[0]
    mod_mine = lax.dynamic_slice_in_dim(mod_all, me * nb, nb, axis=2)
    mod_mine = mod_mine.transpose(1, 2, 0, 3).reshape(2, nb, 3, d)
    mod0, mod1 = mod_mine[0], mod_mine[1]

    di = a_w_out.shape[1] * NDEV

    xf = x.reshape(m, d)
    tgt = loss_target.reshape(m, d)
    ng0, ng1 = norm_gain[0:1], norm_gain[1:2]
    ncb = b_w_in.shape[2]
    wb_lo, wb_hi = b_w_in[0][:, :ncb // 2].astype(BF16), b_w_in[0][:, ncb // 2:].astype(BF16)
    h0, h0_t = _prenorm(xf, ng0, mod0, t_seq, "prenorm_a")
    proj_a, half = _mm_in(h0, [wa_in_g], 1, "in_proj_a", comm=_gather_first([a_w_out[0].astype(BF16), wb_lo]))
    bs_t = jnp.pad(a_b_s[0].T, ((0, 0), (0, 128 - SG_GROUPS)))
    ybr_a, (wa_out_g, wb_lo_g, wb_hi_half) = _a_mid_fwd(
        proj_a, a_ln_gain, a_ln_bias, a_w_s[0], bs_t, t_seq, comm=_join(_gather_second(half), _gather_first([wb_hi])))
    wa_out = wa_out_g.reshape(di, d)
    (yout_a, x1), (wb_hi_g, wb_out_half) = _out_proj(
        ybr_a, wa_out, xf, mod0, t_seq, "out_proj_a",
        comm=_join(_gather_second([wb_hi_half]), _gather_first([b_w_out[0].astype(BF16)])))
    wb_in_g = [wb_lo_g, wb_hi_g]
    h1, h1_t = _prenorm(x1, ng1, mod1, t_seq, "prenorm_b")
    (proj_b, proj_bi), (wb_out_g,) = _mm_in_hgrn(h1, wb_in_g, "in_proj_b", comm=_gather_second([wb_out_half]))
    wb_out = wb_out_g.reshape(di, d)
    o_b, ybr_b, states = _hgrn_fwd(proj_b, proj_bi, lbj, b_gn_gain, nb, t_seq)
    yout_b, dx2, loss_part, d_final_gain = _out_proj_loss(ybr_b, wb_out, x1, mod1, final_gain.reshape(1, d), tgt, t_seq)

    rows_out = a_w_out.shape[1]
    dy_b, dgate1, dybr_b = _gate_dybr(dx2, yout_b, mod1, wb_out, t_seq, "dybr_b")
    rs_wb_out = _ReduceScatter(_mm_dw_out(ybr_b, dy_b, "dw_out_b").reshape(NDEV, rows_out, d), "b_w_out")
    (dproj_b, d_lb, d_gn), got = _hgrn_bwd(proj_b, proj_bi, o_b, dybr_b, states, lbj, b_gn_gain, nb, t_seq,
                                           comm=rs_wb_out.swap_core())
    rs_wb_out.after_core(got[0])
    dh1, got = _mm_din(dproj_b, wb_in_g, 4, "dh_b", comm=rs_wb_out.swap_chips())
    rs_wb_out.after_chips(got[0])
    dx1, dss1, dgain1 = _prenorm_bwd(dh1, x1, ng1, mod1, dx2, t_seq, "prenorm_bwd_b")
    rs_wb_in = _ReduceScatter(_mm_dw_in(h1_t, dproj_b, ncb, 4, "dw_in_b"), "b_w_in")

    dy_a, dgate0, dybr_a = _gate_dybr(dx1, yout_a, mod0, wa_out, t_seq, "dybr_a")
    g_wa_out, got = _mm_dw_out(ybr_a, dy_a, "dw_out_a", comm=rs_wb_in.swap_core())
    rs_wb_in.after_core(got[0])
    rs_wa_out = _ReduceScatter(g_wa_out.reshape(NDEV, rows_out, d), "a_w_out")
    (dproj_a, d_lng, d_lnb, d_ws, d_bs_t), got = _a_mid_bwd(
        proj_a, dybr_a, a_ln_gain, a_ln_bias, a_w_s[0], bs_t, t_seq,
        comm=_join(rs_wb_in.swap_chips(), rs_wa_out.swap_core()))
    rs_wb_in.after_chips(got[0])
    rs_wa_out.after_core(got[1])
    part = dict(a_ln_gain=d_lng, a_ln_bias=d_lnb, a_w_s=d_ws[None], a_b_s=d_bs_t[:, :SG_GROUPS].T[None],
                b_lower_bounds=jnp.concatenate([-d_lb, d_lb], axis=0), b_gn_gain=d_gn)
    early_pack = _pack([part[k].reshape(w[k].shape) for k in _EARLY])
    g_wa_in, got = _mm_dw_in(h0_t, dproj_a, wa_in_g.shape[2], 1, "dw_in_a",
                             comm=_join(rs_wa_out.swap_chips(), _gather_first([early_pack])))
    rs_wa_out.after_chips(got[0])
    rs_wa_in = _ReduceScatter(g_wa_in, "a_w_in")
    n_tiles = m // _din_tile(m)
    assert n_tiles >= 2
    first_tiles = max(1, (3 * n_tiles) // 8)
    dh0, got2 = _mm_din(dproj_a, [wa_in_g], 1, "dh_a_first", tiles=(0, first_tiles),
                        comm=_join(rs_wa_in.swap_core(), _gather_second([got[1]])))
    rs_wa_in.after_core(got2[0])
    early_all = got2[1]
    dh0, got = _mm_din(dproj_a, [wa_in_g], 1, "dh_a_rest", comm=rs_wa_in.swap_chips(),
                       tiles=(first_tiles, n_tiles - first_tiles), prev=dh0)
    rs_wa_in.after_chips(got[0])
    dx0, dss0, dgain0 = _prenorm_bwd(dh0, xf, ng0, mod0, dx1, t_seq, "prenorm_bwd_a")
    grad_x = dx0.reshape(nb, t_seq, d)

    dmod = jnp.stack([jnp.concatenate([dss0, dgate0], axis=1), jnp.concatenate([dss1, dgate1], axis=1)])
    late_like = [norm_gain, final_gain, loss_part.reshape(1)]
    late_pack = _pack([jnp.concatenate([dgain0, dgain1], axis=0), d_final_gain[0], loss_part.reshape(1)])
    dmod_all, late_all = _all_gather([dmod.reshape(2, nb, 3 * d), late_pack], "gather_tail")
    dmod_all = dmod_all.transpose(1, 0, 2, 3).reshape(2, NDEV * nb, 3 * d)
    dmod_cols = lax.dynamic_slice_in_dim(dmod_all, me * ncol_ada, ncol_ada, axis=2)
    g_w_ada, g_b_ada = _ada_bwd(c_all, dmod_cols, dmod_all)

    res = {}
    early_like = [w[k] for k in _EARLY]
    dev_order = jnp.arange(NDEV, dtype=jnp.int32)
    sm = _adamw_blocks([early_all] * NDEV, dev_order, _pack(early_like), _pack([mo[k] for k in _EARLY]),
                       _pack([vo[k] for k in _EARLY]), "adamw_small_early")
    sm = [dict(zip(_EARLY, _unpack(buf, early_like))) for buf in sm]
    for k in _EARLY:
        res[k] = tuple(s[k] for s in sm)
    zero = jnp.zeros((1,), F32)
    sm = _adamw_blocks([late_all] * NDEV, dev_order, _pack([norm_gain, final_gain, zero]),
                       _pack([mo["norm_gain"], mo["final_gain"], zero]),
                       _pack([vo["norm_gain"], vo["final_gain"], zero]), "adamw_small_late")
    sm = [_unpack(buf, late_like) for buf in sm]
    res["norm_gain"] = tuple(s[0] for s in sm)
    res["final_gain"] = tuple(s[1] for s in sm)
    loss = sm[0][2][0]
    rb = _adamw([g_b_ada], b_ada, mo["b_ada"], vo["b_ada"], "adamw_b_ada")
    res["b_ada"] = tuple(rb)
    sh = w_ada.shape
    ra = _adamw([g_w_ada.reshape(sh[0] * sh[1], sh[2])], w_ada.reshape(sh[0] * sh[1], sh[2]),
                mo["w_ada"].reshape(sh[0] * sh[1], sh[2]), vo["w_ada"].reshape(sh[0] * sh[1], sh[2]), "adamw_w_ada")
    res["w_ada"] = tuple(z.reshape(sh) for z in ra)

    for k, rs in (("b_w_out", rs_wb_out), ("b_w_in", rs_wb_in), ("a_w_out", rs_wa_out), ("a_w_in", rs_wa_in)):
        res[k] = tuple(z[None] for z in _adamw_blocks(rs.parts, rs.idx, w[k][0], mo[k][0], vo[k][0], "adamw_" + k))

    order = ["norm_gain", "w_ada", "b_ada", "a_w_in", "a_ln_gain", "a_ln_bias", "a_w_s", "a_b_s", "a_w_out",
             "b_w_in", "b_lower_bounds", "b_gn_gain", "b_w_out", "final_gain"]
    return (loss, grad_x, *[res[k][0] for k in order], *[res[k][1] for k in order],
            *[res[k][2] for k in order], *[res[k][3] for k in order])
```

```python
import functools
import math

import jax
import jax.numpy as jnp
from jax import lax
from jax.experimental import pallas as pl
from jax.experimental.pallas import tpu as pltpu

F32 = jnp.float32
BF16 = jnp.bfloat16
MESH = pl.DeviceIdType.MESH
NDEV = 8
EPS = 1e-6
CHUNK = 64
SG_BLOCK = 128
SG_GROUPS = 8
HEAD_DIM = 128
CUM_ROWS = 256
PHASE_HEADS = 8
ADAM_LR, ADAM_B1, ADAM_B2, ADAM_EPS, ADAM_WD, ADAM_STEP = 0.001, 0.9, 0.999, 1e-08, 0.01, 10
VMEM_LIMIT = 56 * 1024 * 1024
ANY = pl.BlockSpec(memory_space=pl.ANY)


class _Hosted:
    def __init__(self, arrays, out_shapes, nsem, start, finish, aliases=None):
        self.arrays, self.out_shapes, self.nsem = list(arrays), list(out_shapes), nsem
        self.start, self.finish = start, finish
        self.aliases = dict(aliases or {})


def _join(*comms):
    arrays, outs, aliases, offs, nsem = [], [], {}, [], 0
    for cm in comms:
        offs.append((len(arrays), len(outs), nsem))
        for i, o in cm.aliases.items():
            aliases[len(arrays) + i] = len(outs) + o
        arrays += cm.arrays
        outs += cm.out_shapes
        nsem += cm.nsem

    def run(which):
        def f(ins, outs_, ss, rs, base):
            for cm, (ia, io, isem) in zip(comms, offs):
                getattr(cm, which)(ins[ia:ia + len(cm.arrays)], outs_[io:io + len(cm.out_shapes)], ss, rs, base + isem)
        return f

    return _Hosted(arrays, outs, nsem, run("start"), run("finish"), aliases)


def _pc(body, *, name, out_shape, grid=None, in_specs=None, out_specs=None, scratch=(), sem=None,
        grid_spec=None, comm=None, aliases=None):
    cp = dict(vmem_limit_bytes=VMEM_LIMIT)
    aliases = dict(aliases or {})
    if comm is None:
        if sem is not None:
            cp["dimension_semantics"] = sem
        kw = {"input_output_aliases": aliases}
        if grid_spec is not None:
            kw["grid_spec"] = grid_spec
        else:
            if grid is not None:
                kw["grid"] = grid
            if in_specs is not None:
                kw["in_specs"] = in_specs
            if out_specs is not None:
                kw["out_specs"] = out_specs
            kw["scratch_shapes"] = list(scratch)
        return pl.pallas_call(functools.partial(body), name=name, out_shape=out_shape,
                              compiler_params=pltpu.CompilerParams(**cp), **kw)

    single = not isinstance(out_shape, (list, tuple))
    outs_list = [out_shape] if single else list(out_shape)
    ospecs = [out_specs] if single else list(out_specs)
    n_in, n_out, n_ci, n_co, n_scr = len(in_specs), len(outs_list), len(comm.arrays), len(comm.out_shapes), len(scratch)
    cp["dimension_semantics"] = ("arbitrary",) * len(grid)

    def hosted(*refs):
        cin, hin = refs[:n_in], refs[n_in:n_in + n_ci]
        cout = refs[n_in + n_ci:n_in + n_ci + n_out]
        hout = refs[n_in + n_ci + n_out:n_in + n_ci + n_out + n_co]
        scr = refs[n_in + n_ci + n_out + n_co:n_in + n_ci + n_out + n_co + n_scr]
        ssem, rsem = refs[-2], refs[-1]
        first = functools.reduce(lambda p, q: p & q, [pl.program_id(a) == 0 for a in range(len(grid))])
        last = functools.reduce(lambda p, q: p & q, [pl.program_id(a) == grid[a] - 1 for a in range(len(grid))])

        @pl.when(first)
        def _():
            comm.start(hin, hout, ssem, rsem, 0)

        body(*cin, *cout, *scr)

        @pl.when(last)
        def _():
            comm.finish(hin, hout, ssem, rsem, 0)

    call = pl.pallas_call(
        hosted, name=name, grid=grid, in_specs=list(in_specs) + [ANY] * n_ci, out_specs=ospecs + [ANY] * n_co,
        out_shape=outs_list + comm.out_shapes,
        scratch_shapes=list(scratch) + [pltpu.SemaphoreType.DMA((comm.nsem,)), pltpu.SemaphoreType.DMA((comm.nsem,))],
        input_output_aliases={**aliases, **{n_in + i: n_out + o for i, o in comm.aliases.items()}},
        compiler_params=pltpu.CompilerParams(**cp))

    def run(*args):
        res = call(*args, *comm.arrays)
        comp = res[:n_out]
        return (comp[0] if single else comp), list(res[n_out:])

    return run


def _tile(n, pref):
    return pref if n % pref == 0 else n


def _sigmoid(x):
    return 1.0 / (1.0 + jnp.exp(-x))


def _gelu(x):
    c = math.sqrt(2.0 / math.pi)
    return 0.5 * x * (1.0 + jnp.tanh(c * (x + 0.044715 * (x * x * x))))


def _gelu_and_grad(x):
    c = math.sqrt(2.0 / math.pi)
    x2 = x * x
    t = jnp.tanh(c * (x + 0.044715 * (x2 * x)))
    half = 0.5 * (1.0 + t)
    return x * half, half + (0.5 * x) * (1.0 - t * t) * (c + (3.0 * 0.044715 * c) * x2)


def _dot(a, b):
    return jnp.dot(a, b, preferred_element_type=F32)


def _dot_nt(a, b):
    return lax.dot_general(a, b, (((1,), (1,)), ((), ())), preferred_element_type=F32)


def _dot_tn(a, b):
    return lax.dot_general(a, b, (((0,), (0,)), ((), ())), preferred_element_type=F32)


def _tri_mask(n, reverse):
    r = lax.broadcasted_iota(jnp.int32, (n, n), 0)
    c = lax.broadcasted_iota(jnp.int32, (n, n), 1)
    same = (r // CHUNK) == (c // CHUNK)
    tri = (c >= r) if reverse else (c <= r)
    return jnp.where(same & tri, 1.0, 0.0).astype(BF16)


def _tri_apply(tri, x):
    hi = x.astype(BF16)
    r1 = x - hi.astype(F32)
    mid = r1.astype(BF16)
    lo = (r1 - mid.astype(F32)).astype(BF16)
    return _dot(tri, hi) + (_dot(tri, mid) + _dot(tri, lo))


def _all_gather(arrs, name):
    n = len(arrs)

    def body(*refs):
        ins, outs = refs[:n], refs[n:2 * n]
        send_sems, recv_sems, local_sems = refs[2 * n:]
        x, y, c = lax.axis_index("x"), lax.axis_index("y"), lax.axis_index("c")
        me, sibling = (x, y, c), (x, y, 1 - c)
        near = (x + c - 2 * x * c, y + (1 - c) - 2 * y * (1 - c))
        far = (x + (1 - c) - 2 * x * (1 - c), y + c - 2 * y * c)
        diag = (1 - x, 1 - y)

        def blk(a, p):
            return outs[a].at[4 * p[0] + 2 * p[1] + p[2]]

        def copy(a, k, block, to, src=None):
            return pltpu.make_async_remote_copy(
                src_ref=blk(a, block) if src is None else src, dst_ref=blk(a, block),
                send_sem=send_sems.at[7 * a + k], recv_sem=recv_sems.at[7 * a + k],
                device_id=to, device_id_type=MESH)

        mine = [pltpu.make_async_copy(ins[a], blk(a, me), local_sems.at[a]) for a in range(n)]
        for m in mine:
            m.start()
        sends = []
        for a in range(n):
            sends += [copy(a, 0, me, sibling, src=ins[a]), copy(a, 1, me, (*near, c), src=ins[a]),
                      copy(a, 2, me, (*far, c), src=ins[a])]
        for cp in sends:
            cp.start()
        for a in range(n):
            copy(a, 1, (*near, c), me).wait_recv()
            sends.append(copy(a, 3, (*near, c), (*far, c)))
            sends[-1].start()
        for a in range(n):
            sends.append(copy(a, 4, (*near, c), sibling))
            sends[-1].start()
            copy(a, 2, (*far, c), me).wait_recv()
            sends.append(copy(a, 5, (*far, c), sibling))
            sends[-1].start()
        for a in range(n):
            copy(a, 3, (*diag, c), me).wait_recv()
            sends.append(copy(a, 6, (*diag, c), sibling))
            sends[-1].start()
        for a in range(n):
            copy(a, 0, sibling, me).wait_recv()
            copy(a, 4, (*far, 1 - c), me).wait_recv()
            copy(a, 5, (*near, 1 - c), me).wait_recv()
            copy(a, 6, (*diag, 1 - c), me).wait_recv()
        for cp in sends:
            cp.wait_send()
        for m in mine:
            m.wait()

    out_shape = [jax.ShapeDtypeStruct((NDEV,) + a.shape, a.dtype) for a in arrs]
    return _pc(body, name=name, out_shape=out_shape, in_specs=[ANY] * n, out_specs=[ANY] * n,
               scratch=[pltpu.SemaphoreType.DMA((7 * n,)), pltpu.SemaphoreType.DMA((7 * n,)),
                        pltpu.SemaphoreType.DMA((n,))])(*arrs)


def _gather_first(arrs):
    n = len(arrs)

    def parts(ins, outs, ss, rs, base):
        x, y, c = lax.axis_index("x"), lax.axis_index("y"), lax.axis_index("c")
        me, sibling = (x, y, c), (x, y, 1 - c)
        chips = [(1 - x, y), (x, 1 - y), (1 - x, 1 - y)]

        def blk(a, p):
            return outs[a].at[4 * p[0] + 2 * p[1] + p[2]]

        def copy(a, k, block, to):
            return pltpu.make_async_remote_copy(
                src_ref=ins[a], dst_ref=blk(a, block), send_sem=ss.at[base + 4 * a + k],
                recv_sem=rs.at[base + 4 * a + k], device_id=to, device_id_type=MESH)

        local = [pltpu.make_async_copy(ins[a], blk(a, me), ss.at[base + 4 * n + a]) for a in range(n)]
        sends, recvs = [], []
        for a in range(n):
            sends.append(copy(a, 0, me, sibling))
            recvs.append(copy(a, 0, sibling, me))
            for j, chip in enumerate(chips):
                sends.append(copy(a, 1 + j, me, (*chip, c)))
                recvs.append(copy(a, 1 + j, (*chip, c), me))
        return local, sends, recvs

    def start(ins, outs, ss, rs, base):
        local, sends, _ = parts(ins, outs, ss, rs, base)
        for cp in local + sends:
            cp.start()

    def finish(ins, outs, ss, rs, base):
        local, sends, recvs = parts(ins, outs, ss, rs, base)
        for cp in recvs:
            cp.wait_recv()
        for cp in sends:
            cp.wait_send()
        for cp in local:
            cp.wait()

    return _Hosted(arrs, [jax.ShapeDtypeStruct((NDEV,) + a.shape, a.dtype) for a in arrs], 5 * n, start, finish)


def _gather_second(bufs):
    n = len(bufs)

    def parts(ins, outs, ss, rs, base):
        x, y, c = lax.axis_index("x"), lax.axis_index("y"), lax.axis_index("c")
        sibling = (x, y, 1 - c)
        chips = [(1 - x, y), (x, 1 - y), (1 - x, 1 - y)]
        sends, recvs = [], []
        for a in range(n):
            for j, chip in enumerate(chips):
                mine = 4 * chip[0] + 2 * chip[1] + c
                theirs = 4 * chip[0] + 2 * chip[1] + (1 - c)
                sends.append(pltpu.make_async_remote_copy(
                    src_ref=ins[a].at[mine], dst_ref=outs[a].at[mine], send_sem=ss.at[base + 3 * a + j],
                    recv_sem=rs.at[base + 3 * a + j], device_id=sibling, device_id_type=MESH))
                recvs.append(pltpu.make_async_remote_copy(
                    src_ref=ins[a].at[theirs], dst_ref=outs[a].at[theirs], send_sem=ss.at[base + 3 * a + j],
                    recv_sem=rs.at[base + 3 * a + j], device_id=sibling, device_id_type=MESH))
        return sends, recvs

    def start(ins, outs, ss, rs, base):
        for cp in parts(ins, outs, ss, rs, base)[0]:
            cp.start()

    def finish(ins, outs, ss, rs, base):
        sends, recvs = parts(ins, outs, ss, rs, base)
        for cp in recvs:
            cp.wait_recv()
        for cp in sends:
            cp.wait_send()

    return _Hosted(bufs, [jax.ShapeDtypeStruct(b.shape, b.dtype) for b in bufs], 3 * n, start, finish,
                   aliases={a: a for a in range(n)})


def _swap(src, nblk, ids_fn, partner_fn):
    def copies(ins, outs, ss, rs, base):
        x, y, c = lax.axis_index("x"), lax.axis_index("y"), lax.axis_index("c")
        ids = ids_fn(x, y, c)
        partner = partner_fn(x, y, c)
        return [pltpu.make_async_remote_copy(
            src_ref=ins[0].at[ids[k]], dst_ref=outs[0].at[k], send_sem=ss.at[base + k], recv_sem=rs.at[base + k],
            device_id=partner, device_id_type=MESH) for k in range(nblk)]

    def start(ins, outs, ss, rs, base):
        for cp in copies(ins, outs, ss, rs, base):
            cp.start()

    def finish(ins, outs, ss, rs, base):
        for cp in copies(ins, outs, ss, rs, base):
            cp.wait()

    return _Hosted([src], [jax.ShapeDtypeStruct((nblk,) + src.shape[1:], src.dtype)], nblk, start, finish)


def _swap_chips(send):
    def copies(ins, outs, ss, rs, base):
        x, y, c = lax.axis_index("x"), lax.axis_index("y"), lax.axis_index("c")
        chips = [(1 - x, y), (x, 1 - y), (1 - x, 1 - y)]
        return [pltpu.make_async_remote_copy(
            src_ref=ins[0].at[j], dst_ref=outs[0].at[j], send_sem=ss.at[base + j], recv_sem=rs.at[base + j],
            device_id=(*chip, c), device_id_type=MESH) for j, chip in enumerate(chips)]

    def start(ins, outs, ss, rs, base):
        for cp in copies(ins, outs, ss, rs, base):
            cp.start()

    def finish(ins, outs, ss, rs, base):
        for cp in copies(ins, outs, ss, rs, base):
            cp.wait()

    return _Hosted([send], [jax.ShapeDtypeStruct(send.shape, send.dtype)], 3, start, finish)


def _add_send(a, b, idx, ns, name):
    _, r, c = a.shape
    tr = _tile(r, 256)

    def body(idx_ref, a_ref, b_ref, send_ref):
        send_ref[...] = (a_ref[...] + b_ref[...]).astype(BF16)

    def sel(off):
        return pl.BlockSpec((None, tr, c), lambda k, i, s: (s[off + k], i, 0))

    gs = pltpu.PrefetchScalarGridSpec(num_scalar_prefetch=1, grid=(ns, r // tr), in_specs=[sel(0), sel(ns)],
                                      out_specs=pl.BlockSpec((None, tr, c), lambda k, i, s: (k, i, 0)))
    return _pc(body, name=name, grid_spec=gs, sem=("arbitrary", "arbitrary"),
               out_shape=jax.ShapeDtypeStruct((ns, r, c), BF16))(idx, a, b)


class _ReduceScatter:
    def __init__(self, g, tag):
        self.g, self.tag = g, tag

    def swap_core(self):
        return _swap(self.g, 4, lambda x, y, c: [1 - c, 3 - c, 5 - c, 7 - c], lambda x, y, c: (x, y, 1 - c))

    def after_core(self, recv):
        x, y, c = lax.axis_index("x"), lax.axis_index("y"), lax.axis_index("c")
        chips = [(1 - x, y), (x, 1 - y), (1 - x, 1 - y)]
        idx = jnp.stack([4 * p + 2 * q + c for p, q in chips] + [2 * p + q for p, q in chips]).astype(jnp.int32)
        self.send = _add_send(self.g, recv, idx, 3, "rs_add_" + self.tag)
        self.recv_core = recv
        zero = jnp.zeros((), jnp.int32)
        self.idx = jnp.stack([4 * x + 2 * y + c, 2 * x + y, zero, zero + 1, zero + 2]).astype(jnp.int32)

    def swap_chips(self):
        return _swap_chips(self.send)

    def after_chips(self, recv):
        self.parts = [self.g, self.recv_core, recv, recv, recv]


def _ada_fwd(c_all, w_ada, b_cols, b_lb):
    nl, d, ncol = w_ada.shape
    nseq = c_all.shape[0]
    di = b_lb.shape[1]

    def body(c_ref, w_ref, b_ref, lb_ref, mod_ref, lbj_ref):
        cv = c_ref[...]
        cact = (cv * _sigmoid(cv)).astype(BF16)
        for l in range(nl):
            mod_ref[l] = _dot(cact, w_ref[l].astype(BF16)) + b_ref[l]
        b0, b1 = lb_ref[0:1, :], lb_ref[1:2, :]
        mx = jnp.maximum(b0, b1)
        e0, e1 = jnp.exp(b0 - mx), jnp.exp(b1 - mx)
        s = e0 + e1
        p0, p1 = e0 / s, e1 / s
        lbj_ref[0:1, :] = (p0 + p1) - p0
        lbj_ref[1:2, :] = p0 * p1

    return _pc(body, name="ada_fwd",
               out_shape=[jax.ShapeDtypeStruct((nl, nseq, ncol), F32), jax.ShapeDtypeStruct((2, di), F32)]
               )(c_all, w_ada, b_cols, b_lb)


def _ada_bwd(c_all, dmod_cols, dmod_full):
    nl, nseq, ncol = dmod_cols.shape
    d = c_all.shape[1]
    d3 = dmod_full.shape[2]

    def body(c_ref, dc_ref, df_ref, gw_ref, gb_ref):
        cv = c_ref[...]
        cact = (cv * _sigmoid(cv)).astype(BF16)
        for l in range(nl):
            gw_ref[l] = _dot_tn(cact, dc_ref[l].astype(BF16))
            gb_ref[l:l + 1, :] = jnp.sum(df_ref[l], axis=0, keepdims=True)

    return _pc(body, name="ada_bwd",
               out_shape=[jax.ShapeDtypeStruct((nl, d, ncol), F32), jax.ShapeDtypeStruct((nl, d3), F32)]
               )(c_all, dmod_cols, dmod_full)


def _prenorm(x, gain, mod, t_seq, name):
    m, d = x.shape
    tm = _tile(t_seq, 1024)
    per = t_seq // tm

    def body(x_ref, g_ref, mod_ref, h_ref, ht_ref):
        xv = x_ref[...]
        rstd = lax.rsqrt(jnp.mean(xv * xv, axis=-1, keepdims=True) + EPS)
        r = xv * rstd * g_ref[...]
        h = r * (1.0 + mod_ref[0, 1:2, :]) + mod_ref[0, 0:1, :]
        h_ref[...] = h.astype(BF16)
        ht_ref[...] = h.T.astype(BF16)

    return _pc(body, name=name, out_shape=[jax.ShapeDtypeStruct((m, d), BF16), jax.ShapeDtypeStruct((d, m), BF16)],
               grid=(m // tm,),
               in_specs=[pl.BlockSpec((tm, d), lambda i: (i, 0)), pl.BlockSpec((1, d), lambda i: (0, 0)),
                         pl.BlockSpec((1, 3, d), lambda i: (i // per, 0, 0))],
               out_specs=[pl.BlockSpec((tm, d), lambda i: (i, 0)), pl.BlockSpec((d, tm), lambda i: (0, i))],
               sem=("parallel",))(x, gain, mod)


def _prenorm_bwd(dh, x, gain, mod, dxn, t_seq, name):
    m, d = x.shape
    nb = m // t_seq
    tm = _tile(t_seq, 1024)
    per = t_seq // tm

    def body(dh_ref, x_ref, g_ref, mod_ref, dxn_ref, dx_ref, dss_ref, dg_ref):
        i = pl.program_id(0)
        xv, dhv, g = x_ref[...], dh_ref[...], g_ref[...]
        rstd = lax.rsqrt(jnp.mean(xv * xv, axis=-1, keepdims=True) + EPS)
        xhat = xv * rstd
        dr = dhv * (1.0 + mod_ref[0, 1:2, :])
        dxhat = dr * g
        dx_ref[...] = dxn_ref[...] + rstd * (dxhat - xhat * jnp.mean(dxhat * xhat, axis=-1, keepdims=True))

        @pl.when(i % per == 0)
        def _():
            dss_ref[...] = jnp.zeros_like(dss_ref)

        @pl.when(i == 0)
        def _():
            dg_ref[...] = jnp.zeros_like(dg_ref)

        dss_ref[0, 0:1, :] += jnp.sum(dhv, axis=0, keepdims=True)
        dss_ref[0, 1:2, :] += jnp.sum(dhv * (xhat * g), axis=0, keepdims=True)
        dg_ref[...] += jnp.sum(dr * xhat, axis=0, keepdims=True)

    row = pl.BlockSpec((tm, d), lambda i: (i, 0))
    return _pc(body, name=name,
               out_shape=[jax.ShapeDtypeStruct((m, d), F32), jax.ShapeDtypeStruct((nb, 2, d), F32),
                          jax.ShapeDtypeStruct((1, d), F32)],
               grid=(m // tm,),
               in_specs=[row, row, pl.BlockSpec((1, d), lambda i: (0, 0)),
                         pl.BlockSpec((1, 3, d), lambda i: (i // per, 0, 0)), row],
               out_specs=[row, pl.BlockSpec((1, 2, d), lambda i: (i // per, 0, 0)),
                          pl.BlockSpec((1, d), lambda i: (0, 0))],
               sem=("arbitrary",))(dh, x, gain, mod, dxn)


def _mm_in(h, ws, sections, name, comm=None):
    m, k = h.shape
    nw = len(ws)
    widths = [w.shape[2] for w in ws]
    offs = [sum(widths[:a]) for a in range(nw)]
    nc = sum(widths)
    per = NDEV // sections if sections > 1 else NDEV
    tm = _din_tile(m)
    assert per % 2 == 0

    def body(*refs):
        hv = refs[0][...]
        o_ref = refs[1 + nw]
        for b in range(2):
            for a in range(nw):
                lo = b * nc + offs[a]
                o_ref[:, lo:lo + widths[a]] = _dot(hv, refs[1 + a][b])

    w_specs = [pl.BlockSpec((2, k, wd), lambda j, i: (j, 0, 0)) for wd in widths]
    if sections > 1:
        out_shape = jax.ShapeDtypeStruct((sections, m, per * nc), F32)
        out_spec = pl.BlockSpec((None, tm, 2 * nc), lambda j, i: ((2 * j) // per, i, ((2 * j) % per) // 2))
    else:
        out_shape = jax.ShapeDtypeStruct((m, NDEV * nc), F32)
        out_spec = pl.BlockSpec((tm, 2 * nc), lambda j, i: (i, j))
    return _pc(body, name=name, out_shape=out_shape, grid=(NDEV // 2, m // tm),
               in_specs=[pl.BlockSpec((tm, k), lambda j, i: (i, 0))] + w_specs,
               out_specs=out_spec, sem=("parallel", "parallel"), comm=comm)(h, *ws)


def _din_tile(m):
    return 1024 if m % 1024 == 0 and m >= 2048 else _tile(m, 512)


def _mm_din(dproj, ws, sections, name, comm=None, tiles=None, prev=None):
    nw, k = len(ws), ws[0].shape[1]
    widths = [w.shape[2] for w in ws]
    offs = [sum(widths[:a]) for a in range(nw)]
    nc = sum(widths)
    m = dproj.shape[-2]
    tm = _din_tile(m)
    t0, nt = tiles if tiles is not None else (0, m // tm)
    per = NDEV // sections if sections > 1 else NDEV
    assert per % 2 == 0

    def body(*refs):
        d_ref, o_ref = refs[0], refs[-1]
        j = pl.program_id(1)
        acc = None
        for b in range(2):
            for a in range(nw):
                lo = b * nc + offs[a]
                term = _dot_nt(d_ref[:, lo:lo + widths[a]], refs[1 + a][b])
                acc = term if acc is None else acc + term

        @pl.when(j == 0)
        def _():
            o_ref[...] = acc

        @pl.when(j > 0)
        def _():
            o_ref[...] += acc

    if sections > 1:
        dspec = pl.BlockSpec((None, tm, 2 * nc), lambda i, j: ((2 * j) // per, i + t0, ((2 * j) % per) // 2))
    else:
        dspec = pl.BlockSpec((tm, 2 * nc), lambda i, j: (i + t0, j))
    in_specs = [dspec] + [pl.BlockSpec((2, k, wd), lambda i, j: (j, 0, 0)) for wd in widths]
    args = [dproj, *ws]
    if prev is not None:
        in_specs.append(ANY)
        args.append(prev)
    return _pc(body, name=name, out_shape=jax.ShapeDtypeStruct((m, k), F32), grid=(nt, NDEV // 2), in_specs=in_specs,
               out_specs=pl.BlockSpec((tm, k), lambda i, j: (i + t0, 0)), sem=("parallel", "arbitrary"),
               comm=comm, aliases={1 + nw: 0} if prev is not None else None)(*args)


def _mm_dw_in(ht, dproj, nc, sections, name, comm=None):
    k, m = ht.shape
    per = NDEV // sections if sections > 1 else NDEV

    def body(h_ref, d_ref, o_ref):
        o_ref[...] = _dot(h_ref[...], d_ref[...])

    if sections > 1:
        dspec = pl.BlockSpec((None, m, nc), lambda j: (j // per, 0, j % per))
    else:
        dspec = pl.BlockSpec((m, nc), lambda j: (0, j))
    return _pc(body, name=name, out_shape=jax.ShapeDtypeStruct((NDEV, k, nc), F32), grid=(NDEV,),
               in_specs=[pl.BlockSpec((k, m), lambda j: (0, 0)), dspec],
               out_specs=pl.BlockSpec((None, k, nc), lambda j: (j, 0, 0)),
               sem=("parallel",), comm=comm)(ht, dproj)


def _out_proj(ybr, w_out, x, mod, t_seq, name, comm=None):
    m, di = ybr.shape
    d = w_out.shape[1]
    tm = _tile(t_seq, 1024)
    per = t_seq // tm

    def body(y_ref, w_ref, x_ref, mod_ref, yo_ref, xn_ref):
        yo = _dot(y_ref[...], w_ref[...])
        yo_ref[...] = yo
        xn_ref[...] = x_ref[...] + mod_ref[0, 2:3, :] * yo

    row = pl.BlockSpec((tm, d), lambda i: (i, 0))
    return _pc(body, name=name,
               out_shape=[jax.ShapeDtypeStruct((m, d), F32), jax.ShapeDtypeStruct((m, d), F32)],
               grid=(m // tm,),
               in_specs=[pl.BlockSpec((tm, di), lambda i: (i, 0)), pl.BlockSpec((di, d), lambda i: (0, 0)), row,
                         pl.BlockSpec((1, 3, d), lambda i: (i // per, 0, 0))],
               out_specs=[row, row], sem=("parallel",), comm=comm)(ybr, w_out, x, mod)


def _out_proj_loss(ybr, w_out, x, mod, gain, target, t_seq):
    m, di = ybr.shape
    d = w_out.shape[1]
    tm = _tile(t_seq, 512)
    per = t_seq // tm

    def body(y_ref, w_ref, x_ref, mod_ref, g_ref, t_ref, yo_ref, dx_ref, loss_ref, dg_ref):
        i = pl.program_id(0)
        yo = _dot(y_ref[...], w_ref[...])
        yo_ref[...] = yo
        xv = x_ref[...] + mod_ref[0, 2:3, :] * yo
        g = g_ref[...]
        rstd = lax.rsqrt(jnp.mean(xv * xv, axis=-1, keepdims=True) + EPS)
        xhat = xv * rstd
        err = xhat * g - t_ref[...]
        dy = err * (1.0 / d)
        dxhat = dy * g
        dx_ref[...] = rstd * (dxhat - xhat * jnp.mean(dxhat * xhat, axis=-1, keepdims=True))

        @pl.when(i == 0)
        def _():
            loss_ref[...] = jnp.zeros_like(loss_ref)
            dg_ref[...] = jnp.zeros_like(dg_ref)

        loss_ref[...] += 0.5 * jnp.sum(jnp.mean(err * err, axis=-1, keepdims=True), axis=0, keepdims=True)
        dg_ref[...] += jnp.sum(dy * xhat, axis=0, keepdims=True)

    row = pl.BlockSpec((tm, d), lambda i: (i, 0))
    vec = pl.BlockSpec((1, d), lambda i: (0, 0))
    return _pc(body, name="out_proj_loss",
               out_shape=[jax.ShapeDtypeStruct((m, d), F32), jax.ShapeDtypeStruct((m, d), F32),
                          jax.ShapeDtypeStruct((1, 1), F32), jax.ShapeDtypeStruct((1, d), F32)],
               grid=(m // tm,),
               in_specs=[pl.BlockSpec((tm, di), lambda i: (i, 0)), pl.BlockSpec((di, d), lambda i: (0, 0)), row,
                         pl.BlockSpec((1, 3, d), lambda i: (i // per, 0, 0)), vec, row],
               out_specs=[row, row, pl.BlockSpec((1, 1), lambda i: (0, 0)), vec],
               sem=("arbitrary",))(ybr, w_out, x, mod, gain, target)


def _gate_dybr(dxn, yout, mod, w_out, t_seq, name):
    m, d = dxn.shape
    di = w_out.shape[0]
    nb = m // t_seq
    tm = _tile(t_seq, 1024)
    per = t_seq // tm

    def body(dxn_ref, yo_ref, mod_ref, w_ref, dy_ref, dgate_ref, o_ref):
        i = pl.program_id(0)
        dv = dxn_ref[...]
        dy = (mod_ref[0, 2:3, :] * dv).astype(BF16)
        dy_ref[...] = dy
        o_ref[...] = _dot_nt(dy, w_ref[...])

        @pl.when(i % per == 0)
        def _():
            dgate_ref[...] = jnp.zeros_like(dgate_ref)

        dgate_ref[0] += jnp.sum(dv * yo_ref[...], axis=0, keepdims=True)

    row = pl.BlockSpec((tm, d), lambda i: (i, 0))
    return _pc(body, name=name,
               out_shape=[jax.ShapeDtypeStruct((m, d), BF16), jax.ShapeDtypeStruct((nb, 1, d), F32),
                          jax.ShapeDtypeStruct((m, di), F32)],
               grid=(m // tm,),
               in_specs=[row, row, pl.BlockSpec((1, 3, d), lambda i: (i // per, 0, 0)),
                         pl.BlockSpec((di, d), lambda i: (0, 0))],
               out_specs=[row, pl.BlockSpec((1, 1, d), lambda i: (i // per, 0, 0)),
                          pl.BlockSpec((tm, di), lambda i: (i, 0))],
               sem=("arbitrary",))(dxn, yout, mod, w_out)


def _mm_dw_out(ybr, dy, name, comm=None):
    m, di = ybr.shape
    d = dy.shape[1]
    tn = _tile(di, 1024)

    def body(y_ref, dy_ref, o_ref):
        o_ref[...] = _dot_tn(y_ref[...], dy_ref[...])

    return _pc(body, name=name, out_shape=jax.ShapeDtypeStruct((di, d), F32), grid=(di // tn,),
               in_specs=[pl.BlockSpec((m, tn), lambda n: (0, n)), pl.BlockSpec((m, d), lambda n: (0, 0))],
               out_specs=pl.BlockSpec((tn, d), lambda n: (n, 0)), sem=("parallel",), comm=comm)(ybr, dy)


def _sgu_mask():
    t = lax.broadcasted_iota(jnp.int32, (SG_BLOCK, SG_BLOCK), 0)
    s = lax.broadcasted_iota(jnp.int32, (SG_BLOCK, SG_BLOCK), 1)
    return (s // CHUNK) <= (t // CHUNK)


def _a_mid_fwd(proj, ln_g, ln_b, w_s, bs_t, t_seq, comm=None):
    m, n3 = proj.shape
    di = n3 // 3
    gd = di // SG_GROUPS
    r = _tile(t_seq, 256)
    nblk = r // SG_BLOCK

    def body(p_ref, lg_ref, lb_ref, ws_ref, bs_ref, ybr_ref, s_scr):
        v = _gelu(p_ref[:, di:2 * di])
        mu = jnp.mean(v, axis=-1, keepdims=True)
        vc = v - mu
        rstd = lax.rsqrt(jnp.mean(vc * vc, axis=-1, keepdims=True) + EPS)
        vb = (vc * rstd * lg_ref[...] + lb_ref[...]).astype(BF16)
        mask = _sgu_mask()
        for gi in range(SG_GROUPS):
            ws = jnp.where(mask, ws_ref[gi], 0.0).astype(BF16)
            bcol = bs_ref[:, gi:gi + 1]
            for b in range(nblk):
                rows = slice(b * SG_BLOCK, (b + 1) * SG_BLOCK)
                cols = slice(gi * gd, (gi + 1) * gd)
                s_scr[rows, cols] = _dot(ws, vb[rows, cols]) + bcol
        gg = p_ref[:, 2 * di:]
        ybr_ref[...] = (_gelu(p_ref[:, :di]) * s_scr[...] * (gg * _sigmoid(gg))).astype(BF16)

    vec = pl.BlockSpec((1, di), lambda i: (0, 0))
    return _pc(body, name="a_mid_fwd", out_shape=jax.ShapeDtypeStruct((m, di), BF16), grid=(m // r,),
               in_specs=[pl.BlockSpec((r, n3), lambda i: (i, 0)), vec, vec,
                         pl.BlockSpec((SG_GROUPS, SG_BLOCK, SG_BLOCK), lambda i: (0, 0, 0)),
                         pl.BlockSpec((SG_BLOCK, 128), lambda i: (0, 0))],
               out_specs=pl.BlockSpec((r, di), lambda i: (i, 0)),
               scratch=[pltpu.VMEM((r, di), F32)], sem=("parallel",), comm=comm)(proj, ln_g, ln_b, w_s, bs_t)


def _a_mid_bwd(proj, dybr, ln_g, ln_b, w_s, bs_t, t_seq, comm=None):
    m, n3 = proj.shape
    di = n3 // 3
    gd = di // SG_GROUPS
    r = _tile(t_seq, 256)
    nblk = r // SG_BLOCK

    def body(p_ref, dy_ref, lg_ref, lb_ref, ws_ref, bs_ref,
             dp_ref, dlg_ref, dlb_ref, dws_ref, dbs_ref, s_scr, dvl_scr):
        i = pl.program_id(0)

        @pl.when(i == 0)
        def _():
            dlg_ref[...] = jnp.zeros_like(dlg_ref)
            dlb_ref[...] = jnp.zeros_like(dlb_ref)
            dws_ref[...] = jnp.zeros_like(dws_ref)
            dbs_ref[...] = jnp.zeros_like(dbs_ref)

        v, dgelu_v = _gelu_and_grad(p_ref[:, di:2 * di])
        mu = jnp.mean(v, axis=-1, keepdims=True)
        vc = v - mu
        rstd = lax.rsqrt(jnp.mean(vc * vc, axis=-1, keepdims=True) + EPS)
        vhat = vc * rstd
        lg = lg_ref[...]
        vb = (vhat * lg + lb_ref[...]).astype(BF16)
        u, dgelu_u = _gelu_and_grad(p_ref[:, :di])
        gg = p_ref[:, 2 * di:]
        sg = _sigmoid(gg)
        dyv = dy_ref[...]
        dus = dyv * (gg * sg)
        dsb = (dus * u).astype(BF16)
        ds32 = dus * u
        mask = _sgu_mask()
        lane = lax.broadcasted_iota(jnp.int32, (SG_BLOCK, 128), 1)
        dbs_acc = jnp.zeros((SG_BLOCK, 128), F32)
        for gi in range(SG_GROUPS):
            ws = jnp.where(mask, ws_ref[gi], 0.0).astype(BF16)
            bcol = bs_ref[:, gi:gi + 1]
            cols = slice(gi * gd, (gi + 1) * gd)
            dws_acc = jnp.zeros((SG_BLOCK, SG_BLOCK), F32)
            dbs_col = jnp.zeros((SG_BLOCK, 1), F32)
            for b in range(nblk):
                rows = slice(b * SG_BLOCK, (b + 1) * SG_BLOCK)
                s_scr[rows, cols] = _dot(ws, vb[rows, cols]) + bcol
                dvl_scr[rows, cols] = _dot_tn(ws, dsb[rows, cols])
                dws_acc += _dot_nt(dsb[rows, cols], vb[rows, cols])
                dbs_col += jnp.sum(ds32[rows, cols], axis=-1, keepdims=True)
            dws_ref[gi] += jnp.where(mask, dws_acc, 0.0)
            dbs_acc += jnp.where(lane == gi, dbs_col, 0.0)
        dbs_ref[...] += dbs_acc
        s = s_scr[...]
        dp_ref[:, :di] = (dus * s * dgelu_u).astype(BF16)
        dp_ref[:, 2 * di:] = (dyv * u * s * (sg * (1.0 + gg * (1.0 - sg)))).astype(BF16)
        dvl = dvl_scr[...]
        dlg_ref[...] += jnp.sum(dvl * vhat, axis=0, keepdims=True)
        dlb_ref[...] += jnp.sum(dvl, axis=0, keepdims=True)
        dvh = dvl * lg
        dv = rstd * (dvh - jnp.mean(dvh, axis=-1, keepdims=True)
                     - vhat * jnp.mean(dvh * vhat, axis=-1, keepdims=True))
        dp_ref[:, di:2 * di] = (dv * dgelu_v).astype(BF16)

    vec = pl.BlockSpec((1, di), lambda i: (0, 0))
    wsb = pl.BlockSpec((SG_GROUPS, SG_BLOCK, SG_BLOCK), lambda i: (0, 0, 0))
    bsb = pl.BlockSpec((SG_BLOCK, 128), lambda i: (0, 0))
    return _pc(body, name="a_mid_bwd",
               out_shape=[jax.ShapeDtypeStruct((m, n3), BF16), jax.ShapeDtypeStruct((1, di), F32),
                          jax.ShapeDtypeStruct((1, di), F32),
                          jax.ShapeDtypeStruct((SG_GROUPS, SG_BLOCK, SG_BLOCK), F32),
                          jax.ShapeDtypeStruct((SG_BLOCK, 128), F32)],
               grid=(m // r,),
               in_specs=[pl.BlockSpec((r, n3), lambda i: (i, 0)), pl.BlockSpec((r, di), lambda i: (i, 0)),
                         vec, vec, wsb, bsb],
               out_specs=[pl.BlockSpec((r, n3), lambda i: (i, 0)), vec, vec, wsb, bsb],
               scratch=[pltpu.VMEM((r, di), F32), pltpu.VMEM((r, di), F32)],
               sem=("arbitrary",), comm=comm)(proj, dybr, ln_g, ln_b, w_s, bs_t)


def _chunk_rows(n):
    if isinstance(n, int):
        return pl.ds(n * CHUNK, CHUNK)
    return pl.ds(pl.multiple_of(n * CHUNK, CHUNK), CHUNK)


def _hgrn_dims(t_seq, di):
    tr = _tile(t_seq, 128)
    hc = _tile(di, 2048)
    return tr, hc, hc // HEAD_DIM


def _hgrn_gates(f_ref, lb, a_scr, k_scr, tr):
    sig = _sigmoid(f_ref[...])
    fg = lb + (1.0 - lb) * sig
    k_scr[...] = 1.0 - fg
    logf = jnp.log(fg)
    g = min(CUM_ROWS, tr)
    tri = _tri_mask(g, reverse=False)
    for rg in range(tr // g):
        a_scr[rg * g:(rg + 1) * g, :] = _tri_apply(tri, logf[rg * g:(rg + 1) * g, :])
    return sig, fg


def _hgrn_fwd(proj, lbj, gn, nb, t_seq):
    _, m, di = proj.shape
    tr, hc, hpg = _hgrn_dims(t_seq, di)
    nt, nhg, ncl = t_seq // tr, di // hc, tr // CHUNK
    nheads = di // HEAD_DIM

    def body(p_ref, lb_ref, gn_ref, o_ref, ybr_ref, st_ref, st_scr, a_scr, k_scr):
        q_ref, f_ref, i_ref, g_ref = (p_ref.at[s] for s in range(4))
        t = pl.program_id(2)

        @pl.when(t == 0)
        def _():
            st_scr[...] = jnp.zeros_like(st_scr)

        _hgrn_gates(f_ref, lb_ref[0:1, :], a_scr, k_scr, tr)
        gnv = gn_ref[...]
        rr = lax.broadcasted_iota(jnp.int32, (CHUNK, CHUNK), 0)
        cc = lax.broadcasted_iota(jnp.int32, (CHUNK, CHUNK), 1)
        causal = cc <= rr

        def chunk(n, carry):
            rows = _chunk_rows(n)
            lanes = [slice(hd * HEAD_DIM, (hd + 1) * HEAD_DIM) for hd in range(hpg)]
            hs = []
            for hd, ls in enumerate(lanes):
                h = {}
                ah, kh = a_scr[rows, ls], k_scr[rows, ls]
                qp = q_ref[rows, ls]
                qh = qp * _sigmoid(qp)
                h["vb"] = i_ref[rows, ls].astype(BF16)
                aref, alast = ah[CHUNK // 2 - 1:CHUNK // 2, :], ah[CHUNK - 1:CHUNK, :]
                h["q_in"] = (qh * jnp.exp(ah - aref)).astype(BF16)
                h["k_in"] = (kh * jnp.exp(aref - ah)).astype(BF16)
                h["q_out"] = (qh * jnp.exp(ah)).astype(BF16)
                h["k_out"] = (kh * jnp.exp(alast - ah)).astype(BF16)
                h["dec"] = jnp.exp(alast)
                st = st_scr[hd]
                st_ref[n, hd] = st
                h["st"] = st
                hs.append(h)
            for h in hs:
                h["scores"] = _dot_nt(h["q_in"], h["k_in"])
                h["o_inter"] = _dot_nt(h["q_out"], h["st"].astype(BF16))
                h["st_mm"] = _dot_tn(h["vb"], h["k_out"])
            for h in hs:
                h["o"] = _dot(jnp.where(causal, h["scores"], 0.0).astype(BF16), h["vb"]) + h["o_inter"]
            for hd, (h, ls) in enumerate(zip(hs, lanes)):
                st_scr[hd] = h["st"] * h["dec"] + h["st_mm"]
                o = h["o"]
                o_ref[rows, ls] = o
                rstd = lax.rsqrt(jnp.mean(o * o, axis=-1, keepdims=True) + EPS)
                gg = g_ref[rows, ls]
                ybr_ref[rows, ls] = ((o * rstd * gnv) * (gg * _sigmoid(gg))).astype(BF16)
            return carry

        lax.fori_loop(0, ncl, chunk, 0)

    blk = pl.BlockSpec((tr, hc), lambda hg, b, t: (b * nt + t, hg))
    return _pc(body, name="hgrn_fwd",
               out_shape=[jax.ShapeDtypeStruct((m, di), F32), jax.ShapeDtypeStruct((m, di), BF16),
                          jax.ShapeDtypeStruct((m // CHUNK, nheads, HEAD_DIM, HEAD_DIM), F32)],
               grid=(nhg, nb, nt),
               in_specs=[pl.BlockSpec((4, tr, hc), lambda hg, b, t: (0, b * nt + t, hg)),
                         pl.BlockSpec((2, hc), lambda hg, b, t: (0, hg)),
                         pl.BlockSpec((1, HEAD_DIM), lambda hg, b, t: (0, 0))],
               out_specs=[blk, blk, pl.BlockSpec((ncl, hpg, HEAD_DIM, HEAD_DIM),
                                                 lambda hg, b, t: (b * nt + t, hg, 0, 0))],
               scratch=[pltpu.VMEM((hpg, HEAD_DIM, HEAD_DIM), F32), pltpu.VMEM((tr, hc), F32),
                        pltpu.VMEM((tr, hc), F32)],
               sem=("parallel", "arbitrary", "arbitrary"))(proj, lbj, gn)


def _hgrn_bwd(proj, o_all, dybr, states, lbj, gn, nb, t_seq, comm=None):
    _, m, di = proj.shape
    tr, hc, hpg = _hgrn_dims(t_seq, di)
    nt, nhg, ncl = t_seq // tr, di // hc, tr // CHUNK

    def body(p_ref, o_ref, dy_ref, st_ref, lb_ref, gn_ref,
             dp_ref, dlb_ref, dgn_ref, dst_scr, a_scr, k_scr, da_scr, dk_scr):
        q_ref, f_ref, i_ref, g_ref = (p_ref.at[s] for s in range(4))
        hg, b, t = pl.program_id(0), pl.program_id(1), pl.program_id(2)

        @pl.when(t == 0)
        def _():
            dst_scr[...] = jnp.zeros_like(dst_scr)

        @pl.when((b == 0) & (t == 0))
        def _():
            dlb_ref[...] = jnp.zeros_like(dlb_ref)

        @pl.when((hg == 0) & (b == 0) & (t == 0))
        def _():
            dgn_ref[...] = jnp.zeros_like(dgn_ref)

        lb = lb_ref[0:1, :]
        sig, fg = _hgrn_gates(f_ref, lb, a_scr, k_scr, tr)
        gnv = gn_ref[...]
        rr = lax.broadcasted_iota(jnp.int32, (CHUNK, CHUNK), 0)
        cc = lax.broadcasted_iota(jnp.int32, (CHUNK, CHUNK), 1)
        causal = cc <= rr
        rowi = lax.broadcasted_iota(jnp.int32, (CHUNK, HEAD_DIM), 0)

        def chunk(it, carry):
            n = ncl - 1 - it
            rows = _chunk_rows(n)
            for hd0 in range(0, hpg, PHASE_HEADS):
                heads(n, rows, range(hd0, min(hpg, hd0 + PHASE_HEADS)))
            return carry

        def heads(n, rows, ids):
            lanes = [slice(hd * HEAD_DIM, (hd + 1) * HEAD_DIM) for hd in ids]
            hs = []
            for hd, ls in zip(ids, lanes):
                h = {}
                ah, kh = a_scr[rows, ls], k_scr[rows, ls]
                qp = q_ref[rows, ls]
                sq = _sigmoid(qp)
                qh = qp * sq
                h["dsilu_q"] = sq * (1.0 + qp * (1.0 - sq))
                h["vb"] = i_ref[rows, ls].astype(BF16)
                aref, alast = ah[CHUNK // 2 - 1:CHUNK // 2, :], ah[CHUNK - 1:CHUNK, :]
                h["e1"], h["e2"] = jnp.exp(ah - aref), jnp.exp(aref - ah)
                h["e3"], h["e4"] = jnp.exp(ah), jnp.exp(alast - ah)
                h["dec"] = jnp.exp(alast)
                h["q_in"], h["k_in"], h["q_out"], h["k_out"] = qh * h["e1"], kh * h["e2"], qh * h["e3"], kh * h["e4"]
                for nm in ("q_in", "k_in", "q_out", "k_out"):
                    h[nm + "_b"] = h[nm].astype(BF16)
                o = o_ref[rows, ls]
                rstd = lax.rsqrt(jnp.mean(o * o, axis=-1, keepdims=True) + EPS)
                ohat = o * rstd
                gg = g_ref[rows, ls]
                sg = _sigmoid(gg)
                dyv = dy_ref[rows, ls]
                d_on = dyv * (gg * sg)
                dp_ref[3, rows, ls] = (dyv * (ohat * gnv) * (sg * (1.0 + gg * (1.0 - sg)))).astype(BF16)
                h["dgn"] = jnp.sum(d_on * ohat, axis=0, keepdims=True)
                dohat = d_on * gnv
                do = rstd * (dohat - ohat * jnp.mean(dohat * ohat, axis=-1, keepdims=True))
                h["do_b"] = do.astype(BF16)
                h["st_prev"] = st_ref[n, hd]
                h["dst"] = dst_scr[hd]
                hs.append(h)
            for h in hs:
                dst_b = h["dst"].astype(BF16)
                h["scores"] = _dot_nt(h["q_in_b"], h["k_in_b"])
                h["dscores"] = _dot_nt(h["do_b"], h["vb"])
                h["dv_inter"] = _dot_nt(h["k_out_b"], dst_b)
                h["dq_out"] = _dot(h["do_b"], h["st_prev"].astype(BF16))
                h["dk_out"] = _dot(h["vb"], dst_b)
                h["dst_mm"] = _dot_tn(h["do_b"], h["q_out_b"])
            for h in hs:
                scores = jnp.where(causal, h["scores"], 0.0).astype(BF16)
                dscores = jnp.where(causal, h["dscores"], 0.0).astype(BF16)
                h["dv"] = _dot_tn(scores, h["do_b"]) + h["dv_inter"]
                h["dq_in"] = _dot(dscores, h["k_in_b"])
                h["dk_in"] = _dot_tn(dscores, h["q_in_b"])
            dgn = hs[0]["dgn"]
            for h in hs[1:]:
                dgn = dgn + h["dgn"]
            dgn_ref[...] += dgn
            for hd, h, ls in zip(ids, hs, lanes):
                ddec = jnp.sum(h["dst"] * h["st_prev"], axis=0, keepdims=True)
                dst_scr[hd] = h["dst"] * h["dec"] + h["dst_mm"]
                dp_ref[2, rows, ls] = h["dv"].astype(BF16)
                dq = h["dq_in"] * h["e1"] + h["dq_out"] * h["e3"]
                dp_ref[0, rows, ls] = (dq * h["dsilu_q"]).astype(BF16)
                dk_scr[rows, ls] = h["dk_in"] * h["e2"] + h["dk_out"] * h["e4"]
                t_in = h["dq_in"] * h["q_in"] - h["dk_in"] * h["k_in"]
                t_out = h["dk_out"] * h["k_out"]
                da = t_in + h["dq_out"] * h["q_out"] - t_out
                da_ref_row = -jnp.sum(t_in, axis=0, keepdims=True)
                da_last_row = jnp.sum(t_out, axis=0, keepdims=True) + ddec * h["dec"]
                da = da + jnp.where(rowi == CHUNK // 2 - 1, da_ref_row, 0.0) \
                        + jnp.where(rowi == CHUNK - 1, da_last_row, 0.0)
                da_scr[rows, ls] = da

        if ncl <= 2:
            for it in range(ncl):
                chunk(it, 0)
        else:
            lax.fori_loop(0, ncl, chunk, 0)
        g = min(CUM_ROWS, tr)
        tri = _tri_mask(g, reverse=True)
        for rg in range(tr // g):
            rs = slice(rg * g, (rg + 1) * g)
            dlogf = _tri_apply(tri, da_scr[rs, :])
            df = dlogf / fg[rs, :] - dk_scr[rs, :]
            sgr = sig[rs, :]
            dp_ref[1, rs, :] = (df * (1.0 - lb) * (sgr * (1.0 - sgr))).astype(BF16)
            dlb_ref[...] += jnp.sum(df * (1.0 - sgr), axis=0, keepdims=True) * lb_ref[1:2, :]

    blk = pl.BlockSpec((tr, hc), lambda hg, b, t: (b * nt + (nt - 1 - t), hg))
    return _pc(body, name="hgrn_bwd",
               out_shape=[jax.ShapeDtypeStruct((4, m, di), BF16), jax.ShapeDtypeStruct((1, di), F32),
                          jax.ShapeDtypeStruct((1, HEAD_DIM), F32)],
               grid=(nhg, nb, nt),
               in_specs=[pl.BlockSpec((4, tr, hc), lambda hg, b, t: (0, b * nt + (nt - 1 - t), hg)), blk, blk,
                         pl.BlockSpec((ncl, hpg, HEAD_DIM, HEAD_DIM),
                                      lambda hg, b, t: (b * nt + (nt - 1 - t), hg, 0, 0)),
                         pl.BlockSpec((2, hc), lambda hg, b, t: (0, hg)),
                         pl.BlockSpec((1, HEAD_DIM), lambda hg, b, t: (0, 0))],
               out_specs=[pl.BlockSpec((4, tr, hc), lambda hg, b, t: (0, b * nt + (nt - 1 - t), hg)),
                          pl.BlockSpec((1, hc), lambda hg, b, t: (0, hg)),
                          pl.BlockSpec((1, HEAD_DIM), lambda hg, b, t: (0, 0))],
               scratch=[pltpu.VMEM((hpg, HEAD_DIM, HEAD_DIM), F32)] + [pltpu.VMEM((tr, hc), F32)] * 4,
               sem=("arbitrary", "arbitrary", "arbitrary"), comm=comm)(
                   proj, o_all, dybr, states, lbj, gn)


def _adamw(parts, w, m, v, name):
    r, c = w.shape
    tr = _tile(r, 256)
    npart = len(parts)
    c1 = 1.0 - ADAM_B1 ** ADAM_STEP
    c2 = 1.0 - ADAM_B2 ** ADAM_STEP

    def body(*refs):
        p_refs = refs[:npart]
        _adamw_math(p_refs, *refs[npart:], c1, c2)

    blk = pl.BlockSpec((tr, c), lambda i: (i, 0))
    return _pc(body, name=name, out_shape=[jax.ShapeDtypeStruct((r, c), F32)] * 4, grid=(r // tr,),
               in_specs=[blk] * (npart + 3), out_specs=[blk] * 4, sem=("parallel",))(*parts, w, m, v)


def _adamw_math(p_refs, w_ref, m_ref, v_ref, g_ref, d_ref, nm_ref, nv_ref, c1, c2):
    g = p_refs[0][...].astype(F32)
    for p in p_refs[1:]:
        g = g + p[...].astype(F32)
    nm = ADAM_B1 * m_ref[...] + (1.0 - ADAM_B1) * g
    nv = ADAM_B2 * v_ref[...] + (1.0 - ADAM_B2) * (g * g)
    g_ref[...] = g
    nm_ref[...] = nm
    nv_ref[...] = nv
    d_ref[...] = -ADAM_LR * ((nm / c1) / (jnp.sqrt(nv / c2) + ADAM_EPS) + ADAM_WD * w_ref[...])


def _adamw_blocks(parts, idx, w, m, v, name):
    r, c = w.shape
    tr = _tile(r, 256)
    npart = len(parts)
    c1 = 1.0 - ADAM_B1 ** ADAM_STEP
    c2 = 1.0 - ADAM_B2 ** ADAM_STEP

    def body(idx_ref, *refs):
        _adamw_math(refs[:npart], *refs[npart:], c1, c2)

    def sel(p):
        return pl.BlockSpec((None, tr, c), lambda i, s: (s[p], i, 0))

    blk = pl.BlockSpec((tr, c), lambda i, s: (i, 0))
    gs = pltpu.PrefetchScalarGridSpec(num_scalar_prefetch=1, grid=(r // tr,),
                                      in_specs=[sel(p) for p in range(npart)] + [blk] * 3, out_specs=[blk] * 4)
    return _pc(body, name=name, out_shape=[jax.ShapeDtypeStruct((r, c), F32)] * 4, grid_spec=gs,
               sem=("parallel",))(idx, *parts, w, m, v)


_EARLY = ["a_ln_gain", "a_ln_bias", "a_w_s", "a_b_s", "b_lower_bounds", "b_gn_gain"]


def _pack(arrs):
    flat = jnp.concatenate([a.reshape(-1) for a in arrs])
    rows = -(-flat.shape[0] // 1024) * 8
    return jnp.pad(flat, (0, rows * 128 - flat.shape[0])).reshape(rows, 128)


def _unpack(buf, like):
    flat = buf.reshape(-1)
    out, off = [], 0
    for a in like:
        out.append(flat[off:off + a.size].reshape(a.shape))
        off += a.size
    return out


def kernel(x, c, norm_gain, w_ada, b_ada, a_w_in, a_ln_gain, a_ln_bias, a_w_s, a_b_s, a_w_out, b_w_in, b_lower_bounds, b_gn_gain, b_w_out, final_gain, loss_target, m_norm_gain, m_w_ada, m_b_ada, m_a_w_in, m_a_ln_gain, m_a_ln_bias, m_a_w_s, m_a_b_s, m_a_w_out, m_b_w_in, m_b_lower_bounds, m_b_gn_gain, m_b_w_out, m_final_gain, v_norm_gain, v_w_ada, v_b_ada, v_a_w_in, v_a_ln_gain, v_a_ln_bias, v_a_w_s, v_a_b_s, v_a_w_out, v_b_w_in, v_b_lower_bounds, v_b_gn_gain, v_b_w_out, v_final_gain):
    w = dict(norm_gain=norm_gain, w_ada=w_ada, b_ada=b_ada, a_w_in=a_w_in, a_ln_gain=a_ln_gain,
             a_ln_bias=a_ln_bias, a_w_s=a_w_s, a_b_s=a_b_s, a_w_out=a_w_out, b_w_in=b_w_in,
             b_lower_bounds=b_lower_bounds, b_gn_gain=b_gn_gain, b_w_out=b_w_out, final_gain=final_gain)
    mo = dict(norm_gain=m_norm_gain, w_ada=m_w_ada, b_ada=m_b_ada, a_w_in=m_a_w_in, a_ln_gain=m_a_ln_gain,
              a_ln_bias=m_a_ln_bias, a_w_s=m_a_w_s, a_b_s=m_a_b_s, a_w_out=m_a_w_out, b_w_in=m_b_w_in,
              b_lower_bounds=m_b_lower_bounds, b_gn_gain=m_b_gn_gain, b_w_out=m_b_w_out, final_gain=m_final_gain)
    vo = dict(norm_gain=v_norm_gain, w_ada=v_w_ada, b_ada=v_b_ada, a_w_in=v_a_w_in, a_ln_gain=v_a_ln_gain,
              a_ln_bias=v_a_ln_bias, a_w_s=v_a_w_s, a_b_s=v_a_b_s, a_w_out=v_a_w_out, b_w_in=v_b_w_in,
              b_lower_bounds=v_b_lower_bounds, b_gn_gain=v_b_gn_gain, b_w_out=v_b_w_out, final_gain=v_final_gain)

    nb, t_seq, d = x.shape
    m = nb * t_seq
    ncol_ada = w_ada.shape[2]
    xi, yi, ci = lax.axis_index("x"), lax.axis_index("y"), lax.axis_index("c")
    me = 4 * xi + 2 * yi + ci

    c_g, wa_in_g = _all_gather([c, a_w_in[0].astype(BF16)], "gather_c_wa")

    c_all = c_g.reshape(NDEV * nb, d)
    b_cols = lax.dynamic_slice(b_ada, (0, me * ncol_ada), (2, ncol_ada)).reshape(2, 1, ncol_ada)
    mod_part, lbj = _ada_fwd(c_all, w_ada, b_cols, b_lower_bounds)
    mod_all = _all_gather([mod_part], "gather_mod")[0]
    mod_mine = lax.dynamic_slice_in_dim(mod_all, me * nb, nb, axis=2)
    mod_mine = mod_mine.transpose(1, 2, 0, 3).reshape(2, nb, 3, d)
    mod0, mod1 = mod_mine[0], mod_mine[1]

    di = a_w_out.shape[1] * NDEV

    xf = x.reshape(m, d)
    tgt = loss_target.reshape(m, d)
    ng0, ng1 = norm_gain[0:1], norm_gain[1:2]
    ncb = b_w_in.shape[2]
    wb_lo, wb_hi = b_w_in[0][:, :ncb // 2].astype(BF16), b_w_in[0][:, ncb // 2:].astype(BF16)
    h0, h0_t = _prenorm(xf, ng0, mod0, t_seq, "prenorm_a")
    proj_a, half = _mm_in(h0, [wa_in_g], 1, "in_proj_a", comm=_gather_first([a_w_out[0].astype(BF16), wb_lo]))
    bs_t = jnp.pad(a_b_s[0].T, ((0, 0), (0, 128 - SG_GROUPS)))
    ybr_a, (wa_out_g, wb_lo_g, wb_hi_half) = _a_mid_fwd(
        proj_a, a_ln_gain, a_ln_bias, a_w_s[0], bs_t, t_seq, comm=_join(_gather_second(half), _gather_first([wb_hi])))
    wa_out = wa_out_g.reshape(di, d)
    (yout_a, x1), (wb_hi_g, wb_out_half) = _out_proj(
        ybr_a, wa_out, xf, mod0, t_seq, "out_proj_a",
        comm=_join(_gather_second([wb_hi_half]), _gather_first([b_w_out[0].astype(BF16)])))
    wb_in_g = [wb_lo_g, wb_hi_g]
    h1, h1_t = _prenorm(x1, ng1, mod1, t_seq, "prenorm_b")
    proj_b, (wb_out_g,) = _mm_in(h1, wb_in_g, 4, "in_proj_b", comm=_gather_second([wb_out_half]))
    wb_out = wb_out_g.reshape(di, d)
    o_b, ybr_b, states = _hgrn_fwd(proj_b, lbj, b_gn_gain, nb, t_seq)
    yout_b, dx2, loss_part, d_final_gain = _out_proj_loss(ybr_b, wb_out, x1, mod1, final_gain.reshape(1, d), tgt, t_seq)

    rows_out = a_w_out.shape[1]
    dy_b, dgate1, dybr_b = _gate_dybr(dx2, yout_b, mod1, wb_out, t_seq, "dybr_b")
    rs_wb_out = _ReduceScatter(_mm_dw_out(ybr_b, dy_b, "dw_out_b").reshape(NDEV, rows_out, d), "b_w_out")
    (dproj_b, d_lb, d_gn), got = _hgrn_bwd(proj_b, o_b, dybr_b, states, lbj, b_gn_gain, nb, t_seq,
                                           comm=rs_wb_out.swap_core())
    rs_wb_out.after_core(got[0])
    dh1, got = _mm_din(dproj_b, wb_in_g, 4, "dh_b", comm=rs_wb_out.swap_chips())
    rs_wb_out.after_chips(got[0])
    dx1, dss1, dgain1 = _prenorm_bwd(dh1, x1, ng1, mod1, dx2, t_seq, "prenorm_bwd_b")
    rs_wb_in = _ReduceScatter(_mm_dw_in(h1_t, dproj_b, ncb, 4, "dw_in_b"), "b_w_in")

    dy_a, dgate0, dybr_a = _gate_dybr(dx1, yout_a, mod0, wa_out, t_seq, "dybr_a")
    g_wa_out, got = _mm_dw_out(ybr_a, dy_a, "dw_out_a", comm=rs_wb_in.swap_core())
    rs_wb_in.after_core(got[0])
    rs_wa_out = _ReduceScatter(g_wa_out.reshape(NDEV, rows_out, d), "a_w_out")
    (dproj_a, d_lng, d_lnb, d_ws, d_bs_t), got = _a_mid_bwd(
        proj_a, dybr_a, a_ln_gain, a_ln_bias, a_w_s[0], bs_t, t_seq,
        comm=_join(rs_wb_in.swap_chips(), rs_wa_out.swap_core()))
    rs_wb_in.after_chips(got[0])
    rs_wa_out.after_core(got[1])
    part = dict(a_ln_gain=d_lng, a_ln_bias=d_lnb, a_w_s=d_ws[None], a_b_s=d_bs_t[:, :SG_GROUPS].T[None],
                b_lower_bounds=jnp.concatenate([-d_lb, d_lb], axis=0), b_gn_gain=d_gn)
    early_pack = _pack([part[k].reshape(w[k].shape) for k in _EARLY])
    g_wa_in, got = _mm_dw_in(h0_t, dproj_a, wa_in_g.shape[2], 1, "dw_in_a",
                             comm=_join(rs_wa_out.swap_chips(), _gather_first([early_pack])))
    rs_wa_out.after_chips(got[0])
    rs_wa_in = _ReduceScatter(g_wa_in, "a_w_in")
    n_tiles = m // _din_tile(m)
    assert n_tiles >= 2
    first_tiles = max(1, (3 * n_tiles) // 8)
    dh0, got2 = _mm_din(dproj_a, [wa_in_g], 1, "dh_a_first", tiles=(0, first_tiles),
                        comm=_join(rs_wa_in.swap_core(), _gather_second([got[1]])))
    rs_wa_in.after_core(got2[0])
    early_all = got2[1]
    dh0, got = _mm_din(dproj_a, [wa_in_g], 1, "dh_a_rest", comm=rs_wa_in.swap_chips(),
                       tiles=(first_tiles, n_tiles - first_tiles), prev=dh0)
    rs_wa_in.after_chips(got[0])
    dx0, dss0, dgain0 = _prenorm_bwd(dh0, xf, ng0, mod0, dx1, t_seq, "prenorm_bwd_a")
    grad_x = dx0.reshape(nb, t_seq, d)

    dmod = jnp.stack([jnp.concatenate([dss0, dgate0], axis=1), jnp.concatenate([dss1, dgate1], axis=1)])
    late_like = [norm_gain, final_gain, loss_part.reshape(1)]
    late_pack = _pack([jnp.concatenate([dgain0, dgain1], axis=0), d_final_gain[0], loss_part.reshape(1)])
    dmod_all, late_all = _all_gather([dmod.reshape(2, nb, 3 * d), late_pack], "gather_tail")
    dmod_all = dmod_all.transpose(1, 0, 2, 3).reshape(2, NDEV * nb, 3 * d)
    dmod_cols = lax.dynamic_slice_in_dim(dmod_all, me * ncol_ada, ncol_ada, axis=2)
    g_w_ada, g_b_ada = _ada_bwd(c_all, dmod_cols, dmod_all)

    res = {}
    early_like = [w[k] for k in _EARLY]
    dev_order = jnp.arange(NDEV, dtype=jnp.int32)
    sm = _adamw_blocks([early_all] * NDEV, dev_order, _pack(early_like), _pack([mo[k] for k in _EARLY]),
                       _pack([vo[k] for k in _EARLY]), "adamw_small_early")
    sm = [dict(zip(_EARLY, _unpack(buf, early_like))) for buf in sm]
    for k in _EARLY:
        res[k] = tuple(s[k] for s in sm)
    zero = jnp.zeros((1,), F32)
    sm = _adamw_blocks([late_all] * NDEV, dev_order, _pack([norm_gain, final_gain, zero]),
                       _pack([mo["norm_gain"], mo["final_gain"], zero]),
                       _pack([vo["norm_gain"], vo["final_gain"], zero]), "adamw_small_late")
    sm = [_unpack(buf, late_like) for buf in sm]
    res["norm_gain"] = tuple(s[0] for s in sm)
    res["final_gain"] = tuple(s[1] for s in sm)
    loss = sm[0][2][0]
    rb = _adamw([g_b_ada], b_ada, mo["b_ada"], vo["b_ada"], "adamw_b_ada")
    res["b_ada"] = tuple(rb)
    sh = w_ada.shape
    ra = _adamw([g_w_ada.reshape(sh[0] * sh[1], sh[2])], w_ada.reshape(sh[0] * sh[1], sh[2]),
                mo["w_ada"].reshape(sh[0] * sh[1], sh[2]), vo["w_ada"].reshape(sh[0] * sh[1], sh[2]), "adamw_w_ada")
    res["w_ada"] = tuple(z.reshape(sh) for z in ra)

    for k, rs in (("b_w_out", rs_wb_out), ("b_w_in", rs_wb_in), ("a_w_out", rs_wa_out), ("a_w_in", rs_wa_in)):
        res[k] = tuple(z[None] for z in _adamw_blocks(rs.parts, rs.idx, w[k][0], mo[k][0], vo[k][0], "adamw_" + k))

    order = ["norm_gain", "w_ada", "b_ada", "a_w_in", "a_ln_gain", "a_ln_bias", "a_w_s", "a_b_s", "a_w_out",
             "b_w_in", "b_lower_bounds", "b_gn_gain", "b_w_out", "final_gain"]
    return (loss, grad_x, *[res[k][0] for k in order], *[res[k][1] for k in order],
            *[res[k][2] for k in order], *[res[k][3] for k in order])
```

```python
import functools
import math

import jax
import jax.numpy as jnp
from jax import lax
from jax.experimental import pallas as pl
from jax.experimental.pallas import tpu as pltpu

F32 = jnp.float32
BF16 = jnp.bfloat16
MESH = pl.DeviceIdType.MESH
NDEV = 8
EPS = 1e-6
CHUNK = 64
SG_BLOCK = 128
SG_GROUPS = 8
HEAD_DIM = 128
CUM_ROWS = 256
PHASE_HEADS = 8
ADAM_LR, ADAM_B1, ADAM_B2, ADAM_EPS, ADAM_WD, ADAM_STEP = 0.001, 0.9, 0.999, 1e-08, 0.01, 10
VMEM_LIMIT = 56 * 1024 * 1024
ANY = pl.BlockSpec(memory_space=pl.ANY)


class _Hosted:
    def __init__(self, arrays, out_shapes, nsem, start, finish, aliases=None):
        self.arrays, self.out_shapes, self.nsem = list(arrays), list(out_shapes), nsem
        self.start, self.finish = start, finish
        self.aliases = dict(aliases or {})


def _join(*comms):
    arrays, outs, aliases, offs, nsem = [], [], {}, [], 0
    for cm in comms:
        offs.append((len(arrays), len(outs), nsem))
        for i, o in cm.aliases.items():
            aliases[len(arrays) + i] = len(outs) + o
        arrays += cm.arrays
        outs += cm.out_shapes
        nsem += cm.nsem

    def run(which):
        def f(ins, outs_, ss, rs, base):
            for cm, (ia, io, isem) in zip(comms, offs):
                getattr(cm, which)(ins[ia:ia + len(cm.arrays)], outs_[io:io + len(cm.out_shapes)], ss, rs, base + isem)
        return f

    return _Hosted(arrays, outs, nsem, run("start"), run("finish"), aliases)


def _pc(body, *, name, out_shape, grid=None, in_specs=None, out_specs=None, scratch=(), sem=None,
        grid_spec=None, comm=None, aliases=None):
    cp = dict(vmem_limit_bytes=VMEM_LIMIT)
    aliases = dict(aliases or {})
    if comm is None:
        if sem is not None:
            cp["dimension_semantics"] = sem
        kw = {"input_output_aliases": aliases}
        if grid_spec is not None:
            kw["grid_spec"] = grid_spec
        else:
            if grid is not None:
                kw["grid"] = grid
            if in_specs is not None:
                kw["in_specs"] = in_specs
            if out_specs is not None:
                kw["out_specs"] = out_specs
            kw["scratch_shapes"] = list(scratch)
        return pl.pallas_call(functools.partial(body), name=name, out_shape=out_shape,
                              compiler_params=pltpu.CompilerParams(**cp), **kw)

    single = not isinstance(out_shape, (list, tuple))
    outs_list = [out_shape] if single else list(out_shape)
    ospecs = [out_specs] if single else list(out_specs)
    n_in, n_out, n_ci, n_co, n_scr = len(in_specs), len(outs_list), len(comm.arrays), len(comm.out_shapes), len(scratch)
    cp["dimension_semantics"] = ("arbitrary",) * len(grid)

    def hosted(*refs):
        cin, hin = refs[:n_in], refs[n_in:n_in + n_ci]
        cout = refs[n_in + n_ci:n_in + n_ci + n_out]
        hout = refs[n_in + n_ci + n_out:n_in + n_ci + n_out + n_co]
        scr = refs[n_in + n_ci + n_out + n_co:n_in + n_ci + n_out + n_co + n_scr]
        ssem, rsem = refs[-2], refs[-1]
        first = functools.reduce(lambda p, q: p & q, [pl.program_id(a) == 0 for a in range(len(grid))])
        last = functools.reduce(lambda p, q: p & q, [pl.program_id(a) == grid[a] - 1 for a in range(len(grid))])

        @pl.when(first)
        def _():
            comm.start(hin, hout, ssem, rsem, 0)

        body(*cin, *cout, *scr)

        @pl.when(last)
        def _():
            comm.finish(hin, hout, ssem, rsem, 0)

    call = pl.pallas_call(
        hosted, name=name, grid=grid, in_specs=list(in_specs) + [ANY] * n_ci, out_specs=ospecs + [ANY] * n_co,
        out_shape=outs_list + comm.out_shapes,
        scratch_shapes=list(scratch) + [pltpu.SemaphoreType.DMA((comm.nsem,)), pltpu.SemaphoreType.DMA((comm.nsem,))],
        input_output_aliases={**aliases, **{n_in + i: n_out + o for i, o in comm.aliases.items()}},
        compiler_params=pltpu.CompilerParams(**cp))

    def run(*args):
        res = call(*args, *comm.arrays)
        comp = res[:n_out]
        return (comp[0] if single else comp), list(res[n_out:])

    return run


def _tile(n, pref):
    return pref if n % pref == 0 else n


def _sigmoid(x):
    return 1.0 / (1.0 + jnp.exp(-x))


def _gelu(x):
    c = math.sqrt(2.0 / math.pi)
    return 0.5 * x * (1.0 + jnp.tanh(c * (x + 0.044715 * (x * x * x))))


def _gelu_and_grad(x):
    c = math.sqrt(2.0 / math.pi)
    x2 = x * x
    t = jnp.tanh(c * (x + 0.044715 * (x2 * x)))
    half = 0.5 * (1.0 + t)
    return x * half, half + (0.5 * x) * (1.0 - t * t) * (c + (3.0 * 0.044715 * c) * x2)


def _dot(a, b):
    return jnp.dot(a, b, preferred_element_type=F32)


def _dot_nt(a, b):
    return lax.dot_general(a, b, (((1,), (1,)), ((), ())), preferred_element_type=F32)


def _dot_tn(a, b):
    return lax.dot_general(a, b, (((0,), (0,)), ((), ())), preferred_element_type=F32)


def _tri_mask(n, reverse):
    r = lax.broadcasted_iota(jnp.int32, (n, n), 0)
    c = lax.broadcasted_iota(jnp.int32, (n, n), 1)
    same = (r // CHUNK) == (c // CHUNK)
    tri = (c >= r) if reverse else (c <= r)
    return jnp.where(same & tri, 1.0, 0.0).astype(BF16)


def _tri_apply(tri, x):
    hi = x.astype(BF16)
    r1 = x - hi.astype(F32)
    mid = r1.astype(BF16)
    lo = (r1 - mid.astype(F32)).astype(BF16)
    return _dot(tri, hi) + (_dot(tri, mid) + _dot(tri, lo))


def _all_gather(arrs, name):
    n = len(arrs)

    def body(*refs):
        ins, outs = refs[:n], refs[n:2 * n]
        send_sems, recv_sems, local_sems = refs[2 * n:]
        x, y, c = lax.axis_index("x"), lax.axis_index("y"), lax.axis_index("c")
        me, sibling = (x, y, c), (x, y, 1 - c)
        near = (x + c - 2 * x * c, y + (1 - c) - 2 * y * (1 - c))
        far = (x + (1 - c) - 2 * x * (1 - c), y + c - 2 * y * c)
        diag = (1 - x, 1 - y)

        def blk(a, p):
            return outs[a].at[4 * p[0] + 2 * p[1] + p[2]]

        def copy(a, k, block, to, src=None):
            return pltpu.make_async_remote_copy(
                src_ref=blk(a, block) if src is None else src, dst_ref=blk(a, block),
                send_sem=send_sems.at[7 * a + k], recv_sem=recv_sems.at[7 * a + k],
                device_id=to, device_id_type=MESH)

        mine = [pltpu.make_async_copy(ins[a], blk(a, me), local_sems.at[a]) for a in range(n)]
        for m in mine:
            m.start()
        sends = []
        for a in range(n):
            sends += [copy(a, 0, me, sibling, src=ins[a]), copy(a, 1, me, (*near, c), src=ins[a]),
                      copy(a, 2, me, (*far, c), src=ins[a])]
        for cp in sends:
            cp.start()
        for a in range(n):
            copy(a, 1, (*near, c), me).wait_recv()
            sends.append(copy(a, 3, (*near, c), (*far, c)))
            sends[-1].start()
        for a in range(n):
            sends.append(copy(a, 4, (*near, c), sibling))
            sends[-1].start()
            copy(a, 2, (*far, c), me).wait_recv()
            sends.append(copy(a, 5, (*far, c), sibling))
            sends[-1].start()
        for a in range(n):
            copy(a, 3, (*diag, c), me).wait_recv()
            sends.append(copy(a, 6, (*diag, c), sibling))
            sends[-1].start()
        for a in range(n):
            copy(a, 0, sibling, me).wait_recv()
            copy(a, 4, (*far, 1 - c), me).wait_recv()
            copy(a, 5, (*near, 1 - c), me).wait_recv()
            copy(a, 6, (*diag, 1 - c), me).wait_recv()
        for cp in sends:
            cp.wait_send()
        for m in mine:
            m.wait()

    out_shape = [jax.ShapeDtypeStruct((NDEV,) + a.shape, a.dtype) for a in arrs]
    return _pc(body, name=name, out_shape=out_shape, in_specs=[ANY] * n, out_specs=[ANY] * n,
               scratch=[pltpu.SemaphoreType.DMA((7 * n,)), pltpu.SemaphoreType.DMA((7 * n,)),
                        pltpu.SemaphoreType.DMA((n,))])(*arrs)


def _gather_first(arrs):
    n = len(arrs)

    def parts(ins, outs, ss, rs, base):
        x, y, c = lax.axis_index("x"), lax.axis_index("y"), lax.axis_index("c")
        me, sibling = (x, y, c), (x, y, 1 - c)
        chips = [(1 - x, y), (x, 1 - y), (1 - x, 1 - y)]

        def blk(a, p):
            return outs[a].at[4 * p[0] + 2 * p[1] + p[2]]

        def copy(a, k, block, to):
            return pltpu.make_async_remote_copy(
                src_ref=ins[a], dst_ref=blk(a, block), send_sem=ss.at[base + 4 * a + k],
                recv_sem=rs.at[base + 4 * a + k], device_id=to, device_id_type=MESH)

        local = [pltpu.make_async_copy(ins[a], blk(a, me), ss.at[base + 4 * n + a]) for a in range(n)]
        sends, recvs = [], []
        for a in range(n):
            sends.append(copy(a, 0, me, sibling))
            recvs.append(copy(a, 0, sibling, me))
            for j, chip in enumerate(chips):
                sends.append(copy(a, 1 + j, me, (*chip, c)))
                recvs.append(copy(a, 1 + j, (*chip, c), me))
        return local, sends, recvs

    def start(ins, outs, ss, rs, base):
        local, sends, _ = parts(ins, outs, ss, rs, base)
        for cp in local + sends:
            cp.start()

    def finish(ins, outs, ss, rs, base):
        local, sends, recvs = parts(ins, outs, ss, rs, base)
        for cp in recvs:
            cp.wait_recv()
        for cp in sends:
            cp.wait_send()
        for cp in local:
            cp.wait()

    return _Hosted(arrs, [jax.ShapeDtypeStruct((NDEV,) + a.shape, a.dtype) for a in arrs], 5 * n, start, finish)


def _gather_second(bufs):
    n = len(bufs)

    def parts(ins, outs, ss, rs, base):
        x, y, c = lax.axis_index("x"), lax.axis_index("y"), lax.axis_index("c")
        sibling = (x, y, 1 - c)
        chips = [(1 - x, y), (x, 1 - y), (1 - x, 1 - y)]
        sends, recvs = [], []
        for a in range(n):
            for j, chip in enumerate(chips):
                mine = 4 * chip[0] + 2 * chip[1] + c
                theirs = 4 * chip[0] + 2 * chip[1] + (1 - c)
                sends.append(pltpu.make_async_remote_copy(
                    src_ref=ins[a].at[mine], dst_ref=outs[a].at[mine], send_sem=ss.at[base + 3 * a + j],
                    recv_sem=rs.at[base + 3 * a + j], device_id=sibling, device_id_type=MESH))
                recvs.append(pltpu.make_async_remote_copy(
                    src_ref=ins[a].at[theirs], dst_ref=outs[a].at[theirs], send_sem=ss.at[base + 3 * a + j],
                    recv_sem=rs.at[base + 3 * a + j], device_id=sibling, device_id_type=MESH))
        return sends, recvs

    def start(ins, outs, ss, rs, base):
        for cp in parts(ins, outs, ss, rs, base)[0]:
            cp.start()

    def finish(ins, outs, ss, rs, base):
        sends, recvs = parts(ins, outs, ss, rs, base)
        for cp in recvs:
            cp.wait_recv()
        for cp in sends:
            cp.wait_send()

    return _Hosted(bufs, [jax.ShapeDtypeStruct(b.shape, b.dtype) for b in bufs], 3 * n, start, finish,
                   aliases={a: a for a in range(n)})


def _swap(src, nblk, ids_fn, partner_fn):
    def copies(ins, outs, ss, rs, base):
        x, y, c = lax.axis_index("x"), lax.axis_index("y"), lax.axis_index("c")
        ids = ids_fn(x, y, c)
        partner = partner_fn(x, y, c)
        return [pltpu.make_async_remote_copy(
            src_ref=ins[0].at[ids[k]], dst_ref=outs[0].at[k], send_sem=ss.at[base + k], recv_sem=rs.at[base + k],
            device_id=partner, device_id_type=MESH) for k in range(nblk)]

    def start(ins, outs, ss, rs, base):
        for cp in copies(ins, outs, ss, rs, base):
            cp.start()

    def finish(ins, outs, ss, rs, base):
        for cp in copies(ins, outs, ss, rs, base):
            cp.wait()

    return _Hosted([src], [jax.ShapeDtypeStruct((nblk,) + src.shape[1:], src.dtype)], nblk, start, finish)


def _swap_chips(send):
    def copies(ins, outs, ss, rs, base):
        x, y, c = lax.axis_index("x"), lax.axis_index("y"), lax.axis_index("c")
        chips = [(1 - x, y), (x, 1 - y), (1 - x, 1 - y)]
        return [pltpu.make_async_remote_copy(
            src_ref=ins[0].at[j], dst_ref=outs[0].at[j], send_sem=ss.at[base + j], recv_sem=rs.at[base + j],
            device_id=(*chip, c), device_id_type=MESH) for j, chip in enumerate(chips)]

    def start(ins, outs, ss, rs, base):
        for cp in copies(ins, outs, ss, rs, base):
            cp.start()

    def finish(ins, outs, ss, rs, base):
        for cp in copies(ins, outs, ss, rs, base):
            cp.wait()

    return _Hosted([send], [jax.ShapeDtypeStruct(send.shape, send.dtype)], 3, start, finish)


def _add_send(a, b, idx, ns, name):
    _, r, c = a.shape
    tr = _tile(r, 256)

    def body(idx_ref, a_ref, b_ref, send_ref):
        send_ref[...] = (a_ref[...] + b_ref[...]).astype(BF16)

    def sel(off):
        return pl.BlockSpec((None, tr, c), lambda k, i, s: (s[off + k], i, 0))

    gs = pltpu.PrefetchScalarGridSpec(num_scalar_prefetch=1, grid=(ns, r // tr), in_specs=[sel(0), sel(ns)],
                                      out_specs=pl.BlockSpec((None, tr, c), lambda k, i, s: (k, i, 0)))
    return _pc(body, name=name, grid_spec=gs, sem=("arbitrary", "arbitrary"),
               out_shape=jax.ShapeDtypeStruct((ns, r, c), BF16))(idx, a, b)


class _ReduceScatter:
    def __init__(self, g, tag):
        self.g, self.tag = g, tag

    def swap_core(self):
        return _swap(self.g, 4, lambda x, y, c: [1 - c, 3 - c, 5 - c, 7 - c], lambda x, y, c: (x, y, 1 - c))

    def after_core(self, recv):
        x, y, c = lax.axis_index("x"), lax.axis_index("y"), lax.axis_index("c")
        chips = [(1 - x, y), (x, 1 - y), (1 - x, 1 - y)]
        idx = jnp.stack([4 * p + 2 * q + c for p, q in chips] + [2 * p + q for p, q in chips]).astype(jnp.int32)
        self.send = _add_send(self.g, recv, idx, 3, "rs_add_" + self.tag)
        self.recv_core = recv
        zero = jnp.zeros((), jnp.int32)
        self.idx = jnp.stack([4 * x + 2 * y + c, 2 * x + y, zero, zero + 1, zero + 2]).astype(jnp.int32)

    def swap_chips(self):
        return _swap_chips(self.send)

    def after_chips(self, recv):
        self.parts = [self.g, self.recv_core, recv, recv, recv]


def _ada_fwd(c_all, w_ada, b_cols, b_lb):
    nl, d, ncol = w_ada.shape
    nseq = c_all.shape[0]
    di = b_lb.shape[1]

    def body(c_ref, w_ref, b_ref, lb_ref, mod_ref, lbj_ref):
        cv = c_ref[...]
        cact = (cv * _sigmoid(cv)).astype(BF16)
        for l in range(nl):
            mod_ref[l] = _dot(cact, w_ref[l].astype(BF16)) + b_ref[l]
        b0, b1 = lb_ref[0:1, :], lb_ref[1:2, :]
        mx = jnp.maximum(b0, b1)
        e0, e1 = jnp.exp(b0 - mx), jnp.exp(b1 - mx)
        s = e0 + e1
        p0, p1 = e0 / s, e1 / s
        lbj_ref[0:1, :] = (p0 + p1) - p0
        lbj_ref[1:2, :] = p0 * p1

    return _pc(body, name="ada_fwd",
               out_shape=[jax.ShapeDtypeStruct((nl, nseq, ncol), F32), jax.ShapeDtypeStruct((2, di), F32)]
               )(c_all, w_ada, b_cols, b_lb)


def _ada_bwd(c_all, dmod_cols, dmod_full):
    nl, nseq, ncol = dmod_cols.shape
    d = c_all.shape[1]
    d3 = dmod_full.shape[2]

    def body(c_ref, dc_ref, df_ref, gw_ref, gb_ref):
        cv = c_ref[...]
        cact = (cv * _sigmoid(cv)).astype(BF16)
        for l in range(nl):
            gw_ref[l] = _dot_tn(cact, dc_ref[l].astype(BF16))
            gb_ref[l:l + 1, :] = jnp.sum(df_ref[l], axis=0, keepdims=True)

    return _pc(body, name="ada_bwd",
               out_shape=[jax.ShapeDtypeStruct((nl, d, ncol), F32), jax.ShapeDtypeStruct((nl, d3), F32)]
               )(c_all, dmod_cols, dmod_full)


def _prenorm(x, gain, mod, t_seq, name):
    m, d = x.shape
    tm = _tile(t_seq, 1024)
    per = t_seq // tm

    def body(x_ref, g_ref, mod_ref, h_ref, ht_ref):
        xv = x_ref[...]
        rstd = lax.rsqrt(jnp.mean(xv * xv, axis=-1, keepdims=True) + EPS)
        r = xv * rstd * g_ref[...]
        h = r * (1.0 + mod_ref[0, 1:2, :]) + mod_ref[0, 0:1, :]
        h_ref[...] = h.astype(BF16)
        ht_ref[...] = h.T.astype(BF16)

    return _pc(body, name=name, out_shape=[jax.ShapeDtypeStruct((m, d), BF16), jax.ShapeDtypeStruct((d, m), BF16)],
               grid=(m // tm,),
               in_specs=[pl.BlockSpec((tm, d), lambda i: (i, 0)), pl.BlockSpec((1, d), lambda i: (0, 0)),
                         pl.BlockSpec((1, 3, d), lambda i: (i // per, 0, 0))],
               out_specs=[pl.BlockSpec((tm, d), lambda i: (i, 0)), pl.BlockSpec((d, tm), lambda i: (0, i))],
               sem=("parallel",))(x, gain, mod)


def _prenorm_bwd(dh, x, gain, mod, dxn, t_seq, name):
    m, d = x.shape
    nb = m // t_seq
    tm = _tile(t_seq, 1024)
    per = t_seq // tm

    def body(dh_ref, x_ref, g_ref, mod_ref, dxn_ref, dx_ref, dss_ref, dg_ref):
        i = pl.program_id(0)
        xv, dhv, g = x_ref[...], dh_ref[...], g_ref[...]
        rstd = lax.rsqrt(jnp.mean(xv * xv, axis=-1, keepdims=True) + EPS)
        xhat = xv * rstd
        dr = dhv * (1.0 + mod_ref[0, 1:2, :])
        dxhat = dr * g
        dx_ref[...] = dxn_ref[...] + rstd * (dxhat - xhat * jnp.mean(dxhat * xhat, axis=-1, keepdims=True))

        @pl.when(i % per == 0)
        def _():
            dss_ref[...] = jnp.zeros_like(dss_ref)

        @pl.when(i == 0)
        def _():
            dg_ref[...] = jnp.zeros_like(dg_ref)

        dss_ref[0, 0:1, :] += jnp.sum(dhv, axis=0, keepdims=True)
        dss_ref[0, 1:2, :] += jnp.sum(dhv * (xhat * g), axis=0, keepdims=True)
        dg_ref[...] += jnp.sum(dr * xhat, axis=0, keepdims=True)

    row = pl.BlockSpec((tm, d), lambda i: (i, 0))
    return _pc(body, name=name,
               out_shape=[jax.ShapeDtypeStruct((m, d), F32), jax.ShapeDtypeStruct((nb, 2, d), F32),
                          jax.ShapeDtypeStruct((1, d), F32)],
               grid=(m // tm,),
               in_specs=[row, row, pl.BlockSpec((1, d), lambda i: (0, 0)),
                         pl.BlockSpec((1, 3, d), lambda i: (i // per, 0, 0)), row],
               out_specs=[row, pl.BlockSpec((1, 2, d), lambda i: (i // per, 0, 0)),
                          pl.BlockSpec((1, d), lambda i: (0, 0))],
               sem=("arbitrary",))(dh, x, gain, mod, dxn)


def _mm_in(h, ws, sections, name, comm=None):
    m, k = h.shape
    nw = len(ws)
    widths = [w.shape[2] for w in ws]
    offs = [sum(widths[:a]) for a in range(nw)]
    nc = sum(widths)
    per = NDEV // sections if sections > 1 else NDEV
    tm = _din_tile(m)
    assert per % 2 == 0

    def body(*refs):
        hv = refs[0][...]
        o_ref = refs[1 + nw]
        for b in range(2):
            for a in range(nw):
                lo = b * nc + offs[a]
                o_ref[:, lo:lo + widths[a]] = _dot(hv, refs[1 + a][b])

    w_specs = [pl.BlockSpec((2, k, wd), lambda j, i: (j, 0, 0)) for wd in widths]
    if sections > 1:
        out_shape = jax.ShapeDtypeStruct((sections, m, per * nc), F32)
        out_spec = pl.BlockSpec((None, tm, 2 * nc), lambda j, i: ((2 * j) // per, i, ((2 * j) % per) // 2))
    else:
        out_shape = jax.ShapeDtypeStruct((m, NDEV * nc), F32)
        out_spec = pl.BlockSpec((tm, 2 * nc), lambda j, i: (i, j))
    return _pc(body, name=name, out_shape=out_shape, grid=(NDEV // 2, m // tm),
               in_specs=[pl.BlockSpec((tm, k), lambda j, i: (i, 0))] + w_specs,
               out_specs=out_spec, sem=("parallel", "parallel"), comm=comm)(h, *ws)


def _din_tile(m):
    return 1024 if m % 1024 == 0 and m >= 2048 else _tile(m, 512)


def _mm_din(dproj, ws, sections, name, comm=None, tiles=None, prev=None):
    nw, k = len(ws), ws[0].shape[1]
    widths = [w.shape[2] for w in ws]
    offs = [sum(widths[:a]) for a in range(nw)]
    nc = sum(widths)
    m = dproj.shape[-2]
    tm = _din_tile(m)
    t0, nt = tiles if tiles is not None else (0, m // tm)
    per = NDEV // sections if sections > 1 else NDEV
    assert per % 2 == 0

    def body(*refs):
        d_ref, o_ref = refs[0], refs[-1]
        j = pl.program_id(1)
        acc = None
        for b in range(2):
            for a in range(nw):
                lo = b * nc + offs[a]
                term = _dot_nt(d_ref[:, lo:lo + widths[a]], refs[1 + a][b])
                acc = term if acc is None else acc + term

        @pl.when(j == 0)
        def _():
            o_ref[...] = acc

        @pl.when(j > 0)
        def _():
            o_ref[...] += acc

    if sections > 1:
        dspec = pl.BlockSpec((None, tm, 2 * nc), lambda i, j: ((2 * j) // per, i + t0, ((2 * j) % per) // 2))
    else:
        dspec = pl.BlockSpec((tm, 2 * nc), lambda i, j: (i + t0, j))
    in_specs = [dspec] + [pl.BlockSpec((2, k, wd), lambda i, j: (j, 0, 0)) for wd in widths]
    args = [dproj, *ws]
    if prev is not None:
        in_specs.append(ANY)
        args.append(prev)
    return _pc(body, name=name, out_shape=jax.ShapeDtypeStruct((m, k), F32), grid=(nt, NDEV // 2), in_specs=in_specs,
               out_specs=pl.BlockSpec((tm, k), lambda i, j: (i + t0, 0)), sem=("parallel", "arbitrary"),
               comm=comm, aliases={1 + nw: 0} if prev is not None else None)(*args)


def _mm_dw_in(ht, dproj, nc, sections, name, comm=None):
    k, m = ht.shape
    per = NDEV // sections if sections > 1 else NDEV

    def body(h_ref, d_ref, o_ref):
        o_ref[...] = _dot(h_ref[...], d_ref[...])

    if sections > 1:
        dspec = pl.BlockSpec((None, m, nc), lambda j: (j // per, 0, j % per))
    else:
        dspec = pl.BlockSpec((m, nc), lambda j: (0, j))
    return _pc(body, name=name, out_shape=jax.ShapeDtypeStruct((NDEV, k, nc), F32), grid=(NDEV,),
               in_specs=[pl.BlockSpec((k, m), lambda j: (0, 0)), dspec],
               out_specs=pl.BlockSpec((None, k, nc), lambda j: (j, 0, 0)),
               sem=("parallel",), comm=comm)(ht, dproj)


def _out_proj(ybr, w_out, x, mod, t_seq, name, comm=None):
    m, di = ybr.shape
    d = w_out.shape[1]
    tm = _tile(t_seq, 512)
    per = t_seq // tm

    def body(y_ref, w_ref, x_ref, mod_ref, yo_ref, xn_ref):
        yo = _dot(y_ref[...], w_ref[...])
        yo_ref[...] = yo
        xn_ref[...] = x_ref[...] + mod_ref[0, 2:3, :] * yo

    row = pl.BlockSpec((tm, d), lambda i: (i, 0))
    return _pc(body, name=name,
               out_shape=[jax.ShapeDtypeStruct((m, d), F32), jax.ShapeDtypeStruct((m, d), F32)],
               grid=(m // tm,),
               in_specs=[pl.BlockSpec((tm, di), lambda i: (i, 0)), pl.BlockSpec((di, d), lambda i: (0, 0)), row,
                         pl.BlockSpec((1, 3, d), lambda i: (i // per, 0, 0))],
               out_specs=[row, row], sem=("parallel",), comm=comm)(ybr, w_out, x, mod)


def _out_proj_loss(ybr, w_out, x, mod, gain, target, t_seq):
    m, di = ybr.shape
    d = w_out.shape[1]
    tm = _tile(t_seq, 512)
    per = t_seq // tm

    def body(y_ref, w_ref, x_ref, mod_ref, g_ref, t_ref, yo_ref, dx_ref, loss_ref, dg_ref):
        i = pl.program_id(0)
        yo = _dot(y_ref[...], w_ref[...])
        yo_ref[...] = yo
        xv = x_ref[...] + mod_ref[0, 2:3, :] * yo
        g = g_ref[...]
        rstd = lax.rsqrt(jnp.mean(xv * xv, axis=-1, keepdims=True) + EPS)
        xhat = xv * rstd
        err = xhat * g - t_ref[...]
        dy = err * (1.0 / d)
        dxhat = dy * g
        dx_ref[...] = rstd * (dxhat - xhat * jnp.mean(dxhat * xhat, axis=-1, keepdims=True))

        @pl.when(i == 0)
        def _():
            loss_ref[...] = jnp.zeros_like(loss_ref)
            dg_ref[...] = jnp.zeros_like(dg_ref)

        loss_ref[...] += 0.5 * jnp.sum(jnp.mean(err * err, axis=-1, keepdims=True), axis=0, keepdims=True)
        dg_ref[...] += jnp.sum(dy * xhat, axis=0, keepdims=True)

    row = pl.BlockSpec((tm, d), lambda i: (i, 0))
    vec = pl.BlockSpec((1, d), lambda i: (0, 0))
    return _pc(body, name="out_proj_loss",
               out_shape=[jax.ShapeDtypeStruct((m, d), F32), jax.ShapeDtypeStruct((m, d), F32),
                          jax.ShapeDtypeStruct((1, 1), F32), jax.ShapeDtypeStruct((1, d), F32)],
               grid=(m // tm,),
               in_specs=[pl.BlockSpec((tm, di), lambda i: (i, 0)), pl.BlockSpec((di, d), lambda i: (0, 0)), row,
                         pl.BlockSpec((1, 3, d), lambda i: (i // per, 0, 0)), vec, row],
               out_specs=[row, row, pl.BlockSpec((1, 1), lambda i: (0, 0)), vec],
               sem=("arbitrary",))(ybr, w_out, x, mod, gain, target)


def _gate_dybr(dxn, yout, mod, w_out, t_seq, name):
    m, d = dxn.shape
    di = w_out.shape[0]
    nb = m // t_seq
    tm = _tile(t_seq, 1024)
    per = t_seq // tm

    def body(dxn_ref, yo_ref, mod_ref, w_ref, dy_ref, dgate_ref, o_ref):
        i = pl.program_id(0)
        dv = dxn_ref[...]
        dy = (mod_ref[0, 2:3, :] * dv).astype(BF16)
        dy_ref[...] = dy
        o_ref[...] = _dot_nt(dy, w_ref[...])

        @pl.when(i % per == 0)
        def _():
            dgate_ref[...] = jnp.zeros_like(dgate_ref)

        dgate_ref[0] += jnp.sum(dv * yo_ref[...], axis=0, keepdims=True)

    row = pl.BlockSpec((tm, d), lambda i: (i, 0))
    return _pc(body, name=name,
               out_shape=[jax.ShapeDtypeStruct((m, d), BF16), jax.ShapeDtypeStruct((nb, 1, d), F32),
                          jax.ShapeDtypeStruct((m, di), F32)],
               grid=(m // tm,),
               in_specs=[row, row, pl.BlockSpec((1, 3, d), lambda i: (i // per, 0, 0)),
                         pl.BlockSpec((di, d), lambda i: (0, 0))],
               out_specs=[row, pl.BlockSpec((1, 1, d), lambda i: (i // per, 0, 0)),
                          pl.BlockSpec((tm, di), lambda i: (i, 0))],
               sem=("arbitrary",))(dxn, yout, mod, w_out)


def _mm_dw_out(ybr, dy, name, comm=None):
    m, di = ybr.shape
    d = dy.shape[1]
    tn = _tile(di, 512)

    def body(y_ref, dy_ref, o_ref):
        o_ref[...] = _dot_tn(y_ref[...], dy_ref[...])

    return _pc(body, name=name, out_shape=jax.ShapeDtypeStruct((di, d), F32), grid=(di // tn,),
               in_specs=[pl.BlockSpec((m, tn), lambda n: (0, n)), pl.BlockSpec((m, d), lambda n: (0, 0))],
               out_specs=pl.BlockSpec((tn, d), lambda n: (n, 0)), sem=("parallel",), comm=comm)(ybr, dy)


def _sgu_mask():
    t = lax.broadcasted_iota(jnp.int32, (SG_BLOCK, SG_BLOCK), 0)
    s = lax.broadcasted_iota(jnp.int32, (SG_BLOCK, SG_BLOCK), 1)
    return (s // CHUNK) <= (t // CHUNK)


def _a_mid_fwd(proj, ln_g, ln_b, w_s, bs_t, t_seq, comm=None):
    m, n3 = proj.shape
    di = n3 // 3
    gd = di // SG_GROUPS
    r = _tile(t_seq, 256)
    nblk = r // SG_BLOCK

    def body(p_ref, lg_ref, lb_ref, ws_ref, bs_ref, ybr_ref, s_scr):
        v = _gelu(p_ref[:, di:2 * di])
        mu = jnp.mean(v, axis=-1, keepdims=True)
        vc = v - mu
        rstd = lax.rsqrt(jnp.mean(vc * vc, axis=-1, keepdims=True) + EPS)
        vb = (vc * rstd * lg_ref[...] + lb_ref[...]).astype(BF16)
        mask = _sgu_mask()
        for gi in range(SG_GROUPS):
            ws = jnp.where(mask, ws_ref[gi], 0.0).astype(BF16)
            bcol = bs_ref[:, gi:gi + 1]
            for b in range(nblk):
                rows = slice(b * SG_BLOCK, (b + 1) * SG_BLOCK)
                cols = slice(gi * gd, (gi + 1) * gd)
                s_scr[rows, cols] = _dot(ws, vb[rows, cols]) + bcol
        gg = p_ref[:, 2 * di:]
        ybr_ref[...] = (_gelu(p_ref[:, :di]) * s_scr[...] * (gg * _sigmoid(gg))).astype(BF16)

    vec = pl.BlockSpec((1, di), lambda i: (0, 0))
    return _pc(body, name="a_mid_fwd", out_shape=jax.ShapeDtypeStruct((m, di), BF16), grid=(m // r,),
               in_specs=[pl.BlockSpec((r, n3), lambda i: (i, 0)), vec, vec,
                         pl.BlockSpec((SG_GROUPS, SG_BLOCK, SG_BLOCK), lambda i: (0, 0, 0)),
                         pl.BlockSpec((SG_BLOCK, 128), lambda i: (0, 0))],
               out_specs=pl.BlockSpec((r, di), lambda i: (i, 0)),
               scratch=[pltpu.VMEM((r, di), F32)], sem=("parallel",), comm=comm)(proj, ln_g, ln_b, w_s, bs_t)


def _a_mid_bwd(proj, dybr, ln_g, ln_b, w_s, bs_t, t_seq, comm=None):
    m, n3 = proj.shape
    di = n3 // 3
    gd = di // SG_GROUPS
    r = _tile(t_seq, 256)
    nblk = r // SG_BLOCK

    def body(p_ref, dy_ref, lg_ref, lb_ref, ws_ref, bs_ref,
             dp_ref, dlg_ref, dlb_ref, dws_ref, dbs_ref, s_scr, dvl_scr):
        i = pl.program_id(0)

        @pl.when(i == 0)
        def _():
            dlg_ref[...] = jnp.zeros_like(dlg_ref)
            dlb_ref[...] = jnp.zeros_like(dlb_ref)
            dws_ref[...] = jnp.zeros_like(dws_ref)
            dbs_ref[...] = jnp.zeros_like(dbs_ref)

        v, dgelu_v = _gelu_and_grad(p_ref[:, di:2 * di])
        mu = jnp.mean(v, axis=-1, keepdims=True)
        vc = v - mu
        rstd = lax.rsqrt(jnp.mean(vc * vc, axis=-1, keepdims=True) + EPS)
        vhat = vc * rstd
        lg = lg_ref[...]
        vb = (vhat * lg + lb_ref[...]).astype(BF16)
        u, dgelu_u = _gelu_and_grad(p_ref[:, :di])
        gg = p_ref[:, 2 * di:]
        sg = _sigmoid(gg)
        dyv = dy_ref[...]
        dus = dyv * (gg * sg)
        dsb = (dus * u).astype(BF16)
        ds32 = dus * u
        mask = _sgu_mask()
        lane = lax.broadcasted_iota(jnp.int32, (SG_BLOCK, 128), 1)
        dbs_acc = jnp.zeros((SG_BLOCK, 128), F32)
        for gi in range(SG_GROUPS):
            ws = jnp.where(mask, ws_ref[gi], 0.0).astype(BF16)
            bcol = bs_ref[:, gi:gi + 1]
            cols = slice(gi * gd, (gi + 1) * gd)
            dws_acc = jnp.zeros((SG_BLOCK, SG_BLOCK), F32)
            dbs_col = jnp.zeros((SG_BLOCK, 1), F32)
            for b in range(nblk):
                rows = slice(b * SG_BLOCK, (b + 1) * SG_BLOCK)
                s_scr[rows, cols] = _dot(ws, vb[rows, cols]) + bcol
                dvl_scr[rows, cols] = _dot_tn(ws, dsb[rows, cols])
                dws_acc += _dot_nt(dsb[rows, cols], vb[rows, cols])
                dbs_col += jnp.sum(ds32[rows, cols], axis=-1, keepdims=True)
            dws_ref[gi] += jnp.where(mask, dws_acc, 0.0)
            dbs_acc += jnp.where(lane == gi, dbs_col, 0.0)
        dbs_ref[...] += dbs_acc
        s = s_scr[...]
        dp_ref[:, :di] = (dus * s * dgelu_u).astype(BF16)
        dp_ref[:, 2 * di:] = (dyv * u * s * (sg * (1.0 + gg * (1.0 - sg)))).astype(BF16)
        dvl = dvl_scr[...]
        dlg_ref[...] += jnp.sum(dvl * vhat, axis=0, keepdims=True)
        dlb_ref[...] += jnp.sum(dvl, axis=0, keepdims=True)
        dvh = dvl * lg
        dv = rstd * (dvh - jnp.mean(dvh, axis=-1, keepdims=True)
                     - vhat * jnp.mean(dvh * vhat, axis=-1, keepdims=True))
        dp_ref[:, di:2 * di] = (dv * dgelu_v).astype(BF16)

    vec = pl.BlockSpec((1, di), lambda i: (0, 0))
    wsb = pl.BlockSpec((SG_GROUPS, SG_BLOCK, SG_BLOCK), lambda i: (0, 0, 0))
    bsb = pl.BlockSpec((SG_BLOCK, 128), lambda i: (0, 0))
    return _pc(body, name="a_mid_bwd",
               out_shape=[jax.ShapeDtypeStruct((m, n3), BF16), jax.ShapeDtypeStruct((1, di), F32),
                          jax.ShapeDtypeStruct((1, di), F32),
                          jax.ShapeDtypeStruct((SG_GROUPS, SG_BLOCK, SG_BLOCK), F32),
                          jax.ShapeDtypeStruct((SG_BLOCK, 128), F32)],
               grid=(m // r,),
               in_specs=[pl.BlockSpec((r, n3), lambda i: (i, 0)), pl.BlockSpec((r, di), lambda i: (i, 0)),
                         vec, vec, wsb, bsb],
               out_specs=[pl.BlockSpec((r, n3), lambda i: (i, 0)), vec, vec, wsb, bsb],
               scratch=[pltpu.VMEM((r, di), F32), pltpu.VMEM((r, di), F32)],
               sem=("arbitrary",), comm=comm)(proj, dybr, ln_g, ln_b, w_s, bs_t)


def _chunk_rows(n):
    if isinstance(n, int):
        return pl.ds(n * CHUNK, CHUNK)
    return pl.ds(pl.multiple_of(n * CHUNK, CHUNK), CHUNK)


def _hgrn_dims(t_seq, di):
    tr = _tile(t_seq, 128)
    hc = _tile(di, 2048)
    return tr, hc, hc // HEAD_DIM


def _hgrn_gates(f_ref, lb, a_scr, k_scr, tr):
    sig = _sigmoid(f_ref[...])
    fg = lb + (1.0 - lb) * sig
    k_scr[...] = 1.0 - fg
    logf = jnp.log(fg)
    g = min(CUM_ROWS, tr)
    tri = _tri_mask(g, reverse=False)
    for rg in range(tr // g):
        a_scr[rg * g:(rg + 1) * g, :] = _tri_apply(tri, logf[rg * g:(rg + 1) * g, :])
    return sig, fg


def _hgrn_fwd(proj, lbj, gn, nb, t_seq):
    _, m, di = proj.shape
    tr, hc, hpg = _hgrn_dims(t_seq, di)
    nt, nhg, ncl = t_seq // tr, di // hc, tr // CHUNK
    nheads = di // HEAD_DIM

    def body(p_ref, lb_ref, gn_ref, o_ref, ybr_ref, st_ref, st_scr, a_scr, k_scr):
        q_ref, f_ref, i_ref, g_ref = (p_ref.at[s] for s in range(4))
        t = pl.program_id(2)

        @pl.when(t == 0)
        def _():
            st_scr[...] = jnp.zeros_like(st_scr)

        _hgrn_gates(f_ref, lb_ref[0:1, :], a_scr, k_scr, tr)
        gnv = gn_ref[...]
        rr = lax.broadcasted_iota(jnp.int32, (CHUNK, CHUNK), 0)
        cc = lax.broadcasted_iota(jnp.int32, (CHUNK, CHUNK), 1)
        causal = cc <= rr

        def chunk(n, carry):
            rows = _chunk_rows(n)
            lanes = [slice(hd * HEAD_DIM, (hd + 1) * HEAD_DIM) for hd in range(hpg)]
            hs = []
            for hd, ls in enumerate(lanes):
                h = {}
                ah, kh = a_scr[rows, ls], k_scr[rows, ls]
                qp = q_ref[rows, ls]
                qh = qp * _sigmoid(qp)
                h["vb"] = i_ref[rows, ls].astype(BF16)
                aref, alast = ah[CHUNK // 2 - 1:CHUNK // 2, :], ah[CHUNK - 1:CHUNK, :]
                h["q_in"] = (qh * jnp.exp(ah - aref)).astype(BF16)
                h["k_in"] = (kh * jnp.exp(aref - ah)).astype(BF16)
                h["q_out"] = (qh * jnp.exp(ah)).astype(BF16)
                h["k_out"] = (kh * jnp.exp(alast - ah)).astype(BF16)
                h["dec"] = jnp.exp(alast)
                st = st_scr[hd]
                st_ref[n, hd] = st
                h["st"] = st
                hs.append(h)
            for h in hs:
                h["scores"] = _dot_nt(h["q_in"], h["k_in"])
                h["o_inter"] = _dot_nt(h["q_out"], h["st"].astype(BF16))
                h["st_mm"] = _dot_tn(h["vb"], h["k_out"])
            for h in hs:
                h["o"] = _dot(jnp.where(causal, h["scores"], 0.0).astype(BF16), h["vb"]) + h["o_inter"]
            for hd, (h, ls) in enumerate(zip(hs, lanes)):
                st_scr[hd] = h["st"] * h["dec"] + h["st_mm"]
                o = h["o"]
                o_ref[rows, ls] = o
                rstd = lax.rsqrt(jnp.mean(o * o, axis=-1, keepdims=True) + EPS)
                gg = g_ref[rows, ls]
                ybr_ref[rows, ls] = ((o * rstd * gnv) * (gg * _sigmoid(gg))).astype(BF16)
            return carry

        lax.fori_loop(0, ncl, chunk, 0)

    blk = pl.BlockSpec((tr, hc), lambda hg, b, t: (b * nt + t, hg))
    return _pc(body, name="hgrn_fwd",
               out_shape=[jax.ShapeDtypeStruct((m, di), F32), jax.ShapeDtypeStruct((m, di), BF16),
                          jax.ShapeDtypeStruct((m // CHUNK, nheads, HEAD_DIM, HEAD_DIM), F32)],
               grid=(nhg, nb, nt),
               in_specs=[pl.BlockSpec((4, tr, hc), lambda hg, b, t: (0, b * nt + t, hg)),
                         pl.BlockSpec((2, hc), lambda hg, b, t: (0, hg)),
                         pl.BlockSpec((1, HEAD_DIM), lambda hg, b, t: (0, 0))],
               out_specs=[blk, blk, pl.BlockSpec((ncl, hpg, HEAD_DIM, HEAD_DIM),
                                                 lambda hg, b, t: (b * nt + t, hg, 0, 0))],
               scratch=[pltpu.VMEM((hpg, HEAD_DIM, HEAD_DIM), F32), pltpu.VMEM((tr, hc), F32),
                        pltpu.VMEM((tr, hc), F32)],
               sem=("parallel", "arbitrary", "arbitrary"))(proj, lbj, gn)


def _hgrn_bwd(proj, o_all, dybr, states, lbj, gn, nb, t_seq, comm=None):
    _, m, di = proj.shape
    tr, hc, hpg = _hgrn_dims(t_seq, di)
    nt, nhg, ncl = t_seq // tr, di // hc, tr // CHUNK

    def body(p_ref, o_ref, dy_ref, st_ref, lb_ref, gn_ref,
             dp_ref, dlb_ref, dgn_ref, dst_scr, a_scr, k_scr, da_scr, dk_scr):
        q_ref, f_ref, i_ref, g_ref = (p_ref.at[s] for s in range(4))
        hg, b, t = pl.program_id(0), pl.program_id(1), pl.program_id(2)

        @pl.when(t == 0)
        def _():
            dst_scr[...] = jnp.zeros_like(dst_scr)

        @pl.when((b == 0) & (t == 0))
        def _():
            dlb_ref[...] = jnp.zeros_like(dlb_ref)

        @pl.when((hg == 0) & (b == 0) & (t == 0))
        def _():
            dgn_ref[...] = jnp.zeros_like(dgn_ref)

        lb = lb_ref[0:1, :]
        sig, fg = _hgrn_gates(f_ref, lb, a_scr, k_scr, tr)
        gnv = gn_ref[...]
        rr = lax.broadcasted_iota(jnp.int32, (CHUNK, CHUNK), 0)
        cc = lax.broadcasted_iota(jnp.int32, (CHUNK, CHUNK), 1)
        causal = cc <= rr
        rowi = lax.broadcasted_iota(jnp.int32, (CHUNK, HEAD_DIM), 0)

        def chunk(it, carry):
            n = ncl - 1 - it
            rows = _chunk_rows(n)
            for hd0 in range(0, hpg, PHASE_HEADS):
                heads(n, rows, range(hd0, min(hpg, hd0 + PHASE_HEADS)))
            return carry

        def heads(n, rows, ids):
            lanes = [slice(hd * HEAD_DIM, (hd + 1) * HEAD_DIM) for hd in ids]
            hs = []
            for hd, ls in zip(ids, lanes):
                h = {}
                ah, kh = a_scr[rows, ls], k_scr[rows, ls]
                qp = q_ref[rows, ls]
                sq = _sigmoid(qp)
                qh = qp * sq
                h["dsilu_q"] = sq * (1.0 + qp * (1.0 - sq))
                h["vb"] = i_ref[rows, ls].astype(BF16)
                aref, alast = ah[CHUNK // 2 - 1:CHUNK // 2, :], ah[CHUNK - 1:CHUNK, :]
                h["e1"], h["e2"] = jnp.exp(ah - aref), jnp.exp(aref - ah)
                h["e3"], h["e4"] = jnp.exp(ah), jnp.exp(alast - ah)
                h["dec"] = jnp.exp(alast)
                h["q_in"], h["k_in"], h["q_out"], h["k_out"] = qh * h["e1"], kh * h["e2"], qh * h["e3"], kh * h["e4"]
                for nm in ("q_in", "k_in", "q_out", "k_out"):
                    h[nm + "_b"] = h[nm].astype(BF16)
                o = o_ref[rows, ls]
                rstd = lax.rsqrt(jnp.mean(o * o, axis=-1, keepdims=True) + EPS)
                ohat = o * rstd
                gg = g_ref[rows, ls]
                sg = _sigmoid(gg)
                dyv = dy_ref[rows, ls]
                d_on = dyv * (gg * sg)
                dp_ref[3, rows, ls] = (dyv * (ohat * gnv) * (sg * (1.0 + gg * (1.0 - sg)))).astype(BF16)
                h["dgn"] = jnp.sum(d_on * ohat, axis=0, keepdims=True)
                dohat = d_on * gnv
                do = rstd * (dohat - ohat * jnp.mean(dohat * ohat, axis=-1, keepdims=True))
                h["do_b"] = do.astype(BF16)
                h["st_prev"] = st_ref[n, hd]
                h["dst"] = dst_scr[hd]
                hs.append(h)
            for h in hs:
                dst_b = h["dst"].astype(BF16)
                h["scores"] = _dot_nt(h["q_in_b"], h["k_in_b"])
                h["dscores"] = _dot_nt(h["do_b"], h["vb"])
                h["dv_inter"] = _dot_nt(h["k_out_b"], dst_b)
                h["dq_out"] = _dot(h["do_b"], h["st_prev"].astype(BF16))
                h["dk_out"] = _dot(h["vb"], dst_b)
                h["dst_mm"] = _dot_tn(h["do_b"], h["q_out_b"])
            for h in hs:
                scores = jnp.where(causal, h["scores"], 0.0).astype(BF16)
                dscores = jnp.where(causal, h["dscores"], 0.0).astype(BF16)
                h["dv"] = _dot_tn(scores, h["do_b"]) + h["dv_inter"]
                h["dq_in"] = _dot(dscores, h["k_in_b"])
                h["dk_in"] = _dot_tn(dscores, h["q_in_b"])
            dgn = hs[0]["dgn"]
            for h in hs[1:]:
                dgn = dgn + h["dgn"]
            dgn_ref[...] += dgn
            for hd, h, ls in zip(ids, hs, lanes):
                ddec = jnp.sum(h["dst"] * h["st_prev"], axis=0, keepdims=True)
                dst_scr[hd] = h["dst"] * h["dec"] + h["dst_mm"]
                dp_ref[2, rows, ls] = h["dv"].astype(BF16)
                dq = h["dq_in"] * h["e1"] + h["dq_out"] * h["e3"]
                dp_ref[0, rows, ls] = (dq * h["dsilu_q"]).astype(BF16)
                dk_scr[rows, ls] = h["dk_in"] * h["e2"] + h["dk_out"] * h["e4"]
                t_in = h["dq_in"] * h["q_in"] - h["dk_in"] * h["k_in"]
                t_out = h["dk_out"] * h["k_out"]
                da = t_in + h["dq_out"] * h["q_out"] - t_out
                da_ref_row = -jnp.sum(t_in, axis=0, keepdims=True)
                da_last_row = jnp.sum(t_out, axis=0, keepdims=True) + ddec * h["dec"]
                da = da + jnp.where(rowi == CHUNK // 2 - 1, da_ref_row, 0.0) \
                        + jnp.where(rowi == CHUNK - 1, da_last_row, 0.0)
                da_scr[rows, ls] = da

        if ncl <= 2:
            for it in range(ncl):
                chunk(it, 0)
        else:
            lax.fori_loop(0, ncl, chunk, 0)
        g = min(CUM_ROWS, tr)
        tri = _tri_mask(g, reverse=True)
        for rg in range(tr // g):
            rs = slice(rg * g, (rg + 1) * g)
            dlogf = _tri_apply(tri, da_scr[rs, :])
            df = dlogf / fg[rs, :] - dk_scr[rs, :]
            sgr = sig[rs, :]
            dp_ref[1, rs, :] = (df * (1.0 - lb) * (sgr * (1.0 - sgr))).astype(BF16)
            dlb_ref[...] += jnp.sum(df * (1.0 - sgr), axis=0, keepdims=True) * lb_ref[1:2, :]

    blk = pl.BlockSpec((tr, hc), lambda hg, b, t: (b * nt + (nt - 1 - t), hg))
    return _pc(body, name="hgrn_bwd",
               out_shape=[jax.ShapeDtypeStruct((4, m, di), BF16), jax.ShapeDtypeStruct((1, di), F32),
                          jax.ShapeDtypeStruct((1, HEAD_DIM), F32)],
               grid=(nhg, nb, nt),
               in_specs=[pl.BlockSpec((4, tr, hc), lambda hg, b, t: (0, b * nt + (nt - 1 - t), hg)), blk, blk,
                         pl.BlockSpec((ncl, hpg, HEAD_DIM, HEAD_DIM),
                                      lambda hg, b, t: (b * nt + (nt - 1 - t), hg, 0, 0)),
                         pl.BlockSpec((2, hc), lambda hg, b, t: (0, hg)),
                         pl.BlockSpec((1, HEAD_DIM), lambda hg, b, t: (0, 0))],
               out_specs=[pl.BlockSpec((4, tr, hc), lambda hg, b, t: (0, b * nt + (nt - 1 - t), hg)),
                          pl.BlockSpec((1, hc), lambda hg, b, t: (0, hg)),
                          pl.BlockSpec((1, HEAD_DIM), lambda hg, b, t: (0, 0))],
               scratch=[pltpu.VMEM((hpg, HEAD_DIM, HEAD_DIM), F32)] + [pltpu.VMEM((tr, hc), F32)] * 4,
               sem=("arbitrary", "arbitrary", "arbitrary"), comm=comm)(
                   proj, o_all, dybr, states, lbj, gn)


def _adamw(parts, w, m, v, name):
    r, c = w.shape
    tr = _tile(r, 256)
    npart = len(parts)
    c1 = 1.0 - ADAM_B1 ** ADAM_STEP
    c2 = 1.0 - ADAM_B2 ** ADAM_STEP

    def body(*refs):
        p_refs = refs[:npart]
        _adamw_math(p_refs, *refs[npart:], c1, c2)

    blk = pl.BlockSpec((tr, c), lambda i: (i, 0))
    return _pc(body, name=name, out_shape=[jax.ShapeDtypeStruct((r, c), F32)] * 4, grid=(r // tr,),
               in_specs=[blk] * (npart + 3), out_specs=[blk] * 4, sem=("parallel",))(*parts, w, m, v)


def _adamw_math(p_refs, w_ref, m_ref, v_ref, g_ref, d_ref, nm_ref, nv_ref, c1, c2):
    g = p_refs[0][...].astype(F32)
    for p in p_refs[1:]:
        g = g + p[...].astype(F32)
    nm = ADAM_B1 * m_ref[...] + (1.0 - ADAM_B1) * g
    nv = ADAM_B2 * v_ref[...] + (1.0 - ADAM_B2) * (g * g)
    g_ref[...] = g
    nm_ref[...] = nm
    nv_ref[...] = nv
    d_ref[...] = -ADAM_LR * ((nm / c1) / (jnp.sqrt(nv / c2) + ADAM_EPS) + ADAM_WD * w_ref[...])


def _adamw_blocks(parts, idx, w, m, v, name):
    r, c = w.shape
    tr = _tile(r, 256)
    npart = len(parts)
    c1 = 1.0 - ADAM_B1 ** ADAM_STEP
    c2 = 1.0 - ADAM_B2 ** ADAM_STEP

    def body(idx_ref, *refs):
        _adamw_math(refs[:npart], *refs[npart:], c1, c2)

    def sel(p):
        return pl.BlockSpec((None, tr, c), lambda i, s: (s[p], i, 0))

    blk = pl.BlockSpec((tr, c), lambda i, s: (i, 0))
    gs = pltpu.PrefetchScalarGridSpec(num_scalar_prefetch=1, grid=(r // tr,),
                                      in_specs=[sel(p) for p in range(npart)] + [blk] * 3, out_specs=[blk] * 4)
    return _pc(body, name=name, out_shape=[jax.ShapeDtypeStruct((r, c), F32)] * 4, grid_spec=gs,
               sem=("parallel",))(idx, *parts, w, m, v)


_EARLY = ["a_ln_gain", "a_ln_bias", "a_w_s", "a_b_s", "b_lower_bounds", "b_gn_gain"]


def _pack(arrs):
    flat = jnp.concatenate([a.reshape(-1) for a in arrs])
    rows = -(-flat.shape[0] // 1024) * 8
    return jnp.pad(flat, (0, rows * 128 - flat.shape[0])).reshape(rows, 128)


def _unpack(buf, like):
    flat = buf.reshape(-1)
    out, off = [], 0
    for a in like:
        out.append(flat[off:off + a.size].reshape(a.shape))
        off += a.size
    return out


def kernel(x, c, norm_gain, w_ada, b_ada, a_w_in, a_ln_gain, a_ln_bias, a_w_s, a_b_s, a_w_out, b_w_in, b_lower_bounds, b_gn_gain, b_w_out, final_gain, loss_target, m_norm_gain, m_w_ada, m_b_ada, m_a_w_in, m_a_ln_gain, m_a_ln_bias, m_a_w_s, m_a_b_s, m_a_w_out, m_b_w_in, m_b_lower_bounds, m_b_gn_gain, m_b_w_out, m_final_gain, v_norm_gain, v_w_ada, v_b_ada, v_a_w_in, v_a_ln_gain, v_a_ln_bias, v_a_w_s, v_a_b_s, v_a_w_out, v_b_w_in, v_b_lower_bounds, v_b_gn_gain, v_b_w_out, v_final_gain):
    w = dict(norm_gain=norm_gain, w_ada=w_ada, b_ada=b_ada, a_w_in=a_w_in, a_ln_gain=a_ln_gain,
             a_ln_bias=a_ln_bias, a_w_s=a_w_s, a_b_s=a_b_s, a_w_out=a_w_out, b_w_in=b_w_in,
             b_lower_bounds=b_lower_bounds, b_gn_gain=b_gn_gain, b_w_out=b_w_out, final_gain=final_gain)
    mo = dict(norm_gain=m_norm_gain, w_ada=m_w_ada, b_ada=m_b_ada, a_w_in=m_a_w_in, a_ln_gain=m_a_ln_gain,
              a_ln_bias=m_a_ln_bias, a_w_s=m_a_w_s, a_b_s=m_a_b_s, a_w_out=m_a_w_out, b_w_in=m_b_w_in,
              b_lower_bounds=m_b_lower_bounds, b_gn_gain=m_b_gn_gain, b_w_out=m_b_w_out, final_gain=m_final_gain)
    vo = dict(norm_gain=v_norm_gain, w_ada=v_w_ada, b_ada=v_b_ada, a_w_in=v_a_w_in, a_ln_gain=v_a_ln_gain,
              a_ln_bias=v_a_ln_bias, a_w_s=v_a_w_s, a_b_s=v_a_b_s, a_w_out=v_a_w_out, b_w_in=v_b_w_in,
              b_lower_bounds=v_b_lower_bounds, b_gn_gain=v_b_gn_gain, b_w_out=v_b_w_out, final_gain=v_final_gain)

    nb, t_seq, d = x.shape
    m = nb * t_seq
    ncol_ada = w_ada.shape[2]
    xi, yi, ci = lax.axis_index("x"), lax.axis_index("y"), lax.axis_index("c")
    me = 4 * xi + 2 * yi + ci

    c_g, wa_in_g = _all_gather([c, a_w_in[0].astype(BF16)], "gather_c_wa")

    c_all = c_g.reshape(NDEV * nb, d)
    b_cols = lax.dynamic_slice(b_ada, (0, me * ncol_ada), (2, ncol_ada)).reshape(2, 1, ncol_ada)
    mod_part, lbj = _ada_fwd(c_all, w_ada, b_cols, b_lower_bounds)
    mod_all = _all_gather([mod_part], "gather_mod")[0]
    mod_mine = lax.dynamic_slice_in_dim(mod_all, me * nb, nb, axis=2)
    mod_mine = mod_mine.transpose(1, 2, 0, 3).reshape(2, nb, 3, d)
    mod0, mod1 = mod_mine[0], mod_mine[1]

    di = a_w_out.shape[1] * NDEV

    xf = x.reshape(m, d)
    tgt = loss_target.reshape(m, d)
    ng0, ng1 = norm_gain[0:1], norm_gain[1:2]
    ncb = b_w_in.shape[2]
    wb_lo, wb_hi = b_w_in[0][:, :ncb // 2].astype(BF16), b_w_in[0][:, ncb // 2:].astype(BF16)
    h0, h0_t = _prenorm(xf, ng0, mod0, t_seq, "prenorm_a")
    proj_a, half = _mm_in(h0, [wa_in_g], 1, "in_proj_a", comm=_gather_first([a_w_out[0].astype(BF16), wb_lo]))
    bs_t = jnp.pad(a_b_s[0].T, ((0, 0), (0, 128 - SG_GROUPS)))
    ybr_a, (wa_out_g, wb_lo_g, wb_hi_half) = _a_mid_fwd(
        proj_a, a_ln_gain, a_ln_bias, a_w_s[0], bs_t, t_seq, comm=_join(_gather_second(half), _gather_first([wb_hi])))
    wa_out = wa_out_g.reshape(di, d)
    (yout_a, x1), (wb_hi_g, wb_out_half) = _out_proj(
        ybr_a, wa_out, xf, mod0, t_seq, "out_proj_a",
        comm=_join(_gather_second([wb_hi_half]), _gather_first([b_w_out[0].astype(BF16)])))
    wb_in_g = [wb_lo_g, wb_hi_g]
    h1, h1_t = _prenorm(x1, ng1, mod1, t_seq, "prenorm_b")
    proj_b, (wb_out_g,) = _mm_in(h1, wb_in_g, 4, "in_proj_b", comm=_gather_second([wb_out_half]))
    wb_out = wb_out_g.reshape(di, d)
    o_b, ybr_b, states = _hgrn_fwd(proj_b, lbj, b_gn_gain, nb, t_seq)
    yout_b, dx2, loss_part, d_final_gain = _out_proj_loss(ybr_b, wb_out, x1, mod1, final_gain.reshape(1, d), tgt, t_seq)

    rows_out = a_w_out.shape[1]
    dy_b, dgate1, dybr_b = _gate_dybr(dx2, yout_b, mod1, wb_out, t_seq, "dybr_b")
    rs_wb_out = _ReduceScatter(_mm_dw_out(ybr_b, dy_b, "dw_out_b").reshape(NDEV, rows_out, d), "b_w_out")
    (dproj_b, d_lb, d_gn), got = _hgrn_bwd(proj_b, o_b, dybr_b, states, lbj, b_gn_gain, nb, t_seq,
                                           comm=rs_wb_out.swap_core())
    rs_wb_out.after_core(got[0])
    dh1, got = _mm_din(dproj_b, wb_in_g, 4, "dh_b", comm=rs_wb_out.swap_chips())
    rs_wb_out.after_chips(got[0])
    dx1, dss1, dgain1 = _prenorm_bwd(dh1, x1, ng1, mod1, dx2, t_seq, "prenorm_bwd_b")
    rs_wb_in = _ReduceScatter(_mm_dw_in(h1_t, dproj_b, ncb, 4, "dw_in_b"), "b_w_in")

    dy_a, dgate0, dybr_a = _gate_dybr(dx1, yout_a, mod0, wa_out, t_seq, "dybr_a")
    g_wa_out, got = _mm_dw_out(ybr_a, dy_a, "dw_out_a", comm=rs_wb_in.swap_core())
    rs_wb_in.after_core(got[0])
    rs_wa_out = _ReduceScatter(g_wa_out.reshape(NDEV, rows_out, d), "a_w_out")
    (dproj_a, d_lng, d_lnb, d_ws, d_bs_t), got = _a_mid_bwd(
        proj_a, dybr_a, a_ln_gain, a_ln_bias, a_w_s[0], bs_t, t_seq,
        comm=_join(rs_wb_in.swap_chips(), rs_wa_out.swap_core()))
    rs_wb_in.after_chips(got[0])
    rs_wa_out.after_core(got[1])
    part = dict(a_ln_gain=d_lng, a_ln_bias=d_lnb, a_w_s=d_ws[None], a_b_s=d_bs_t[:, :SG_GROUPS].T[None],
                b_lower_bounds=jnp.concatenate([-d_lb, d_lb], axis=0), b_gn_gain=d_gn)
    early_pack = _pack([part[k].reshape(w[k].shape) for k in _EARLY])
    g_wa_in, got = _mm_dw_in(h0_t, dproj_a, wa_in_g.shape[2], 1, "dw_in_a",
                             comm=_join(rs_wa_out.swap_chips(), _gather_first([early_pack])))
    rs_wa_out.after_chips(got[0])
    rs_wa_in = _ReduceScatter(g_wa_in, "a_w_in")
    n_tiles = m // _din_tile(m)
    assert n_tiles >= 2
    first_tiles = max(1, (3 * n_tiles) // 8)
    dh0, got2 = _mm_din(dproj_a, [wa_in_g], 1, "dh_a_first", tiles=(0, first_tiles),
                        comm=_join(rs_wa_in.swap_core(), _gather_second([got[1]])))
    rs_wa_in.after_core(got2[0])
    early_all = got2[1]
    dh0, got = _mm_din(dproj_a, [wa_in_g], 1, "dh_a_rest", comm=rs_wa_in.swap_chips(),
                       tiles=(first_tiles, n_tiles - first_tiles), prev=dh0)
    rs_wa_in.after_chips(got[0])
    dx0, dss0, dgain0 = _prenorm_bwd(dh0, xf, ng0, mod0, dx1, t_seq, "prenorm_bwd_a")
    grad_x = dx0.reshape(nb, t_seq, d)

    dmod = jnp.stack([jnp.concatenate([dss0, dgate0], axis=1), jnp.concatenate([dss1, dgate1], axis=1)])
    late_like = [norm_gain, final_gain, loss_part.reshape(1)]
    late_pack = _pack([jnp.concatenate([dgain0, dgain1], axis=0), d_final_gain[0], loss_part.reshape(1)])
    dmod_all, late_all = _all_gather([dmod.reshape(2, nb, 3 * d), late_pack], "gather_tail")
    dmod_all = dmod_all.transpose(1, 0, 2, 3).reshape(2, NDEV * nb, 3 * d)
    dmod_cols = lax.dynamic_slice_in_dim(dmod_all, me * ncol_ada, ncol_ada, axis=2)
    g_w_ada, g_b_ada = _ada_bwd(c_all, dmod_cols, dmod_all)

    res = {}
    early_like = [w[k] for k in _EARLY]
    dev_order = jnp.arange(NDEV, dtype=jnp.int32)
    sm = _adamw_blocks([early_all] * NDEV, dev_order, _pack(early_like), _pack([mo[k] for k in _EARLY]),
                       _pack([vo[k] for k in _EARLY]), "adamw_small_early")
    sm = [dict(zip(_EARLY, _unpack(buf, early_like))) for buf in sm]
    for k in _EARLY:
        res[k] = tuple(s[k] for s in sm)
    zero = jnp.zeros((1,), F32)
    sm = _adamw_blocks([late_all] * NDEV, dev_order, _pack([norm_gain, final_gain, zero]),
                       _pack([mo["norm_gain"], mo["final_gain"], zero]),
                       _pack([vo["norm_gain"], vo["final_gain"], zero]), "adamw_small_late")
    sm = [_unpack(buf, late_like) for buf in sm]
    res["norm_gain"] = tuple(s[0] for s in sm)
    res["final_gain"] = tuple(s[1] for s in sm)
    loss = sm[0][2][0]
    rb = _adamw([g_b_ada], b_ada, mo["b_ada"], vo["b_ada"], "adamw_b_ada")
    res["b_ada"] = tuple(rb)
    sh = w_ada.shape
    ra = _adamw([g_w_ada.reshape(sh[0] * sh[1], sh[2])], w_ada.reshape(sh[0] * sh[1], sh[2]),
                mo["w_ada"].reshape(sh[0] * sh[1], sh[2]), vo["w_ada"].reshape(sh[0] * sh[1], sh[2]), "adamw_w_ada")
    res["w_ada"] = tuple(z.reshape(sh) for z in ra)

    for k, rs in (("b_w_out", rs_wb_out), ("b_w_in", rs_wb_in), ("a_w_out", rs_wa_out), ("a_w_in", rs_wa_in)):
        res[k] = tuple(z[None] for z in _adamw_blocks(rs.parts, rs.idx, w[k][0], mo[k][0], vo[k][0], "adamw_" + k))

    order = ["norm_gain", "w_ada", "b_ada", "a_w_in", "a_ln_gain", "a_ln_bias", "a_w_s", "a_b_s", "a_w_out",
             "b_w_in", "b_lower_bounds", "b_gn_gain", "b_w_out", "final_gain"]
    return (loss, grad_x, *[res[k][0] for k in order], *[res[k][1] for k in order],
            *[res[k][2] for k in order], *[res[k][3] for k in order])
```

```python
import functools
import math

import jax
import jax.numpy as jnp
from jax import lax
from jax.experimental import pallas as pl
from jax.experimental.pallas import tpu as pltpu

F32 = jnp.float32
BF16 = jnp.bfloat16
MESH = pl.DeviceIdType.MESH
NDEV = 8
EPS = 1e-6
CHUNK = 64
SG_BLOCK = 128
SG_GROUPS = 8
HEAD_DIM = 128
CUM_ROWS = 256
PHASE_HEADS = 8
ADAM_LR, ADAM_B1, ADAM_B2, ADAM_EPS, ADAM_WD, ADAM_STEP = 0.001, 0.9, 0.999, 1e-08, 0.01, 10
VMEM_LIMIT = 56 * 1024 * 1024
ANY = pl.BlockSpec(memory_space=pl.ANY)


class _Hosted:
    def __init__(self, arrays, out_shapes, nsem, start, finish, aliases=None):
        self.arrays, self.out_shapes, self.nsem = list(arrays), list(out_shapes), nsem
        self.start, self.finish = start, finish
        self.aliases = dict(aliases or {})


def _join(*comms):
    arrays, outs, aliases, offs, nsem = [], [], {}, [], 0
    for cm in comms:
        offs.append((len(arrays), len(outs), nsem))
        for i, o in cm.aliases.items():
            aliases[len(arrays) + i] = len(outs) + o
        arrays += cm.arrays
        outs += cm.out_shapes
        nsem += cm.nsem

    def run(which):
        def f(ins, outs_, ss, rs, base):
            for cm, (ia, io, isem) in zip(comms, offs):
                getattr(cm, which)(ins[ia:ia + len(cm.arrays)], outs_[io:io + len(cm.out_shapes)], ss, rs, base + isem)
        return f

    return _Hosted(arrays, outs, nsem, run("start"), run("finish"), aliases)


def _pc(body, *, name, out_shape, grid=None, in_specs=None, out_specs=None, scratch=(), sem=None,
        grid_spec=None, comm=None, aliases=None):
    cp = dict(vmem_limit_bytes=VMEM_LIMIT)
    aliases = dict(aliases or {})
    if comm is None:
        if sem is not None:
            cp["dimension_semantics"] = sem
        kw = {"input_output_aliases": aliases}
        if grid_spec is not None:
            kw["grid_spec"] = grid_spec
        else:
            if grid is not None:
                kw["grid"] = grid
            if in_specs is not None:
                kw["in_specs"] = in_specs
            if out_specs is not None:
                kw["out_specs"] = out_specs
            kw["scratch_shapes"] = list(scratch)
        return pl.pallas_call(functools.partial(body), name=name, out_shape=out_shape,
                              compiler_params=pltpu.CompilerParams(**cp), **kw)

    single = not isinstance(out_shape, (list, tuple))
    outs_list = [out_shape] if single else list(out_shape)
    ospecs = [out_specs] if single else list(out_specs)
    n_in, n_out, n_ci, n_co, n_scr = len(in_specs), len(outs_list), len(comm.arrays), len(comm.out_shapes), len(scratch)
    cp["dimension_semantics"] = ("arbitrary",) * len(grid)

    def hosted(*refs):
        cin, hin = refs[:n_in], refs[n_in:n_in + n_ci]
        cout = refs[n_in + n_ci:n_in + n_ci + n_out]
        hout = refs[n_in + n_ci + n_out:n_in + n_ci + n_out + n_co]
        scr = refs[n_in + n_ci + n_out + n_co:n_in + n_ci + n_out + n_co + n_scr]
        ssem, rsem = refs[-2], refs[-1]
        first = functools.reduce(lambda p, q: p & q, [pl.program_id(a) == 0 for a in range(len(grid))])
        last = functools.reduce(lambda p, q: p & q, [pl.program_id(a) == grid[a] - 1 for a in range(len(grid))])

        @pl.when(first)
        def _():
            comm.start(hin, hout, ssem, rsem, 0)

        body(*cin, *cout, *scr)

        @pl.when(last)
        def _():
            comm.finish(hin, hout, ssem, rsem, 0)

    call = pl.pallas_call(
        hosted, name=name, grid=grid, in_specs=list(in_specs) + [ANY] * n_ci, out_specs=ospecs + [ANY] * n_co,
        out_shape=outs_list + comm.out_shapes,
        scratch_shapes=list(scratch) + [pltpu.SemaphoreType.DMA((comm.nsem,)), pltpu.SemaphoreType.DMA((comm.nsem,))],
        input_output_aliases={**aliases, **{n_in + i: n_out + o for i, o in comm.aliases.items()}},
        compiler_params=pltpu.CompilerParams(**cp))

    def run(*args):
        res = call(*args, *comm.arrays)
        comp = res[:n_out]
        return (comp[0] if single else comp), list(res[n_out:])

    return run


def _tile(n, pref):
    return pref if n % pref == 0 else n


def _sigmoid(x):
    return 1.0 / (1.0 + jnp.exp(-x))


def _gelu(x):
    c = math.sqrt(2.0 / math.pi)
    return 0.5 * x * (1.0 + jnp.tanh(c * (x + 0.044715 * (x * x * x))))


def _gelu_and_grad(x):
    c = math.sqrt(2.0 / math.pi)
    x2 = x * x
    t = jnp.tanh(c * (x + 0.044715 * (x2 * x)))
    half = 0.5 * (1.0 + t)
    return x * half, half + (0.5 * x) * (1.0 - t * t) * (c + (3.0 * 0.044715 * c) * x2)


def _dot(a, b):
    return jnp.dot(a, b, preferred_element_type=F32)


def _dot_nt(a, b):
    return lax.dot_general(a, b, (((1,), (1,)), ((), ())), preferred_element_type=F32)


def _dot_tn(a, b):
    return lax.dot_general(a, b, (((0,), (0,)), ((), ())), preferred_element_type=F32)


def _tri_mask(n, reverse):
    r = lax.broadcasted_iota(jnp.int32, (n, n), 0)
    c = lax.broadcasted_iota(jnp.int32, (n, n), 1)
    same = (r // CHUNK) == (c // CHUNK)
    tri = (c >= r) if reverse else (c <= r)
    return jnp.where(same & tri, 1.0, 0.0).astype(BF16)


def _tri_apply(tri, x):
    hi = x.astype(BF16)
    r1 = x - hi.astype(F32)
    mid = r1.astype(BF16)
    lo = (r1 - mid.astype(F32)).astype(BF16)
    return _dot(tri, hi) + (_dot(tri, mid) + _dot(tri, lo))


def _all_gather(arrs, name):
    n = len(arrs)

    def body(*refs):
        ins, outs = refs[:n], refs[n:2 * n]
        send_sems, recv_sems, local_sems = refs[2 * n:]
        x, y, c = lax.axis_index("x"), lax.axis_index("y"), lax.axis_index("c")
        me, sibling = (x, y, c), (x, y, 1 - c)
        near = (x + c - 2 * x * c, y + (1 - c) - 2 * y * (1 - c))
        far = (x + (1 - c) - 2 * x * (1 - c), y + c - 2 * y * c)
        diag = (1 - x, 1 - y)

        def blk(a, p):
            return outs[a].at[4 * p[0] + 2 * p[1] + p[2]]

        def copy(a, k, block, to, src=None):
            return pltpu.make_async_remote_copy(
                src_ref=blk(a, block) if src is None else src, dst_ref=blk(a, block),
                send_sem=send_sems.at[7 * a + k], recv_sem=recv_sems.at[7 * a + k],
                device_id=to, device_id_type=MESH)

        mine = [pltpu.make_async_copy(ins[a], blk(a, me), local_sems.at[a]) for a in range(n)]
        for m in mine:
            m.start()
        sends = []
        for a in range(n):
            sends += [copy(a, 0, me, sibling, src=ins[a]), copy(a, 1, me, (*near, c), src=ins[a]),
                      copy(a, 2, me, (*far, c), src=ins[a])]
        for cp in sends:
            cp.start()
        for a in range(n):
            copy(a, 1, (*near, c), me).wait_recv()
            sends.append(copy(a, 3, (*near, c), (*far, c)))
            sends[-1].start()
        for a in range(n):
            sends.append(copy(a, 4, (*near, c), sibling))
            sends[-1].start()
            copy(a, 2, (*far, c), me).wait_recv()
            sends.append(copy(a, 5, (*far, c), sibling))
            sends[-1].start()
        for a in range(n):
            copy(a, 3, (*diag, c), me).wait_recv()
            sends.append(copy(a, 6, (*diag, c), sibling))
            sends[-1].start()
        for a in range(n):
            copy(a, 0, sibling, me).wait_recv()
            copy(a, 4, (*far, 1 - c), me).wait_recv()
            copy(a, 5, (*near, 1 - c), me).wait_recv()
            copy(a, 6, (*diag, 1 - c), me).wait_recv()
        for cp in sends:
            cp.wait_send()
        for m in mine:
            m.wait()

    out_shape = [jax.ShapeDtypeStruct((NDEV,) + a.shape, a.dtype) for a in arrs]
    return _pc(body, name=name, out_shape=out_shape, in_specs=[ANY] * n, out_specs=[ANY] * n,
               scratch=[pltpu.SemaphoreType.DMA((7 * n,)), pltpu.SemaphoreType.DMA((7 * n,)),
                        pltpu.SemaphoreType.DMA((n,))])(*arrs)


def _gather_first(arrs):
    n = len(arrs)

    def parts(ins, outs, ss, rs, base):
        x, y, c = lax.axis_index("x"), lax.axis_index("y"), lax.axis_index("c")
        me, sibling = (x, y, c), (x, y, 1 - c)
        chips = [(1 - x, y), (x, 1 - y), (1 - x, 1 - y)]

        def blk(a, p):
            return outs[a].at[4 * p[0] + 2 * p[1] + p[2]]

        def copy(a, k, block, to):
            return pltpu.make_async_remote_copy(
                src_ref=ins[a], dst_ref=blk(a, block), send_sem=ss.at[base + 4 * a + k],
                recv_sem=rs.at[base + 4 * a + k], device_id=to, device_id_type=MESH)

        local = [pltpu.make_async_copy(ins[a], blk(a, me), ss.at[base + 4 * n + a]) for a in range(n)]
        sends, recvs = [], []
        for a in range(n):
            sends.append(copy(a, 0, me, sibling))
            recvs.append(copy(a, 0, sibling, me))
            for j, chip in enumerate(chips):
                sends.append(copy(a, 1 + j, me, (*chip, c)))
                recvs.append(copy(a, 1 + j, (*chip, c), me))
        return local, sends, recvs

    def start(ins, outs, ss, rs, base):
        local, sends, _ = parts(ins, outs, ss, rs, base)
        for cp in local + sends:
            cp.start()

    def finish(ins, outs, ss, rs, base):
        local, sends, recvs = parts(ins, outs, ss, rs, base)
        for cp in recvs:
            cp.wait_recv()
        for cp in sends:
            cp.wait_send()
        for cp in local:
            cp.wait()

    return _Hosted(arrs, [jax.ShapeDtypeStruct((NDEV,) + a.shape, a.dtype) for a in arrs], 5 * n, start, finish)


def _gather_second(bufs):
    n = len(bufs)

    def parts(ins, outs, ss, rs, base):
        x, y, c = lax.axis_index("x"), lax.axis_index("y"), lax.axis_index("c")
        sibling = (x, y, 1 - c)
        chips = [(1 - x, y), (x, 1 - y), (1 - x, 1 - y)]
        sends, recvs = [], []
        for a in range(n):
            for j, chip in enumerate(chips):
                mine = 4 * chip[0] + 2 * chip[1] + c
                theirs = 4 * chip[0] + 2 * chip[1] + (1 - c)
                sends.append(pltpu.make_async_remote_copy(
                    src_ref=ins[a].at[mine], dst_ref=outs[a].at[mine], send_sem=ss.at[base + 3 * a + j],
                    recv_sem=rs.at[base + 3 * a + j], device_id=sibling, device_id_type=MESH))
                recvs.append(pltpu.make_async_remote_copy(
                    src_ref=ins[a].at[theirs], dst_ref=outs[a].at[theirs], send_sem=ss.at[base + 3 * a + j],
                    recv_sem=rs.at[base + 3 * a + j], device_id=sibling, device_id_type=MESH))
        return sends, recvs

    def start(ins, outs, ss, rs, base):
        for cp in parts(ins, outs, ss, rs, base)[0]:
            cp.start()

    def finish(ins, outs, ss, rs, base):
        sends, recvs = parts(ins, outs, ss, rs, base)
        for cp in recvs:
            cp.wait_recv()
        for cp in sends:
            cp.wait_send()

    return _Hosted(bufs, [jax.ShapeDtypeStruct(b.shape, b.dtype) for b in bufs], 3 * n, start, finish,
                   aliases={a: a for a in range(n)})


def _swap(src, nblk, ids_fn, partner_fn):
    def copies(ins, outs, ss, rs, base):
        x, y, c = lax.axis_index("x"), lax.axis_index("y"), lax.axis_index("c")
        ids = ids_fn(x, y, c)
        partner = partner_fn(x, y, c)
        return [pltpu.make_async_remote_copy(
            src_ref=ins[0].at[ids[k]], dst_ref=outs[0].at[k], send_sem=ss.at[base + k], recv_sem=rs.at[base + k],
            device_id=partner, device_id_type=MESH) for k in range(nblk)]

    def start(ins, outs, ss, rs, base):
        for cp in copies(ins, outs, ss, rs, base):
            cp.start()

    def finish(ins, outs, ss, rs, base):
        for cp in copies(ins, outs, ss, rs, base):
            cp.wait()

    return _Hosted([src], [jax.ShapeDtypeStruct((nblk,) + src.shape[1:], src.dtype)], nblk, start, finish)


def _swap_chips(send):
    def copies(ins, outs, ss, rs, base):
        x, y, c = lax.axis_index("x"), lax.axis_index("y"), lax.axis_index("c")
        chips = [(1 - x, y), (x, 1 - y), (1 - x, 1 - y)]
        return [pltpu.make_async_remote_copy(
            src_ref=ins[0].at[j], dst_ref=outs[0].at[j], send_sem=ss.at[base + j], recv_sem=rs.at[base + j],
            device_id=(*chip, c), device_id_type=MESH) for j, chip in enumerate(chips)]

    def start(ins, outs, ss, rs, base):
        for cp in copies(ins, outs, ss, rs, base):
            cp.start()

    def finish(ins, outs, ss, rs, base):
        for cp in copies(ins, outs, ss, rs, base):
            cp.wait()

    return _Hosted([send], [jax.ShapeDtypeStruct(send.shape, send.dtype)], 3, start, finish)


def _add_send(a, b, idx, ns, name):
    _, r, c = a.shape
    tr = _tile(r, 256)

    def body(idx_ref, a_ref, b_ref, send_ref):
        send_ref[...] = (a_ref[...] + b_ref[...]).astype(BF16)

    def sel(off):
        return pl.BlockSpec((None, tr, c), lambda k, i, s: (s[off + k], i, 0))

    gs = pltpu.PrefetchScalarGridSpec(num_scalar_prefetch=1, grid=(ns, r // tr), in_specs=[sel(0), sel(ns)],
                                      out_specs=pl.BlockSpec((None, tr, c), lambda k, i, s: (k, i, 0)))
    return _pc(body, name=name, grid_spec=gs, sem=("arbitrary", "arbitrary"),
               out_shape=jax.ShapeDtypeStruct((ns, r, c), BF16))(idx, a, b)


class _ReduceScatter:
    def __init__(self, g, tag):
        self.g, self.tag = g, tag

    def swap_core(self):
        return _swap(self.g, 4, lambda x, y, c: [1 - c, 3 - c, 5 - c, 7 - c], lambda x, y, c: (x, y, 1 - c))

    def after_core(self, recv):
        x, y, c = lax.axis_index("x"), lax.axis_index("y"), lax.axis_index("c")
        chips = [(1 - x, y), (x, 1 - y), (1 - x, 1 - y)]
        idx = jnp.stack([4 * p + 2 * q + c for p, q in chips] + [2 * p + q for p, q in chips]).astype(jnp.int32)
        self.send = _add_send(self.g, recv, idx, 3, "rs_add_" + self.tag)
        self.recv_core = recv
        zero = jnp.zeros((), jnp.int32)
        self.idx = jnp.stack([4 * x + 2 * y + c, 2 * x + y, zero, zero + 1, zero + 2]).astype(jnp.int32)

    def swap_chips(self):
        return _swap_chips(self.send)

    def after_chips(self, recv):
        self.parts = [self.g, self.recv_core, recv, recv, recv]


def _ada_fwd(c_all, w_ada, b_cols, b_lb):
    nl, d, ncol = w_ada.shape
    nseq = c_all.shape[0]
    di = b_lb.shape[1]

    def body(c_ref, w_ref, b_ref, lb_ref, mod_ref, lbj_ref):
        cv = c_ref[...]
        cact = (cv * _sigmoid(cv)).astype(BF16)
        for l in range(nl):
            mod_ref[l] = _dot(cact, w_ref[l].astype(BF16)) + b_ref[l]
        b0, b1 = lb_ref[0:1, :], lb_ref[1:2, :]
        mx = jnp.maximum(b0, b1)
        e0, e1 = jnp.exp(b0 - mx), jnp.exp(b1 - mx)
        s = e0 + e1
        p0, p1 = e0 / s, e1 / s
        lbj_ref[0:1, :] = (p0 + p1) - p0
        lbj_ref[1:2, :] = p0 * p1

    return _pc(body, name="ada_fwd",
               out_shape=[jax.ShapeDtypeStruct((nl, nseq, ncol), F32), jax.ShapeDtypeStruct((2, di), F32)]
               )(c_all, w_ada, b_cols, b_lb)


def _ada_bwd(c_all, dmod_cols, dmod_full):
    nl, nseq, ncol = dmod_cols.shape
    d = c_all.shape[1]
    d3 = dmod_full.shape[2]

    def body(c_ref, dc_ref, df_ref, gw_ref, gb_ref):
        cv = c_ref[...]
        cact = (cv * _sigmoid(cv)).astype(BF16)
        for l in range(nl):
            gw_ref[l] = _dot_tn(cact, dc_ref[l].astype(BF16))
            gb_ref[l:l + 1, :] = jnp.sum(df_ref[l], axis=0, keepdims=True)

    return _pc(body, name="ada_bwd",
               out_shape=[jax.ShapeDtypeStruct((nl, d, ncol), F32), jax.ShapeDtypeStruct((nl, d3), F32)]
               )(c_all, dmod_cols, dmod_full)


def _prenorm(x, gain, mod, t_seq, name):
    m, d = x.shape
    tm = _tile(t_seq, 1024)
    per = t_seq // tm

    def body(x_ref, g_ref, mod_ref, h_ref, ht_ref):
        xv = x_ref[...]
        rstd = lax.rsqrt(jnp.mean(xv * xv, axis=-1, keepdims=True) + EPS)
        r = xv * rstd * g_ref[...]
        h = r * (1.0 + mod_ref[0, 1:2, :]) + mod_ref[0, 0:1, :]
        h_ref[...] = h.astype(BF16)
        ht_ref[...] = h.T.astype(BF16)

    return _pc(body, name=name, out_shape=[jax.ShapeDtypeStruct((m, d), BF16), jax.ShapeDtypeStruct((d, m), BF16)],
               grid=(m // tm,),
               in_specs=[pl.BlockSpec((tm, d), lambda i: (i, 0)), pl.BlockSpec((1, d), lambda i: (0, 0)),
                         pl.BlockSpec((1, 3, d), lambda i: (i // per, 0, 0))],
               out_specs=[pl.BlockSpec((tm, d), lambda i: (i, 0)), pl.BlockSpec((d, tm), lambda i: (0, i))],
               sem=("parallel",))(x, gain, mod)


def _prenorm_bwd(dh, x, gain, mod, dxn, t_seq, name):
    m, d = x.shape
    nb = m // t_seq
    tm = _tile(t_seq, 1024)
    per = t_seq // tm

    def body(dh_ref, x_ref, g_ref, mod_ref, dxn_ref, dx_ref, dss_ref, dg_ref):
        i = pl.program_id(0)
        xv, dhv, g = x_ref[...], dh_ref[...], g_ref[...]
        rstd = lax.rsqrt(jnp.mean(xv * xv, axis=-1, keepdims=True) + EPS)
        xhat = xv * rstd
        dr = dhv * (1.0 + mod_ref[0, 1:2, :])
        dxhat = dr * g
        dx_ref[...] = dxn_ref[...] + rstd * (dxhat - xhat * jnp.mean(dxhat * xhat, axis=-1, keepdims=True))

        @pl.when(i % per == 0)
        def _():
            dss_ref[...] = jnp.zeros_like(dss_ref)

        @pl.when(i == 0)
        def _():
            dg_ref[...] = jnp.zeros_like(dg_ref)

        dss_ref[0, 0:1, :] += jnp.sum(dhv, axis=0, keepdims=True)
        dss_ref[0, 1:2, :] += jnp.sum(dhv * (xhat * g), axis=0, keepdims=True)
        dg_ref[...] += jnp.sum(dr * xhat, axis=0, keepdims=True)

    row = pl.BlockSpec((tm, d), lambda i: (i, 0))
    return _pc(body, name=name,
               out_shape=[jax.ShapeDtypeStruct((m, d), F32), jax.ShapeDtypeStruct((nb, 2, d), F32),
                          jax.ShapeDtypeStruct((1, d), F32)],
               grid=(m // tm,),
               in_specs=[row, row, pl.BlockSpec((1, d), lambda i: (0, 0)),
                         pl.BlockSpec((1, 3, d), lambda i: (i // per, 0, 0)), row],
               out_specs=[row, pl.BlockSpec((1, 2, d), lambda i: (i // per, 0, 0)),
                          pl.BlockSpec((1, d), lambda i: (0, 0))],
               sem=("arbitrary",))(dh, x, gain, mod, dxn)


def _mm_in(h, ws, sections, name, comm=None):
    m, k = h.shape
    nw = len(ws)
    widths = [w.shape[2] for w in ws]
    offs = [sum(widths[:a]) for a in range(nw)]
    nc = sum(widths)
    per = NDEV // sections if sections > 1 else NDEV
    tm = _din_tile(m)
    assert per % 2 == 0

    def body(*refs):
        hv = refs[0][...]
        o_ref = refs[1 + nw]
        for b in range(2):
            for a in range(nw):
                lo = b * nc + offs[a]
                o_ref[:, lo:lo + widths[a]] = _dot(hv, refs[1 + a][b])

    w_specs = [pl.BlockSpec((2, k, wd), lambda j, i: (j, 0, 0)) for wd in widths]
    if sections > 1:
        out_shape = jax.ShapeDtypeStruct((sections, m, per * nc), F32)
        out_spec = pl.BlockSpec((None, tm, 2 * nc), lambda j, i: ((2 * j) // per, i, ((2 * j) % per) // 2))
    else:
        out_shape = jax.ShapeDtypeStruct((m, NDEV * nc), F32)
        out_spec = pl.BlockSpec((tm, 2 * nc), lambda j, i: (i, j))
    return _pc(body, name=name, out_shape=out_shape, grid=(NDEV // 2, m // tm),
               in_specs=[pl.BlockSpec((tm, k), lambda j, i: (i, 0))] + w_specs,
               out_specs=out_spec, sem=("parallel", "parallel"), comm=comm)(h, *ws)


def _din_tile(m):
    return 1024 if m % 1024 == 0 and m >= 2048 else _tile(m, 512)


def _mm_din(dproj, ws, sections, name, comm=None, tiles=None, prev=None):
    nw, k = len(ws), ws[0].shape[1]
    widths = [w.shape[2] for w in ws]
    offs = [sum(widths[:a]) for a in range(nw)]
    nc = sum(widths)
    m = dproj.shape[-2]
    tm = _din_tile(m)
    t0, nt = tiles if tiles is not None else (0, m // tm)
    per = NDEV // sections if sections > 1 else NDEV
    assert per % 2 == 0

    def body(*refs):
        d_ref, o_ref = refs[0], refs[-1]
        j = pl.program_id(1)
        acc = None
        for b in range(2):
            for a in range(nw):
                lo = b * nc + offs[a]
                term = _dot_nt(d_ref[:, lo:lo + widths[a]], refs[1 + a][b])
                acc = term if acc is None else acc + term

        @pl.when(j == 0)
        def _():
            o_ref[...] = acc

        @pl.when(j > 0)
        def _():
            o_ref[...] += acc

    if sections > 1:
        dspec = pl.BlockSpec((None, tm, 2 * nc), lambda i, j: ((2 * j) // per, i + t0, ((2 * j) % per) // 2))
    else:
        dspec = pl.BlockSpec((tm, 2 * nc), lambda i, j: (i + t0, j))
    in_specs = [dspec] + [pl.BlockSpec((2, k, wd), lambda i, j: (j, 0, 0)) for wd in widths]
    args = [dproj, *ws]
    if prev is not None:
        in_specs.append(ANY)
        args.append(prev)
    return _pc(body, name=name, out_shape=jax.ShapeDtypeStruct((m, k), F32), grid=(nt, NDEV // 2), in_specs=in_specs,
               out_specs=pl.BlockSpec((tm, k), lambda i, j: (i + t0, 0)), sem=("parallel", "arbitrary"),
               comm=comm, aliases={1 + nw: 0} if prev is not None else None)(*args)


def _mm_dw_in(ht, dproj, nc, sections, name, comm=None):
    k, m = ht.shape
    per = NDEV // sections if sections > 1 else NDEV

    def body(h_ref, d_ref, o_ref):
        o_ref[...] = _dot(h_ref[...], d_ref[...])

    if sections > 1:
        dspec = pl.BlockSpec((None, m, nc), lambda j: (j // per, 0, j % per))
    else:
        dspec = pl.BlockSpec((m, nc), lambda j: (0, j))
    return _pc(body, name=name, out_shape=jax.ShapeDtypeStruct((NDEV, k, nc), F32), grid=(NDEV,),
               in_specs=[pl.BlockSpec((k, m), lambda j: (0, 0)), dspec],
               out_specs=pl.BlockSpec((None, k, nc), lambda j: (j, 0, 0)),
               sem=("parallel",), comm=comm)(ht, dproj)


def _out_proj(ybr, w_out, x, mod, t_seq, name, comm=None):
    m, di = ybr.shape
    d = w_out.shape[1]
    tm = _tile(t_seq, 1024)
    per = t_seq // tm

    def body(y_ref, w_ref, x_ref, mod_ref, yo_ref, xn_ref):
        yo = _dot(y_ref[...], w_ref[...])
        yo_ref[...] = yo
        xn_ref[...] = x_ref[...] + mod_ref[0, 2:3, :] * yo

    row = pl.BlockSpec((tm, d), lambda i: (i, 0))
    return _pc(body, name=name,
               out_shape=[jax.ShapeDtypeStruct((m, d), F32), jax.ShapeDtypeStruct((m, d), F32)],
               grid=(m // tm,),
               in_specs=[pl.BlockSpec((tm, di), lambda i: (i, 0)), pl.BlockSpec((di, d), lambda i: (0, 0)), row,
                         pl.BlockSpec((1, 3, d), lambda i: (i // per, 0, 0))],
               out_specs=[row, row], sem=("parallel",), comm=comm)(ybr, w_out, x, mod)


def _out_proj_loss(ybr, w_out, x, mod, gain, target, t_seq):
    m, di = ybr.shape
    d = w_out.shape[1]
    tm = _tile(t_seq, 512)
    per = t_seq // tm

    def body(y_ref, w_ref, x_ref, mod_ref, g_ref, t_ref, yo_ref, dx_ref, loss_ref, dg_ref):
        i = pl.program_id(0)
        yo = _dot(y_ref[...], w_ref[...])
        yo_ref[...] = yo
        xv = x_ref[...] + mod_ref[0, 2:3, :] * yo
        g = g_ref[...]
        rstd = lax.rsqrt(jnp.mean(xv * xv, axis=-1, keepdims=True) + EPS)
        xhat = xv * rstd
        err = xhat * g - t_ref[...]
        dy = err * (1.0 / d)
        dxhat = dy * g
        dx_ref[...] = rstd * (dxhat - xhat * jnp.mean(dxhat * xhat, axis=-1, keepdims=True))

        @pl.when(i == 0)
        def _():
            loss_ref[...] = jnp.zeros_like(loss_ref)
            dg_ref[...] = jnp.zeros_like(dg_ref)

        loss_ref[...] += 0.5 * jnp.sum(jnp.mean(err * err, axis=-1, keepdims=True), axis=0, keepdims=True)
        dg_ref[...] += jnp.sum(dy * xhat, axis=0, keepdims=True)

    row = pl.BlockSpec((tm, d), lambda i: (i, 0))
    vec = pl.BlockSpec((1, d), lambda i: (0, 0))
    return _pc(body, name="out_proj_loss",
               out_shape=[jax.ShapeDtypeStruct((m, d), F32), jax.ShapeDtypeStruct((m, d), F32),
                          jax.ShapeDtypeStruct((1, 1), F32), jax.ShapeDtypeStruct((1, d), F32)],
               grid=(m // tm,),
               in_specs=[pl.BlockSpec((tm, di), lambda i: (i, 0)), pl.BlockSpec((di, d), lambda i: (0, 0)), row,
                         pl.BlockSpec((1, 3, d), lambda i: (i // per, 0, 0)), vec, row],
               out_specs=[row, row, pl.BlockSpec((1, 1), lambda i: (0, 0)), vec],
               sem=("arbitrary",))(ybr, w_out, x, mod, gain, target)


def _gate_dybr(dxn, yout, mod, w_out, t_seq, name):
    m, d = dxn.shape
    di = w_out.shape[0]
    nb = m // t_seq
    tm = _tile(t_seq, 1024)
    per = t_seq // tm

    def body(dxn_ref, yo_ref, mod_ref, w_ref, dy_ref, dgate_ref, o_ref):
        i = pl.program_id(0)
        dv = dxn_ref[...]
        dy = (mod_ref[0, 2:3, :] * dv).astype(BF16)
        dy_ref[...] = dy
        o_ref[...] = _dot_nt(dy, w_ref[...])

        @pl.when(i % per == 0)
        def _():
            dgate_ref[...] = jnp.zeros_like(dgate_ref)

        dgate_ref[0] += jnp.sum(dv * yo_ref[...], axis=0, keepdims=True)

    row = pl.BlockSpec((tm, d), lambda i: (i, 0))
    return _pc(body, name=name,
               out_shape=[jax.ShapeDtypeStruct((m, d), BF16), jax.ShapeDtypeStruct((nb, 1, d), F32),
                          jax.ShapeDtypeStruct((m, di), F32)],
               grid=(m // tm,),
               in_specs=[row, row, pl.BlockSpec((1, 3, d), lambda i: (i // per, 0, 0)),
                         pl.BlockSpec((di, d), lambda i: (0, 0))],
               out_specs=[row, pl.BlockSpec((1, 1, d), lambda i: (i // per, 0, 0)),
                          pl.BlockSpec((tm, di), lambda i: (i, 0))],
               sem=("arbitrary",))(dxn, yout, mod, w_out)


def _mm_dw_out(ybr, dy, name, comm=None):
    m, di = ybr.shape
    d = dy.shape[1]
    tn = _tile(di, 1024)

    def body(y_ref, dy_ref, o_ref):
        o_ref[...] = _dot_tn(y_ref[...], dy_ref[...])

    return _pc(body, name=name, out_shape=jax.ShapeDtypeStruct((di, d), F32), grid=(di // tn,),
               in_specs=[pl.BlockSpec((m, tn), lambda n: (0, n)), pl.BlockSpec((m, d), lambda n: (0, 0))],
               out_specs=pl.BlockSpec((tn, d), lambda n: (n, 0)), sem=("parallel",), comm=comm)(ybr, dy)


def _sgu_mask():
    t = lax.broadcasted_iota(jnp.int32, (SG_BLOCK, SG_BLOCK), 0)
    s = lax.broadcasted_iota(jnp.int32, (SG_BLOCK, SG_BLOCK), 1)
    return (s // CHUNK) <= (t // CHUNK)


def _a_mid_fwd(proj, ln_g, ln_b, w_s, bs_t, t_seq, comm=None):
    m, n3 = proj.shape
    di = n3 // 3
    gd = di // SG_GROUPS
    r = _tile(t_seq, 256)
    nblk = r // SG_BLOCK

    def body(p_ref, lg_ref, lb_ref, ws_ref, bs_ref, ybr_ref, s_scr):
        v = _gelu(p_ref[:, di:2 * di])
        mu = jnp.mean(v, axis=-1, keepdims=True)
        vc = v - mu
        rstd = lax.rsqrt(jnp.mean(vc * vc, axis=-1, keepdims=True) + EPS)
        vb = (vc * rstd * lg_ref[...] + lb_ref[...]).astype(BF16)
        mask = _sgu_mask()
        for gi in range(SG_GROUPS):
            ws = jnp.where(mask, ws_ref[gi], 0.0).astype(BF16)
            bcol = bs_ref[:, gi:gi + 1]
            for b in range(nblk):
                rows = slice(b * SG_BLOCK, (b + 1) * SG_BLOCK)
                cols = slice(gi * gd, (gi + 1) * gd)
                s_scr[rows, cols] = _dot(ws, vb[rows, cols]) + bcol
        gg = p_ref[:, 2 * di:]
        ybr_ref[...] = (_gelu(p_ref[:, :di]) * s_scr[...] * (gg * _sigmoid(gg))).astype(BF16)

    vec = pl.BlockSpec((1, di), lambda i: (0, 0))
    return _pc(body, name="a_mid_fwd", out_shape=jax.ShapeDtypeStruct((m, di), BF16), grid=(m // r,),
               in_specs=[pl.BlockSpec((r, n3), lambda i: (i, 0)), vec, vec,
                         pl.BlockSpec((SG_GROUPS, SG_BLOCK, SG_BLOCK), lambda i: (0, 0, 0)),
                         pl.BlockSpec((SG_BLOCK, 128), lambda i: (0, 0))],
               out_specs=pl.BlockSpec((r, di), lambda i: (i, 0)),
               scratch=[pltpu.VMEM((r, di), F32)], sem=("parallel",), comm=comm)(proj, ln_g, ln_b, w_s, bs_t)


def _a_mid_bwd(proj, dybr, ln_g, ln_b, w_s, bs_t, t_seq, comm=None):
    m, n3 = proj.shape
    di = n3 // 3
    gd = di // SG_GROUPS
    r = _tile(t_seq, 256)
    nblk = r // SG_BLOCK

    def body(p_ref, dy_ref, lg_ref, lb_ref, ws_ref, bs_ref,
             dp_ref, dlg_ref, dlb_ref, dws_ref, dbs_ref, s_scr, dvl_scr):
        i = pl.program_id(0)

        @pl.when(i == 0)
        def _():
            dlg_ref[...] = jnp.zeros_like(dlg_ref)
            dlb_ref[...] = jnp.zeros_like(dlb_ref)
            dws_ref[...] = jnp.zeros_like(dws_ref)
            dbs_ref[...] = jnp.zeros_like(dbs_ref)

        v, dgelu_v = _gelu_and_grad(p_ref[:, di:2 * di])
        mu = jnp.mean(v, axis=-1, keepdims=True)
        vc = v - mu
        rstd = lax.rsqrt(jnp.mean(vc * vc, axis=-1, keepdims=True) + EPS)
        vhat = vc * rstd
        lg = lg_ref[...]
        vb = (vhat * lg + lb_ref[...]).astype(BF16)
        u, dgelu_u = _gelu_and_grad(p_ref[:, :di])
        gg = p_ref[:, 2 * di:]
        sg = _sigmoid(gg)
        dyv = dy_ref[...]
        dus = dyv * (gg * sg)
        dsb = (dus * u).astype(BF16)
        ds32 = dus * u
        mask = _sgu_mask()
        lane = lax.broadcasted_iota(jnp.int32, (SG_BLOCK, 128), 1)
        dbs_acc = jnp.zeros((SG_BLOCK, 128), F32)
        for gi in range(SG_GROUPS):
            ws = jnp.where(mask, ws_ref[gi], 0.0).astype(BF16)
            bcol = bs_ref[:, gi:gi + 1]
            cols = slice(gi * gd, (gi + 1) * gd)
            dws_acc = jnp.zeros((SG_BLOCK, SG_BLOCK), F32)
            dbs_col = jnp.zeros((SG_BLOCK, 1), F32)
            for b in range(nblk):
                rows = slice(b * SG_BLOCK, (b + 1) * SG_BLOCK)
                s_scr[rows, cols] = _dot(ws, vb[rows, cols]) + bcol
                dvl_scr[rows, cols] = _dot_tn(ws, dsb[rows, cols])
                dws_acc += _dot_nt(dsb[rows, cols], vb[rows, cols])
                dbs_col += jnp.sum(ds32[rows, cols], axis=-1, keepdims=True)
            dws_ref[gi] += jnp.where(mask, dws_acc, 0.0)
            dbs_acc += jnp.where(lane == gi, dbs_col, 0.0)
        dbs_ref[...] += dbs_acc
        s = s_scr[...]
        dp_ref[:, :di] = (dus * s * dgelu_u).astype(BF16)
        dp_ref[:, 2 * di:] = (dyv * u * s * (sg * (1.0 + gg * (1.0 - sg)))).astype(BF16)
        dvl = dvl_scr[...]
        dlg_ref[...] += jnp.sum(dvl * vhat, axis=0, keepdims=True)
        dlb_ref[...] += jnp.sum(dvl, axis=0, keepdims=True)
        dvh = dvl * lg
        dv = rstd * (dvh - jnp.mean(dvh, axis=-1, keepdims=True)
                     - vhat * jnp.mean(dvh * vhat, axis=-1, keepdims=True))
        dp_ref[:, di:2 * di] = (dv * dgelu_v).astype(BF16)

    vec = pl.BlockSpec((1, di), lambda i: (0, 0))
    wsb = pl.BlockSpec((SG_GROUPS, SG_BLOCK, SG_BLOCK), lambda i: (0, 0, 0))
    bsb = pl.BlockSpec((SG_BLOCK, 128), lambda i: (0, 0))
    return _pc(body, name="a_mid_bwd",
               out_shape=[jax.ShapeDtypeStruct((m, n3), BF16), jax.ShapeDtypeStruct((1, di), F32),
                          jax.ShapeDtypeStruct((1, di), F32),
                          jax.ShapeDtypeStruct((SG_GROUPS, SG_BLOCK, SG_BLOCK), F32),
                          jax.ShapeDtypeStruct((SG_BLOCK, 128), F32)],
               grid=(m // r,),
               in_specs=[pl.BlockSpec((r, n3), lambda i: (i, 0)), pl.BlockSpec((r, di), lambda i: (i, 0)),
                         vec, vec, wsb, bsb],
               out_specs=[pl.BlockSpec((r, n3), lambda i: (i, 0)), vec, vec, wsb, bsb],
               scratch=[pltpu.VMEM((r, di), F32), pltpu.VMEM((r, di), F32)],
               sem=("arbitrary",), comm=comm)(proj, dybr, ln_g, ln_b, w_s, bs_t)


def _chunk_rows(n):
    if isinstance(n, int):
        return pl.ds(n * CHUNK, CHUNK)
    return pl.ds(pl.multiple_of(n * CHUNK, CHUNK), CHUNK)


def _hgrn_dims(t_seq, di):
    tr = _tile(t_seq, 128)
    hc = _tile(di, 2048)
    return tr, hc, hc // HEAD_DIM


def _hgrn_gates(f_ref, lb, a_scr, k_scr, tr):
    sig = _sigmoid(f_ref[...])
    fg = lb + (1.0 - lb) * sig
    k_scr[...] = 1.0 - fg
    logf = jnp.log(fg)
    g = min(CUM_ROWS, tr)
    tri = _tri_mask(g, reverse=False)
    for rg in range(tr // g):
        a_scr[rg * g:(rg + 1) * g, :] = _tri_apply(tri, logf[rg * g:(rg + 1) * g, :])
    return sig, fg


def _hgrn_fwd(proj, lbj, gn, nb, t_seq):
    _, m, di = proj.shape
    tr, hc, hpg = _hgrn_dims(t_seq, di)
    nt, nhg, ncl = t_seq // tr, di // hc, tr // CHUNK
    nheads = di // HEAD_DIM

    def body(p_ref, lb_ref, gn_ref, o_ref, ybr_ref, st_ref, st_scr, a_scr, k_scr):
        q_ref, f_ref, i_ref, g_ref = (p_ref.at[s] for s in range(4))
        t = pl.program_id(2)

        @pl.when(t == 0)
        def _():
            st_scr[...] = jnp.zeros_like(st_scr)

        _hgrn_gates(f_ref, lb_ref[0:1, :], a_scr, k_scr, tr)
        gnv = gn_ref[...]
        rr = lax.broadcasted_iota(jnp.int32, (CHUNK, CHUNK), 0)
        cc = lax.broadcasted_iota(jnp.int32, (CHUNK, CHUNK), 1)
        causal = cc <= rr

        def chunk(n, carry):
            rows = _chunk_rows(n)
            lanes = [slice(hd * HEAD_DIM, (hd + 1) * HEAD_DIM) for hd in range(hpg)]
            hs = []
            for hd, ls in enumerate(lanes):
                h = {}
                ah, kh = a_scr[rows, ls], k_scr[rows, ls]
                qp = q_ref[rows, ls]
                qh = qp * _sigmoid(qp)
                h["vb"] = i_ref[rows, ls].astype(BF16)
                aref, alast = ah[CHUNK // 2 - 1:CHUNK // 2, :], ah[CHUNK - 1:CHUNK, :]
                h["q_in"] = (qh * jnp.exp(ah - aref)).astype(BF16)
                h["k_in"] = (kh * jnp.exp(aref - ah)).astype(BF16)
                h["q_out"] = (qh * jnp.exp(ah)).astype(BF16)
                h["k_out"] = (kh * jnp.exp(alast - ah)).astype(BF16)
                h["dec"] = jnp.exp(alast)
                st = st_scr[hd]
                st_ref[n, hd] = st
                h["st"] = st
                hs.append(h)
            for h in hs:
                h["scores"] = _dot_nt(h["q_in"], h["k_in"])
                h["o_inter"] = _dot_nt(h["q_out"], h["st"].astype(BF16))
                h["st_mm"] = _dot_tn(h["vb"], h["k_out"])
            for h in hs:
                h["o"] = _dot(jnp.where(causal, h["scores"], 0.0).astype(BF16), h["vb"]) + h["o_inter"]
            for hd, (h, ls) in enumerate(zip(hs, lanes)):
                st_scr[hd] = h["st"] * h["dec"] + h["st_mm"]
                o = h["o"]
                o_ref[rows, ls] = o
                rstd = lax.rsqrt(jnp.mean(o * o, axis=-1, keepdims=True) + EPS)
                gg = g_ref[rows, ls]
                ybr_ref[rows, ls] = ((o * rstd * gnv) * (gg * _sigmoid(gg))).astype(BF16)
            return carry

        lax.fori_loop(0, ncl, chunk, 0)

    blk = pl.BlockSpec((tr, hc), lambda hg, b, t: (b * nt + t, hg))
    return _pc(body, name="hgrn_fwd",
               out_shape=[jax.ShapeDtypeStruct((m, di), F32), jax.ShapeDtypeStruct((m, di), BF16),
                          jax.ShapeDtypeStruct((m // CHUNK, nheads, HEAD_DIM, HEAD_DIM), F32)],
               grid=(nhg, nb, nt),
               in_specs=[pl.BlockSpec((4, tr, hc), lambda hg, b, t: (0, b * nt + t, hg)),
                         pl.BlockSpec((2, hc), lambda hg, b, t: (0, hg)),
                         pl.BlockSpec((1, HEAD_DIM), lambda hg, b, t: (0, 0))],
               out_specs=[blk, blk, pl.BlockSpec((ncl, hpg, HEAD_DIM, HEAD_DIM),
                                                 lambda hg, b, t: (b * nt + t, hg, 0, 0))],
               scratch=[pltpu.VMEM((hpg, HEAD_DIM, HEAD_DIM), F32), pltpu.VMEM((tr, hc), F32),
                        pltpu.VMEM((tr, hc), F32)],
               sem=("parallel", "arbitrary", "arbitrary"))(proj, lbj, gn)


def _hgrn_bwd(proj, o_all, dybr, states, lbj, gn, nb, t_seq, comm=None):
    _, m, di = proj.shape
    tr, hc, hpg = _hgrn_dims(t_seq, di)
    nt, nhg, ncl = t_seq // tr, di // hc, tr // CHUNK

    def body(p_ref, o_ref, dy_ref, st_ref, lb_ref, gn_ref,
             dp_ref, dlb_ref, dgn_ref, dst_scr, a_scr, k_scr, da_scr, dk_scr):
        q_ref, f_ref, i_ref, g_ref = (p_ref.at[s] for s in range(4))
        hg, b, t = pl.program_id(0), pl.program_id(1), pl.program_id(2)

        @pl.when(t == 0)
        def _():
            dst_scr[...] = jnp.zeros_like(dst_scr)

        @pl.when((b == 0) & (t == 0))
        def _():
            dlb_ref[...] = jnp.zeros_like(dlb_ref)

        @pl.when((hg == 0) & (b == 0) & (t == 0))
        def _():
            dgn_ref[...] = jnp.zeros_like(dgn_ref)

        lb = lb_ref[0:1, :]
        sig, fg = _hgrn_gates(f_ref, lb, a_scr, k_scr, tr)
        gnv = gn_ref[...]
        rr = lax.broadcasted_iota(jnp.int32, (CHUNK, CHUNK), 0)
        cc = lax.broadcasted_iota(jnp.int32, (CHUNK, CHUNK), 1)
        causal = cc <= rr
        rowi = lax.broadcasted_iota(jnp.int32, (CHUNK, HEAD_DIM), 0)

        def chunk(it, carry):
            n = ncl - 1 - it
            rows = _chunk_rows(n)
            for hd0 in range(0, hpg, PHASE_HEADS):
                heads(n, rows, range(hd0, min(hpg, hd0 + PHASE_HEADS)))
            return carry

        def heads(n, rows, ids):
            lanes = [slice(hd * HEAD_DIM, (hd + 1) * HEAD_DIM) for hd in ids]
            hs = []
            for hd, ls in zip(ids, lanes):
                h = {}
                ah, kh = a_scr[rows, ls], k_scr[rows, ls]
                qp = q_ref[rows, ls]
                sq = _sigmoid(qp)
                qh = qp * sq
                h["dsilu_q"] = sq * (1.0 + qp * (1.0 - sq))
                h["vb"] = i_ref[rows, ls].astype(BF16)
                aref, alast = ah[CHUNK // 2 - 1:CHUNK // 2, :], ah[CHUNK - 1:CHUNK, :]
                h["e1"], h["e2"] = jnp.exp(ah - aref), jnp.exp(aref - ah)
                h["e3"], h["e4"] = jnp.exp(ah), jnp.exp(alast - ah)
                h["dec"] = jnp.exp(alast)
                h["q_in"], h["k_in"], h["q_out"], h["k_out"] = qh * h["e1"], kh * h["e2"], qh * h["e3"], kh * h["e4"]
                for nm in ("q_in", "k_in", "q_out", "k_out"):
                    h[nm + "_b"] = h[nm].astype(BF16)
                o = o_ref[rows, ls]
                rstd = lax.rsqrt(jnp.mean(o * o, axis=-1, keepdims=True) + EPS)
                ohat = o * rstd
                gg = g_ref[rows, ls]
                sg = _sigmoid(gg)
                dyv = dy_ref[rows, ls]
                d_on = dyv * (gg * sg)
                dp_ref[3, rows, ls] = (dyv * (ohat * gnv) * (sg * (1.0 + gg * (1.0 - sg)))).astype(BF16)
                h["dgn"] = jnp.sum(d_on * ohat, axis=0, keepdims=True)
                dohat = d_on * gnv
                do = rstd * (dohat - ohat * jnp.mean(dohat * ohat, axis=-1, keepdims=True))
                h["do_b"] = do.astype(BF16)
                h["st_prev"] = st_ref[n, hd]
                h["dst"] = dst_scr[hd]
                hs.append(h)
            for h in hs:
                dst_b = h["dst"].astype(BF16)
                h["scores"] = _dot_nt(h["q_in_b"], h["k_in_b"])
                h["dscores"] = _dot_nt(h["do_b"], h["vb"])
                h["dv_inter"] = _dot_nt(h["k_out_b"], dst_b)
                h["dq_out"] = _dot(h["do_b"], h["st_prev"].astype(BF16))
                h["dk_out"] = _dot(h["vb"], dst_b)
                h["dst_mm"] = _dot_tn(h["do_b"], h["q_out_b"])
            for h in hs:
                scores = jnp.where(causal, h["scores"], 0.0).astype(BF16)
                dscores = jnp.where(causal, h["dscores"], 0.0).astype(BF16)
                h["dv"] = _dot_tn(scores, h["do_b"]) + h["dv_inter"]
                h["dq_in"] = _dot(dscores, h["k_in_b"])
                h["dk_in"] = _dot_tn(dscores, h["q_in_b"])
            dgn = hs[0]["dgn"]
            for h in hs[1:]:
                dgn = dgn + h["dgn"]
            dgn_ref[...] += dgn
            for hd, h, ls in zip(ids, hs, lanes):
                ddec = jnp.sum(h["dst"] * h["st_prev"], axis=0, keepdims=True)
                dst_scr[hd] = h["dst"] * h["dec"] + h["dst_mm"]
                dp_ref[2, rows, ls] = h["dv"].astype(BF16)
                dq = h["dq_in"] * h["e1"] + h["dq_out"] * h["e3"]
                dp_ref[0, rows, ls] = (dq * h["dsilu_q"]).astype(BF16)
                dk_scr[rows, ls] = h["dk_in"] * h["e2"] + h["dk_out"] * h["e4"]
                t_in = h["dq_in"] * h["q_in"] - h["dk_in"] * h["k_in"]
                t_out = h["dk_out"] * h["k_out"]
                da = t_in + h["dq_out"] * h["q_out"] - t_out
                da_ref_row = -jnp.sum(t_in, axis=0, keepdims=True)
                da_last_row = jnp.sum(t_out, axis=0, keepdims=True) + ddec * h["dec"]
                da = da + jnp.where(rowi == CHUNK // 2 - 1, da_ref_row, 0.0) \
                        + jnp.where(rowi == CHUNK - 1, da_last_row, 0.0)
                da_scr[rows, ls] = da

        if ncl <= 2:
            for it in range(ncl):
                chunk(it, 0)
        else:
            lax.fori_loop(0, ncl, chunk, 0)
        g = min(CUM_ROWS, tr)
        tri = _tri_mask(g, reverse=True)
        for rg in range(tr // g):
            rs = slice(rg * g, (rg + 1) * g)
            dlogf = _tri_apply(tri, da_scr[rs, :])
            df = dlogf / fg[rs, :] - dk_scr[rs, :]
            sgr = sig[rs, :]
            dp_ref[1, rs, :] = (df * (1.0 - lb) * (sgr * (1.0 - sgr))).astype(BF16)
            dlb_ref[...] += jnp.sum(df * (1.0 - sgr), axis=0, keepdims=True) * lb_ref[1:2, :]

    blk = pl.BlockSpec((tr, hc), lambda hg, b, t: (b * nt + (nt - 1 - t), hg))
    return _pc(body, name="hgrn_bwd",
               out_shape=[jax.ShapeDtypeStruct((4, m, di), BF16), jax.ShapeDtypeStruct((1, di), F32),
                          jax.ShapeDtypeStruct((1, HEAD_DIM), F32)],
               grid=(nhg, nb, nt),
               in_specs=[pl.BlockSpec((4, tr, hc), lambda hg, b, t: (0, b * nt + (nt - 1 - t), hg)), blk, blk,
                         pl.BlockSpec((ncl, hpg, HEAD_DIM, HEAD_DIM),
                                      lambda hg, b, t: (b * nt + (nt - 1 - t), hg, 0, 0)),
                         pl.BlockSpec((2, hc), lambda hg, b, t: (0, hg)),
                         pl.BlockSpec((1, HEAD_DIM), lambda hg, b, t: (0, 0))],
               out_specs=[pl.BlockSpec((4, tr, hc), lambda hg, b, t: (0, b * nt + (nt - 1 - t), hg)),
                          pl.BlockSpec((1, hc), lambda hg, b, t: (0, hg)),
                          pl.BlockSpec((1, HEAD_DIM), lambda hg, b, t: (0, 0))],
               scratch=[pltpu.VMEM((hpg, HEAD_DIM, HEAD_DIM), F32)] + [pltpu.VMEM((tr, hc), F32)] * 4,
               sem=("arbitrary", "arbitrary", "arbitrary"), comm=comm)(
                   proj, o_all, dybr, states, lbj, gn)


def _adamw(parts, w, m, v, name):
    r, c = w.shape
    tr = _tile(r, 256)
    npart = len(parts)
    c1 = 1.0 - ADAM_B1 ** ADAM_STEP
    c2 = 1.0 - ADAM_B2 ** ADAM_STEP

    def body(*refs):
        p_refs = refs[:npart]
        _adamw_math(p_refs, *refs[npart:], c1, c2)

    blk = pl.BlockSpec((tr, c), lambda i: (i, 0))
    return _pc(body, name=name, out_shape=[jax.ShapeDtypeStruct((r, c), F32)] * 4, grid=(r // tr,),
               in_specs=[blk] * (npart + 3), out_specs=[blk] * 4, sem=("parallel",))(*parts, w, m, v)


def _adamw_math(p_refs, w_ref, m_ref, v_ref, g_ref, d_ref, nm_ref, nv_ref, c1, c2):
    g = p_refs[0][...].astype(F32)
    for p in p_refs[1:]:
        g = g + p[...].astype(F32)
    nm = ADAM_B1 * m_ref[...] + (1.0 - ADAM_B1) * g
    nv = ADAM_B2 * v_ref[...] + (1.0 - ADAM_B2) * (g * g)
    g_ref[...] = g
    nm_ref[...] = nm
    nv_ref[...] = nv
    d_ref[...] = -ADAM_LR * ((nm / c1) / (jnp.sqrt(nv / c2) + ADAM_EPS) + ADAM_WD * w_ref[...])


def _adamw_blocks(parts, idx, w, m, v, name):
    r, c = w.shape
    tr = _tile(r, 256)
    npart = len(parts)
    c1 = 1.0 - ADAM_B1 ** ADAM_STEP
    c2 = 1.0 - ADAM_B2 ** ADAM_STEP

    def body(idx_ref, *refs):
        _adamw_math(refs[:npart], *refs[npart:], c1, c2)

    def sel(p):
        return pl.BlockSpec((None, tr, c), lambda i, s: (s[p], i, 0))

    blk = pl.BlockSpec((tr, c), lambda i, s: (i, 0))
    gs = pltpu.PrefetchScalarGridSpec(num_scalar_prefetch=1, grid=(r // tr,),
                                      in_specs=[sel(p) for p in range(npart)] + [blk] * 3, out_specs=[blk] * 4)
    return _pc(body, name=name, out_shape=[jax.ShapeDtypeStruct((r, c), F32)] * 4, grid_spec=gs,
               sem=("parallel",))(idx, *parts, w, m, v)


_EARLY = ["a_ln_gain", "a_ln_bias", "a_w_s", "a_b_s", "b_lower_bounds", "b_gn_gain"]


def _pack(arrs):
    flat = jnp.concatenate([a.reshape(-1) for a in arrs])
    rows = -(-flat.shape[0] // 1024) * 8
    return jnp.pad(flat, (0, rows * 128 - flat.shape[0])).reshape(rows, 128)


def _unpack(buf, like):
    flat = buf.reshape(-1)
    out, off = [], 0
    for a in like:
        out.append(flat[off:off + a.size].reshape(a.shape))
        off += a.size
    return out


def kernel(x, c, norm_gain, w_ada, b_ada, a_w_in, a_ln_gain, a_ln_bias, a_w_s, a_b_s, a_w_out, b_w_in, b_lower_bounds, b_gn_gain, b_w_out, final_gain, loss_target, m_norm_gain, m_w_ada, m_b_ada, m_a_w_in, m_a_ln_gain, m_a_ln_bias, m_a_w_s, m_a_b_s, m_a_w_out, m_b_w_in, m_b_lower_bounds, m_b_gn_gain, m_b_w_out, m_final_gain, v_norm_gain, v_w_ada, v_b_ada, v_a_w_in, v_a_ln_gain, v_a_ln_bias, v_a_w_s, v_a_b_s, v_a_w_out, v_b_w_in, v_b_lower_bounds, v_b_gn_gain, v_b_w_out, v_final_gain):
    w = dict(norm_gain=norm_gain, w_ada=w_ada, b_ada=b_ada, a_w_in=a_w_in, a_ln_gain=a_ln_gain,
             a_ln_bias=a_ln_bias, a_w_s=a_w_s, a_b_s=a_b_s, a_w_out=a_w_out, b_w_in=b_w_in,
             b_lower_bounds=b_lower_bounds, b_gn_gain=b_gn_gain, b_w_out=b_w_out, final_gain=final_gain)
    mo = dict(norm_gain=m_norm_gain, w_ada=m_w_ada, b_ada=m_b_ada, a_w_in=m_a_w_in, a_ln_gain=m_a_ln_gain,
              a_ln_bias=m_a_ln_bias, a_w_s=m_a_w_s, a_b_s=m_a_b_s, a_w_out=m_a_w_out, b_w_in=m_b_w_in,
              b_lower_bounds=m_b_lower_bounds, b_gn_gain=m_b_gn_gain, b_w_out=m_b_w_out, final_gain=m_final_gain)
    vo = dict(norm_gain=v_norm_gain, w_ada=v_w_ada, b_ada=v_b_ada, a_w_in=v_a_w_in, a_ln_gain=v_a_ln_gain,
              a_ln_bias=v_a_ln_bias, a_w_s=v_a_w_s, a_b_s=v_a_b_s, a_w_out=v_a_w_out, b_w_in=v_b_w_in,
              b_lower_bounds=v_b_lower_bounds, b_gn_gain=v_b_gn_gain, b_w_out=v_b_w_out, final_gain=v_final_gain)

    nb, t_seq, d = x.shape
    m = nb * t_seq
    ncol_ada = w_ada.shape[2]
    xi, yi, ci = lax.axis_index("x"), lax.axis_index("y"), lax.axis_index("c")
    me = 4 * xi + 2 * yi + ci

    c_g, wa_in_g = _all_gather([c, a_w_in[0].astype(BF16)], "gather_c_wa")

    c_all = c_g.reshape(NDEV * nb, d)
    b_cols = lax.dynamic_slice(b_ada, (0, me * ncol_ada), (2, ncol_ada)).reshape(2, 1, ncol_ada)
    mod_part, lbj = _ada_fwd(c_all, w_ada, b_cols, b_lower_bounds)
    mod_all = _all_gather([mod_part], "gather_mod")[0]
    mod_mine = lax.dynamic_slice_in_dim(mod_all, me * nb, nb, axis=2)
    mod_mine = mod_mine.transpose(1, 2, 0, 3).reshape(2, nb, 3, d)
    mod0, mod1 = mod_mine[0], mod_mine[1]

    di = a_w_out.shape[1] * NDEV

    xf = x.reshape(m, d)
    tgt = loss_target.reshape(m, d)
    ng0, ng1 = norm_gain[0:1], norm_gain[1:2]
    ncb = b_w_in.shape[2]
    wb_lo, wb_hi = b_w_in[0][:, :ncb // 2].astype(BF16), b_w_in[0][:, ncb // 2:].astype(BF16)
    h0, h0_t = _prenorm(xf, ng0, mod0, t_seq, "prenorm_a")
    proj_a, half = _mm_in(h0, [wa_in_g], 1, "in_proj_a", comm=_gather_first([a_w_out[0].astype(BF16), wb_lo]))
    bs_t = jnp.pad(a_b_s[0].T, ((0, 0), (0, 128 - SG_GROUPS)))
    ybr_a, (wa_out_g, wb_lo_g, wb_hi_half) = _a_mid_fwd(
        proj_a, a_ln_gain, a_ln_bias, a_w_s[0], bs_t, t_seq, comm=_join(_gather_second(half), _gather_first([wb_hi])))
    wa_out = wa_out_g.reshape(di, d)
    (yout_a, x1), (wb_hi_g, wb_out_half) = _out_proj(
        ybr_a, wa_out, xf, mod0, t_seq, "out_proj_a",
        comm=_join(_gather_second([wb_hi_half]), _gather_first([b_w_out[0].astype(BF16)])))
    wb_in_g = [wb_lo_g, wb_hi_g]
    h1, h1_t = _prenorm(x1, ng1, mod1, t_seq, "prenorm_b")
    proj_b, (wb_out_g,) = _mm_in(h1, wb_in_g, 4, "in_proj_b", comm=_gather_second([wb_out_half]))
    wb_out = wb_out_g.reshape(di, d)
    o_b, ybr_b, states = _hgrn_fwd(proj_b, lbj, b_gn_gain, nb, t_seq)
    yout_b, dx2, loss_part, d_final_gain = _out_proj_loss(ybr_b, wb_out, x1, mod1, final_gain.reshape(1, d), tgt, t_seq)

    rows_out = a_w_out.shape[1]
    dy_b, dgate1, dybr_b = _gate_dybr(dx2, yout_b, mod1, wb_out, t_seq, "dybr_b")
    rs_wb_out = _ReduceScatter(_mm_dw_out(ybr_b, dy_b, "dw_out_b").reshape(NDEV, rows_out, d), "b_w_out")
    (dproj_b, d_lb, d_gn), got = _hgrn_bwd(proj_b, o_b, dybr_b, states, lbj, b_gn_gain, nb, t_seq,
                                           comm=rs_wb_out.swap_core())
    rs_wb_out.after_core(got[0])
    dh1, got = _mm_din(dproj_b, wb_in_g, 4, "dh_b", comm=rs_wb_out.swap_chips())
    rs_wb_out.after_chips(got[0])
    dx1, dss1, dgain1 = _prenorm_bwd(dh1, x1, ng1, mod1, dx2, t_seq, "prenorm_bwd_b")
    rs_wb_in = _ReduceScatter(_mm_dw_in(h1_t, dproj_b, ncb, 4, "dw_in_b"), "b_w_in")

    dy_a, dgate0, dybr_a = _gate_dybr(dx1, yout_a, mod0, wa_out, t_seq, "dybr_a")
    g_wa_out, got = _mm_dw_out(ybr_a, dy_a, "dw_out_a", comm=rs_wb_in.swap_core())
    rs_wb_in.after_core(got[0])
    rs_wa_out = _ReduceScatter(g_wa_out.reshape(NDEV, rows_out, d), "a_w_out")
    (dproj_a, d_lng, d_lnb, d_ws, d_bs_t), got = _a_mid_bwd(
        proj_a, dybr_a, a_ln_gain, a_ln_bias, a_w_s[0], bs_t, t_seq,
        comm=_join(rs_wb_in.swap_chips(), rs_wa_out.swap_core()))
    rs_wb_in.after_chips(got[0])
    rs_wa_out.after_core(got[1])
    part = dict(a_ln_gain=d_lng, a_ln_bias=d_lnb, a_w_s=d_ws[None], a_b_s=d_bs_t[:, :SG_GROUPS].T[None],
                b_lower_bounds=jnp.concatenate([-d_lb, d_lb], axis=0), b_gn_gain=d_gn)
    early_pack = _pack([part[k].reshape(w[k].shape) for k in _EARLY])
    g_wa_in, got = _mm_dw_in(h0_t, dproj_a, wa_in_g.shape[2], 1, "dw_in_a", comm=rs_wa_out.swap_chips())
    rs_wa_out.after_chips(got[0])
    rs_wa_in = _ReduceScatter(g_wa_in, "a_w_in")
    n_tiles = m // _din_tile(m)
    assert n_tiles >= 2
    first_tiles = max(1, (3 * n_tiles) // 8)
    dh0, got = _mm_din(dproj_a, [wa_in_g], 1, "dh_a_first", tiles=(0, first_tiles),
                       comm=_join(rs_wa_in.swap_core(), _gather_first([early_pack])))
    rs_wa_in.after_core(got[0])
    dh0, got2 = _mm_din(dproj_a, [wa_in_g], 1, "dh_a_rest", tiles=(first_tiles, n_tiles - first_tiles), prev=dh0,
                        comm=_join(_gather_second([got[1]]), rs_wa_in.swap_chips()))
    early_all = got2[0]
    rs_wa_in.after_chips(got2[1])
    dx0, dss0, dgain0 = _prenorm_bwd(dh0, xf, ng0, mod0, dx1, t_seq, "prenorm_bwd_a")
    grad_x = dx0.reshape(nb, t_seq, d)

    dmod = jnp.stack([jnp.concatenate([dss0, dgate0], axis=1), jnp.concatenate([dss1, dgate1], axis=1)])
    late_like = [norm_gain, final_gain, loss_part.reshape(1)]
    late_pack = _pack([jnp.concatenate([dgain0, dgain1], axis=0), d_final_gain[0], loss_part.reshape(1)])
    dmod_all, late_all = _all_gather([dmod.reshape(2, nb, 3 * d), late_pack], "gather_tail")
    dmod_all = dmod_all.transpose(1, 0, 2, 3).reshape(2, NDEV * nb, 3 * d)
    dmod_cols = lax.dynamic_slice_in_dim(dmod_all, me * ncol_ada, ncol_ada, axis=2)
    g_w_ada, g_b_ada = _ada_bwd(c_all, dmod_cols, dmod_all)

    res = {}
    early_like = [w[k] for k in _EARLY]
    dev_order = jnp.arange(NDEV, dtype=jnp.int32)
    sm = _adamw_blocks([early_all] * NDEV, dev_order, _pack(early_like), _pack([mo[k] for k in _EARLY]),
                       _pack([vo[k] for k in _EARLY]), "adamw_small_early")
    sm = [dict(zip(_EARLY, _unpack(buf, early_like))) for buf in sm]
    for k in _EARLY:
        res[k] = tuple(s[k] for s in sm)
    zero = jnp.zeros((1,), F32)
    sm = _adamw_blocks([late_all] * NDEV, dev_order, _pack([norm_gain, final_gain, zero]),
                       _pack([mo["norm_gain"], mo["final_gain"], zero]),
                       _pack([vo["norm_gain"], vo["final_gain"], zero]), "adamw_small_late")
    sm = [_unpack(buf, late_like) for buf in sm]
    res["norm_gain"] = tuple(s[0] for s in sm)
    res["final_gain"] = tuple(s[1] for s in sm)
    loss = sm[0][2][0]
    rb = _adamw([g_b_ada], b_ada, mo["b_ada"], vo["b_ada"], "adamw_b_ada")
    res["b_ada"] = tuple(rb)
    sh = w_ada.shape
    ra = _adamw([g_w_ada.reshape(sh[0] * sh[1], sh[2])], w_ada.reshape(sh[0] * sh[1], sh[2]),
                mo["w_ada"].reshape(sh[0] * sh[1], sh[2]), vo["w_ada"].reshape(sh[0] * sh[1], sh[2]), "adamw_w_ada")
    res["w_ada"] = tuple(z.reshape(sh) for z in ra)

    for k, rs in (("b_w_out", rs_wb_out), ("b_w_in", rs_wb_in), ("a_w_out", rs_wa_out), ("a_w_in", rs_wa_in)):
        res[k] = tuple(z[None] for z in _adamw_blocks(rs.parts, rs.idx, w[k][0], mo[k][0], vo[k][0], "adamw_" + k))

    order = ["norm_gain", "w_ada", "b_ada", "a_w_in", "a_ln_gain", "a_ln_bias", "a_w_s", "a_b_s", "a_w_out",
             "b_w_in", "b_lower_bounds", "b_gn_gain", "b_w_out", "final_gain"]
    return (loss, grad_x, *[res[k][0] for k in order], *[res[k][1] for k in order],
            *[res[k][2] for k in order], *[res[k][3] for k in order])
```

```python
import functools
import math

import jax
import jax.numpy as jnp
from jax import lax
from jax.experimental import pallas as pl
from jax.experimental.pallas import tpu as pltpu

F32 = jnp.float32
BF16 = jnp.bfloat16
MESH = pl.DeviceIdType.MESH
NDEV = 8
EPS = 1e-6
CHUNK = 64
SG_BLOCK = 128
SG_GROUPS = 8
HEAD_DIM = 128
CUM_ROWS = 256
PHASE_HEADS = 8
ADAM_LR, ADAM_B1, ADAM_B2, ADAM_EPS, ADAM_WD, ADAM_STEP = 0.001, 0.9, 0.999, 1e-08, 0.01, 10
VMEM_LIMIT = 56 * 1024 * 1024
ANY = pl.BlockSpec(memory_space=pl.ANY)


class _Hosted:
    def __init__(self, arrays, out_shapes, nsem, start, finish, aliases=None):
        self.arrays, self.out_shapes, self.nsem = list(arrays), list(out_shapes), nsem
        self.start, self.finish = start, finish
        self.aliases = dict(aliases or {})


def _join(*comms):
    arrays, outs, aliases, offs, nsem = [], [], {}, [], 0
    for cm in comms:
        offs.append((len(arrays), len(outs), nsem))
        for i, o in cm.aliases.items():
            aliases[len(arrays) + i] = len(outs) + o
        arrays += cm.arrays
        outs += cm.out_shapes
        nsem += cm.nsem

    def run(which):
        def f(ins, outs_, ss, rs, base):
            for cm, (ia, io, isem) in zip(comms, offs):
                getattr(cm, which)(ins[ia:ia + len(cm.arrays)], outs_[io:io + len(cm.out_shapes)], ss, rs, base + isem)
        return f

    return _Hosted(arrays, outs, nsem, run("start"), run("finish"), aliases)


def _pc(body, *, name, out_shape, grid=None, in_specs=None, out_specs=None, scratch=(), sem=None,
        grid_spec=None, comm=None, aliases=None):
    cp = dict(vmem_limit_bytes=VMEM_LIMIT)
    aliases = dict(aliases or {})
    if comm is None:
        if sem is not None:
            cp["dimension_semantics"] = sem
        kw = {"input_output_aliases": aliases}
        if grid_spec is not None:
            kw["grid_spec"] = grid_spec
        else:
            if grid is not None:
                kw["grid"] = grid
            if in_specs is not None:
                kw["in_specs"] = in_specs
            if out_specs is not None:
                kw["out_specs"] = out_specs
            kw["scratch_shapes"] = list(scratch)
        return pl.pallas_call(functools.partial(body), name=name, out_shape=out_shape,
                              compiler_params=pltpu.CompilerParams(**cp), **kw)

    single = not isinstance(out_shape, (list, tuple))
    outs_list = [out_shape] if single else list(out_shape)
    ospecs = [out_specs] if single else list(out_specs)
    n_in, n_out, n_ci, n_co, n_scr = len(in_specs), len(outs_list), len(comm.arrays), len(comm.out_shapes), len(scratch)
    cp["dimension_semantics"] = ("arbitrary",) * len(grid)

    def hosted(*refs):
        cin, hin = refs[:n_in], refs[n_in:n_in + n_ci]
        cout = refs[n_in + n_ci:n_in + n_ci + n_out]
        hout = refs[n_in + n_ci + n_out:n_in + n_ci + n_out + n_co]
        scr = refs[n_in + n_ci + n_out + n_co:n_in + n_ci + n_out + n_co + n_scr]
        ssem, rsem = refs[-2], refs[-1]
        first = functools.reduce(lambda p, q: p & q, [pl.program_id(a) == 0 for a in range(len(grid))])
        last = functools.reduce(lambda p, q: p & q, [pl.program_id(a) == grid[a] - 1 for a in range(len(grid))])

        @pl.when(first)
        def _():
            comm.start(hin, hout, ssem, rsem, 0)

        body(*cin, *cout, *scr)

        @pl.when(last)
        def _():
            comm.finish(hin, hout, ssem, rsem, 0)

    call = pl.pallas_call(
        hosted, name=name, grid=grid, in_specs=list(in_specs) + [ANY] * n_ci, out_specs=ospecs + [ANY] * n_co,
        out_shape=outs_list + comm.out_shapes,
        scratch_shapes=list(scratch) + [pltpu.SemaphoreType.DMA((comm.nsem,)), pltpu.SemaphoreType.DMA((comm.nsem,))],
        input_output_aliases={**aliases, **{n_in + i: n_out + o for i, o in comm.aliases.items()}},
        compiler_params=pltpu.CompilerParams(**cp))

    def run(*args):
        res = call(*args, *comm.arrays)
        comp = res[:n_out]
        return (comp[0] if single else comp), list(res[n_out:])

    return run


def _tile(n, pref):
    return pref if n % pref == 0 else n


def _sigmoid(x):
    return 1.0 / (1.0 + jnp.exp(-x))


def _gelu(x):
    c = math.sqrt(2.0 / math.pi)
    return 0.5 * x * (1.0 + jnp.tanh(c * (x + 0.044715 * (x * x * x))))


def _gelu_and_grad(x):
    c = math.sqrt(2.0 / math.pi)
    x2 = x * x
    t = jnp.tanh(c * (x + 0.044715 * (x2 * x)))
    half = 0.5 * (1.0 + t)
    return x * half, half + (0.5 * x) * (1.0 - t * t) * (c + (3.0 * 0.044715 * c) * x2)


def _dot(a, b):
    return jnp.dot(a, b, preferred_element_type=F32)


def _dot_nt(a, b):
    return lax.dot_general(a, b, (((1,), (1,)), ((), ())), preferred_element_type=F32)


def _dot_tn(a, b):
    return lax.dot_general(a, b, (((0,), (0,)), ((), ())), preferred_element_type=F32)


def _tri_mask(n, reverse):
    r = lax.broadcasted_iota(jnp.int32, (n, n), 0)
    c = lax.broadcasted_iota(jnp.int32, (n, n), 1)
    same = (r // CHUNK) == (c // CHUNK)
    tri = (c >= r) if reverse else (c <= r)
    return jnp.where(same & tri, 1.0, 0.0).astype(BF16)


def _tri_apply(tri, x):
    hi = x.astype(BF16)
    r1 = x - hi.astype(F32)
    mid = r1.astype(BF16)
    lo = (r1 - mid.astype(F32)).astype(BF16)
    return _dot(tri, hi) + (_dot(tri, mid) + _dot(tri, lo))


def _all_gather(arrs, name):
    n = len(arrs)

    def body(*refs):
        ins, outs = refs[:n], refs[n:2 * n]
        send_sems, recv_sems, local_sems = refs[2 * n:]
        x, y, c = lax.axis_index("x"), lax.axis_index("y"), lax.axis_index("c")
        me, sibling = (x, y, c), (x, y, 1 - c)
        near = (x + c - 2 * x * c, y + (1 - c) - 2 * y * (1 - c))
        far = (x + (1 - c) - 2 * x * (1 - c), y + c - 2 * y * c)
        diag = (1 - x, 1 - y)

        def blk(a, p):
            return outs[a].at[4 * p[0] + 2 * p[1] + p[2]]

        def copy(a, k, block, to, src=None):
            return pltpu.make_async_remote_copy(
                src_ref=blk(a, block) if src is None else src, dst_ref=blk(a, block),
                send_sem=send_sems.at[7 * a + k], recv_sem=recv_sems.at[7 * a + k],
                device_id=to, device_id_type=MESH)

        mine = [pltpu.make_async_copy(ins[a], blk(a, me), local_sems.at[a]) for a in range(n)]
        for m in mine:
            m.start()
        sends = []
        for a in range(n):
            sends += [copy(a, 0, me, sibling, src=ins[a]), copy(a, 1, me, (*near, c), src=ins[a]),
                      copy(a, 2, me, (*far, c), src=ins[a])]
        for cp in sends:
            cp.start()
        for a in range(n):
            copy(a, 1, (*near, c), me).wait_recv()
            sends.append(copy(a, 3, (*near, c), (*far, c)))
            sends[-1].start()
        for a in range(n):
            sends.append(copy(a, 4, (*near, c), sibling))
            sends[-1].start()
            copy(a, 2, (*far, c), me).wait_recv()
            sends.append(copy(a, 5, (*far, c), sibling))
            sends[-1].start()
        for a in range(n):
            copy(a, 3, (*diag, c), me).wait_recv()
            sends.append(copy(a, 6, (*diag, c), sibling))
            sends[-1].start()
        for a in range(n):
            copy(a, 0, sibling, me).wait_recv()
            copy(a, 4, (*far, 1 - c), me).wait_recv()
            copy(a, 5, (*near, 1 - c), me).wait_recv()
            copy(a, 6, (*diag, 1 - c), me).wait_recv()
        for cp in sends:
            cp.wait_send()
        for m in mine:
            m.wait()

    out_shape = [jax.ShapeDtypeStruct((NDEV,) + a.shape, a.dtype) for a in arrs]
    return _pc(body, name=name, out_shape=out_shape, in_specs=[ANY] * n, out_specs=[ANY] * n,
               scratch=[pltpu.SemaphoreType.DMA((7 * n,)), pltpu.SemaphoreType.DMA((7 * n,)),
                        pltpu.SemaphoreType.DMA((n,))])(*arrs)


def _gather_first(arrs):
    n = len(arrs)

    def parts(ins, outs, ss, rs, base):
        x, y, c = lax.axis_index("x"), lax.axis_index("y"), lax.axis_index("c")
        me, sibling = (x, y, c), (x, y, 1 - c)
        chips = [(1 - x, y), (x, 1 - y), (1 - x, 1 - y)]

        def blk(a, p):
            return outs[a].at[4 * p[0] + 2 * p[1] + p[2]]

        def copy(a, k, block, to):
            return pltpu.make_async_remote_copy(
                src_ref=ins[a], dst_ref=blk(a, block), send_sem=ss.at[base + 4 * a + k],
                recv_sem=rs.at[base + 4 * a + k], device_id=to, device_id_type=MESH)

        local = [pltpu.make_async_copy(ins[a], blk(a, me), ss.at[base + 4 * n + a]) for a in range(n)]
        sends, recvs = [], []
        for a in range(n):
            sends.append(copy(a, 0, me, sibling))
            recvs.append(copy(a, 0, sibling, me))
            for j, chip in enumerate(chips):
                sends.append(copy(a, 1 + j, me, (*chip, c)))
                recvs.append(copy(a, 1 + j, (*chip, c), me))
        return local, sends, recvs

    def start(ins, outs, ss, rs, base):
        local, sends, _ = parts(ins, outs, ss, rs, base)
        for cp in local + sends:
            cp.start()

    def finish(ins, outs, ss, rs, base):
        local, sends, recvs = parts(ins, outs, ss, rs, base)
        for cp in recvs:
            cp.wait_recv()
        for cp in sends:
            cp.wait_send()
        for cp in local:
            cp.wait()

    return _Hosted(arrs, [jax.ShapeDtypeStruct((NDEV,) + a.shape, a.dtype) for a in arrs], 5 * n, start, finish)


def _gather_second(bufs):
    n = len(bufs)

    def parts(ins, outs, ss, rs, base):
        x, y, c = lax.axis_index("x"), lax.axis_index("y"), lax.axis_index("c")
        sibling = (x, y, 1 - c)
        chips = [(1 - x, y), (x, 1 - y), (1 - x, 1 - y)]
        sends, recvs = [], []
        for a in range(n):
            for j, chip in enumerate(chips):
                mine = 4 * chip[0] + 2 * chip[1] + c
                theirs = 4 * chip[0] + 2 * chip[1] + (1 - c)
                sends.append(pltpu.make_async_remote_copy(
                    src_ref=ins[a].at[mine], dst_ref=outs[a].at[mine], send_sem=ss.at[base + 3 * a + j],
                    recv_sem=rs.at[base + 3 * a + j], device_id=sibling, device_id_type=MESH))
                recvs.append(pltpu.make_async_remote_copy(
                    src_ref=ins[a].at[theirs], dst_ref=outs[a].at[theirs], send_sem=ss.at[base + 3 * a + j],
                    recv_sem=rs.at[base + 3 * a + j], device_id=sibling, device_id_type=MESH))
        return sends, recvs

    def start(ins, outs, ss, rs, base):
        for cp in parts(ins, outs, ss, rs, base)[0]:
            cp.start()

    def finish(ins, outs, ss, rs, base):
        sends, recvs = parts(ins, outs, ss, rs, base)
        for cp in recvs:
            cp.wait_recv()
        for cp in sends:
            cp.wait_send()

    return _Hosted(bufs, [jax.ShapeDtypeStruct(b.shape, b.dtype) for b in bufs], 3 * n, start, finish,
                   aliases={a: a for a in range(n)})


def _swap(src, nblk, ids_fn, partner_fn):
    def copies(ins, outs, ss, rs, base):
        x, y, c = lax.axis_index("x"), lax.axis_index("y"), lax.axis_index("c")
        ids = ids_fn(x, y, c)
        partner = partner_fn(x, y, c)
        return [pltpu.make_async_remote_copy(
            src_ref=ins[0].at[ids[k]], dst_ref=outs[0].at[k], send_sem=ss.at[base + k], recv_sem=rs.at[base + k],
            device_id=partner, device_id_type=MESH) for k in range(nblk)]

    def start(ins, outs, ss, rs, base):
        for cp in copies(ins, outs, ss, rs, base):
            cp.start()

    def finish(ins, outs, ss, rs, base):
        for cp in copies(ins, outs, ss, rs, base):
            cp.wait()

    return _Hosted([src], [jax.ShapeDtypeStruct((nblk,) + src.shape[1:], src.dtype)], nblk, start, finish)


def _swap_chips(send):
    def copies(ins, outs, ss, rs, base):
        x, y, c = lax.axis_index("x"), lax.axis_index("y"), lax.axis_index("c")
        chips = [(1 - x, y), (x, 1 - y), (1 - x, 1 - y)]
        return [pltpu.make_async_remote_copy(
            src_ref=ins[0].at[j], dst_ref=outs[0].at[j], send_sem=ss.at[base + j], recv_sem=rs.at[base + j],
            device_id=(*chip, c), device_id_type=MESH) for j, chip in enumerate(chips)]

    def start(ins, outs, ss, rs, base):
        for cp in copies(ins, outs, ss, rs, base):
            cp.start()

    def finish(ins, outs, ss, rs, base):
        for cp in copies(ins, outs, ss, rs, base):
            cp.wait()

    return _Hosted([send], [jax.ShapeDtypeStruct(send.shape, send.dtype)], 3, start, finish)


def _add_send(a, b, idx, ns, name):
    _, r, c = a.shape
    tr = _tile(r, 256)

    def body(idx_ref, a_ref, b_ref, send_ref):
        send_ref[...] = (a_ref[...] + b_ref[...]).astype(BF16)

    def sel(off):
        return pl.BlockSpec((None, tr, c), lambda k, i, s: (s[off + k], i, 0))

    gs = pltpu.PrefetchScalarGridSpec(num_scalar_prefetch=1, grid=(ns, r // tr), in_specs=[sel(0), sel(ns)],
                                      out_specs=pl.BlockSpec((None, tr, c), lambda k, i, s: (k, i, 0)))
    return _pc(body, name=name, grid_spec=gs, sem=("arbitrary", "arbitrary"),
               out_shape=jax.ShapeDtypeStruct((ns, r, c), BF16))(idx, a, b)


class _ReduceScatter:
    def __init__(self, g, tag):
        self.g, self.tag = g, tag

    def swap_core(self):
        return _swap(self.g, 4, lambda x, y, c: [1 - c, 3 - c, 5 - c, 7 - c], lambda x, y, c: (x, y, 1 - c))

    def after_core(self, recv):
        x, y, c = lax.axis_index("x"), lax.axis_index("y"), lax.axis_index("c")
        chips = [(1 - x, y), (x, 1 - y), (1 - x, 1 - y)]
        idx = jnp.stack([4 * p + 2 * q + c for p, q in chips] + [2 * p + q for p, q in chips]).astype(jnp.int32)
        self.send = _add_send(self.g, recv, idx, 3, "rs_add_" + self.tag)
        self.recv_core = recv
        zero = jnp.zeros((), jnp.int32)
        self.idx = jnp.stack([4 * x + 2 * y + c, 2 * x + y, zero, zero + 1, zero + 2]).astype(jnp.int32)

    def swap_chips(self):
        return _swap_chips(self.send)

    def after_chips(self, recv):
        self.parts = [self.g, self.recv_core, recv, recv, recv]


def _ada_fwd(c_all, w_ada, b_cols, b_lb):
    nl, d, ncol = w_ada.shape
    nseq = c_all.shape[0]
    di = b_lb.shape[1]

    def body(c_ref, w_ref, b_ref, lb_ref, mod_ref, lbj_ref):
        cv = c_ref[...]
        cact = (cv * _sigmoid(cv)).astype(BF16)
        for l in range(nl):
            mod_ref[l] = _dot(cact, w_ref[l].astype(BF16)) + b_ref[l]
        b0, b1 = lb_ref[0:1, :], lb_ref[1:2, :]
        mx = jnp.maximum(b0, b1)
        e0, e1 = jnp.exp(b0 - mx), jnp.exp(b1 - mx)
        s = e0 + e1
        p0, p1 = e0 / s, e1 / s
        lbj_ref[0:1, :] = (p0 + p1) - p0
        lbj_ref[1:2, :] = p0 * p1

    return _pc(body, name="ada_fwd",
               out_shape=[jax.ShapeDtypeStruct((nl, nseq, ncol), F32), jax.ShapeDtypeStruct((2, di), F32)]
               )(c_all, w_ada, b_cols, b_lb)


def _ada_bwd(c_all, dmod_cols, dmod_full):
    nl, nseq, ncol = dmod_cols.shape
    d = c_all.shape[1]
    d3 = dmod_full.shape[2]

    def body(c_ref, dc_ref, df_ref, gw_ref, gb_ref):
        cv = c_ref[...]
        cact = (cv * _sigmoid(cv)).astype(BF16)
        for l in range(nl):
            gw_ref[l] = _dot_tn(cact, dc_ref[l].astype(BF16))
            gb_ref[l:l + 1, :] = jnp.sum(df_ref[l], axis=0, keepdims=True)

    return _pc(body, name="ada_bwd",
               out_shape=[jax.ShapeDtypeStruct((nl, d, ncol), F32), jax.ShapeDtypeStruct((nl, d3), F32)]
               )(c_all, dmod_cols, dmod_full)


def _prenorm(x, gain, mod, t_seq, name):
    m, d = x.shape
    tm = _tile(t_seq, 1024)
    per = t_seq // tm

    def body(x_ref, g_ref, mod_ref, h_ref, ht_ref):
        xv = x_ref[...]
        rstd = lax.rsqrt(jnp.mean(xv * xv, axis=-1, keepdims=True) + EPS)
        r = xv * rstd * g_ref[...]
        h = r * (1.0 + mod_ref[0, 1:2, :]) + mod_ref[0, 0:1, :]
        h_ref[...] = h.astype(BF16)
        ht_ref[...] = h.T.astype(BF16)

    return _pc(body, name=name, out_shape=[jax.ShapeDtypeStruct((m, d), BF16), jax.ShapeDtypeStruct((d, m), BF16)],
               grid=(m // tm,),
               in_specs=[pl.BlockSpec((tm, d), lambda i: (i, 0)), pl.BlockSpec((1, d), lambda i: (0, 0)),
                         pl.BlockSpec((1, 3, d), lambda i: (i // per, 0, 0))],
               out_specs=[pl.BlockSpec((tm, d), lambda i: (i, 0)), pl.BlockSpec((d, tm), lambda i: (0, i))],
               sem=("parallel",))(x, gain, mod)


def _prenorm_bwd(dh, x, gain, mod, dxn, t_seq, name):
    m, d = x.shape
    nb = m // t_seq
    tm = _tile(t_seq, 1024)
    per = t_seq // tm

    def body(dh_ref, x_ref, g_ref, mod_ref, dxn_ref, dx_ref, dss_ref, dg_ref):
        i = pl.program_id(0)
        xv, dhv, g = x_ref[...], dh_ref[...], g_ref[...]
        rstd = lax.rsqrt(jnp.mean(xv * xv, axis=-1, keepdims=True) + EPS)
        xhat = xv * rstd
        dr = dhv * (1.0 + mod_ref[0, 1:2, :])
        dxhat = dr * g
        dx_ref[...] = dxn_ref[...] + rstd * (dxhat - xhat * jnp.mean(dxhat * xhat, axis=-1, keepdims=True))

        @pl.when(i % per == 0)
        def _():
            dss_ref[...] = jnp.zeros_like(dss_ref)

        @pl.when(i == 0)
        def _():
            dg_ref[...] = jnp.zeros_like(dg_ref)

        dss_ref[0, 0:1, :] += jnp.sum(dhv, axis=0, keepdims=True)
        dss_ref[0, 1:2, :] += jnp.sum(dhv * (xhat * g), axis=0, keepdims=True)
        dg_ref[...] += jnp.sum(dr * xhat, axis=0, keepdims=True)

    row = pl.BlockSpec((tm, d), lambda i: (i, 0))
    return _pc(body, name=name,
               out_shape=[jax.ShapeDtypeStruct((m, d), F32), jax.ShapeDtypeStruct((nb, 2, d), F32),
                          jax.ShapeDtypeStruct((1, d), F32)],
               grid=(m // tm,),
               in_specs=[row, row, pl.BlockSpec((1, d), lambda i: (0, 0)),
                         pl.BlockSpec((1, 3, d), lambda i: (i // per, 0, 0)), row],
               out_specs=[row, pl.BlockSpec((1, 2, d), lambda i: (i // per, 0, 0)),
                          pl.BlockSpec((1, d), lambda i: (0, 0))],
               sem=("arbitrary",))(dh, x, gain, mod, dxn)


def _mm_in(h, ws, sections, name, comm=None):
    m, k = h.shape
    nw = len(ws)
    widths = [w.shape[2] for w in ws]
    offs = [sum(widths[:a]) for a in range(nw)]
    nc = sum(widths)
    per = NDEV // sections if sections > 1 else NDEV
    tm = _din_tile(m)
    assert per % 2 == 0

    def body(*refs):
        hv = refs[0][...]
        o_ref = refs[1 + nw]
        for b in range(2):
            for a in range(nw):
                lo = b * nc + offs[a]
                o_ref[:, lo:lo + widths[a]] = _dot(hv, refs[1 + a][b])

    w_specs = [pl.BlockSpec((2, k, wd), lambda j, i: (j, 0, 0)) for wd in widths]
    if sections > 1:
        out_shape = jax.ShapeDtypeStruct((sections, m, per * nc), F32)
        out_spec = pl.BlockSpec((None, tm, 2 * nc), lambda j, i: ((2 * j) // per, i, ((2 * j) % per) // 2))
    else:
        out_shape = jax.ShapeDtypeStruct((m, NDEV * nc), F32)
        out_spec = pl.BlockSpec((tm, 2 * nc), lambda j, i: (i, j))
    return _pc(body, name=name, out_shape=out_shape, grid=(NDEV // 2, m // tm),
               in_specs=[pl.BlockSpec((tm, k), lambda j, i: (i, 0))] + w_specs,
               out_specs=out_spec, sem=("parallel", "parallel"), comm=comm)(h, *ws)


def _din_tile(m):
    return 1024 if m % 1024 == 0 and m >= 2048 else _tile(m, 512)


def _mm_din(dproj, ws, sections, name, comm=None, tiles=None, prev=None):
    nw, k = len(ws), ws[0].shape[1]
    widths = [w.shape[2] for w in ws]
    offs = [sum(widths[:a]) for a in range(nw)]
    nc = sum(widths)
    m = dproj.shape[-2]
    tm = _din_tile(m)
    t0, nt = tiles if tiles is not None else (0, m // tm)
    per = NDEV // sections if sections > 1 else NDEV
    assert per % 2 == 0

    def body(*refs):
        d_ref, o_ref = refs[0], refs[-1]
        j = pl.program_id(1)
        acc = None
        for b in range(2):
            for a in range(nw):
                lo = b * nc + offs[a]
                term = _dot_nt(d_ref[:, lo:lo + widths[a]], refs[1 + a][b])
                acc = term if acc is None else acc + term

        @pl.when(j == 0)
        def _():
            o_ref[...] = acc

        @pl.when(j > 0)
        def _():
            o_ref[...] += acc

    if sections > 1:
        dspec = pl.BlockSpec((None, tm, 2 * nc), lambda i, j: ((2 * j) // per, i + t0, ((2 * j) % per) // 2))
    else:
        dspec = pl.BlockSpec((tm, 2 * nc), lambda i, j: (i + t0, j))
    in_specs = [dspec] + [pl.BlockSpec((2, k, wd), lambda i, j: (j, 0, 0)) for wd in widths]
    args = [dproj, *ws]
    if prev is not None:
        in_specs.append(ANY)
        args.append(prev)
    return _pc(body, name=name, out_shape=jax.ShapeDtypeStruct((m, k), F32), grid=(nt, NDEV // 2), in_specs=in_specs,
               out_specs=pl.BlockSpec((tm, k), lambda i, j: (i + t0, 0)), sem=("parallel", "arbitrary"),
               comm=comm, aliases={1 + nw: 0} if prev is not None else None)(*args)


def _mm_dw_in(ht, dproj, nc, sections, name, comm=None):
    k, m = ht.shape
    per = NDEV // sections if sections > 1 else NDEV

    def body(h_ref, d_ref, o_ref):
        o_ref[...] = _dot(h_ref[...], d_ref[...])

    if sections > 1:
        dspec = pl.BlockSpec((None, m, nc), lambda j: (j // per, 0, j % per))
    else:
        dspec = pl.BlockSpec((m, nc), lambda j: (0, j))
    return _pc(body, name=name, out_shape=jax.ShapeDtypeStruct((NDEV, k, nc), F32), grid=(NDEV,),
               in_specs=[pl.BlockSpec((k, m), lambda j: (0, 0)), dspec],
               out_specs=pl.BlockSpec((None, k, nc), lambda j: (j, 0, 0)),
               sem=("parallel",), comm=comm)(ht, dproj)


def _out_proj(ybr, w_out, x, mod, t_seq, name, comm=None):
    m, di = ybr.shape
    d = w_out.shape[1]
    tm = _tile(t_seq, 1024)
    per = t_seq // tm

    def body(y_ref, w_ref, x_ref, mod_ref, yo_ref, xn_ref):
        yo = _dot(y_ref[...], w_ref[...])
        yo_ref[...] = yo
        xn_ref[...] = x_ref[...] + mod_ref[0, 2:3, :] * yo

    row = pl.BlockSpec((tm, d), lambda i: (i, 0))
    return _pc(body, name=name,
               out_shape=[jax.ShapeDtypeStruct((m, d), F32), jax.ShapeDtypeStruct((m, d), F32)],
               grid=(m // tm,),
               in_specs=[pl.BlockSpec((tm, di), lambda i: (i, 0)), pl.BlockSpec((di, d), lambda i: (0, 0)), row,
                         pl.BlockSpec((1, 3, d), lambda i: (i // per, 0, 0))],
               out_specs=[row, row], sem=("parallel",), comm=comm)(ybr, w_out, x, mod)


def _out_proj_loss(ybr, w_out, x, mod, gain, target, t_seq):
    m, di = ybr.shape
    d = w_out.shape[1]
    tm = _tile(t_seq, 512)
    per = t_seq // tm

    def body(y_ref, w_ref, x_ref, mod_ref, g_ref, t_ref, yo_ref, dx_ref, loss_ref, dg_ref):
        i = pl.program_id(0)
        yo = _dot(y_ref[...], w_ref[...])
        yo_ref[...] = yo
        xv = x_ref[...] + mod_ref[0, 2:3, :] * yo
        g = g_ref[...]
        rstd = lax.rsqrt(jnp.mean(xv * xv, axis=-1, keepdims=True) + EPS)
        xhat = xv * rstd
        err = xhat * g - t_ref[...]
        dy = err * (1.0 / d)
        dxhat = dy * g
        dx_ref[...] = rstd * (dxhat - xhat * jnp.mean(dxhat * xhat, axis=-1, keepdims=True))

        @pl.when(i == 0)
        def _():
            loss_ref[...] = jnp.zeros_like(loss_ref)
            dg_ref[...] = jnp.zeros_like(dg_ref)

        loss_ref[...] += 0.5 * jnp.sum(jnp.mean(err * err, axis=-1, keepdims=True), axis=0, keepdims=True)
        dg_ref[...] += jnp.sum(dy * xhat, axis=0, keepdims=True)

    row = pl.BlockSpec((tm, d), lambda i: (i, 0))
    vec = pl.BlockSpec((1, d), lambda i: (0, 0))
    return _pc(body, name="out_proj_loss",
               out_shape=[jax.ShapeDtypeStruct((m, d), F32), jax.ShapeDtypeStruct((m, d), F32),
                          jax.ShapeDtypeStruct((1, 1), F32), jax.ShapeDtypeStruct((1, d), F32)],
               grid=(m // tm,),
               in_specs=[pl.BlockSpec((tm, di), lambda i: (i, 0)), pl.BlockSpec((di, d), lambda i: (0, 0)), row,
                         pl.BlockSpec((1, 3, d), lambda i: (i // per, 0, 0)), vec, row],
               out_specs=[row, row, pl.BlockSpec((1, 1), lambda i: (0, 0)), vec],
               sem=("arbitrary",))(ybr, w_out, x, mod, gain, target)


def _gate_dybr(dxn, yout, mod, w_out, t_seq, name):
    m, d = dxn.shape
    di = w_out.shape[0]
    nb = m // t_seq
    tm = _tile(t_seq, 1024)
    per = t_seq // tm

    def body(dxn_ref, yo_ref, mod_ref, w_ref, dy_ref, dgate_ref, o_ref):
        i = pl.program_id(0)
        dv = dxn_ref[...]
        dy = (mod_ref[0, 2:3, :] * dv).astype(BF16)
        dy_ref[...] = dy
        o_ref[...] = _dot_nt(dy, w_ref[...])

        @pl.when(i % per == 0)
        def _():
            dgate_ref[...] = jnp.zeros_like(dgate_ref)

        dgate_ref[0] += jnp.sum(dv * yo_ref[...], axis=0, keepdims=True)

    row = pl.BlockSpec((tm, d), lambda i: (i, 0))
    return _pc(body, name=name,
               out_shape=[jax.ShapeDtypeStruct((m, d), BF16), jax.ShapeDtypeStruct((nb, 1, d), F32),
                          jax.ShapeDtypeStruct((m, di), F32)],
               grid=(m // tm,),
               in_specs=[row, row, pl.BlockSpec((1, 3, d), lambda i: (i // per, 0, 0)),
                         pl.BlockSpec((di, d), lambda i: (0, 0))],
               out_specs=[row, pl.BlockSpec((1, 1, d), lambda i: (i // per, 0, 0)),
                          pl.BlockSpec((tm, di), lambda i: (i, 0))],
               sem=("arbitrary",))(dxn, yout, mod, w_out)


def _mm_dw_out(ybr, dy, name, comm=None):
    m, di = ybr.shape
    d = dy.shape[1]
    tn = _tile(di, 1024)

    def body(y_ref, dy_ref, o_ref):
        o_ref[...] = _dot_tn(y_ref[...], dy_ref[...])

    return _pc(body, name=name, out_shape=jax.ShapeDtypeStruct((di, d), F32), grid=(di // tn,),
               in_specs=[pl.BlockSpec((m, tn), lambda n: (0, n)), pl.BlockSpec((m, d), lambda n: (0, 0))],
               out_specs=pl.BlockSpec((tn, d), lambda n: (n, 0)), sem=("parallel",), comm=comm)(ybr, dy)


def _sgu_mask():
    t = lax.broadcasted_iota(jnp.int32, (SG_BLOCK, SG_BLOCK), 0)
    s = lax.broadcasted_iota(jnp.int32, (SG_BLOCK, SG_BLOCK), 1)
    return (s // CHUNK) <= (t // CHUNK)


def _a_mid_fwd(proj, ln_g, ln_b, w_s, bs_t, t_seq, comm=None):
    m, n3 = proj.shape
    di = n3 // 3
    gd = di // SG_GROUPS
    r = _tile(t_seq, 256)
    nblk = r // SG_BLOCK

    def body(p_ref, lg_ref, lb_ref, ws_ref, bs_ref, ybr_ref, s_scr):
        v = _gelu(p_ref[:, di:2 * di])
        mu = jnp.mean(v, axis=-1, keepdims=True)
        vc = v - mu
        rstd = lax.rsqrt(jnp.mean(vc * vc, axis=-1, keepdims=True) + EPS)
        vb = (vc * rstd * lg_ref[...] + lb_ref[...]).astype(BF16)
        mask = _sgu_mask()
        for gi in range(SG_GROUPS):
            ws = jnp.where(mask, ws_ref[gi], 0.0).astype(BF16)
            bcol = bs_ref[:, gi:gi + 1]
            for b in range(nblk):
                rows = slice(b * SG_BLOCK, (b + 1) * SG_BLOCK)
                cols = slice(gi * gd, (gi + 1) * gd)
                s_scr[rows, cols] = _dot(ws, vb[rows, cols]) + bcol
        gg = p_ref[:, 2 * di:]
        ybr_ref[...] = (_gelu(p_ref[:, :di]) * s_scr[...] * (gg * _sigmoid(gg))).astype(BF16)

    vec = pl.BlockSpec((1, di), lambda i: (0, 0))
    return _pc(body, name="a_mid_fwd", out_shape=jax.ShapeDtypeStruct((m, di), BF16), grid=(m // r,),
               in_specs=[pl.BlockSpec((r, n3), lambda i: (i, 0)), vec, vec,
                         pl.BlockSpec((SG_GROUPS, SG_BLOCK, SG_BLOCK), lambda i: (0, 0, 0)),
                         pl.BlockSpec((SG_BLOCK, 128), lambda i: (0, 0))],
               out_specs=pl.BlockSpec((r, di), lambda i: (i, 0)),
               scratch=[pltpu.VMEM((r, di), F32)], sem=("parallel",), comm=comm)(proj, ln_g, ln_b, w_s, bs_t)


def _a_mid_bwd(proj, dybr, ln_g, ln_b, w_s, bs_t, t_seq, comm=None):
    m, n3 = proj.shape
    di = n3 // 3
    gd = di // SG_GROUPS
    r = _tile(t_seq, 256)
    nblk = r // SG_BLOCK

    def body(p_ref, dy_ref, lg_ref, lb_ref, ws_ref, bs_ref,
             dp_ref, dlg_ref, dlb_ref, dws_ref, dbs_ref, s_scr, dvl_scr):
        i = pl.program_id(0)

        @pl.when(i == 0)
        def _():
            dlg_ref[...] = jnp.zeros_like(dlg_ref)
            dlb_ref[...] = jnp.zeros_like(dlb_ref)
            dws_ref[...] = jnp.zeros_like(dws_ref)
            dbs_ref[...] = jnp.zeros_like(dbs_ref)

        v, dgelu_v = _gelu_and_grad(p_ref[:, di:2 * di])
        mu = jnp.mean(v, axis=-1, keepdims=True)
        vc = v - mu
        rstd = lax.rsqrt(jnp.mean(vc * vc, axis=-1, keepdims=True) + EPS)
        vhat = vc * rstd
        lg = lg_ref[...]
        vb = (vhat * lg + lb_ref[...]).astype(BF16)
        u, dgelu_u = _gelu_and_grad(p_ref[:, :di])
        gg = p_ref[:, 2 * di:]
        sg = _sigmoid(gg)
        dyv = dy_ref[...]
        dus = dyv * (gg * sg)
        dsb = (dus * u).astype(BF16)
        ds32 = dus * u
        mask = _sgu_mask()
        lane = lax.broadcasted_iota(jnp.int32, (SG_BLOCK, 128), 1)
        dbs_acc = jnp.zeros((SG_BLOCK, 128), F32)
        for gi in range(SG_GROUPS):
            ws = jnp.where(mask, ws_ref[gi], 0.0).astype(BF16)
            bcol = bs_ref[:, gi:gi + 1]
            cols = slice(gi * gd, (gi + 1) * gd)
            dws_acc = jnp.zeros((SG_BLOCK, SG_BLOCK), F32)
            dbs_col = jnp.zeros((SG_BLOCK, 1), F32)
            for b in range(nblk):
                rows = slice(b * SG_BLOCK, (b + 1) * SG_BLOCK)
                s_scr[rows, cols] = _dot(ws, vb[rows, cols]) + bcol
                dvl_scr[rows, cols] = _dot_tn(ws, dsb[rows, cols])
                dws_acc += _dot_nt(dsb[rows, cols], vb[rows, cols])
                dbs_col += jnp.sum(ds32[rows, cols], axis=-1, keepdims=True)
            dws_ref[gi] += jnp.where(mask, dws_acc, 0.0)
            dbs_acc += jnp.where(lane == gi, dbs_col, 0.0)
        dbs_ref[...] += dbs_acc
        s = s_scr[...]
        dp_ref[:, :di] = (dus * s * dgelu_u).astype(BF16)
        dp_ref[:, 2 * di:] = (dyv * u * s * (sg * (1.0 + gg * (1.0 - sg)))).astype(BF16)
        dvl = dvl_scr[...]
        dlg_ref[...] += jnp.sum(dvl * vhat, axis=0, keepdims=True)
        dlb_ref[...] += jnp.sum(dvl, axis=0, keepdims=True)
        dvh = dvl * lg
        dv = rstd * (dvh - jnp.mean(dvh, axis=-1, keepdims=True)
                     - vhat * jnp.mean(dvh * vhat, axis=-1, keepdims=True))
        dp_ref[:, di:2 * di] = (dv * dgelu_v).astype(BF16)

    vec = pl.BlockSpec((1, di), lambda i: (0, 0))
    wsb = pl.BlockSpec((SG_GROUPS, SG_BLOCK, SG_BLOCK), lambda i: (0, 0, 0))
    bsb = pl.BlockSpec((SG_BLOCK, 128), lambda i: (0, 0))
    return _pc(body, name="a_mid_bwd",
               out_shape=[jax.ShapeDtypeStruct((m, n3), BF16), jax.ShapeDtypeStruct((1, di), F32),
                          jax.ShapeDtypeStruct((1, di), F32),
                          jax.ShapeDtypeStruct((SG_GROUPS, SG_BLOCK, SG_BLOCK), F32),
                          jax.ShapeDtypeStruct((SG_BLOCK, 128), F32)],
               grid=(m // r,),
               in_specs=[pl.BlockSpec((r, n3), lambda i: (i, 0)), pl.BlockSpec((r, di), lambda i: (i, 0)),
                         vec, vec, wsb, bsb],
               out_specs=[pl.BlockSpec((r, n3), lambda i: (i, 0)), vec, vec, wsb, bsb],
               scratch=[pltpu.VMEM((r, di), F32), pltpu.VMEM((r, di), F32)],
               sem=("arbitrary",), comm=comm)(proj, dybr, ln_g, ln_b, w_s, bs_t)


def _chunk_rows(n):
    if isinstance(n, int):
        return pl.ds(n * CHUNK, CHUNK)
    return pl.ds(pl.multiple_of(n * CHUNK, CHUNK), CHUNK)


def _hgrn_dims(t_seq, di):
    tr = _tile(t_seq, 128)
    hc = _tile(di, 2048)
    return tr, hc, hc // HEAD_DIM


def _hgrn_gates(f_ref, lb, a_scr, k_scr, tr):
    sig = _sigmoid(f_ref[...])
    fg = lb + (1.0 - lb) * sig
    k_scr[...] = 1.0 - fg
    logf = jnp.log(fg)
    g = min(CUM_ROWS, tr)
    tri = _tri_mask(g, reverse=False)
    for rg in range(tr // g):
        a_scr[rg * g:(rg + 1) * g, :] = _tri_apply(tri, logf[rg * g:(rg + 1) * g, :])
    return sig, fg


def _hgrn_fwd(proj, lbj, gn, nb, t_seq):
    _, m, di = proj.shape
    tr, hc, hpg = _hgrn_dims(t_seq, di)
    nt, nhg, ncl = t_seq // tr, di // hc, tr // CHUNK
    nheads = di // HEAD_DIM

    def body(p_ref, lb_ref, gn_ref, o_ref, ybr_ref, st_ref, st_scr, a_scr, k_scr):
        q_ref, f_ref, i_ref, g_ref = (p_ref.at[s] for s in range(4))
        t = pl.program_id(2)

        @pl.when(t == 0)
        def _():
            st_scr[...] = jnp.zeros_like(st_scr)

        _hgrn_gates(f_ref, lb_ref[0:1, :], a_scr, k_scr, tr)
        gnv = gn_ref[...]
        rr = lax.broadcasted_iota(jnp.int32, (CHUNK, CHUNK), 0)
        cc = lax.broadcasted_iota(jnp.int32, (CHUNK, CHUNK), 1)
        causal = cc <= rr

        def chunk(n, carry):
            rows = _chunk_rows(n)
            lanes = [slice(hd * HEAD_DIM, (hd + 1) * HEAD_DIM) for hd in range(hpg)]
            hs = []
            for hd, ls in enumerate(lanes):
                h = {}
                ah, kh = a_scr[rows, ls], k_scr[rows, ls]
                qp = q_ref[rows, ls]
                qh = qp * _sigmoid(qp)
                h["vb"] = i_ref[rows, ls].astype(BF16)
                aref, alast = ah[CHUNK // 2 - 1:CHUNK // 2, :], ah[CHUNK - 1:CHUNK, :]
                h["q_in"] = (qh * jnp.exp(ah - aref)).astype(BF16)
                h["k_in"] = (kh * jnp.exp(aref - ah)).astype(BF16)
                h["q_out"] = (qh * jnp.exp(ah)).astype(BF16)
                h["k_out"] = (kh * jnp.exp(alast - ah)).astype(BF16)
                h["dec"] = jnp.exp(alast)
                st = st_scr[hd]
                st_ref[n, hd] = st
                h["st"] = st
                hs.append(h)
            for h in hs:
                h["scores"] = _dot_nt(h["q_in"], h["k_in"])
                h["o_inter"] = _dot_nt(h["q_out"], h["st"].astype(BF16))
                h["st_mm"] = _dot_tn(h["vb"], h["k_out"])
            for h in hs:
                h["o"] = _dot(jnp.where(causal, h["scores"], 0.0).astype(BF16), h["vb"]) + h["o_inter"]
            for hd, (h, ls) in enumerate(zip(hs, lanes)):
                st_scr[hd] = h["st"] * h["dec"] + h["st_mm"]
                o = h["o"]
                o_ref[rows, ls] = o
                rstd = lax.rsqrt(jnp.mean(o * o, axis=-1, keepdims=True) + EPS)
                gg = g_ref[rows, ls]
                ybr_ref[rows, ls] = ((o * rstd * gnv) * (gg * _sigmoid(gg))).astype(BF16)
            return carry

        lax.fori_loop(0, ncl, chunk, 0)

    blk = pl.BlockSpec((tr, hc), lambda hg, b, t: (b * nt + t, hg))
    return _pc(body, name="hgrn_fwd",
               out_shape=[jax.ShapeDtypeStruct((m, di), F32), jax.ShapeDtypeStruct((m, di), BF16),
                          jax.ShapeDtypeStruct((m // CHUNK, nheads, HEAD_DIM, HEAD_DIM), F32)],
               grid=(nhg, nb, nt),
               in_specs=[pl.BlockSpec((4, tr, hc), lambda hg, b, t: (0, b * nt + t, hg)),
                         pl.BlockSpec((2, hc), lambda hg, b, t: (0, hg)),
                         pl.BlockSpec((1, HEAD_DIM), lambda hg, b, t: (0, 0))],
               out_specs=[blk, blk, pl.BlockSpec((ncl, hpg, HEAD_DIM, HEAD_DIM),
                                                 lambda hg, b, t: (b * nt + t, hg, 0, 0))],
               scratch=[pltpu.VMEM((hpg, HEAD_DIM, HEAD_DIM), F32), pltpu.VMEM((tr, hc), F32),
                        pltpu.VMEM((tr, hc), F32)],
               sem=("parallel", "arbitrary", "arbitrary"))(proj, lbj, gn)


def _hgrn_bwd(proj, o_all, dybr, states, lbj, gn, nb, t_seq, comm=None):
    _, m, di = proj.shape
    tr, hc, hpg = _hgrn_dims(t_seq, di)
    nt, nhg, ncl = t_seq // tr, di // hc, tr // CHUNK

    def body(p_ref, o_ref, dy_ref, st_ref, lb_ref, gn_ref,
             dp_ref, dlb_ref, dgn_ref, dst_scr, a_scr, k_scr, da_scr, dk_scr):
        q_ref, f_ref, i_ref, g_ref = (p_ref.at[s] for s in range(4))
        hg, b, t = pl.program_id(0), pl.program_id(1), pl.program_id(2)

        @pl.when(t == 0)
        def _():
            dst_scr[...] = jnp.zeros_like(dst_scr)

        @pl.when((b == 0) & (t == 0))
        def _():
            dlb_ref[...] = jnp.zeros_like(dlb_ref)

        @pl.when((hg == 0) & (b == 0) & (t == 0))
        def _():
            dgn_ref[...] = jnp.zeros_like(dgn_ref)

        lb = lb_ref[0:1, :]
        sig, fg = _hgrn_gates(f_ref, lb, a_scr, k_scr, tr)
        gnv = gn_ref[...]
        rr = lax.broadcasted_iota(jnp.int32, (CHUNK, CHUNK), 0)
        cc = lax.broadcasted_iota(jnp.int32, (CHUNK, CHUNK), 1)
        causal = cc <= rr
        rowi = lax.broadcasted_iota(jnp.int32, (CHUNK, HEAD_DIM), 0)

        def chunk(it, carry):
            n = ncl - 1 - it
            rows = _chunk_rows(n)
            for hd0 in range(0, hpg, PHASE_HEADS):
                heads(n, rows, range(hd0, min(hpg, hd0 + PHASE_HEADS)))
            return carry

        def heads(n, rows, ids):
            lanes = [slice(hd * HEAD_DIM, (hd + 1) * HEAD_DIM) for hd in ids]
            hs = []
            for hd, ls in zip(ids, lanes):
                h = {}
                ah, kh = a_scr[rows, ls], k_scr[rows, ls]
                qp = q_ref[rows, ls]
                sq = _sigmoid(qp)
                qh = qp * sq
                h["dsilu_q"] = sq * (1.0 + qp * (1.0 - sq))
                h["vb"] = i_ref[rows, ls].astype(BF16)
                aref, alast = ah[CHUNK // 2 - 1:CHUNK // 2, :], ah[CHUNK - 1:CHUNK, :]
                h["e1"], h["e2"] = jnp.exp(ah - aref), jnp.exp(aref - ah)
                h["e3"], h["e4"] = jnp.exp(ah), jnp.exp(alast - ah)
                h["dec"] = jnp.exp(alast)
                h["q_in"], h["k_in"], h["q_out"], h["k_out"] = qh * h["e1"], kh * h["e2"], qh * h["e3"], kh * h["e4"]
                for nm in ("q_in", "k_in", "q_out", "k_out"):
                    h[nm + "_b"] = h[nm].astype(BF16)
                o = o_ref[rows, ls]
                rstd = lax.rsqrt(jnp.mean(o * o, axis=-1, keepdims=True) + EPS)
                ohat = o * rstd
                gg = g_ref[rows, ls]
                sg = _sigmoid(gg)
                dyv = dy_ref[rows, ls]
                d_on = dyv * (gg * sg)
                dp_ref[3, rows, ls] = (dyv * (ohat * gnv) * (sg * (1.0 + gg * (1.0 - sg)))).astype(BF16)
                h["dgn"] = jnp.sum(d_on * ohat, axis=0, keepdims=True)
                dohat = d_on * gnv
                do = rstd * (dohat - ohat * jnp.mean(dohat * ohat, axis=-1, keepdims=True))
                h["do_b"] = do.astype(BF16)
                h["st_prev"] = st_ref[n, hd]
                h["dst"] = dst_scr[hd]
                hs.append(h)
            for h in hs:
                dst_b = h["dst"].astype(BF16)
                h["scores"] = _dot_nt(h["q_in_b"], h["k_in_b"])
                h["dscores"] = _dot_nt(h["do_b"], h["vb"])
                h["dv_inter"] = _dot_nt(h["k_out_b"], dst_b)
                h["dq_out"] = _dot(h["do_b"], h["st_prev"].astype(BF16))
                h["dk_out"] = _dot(h["vb"], dst_b)
                h["dst_mm"] = _dot_tn(h["do_b"], h["q_out_b"])
            for h in hs:
                scores = jnp.where(causal, h["scores"], 0.0).astype(BF16)
                dscores = jnp.where(causal, h["dscores"], 0.0).astype(BF16)
                h["dv"] = _dot_tn(scores, h["do_b"]) + h["dv_inter"]
                h["dq_in"] = _dot(dscores, h["k_in_b"])
                h["dk_in"] = _dot_tn(dscores, h["q_in_b"])
            dgn = hs[0]["dgn"]
            for h in hs[1:]:
                dgn = dgn + h["dgn"]
            dgn_ref[...] += dgn
            for hd, h, ls in zip(ids, hs, lanes):
                ddec = jnp.sum(h["dst"] * h["st_prev"], axis=0, keepdims=True)
                dst_scr[hd] = h["dst"] * h["dec"] + h["dst_mm"]
                dp_ref[2, rows, ls] = h["dv"].astype(BF16)
                dq = h["dq_in"] * h["e1"] + h["dq_out"] * h["e3"]
                dp_ref[0, rows, ls] = (dq * h["dsilu_q"]).astype(BF16)
                dk_scr[rows, ls] = h["dk_in"] * h["e2"] + h["dk_out"] * h["e4"]
                t_in = h["dq_in"] * h["q_in"] - h["dk_in"] * h["k_in"]
                t_out = h["dk_out"] * h["k_out"]
                da = t_in + h["dq_out"] * h["q_out"] - t_out
                da_ref_row = -jnp.sum(t_in, axis=0, keepdims=True)
                da_last_row = jnp.sum(t_out, axis=0, keepdims=True) + ddec * h["dec"]
                da = da + jnp.where(rowi == CHUNK // 2 - 1, da_ref_row, 0.0) \
                        + jnp.where(rowi == CHUNK - 1, da_last_row, 0.0)
                da_scr[rows, ls] = da

        if ncl <= 2:
            for it in range(ncl):
                chunk(it, 0)
        else:
            lax.fori_loop(0, ncl, chunk, 0)
        g = min(CUM_ROWS, tr)
        tri = _tri_mask(g, reverse=True)
        for rg in range(tr // g):
            rs = slice(rg * g, (rg + 1) * g)
            dlogf = _tri_apply(tri, da_scr[rs, :])
            df = dlogf / fg[rs, :] - dk_scr[rs, :]
            sgr = sig[rs, :]
            dp_ref[1, rs, :] = (df * (1.0 - lb) * (sgr * (1.0 - sgr))).astype(BF16)
            dlb_ref[...] += jnp.sum(df * (1.0 - sgr), axis=0, keepdims=True) * lb_ref[1:2, :]

    blk = pl.BlockSpec((tr, hc), lambda hg, b, t: (b * nt + (nt - 1 - t), hg))
    return _pc(body, name="hgrn_bwd",
               out_shape=[jax.ShapeDtypeStruct((4, m, di), BF16), jax.ShapeDtypeStruct((1, di), F32),
                          jax.ShapeDtypeStruct((1, HEAD_DIM), F32)],
               grid=(nhg, nb, nt),
               in_specs=[pl.BlockSpec((4, tr, hc), lambda hg, b, t: (0, b * nt + (nt - 1 - t), hg)), blk, blk,
                         pl.BlockSpec((ncl, hpg, HEAD_DIM, HEAD_DIM),
                                      lambda hg, b, t: (b * nt + (nt - 1 - t), hg, 0, 0)),
                         pl.BlockSpec((2, hc), lambda hg, b, t: (0, hg)),
                         pl.BlockSpec((1, HEAD_DIM), lambda hg, b, t: (0, 0))],
               out_specs=[pl.BlockSpec((4, tr, hc), lambda hg, b, t: (0, b * nt + (nt - 1 - t), hg)),
                          pl.BlockSpec((1, hc), lambda hg, b, t: (0, hg)),
                          pl.BlockSpec((1, HEAD_DIM), lambda hg, b, t: (0, 0))],
               scratch=[pltpu.VMEM((hpg, HEAD_DIM, HEAD_DIM), F32)] + [pltpu.VMEM((tr, hc), F32)] * 4,
               sem=("arbitrary", "arbitrary", "arbitrary"), comm=comm)(
                   proj, o_all, dybr, states, lbj, gn)


def _adamw(parts, w, m, v, name):
    r, c = w.shape
    tr = _tile(r, 256)
    npart = len(parts)
    c1 = 1.0 - ADAM_B1 ** ADAM_STEP
    c2 = 1.0 - ADAM_B2 ** ADAM_STEP

    def body(*refs):
        p_refs = refs[:npart]
        _adamw_math(p_refs, *refs[npart:], c1, c2)

    blk = pl.BlockSpec((tr, c), lambda i: (i, 0))
    return _pc(body, name=name, out_shape=[jax.ShapeDtypeStruct((r, c), F32)] * 4, grid=(r // tr,),
               in_specs=[blk] * (npart + 3), out_specs=[blk] * 4, sem=("parallel",))(*parts, w, m, v)


def _adamw_math(p_refs, w_ref, m_ref, v_ref, g_ref, d_ref, nm_ref, nv_ref, c1, c2):
    g = p_refs[0][...].astype(F32)
    for p in p_refs[1:]:
        g = g + p[...].astype(F32)
    nm = ADAM_B1 * m_ref[...] + (1.0 - ADAM_B1) * g
    nv = ADAM_B2 * v_ref[...] + (1.0 - ADAM_B2) * (g * g)
    g_ref[...] = g
    nm_ref[...] = nm
    nv_ref[...] = nv
    d_ref[...] = -ADAM_LR * ((nm / c1) / (jnp.sqrt(nv / c2) + ADAM_EPS) + ADAM_WD * w_ref[...])


def _adamw_blocks(parts, idx, w, m, v, name):
    r, c = w.shape
    tr = _tile(r, 256)
    npart = len(parts)
    c1 = 1.0 - ADAM_B1 ** ADAM_STEP
    c2 = 1.0 - ADAM_B2 ** ADAM_STEP

    def body(idx_ref, *refs):
        _adamw_math(refs[:npart], *refs[npart:], c1, c2)

    def sel(p):
        return pl.BlockSpec((None, tr, c), lambda i, s: (s[p], i, 0))

    blk = pl.BlockSpec((tr, c), lambda i, s: (i, 0))
    gs = pltpu.PrefetchScalarGridSpec(num_scalar_prefetch=1, grid=(r // tr,),
                                      in_specs=[sel(p) for p in range(npart)] + [blk] * 3, out_specs=[blk] * 4)
    return _pc(body, name=name, out_shape=[jax.ShapeDtypeStruct((r, c), F32)] * 4, grid_spec=gs,
               sem=("parallel",))(idx, *parts, w, m, v)


_EARLY = ["a_ln_gain", "a_ln_bias", "a_w_s", "a_b_s", "b_lower_bounds", "b_gn_gain"]


def _pack(arrs):
    flat = jnp.concatenate([a.reshape(-1) for a in arrs])
    rows = -(-flat.shape[0] // 1024) * 8
    return jnp.pad(flat, (0, rows * 128 - flat.shape[0])).reshape(rows, 128)


def _unpack(buf, like):
    flat = buf.reshape(-1)
    out, off = [], 0
    for a in like:
        out.append(flat[off:off + a.size].reshape(a.shape))
        off += a.size
    return out


def kernel(x, c, norm_gain, w_ada, b_ada, a_w_in, a_ln_gain, a_ln_bias, a_w_s, a_b_s, a_w_out, b_w_in, b_lower_bounds, b_gn_gain, b_w_out, final_gain, loss_target, m_norm_gain, m_w_ada, m_b_ada, m_a_w_in, m_a_ln_gain, m_a_ln_bias, m_a_w_s, m_a_b_s, m_a_w_out, m_b_w_in, m_b_lower_bounds, m_b_gn_gain, m_b_w_out, m_final_gain, v_norm_gain, v_w_ada, v_b_ada, v_a_w_in, v_a_ln_gain, v_a_ln_bias, v_a_w_s, v_a_b_s, v_a_w_out, v_b_w_in, v_b_lower_bounds, v_b_gn_gain, v_b_w_out, v_final_gain):
    w = dict(norm_gain=norm_gain, w_ada=w_ada, b_ada=b_ada, a_w_in=a_w_in, a_ln_gain=a_ln_gain,
             a_ln_bias=a_ln_bias, a_w_s=a_w_s, a_b_s=a_b_s, a_w_out=a_w_out, b_w_in=b_w_in,
             b_lower_bounds=b_lower_bounds, b_gn_gain=b_gn_gain, b_w_out=b_w_out, final_gain=final_gain)
    mo = dict(norm_gain=m_norm_gain, w_ada=m_w_ada, b_ada=m_b_ada, a_w_in=m_a_w_in, a_ln_gain=m_a_ln_gain,
              a_ln_bias=m_a_ln_bias, a_w_s=m_a_w_s, a_b_s=m_a_b_s, a_w_out=m_a_w_out, b_w_in=m_b_w_in,
              b_lower_bounds=m_b_lower_bounds, b_gn_gain=m_b_gn_gain, b_w_out=m_b_w_out, final_gain=m_final_gain)
    vo = dict(norm_gain=v_norm_gain, w_ada=v_w_ada, b_ada=v_b_ada, a_w_in=v_a_w_in, a_ln_gain=v_a_ln_gain,
              a_ln_bias=v_a_ln_bias, a_w_s=v_a_w_s, a_b_s=v_a_b_s, a_w_out=v_a_w_out, b_w_in=v_b_w_in,
              b_lower_bounds=v_b_lower_bounds, b_gn_gain=v_b_gn_gain, b_w_out=v_b_w_out, final_gain=v_final_gain)

    nb, t_seq, d = x.shape
    m = nb * t_seq
    ncol_ada = w_ada.shape[2]
    xi, yi, ci = lax.axis_index("x"), lax.axis_index("y"), lax.axis_index("c")
    me = 4 * xi + 2 * yi + ci

    c_g, wa_in_g = _all_gather([c, a_w_in[0].astype(BF16)], "gather_c_wa")

    c_all = c_g.reshape(NDEV * nb, d)
    b_cols = lax.dynamic_slice(b_ada, (0, me * ncol_ada), (2, ncol_ada)).reshape(2, 1, ncol_ada)
    mod_part, lbj = _ada_fwd(c_all, w_ada, b_cols, b_lower_bounds)
    mod_all = _all_gather([mod_part], "gather_mod")[0]
    mod_mine = lax.dynamic_slice_in_dim(mod_all, me * nb, nb, axis=2)
    mod_mine = mod_mine.transpose(1, 2, 0, 3).reshape(2, nb, 3, d)
    mod0, mod1 = mod_mine[0], mod_mine[1]

    di = a_w_out.shape[1] * NDEV

    xf = x.reshape(m, d)
    tgt = loss_target.reshape(m, d)
    ng0, ng1 = norm_gain[0:1], norm_gain[1:2]
    ncb = b_w_in.shape[2]
    wb_lo, wb_hi = b_w_in[0][:, :ncb // 2].astype(BF16), b_w_in[0][:, ncb // 2:].astype(BF16)
    h0, h0_t = _prenorm(xf, ng0, mod0, t_seq, "prenorm_a")
    proj_a, half = _mm_in(h0, [wa_in_g], 1, "in_proj_a", comm=_gather_first([a_w_out[0].astype(BF16), wb_lo]))
    bs_t = jnp.pad(a_b_s[0].T, ((0, 0), (0, 128 - SG_GROUPS)))
    ybr_a, (wa_out_g, wb_lo_g, wb_hi_half) = _a_mid_fwd(
        proj_a, a_ln_gain, a_ln_bias, a_w_s[0], bs_t, t_seq, comm=_join(_gather_second(half), _gather_first([wb_hi])))
    wa_out = wa_out_g.reshape(di, d)
    (yout_a, x1), (wb_hi_g, wb_out_half) = _out_proj(
        ybr_a, wa_out, xf, mod0, t_seq, "out_proj_a",
        comm=_join(_gather_second([wb_hi_half]), _gather_first([b_w_out[0].astype(BF16)])))
    wb_in_g = [wb_lo_g, wb_hi_g]
    h1, h1_t = _prenorm(x1, ng1, mod1, t_seq, "prenorm_b")
    proj_b, (wb_out_g,) = _mm_in(h1, wb_in_g, 4, "in_proj_b", comm=_gather_second([wb_out_half]))
    wb_out = wb_out_g.reshape(di, d)
    o_b, ybr_b, states = _hgrn_fwd(proj_b, lbj, b_gn_gain, nb, t_seq)
    yout_b, dx2, loss_part, d_final_gain = _out_proj_loss(ybr_b, wb_out, x1, mod1, final_gain.reshape(1, d), tgt, t_seq)

    rows_out = a_w_out.shape[1]
    dy_b, dgate1, dybr_b = _gate_dybr(dx2, yout_b, mod1, wb_out, t_seq, "dybr_b")
    rs_wb_out = _ReduceScatter(_mm_dw_out(ybr_b, dy_b, "dw_out_b").reshape(NDEV, rows_out, d), "b_w_out")
    dproj_b, d_lb, d_gn = _hgrn_bwd(proj_b, o_b, dybr_b, states, lbj, b_gn_gain, nb, t_seq)
    dh1, got = _mm_din(dproj_b, wb_in_g, 4, "dh_b", comm=rs_wb_out.swap_core())
    rs_wb_out.after_core(got[0])
    dx1, dss1, dgain1 = _prenorm_bwd(dh1, x1, ng1, mod1, dx2, t_seq, "prenorm_bwd_b")
    rs_wb_in = _ReduceScatter(_mm_dw_in(h1_t, dproj_b, ncb, 4, "dw_in_b"), "b_w_in")

    dy_a, dgate0, dybr_a = _gate_dybr(dx1, yout_a, mod0, wa_out, t_seq, "dybr_a")
    g_wa_out, got = _mm_dw_out(ybr_a, dy_a, "dw_out_a", comm=_join(rs_wb_in.swap_core(), rs_wb_out.swap_chips()))
    rs_wb_in.after_core(got[0])
    rs_wb_out.after_chips(got[1])
    rs_wa_out = _ReduceScatter(g_wa_out.reshape(NDEV, rows_out, d), "a_w_out")
    (dproj_a, d_lng, d_lnb, d_ws, d_bs_t), got = _a_mid_bwd(
        proj_a, dybr_a, a_ln_gain, a_ln_bias, a_w_s[0], bs_t, t_seq,
        comm=_join(rs_wb_in.swap_chips(), rs_wa_out.swap_core()))
    rs_wb_in.after_chips(got[0])
    rs_wa_out.after_core(got[1])
    part = dict(a_ln_gain=d_lng, a_ln_bias=d_lnb, a_w_s=d_ws[None], a_b_s=d_bs_t[:, :SG_GROUPS].T[None],
                b_lower_bounds=jnp.concatenate([-d_lb, d_lb], axis=0), b_gn_gain=d_gn)
    early_pack = _pack([part[k].reshape(w[k].shape) for k in _EARLY])
    g_wa_in, got = _mm_dw_in(h0_t, dproj_a, wa_in_g.shape[2], 1, "dw_in_a",
                             comm=_join(rs_wa_out.swap_chips(), _gather_first([early_pack])))
    rs_wa_out.after_chips(got[0])
    rs_wa_in = _ReduceScatter(g_wa_in, "a_w_in")
    n_tiles = m // _din_tile(m)
    assert n_tiles >= 2
    first_tiles = max(1, (3 * n_tiles) // 8)
    dh0, got2 = _mm_din(dproj_a, [wa_in_g], 1, "dh_a_first", tiles=(0, first_tiles),
                        comm=_join(rs_wa_in.swap_core(), _gather_second([got[1]])))
    rs_wa_in.after_core(got2[0])
    early_all = got2[1]
    dh0, got = _mm_din(dproj_a, [wa_in_g], 1, "dh_a_rest", comm=rs_wa_in.swap_chips(),
                       tiles=(first_tiles, n_tiles - first_tiles), prev=dh0)
    rs_wa_in.after_chips(got[0])
    dx0, dss0, dgain0 = _prenorm_bwd(dh0, xf, ng0, mod0, dx1, t_seq, "prenorm_bwd_a")
    grad_x = dx0.reshape(nb, t_seq, d)

    dmod = jnp.stack([jnp.concatenate([dss0, dgate0], axis=1), jnp.concatenate([dss1, dgate1], axis=1)])
    late_like = [norm_gain, final_gain, loss_part.reshape(1)]
    late_pack = _pack([jnp.concatenate([dgain0, dgain1], axis=0), d_final_gain[0], loss_part.reshape(1)])
    dmod_all, late_all = _all_gather([dmod.reshape(2, nb, 3 * d), late_pack], "gather_tail")
    dmod_all = dmod_all.transpose(1, 0, 2, 3).reshape(2, NDEV * nb, 3 * d)
    dmod_cols = lax.dynamic_slice_in_dim(dmod_all, me * ncol_ada, ncol_ada, axis=2)
    g_w_ada, g_b_ada = _ada_bwd(c_all, dmod_cols, dmod_all)

    res = {}
    early_like = [w[k] for k in _EARLY]
    dev_order = jnp.arange(NDEV, dtype=jnp.int32)
    sm = _adamw_blocks([early_all] * NDEV, dev_order, _pack(early_like), _pack([mo[k] for k in _EARLY]),
                       _pack([vo[k] for k in _EARLY]), "adamw_small_early")
    sm = [dict(zip(_EARLY, _unpack(buf, early_like))) for buf in sm]
    for k in _EARLY:
        res[k] = tuple(s[k] for s in sm)
    zero = jnp.zeros((1,), F32)
    sm = _adamw_blocks([late_all] * NDEV, dev_order, _pack([norm_gain, final_gain, zero]),
                       _pack([mo["norm_gain"], mo["final_gain"], zero]),
                       _pack([vo["norm_gain"], vo["final_gain"], zero]), "adamw_small_late")
    sm = [_unpack(buf, late_like) for buf in sm]
    res["norm_gain"] = tuple(s[0] for s in sm)
    res["final_gain"] = tuple(s[1] for s in sm)
    loss = sm[0][2][0]
    rb = _adamw([g_b_ada], b_ada, mo["b_ada"], vo["b_ada"], "adamw_b_ada")
    res["b_ada"] = tuple(rb)
    sh = w_ada.shape
    ra = _adamw([g_w_ada.reshape(sh[0] * sh[1], sh[2])], w_ada.reshape(sh[0] * sh[1], sh[2]),
                mo["w_ada"].reshape(sh[0] * sh[1], sh[2]), vo["w_ada"].reshape(sh[0] * sh[1], sh[2]), "adamw_w_ada")
    res["w_ada"] = tuple(z.reshape(sh) for z in ra)

    for k, rs in (("b_w_out", rs_wb_out), ("b_w_in", rs_wb_in), ("a_w_out", rs_wa_out), ("a_w_in", rs_wa_in)):
        res[k] = tuple(z[None] for z in _adamw_blocks(rs.parts, rs.idx, w[k][0], mo[k][0], vo[k][0], "adamw_" + k))

    order = ["norm_gain", "w_ada", "b_ada", "a_w_in", "a_ln_gain", "a_ln_bias", "a_w_s", "a_b_s", "a_w_out",
             "b_w_in", "b_lower_bounds", "b_gn_gain", "b_w_out", "final_gain"]
    return (loss, grad_x, *[res[k][0] for k in order], *[res[k][1] for k in order],
            *[res[k][2] for k in order], *[res[k][3] for k in order])
```

```python
import functools
import math

import jax
import jax.numpy as jnp
from jax import lax
from jax.experimental import pallas as pl
from jax.experimental.pallas import tpu as pltpu

F32 = jnp.float32
BF16 = jnp.bfloat16
MESH = pl.DeviceIdType.MESH
NDEV = 8
EPS = 1e-6
CHUNK = 64
SG_BLOCK = 128
SG_GROUPS = 8
HEAD_DIM = 128
CUM_ROWS = 256
PHASE_HEADS = 8
ADAM_LR, ADAM_B1, ADAM_B2, ADAM_EPS, ADAM_WD, ADAM_STEP = 0.001, 0.9, 0.999, 1e-08, 0.01, 10
VMEM_LIMIT = 56 * 1024 * 1024
ANY = pl.BlockSpec(memory_space=pl.ANY)


class _Hosted:
    def __init__(self, arrays, out_shapes, nsem, start, finish, aliases=None):
        self.arrays, self.out_shapes, self.nsem = list(arrays), list(out_shapes), nsem
        self.start, self.finish = start, finish
        self.aliases = dict(aliases or {})


def _join(*comms):
    arrays, outs, aliases, offs, nsem = [], [], {}, [], 0
    for cm in comms:
        offs.append((len(arrays), len(outs), nsem))
        for i, o in cm.aliases.items():
            aliases[len(arrays) + i] = len(outs) + o
        arrays += cm.arrays
        outs += cm.out_shapes
        nsem += cm.nsem

    def run(which):
        def f(ins, outs_, ss, rs, base):
            for cm, (ia, io, isem) in zip(comms, offs):
                getattr(cm, which)(ins[ia:ia + len(cm.arrays)], outs_[io:io + len(cm.out_shapes)], ss, rs, base + isem)
        return f

    return _Hosted(arrays, outs, nsem, run("start"), run("finish"), aliases)


def _pc(body, *, name, out_shape, grid=None, in_specs=None, out_specs=None, scratch=(), sem=None,
        grid_spec=None, comm=None, aliases=None):
    cp = dict(vmem_limit_bytes=VMEM_LIMIT)
    aliases = dict(aliases or {})
    if comm is None:
        if sem is not None:
            cp["dimension_semantics"] = sem
        kw = {"input_output_aliases": aliases}
        if grid_spec is not None:
            kw["grid_spec"] = grid_spec
        else:
            if grid is not None:
                kw["grid"] = grid
            if in_specs is not None:
                kw["in_specs"] = in_specs
            if out_specs is not None:
                kw["out_specs"] = out_specs
            kw["scratch_shapes"] = list(scratch)
        return pl.pallas_call(functools.partial(body), name=name, out_shape=out_shape,
                              compiler_params=pltpu.CompilerParams(**cp), **kw)

    single = not isinstance(out_shape, (list, tuple))
    outs_list = [out_shape] if single else list(out_shape)
    ospecs = [out_specs] if single else list(out_specs)
    n_in, n_out, n_ci, n_co, n_scr = len(in_specs), len(outs_list), len(comm.arrays), len(comm.out_shapes), len(scratch)
    cp["dimension_semantics"] = ("arbitrary",) * len(grid)

    def hosted(*refs):
        cin, hin = refs[:n_in], refs[n_in:n_in + n_ci]
        cout = refs[n_in + n_ci:n_in + n_ci + n_out]
        hout = refs[n_in + n_ci + n_out:n_in + n_ci + n_out + n_co]
        scr = refs[n_in + n_ci + n_out + n_co:n_in + n_ci + n_out + n_co + n_scr]
        ssem, rsem = refs[-2], refs[-1]
        first = functools.reduce(lambda p, q: p & q, [pl.program_id(a) == 0 for a in range(len(grid))])
        last = functools.reduce(lambda p, q: p & q, [pl.program_id(a) == grid[a] - 1 for a in range(len(grid))])

        @pl.when(first)
        def _():
            comm.start(hin, hout, ssem, rsem, 0)

        body(*cin, *cout, *scr)

        @pl.when(last)
        def _():
            comm.finish(hin, hout, ssem, rsem, 0)

    call = pl.pallas_call(
        hosted, name=name, grid=grid, in_specs=list(in_specs) + [ANY] * n_ci, out_specs=ospecs + [ANY] * n_co,
        out_shape=outs_list + comm.out_shapes,
        scratch_shapes=list(scratch) + [pltpu.SemaphoreType.DMA((comm.nsem,)), pltpu.SemaphoreType.DMA((comm.nsem,))],
        input_output_aliases={**aliases, **{n_in + i: n_out + o for i, o in comm.aliases.items()}},
        compiler_params=pltpu.CompilerParams(**cp))

    def run(*args):
        res = call(*args, *comm.arrays)
        comp = res[:n_out]
        return (comp[0] if single else comp), list(res[n_out:])

    return run


def _tile(n, pref):
    return pref if n % pref == 0 else n


def _sigmoid(x):
    return 1.0 / (1.0 + jnp.exp(-x))


def _gelu(x):
    c = math.sqrt(2.0 / math.pi)
    return 0.5 * x * (1.0 + jnp.tanh(c * (x + 0.044715 * (x * x * x))))


def _gelu_and_grad(x):
    c = math.sqrt(2.0 / math.pi)
    x2 = x * x
    t = jnp.tanh(c * (x + 0.044715 * (x2 * x)))
    half = 0.5 * (1.0 + t)
    return x * half, half + (0.5 * x) * (1.0 - t * t) * (c + (3.0 * 0.044715 * c) * x2)


def _dot(a, b):
    return jnp.dot(a, b, preferred_element_type=F32)


def _dot_nt(a, b):
    return lax.dot_general(a, b, (((1,), (1,)), ((), ())), preferred_element_type=F32)


def _dot_tn(a, b):
    return lax.dot_general(a, b, (((0,), (0,)), ((), ())), preferred_element_type=F32)


def _tri_mask(n, reverse):
    r = lax.broadcasted_iota(jnp.int32, (n, n), 0)
    c = lax.broadcasted_iota(jnp.int32, (n, n), 1)
    same = (r // CHUNK) == (c // CHUNK)
    tri = (c >= r) if reverse else (c <= r)
    return jnp.where(same & tri, 1.0, 0.0).astype(BF16)


def _tri_apply(tri, x):
    hi = x.astype(BF16)
    r1 = x - hi.astype(F32)
    mid = r1.astype(BF16)
    lo = (r1 - mid.astype(F32)).astype(BF16)
    return _dot(tri, hi) + (_dot(tri, mid) + _dot(tri, lo))


def _all_gather(arrs, name):
    n = len(arrs)

    def body(*refs):
        ins, outs = refs[:n], refs[n:2 * n]
        send_sems, recv_sems, local_sems = refs[2 * n:]
        x, y, c = lax.axis_index("x"), lax.axis_index("y"), lax.axis_index("c")
        me, sibling = (x, y, c), (x, y, 1 - c)
        near = (x + c - 2 * x * c, y + (1 - c) - 2 * y * (1 - c))
        far = (x + (1 - c) - 2 * x * (1 - c), y + c - 2 * y * c)
        diag = (1 - x, 1 - y)

        def blk(a, p):
            return outs[a].at[4 * p[0] + 2 * p[1] + p[2]]

        def copy(a, k, block, to, src=None):
            return pltpu.make_async_remote_copy(
                src_ref=blk(a, block) if src is None else src, dst_ref=blk(a, block),
                send_sem=send_sems.at[7 * a + k], recv_sem=recv_sems.at[7 * a + k],
                device_id=to, device_id_type=MESH)

        mine = [pltpu.make_async_copy(ins[a], blk(a, me), local_sems.at[a]) for a in range(n)]
        for m in mine:
            m.start()
        sends = []
        for a in range(n):
            sends += [copy(a, 0, me, sibling, src=ins[a]), copy(a, 1, me, (*near, c), src=ins[a]),
                      copy(a, 2, me, (*far, c), src=ins[a])]
        for cp in sends:
            cp.start()
        for a in range(n):
            copy(a, 1, (*near, c), me).wait_recv()
            sends.append(copy(a, 3, (*near, c), (*far, c)))
            sends[-1].start()
        for a in range(n):
            sends.append(copy(a, 4, (*near, c), sibling))
            sends[-1].start()
            copy(a, 2, (*far, c), me).wait_recv()
            sends.append(copy(a, 5, (*far, c), sibling))
            sends[-1].start()
        for a in range(n):
            copy(a, 3, (*diag, c), me).wait_recv()
            sends.append(copy(a, 6, (*diag, c), sibling))
            sends[-1].start()
        for a in range(n):
            copy(a, 0, sibling, me).wait_recv()
            copy(a, 4, (*far, 1 - c), me).wait_recv()
            copy(a, 5, (*near, 1 - c), me).wait_recv()
            copy(a, 6, (*diag, 1 - c), me).wait_recv()
        for cp in sends:
            cp.wait_send()
        for m in mine:
            m.wait()

    out_shape = [jax.ShapeDtypeStruct((NDEV,) + a.shape, a.dtype) for a in arrs]
    return _pc(body, name=name, out_shape=out_shape, in_specs=[ANY] * n, out_specs=[ANY] * n,
               scratch=[pltpu.SemaphoreType.DMA((7 * n,)), pltpu.SemaphoreType.DMA((7 * n,)),
                        pltpu.SemaphoreType.DMA((n,))])(*arrs)


def _gather_first(arrs):
    n = len(arrs)

    def parts(ins, outs, ss, rs, base):
        x, y, c = lax.axis_index("x"), lax.axis_index("y"), lax.axis_index("c")
        me, sibling = (x, y, c), (x, y, 1 - c)
        chips = [(1 - x, y), (x, 1 - y), (1 - x, 1 - y)]

        def blk(a, p):
            return outs[a].at[4 * p[0] + 2 * p[1] + p[2]]

        def copy(a, k, block, to):
            return pltpu.make_async_remote_copy(
                src_ref=ins[a], dst_ref=blk(a, block), send_sem=ss.at[base + 4 * a + k],
                recv_sem=rs.at[base + 4 * a + k], device_id=to, device_id_type=MESH)

        local = [pltpu.make_async_copy(ins[a], blk(a, me), ss.at[base + 4 * n + a]) for a in range(n)]
        sends, recvs = [], []
        for a in range(n):
            sends.append(copy(a, 0, me, sibling))
            recvs.append(copy(a, 0, sibling, me))
            for j, chip in enumerate(chips):
                sends.append(copy(a, 1 + j, me, (*chip, c)))
                recvs.append(copy(a, 1 + j, (*chip, c), me))
        return local, sends, recvs

    def start(ins, outs, ss, rs, base):
        local, sends, _ = parts(ins, outs, ss, rs, base)
        for cp in local + sends:
            cp.start()

    def finish(ins, outs, ss, rs, base):
        local, sends, recvs = parts(ins, outs, ss, rs, base)
        for cp in recvs:
            cp.wait_recv()
        for cp in sends:
            cp.wait_send()
        for cp in local:
            cp.wait()

    return _Hosted(arrs, [jax.ShapeDtypeStruct((NDEV,) + a.shape, a.dtype) for a in arrs], 5 * n, start, finish)


def _gather_second(bufs):
    n = len(bufs)

    def parts(ins, outs, ss, rs, base):
        x, y, c = lax.axis_index("x"), lax.axis_index("y"), lax.axis_index("c")
        sibling = (x, y, 1 - c)
        chips = [(1 - x, y), (x, 1 - y), (1 - x, 1 - y)]
        sends, recvs = [], []
        for a in range(n):
            for j, chip in enumerate(chips):
                mine = 4 * chip[0] + 2 * chip[1] + c
                theirs = 4 * chip[0] + 2 * chip[1] + (1 - c)
                sends.append(pltpu.make_async_remote_copy(
                    src_ref=ins[a].at[mine], dst_ref=outs[a].at[mine], send_sem=ss.at[base + 3 * a + j],
                    recv_sem=rs.at[base + 3 * a + j], device_id=sibling, device_id_type=MESH))
                recvs.append(pltpu.make_async_remote_copy(
                    src_ref=ins[a].at[theirs], dst_ref=outs[a].at[theirs], send_sem=ss.at[base + 3 * a + j],
                    recv_sem=rs.at[base + 3 * a + j], device_id=sibling, device_id_type=MESH))
        return sends, recvs

    def start(ins, outs, ss, rs, base):
        for cp in parts(ins, outs, ss, rs, base)[0]:
            cp.start()

    def finish(ins, outs, ss, rs, base):
        sends, recvs = parts(ins, outs, ss, rs, base)
        for cp in recvs:
            cp.wait_recv()
        for cp in sends:
            cp.wait_send()

    return _Hosted(bufs, [jax.ShapeDtypeStruct(b.shape, b.dtype) for b in bufs], 3 * n, start, finish,
                   aliases={a: a for a in range(n)})


def _swap(src, nblk, ids_fn, partner_fn):
    def copies(ins, outs, ss, rs, base):
        x, y, c = lax.axis_index("x"), lax.axis_index("y"), lax.axis_index("c")
        ids = ids_fn(x, y, c)
        partner = partner_fn(x, y, c)
        return [pltpu.make_async_remote_copy(
            src_ref=ins[0].at[ids[k]], dst_ref=outs[0].at[k], send_sem=ss.at[base + k], recv_sem=rs.at[base + k],
            device_id=partner, device_id_type=MESH) for k in range(nblk)]

    def start(ins, outs, ss, rs, base):
        for cp in copies(ins, outs, ss, rs, base):
            cp.start()

    def finish(ins, outs, ss, rs, base):
        for cp in copies(ins, outs, ss, rs, base):
            cp.wait()

    return _Hosted([src], [jax.ShapeDtypeStruct((nblk,) + src.shape[1:], src.dtype)], nblk, start, finish)


def _swap_chips(send):
    def copies(ins, outs, ss, rs, base):
        x, y, c = lax.axis_index("x"), lax.axis_index("y"), lax.axis_index("c")
        chips = [(1 - x, y), (x, 1 - y), (1 - x, 1 - y)]
        return [pltpu.make_async_remote_copy(
            src_ref=ins[0].at[j], dst_ref=outs[0].at[j], send_sem=ss.at[base + j], recv_sem=rs.at[base + j],
            device_id=(*chip, c), device_id_type=MESH) for j, chip in enumerate(chips)]

    def start(ins, outs, ss, rs, base):
        for cp in copies(ins, outs, ss, rs, base):
            cp.start()

    def finish(ins, outs, ss, rs, base):
        for cp in copies(ins, outs, ss, rs, base):
            cp.wait()

    return _Hosted([send], [jax.ShapeDtypeStruct(send.shape, send.dtype)], 3, start, finish)


def _add_send(a, b, idx, ns, name):
    _, r, c = a.shape
    tr = _tile(r, 256)

    def body(idx_ref, a_ref, b_ref, send_ref):
        send_ref[...] = (a_ref[...] + b_ref[...]).astype(BF16)

    def sel(off):
        return pl.BlockSpec((None, tr, c), lambda k, i, s: (s[off + k], i, 0))

    gs = pltpu.PrefetchScalarGridSpec(num_scalar_prefetch=1, grid=(ns, r // tr), in_specs=[sel(0), sel(ns)],
                                      out_specs=pl.BlockSpec((None, tr, c), lambda k, i, s: (k, i, 0)))
    return _pc(body, name=name, grid_spec=gs, sem=("arbitrary", "arbitrary"),
               out_shape=jax.ShapeDtypeStruct((ns, r, c), BF16))(idx, a, b)


class _ReduceScatter:
    def __init__(self, g, tag):
        self.g, self.tag = g, tag

    def swap_core(self):
        return _swap(self.g, 4, lambda x, y, c: [1 - c, 3 - c, 5 - c, 7 - c], lambda x, y, c: (x, y, 1 - c))

    def after_core(self, recv):
        x, y, c = lax.axis_index("x"), lax.axis_index("y"), lax.axis_index("c")
        chips = [(1 - x, y), (x, 1 - y), (1 - x, 1 - y)]
        idx = jnp.stack([4 * p + 2 * q + c for p, q in chips] + [2 * p + q for p, q in chips]).astype(jnp.int32)
        self.send = _add_send(self.g, recv, idx, 3, "rs_add_" + self.tag)
        self.recv_core = recv
        zero = jnp.zeros((), jnp.int32)
        self.idx = jnp.stack([4 * x + 2 * y + c, 2 * x + y, zero, zero + 1, zero + 2]).astype(jnp.int32)

    def swap_chips(self):
        return _swap_chips(self.send)

    def after_chips(self, recv):
        self.parts = [self.g, self.recv_core, recv, recv, recv]


def _ada_fwd(c_all, w_ada, b_cols, b_lb):
    nl, d, ncol = w_ada.shape
    nseq = c_all.shape[0]
    di = b_lb.shape[1]

    def body(c_ref, w_ref, b_ref, lb_ref, mod_ref, lbj_ref):
        cv = c_ref[...]
        cact = (cv * _sigmoid(cv)).astype(BF16)
        for l in range(nl):
            mod_ref[l] = _dot(cact, w_ref[l].astype(BF16)) + b_ref[l]
        b0, b1 = lb_ref[0:1, :], lb_ref[1:2, :]
        mx = jnp.maximum(b0, b1)
        e0, e1 = jnp.exp(b0 - mx), jnp.exp(b1 - mx)
        s = e0 + e1
        p0, p1 = e0 / s, e1 / s
        lbj_ref[0:1, :] = (p0 + p1) - p0
        lbj_ref[1:2, :] = p0 * p1

    return _pc(body, name="ada_fwd",
               out_shape=[jax.ShapeDtypeStruct((nl, nseq, ncol), F32), jax.ShapeDtypeStruct((2, di), F32)]
               )(c_all, w_ada, b_cols, b_lb)


def _ada_bwd(c_all, dmod_cols, dmod_full):
    nl, nseq, ncol = dmod_cols.shape
    d = c_all.shape[1]
    d3 = dmod_full.shape[2]

    def body(c_ref, dc_ref, df_ref, gw_ref, gb_ref):
        cv = c_ref[...]
        cact = (cv * _sigmoid(cv)).astype(BF16)
        for l in range(nl):
            gw_ref[l] = _dot_tn(cact, dc_ref[l].astype(BF16))
            gb_ref[l:l + 1, :] = jnp.sum(df_ref[l], axis=0, keepdims=True)

    return _pc(body, name="ada_bwd",
               out_shape=[jax.ShapeDtypeStruct((nl, d, ncol), F32), jax.ShapeDtypeStruct((nl, d3), F32)]
               )(c_all, dmod_cols, dmod_full)


def _prenorm(x, gain, mod, t_seq, name):
    m, d = x.shape
    tm = _tile(t_seq, 1024)
    per = t_seq // tm

    def body(x_ref, g_ref, mod_ref, h_ref, ht_ref):
        xv = x_ref[...]
        rstd = lax.rsqrt(jnp.mean(xv * xv, axis=-1, keepdims=True) + EPS)
        r = xv * rstd * g_ref[...]
        h = r * (1.0 + mod_ref[0, 1:2, :]) + mod_ref[0, 0:1, :]
        h_ref[...] = h.astype(BF16)
        ht_ref[...] = h.T.astype(BF16)

    return _pc(body, name=name, out_shape=[jax.ShapeDtypeStruct((m, d), BF16), jax.ShapeDtypeStruct((d, m), BF16)],
               grid=(m // tm,),
               in_specs=[pl.BlockSpec((tm, d), lambda i: (i, 0)), pl.BlockSpec((1, d), lambda i: (0, 0)),
                         pl.BlockSpec((1, 3, d), lambda i: (i // per, 0, 0))],
               out_specs=[pl.BlockSpec((tm, d), lambda i: (i, 0)), pl.BlockSpec((d, tm), lambda i: (0, i))],
               sem=("parallel",))(x, gain, mod)


def _prenorm_bwd(dh, x, gain, mod, dxn, t_seq, name):
    m, d = x.shape
    nb = m // t_seq
    tm = _tile(t_seq, 1024)
    per = t_seq // tm

    def body(dh_ref, x_ref, g_ref, mod_ref, dxn_ref, dx_ref, dss_ref, dg_ref):
        i = pl.program_id(0)
        xv, dhv, g = x_ref[...], dh_ref[...], g_ref[...]
        rstd = lax.rsqrt(jnp.mean(xv * xv, axis=-1, keepdims=True) + EPS)
        xhat = xv * rstd
        dr = dhv * (1.0 + mod_ref[0, 1:2, :])
        dxhat = dr * g
        dx_ref[...] = dxn_ref[...] + rstd * (dxhat - xhat * jnp.mean(dxhat * xhat, axis=-1, keepdims=True))

        @pl.when(i % per == 0)
        def _():
            dss_ref[...] = jnp.zeros_like(dss_ref)

        @pl.when(i == 0)
        def _():
            dg_ref[...] = jnp.zeros_like(dg_ref)

        dss_ref[0, 0:1, :] += jnp.sum(dhv, axis=0, keepdims=True)
        dss_ref[0, 1:2, :] += jnp.sum(dhv * (xhat * g), axis=0, keepdims=True)
        dg_ref[...] += jnp.sum(dr * xhat, axis=0, keepdims=True)

    row = pl.BlockSpec((tm, d), lambda i: (i, 0))
    return _pc(body, name=name,
               out_shape=[jax.ShapeDtypeStruct((m, d), F32), jax.ShapeDtypeStruct((nb, 2, d), F32),
                          jax.ShapeDtypeStruct((1, d), F32)],
               grid=(m // tm,),
               in_specs=[row, row, pl.BlockSpec((1, d), lambda i: (0, 0)),
                         pl.BlockSpec((1, 3, d), lambda i: (i // per, 0, 0)), row],
               out_specs=[row, pl.BlockSpec((1, 2, d), lambda i: (i // per, 0, 0)),
                          pl.BlockSpec((1, d), lambda i: (0, 0))],
               sem=("arbitrary",))(dh, x, gain, mod, dxn)


def _mm_in(h, ws, sections, name, comm=None):
    m, k = h.shape
    nw = len(ws)
    widths = [w.shape[2] for w in ws]
    offs = [sum(widths[:a]) for a in range(nw)]
    nc = sum(widths)
    per = NDEV // sections if sections > 1 else NDEV
    tm = _din_tile(m)
    assert per % 2 == 0

    def body(*refs):
        hv = refs[0][...]
        o_ref = refs[1 + nw]
        for b in range(2):
            for a in range(nw):
                lo = b * nc + offs[a]
                o_ref[:, lo:lo + widths[a]] = _dot(hv, refs[1 + a][b])

    w_specs = [pl.BlockSpec((2, k, wd), lambda j, i: (j, 0, 0)) for wd in widths]
    if sections > 1:
        out_shape = jax.ShapeDtypeStruct((sections, m, per * nc), F32)
        out_spec = pl.BlockSpec((None, tm, 2 * nc), lambda j, i: ((2 * j) // per, i, ((2 * j) % per) // 2))
    else:
        out_shape = jax.ShapeDtypeStruct((m, NDEV * nc), F32)
        out_spec = pl.BlockSpec((tm, 2 * nc), lambda j, i: (i, j))
    return _pc(body, name=name, out_shape=out_shape, grid=(NDEV // 2, m // tm),
               in_specs=[pl.BlockSpec((tm, k), lambda j, i: (i, 0))] + w_specs,
               out_specs=out_spec, sem=("parallel", "parallel"), comm=comm)(h, *ws)


def _din_tile(m):
    return 1024 if m % 1024 == 0 and m >= 2048 else _tile(m, 512)


def _mm_din(dproj, ws, sections, name, comm=None, tiles=None, prev=None):
    nw, k = len(ws), ws[0].shape[1]
    widths = [w.shape[2] for w in ws]
    offs = [sum(widths[:a]) for a in range(nw)]
    nc = sum(widths)
    m = dproj.shape[-2]
    tm = _din_tile(m)
    t0, nt = tiles if tiles is not None else (0, m // tm)
    per = NDEV // sections if sections > 1 else NDEV
    assert per % 2 == 0

    def body(*refs):
        d_ref, o_ref = refs[0], refs[-1]
        j = pl.program_id(1)
        acc = None
        for b in range(2):
            for a in range(nw):
                lo = b * nc + offs[a]
                term = _dot_nt(d_ref[:, lo:lo + widths[a]], refs[1 + a][b])
                acc = term if acc is None else acc + term

        @pl.when(j == 0)
        def _():
            o_ref[...] = acc

        @pl.when(j > 0)
        def _():
            o_ref[...] += acc

    if sections > 1:
        dspec = pl.BlockSpec((None, tm, 2 * nc), lambda i, j: ((2 * j) // per, i + t0, ((2 * j) % per) // 2))
    else:
        dspec = pl.BlockSpec((tm, 2 * nc), lambda i, j: (i + t0, j))
    in_specs = [dspec] + [pl.BlockSpec((2, k, wd), lambda i, j: (j, 0, 0)) for wd in widths]
    args = [dproj, *ws]
    if prev is not None:
        in_specs.append(ANY)
        args.append(prev)
    return _pc(body, name=name, out_shape=jax.ShapeDtypeStruct((m, k), F32), grid=(nt, NDEV // 2), in_specs=in_specs,
               out_specs=pl.BlockSpec((tm, k), lambda i, j: (i + t0, 0)), sem=("parallel", "arbitrary"),
               comm=comm, aliases={1 + nw: 0} if prev is not None else None)(*args)


def _mm_dw_in(ht, dproj, nc, sections, name, comm=None):
    k, m = ht.shape
    per = NDEV // sections if sections > 1 else NDEV

    def body(h_ref, d_ref, o_ref):
        o_ref[...] = _dot(h_ref[...], d_ref[...])

    if sections > 1:
        dspec = pl.BlockSpec((None, m, nc), lambda j: (j // per, 0, j % per))
    else:
        dspec = pl.BlockSpec((m, nc), lambda j: (0, j))
    return _pc(body, name=name, out_shape=jax.ShapeDtypeStruct((NDEV, k, nc), F32), grid=(NDEV,),
               in_specs=[pl.BlockSpec((k, m), lambda j: (0, 0)), dspec],
               out_specs=pl.BlockSpec((None, k, nc), lambda j: (j, 0, 0)),
               sem=("parallel",), comm=comm)(ht, dproj)


def _out_proj(ybr, w_out, x, mod, t_seq, name, comm=None):
    m, di = ybr.shape
    d = w_out.shape[1]
    tm = _tile(t_seq, 1024)
    per = t_seq // tm

    def body(y_ref, w_ref, x_ref, mod_ref, yo_ref, xn_ref):
        yo = _dot(y_ref[...], w_ref[...])
        yo_ref[...] = yo
        xn_ref[...] = x_ref[...] + mod_ref[0, 2:3, :] * yo

    row = pl.BlockSpec((tm, d), lambda i: (i, 0))
    return _pc(body, name=name,
               out_shape=[jax.ShapeDtypeStruct((m, d), F32), jax.ShapeDtypeStruct((m, d), F32)],
               grid=(m // tm,),
               in_specs=[pl.BlockSpec((tm, di), lambda i: (i, 0)), pl.BlockSpec((di, d), lambda i: (0, 0)), row,
                         pl.BlockSpec((1, 3, d), lambda i: (i // per, 0, 0))],
               out_specs=[row, row], sem=("parallel",), comm=comm)(ybr, w_out, x, mod)


def _out_proj_loss(ybr, w_out, x, mod, gain, target, t_seq):
    m, di = ybr.shape
    d = w_out.shape[1]
    tm = _tile(t_seq, 512)
    per = t_seq // tm

    def body(y_ref, w_ref, x_ref, mod_ref, g_ref, t_ref, yo_ref, dx_ref, loss_ref, dg_ref):
        i = pl.program_id(0)
        yo = _dot(y_ref[...], w_ref[...])
        yo_ref[...] = yo
        xv = x_ref[...] + mod_ref[0, 2:3, :] * yo
        g = g_ref[...]
        rstd = lax.rsqrt(jnp.mean(xv * xv, axis=-1, keepdims=True) + EPS)
        xhat = xv * rstd
        err = xhat * g - t_ref[...]
        dy = err * (1.0 / d)
        dxhat = dy * g
        dx_ref[...] = rstd * (dxhat - xhat * jnp.mean(dxhat * xhat, axis=-1, keepdims=True))

        @pl.when(i == 0)
        def _():
            loss_ref[...] = jnp.zeros_like(loss_ref)
            dg_ref[...] = jnp.zeros_like(dg_ref)

        loss_ref[...] += 0.5 * jnp.sum(jnp.mean(err * err, axis=-1, keepdims=True), axis=0, keepdims=True)
        dg_ref[...] += jnp.sum(dy * xhat, axis=0, keepdims=True)

    row = pl.BlockSpec((tm, d), lambda i: (i, 0))
    vec = pl.BlockSpec((1, d), lambda i: (0, 0))
    return _pc(body, name="out_proj_loss",
               out_shape=[jax.ShapeDtypeStruct((m, d), F32), jax.ShapeDtypeStruct((m, d), F32),
                          jax.ShapeDtypeStruct((1, 1), F32), jax.ShapeDtypeStruct((1, d), F32)],
               grid=(m // tm,),
               in_specs=[pl.BlockSpec((tm, di), lambda i: (i, 0)), pl.BlockSpec((di, d), lambda i: (0, 0)), row,
                         pl.BlockSpec((1, 3, d), lambda i: (i // per, 0, 0)), vec, row],
               out_specs=[row, row, pl.BlockSpec((1, 1), lambda i: (0, 0)), vec],
               sem=("arbitrary",))(ybr, w_out, x, mod, gain, target)


def _gate_dybr(dxn, yout, mod, w_out, t_seq, name):
    m, d = dxn.shape
    di = w_out.shape[0]
    nb = m // t_seq
    tm = _tile(t_seq, 1024)
    per = t_seq // tm

    def body(dxn_ref, yo_ref, mod_ref, w_ref, dy_ref, dgate_ref, o_ref):
        i = pl.program_id(0)
        dv = dxn_ref[...]
        dy = (mod_ref[0, 2:3, :] * dv).astype(BF16)
        dy_ref[...] = dy
        o_ref[...] = _dot_nt(dy, w_ref[...])

        @pl.when(i % per == 0)
        def _():
            dgate_ref[...] = jnp.zeros_like(dgate_ref)

        dgate_ref[0] += jnp.sum(dv * yo_ref[...], axis=0, keepdims=True)

    row = pl.BlockSpec((tm, d), lambda i: (i, 0))
    return _pc(body, name=name,
               out_shape=[jax.ShapeDtypeStruct((m, d), BF16), jax.ShapeDtypeStruct((nb, 1, d), F32),
                          jax.ShapeDtypeStruct((m, di), F32)],
               grid=(m // tm,),
               in_specs=[row, row, pl.BlockSpec((1, 3, d), lambda i: (i // per, 0, 0)),
                         pl.BlockSpec((di, d), lambda i: (0, 0))],
               out_specs=[row, pl.BlockSpec((1, 1, d), lambda i: (i // per, 0, 0)),
                          pl.BlockSpec((tm, di), lambda i: (i, 0))],
               sem=("arbitrary",))(dxn, yout, mod, w_out)


def _mm_dw_out(ybr, dy, name, comm=None):
    m, di = ybr.shape
    d = dy.shape[1]
    tn = _tile(di, 1024)

    def body(y_ref, dy_ref, o_ref):
        o_ref[...] = _dot_tn(y_ref[...], dy_ref[...])

    return _pc(body, name=name, out_shape=jax.ShapeDtypeStruct((di, d), F32), grid=(di // tn,),
               in_specs=[pl.BlockSpec((m, tn), lambda n: (0, n)), pl.BlockSpec((m, d), lambda n: (0, 0))],
               out_specs=pl.BlockSpec((tn, d), lambda n: (n, 0)), sem=("parallel",), comm=comm)(ybr, dy)


def _sgu_mask():
    t = lax.broadcasted_iota(jnp.int32, (SG_BLOCK, SG_BLOCK), 0)
    s = lax.broadcasted_iota(jnp.int32, (SG_BLOCK, SG_BLOCK), 1)
    return (s // CHUNK) <= (t // CHUNK)


def _a_mid_fwd(proj, ln_g, ln_b, w_s, bs_t, t_seq, comm=None):
    m, n3 = proj.shape
    di = n3 // 3
    gd = di // SG_GROUPS
    r = _tile(t_seq, 256)
    nblk = r // SG_BLOCK

    def body(p_ref, lg_ref, lb_ref, ws_ref, bs_ref, ybr_ref, s_scr):
        v = _gelu(p_ref[:, di:2 * di])
        mu = jnp.mean(v, axis=-1, keepdims=True)
        vc = v - mu
        rstd = lax.rsqrt(jnp.mean(vc * vc, axis=-1, keepdims=True) + EPS)
        vb = (vc * rstd * lg_ref[...] + lb_ref[...]).astype(BF16)
        mask = _sgu_mask()
        for gi in range(SG_GROUPS):
            ws = jnp.where(mask, ws_ref[gi], 0.0).astype(BF16)
            bcol = bs_ref[:, gi:gi + 1]
            for b in range(nblk):
                rows = slice(b * SG_BLOCK, (b + 1) * SG_BLOCK)
                cols = slice(gi * gd, (gi + 1) * gd)
                s_scr[rows, cols] = _dot(ws, vb[rows, cols]) + bcol
        gg = p_ref[:, 2 * di:]
        ybr_ref[...] = (_gelu(p_ref[:, :di]) * s_scr[...] * (gg * _sigmoid(gg))).astype(BF16)

    vec = pl.BlockSpec((1, di), lambda i: (0, 0))
    return _pc(body, name="a_mid_fwd", out_shape=jax.ShapeDtypeStruct((m, di), BF16), grid=(m // r,),
               in_specs=[pl.BlockSpec((r, n3), lambda i: (i, 0)), vec, vec,
                         pl.BlockSpec((SG_GROUPS, SG_BLOCK, SG_BLOCK), lambda i: (0, 0, 0)),
                         pl.BlockSpec((SG_BLOCK, 128), lambda i: (0, 0))],
               out_specs=pl.BlockSpec((r, di), lambda i: (i, 0)),
               scratch=[pltpu.VMEM((r, di), F32)], sem=("parallel",), comm=comm)(proj, ln_g, ln_b, w_s, bs_t)


def _a_mid_bwd(proj, dybr, ln_g, ln_b, w_s, bs_t, t_seq, comm=None):
    m, n3 = proj.shape
    di = n3 // 3
    gd = di // SG_GROUPS
    r = _tile(t_seq, 256)
    nblk = r // SG_BLOCK

    def body(p_ref, dy_ref, lg_ref, lb_ref, ws_ref, bs_ref,
             dp_ref, dlg_ref, dlb_ref, dws_ref, dbs_ref, s_scr, dvl_scr):
        i = pl.program_id(0)

        @pl.when(i == 0)
        def _():
            dlg_ref[...] = jnp.zeros_like(dlg_ref)
            dlb_ref[...] = jnp.zeros_like(dlb_ref)
            dws_ref[...] = jnp.zeros_like(dws_ref)
            dbs_ref[...] = jnp.zeros_like(dbs_ref)

        v, dgelu_v = _gelu_and_grad(p_ref[:, di:2 * di])
        mu = jnp.mean(v, axis=-1, keepdims=True)
        vc = v - mu
        rstd = lax.rsqrt(jnp.mean(vc * vc, axis=-1, keepdims=True) + EPS)
        vhat = vc * rstd
        lg = lg_ref[...]
        vb = (vhat * lg + lb_ref[...]).astype(BF16)
        u, dgelu_u = _gelu_and_grad(p_ref[:, :di])
        gg = p_ref[:, 2 * di:]
        sg = _sigmoid(gg)
        dyv = dy_ref[...]
        dus = dyv * (gg * sg)
        dsb = (dus * u).astype(BF16)
        ds32 = dus * u
        mask = _sgu_mask()
        lane = lax.broadcasted_iota(jnp.int32, (SG_BLOCK, 128), 1)
        dbs_acc = jnp.zeros((SG_BLOCK, 128), F32)
        for gi in range(SG_GROUPS):
            ws = jnp.where(mask, ws_ref[gi], 0.0).astype(BF16)
            bcol = bs_ref[:, gi:gi + 1]
            cols = slice(gi * gd, (gi + 1) * gd)
            dws_acc = jnp.zeros((SG_BLOCK, SG_BLOCK), F32)
            dbs_col = jnp.zeros((SG_BLOCK, 1), F32)
            for b in range(nblk):
                rows = slice(b * SG_BLOCK, (b + 1) * SG_BLOCK)
                s_scr[rows, cols] = _dot(ws, vb[rows, cols]) + bcol
                dvl_scr[rows, cols] = _dot_tn(ws, dsb[rows, cols])
                dws_acc += _dot_nt(dsb[rows, cols], vb[rows, cols])
                dbs_col += jnp.sum(ds32[rows, cols], axis=-1, keepdims=True)
            dws_ref[gi] += jnp.where(mask, dws_acc, 0.0)
            dbs_acc += jnp.where(lane == gi, dbs_col, 0.0)
        dbs_ref[...] += dbs_acc
        s = s_scr[...]
        dp_ref[:, :di] = (dus * s * dgelu_u).astype(BF16)
        dp_ref[:, 2 * di:] = (dyv * u * s * (sg * (1.0 + gg * (1.0 - sg)))).astype(BF16)
        dvl = dvl_scr[...]
        dlg_ref[...] += jnp.sum(dvl * vhat, axis=0, keepdims=True)
        dlb_ref[...] += jnp.sum(dvl, axis=0, keepdims=True)
        dvh = dvl * lg
        dv = rstd * (dvh - jnp.mean(dvh, axis=-1, keepdims=True)
                     - vhat * jnp.mean(dvh * vhat, axis=-1, keepdims=True))
        dp_ref[:, di:2 * di] = (dv * dgelu_v).astype(BF16)

    vec = pl.BlockSpec((1, di), lambda i: (0, 0))
    wsb = pl.BlockSpec((SG_GROUPS, SG_BLOCK, SG_BLOCK), lambda i: (0, 0, 0))
    bsb = pl.BlockSpec((SG_BLOCK, 128), lambda i: (0, 0))
    return _pc(body, name="a_mid_bwd",
               out_shape=[jax.ShapeDtypeStruct((m, n3), BF16), jax.ShapeDtypeStruct((1, di), F32),
                          jax.ShapeDtypeStruct((1, di), F32),
                          jax.ShapeDtypeStruct((SG_GROUPS, SG_BLOCK, SG_BLOCK), F32),
                          jax.ShapeDtypeStruct((SG_BLOCK, 128), F32)],
               grid=(m // r,),
               in_specs=[pl.BlockSpec((r, n3), lambda i: (i, 0)), pl.BlockSpec((r, di), lambda i: (i, 0)),
                         vec, vec, wsb, bsb],
               out_specs=[pl.BlockSpec((r, n3), lambda i: (i, 0)), vec, vec, wsb, bsb],
               scratch=[pltpu.VMEM((r, di), F32), pltpu.VMEM((r, di), F32)],
               sem=("arbitrary",), comm=comm)(proj, dybr, ln_g, ln_b, w_s, bs_t)


def _chunk_rows(n):
    if isinstance(n, int):
        return pl.ds(n * CHUNK, CHUNK)
    return pl.ds(pl.multiple_of(n * CHUNK, CHUNK), CHUNK)


def _hgrn_dims(t_seq, di):
    tr = _tile(t_seq, 128)
    hc = _tile(di, 2048)
    return tr, hc, hc // HEAD_DIM


def _hgrn_gates(f_ref, lb, a_scr, k_scr, tr):
    sig = _sigmoid(f_ref[...])
    fg = lb + (1.0 - lb) * sig
    k_scr[...] = 1.0 - fg
    logf = jnp.log(fg)
    g = min(CUM_ROWS, tr)
    tri = _tri_mask(g, reverse=False)
    for rg in range(tr // g):
        a_scr[rg * g:(rg + 1) * g, :] = _tri_apply(tri, logf[rg * g:(rg + 1) * g, :])
    return sig, fg


def _hgrn_fwd(proj, lbj, gn, nb, t_seq):
    _, m, di = proj.shape
    tr, hc, hpg = _hgrn_dims(t_seq, di)
    nt, nhg, ncl = t_seq // tr, di // hc, tr // CHUNK
    nheads = di // HEAD_DIM

    def body(p_ref, lb_ref, gn_ref, o_ref, ybr_ref, st_ref, st_scr, a_scr, k_scr):
        q_ref, f_ref, i_ref, g_ref = (p_ref.at[s] for s in range(4))
        t = pl.program_id(2)

        @pl.when(t == 0)
        def _():
            st_scr[...] = jnp.zeros_like(st_scr)

        _hgrn_gates(f_ref, lb_ref[0:1, :], a_scr, k_scr, tr)
        gnv = gn_ref[...]
        rr = lax.broadcasted_iota(jnp.int32, (CHUNK, CHUNK), 0)
        cc = lax.broadcasted_iota(jnp.int32, (CHUNK, CHUNK), 1)
        causal = cc <= rr

        def chunk(n, carry):
            rows = _chunk_rows(n)
            lanes = [slice(hd * HEAD_DIM, (hd + 1) * HEAD_DIM) for hd in range(hpg)]
            hs = []
            for hd, ls in enumerate(lanes):
                h = {}
                ah, kh = a_scr[rows, ls], k_scr[rows, ls]
                qp = q_ref[rows, ls]
                qh = qp * _sigmoid(qp)
                h["vb"] = i_ref[rows, ls].astype(BF16)
                aref, alast = ah[CHUNK // 2 - 1:CHUNK // 2, :], ah[CHUNK - 1:CHUNK, :]
                h["q_in"] = (qh * jnp.exp(ah - aref)).astype(BF16)
                h["k_in"] = (kh * jnp.exp(aref - ah)).astype(BF16)
                h["q_out"] = (qh * jnp.exp(ah)).astype(BF16)
                h["k_out"] = (kh * jnp.exp(alast - ah)).astype(BF16)
                h["dec"] = jnp.exp(alast)
                st = st_scr[hd]
                st_ref[n, hd] = st
                h["st"] = st
                hs.append(h)
            for h in hs:
                h["scores"] = _dot_nt(h["q_in"], h["k_in"])
                h["o_inter"] = _dot_nt(h["q_out"], h["st"].astype(BF16))
                h["st_mm"] = _dot_tn(h["vb"], h["k_out"])
            for h in hs:
                h["o"] = _dot(jnp.where(causal, h["scores"], 0.0).astype(BF16), h["vb"]) + h["o_inter"]
            for hd, (h, ls) in enumerate(zip(hs, lanes)):
                st_scr[hd] = h["st"] * h["dec"] + h["st_mm"]
                o = h["o"]
                o_ref[rows, ls] = o
                rstd = lax.rsqrt(jnp.mean(o * o, axis=-1, keepdims=True) + EPS)
                gg = g_ref[rows, ls]
                ybr_ref[rows, ls] = ((o * rstd * gnv) * (gg * _sigmoid(gg))).astype(BF16)
            return carry

        lax.fori_loop(0, ncl, chunk, 0)

    blk = pl.BlockSpec((tr, hc), lambda hg, b, t: (b * nt + t, hg))
    return _pc(body, name="hgrn_fwd",
               out_shape=[jax.ShapeDtypeStruct((m, di), F32), jax.ShapeDtypeStruct((m, di), BF16),
                          jax.ShapeDtypeStruct((m // CHUNK, nheads, HEAD_DIM, HEAD_DIM), F32)],
               grid=(nhg, nb, nt),
               in_specs=[pl.BlockSpec((4, tr, hc), lambda hg, b, t: (0, b * nt + t, hg)),
                         pl.BlockSpec((2, hc), lambda hg, b, t: (0, hg)),
                         pl.BlockSpec((1, HEAD_DIM), lambda hg, b, t: (0, 0))],
               out_specs=[blk, blk, pl.BlockSpec((ncl, hpg, HEAD_DIM, HEAD_DIM),
                                                 lambda hg, b, t: (b * nt + t, hg, 0, 0))],
               scratch=[pltpu.VMEM((hpg, HEAD_DIM, HEAD_DIM), F32), pltpu.VMEM((tr, hc), F32),
                        pltpu.VMEM((tr, hc), F32)],
               sem=("parallel", "arbitrary", "arbitrary"))(proj, lbj, gn)


def _hgrn_bwd(proj, o_all, dybr, states, lbj, gn, nb, t_seq, comm=None):
    _, m, di = proj.shape
    tr, hc, hpg = _hgrn_dims(t_seq, di)
    nt, nhg, ncl = t_seq // tr, di // hc, tr // CHUNK

    def body(p_ref, o_ref, dy_ref, st_ref, lb_ref, gn_ref,
             dp_ref, dlb_ref, dgn_ref, dst_scr, a_scr, k_scr, da_scr, dk_scr):
        q_ref, f_ref, i_ref, g_ref = (p_ref.at[s] for s in range(4))
        hg, b, t = pl.program_id(0), pl.program_id(1), pl.program_id(2)

        @pl.when(t == 0)
        def _():
            dst_scr[...] = jnp.zeros_like(dst_scr)

        @pl.when((b == 0) & (t == 0))
        def _():
            dlb_ref[...] = jnp.zeros_like(dlb_ref)

        @pl.when((hg == 0) & (b == 0) & (t == 0))
        def _():
            dgn_ref[...] = jnp.zeros_like(dgn_ref)

        lb = lb_ref[0:1, :]
        sig, fg = _hgrn_gates(f_ref, lb, a_scr, k_scr, tr)
        gnv = gn_ref[...]
        rr = lax.broadcasted_iota(jnp.int32, (CHUNK, CHUNK), 0)
        cc = lax.broadcasted_iota(jnp.int32, (CHUNK, CHUNK), 1)
        causal = cc <= rr
        rowi = lax.broadcasted_iota(jnp.int32, (CHUNK, HEAD_DIM), 0)

        def chunk(it, carry):
            n = ncl - 1 - it
            rows = _chunk_rows(n)
            for hd0 in range(0, hpg, PHASE_HEADS):
                heads(n, rows, range(hd0, min(hpg, hd0 + PHASE_HEADS)))
            return carry

        def heads(n, rows, ids):
            lanes = [slice(hd * HEAD_DIM, (hd + 1) * HEAD_DIM) for hd in ids]
            hs = []
            for hd, ls in zip(ids, lanes):
                h = {}
                ah, kh = a_scr[rows, ls], k_scr[rows, ls]
                qp = q_ref[rows, ls]
                sq = _sigmoid(qp)
                qh = qp * sq
                h["dsilu_q"] = sq * (1.0 + qp * (1.0 - sq))
                h["vb"] = i_ref[rows, ls].astype(BF16)
                aref, alast = ah[CHUNK // 2 - 1:CHUNK // 2, :], ah[CHUNK - 1:CHUNK, :]
                h["e1"], h["e2"] = jnp.exp(ah - aref), jnp.exp(aref - ah)
                h["e3"], h["e4"] = jnp.exp(ah), jnp.exp(alast - ah)
                h["dec"] = jnp.exp(alast)
                h["q_in"], h["k_in"], h["q_out"], h["k_out"] = qh * h["e1"], kh * h["e2"], qh * h["e3"], kh * h["e4"]
                for nm in ("q_in", "k_in", "q_out", "k_out"):
                    h[nm + "_b"] = h[nm].astype(BF16)
                o = o_ref[rows, ls]
                rstd = lax.rsqrt(jnp.mean(o * o, axis=-1, keepdims=True) + EPS)
                ohat = o * rstd
                gg = g_ref[rows, ls]
                sg = _sigmoid(gg)
                dyv = dy_ref[rows, ls]
                d_on = dyv * (gg * sg)
                dp_ref[3, rows, ls] = (dyv * (ohat * gnv) * (sg * (1.0 + gg * (1.0 - sg)))).astype(BF16)
                h["dgn"] = jnp.sum(d_on * ohat, axis=0, keepdims=True)
                dohat = d_on * gnv
                do = rstd * (dohat - ohat * jnp.mean(dohat * ohat, axis=-1, keepdims=True))
                h["do_b"] = do.astype(BF16)
                h["st_prev"] = st_ref[n, hd]
                h["dst"] = dst_scr[hd]
                hs.append(h)
            for h in hs:
                dst_b = h["dst"].astype(BF16)
                h["scores"] = _dot_nt(h["q_in_b"], h["k_in_b"])
                h["dscores"] = _dot_nt(h["do_b"], h["vb"])
                h["dv_inter"] = _dot_nt(h["k_out_b"], dst_b)
                h["dq_out"] = _dot(h["do_b"], h["st_prev"].astype(BF16))
                h["dk_out"] = _dot(h["vb"], dst_b)
                h["dst_mm"] = _dot_tn(h["do_b"], h["q_out_b"])
            for h in hs:
                scores = jnp.where(causal, h["scores"], 0.0).astype(BF16)
                dscores = jnp.where(causal, h["dscores"], 0.0).astype(BF16)
                h["dv"] = _dot_tn(scores, h["do_b"]) + h["dv_inter"]
                h["dq_in"] = _dot(dscores, h["k_in_b"])
                h["dk_in"] = _dot_tn(dscores, h["q_in_b"])
            dgn = hs[0]["dgn"]
            for h in hs[1:]:
                dgn = dgn + h["dgn"]
            dgn_ref[...] += dgn
            for hd, h, ls in zip(ids, hs, lanes):
                ddec = jnp.sum(h["dst"] * h["st_prev"], axis=0, keepdims=True)
                dst_scr[hd] = h["dst"] * h["dec"] + h["dst_mm"]
                dp_ref[2, rows, ls] = h["dv"].astype(BF16)
                dq = h["dq_in"] * h["e1"] + h["dq_out"] * h["e3"]
                dp_ref[0, rows, ls] = (dq * h["dsilu_q"]).astype(BF16)
                dk_scr[rows, ls] = h["dk_in"] * h["e2"] + h["dk_out"] * h["e4"]
                t_in = h["dq_in"] * h["q_in"] - h["dk_in"] * h["k_in"]
                t_out = h["dk_out"] * h["k_out"]
                da = t_in + h["dq_out"] * h["q_out"] - t_out
                da_ref_row = -jnp.sum(t_in, axis=0, keepdims=True)
                da_last_row = jnp.sum(t_out, axis=0, keepdims=True) + ddec * h["dec"]
                da = da + jnp.where(rowi == CHUNK // 2 - 1, da_ref_row, 0.0) \
                        + jnp.where(rowi == CHUNK - 1, da_last_row, 0.0)
                da_scr[rows, ls] = da

        if ncl <= 2:
            for it in range(ncl):
                chunk(it, 0)
        else:
            lax.fori_loop(0, ncl, chunk, 0)
        g = min(CUM_ROWS, tr)
        tri = _tri_mask(g, reverse=True)
        for rg in range(tr // g):
            rs = slice(rg * g, (rg + 1) * g)
            dlogf = _tri_apply(tri, da_scr[rs, :])
            df = dlogf / fg[rs, :] - dk_scr[rs, :]
            sgr = sig[rs, :]
            dp_ref[1, rs, :] = (df * (1.0 - lb) * (sgr * (1.0 - sgr))).astype(BF16)
            dlb_ref[...] += jnp.sum(df * (1.0 - sgr), axis=0, keepdims=True) * lb_ref[1:2, :]

    blk = pl.BlockSpec((tr, hc), lambda hg, b, t: (b * nt + (nt - 1 - t), hg))
    return _pc(body, name="hgrn_bwd",
               out_shape=[jax.ShapeDtypeStruct((4, m, di), BF16), jax.ShapeDtypeStruct((1, di), F32),
                          jax.ShapeDtypeStruct((1, HEAD_DIM), F32)],
               grid=(nhg, nb, nt),
               in_specs=[pl.BlockSpec((4, tr, hc), lambda hg, b, t: (0, b * nt + (nt - 1 - t), hg)), blk, blk,
                         pl.BlockSpec((ncl, hpg, HEAD_DIM, HEAD_DIM),
                                      lambda hg, b, t: (b * nt + (nt - 1 - t), hg, 0, 0)),
                         pl.BlockSpec((2, hc), lambda hg, b, t: (0, hg)),
                         pl.BlockSpec((1, HEAD_DIM), lambda hg, b, t: (0, 0))],
               out_specs=[pl.BlockSpec((4, tr, hc), lambda hg, b, t: (0, b * nt + (nt - 1 - t), hg)),
                          pl.BlockSpec((1, hc), lambda hg, b, t: (0, hg)),
                          pl.BlockSpec((1, HEAD_DIM), lambda hg, b, t: (0, 0))],
               scratch=[pltpu.VMEM((hpg, HEAD_DIM, HEAD_DIM), F32)] + [pltpu.VMEM((tr, hc), F32)] * 4,
               sem=("arbitrary", "arbitrary", "arbitrary"), comm=comm)(
                   proj, o_all, dybr, states, lbj, gn)


def _adamw(parts, w, m, v, name):
    r, c = w.shape
    tr = _tile(r, 256)
    npart = len(parts)
    c1 = 1.0 - ADAM_B1 ** ADAM_STEP
    c2 = 1.0 - ADAM_B2 ** ADAM_STEP

    def body(*refs):
        p_refs = refs[:npart]
        _adamw_math(p_refs, *refs[npart:], c1, c2)

    blk = pl.BlockSpec((tr, c), lambda i: (i, 0))
    return _pc(body, name=name, out_shape=[jax.ShapeDtypeStruct((r, c), F32)] * 4, grid=(r // tr,),
               in_specs=[blk] * (npart + 3), out_specs=[blk] * 4, sem=("parallel",))(*parts, w, m, v)


def _adamw_math(p_refs, w_ref, m_ref, v_ref, g_ref, d_ref, nm_ref, nv_ref, c1, c2):
    g = p_refs[0][...].astype(F32)
    for p in p_refs[1:]:
        g = g + p[...].astype(F32)
    nm = ADAM_B1 * m_ref[...] + (1.0 - ADAM_B1) * g
    nv = ADAM_B2 * v_ref[...] + (1.0 - ADAM_B2) * (g * g)
    g_ref[...] = g
    nm_ref[...] = nm
    nv_ref[...] = nv
    d_ref[...] = -ADAM_LR * ((nm / c1) / (jnp.sqrt(nv / c2) + ADAM_EPS) + ADAM_WD * w_ref[...])


def _adamw_small(gathered, ws, ms, vs, name, sums=()):
    n, ns = len(ws), len(sums)
    c1 = 1.0 - ADAM_B1 ** ADAM_STEP
    c2 = 1.0 - ADAM_B2 ** ADAM_STEP

    def total(ref):
        g = ref[0]
        for dev in range(1, NDEV):
            g = g + ref[dev]
        return g

    def body(*refs):
        g_in, w_in, m_in, v_in = refs[:n], refs[n:2 * n], refs[2 * n:3 * n], refs[3 * n:4 * n]
        s_in = refs[4 * n:4 * n + ns]
        outs = refs[4 * n + ns:]
        for k in range(n):
            g = total(g_in[k])
            nm = ADAM_B1 * m_in[k][...] + (1.0 - ADAM_B1) * g
            nv = ADAM_B2 * v_in[k][...] + (1.0 - ADAM_B2) * (g * g)
            outs[4 * k][...] = g
            outs[4 * k + 1][...] = -ADAM_LR * ((nm / c1) / (jnp.sqrt(nv / c2) + ADAM_EPS) + ADAM_WD * w_in[k][...])
            outs[4 * k + 2][...] = nm
            outs[4 * k + 3][...] = nv
        for k in range(ns):
            outs[4 * n + k][...] = total(s_in[k])

    out_shape = [jax.ShapeDtypeStruct(w.shape, F32) for w in ws for _ in range(4)]
    out_shape += [jax.ShapeDtypeStruct(s.shape[1:], F32) for s in sums]
    res = _pc(body, name=name, out_shape=out_shape)(*gathered, *ws, *ms, *vs, *sums)
    return [res[4 * k:4 * k + 4] for k in range(n)] + list(res[4 * n:])


def _adamw_blocks(parts, idx, w, m, v, name):
    r, c = w.shape
    tr = _tile(r, 256)
    npart = len(parts)
    c1 = 1.0 - ADAM_B1 ** ADAM_STEP
    c2 = 1.0 - ADAM_B2 ** ADAM_STEP

    def body(idx_ref, *refs):
        _adamw_math(refs[:npart], *refs[npart:], c1, c2)

    def sel(p):
        return pl.BlockSpec((None, tr, c), lambda i, s: (s[p], i, 0))

    blk = pl.BlockSpec((tr, c), lambda i, s: (i, 0))
    gs = pltpu.PrefetchScalarGridSpec(num_scalar_prefetch=1, grid=(r // tr,),
                                      in_specs=[sel(p) for p in range(npart)] + [blk] * 3, out_specs=[blk] * 4)
    return _pc(body, name=name, out_shape=[jax.ShapeDtypeStruct((r, c), F32)] * 4, grid_spec=gs,
               sem=("parallel",))(idx, *parts, w, m, v)


_EARLY = ["a_ln_gain", "a_ln_bias", "a_w_s", "a_b_s", "b_lower_bounds", "b_gn_gain"]


def _pack(arrs):
    flat = jnp.concatenate([a.reshape(-1) for a in arrs])
    rows = -(-flat.shape[0] // 1024) * 8
    return jnp.pad(flat, (0, rows * 128 - flat.shape[0])).reshape(rows, 128)


def _unpack(buf, like):
    flat = buf.reshape(-1)
    out, off = [], 0
    for a in like:
        out.append(flat[off:off + a.size].reshape(a.shape))
        off += a.size
    return out


def kernel(x, c, norm_gain, w_ada, b_ada, a_w_in, a_ln_gain, a_ln_bias, a_w_s, a_b_s, a_w_out, b_w_in, b_lower_bounds, b_gn_gain, b_w_out, final_gain, loss_target, m_norm_gain, m_w_ada, m_b_ada, m_a_w_in, m_a_ln_gain, m_a_ln_bias, m_a_w_s, m_a_b_s, m_a_w_out, m_b_w_in, m_b_lower_bounds, m_b_gn_gain, m_b_w_out, m_final_gain, v_norm_gain, v_w_ada, v_b_ada, v_a_w_in, v_a_ln_gain, v_a_ln_bias, v_a_w_s, v_a_b_s, v_a_w_out, v_b_w_in, v_b_lower_bounds, v_b_gn_gain, v_b_w_out, v_final_gain):
    w = dict(norm_gain=norm_gain, w_ada=w_ada, b_ada=b_ada, a_w_in=a_w_in, a_ln_gain=a_ln_gain,
             a_ln_bias=a_ln_bias, a_w_s=a_w_s, a_b_s=a_b_s, a_w_out=a_w_out, b_w_in=b_w_in,
             b_lower_bounds=b_lower_bounds, b_gn_gain=b_gn_gain, b_w_out=b_w_out, final_gain=final_gain)
    mo = dict(norm_gain=m_norm_gain, w_ada=m_w_ada, b_ada=m_b_ada, a_w_in=m_a_w_in, a_ln_gain=m_a_ln_gain,
              a_ln_bias=m_a_ln_bias, a_w_s=m_a_w_s, a_b_s=m_a_b_s, a_w_out=m_a_w_out, b_w_in=m_b_w_in,
              b_lower_bounds=m_b_lower_bounds, b_gn_gain=m_b_gn_gain, b_w_out=m_b_w_out, final_gain=m_final_gain)
    vo = dict(norm_gain=v_norm_gain, w_ada=v_w_ada, b_ada=v_b_ada, a_w_in=v_a_w_in, a_ln_gain=v_a_ln_gain,
              a_ln_bias=v_a_ln_bias, a_w_s=v_a_w_s, a_b_s=v_a_b_s, a_w_out=v_a_w_out, b_w_in=v_b_w_in,
              b_lower_bounds=v_b_lower_bounds, b_gn_gain=v_b_gn_gain, b_w_out=v_b_w_out, final_gain=v_final_gain)

    nb, t_seq, d = x.shape
    m = nb * t_seq
    ncol_ada = w_ada.shape[2]
    xi, yi, ci = lax.axis_index("x"), lax.axis_index("y"), lax.axis_index("c")
    me = 4 * xi + 2 * yi + ci

    c_g, wa_in_g = _all_gather([c, a_w_in[0].astype(BF16)], "gather_c_wa")

    c_all = c_g.reshape(NDEV * nb, d)
    b_cols = lax.dynamic_slice(b_ada, (0, me * ncol_ada), (2, ncol_ada)).reshape(2, 1, ncol_ada)
    mod_part, lbj = _ada_fwd(c_all, w_ada, b_cols, b_lower_bounds)
    mod_all = _all_gather([mod_part], "gather_mod")[0]
    mod_mine = lax.dynamic_slice_in_dim(mod_all, me * nb, nb, axis=2)
    mod_mine = mod_mine.transpose(1, 2, 0, 3).reshape(2, nb, 3, d)
    mod0, mod1 = mod_mine[0], mod_mine[1]

    di = a_w_out.shape[1] * NDEV

    xf = x.reshape(m, d)
    tgt = loss_target.reshape(m, d)
    ng0, ng1 = norm_gain[0:1], norm_gain[1:2]
    ncb = b_w_in.shape[2]
    wb_lo, wb_hi = b_w_in[0][:, :ncb // 2].astype(BF16), b_w_in[0][:, ncb // 2:].astype(BF16)
    h0, h0_t = _prenorm(xf, ng0, mod0, t_seq, "prenorm_a")
    proj_a, half = _mm_in(h0, [wa_in_g], 1, "in_proj_a", comm=_gather_first([a_w_out[0].astype(BF16), wb_lo]))
    bs_t = jnp.pad(a_b_s[0].T, ((0, 0), (0, 128 - SG_GROUPS)))
    ybr_a, (wa_out_g, wb_lo_g, wb_hi_half) = _a_mid_fwd(
        proj_a, a_ln_gain, a_ln_bias, a_w_s[0], bs_t, t_seq, comm=_join(_gather_second(half), _gather_first([wb_hi])))
    wa_out = wa_out_g.reshape(di, d)
    (yout_a, x1), (wb_hi_g, wb_out_half) = _out_proj(
        ybr_a, wa_out, xf, mod0, t_seq, "out_proj_a",
        comm=_join(_gather_second([wb_hi_half]), _gather_first([b_w_out[0].astype(BF16)])))
    wb_in_g = [wb_lo_g, wb_hi_g]
    h1, h1_t = _prenorm(x1, ng1, mod1, t_seq, "prenorm_b")
    proj_b, (wb_out_g,) = _mm_in(h1, wb_in_g, 4, "in_proj_b", comm=_gather_second([wb_out_half]))
    wb_out = wb_out_g.reshape(di, d)
    o_b, ybr_b, states = _hgrn_fwd(proj_b, lbj, b_gn_gain, nb, t_seq)
    yout_b, dx2, loss_part, d_final_gain = _out_proj_loss(ybr_b, wb_out, x1, mod1, final_gain.reshape(1, d), tgt, t_seq)

    rows_out = a_w_out.shape[1]
    dy_b, dgate1, dybr_b = _gate_dybr(dx2, yout_b, mod1, wb_out, t_seq, "dybr_b")
    rs_wb_out = _ReduceScatter(_mm_dw_out(ybr_b, dy_b, "dw_out_b").reshape(NDEV, rows_out, d), "b_w_out")
    (dproj_b, d_lb, d_gn), got = _hgrn_bwd(proj_b, o_b, dybr_b, states, lbj, b_gn_gain, nb, t_seq,
                                           comm=rs_wb_out.swap_core())
    rs_wb_out.after_core(got[0])
    dh1, got = _mm_din(dproj_b, wb_in_g, 4, "dh_b", comm=rs_wb_out.swap_chips())
    rs_wb_out.after_chips(got[0])
    dx1, dss1, dgain1 = _prenorm_bwd(dh1, x1, ng1, mod1, dx2, t_seq, "prenorm_bwd_b")
    rs_wb_in = _ReduceScatter(_mm_dw_in(h1_t, dproj_b, ncb, 4, "dw_in_b"), "b_w_in")

    dy_a, dgate0, dybr_a = _gate_dybr(dx1, yout_a, mod0, wa_out, t_seq, "dybr_a")
    g_wa_out, got = _mm_dw_out(ybr_a, dy_a, "dw_out_a", comm=rs_wb_in.swap_core())
    rs_wb_in.after_core(got[0])
    rs_wa_out = _ReduceScatter(g_wa_out.reshape(NDEV, rows_out, d), "a_w_out")
    (dproj_a, d_lng, d_lnb, d_ws, d_bs_t), got = _a_mid_bwd(
        proj_a, dybr_a, a_ln_gain, a_ln_bias, a_w_s[0], bs_t, t_seq,
        comm=_join(rs_wb_in.swap_chips(), rs_wa_out.swap_core()))
    rs_wb_in.after_chips(got[0])
    rs_wa_out.after_core(got[1])
    early_parts = [d_lng, d_lnb, d_ws.reshape(SG_GROUPS * SG_BLOCK, SG_BLOCK), d_bs_t[:, :SG_GROUPS].T,
                   jnp.concatenate([-d_lb, d_lb], axis=0), d_gn]
    g_wa_in, got = _mm_dw_in(h0_t, dproj_a, wa_in_g.shape[2], 1, "dw_in_a",
                             comm=_join(rs_wa_out.swap_chips(), _gather_first(early_parts)))
    rs_wa_out.after_chips(got[0])
    rs_wa_in = _ReduceScatter(g_wa_in, "a_w_in")
    n_tiles = m // _din_tile(m)
    assert n_tiles >= 2
    first_tiles = max(1, (3 * n_tiles) // 8)
    dh0, got2 = _mm_din(dproj_a, [wa_in_g], 1, "dh_a_first", tiles=(0, first_tiles),
                        comm=_join(rs_wa_in.swap_core(), _gather_second(got[1:])))
    rs_wa_in.after_core(got2[0])
    early_all = got2[1:]
    dh0, got = _mm_din(dproj_a, [wa_in_g], 1, "dh_a_rest", comm=rs_wa_in.swap_chips(),
                       tiles=(first_tiles, n_tiles - first_tiles), prev=dh0)
    rs_wa_in.after_chips(got[0])
    dx0, dss0, dgain0 = _prenorm_bwd(dh0, xf, ng0, mod0, dx1, t_seq, "prenorm_bwd_a")
    grad_x = dx0.reshape(nb, t_seq, d)

    dmod = jnp.stack([jnp.concatenate([dss0, dgate0], axis=1), jnp.concatenate([dss1, dgate1], axis=1)])
    dmod_all, dgain_all, dfinal_all, loss_all = _all_gather(
        [dmod.reshape(2, nb, 3 * d), jnp.concatenate([dgain0, dgain1], axis=0), d_final_gain,
         jnp.broadcast_to(loss_part, (1, 128))], "gather_tail")
    dmod_all = dmod_all.transpose(1, 0, 2, 3).reshape(2, NDEV * nb, 3 * d)
    dmod_cols = lax.dynamic_slice_in_dim(dmod_all, me * ncol_ada, ncol_ada, axis=2)
    g_w_ada, g_b_ada = _ada_bwd(c_all, dmod_cols, dmod_all)

    def small2d(k, t):
        return t[k].reshape(early_parts[_EARLY.index(k)].shape) if k in _EARLY else t[k].reshape(-1, d)

    res = {}
    sm = _adamw_small(early_all, *[[small2d(k, t) for k in _EARLY] for t in (w, mo, vo)], "adamw_small_early")
    for k, r in zip(_EARLY, sm):
        res[k] = tuple(z.reshape(w[k].shape) for z in r)
    late = ["norm_gain", "final_gain"]
    sm = _adamw_small([dgain_all, dfinal_all], *[[small2d(k, t) for k in late] for t in (w, mo, vo)],
                      "adamw_small_late", sums=[loss_all])
    for k, r in zip(late, sm):
        res[k] = tuple(z.reshape(w[k].shape) for z in r)
    loss = sm[2][0, 0]
    rb = _adamw([g_b_ada], b_ada, mo["b_ada"], vo["b_ada"], "adamw_b_ada")
    res["b_ada"] = tuple(rb)
    sh = w_ada.shape
    ra = _adamw([g_w_ada.reshape(sh[0] * sh[1], sh[2])], w_ada.reshape(sh[0] * sh[1], sh[2]),
                mo["w_ada"].reshape(sh[0] * sh[1], sh[2]), vo["w_ada"].reshape(sh[0] * sh[1], sh[2]), "adamw_w_ada")
    res["w_ada"] = tuple(z.reshape(sh) for z in ra)

    for k, rs in (("b_w_out", rs_wb_out), ("b_w_in", rs_wb_in), ("a_w_out", rs_wa_out), ("a_w_in", rs_wa_in)):
        res[k] = tuple(z[None] for z in _adamw_blocks(rs.parts, rs.idx, w[k][0], mo[k][0], vo[k][0], "adamw_" + k))

    order = ["norm_gain", "w_ada", "b_ada", "a_w_in", "a_ln_gain", "a_ln_bias", "a_w_s", "a_b_s", "a_w_out",
             "b_w_in", "b_lower_bounds", "b_gn_gain", "b_w_out", "final_gain"]
    return (loss, grad_x, *[res[k][0] for k in order], *[res[k][1] for k in order],
            *[res[k][2] for k in order], *[res[k][3] for k in order])
```

```python
import functools
import math

import jax
import jax.numpy as jnp
from jax import lax
from jax.experimental import pallas as pl
from jax.experimental.pallas import tpu as pltpu

F32 = jnp.float32
BF16 = jnp.bfloat16
MESH = pl.DeviceIdType.MESH
NDEV = 8
EPS = 1e-6
CHUNK = 64
SG_BLOCK = 128
SG_GROUPS = 8
HEAD_DIM = 128
CUM_ROWS = 256
PHASE_HEADS = 8
ADAM_LR, ADAM_B1, ADAM_B2, ADAM_EPS, ADAM_WD, ADAM_STEP = 0.001, 0.9, 0.999, 1e-08, 0.01, 10
VMEM_LIMIT = 56 * 1024 * 1024
ANY = pl.BlockSpec(memory_space=pl.ANY)


class _Hosted:
    def __init__(self, arrays, out_shapes, nsem, start, finish, aliases=None):
        self.arrays, self.out_shapes, self.nsem = list(arrays), list(out_shapes), nsem
        self.start, self.finish = start, finish
        self.aliases = dict(aliases or {})


def _join(*comms):
    arrays, outs, aliases, offs, nsem = [], [], {}, [], 0
    for cm in comms:
        offs.append((len(arrays), len(outs), nsem))
        for i, o in cm.aliases.items():
            aliases[len(arrays) + i] = len(outs) + o
        arrays += cm.arrays
        outs += cm.out_shapes
        nsem += cm.nsem

    def run(which):
        def f(ins, outs_, ss, rs, base):
            for cm, (ia, io, isem) in zip(comms, offs):
                getattr(cm, which)(ins[ia:ia + len(cm.arrays)], outs_[io:io + len(cm.out_shapes)], ss, rs, base + isem)
        return f

    return _Hosted(arrays, outs, nsem, run("start"), run("finish"), aliases)


def _pc(body, *, name, out_shape, grid=None, in_specs=None, out_specs=None, scratch=(), sem=None,
        grid_spec=None, comm=None, aliases=None):
    cp = dict(vmem_limit_bytes=VMEM_LIMIT)
    aliases = dict(aliases or {})
    if comm is None:
        if sem is not None:
            cp["dimension_semantics"] = sem
        kw = {"input_output_aliases": aliases}
        if grid_spec is not None:
            kw["grid_spec"] = grid_spec
        else:
            if grid is not None:
                kw["grid"] = grid
            if in_specs is not None:
                kw["in_specs"] = in_specs
            if out_specs is not None:
                kw["out_specs"] = out_specs
            kw["scratch_shapes"] = list(scratch)
        return pl.pallas_call(functools.partial(body), name=name, out_shape=out_shape,
                              compiler_params=pltpu.CompilerParams(**cp), **kw)

    single = not isinstance(out_shape, (list, tuple))
    outs_list = [out_shape] if single else list(out_shape)
    ospecs = [out_specs] if single else list(out_specs)
    n_in, n_out, n_ci, n_co, n_scr = len(in_specs), len(outs_list), len(comm.arrays), len(comm.out_shapes), len(scratch)
    cp["dimension_semantics"] = ("arbitrary",) * len(grid)

    def hosted(*refs):
        cin, hin = refs[:n_in], refs[n_in:n_in + n_ci]
        cout = refs[n_in + n_ci:n_in + n_ci + n_out]
        hout = refs[n_in + n_ci + n_out:n_in + n_ci + n_out + n_co]
        scr = refs[n_in + n_ci + n_out + n_co:n_in + n_ci + n_out + n_co + n_scr]
        ssem, rsem = refs[-2], refs[-1]
        first = functools.reduce(lambda p, q: p & q, [pl.program_id(a) == 0 for a in range(len(grid))])
        last = functools.reduce(lambda p, q: p & q, [pl.program_id(a) == grid[a] - 1 for a in range(len(grid))])

        @pl.when(first)
        def _():
            comm.start(hin, hout, ssem, rsem, 0)

        body(*cin, *cout, *scr)

        @pl.when(last)
        def _():
            comm.finish(hin, hout, ssem, rsem, 0)

    call = pl.pallas_call(
        hosted, name=name, grid=grid, in_specs=list(in_specs) + [ANY] * n_ci, out_specs=ospecs + [ANY] * n_co,
        out_shape=outs_list + comm.out_shapes,
        scratch_shapes=list(scratch) + [pltpu.SemaphoreType.DMA((comm.nsem,)), pltpu.SemaphoreType.DMA((comm.nsem,))],
        input_output_aliases={**aliases, **{n_in + i: n_out + o for i, o in comm.aliases.items()}},
        compiler_params=pltpu.CompilerParams(**cp))

    def run(*args):
        res = call(*args, *comm.arrays)
        comp = res[:n_out]
        return (comp[0] if single else comp), list(res[n_out:])

    return run


def _tile(n, pref):
    return pref if n % pref == 0 else n


def _sigmoid(x):
    return 1.0 / (1.0 + jnp.exp(-x))


def _gelu(x):
    c = math.sqrt(2.0 / math.pi)
    return 0.5 * x * (1.0 + jnp.tanh(c * (x + 0.044715 * (x * x * x))))


def _gelu_and_grad(x):
    c = math.sqrt(2.0 / math.pi)
    x2 = x * x
    t = jnp.tanh(c * (x + 0.044715 * (x2 * x)))
    half = 0.5 * (1.0 + t)
    return x * half, half + (0.5 * x) * (1.0 - t * t) * (c + (3.0 * 0.044715 * c) * x2)


def _dot(a, b):
    return jnp.dot(a, b, preferred_element_type=F32)


def _dot_nt(a, b):
    return lax.dot_general(a, b, (((1,), (1,)), ((), ())), preferred_element_type=F32)


def _dot_tn(a, b):
    return lax.dot_general(a, b, (((0,), (0,)), ((), ())), preferred_element_type=F32)


def _tri_mask(n, reverse):
    r = lax.broadcasted_iota(jnp.int32, (n, n), 0)
    c = lax.broadcasted_iota(jnp.int32, (n, n), 1)
    same = (r // CHUNK) == (c // CHUNK)
    tri = (c >= r) if reverse else (c <= r)
    return jnp.where(same & tri, 1.0, 0.0).astype(BF16)


def _tri_apply(tri, x):
    hi = x.astype(BF16)
    r1 = x - hi.astype(F32)
    mid = r1.astype(BF16)
    lo = (r1 - mid.astype(F32)).astype(BF16)
    return _dot(tri, hi) + (_dot(tri, mid) + _dot(tri, lo))


def _all_gather(arrs, name):
    n = len(arrs)

    def body(*refs):
        ins, outs = refs[:n], refs[n:2 * n]
        send_sems, recv_sems, local_sems = refs[2 * n:]
        x, y, c = lax.axis_index("x"), lax.axis_index("y"), lax.axis_index("c")
        me, sibling = (x, y, c), (x, y, 1 - c)
        near = (x + c - 2 * x * c, y + (1 - c) - 2 * y * (1 - c))
        far = (x + (1 - c) - 2 * x * (1 - c), y + c - 2 * y * c)
        diag = (1 - x, 1 - y)

        def blk(a, p):
            return outs[a].at[4 * p[0] + 2 * p[1] + p[2]]

        def copy(a, k, block, to, src=None):
            return pltpu.make_async_remote_copy(
                src_ref=blk(a, block) if src is None else src, dst_ref=blk(a, block),
                send_sem=send_sems.at[7 * a + k], recv_sem=recv_sems.at[7 * a + k],
                device_id=to, device_id_type=MESH)

        mine = [pltpu.make_async_copy(ins[a], blk(a, me), local_sems.at[a]) for a in range(n)]
        for m in mine:
            m.start()
        sends = []
        for a in range(n):
            sends += [copy(a, 0, me, sibling, src=ins[a]), copy(a, 1, me, (*near, c), src=ins[a]),
                      copy(a, 2, me, (*far, c), src=ins[a])]
        for cp in sends:
            cp.start()
        for a in range(n):
            copy(a, 1, (*near, c), me).wait_recv()
            sends.append(copy(a, 3, (*near, c), (*far, c)))
            sends[-1].start()
        for a in range(n):
            sends.append(copy(a, 4, (*near, c), sibling))
            sends[-1].start()
            copy(a, 2, (*far, c), me).wait_recv()
            sends.append(copy(a, 5, (*far, c), sibling))
            sends[-1].start()
        for a in range(n):
            copy(a, 3, (*diag, c), me).wait_recv()
            sends.append(copy(a, 6, (*diag, c), sibling))
            sends[-1].start()
        for a in range(n):
            copy(a, 0, sibling, me).wait_recv()
            copy(a, 4, (*far, 1 - c), me).wait_recv()
            copy(a, 5, (*near, 1 - c), me).wait_recv()
            copy(a, 6, (*diag, 1 - c), me).wait_recv()
        for cp in sends:
            cp.wait_send()
        for m in mine:
            m.wait()

    out_shape = [jax.ShapeDtypeStruct((NDEV,) + a.shape, a.dtype) for a in arrs]
    return _pc(body, name=name, out_shape=out_shape, in_specs=[ANY] * n, out_specs=[ANY] * n,
               scratch=[pltpu.SemaphoreType.DMA((7 * n,)), pltpu.SemaphoreType.DMA((7 * n,)),
                        pltpu.SemaphoreType.DMA((n,))])(*arrs)


def _gather_first(arrs):
    n = len(arrs)

    def parts(ins, outs, ss, rs, base):
        x, y, c = lax.axis_index("x"), lax.axis_index("y"), lax.axis_index("c")
        me, sibling = (x, y, c), (x, y, 1 - c)
        chips = [(1 - x, y), (x, 1 - y), (1 - x, 1 - y)]

        def blk(a, p):
            return outs[a].at[4 * p[0] + 2 * p[1] + p[2]]

        def copy(a, k, block, to):
            return pltpu.make_async_remote_copy(
                src_ref=ins[a], dst_ref=blk(a, block), send_sem=ss.at[base + 4 * a + k],
                recv_sem=rs.at[base + 4 * a + k], device_id=to, device_id_type=MESH)

        local = [pltpu.make_async_copy(ins[a], blk(a, me), ss.at[base + 4 * n + a]) for a in range(n)]
        sends, recvs = [], []
        for a in range(n):
            sends.append(copy(a, 0, me, sibling))
            recvs.append(copy(a, 0, sibling, me))
            for j, chip in enumerate(chips):
                sends.append(copy(a, 1 + j, me, (*chip, c)))
                recvs.append(copy(a, 1 + j, (*chip, c), me))
        return local, sends, recvs

    def start(ins, outs, ss, rs, base):
        local, sends, _ = parts(ins, outs, ss, rs, base)
        for cp in local + sends:
            cp.start()

    def finish(ins, outs, ss, rs, base):
        local, sends, recvs = parts(ins, outs, ss, rs, base)
        for cp in recvs:
            cp.wait_recv()
        for cp in sends:
            cp.wait_send()
        for cp in local:
            cp.wait()

    return _Hosted(arrs, [jax.ShapeDtypeStruct((NDEV,) + a.shape, a.dtype) for a in arrs], 5 * n, start, finish)


def _gather_second(bufs):
    n = len(bufs)

    def parts(ins, outs, ss, rs, base):
        x, y, c = lax.axis_index("x"), lax.axis_index("y"), lax.axis_index("c")
        sibling = (x, y, 1 - c)
        chips = [(1 - x, y), (x, 1 - y), (1 - x, 1 - y)]
        sends, recvs = [], []
        for a in range(n):
            for j, chip in enumerate(chips):
                mine = 4 * chip[0] + 2 * chip[1] + c
                theirs = 4 * chip[0] + 2 * chip[1] + (1 - c)
                sends.append(pltpu.make_async_remote_copy(
                    src_ref=ins[a].at[mine], dst_ref=outs[a].at[mine], send_sem=ss.at[base + 3 * a + j],
                    recv_sem=rs.at[base + 3 * a + j], device_id=sibling, device_id_type=MESH))
                recvs.append(pltpu.make_async_remote_copy(
                    src_ref=ins[a].at[theirs], dst_ref=outs[a].at[theirs], send_sem=ss.at[base + 3 * a + j],
                    recv_sem=rs.at[base + 3 * a + j], device_id=sibling, device_id_type=MESH))
        return sends, recvs

    def start(ins, outs, ss, rs, base):
        for cp in parts(ins, outs, ss, rs, base)[0]:
            cp.start()

    def finish(ins, outs, ss, rs, base):
        sends, recvs = parts(ins, outs, ss, rs, base)
        for cp in recvs:
            cp.wait_recv()
        for cp in sends:
            cp.wait_send()

    return _Hosted(bufs, [jax.ShapeDtypeStruct(b.shape, b.dtype) for b in bufs], 3 * n, start, finish,
                   aliases={a: a for a in range(n)})


def _swap(src, nblk, ids_fn, partner_fn):
    def copies(ins, outs, ss, rs, base):
        x, y, c = lax.axis_index("x"), lax.axis_index("y"), lax.axis_index("c")
        ids = ids_fn(x, y, c)
        partner = partner_fn(x, y, c)
        return [pltpu.make_async_remote_copy(
            src_ref=ins[0].at[ids[k]], dst_ref=outs[0].at[k], send_sem=ss.at[base + k], recv_sem=rs.at[base + k],
            device_id=partner, device_id_type=MESH) for k in range(nblk)]

    def start(ins, outs, ss, rs, base):
        for cp in copies(ins, outs, ss, rs, base):
            cp.start()

    def finish(ins, outs, ss, rs, base):
        for cp in copies(ins, outs, ss, rs, base):
            cp.wait()

    return _Hosted([src], [jax.ShapeDtypeStruct((nblk,) + src.shape[1:], src.dtype)], nblk, start, finish)


def _swap_chips(send):
    def copies(ins, outs, ss, rs, base):
        x, y, c = lax.axis_index("x"), lax.axis_index("y"), lax.axis_index("c")
        chips = [(1 - x, y), (x, 1 - y), (1 - x, 1 - y)]
        return [pltpu.make_async_remote_copy(
            src_ref=ins[0].at[j], dst_ref=outs[0].at[j], send_sem=ss.at[base + j], recv_sem=rs.at[base + j],
            device_id=(*chip, c), device_id_type=MESH) for j, chip in enumerate(chips)]

    def start(ins, outs, ss, rs, base):
        for cp in copies(ins, outs, ss, rs, base):
            cp.start()

    def finish(ins, outs, ss, rs, base):
        for cp in copies(ins, outs, ss, rs, base):
            cp.wait()

    return _Hosted([send], [jax.ShapeDtypeStruct(send.shape, send.dtype)], 3, start, finish)


def _add_send(a, b, idx, ns, name):
    _, r, c = a.shape
    tr = _tile(r, 256)

    def body(idx_ref, a_ref, b_ref, send_ref):
        send_ref[...] = (a_ref[...] + b_ref[...]).astype(BF16)

    def sel(off):
        return pl.BlockSpec((None, tr, c), lambda k, i, s: (s[off + k], i, 0))

    gs = pltpu.PrefetchScalarGridSpec(num_scalar_prefetch=1, grid=(ns, r // tr), in_specs=[sel(0), sel(ns)],
                                      out_specs=pl.BlockSpec((None, tr, c), lambda k, i, s: (k, i, 0)))
    return _pc(body, name=name, grid_spec=gs, sem=("arbitrary", "arbitrary"),
               out_shape=jax.ShapeDtypeStruct((ns, r, c), BF16))(idx, a, b)


class _ReduceScatter:
    def __init__(self, g, tag):
        self.g, self.tag = g, tag

    def swap_core(self):
        return _swap(self.g, 4, lambda x, y, c: [1 - c, 3 - c, 5 - c, 7 - c], lambda x, y, c: (x, y, 1 - c))

    def after_core(self, recv):
        x, y, c = lax.axis_index("x"), lax.axis_index("y"), lax.axis_index("c")
        chips = [(1 - x, y), (x, 1 - y), (1 - x, 1 - y)]
        idx = jnp.stack([4 * p + 2 * q + c for p, q in chips] + [2 * p + q for p, q in chips]).astype(jnp.int32)
        self.send = _add_send(self.g, recv, idx, 3, "rs_add_" + self.tag)
        self.recv_core = recv
        zero = jnp.zeros((), jnp.int32)
        self.idx = jnp.stack([4 * x + 2 * y + c, 2 * x + y, zero, zero + 1, zero + 2]).astype(jnp.int32)

    def swap_chips(self):
        return _swap_chips(self.send)

    def after_chips(self, recv):
        self.parts = [self.g, self.recv_core, recv, recv, recv]


def _ada_fwd(c_all, w_ada, b_cols, b_lb):
    nl, d, ncol = w_ada.shape
    nseq = c_all.shape[0]
    di = b_lb.shape[1]

    def body(c_ref, w_ref, b_ref, lb_ref, mod_ref, lbj_ref):
        cv = c_ref[...]
        cact = (cv * _sigmoid(cv)).astype(BF16)
        for l in range(nl):
            mod_ref[l] = _dot(cact, w_ref[l].astype(BF16)) + b_ref[l]
        b0, b1 = lb_ref[0:1, :], lb_ref[1:2, :]
        mx = jnp.maximum(b0, b1)
        e0, e1 = jnp.exp(b0 - mx), jnp.exp(b1 - mx)
        s = e0 + e1
        p0, p1 = e0 / s, e1 / s
        lbj_ref[0:1, :] = (p0 + p1) - p0
        lbj_ref[1:2, :] = p0 * p1

    return _pc(body, name="ada_fwd",
               out_shape=[jax.ShapeDtypeStruct((nl, nseq, ncol), F32), jax.ShapeDtypeStruct((2, di), F32)]
               )(c_all, w_ada, b_cols, b_lb)


def _ada_bwd(c_all, dmod_cols, dmod_full):
    nl, nseq, ncol = dmod_cols.shape
    d = c_all.shape[1]
    d3 = dmod_full.shape[2]

    def body(c_ref, dc_ref, df_ref, gw_ref, gb_ref):
        cv = c_ref[...]
        cact = (cv * _sigmoid(cv)).astype(BF16)
        for l in range(nl):
            gw_ref[l] = _dot_tn(cact, dc_ref[l].astype(BF16))
            gb_ref[l:l + 1, :] = jnp.sum(df_ref[l], axis=0, keepdims=True)

    return _pc(body, name="ada_bwd",
               out_shape=[jax.ShapeDtypeStruct((nl, d, ncol), F32), jax.ShapeDtypeStruct((nl, d3), F32)]
               )(c_all, dmod_cols, dmod_full)


def _prenorm(x, gain, mod, t_seq, name):
    m, d = x.shape
    tm = _tile(t_seq, 1024)
    per = t_seq // tm

    def body(x_ref, g_ref, mod_ref, h_ref, ht_ref):
        xv = x_ref[...]
        rstd = lax.rsqrt(jnp.mean(xv * xv, axis=-1, keepdims=True) + EPS)
        r = xv * rstd * g_ref[...]
        h = r * (1.0 + mod_ref[0, 1:2, :]) + mod_ref[0, 0:1, :]
        h_ref[...] = h.astype(BF16)
        ht_ref[...] = h.T.astype(BF16)

    return _pc(body, name=name, out_shape=[jax.ShapeDtypeStruct((m, d), BF16), jax.ShapeDtypeStruct((d, m), BF16)],
               grid=(m // tm,),
               in_specs=[pl.BlockSpec((tm, d), lambda i: (i, 0)), pl.BlockSpec((1, d), lambda i: (0, 0)),
                         pl.BlockSpec((1, 3, d), lambda i: (i // per, 0, 0))],
               out_specs=[pl.BlockSpec((tm, d), lambda i: (i, 0)), pl.BlockSpec((d, tm), lambda i: (0, i))],
               sem=("parallel",))(x, gain, mod)


def _prenorm_bwd(dh, x, gain, mod, dxn, t_seq, name):
    m, d = x.shape
    nb = m // t_seq
    tm = _tile(t_seq, 1024)
    per = t_seq // tm

    def body(dh_ref, x_ref, g_ref, mod_ref, dxn_ref, dx_ref, dss_ref, dg_ref):
        i = pl.program_id(0)
        xv, dhv, g = x_ref[...], dh_ref[...], g_ref[...]
        rstd = lax.rsqrt(jnp.mean(xv * xv, axis=-1, keepdims=True) + EPS)
        xhat = xv * rstd
        dr = dhv * (1.0 + mod_ref[0, 1:2, :])
        dxhat = dr * g
        dx_ref[...] = dxn_ref[...] + rstd * (dxhat - xhat * jnp.mean(dxhat * xhat, axis=-1, keepdims=True))

        @pl.when(i % per == 0)
        def _():
            dss_ref[...] = jnp.zeros_like(dss_ref)

        @pl.when(i == 0)
        def _():
            dg_ref[...] = jnp.zeros_like(dg_ref)

        dss_ref[0, 0:1, :] += jnp.sum(dhv, axis=0, keepdims=True)
        dss_ref[0, 1:2, :] += jnp.sum(dhv * (xhat * g), axis=0, keepdims=True)
        dg_ref[...] += jnp.sum(dr * xhat, axis=0, keepdims=True)

    row = pl.BlockSpec((tm, d), lambda i: (i, 0))
    return _pc(body, name=name,
               out_shape=[jax.ShapeDtypeStruct((m, d), F32), jax.ShapeDtypeStruct((nb, 2, d), F32),
                          jax.ShapeDtypeStruct((1, d), F32)],
               grid=(m // tm,),
               in_specs=[row, row, pl.BlockSpec((1, d), lambda i: (0, 0)),
                         pl.BlockSpec((1, 3, d), lambda i: (i // per, 0, 0)), row],
               out_specs=[row, pl.BlockSpec((1, 2, d), lambda i: (i // per, 0, 0)),
                          pl.BlockSpec((1, d), lambda i: (0, 0))],
               sem=("arbitrary",))(dh, x, gain, mod, dxn)


def _mm_in(h, ws, sections, name, comm=None):
    m, k = h.shape
    nw = len(ws)
    widths = [w.shape[2] for w in ws]
    offs = [sum(widths[:a]) for a in range(nw)]
    nc = sum(widths)
    per = NDEV // sections if sections > 1 else NDEV
    tm = _din_tile(m)
    assert per % 2 == 0

    def body(*refs):
        hv = refs[0][...]
        o_ref = refs[1 + nw]
        for b in range(2):
            for a in range(nw):
                lo = b * nc + offs[a]
                o_ref[:, lo:lo + widths[a]] = _dot(hv, refs[1 + a][b])

    w_specs = [pl.BlockSpec((2, k, wd), lambda j, i: (j, 0, 0)) for wd in widths]
    if sections > 1:
        out_shape = jax.ShapeDtypeStruct((sections, m, per * nc), F32)
        out_spec = pl.BlockSpec((None, tm, 2 * nc), lambda j, i: ((2 * j) // per, i, ((2 * j) % per) // 2))
    else:
        out_shape = jax.ShapeDtypeStruct((m, NDEV * nc), F32)
        out_spec = pl.BlockSpec((tm, 2 * nc), lambda j, i: (i, j))
    return _pc(body, name=name, out_shape=out_shape, grid=(NDEV // 2, m // tm),
               in_specs=[pl.BlockSpec((tm, k), lambda j, i: (i, 0))] + w_specs,
               out_specs=out_spec, sem=("parallel", "parallel"), comm=comm)(h, *ws)


def _din_tile(m):
    return 1024 if m % 1024 == 0 and m >= 2048 else _tile(m, 512)


def _mm_din(dproj, ws, sections, name, comm=None, tiles=None, prev=None):
    nw, k = len(ws), ws[0].shape[1]
    widths = [w.shape[2] for w in ws]
    offs = [sum(widths[:a]) for a in range(nw)]
    nc = sum(widths)
    m = dproj.shape[-2]
    tm = _din_tile(m)
    t0, nt = tiles if tiles is not None else (0, m // tm)
    per = NDEV // sections if sections > 1 else NDEV
    assert per % 2 == 0

    def body(*refs):
        d_ref, o_ref = refs[0], refs[-1]
        j = pl.program_id(1)
        acc = None
        for b in range(2):
            for a in range(nw):
                lo = b * nc + offs[a]
                term = _dot_nt(d_ref[:, lo:lo + widths[a]], refs[1 + a][b])
                acc = term if acc is None else acc + term

        @pl.when(j == 0)
        def _():
            o_ref[...] = acc

        @pl.when(j > 0)
        def _():
            o_ref[...] += acc

    if sections > 1:
        dspec = pl.BlockSpec((None, tm, 2 * nc), lambda i, j: ((2 * j) // per, i + t0, ((2 * j) % per) // 2))
    else:
        dspec = pl.BlockSpec((tm, 2 * nc), lambda i, j: (i + t0, j))
    in_specs = [dspec] + [pl.BlockSpec((2, k, wd), lambda i, j: (j, 0, 0)) for wd in widths]
    args = [dproj, *ws]
    if prev is not None:
        in_specs.append(ANY)
        args.append(prev)
    return _pc(body, name=name, out_shape=jax.ShapeDtypeStruct((m, k), F32), grid=(nt, NDEV // 2), in_specs=in_specs,
               out_specs=pl.BlockSpec((tm, k), lambda i, j: (i + t0, 0)), sem=("parallel", "arbitrary"),
               comm=comm, aliases={1 + nw: 0} if prev is not None else None)(*args)


def _mm_dw_in(ht, dproj, nc, sections, name, comm=None):
    k, m = ht.shape
    per = NDEV // sections if sections > 1 else NDEV

    def body(h_ref, d_ref, o_ref):
        o_ref[...] = _dot(h_ref[...], d_ref[...])

    if sections > 1:
        dspec = pl.BlockSpec((None, m, nc), lambda j: (j // per, 0, j % per))
    else:
        dspec = pl.BlockSpec((m, nc), lambda j: (0, j))
    return _pc(body, name=name, out_shape=jax.ShapeDtypeStruct((NDEV, k, nc), F32), grid=(NDEV,),
               in_specs=[pl.BlockSpec((k, m), lambda j: (0, 0)), dspec],
               out_specs=pl.BlockSpec((None, k, nc), lambda j: (j, 0, 0)),
               sem=("parallel",), comm=comm)(ht, dproj)


def _out_proj(ybr, w_out, x, mod, t_seq, name, comm=None):
    m, di = ybr.shape
    d = w_out.shape[1]
    tm = _tile(t_seq, 1024)
    per = t_seq // tm

    def body(y_ref, w_ref, x_ref, mod_ref, yo_ref, xn_ref):
        yo = _dot(y_ref[...], w_ref[...])
        yo_ref[...] = yo
        xn_ref[...] = x_ref[...] + mod_ref[0, 2:3, :] * yo

    row = pl.BlockSpec((tm, d), lambda i: (i, 0))
    return _pc(body, name=name,
               out_shape=[jax.ShapeDtypeStruct((m, d), F32), jax.ShapeDtypeStruct((m, d), F32)],
               grid=(m // tm,),
               in_specs=[pl.BlockSpec((tm, di), lambda i: (i, 0)), pl.BlockSpec((di, d), lambda i: (0, 0)), row,
                         pl.BlockSpec((1, 3, d), lambda i: (i // per, 0, 0))],
               out_specs=[row, row], sem=("parallel",), comm=comm)(ybr, w_out, x, mod)


def _out_proj_loss(ybr, w_out, x, mod, gain, target, t_seq):
    m, di = ybr.shape
    d = w_out.shape[1]
    tm = _tile(t_seq, 512)
    per = t_seq // tm

    def body(y_ref, w_ref, x_ref, mod_ref, g_ref, t_ref, yo_ref, dx_ref, loss_ref, dg_ref):
        i = pl.program_id(0)
        yo = _dot(y_ref[...], w_ref[...])
        yo_ref[...] = yo
        xv = x_ref[...] + mod_ref[0, 2:3, :] * yo
        g = g_ref[...]
        rstd = lax.rsqrt(jnp.mean(xv * xv, axis=-1, keepdims=True) + EPS)
        xhat = xv * rstd
        err = xhat * g - t_ref[...]
        dy = err * (1.0 / d)
        dxhat = dy * g
        dx_ref[...] = rstd * (dxhat - xhat * jnp.mean(dxhat * xhat, axis=-1, keepdims=True))

        @pl.when(i == 0)
        def _():
            loss_ref[...] = jnp.zeros_like(loss_ref)
            dg_ref[...] = jnp.zeros_like(dg_ref)

        loss_ref[...] += 0.5 * jnp.sum(jnp.mean(err * err, axis=-1, keepdims=True), axis=0, keepdims=True)
        dg_ref[...] += jnp.sum(dy * xhat, axis=0, keepdims=True)

    row = pl.BlockSpec((tm, d), lambda i: (i, 0))
    vec = pl.BlockSpec((1, d), lambda i: (0, 0))
    return _pc(body, name="out_proj_loss",
               out_shape=[jax.ShapeDtypeStruct((m, d), F32), jax.ShapeDtypeStruct((m, d), F32),
                          jax.ShapeDtypeStruct((1, 1), F32), jax.ShapeDtypeStruct((1, d), F32)],
               grid=(m // tm,),
               in_specs=[pl.BlockSpec((tm, di), lambda i: (i, 0)), pl.BlockSpec((di, d), lambda i: (0, 0)), row,
                         pl.BlockSpec((1, 3, d), lambda i: (i // per, 0, 0)), vec, row],
               out_specs=[row, row, pl.BlockSpec((1, 1), lambda i: (0, 0)), vec],
               sem=("arbitrary",))(ybr, w_out, x, mod, gain, target)


def _gate_dybr(dxn, yout, mod, w_out, t_seq, name):
    m, d = dxn.shape
    di = w_out.shape[0]
    nb = m // t_seq
    tm = _tile(t_seq, 1024)
    per = t_seq // tm

    def body(dxn_ref, yo_ref, mod_ref, w_ref, dy_ref, dgate_ref, o_ref):
        i = pl.program_id(0)
        dv = dxn_ref[...]
        dy = (mod_ref[0, 2:3, :] * dv).astype(BF16)
        dy_ref[...] = dy
        o_ref[...] = _dot_nt(dy, w_ref[...])

        @pl.when(i % per == 0)
        def _():
            dgate_ref[...] = jnp.zeros_like(dgate_ref)

        dgate_ref[0] += jnp.sum(dv * yo_ref[...], axis=0, keepdims=True)

    row = pl.BlockSpec((tm, d), lambda i: (i, 0))
    return _pc(body, name=name,
               out_shape=[jax.ShapeDtypeStruct((m, d), BF16), jax.ShapeDtypeStruct((nb, 1, d), F32),
                          jax.ShapeDtypeStruct((m, di), F32)],
               grid=(m // tm,),
               in_specs=[row, row, pl.BlockSpec((1, 3, d), lambda i: (i // per, 0, 0)),
                         pl.BlockSpec((di, d), lambda i: (0, 0))],
               out_specs=[row, pl.BlockSpec((1, 1, d), lambda i: (i // per, 0, 0)),
                          pl.BlockSpec((tm, di), lambda i: (i, 0))],
               sem=("arbitrary",))(dxn, yout, mod, w_out)


def _mm_dw_out(ybr, dy, name, comm=None):
    m, di = ybr.shape
    d = dy.shape[1]
    tn = _tile(di, 1024)

    def body(y_ref, dy_ref, o_ref):
        o_ref[...] = _dot_tn(y_ref[...], dy_ref[...])

    return _pc(body, name=name, out_shape=jax.ShapeDtypeStruct((di, d), F32), grid=(di // tn,),
               in_specs=[pl.BlockSpec((m, tn), lambda n: (0, n)), pl.BlockSpec((m, d), lambda n: (0, 0))],
               out_specs=pl.BlockSpec((tn, d), lambda n: (n, 0)), sem=("parallel",), comm=comm)(ybr, dy)


def _sgu_mask():
    t = lax.broadcasted_iota(jnp.int32, (SG_BLOCK, SG_BLOCK), 0)
    s = lax.broadcasted_iota(jnp.int32, (SG_BLOCK, SG_BLOCK), 1)
    return (s // CHUNK) <= (t // CHUNK)


def _a_mid_fwd(proj, ln_g, ln_b, w_s, bs_t, t_seq, comm=None):
    m, n3 = proj.shape
    di = n3 // 3
    gd = di // SG_GROUPS
    r = _tile(t_seq, 256)
    nblk = r // SG_BLOCK

    def body(p_ref, lg_ref, lb_ref, ws_ref, bs_ref, ybr_ref, s_scr):
        v = _gelu(p_ref[:, di:2 * di])
        mu = jnp.mean(v, axis=-1, keepdims=True)
        vc = v - mu
        rstd = lax.rsqrt(jnp.mean(vc * vc, axis=-1, keepdims=True) + EPS)
        vb = (vc * rstd * lg_ref[...] + lb_ref[...]).astype(BF16)
        mask = _sgu_mask()
        for gi in range(SG_GROUPS):
            ws = jnp.where(mask, ws_ref[gi], 0.0).astype(BF16)
            bcol = bs_ref[:, gi:gi + 1]
            for b in range(nblk):
                rows = slice(b * SG_BLOCK, (b + 1) * SG_BLOCK)
                cols = slice(gi * gd, (gi + 1) * gd)
                s_scr[rows, cols] = _dot(ws, vb[rows, cols]) + bcol
        gg = p_ref[:, 2 * di:]
        ybr_ref[...] = (_gelu(p_ref[:, :di]) * s_scr[...] * (gg * _sigmoid(gg))).astype(BF16)

    vec = pl.BlockSpec((1, di), lambda i: (0, 0))
    return _pc(body, name="a_mid_fwd", out_shape=jax.ShapeDtypeStruct((m, di), BF16), grid=(m // r,),
               in_specs=[pl.BlockSpec((r, n3), lambda i: (i, 0)), vec, vec,
                         pl.BlockSpec((SG_GROUPS, SG_BLOCK, SG_BLOCK), lambda i: (0, 0, 0)),
                         pl.BlockSpec((SG_BLOCK, 128), lambda i: (0, 0))],
               out_specs=pl.BlockSpec((r, di), lambda i: (i, 0)),
               scratch=[pltpu.VMEM((r, di), F32)], sem=("parallel",), comm=comm)(proj, ln_g, ln_b, w_s, bs_t)


def _a_mid_bwd(proj, dybr, ln_g, ln_b, w_s, bs_t, t_seq, comm=None):
    m, n3 = proj.shape
    di = n3 // 3
    gd = di // SG_GROUPS
    r = _tile(t_seq, 256)
    nblk = r // SG_BLOCK

    def body(p_ref, dy_ref, lg_ref, lb_ref, ws_ref, bs_ref,
             dp_ref, dlg_ref, dlb_ref, dws_ref, dbs_ref, s_scr, dvl_scr):
        i = pl.program_id(0)

        @pl.when(i == 0)
        def _():
            dlg_ref[...] = jnp.zeros_like(dlg_ref)
            dlb_ref[...] = jnp.zeros_like(dlb_ref)
            dws_ref[...] = jnp.zeros_like(dws_ref)
            dbs_ref[...] = jnp.zeros_like(dbs_ref)

        v, dgelu_v = _gelu_and_grad(p_ref[:, di:2 * di])
        mu = jnp.mean(v, axis=-1, keepdims=True)
        vc = v - mu
        rstd = lax.rsqrt(jnp.mean(vc * vc, axis=-1, keepdims=True) + EPS)
        vhat = vc * rstd
        lg = lg_ref[...]
        vb = (vhat * lg + lb_ref[...]).astype(BF16)
        u, dgelu_u = _gelu_and_grad(p_ref[:, :di])
        gg = p_ref[:, 2 * di:]
        sg = _sigmoid(gg)
        dyv = dy_ref[...]
        dus = dyv * (gg * sg)
        dsb = (dus * u).astype(BF16)
        ds32 = dus * u
        mask = _sgu_mask()
        lane = lax.broadcasted_iota(jnp.int32, (SG_BLOCK, 128), 1)
        dbs_acc = jnp.zeros((SG_BLOCK, 128), F32)
        for gi in range(SG_GROUPS):
            ws = jnp.where(mask, ws_ref[gi], 0.0).astype(BF16)
            bcol = bs_ref[:, gi:gi + 1]
            cols = slice(gi * gd, (gi + 1) * gd)
            dws_acc = jnp.zeros((SG_BLOCK, SG_BLOCK), F32)
            dbs_col = jnp.zeros((SG_BLOCK, 1), F32)
            for b in range(nblk):
                rows = slice(b * SG_BLOCK, (b + 1) * SG_BLOCK)
                s_scr[rows, cols] = _dot(ws, vb[rows, cols]) + bcol
                dvl_scr[rows, cols] = _dot_tn(ws, dsb[rows, cols])
                dws_acc += _dot_nt(dsb[rows, cols], vb[rows, cols])
                dbs_col += jnp.sum(ds32[rows, cols], axis=-1, keepdims=True)
            dws_ref[gi] += jnp.where(mask, dws_acc, 0.0)
            dbs_acc += jnp.where(lane == gi, dbs_col, 0.0)
        dbs_ref[...] += dbs_acc
        s = s_scr[...]
        dp_ref[:, :di] = (dus * s * dgelu_u).astype(BF16)
        dp_ref[:, 2 * di:] = (dyv * u * s * (sg * (1.0 + gg * (1.0 - sg)))).astype(BF16)
        dvl = dvl_scr[...]
        dlg_ref[...] += jnp.sum(dvl * vhat, axis=0, keepdims=True)
        dlb_ref[...] += jnp.sum(dvl, axis=0, keepdims=True)
        dvh = dvl * lg
        dv = rstd * (dvh - jnp.mean(dvh, axis=-1, keepdims=True)
                     - vhat * jnp.mean(dvh * vhat, axis=-1, keepdims=True))
        dp_ref[:, di:2 * di] = (dv * dgelu_v).astype(BF16)

    vec = pl.BlockSpec((1, di), lambda i: (0, 0))
    wsb = pl.BlockSpec((SG_GROUPS, SG_BLOCK, SG_BLOCK), lambda i: (0, 0, 0))
    bsb = pl.BlockSpec((SG_BLOCK, 128), lambda i: (0, 0))
    return _pc(body, name="a_mid_bwd",
               out_shape=[jax.ShapeDtypeStruct((m, n3), BF16), jax.ShapeDtypeStruct((1, di), F32),
                          jax.ShapeDtypeStruct((1, di), F32),
                          jax.ShapeDtypeStruct((SG_GROUPS, SG_BLOCK, SG_BLOCK), F32),
                          jax.ShapeDtypeStruct((SG_BLOCK, 128), F32)],
               grid=(m // r,),
               in_specs=[pl.BlockSpec((r, n3), lambda i: (i, 0)), pl.BlockSpec((r, di), lambda i: (i, 0)),
                         vec, vec, wsb, bsb],
               out_specs=[pl.BlockSpec((r, n3), lambda i: (i, 0)), vec, vec, wsb, bsb],
               scratch=[pltpu.VMEM((r, di), F32), pltpu.VMEM((r, di), F32)],
               sem=("arbitrary",), comm=comm)(proj, dybr, ln_g, ln_b, w_s, bs_t)


def _chunk_rows(n):
    if isinstance(n, int):
        return pl.ds(n * CHUNK, CHUNK)
    return pl.ds(pl.multiple_of(n * CHUNK, CHUNK), CHUNK)


def _hgrn_dims(t_seq, di):
    tr = _tile(t_seq, 128)
    hc = _tile(di, 2048)
    return tr, hc, hc // HEAD_DIM


def _hgrn_gates(f_ref, lb, a_scr, k_scr, tr):
    sig = _sigmoid(f_ref[...])
    fg = lb + (1.0 - lb) * sig
    k_scr[...] = 1.0 - fg
    logf = jnp.log(fg)
    g = min(CUM_ROWS, tr)
    tri = _tri_mask(g, reverse=False)
    for rg in range(tr // g):
        a_scr[rg * g:(rg + 1) * g, :] = _tri_apply(tri, logf[rg * g:(rg + 1) * g, :])
    return sig, fg


def _hgrn_fwd(proj, lbj, gn, nb, t_seq):
    _, m, di = proj.shape
    tr, hc, hpg = _hgrn_dims(t_seq, di)
    nt, nhg, ncl = t_seq // tr, di // hc, tr // CHUNK
    nheads = di // HEAD_DIM

    def body(p_ref, lb_ref, gn_ref, o_ref, ybr_ref, st_ref, st_scr, a_scr, k_scr):
        q_ref, f_ref, i_ref, g_ref = (p_ref.at[s] for s in range(4))
        t = pl.program_id(2)

        @pl.when(t == 0)
        def _():
            st_scr[...] = jnp.zeros_like(st_scr)

        _hgrn_gates(f_ref, lb_ref[0:1, :], a_scr, k_scr, tr)
        gnv = gn_ref[...]
        rr = lax.broadcasted_iota(jnp.int32, (CHUNK, CHUNK), 0)
        cc = lax.broadcasted_iota(jnp.int32, (CHUNK, CHUNK), 1)
        causal = cc <= rr

        def chunk(n, carry):
            rows = _chunk_rows(n)
            lanes = [slice(hd * HEAD_DIM, (hd + 1) * HEAD_DIM) for hd in range(hpg)]
            hs = []
            for hd, ls in enumerate(lanes):
                h = {}
                ah, kh = a_scr[rows, ls], k_scr[rows, ls]
                qp = q_ref[rows, ls]
                qh = qp * _sigmoid(qp)
                h["vb"] = i_ref[rows, ls].astype(BF16)
                aref, alast = ah[CHUNK // 2 - 1:CHUNK // 2, :], ah[CHUNK - 1:CHUNK, :]
                h["q_in"] = (qh * jnp.exp(ah - aref)).astype(BF16)
                h["k_in"] = (kh * jnp.exp(aref - ah)).astype(BF16)
                h["q_out"] = (qh * jnp.exp(ah)).astype(BF16)
                h["k_out"] = (kh * jnp.exp(alast - ah)).astype(BF16)
                h["dec"] = jnp.exp(alast)
                st = st_scr[hd]
                st_ref[n, hd] = st
                h["st"] = st
                hs.append(h)
            for h in hs:
                h["scores"] = _dot_nt(h["q_in"], h["k_in"])
                h["o_inter"] = _dot_nt(h["q_out"], h["st"].astype(BF16))
                h["st_mm"] = _dot_tn(h["vb"], h["k_out"])
            for h in hs:
                h["o"] = _dot(jnp.where(causal, h["scores"], 0.0).astype(BF16), h["vb"]) + h["o_inter"]
            for hd, (h, ls) in enumerate(zip(hs, lanes)):
                st_scr[hd] = h["st"] * h["dec"] + h["st_mm"]
                o = h["o"]
                o_ref[rows, ls] = o
                rstd = lax.rsqrt(jnp.mean(o * o, axis=-1, keepdims=True) + EPS)
                gg = g_ref[rows, ls]
                ybr_ref[rows, ls] = ((o * rstd * gnv) * (gg * _sigmoid(gg))).astype(BF16)
            return carry

        lax.fori_loop(0, ncl, chunk, 0)

    blk = pl.BlockSpec((tr, hc), lambda hg, b, t: (b * nt + t, hg))
    return _pc(body, name="hgrn_fwd",
               out_shape=[jax.ShapeDtypeStruct((m, di), F32), jax.ShapeDtypeStruct((m, di), BF16),
                          jax.ShapeDtypeStruct((m // CHUNK, nheads, HEAD_DIM, HEAD_DIM), F32)],
               grid=(nhg, nb, nt),
               in_specs=[pl.BlockSpec((4, tr, hc), lambda hg, b, t: (0, b * nt + t, hg)),
                         pl.BlockSpec((2, hc), lambda hg, b, t: (0, hg)),
                         pl.BlockSpec((1, HEAD_DIM), lambda hg, b, t: (0, 0))],
               out_specs=[blk, blk, pl.BlockSpec((ncl, hpg, HEAD_DIM, HEAD_DIM),
                                                 lambda hg, b, t: (b * nt + t, hg, 0, 0))],
               scratch=[pltpu.VMEM((hpg, HEAD_DIM, HEAD_DIM), F32), pltpu.VMEM((tr, hc), F32),
                        pltpu.VMEM((tr, hc), F32)],
               sem=("parallel", "arbitrary", "arbitrary"))(proj, lbj, gn)


def _hgrn_bwd(proj, o_all, dybr, states, lbj, gn, nb, t_seq, comm=None):
    _, m, di = proj.shape
    tr, hc, hpg = _hgrn_dims(t_seq, di)
    nt, nhg, ncl = t_seq // tr, di // hc, tr // CHUNK

    def body(p_ref, o_ref, dy_ref, st_ref, lb_ref, gn_ref,
             dp_ref, dlb_ref, dgn_ref, dst_scr, a_scr, k_scr, da_scr, dk_scr):
        q_ref, f_ref, i_ref, g_ref = (p_ref.at[s] for s in range(4))
        hg, b, t = pl.program_id(0), pl.program_id(1), pl.program_id(2)

        @pl.when(t == 0)
        def _():
            dst_scr[...] = jnp.zeros_like(dst_scr)

        @pl.when((b == 0) & (t == 0))
        def _():
            dlb_ref[...] = jnp.zeros_like(dlb_ref)

        @pl.when((hg == 0) & (b == 0) & (t == 0))
        def _():
            dgn_ref[...] = jnp.zeros_like(dgn_ref)

        lb = lb_ref[0:1, :]
        sig, fg = _hgrn_gates(f_ref, lb, a_scr, k_scr, tr)
        gnv = gn_ref[...]
        rr = lax.broadcasted_iota(jnp.int32, (CHUNK, CHUNK), 0)
        cc = lax.broadcasted_iota(jnp.int32, (CHUNK, CHUNK), 1)
        causal = cc <= rr
        rowi = lax.broadcasted_iota(jnp.int32, (CHUNK, HEAD_DIM), 0)

        def chunk(it, carry):
            n = ncl - 1 - it
            rows = _chunk_rows(n)
            for hd0 in range(0, hpg, PHASE_HEADS):
                heads(n, rows, range(hd0, min(hpg, hd0 + PHASE_HEADS)))
            return carry

        def heads(n, rows, ids):
            lanes = [slice(hd * HEAD_DIM, (hd + 1) * HEAD_DIM) for hd in ids]
            hs = []
            for hd, ls in zip(ids, lanes):
                h = {}
                ah, kh = a_scr[rows, ls], k_scr[rows, ls]
                qp = q_ref[rows, ls]
                sq = _sigmoid(qp)
                qh = qp * sq
                h["dsilu_q"] = sq * (1.0 + qp * (1.0 - sq))
                h["vb"] = i_ref[rows, ls].astype(BF16)
                aref, alast = ah[CHUNK // 2 - 1:CHUNK // 2, :], ah[CHUNK - 1:CHUNK, :]
                h["e1"], h["e2"] = jnp.exp(ah - aref), jnp.exp(aref - ah)
                h["e3"], h["e4"] = jnp.exp(ah), jnp.exp(alast - ah)
                h["dec"] = jnp.exp(alast)
                h["q_in"], h["k_in"], h["q_out"], h["k_out"] = qh * h["e1"], kh * h["e2"], qh * h["e3"], kh * h["e4"]
                for nm in ("q_in", "k_in", "q_out", "k_out"):
                    h[nm + "_b"] = h[nm].astype(BF16)
                o = o_ref[rows, ls]
                rstd = lax.rsqrt(jnp.mean(o * o, axis=-1, keepdims=True) + EPS)
                ohat = o * rstd
                gg = g_ref[rows, ls]
                sg = _sigmoid(gg)
                dyv = dy_ref[rows, ls]
                d_on = dyv * (gg * sg)
                dp_ref[3, rows, ls] = (dyv * (ohat * gnv) * (sg * (1.0 + gg * (1.0 - sg)))).astype(BF16)
                h["dgn"] = jnp.sum(d_on * ohat, axis=0, keepdims=True)
                dohat = d_on * gnv
                do = rstd * (dohat - ohat * jnp.mean(dohat * ohat, axis=-1, keepdims=True))
                h["do_b"] = do.astype(BF16)
                h["st_prev"] = st_ref[n, hd]
                h["dst"] = dst_scr[hd]
                hs.append(h)
            for h in hs:
                dst_b = h["dst"].astype(BF16)
                h["scores"] = _dot_nt(h["q_in_b"], h["k_in_b"])
                h["dscores"] = _dot_nt(h["do_b"], h["vb"])
                h["dv_inter"] = _dot_nt(h["k_out_b"], dst_b)
                h["dq_out"] = _dot(h["do_b"], h["st_prev"].astype(BF16))
                h["dk_out"] = _dot(h["vb"], dst_b)
                h["dst_mm"] = _dot_tn(h["do_b"], h["q_out_b"])
            for h in hs:
                scores = jnp.where(causal, h["scores"], 0.0).astype(BF16)
                dscores = jnp.where(causal, h["dscores"], 0.0).astype(BF16)
                h["dv"] = _dot_tn(scores, h["do_b"]) + h["dv_inter"]
                h["dq_in"] = _dot(dscores, h["k_in_b"])
                h["dk_in"] = _dot_tn(dscores, h["q_in_b"])
            dgn = hs[0]["dgn"]
            for h in hs[1:]:
                dgn = dgn + h["dgn"]
            dgn_ref[...] += dgn
            for hd, h, ls in zip(ids, hs, lanes):
                ddec = jnp.sum(h["dst"] * h["st_prev"], axis=0, keepdims=True)
                dst_scr[hd] = h["dst"] * h["dec"] + h["dst_mm"]
                dp_ref[2, rows, ls] = h["dv"].astype(BF16)
                dq = h["dq_in"] * h["e1"] + h["dq_out"] * h["e3"]
                dp_ref[0, rows, ls] = (dq * h["dsilu_q"]).astype(BF16)
                dk_scr[rows, ls] = h["dk_in"] * h["e2"] + h["dk_out"] * h["e4"]
                t_in = h["dq_in"] * h["q_in"] - h["dk_in"] * h["k_in"]
                t_out = h["dk_out"] * h["k_out"]
                da = t_in + h["dq_out"] * h["q_out"] - t_out
                da_ref_row = -jnp.sum(t_in, axis=0, keepdims=True)
                da_last_row = jnp.sum(t_out, axis=0, keepdims=True) + ddec * h["dec"]
                da = da + jnp.where(rowi == CHUNK // 2 - 1, da_ref_row, 0.0) \
                        + jnp.where(rowi == CHUNK - 1, da_last_row, 0.0)
                da_scr[rows, ls] = da

        if ncl <= 2:
            for it in range(ncl):
                chunk(it, 0)
        else:
            lax.fori_loop(0, ncl, chunk, 0)
        g = min(CUM_ROWS, tr)
        tri = _tri_mask(g, reverse=True)
        for rg in range(tr // g):
            rs = slice(rg * g, (rg + 1) * g)
            dlogf = _tri_apply(tri, da_scr[rs, :])
            df = dlogf / fg[rs, :] - dk_scr[rs, :]
            sgr = sig[rs, :]
            dp_ref[1, rs, :] = (df * (1.0 - lb) * (sgr * (1.0 - sgr))).astype(BF16)
            dlb_ref[...] += jnp.sum(df * (1.0 - sgr), axis=0, keepdims=True) * lb_ref[1:2, :]

    blk = pl.BlockSpec((tr, hc), lambda hg, b, t: (b * nt + (nt - 1 - t), hg))
    return _pc(body, name="hgrn_bwd",
               out_shape=[jax.ShapeDtypeStruct((4, m, di), BF16), jax.ShapeDtypeStruct((1, di), F32),
                          jax.ShapeDtypeStruct((1, HEAD_DIM), F32)],
               grid=(nhg, nb, nt),
               in_specs=[pl.BlockSpec((4, tr, hc), lambda hg, b, t: (0, b * nt + (nt - 1 - t), hg)), blk, blk,
                         pl.BlockSpec((ncl, hpg, HEAD_DIM, HEAD_DIM),
                                      lambda hg, b, t: (b * nt + (nt - 1 - t), hg, 0, 0)),
                         pl.BlockSpec((2, hc), lambda hg, b, t: (0, hg)),
                         pl.BlockSpec((1, HEAD_DIM), lambda hg, b, t: (0, 0))],
               out_specs=[pl.BlockSpec((4, tr, hc), lambda hg, b, t: (0, b * nt + (nt - 1 - t), hg)),
                          pl.BlockSpec((1, hc), lambda hg, b, t: (0, hg)),
                          pl.BlockSpec((1, HEAD_DIM), lambda hg, b, t: (0, 0))],
               scratch=[pltpu.VMEM((hpg, HEAD_DIM, HEAD_DIM), F32)] + [pltpu.VMEM((tr, hc), F32)] * 4,
               sem=("arbitrary", "arbitrary", "arbitrary"), comm=comm)(
                   proj, o_all, dybr, states, lbj, gn)


def _adamw(parts, w, m, v, name):
    r, c = w.shape
    tr = _tile(r, 256)
    npart = len(parts)
    c1 = 1.0 - ADAM_B1 ** ADAM_STEP
    c2 = 1.0 - ADAM_B2 ** ADAM_STEP

    def body(*refs):
        p_refs = refs[:npart]
        _adamw_math(p_refs, *refs[npart:], c1, c2)

    blk = pl.BlockSpec((tr, c), lambda i: (i, 0))
    return _pc(body, name=name, out_shape=[jax.ShapeDtypeStruct((r, c), F32)] * 4, grid=(r // tr,),
               in_specs=[blk] * (npart + 3), out_specs=[blk] * 4, sem=("parallel",))(*parts, w, m, v)


def _adamw_math(p_refs, w_ref, m_ref, v_ref, g_ref, d_ref, nm_ref, nv_ref, c1, c2):
    g = p_refs[0][...].astype(F32)
    for p in p_refs[1:]:
        g = g + p[...].astype(F32)
    nm = ADAM_B1 * m_ref[...] + (1.0 - ADAM_B1) * g
    nv = ADAM_B2 * v_ref[...] + (1.0 - ADAM_B2) * (g * g)
    g_ref[...] = g
    nm_ref[...] = nm
    nv_ref[...] = nv
    d_ref[...] = -ADAM_LR * ((nm / c1) / (jnp.sqrt(nv / c2) + ADAM_EPS) + ADAM_WD * w_ref[...])


def _adamw_small(gathered, ws, ms, vs, name, sums=()):
    n, ns = len(ws), len(sums)
    c1 = 1.0 - ADAM_B1 ** ADAM_STEP
    c2 = 1.0 - ADAM_B2 ** ADAM_STEP

    def total(ref):
        g = ref[0]
        for part in range(1, ref.shape[0]):
            g = g + ref[part]
        return g

    def body(*refs):
        g_in, w_in, m_in, v_in = refs[:n], refs[n:2 * n], refs[2 * n:3 * n], refs[3 * n:4 * n]
        s_in = refs[4 * n:4 * n + ns]
        outs = refs[4 * n + ns:]
        for k in range(n):
            g = total(g_in[k])
            nm = ADAM_B1 * m_in[k][...] + (1.0 - ADAM_B1) * g
            nv = ADAM_B2 * v_in[k][...] + (1.0 - ADAM_B2) * (g * g)
            outs[4 * k][...] = g
            outs[4 * k + 1][...] = -ADAM_LR * ((nm / c1) / (jnp.sqrt(nv / c2) + ADAM_EPS) + ADAM_WD * w_in[k][...])
            outs[4 * k + 2][...] = nm
            outs[4 * k + 3][...] = nv
        for k in range(ns):
            outs[4 * n + k][...] = total(s_in[k])

    out_shape = [jax.ShapeDtypeStruct(w.shape, F32) for w in ws for _ in range(4)]
    out_shape += [jax.ShapeDtypeStruct(s.shape[1:], F32) for s in sums]
    res = _pc(body, name=name, out_shape=out_shape)(*gathered, *ws, *ms, *vs, *sums)
    return [res[4 * k:4 * k + 4] for k in range(n)] + list(res[4 * n:])


def _adamw_blocks(parts, idx, w, m, v, name):
    r, c = w.shape
    tr = _tile(r, 256)
    npart = len(parts)
    c1 = 1.0 - ADAM_B1 ** ADAM_STEP
    c2 = 1.0 - ADAM_B2 ** ADAM_STEP

    def body(idx_ref, *refs):
        _adamw_math(refs[:npart], *refs[npart:], c1, c2)

    def sel(p):
        return pl.BlockSpec((None, tr, c), lambda i, s: (s[p], i, 0))

    blk = pl.BlockSpec((tr, c), lambda i, s: (i, 0))
    gs = pltpu.PrefetchScalarGridSpec(num_scalar_prefetch=1, grid=(r // tr,),
                                      in_specs=[sel(p) for p in range(npart)] + [blk] * 3, out_specs=[blk] * 4)
    return _pc(body, name=name, out_shape=[jax.ShapeDtypeStruct((r, c), F32)] * 4, grid_spec=gs,
               sem=("parallel",))(idx, *parts, w, m, v)


_EARLY = ["a_ln_gain", "a_ln_bias", "a_w_s", "a_b_s", "b_lower_bounds", "b_gn_gain"]


def kernel(x, c, norm_gain, w_ada, b_ada, a_w_in, a_ln_gain, a_ln_bias, a_w_s, a_b_s, a_w_out, b_w_in, b_lower_bounds, b_gn_gain, b_w_out, final_gain, loss_target, m_norm_gain, m_w_ada, m_b_ada, m_a_w_in, m_a_ln_gain, m_a_ln_bias, m_a_w_s, m_a_b_s, m_a_w_out, m_b_w_in, m_b_lower_bounds, m_b_gn_gain, m_b_w_out, m_final_gain, v_norm_gain, v_w_ada, v_b_ada, v_a_w_in, v_a_ln_gain, v_a_ln_bias, v_a_w_s, v_a_b_s, v_a_w_out, v_b_w_in, v_b_lower_bounds, v_b_gn_gain, v_b_w_out, v_final_gain):
    w = dict(norm_gain=norm_gain, w_ada=w_ada, b_ada=b_ada, a_w_in=a_w_in, a_ln_gain=a_ln_gain,
             a_ln_bias=a_ln_bias, a_w_s=a_w_s, a_b_s=a_b_s, a_w_out=a_w_out, b_w_in=b_w_in,
             b_lower_bounds=b_lower_bounds, b_gn_gain=b_gn_gain, b_w_out=b_w_out, final_gain=final_gain)
    mo = dict(norm_gain=m_norm_gain, w_ada=m_w_ada, b_ada=m_b_ada, a_w_in=m_a_w_in, a_ln_gain=m_a_ln_gain,
              a_ln_bias=m_a_ln_bias, a_w_s=m_a_w_s, a_b_s=m_a_b_s, a_w_out=m_a_w_out, b_w_in=m_b_w_in,
              b_lower_bounds=m_b_lower_bounds, b_gn_gain=m_b_gn_gain, b_w_out=m_b_w_out, final_gain=m_final_gain)
    vo = dict(norm_gain=v_norm_gain, w_ada=v_w_ada, b_ada=v_b_ada, a_w_in=v_a_w_in, a_ln_gain=v_a_ln_gain,
              a_ln_bias=v_a_ln_bias, a_w_s=v_a_w_s, a_b_s=v_a_b_s, a_w_out=v_a_w_out, b_w_in=v_b_w_in,
              b_lower_bounds=v_b_lower_bounds, b_gn_gain=v_b_gn_gain, b_w_out=v_b_w_out, final_gain=v_final_gain)

    nb, t_seq, d = x.shape
    m = nb * t_seq
    ncol_ada = w_ada.shape[2]
    xi, yi, ci = lax.axis_index("x"), lax.axis_index("y"), lax.axis_index("c")
    me = 4 * xi + 2 * yi + ci

    c_g, wa_in_g = _all_gather([c, a_w_in[0].astype(BF16)], "gather_c_wa")

    c_all = c_g.reshape(NDEV * nb, d)
    b_cols = lax.dynamic_slice(b_ada, (0, me * ncol_ada), (2, ncol_ada)).reshape(2, 1, ncol_ada)
    mod_part, lbj = _ada_fwd(c_all, w_ada, b_cols, b_lower_bounds)
    mod_all = _all_gather([mod_part], "gather_mod")[0]
    mod_mine = lax.dynamic_slice_in_dim(mod_all, me * nb, nb, axis=2)
    mod_mine = mod_mine.transpose(1, 2, 0, 3).reshape(2, nb, 3, d)
    mod0, mod1 = mod_mine[0], mod_mine[1]

    di = a_w_out.shape[1] * NDEV

    xf = x.reshape(m, d)
    tgt = loss_target.reshape(m, d)
    ng0, ng1 = norm_gain[0:1], norm_gain[1:2]
    ncb = b_w_in.shape[2]
    wb_lo, wb_hi = b_w_in[0][:, :ncb // 2].astype(BF16), b_w_in[0][:, ncb // 2:].astype(BF16)
    h0, h0_t = _prenorm(xf, ng0, mod0, t_seq, "prenorm_a")
    proj_a, half = _mm_in(h0, [wa_in_g], 1, "in_proj_a", comm=_gather_first([a_w_out[0].astype(BF16), wb_lo]))
    bs_t = jnp.pad(a_b_s[0].T, ((0, 0), (0, 128 - SG_GROUPS)))
    ybr_a, (wa_out_g, wb_lo_g, wb_hi_half) = _a_mid_fwd(
        proj_a, a_ln_gain, a_ln_bias, a_w_s[0], bs_t, t_seq, comm=_join(_gather_second(half), _gather_first([wb_hi])))
    wa_out = wa_out_g.reshape(di, d)
    (yout_a, x1), (wb_hi_g, wb_out_half) = _out_proj(
        ybr_a, wa_out, xf, mod0, t_seq, "out_proj_a",
        comm=_join(_gather_second([wb_hi_half]), _gather_first([b_w_out[0].astype(BF16)])))
    wb_in_g = [wb_lo_g, wb_hi_g]
    h1, h1_t = _prenorm(x1, ng1, mod1, t_seq, "prenorm_b")
    proj_b, (wb_out_g,) = _mm_in(h1, wb_in_g, 4, "in_proj_b", comm=_gather_second([wb_out_half]))
    wb_out = wb_out_g.reshape(di, d)
    o_b, ybr_b, states = _hgrn_fwd(proj_b, lbj, b_gn_gain, nb, t_seq)
    yout_b, dx2, loss_part, d_final_gain = _out_proj_loss(ybr_b, wb_out, x1, mod1, final_gain.reshape(1, d), tgt, t_seq)

    rows_out = a_w_out.shape[1]
    dy_b, dgate1, dybr_b = _gate_dybr(dx2, yout_b, mod1, wb_out, t_seq, "dybr_b")
    rs_wb_out = _ReduceScatter(_mm_dw_out(ybr_b, dy_b, "dw_out_b").reshape(NDEV, rows_out, d), "b_w_out")
    (dproj_b, d_lb, d_gn), got = _hgrn_bwd(proj_b, o_b, dybr_b, states, lbj, b_gn_gain, nb, t_seq,
                                           comm=rs_wb_out.swap_core())
    rs_wb_out.after_core(got[0])
    dh1, got = _mm_din(dproj_b, wb_in_g, 4, "dh_b", comm=rs_wb_out.swap_chips())
    rs_wb_out.after_chips(got[0])
    dx1, dss1, dgain1 = _prenorm_bwd(dh1, x1, ng1, mod1, dx2, t_seq, "prenorm_bwd_b")
    rs_wb_in = _ReduceScatter(_mm_dw_in(h1_t, dproj_b, ncb, 4, "dw_in_b"), "b_w_in")

    dy_a, dgate0, dybr_a = _gate_dybr(dx1, yout_a, mod0, wa_out, t_seq, "dybr_a")
    g_wa_out, got = _mm_dw_out(ybr_a, dy_a, "dw_out_a", comm=rs_wb_in.swap_core())
    rs_wb_in.after_core(got[0])
    rs_wa_out = _ReduceScatter(g_wa_out.reshape(NDEV, rows_out, d), "a_w_out")
    (dproj_a, d_lng, d_lnb, d_ws, d_bs_t), got = _a_mid_bwd(
        proj_a, dybr_a, a_ln_gain, a_ln_bias, a_w_s[0], bs_t, t_seq,
        comm=_join(rs_wb_in.swap_chips(), rs_wa_out.swap_core()))
    rs_wb_in.after_chips(got[0])
    rs_wa_out.after_core(got[1])
    early_parts = [d_lng, d_lnb, d_ws.reshape(SG_GROUPS * SG_BLOCK, SG_BLOCK), d_bs_t[:, :SG_GROUPS].T,
                   jnp.concatenate([-d_lb, d_lb], axis=0), d_gn]
    g_wa_in, got = _mm_dw_in(h0_t, dproj_a, wa_in_g.shape[2], 1, "dw_in_a",
                             comm=_join(rs_wa_out.swap_chips(), _gather_first(early_parts)))
    rs_wa_out.after_chips(got[0])
    rs_wa_in = _ReduceScatter(g_wa_in, "a_w_in")
    n_tiles = m // _din_tile(m)
    assert n_tiles >= 2
    first_tiles = max(1, (3 * n_tiles) // 8)
    dh0, got2 = _mm_din(dproj_a, [wa_in_g], 1, "dh_a_first", tiles=(0, first_tiles),
                        comm=_join(rs_wa_in.swap_core(), _gather_second(got[1:])))
    rs_wa_in.after_core(got2[0])
    early_all = got2[1:]
    dh0, got = _mm_din(dproj_a, [wa_in_g], 1, "dh_a_rest", comm=rs_wa_in.swap_chips(),
                       tiles=(first_tiles, n_tiles - first_tiles), prev=dh0)
    rs_wa_in.after_chips(got[0])
    dx0, dss0, dgain0 = _prenorm_bwd(dh0, xf, ng0, mod0, dx1, t_seq, "prenorm_bwd_a")
    grad_x = dx0.reshape(nb, t_seq, d)

    dmod = jnp.stack([jnp.concatenate([dss0, dgate0], axis=1), jnp.concatenate([dss1, dgate1], axis=1)])
    dmod_all, dgain_all, dfinal_all, loss_all = _all_gather(
        [dmod.reshape(2, nb, 3 * d), jnp.concatenate([dgain0, dgain1], axis=0), d_final_gain,
         jnp.broadcast_to(loss_part, (1, 128))], "gather_tail")
    dmod_all = dmod_all.transpose(1, 0, 2, 3).reshape(2, NDEV * nb, 3 * d)
    dmod_cols = lax.dynamic_slice_in_dim(dmod_all, me * ncol_ada, ncol_ada, axis=2)
    g_w_ada, g_b_ada = _ada_bwd(c_all, dmod_cols, dmod_all)

    def small2d(k, t):
        return t[k].reshape(early_parts[_EARLY.index(k)].shape if k in _EARLY else (-1, t[k].shape[-1]))

    res = {}
    sm = _adamw_small(early_all, *[[small2d(k, t) for k in _EARLY] for t in (w, mo, vo)], "adamw_small_early")
    for k, r in zip(_EARLY, sm):
        res[k] = tuple(z.reshape(w[k].shape) for z in r)
    late = ["norm_gain", "final_gain", "b_ada"]
    sm = _adamw_small([dgain_all, dfinal_all, g_b_ada[None]], *[[small2d(k, t) for k in late] for t in (w, mo, vo)],
                      "adamw_small_late", sums=[loss_all])
    for k, r in zip(late, sm):
        res[k] = tuple(z.reshape(w[k].shape) for z in r)
    loss = sm[3][0, 0]
    sh = w_ada.shape
    ra = _adamw([g_w_ada.reshape(sh[0] * sh[1], sh[2])], w_ada.reshape(sh[0] * sh[1], sh[2]),
                mo["w_ada"].reshape(sh[0] * sh[1], sh[2]), vo["w_ada"].reshape(sh[0] * sh[1], sh[2]), "adamw_w_ada")
    res["w_ada"] = tuple(z.reshape(sh) for z in ra)

    for k, rs in (("b_w_out", rs_wb_out), ("b_w_in", rs_wb_in), ("a_w_out", rs_wa_out), ("a_w_in", rs_wa_in)):
        res[k] = tuple(z[None] for z in _adamw_blocks(rs.parts, rs.idx, w[k][0], mo[k][0], vo[k][0], "adamw_" + k))

    order = ["norm_gain", "w_ada", "b_ada", "a_w_in", "a_ln_gain", "a_ln_bias", "a_w_s", "a_b_s", "a_w_out",
             "b_w_in", "b_lower_bounds", "b_gn_gain", "b_w_out", "final_gain"]
    return (loss, grad_x, *[res[k][0] for k in order], *[res[k][1] for k in order],
            *[res[k][2] for k in order], *[res[k][3] for k in order])
```

```python
import functools
import math

import jax
import jax.numpy as jnp
from jax import lax
from jax.experimental import pallas as pl
from jax.experimental.pallas import tpu as pltpu

F32 = jnp.float32
BF16 = jnp.bfloat16
MESH = pl.DeviceIdType.MESH
NDEV = 8
EPS = 1e-6
CHUNK = 64
SG_BLOCK = 128
SG_GROUPS = 8
HEAD_DIM = 128
CUM_ROWS = 256
PHASE_HEADS = 8
ADAM_LR, ADAM_B1, ADAM_B2, ADAM_EPS, ADAM_WD, ADAM_STEP = 0.001, 0.9, 0.999, 1e-08, 0.01, 10
VMEM_LIMIT = 56 * 1024 * 1024
ANY = pl.BlockSpec(memory_space=pl.ANY)


class _Hosted:
    def __init__(self, arrays, out_shapes, nsem, start, finish, aliases=None):
        self.arrays, self.out_shapes, self.nsem = list(arrays), list(out_shapes), nsem
        self.start, self.finish = start, finish
        self.aliases = dict(aliases or {})


def _join(*comms):
    arrays, outs, aliases, offs, nsem = [], [], {}, [], 0
    for cm in comms:
        offs.append((len(arrays), len(outs), nsem))
        for i, o in cm.aliases.items():
            aliases[len(arrays) + i] = len(outs) + o
        arrays += cm.arrays
        outs += cm.out_shapes
        nsem += cm.nsem

    def run(which):
        def f(ins, outs_, ss, rs, base):
            for cm, (ia, io, isem) in zip(comms, offs):
                getattr(cm, which)(ins[ia:ia + len(cm.arrays)], outs_[io:io + len(cm.out_shapes)], ss, rs, base + isem)
        return f

    return _Hosted(arrays, outs, nsem, run("start"), run("finish"), aliases)


def _pc(body, *, name, out_shape, grid=None, in_specs=None, out_specs=None, scratch=(), sem=None,
        grid_spec=None, comm=None, aliases=None):
    cp = dict(vmem_limit_bytes=VMEM_LIMIT)
    aliases = dict(aliases or {})
    if comm is None:
        if sem is not None:
            cp["dimension_semantics"] = sem
        kw = {"input_output_aliases": aliases}
        if grid_spec is not None:
            kw["grid_spec"] = grid_spec
        else:
            if grid is not None:
                kw["grid"] = grid
            if in_specs is not None:
                kw["in_specs"] = in_specs
            if out_specs is not None:
                kw["out_specs"] = out_specs
            kw["scratch_shapes"] = list(scratch)
        return pl.pallas_call(functools.partial(body), name=name, out_shape=out_shape,
                              compiler_params=pltpu.CompilerParams(**cp), **kw)

    single = not isinstance(out_shape, (list, tuple))
    outs_list = [out_shape] if single else list(out_shape)
    ospecs = [out_specs] if single else list(out_specs)
    n_in, n_out, n_ci, n_co, n_scr = len(in_specs), len(outs_list), len(comm.arrays), len(comm.out_shapes), len(scratch)
    cp["dimension_semantics"] = ("arbitrary",) * len(grid)

    def hosted(*refs):
        cin, hin = refs[:n_in], refs[n_in:n_in + n_ci]
        cout = refs[n_in + n_ci:n_in + n_ci + n_out]
        hout = refs[n_in + n_ci + n_out:n_in + n_ci + n_out + n_co]
        scr = refs[n_in + n_ci + n_out + n_co:n_in + n_ci + n_out + n_co + n_scr]
        ssem, rsem = refs[-2], refs[-1]
        first = functools.reduce(lambda p, q: p & q, [pl.program_id(a) == 0 for a in range(len(grid))])
        last = functools.reduce(lambda p, q: p & q, [pl.program_id(a) == grid[a] - 1 for a in range(len(grid))])

        @pl.when(first)
        def _():
            comm.start(hin, hout, ssem, rsem, 0)

        body(*cin, *cout, *scr)

        @pl.when(last)
        def _():
            comm.finish(hin, hout, ssem, rsem, 0)

    call = pl.pallas_call(
        hosted, name=name, grid=grid, in_specs=list(in_specs) + [ANY] * n_ci, out_specs=ospecs + [ANY] * n_co,
        out_shape=outs_list + comm.out_shapes,
        scratch_shapes=list(scratch) + [pltpu.SemaphoreType.DMA((comm.nsem,)), pltpu.SemaphoreType.DMA((comm.nsem,))],
        input_output_aliases={**aliases, **{n_in + i: n_out + o for i, o in comm.aliases.items()}},
        compiler_params=pltpu.CompilerParams(**cp))

    def run(*args):
        res = call(*args, *comm.arrays)
        comp = res[:n_out]
        return (comp[0] if single else comp), list(res[n_out:])

    return run


def _tile(n, pref):
    return pref if n % pref == 0 else n


def _sigmoid(x):
    return 1.0 / (1.0 + jnp.exp(-x))


def _gelu(x):
    c = math.sqrt(2.0 / math.pi)
    return 0.5 * x * (1.0 + jnp.tanh(c * (x + 0.044715 * (x * x * x))))


def _gelu_and_grad(x):
    c = math.sqrt(2.0 / math.pi)
    x2 = x * x
    t = jnp.tanh(c * (x + 0.044715 * (x2 * x)))
    half = 0.5 * (1.0 + t)
    return x * half, half + (0.5 * x) * (1.0 - t * t) * (c + (3.0 * 0.044715 * c) * x2)


def _dot(a, b):
    return jnp.dot(a, b, preferred_element_type=F32)


def _dot_nt(a, b):
    return lax.dot_general(a, b, (((1,), (1,)), ((), ())), preferred_element_type=F32)


def _dot_tn(a, b):
    return lax.dot_general(a, b, (((0,), (0,)), ((), ())), preferred_element_type=F32)


def _tri_mask(n, reverse):
    r = lax.broadcasted_iota(jnp.int32, (n, n), 0)
    c = lax.broadcasted_iota(jnp.int32, (n, n), 1)
    same = (r // CHUNK) == (c // CHUNK)
    tri = (c >= r) if reverse else (c <= r)
    return jnp.where(same & tri, 1.0, 0.0).astype(BF16)


def _tri_apply(tri, x):
    hi = x.astype(BF16)
    r1 = x - hi.astype(F32)
    mid = r1.astype(BF16)
    lo = (r1 - mid.astype(F32)).astype(BF16)
    return _dot(tri, hi) + (_dot(tri, mid) + _dot(tri, lo))


def _all_gather(arrs, name):
    n = len(arrs)

    def body(*refs):
        ins, outs = refs[:n], refs[n:2 * n]
        send_sems, recv_sems, local_sems = refs[2 * n:]
        x, y, c = lax.axis_index("x"), lax.axis_index("y"), lax.axis_index("c")
        me, sibling = (x, y, c), (x, y, 1 - c)
        near = (x + c - 2 * x * c, y + (1 - c) - 2 * y * (1 - c))
        far = (x + (1 - c) - 2 * x * (1 - c), y + c - 2 * y * c)
        diag = (1 - x, 1 - y)

        def blk(a, p):
            return outs[a].at[4 * p[0] + 2 * p[1] + p[2]]

        def copy(a, k, block, to, src=None):
            return pltpu.make_async_remote_copy(
                src_ref=blk(a, block) if src is None else src, dst_ref=blk(a, block),
                send_sem=send_sems.at[7 * a + k], recv_sem=recv_sems.at[7 * a + k],
                device_id=to, device_id_type=MESH)

        mine = [pltpu.make_async_copy(ins[a], blk(a, me), local_sems.at[a]) for a in range(n)]
        for m in mine:
            m.start()
        sends = []
        for a in range(n):
            sends += [copy(a, 0, me, sibling, src=ins[a]), copy(a, 1, me, (*near, c), src=ins[a]),
                      copy(a, 2, me, (*far, c), src=ins[a])]
        for cp in sends:
            cp.start()
        for a in range(n):
            copy(a, 1, (*near, c), me).wait_recv()
            sends.append(copy(a, 3, (*near, c), (*far, c)))
            sends[-1].start()
        for a in range(n):
            sends.append(copy(a, 4, (*near, c), sibling))
            sends[-1].start()
            copy(a, 2, (*far, c), me).wait_recv()
            sends.append(copy(a, 5, (*far, c), sibling))
            sends[-1].start()
        for a in range(n):
            copy(a, 3, (*diag, c), me).wait_recv()
            sends.append(copy(a, 6, (*diag, c), sibling))
            sends[-1].start()
        for a in range(n):
            copy(a, 0, sibling, me).wait_recv()
            copy(a, 4, (*far, 1 - c), me).wait_recv()
            copy(a, 5, (*near, 1 - c), me).wait_recv()
            copy(a, 6, (*diag, 1 - c), me).wait_recv()
        for cp in sends:
            cp.wait_send()
        for m in mine:
            m.wait()

    out_shape = [jax.ShapeDtypeStruct((NDEV,) + a.shape, a.dtype) for a in arrs]
    return _pc(body, name=name, out_shape=out_shape, in_specs=[ANY] * n, out_specs=[ANY] * n,
               scratch=[pltpu.SemaphoreType.DMA((7 * n,)), pltpu.SemaphoreType.DMA((7 * n,)),
                        pltpu.SemaphoreType.DMA((n,))])(*arrs)


def _gather_first(arrs):
    n = len(arrs)

    def parts(ins, outs, ss, rs, base):
        x, y, c = lax.axis_index("x"), lax.axis_index("y"), lax.axis_index("c")
        me, sibling = (x, y, c), (x, y, 1 - c)
        chips = [(1 - x, y), (x, 1 - y), (1 - x, 1 - y)]

        def blk(a, p):
            return outs[a].at[4 * p[0] + 2 * p[1] + p[2]]

        def copy(a, k, block, to):
            return pltpu.make_async_remote_copy(
                src_ref=ins[a], dst_ref=blk(a, block), send_sem=ss.at[base + 4 * a + k],
                recv_sem=rs.at[base + 4 * a + k], device_id=to, device_id_type=MESH)

        local = [pltpu.make_async_copy(ins[a], blk(a, me), ss.at[base + 4 * n + a]) for a in range(n)]
        sends, recvs = [], []
        for a in range(n):
            sends.append(copy(a, 0, me, sibling))
            recvs.append(copy(a, 0, sibling, me))
            for j, chip in enumerate(chips):
                sends.append(copy(a, 1 + j, me, (*chip, c)))
                recvs.append(copy(a, 1 + j, (*chip, c), me))
        return local, sends, recvs

    def start(ins, outs, ss, rs, base):
        local, sends, _ = parts(ins, outs, ss, rs, base)
        for cp in local + sends:
            cp.start()

    def finish(ins, outs, ss, rs, base):
        local, sends, recvs = parts(ins, outs, ss, rs, base)
        for cp in recvs:
            cp.wait_recv()
        for cp in sends:
            cp.wait_send()
        for cp in local:
            cp.wait()

    return _Hosted(arrs, [jax.ShapeDtypeStruct((NDEV,) + a.shape, a.dtype) for a in arrs], 5 * n, start, finish)


def _gather_second(bufs):
    n = len(bufs)

    def parts(ins, outs, ss, rs, base):
        x, y, c = lax.axis_index("x"), lax.axis_index("y"), lax.axis_index("c")
        sibling = (x, y, 1 - c)
        chips = [(1 - x, y), (x, 1 - y), (1 - x, 1 - y)]
        sends, recvs = [], []
        for a in range(n):
            for j, chip in enumerate(chips):
                mine = 4 * chip[0] + 2 * chip[1] + c
                theirs = 4 * chip[0] + 2 * chip[1] + (1 - c)
                sends.append(pltpu.make_async_remote_copy(
                    src_ref=ins[a].at[mine], dst_ref=outs[a].at[mine], send_sem=ss.at[base + 3 * a + j],
                    recv_sem=rs.at[base + 3 * a + j], device_id=sibling, device_id_type=MESH))
                recvs.append(pltpu.make_async_remote_copy(
                    src_ref=ins[a].at[theirs], dst_ref=outs[a].at[theirs], send_sem=ss.at[base + 3 * a + j],
                    recv_sem=rs.at[base + 3 * a + j], device_id=sibling, device_id_type=MESH))
        return sends, recvs

    def start(ins, outs, ss, rs, base):
        for cp in parts(ins, outs, ss, rs, base)[0]:
            cp.start()

    def finish(ins, outs, ss, rs, base):
        sends, recvs = parts(ins, outs, ss, rs, base)
        for cp in recvs:
            cp.wait_recv()
        for cp in sends:
            cp.wait_send()

    return _Hosted(bufs, [jax.ShapeDtypeStruct(b.shape, b.dtype) for b in bufs], 3 * n, start, finish,
                   aliases={a: a for a in range(n)})


def _swap(src, nblk, ids_fn, partner_fn):
    def copies(ins, outs, ss, rs, base):
        x, y, c = lax.axis_index("x"), lax.axis_index("y"), lax.axis_index("c")
        ids = ids_fn(x, y, c)
        partner = partner_fn(x, y, c)
        return [pltpu.make_async_remote_copy(
            src_ref=ins[0].at[ids[k]], dst_ref=outs[0].at[k], send_sem=ss.at[base + k], recv_sem=rs.at[base + k],
            device_id=partner, device_id_type=MESH) for k in range(nblk)]

    def start(ins, outs, ss, rs, base):
        for cp in copies(ins, outs, ss, rs, base):
            cp.start()

    def finish(ins, outs, ss, rs, base):
        for cp in copies(ins, outs, ss, rs, base):
            cp.wait()

    return _Hosted([src], [jax.ShapeDtypeStruct((nblk,) + src.shape[1:], src.dtype)], nblk, start, finish)


def _swap_chips(send):
    def copies(ins, outs, ss, rs, base):
        x, y, c = lax.axis_index("x"), lax.axis_index("y"), lax.axis_index("c")
        chips = [(1 - x, y), (x, 1 - y), (1 - x, 1 - y)]
        return [pltpu.make_async_remote_copy(
            src_ref=ins[0].at[j], dst_ref=outs[0].at[j], send_sem=ss.at[base + j], recv_sem=rs.at[base + j],
            device_id=(*chip, c), device_id_type=MESH) for j, chip in enumerate(chips)]

    def start(ins, outs, ss, rs, base):
        for cp in copies(ins, outs, ss, rs, base):
            cp.start()

    def finish(ins, outs, ss, rs, base):
        for cp in copies(ins, outs, ss, rs, base):
            cp.wait()

    return _Hosted([send], [jax.ShapeDtypeStruct(send.shape, send.dtype)], 3, start, finish)


def _add_send(a, b, idx, ns, name):
    _, r, c = a.shape
    tr = _tile(r, 256)

    def body(idx_ref, a_ref, b_ref, send_ref):
        send_ref[...] = (a_ref[...] + b_ref[...]).astype(BF16)

    def sel(off):
        return pl.BlockSpec((None, tr, c), lambda k, i, s: (s[off + k], i, 0))

    gs = pltpu.PrefetchScalarGridSpec(num_scalar_prefetch=1, grid=(ns, r // tr), in_specs=[sel(0), sel(ns)],
                                      out_specs=pl.BlockSpec((None, tr, c), lambda k, i, s: (k, i, 0)))
    return _pc(body, name=name, grid_spec=gs, sem=("arbitrary", "arbitrary"),
               out_shape=jax.ShapeDtypeStruct((ns, r, c), BF16))(idx, a, b)


class _ReduceScatter:
    def __init__(self, g, tag):
        self.g, self.tag = g, tag

    def swap_core(self):
        return _swap(self.g, 4, lambda x, y, c: [1 - c, 3 - c, 5 - c, 7 - c], lambda x, y, c: (x, y, 1 - c))

    def after_core(self, recv):
        x, y, c = lax.axis_index("x"), lax.axis_index("y"), lax.axis_index("c")
        chips = [(1 - x, y), (x, 1 - y), (1 - x, 1 - y)]
        idx = jnp.stack([4 * p + 2 * q + c for p, q in chips] + [2 * p + q for p, q in chips]).astype(jnp.int32)
        self.send = _add_send(self.g, recv, idx, 3, "rs_add_" + self.tag)
        self.recv_core = recv
        zero = jnp.zeros((), jnp.int32)
        self.idx = jnp.stack([4 * x + 2 * y + c, 2 * x + y, zero, zero + 1, zero + 2]).astype(jnp.int32)

    def swap_chips(self):
        return _swap_chips(self.send)

    def after_chips(self, recv):
        self.parts = [self.g, self.recv_core, recv, recv, recv]


def _ada_fwd(c_all, w_ada, b_cols, b_lb):
    nl, d, ncol = w_ada.shape
    nseq = c_all.shape[0]
    di = b_lb.shape[1]

    def body(c_ref, w_ref, b_ref, lb_ref, mod_ref, lbj_ref):
        cv = c_ref[...]
        cact = (cv * _sigmoid(cv)).astype(BF16)
        for l in range(nl):
            mod_ref[l] = _dot(cact, w_ref[l].astype(BF16)) + b_ref[l]
        b0, b1 = lb_ref[0:1, :], lb_ref[1:2, :]
        mx = jnp.maximum(b0, b1)
        e0, e1 = jnp.exp(b0 - mx), jnp.exp(b1 - mx)
        s = e0 + e1
        p0, p1 = e0 / s, e1 / s
        lbj_ref[0:1, :] = (p0 + p1) - p0
        lbj_ref[1:2, :] = p0 * p1

    return _pc(body, name="ada_fwd",
               out_shape=[jax.ShapeDtypeStruct((nl, nseq, ncol), F32), jax.ShapeDtypeStruct((2, di), F32)]
               )(c_all, w_ada, b_cols, b_lb)


def _ada_bwd(c_all, dmod_cols, dmod_full):
    nl, nseq, ncol = dmod_cols.shape
    d = c_all.shape[1]
    d3 = dmod_full.shape[2]

    def body(c_ref, dc_ref, df_ref, gw_ref, gb_ref):
        cv = c_ref[...]
        cact = (cv * _sigmoid(cv)).astype(BF16)
        for l in range(nl):
            gw_ref[l] = _dot_tn(cact, dc_ref[l].astype(BF16))
            gb_ref[l:l + 1, :] = jnp.sum(df_ref[l], axis=0, keepdims=True)

    return _pc(body, name="ada_bwd",
               out_shape=[jax.ShapeDtypeStruct((nl, d, ncol), F32), jax.ShapeDtypeStruct((nl, d3), F32)]
               )(c_all, dmod_cols, dmod_full)


def _prenorm(x, gain, mod, t_seq, name):
    m, d = x.shape
    tm = _tile(t_seq, 1024)
    per = t_seq // tm

    def body(x_ref, g_ref, mod_ref, h_ref, ht_ref):
        xv = x_ref[...]
        rstd = lax.rsqrt(jnp.mean(xv * xv, axis=-1, keepdims=True) + EPS)
        r = xv * rstd * g_ref[...]
        h = r * (1.0 + mod_ref[0, 1:2, :]) + mod_ref[0, 0:1, :]
        h_ref[...] = h.astype(BF16)
        ht_ref[...] = h.T.astype(BF16)

    return _pc(body, name=name, out_shape=[jax.ShapeDtypeStruct((m, d), BF16), jax.ShapeDtypeStruct((d, m), BF16)],
               grid=(m // tm,),
               in_specs=[pl.BlockSpec((tm, d), lambda i: (i, 0)), pl.BlockSpec((1, d), lambda i: (0, 0)),
                         pl.BlockSpec((1, 3, d), lambda i: (i // per, 0, 0))],
               out_specs=[pl.BlockSpec((tm, d), lambda i: (i, 0)), pl.BlockSpec((d, tm), lambda i: (0, i))],
               sem=("parallel",))(x, gain, mod)


def _prenorm_bwd(dh, x, gain, mod, dxn, t_seq, name):
    m, d = x.shape
    nb = m // t_seq
    tm = _tile(t_seq, 1024)
    per = t_seq // tm

    def body(dh_ref, x_ref, g_ref, mod_ref, dxn_ref, dx_ref, dss_ref, dg_ref):
        i = pl.program_id(0)
        xv, dhv, g = x_ref[...], dh_ref[...], g_ref[...]
        rstd = lax.rsqrt(jnp.mean(xv * xv, axis=-1, keepdims=True) + EPS)
        xhat = xv * rstd
        dr = dhv * (1.0 + mod_ref[0, 1:2, :])
        dxhat = dr * g
        dx_ref[...] = dxn_ref[...] + rstd * (dxhat - xhat * jnp.mean(dxhat * xhat, axis=-1, keepdims=True))

        @pl.when(i % per == 0)
        def _():
            dss_ref[...] = jnp.zeros_like(dss_ref)

        @pl.when(i == 0)
        def _():
            dg_ref[...] = jnp.zeros_like(dg_ref)

        dss_ref[0, 0:1, :] += jnp.sum(dhv, axis=0, keepdims=True)
        dss_ref[0, 1:2, :] += jnp.sum(dhv * (xhat * g), axis=0, keepdims=True)
        dg_ref[...] += jnp.sum(dr * xhat, axis=0, keepdims=True)

    row = pl.BlockSpec((tm, d), lambda i: (i, 0))
    return _pc(body, name=name,
               out_shape=[jax.ShapeDtypeStruct((m, d), F32), jax.ShapeDtypeStruct((nb, 2, d), F32),
                          jax.ShapeDtypeStruct((1, d), F32)],
               grid=(m // tm,),
               in_specs=[row, row, pl.BlockSpec((1, d), lambda i: (0, 0)),
                         pl.BlockSpec((1, 3, d), lambda i: (i // per, 0, 0)), row],
               out_specs=[row, pl.BlockSpec((1, 2, d), lambda i: (i // per, 0, 0)),
                          pl.BlockSpec((1, d), lambda i: (0, 0))],
               sem=("arbitrary",))(dh, x, gain, mod, dxn)


def _mm_in(h, ws, sections, name, comm=None):
    m, k = h.shape
    nw = len(ws)
    widths = [w.shape[2] for w in ws]
    offs = [sum(widths[:a]) for a in range(nw)]
    nc = sum(widths)
    per = NDEV // sections if sections > 1 else NDEV
    tm = _din_tile(m)
    assert per % 2 == 0

    def body(*refs):
        hv = refs[0][...]
        o_ref = refs[1 + nw]
        for b in range(2):
            for a in range(nw):
                lo = b * nc + offs[a]
                o_ref[:, lo:lo + widths[a]] = _dot(hv, refs[1 + a][b])

    w_specs = [pl.BlockSpec((2, k, wd), lambda j, i: (j, 0, 0)) for wd in widths]
    if sections > 1:
        out_shape = jax.ShapeDtypeStruct((sections, m, per * nc), F32)
        out_spec = pl.BlockSpec((None, tm, 2 * nc), lambda j, i: ((2 * j) // per, i, ((2 * j) % per) // 2))
    else:
        out_shape = jax.ShapeDtypeStruct((m, NDEV * nc), F32)
        out_spec = pl.BlockSpec((tm, 2 * nc), lambda j, i: (i, j))
    return _pc(body, name=name, out_shape=out_shape, grid=(NDEV // 2, m // tm),
               in_specs=[pl.BlockSpec((tm, k), lambda j, i: (i, 0))] + w_specs,
               out_specs=out_spec, sem=("parallel", "parallel"), comm=comm)(h, *ws)


def _din_tile(m):
    return 1024 if m % 1024 == 0 and m >= 2048 else _tile(m, 512)


def _mm_din(dproj, ws, sections, name, comm=None, tiles=None, prev=None):
    nw, k = len(ws), ws[0].shape[1]
    widths = [w.shape[2] for w in ws]
    offs = [sum(widths[:a]) for a in range(nw)]
    nc = sum(widths)
    m = dproj.shape[-2]
    tm = _din_tile(m)
    t0, nt = tiles if tiles is not None else (0, m // tm)
    per = NDEV // sections if sections > 1 else NDEV
    assert per % 2 == 0

    def body(*refs):
        d_ref, o_ref = refs[0], refs[-1]
        j = pl.program_id(1)
        acc = None
        for b in range(2):
            for a in range(nw):
                lo = b * nc + offs[a]
                term = _dot_nt(d_ref[:, lo:lo + widths[a]], refs[1 + a][b])
                acc = term if acc is None else acc + term

        @pl.when(j == 0)
        def _():
            o_ref[...] = acc

        @pl.when(j > 0)
        def _():
            o_ref[...] += acc

    if sections > 1:
        dspec = pl.BlockSpec((None, tm, 2 * nc), lambda i, j: ((2 * j) // per, i + t0, ((2 * j) % per) // 2))
    else:
        dspec = pl.BlockSpec((tm, 2 * nc), lambda i, j: (i + t0, j))
    in_specs = [dspec] + [pl.BlockSpec((2, k, wd), lambda i, j: (j, 0, 0)) for wd in widths]
    args = [dproj, *ws]
    if prev is not None:
        in_specs.append(ANY)
        args.append(prev)
    return _pc(body, name=name, out_shape=jax.ShapeDtypeStruct((m, k), F32), grid=(nt, NDEV // 2), in_specs=in_specs,
               out_specs=pl.BlockSpec((tm, k), lambda i, j: (i + t0, 0)), sem=("parallel", "arbitrary"),
               comm=comm, aliases={1 + nw: 0} if prev is not None else None)(*args)


def _mm_dw_in(ht, dproj, nc, sections, name, comm=None):
    k, m = ht.shape
    per = NDEV // sections if sections > 1 else NDEV

    def body(h_ref, d_ref, o_ref):
        o_ref[...] = _dot(h_ref[...], d_ref[...])

    if sections > 1:
        dspec = pl.BlockSpec((None, m, nc), lambda j: (j // per, 0, j % per))
    else:
        dspec = pl.BlockSpec((m, nc), lambda j: (0, j))
    return _pc(body, name=name, out_shape=jax.ShapeDtypeStruct((NDEV, k, nc), F32), grid=(NDEV,),
               in_specs=[pl.BlockSpec((k, m), lambda j: (0, 0)), dspec],
               out_specs=pl.BlockSpec((None, k, nc), lambda j: (j, 0, 0)),
               sem=("parallel",), comm=comm)(ht, dproj)


def _out_proj(ybr, w_out, x, mod, t_seq, name, comm=None):
    m, di = ybr.shape
    d = w_out.shape[1]
    tm = _tile(t_seq, 512)
    per = t_seq // tm

    def body(y_ref, w_ref, x_ref, mod_ref, yo_ref, xn_ref):
        yo = _dot(y_ref[...], w_ref[...])
        yo_ref[...] = yo
        xn_ref[...] = x_ref[...] + mod_ref[0, 2:3, :] * yo

    row = pl.BlockSpec((tm, d), lambda i: (i, 0))
    return _pc(body, name=name,
               out_shape=[jax.ShapeDtypeStruct((m, d), F32), jax.ShapeDtypeStruct((m, d), F32)],
               grid=(m // tm,),
               in_specs=[pl.BlockSpec((tm, di), lambda i: (i, 0)), pl.BlockSpec((di, d), lambda i: (0, 0)), row,
                         pl.BlockSpec((1, 3, d), lambda i: (i // per, 0, 0))],
               out_specs=[row, row], sem=("parallel",), comm=comm)(ybr, w_out, x, mod)


def _out_proj_loss(ybr, w_out, x, mod, gain, target, t_seq):
    m, di = ybr.shape
    d = w_out.shape[1]
    tm = _tile(t_seq, 512)
    per = t_seq // tm

    def body(y_ref, w_ref, x_ref, mod_ref, g_ref, t_ref, yo_ref, dx_ref, loss_ref, dg_ref):
        i = pl.program_id(0)
        yo = _dot(y_ref[...], w_ref[...])
        yo_ref[...] = yo
        xv = x_ref[...] + mod_ref[0, 2:3, :] * yo
        g = g_ref[...]
        rstd = lax.rsqrt(jnp.mean(xv * xv, axis=-1, keepdims=True) + EPS)
        xhat = xv * rstd
        err = xhat * g - t_ref[...]
        dy = err * (1.0 / d)
        dxhat = dy * g
        dx_ref[...] = rstd * (dxhat - xhat * jnp.mean(dxhat * xhat, axis=-1, keepdims=True))

        @pl.when(i == 0)
        def _():
            loss_ref[...] = jnp.zeros_like(loss_ref)
            dg_ref[...] = jnp.zeros_like(dg_ref)

        loss_ref[...] += 0.5 * jnp.sum(jnp.mean(err * err, axis=-1, keepdims=True), axis=0, keepdims=True)
        dg_ref[...] += jnp.sum(dy * xhat, axis=0, keepdims=True)

    row = pl.BlockSpec((tm, d), lambda i: (i, 0))
    vec = pl.BlockSpec((1, d), lambda i: (0, 0))
    return _pc(body, name="out_proj_loss",
               out_shape=[jax.ShapeDtypeStruct((m, d), F32), jax.ShapeDtypeStruct((m, d), F32),
                          jax.ShapeDtypeStruct((1, 1), F32), jax.ShapeDtypeStruct((1, d), F32)],
               grid=(m // tm,),
               in_specs=[pl.BlockSpec((tm, di), lambda i: (i, 0)), pl.BlockSpec((di, d), lambda i: (0, 0)), row,
                         pl.BlockSpec((1, 3, d), lambda i: (i // per, 0, 0)), vec, row],
               out_specs=[row, row, pl.BlockSpec((1, 1), lambda i: (0, 0)), vec],
               sem=("arbitrary",))(ybr, w_out, x, mod, gain, target)


def _gate_dybr(dxn, yout, mod, w_out, t_seq, name):
    m, d = dxn.shape
    di = w_out.shape[0]
    nb = m // t_seq
    tm = _tile(t_seq, 1024)
    per = t_seq // tm

    def body(dxn_ref, yo_ref, mod_ref, w_ref, dy_ref, dgate_ref, o_ref):
        i = pl.program_id(0)
        dv = dxn_ref[...]
        dy = (mod_ref[0, 2:3, :] * dv).astype(BF16)
        dy_ref[...] = dy
        o_ref[...] = _dot_nt(dy, w_ref[...])

        @pl.when(i % per == 0)
        def _():
            dgate_ref[...] = jnp.zeros_like(dgate_ref)

        dgate_ref[0] += jnp.sum(dv * yo_ref[...], axis=0, keepdims=True)

    row = pl.BlockSpec((tm, d), lambda i: (i, 0))
    return _pc(body, name=name,
               out_shape=[jax.ShapeDtypeStruct((m, d), BF16), jax.ShapeDtypeStruct((nb, 1, d), F32),
                          jax.ShapeDtypeStruct((m, di), F32)],
               grid=(m // tm,),
               in_specs=[row, row, pl.BlockSpec((1, 3, d), lambda i: (i // per, 0, 0)),
                         pl.BlockSpec((di, d), lambda i: (0, 0))],
               out_specs=[row, pl.BlockSpec((1, 1, d), lambda i: (i // per, 0, 0)),
                          pl.BlockSpec((tm, di), lambda i: (i, 0))],
               sem=("arbitrary",))(dxn, yout, mod, w_out)


def _mm_dw_out(ybr, dy, name, comm=None):
    m, di = ybr.shape
    d = dy.shape[1]
    tn = _tile(di, 512)

    def body(y_ref, dy_ref, o_ref):
        o_ref[...] = _dot_tn(y_ref[...], dy_ref[...])

    return _pc(body, name=name, out_shape=jax.ShapeDtypeStruct((di, d), F32), grid=(di // tn,),
               in_specs=[pl.BlockSpec((m, tn), lambda n: (0, n)), pl.BlockSpec((m, d), lambda n: (0, 0))],
               out_specs=pl.BlockSpec((tn, d), lambda n: (n, 0)), sem=("parallel",), comm=comm)(ybr, dy)


def _sgu_mask():
    t = lax.broadcasted_iota(jnp.int32, (SG_BLOCK, SG_BLOCK), 0)
    s = lax.broadcasted_iota(jnp.int32, (SG_BLOCK, SG_BLOCK), 1)
    return (s // CHUNK) <= (t // CHUNK)


def _a_mid_fwd(proj, ln_g, ln_b, w_s, bs_t, t_seq, comm=None):
    m, n3 = proj.shape
    di = n3 // 3
    gd = di // SG_GROUPS
    r = _tile(t_seq, 256)
    nblk = r // SG_BLOCK

    def body(p_ref, lg_ref, lb_ref, ws_ref, bs_ref, ybr_ref, s_scr):
        v = _gelu(p_ref[:, di:2 * di])
        mu = jnp.mean(v, axis=-1, keepdims=True)
        vc = v - mu
        rstd = lax.rsqrt(jnp.mean(vc * vc, axis=-1, keepdims=True) + EPS)
        vb = (vc * rstd * lg_ref[...] + lb_ref[...]).astype(BF16)
        mask = _sgu_mask()
        for gi in range(SG_GROUPS):
            ws = jnp.where(mask, ws_ref[gi], 0.0).astype(BF16)
            bcol = bs_ref[:, gi:gi + 1]
            for b in range(nblk):
                rows = slice(b * SG_BLOCK, (b + 1) * SG_BLOCK)
                cols = slice(gi * gd, (gi + 1) * gd)
                s_scr[rows, cols] = _dot(ws, vb[rows, cols]) + bcol
        gg = p_ref[:, 2 * di:]
        ybr_ref[...] = (_gelu(p_ref[:, :di]) * s_scr[...] * (gg * _sigmoid(gg))).astype(BF16)

    vec = pl.BlockSpec((1, di), lambda i: (0, 0))
    return _pc(body, name="a_mid_fwd", out_shape=jax.ShapeDtypeStruct((m, di), BF16), grid=(m // r,),
               in_specs=[pl.BlockSpec((r, n3), lambda i: (i, 0)), vec, vec,
                         pl.BlockSpec((SG_GROUPS, SG_BLOCK, SG_BLOCK), lambda i: (0, 0, 0)),
                         pl.BlockSpec((SG_BLOCK, 128), lambda i: (0, 0))],
               out_specs=pl.BlockSpec((r, di), lambda i: (i, 0)),
               scratch=[pltpu.VMEM((r, di), F32)], sem=("parallel",), comm=comm)(proj, ln_g, ln_b, w_s, bs_t)


def _a_mid_bwd(proj, dybr, ln_g, ln_b, w_s, bs_t, t_seq, comm=None):
    m, n3 = proj.shape
    di = n3 // 3
    gd = di // SG_GROUPS
    r = _tile(t_seq, 256)
    nblk = r // SG_BLOCK

    def body(p_ref, dy_ref, lg_ref, lb_ref, ws_ref, bs_ref,
             dp_ref, dlg_ref, dlb_ref, dws_ref, dbs_ref, s_scr, dvl_scr):
        i = pl.program_id(0)

        @pl.when(i == 0)
        def _():
            dlg_ref[...] = jnp.zeros_like(dlg_ref)
            dlb_ref[...] = jnp.zeros_like(dlb_ref)
            dws_ref[...] = jnp.zeros_like(dws_ref)
            dbs_ref[...] = jnp.zeros_like(dbs_ref)

        v, dgelu_v = _gelu_and_grad(p_ref[:, di:2 * di])
        mu = jnp.mean(v, axis=-1, keepdims=True)
        vc = v - mu
        rstd = lax.rsqrt(jnp.mean(vc * vc, axis=-1, keepdims=True) + EPS)
        vhat = vc * rstd
        lg = lg_ref[...]
        vb = (vhat * lg + lb_ref[...]).astype(BF16)
        u, dgelu_u = _gelu_and_grad(p_ref[:, :di])
        gg = p_ref[:, 2 * di:]
        sg = _sigmoid(gg)
        dyv = dy_ref[...]
        dus = dyv * (gg * sg)
        dsb = (dus * u).astype(BF16)
        ds32 = dus * u
        mask = _sgu_mask()
        lane = lax.broadcasted_iota(jnp.int32, (SG_BLOCK, 128), 1)
        dbs_acc = jnp.zeros((SG_BLOCK, 128), F32)
        for gi in range(SG_GROUPS):
            ws = jnp.where(mask, ws_ref[gi], 0.0).astype(BF16)
            bcol = bs_ref[:, gi:gi + 1]
            cols = slice(gi * gd, (gi + 1) * gd)
            dws_acc = jnp.zeros((SG_BLOCK, SG_BLOCK), F32)
            dbs_col = jnp.zeros((SG_BLOCK, 1), F32)
            for b in range(nblk):
                rows = slice(b * SG_BLOCK, (b + 1) * SG_BLOCK)
                s_scr[rows, cols] = _dot(ws, vb[rows, cols]) + bcol
                dvl_scr[rows, cols] = _dot_tn(ws, dsb[rows, cols])
                dws_acc += _dot_nt(dsb[rows, cols], vb[rows, cols])
                dbs_col += jnp.sum(ds32[rows, cols], axis=-1, keepdims=True)
            dws_ref[gi] += jnp.where(mask, dws_acc, 0.0)
            dbs_acc += jnp.where(lane == gi, dbs_col, 0.0)
        dbs_ref[...] += dbs_acc
        s = s_scr[...]
        dp_ref[:, :di] = (dus * s * dgelu_u).astype(BF16)
        dp_ref[:, 2 * di:] = (dyv * u * s * (sg * (1.0 + gg * (1.0 - sg)))).astype(BF16)
        dvl = dvl_scr[...]
        dlg_ref[...] += jnp.sum(dvl * vhat, axis=0, keepdims=True)
        dlb_ref[...] += jnp.sum(dvl, axis=0, keepdims=True)
        dvh = dvl * lg
        dv = rstd * (dvh - jnp.mean(dvh, axis=-1, keepdims=True)
                     - vhat * jnp.mean(dvh * vhat, axis=-1, keepdims=True))
        dp_ref[:, di:2 * di] = (dv * dgelu_v).astype(BF16)

    vec = pl.BlockSpec((1, di), lambda i: (0, 0))
    wsb = pl.BlockSpec((SG_GROUPS, SG_BLOCK, SG_BLOCK), lambda i: (0, 0, 0))
    bsb = pl.BlockSpec((SG_BLOCK, 128), lambda i: (0, 0))
    return _pc(body, name="a_mid_bwd",
               out_shape=[jax.ShapeDtypeStruct((m, n3), BF16), jax.ShapeDtypeStruct((1, di), F32),
                          jax.ShapeDtypeStruct((1, di), F32),
                          jax.ShapeDtypeStruct((SG_GROUPS, SG_BLOCK, SG_BLOCK), F32),
                          jax.ShapeDtypeStruct((SG_BLOCK, 128), F32)],
               grid=(m // r,),
               in_specs=[pl.BlockSpec((r, n3), lambda i: (i, 0)), pl.BlockSpec((r, di), lambda i: (i, 0)),
                         vec, vec, wsb, bsb],
               out_specs=[pl.BlockSpec((r, n3), lambda i: (i, 0)), vec, vec, wsb, bsb],
               scratch=[pltpu.VMEM((r, di), F32), pltpu.VMEM((r, di), F32)],
               sem=("arbitrary",), comm=comm)(proj, dybr, ln_g, ln_b, w_s, bs_t)


def _chunk_rows(n):
    if isinstance(n, int):
        return pl.ds(n * CHUNK, CHUNK)
    return pl.ds(pl.multiple_of(n * CHUNK, CHUNK), CHUNK)


def _hgrn_dims(t_seq, di):
    tr = _tile(t_seq, 128)
    hc = _tile(di, 2048)
    return tr, hc, hc // HEAD_DIM


def _hgrn_gates(f_ref, lb, a_scr, k_scr, tr):
    sig = _sigmoid(f_ref[...])
    fg = lb + (1.0 - lb) * sig
    k_scr[...] = 1.0 - fg
    logf = jnp.log(fg)
    g = min(CUM_ROWS, tr)
    tri = _tri_mask(g, reverse=False)
    for rg in range(tr // g):
        a_scr[rg * g:(rg + 1) * g, :] = _tri_apply(tri, logf[rg * g:(rg + 1) * g, :])
    return sig, fg


def _hgrn_fwd(proj, lbj, gn, nb, t_seq):
    _, m, di = proj.shape
    tr, hc, hpg = _hgrn_dims(t_seq, di)
    nt, nhg, ncl = t_seq // tr, di // hc, tr // CHUNK
    nheads = di // HEAD_DIM

    def body(p_ref, lb_ref, gn_ref, o_ref, ybr_ref, st_ref, st_scr, a_scr, k_scr):
        q_ref, f_ref, i_ref, g_ref = (p_ref.at[s] for s in range(4))
        t = pl.program_id(2)

        @pl.when(t == 0)
        def _():
            st_scr[...] = jnp.zeros_like(st_scr)

        _hgrn_gates(f_ref, lb_ref[0:1, :], a_scr, k_scr, tr)
        gnv = gn_ref[...]
        rr = lax.broadcasted_iota(jnp.int32, (CHUNK, CHUNK), 0)
        cc = lax.broadcasted_iota(jnp.int32, (CHUNK, CHUNK), 1)
        causal = cc <= rr

        def chunk(n, carry):
            rows = _chunk_rows(n)
            lanes = [slice(hd * HEAD_DIM, (hd + 1) * HEAD_DIM) for hd in range(hpg)]
            hs = []
            for hd, ls in enumerate(lanes):
                h = {}
                ah, kh = a_scr[rows, ls], k_scr[rows, ls]
                qp = q_ref[rows, ls]
                qh = qp * _sigmoid(qp)
                h["vb"] = i_ref[rows, ls].astype(BF16)
                aref, alast = ah[CHUNK // 2 - 1:CHUNK // 2, :], ah[CHUNK - 1:CHUNK, :]
                h["q_in"] = (qh * jnp.exp(ah - aref)).astype(BF16)
                h["k_in"] = (kh * jnp.exp(aref - ah)).astype(BF16)
                h["q_out"] = (qh * jnp.exp(ah)).astype(BF16)
                h["k_out"] = (kh * jnp.exp(alast - ah)).astype(BF16)
                h["dec"] = jnp.exp(alast)
                st = st_scr[hd]
                st_ref[n, hd] = st
                h["st"] = st
                hs.append(h)
            for h in hs:
                h["scores"] = _dot_nt(h["q_in"], h["k_in"])
                h["o_inter"] = _dot_nt(h["q_out"], h["st"].astype(BF16))
                h["st_mm"] = _dot_tn(h["vb"], h["k_out"])
            for h in hs:
                h["o"] = _dot(jnp.where(causal, h["scores"], 0.0).astype(BF16), h["vb"]) + h["o_inter"]
            for hd, (h, ls) in enumerate(zip(hs, lanes)):
                st_scr[hd] = h["st"] * h["dec"] + h["st_mm"]
                o = h["o"]
                o_ref[rows, ls] = o
                rstd = lax.rsqrt(jnp.mean(o * o, axis=-1, keepdims=True) + EPS)
                gg = g_ref[rows, ls]
                ybr_ref[rows, ls] = ((o * rstd * gnv) * (gg * _sigmoid(gg))).astype(BF16)
            return carry

        lax.fori_loop(0, ncl, chunk, 0)

    blk = pl.BlockSpec((tr, hc), lambda hg, b, t: (b * nt + t, hg))
    return _pc(body, name="hgrn_fwd",
               out_shape=[jax.ShapeDtypeStruct((m, di), F32), jax.ShapeDtypeStruct((m, di), BF16),
                          jax.ShapeDtypeStruct((m // CHUNK, nheads, HEAD_DIM, HEAD_DIM), F32)],
               grid=(nhg, nb, nt),
               in_specs=[pl.BlockSpec((4, tr, hc), lambda hg, b, t: (0, b * nt + t, hg)),
                         pl.BlockSpec((2, hc), lambda hg, b, t: (0, hg)),
                         pl.BlockSpec((1, HEAD_DIM), lambda hg, b, t: (0, 0))],
               out_specs=[blk, blk, pl.BlockSpec((ncl, hpg, HEAD_DIM, HEAD_DIM),
                                                 lambda hg, b, t: (b * nt + t, hg, 0, 0))],
               scratch=[pltpu.VMEM((hpg, HEAD_DIM, HEAD_DIM), F32), pltpu.VMEM((tr, hc), F32),
                        pltpu.VMEM((tr, hc), F32)],
               sem=("parallel", "arbitrary", "arbitrary"))(proj, lbj, gn)


def _hgrn_bwd(proj, o_all, dybr, states, lbj, gn, nb, t_seq, comm=None):
    _, m, di = proj.shape
    tr, hc, hpg = _hgrn_dims(t_seq, di)
    nt, nhg, ncl = t_seq // tr, di // hc, tr // CHUNK

    def body(p_ref, o_ref, dy_ref, st_ref, lb_ref, gn_ref,
             dp_ref, dlb_ref, dgn_ref, dst_scr, a_scr, k_scr, da_scr, dk_scr):
        q_ref, f_ref, i_ref, g_ref = (p_ref.at[s] for s in range(4))
        hg, b, t = pl.program_id(0), pl.program_id(1), pl.program_id(2)

        @pl.when(t == 0)
        def _():
            dst_scr[...] = jnp.zeros_like(dst_scr)

        @pl.when((b == 0) & (t == 0))
        def _():
            dlb_ref[...] = jnp.zeros_like(dlb_ref)

        @pl.when((hg == 0) & (b == 0) & (t == 0))
        def _():
            dgn_ref[...] = jnp.zeros_like(dgn_ref)

        lb = lb_ref[0:1, :]
        sig, fg = _hgrn_gates(f_ref, lb, a_scr, k_scr, tr)
        gnv = gn_ref[...]
        rr = lax.broadcasted_iota(jnp.int32, (CHUNK, CHUNK), 0)
        cc = lax.broadcasted_iota(jnp.int32, (CHUNK, CHUNK), 1)
        causal = cc <= rr
        rowi = lax.broadcasted_iota(jnp.int32, (CHUNK, HEAD_DIM), 0)

        def chunk(it, carry):
            n = ncl - 1 - it
            rows = _chunk_rows(n)
            for hd0 in range(0, hpg, PHASE_HEADS):
                heads(n, rows, range(hd0, min(hpg, hd0 + PHASE_HEADS)))
            return carry

        def heads(n, rows, ids):
            lanes = [slice(hd * HEAD_DIM, (hd + 1) * HEAD_DIM) for hd in ids]
            hs = []
            for hd, ls in zip(ids, lanes):
                h = {}
                ah, kh = a_scr[rows, ls], k_scr[rows, ls]
                qp = q_ref[rows, ls]
                sq = _sigmoid(qp)
                qh = qp * sq
                h["dsilu_q"] = sq * (1.0 + qp * (1.0 - sq))
                h["vb"] = i_ref[rows, ls].astype(BF16)
                aref, alast = ah[CHUNK // 2 - 1:CHUNK // 2, :], ah[CHUNK - 1:CHUNK, :]
                h["e1"], h["e2"] = jnp.exp(ah - aref), jnp.exp(aref - ah)
                h["e3"], h["e4"] = jnp.exp(ah), jnp.exp(alast - ah)
                h["dec"] = jnp.exp(alast)
                h["q_in"], h["k_in"], h["q_out"], h["k_out"] = qh * h["e1"], kh * h["e2"], qh * h["e3"], kh * h["e4"]
                for nm in ("q_in", "k_in", "q_out", "k_out"):
                    h[nm + "_b"] = h[nm].astype(BF16)
                o = o_ref[rows, ls]
                rstd = lax.rsqrt(jnp.mean(o * o, axis=-1, keepdims=True) + EPS)
                ohat = o * rstd
                gg = g_ref[rows, ls]
                sg = _sigmoid(gg)
                dyv = dy_ref[rows, ls]
                d_on = dyv * (gg * sg)
                dp_ref[3, rows, ls] = (dyv * (ohat * gnv) * (sg * (1.0 + gg * (1.0 - sg)))).astype(BF16)
                h["dgn"] = jnp.sum(d_on * ohat, axis=0, keepdims=True)
                dohat = d_on * gnv
                do = rstd * (dohat - ohat * jnp.mean(dohat * ohat, axis=-1, keepdims=True))
                h["do_b"] = do.astype(BF16)
                h["st_prev"] = st_ref[n, hd]
                h["dst"] = dst_scr[hd]
                hs.append(h)
            for h in hs:
                dst_b = h["dst"].astype(BF16)
                h["scores"] = _dot_nt(h["q_in_b"], h["k_in_b"])
                h["dscores"] = _dot_nt(h["do_b"], h["vb"])
                h["dv_inter"] = _dot_nt(h["k_out_b"], dst_b)
                h["dq_out"] = _dot(h["do_b"], h["st_prev"].astype(BF16))
                h["dk_out"] = _dot(h["vb"], dst_b)
                h["dst_mm"] = _dot_tn(h["do_b"], h["q_out_b"])
            for h in hs:
                scores = jnp.where(causal, h["scores"], 0.0).astype(BF16)
                dscores = jnp.where(causal, h["dscores"], 0.0).astype(BF16)
                h["dv"] = _dot_tn(scores, h["do_b"]) + h["dv_inter"]
                h["dq_in"] = _dot(dscores, h["k_in_b"])
                h["dk_in"] = _dot_tn(dscores, h["q_in_b"])
            dgn = hs[0]["dgn"]
            for h in hs[1:]:
                dgn = dgn + h["dgn"]
            dgn_ref[...] += dgn
            for hd, h, ls in zip(ids, hs, lanes):
                ddec = jnp.sum(h["dst"] * h["st_prev"], axis=0, keepdims=True)
                dst_scr[hd] = h["dst"] * h["dec"] + h["dst_mm"]
                dp_ref[2, rows, ls] = h["dv"].astype(BF16)
                dq = h["dq_in"] * h["e1"] + h["dq_out"] * h["e3"]
                dp_ref[0, rows, ls] = (dq * h["dsilu_q"]).astype(BF16)
                dk_scr[rows, ls] = h["dk_in"] * h["e2"] + h["dk_out"] * h["e4"]
                t_in = h["dq_in"] * h["q_in"] - h["dk_in"] * h["k_in"]
                t_out = h["dk_out"] * h["k_out"]
                da = t_in + h["dq_out"] * h["q_out"] - t_out
                da_ref_row = -jnp.sum(t_in, axis=0, keepdims=True)
                da_last_row = jnp.sum(t_out, axis=0, keepdims=True) + ddec * h["dec"]
                da = da + jnp.where(rowi == CHUNK // 2 - 1, da_ref_row, 0.0) \
                        + jnp.where(rowi == CHUNK - 1, da_last_row, 0.0)
                da_scr[rows, ls] = da

        if ncl <= 2:
            for it in range(ncl):
                chunk(it, 0)
        else:
            lax.fori_loop(0, ncl, chunk, 0)
        g = min(CUM_ROWS, tr)
        tri = _tri_mask(g, reverse=True)
        for rg in range(tr // g):
            rs = slice(rg * g, (rg + 1) * g)
            dlogf = _tri_apply(tri, da_scr[rs, :])
            df = dlogf / fg[rs, :] - dk_scr[rs, :]
            sgr = sig[rs, :]
            dp_ref[1, rs, :] = (df * (1.0 - lb) * (sgr * (1.0 - sgr))).astype(BF16)
            dlb_ref[...] += jnp.sum(df * (1.0 - sgr), axis=0, keepdims=True) * lb_ref[1:2, :]

    blk = pl.BlockSpec((tr, hc), lambda hg, b, t: (b * nt + (nt - 1 - t), hg))
    return _pc(body, name="hgrn_bwd",
               out_shape=[jax.ShapeDtypeStruct((4, m, di), BF16), jax.ShapeDtypeStruct((1, di), F32),
                          jax.ShapeDtypeStruct((1, HEAD_DIM), F32)],
               grid=(nhg, nb, nt),
               in_specs=[pl.BlockSpec((4, tr, hc), lambda hg, b, t: (0, b * nt + (nt - 1 - t), hg)), blk, blk,
                         pl.BlockSpec((ncl, hpg, HEAD_DIM, HEAD_DIM),
                                      lambda hg, b, t: (b * nt + (nt - 1 - t), hg, 0, 0)),
                         pl.BlockSpec((2, hc), lambda hg, b, t: (0, hg)),
                         pl.BlockSpec((1, HEAD_DIM), lambda hg, b, t: (0, 0))],
               out_specs=[pl.BlockSpec((4, tr, hc), lambda hg, b, t: (0, b * nt + (nt - 1 - t), hg)),
                          pl.BlockSpec((1, hc), lambda hg, b, t: (0, hg)),
                          pl.BlockSpec((1, HEAD_DIM), lambda hg, b, t: (0, 0))],
               scratch=[pltpu.VMEM((hpg, HEAD_DIM, HEAD_DIM), F32)] + [pltpu.VMEM((tr, hc), F32)] * 4,
               sem=("arbitrary", "arbitrary", "arbitrary"), comm=comm)(
                   proj, o_all, dybr, states, lbj, gn)


def _adamw(parts, w, m, v, name):
    r, c = w.shape
    tr = _tile(r, 256)
    npart = len(parts)
    c1 = 1.0 - ADAM_B1 ** ADAM_STEP
    c2 = 1.0 - ADAM_B2 ** ADAM_STEP

    def body(*refs):
        p_refs = refs[:npart]
        _adamw_math(p_refs, *refs[npart:], c1, c2)

    blk = pl.BlockSpec((tr, c), lambda i: (i, 0))
    return _pc(body, name=name, out_shape=[jax.ShapeDtypeStruct((r, c), F32)] * 4, grid=(r // tr,),
               in_specs=[blk] * (npart + 3), out_specs=[blk] * 4, sem=("parallel",))(*parts, w, m, v)


def _adamw_math(p_refs, w_ref, m_ref, v_ref, g_ref, d_ref, nm_ref, nv_ref, c1, c2):
    g = p_refs[0][...].astype(F32)
    for p in p_refs[1:]:
        g = g + p[...].astype(F32)
    nm = ADAM_B1 * m_ref[...] + (1.0 - ADAM_B1) * g
    nv = ADAM_B2 * v_ref[...] + (1.0 - ADAM_B2) * (g * g)
    g_ref[...] = g
    nm_ref[...] = nm
    nv_ref[...] = nv
    d_ref[...] = -ADAM_LR * ((nm / c1) / (jnp.sqrt(nv / c2) + ADAM_EPS) + ADAM_WD * w_ref[...])


def _adamw_small(gathered, ws, ms, vs, name, sums=()):
    n, ns = len(ws), len(sums)
    c1 = 1.0 - ADAM_B1 ** ADAM_STEP
    c2 = 1.0 - ADAM_B2 ** ADAM_STEP

    def total(ref):
        g = ref[0]
        for part in range(1, ref.shape[0]):
            g = g + ref[part]
        return g

    def body(*refs):
        g_in, w_in, m_in, v_in = refs[:n], refs[n:2 * n], refs[2 * n:3 * n], refs[3 * n:4 * n]
        s_in = refs[4 * n:4 * n + ns]
        outs = refs[4 * n + ns:]
        for k in range(n):
            g = total(g_in[k])
            nm = ADAM_B1 * m_in[k][...] + (1.0 - ADAM_B1) * g
            nv = ADAM_B2 * v_in[k][...] + (1.0 - ADAM_B2) * (g * g)
            outs[4 * k][...] = g
            outs[4 * k + 1][...] = -ADAM_LR * ((nm / c1) / (jnp.sqrt(nv / c2) + ADAM_EPS) + ADAM_WD * w_in[k][...])
            outs[4 * k + 2][...] = nm
            outs[4 * k + 3][...] = nv
        for k in range(ns):
            outs[4 * n + k][...] = total(s_in[k])

    out_shape = [jax.ShapeDtypeStruct(w.shape, F32) for w in ws for _ in range(4)]
    out_shape += [jax.ShapeDtypeStruct(s.shape[1:], F32) for s in sums]
    res = _pc(body, name=name, out_shape=out_shape)(*gathered, *ws, *ms, *vs, *sums)
    return [res[4 * k:4 * k + 4] for k in range(n)] + list(res[4 * n:])


def _adamw_blocks(parts, idx, w, m, v, name):
    r, c = w.shape
    tr = _tile(r, 256)
    npart = len(parts)
    c1 = 1.0 - ADAM_B1 ** ADAM_STEP
    c2 = 1.0 - ADAM_B2 ** ADAM_STEP

    def body(idx_ref, *refs):
        _adamw_math(refs[:npart], *refs[npart:], c1, c2)

    def sel(p):
        return pl.BlockSpec((None, tr, c), lambda i, s: (s[p], i, 0))

    blk = pl.BlockSpec((tr, c), lambda i, s: (i, 0))
    gs = pltpu.PrefetchScalarGridSpec(num_scalar_prefetch=1, grid=(r // tr,),
                                      in_specs=[sel(p) for p in range(npart)] + [blk] * 3, out_specs=[blk] * 4)
    return _pc(body, name=name, out_shape=[jax.ShapeDtypeStruct((r, c), F32)] * 4, grid_spec=gs,
               sem=("parallel",))(idx, *parts, w, m, v)


_EARLY = ["a_ln_gain", "a_ln_bias", "a_w_s", "a_b_s", "b_lower_bounds", "b_gn_gain"]


def kernel(x, c, norm_gain, w_ada, b_ada, a_w_in, a_ln_gain, a_ln_bias, a_w_s, a_b_s, a_w_out, b_w_in, b_lower_bounds, b_gn_gain, b_w_out, final_gain, loss_target, m_norm_gain, m_w_ada, m_b_ada, m_a_w_in, m_a_ln_gain, m_a_ln_bias, m_a_w_s, m_a_b_s, m_a_w_out, m_b_w_in, m_b_lower_bounds, m_b_gn_gain, m_b_w_out, m_final_gain, v_norm_gain, v_w_ada, v_b_ada, v_a_w_in, v_a_ln_gain, v_a_ln_bias, v_a_w_s, v_a_b_s, v_a_w_out, v_b_w_in, v_b_lower_bounds, v_b_gn_gain, v_b_w_out, v_final_gain):
    w = dict(norm_gain=norm_gain, w_ada=w_ada, b_ada=b_ada, a_w_in=a_w_in, a_ln_gain=a_ln_gain,
             a_ln_bias=a_ln_bias, a_w_s=a_w_s, a_b_s=a_b_s, a_w_out=a_w_out, b_w_in=b_w_in,
             b_lower_bounds=b_lower_bounds, b_gn_gain=b_gn_gain, b_w_out=b_w_out, final_gain=final_gain)
    mo = dict(norm_gain=m_norm_gain, w_ada=m_w_ada, b_ada=m_b_ada, a_w_in=m_a_w_in, a_ln_gain=m_a_ln_gain,
              a_ln_bias=m_a_ln_bias, a_w_s=m_a_w_s, a_b_s=m_a_b_s, a_w_out=m_a_w_out, b_w_in=m_b_w_in,
              b_lower_bounds=m_b_lower_bounds, b_gn_gain=m_b_gn_gain, b_w_out=m_b_w_out, final_gain=m_final_gain)
    vo = dict(norm_gain=v_norm_gain, w_ada=v_w_ada, b_ada=v_b_ada, a_w_in=v_a_w_in, a_ln_gain=v_a_ln_gain,
              a_ln_bias=v_a_ln_bias, a_w_s=v_a_w_s, a_b_s=v_a_b_s, a_w_out=v_a_w_out, b_w_in=v_b_w_in,
              b_lower_bounds=v_b_lower_bounds, b_gn_gain=v_b_gn_gain, b_w_out=v_b_w_out, final_gain=v_final_gain)

    nb, t_seq, d = x.shape
    m = nb * t_seq
    ncol_ada = w_ada.shape[2]
    xi, yi, ci = lax.axis_index("x"), lax.axis_index("y"), lax.axis_index("c")
    me = 4 * xi + 2 * yi + ci

    c_g, wa_in_g = _all_gather([c, a_w_in[0].astype(BF16)], "gather_c_wa")

    c_all = c_g.reshape(NDEV * nb, d)
    b_cols = lax.dynamic_slice(b_ada, (0, me * ncol_ada), (2, ncol_ada)).reshape(2, 1, ncol_ada)
    mod_part, lbj = _ada_fwd(c_all, w_ada, b_cols, b_lower_bounds)
    mod_all = _all_gather([mod_part], "gather_mod")[0]
    mod_mine = lax.dynamic_slice_in_dim(mod_all, me * nb, nb, axis=2)
    mod_mine = mod_mine.transpose(1, 2, 0, 3).reshape(2, nb, 3, d)
    mod0, mod1 = mod_mine[0], mod_mine[1]

    di = a_w_out.shape[1] * NDEV

    xf = x.reshape(m, d)
    tgt = loss_target.reshape(m, d)
    ng0, ng1 = norm_gain[0:1], norm_gain[1:2]
    ncb = b_w_in.shape[2]
    wb_lo, wb_hi = b_w_in[0][:, :ncb // 2].astype(BF16), b_w_in[0][:, ncb // 2:].astype(BF16)
    h0, h0_t = _prenorm(xf, ng0, mod0, t_seq, "prenorm_a")
    proj_a, half = _mm_in(h0, [wa_in_g], 1, "in_proj_a", comm=_gather_first([a_w_out[0].astype(BF16), wb_lo]))
    bs_t = jnp.pad(a_b_s[0].T, ((0, 0), (0, 128 - SG_GROUPS)))
    ybr_a, (wa_out_g, wb_lo_g, wb_hi_half) = _a_mid_fwd(
        proj_a, a_ln_gain, a_ln_bias, a_w_s[0], bs_t, t_seq, comm=_join(_gather_second(half), _gather_first([wb_hi])))
    wa_out = wa_out_g.reshape(di, d)
    (yout_a, x1), (wb_hi_g, wb_out_half) = _out_proj(
        ybr_a, wa_out, xf, mod0, t_seq, "out_proj_a",
        comm=_join(_gather_second([wb_hi_half]), _gather_first([b_w_out[0].astype(BF16)])))
    wb_in_g = [wb_lo_g, wb_hi_g]
    h1, h1_t = _prenorm(x1, ng1, mod1, t_seq, "prenorm_b")
    proj_b, (wb_out_g,) = _mm_in(h1, wb_in_g, 4, "in_proj_b", comm=_gather_second([wb_out_half]))
    wb_out = wb_out_g.reshape(di, d)
    o_b, ybr_b, states = _hgrn_fwd(proj_b, lbj, b_gn_gain, nb, t_seq)
    yout_b, dx2, loss_part, d_final_gain = _out_proj_loss(ybr_b, wb_out, x1, mod1, final_gain.reshape(1, d), tgt, t_seq)

    rows_out = a_w_out.shape[1]
    dy_b, dgate1, dybr_b = _gate_dybr(dx2, yout_b, mod1, wb_out, t_seq, "dybr_b")
    rs_wb_out = _ReduceScatter(_mm_dw_out(ybr_b, dy_b, "dw_out_b").reshape(NDEV, rows_out, d), "b_w_out")
    (dproj_b, d_lb, d_gn), got = _hgrn_bwd(proj_b, o_b, dybr_b, states, lbj, b_gn_gain, nb, t_seq,
                                           comm=rs_wb_out.swap_core())
    rs_wb_out.after_core(got[0])
    dh1, got = _mm_din(dproj_b, wb_in_g, 4, "dh_b", comm=rs_wb_out.swap_chips())
    rs_wb_out.after_chips(got[0])
    dx1, dss1, dgain1 = _prenorm_bwd(dh1, x1, ng1, mod1, dx2, t_seq, "prenorm_bwd_b")
    rs_wb_in = _ReduceScatter(_mm_dw_in(h1_t, dproj_b, ncb, 4, "dw_in_b"), "b_w_in")

    dy_a, dgate0, dybr_a = _gate_dybr(dx1, yout_a, mod0, wa_out, t_seq, "dybr_a")
    g_wa_out, got = _mm_dw_out(ybr_a, dy_a, "dw_out_a", comm=rs_wb_in.swap_core())
    rs_wb_in.after_core(got[0])
    rs_wa_out = _ReduceScatter(g_wa_out.reshape(NDEV, rows_out, d), "a_w_out")
    (dproj_a, d_lng, d_lnb, d_ws, d_bs_t), got = _a_mid_bwd(
        proj_a, dybr_a, a_ln_gain, a_ln_bias, a_w_s[0], bs_t, t_seq,
        comm=_join(rs_wb_in.swap_chips(), rs_wa_out.swap_core()))
    rs_wb_in.after_chips(got[0])
    rs_wa_out.after_core(got[1])
    early_parts = [d_lng, d_lnb, d_ws.reshape(SG_GROUPS * SG_BLOCK, SG_BLOCK), d_bs_t[:, :SG_GROUPS].T,
                   jnp.concatenate([-d_lb, d_lb], axis=0), d_gn]
    g_wa_in, got = _mm_dw_in(h0_t, dproj_a, wa_in_g.shape[2], 1, "dw_in_a",
                             comm=_join(rs_wa_out.swap_chips(), _gather_first(early_parts)))
    rs_wa_out.after_chips(got[0])
    rs_wa_in = _ReduceScatter(g_wa_in, "a_w_in")
    n_tiles = m // _din_tile(m)
    assert n_tiles >= 2
    first_tiles = max(1, (3 * n_tiles) // 8)
    dh0, got2 = _mm_din(dproj_a, [wa_in_g], 1, "dh_a_first", tiles=(0, first_tiles),
                        comm=_join(rs_wa_in.swap_core(), _gather_second(got[1:])))
    rs_wa_in.after_core(got2[0])
    early_all = got2[1:]
    dh0, got = _mm_din(dproj_a, [wa_in_g], 1, "dh_a_rest", comm=rs_wa_in.swap_chips(),
                       tiles=(first_tiles, n_tiles - first_tiles), prev=dh0)
    rs_wa_in.after_chips(got[0])
    dx0, dss0, dgain0 = _prenorm_bwd(dh0, xf, ng0, mod0, dx1, t_seq, "prenorm_bwd_a")
    grad_x = dx0.reshape(nb, t_seq, d)

    dmod = jnp.stack([jnp.concatenate([dss0, dgate0], axis=1), jnp.concatenate([dss1, dgate1], axis=1)])
    dmod_all, dgain_all, dfinal_all, loss_all = _all_gather(
        [dmod.reshape(2, nb, 3 * d), jnp.concatenate([dgain0, dgain1], axis=0), d_final_gain,
         jnp.broadcast_to(loss_part, (1, 128))], "gather_tail")
    dmod_all = dmod_all.transpose(1, 0, 2, 3).reshape(2, NDEV * nb, 3 * d)
    dmod_cols = lax.dynamic_slice_in_dim(dmod_all, me * ncol_ada, ncol_ada, axis=2)
    g_w_ada, g_b_ada = _ada_bwd(c_all, dmod_cols, dmod_all)

    def small2d(k, t):
        return t[k].reshape(early_parts[_EARLY.index(k)].shape if k in _EARLY else (-1, t[k].shape[-1]))

    res = {}
    sm = _adamw_small(early_all, *[[small2d(k, t) for k in _EARLY] for t in (w, mo, vo)], "adamw_small_early")
    for k, r in zip(_EARLY, sm):
        res[k] = tuple(z.reshape(w[k].shape) for z in r)
    late = ["norm_gain", "final_gain", "b_ada"]
    sm = _adamw_small([dgain_all, dfinal_all, g_b_ada[None]], *[[small2d(k, t) for k in late] for t in (w, mo, vo)],
                      "adamw_small_late", sums=[loss_all])
    for k, r in zip(late, sm):
        res[k] = tuple(z.reshape(w[k].shape) for z in r)
    loss = sm[3][0, 0]
    sh = w_ada.shape
    ra = _adamw([g_w_ada.reshape(sh[0] * sh[1], sh[2])], w_ada.reshape(sh[0] * sh[1], sh[2]),
                mo["w_ada"].reshape(sh[0] * sh[1], sh[2]), vo["w_ada"].reshape(sh[0] * sh[1], sh[2]), "adamw_w_ada")
    res["w_ada"] = tuple(z.reshape(sh) for z in ra)

    for k, rs in (("b_w_out", rs_wb_out), ("b_w_in", rs_wb_in), ("a_w_out", rs_wa_out), ("a_w_in", rs_wa_in)):
        res[k] = tuple(z[None] for z in _adamw_blocks(rs.parts, rs.idx, w[k][0], mo[k][0], vo[k][0], "adamw_" + k))

    order = ["norm_gain", "w_ada", "b_ada", "a_w_in", "a_ln_gain", "a_ln_bias", "a_w_s", "a_b_s", "a_w_out",
             "b_w_in", "b_lower_bounds", "b_gn_gain", "b_w_out", "final_gain"]
    return (loss, grad_x, *[res[k][0] for k in order], *[res[k][1] for k in order],
            *[res[k][2] for k in order], *[res[k][3] for k in order])
```

```python
import functools
import math

import jax
import jax.numpy as jnp
from jax import lax
from jax.experimental import pallas as pl
from jax.experimental.pallas import tpu as pltpu

F32 = jnp.float32
BF16 = jnp.bfloat16
MESH = pl.DeviceIdType.MESH
NDEV = 8
EPS = 1e-6
CHUNK = 64
SG_BLOCK = 128
SG_GROUPS = 8
HEAD_DIM = 128
CUM_ROWS = 256
PHASE_HEADS = 8
ADAM_LR, ADAM_B1, ADAM_B2, ADAM_EPS, ADAM_WD, ADAM_STEP = 0.001, 0.9, 0.999, 1e-08, 0.01, 10
VMEM_LIMIT = 56 * 1024 * 1024
ANY = pl.BlockSpec(memory_space=pl.ANY)


class _Hosted:
    def __init__(self, arrays, out_shapes, nsem, start, finish, aliases=None):
        self.arrays, self.out_shapes, self.nsem = list(arrays), list(out_shapes), nsem
        self.start, self.finish = start, finish
        self.aliases = dict(aliases or {})


def _join(*comms):
    arrays, outs, aliases, offs, nsem = [], [], {}, [], 0
    for cm in comms:
        offs.append((len(arrays), len(outs), nsem))
        for i, o in cm.aliases.items():
            aliases[len(arrays) + i] = len(outs) + o
        arrays += cm.arrays
        outs += cm.out_shapes
        nsem += cm.nsem

    def run(which):
        def f(ins, outs_, ss, rs, base):
            for cm, (ia, io, isem) in zip(comms, offs):
                getattr(cm, which)(ins[ia:ia + len(cm.arrays)], outs_[io:io + len(cm.out_shapes)], ss, rs, base + isem)
        return f

    return _Hosted(arrays, outs, nsem, run("start"), run("finish"), aliases)


def _pc(body, *, name, out_shape, grid=None, in_specs=None, out_specs=None, scratch=(), sem=None,
        grid_spec=None, comm=None, aliases=None):
    cp = dict(vmem_limit_bytes=VMEM_LIMIT)
    aliases = dict(aliases or {})
    if comm is None:
        if sem is not None:
            cp["dimension_semantics"] = sem
        kw = {"input_output_aliases": aliases}
        if grid_spec is not None:
            kw["grid_spec"] = grid_spec
        else:
            if grid is not None:
                kw["grid"] = grid
            if in_specs is not None:
                kw["in_specs"] = in_specs
            if out_specs is not None:
                kw["out_specs"] = out_specs
            kw["scratch_shapes"] = list(scratch)
        return pl.pallas_call(functools.partial(body), name=name, out_shape=out_shape,
                              compiler_params=pltpu.CompilerParams(**cp), **kw)

    single = not isinstance(out_shape, (list, tuple))
    outs_list = [out_shape] if single else list(out_shape)
    ospecs = [out_specs] if single else list(out_specs)
    n_in, n_out, n_ci, n_co, n_scr = len(in_specs), len(outs_list), len(comm.arrays), len(comm.out_shapes), len(scratch)
    cp["dimension_semantics"] = ("arbitrary",) * len(grid)

    def hosted(*refs):
        cin, hin = refs[:n_in], refs[n_in:n_in + n_ci]
        cout = refs[n_in + n_ci:n_in + n_ci + n_out]
        hout = refs[n_in + n_ci + n_out:n_in + n_ci + n_out + n_co]
        scr = refs[n_in + n_ci + n_out + n_co:n_in + n_ci + n_out + n_co + n_scr]
        ssem, rsem = refs[-2], refs[-1]
        first = functools.reduce(lambda p, q: p & q, [pl.program_id(a) == 0 for a in range(len(grid))])
        last = functools.reduce(lambda p, q: p & q, [pl.program_id(a) == grid[a] - 1 for a in range(len(grid))])

        @pl.when(first)
        def _():
            comm.start(hin, hout, ssem, rsem, 0)

        body(*cin, *cout, *scr)

        @pl.when(last)
        def _():
            comm.finish(hin, hout, ssem, rsem, 0)

    call = pl.pallas_call(
        hosted, name=name, grid=grid, in_specs=list(in_specs) + [ANY] * n_ci, out_specs=ospecs + [ANY] * n_co,
        out_shape=outs_list + comm.out_shapes,
        scratch_shapes=list(scratch) + [pltpu.SemaphoreType.DMA((comm.nsem,)), pltpu.SemaphoreType.DMA((comm.nsem,))],
        input_output_aliases={**aliases, **{n_in + i: n_out + o for i, o in comm.aliases.items()}},
        compiler_params=pltpu.CompilerParams(**cp))

    def run(*args):
        res = call(*args, *comm.arrays)
        comp = res[:n_out]
        return (comp[0] if single else comp), list(res[n_out:])

    return run


def _tile(n, pref):
    return pref if n % pref == 0 else n


def _sigmoid(x):
    return 1.0 / (1.0 + jnp.exp(-x))


def _gelu(x):
    c = math.sqrt(2.0 / math.pi)
    return 0.5 * x * (1.0 + jnp.tanh(c * (x + 0.044715 * (x * x * x))))


def _gelu_and_grad(x):
    c = math.sqrt(2.0 / math.pi)
    x2 = x * x
    t = jnp.tanh(c * (x + 0.044715 * (x2 * x)))
    half = 0.5 * (1.0 + t)
    return x * half, half + (0.5 * x) * (1.0 - t * t) * (c + (3.0 * 0.044715 * c) * x2)


def _dot(a, b):
    return jnp.dot(a, b, preferred_element_type=F32)


def _dot_nt(a, b):
    return lax.dot_general(a, b, (((1,), (1,)), ((), ())), preferred_element_type=F32)


def _dot_tn(a, b):
    return lax.dot_general(a, b, (((0,), (0,)), ((), ())), preferred_element_type=F32)


def _tri_mask(n, reverse):
    r = lax.broadcasted_iota(jnp.int32, (n, n), 0)
    c = lax.broadcasted_iota(jnp.int32, (n, n), 1)
    same = (r // CHUNK) == (c // CHUNK)
    tri = (c >= r) if reverse else (c <= r)
    return jnp.where(same & tri, 1.0, 0.0).astype(BF16)


def _tri_apply(tri, x):
    hi = x.astype(BF16)
    r1 = x - hi.astype(F32)
    mid = r1.astype(BF16)
    lo = (r1 - mid.astype(F32)).astype(BF16)
    return _dot(tri, hi) + (_dot(tri, mid) + _dot(tri, lo))


def _all_gather(arrs, name):
    n = len(arrs)

    def body(*refs):
        ins, outs = refs[:n], refs[n:2 * n]
        send_sems, recv_sems, local_sems = refs[2 * n:]
        x, y, c = lax.axis_index("x"), lax.axis_index("y"), lax.axis_index("c")
        me, sibling = (x, y, c), (x, y, 1 - c)
        near = (x + c - 2 * x * c, y + (1 - c) - 2 * y * (1 - c))
        far = (x + (1 - c) - 2 * x * (1 - c), y + c - 2 * y * c)
        diag = (1 - x, 1 - y)

        def blk(a, p):
            return outs[a].at[4 * p[0] + 2 * p[1] + p[2]]

        def copy(a, k, block, to, src=None):
            return pltpu.make_async_remote_copy(
                src_ref=blk(a, block) if src is None else src, dst_ref=blk(a, block),
                send_sem=send_sems.at[7 * a + k], recv_sem=recv_sems.at[7 * a + k],
                device_id=to, device_id_type=MESH)

        mine = [pltpu.make_async_copy(ins[a], blk(a, me), local_sems.at[a]) for a in range(n)]
        for m in mine:
            m.start()
        sends = []
        for a in range(n):
            sends += [copy(a, 0, me, sibling, src=ins[a]), copy(a, 1, me, (*near, c), src=ins[a]),
                      copy(a, 2, me, (*far, c), src=ins[a])]
        for cp in sends:
            cp.start()
        for a in range(n):
            copy(a, 1, (*near, c), me).wait_recv()
            sends.append(copy(a, 3, (*near, c), (*far, c)))
            sends[-1].start()
        for a in range(n):
            sends.append(copy(a, 4, (*near, c), sibling))
            sends[-1].start()
            copy(a, 2, (*far, c), me).wait_recv()
            sends.append(copy(a, 5, (*far, c), sibling))
            sends[-1].start()
        for a in range(n):
            copy(a, 3, (*diag, c), me).wait_recv()
            sends.append(copy(a, 6, (*diag, c), sibling))
            sends[-1].start()
        for a in range(n):
            copy(a, 0, sibling, me).wait_recv()
            copy(a, 4, (*far, 1 - c), me).wait_recv()
            copy(a, 5, (*near, 1 - c), me).wait_recv()
            copy(a, 6, (*diag, 1 - c), me).wait_recv()
        for cp in sends:
            cp.wait_send()
        for m in mine:
            m.wait()

    out_shape = [jax.ShapeDtypeStruct((NDEV,) + a.shape, a.dtype) for a in arrs]
    return _pc(body, name=name, out_shape=out_shape, in_specs=[ANY] * n, out_specs=[ANY] * n,
               scratch=[pltpu.SemaphoreType.DMA((7 * n,)), pltpu.SemaphoreType.DMA((7 * n,)),
                        pltpu.SemaphoreType.DMA((n,))])(*arrs)


def _gather_first(arrs):
    n = len(arrs)

    def parts(ins, outs, ss, rs, base):
        x, y, c = lax.axis_index("x"), lax.axis_index("y"), lax.axis_index("c")
        me, sibling = (x, y, c), (x, y, 1 - c)
        chips = [(1 - x, y), (x, 1 - y), (1 - x, 1 - y)]

        def blk(a, p):
            return outs[a].at[4 * p[0] + 2 * p[1] + p[2]]

        def copy(a, k, block, to):
            return pltpu.make_async_remote_copy(
                src_ref=ins[a], dst_ref=blk(a, block), send_sem=ss.at[base + 4 * a + k],
                recv_sem=rs.at[base + 4 * a + k], device_id=to, device_id_type=MESH)

        local = [pltpu.make_async_copy(ins[a], blk(a, me), ss.at[base + 4 * n + a]) for a in range(n)]
        sends, recvs = [], []
        for a in range(n):
            sends.append(copy(a, 0, me, sibling))
            recvs.append(copy(a, 0, sibling, me))
            for j, chip in enumerate(chips):
                sends.append(copy(a, 1 + j, me, (*chip, c)))
                recvs.append(copy(a, 1 + j, (*chip, c), me))
        return local, sends, recvs

    def start(ins, outs, ss, rs, base):
        local, sends, _ = parts(ins, outs, ss, rs, base)
        for cp in local + sends:
            cp.start()

    def finish(ins, outs, ss, rs, base):
        local, sends, recvs = parts(ins, outs, ss, rs, base)
        for cp in recvs:
            cp.wait_recv()
        for cp in sends:
            cp.wait_send()
        for cp in local:
            cp.wait()

    return _Hosted(arrs, [jax.ShapeDtypeStruct((NDEV,) + a.shape, a.dtype) for a in arrs], 5 * n, start, finish)


def _gather_second(bufs):
    n = len(bufs)

    def parts(ins, outs, ss, rs, base):
        x, y, c = lax.axis_index("x"), lax.axis_index("y"), lax.axis_index("c")
        sibling = (x, y, 1 - c)
        chips = [(1 - x, y), (x, 1 - y), (1 - x, 1 - y)]
        sends, recvs = [], []
        for a in range(n):
            for j, chip in enumerate(chips):
                mine = 4 * chip[0] + 2 * chip[1] + c
                theirs = 4 * chip[0] + 2 * chip[1] + (1 - c)
                sends.append(pltpu.make_async_remote_copy(
                    src_ref=ins[a].at[mine], dst_ref=outs[a].at[mine], send_sem=ss.at[base + 3 * a + j],
                    recv_sem=rs.at[base + 3 * a + j], device_id=sibling, device_id_type=MESH))
                recvs.append(pltpu.make_async_remote_copy(
                    src_ref=ins[a].at[theirs], dst_ref=outs[a].at[theirs], send_sem=ss.at[base + 3 * a + j],
                    recv_sem=rs.at[base + 3 * a + j], device_id=sibling, device_id_type=MESH))
        return sends, recvs

    def start(ins, outs, ss, rs, base):
        for cp in parts(ins, outs, ss, rs, base)[0]:
            cp.start()

    def finish(ins, outs, ss, rs, base):
        sends, recvs = parts(ins, outs, ss, rs, base)
        for cp in recvs:
            cp.wait_recv()
        for cp in sends:
            cp.wait_send()

    return _Hosted(bufs, [jax.ShapeDtypeStruct(b.shape, b.dtype) for b in bufs], 3 * n, start, finish,
                   aliases={a: a for a in range(n)})


def _swap(src, nblk, ids_fn, partner_fn):
    def copies(ins, outs, ss, rs, base):
        x, y, c = lax.axis_index("x"), lax.axis_index("y"), lax.axis_index("c")
        ids = ids_fn(x, y, c)
        partner = partner_fn(x, y, c)
        return [pltpu.make_async_remote_copy(
            src_ref=ins[0].at[ids[k]], dst_ref=outs[0].at[k], send_sem=ss.at[base + k], recv_sem=rs.at[base + k],
            device_id=partner, device_id_type=MESH) for k in range(nblk)]

    def start(ins, outs, ss, rs, base):
        for cp in copies(ins, outs, ss, rs, base):
            cp.start()

    def finish(ins, outs, ss, rs, base):
        for cp in copies(ins, outs, ss, rs, base):
            cp.wait()

    return _Hosted([src], [jax.ShapeDtypeStruct((nblk,) + src.shape[1:], src.dtype)], nblk, start, finish)


def _swap_chips(send):
    def copies(ins, outs, ss, rs, base):
        x, y, c = lax.axis_index("x"), lax.axis_index("y"), lax.axis_index("c")
        chips = [(1 - x, y), (x, 1 - y), (1 - x, 1 - y)]
        return [pltpu.make_async_remote_copy(
            src_ref=ins[0].at[j], dst_ref=outs[0].at[j], send_sem=ss.at[base + j], recv_sem=rs.at[base + j],
            device_id=(*chip, c), device_id_type=MESH) for j, chip in enumerate(chips)]

    def start(ins, outs, ss, rs, base):
        for cp in copies(ins, outs, ss, rs, base):
            cp.start()

    def finish(ins, outs, ss, rs, base):
        for cp in copies(ins, outs, ss, rs, base):
            cp.wait()

    return _Hosted([send], [jax.ShapeDtypeStruct(send.shape, send.dtype)], 3, start, finish)


def _add_send(a, b, idx, ns, name):
    _, r, c = a.shape
    tr = _tile(r, 256)

    def body(idx_ref, a_ref, b_ref, send_ref):
        send_ref[...] = (a_ref[...] + b_ref[...]).astype(BF16)

    def sel(off):
        return pl.BlockSpec((None, tr, c), lambda k, i, s: (s[off + k], i, 0))

    gs = pltpu.PrefetchScalarGridSpec(num_scalar_prefetch=1, grid=(ns, r // tr), in_specs=[sel(0), sel(ns)],
                                      out_specs=pl.BlockSpec((None, tr, c), lambda k, i, s: (k, i, 0)))
    return _pc(body, name=name, grid_spec=gs, sem=("arbitrary", "arbitrary"),
               out_shape=jax.ShapeDtypeStruct((ns, r, c), BF16))(idx, a, b)


class _ReduceScatter:
    def __init__(self, g, tag):
        self.g, self.tag = g, tag

    def swap_core(self):
        return _swap(self.g, 4, lambda x, y, c: [1 - c, 3 - c, 5 - c, 7 - c], lambda x, y, c: (x, y, 1 - c))

    def after_core(self, recv):
        x, y, c = lax.axis_index("x"), lax.axis_index("y"), lax.axis_index("c")
        chips = [(1 - x, y), (x, 1 - y), (1 - x, 1 - y)]
        idx = jnp.stack([4 * p + 2 * q + c for p, q in chips] + [2 * p + q for p, q in chips]).astype(jnp.int32)
        self.send = _add_send(self.g, recv, idx, 3, "rs_add_" + self.tag)
        self.recv_core = recv
        zero = jnp.zeros((), jnp.int32)
        self.idx = jnp.stack([4 * x + 2 * y + c, 2 * x + y, zero, zero + 1, zero + 2]).astype(jnp.int32)

    def swap_chips(self):
        return _swap_chips(self.send)

    def after_chips(self, recv):
        self.parts = [self.g, self.recv_core, recv, recv, recv]


def _ada_fwd(c_all, w_ada, b_cols, b_lb):
    nl, d, ncol = w_ada.shape
    nseq = c_all.shape[0]
    di = b_lb.shape[1]

    def body(c_ref, w_ref, b_ref, lb_ref, mod_ref, lbj_ref):
        cv = c_ref[...]
        cact = (cv * _sigmoid(cv)).astype(BF16)
        for l in range(nl):
            mod_ref[l] = _dot(cact, w_ref[l].astype(BF16)) + b_ref[l]
        b0, b1 = lb_ref[0:1, :], lb_ref[1:2, :]
        mx = jnp.maximum(b0, b1)
        e0, e1 = jnp.exp(b0 - mx), jnp.exp(b1 - mx)
        s = e0 + e1
        p0, p1 = e0 / s, e1 / s
        lbj_ref[0:1, :] = (p0 + p1) - p0
        lbj_ref[1:2, :] = p0 * p1

    return _pc(body, name="ada_fwd",
               out_shape=[jax.ShapeDtypeStruct((nl, nseq, ncol), F32), jax.ShapeDtypeStruct((2, di), F32)]
               )(c_all, w_ada, b_cols, b_lb)


def _ada_bwd(c_all, dmod_cols, dmod_full):
    nl, nseq, ncol = dmod_cols.shape
    d = c_all.shape[1]
    d3 = dmod_full.shape[2]

    def body(c_ref, dc_ref, df_ref, gw_ref, gb_ref):
        cv = c_ref[...]
        cact = (cv * _sigmoid(cv)).astype(BF16)
        for l in range(nl):
            gw_ref[l] = _dot_tn(cact, dc_ref[l].astype(BF16))
            gb_ref[l:l + 1, :] = jnp.sum(df_ref[l], axis=0, keepdims=True)

    return _pc(body, name="ada_bwd",
               out_shape=[jax.ShapeDtypeStruct((nl, d, ncol), F32), jax.ShapeDtypeStruct((nl, d3), F32)]
               )(c_all, dmod_cols, dmod_full)


def _prenorm(x, gain, mod, t_seq, name):
    m, d = x.shape
    tm = _tile(t_seq, 1024)
    per = t_seq // tm

    def body(x_ref, g_ref, mod_ref, h_ref, ht_ref):
        xv = x_ref[...]
        rstd = lax.rsqrt(jnp.mean(xv * xv, axis=-1, keepdims=True) + EPS)
        r = xv * rstd * g_ref[...]
        h = r * (1.0 + mod_ref[0, 1:2, :]) + mod_ref[0, 0:1, :]
        h_ref[...] = h.astype(BF16)
        ht_ref[...] = h.T.astype(BF16)

    return _pc(body, name=name, out_shape=[jax.ShapeDtypeStruct((m, d), BF16), jax.ShapeDtypeStruct((d, m), BF16)],
               grid=(m // tm,),
               in_specs=[pl.BlockSpec((tm, d), lambda i: (i, 0)), pl.BlockSpec((1, d), lambda i: (0, 0)),
                         pl.BlockSpec((1, 3, d), lambda i: (i // per, 0, 0))],
               out_specs=[pl.BlockSpec((tm, d), lambda i: (i, 0)), pl.BlockSpec((d, tm), lambda i: (0, i))],
               sem=("parallel",))(x, gain, mod)


def _prenorm_bwd(dh, x, gain, mod, dxn, t_seq, name):
    m, d = x.shape
    nb = m // t_seq
    tm = _tile(t_seq, 1024)
    per = t_seq // tm

    def body(dh_ref, x_ref, g_ref, mod_ref, dxn_ref, dx_ref, dss_ref, dg_ref):
        i = pl.program_id(0)
        xv, dhv, g = x_ref[...], dh_ref[...], g_ref[...]
        rstd = lax.rsqrt(jnp.mean(xv * xv, axis=-1, keepdims=True) + EPS)
        xhat = xv * rstd
        dr = dhv * (1.0 + mod_ref[0, 1:2, :])
        dxhat = dr * g
        dx_ref[...] = dxn_ref[...] + rstd * (dxhat - xhat * jnp.mean(dxhat * xhat, axis=-1, keepdims=True))

        @pl.when(i % per == 0)
        def _():
            dss_ref[...] = jnp.zeros_like(dss_ref)

        @pl.when(i == 0)
        def _():
            dg_ref[...] = jnp.zeros_like(dg_ref)

        dss_ref[0, 0:1, :] += jnp.sum(dhv, axis=0, keepdims=True)
        dss_ref[0, 1:2, :] += jnp.sum(dhv * (xhat * g), axis=0, keepdims=True)
        dg_ref[...] += jnp.sum(dr * xhat, axis=0, keepdims=True)

    row = pl.BlockSpec((tm, d), lambda i: (i, 0))
    return _pc(body, name=name,
               out_shape=[jax.ShapeDtypeStruct((m, d), F32), jax.ShapeDtypeStruct((nb, 2, d), F32),
                          jax.ShapeDtypeStruct((1, d), F32)],
               grid=(m // tm,),
               in_specs=[row, row, pl.BlockSpec((1, d), lambda i: (0, 0)),
                         pl.BlockSpec((1, 3, d), lambda i: (i // per, 0, 0)), row],
               out_specs=[row, pl.BlockSpec((1, 2, d), lambda i: (i // per, 0, 0)),
                          pl.BlockSpec((1, d), lambda i: (0, 0))],
               sem=("arbitrary",))(dh, x, gain, mod, dxn)


def _mm_in(h, ws, sections, name, comm=None):
    m, k = h.shape
    nw = len(ws)
    widths = [w.shape[2] for w in ws]
    offs = [sum(widths[:a]) for a in range(nw)]
    nc = sum(widths)
    per = NDEV // sections if sections > 1 else NDEV
    tm = _din_tile(m)
    assert per % 2 == 0

    def body(*refs):
        hv = refs[0][...]
        o_ref = refs[1 + nw]
        for b in range(2):
            for a in range(nw):
                lo = b * nc + offs[a]
                o_ref[:, lo:lo + widths[a]] = _dot(hv, refs[1 + a][b])

    w_specs = [pl.BlockSpec((2, k, wd), lambda j, i: (j, 0, 0)) for wd in widths]
    if sections > 1:
        out_shape = jax.ShapeDtypeStruct((sections, m, per * nc), F32)
        out_spec = pl.BlockSpec((None, tm, 2 * nc), lambda j, i: ((2 * j) // per, i, ((2 * j) % per) // 2))
    else:
        out_shape = jax.ShapeDtypeStruct((m, NDEV * nc), F32)
        out_spec = pl.BlockSpec((tm, 2 * nc), lambda j, i: (i, j))
    return _pc(body, name=name, out_shape=out_shape, grid=(NDEV // 2, m // tm),
               in_specs=[pl.BlockSpec((tm, k), lambda j, i: (i, 0))] + w_specs,
               out_specs=out_spec, sem=("parallel", "parallel"), comm=comm)(h, *ws)


def _din_tile(m):
    return 1024 if m % 1024 == 0 and m >= 2048 else _tile(m, 512)


def _mm_din(dproj, ws, sections, name, comm=None, tiles=None, prev=None):
    nw, k = len(ws), ws[0].shape[1]
    widths = [w.shape[2] for w in ws]
    offs = [sum(widths[:a]) for a in range(nw)]
    nc = sum(widths)
    m = dproj.shape[-2]
    tm = _din_tile(m)
    t0, nt = tiles if tiles is not None else (0, m // tm)
    per = NDEV // sections if sections > 1 else NDEV
    assert per % 2 == 0

    def body(*refs):
        d_ref, o_ref = refs[0], refs[-1]
        j = pl.program_id(1)
        acc = None
        for b in range(2):
            for a in range(nw):
                lo = b * nc + offs[a]
                term = _dot_nt(d_ref[:, lo:lo + widths[a]], refs[1 + a][b])
                acc = term if acc is None else acc + term

        @pl.when(j == 0)
        def _():
            o_ref[...] = acc

        @pl.when(j > 0)
        def _():
            o_ref[...] += acc

    if sections > 1:
        dspec = pl.BlockSpec((None, tm, 2 * nc), lambda i, j: ((2 * j) // per, i + t0, ((2 * j) % per) // 2))
    else:
        dspec = pl.BlockSpec((tm, 2 * nc), lambda i, j: (i + t0, j))
    in_specs = [dspec] + [pl.BlockSpec((2, k, wd), lambda i, j: (j, 0, 0)) for wd in widths]
    args = [dproj, *ws]
    if prev is not None:
        in_specs.append(ANY)
        args.append(prev)
    return _pc(body, name=name, out_shape=jax.ShapeDtypeStruct((m, k), F32), grid=(nt, NDEV // 2), in_specs=in_specs,
               out_specs=pl.BlockSpec((tm, k), lambda i, j: (i + t0, 0)), sem=("parallel", "arbitrary"),
               comm=comm, aliases={1 + nw: 0} if prev is not None else None)(*args)


def _mm_dw_in(ht, dproj, nc, sections, name, comm=None):
    k, m = ht.shape
    per = NDEV // sections if sections > 1 else NDEV

    def body(h_ref, d_ref, o_ref):
        o_ref[...] = _dot(h_ref[...], d_ref[...])

    if sections > 1:
        dspec = pl.BlockSpec((None, m, nc), lambda j: (j // per, 0, j % per))
    else:
        dspec = pl.BlockSpec((m, nc), lambda j: (0, j))
    return _pc(body, name=name, out_shape=jax.ShapeDtypeStruct((NDEV, k, nc), F32), grid=(NDEV,),
               in_specs=[pl.BlockSpec((k, m), lambda j: (0, 0)), dspec],
               out_specs=pl.BlockSpec((None, k, nc), lambda j: (j, 0, 0)),
               sem=("parallel",), comm=comm)(ht, dproj)


def _out_proj(ybr, w_out, x, mod, t_seq, name, comm=None):
    m, di = ybr.shape
    d = w_out.shape[1]
    tm = _tile(t_seq, 512)
    per = t_seq // tm

    def body(y_ref, w_ref, x_ref, mod_ref, yo_ref, xn_ref):
        yo = _dot(y_ref[...], w_ref[...])
        yo_ref[...] = yo
        xn_ref[...] = x_ref[...] + mod_ref[0, 2:3, :] * yo

    row = pl.BlockSpec((tm, d), lambda i: (i, 0))
    return _pc(body, name=name,
               out_shape=[jax.ShapeDtypeStruct((m, d), F32), jax.ShapeDtypeStruct((m, d), F32)],
               grid=(m // tm,),
               in_specs=[pl.BlockSpec((tm, di), lambda i: (i, 0)), pl.BlockSpec((di, d), lambda i: (0, 0)), row,
                         pl.BlockSpec((1, 3, d), lambda i: (i // per, 0, 0))],
               out_specs=[row, row], sem=("parallel",), comm=comm)(ybr, w_out, x, mod)


def _out_proj_loss(ybr, w_out, x, mod, gain, target, t_seq):
    m, di = ybr.shape
    d = w_out.shape[1]
    tm = _tile(t_seq, 512)
    per = t_seq // tm

    def body(y_ref, w_ref, x_ref, mod_ref, g_ref, t_ref, yo_ref, dx_ref, loss_ref, dg_ref):
        i = pl.program_id(0)
        yo = _dot(y_ref[...], w_ref[...])
        yo_ref[...] = yo
        xv = x_ref[...] + mod_ref[0, 2:3, :] * yo
        g = g_ref[...]
        rstd = lax.rsqrt(jnp.mean(xv * xv, axis=-1, keepdims=True) + EPS)
        xhat = xv * rstd
        err = xhat * g - t_ref[...]
        dy = err * (1.0 / d)
        dxhat = dy * g
        dx_ref[...] = rstd * (dxhat - xhat * jnp.mean(dxhat * xhat, axis=-1, keepdims=True))

        @pl.when(i == 0)
        def _():
            loss_ref[...] = jnp.zeros_like(loss_ref)
            dg_ref[...] = jnp.zeros_like(dg_ref)

        loss_ref[...] += 0.5 * jnp.sum(jnp.mean(err * err, axis=-1, keepdims=True), axis=0, keepdims=True)
        dg_ref[...] += jnp.sum(dy * xhat, axis=0, keepdims=True)

    row = pl.BlockSpec((tm, d), lambda i: (i, 0))
    vec = pl.BlockSpec((1, d), lambda i: (0, 0))
    return _pc(body, name="out_proj_loss",
               out_shape=[jax.ShapeDtypeStruct((m, d), F32), jax.ShapeDtypeStruct((m, d), F32),
                          jax.ShapeDtypeStruct((1, 1), F32), jax.ShapeDtypeStruct((1, d), F32)],
               grid=(m // tm,),
               in_specs=[pl.BlockSpec((tm, di), lambda i: (i, 0)), pl.BlockSpec((di, d), lambda i: (0, 0)), row,
                         pl.BlockSpec((1, 3, d), lambda i: (i // per, 0, 0)), vec, row],
               out_specs=[row, row, pl.BlockSpec((1, 1), lambda i: (0, 0)), vec],
               sem=("arbitrary",))(ybr, w_out, x, mod, gain, target)


def _gate_dybr(dxn, yout, mod, w_out, t_seq, name):
    m, d = dxn.shape
    di = w_out.shape[0]
    nb = m // t_seq
    tm = _tile(t_seq, 1024)
    per = t_seq // tm

    def body(dxn_ref, yo_ref, mod_ref, w_ref, dy_ref, dgate_ref, o_ref):
        i = pl.program_id(0)
        dv = dxn_ref[...]
        dy = (mod_ref[0, 2:3, :] * dv).astype(BF16)
        dy_ref[...] = dy
        o_ref[...] = _dot_nt(dy, w_ref[...])

        @pl.when(i % per == 0)
        def _():
            dgate_ref[...] = jnp.zeros_like(dgate_ref)

        dgate_ref[0] += jnp.sum(dv * yo_ref[...], axis=0, keepdims=True)

    row = pl.BlockSpec((tm, d), lambda i: (i, 0))
    return _pc(body, name=name,
               out_shape=[jax.ShapeDtypeStruct((m, d), BF16), jax.ShapeDtypeStruct((nb, 1, d), F32),
                          jax.ShapeDtypeStruct((m, di), F32)],
               grid=(m // tm,),
               in_specs=[row, row, pl.BlockSpec((1, 3, d), lambda i: (i // per, 0, 0)),
                         pl.BlockSpec((di, d), lambda i: (0, 0))],
               out_specs=[row, pl.BlockSpec((1, 1, d), lambda i: (i // per, 0, 0)),
                          pl.BlockSpec((tm, di), lambda i: (i, 0))],
               sem=("arbitrary",))(dxn, yout, mod, w_out)


def _mm_dw_out(ybr, dy, name, comm=None):
    m, di = ybr.shape
    d = dy.shape[1]
    tn = _tile(di, 512)

    def body(y_ref, dy_ref, o_ref):
        o_ref[...] = _dot_tn(y_ref[...], dy_ref[...])

    return _pc(body, name=name, out_shape=jax.ShapeDtypeStruct((di, d), F32), grid=(di // tn,),
               in_specs=[pl.BlockSpec((m, tn), lambda n: (0, n)), pl.BlockSpec((m, d), lambda n: (0, 0))],
               out_specs=pl.BlockSpec((tn, d), lambda n: (n, 0)), sem=("parallel",), comm=comm)(ybr, dy)


def _sgu_mask():
    t = lax.broadcasted_iota(jnp.int32, (SG_BLOCK, SG_BLOCK), 0)
    s = lax.broadcasted_iota(jnp.int32, (SG_BLOCK, SG_BLOCK), 1)
    return (s // CHUNK) <= (t // CHUNK)


def _a_mid_fwd(proj, ln_g, ln_b, w_s, bs_t, t_seq, comm=None):
    m, n3 = proj.shape
    di = n3 // 3
    gd = di // SG_GROUPS
    r = _tile(t_seq, 256)
    nblk = r // SG_BLOCK

    def body(p_ref, lg_ref, lb_ref, ws_ref, bs_ref, ybr_ref, s_scr):
        v = _gelu(p_ref[:, di:2 * di])
        mu = jnp.mean(v, axis=-1, keepdims=True)
        vc = v - mu
        rstd = lax.rsqrt(jnp.mean(vc * vc, axis=-1, keepdims=True) + EPS)
        vb = (vc * rstd * lg_ref[...] + lb_ref[...]).astype(BF16)
        mask = _sgu_mask()
        for gi in range(SG_GROUPS):
            ws = jnp.where(mask, ws_ref[gi], 0.0).astype(BF16)
            bcol = bs_ref[:, gi:gi + 1]
            for b in range(nblk):
                rows = slice(b * SG_BLOCK, (b + 1) * SG_BLOCK)
                cols = slice(gi * gd, (gi + 1) * gd)
                s_scr[rows, cols] = _dot(ws, vb[rows, cols]) + bcol
        gg = p_ref[:, 2 * di:]
        ybr_ref[...] = (_gelu(p_ref[:, :di]) * s_scr[...] * (gg * _sigmoid(gg))).astype(BF16)

    vec = pl.BlockSpec((1, di), lambda i: (0, 0))
    return _pc(body, name="a_mid_fwd", out_shape=jax.ShapeDtypeStruct((m, di), BF16), grid=(m // r,),
               in_specs=[pl.BlockSpec((r, n3), lambda i: (i, 0)), vec, vec,
                         pl.BlockSpec((SG_GROUPS, SG_BLOCK, SG_BLOCK), lambda i: (0, 0, 0)),
                         pl.BlockSpec((SG_BLOCK, 128), lambda i: (0, 0))],
               out_specs=pl.BlockSpec((r, di), lambda i: (i, 0)),
               scratch=[pltpu.VMEM((r, di), F32)], sem=("parallel",), comm=comm)(proj, ln_g, ln_b, w_s, bs_t)


def _a_mid_bwd(proj, dybr, ln_g, ln_b, w_s, bs_t, t_seq, comm=None):
    m, n3 = proj.shape
    di = n3 // 3
    gd = di // SG_GROUPS
    r = _tile(t_seq, 256)
    nblk = r // SG_BLOCK

    def body(p_ref, dy_ref, lg_ref, lb_ref, ws_ref, bs_ref,
             dp_ref, dlg_ref, dlb_ref, dws_ref, dbs_ref, s_scr, dvl_scr):
        i = pl.program_id(0)

        @pl.when(i == 0)
        def _():
            dlg_ref[...] = jnp.zeros_like(dlg_ref)
            dlb_ref[...] = jnp.zeros_like(dlb_ref)
            dws_ref[...] = jnp.zeros_like(dws_ref)
            dbs_ref[...] = jnp.zeros_like(dbs_ref)

        v, dgelu_v = _gelu_and_grad(p_ref[:, di:2 * di])
        mu = jnp.mean(v, axis=-1, keepdims=True)
        vc = v - mu
        rstd = lax.rsqrt(jnp.mean(vc * vc, axis=-1, keepdims=True) + EPS)
        vhat = vc * rstd
        lg = lg_ref[...]
        vb = (vhat * lg + lb_ref[...]).astype(BF16)
        u, dgelu_u = _gelu_and_grad(p_ref[:, :di])
        gg = p_ref[:, 2 * di:]
        sg = _sigmoid(gg)
        dyv = dy_ref[...]
        dus = dyv * (gg * sg)
        dsb = (dus * u).astype(BF16)
        ds32 = dus * u
        mask = _sgu_mask()
        lane = lax.broadcasted_iota(jnp.int32, (SG_BLOCK, 128), 1)
        dbs_acc = jnp.zeros((SG_BLOCK, 128), F32)
        for gi in range(SG_GROUPS):
            ws = jnp.where(mask, ws_ref[gi], 0.0).astype(BF16)
            bcol = bs_ref[:, gi:gi + 1]
            cols = slice(gi * gd, (gi + 1) * gd)
            dws_acc = jnp.zeros((SG_BLOCK, SG_BLOCK), F32)
            dbs_col = jnp.zeros((SG_BLOCK, 1), F32)
            for b in range(nblk):
                rows = slice(b * SG_BLOCK, (b + 1) * SG_BLOCK)
                s_scr[rows, cols] = _dot(ws, vb[rows, cols]) + bcol
                dvl_scr[rows, cols] = _dot_tn(ws, dsb[rows, cols])
                dws_acc += _dot_nt(dsb[rows, cols], vb[rows, cols])
                dbs_col += jnp.sum(ds32[rows, cols], axis=-1, keepdims=True)
            dws_ref[gi] += jnp.where(mask, dws_acc, 0.0)
            dbs_acc += jnp.where(lane == gi, dbs_col, 0.0)
        dbs_ref[...] += dbs_acc
        s = s_scr[...]
        dp_ref[:, :di] = (dus * s * dgelu_u).astype(BF16)
        dp_ref[:, 2 * di:] = (dyv * u * s * (sg * (1.0 + gg * (1.0 - sg)))).astype(BF16)
        dvl = dvl_scr[...]
        dlg_ref[...] += jnp.sum(dvl * vhat, axis=0, keepdims=True)
        dlb_ref[...] += jnp.sum(dvl, axis=0, keepdims=True)
        dvh = dvl * lg
        dv = rstd * (dvh - jnp.mean(dvh, axis=-1, keepdims=True)
                     - vhat * jnp.mean(dvh * vhat, axis=-1, keepdims=True))
        dp_ref[:, di:2 * di] = (dv * dgelu_v).astype(BF16)

    vec = pl.BlockSpec((1, di), lambda i: (0, 0))
    wsb = pl.BlockSpec((SG_GROUPS, SG_BLOCK, SG_BLOCK), lambda i: (0, 0, 0))
    bsb = pl.BlockSpec((SG_BLOCK, 128), lambda i: (0, 0))
    return _pc(body, name="a_mid_bwd",
               out_shape=[jax.ShapeDtypeStruct((m, n3), BF16), jax.ShapeDtypeStruct((1, di), F32),
                          jax.ShapeDtypeStruct((1, di), F32),
                          jax.ShapeDtypeStruct((SG_GROUPS, SG_BLOCK, SG_BLOCK), F32),
                          jax.ShapeDtypeStruct((SG_BLOCK, 128), F32)],
               grid=(m // r,),
               in_specs=[pl.BlockSpec((r, n3), lambda i: (i, 0)), pl.BlockSpec((r, di), lambda i: (i, 0)),
                         vec, vec, wsb, bsb],
               out_specs=[pl.BlockSpec((r, n3), lambda i: (i, 0)), vec, vec, wsb, bsb],
               scratch=[pltpu.VMEM((r, di), F32), pltpu.VMEM((r, di), F32)],
               sem=("arbitrary",), comm=comm)(proj, dybr, ln_g, ln_b, w_s, bs_t)


def _chunk_rows(n):
    if isinstance(n, int):
        return pl.ds(n * CHUNK, CHUNK)
    return pl.ds(pl.multiple_of(n * CHUNK, CHUNK), CHUNK)


def _hgrn_dims(t_seq, di):
    tr = _tile(t_seq, 128)
    hc = _tile(di, 2048)
    return tr, hc, hc // HEAD_DIM


def _hgrn_gates(f_ref, lb, a_scr, k_scr, tr):
    sig = _sigmoid(f_ref[...])
    fg = lb + (1.0 - lb) * sig
    k_scr[...] = 1.0 - fg
    logf = jnp.log(fg)
    g = min(CUM_ROWS, tr)
    tri = _tri_mask(g, reverse=False)
    for rg in range(tr // g):
        a_scr[rg * g:(rg + 1) * g, :] = _tri_apply(tri, logf[rg * g:(rg + 1) * g, :])
    return sig, fg


def _hgrn_fwd(proj, lbj, gn, nb, t_seq):
    _, m, di = proj.shape
    tr, hc, hpg = _hgrn_dims(t_seq, di)
    nt, nhg, ncl = t_seq // tr, di // hc, tr // CHUNK
    nheads = di // HEAD_DIM

    nbuf, nsteps = 3, nhg * nb * nt

    def body(p_hbm, lb_ref, gn_ref, o_ref, ybr_ref, st_ref, st_scr, a_scr, k_scr, buf, sems):
        t = pl.program_id(2)
        step = (pl.program_id(0) * nb + pl.program_id(1)) * nt + t

        def fetch(s):
            rt, hg = s % (nb * nt), s // (nb * nt)
            return pltpu.make_async_copy(p_hbm.at[:, pl.ds(rt * tr, tr), pl.ds(hg * hc, hc)],
                                         buf.at[s % nbuf], sems.at[s % nbuf])

        @pl.when(step == 0)
        def _():
            for s in range(min(nbuf - 1, nsteps)):
                fetch(s).start()

        @pl.when(step + nbuf - 1 < nsteps)
        def _():
            fetch(step + nbuf - 1).start()

        fetch(step).wait()
        p_ref = buf.at[step % nbuf]
        q_ref, f_ref, i_ref, g_ref = (p_ref.at[s] for s in range(4))

        @pl.when(t == 0)
        def _():
            st_scr[...] = jnp.zeros_like(st_scr)

        _hgrn_gates(f_ref, lb_ref[0:1, :], a_scr, k_scr, tr)
        gnv = gn_ref[...]
        rr = lax.broadcasted_iota(jnp.int32, (CHUNK, CHUNK), 0)
        cc = lax.broadcasted_iota(jnp.int32, (CHUNK, CHUNK), 1)
        causal = cc <= rr

        def chunk(n, carry):
            rows = _chunk_rows(n)
            lanes = [slice(hd * HEAD_DIM, (hd + 1) * HEAD_DIM) for hd in range(hpg)]
            hs = []
            for hd, ls in enumerate(lanes):
                h = {}
                ah, kh = a_scr[rows, ls], k_scr[rows, ls]
                qp = q_ref[rows, ls]
                qh = qp * _sigmoid(qp)
                h["vb"] = i_ref[rows, ls].astype(BF16)
                aref, alast = ah[CHUNK // 2 - 1:CHUNK // 2, :], ah[CHUNK - 1:CHUNK, :]
                h["q_in"] = (qh * jnp.exp(ah - aref)).astype(BF16)
                h["k_in"] = (kh * jnp.exp(aref - ah)).astype(BF16)
                h["q_out"] = (qh * jnp.exp(ah)).astype(BF16)
                h["k_out"] = (kh * jnp.exp(alast - ah)).astype(BF16)
                h["dec"] = jnp.exp(alast)
                st = st_scr[hd]
                st_ref[n, hd] = st
                h["st"] = st
                hs.append(h)
            for h in hs:
                h["scores"] = _dot_nt(h["q_in"], h["k_in"])
                h["o_inter"] = _dot_nt(h["q_out"], h["st"].astype(BF16))
                h["st_mm"] = _dot_tn(h["vb"], h["k_out"])
            for h in hs:
                h["o"] = _dot(jnp.where(causal, h["scores"], 0.0).astype(BF16), h["vb"]) + h["o_inter"]
            for hd, (h, ls) in enumerate(zip(hs, lanes)):
                st_scr[hd] = h["st"] * h["dec"] + h["st_mm"]
                o = h["o"]
                o_ref[rows, ls] = o
                rstd = lax.rsqrt(jnp.mean(o * o, axis=-1, keepdims=True) + EPS)
                gg = g_ref[rows, ls]
                ybr_ref[rows, ls] = ((o * rstd * gnv) * (gg * _sigmoid(gg))).astype(BF16)
            return carry

        lax.fori_loop(0, ncl, chunk, 0)

    blk = pl.BlockSpec((tr, hc), lambda hg, b, t: (b * nt + t, hg))
    return _pc(body, name="hgrn_fwd",
               out_shape=[jax.ShapeDtypeStruct((m, di), F32), jax.ShapeDtypeStruct((m, di), BF16),
                          jax.ShapeDtypeStruct((m // CHUNK, nheads, HEAD_DIM, HEAD_DIM), F32)],
               grid=(nhg, nb, nt),
               in_specs=[ANY, pl.BlockSpec((2, hc), lambda hg, b, t: (0, hg)),
                         pl.BlockSpec((1, HEAD_DIM), lambda hg, b, t: (0, 0))],
               out_specs=[blk, blk, pl.BlockSpec((ncl, hpg, HEAD_DIM, HEAD_DIM),
                                                 lambda hg, b, t: (b * nt + t, hg, 0, 0))],
               scratch=[pltpu.VMEM((hpg, HEAD_DIM, HEAD_DIM), F32), pltpu.VMEM((tr, hc), F32),
                        pltpu.VMEM((tr, hc), F32), pltpu.VMEM((nbuf, 4, tr, hc), F32),
                        pltpu.SemaphoreType.DMA((nbuf,))],
               sem=("arbitrary", "arbitrary", "arbitrary"))(proj, lbj, gn)


def _hgrn_bwd(proj, o_all, dybr, states, lbj, gn, nb, t_seq, comm=None):
    _, m, di = proj.shape
    tr, hc, hpg = _hgrn_dims(t_seq, di)
    nt, nhg, ncl = t_seq // tr, di // hc, tr // CHUNK

    def body(p_ref, o_ref, dy_ref, st_ref, lb_ref, gn_ref,
             dp_ref, dlb_ref, dgn_ref, dst_scr, a_scr, k_scr, da_scr, dk_scr):
        q_ref, f_ref, i_ref, g_ref = (p_ref.at[s] for s in range(4))
        hg, b, t = pl.program_id(0), pl.program_id(1), pl.program_id(2)

        @pl.when(t == 0)
        def _():
            dst_scr[...] = jnp.zeros_like(dst_scr)

        @pl.when((b == 0) & (t == 0))
        def _():
            dlb_ref[...] = jnp.zeros_like(dlb_ref)

        @pl.when((hg == 0) & (b == 0) & (t == 0))
        def _():
            dgn_ref[...] = jnp.zeros_like(dgn_ref)

        lb = lb_ref[0:1, :]
        sig, fg = _hgrn_gates(f_ref, lb, a_scr, k_scr, tr)
        gnv = gn_ref[...]
        rr = lax.broadcasted_iota(jnp.int32, (CHUNK, CHUNK), 0)
        cc = lax.broadcasted_iota(jnp.int32, (CHUNK, CHUNK), 1)
        causal = cc <= rr
        rowi = lax.broadcasted_iota(jnp.int32, (CHUNK, HEAD_DIM), 0)

        def chunk(it, carry):
            n = ncl - 1 - it
            rows = _chunk_rows(n)
            for hd0 in range(0, hpg, PHASE_HEADS):
                heads(n, rows, range(hd0, min(hpg, hd0 + PHASE_HEADS)))
            return carry

        def heads(n, rows, ids):
            lanes = [slice(hd * HEAD_DIM, (hd + 1) * HEAD_DIM) for hd in ids]
            hs = []
            for hd, ls in zip(ids, lanes):
                h = {}
                ah, kh = a_scr[rows, ls], k_scr[rows, ls]
                qp = q_ref[rows, ls]
                sq = _sigmoid(qp)
                qh = qp * sq
                h["dsilu_q"] = sq * (1.0 + qp * (1.0 - sq))
                h["vb"] = i_ref[rows, ls].astype(BF16)
                aref, alast = ah[CHUNK // 2 - 1:CHUNK // 2, :], ah[CHUNK - 1:CHUNK, :]
                h["e1"], h["e2"] = jnp.exp(ah - aref), jnp.exp(aref - ah)
                h["e3"], h["e4"] = jnp.exp(ah), jnp.exp(alast - ah)
                h["dec"] = jnp.exp(alast)
                h["q_in"], h["k_in"], h["q_out"], h["k_out"] = qh * h["e1"], kh * h["e2"], qh * h["e3"], kh * h["e4"]
                for nm in ("q_in", "k_in", "q_out", "k_out"):
                    h[nm + "_b"] = h[nm].astype(BF16)
                o = o_ref[rows, ls]
                rstd = lax.rsqrt(jnp.mean(o * o, axis=-1, keepdims=True) + EPS)
                ohat = o * rstd
                gg = g_ref[rows, ls]
                sg = _sigmoid(gg)
                dyv = dy_ref[rows, ls]
                d_on = dyv * (gg * sg)
                dp_ref[3, rows, ls] = (dyv * (ohat * gnv) * (sg * (1.0 + gg * (1.0 - sg)))).astype(BF16)
                h["dgn"] = jnp.sum(d_on * ohat, axis=0, keepdims=True)
                dohat = d_on * gnv
                do = rstd * (dohat - ohat * jnp.mean(dohat * ohat, axis=-1, keepdims=True))
                h["do_b"] = do.astype(BF16)
                h["st_prev"] = st_ref[n, hd]
                h["dst"] = dst_scr[hd]
                hs.append(h)
            for h in hs:
                dst_b = h["dst"].astype(BF16)
                h["scores"] = _dot_nt(h["q_in_b"], h["k_in_b"])
                h["dscores"] = _dot_nt(h["do_b"], h["vb"])
                h["dv_inter"] = _dot_nt(h["k_out_b"], dst_b)
                h["dq_out"] = _dot(h["do_b"], h["st_prev"].astype(BF16))
                h["dk_out"] = _dot(h["vb"], dst_b)
                h["dst_mm"] = _dot_tn(h["do_b"], h["q_out_b"])
            for h in hs:
                scores = jnp.where(causal, h["scores"], 0.0).astype(BF16)
                dscores = jnp.where(causal, h["dscores"], 0.0).astype(BF16)
                h["dv"] = _dot_tn(scores, h["do_b"]) + h["dv_inter"]
                h["dq_in"] = _dot(dscores, h["k_in_b"])
                h["dk_in"] = _dot_tn(dscores, h["q_in_b"])
            dgn = hs[0]["dgn"]
            for h in hs[1:]:
                dgn = dgn + h["dgn"]
            dgn_ref[...] += dgn
            for hd, h, ls in zip(ids, hs, lanes):
                ddec = jnp.sum(h["dst"] * h["st_prev"], axis=0, keepdims=True)
                dst_scr[hd] = h["dst"] * h["dec"] + h["dst_mm"]
                dp_ref[2, rows, ls] = h["dv"].astype(BF16)
                dq = h["dq_in"] * h["e1"] + h["dq_out"] * h["e3"]
                dp_ref[0, rows, ls] = (dq * h["dsilu_q"]).astype(BF16)
                dk_scr[rows, ls] = h["dk_in"] * h["e2"] + h["dk_out"] * h["e4"]
                t_in = h["dq_in"] * h["q_in"] - h["dk_in"] * h["k_in"]
                t_out = h["dk_out"] * h["k_out"]
                da = t_in + h["dq_out"] * h["q_out"] - t_out
                da_ref_row = -jnp.sum(t_in, axis=0, keepdims=True)
                da_last_row = jnp.sum(t_out, axis=0, keepdims=True) + ddec * h["dec"]
                da = da + jnp.where(rowi == CHUNK // 2 - 1, da_ref_row, 0.0) \
                        + jnp.where(rowi == CHUNK - 1, da_last_row, 0.0)
                da_scr[rows, ls] = da

        if ncl <= 2:
            for it in range(ncl):
                chunk(it, 0)
        else:
            lax.fori_loop(0, ncl, chunk, 0)
        g = min(CUM_ROWS, tr)
        tri = _tri_mask(g, reverse=True)
        for rg in range(tr // g):
            rs = slice(rg * g, (rg + 1) * g)
            dlogf = _tri_apply(tri, da_scr[rs, :])
            df = dlogf / fg[rs, :] - dk_scr[rs, :]
            sgr = sig[rs, :]
            dp_ref[1, rs, :] = (df * (1.0 - lb) * (sgr * (1.0 - sgr))).astype(BF16)
            dlb_ref[...] += jnp.sum(df * (1.0 - sgr), axis=0, keepdims=True) * lb_ref[1:2, :]

    blk = pl.BlockSpec((tr, hc), lambda hg, b, t: (b * nt + (nt - 1 - t), hg))
    return _pc(body, name="hgrn_bwd",
               out_shape=[jax.ShapeDtypeStruct((4, m, di), BF16), jax.ShapeDtypeStruct((1, di), F32),
                          jax.ShapeDtypeStruct((1, HEAD_DIM), F32)],
               grid=(nhg, nb, nt),
               in_specs=[pl.BlockSpec((4, tr, hc), lambda hg, b, t: (0, b * nt + (nt - 1 - t), hg)), blk, blk,
                         pl.BlockSpec((ncl, hpg, HEAD_DIM, HEAD_DIM),
                                      lambda hg, b, t: (b * nt + (nt - 1 - t), hg, 0, 0)),
                         pl.BlockSpec((2, hc), lambda hg, b, t: (0, hg)),
                         pl.BlockSpec((1, HEAD_DIM), lambda hg, b, t: (0, 0))],
               out_specs=[pl.BlockSpec((4, tr, hc), lambda hg, b, t: (0, b * nt + (nt - 1 - t), hg)),
                          pl.BlockSpec((1, hc), lambda hg, b, t: (0, hg)),
                          pl.BlockSpec((1, HEAD_DIM), lambda hg, b, t: (0, 0))],
               scratch=[pltpu.VMEM((hpg, HEAD_DIM, HEAD_DIM), F32)] + [pltpu.VMEM((tr, hc), F32)] * 4,
               sem=("arbitrary", "arbitrary", "arbitrary"), comm=comm)(
                   proj, o_all, dybr, states, lbj, gn)


def _adamw(parts, w, m, v, name):
    r, c = w.shape
    tr = _tile(r, 256)
    npart = len(parts)
    c1 = 1.0 - ADAM_B1 ** ADAM_STEP
    c2 = 1.0 - ADAM_B2 ** ADAM_STEP

    def body(*refs):
        p_refs = refs[:npart]
        _adamw_math(p_refs, *refs[npart:], c1, c2)

    blk = pl.BlockSpec((tr, c), lambda i: (i, 0))
    return _pc(body, name=name, out_shape=[jax.ShapeDtypeStruct((r, c), F32)] * 4, grid=(r // tr,),
               in_specs=[blk] * (npart + 3), out_specs=[blk] * 4, sem=("parallel",))(*parts, w, m, v)


def _adamw_math(p_refs, w_ref, m_ref, v_ref, g_ref, d_ref, nm_ref, nv_ref, c1, c2):
    g = p_refs[0][...].astype(F32)
    for p in p_refs[1:]:
        g = g + p[...].astype(F32)
    nm = ADAM_B1 * m_ref[...] + (1.0 - ADAM_B1) * g
    nv = ADAM_B2 * v_ref[...] + (1.0 - ADAM_B2) * (g * g)
    g_ref[...] = g
    nm_ref[...] = nm
    nv_ref[...] = nv
    d_ref[...] = -ADAM_LR * ((nm / c1) / (jnp.sqrt(nv / c2) + ADAM_EPS) + ADAM_WD * w_ref[...])


def _adamw_small(gathered, ws, ms, vs, name, sums=()):
    n, ns = len(ws), len(sums)
    c1 = 1.0 - ADAM_B1 ** ADAM_STEP
    c2 = 1.0 - ADAM_B2 ** ADAM_STEP

    def total(ref):
        g = ref[0]
        for part in range(1, ref.shape[0]):
            g = g + ref[part]
        return g

    def body(*refs):
        g_in, w_in, m_in, v_in = refs[:n], refs[n:2 * n], refs[2 * n:3 * n], refs[3 * n:4 * n]
        s_in = refs[4 * n:4 * n + ns]
        outs = refs[4 * n + ns:]
        for k in range(n):
            g = total(g_in[k])
            nm = ADAM_B1 * m_in[k][...] + (1.0 - ADAM_B1) * g
            nv = ADAM_B2 * v_in[k][...] + (1.0 - ADAM_B2) * (g * g)
            outs[4 * k][...] = g
            outs[4 * k + 1][...] = -ADAM_LR * ((nm / c1) / (jnp.sqrt(nv / c2) + ADAM_EPS) + ADAM_WD * w_in[k][...])
            outs[4 * k + 2][...] = nm
            outs[4 * k + 3][...] = nv
        for k in range(ns):
            outs[4 * n + k][...] = total(s_in[k])

    out_shape = [jax.ShapeDtypeStruct(w.shape, F32) for w in ws for _ in range(4)]
    out_shape += [jax.ShapeDtypeStruct(s.shape[1:], F32) for s in sums]
    res = _pc(body, name=name, out_shape=out_shape)(*gathered, *ws, *ms, *vs, *sums)
    return [res[4 * k:4 * k + 4] for k in range(n)] + list(res[4 * n:])


def _adamw_blocks(parts, idx, w, m, v, name):
    r, c = w.shape
    tr = _tile(r, 256)
    npart = len(parts)
    c1 = 1.0 - ADAM_B1 ** ADAM_STEP
    c2 = 1.0 - ADAM_B2 ** ADAM_STEP

    def body(idx_ref, *refs):
        _adamw_math(refs[:npart], *refs[npart:], c1, c2)

    def sel(p):
        return pl.BlockSpec((None, tr, c), lambda i, s: (s[p], i, 0))

    blk = pl.BlockSpec((tr, c), lambda i, s: (i, 0))
    gs = pltpu.PrefetchScalarGridSpec(num_scalar_prefetch=1, grid=(r // tr,),
                                      in_specs=[sel(p) for p in range(npart)] + [blk] * 3, out_specs=[blk] * 4)
    return _pc(body, name=name, out_shape=[jax.ShapeDtypeStruct((r, c), F32)] * 4, grid_spec=gs,
               sem=("parallel",))(idx, *parts, w, m, v)


_EARLY = ["a_ln_gain", "a_ln_bias", "a_w_s", "a_b_s", "b_lower_bounds", "b_gn_gain"]


def kernel(x, c, norm_gain, w_ada, b_ada, a_w_in, a_ln_gain, a_ln_bias, a_w_s, a_b_s, a_w_out, b_w_in, b_lower_bounds, b_gn_gain, b_w_out, final_gain, loss_target, m_norm_gain, m_w_ada, m_b_ada, m_a_w_in, m_a_ln_gain, m_a_ln_bias, m_a_w_s, m_a_b_s, m_a_w_out, m_b_w_in, m_b_lower_bounds, m_b_gn_gain, m_b_w_out, m_final_gain, v_norm_gain, v_w_ada, v_b_ada, v_a_w_in, v_a_ln_gain, v_a_ln_bias, v_a_w_s, v_a_b_s, v_a_w_out, v_b_w_in, v_b_lower_bounds, v_b_gn_gain, v_b_w_out, v_final_gain):
    w = dict(norm_gain=norm_gain, w_ada=w_ada, b_ada=b_ada, a_w_in=a_w_in, a_ln_gain=a_ln_gain,
             a_ln_bias=a_ln_bias, a_w_s=a_w_s, a_b_s=a_b_s, a_w_out=a_w_out, b_w_in=b_w_in,
             b_lower_bounds=b_lower_bounds, b_gn_gain=b_gn_gain, b_w_out=b_w_out, final_gain=final_gain)
    mo = dict(norm_gain=m_norm_gain, w_ada=m_w_ada, b_ada=m_b_ada, a_w_in=m_a_w_in, a_ln_gain=m_a_ln_gain,
              a_ln_bias=m_a_ln_bias, a_w_s=m_a_w_s, a_b_s=m_a_b_s, a_w_out=m_a_w_out, b_w_in=m_b_w_in,
              b_lower_bounds=m_b_lower_bounds, b_gn_gain=m_b_gn_gain, b_w_out=m_b_w_out, final_gain=m_final_gain)
    vo = dict(norm_gain=v_norm_gain, w_ada=v_w_ada, b_ada=v_b_ada, a_w_in=v_a_w_in, a_ln_gain=v_a_ln_gain,
              a_ln_bias=v_a_ln_bias, a_w_s=v_a_w_s, a_b_s=v_a_b_s, a_w_out=v_a_w_out, b_w_in=v_b_w_in,
              b_lower_bounds=v_b_lower_bounds, b_gn_gain=v_b_gn_gain, b_w_out=v_b_w_out, final_gain=v_final_gain)

    nb, t_seq, d = x.shape
    m = nb * t_seq
    ncol_ada = w_ada.shape[2]
    xi, yi, ci = lax.axis_index("x"), lax.axis_index("y"), lax.axis_index("c")
    me = 4 * xi + 2 * yi + ci

    c_g, wa_in_g = _all_gather([c, a_w_in[0].astype(BF16)], "gather_c_wa")

    c_all = c_g.reshape(NDEV * nb, d)
    b_cols = lax.dynamic_slice(b_ada, (0, me * ncol_ada), (2, ncol_ada)).reshape(2, 1, ncol_ada)
    mod_part, lbj = _ada_fwd(c_all, w_ada, b_cols, b_lower_bounds)
    mod_all = _all_gather([mod_part], "gather_mod")[0]
    mod_mine = lax.dynamic_slice_in_dim(mod_all, me * nb, nb, axis=2)
    mod_mine = mod_mine.transpose(1, 2, 0, 3).reshape(2, nb, 3, d)
    mod0, mod1 = mod_mine[0], mod_mine[1]

    di = a_w_out.shape[1] * NDEV

    xf = x.reshape(m, d)
    tgt = loss_target.reshape(m, d)
    ng0, ng1 = norm_gain[0:1], norm_gain[1:2]
    ncb = b_w_in.shape[2]
    wb_lo, wb_hi = b_w_in[0][:, :ncb // 2].astype(BF16), b_w_in[0][:, ncb // 2:].astype(BF16)
    h0, h0_t = _prenorm(xf, ng0, mod0, t_seq, "prenorm_a")
    proj_a, half = _mm_in(h0, [wa_in_g], 1, "in_proj_a", comm=_gather_first([a_w_out[0].astype(BF16), wb_lo]))
    bs_t = jnp.pad(a_b_s[0].T, ((0, 0), (0, 128 - SG_GROUPS)))
    ybr_a, (wa_out_g, wb_lo_g, wb_hi_half) = _a_mid_fwd(
        proj_a, a_ln_gain, a_ln_bias, a_w_s[0], bs_t, t_seq, comm=_join(_gather_second(half), _gather_first([wb_hi])))
    wa_out = wa_out_g.reshape(di, d)
    (yout_a, x1), (wb_hi_g, wb_out_half) = _out_proj(
        ybr_a, wa_out, xf, mod0, t_seq, "out_proj_a",
        comm=_join(_gather_second([wb_hi_half]), _gather_first([b_w_out[0].astype(BF16)])))
    wb_in_g = [wb_lo_g, wb_hi_g]
    h1, h1_t = _prenorm(x1, ng1, mod1, t_seq, "prenorm_b")
    proj_b, (wb_out_g,) = _mm_in(h1, wb_in_g, 4, "in_proj_b", comm=_gather_second([wb_out_half]))
    wb_out = wb_out_g.reshape(di, d)
    o_b, ybr_b, states = _hgrn_fwd(proj_b, lbj, b_gn_gain, nb, t_seq)
    yout_b, dx2, loss_part, d_final_gain = _out_proj_loss(ybr_b, wb_out, x1, mod1, final_gain.reshape(1, d), tgt, t_seq)

    rows_out = a_w_out.shape[1]
    dy_b, dgate1, dybr_b = _gate_dybr(dx2, yout_b, mod1, wb_out, t_seq, "dybr_b")
    rs_wb_out = _ReduceScatter(_mm_dw_out(ybr_b, dy_b, "dw_out_b").reshape(NDEV, rows_out, d), "b_w_out")
    (dproj_b, d_lb, d_gn), got = _hgrn_bwd(proj_b, o_b, dybr_b, states, lbj, b_gn_gain, nb, t_seq,
                                           comm=rs_wb_out.swap_core())
    rs_wb_out.after_core(got[0])
    dh1, got = _mm_din(dproj_b, wb_in_g, 4, "dh_b", comm=rs_wb_out.swap_chips())
    rs_wb_out.after_chips(got[0])
    dx1, dss1, dgain1 = _prenorm_bwd(dh1, x1, ng1, mod1, dx2, t_seq, "prenorm_bwd_b")
    rs_wb_in = _ReduceScatter(_mm_dw_in(h1_t, dproj_b, ncb, 4, "dw_in_b"), "b_w_in")

    dy_a, dgate0, dybr_a = _gate_dybr(dx1, yout_a, mod0, wa_out, t_seq, "dybr_a")
    g_wa_out, got = _mm_dw_out(ybr_a, dy_a, "dw_out_a", comm=rs_wb_in.swap_core())
    rs_wb_in.after_core(got[0])
    rs_wa_out = _ReduceScatter(g_wa_out.reshape(NDEV, rows_out, d), "a_w_out")
    (dproj_a, d_lng, d_lnb, d_ws, d_bs_t), got = _a_mid_bwd(
        proj_a, dybr_a, a_ln_gain, a_ln_bias, a_w_s[0], bs_t, t_seq,
        comm=_join(rs_wb_in.swap_chips(), rs_wa_out.swap_core()))
    rs_wb_in.after_chips(got[0])
    rs_wa_out.after_core(got[1])
    early_parts = [d_lng, d_lnb, d_ws.reshape(SG_GROUPS * SG_BLOCK, SG_BLOCK), d_bs_t[:, :SG_GROUPS].T,
                   jnp.concatenate([-d_lb, d_lb], axis=0), d_gn]
    g_wa_in, got = _mm_dw_in(h0_t, dproj_a, wa_in_g.shape[2], 1, "dw_in_a",
                             comm=_join(rs_wa_out.swap_chips(), _gather_first(early_parts)))
    rs_wa_out.after_chips(got[0])
    rs_wa_in = _ReduceScatter(g_wa_in, "a_w_in")
    n_tiles = m // _din_tile(m)
    assert n_tiles >= 2
    first_tiles = max(1, (3 * n_tiles) // 8)
    dh0, got2 = _mm_din(dproj_a, [wa_in_g], 1, "dh_a_first", tiles=(0, first_tiles),
                        comm=_join(rs_wa_in.swap_core(), _gather_second(got[1:])))
    rs_wa_in.after_core(got2[0])
    early_all = got2[1:]
    dh0, got = _mm_din(dproj_a, [wa_in_g], 1, "dh_a_rest", comm=rs_wa_in.swap_chips(),
                       tiles=(first_tiles, n_tiles - first_tiles), prev=dh0)
    rs_wa_in.after_chips(got[0])
    dx0, dss0, dgain0 = _prenorm_bwd(dh0, xf, ng0, mod0, dx1, t_seq, "prenorm_bwd_a")
    grad_x = dx0.reshape(nb, t_seq, d)

    dmod = jnp.stack([jnp.concatenate([dss0, dgate0], axis=1), jnp.concatenate([dss1, dgate1], axis=1)])
    dmod_all, dgain_all, dfinal_all, loss_all = _all_gather(
        [dmod.reshape(2, nb, 3 * d), jnp.concatenate([dgain0, dgain1], axis=0), d_final_gain,
         jnp.broadcast_to(loss_part, (1, 128))], "gather_tail")
    dmod_all = dmod_all.transpose(1, 0, 2, 3).reshape(2, NDEV * nb, 3 * d)
    dmod_cols = lax.dynamic_slice_in_dim(dmod_all, me * ncol_ada, ncol_ada, axis=2)
    g_w_ada, g_b_ada = _ada_bwd(c_all, dmod_cols, dmod_all)

    def small2d(k, t):
        return t[k].reshape(early_parts[_EARLY.index(k)].shape if k in _EARLY else (-1, t[k].shape[-1]))

    res = {}
    sm = _adamw_small(early_all, *[[small2d(k, t) for k in _EARLY] for t in (w, mo, vo)], "adamw_small_early")
    for k, r in zip(_EARLY, sm):
        res[k] = tuple(z.reshape(w[k].shape) for z in r)
    late = ["norm_gain", "final_gain", "b_ada"]
    sm = _adamw_small([dgain_all, dfinal_all, g_b_ada[None]], *[[small2d(k, t) for k in late] for t in (w, mo, vo)],
                      "adamw_small_late", sums=[loss_all])
    for k, r in zip(late, sm):
        res[k] = tuple(z.reshape(w[k].shape) for z in r)
    loss = sm[3][0, 0]
    sh = w_ada.shape
    ra = _adamw([g_w_ada.reshape(sh[0] * sh[1], sh[2])], w_ada.reshape(sh[0] * sh[1], sh[2]),
                mo["w_ada"].reshape(sh[0] * sh[1], sh[2]), vo["w_ada"].reshape(sh[0] * sh[1], sh[2]), "adamw_w_ada")
    res["w_ada"] = tuple(z.reshape(sh) for z in ra)

    for k, rs in (("b_w_out", rs_wb_out), ("b_w_in", rs_wb_in), ("a_w_out", rs_wa_out), ("a_w_in", rs_wa_in)):
        res[k] = tuple(z[None] for z in _adamw_blocks(rs.parts, rs.idx, w[k][0], mo[k][0], vo[k][0], "adamw_" + k))

    order = ["norm_gain", "w_ada", "b_ada", "a_w_in", "a_ln_gain", "a_ln_bias", "a_w_s", "a_b_s", "a_w_out",
             "b_w_in", "b_lower_bounds", "b_gn_gain", "b_w_out", "final_gain"]
    return (loss, grad_x, *[res[k][0] for k in order], *[res[k][1] for k in order],
            *[res[k][2] for k in order], *[res[k][3] for k in order])
```

```python
import functools
import math

import jax
import jax.numpy as jnp
from jax import lax
from jax.experimental import pallas as pl
from jax.experimental.pallas import tpu as pltpu

F32 = jnp.float32
BF16 = jnp.bfloat16
MESH = pl.DeviceIdType.MESH
NDEV = 8
EPS = 1e-6
CHUNK = 64
SG_BLOCK = 128
SG_GROUPS = 8
HEAD_DIM = 128
CUM_ROWS = 256
PHASE_HEADS = 8
ADAM_LR, ADAM_B1, ADAM_B2, ADAM_EPS, ADAM_WD, ADAM_STEP = 0.001, 0.9, 0.999, 1e-08, 0.01, 10
VMEM_LIMIT = 56 * 1024 * 1024
ANY = pl.BlockSpec(memory_space=pl.ANY)


class _Hosted:
    def __init__(self, arrays, out_shapes, nsem, start, finish, aliases=None):
        self.arrays, self.out_shapes, self.nsem = list(arrays), list(out_shapes), nsem
        self.start, self.finish = start, finish
        self.aliases = dict(aliases or {})


def _join(*comms):
    arrays, outs, aliases, offs, nsem = [], [], {}, [], 0
    for cm in comms:
        offs.append((len(arrays), len(outs), nsem))
        for i, o in cm.aliases.items():
            aliases[len(arrays) + i] = len(outs) + o
        arrays += cm.arrays
        outs += cm.out_shapes
        nsem += cm.nsem

    def run(which):
        def f(ins, outs_, ss, rs, base):
            for cm, (ia, io, isem) in zip(comms, offs):
                getattr(cm, which)(ins[ia:ia + len(cm.arrays)], outs_[io:io + len(cm.out_shapes)], ss, rs, base + isem)
        return f

    return _Hosted(arrays, outs, nsem, run("start"), run("finish"), aliases)


def _pc(body, *, name, out_shape, grid=None, in_specs=None, out_specs=None, scratch=(), sem=None,
        grid_spec=None, comm=None, aliases=None):
    cp = dict(vmem_limit_bytes=VMEM_LIMIT)
    aliases = dict(aliases or {})
    if comm is None:
        if sem is not None:
            cp["dimension_semantics"] = sem
        kw = {"input_output_aliases": aliases}
        if grid_spec is not None:
            kw["grid_spec"] = grid_spec
        else:
            if grid is not None:
                kw["grid"] = grid
            if in_specs is not None:
                kw["in_specs"] = in_specs
            if out_specs is not None:
                kw["out_specs"] = out_specs
            kw["scratch_shapes"] = list(scratch)
        return pl.pallas_call(functools.partial(body), name=name, out_shape=out_shape,
                              compiler_params=pltpu.CompilerParams(**cp), **kw)

    single = not isinstance(out_shape, (list, tuple))
    outs_list = [out_shape] if single else list(out_shape)
    ospecs = [out_specs] if single else list(out_specs)
    n_in, n_out, n_ci, n_co, n_scr = len(in_specs), len(outs_list), len(comm.arrays), len(comm.out_shapes), len(scratch)
    cp["dimension_semantics"] = ("arbitrary",) * len(grid)

    def hosted(*refs):
        cin, hin = refs[:n_in], refs[n_in:n_in + n_ci]
        cout = refs[n_in + n_ci:n_in + n_ci + n_out]
        hout = refs[n_in + n_ci + n_out:n_in + n_ci + n_out + n_co]
        scr = refs[n_in + n_ci + n_out + n_co:n_in + n_ci + n_out + n_co + n_scr]
        ssem, rsem = refs[-2], refs[-1]
        first = functools.reduce(lambda p, q: p & q, [pl.program_id(a) == 0 for a in range(len(grid))])
        last = functools.reduce(lambda p, q: p & q, [pl.program_id(a) == grid[a] - 1 for a in range(len(grid))])

        @pl.when(first)
        def _():
            comm.start(hin, hout, ssem, rsem, 0)

        body(*cin, *cout, *scr)

        @pl.when(last)
        def _():
            comm.finish(hin, hout, ssem, rsem, 0)

    call = pl.pallas_call(
        hosted, name=name, grid=grid, in_specs=list(in_specs) + [ANY] * n_ci, out_specs=ospecs + [ANY] * n_co,
        out_shape=outs_list + comm.out_shapes,
        scratch_shapes=list(scratch) + [pltpu.SemaphoreType.DMA((comm.nsem,)), pltpu.SemaphoreType.DMA((comm.nsem,))],
        input_output_aliases={**aliases, **{n_in + i: n_out + o for i, o in comm.aliases.items()}},
        compiler_params=pltpu.CompilerParams(**cp))

    def run(*args):
        res = call(*args, *comm.arrays)
        comp = res[:n_out]
        return (comp[0] if single else comp), list(res[n_out:])

    return run


def _tile(n, pref):
    return pref if n % pref == 0 else n


def _sigmoid(x):
    return 1.0 / (1.0 + jnp.exp(-x))


def _gelu(x):
    c = math.sqrt(2.0 / math.pi)
    return 0.5 * x * (1.0 + jnp.tanh(c * (x + 0.044715 * (x * x * x))))


def _gelu_and_grad(x):
    c = math.sqrt(2.0 / math.pi)
    x2 = x * x
    t = jnp.tanh(c * (x + 0.044715 * (x2 * x)))
    half = 0.5 * (1.0 + t)
    return x * half, half + (0.5 * x) * (1.0 - t * t) * (c + (3.0 * 0.044715 * c) * x2)


def _dot(a, b):
    return jnp.dot(a, b, preferred_element_type=F32)


def _dot_nt(a, b):
    return lax.dot_general(a, b, (((1,), (1,)), ((), ())), preferred_element_type=F32)


def _dot_tn(a, b):
    return lax.dot_general(a, b, (((0,), (0,)), ((), ())), preferred_element_type=F32)


def _tri_mask(n, reverse):
    r = lax.broadcasted_iota(jnp.int32, (n, n), 0)
    c = lax.broadcasted_iota(jnp.int32, (n, n), 1)
    same = (r // CHUNK) == (c // CHUNK)
    tri = (c >= r) if reverse else (c <= r)
    return jnp.where(same & tri, 1.0, 0.0).astype(BF16)


def _tri_apply(tri, x):
    hi = x.astype(BF16)
    r1 = x - hi.astype(F32)
    mid = r1.astype(BF16)
    lo = (r1 - mid.astype(F32)).astype(BF16)
    return _dot(tri, hi) + (_dot(tri, mid) + _dot(tri, lo))


def _all_gather(arrs, name):
    n = len(arrs)

    def body(*refs):
        ins, outs = refs[:n], refs[n:2 * n]
        send_sems, recv_sems, local_sems = refs[2 * n:]
        x, y, c = lax.axis_index("x"), lax.axis_index("y"), lax.axis_index("c")
        me, sibling = (x, y, c), (x, y, 1 - c)
        near = (x + c - 2 * x * c, y + (1 - c) - 2 * y * (1 - c))
        far = (x + (1 - c) - 2 * x * (1 - c), y + c - 2 * y * c)
        diag = (1 - x, 1 - y)

        def blk(a, p):
            return outs[a].at[4 * p[0] + 2 * p[1] + p[2]]

        def copy(a, k, block, to, src=None):
            return pltpu.make_async_remote_copy(
                src_ref=blk(a, block) if src is None else src, dst_ref=blk(a, block),
                send_sem=send_sems.at[7 * a + k], recv_sem=recv_sems.at[7 * a + k],
                device_id=to, device_id_type=MESH)

        mine = [pltpu.make_async_copy(ins[a], blk(a, me), local_sems.at[a]) for a in range(n)]
        for m in mine:
            m.start()
        sends = []
        for a in range(n):
            sends += [copy(a, 0, me, sibling, src=ins[a]), copy(a, 1, me, (*near, c), src=ins[a]),
                      copy(a, 2, me, (*far, c), src=ins[a])]
        for cp in sends:
            cp.start()
        for a in range(n):
            copy(a, 1, (*near, c), me).wait_recv()
            sends.append(copy(a, 3, (*near, c), (*far, c)))
            sends[-1].start()
        for a in range(n):
            sends.append(copy(a, 4, (*near, c), sibling))
            sends[-1].start()
            copy(a, 2, (*far, c), me).wait_recv()
            sends.append(copy(a, 5, (*far, c), sibling))
            sends[-1].start()
        for a in range(n):
            copy(a, 3, (*diag, c), me).wait_recv()
            sends.append(copy(a, 6, (*diag, c), sibling))
            sends[-1].start()
        for a in range(n):
            copy(a, 0, sibling, me).wait_recv()
            copy(a, 4, (*far, 1 - c), me).wait_recv()
            copy(a, 5, (*near, 1 - c), me).wait_recv()
            copy(a, 6, (*diag, 1 - c), me).wait_recv()
        for cp in sends:
            cp.wait_send()
        for m in mine:
            m.wait()

    out_shape = [jax.ShapeDtypeStruct((NDEV,) + a.shape, a.dtype) for a in arrs]
    return _pc(body, name=name, out_shape=out_shape, in_specs=[ANY] * n, out_specs=[ANY] * n,
               scratch=[pltpu.SemaphoreType.DMA((7 * n,)), pltpu.SemaphoreType.DMA((7 * n,)),
                        pltpu.SemaphoreType.DMA((n,))])(*arrs)


def _gather_first(arrs):
    n = len(arrs)

    def parts(ins, outs, ss, rs, base):
        x, y, c = lax.axis_index("x"), lax.axis_index("y"), lax.axis_index("c")
        me, sibling = (x, y, c), (x, y, 1 - c)
        chips = [(1 - x, y), (x, 1 - y), (1 - x, 1 - y)]

        def blk(a, p):
            return outs[a].at[4 * p[0] + 2 * p[1] + p[2]]

        def copy(a, k, block, to):
            return pltpu.make_async_remote_copy(
                src_ref=ins[a], dst_ref=blk(a, block), send_sem=ss.at[base + 4 * a + k],
                recv_sem=rs.at[base + 4 * a + k], device_id=to, device_id_type=MESH)

        local = [pltpu.make_async_copy(ins[a], blk(a, me), ss.at[base + 4 * n + a]) for a in range(n)]
        sends, recvs = [], []
        for a in range(n):
            sends.append(copy(a, 0, me, sibling))
            recvs.append(copy(a, 0, sibling, me))
            for j, chip in enumerate(chips):
                sends.append(copy(a, 1 + j, me, (*chip, c)))
                recvs.append(copy(a, 1 + j, (*chip, c), me))
        return local, sends, recvs

    def start(ins, outs, ss, rs, base):
        local, sends, _ = parts(ins, outs, ss, rs, base)
        for cp in local + sends:
            cp.start()

    def finish(ins, outs, ss, rs, base):
        local, sends, recvs = parts(ins, outs, ss, rs, base)
        for cp in recvs:
            cp.wait_recv()
        for cp in sends:
            cp.wait_send()
        for cp in local:
            cp.wait()

    return _Hosted(arrs, [jax.ShapeDtypeStruct((NDEV,) + a.shape, a.dtype) for a in arrs], 5 * n, start, finish)


def _gather_second(bufs):
    n = len(bufs)

    def parts(ins, outs, ss, rs, base):
        x, y, c = lax.axis_index("x"), lax.axis_index("y"), lax.axis_index("c")
        sibling = (x, y, 1 - c)
        chips = [(1 - x, y), (x, 1 - y), (1 - x, 1 - y)]
        sends, recvs = [], []
        for a in range(n):
            for j, chip in enumerate(chips):
                mine = 4 * chip[0] + 2 * chip[1] + c
                theirs = 4 * chip[0] + 2 * chip[1] + (1 - c)
                sends.append(pltpu.make_async_remote_copy(
                    src_ref=ins[a].at[mine], dst_ref=outs[a].at[mine], send_sem=ss.at[base + 3 * a + j],
                    recv_sem=rs.at[base + 3 * a + j], device_id=sibling, device_id_type=MESH))
                recvs.append(pltpu.make_async_remote_copy(
                    src_ref=ins[a].at[theirs], dst_ref=outs[a].at[theirs], send_sem=ss.at[base + 3 * a + j],
                    recv_sem=rs.at[base + 3 * a + j], device_id=sibling, device_id_type=MESH))
        return sends, recvs

    def start(ins, outs, ss, rs, base):
        for cp in parts(ins, outs, ss, rs, base)[0]:
            cp.start()

    def finish(ins, outs, ss, rs, base):
        sends, recvs = parts(ins, outs, ss, rs, base)
        for cp in recvs:
            cp.wait_recv()
        for cp in sends:
            cp.wait_send()

    return _Hosted(bufs, [jax.ShapeDtypeStruct(b.shape, b.dtype) for b in bufs], 3 * n, start, finish,
                   aliases={a: a for a in range(n)})


def _swap(src, nblk, ids_fn, partner_fn):
    def copies(ins, outs, ss, rs, base):
        x, y, c = lax.axis_index("x"), lax.axis_index("y"), lax.axis_index("c")
        ids = ids_fn(x, y, c)
        partner = partner_fn(x, y, c)
        return [pltpu.make_async_remote_copy(
            src_ref=ins[0].at[ids[k]], dst_ref=outs[0].at[k], send_sem=ss.at[base + k], recv_sem=rs.at[base + k],
            device_id=partner, device_id_type=MESH) for k in range(nblk)]

    def start(ins, outs, ss, rs, base):
        for cp in copies(ins, outs, ss, rs, base):
            cp.start()

    def finish(ins, outs, ss, rs, base):
        for cp in copies(ins, outs, ss, rs, base):
            cp.wait()

    return _Hosted([src], [jax.ShapeDtypeStruct((nblk,) + src.shape[1:], src.dtype)], nblk, start, finish)


def _swap_chips(send):
    def copies(ins, outs, ss, rs, base):
        x, y, c = lax.axis_index("x"), lax.axis_index("y"), lax.axis_index("c")
        chips = [(1 - x, y), (x, 1 - y), (1 - x, 1 - y)]
        return [pltpu.make_async_remote_copy(
            src_ref=ins[0].at[j], dst_ref=outs[0].at[j], send_sem=ss.at[base + j], recv_sem=rs.at[base + j],
            device_id=(*chip, c), device_id_type=MESH) for j, chip in enumerate(chips)]

    def start(ins, outs, ss, rs, base):
        for cp in copies(ins, outs, ss, rs, base):
            cp.start()

    def finish(ins, outs, ss, rs, base):
        for cp in copies(ins, outs, ss, rs, base):
            cp.wait()

    return _Hosted([send], [jax.ShapeDtypeStruct(send.shape, send.dtype)], 3, start, finish)


def _add_send(a, b, idx, ns, name):
    _, r, c = a.shape
    tr = _tile(r, 256)

    def body(idx_ref, a_ref, b_ref, send_ref):
        send_ref[...] = (a_ref[...] + b_ref[...]).astype(BF16)

    def sel(off):
        return pl.BlockSpec((None, tr, c), lambda k, i, s: (s[off + k], i, 0))

    gs = pltpu.PrefetchScalarGridSpec(num_scalar_prefetch=1, grid=(ns, r // tr), in_specs=[sel(0), sel(ns)],
                                      out_specs=pl.BlockSpec((None, tr, c), lambda k, i, s: (k, i, 0)))
    return _pc(body, name=name, grid_spec=gs, sem=("arbitrary", "arbitrary"),
               out_shape=jax.ShapeDtypeStruct((ns, r, c), BF16))(idx, a, b)


class _ReduceScatter:
    def __init__(self, g, tag):
        self.g, self.tag = g, tag

    def swap_core(self):
        return _swap(self.g, 4, lambda x, y, c: [1 - c, 3 - c, 5 - c, 7 - c], lambda x, y, c: (x, y, 1 - c))

    def after_core(self, recv):
        x, y, c = lax.axis_index("x"), lax.axis_index("y"), lax.axis_index("c")
        chips = [(1 - x, y), (x, 1 - y), (1 - x, 1 - y)]
        idx = jnp.stack([4 * p + 2 * q + c for p, q in chips] + [2 * p + q for p, q in chips]).astype(jnp.int32)
        self.send = _add_send(self.g, recv, idx, 3, "rs_add_" + self.tag)
        self.recv_core = recv
        zero = jnp.zeros((), jnp.int32)
        self.idx = jnp.stack([4 * x + 2 * y + c, 2 * x + y, zero, zero + 1, zero + 2]).astype(jnp.int32)

    def swap_chips(self):
        return _swap_chips(self.send)

    def after_chips(self, recv):
        self.parts = [self.g, self.recv_core, recv, recv, recv]


def _ada_fwd(c_all, w_ada, b_cols, b_lb):
    nl, d, ncol = w_ada.shape
    nseq = c_all.shape[0]
    di = b_lb.shape[1]

    def body(c_ref, w_ref, b_ref, lb_ref, mod_ref, lbj_ref):
        cv = c_ref[...]
        cact = (cv * _sigmoid(cv)).astype(BF16)
        for l in range(nl):
            mod_ref[l] = _dot(cact, w_ref[l].astype(BF16)) + b_ref[l]
        b0, b1 = lb_ref[0:1, :], lb_ref[1:2, :]
        mx = jnp.maximum(b0, b1)
        e0, e1 = jnp.exp(b0 - mx), jnp.exp(b1 - mx)
        s = e0 + e1
        p0, p1 = e0 / s, e1 / s
        lbj_ref[0:1, :] = (p0 + p1) - p0
        lbj_ref[1:2, :] = p0 * p1

    return _pc(body, name="ada_fwd",
               out_shape=[jax.ShapeDtypeStruct((nl, nseq, ncol), F32), jax.ShapeDtypeStruct((2, di), F32)]
               )(c_all, w_ada, b_cols, b_lb)


def _ada_bwd(c_all, dmod_cols, dmod_full):
    nl, nseq, ncol = dmod_cols.shape
    d = c_all.shape[1]
    d3 = dmod_full.shape[2]

    def body(c_ref, dc_ref, df_ref, gw_ref, gb_ref):
        cv = c_ref[...]
        cact = (cv * _sigmoid(cv)).astype(BF16)
        for l in range(nl):
            gw_ref[l] = _dot_tn(cact, dc_ref[l].astype(BF16))
            gb_ref[l:l + 1, :] = jnp.sum(df_ref[l], axis=0, keepdims=True)

    return _pc(body, name="ada_bwd",
               out_shape=[jax.ShapeDtypeStruct((nl, d, ncol), F32), jax.ShapeDtypeStruct((nl, d3), F32)]
               )(c_all, dmod_cols, dmod_full)


def _prenorm(x, gain, mod, t_seq, name):
    m, d = x.shape
    tm = _tile(t_seq, 1024)
    per = t_seq // tm

    def body(x_ref, g_ref, mod_ref, h_ref, ht_ref):
        xv = x_ref[...]
        rstd = lax.rsqrt(jnp.mean(xv * xv, axis=-1, keepdims=True) + EPS)
        r = xv * rstd * g_ref[...]
        h = r * (1.0 + mod_ref[0, 1:2, :]) + mod_ref[0, 0:1, :]
        h_ref[...] = h.astype(BF16)
        ht_ref[...] = h.T.astype(BF16)

    return _pc(body, name=name, out_shape=[jax.ShapeDtypeStruct((m, d), BF16), jax.ShapeDtypeStruct((d, m), BF16)],
               grid=(m // tm,),
               in_specs=[pl.BlockSpec((tm, d), lambda i: (i, 0)), pl.BlockSpec((1, d), lambda i: (0, 0)),
                         pl.BlockSpec((1, 3, d), lambda i: (i // per, 0, 0))],
               out_specs=[pl.BlockSpec((tm, d), lambda i: (i, 0)), pl.BlockSpec((d, tm), lambda i: (0, i))],
               sem=("parallel",))(x, gain, mod)


def _prenorm_bwd(dh, x, gain, mod, dxn, t_seq, name):
    m, d = x.shape
    nb = m // t_seq
    tm = _tile(t_seq, 1024)
    per = t_seq // tm

    def body(dh_ref, x_ref, g_ref, mod_ref, dxn_ref, dx_ref, dss_ref, dg_ref):
        i = pl.program_id(0)
        xv, dhv, g = x_ref[...], dh_ref[...], g_ref[...]
        rstd = lax.rsqrt(jnp.mean(xv * xv, axis=-1, keepdims=True) + EPS)
        xhat = xv * rstd
        dr = dhv * (1.0 + mod_ref[0, 1:2, :])
        dxhat = dr * g
        dx_ref[...] = dxn_ref[...] + rstd * (dxhat - xhat * jnp.mean(dxhat * xhat, axis=-1, keepdims=True))

        @pl.when(i % per == 0)
        def _():
            dss_ref[...] = jnp.zeros_like(dss_ref)

        @pl.when(i == 0)
        def _():
            dg_ref[...] = jnp.zeros_like(dg_ref)

        dss_ref[0, 0:1, :] += jnp.sum(dhv, axis=0, keepdims=True)
        dss_ref[0, 1:2, :] += jnp.sum(dhv * (xhat * g), axis=0, keepdims=True)
        dg_ref[...] += jnp.sum(dr * xhat, axis=0, keepdims=True)

    row = pl.BlockSpec((tm, d), lambda i: (i, 0))
    return _pc(body, name=name,
               out_shape=[jax.ShapeDtypeStruct((m, d), F32), jax.ShapeDtypeStruct((nb, 2, d), F32),
                          jax.ShapeDtypeStruct((1, d), F32)],
               grid=(m // tm,),
               in_specs=[row, row, pl.BlockSpec((1, d), lambda i: (0, 0)),
                         pl.BlockSpec((1, 3, d), lambda i: (i // per, 0, 0)), row],
               out_specs=[row, pl.BlockSpec((1, 2, d), lambda i: (i // per, 0, 0)),
                          pl.BlockSpec((1, d), lambda i: (0, 0))],
               sem=("arbitrary",))(dh, x, gain, mod, dxn)


def _mm_in(h, ws, sections, name, comm=None):
    m, k = h.shape
    nw = len(ws)
    widths = [w.shape[2] for w in ws]
    offs = [sum(widths[:a]) for a in range(nw)]
    nc = sum(widths)
    per = NDEV // sections if sections > 1 else NDEV
    tm = _din_tile(m)
    assert per % 2 == 0

    def body(*refs):
        hv = refs[0][...]
        o_ref = refs[1 + nw]
        for b in range(2):
            for a in range(nw):
                lo = b * nc + offs[a]
                o_ref[:, lo:lo + widths[a]] = _dot(hv, refs[1 + a][b])

    w_specs = [pl.BlockSpec((2, k, wd), lambda j, i: (j, 0, 0)) for wd in widths]
    if sections > 1:
        out_shape = jax.ShapeDtypeStruct((sections, m, per * nc), F32)
        out_spec = pl.BlockSpec((None, tm, 2 * nc), lambda j, i: ((2 * j) // per, i, ((2 * j) % per) // 2))
    else:
        out_shape = jax.ShapeDtypeStruct((m, NDEV * nc), F32)
        out_spec = pl.BlockSpec((tm, 2 * nc), lambda j, i: (i, j))
    return _pc(body, name=name, out_shape=out_shape, grid=(NDEV // 2, m // tm),
               in_specs=[pl.BlockSpec((tm, k), lambda j, i: (i, 0))] + w_specs,
               out_specs=out_spec, sem=("parallel", "parallel"), comm=comm)(h, *ws)


def _din_tile(m):
    return 1024 if m % 1024 == 0 and m >= 2048 else _tile(m, 512)


def _mm_din(dproj, ws, sections, name, comm=None, tiles=None, prev=None):
    nw, k = len(ws), ws[0].shape[1]
    widths = [w.shape[2] for w in ws]
    offs = [sum(widths[:a]) for a in range(nw)]
    nc = sum(widths)
    m = dproj.shape[-2]
    tm = _din_tile(m)
    t0, nt = tiles if tiles is not None else (0, m // tm)
    per = NDEV // sections if sections > 1 else NDEV
    assert per % 2 == 0

    def body(*refs):
        d_ref, o_ref = refs[0], refs[-1]
        j = pl.program_id(1)
        acc = None
        for b in range(2):
            for a in range(nw):
                lo = b * nc + offs[a]
                term = _dot_nt(d_ref[:, lo:lo + widths[a]], refs[1 + a][b])
                acc = term if acc is None else acc + term

        @pl.when(j == 0)
        def _():
            o_ref[...] = acc

        @pl.when(j > 0)
        def _():
            o_ref[...] += acc

    if sections > 1:
        dspec = pl.BlockSpec((None, tm, 2 * nc), lambda i, j: ((2 * j) // per, i + t0, ((2 * j) % per) // 2))
    else:
        dspec = pl.BlockSpec((tm, 2 * nc), lambda i, j: (i + t0, j))
    in_specs = [dspec] + [pl.BlockSpec((2, k, wd), lambda i, j: (j, 0, 0)) for wd in widths]
    args = [dproj, *ws]
    if prev is not None:
        in_specs.append(ANY)
        args.append(prev)
    return _pc(body, name=name, out_shape=jax.ShapeDtypeStruct((m, k), F32), grid=(nt, NDEV // 2), in_specs=in_specs,
               out_specs=pl.BlockSpec((tm, k), lambda i, j: (i + t0, 0)), sem=("parallel", "arbitrary"),
               comm=comm, aliases={1 + nw: 0} if prev is not None else None)(*args)


def _mm_dw_in(ht, dproj, nc, sections, name, comm=None):
    k, m = ht.shape
    per = NDEV // sections if sections > 1 else NDEV

    def body(h_ref, d_ref, o_ref):
        o_ref[...] = _dot(h_ref[...], d_ref[...])

    if sections > 1:
        dspec = pl.BlockSpec((None, m, nc), lambda j: (j // per, 0, j % per))
    else:
        dspec = pl.BlockSpec((m, nc), lambda j: (0, j))
    return _pc(body, name=name, out_shape=jax.ShapeDtypeStruct((NDEV, k, nc), F32), grid=(NDEV,),
               in_specs=[pl.BlockSpec((k, m), lambda j: (0, 0)), dspec],
               out_specs=pl.BlockSpec((None, k, nc), lambda j: (j, 0, 0)),
               sem=("parallel",), comm=comm)(ht, dproj)


def _out_proj(ybr, w_out, x, mod, t_seq, name, comm=None):
    m, di = ybr.shape
    d = w_out.shape[1]
    tm = _tile(t_seq, 512)
    per = t_seq // tm

    def body(y_ref, w_ref, x_ref, mod_ref, yo_ref, xn_ref):
        yo = _dot(y_ref[...], w_ref[...])
        yo_ref[...] = yo
        xn_ref[...] = x_ref[...] + mod_ref[0, 2:3, :] * yo

    row = pl.BlockSpec((tm, d), lambda i: (i, 0))
    return _pc(body, name=name,
               out_shape=[jax.ShapeDtypeStruct((m, d), F32), jax.ShapeDtypeStruct((m, d), F32)],
               grid=(m // tm,),
               in_specs=[pl.BlockSpec((tm, di), lambda i: (i, 0)), pl.BlockSpec((di, d), lambda i: (0, 0)), row,
                         pl.BlockSpec((1, 3, d), lambda i: (i // per, 0, 0))],
               out_specs=[row, row], sem=("parallel",), comm=comm)(ybr, w_out, x, mod)


def _out_proj_loss(ybr, w_out, x, mod, gain, target, t_seq):
    m, di = ybr.shape
    d = w_out.shape[1]
    tm = _tile(t_seq, 512)
    per = t_seq // tm

    def body(y_ref, w_ref, x_ref, mod_ref, g_ref, t_ref, yo_ref, dx_ref, loss_ref, dg_ref):
        i = pl.program_id(0)
        yo = _dot(y_ref[...], w_ref[...])
        yo_ref[...] = yo
        xv = x_ref[...] + mod_ref[0, 2:3, :] * yo
        g = g_ref[...]
        rstd = lax.rsqrt(jnp.mean(xv * xv, axis=-1, keepdims=True) + EPS)
        xhat = xv * rstd
        err = xhat * g - t_ref[...]
        dy = err * (1.0 / d)
        dxhat = dy * g
        dx_ref[...] = rstd * (dxhat - xhat * jnp.mean(dxhat * xhat, axis=-1, keepdims=True))

        @pl.when(i == 0)
        def _():
            loss_ref[...] = jnp.zeros_like(loss_ref)
            dg_ref[...] = jnp.zeros_like(dg_ref)

        loss_ref[...] += 0.5 * jnp.sum(jnp.mean(err * err, axis=-1, keepdims=True), axis=0, keepdims=True)
        dg_ref[...] += jnp.sum(dy * xhat, axis=0, keepdims=True)

    row = pl.BlockSpec((tm, d), lambda i: (i, 0))
    vec = pl.BlockSpec((1, d), lambda i: (0, 0))
    return _pc(body, name="out_proj_loss",
               out_shape=[jax.ShapeDtypeStruct((m, d), F32), jax.ShapeDtypeStruct((m, d), F32),
                          jax.ShapeDtypeStruct((1, 1), F32), jax.ShapeDtypeStruct((1, d), F32)],
               grid=(m // tm,),
               in_specs=[pl.BlockSpec((tm, di), lambda i: (i, 0)), pl.BlockSpec((di, d), lambda i: (0, 0)), row,
                         pl.BlockSpec((1, 3, d), lambda i: (i // per, 0, 0)), vec, row],
               out_specs=[row, row, pl.BlockSpec((1, 1), lambda i: (0, 0)), vec],
               sem=("arbitrary",))(ybr, w_out, x, mod, gain, target)


def _gate_dybr(dxn, yout, mod, w_out, t_seq, name):
    m, d = dxn.shape
    di = w_out.shape[0]
    nb = m // t_seq
    tm = _tile(t_seq, 1024)
    per = t_seq // tm

    def body(dxn_ref, yo_ref, mod_ref, w_ref, dy_ref, dgate_ref, o_ref):
        i = pl.program_id(0)
        dv = dxn_ref[...]
        dy = (mod_ref[0, 2:3, :] * dv).astype(BF16)
        dy_ref[...] = dy
        o_ref[...] = _dot_nt(dy, w_ref[...])

        @pl.when(i % per == 0)
        def _():
            dgate_ref[...] = jnp.zeros_like(dgate_ref)

        dgate_ref[0] += jnp.sum(dv * yo_ref[...], axis=0, keepdims=True)

    row = pl.BlockSpec((tm, d), lambda i: (i, 0))
    return _pc(body, name=name,
               out_shape=[jax.ShapeDtypeStruct((m, d), BF16), jax.ShapeDtypeStruct((nb, 1, d), F32),
                          jax.ShapeDtypeStruct((m, di), F32)],
               grid=(m // tm,),
               in_specs=[row, row, pl.BlockSpec((1, 3, d), lambda i: (i // per, 0, 0)),
                         pl.BlockSpec((di, d), lambda i: (0, 0))],
               out_specs=[row, pl.BlockSpec((1, 1, d), lambda i: (i // per, 0, 0)),
                          pl.BlockSpec((tm, di), lambda i: (i, 0))],
               sem=("arbitrary",))(dxn, yout, mod, w_out)


def _mm_dw_out(ybr, dy, name, comm=None):
    m, di = ybr.shape
    d = dy.shape[1]
    tn = _tile(di, 512)

    def body(y_ref, dy_ref, o_ref):
        o_ref[...] = _dot_tn(y_ref[...], dy_ref[...])

    return _pc(body, name=name, out_shape=jax.ShapeDtypeStruct((di, d), F32), grid=(di // tn,),
               in_specs=[pl.BlockSpec((m, tn), lambda n: (0, n)), pl.BlockSpec((m, d), lambda n: (0, 0))],
               out_specs=pl.BlockSpec((tn, d), lambda n: (n, 0)), sem=("parallel",), comm=comm)(ybr, dy)


def _sgu_mask():
    t = lax.broadcasted_iota(jnp.int32, (SG_BLOCK, SG_BLOCK), 0)
    s = lax.broadcasted_iota(jnp.int32, (SG_BLOCK, SG_BLOCK), 1)
    return (s // CHUNK) <= (t // CHUNK)


def _a_mid_fwd(proj, ln_g, ln_b, w_s, bs_t, t_seq, comm=None):
    m, n3 = proj.shape
    di = n3 // 3
    gd = di // SG_GROUPS
    r = _tile(t_seq, 256)
    nblk = r // SG_BLOCK

    def body(p_ref, lg_ref, lb_ref, ws_ref, bs_ref, ybr_ref, s_scr):
        v = _gelu(p_ref[:, di:2 * di])
        mu = jnp.mean(v, axis=-1, keepdims=True)
        vc = v - mu
        rstd = lax.rsqrt(jnp.mean(vc * vc, axis=-1, keepdims=True) + EPS)
        vb = (vc * rstd * lg_ref[...] + lb_ref[...]).astype(BF16)
        mask = _sgu_mask()
        for gi in range(SG_GROUPS):
            ws = jnp.where(mask, ws_ref[gi], 0.0).astype(BF16)
            bcol = bs_ref[:, gi:gi + 1]
            for b in range(nblk):
                rows = slice(b * SG_BLOCK, (b + 1) * SG_BLOCK)
                cols = slice(gi * gd, (gi + 1) * gd)
                s_scr[rows, cols] = _dot(ws, vb[rows, cols]) + bcol
        gg = p_ref[:, 2 * di:]
        ybr_ref[...] = (_gelu(p_ref[:, :di]) * s_scr[...] * (gg * _sigmoid(gg))).astype(BF16)

    vec = pl.BlockSpec((1, di), lambda i: (0, 0))
    return _pc(body, name="a_mid_fwd", out_shape=jax.ShapeDtypeStruct((m, di), BF16), grid=(m // r,),
               in_specs=[pl.BlockSpec((r, n3), lambda i: (i, 0)), vec, vec,
                         pl.BlockSpec((SG_GROUPS, SG_BLOCK, SG_BLOCK), lambda i: (0, 0, 0)),
                         pl.BlockSpec((SG_BLOCK, 128), lambda i: (0, 0))],
               out_specs=pl.BlockSpec((r, di), lambda i: (i, 0)),
               scratch=[pltpu.VMEM((r, di), F32)], sem=("parallel",), comm=comm)(proj, ln_g, ln_b, w_s, bs_t)


def _a_mid_bwd(proj, dybr, ln_g, ln_b, w_s, bs_t, t_seq, comm=None):
    m, n3 = proj.shape
    di = n3 // 3
    gd = di // SG_GROUPS
    r = _tile(t_seq, 256)
    nblk = r // SG_BLOCK

    def body(p_ref, dy_ref, lg_ref, lb_ref, ws_ref, bs_ref,
             dp_ref, dlg_ref, dlb_ref, dws_ref, dbs_ref, s_scr, dvl_scr):
        i = pl.program_id(0)

        @pl.when(i == 0)
        def _():
            dlg_ref[...] = jnp.zeros_like(dlg_ref)
            dlb_ref[...] = jnp.zeros_like(dlb_ref)
            dws_ref[...] = jnp.zeros_like(dws_ref)
            dbs_ref[...] = jnp.zeros_like(dbs_ref)

        v, dgelu_v = _gelu_and_grad(p_ref[:, di:2 * di])
        mu = jnp.mean(v, axis=-1, keepdims=True)
        vc = v - mu
        rstd = lax.rsqrt(jnp.mean(vc * vc, axis=-1, keepdims=True) + EPS)
        vhat = vc * rstd
        lg = lg_ref[...]
        vb = (vhat * lg + lb_ref[...]).astype(BF16)
        u, dgelu_u = _gelu_and_grad(p_ref[:, :di])
        gg = p_ref[:, 2 * di:]
        sg = _sigmoid(gg)
        dyv = dy_ref[...]
        dus = dyv * (gg * sg)
        dsb = (dus * u).astype(BF16)
        ds32 = dus * u
        mask = _sgu_mask()
        lane = lax.broadcasted_iota(jnp.int32, (SG_BLOCK, 128), 1)
        dbs_acc = jnp.zeros((SG_BLOCK, 128), F32)
        for gi in range(SG_GROUPS):
            ws = jnp.where(mask, ws_ref[gi], 0.0).astype(BF16)
            bcol = bs_ref[:, gi:gi + 1]
            cols = slice(gi * gd, (gi + 1) * gd)
            dws_acc = jnp.zeros((SG_BLOCK, SG_BLOCK), F32)
            dbs_col = jnp.zeros((SG_BLOCK, 1), F32)
            for b in range(nblk):
                rows = slice(b * SG_BLOCK, (b + 1) * SG_BLOCK)
                s_scr[rows, cols] = _dot(ws, vb[rows, cols]) + bcol
                dvl_scr[rows, cols] = _dot_tn(ws, dsb[rows, cols])
                dws_acc += _dot_nt(dsb[rows, cols], vb[rows, cols])
                dbs_col += jnp.sum(ds32[rows, cols], axis=-1, keepdims=True)
            dws_ref[gi] += jnp.where(mask, dws_acc, 0.0)
            dbs_acc += jnp.where(lane == gi, dbs_col, 0.0)
        dbs_ref[...] += dbs_acc
        s = s_scr[...]
        dp_ref[:, :di] = (dus * s * dgelu_u).astype(BF16)
        dp_ref[:, 2 * di:] = (dyv * u * s * (sg * (1.0 + gg * (1.0 - sg)))).astype(BF16)
        dvl = dvl_scr[...]
        dlg_ref[...] += jnp.sum(dvl * vhat, axis=0, keepdims=True)
        dlb_ref[...] += jnp.sum(dvl, axis=0, keepdims=True)
        dvh = dvl * lg
        dv = rstd * (dvh - jnp.mean(dvh, axis=-1, keepdims=True)
                     - vhat * jnp.mean(dvh * vhat, axis=-1, keepdims=True))
        dp_ref[:, di:2 * di] = (dv * dgelu_v).astype(BF16)

    vec = pl.BlockSpec((1, di), lambda i: (0, 0))
    wsb = pl.BlockSpec((SG_GROUPS, SG_BLOCK, SG_BLOCK), lambda i: (0, 0, 0))
    bsb = pl.BlockSpec((SG_BLOCK, 128), lambda i: (0, 0))
    return _pc(body, name="a_mid_bwd",
               out_shape=[jax.ShapeDtypeStruct((m, n3), BF16), jax.ShapeDtypeStruct((1, di), F32),
                          jax.ShapeDtypeStruct((1, di), F32),
                          jax.ShapeDtypeStruct((SG_GROUPS, SG_BLOCK, SG_BLOCK), F32),
                          jax.ShapeDtypeStruct((SG_BLOCK, 128), F32)],
               grid=(m // r,),
               in_specs=[pl.BlockSpec((r, n3), lambda i: (i, 0)), pl.BlockSpec((r, di), lambda i: (i, 0)),
                         vec, vec, wsb, bsb],
               out_specs=[pl.BlockSpec((r, n3), lambda i: (i, 0)), vec, vec, wsb, bsb],
               scratch=[pltpu.VMEM((r, di), F32), pltpu.VMEM((r, di), F32)],
               sem=("arbitrary",), comm=comm)(proj, dybr, ln_g, ln_b, w_s, bs_t)


def _chunk_rows(n):
    if isinstance(n, int):
        return pl.ds(n * CHUNK, CHUNK)
    return pl.ds(pl.multiple_of(n * CHUNK, CHUNK), CHUNK)


def _hgrn_dims(t_seq, di):
    tr = _tile(t_seq, 128)
    hc = _tile(di, 2048)
    return tr, hc, hc // HEAD_DIM


def _hgrn_gates(f_ref, lb, a_scr, k_scr, tr):
    sig = _sigmoid(f_ref[...])
    fg = lb + (1.0 - lb) * sig
    k_scr[...] = 1.0 - fg
    logf = jnp.log(fg)
    g = min(CUM_ROWS, tr)
    tri = _tri_mask(g, reverse=False)
    for rg in range(tr // g):
        a_scr[rg * g:(rg + 1) * g, :] = _tri_apply(tri, logf[rg * g:(rg + 1) * g, :])
    return sig, fg


def _hgrn_fwd(proj, lbj, gn, nb, t_seq):
    _, m, di = proj.shape
    tr, hc, hpg = _hgrn_dims(t_seq, di)
    nt, nhg, ncl = t_seq // tr, di // hc, tr // CHUNK
    nheads = di // HEAD_DIM

    nbuf, nsteps = 3, nhg * nb * nt

    def body(p_hbm, lb_ref, gn_ref, o_ref, ybr_ref, st_ref, st_scr, a_scr, k_scr, buf, sems):
        t = pl.program_id(2)
        step = (pl.program_id(0) * nb + pl.program_id(1)) * nt + t

        def fetch(s):
            rt, hg = s % (nb * nt), s // (nb * nt)
            return pltpu.make_async_copy(p_hbm.at[:, pl.ds(rt * tr, tr), pl.ds(hg * hc, hc)],
                                         buf.at[s % nbuf], sems.at[s % nbuf])

        @pl.when(step == 0)
        def _():
            for s in range(min(nbuf - 1, nsteps)):
                fetch(s).start()

        @pl.when(step + nbuf - 1 < nsteps)
        def _():
            fetch(step + nbuf - 1).start()

        fetch(step).wait()
        p_ref = buf.at[step % nbuf]
        q_ref, f_ref, i_ref, g_ref = (p_ref.at[s] for s in range(4))

        @pl.when(t == 0)
        def _():
            st_scr[...] = jnp.zeros_like(st_scr)

        _hgrn_gates(f_ref, lb_ref[0:1, :], a_scr, k_scr, tr)
        gnv = gn_ref[...]
        rr = lax.broadcasted_iota(jnp.int32, (CHUNK, CHUNK), 0)
        cc = lax.broadcasted_iota(jnp.int32, (CHUNK, CHUNK), 1)
        causal = cc <= rr

        def chunk(n, carry):
            rows = _chunk_rows(n)
            lanes = [slice(hd * HEAD_DIM, (hd + 1) * HEAD_DIM) for hd in range(hpg)]
            hs = []
            for hd, ls in enumerate(lanes):
                h = {}
                ah, kh = a_scr[rows, ls], k_scr[rows, ls]
                qp = q_ref[rows, ls]
                qh = qp * _sigmoid(qp)
                h["vb"] = i_ref[rows, ls].astype(BF16)
                aref, alast = ah[CHUNK // 2 - 1:CHUNK // 2, :], ah[CHUNK - 1:CHUNK, :]
                h["q_in"] = (qh * jnp.exp(ah - aref)).astype(BF16)
                h["k_in"] = (kh * jnp.exp(aref - ah)).astype(BF16)
                h["q_out"] = (qh * jnp.exp(ah)).astype(BF16)
                h["k_out"] = (kh * jnp.exp(alast - ah)).astype(BF16)
                h["dec"] = jnp.exp(alast)
                st = st_scr[hd]
                st_ref[n, hd] = st
                h["st"] = st
                hs.append(h)
            for h in hs:
                h["scores"] = _dot_nt(h["q_in"], h["k_in"])
                h["o_inter"] = _dot_nt(h["q_out"], h["st"].astype(BF16))
                h["st_mm"] = _dot_tn(h["vb"], h["k_out"])
            for h in hs:
                h["o"] = _dot(jnp.where(causal, h["scores"], 0.0).astype(BF16), h["vb"]) + h["o_inter"]
            for hd, (h, ls) in enumerate(zip(hs, lanes)):
                st_scr[hd] = h["st"] * h["dec"] + h["st_mm"]
                o = h["o"]
                o_ref[rows, ls] = o
                rstd = lax.rsqrt(jnp.mean(o * o, axis=-1, keepdims=True) + EPS)
                gg = g_ref[rows, ls]
                ybr_ref[rows, ls] = ((o * rstd * gnv) * (gg * _sigmoid(gg))).astype(BF16)
            return carry

        lax.fori_loop(0, ncl, chunk, 0)

    blk = pl.BlockSpec((tr, hc), lambda hg, b, t: (b * nt + t, hg))
    return _pc(body, name="hgrn_fwd",
               out_shape=[jax.ShapeDtypeStruct((m, di), F32), jax.ShapeDtypeStruct((m, di), BF16),
                          jax.ShapeDtypeStruct((m // CHUNK, nheads, HEAD_DIM, HEAD_DIM), F32)],
               grid=(nhg, nb, nt),
               in_specs=[ANY, pl.BlockSpec((2, hc), lambda hg, b, t: (0, hg)),
                         pl.BlockSpec((1, HEAD_DIM), lambda hg, b, t: (0, 0))],
               out_specs=[blk, blk, pl.BlockSpec((ncl, hpg, HEAD_DIM, HEAD_DIM),
                                                 lambda hg, b, t: (b * nt + t, hg, 0, 0))],
               scratch=[pltpu.VMEM((hpg, HEAD_DIM, HEAD_DIM), F32), pltpu.VMEM((tr, hc), F32),
                        pltpu.VMEM((tr, hc), F32), pltpu.VMEM((nbuf, 4, tr, hc), F32),
                        pltpu.SemaphoreType.DMA((nbuf,))],
               sem=("arbitrary", "arbitrary", "arbitrary"))(proj, lbj, gn)


def _hgrn_bwd(proj, o_all, dybr, states, lbj, gn, nb, t_seq, comm=None):
    _, m, di = proj.shape
    tr, hc, hpg = _hgrn_dims(t_seq, di)
    nt, nhg, ncl = t_seq // tr, di // hc, tr // CHUNK

    nbuf, nsteps = 3, nhg * nb * nt

    def body(p_hbm, o_ref, dy_ref, st_hbm, lb_ref, gn_ref,
             dp_ref, dlb_ref, dgn_ref, dst_scr, a_scr, k_scr, da_scr, dk_scr, pbuf, sbuf, sems):
        hg, b, t = pl.program_id(0), pl.program_id(1), pl.program_id(2)
        step = (hg * nb + b) * nt + t

        def fetch(s):
            rt, cg = (s // nt) % nb * nt + (nt - 1 - s % nt), s // (nb * nt)
            return (pltpu.make_async_copy(p_hbm.at[:, pl.ds(rt * tr, tr), pl.ds(cg * hc, hc)],
                                          pbuf.at[s % nbuf], sems.at[0, s % nbuf]),
                    pltpu.make_async_copy(st_hbm.at[pl.ds(rt * ncl, ncl), pl.ds(cg * hpg, hpg)],
                                          sbuf.at[s % nbuf], sems.at[1, s % nbuf]))

        @pl.when(step == 0)
        def _():
            for s in range(min(nbuf - 1, nsteps)):
                for c in fetch(s):
                    c.start()

        @pl.when(step + nbuf - 1 < nsteps)
        def _():
            for c in fetch(step + nbuf - 1):
                c.start()

        for c in fetch(step):
            c.wait()
        p_ref, st_ref = pbuf.at[step % nbuf], sbuf.at[step % nbuf]
        q_ref, f_ref, i_ref, g_ref = (p_ref.at[s] for s in range(4))

        @pl.when(t == 0)
        def _():
            dst_scr[...] = jnp.zeros_like(dst_scr)

        @pl.when((b == 0) & (t == 0))
        def _():
            dlb_ref[...] = jnp.zeros_like(dlb_ref)

        @pl.when((hg == 0) & (b == 0) & (t == 0))
        def _():
            dgn_ref[...] = jnp.zeros_like(dgn_ref)

        lb = lb_ref[0:1, :]
        sig, fg = _hgrn_gates(f_ref, lb, a_scr, k_scr, tr)
        gnv = gn_ref[...]
        rr = lax.broadcasted_iota(jnp.int32, (CHUNK, CHUNK), 0)
        cc = lax.broadcasted_iota(jnp.int32, (CHUNK, CHUNK), 1)
        causal = cc <= rr
        rowi = lax.broadcasted_iota(jnp.int32, (CHUNK, HEAD_DIM), 0)

        def chunk(it, carry):
            n = ncl - 1 - it
            rows = _chunk_rows(n)
            for hd0 in range(0, hpg, PHASE_HEADS):
                heads(n, rows, range(hd0, min(hpg, hd0 + PHASE_HEADS)))
            return carry

        def heads(n, rows, ids):
            lanes = [slice(hd * HEAD_DIM, (hd + 1) * HEAD_DIM) for hd in ids]
            hs = []
            for hd, ls in zip(ids, lanes):
                h = {}
                ah, kh = a_scr[rows, ls], k_scr[rows, ls]
                qp = q_ref[rows, ls]
                sq = _sigmoid(qp)
                qh = qp * sq
                h["dsilu_q"] = sq * (1.0 + qp * (1.0 - sq))
                h["vb"] = i_ref[rows, ls].astype(BF16)
                aref, alast = ah[CHUNK // 2 - 1:CHUNK // 2, :], ah[CHUNK - 1:CHUNK, :]
                h["e1"], h["e2"] = jnp.exp(ah - aref), jnp.exp(aref - ah)
                h["e3"], h["e4"] = jnp.exp(ah), jnp.exp(alast - ah)
                h["dec"] = jnp.exp(alast)
                h["q_in"], h["k_in"], h["q_out"], h["k_out"] = qh * h["e1"], kh * h["e2"], qh * h["e3"], kh * h["e4"]
                for nm in ("q_in", "k_in", "q_out", "k_out"):
                    h[nm + "_b"] = h[nm].astype(BF16)
                o = o_ref[rows, ls]
                rstd = lax.rsqrt(jnp.mean(o * o, axis=-1, keepdims=True) + EPS)
                ohat = o * rstd
                gg = g_ref[rows, ls]
                sg = _sigmoid(gg)
                dyv = dy_ref[rows, ls]
                d_on = dyv * (gg * sg)
                dp_ref[3, rows, ls] = (dyv * (ohat * gnv) * (sg * (1.0 + gg * (1.0 - sg)))).astype(BF16)
                h["dgn"] = jnp.sum(d_on * ohat, axis=0, keepdims=True)
                dohat = d_on * gnv
                do = rstd * (dohat - ohat * jnp.mean(dohat * ohat, axis=-1, keepdims=True))
                h["do_b"] = do.astype(BF16)
                h["st_prev"] = st_ref[n, hd]
                h["dst"] = dst_scr[hd]
                hs.append(h)
            for h in hs:
                dst_b = h["dst"].astype(BF16)
                h["scores"] = _dot_nt(h["q_in_b"], h["k_in_b"])
                h["dscores"] = _dot_nt(h["do_b"], h["vb"])
                h["dv_inter"] = _dot_nt(h["k_out_b"], dst_b)
                h["dq_out"] = _dot(h["do_b"], h["st_prev"].astype(BF16))
                h["dk_out"] = _dot(h["vb"], dst_b)
                h["dst_mm"] = _dot_tn(h["do_b"], h["q_out_b"])
            for h in hs:
                scores = jnp.where(causal, h["scores"], 0.0).astype(BF16)
                dscores = jnp.where(causal, h["dscores"], 0.0).astype(BF16)
                h["dv"] = _dot_tn(scores, h["do_b"]) + h["dv_inter"]
                h["dq_in"] = _dot(dscores, h["k_in_b"])
                h["dk_in"] = _dot_tn(dscores, h["q_in_b"])
            dgn = hs[0]["dgn"]
            for h in hs[1:]:
                dgn = dgn + h["dgn"]
            dgn_ref[...] += dgn
            for hd, h, ls in zip(ids, hs, lanes):
                ddec = jnp.sum(h["dst"] * h["st_prev"], axis=0, keepdims=True)
                dst_scr[hd] = h["dst"] * h["dec"] + h["dst_mm"]
                dp_ref[2, rows, ls] = h["dv"].astype(BF16)
                dq = h["dq_in"] * h["e1"] + h["dq_out"] * h["e3"]
                dp_ref[0, rows, ls] = (dq * h["dsilu_q"]).astype(BF16)
                dk_scr[rows, ls] = h["dk_in"] * h["e2"] + h["dk_out"] * h["e4"]
                t_in = h["dq_in"] * h["q_in"] - h["dk_in"] * h["k_in"]
                t_out = h["dk_out"] * h["k_out"]
                da = t_in + h["dq_out"] * h["q_out"] - t_out
                da_ref_row = -jnp.sum(t_in, axis=0, keepdims=True)
                da_last_row = jnp.sum(t_out, axis=0, keepdims=True) + ddec * h["dec"]
                da = da + jnp.where(rowi == CHUNK // 2 - 1, da_ref_row, 0.0) \
                        + jnp.where(rowi == CHUNK - 1, da_last_row, 0.0)
                da_scr[rows, ls] = da

        if ncl <= 2:
            for it in range(ncl):
                chunk(it, 0)
        else:
            lax.fori_loop(0, ncl, chunk, 0)
        g = min(CUM_ROWS, tr)
        tri = _tri_mask(g, reverse=True)
        for rg in range(tr // g):
            rs = slice(rg * g, (rg + 1) * g)
            dlogf = _tri_apply(tri, da_scr[rs, :])
            df = dlogf / fg[rs, :] - dk_scr[rs, :]
            sgr = sig[rs, :]
            dp_ref[1, rs, :] = (df * (1.0 - lb) * (sgr * (1.0 - sgr))).astype(BF16)
            dlb_ref[...] += jnp.sum(df * (1.0 - sgr), axis=0, keepdims=True) * lb_ref[1:2, :]

    blk = pl.BlockSpec((tr, hc), lambda hg, b, t: (b * nt + (nt - 1 - t), hg))
    return _pc(body, name="hgrn_bwd",
               out_shape=[jax.ShapeDtypeStruct((4, m, di), BF16), jax.ShapeDtypeStruct((1, di), F32),
                          jax.ShapeDtypeStruct((1, HEAD_DIM), F32)],
               grid=(nhg, nb, nt),
               in_specs=[ANY, blk, blk, ANY,
                         pl.BlockSpec((2, hc), lambda hg, b, t: (0, hg)),
                         pl.BlockSpec((1, HEAD_DIM), lambda hg, b, t: (0, 0))],
               out_specs=[pl.BlockSpec((4, tr, hc), lambda hg, b, t: (0, b * nt + (nt - 1 - t), hg)),
                          pl.BlockSpec((1, hc), lambda hg, b, t: (0, hg)),
                          pl.BlockSpec((1, HEAD_DIM), lambda hg, b, t: (0, 0))],
               scratch=[pltpu.VMEM((hpg, HEAD_DIM, HEAD_DIM), F32)] + [pltpu.VMEM((tr, hc), F32)] * 4 + [
                   pltpu.VMEM((nbuf, 4, tr, hc), F32), pltpu.VMEM((nbuf, ncl, hpg, HEAD_DIM, HEAD_DIM), F32),
                   pltpu.SemaphoreType.DMA((2, nbuf))],
               sem=("arbitrary", "arbitrary", "arbitrary"), comm=comm)(
                   proj, o_all, dybr, states, lbj, gn)


def _adamw(parts, w, m, v, name):
    r, c = w.shape
    tr = _tile(r, 256)
    npart = len(parts)
    c1 = 1.0 - ADAM_B1 ** ADAM_STEP
    c2 = 1.0 - ADAM_B2 ** ADAM_STEP

    def body(*refs):
        p_refs = refs[:npart]
        _adamw_math(p_refs, *refs[npart:], c1, c2)

    blk = pl.BlockSpec((tr, c), lambda i: (i, 0))
    return _pc(body, name=name, out_shape=[jax.ShapeDtypeStruct((r, c), F32)] * 4, grid=(r // tr,),
               in_specs=[blk] * (npart + 3), out_specs=[blk] * 4, sem=("parallel",))(*parts, w, m, v)


def _adamw_math(p_refs, w_ref, m_ref, v_ref, g_ref, d_ref, nm_ref, nv_ref, c1, c2):
    g = p_refs[0][...].astype(F32)
    for p in p_refs[1:]:
        g = g + p[...].astype(F32)
    nm = ADAM_B1 * m_ref[...] + (1.0 - ADAM_B1) * g
    nv = ADAM_B2 * v_ref[...] + (1.0 - ADAM_B2) * (g * g)
    g_ref[...] = g
    nm_ref[...] = nm
    nv_ref[...] = nv
    d_ref[...] = -ADAM_LR * ((nm / c1) / (jnp.sqrt(nv / c2) + ADAM_EPS) + ADAM_WD * w_ref[...])


def _adamw_small(gathered, ws, ms, vs, name, sums=()):
    n, ns = len(ws), len(sums)
    c1 = 1.0 - ADAM_B1 ** ADAM_STEP
    c2 = 1.0 - ADAM_B2 ** ADAM_STEP

    def total(ref):
        g = ref[0]
        for part in range(1, ref.shape[0]):
            g = g + ref[part]
        return g

    def body(*refs):
        g_in, w_in, m_in, v_in = refs[:n], refs[n:2 * n], refs[2 * n:3 * n], refs[3 * n:4 * n]
        s_in = refs[4 * n:4 * n + ns]
        outs = refs[4 * n + ns:]
        for k in range(n):
            g = total(g_in[k])
            nm = ADAM_B1 * m_in[k][...] + (1.0 - ADAM_B1) * g
            nv = ADAM_B2 * v_in[k][...] + (1.0 - ADAM_B2) * (g * g)
            outs[4 * k][...] = g
            outs[4 * k + 1][...] = -ADAM_LR * ((nm / c1) / (jnp.sqrt(nv / c2) + ADAM_EPS) + ADAM_WD * w_in[k][...])
            outs[4 * k + 2][...] = nm
            outs[4 * k + 3][...] = nv
        for k in range(ns):
            outs[4 * n + k][...] = total(s_in[k])

    out_shape = [jax.ShapeDtypeStruct(w.shape, F32) for w in ws for _ in range(4)]
    out_shape += [jax.ShapeDtypeStruct(s.shape[1:], F32) for s in sums]
    res = _pc(body, name=name, out_shape=out_shape)(*gathered, *ws, *ms, *vs, *sums)
    return [res[4 * k:4 * k + 4] for k in range(n)] + list(res[4 * n:])


def _adamw_blocks(parts, idx, w, m, v, name):
    r, c = w.shape
    tr = _tile(r, 256)
    npart = len(parts)
    c1 = 1.0 - ADAM_B1 ** ADAM_STEP
    c2 = 1.0 - ADAM_B2 ** ADAM_STEP

    def body(idx_ref, *refs):
        _adamw_math(refs[:npart], *refs[npart:], c1, c2)

    def sel(p):
        return pl.BlockSpec((None, tr, c), lambda i, s: (s[p], i, 0))

    blk = pl.BlockSpec((tr, c), lambda i, s: (i, 0))
    gs = pltpu.PrefetchScalarGridSpec(num_scalar_prefetch=1, grid=(r // tr,),
                                      in_specs=[sel(p) for p in range(npart)] + [blk] * 3, out_specs=[blk] * 4)
    return _pc(body, name=name, out_shape=[jax.ShapeDtypeStruct((r, c), F32)] * 4, grid_spec=gs,
               sem=("parallel",))(idx, *parts, w, m, v)


_EARLY = ["a_ln_gain", "a_ln_bias", "a_w_s", "a_b_s", "b_lower_bounds", "b_gn_gain"]


def kernel(x, c, norm_gain, w_ada, b_ada, a_w_in, a_ln_gain, a_ln_bias, a_w_s, a_b_s, a_w_out, b_w_in, b_lower_bounds, b_gn_gain, b_w_out, final_gain, loss_target, m_norm_gain, m_w_ada, m_b_ada, m_a_w_in, m_a_ln_gain, m_a_ln_bias, m_a_w_s, m_a_b_s, m_a_w_out, m_b_w_in, m_b_lower_bounds, m_b_gn_gain, m_b_w_out, m_final_gain, v_norm_gain, v_w_ada, v_b_ada, v_a_w_in, v_a_ln_gain, v_a_ln_bias, v_a_w_s, v_a_b_s, v_a_w_out, v_b_w_in, v_b_lower_bounds, v_b_gn_gain, v_b_w_out, v_final_gain):
    w = dict(norm_gain=norm_gain, w_ada=w_ada, b_ada=b_ada, a_w_in=a_w_in, a_ln_gain=a_ln_gain,
             a_ln_bias=a_ln_bias, a_w_s=a_w_s, a_b_s=a_b_s, a_w_out=a_w_out, b_w_in=b_w_in,
             b_lower_bounds=b_lower_bounds, b_gn_gain=b_gn_gain, b_w_out=b_w_out, final_gain=final_gain)
    mo = dict(norm_gain=m_norm_gain, w_ada=m_w_ada, b_ada=m_b_ada, a_w_in=m_a_w_in, a_ln_gain=m_a_ln_gain,
              a_ln_bias=m_a_ln_bias, a_w_s=m_a_w_s, a_b_s=m_a_b_s, a_w_out=m_a_w_out, b_w_in=m_b_w_in,
              b_lower_bounds=m_b_lower_bounds, b_gn_gain=m_b_gn_gain, b_w_out=m_b_w_out, final_gain=m_final_gain)
    vo = dict(norm_gain=v_norm_gain, w_ada=v_w_ada, b_ada=v_b_ada, a_w_in=v_a_w_in, a_ln_gain=v_a_ln_gain,
              a_ln_bias=v_a_ln_bias, a_w_s=v_a_w_s, a_b_s=v_a_b_s, a_w_out=v_a_w_out, b_w_in=v_b_w_in,
              b_lower_bounds=v_b_lower_bounds, b_gn_gain=v_b_gn_gain, b_w_out=v_b_w_out, final_gain=v_final_gain)

    nb, t_seq, d = x.shape
    m = nb * t_seq
    ncol_ada = w_ada.shape[2]
    xi, yi, ci = lax.axis_index("x"), lax.axis_index("y"), lax.axis_index("c")
    me = 4 * xi + 2 * yi + ci

    c_g, wa_in_g = _all_gather([c, a_w_in[0].astype(BF16)], "gather_c_wa")

    c_all = c_g.reshape(NDEV * nb, d)
    b_cols = lax.dynamic_slice(b_ada, (0, me * ncol_ada), (2, ncol_ada)).reshape(2, 1, ncol_ada)
    mod_part, lbj = _ada_fwd(c_all, w_ada, b_cols, b_lower_bounds)
    mod_all = _all_gather([mod_part], "gather_mod")[0]
    mod_mine = lax.dynamic_slice_in_dim(mod_all, me * nb, nb, axis=2)
    mod_mine = mod_mine.transpose(1, 2, 0, 3).reshape(2, nb, 3, d)
    mod0, mod1 = mod_mine[0], mod_mine[1]

    di = a_w_out.shape[1] * NDEV

    xf = x.reshape(m, d)
    tgt = loss_target.reshape(m, d)
    ng0, ng1 = norm_gain[0:1], norm_gain[1:2]
    ncb = b_w_in.shape[2]
    wb_lo, wb_hi = b_w_in[0][:, :ncb // 2].astype(BF16), b_w_in[0][:, ncb // 2:].astype(BF16)
    h0, h0_t = _prenorm(xf, ng0, mod0, t_seq, "prenorm_a")
    proj_a, half = _mm_in(h0, [wa_in_g], 1, "in_proj_a", comm=_gather_first([a_w_out[0].astype(BF16), wb_lo]))
    bs_t = jnp.pad(a_b_s[0].T, ((0, 0), (0, 128 - SG_GROUPS)))
    ybr_a, (wa_out_g, wb_lo_g, wb_hi_half) = _a_mid_fwd(
        proj_a, a_ln_gain, a_ln_bias, a_w_s[0], bs_t, t_seq, comm=_join(_gather_second(half), _gather_first([wb_hi])))
    wa_out = wa_out_g.reshape(di, d)
    (yout_a, x1), (wb_hi_g, wb_out_half) = _out_proj(
        ybr_a, wa_out, xf, mod0, t_seq, "out_proj_a",
        comm=_join(_gather_second([wb_hi_half]), _gather_first([b_w_out[0].astype(BF16)])))
    wb_in_g = [wb_lo_g, wb_hi_g]
    h1, h1_t = _prenorm(x1, ng1, mod1, t_seq, "prenorm_b")
    proj_b, (wb_out_g,) = _mm_in(h1, wb_in_g, 4, "in_proj_b", comm=_gather_second([wb_out_half]))
    wb_out = wb_out_g.reshape(di, d)
    o_b, ybr_b, states = _hgrn_fwd(proj_b, lbj, b_gn_gain, nb, t_seq)
    yout_b, dx2, loss_part, d_final_gain = _out_proj_loss(ybr_b, wb_out, x1, mod1, final_gain.reshape(1, d), tgt, t_seq)

    rows_out = a_w_out.shape[1]
    dy_b, dgate1, dybr_b = _gate_dybr(dx2, yout_b, mod1, wb_out, t_seq, "dybr_b")
    rs_wb_out = _ReduceScatter(_mm_dw_out(ybr_b, dy_b, "dw_out_b").reshape(NDEV, rows_out, d), "b_w_out")
    (dproj_b, d_lb, d_gn), got = _hgrn_bwd(proj_b, o_b, dybr_b, states, lbj, b_gn_gain, nb, t_seq,
                                           comm=rs_wb_out.swap_core())
    rs_wb_out.after_core(got[0])
    dh1, got = _mm_din(dproj_b, wb_in_g, 4, "dh_b", comm=rs_wb_out.swap_chips())
    rs_wb_out.after_chips(got[0])
    dx1, dss1, dgain1 = _prenorm_bwd(dh1, x1, ng1, mod1, dx2, t_seq, "prenorm_bwd_b")
    rs_wb_in = _ReduceScatter(_mm_dw_in(h1_t, dproj_b, ncb, 4, "dw_in_b"), "b_w_in")

    dy_a, dgate0, dybr_a = _gate_dybr(dx1, yout_a, mod0, wa_out, t_seq, "dybr_a")
    g_wa_out, got = _mm_dw_out(ybr_a, dy_a, "dw_out_a", comm=rs_wb_in.swap_core())
    rs_wb_in.after_core(got[0])
    rs_wa_out = _ReduceScatter(g_wa_out.reshape(NDEV, rows_out, d), "a_w_out")
    (dproj_a, d_lng, d_lnb, d_ws, d_bs_t), got = _a_mid_bwd(
        proj_a, dybr_a, a_ln_gain, a_ln_bias, a_w_s[0], bs_t, t_seq,
        comm=_join(rs_wb_in.swap_chips(), rs_wa_out.swap_core()))
    rs_wb_in.after_chips(got[0])
    rs_wa_out.after_core(got[1])
    early_parts = [d_lng, d_lnb, d_ws.reshape(SG_GROUPS * SG_BLOCK, SG_BLOCK), d_bs_t[:, :SG_GROUPS].T,
                   jnp.concatenate([-d_lb, d_lb], axis=0), d_gn]
    g_wa_in, got = _mm_dw_in(h0_t, dproj_a, wa_in_g.shape[2], 1, "dw_in_a",
                             comm=_join(rs_wa_out.swap_chips(), _gather_first(early_parts)))
    rs_wa_out.after_chips(got[0])
    rs_wa_in = _ReduceScatter(g_wa_in, "a_w_in")
    n_tiles = m // _din_tile(m)
    assert n_tiles >= 2
    first_tiles = max(1, (3 * n_tiles) // 8)
    dh0, got2 = _mm_din(dproj_a, [wa_in_g], 1, "dh_a_first", tiles=(0, first_tiles),
                        comm=_join(rs_wa_in.swap_core(), _gather_second(got[1:])))
    rs_wa_in.after_core(got2[0])
    early_all = got2[1:]
    dh0, got = _mm_din(dproj_a, [wa_in_g], 1, "dh_a_rest", comm=rs_wa_in.swap_chips(),
                       tiles=(first_tiles, n_tiles - first_tiles), prev=dh0)
    rs_wa_in.after_chips(got[0])
    dx0, dss0, dgain0 = _prenorm_bwd(dh0, xf, ng0, mod0, dx1, t_seq, "prenorm_bwd_a")
    grad_x = dx0.reshape(nb, t_seq, d)

    dmod = jnp.stack([jnp.concatenate([dss0, dgate0], axis=1), jnp.concatenate([dss1, dgate1], axis=1)])
    dmod_all, dgain_all, dfinal_all, loss_all = _all_gather(
        [dmod.reshape(2, nb, 3 * d), jnp.concatenate([dgain0, dgain1], axis=0), d_final_gain,
         jnp.broadcast_to(loss_part, (1, 128))], "gather_tail")
    dmod_all = dmod_all.transpose(1, 0, 2, 3).reshape(2, NDEV * nb, 3 * d)
    dmod_cols = lax.dynamic_slice_in_dim(dmod_all, me * ncol_ada, ncol_ada, axis=2)
    g_w_ada, g_b_ada = _ada_bwd(c_all, dmod_cols, dmod_all)

    def small2d(k, t):
        return t[k].reshape(early_parts[_EARLY.index(k)].shape if k in _EARLY else (-1, t[k].shape[-1]))

    res = {}
    sm = _adamw_small(early_all, *[[small2d(k, t) for k in _EARLY] for t in (w, mo, vo)], "adamw_small_early")
    for k, r in zip(_EARLY, sm):
        res[k] = tuple(z.reshape(w[k].shape) for z in r)
    late = ["norm_gain", "final_gain", "b_ada"]
    sm = _adamw_small([dgain_all, dfinal_all, g_b_ada[None]], *[[small2d(k, t) for k in late] for t in (w, mo, vo)],
                      "adamw_small_late", sums=[loss_all])
    for k, r in zip(late, sm):
        res[k] = tuple(z.reshape(w[k].shape) for z in r)
    loss = sm[3][0, 0]
    sh = w_ada.shape
    ra = _adamw([g_w_ada.reshape(sh[0] * sh[1], sh[2])], w_ada.reshape(sh[0] * sh[1], sh[2]),
                mo["w_ada"].reshape(sh[0] * sh[1], sh[2]), vo["w_ada"].reshape(sh[0] * sh[1], sh[2]), "adamw_w_ada")
    res["w_ada"] = tuple(z.reshape(sh) for z in ra)

    for k, rs in (("b_w_out", rs_wb_out), ("b_w_in", rs_wb_in), ("a_w_out", rs_wa_out), ("a_w_in", rs_wa_in)):
        res[k] = tuple(z[None] for z in _adamw_blocks(rs.parts, rs.idx, w[k][0], mo[k][0], vo[k][0], "adamw_" + k))

    order = ["norm_gain", "w_ada", "b_ada", "a_w_in", "a_ln_gain", "a_ln_bias", "a_w_s", "a_b_s", "a_w_out",
             "b_w_in", "b_lower_bounds", "b_gn_gain", "b_w_out", "final_gain"]
    return (loss, grad_x, *[res[k][0] for k in order], *[res[k][1] for k in order],
            *[res[k][2] for k in order], *[res[k][3] for k in order])
```

```python
import functools
import math

import jax
import jax.numpy as jnp
from jax import lax
from jax.experimental import pallas as pl
from jax.experimental.pallas import tpu as pltpu

F32 = jnp.float32
BF16 = jnp.bfloat16
MESH = pl.DeviceIdType.MESH
NDEV = 8
EPS = 1e-6
CHUNK = 64
SG_BLOCK = 128
SG_GROUPS = 8
HEAD_DIM = 128
CUM_ROWS = 256
PHASE_HEADS = 8
ADAM_LR, ADAM_B1, ADAM_B2, ADAM_EPS, ADAM_WD, ADAM_STEP = 0.001, 0.9, 0.999, 1e-08, 0.01, 10
VMEM_LIMIT = 56 * 1024 * 1024
ANY = pl.BlockSpec(memory_space=pl.ANY)


class _Hosted:
    def __init__(self, arrays, out_shapes, nsem, start, finish, aliases=None):
        self.arrays, self.out_shapes, self.nsem = list(arrays), list(out_shapes), nsem
        self.start, self.finish = start, finish
        self.aliases = dict(aliases or {})


def _join(*comms):
    arrays, outs, aliases, offs, nsem = [], [], {}, [], 0
    for cm in comms:
        offs.append((len(arrays), len(outs), nsem))
        for i, o in cm.aliases.items():
            aliases[len(arrays) + i] = len(outs) + o
        arrays += cm.arrays
        outs += cm.out_shapes
        nsem += cm.nsem

    def run(which):
        def f(ins, outs_, ss, rs, base):
            for cm, (ia, io, isem) in zip(comms, offs):
                getattr(cm, which)(ins[ia:ia + len(cm.arrays)], outs_[io:io + len(cm.out_shapes)], ss, rs, base + isem)
        return f

    return _Hosted(arrays, outs, nsem, run("start"), run("finish"), aliases)


def _pc(body, *, name, out_shape, grid=None, in_specs=None, out_specs=None, scratch=(), sem=None,
        grid_spec=None, comm=None, aliases=None):
    cp = dict(vmem_limit_bytes=VMEM_LIMIT)
    aliases = dict(aliases or {})
    if comm is None:
        if sem is not None:
            cp["dimension_semantics"] = sem
        kw = {"input_output_aliases": aliases}
        if grid_spec is not None:
            kw["grid_spec"] = grid_spec
        else:
            if grid is not None:
                kw["grid"] = grid
            if in_specs is not None:
                kw["in_specs"] = in_specs
            if out_specs is not None:
                kw["out_specs"] = out_specs
            kw["scratch_shapes"] = list(scratch)
        return pl.pallas_call(functools.partial(body), name=name, out_shape=out_shape,
                              compiler_params=pltpu.CompilerParams(**cp), **kw)

    single = not isinstance(out_shape, (list, tuple))
    outs_list = [out_shape] if single else list(out_shape)
    ospecs = [out_specs] if single else list(out_specs)
    n_in, n_out, n_ci, n_co, n_scr = len(in_specs), len(outs_list), len(comm.arrays), len(comm.out_shapes), len(scratch)
    cp["dimension_semantics"] = ("arbitrary",) * len(grid)

    def hosted(*refs):
        cin, hin = refs[:n_in], refs[n_in:n_in + n_ci]
        cout = refs[n_in + n_ci:n_in + n_ci + n_out]
        hout = refs[n_in + n_ci + n_out:n_in + n_ci + n_out + n_co]
        scr = refs[n_in + n_ci + n_out + n_co:n_in + n_ci + n_out + n_co + n_scr]
        ssem, rsem = refs[-2], refs[-1]
        first = functools.reduce(lambda p, q: p & q, [pl.program_id(a) == 0 for a in range(len(grid))])
        last = functools.reduce(lambda p, q: p & q, [pl.program_id(a) == grid[a] - 1 for a in range(len(grid))])

        @pl.when(first)
        def _():
            comm.start(hin, hout, ssem, rsem, 0)

        body(*cin, *cout, *scr)

        @pl.when(last)
        def _():
            comm.finish(hin, hout, ssem, rsem, 0)

    call = pl.pallas_call(
        hosted, name=name, grid=grid, in_specs=list(in_specs) + [ANY] * n_ci, out_specs=ospecs + [ANY] * n_co,
        out_shape=outs_list + comm.out_shapes,
        scratch_shapes=list(scratch) + [pltpu.SemaphoreType.DMA((comm.nsem,)), pltpu.SemaphoreType.DMA((comm.nsem,))],
        input_output_aliases={**aliases, **{n_in + i: n_out + o for i, o in comm.aliases.items()}},
        compiler_params=pltpu.CompilerParams(**cp))

    def run(*args):
        res = call(*args, *comm.arrays)
        comp = res[:n_out]
        return (comp[0] if single else comp), list(res[n_out:])

    return run


def _tile(n, pref):
    return pref if n % pref == 0 else n


def _sigmoid(x):
    return 1.0 / (1.0 + jnp.exp(-x))


def _gelu(x):
    c = math.sqrt(2.0 / math.pi)
    return 0.5 * x * (1.0 + jnp.tanh(c * (x + 0.044715 * (x * x * x))))


def _gelu_and_grad(x):
    c = math.sqrt(2.0 / math.pi)
    x2 = x * x
    t = jnp.tanh(c * (x + 0.044715 * (x2 * x)))
    half = 0.5 * (1.0 + t)
    return x * half, half + (0.5 * x) * (1.0 - t * t) * (c + (3.0 * 0.044715 * c) * x2)


def _dot(a, b):
    return jnp.dot(a, b, preferred_element_type=F32)


def _dot_nt(a, b):
    return lax.dot_general(a, b, (((1,), (1,)), ((), ())), preferred_element_type=F32)


def _dot_tn(a, b):
    return lax.dot_general(a, b, (((0,), (0,)), ((), ())), preferred_element_type=F32)


def _tri_mask(n, reverse):
    r = lax.broadcasted_iota(jnp.int32, (n, n), 0)
    c = lax.broadcasted_iota(jnp.int32, (n, n), 1)
    same = (r // CHUNK) == (c // CHUNK)
    tri = (c >= r) if reverse else (c <= r)
    return jnp.where(same & tri, 1.0, 0.0).astype(BF16)


def _tri_apply(tri, x):
    hi = x.astype(BF16)
    r1 = x - hi.astype(F32)
    mid = r1.astype(BF16)
    lo = (r1 - mid.astype(F32)).astype(BF16)
    return _dot(tri, hi) + (_dot(tri, mid) + _dot(tri, lo))


def _all_gather(arrs, name):
    n = len(arrs)

    def body(*refs):
        ins, outs = refs[:n], refs[n:2 * n]
        send_sems, recv_sems, local_sems = refs[2 * n:]
        x, y, c = lax.axis_index("x"), lax.axis_index("y"), lax.axis_index("c")
        me, sibling = (x, y, c), (x, y, 1 - c)
        near = (x + c - 2 * x * c, y + (1 - c) - 2 * y * (1 - c))
        far = (x + (1 - c) - 2 * x * (1 - c), y + c - 2 * y * c)
        diag = (1 - x, 1 - y)

        def blk(a, p):
            return outs[a].at[4 * p[0] + 2 * p[1] + p[2]]

        def copy(a, k, block, to, src=None):
            return pltpu.make_async_remote_copy(
                src_ref=blk(a, block) if src is None else src, dst_ref=blk(a, block),
                send_sem=send_sems.at[7 * a + k], recv_sem=recv_sems.at[7 * a + k],
                device_id=to, device_id_type=MESH)

        mine = [pltpu.make_async_copy(ins[a], blk(a, me), local_sems.at[a]) for a in range(n)]
        for m in mine:
            m.start()
        sends = []
        for a in range(n):
            sends += [copy(a, 0, me, sibling, src=ins[a]), copy(a, 1, me, (*near, c), src=ins[a]),
                      copy(a, 2, me, (*far, c), src=ins[a])]
        for cp in sends:
            cp.start()
        for a in range(n):
            copy(a, 1, (*near, c), me).wait_recv()
            sends.append(copy(a, 3, (*near, c), (*far, c)))
            sends[-1].start()
        for a in range(n):
            sends.append(copy(a, 4, (*near, c), sibling))
            sends[-1].start()
            copy(a, 2, (*far, c), me).wait_recv()
            sends.append(copy(a, 5, (*far, c), sibling))
            sends[-1].start()
        for a in range(n):
            copy(a, 3, (*diag, c), me).wait_recv()
            sends.append(copy(a, 6, (*diag, c), sibling))
            sends[-1].start()
        for a in range(n):
            copy(a, 0, sibling, me).wait_recv()
            copy(a, 4, (*far, 1 - c), me).wait_recv()
            copy(a, 5, (*near, 1 - c), me).wait_recv()
            copy(a, 6, (*diag, 1 - c), me).wait_recv()
        for cp in sends:
            cp.wait_send()
        for m in mine:
            m.wait()

    out_shape = [jax.ShapeDtypeStruct((NDEV,) + a.shape, a.dtype) for a in arrs]
    return _pc(body, name=name, out_shape=out_shape, in_specs=[ANY] * n, out_specs=[ANY] * n,
               scratch=[pltpu.SemaphoreType.DMA((7 * n,)), pltpu.SemaphoreType.DMA((7 * n,)),
                        pltpu.SemaphoreType.DMA((n,))])(*arrs)


def _gather_first(arrs):
    n = len(arrs)

    def parts(ins, outs, ss, rs, base):
        x, y, c = lax.axis_index("x"), lax.axis_index("y"), lax.axis_index("c")
        me, sibling = (x, y, c), (x, y, 1 - c)
        chips = [(1 - x, y), (x, 1 - y), (1 - x, 1 - y)]

        def blk(a, p):
            return outs[a].at[4 * p[0] + 2 * p[1] + p[2]]

        def copy(a, k, block, to):
            return pltpu.make_async_remote_copy(
                src_ref=ins[a], dst_ref=blk(a, block), send_sem=ss.at[base + 4 * a + k],
                recv_sem=rs.at[base + 4 * a + k], device_id=to, device_id_type=MESH)

        local = [pltpu.make_async_copy(ins[a], blk(a, me), ss.at[base + 4 * n + a]) for a in range(n)]
        sends, recvs = [], []
        for a in range(n):
            sends.append(copy(a, 0, me, sibling))
            recvs.append(copy(a, 0, sibling, me))
            for j, chip in enumerate(chips):
                sends.append(copy(a, 1 + j, me, (*chip, c)))
                recvs.append(copy(a, 1 + j, (*chip, c), me))
        return local, sends, recvs

    def start(ins, outs, ss, rs, base):
        local, sends, _ = parts(ins, outs, ss, rs, base)
        for cp in local + sends:
            cp.start()

    def finish(ins, outs, ss, rs, base):
        local, sends, recvs = parts(ins, outs, ss, rs, base)
        for cp in recvs:
            cp.wait_recv()
        for cp in sends:
            cp.wait_send()
        for cp in local:
            cp.wait()

    return _Hosted(arrs, [jax.ShapeDtypeStruct((NDEV,) + a.shape, a.dtype) for a in arrs], 5 * n, start, finish)


def _gather_second(bufs):
    n = len(bufs)

    def parts(ins, outs, ss, rs, base):
        x, y, c = lax.axis_index("x"), lax.axis_index("y"), lax.axis_index("c")
        sibling = (x, y, 1 - c)
        chips = [(1 - x, y), (x, 1 - y), (1 - x, 1 - y)]
        sends, recvs = [], []
        for a in range(n):
            for j, chip in enumerate(chips):
                mine = 4 * chip[0] + 2 * chip[1] + c
                theirs = 4 * chip[0] + 2 * chip[1] + (1 - c)
                sends.append(pltpu.make_async_remote_copy(
                    src_ref=ins[a].at[mine], dst_ref=outs[a].at[mine], send_sem=ss.at[base + 3 * a + j],
                    recv_sem=rs.at[base + 3 * a + j], device_id=sibling, device_id_type=MESH))
                recvs.append(pltpu.make_async_remote_copy(
                    src_ref=ins[a].at[theirs], dst_ref=outs[a].at[theirs], send_sem=ss.at[base + 3 * a + j],
                    recv_sem=rs.at[base + 3 * a + j], device_id=sibling, device_id_type=MESH))
        return sends, recvs

    def start(ins, outs, ss, rs, base):
        for cp in parts(ins, outs, ss, rs, base)[0]:
            cp.start()

    def finish(ins, outs, ss, rs, base):
        sends, recvs = parts(ins, outs, ss, rs, base)
        for cp in recvs:
            cp.wait_recv()
        for cp in sends:
            cp.wait_send()

    return _Hosted(bufs, [jax.ShapeDtypeStruct(b.shape, b.dtype) for b in bufs], 3 * n, start, finish,
                   aliases={a: a for a in range(n)})


def _swap(src, nblk, ids_fn, partner_fn):
    def copies(ins, outs, ss, rs, base):
        x, y, c = lax.axis_index("x"), lax.axis_index("y"), lax.axis_index("c")
        ids = ids_fn(x, y, c)
        partner = partner_fn(x, y, c)
        return [pltpu.make_async_remote_copy(
            src_ref=ins[0].at[ids[k]], dst_ref=outs[0].at[k], send_sem=ss.at[base + k], recv_sem=rs.at[base + k],
            device_id=partner, device_id_type=MESH) for k in range(nblk)]

    def start(ins, outs, ss, rs, base):
        for cp in copies(ins, outs, ss, rs, base):
            cp.start()

    def finish(ins, outs, ss, rs, base):
        for cp in copies(ins, outs, ss, rs, base):
            cp.wait()

    return _Hosted([src], [jax.ShapeDtypeStruct((nblk,) + src.shape[1:], src.dtype)], nblk, start, finish)


def _swap_chips(send):
    def copies(ins, outs, ss, rs, base):
        x, y, c = lax.axis_index("x"), lax.axis_index("y"), lax.axis_index("c")
        chips = [(1 - x, y), (x, 1 - y), (1 - x, 1 - y)]
        return [pltpu.make_async_remote_copy(
            src_ref=ins[0].at[j], dst_ref=outs[0].at[j], send_sem=ss.at[base + j], recv_sem=rs.at[base + j],
            device_id=(*chip, c), device_id_type=MESH) for j, chip in enumerate(chips)]

    def start(ins, outs, ss, rs, base):
        for cp in copies(ins, outs, ss, rs, base):
            cp.start()

    def finish(ins, outs, ss, rs, base):
        for cp in copies(ins, outs, ss, rs, base):
            cp.wait()

    return _Hosted([send], [jax.ShapeDtypeStruct(send.shape, send.dtype)], 3, start, finish)


def _add_send(a, b, idx, ns, name):
    _, r, c = a.shape
    tr = _tile(r, 256)

    def body(idx_ref, a_ref, b_ref, send_ref):
        send_ref[...] = (a_ref[...] + b_ref[...]).astype(BF16)

    def sel(off):
        return pl.BlockSpec((None, tr, c), lambda k, i, s: (s[off + k], i, 0))

    gs = pltpu.PrefetchScalarGridSpec(num_scalar_prefetch=1, grid=(ns, r // tr), in_specs=[sel(0), sel(ns)],
                                      out_specs=pl.BlockSpec((None, tr, c), lambda k, i, s: (k, i, 0)))
    return _pc(body, name=name, grid_spec=gs, sem=("arbitrary", "arbitrary"),
               out_shape=jax.ShapeDtypeStruct((ns, r, c), BF16))(idx, a, b)


class _ReduceScatter:
    def __init__(self, g, tag):
        self.g, self.tag = g, tag

    def swap_core(self):
        return _swap(self.g, 4, lambda x, y, c: [1 - c, 3 - c, 5 - c, 7 - c], lambda x, y, c: (x, y, 1 - c))

    def after_core(self, recv):
        x, y, c = lax.axis_index("x"), lax.axis_index("y"), lax.axis_index("c")
        chips = [(1 - x, y), (x, 1 - y), (1 - x, 1 - y)]
        idx = jnp.stack([4 * p + 2 * q + c for p, q in chips] + [2 * p + q for p, q in chips]).astype(jnp.int32)
        self.send = _add_send(self.g, recv, idx, 3, "rs_add_" + self.tag)
        self.recv_core = recv
        zero = jnp.zeros((), jnp.int32)
        self.idx = jnp.stack([4 * x + 2 * y + c, 2 * x + y, zero, zero + 1, zero + 2]).astype(jnp.int32)

    def swap_chips(self):
        return _swap_chips(self.send)

    def after_chips(self, recv):
        self.parts = [self.g, self.recv_core, recv, recv, recv]


def _ada_fwd(c_all, w_ada, b_cols, b_lb):
    nl, d, ncol = w_ada.shape
    nseq = c_all.shape[0]
    di = b_lb.shape[1]

    def body(c_ref, w_ref, b_ref, lb_ref, mod_ref, lbj_ref):
        cv = c_ref[...]
        cact = (cv * _sigmoid(cv)).astype(BF16)
        for l in range(nl):
            mod_ref[l] = _dot(cact, w_ref[l].astype(BF16)) + b_ref[l]
        b0, b1 = lb_ref[0:1, :], lb_ref[1:2, :]
        mx = jnp.maximum(b0, b1)
        e0, e1 = jnp.exp(b0 - mx), jnp.exp(b1 - mx)
        s = e0 + e1
        p0, p1 = e0 / s, e1 / s
        lbj_ref[0:1, :] = (p0 + p1) - p0
        lbj_ref[1:2, :] = p0 * p1

    return _pc(body, name="ada_fwd",
               out_shape=[jax.ShapeDtypeStruct((nl, nseq, ncol), F32), jax.ShapeDtypeStruct((2, di), F32)]
               )(c_all, w_ada, b_cols, b_lb)


def _ada_bwd(c_all, dmod_cols, dmod_full):
    nl, nseq, ncol = dmod_cols.shape
    d = c_all.shape[1]
    d3 = dmod_full.shape[2]

    def body(c_ref, dc_ref, df_ref, gw_ref, gb_ref):
        cv = c_ref[...]
        cact = (cv * _sigmoid(cv)).astype(BF16)
        for l in range(nl):
            gw_ref[l] = _dot_tn(cact, dc_ref[l].astype(BF16))
            gb_ref[l:l + 1, :] = jnp.sum(df_ref[l], axis=0, keepdims=True)

    return _pc(body, name="ada_bwd",
               out_shape=[jax.ShapeDtypeStruct((nl, d, ncol), F32), jax.ShapeDtypeStruct((nl, d3), F32)]
               )(c_all, dmod_cols, dmod_full)


def _prenorm(x, gain, mod, t_seq, name):
    m, d = x.shape
    tm = _tile(t_seq, 1024)
    per = t_seq // tm

    def body(x_ref, g_ref, mod_ref, h_ref, ht_ref):
        xv = x_ref[...]
        rstd = lax.rsqrt(jnp.mean(xv * xv, axis=-1, keepdims=True) + EPS)
        r = xv * rstd * g_ref[...]
        h = r * (1.0 + mod_ref[0, 1:2, :]) + mod_ref[0, 0:1, :]
        h_ref[...] = h.astype(BF16)
        ht_ref[...] = h.T.astype(BF16)

    return _pc(body, name=name, out_shape=[jax.ShapeDtypeStruct((m, d), BF16), jax.ShapeDtypeStruct((d, m), BF16)],
               grid=(m // tm,),
               in_specs=[pl.BlockSpec((tm, d), lambda i: (i, 0)), pl.BlockSpec((1, d), lambda i: (0, 0)),
                         pl.BlockSpec((1, 3, d), lambda i: (i // per, 0, 0))],
               out_specs=[pl.BlockSpec((tm, d), lambda i: (i, 0)), pl.BlockSpec((d, tm), lambda i: (0, i))],
               sem=("parallel",))(x, gain, mod)


def _prenorm_bwd(dh, x, gain, mod, dxn, t_seq, name):
    m, d = x.shape
    nb = m // t_seq
    tm = _tile(t_seq, 1024)
    per = t_seq // tm

    def body(dh_ref, x_ref, g_ref, mod_ref, dxn_ref, dx_ref, dss_ref, dg_ref):
        i = pl.program_id(0)
        xv, dhv, g = x_ref[...], dh_ref[...], g_ref[...]
        rstd = lax.rsqrt(jnp.mean(xv * xv, axis=-1, keepdims=True) + EPS)
        xhat = xv * rstd
        dr = dhv * (1.0 + mod_ref[0, 1:2, :])
        dxhat = dr * g
        dx_ref[...] = dxn_ref[...] + rstd * (dxhat - xhat * jnp.mean(dxhat * xhat, axis=-1, keepdims=True))

        @pl.when(i % per == 0)
        def _():
            dss_ref[...] = jnp.zeros_like(dss_ref)

        @pl.when(i == 0)
        def _():
            dg_ref[...] = jnp.zeros_like(dg_ref)

        dss_ref[0, 0:1, :] += jnp.sum(dhv, axis=0, keepdims=True)
        dss_ref[0, 1:2, :] += jnp.sum(dhv * (xhat * g), axis=0, keepdims=True)
        dg_ref[...] += jnp.sum(dr * xhat, axis=0, keepdims=True)

    row = pl.BlockSpec((tm, d), lambda i: (i, 0))
    return _pc(body, name=name,
               out_shape=[jax.ShapeDtypeStruct((m, d), F32), jax.ShapeDtypeStruct((nb, 2, d), F32),
                          jax.ShapeDtypeStruct((1, d), F32)],
               grid=(m // tm,),
               in_specs=[row, row, pl.BlockSpec((1, d), lambda i: (0, 0)),
                         pl.BlockSpec((1, 3, d), lambda i: (i // per, 0, 0)), row],
               out_specs=[row, pl.BlockSpec((1, 2, d), lambda i: (i // per, 0, 0)),
                          pl.BlockSpec((1, d), lambda i: (0, 0))],
               sem=("arbitrary",))(dh, x, gain, mod, dxn)


def _mm_in(h, ws, sections, name, comm=None):
    m, k = h.shape
    nw = len(ws)
    widths = [w.shape[2] for w in ws]
    offs = [sum(widths[:a]) for a in range(nw)]
    nc = sum(widths)
    per = NDEV // sections if sections > 1 else NDEV
    tm = _din_tile(m)
    assert per % 2 == 0

    def body(*refs):
        hv = refs[0][...]
        o_ref = refs[1 + nw]
        for b in range(2):
            for a in range(nw):
                lo = b * nc + offs[a]
                o_ref[:, lo:lo + widths[a]] = _dot(hv, refs[1 + a][b])

    w_specs = [pl.BlockSpec((2, k, wd), lambda j, i: (j, 0, 0)) for wd in widths]
    if sections > 1:
        out_shape = jax.ShapeDtypeStruct((sections, m, per * nc), F32)
        out_spec = pl.BlockSpec((None, tm, 2 * nc), lambda j, i: ((2 * j) // per, i, ((2 * j) % per) // 2))
    else:
        out_shape = jax.ShapeDtypeStruct((m, NDEV * nc), F32)
        out_spec = pl.BlockSpec((tm, 2 * nc), lambda j, i: (i, j))
    return _pc(body, name=name, out_shape=out_shape, grid=(NDEV // 2, m // tm),
               in_specs=[pl.BlockSpec((tm, k), lambda j, i: (i, 0))] + w_specs,
               out_specs=out_spec, sem=("parallel", "parallel"), comm=comm)(h, *ws)


def _din_tile(m):
    return 1024 if m % 1024 == 0 and m >= 2048 else _tile(m, 512)


def _mm_din(dproj, ws, sections, name, comm=None, tiles=None, prev=None):
    nw, k = len(ws), ws[0].shape[1]
    widths = [w.shape[2] for w in ws]
    offs = [sum(widths[:a]) for a in range(nw)]
    nc = sum(widths)
    m = dproj.shape[-2]
    tm = _din_tile(m)
    t0, nt = tiles if tiles is not None else (0, m // tm)
    per = NDEV // sections if sections > 1 else NDEV
    assert per % 2 == 0

    def body(*refs):
        d_ref, o_ref = refs[0], refs[-1]
        j = pl.program_id(1)
        acc = None
        for b in range(2):
            for a in range(nw):
                lo = b * nc + offs[a]
                term = _dot_nt(d_ref[:, lo:lo + widths[a]], refs[1 + a][b])
                acc = term if acc is None else acc + term

        @pl.when(j == 0)
        def _():
            o_ref[...] = acc

        @pl.when(j > 0)
        def _():
            o_ref[...] += acc

    if sections > 1:
        dspec = pl.BlockSpec((None, tm, 2 * nc), lambda i, j: ((2 * j) // per, i + t0, ((2 * j) % per) // 2))
    else:
        dspec = pl.BlockSpec((tm, 2 * nc), lambda i, j: (i + t0, j))
    in_specs = [dspec] + [pl.BlockSpec((2, k, wd), lambda i, j: (j, 0, 0)) for wd in widths]
    args = [dproj, *ws]
    if prev is not None:
        in_specs.append(ANY)
        args.append(prev)
    return _pc(body, name=name, out_shape=jax.ShapeDtypeStruct((m, k), F32), grid=(nt, NDEV // 2), in_specs=in_specs,
               out_specs=pl.BlockSpec((tm, k), lambda i, j: (i + t0, 0)), sem=("parallel", "arbitrary"),
               comm=comm, aliases={1 + nw: 0} if prev is not None else None)(*args)


def _mm_dw_in(ht, dproj, nc, sections, name, comm=None):
    k, m = ht.shape
    per = NDEV // sections if sections > 1 else NDEV

    def body(h_ref, d_ref, o_ref):
        o_ref[...] = _dot(h_ref[...], d_ref[...])

    if sections > 1:
        dspec = pl.BlockSpec((None, m, nc), lambda j: (j // per, 0, j % per))
    else:
        dspec = pl.BlockSpec((m, nc), lambda j: (0, j))
    return _pc(body, name=name, out_shape=jax.ShapeDtypeStruct((NDEV, k, nc), F32), grid=(NDEV,),
               in_specs=[pl.BlockSpec((k, m), lambda j: (0, 0)), dspec],
               out_specs=pl.BlockSpec((None, k, nc), lambda j: (j, 0, 0)),
               sem=("parallel",), comm=comm)(ht, dproj)


def _out_proj(ybr, w_out, x, mod, t_seq, name, comm=None):
    m, di = ybr.shape
    d = w_out.shape[1]
    tm = _tile(t_seq, 512)
    per = t_seq // tm

    def body(y_ref, w_ref, x_ref, mod_ref, yo_ref, xn_ref):
        yo = _dot(y_ref[...], w_ref[...])
        yo_ref[...] = yo
        xn_ref[...] = x_ref[...] + mod_ref[0, 2:3, :] * yo

    row = pl.BlockSpec((tm, d), lambda i: (i, 0))
    return _pc(body, name=name,
               out_shape=[jax.ShapeDtypeStruct((m, d), F32), jax.ShapeDtypeStruct((m, d), F32)],
               grid=(m // tm,),
               in_specs=[pl.BlockSpec((tm, di), lambda i: (i, 0)), pl.BlockSpec((di, d), lambda i: (0, 0)), row,
                         pl.BlockSpec((1, 3, d), lambda i: (i // per, 0, 0))],
               out_specs=[row, row], sem=("parallel",), comm=comm)(ybr, w_out, x, mod)


def _out_proj_loss(ybr, w_out, x, mod, gain, target, t_seq):
    m, di = ybr.shape
    d = w_out.shape[1]
    tm = _tile(t_seq, 512)
    per = t_seq // tm

    def body(y_ref, w_ref, x_ref, mod_ref, g_ref, t_ref, yo_ref, dx_ref, loss_ref, dg_ref):
        i = pl.program_id(0)
        yo = _dot(y_ref[...], w_ref[...])
        yo_ref[...] = yo
        xv = x_ref[...] + mod_ref[0, 2:3, :] * yo
        g = g_ref[...]
        rstd = lax.rsqrt(jnp.mean(xv * xv, axis=-1, keepdims=True) + EPS)
        xhat = xv * rstd
        err = xhat * g - t_ref[...]
        dy = err * (1.0 / d)
        dxhat = dy * g
        dx_ref[...] = rstd * (dxhat - xhat * jnp.mean(dxhat * xhat, axis=-1, keepdims=True))

        @pl.when(i == 0)
        def _():
            loss_ref[...] = jnp.zeros_like(loss_ref)
            dg_ref[...] = jnp.zeros_like(dg_ref)

        loss_ref[...] += 0.5 * jnp.sum(jnp.mean(err * err, axis=-1, keepdims=True), axis=0, keepdims=True)
        dg_ref[...] += jnp.sum(dy * xhat, axis=0, keepdims=True)

    row = pl.BlockSpec((tm, d), lambda i: (i, 0))
    vec = pl.BlockSpec((1, d), lambda i: (0, 0))
    return _pc(body, name="out_proj_loss",
               out_shape=[jax.ShapeDtypeStruct((m, d), F32), jax.ShapeDtypeStruct((m, d), F32),
                          jax.ShapeDtypeStruct((1, 1), F32), jax.ShapeDtypeStruct((1, d), F32)],
               grid=(m // tm,),
               in_specs=[pl.BlockSpec((tm, di), lambda i: (i, 0)), pl.BlockSpec((di, d), lambda i: (0, 0)), row,
                         pl.BlockSpec((1, 3, d), lambda i: (i // per, 0, 0)), vec, row],
               out_specs=[row, row, pl.BlockSpec((1, 1), lambda i: (0, 0)), vec],
               sem=("arbitrary",))(ybr, w_out, x, mod, gain, target)


def _gate_dybr(dxn, yout, mod, w_out, t_seq, name):
    m, d = dxn.shape
    di = w_out.shape[0]
    nb = m // t_seq
    tm = _tile(t_seq, 1024)
    per = t_seq // tm

    def body(dxn_ref, yo_ref, mod_ref, w_ref, dy_ref, dgate_ref, o_ref):
        i = pl.program_id(0)
        dv = dxn_ref[...]
        dy = (mod_ref[0, 2:3, :] * dv).astype(BF16)
        dy_ref[...] = dy
        o_ref[...] = _dot_nt(dy, w_ref[...])

        @pl.when(i % per == 0)
        def _():
            dgate_ref[...] = jnp.zeros_like(dgate_ref)

        dgate_ref[0] += jnp.sum(dv * yo_ref[...], axis=0, keepdims=True)

    row = pl.BlockSpec((tm, d), lambda i: (i, 0))
    return _pc(body, name=name,
               out_shape=[jax.ShapeDtypeStruct((m, d), BF16), jax.ShapeDtypeStruct((nb, 1, d), F32),
                          jax.ShapeDtypeStruct((m, di), F32)],
               grid=(m // tm,),
               in_specs=[row, row, pl.BlockSpec((1, 3, d), lambda i: (i // per, 0, 0)),
                         pl.BlockSpec((di, d), lambda i: (0, 0))],
               out_specs=[row, pl.BlockSpec((1, 1, d), lambda i: (i // per, 0, 0)),
                          pl.BlockSpec((tm, di), lambda i: (i, 0))],
               sem=("arbitrary",))(dxn, yout, mod, w_out)


def _mm_dw_out(ybr, dy, name, comm=None):
    m, di = ybr.shape
    d = dy.shape[1]
    tn = _tile(di, 512)

    def body(y_ref, dy_ref, o_ref):
        o_ref[...] = _dot_tn(y_ref[...], dy_ref[...])

    return _pc(body, name=name, out_shape=jax.ShapeDtypeStruct((di, d), F32), grid=(di // tn,),
               in_specs=[pl.BlockSpec((m, tn), lambda n: (0, n)), pl.BlockSpec((m, d), lambda n: (0, 0))],
               out_specs=pl.BlockSpec((tn, d), lambda n: (n, 0)), sem=("parallel",), comm=comm)(ybr, dy)


def _sgu_mask():
    t = lax.broadcasted_iota(jnp.int32, (SG_BLOCK, SG_BLOCK), 0)
    s = lax.broadcasted_iota(jnp.int32, (SG_BLOCK, SG_BLOCK), 1)
    return (s // CHUNK) <= (t // CHUNK)


def _a_mid_fwd(proj, ln_g, ln_b, w_s, bs_t, t_seq, comm=None):
    m, n3 = proj.shape
    di = n3 // 3
    gd = di // SG_GROUPS
    r = _tile(t_seq, 256)
    nblk = r // SG_BLOCK

    def body(p_ref, lg_ref, lb_ref, ws_ref, bs_ref, ybr_ref, s_scr):
        v = _gelu(p_ref[:, di:2 * di])
        mu = jnp.mean(v, axis=-1, keepdims=True)
        vc = v - mu
        rstd = lax.rsqrt(jnp.mean(vc * vc, axis=-1, keepdims=True) + EPS)
        vb = (vc * rstd * lg_ref[...] + lb_ref[...]).astype(BF16)
        mask = _sgu_mask()
        for gi in range(SG_GROUPS):
            ws = jnp.where(mask, ws_ref[gi], 0.0).astype(BF16)
            bcol = bs_ref[:, gi:gi + 1]
            for b in range(nblk):
                rows = slice(b * SG_BLOCK, (b + 1) * SG_BLOCK)
                cols = slice(gi * gd, (gi + 1) * gd)
                s_scr[rows, cols] = _dot(ws, vb[rows, cols]) + bcol
        gg = p_ref[:, 2 * di:]
        ybr_ref[...] = (_gelu(p_ref[:, :di]) * s_scr[...] * (gg * _sigmoid(gg))).astype(BF16)

    vec = pl.BlockSpec((1, di), lambda i: (0, 0))
    return _pc(body, name="a_mid_fwd", out_shape=jax.ShapeDtypeStruct((m, di), BF16), grid=(m // r,),
               in_specs=[pl.BlockSpec((r, n3), lambda i: (i, 0)), vec, vec,
                         pl.BlockSpec((SG_GROUPS, SG_BLOCK, SG_BLOCK), lambda i: (0, 0, 0)),
                         pl.BlockSpec((SG_BLOCK, 128), lambda i: (0, 0))],
               out_specs=pl.BlockSpec((r, di), lambda i: (i, 0)),
               scratch=[pltpu.VMEM((r, di), F32)], sem=("parallel",), comm=comm)(proj, ln_g, ln_b, w_s, bs_t)


def _a_mid_bwd(proj, dybr, ln_g, ln_b, w_s, bs_t, t_seq, comm=None):
    m, n3 = proj.shape
    di = n3 // 3
    gd = di // SG_GROUPS
    r = _tile(t_seq, 256)
    nblk = r // SG_BLOCK

    def body(p_ref, dy_ref, lg_ref, lb_ref, ws_ref, bs_ref,
             dp_ref, dlg_ref, dlb_ref, dws_ref, dbs_ref, s_scr, dvl_scr):
        i = pl.program_id(0)

        @pl.when(i == 0)
        def _():
            dlg_ref[...] = jnp.zeros_like(dlg_ref)
            dlb_ref[...] = jnp.zeros_like(dlb_ref)
            dws_ref[...] = jnp.zeros_like(dws_ref)
            dbs_ref[...] = jnp.zeros_like(dbs_ref)

        v, dgelu_v = _gelu_and_grad(p_ref[:, di:2 * di])
        mu = jnp.mean(v, axis=-1, keepdims=True)
        vc = v - mu
        rstd = lax.rsqrt(jnp.mean(vc * vc, axis=-1, keepdims=True) + EPS)
        vhat = vc * rstd
        lg = lg_ref[...]
        vb = (vhat * lg + lb_ref[...]).astype(BF16)
        u, dgelu_u = _gelu_and_grad(p_ref[:, :di])
        gg = p_ref[:, 2 * di:]
        sg = _sigmoid(gg)
        dyv = dy_ref[...]
        dus = dyv * (gg * sg)
        dsb = (dus * u).astype(BF16)
        ds32 = dus * u
        mask = _sgu_mask()
        lane = lax.broadcasted_iota(jnp.int32, (SG_BLOCK, 128), 1)
        dbs_acc = jnp.zeros((SG_BLOCK, 128), F32)
        for gi in range(SG_GROUPS):
            ws = jnp.where(mask, ws_ref[gi], 0.0).astype(BF16)
            bcol = bs_ref[:, gi:gi + 1]
            cols = slice(gi * gd, (gi + 1) * gd)
            dws_acc = jnp.zeros((SG_BLOCK, SG_BLOCK), F32)
            dbs_col = jnp.zeros((SG_BLOCK, 1), F32)
            for b in range(nblk):
                rows = slice(b * SG_BLOCK, (b + 1) * SG_BLOCK)
                s_scr[rows, cols] = _dot(ws, vb[rows, cols]) + bcol
                dvl_scr[rows, cols] = _dot_tn(ws, dsb[rows, cols])
                dws_acc += _dot_nt(dsb[rows, cols], vb[rows, cols])
                dbs_col += jnp.sum(ds32[rows, cols], axis=-1, keepdims=True)
            dws_ref[gi] += jnp.where(mask, dws_acc, 0.0)
            dbs_acc += jnp.where(lane == gi, dbs_col, 0.0)
        dbs_ref[...] += dbs_acc
        s = s_scr[...]
        dp_ref[:, :di] = (dus * s * dgelu_u).astype(BF16)
        dp_ref[:, 2 * di:] = (dyv * u * s * (sg * (1.0 + gg * (1.0 - sg)))).astype(BF16)
        dvl = dvl_scr[...]
        dlg_ref[...] += jnp.sum(dvl * vhat, axis=0, keepdims=True)
        dlb_ref[...] += jnp.sum(dvl, axis=0, keepdims=True)
        dvh = dvl * lg
        dv = rstd * (dvh - jnp.mean(dvh, axis=-1, keepdims=True)
                     - vhat * jnp.mean(dvh * vhat, axis=-1, keepdims=True))
        dp_ref[:, di:2 * di] = (dv * dgelu_v).astype(BF16)

    vec = pl.BlockSpec((1, di), lambda i: (0, 0))
    wsb = pl.BlockSpec((SG_GROUPS, SG_BLOCK, SG_BLOCK), lambda i: (0, 0, 0))
    bsb = pl.BlockSpec((SG_BLOCK, 128), lambda i: (0, 0))
    return _pc(body, name="a_mid_bwd",
               out_shape=[jax.ShapeDtypeStruct((m, n3), BF16), jax.ShapeDtypeStruct((1, di), F32),
                          jax.ShapeDtypeStruct((1, di), F32),
                          jax.ShapeDtypeStruct((SG_GROUPS, SG_BLOCK, SG_BLOCK), F32),
                          jax.ShapeDtypeStruct((SG_BLOCK, 128), F32)],
               grid=(m // r,),
               in_specs=[pl.BlockSpec((r, n3), lambda i: (i, 0)), pl.BlockSpec((r, di), lambda i: (i, 0)),
                         vec, vec, wsb, bsb],
               out_specs=[pl.BlockSpec((r, n3), lambda i: (i, 0)), vec, vec, wsb, bsb],
               scratch=[pltpu.VMEM((r, di), F32), pltpu.VMEM((r, di), F32)],
               sem=("arbitrary",), comm=comm)(proj, dybr, ln_g, ln_b, w_s, bs_t)


def _chunk_rows(n):
    if isinstance(n, int):
        return pl.ds(n * CHUNK, CHUNK)
    return pl.ds(pl.multiple_of(n * CHUNK, CHUNK), CHUNK)


def _hgrn_dims(t_seq, di):
    tr = _tile(t_seq, 128)
    hc = _tile(di, 2048)
    return tr, hc, hc // HEAD_DIM


def _hgrn_gates(f_ref, lb, a_scr, k_scr, tr):
    sig = _sigmoid(f_ref[...])
    fg = lb + (1.0 - lb) * sig
    k_scr[...] = 1.0 - fg
    logf = jnp.log(fg)
    g = min(CUM_ROWS, tr)
    tri = _tri_mask(g, reverse=False)
    for rg in range(tr // g):
        a_scr[rg * g:(rg + 1) * g, :] = _tri_apply(tri, logf[rg * g:(rg + 1) * g, :])
    return sig, fg


def _hgrn_fwd(proj, lbj, gn, nb, t_seq):
    _, m, di = proj.shape
    tr, hc, hpg = _hgrn_dims(t_seq, di)
    nt, nhg, ncl = t_seq // tr, di // hc, tr // CHUNK
    nheads = di // HEAD_DIM

    nbuf, nsteps = 4, nhg * nb * nt

    def body(p_hbm, lb_ref, gn_ref, o_ref, ybr_ref, st_ref, st_scr, a_scr, k_scr, buf, sems):
        t = pl.program_id(2)
        step = (pl.program_id(0) * nb + pl.program_id(1)) * nt + t

        def fetch(s):
            rt, hg = s % (nb * nt), s // (nb * nt)
            return pltpu.make_async_copy(p_hbm.at[:, pl.ds(rt * tr, tr), pl.ds(hg * hc, hc)],
                                         buf.at[s % nbuf], sems.at[s % nbuf])

        @pl.when(step == 0)
        def _():
            for s in range(min(nbuf - 1, nsteps)):
                fetch(s).start()

        @pl.when(step + nbuf - 1 < nsteps)
        def _():
            fetch(step + nbuf - 1).start()

        fetch(step).wait()
        p_ref = buf.at[step % nbuf]
        q_ref, f_ref, i_ref, g_ref = (p_ref.at[s] for s in range(4))

        @pl.when(t == 0)
        def _():
            st_scr[...] = jnp.zeros_like(st_scr)

        _hgrn_gates(f_ref, lb_ref[0:1, :], a_scr, k_scr, tr)
        gnv = gn_ref[...]
        rr = lax.broadcasted_iota(jnp.int32, (CHUNK, CHUNK), 0)
        cc = lax.broadcasted_iota(jnp.int32, (CHUNK, CHUNK), 1)
        causal = cc <= rr

        def chunk(n, carry):
            rows = _chunk_rows(n)
            lanes = [slice(hd * HEAD_DIM, (hd + 1) * HEAD_DIM) for hd in range(hpg)]
            hs = []
            for hd, ls in enumerate(lanes):
                h = {}
                ah, kh = a_scr[rows, ls], k_scr[rows, ls]
                qp = q_ref[rows, ls]
                qh = qp * _sigmoid(qp)
                h["vb"] = i_ref[rows, ls].astype(BF16)
                aref, alast = ah[CHUNK // 2 - 1:CHUNK // 2, :], ah[CHUNK - 1:CHUNK, :]
                h["q_in"] = (qh * jnp.exp(ah - aref)).astype(BF16)
                h["k_in"] = (kh * jnp.exp(aref - ah)).astype(BF16)
                h["q_out"] = (qh * jnp.exp(ah)).astype(BF16)
                h["k_out"] = (kh * jnp.exp(alast - ah)).astype(BF16)
                h["dec"] = jnp.exp(alast)
                st = st_scr[hd]
                st_ref[n, hd] = st
                h["st"] = st
                hs.append(h)
            for h in hs:
                h["scores"] = _dot_nt(h["q_in"], h["k_in"])
                h["o_inter"] = _dot_nt(h["q_out"], h["st"].astype(BF16))
                h["st_mm"] = _dot_tn(h["vb"], h["k_out"])
            for h in hs:
                h["o"] = _dot(jnp.where(causal, h["scores"], 0.0).astype(BF16), h["vb"]) + h["o_inter"]
            for hd, (h, ls) in enumerate(zip(hs, lanes)):
                st_scr[hd] = h["st"] * h["dec"] + h["st_mm"]
                o = h["o"]
                o_ref[rows, ls] = o
                rstd = lax.rsqrt(jnp.mean(o * o, axis=-1, keepdims=True) + EPS)
                gg = g_ref[rows, ls]
                ybr_ref[rows, ls] = ((o * rstd * gnv) * (gg * _sigmoid(gg))).astype(BF16)
            return carry

        lax.fori_loop(0, ncl, chunk, 0)

    blk = pl.BlockSpec((tr, hc), lambda hg, b, t: (b * nt + t, hg))
    return _pc(body, name="hgrn_fwd",
               out_shape=[jax.ShapeDtypeStruct((m, di), F32), jax.ShapeDtypeStruct((m, di), BF16),
                          jax.ShapeDtypeStruct((m // CHUNK, nheads, HEAD_DIM, HEAD_DIM), F32)],
               grid=(nhg, nb, nt),
               in_specs=[ANY, pl.BlockSpec((2, hc), lambda hg, b, t: (0, hg)),
                         pl.BlockSpec((1, HEAD_DIM), lambda hg, b, t: (0, 0))],
               out_specs=[blk, blk, pl.BlockSpec((ncl, hpg, HEAD_DIM, HEAD_DIM),
                                                 lambda hg, b, t: (b * nt + t, hg, 0, 0))],
               scratch=[pltpu.VMEM((hpg, HEAD_DIM, HEAD_DIM), F32), pltpu.VMEM((tr, hc), F32),
                        pltpu.VMEM((tr, hc), F32), pltpu.VMEM((nbuf, 4, tr, hc), F32),
                        pltpu.SemaphoreType.DMA((nbuf,))],
               sem=("arbitrary", "arbitrary", "arbitrary"))(proj, lbj, gn)


def _hgrn_bwd(proj, o_all, dybr, states, lbj, gn, nb, t_seq, comm=None):
    _, m, di = proj.shape
    tr, hc, hpg = _hgrn_dims(t_seq, di)
    nt, nhg, ncl = t_seq // tr, di // hc, tr // CHUNK

    def body(p_ref, o_ref, dy_ref, st_ref, lb_ref, gn_ref,
             dp_ref, dlb_ref, dgn_ref, dst_scr, a_scr, k_scr, da_scr, dk_scr):
        q_ref, f_ref, i_ref, g_ref = (p_ref.at[s] for s in range(4))
        hg, b, t = pl.program_id(0), pl.program_id(1), pl.program_id(2)

        @pl.when(t == 0)
        def _():
            dst_scr[...] = jnp.zeros_like(dst_scr)

        @pl.when((b == 0) & (t == 0))
        def _():
            dlb_ref[...] = jnp.zeros_like(dlb_ref)

        @pl.when((hg == 0) & (b == 0) & (t == 0))
        def _():
            dgn_ref[...] = jnp.zeros_like(dgn_ref)

        lb = lb_ref[0:1, :]
        sig, fg = _hgrn_gates(f_ref, lb, a_scr, k_scr, tr)
        gnv = gn_ref[...]
        rr = lax.broadcasted_iota(jnp.int32, (CHUNK, CHUNK), 0)
        cc = lax.broadcasted_iota(jnp.int32, (CHUNK, CHUNK), 1)
        causal = cc <= rr
        rowi = lax.broadcasted_iota(jnp.int32, (CHUNK, HEAD_DIM), 0)

        def chunk(it, carry):
            n = ncl - 1 - it
            rows = _chunk_rows(n)
            for hd0 in range(0, hpg, PHASE_HEADS):
                heads(n, rows, range(hd0, min(hpg, hd0 + PHASE_HEADS)))
            return carry

        def heads(n, rows, ids):
            lanes = [slice(hd * HEAD_DIM, (hd + 1) * HEAD_DIM) for hd in ids]
            hs = []
            for hd, ls in zip(ids, lanes):
                h = {}
                ah, kh = a_scr[rows, ls], k_scr[rows, ls]
                qp = q_ref[rows, ls]
                sq = _sigmoid(qp)
                qh = qp * sq
                h["dsilu_q"] = sq * (1.0 + qp * (1.0 - sq))
                h["vb"] = i_ref[rows, ls].astype(BF16)
                aref, alast = ah[CHUNK // 2 - 1:CHUNK // 2, :], ah[CHUNK - 1:CHUNK, :]
                h["e1"], h["e2"] = jnp.exp(ah - aref), jnp.exp(aref - ah)
                h["e3"], h["e4"] = jnp.exp(ah), jnp.exp(alast - ah)
                h["dec"] = jnp.exp(alast)
                h["q_in"], h["k_in"], h["q_out"], h["k_out"] = qh * h["e1"], kh * h["e2"], qh * h["e3"], kh * h["e4"]
                for nm in ("q_in", "k_in", "q_out", "k_out"):
                    h[nm + "_b"] = h[nm].astype(BF16)
                o = o_ref[rows, ls]
                rstd = lax.rsqrt(jnp.mean(o * o, axis=-1, keepdims=True) + EPS)
                ohat = o * rstd
                gg = g_ref[rows, ls]
                sg = _sigmoid(gg)
                dyv = dy_ref[rows, ls]
                d_on = dyv * (gg * sg)
                dp_ref[3, rows, ls] = (dyv * (ohat * gnv) * (sg * (1.0 + gg * (1.0 - sg)))).astype(BF16)
                h["dgn"] = jnp.sum(d_on * ohat, axis=0, keepdims=True)
                dohat = d_on * gnv
                do = rstd * (dohat - ohat * jnp.mean(dohat * ohat, axis=-1, keepdims=True))
                h["do_b"] = do.astype(BF16)
                h["st_prev"] = st_ref[n, hd]
                h["dst"] = dst_scr[hd]
                hs.append(h)
            for h in hs:
                dst_b = h["dst"].astype(BF16)
                h["scores"] = _dot_nt(h["q_in_b"], h["k_in_b"])
                h["dscores"] = _dot_nt(h["do_b"], h["vb"])
                h["dv_inter"] = _dot_nt(h["k_out_b"], dst_b)
                h["dq_out"] = _dot(h["do_b"], h["st_prev"].astype(BF16))
                h["dk_out"] = _dot(h["vb"], dst_b)
                h["dst_mm"] = _dot_tn(h["do_b"], h["q_out_b"])
            for h in hs:
                scores = jnp.where(causal, h["scores"], 0.0).astype(BF16)
                dscores = jnp.where(causal, h["dscores"], 0.0).astype(BF16)
                h["dv"] = _dot_tn(scores, h["do_b"]) + h["dv_inter"]
                h["dq_in"] = _dot(dscores, h["k_in_b"])
                h["dk_in"] = _dot_tn(dscores, h["q_in_b"])
            dgn = hs[0]["dgn"]
            for h in hs[1:]:
                dgn = dgn + h["dgn"]
            dgn_ref[...] += dgn
            for hd, h, ls in zip(ids, hs, lanes):
                ddec = jnp.sum(h["dst"] * h["st_prev"], axis=0, keepdims=True)
                dst_scr[hd] = h["dst"] * h["dec"] + h["dst_mm"]
                dp_ref[2, rows, ls] = h["dv"].astype(BF16)
                dq = h["dq_in"] * h["e1"] + h["dq_out"] * h["e3"]
                dp_ref[0, rows, ls] = (dq * h["dsilu_q"]).astype(BF16)
                dk_scr[rows, ls] = h["dk_in"] * h["e2"] + h["dk_out"] * h["e4"]
                t_in = h["dq_in"] * h["q_in"] - h["dk_in"] * h["k_in"]
                t_out = h["dk_out"] * h["k_out"]
                da = t_in + h["dq_out"] * h["q_out"] - t_out
                da_ref_row = -jnp.sum(t_in, axis=0, keepdims=True)
                da_last_row = jnp.sum(t_out, axis=0, keepdims=True) + ddec * h["dec"]
                da = da + jnp.where(rowi == CHUNK // 2 - 1, da_ref_row, 0.0) \
                        + jnp.where(rowi == CHUNK - 1, da_last_row, 0.0)
                da_scr[rows, ls] = da

        if ncl <= 2:
            for it in range(ncl):
                chunk(it, 0)
        else:
            lax.fori_loop(0, ncl, chunk, 0)
        g = min(CUM_ROWS, tr)
        tri = _tri_mask(g, reverse=True)
        for rg in range(tr // g):
            rs = slice(rg * g, (rg + 1) * g)
            dlogf = _tri_apply(tri, da_scr[rs, :])
            df = dlogf / fg[rs, :] - dk_scr[rs, :]
            sgr = sig[rs, :]
            dp_ref[1, rs, :] = (df * (1.0 - lb) * (sgr * (1.0 - sgr))).astype(BF16)
            dlb_ref[...] += jnp.sum(df * (1.0 - sgr), axis=0, keepdims=True) * lb_ref[1:2, :]

    blk = pl.BlockSpec((tr, hc), lambda hg, b, t: (b * nt + (nt - 1 - t), hg))
    return _pc(body, name="hgrn_bwd",
               out_shape=[jax.ShapeDtypeStruct((4, m, di), BF16), jax.ShapeDtypeStruct((1, di), F32),
                          jax.ShapeDtypeStruct((1, HEAD_DIM), F32)],
               grid=(nhg, nb, nt),
               in_specs=[pl.BlockSpec((4, tr, hc), lambda hg, b, t: (0, b * nt + (nt - 1 - t), hg)), blk, blk,
                         pl.BlockSpec((ncl, hpg, HEAD_DIM, HEAD_DIM),
                                      lambda hg, b, t: (b * nt + (nt - 1 - t), hg, 0, 0)),
                         pl.BlockSpec((2, hc), lambda hg, b, t: (0, hg)),
                         pl.BlockSpec((1, HEAD_DIM), lambda hg, b, t: (0, 0))],
               out_specs=[pl.BlockSpec((4, tr, hc), lambda hg, b, t: (0, b * nt + (nt - 1 - t), hg)),
                          pl.BlockSpec((1, hc), lambda hg, b, t: (0, hg)),
                          pl.BlockSpec((1, HEAD_DIM), lambda hg, b, t: (0, 0))],
               scratch=[pltpu.VMEM((hpg, HEAD_DIM, HEAD_DIM), F32)] + [pltpu.VMEM((tr, hc), F32)] * 4,
               sem=("arbitrary", "arbitrary", "arbitrary"), comm=comm)(
                   proj, o_all, dybr, states, lbj, gn)


def _adamw(parts, w, m, v, name):
    r, c = w.shape
    tr = _tile(r, 256)
    npart = len(parts)
    c1 = 1.0 - ADAM_B1 ** ADAM_STEP
    c2 = 1.0 - ADAM_B2 ** ADAM_STEP

    def body(*refs):
        p_refs = refs[:npart]
        _adamw_math(p_refs, *refs[npart:], c1, c2)

    blk = pl.BlockSpec((tr, c), lambda i: (i, 0))
    return _pc(body, name=name, out_shape=[jax.ShapeDtypeStruct((r, c), F32)] * 4, grid=(r // tr,),
               in_specs=[blk] * (npart + 3), out_specs=[blk] * 4, sem=("parallel",))(*parts, w, m, v)


def _adamw_math(p_refs, w_ref, m_ref, v_ref, g_ref, d_ref, nm_ref, nv_ref, c1, c2):
    g = p_refs[0][...].astype(F32)
    for p in p_refs[1:]:
        g = g + p[...].astype(F32)
    nm = ADAM_B1 * m_ref[...] + (1.0 - ADAM_B1) * g
    nv = ADAM_B2 * v_ref[...] + (1.0 - ADAM_B2) * (g * g)
    g_ref[...] = g
    nm_ref[...] = nm
    nv_ref[...] = nv
    d_ref[...] = -ADAM_LR * ((nm / c1) / (jnp.sqrt(nv / c2) + ADAM_EPS) + ADAM_WD * w_ref[...])


def _adamw_small(gathered, ws, ms, vs, name, sums=()):
    n, ns = len(ws), len(sums)
    c1 = 1.0 - ADAM_B1 ** ADAM_STEP
    c2 = 1.0 - ADAM_B2 ** ADAM_STEP

    def total(ref):
        g = ref[0]
        for part in range(1, ref.shape[0]):
            g = g + ref[part]
        return g

    def body(*refs):
        g_in, w_in, m_in, v_in = refs[:n], refs[n:2 * n], refs[2 * n:3 * n], refs[3 * n:4 * n]
        s_in = refs[4 * n:4 * n + ns]
        outs = refs[4 * n + ns:]
        for k in range(n):
            g = total(g_in[k])
            nm = ADAM_B1 * m_in[k][...] + (1.0 - ADAM_B1) * g
            nv = ADAM_B2 * v_in[k][...] + (1.0 - ADAM_B2) * (g * g)
            outs[4 * k][...] = g
            outs[4 * k + 1][...] = -ADAM_LR * ((nm / c1) / (jnp.sqrt(nv / c2) + ADAM_EPS) + ADAM_WD * w_in[k][...])
            outs[4 * k + 2][...] = nm
            outs[4 * k + 3][...] = nv
        for k in range(ns):
            outs[4 * n + k][...] = total(s_in[k])

    out_shape = [jax.ShapeDtypeStruct(w.shape, F32) for w in ws for _ in range(4)]
    out_shape += [jax.ShapeDtypeStruct(s.shape[1:], F32) for s in sums]
    res = _pc(body, name=name, out_shape=out_shape)(*gathered, *ws, *ms, *vs, *sums)
    return [res[4 * k:4 * k + 4] for k in range(n)] + list(res[4 * n:])


def _adamw_blocks(parts, idx, w, m, v, name):
    r, c = w.shape
    tr = _tile(r, 256)
    npart = len(parts)
    c1 = 1.0 - ADAM_B1 ** ADAM_STEP
    c2 = 1.0 - ADAM_B2 ** ADAM_STEP

    def body(idx_ref, *refs):
        _adamw_math(refs[:npart], *refs[npart:], c1, c2)

    def sel(p):
        return pl.BlockSpec((None, tr, c), lambda i, s: (s[p], i, 0))

    blk = pl.BlockSpec((tr, c), lambda i, s: (i, 0))
    gs = pltpu.PrefetchScalarGridSpec(num_scalar_prefetch=1, grid=(r // tr,),
                                      in_specs=[sel(p) for p in range(npart)] + [blk] * 3, out_specs=[blk] * 4)
    return _pc(body, name=name, out_shape=[jax.ShapeDtypeStruct((r, c), F32)] * 4, grid_spec=gs,
               sem=("parallel",))(idx, *parts, w, m, v)


_EARLY = ["a_ln_gain", "a_ln_bias", "a_w_s", "a_b_s", "b_lower_bounds", "b_gn_gain"]


def kernel(x, c, norm_gain, w_ada, b_ada, a_w_in, a_ln_gain, a_ln_bias, a_w_s, a_b_s, a_w_out, b_w_in, b_lower_bounds, b_gn_gain, b_w_out, final_gain, loss_target, m_norm_gain, m_w_ada, m_b_ada, m_a_w_in, m_a_ln_gain, m_a_ln_bias, m_a_w_s, m_a_b_s, m_a_w_out, m_b_w_in, m_b_lower_bounds, m_b_gn_gain, m_b_w_out, m_final_gain, v_norm_gain, v_w_ada, v_b_ada, v_a_w_in, v_a_ln_gain, v_a_ln_bias, v_a_w_s, v_a_b_s, v_a_w_out, v_b_w_in, v_b_lower_bounds, v_b_gn_gain, v_b_w_out, v_final_gain):
    w = dict(norm_gain=norm_gain, w_ada=w_ada, b_ada=b_ada, a_w_in=a_w_in, a_ln_gain=a_ln_gain,
             a_ln_bias=a_ln_bias, a_w_s=a_w_s, a_b_s=a_b_s, a_w_out=a_w_out, b_w_in=b_w_in,
             b_lower_bounds=b_lower_bounds, b_gn_gain=b_gn_gain, b_w_out=b_w_out, final_gain=final_gain)
    mo = dict(norm_gain=m_norm_gain, w_ada=m_w_ada, b_ada=m_b_ada, a_w_in=m_a_w_in, a_ln_gain=m_a_ln_gain,
              a_ln_bias=m_a_ln_bias, a_w_s=m_a_w_s, a_b_s=m_a_b_s, a_w_out=m_a_w_out, b_w_in=m_b_w_in,
              b_lower_bounds=m_b_lower_bounds, b_gn_gain=m_b_gn_gain, b_w_out=m_b_w_out, final_gain=m_final_gain)
    vo = dict(norm_gain=v_norm_gain, w_ada=v_w_ada, b_ada=v_b_ada, a_w_in=v_a_w_in, a_ln_gain=v_a_ln_gain,
              a_ln_bias=v_a_ln_bias, a_w_s=v_a_w_s, a_b_s=v_a_b_s, a_w_out=v_a_w_out, b_w_in=v_b_w_in,
              b_lower_bounds=v_b_lower_bounds, b_gn_gain=v_b_gn_gain, b_w_out=v_b_w_out, final_gain=v_final_gain)

    nb, t_seq, d = x.shape
    m = nb * t_seq
    ncol_ada = w_ada.shape[2]
    xi, yi, ci = lax.axis_index("x"), lax.axis_index("y"), lax.axis_index("c")
    me = 4 * xi + 2 * yi + ci

    c_g, wa_in_g = _all_gather([c, a_w_in[0].astype(BF16)], "gather_c_wa")

    c_all = c_g.reshape(NDEV * nb, d)
    b_cols = lax.dynamic_slice(b_ada, (0, me * ncol_ada), (2, ncol_ada)).reshape(2, 1, ncol_ada)
    mod_part, lbj = _ada_fwd(c_all, w_ada, b_cols, b_lower_bounds)
    mod_all = _all_gather([mod_part], "gather_mod")[0]
    mod_mine = lax.dynamic_slice_in_dim(mod_all, me * nb, nb, axis=2)
    mod_mine = mod_mine.transpose(1, 2, 0, 3).reshape(2, nb, 3, d)
    mod0, mod1 = mod_mine[0], mod_mine[1]

    di = a_w_out.shape[1] * NDEV

    xf = x.reshape(m, d)
    tgt = loss_target.reshape(m, d)
    ng0, ng1 = norm_gain[0:1], norm_gain[1:2]
    ncb = b_w_in.shape[2]
    wb_lo, wb_hi = b_w_in[0][:, :ncb // 2].astype(BF16), b_w_in[0][:, ncb // 2:].astype(BF16)
    h0, h0_t = _prenorm(xf, ng0, mod0, t_seq, "prenorm_a")
    proj_a, half = _mm_in(h0, [wa_in_g], 1, "in_proj_a", comm=_gather_first([a_w_out[0].astype(BF16), wb_lo]))
    bs_t = jnp.pad(a_b_s[0].T, ((0, 0), (0, 128 - SG_GROUPS)))
    ybr_a, (wa_out_g, wb_lo_g, wb_hi_half) = _a_mid_fwd(
        proj_a, a_ln_gain, a_ln_bias, a_w_s[0], bs_t, t_seq, comm=_join(_gather_second(half), _gather_first([wb_hi])))
    wa_out = wa_out_g.reshape(di, d)
    (yout_a, x1), (wb_hi_g, wb_out_half) = _out_proj(
        ybr_a, wa_out, xf, mod0, t_seq, "out_proj_a",
        comm=_join(_gather_second([wb_hi_half]), _gather_first([b_w_out[0].astype(BF16)])))
    wb_in_g = [wb_lo_g, wb_hi_g]
    h1, h1_t = _prenorm(x1, ng1, mod1, t_seq, "prenorm_b")
    proj_b, (wb_out_g,) = _mm_in(h1, wb_in_g, 4, "in_proj_b", comm=_gather_second([wb_out_half]))
    wb_out = wb_out_g.reshape(di, d)
    o_b, ybr_b, states = _hgrn_fwd(proj_b, lbj, b_gn_gain, nb, t_seq)
    yout_b, dx2, loss_part, d_final_gain = _out_proj_loss(ybr_b, wb_out, x1, mod1, final_gain.reshape(1, d), tgt, t_seq)

    rows_out = a_w_out.shape[1]
    dy_b, dgate1, dybr_b = _gate_dybr(dx2, yout_b, mod1, wb_out, t_seq, "dybr_b")
    rs_wb_out = _ReduceScatter(_mm_dw_out(ybr_b, dy_b, "dw_out_b").reshape(NDEV, rows_out, d), "b_w_out")
    (dproj_b, d_lb, d_gn), got = _hgrn_bwd(proj_b, o_b, dybr_b, states, lbj, b_gn_gain, nb, t_seq,
                                           comm=rs_wb_out.swap_core())
    rs_wb_out.after_core(got[0])
    dh1, got = _mm_din(dproj_b, wb_in_g, 4, "dh_b", comm=rs_wb_out.swap_chips())
    rs_wb_out.after_chips(got[0])
    dx1, dss1, dgain1 = _prenorm_bwd(dh1, x1, ng1, mod1, dx2, t_seq, "prenorm_bwd_b")
    rs_wb_in = _ReduceScatter(_mm_dw_in(h1_t, dproj_b, ncb, 4, "dw_in_b"), "b_w_in")

    dy_a, dgate0, dybr_a = _gate_dybr(dx1, yout_a, mod0, wa_out, t_seq, "dybr_a")
    g_wa_out, got = _mm_dw_out(ybr_a, dy_a, "dw_out_a", comm=rs_wb_in.swap_core())
    rs_wb_in.after_core(got[0])
    rs_wa_out = _ReduceScatter(g_wa_out.reshape(NDEV, rows_out, d), "a_w_out")
    (dproj_a, d_lng, d_lnb, d_ws, d_bs_t), got = _a_mid_bwd(
        proj_a, dybr_a, a_ln_gain, a_ln_bias, a_w_s[0], bs_t, t_seq,
        comm=_join(rs_wb_in.swap_chips(), rs_wa_out.swap_core()))
    rs_wb_in.after_chips(got[0])
    rs_wa_out.after_core(got[1])
    early_parts = [d_lng, d_lnb, d_ws.reshape(SG_GROUPS * SG_BLOCK, SG_BLOCK), d_bs_t[:, :SG_GROUPS].T,
                   jnp.concatenate([-d_lb, d_lb], axis=0), d_gn]
    g_wa_in, got = _mm_dw_in(h0_t, dproj_a, wa_in_g.shape[2], 1, "dw_in_a",
                             comm=_join(rs_wa_out.swap_chips(), _gather_first(early_parts)))
    rs_wa_out.after_chips(got[0])
    rs_wa_in = _ReduceScatter(g_wa_in, "a_w_in")
    n_tiles = m // _din_tile(m)
    assert n_tiles >= 2
    first_tiles = max(1, (3 * n_tiles) // 8)
    dh0, got2 = _mm_din(dproj_a, [wa_in_g], 1, "dh_a_first", tiles=(0, first_tiles),
                        comm=_join(rs_wa_in.swap_core(), _gather_second(got[1:])))
    rs_wa_in.after_core(got2[0])
    early_all = got2[1:]
    dh0, got = _mm_din(dproj_a, [wa_in_g], 1, "dh_a_rest", comm=rs_wa_in.swap_chips(),
                       tiles=(first_tiles, n_tiles - first_tiles), prev=dh0)
    rs_wa_in.after_chips(got[0])
    dx0, dss0, dgain0 = _prenorm_bwd(dh0, xf, ng0, mod0, dx1, t_seq, "prenorm_bwd_a")
    grad_x = dx0.reshape(nb, t_seq, d)

    dmod = jnp.stack([jnp.concatenate([dss0, dgate0], axis=1), jnp.concatenate([dss1, dgate1], axis=1)])
    dmod_all, dgain_all, dfinal_all, loss_all = _all_gather(
        [dmod.reshape(2, nb, 3 * d), jnp.concatenate([dgain0, dgain1], axis=0), d_final_gain,
         jnp.broadcast_to(loss_part, (1, 128))], "gather_tail")
    dmod_all = dmod_all.transpose(1, 0, 2, 3).reshape(2, NDEV * nb, 3 * d)
    dmod_cols = lax.dynamic_slice_in_dim(dmod_all, me * ncol_ada, ncol_ada, axis=2)
    g_w_ada, g_b_ada = _ada_bwd(c_all, dmod_cols, dmod_all)

    def small2d(k, t):
        return t[k].reshape(early_parts[_EARLY.index(k)].shape if k in _EARLY else (-1, t[k].shape[-1]))

    res = {}
    sm = _adamw_small(early_all, *[[small2d(k, t) for k in _EARLY] for t in (w, mo, vo)], "adamw_small_early")
    for k, r in zip(_EARLY, sm):
        res[k] = tuple(z.reshape(w[k].shape) for z in r)
    late = ["norm_gain", "final_gain", "b_ada"]
    sm = _adamw_small([dgain_all, dfinal_all, g_b_ada[None]], *[[small2d(k, t) for k in late] for t in (w, mo, vo)],
                      "adamw_small_late", sums=[loss_all])
    for k, r in zip(late, sm):
        res[k] = tuple(z.reshape(w[k].shape) for z in r)
    loss = sm[3][0, 0]
    sh = w_ada.shape
    ra = _adamw([g_w_ada.reshape(sh[0] * sh[1], sh[2])], w_ada.reshape(sh[0] * sh[1], sh[2]),
                mo["w_ada"].reshape(sh[0] * sh[1], sh[2]), vo["w_ada"].reshape(sh[0] * sh[1], sh[2]), "adamw_w_ada")
    res["w_ada"] = tuple(z.reshape(sh) for z in ra)

    for k, rs in (("b_w_out", rs_wb_out), ("b_w_in", rs_wb_in), ("a_w_out", rs_wa_out), ("a_w_in", rs_wa_in)):
        res[k] = tuple(z[None] for z in _adamw_blocks(rs.parts, rs.idx, w[k][0], mo[k][0], vo[k][0], "adamw_" + k))

    order = ["norm_gain", "w_ada", "b_ada", "a_w_in", "a_ln_gain", "a_ln_bias", "a_w_s", "a_b_s", "a_w_out",
             "b_w_in", "b_lower_bounds", "b_gn_gain", "b_w_out", "final_gain"]
    return (loss, grad_x, *[res[k][0] for k in order], *[res[k][1] for k in order],
            *[res[k][2] for k in order], *[res[k][3] for k in order])
```

```python
import functools
import math

import jax
import jax.numpy as jnp
from jax import lax
from jax.experimental import pallas as pl
from jax.experimental.pallas import tpu as pltpu

F32 = jnp.float32
BF16 = jnp.bfloat16
MESH = pl.DeviceIdType.MESH
NDEV = 8
EPS = 1e-6
CHUNK = 64
SG_BLOCK = 128
SG_GROUPS = 8
HEAD_DIM = 128
CUM_ROWS = 256
PHASE_HEADS = 8
ADAM_LR, ADAM_B1, ADAM_B2, ADAM_EPS, ADAM_WD, ADAM_STEP = 0.001, 0.9, 0.999, 1e-08, 0.01, 10
VMEM_LIMIT = 56 * 1024 * 1024
ANY = pl.BlockSpec(memory_space=pl.ANY)


class _Hosted:
    def __init__(self, arrays, out_shapes, nsem, start, finish, aliases=None):
        self.arrays, self.out_shapes, self.nsem = list(arrays), list(out_shapes), nsem
        self.start, self.finish = start, finish
        self.aliases = dict(aliases or {})


def _join(*comms):
    arrays, outs, aliases, offs, nsem = [], [], {}, [], 0
    for cm in comms:
        offs.append((len(arrays), len(outs), nsem))
        for i, o in cm.aliases.items():
            aliases[len(arrays) + i] = len(outs) + o
        arrays += cm.arrays
        outs += cm.out_shapes
        nsem += cm.nsem

    def run(which):
        def f(ins, outs_, ss, rs, base):
            for cm, (ia, io, isem) in zip(comms, offs):
                getattr(cm, which)(ins[ia:ia + len(cm.arrays)], outs_[io:io + len(cm.out_shapes)], ss, rs, base + isem)
        return f

    return _Hosted(arrays, outs, nsem, run("start"), run("finish"), aliases)


def _pc(body, *, name, out_shape, grid=None, in_specs=None, out_specs=None, scratch=(), sem=None,
        grid_spec=None, comm=None, aliases=None):
    cp = dict(vmem_limit_bytes=VMEM_LIMIT)
    aliases = dict(aliases or {})
    if comm is None:
        if sem is not None:
            cp["dimension_semantics"] = sem
        kw = {"input_output_aliases": aliases}
        if grid_spec is not None:
            kw["grid_spec"] = grid_spec
        else:
            if grid is not None:
                kw["grid"] = grid
            if in_specs is not None:
                kw["in_specs"] = in_specs
            if out_specs is not None:
                kw["out_specs"] = out_specs
            kw["scratch_shapes"] = list(scratch)
        return pl.pallas_call(functools.partial(body), name=name, out_shape=out_shape,
                              compiler_params=pltpu.CompilerParams(**cp), **kw)

    single = not isinstance(out_shape, (list, tuple))
    outs_list = [out_shape] if single else list(out_shape)
    ospecs = [out_specs] if single else list(out_specs)
    n_in, n_out, n_ci, n_co, n_scr = len(in_specs), len(outs_list), len(comm.arrays), len(comm.out_shapes), len(scratch)
    cp["dimension_semantics"] = ("arbitrary",) * len(grid)

    def hosted(*refs):
        cin, hin = refs[:n_in], refs[n_in:n_in + n_ci]
        cout = refs[n_in + n_ci:n_in + n_ci + n_out]
        hout = refs[n_in + n_ci + n_out:n_in + n_ci + n_out + n_co]
        scr = refs[n_in + n_ci + n_out + n_co:n_in + n_ci + n_out + n_co + n_scr]
        ssem, rsem = refs[-2], refs[-1]
        first = functools.reduce(lambda p, q: p & q, [pl.program_id(a) == 0 for a in range(len(grid))])
        last = functools.reduce(lambda p, q: p & q, [pl.program_id(a) == grid[a] - 1 for a in range(len(grid))])

        @pl.when(first)
        def _():
            comm.start(hin, hout, ssem, rsem, 0)

        body(*cin, *cout, *scr)

        @pl.when(last)
        def _():
            comm.finish(hin, hout, ssem, rsem, 0)

    call = pl.pallas_call(
        hosted, name=name, grid=grid, in_specs=list(in_specs) + [ANY] * n_ci, out_specs=ospecs + [ANY] * n_co,
        out_shape=outs_list + comm.out_shapes,
        scratch_shapes=list(scratch) + [pltpu.SemaphoreType.DMA((comm.nsem,)), pltpu.SemaphoreType.DMA((comm.nsem,))],
        input_output_aliases={**aliases, **{n_in + i: n_out + o for i, o in comm.aliases.items()}},
        compiler_params=pltpu.CompilerParams(**cp))

    def run(*args):
        res = call(*args, *comm.arrays)
        comp = res[:n_out]
        return (comp[0] if single else comp), list(res[n_out:])

    return run


def _tile(n, pref):
    return pref if n % pref == 0 else n


def _sigmoid(x):
    return 1.0 / (1.0 + jnp.exp(-x))


def _gelu(x):
    c = math.sqrt(2.0 / math.pi)
    return 0.5 * x * (1.0 + jnp.tanh(c * (x + 0.044715 * (x * x * x))))


def _gelu_and_grad(x):
    c = math.sqrt(2.0 / math.pi)
    x2 = x * x
    t = jnp.tanh(c * (x + 0.044715 * (x2 * x)))
    half = 0.5 * (1.0 + t)
    return x * half, half + (0.5 * x) * (1.0 - t * t) * (c + (3.0 * 0.044715 * c) * x2)


def _dot(a, b):
    return jnp.dot(a, b, preferred_element_type=F32)


def _dot_nt(a, b):
    return lax.dot_general(a, b, (((1,), (1,)), ((), ())), preferred_element_type=F32)


def _dot_tn(a, b):
    return lax.dot_general(a, b, (((0,), (0,)), ((), ())), preferred_element_type=F32)


def _tri_mask(n, reverse):
    r = lax.broadcasted_iota(jnp.int32, (n, n), 0)
    c = lax.broadcasted_iota(jnp.int32, (n, n), 1)
    same = (r // CHUNK) == (c // CHUNK)
    tri = (c >= r) if reverse else (c <= r)
    return jnp.where(same & tri, 1.0, 0.0).astype(BF16)


def _tri_apply(tri, x):
    hi = x.astype(BF16)
    r1 = x - hi.astype(F32)
    mid = r1.astype(BF16)
    lo = (r1 - mid.astype(F32)).astype(BF16)
    return _dot(tri, hi) + (_dot(tri, mid) + _dot(tri, lo))


def _all_gather(arrs, name):
    n = len(arrs)

    def body(*refs):
        ins, outs = refs[:n], refs[n:2 * n]
        send_sems, recv_sems, local_sems = refs[2 * n:]
        x, y, c = lax.axis_index("x"), lax.axis_index("y"), lax.axis_index("c")
        me, sibling = (x, y, c), (x, y, 1 - c)
        near = (x + c - 2 * x * c, y + (1 - c) - 2 * y * (1 - c))
        far = (x + (1 - c) - 2 * x * (1 - c), y + c - 2 * y * c)
        diag = (1 - x, 1 - y)

        def blk(a, p):
            return outs[a].at[4 * p[0] + 2 * p[1] + p[2]]

        def copy(a, k, block, to, src=None):
            return pltpu.make_async_remote_copy(
                src_ref=blk(a, block) if src is None else src, dst_ref=blk(a, block),
                send_sem=send_sems.at[7 * a + k], recv_sem=recv_sems.at[7 * a + k],
                device_id=to, device_id_type=MESH)

        mine = [pltpu.make_async_copy(ins[a], blk(a, me), local_sems.at[a]) for a in range(n)]
        for m in mine:
            m.start()
        sends = []
        for a in range(n):
            sends += [copy(a, 0, me, sibling, src=ins[a]), copy(a, 1, me, (*near, c), src=ins[a]),
                      copy(a, 2, me, (*far, c), src=ins[a])]
        for cp in sends:
            cp.start()
        for a in range(n):
            copy(a, 1, (*near, c), me).wait_recv()
            sends.append(copy(a, 3, (*near, c), (*far, c)))
            sends[-1].start()
        for a in range(n):
            sends.append(copy(a, 4, (*near, c), sibling))
            sends[-1].start()
            copy(a, 2, (*far, c), me).wait_recv()
            sends.append(copy(a, 5, (*far, c), sibling))
            sends[-1].start()
        for a in range(n):
            copy(a, 3, (*diag, c), me).wait_recv()
            sends.append(copy(a, 6, (*diag, c), sibling))
            sends[-1].start()
        for a in range(n):
            copy(a, 0, sibling, me).wait_recv()
            copy(a, 4, (*far, 1 - c), me).wait_recv()
            copy(a, 5, (*near, 1 - c), me).wait_recv()
            copy(a, 6, (*diag, 1 - c), me).wait_recv()
        for cp in sends:
            cp.wait_send()
        for m in mine:
            m.wait()

    out_shape = [jax.ShapeDtypeStruct((NDEV,) + a.shape, a.dtype) for a in arrs]
    return _pc(body, name=name, out_shape=out_shape, in_specs=[ANY] * n, out_specs=[ANY] * n,
               scratch=[pltpu.SemaphoreType.DMA((7 * n,)), pltpu.SemaphoreType.DMA((7 * n,)),
                        pltpu.SemaphoreType.DMA((n,))])(*arrs)


def _gather_first(arrs):
    n = len(arrs)

    def parts(ins, outs, ss, rs, base):
        x, y, c = lax.axis_index("x"), lax.axis_index("y"), lax.axis_index("c")
        me, sibling = (x, y, c), (x, y, 1 - c)
        chips = [(1 - x, y), (x, 1 - y), (1 - x, 1 - y)]

        def blk(a, p):
            return outs[a].at[4 * p[0] + 2 * p[1] + p[2]]

        def copy(a, k, block, to):
            return pltpu.make_async_remote_copy(
                src_ref=ins[a], dst_ref=blk(a, block), send_sem=ss.at[base + 4 * a + k],
                recv_sem=rs.at[base + 4 * a + k], device_id=to, device_id_type=MESH)

        local = [pltpu.make_async_copy(ins[a], blk(a, me), ss.at[base + 4 * n + a]) for a in range(n)]
        sends, recvs = [], []
        for a in range(n):
            sends.append(copy(a, 0, me, sibling))
            recvs.append(copy(a, 0, sibling, me))
            for j, chip in enumerate(chips):
                sends.append(copy(a, 1 + j, me, (*chip, c)))
                recvs.append(copy(a, 1 + j, (*chip, c), me))
        return local, sends, recvs

    def start(ins, outs, ss, rs, base):
        local, sends, _ = parts(ins, outs, ss, rs, base)
        for cp in local + sends:
            cp.start()

    def finish(ins, outs, ss, rs, base):
        local, sends, recvs = parts(ins, outs, ss, rs, base)
        for cp in recvs:
            cp.wait_recv()
        for cp in sends:
            cp.wait_send()
        for cp in local:
            cp.wait()

    return _Hosted(arrs, [jax.ShapeDtypeStruct((NDEV,) + a.shape, a.dtype) for a in arrs], 5 * n, start, finish)


def _gather_second(bufs):
    n = len(bufs)

    def parts(ins, outs, ss, rs, base):
        x, y, c = lax.axis_index("x"), lax.axis_index("y"), lax.axis_index("c")
        sibling = (x, y, 1 - c)
        chips = [(1 - x, y), (x, 1 - y), (1 - x, 1 - y)]
        sends, recvs = [], []
        for a in range(n):
            for j, chip in enumerate(chips):
                mine = 4 * chip[0] + 2 * chip[1] + c
                theirs = 4 * chip[0] + 2 * chip[1] + (1 - c)
                sends.append(pltpu.make_async_remote_copy(
                    src_ref=ins[a].at[mine], dst_ref=outs[a].at[mine], send_sem=ss.at[base + 3 * a + j],
                    recv_sem=rs.at[base + 3 * a + j], device_id=sibling, device_id_type=MESH))
                recvs.append(pltpu.make_async_remote_copy(
                    src_ref=ins[a].at[theirs], dst_ref=outs[a].at[theirs], send_sem=ss.at[base + 3 * a + j],
                    recv_sem=rs.at[base + 3 * a + j], device_id=sibling, device_id_type=MESH))
        return sends, recvs

    def start(ins, outs, ss, rs, base):
        for cp in parts(ins, outs, ss, rs, base)[0]:
            cp.start()

    def finish(ins, outs, ss, rs, base):
        sends, recvs = parts(ins, outs, ss, rs, base)
        for cp in recvs:
            cp.wait_recv()
        for cp in sends:
            cp.wait_send()

    return _Hosted(bufs, [jax.ShapeDtypeStruct(b.shape, b.dtype) for b in bufs], 3 * n, start, finish,
                   aliases={a: a for a in range(n)})


def _swap(src, nblk, ids_fn, partner_fn):
    def copies(ins, outs, ss, rs, base):
        x, y, c = lax.axis_index("x"), lax.axis_index("y"), lax.axis_index("c")
        ids = ids_fn(x, y, c)
        partner = partner_fn(x, y, c)
        return [pltpu.make_async_remote_copy(
            src_ref=ins[0].at[ids[k]], dst_ref=outs[0].at[k], send_sem=ss.at[base + k], recv_sem=rs.at[base + k],
            device_id=partner, device_id_type=MESH) for k in range(nblk)]

    def start(ins, outs, ss, rs, base):
        for cp in copies(ins, outs, ss, rs, base):
            cp.start()

    def finish(ins, outs, ss, rs, base):
        for cp in copies(ins, outs, ss, rs, base):
            cp.wait()

    return _Hosted([src], [jax.ShapeDtypeStruct((nblk,) + src.shape[1:], src.dtype)], nblk, start, finish)


def _swap_chips(send):
    def copies(ins, outs, ss, rs, base):
        x, y, c = lax.axis_index("x"), lax.axis_index("y"), lax.axis_index("c")
        chips = [(1 - x, y), (x, 1 - y), (1 - x, 1 - y)]
        return [pltpu.make_async_remote_copy(
            src_ref=ins[0].at[j], dst_ref=outs[0].at[j], send_sem=ss.at[base + j], recv_sem=rs.at[base + j],
            device_id=(*chip, c), device_id_type=MESH) for j, chip in enumerate(chips)]

    def start(ins, outs, ss, rs, base):
        for cp in copies(ins, outs, ss, rs, base):
            cp.start()

    def finish(ins, outs, ss, rs, base):
        for cp in copies(ins, outs, ss, rs, base):
            cp.wait()

    return _Hosted([send], [jax.ShapeDtypeStruct(send.shape, send.dtype)], 3, start, finish)


def _add_send(a, b, idx, ns, name):
    _, r, c = a.shape
    tr = _tile(r, 256)

    def body(idx_ref, a_ref, b_ref, send_ref):
        send_ref[...] = (a_ref[...] + b_ref[...]).astype(BF16)

    def sel(off):
        return pl.BlockSpec((None, tr, c), lambda k, i, s: (s[off + k], i, 0))

    gs = pltpu.PrefetchScalarGridSpec(num_scalar_prefetch=1, grid=(ns, r // tr), in_specs=[sel(0), sel(ns)],
                                      out_specs=pl.BlockSpec((None, tr, c), lambda k, i, s: (k, i, 0)))
    return _pc(body, name=name, grid_spec=gs, sem=("arbitrary", "arbitrary"),
               out_shape=jax.ShapeDtypeStruct((ns, r, c), BF16))(idx, a, b)


class _ReduceScatter:
    def __init__(self, g, tag):
        self.g, self.tag = g, tag

    def swap_core(self):
        return _swap(self.g, 4, lambda x, y, c: [1 - c, 3 - c, 5 - c, 7 - c], lambda x, y, c: (x, y, 1 - c))

    def after_core(self, recv):
        x, y, c = lax.axis_index("x"), lax.axis_index("y"), lax.axis_index("c")
        chips = [(1 - x, y), (x, 1 - y), (1 - x, 1 - y)]
        idx = jnp.stack([4 * p + 2 * q + c for p, q in chips] + [2 * p + q for p, q in chips]).astype(jnp.int32)
        self.send = _add_send(self.g, recv, idx, 3, "rs_add_" + self.tag)
        self.recv_core = recv
        zero = jnp.zeros((), jnp.int32)
        self.idx = jnp.stack([4 * x + 2 * y + c, 2 * x + y, zero, zero + 1, zero + 2]).astype(jnp.int32)

    def swap_chips(self):
        return _swap_chips(self.send)

    def after_chips(self, recv):
        self.parts = [self.g, self.recv_core, recv, recv, recv]


def _ada_fwd(c_all, w_ada, b_cols, b_lb):
    nl, d, ncol = w_ada.shape
    nseq = c_all.shape[0]
    di = b_lb.shape[1]

    def body(c_ref, w_ref, b_ref, lb_ref, mod_ref, lbj_ref):
        cv = c_ref[...]
        cact = (cv * _sigmoid(cv)).astype(BF16)
        for l in range(nl):
            mod_ref[l] = _dot(cact, w_ref[l].astype(BF16)) + b_ref[l]
        b0, b1 = lb_ref[0:1, :], lb_ref[1:2, :]
        mx = jnp.maximum(b0, b1)
        e0, e1 = jnp.exp(b0 - mx), jnp.exp(b1 - mx)
        s = e0 + e1
        p0, p1 = e0 / s, e1 / s
        lbj_ref[0:1, :] = (p0 + p1) - p0
        lbj_ref[1:2, :] = p0 * p1

    return _pc(body, name="ada_fwd",
               out_shape=[jax.ShapeDtypeStruct((nl, nseq, ncol), F32), jax.ShapeDtypeStruct((2, di), F32)]
               )(c_all, w_ada, b_cols, b_lb)


def _ada_bwd(c_all, dmod_cols, dmod_full):
    nl, nseq, ncol = dmod_cols.shape
    d = c_all.shape[1]
    d3 = dmod_full.shape[2]

    def body(c_ref, dc_ref, df_ref, gw_ref, gb_ref):
        cv = c_ref[...]
        cact = (cv * _sigmoid(cv)).astype(BF16)
        for l in range(nl):
            gw_ref[l] = _dot_tn(cact, dc_ref[l].astype(BF16))
            gb_ref[l:l + 1, :] = jnp.sum(df_ref[l], axis=0, keepdims=True)

    return _pc(body, name="ada_bwd",
               out_shape=[jax.ShapeDtypeStruct((nl, d, ncol), F32), jax.ShapeDtypeStruct((nl, d3), F32)]
               )(c_all, dmod_cols, dmod_full)


def _prenorm(x, gain, mod, t_seq, name):
    m, d = x.shape
    tm = _tile(t_seq, 1024)
    per = t_seq // tm

    def body(x_ref, g_ref, mod_ref, h_ref, ht_ref):
        xv = x_ref[...]
        rstd = lax.rsqrt(jnp.mean(xv * xv, axis=-1, keepdims=True) + EPS)
        r = xv * rstd * g_ref[...]
        h = r * (1.0 + mod_ref[0, 1:2, :]) + mod_ref[0, 0:1, :]
        h_ref[...] = h.astype(BF16)
        ht_ref[...] = h.T.astype(BF16)

    return _pc(body, name=name, out_shape=[jax.ShapeDtypeStruct((m, d), BF16), jax.ShapeDtypeStruct((d, m), BF16)],
               grid=(m // tm,),
               in_specs=[pl.BlockSpec((tm, d), lambda i: (i, 0)), pl.BlockSpec((1, d), lambda i: (0, 0)),
                         pl.BlockSpec((1, 3, d), lambda i: (i // per, 0, 0))],
               out_specs=[pl.BlockSpec((tm, d), lambda i: (i, 0)), pl.BlockSpec((d, tm), lambda i: (0, i))],
               sem=("parallel",))(x, gain, mod)


def _prenorm_bwd(dh, x, gain, mod, dxn, t_seq, name):
    m, d = x.shape
    nb = m // t_seq
    tm = _tile(t_seq, 1024)
    per = t_seq // tm

    def body(dh_ref, x_ref, g_ref, mod_ref, dxn_ref, dx_ref, dss_ref, dg_ref):
        i = pl.program_id(0)
        xv, dhv, g = x_ref[...], dh_ref[...], g_ref[...]
        rstd = lax.rsqrt(jnp.mean(xv * xv, axis=-1, keepdims=True) + EPS)
        xhat = xv * rstd
        dr = dhv * (1.0 + mod_ref[0, 1:2, :])
        dxhat = dr * g
        dx_ref[...] = dxn_ref[...] + rstd * (dxhat - xhat * jnp.mean(dxhat * xhat, axis=-1, keepdims=True))

        @pl.when(i % per == 0)
        def _():
            dss_ref[...] = jnp.zeros_like(dss_ref)

        @pl.when(i == 0)
        def _():
            dg_ref[...] = jnp.zeros_like(dg_ref)

        dss_ref[0, 0:1, :] += jnp.sum(dhv, axis=0, keepdims=True)
        dss_ref[0, 1:2, :] += jnp.sum(dhv * (xhat * g), axis=0, keepdims=True)
        dg_ref[...] += jnp.sum(dr * xhat, axis=0, keepdims=True)

    row = pl.BlockSpec((tm, d), lambda i: (i, 0))
    return _pc(body, name=name,
               out_shape=[jax.ShapeDtypeStruct((m, d), F32), jax.ShapeDtypeStruct((nb, 2, d), F32),
                          jax.ShapeDtypeStruct((1, d), F32)],
               grid=(m // tm,),
               in_specs=[row, row, pl.BlockSpec((1, d), lambda i: (0, 0)),
                         pl.BlockSpec((1, 3, d), lambda i: (i // per, 0, 0)), row],
               out_specs=[row, pl.BlockSpec((1, 2, d), lambda i: (i // per, 0, 0)),
                          pl.BlockSpec((1, d), lambda i: (0, 0))],
               sem=("arbitrary",))(dh, x, gain, mod, dxn)


def _mm_in(h, ws, sections, name, comm=None):
    m, k = h.shape
    nw = len(ws)
    widths = [w.shape[2] for w in ws]
    offs = [sum(widths[:a]) for a in range(nw)]
    nc = sum(widths)
    per = NDEV // sections if sections > 1 else NDEV
    tm = _din_tile(m)
    assert per % 2 == 0

    def body(*refs):
        hv = refs[0][...]
        o_ref = refs[1 + nw]
        for b in range(2):
            for a in range(nw):
                lo = b * nc + offs[a]
                o_ref[:, lo:lo + widths[a]] = _dot(hv, refs[1 + a][b])

    w_specs = [pl.BlockSpec((2, k, wd), lambda j, i: (j, 0, 0)) for wd in widths]
    if sections > 1:
        out_shape = jax.ShapeDtypeStruct((sections, m, per * nc), F32)
        out_spec = pl.BlockSpec((None, tm, 2 * nc), lambda j, i: ((2 * j) // per, i, ((2 * j) % per) // 2))
    else:
        out_shape = jax.ShapeDtypeStruct((m, NDEV * nc), F32)
        out_spec = pl.BlockSpec((tm, 2 * nc), lambda j, i: (i, j))
    return _pc(body, name=name, out_shape=out_shape, grid=(NDEV // 2, m // tm),
               in_specs=[pl.BlockSpec((tm, k), lambda j, i: (i, 0))] + w_specs,
               out_specs=out_spec, sem=("parallel", "parallel"), comm=comm)(h, *ws)


def _din_tile(m):
    return 1024 if m % 1024 == 0 and m >= 2048 else _tile(m, 512)


def _mm_din(dproj, ws, sections, name, comm=None, tiles=None, prev=None):
    nw, k = len(ws), ws[0].shape[1]
    widths = [w.shape[2] for w in ws]
    offs = [sum(widths[:a]) for a in range(nw)]
    nc = sum(widths)
    m = dproj.shape[-2]
    tm = _din_tile(m)
    t0, nt = tiles if tiles is not None else (0, m // tm)
    per = NDEV // sections if sections > 1 else NDEV
    assert per % 2 == 0

    def body(*refs):
        d_ref, o_ref = refs[0], refs[-1]
        j = pl.program_id(1)
        acc = None
        for b in range(2):
            for a in range(nw):
                lo = b * nc + offs[a]
                term = _dot_nt(d_ref[:, lo:lo + widths[a]], refs[1 + a][b])
                acc = term if acc is None else acc + term

        @pl.when(j == 0)
        def _():
            o_ref[...] = acc

        @pl.when(j > 0)
        def _():
            o_ref[...] += acc

    if sections > 1:
        dspec = pl.BlockSpec((None, tm, 2 * nc), lambda i, j: ((2 * j) // per, i + t0, ((2 * j) % per) // 2))
    else:
        dspec = pl.BlockSpec((tm, 2 * nc), lambda i, j: (i + t0, j))
    in_specs = [dspec] + [pl.BlockSpec((2, k, wd), lambda i, j: (j, 0, 0)) for wd in widths]
    args = [dproj, *ws]
    if prev is not None:
        in_specs.append(ANY)
        args.append(prev)
    return _pc(body, name=name, out_shape=jax.ShapeDtypeStruct((m, k), F32), grid=(nt, NDEV // 2), in_specs=in_specs,
               out_specs=pl.BlockSpec((tm, k), lambda i, j: (i + t0, 0)), sem=("parallel", "arbitrary"),
               comm=comm, aliases={1 + nw: 0} if prev is not None else None)(*args)


def _mm_dw_in(ht, dproj, nc, sections, name, comm=None):
    k, m = ht.shape
    per = NDEV // sections if sections > 1 else NDEV

    def body(h_ref, d_ref, o_ref):
        o_ref[...] = _dot(h_ref[...], d_ref[...])

    if sections > 1:
        dspec = pl.BlockSpec((None, m, nc), lambda j: (j // per, 0, j % per))
    else:
        dspec = pl.BlockSpec((m, nc), lambda j: (0, j))
    return _pc(body, name=name, out_shape=jax.ShapeDtypeStruct((NDEV, k, nc), F32), grid=(NDEV,),
               in_specs=[pl.BlockSpec((k, m), lambda j: (0, 0)), dspec],
               out_specs=pl.BlockSpec((None, k, nc), lambda j: (j, 0, 0)),
               sem=("parallel",), comm=comm)(ht, dproj)


def _out_proj(ybr, w_out, x, mod, t_seq, name, comm=None):
    m, di = ybr.shape
    d = w_out.shape[1]
    tm = _tile(t_seq, 512)
    per = t_seq // tm

    def body(y_ref, w_ref, x_ref, mod_ref, yo_ref, xn_ref):
        yo = _dot(y_ref[...], w_ref[...])
        yo_ref[...] = yo
        xn_ref[...] = x_ref[...] + mod_ref[0, 2:3, :] * yo

    row = pl.BlockSpec((tm, d), lambda i: (i, 0))
    return _pc(body, name=name,
               out_shape=[jax.ShapeDtypeStruct((m, d), F32), jax.ShapeDtypeStruct((m, d), F32)],
               grid=(m // tm,),
               in_specs=[pl.BlockSpec((tm, di), lambda i: (i, 0)), pl.BlockSpec((di, d), lambda i: (0, 0)), row,
                         pl.BlockSpec((1, 3, d), lambda i: (i // per, 0, 0))],
               out_specs=[row, row], sem=("parallel",), comm=comm)(ybr, w_out, x, mod)


def _out_proj_loss(ybr, w_out, x, mod, gain, target, t_seq):
    m, di = ybr.shape
    d = w_out.shape[1]
    tm = _tile(t_seq, 512)
    per = t_seq // tm

    def body(y_ref, w_ref, x_ref, mod_ref, g_ref, t_ref, yo_ref, dx_ref, loss_ref, dg_ref):
        i = pl.program_id(0)
        yo = _dot(y_ref[...], w_ref[...])
        yo_ref[...] = yo
        xv = x_ref[...] + mod_ref[0, 2:3, :] * yo
        g = g_ref[...]
        rstd = lax.rsqrt(jnp.mean(xv * xv, axis=-1, keepdims=True) + EPS)
        xhat = xv * rstd
        err = xhat * g - t_ref[...]
        dy = err * (1.0 / d)
        dxhat = dy * g
        dx_ref[...] = rstd * (dxhat - xhat * jnp.mean(dxhat * xhat, axis=-1, keepdims=True))

        @pl.when(i == 0)
        def _():
            loss_ref[...] = jnp.zeros_like(loss_ref)
            dg_ref[...] = jnp.zeros_like(dg_ref)

        loss_ref[...] += 0.5 * jnp.sum(jnp.mean(err * err, axis=-1, keepdims=True), axis=0, keepdims=True)
        dg_ref[...] += jnp.sum(dy * xhat, axis=0, keepdims=True)

    row = pl.BlockSpec((tm, d), lambda i: (i, 0))
    vec = pl.BlockSpec((1, d), lambda i: (0, 0))
    return _pc(body, name="out_proj_loss",
               out_shape=[jax.ShapeDtypeStruct((m, d), F32), jax.ShapeDtypeStruct((m, d), F32),
                          jax.ShapeDtypeStruct((1, 1), F32), jax.ShapeDtypeStruct((1, d), F32)],
               grid=(m // tm,),
               in_specs=[pl.BlockSpec((tm, di), lambda i: (i, 0)), pl.BlockSpec((di, d), lambda i: (0, 0)), row,
                         pl.BlockSpec((1, 3, d), lambda i: (i // per, 0, 0)), vec, row],
               out_specs=[row, row, pl.BlockSpec((1, 1), lambda i: (0, 0)), vec],
               sem=("arbitrary",))(ybr, w_out, x, mod, gain, target)


def _gate_dybr(dxn, yout, mod, w_out, t_seq, name):
    m, d = dxn.shape
    di = w_out.shape[0]
    nb = m // t_seq
    tm = _tile(t_seq, 1024)
    per = t_seq // tm

    def body(dxn_ref, yo_ref, mod_ref, w_ref, dy_ref, dgate_ref, o_ref):
        i = pl.program_id(0)
        dv = dxn_ref[...]
        dy = (mod_ref[0, 2:3, :] * dv).astype(BF16)
        dy_ref[...] = dy
        o_ref[...] = _dot_nt(dy, w_ref[...])

        @pl.when(i % per == 0)
        def _():
            dgate_ref[...] = jnp.zeros_like(dgate_ref)

        dgate_ref[0] += jnp.sum(dv * yo_ref[...], axis=0, keepdims=True)

    row = pl.BlockSpec((tm, d), lambda i: (i, 0))
    return _pc(body, name=name,
               out_shape=[jax.ShapeDtypeStruct((m, d), BF16), jax.ShapeDtypeStruct((nb, 1, d), F32),
                          jax.ShapeDtypeStruct((m, di), F32)],
               grid=(m // tm,),
               in_specs=[row, row, pl.BlockSpec((1, 3, d), lambda i: (i // per, 0, 0)),
                         pl.BlockSpec((di, d), lambda i: (0, 0))],
               out_specs=[row, pl.BlockSpec((1, 1, d), lambda i: (i // per, 0, 0)),
                          pl.BlockSpec((tm, di), lambda i: (i, 0))],
               sem=("arbitrary",))(dxn, yout, mod, w_out)


def _mm_dw_out(ybr, dy, name, comm=None):
    m, di = ybr.shape
    d = dy.shape[1]
    tn = _tile(di, 512)

    def body(y_ref, dy_ref, o_ref):
        o_ref[...] = _dot_tn(y_ref[...], dy_ref[...])

    return _pc(body, name=name, out_shape=jax.ShapeDtypeStruct((di, d), F32), grid=(di // tn,),
               in_specs=[pl.BlockSpec((m, tn), lambda n: (0, n)), pl.BlockSpec((m, d), lambda n: (0, 0))],
               out_specs=pl.BlockSpec((tn, d), lambda n: (n, 0)), sem=("parallel",), comm=comm)(ybr, dy)


def _sgu_mask():
    t = lax.broadcasted_iota(jnp.int32, (SG_BLOCK, SG_BLOCK), 0)
    s = lax.broadcasted_iota(jnp.int32, (SG_BLOCK, SG_BLOCK), 1)
    return (s // CHUNK) <= (t // CHUNK)


def _a_mid_fwd(proj, ln_g, ln_b, w_s, bs_t, t_seq, comm=None):
    m, n3 = proj.shape
    di = n3 // 3
    gd = di // SG_GROUPS
    r = _tile(t_seq, 256)
    nblk = r // SG_BLOCK

    def body(p_ref, lg_ref, lb_ref, ws_ref, bs_ref, ybr_ref, s_scr):
        v = _gelu(p_ref[:, di:2 * di])
        mu = jnp.mean(v, axis=-1, keepdims=True)
        vc = v - mu
        rstd = lax.rsqrt(jnp.mean(vc * vc, axis=-1, keepdims=True) + EPS)
        vb = (vc * rstd * lg_ref[...] + lb_ref[...]).astype(BF16)
        mask = _sgu_mask()
        for gi in range(SG_GROUPS):
            ws = jnp.where(mask, ws_ref[gi], 0.0).astype(BF16)
            bcol = bs_ref[:, gi:gi + 1]
            for b in range(nblk):
                rows = slice(b * SG_BLOCK, (b + 1) * SG_BLOCK)
                cols = slice(gi * gd, (gi + 1) * gd)
                s_scr[rows, cols] = _dot(ws, vb[rows, cols]) + bcol
        gg = p_ref[:, 2 * di:]
        ybr_ref[...] = (_gelu(p_ref[:, :di]) * s_scr[...] * (gg * _sigmoid(gg))).astype(BF16)

    vec = pl.BlockSpec((1, di), lambda i: (0, 0))
    return _pc(body, name="a_mid_fwd", out_shape=jax.ShapeDtypeStruct((m, di), BF16), grid=(m // r,),
               in_specs=[pl.BlockSpec((r, n3), lambda i: (i, 0)), vec, vec,
                         pl.BlockSpec((SG_GROUPS, SG_BLOCK, SG_BLOCK), lambda i: (0, 0, 0)),
                         pl.BlockSpec((SG_BLOCK, 128), lambda i: (0, 0))],
               out_specs=pl.BlockSpec((r, di), lambda i: (i, 0)),
               scratch=[pltpu.VMEM((r, di), F32)], sem=("parallel",), comm=comm)(proj, ln_g, ln_b, w_s, bs_t)


def _a_mid_bwd(proj, dybr, ln_g, ln_b, w_s, bs_t, t_seq, comm=None):
    m, n3 = proj.shape
    di = n3 // 3
    gd = di // SG_GROUPS
    r = _tile(t_seq, 256)
    nblk = r // SG_BLOCK

    def body(p_ref, dy_ref, lg_ref, lb_ref, ws_ref, bs_ref,
             dp_ref, dlg_ref, dlb_ref, dws_ref, dbs_ref, s_scr, dvl_scr):
        i = pl.program_id(0)

        @pl.when(i == 0)
        def _():
            dlg_ref[...] = jnp.zeros_like(dlg_ref)
            dlb_ref[...] = jnp.zeros_like(dlb_ref)
            dws_ref[...] = jnp.zeros_like(dws_ref)
            dbs_ref[...] = jnp.zeros_like(dbs_ref)

        v, dgelu_v = _gelu_and_grad(p_ref[:, di:2 * di])
        mu = jnp.mean(v, axis=-1, keepdims=True)
        vc = v - mu
        rstd = lax.rsqrt(jnp.mean(vc * vc, axis=-1, keepdims=True) + EPS)
        vhat = vc * rstd
        lg = lg_ref[...]
        vb = (vhat * lg + lb_ref[...]).astype(BF16)
        u, dgelu_u = _gelu_and_grad(p_ref[:, :di])
        gg = p_ref[:, 2 * di:]
        sg = _sigmoid(gg)
        dyv = dy_ref[...]
        dus = dyv * (gg * sg)
        dsb = (dus * u).astype(BF16)
        ds32 = dus * u
        mask = _sgu_mask()
        lane = lax.broadcasted_iota(jnp.int32, (SG_BLOCK, 128), 1)
        dbs_acc = jnp.zeros((SG_BLOCK, 128), F32)
        for gi in range(SG_GROUPS):
            ws = jnp.where(mask, ws_ref[gi], 0.0).astype(BF16)
            bcol = bs_ref[:, gi:gi + 1]
            cols = slice(gi * gd, (gi + 1) * gd)
            dws_acc = jnp.zeros((SG_BLOCK, SG_BLOCK), F32)
            dbs_col = jnp.zeros((SG_BLOCK, 1), F32)
            for b in range(nblk):
                rows = slice(b * SG_BLOCK, (b + 1) * SG_BLOCK)
                s_scr[rows, cols] = _dot(ws, vb[rows, cols]) + bcol
                dvl_scr[rows, cols] = _dot_tn(ws, dsb[rows, cols])
                dws_acc += _dot_nt(dsb[rows, cols], vb[rows, cols])
                dbs_col += jnp.sum(ds32[rows, cols], axis=-1, keepdims=True)
            dws_ref[gi] += jnp.where(mask, dws_acc, 0.0)
            dbs_acc += jnp.where(lane == gi, dbs_col, 0.0)
        dbs_ref[...] += dbs_acc
        s = s_scr[...]
        dp_ref[:, :di] = (dus * s * dgelu_u).astype(BF16)
        dp_ref[:, 2 * di:] = (dyv * u * s * (sg * (1.0 + gg * (1.0 - sg)))).astype(BF16)
        dvl = dvl_scr[...]
        dlg_ref[...] += jnp.sum(dvl * vhat, axis=0, keepdims=True)
        dlb_ref[...] += jnp.sum(dvl, axis=0, keepdims=True)
        dvh = dvl * lg
        dv = rstd * (dvh - jnp.mean(dvh, axis=-1, keepdims=True)
                     - vhat * jnp.mean(dvh * vhat, axis=-1, keepdims=True))
        dp_ref[:, di:2 * di] = (dv * dgelu_v).astype(BF16)

    vec = pl.BlockSpec((1, di), lambda i: (0, 0))
    wsb = pl.BlockSpec((SG_GROUPS, SG_BLOCK, SG_BLOCK), lambda i: (0, 0, 0))
    bsb = pl.BlockSpec((SG_BLOCK, 128), lambda i: (0, 0))
    return _pc(body, name="a_mid_bwd",
               out_shape=[jax.ShapeDtypeStruct((m, n3), BF16), jax.ShapeDtypeStruct((1, di), F32),
                          jax.ShapeDtypeStruct((1, di), F32),
                          jax.ShapeDtypeStruct((SG_GROUPS, SG_BLOCK, SG_BLOCK), F32),
                          jax.ShapeDtypeStruct((SG_BLOCK, 128), F32)],
               grid=(m // r,),
               in_specs=[pl.BlockSpec((r, n3), lambda i: (i, 0)), pl.BlockSpec((r, di), lambda i: (i, 0)),
                         vec, vec, wsb, bsb],
               out_specs=[pl.BlockSpec((r, n3), lambda i: (i, 0)), vec, vec, wsb, bsb],
               scratch=[pltpu.VMEM((r, di), F32), pltpu.VMEM((r, di), F32)],
               sem=("arbitrary",), comm=comm)(proj, dybr, ln_g, ln_b, w_s, bs_t)


def _chunk_rows(n):
    if isinstance(n, int):
        return pl.ds(n * CHUNK, CHUNK)
    return pl.ds(pl.multiple_of(n * CHUNK, CHUNK), CHUNK)


def _hgrn_dims(t_seq, di):
    tr = _tile(t_seq, 128)
    hc = _tile(di, 2048)
    return tr, hc, hc // HEAD_DIM


def _hgrn_gates(f_ref, lb, a_scr, k_scr, tr):
    sig = _sigmoid(f_ref[...])
    fg = lb + (1.0 - lb) * sig
    k_scr[...] = 1.0 - fg
    logf = jnp.log(fg)
    g = min(CUM_ROWS, tr)
    tri = _tri_mask(g, reverse=False)
    for rg in range(tr // g):
        a_scr[rg * g:(rg + 1) * g, :] = _tri_apply(tri, logf[rg * g:(rg + 1) * g, :])
    return sig, fg


def _hgrn_fwd(proj, lbj, gn, nb, t_seq):
    _, m, di = proj.shape
    tr, hc, hpg = _hgrn_dims(t_seq, di)
    nt, nhg, ncl = t_seq // tr, di // hc, tr // CHUNK
    nheads = di // HEAD_DIM

    nbuf, nsteps = 3, nhg * nb * nt

    def body(p_hbm, lb_ref, gn_ref, o_ref, ybr_ref, st_ref, st_scr, a_scr, k_scr, buf, sems):
        t = pl.program_id(2)
        step = (pl.program_id(0) * nb + pl.program_id(1)) * nt + t

        def fetch(s):
            rt, hg = s % (nb * nt), s // (nb * nt)
            return pltpu.make_async_copy(p_hbm.at[:, pl.ds(rt * tr, tr), pl.ds(hg * hc, hc)],
                                         buf.at[s % nbuf], sems.at[s % nbuf])

        @pl.when(step == 0)
        def _():
            for s in range(min(nbuf - 1, nsteps)):
                fetch(s).start()

        @pl.when(step + nbuf - 1 < nsteps)
        def _():
            fetch(step + nbuf - 1).start()

        fetch(step).wait()
        p_ref = buf.at[step % nbuf]
        q_ref, f_ref, i_ref, g_ref = (p_ref.at[s] for s in range(4))

        @pl.when(t == 0)
        def _():
            st_scr[...] = jnp.zeros_like(st_scr)

        _hgrn_gates(f_ref, lb_ref[0:1, :], a_scr, k_scr, tr)
        gnv = gn_ref[...]
        rr = lax.broadcasted_iota(jnp.int32, (CHUNK, CHUNK), 0)
        cc = lax.broadcasted_iota(jnp.int32, (CHUNK, CHUNK), 1)
        causal = cc <= rr

        def chunk(n, carry):
            rows = _chunk_rows(n)
            lanes = [slice(hd * HEAD_DIM, (hd + 1) * HEAD_DIM) for hd in range(hpg)]
            hs = []
            for hd, ls in enumerate(lanes):
                h = {}
                ah, kh = a_scr[rows, ls], k_scr[rows, ls]
                qp = q_ref[rows, ls]
                qh = qp * _sigmoid(qp)
                h["vb"] = i_ref[rows, ls].astype(BF16)
                aref, alast = ah[CHUNK // 2 - 1:CHUNK // 2, :], ah[CHUNK - 1:CHUNK, :]
                h["q_in"] = (qh * jnp.exp(ah - aref)).astype(BF16)
                h["k_in"] = (kh * jnp.exp(aref - ah)).astype(BF16)
                h["q_out"] = (qh * jnp.exp(ah)).astype(BF16)
                h["k_out"] = (kh * jnp.exp(alast - ah)).astype(BF16)
                h["dec"] = jnp.exp(alast)
                st = st_scr[hd]
                st_ref[n, hd] = st
                h["st"] = st
                hs.append(h)
            for h in hs:
                h["scores"] = _dot_nt(h["q_in"], h["k_in"])
                h["o_inter"] = _dot_nt(h["q_out"], h["st"].astype(BF16))
                h["st_mm"] = _dot_tn(h["vb"], h["k_out"])
            for h in hs:
                h["o"] = _dot(jnp.where(causal, h["scores"], 0.0).astype(BF16), h["vb"]) + h["o_inter"]
            for hd, (h, ls) in enumerate(zip(hs, lanes)):
                st_scr[hd] = h["st"] * h["dec"] + h["st_mm"]
                o = h["o"]
                o_ref[rows, ls] = o
                rstd = lax.rsqrt(jnp.mean(o * o, axis=-1, keepdims=True) + EPS)
                gg = g_ref[rows, ls]
                ybr_ref[rows, ls] = ((o * rstd * gnv) * (gg * _sigmoid(gg))).astype(BF16)
            return carry

        lax.fori_loop(0, ncl, chunk, 0)

    blk = pl.BlockSpec((tr, hc), lambda hg, b, t: (b * nt + t, hg))
    return _pc(body, name="hgrn_fwd",
               out_shape=[jax.ShapeDtypeStruct((m, di), F32), jax.ShapeDtypeStruct((m, di), BF16),
                          jax.ShapeDtypeStruct((m // CHUNK, nheads, HEAD_DIM, HEAD_DIM), F32)],
               grid=(nhg, nb, nt),
               in_specs=[ANY, pl.BlockSpec((2, hc), lambda hg, b, t: (0, hg)),
                         pl.BlockSpec((1, HEAD_DIM), lambda hg, b, t: (0, 0))],
               out_specs=[blk, blk, pl.BlockSpec((ncl, hpg, HEAD_DIM, HEAD_DIM),
                                                 lambda hg, b, t: (b * nt + t, hg, 0, 0))],
               scratch=[pltpu.VMEM((hpg, HEAD_DIM, HEAD_DIM), F32), pltpu.VMEM((tr, hc), F32),
                        pltpu.VMEM((tr, hc), F32), pltpu.VMEM((nbuf, 4, tr, hc), F32),
                        pltpu.SemaphoreType.DMA((nbuf,))],
               sem=("arbitrary", "arbitrary", "arbitrary"))(proj, lbj, gn)


def _hgrn_bwd(proj, o_all, dybr, states, lbj, gn, nb, t_seq, comm=None):
    _, m, di = proj.shape
    tr, hc, hpg = _hgrn_dims(t_seq, di)
    nt, nhg, ncl = t_seq // tr, di // hc, tr // CHUNK

    nbuf, nsteps = 3, nhg * nb * nt

    def body(p_hbm, o_ref, dy_ref, st_ref, lb_ref, gn_ref,
             dp_ref, dlb_ref, dgn_ref, dst_scr, a_scr, k_scr, da_scr, dk_scr, pbuf, sems):
        hg, b, t = pl.program_id(0), pl.program_id(1), pl.program_id(2)
        step = (hg * nb + b) * nt + t

        def fetch(s):
            rt, cg = (s // nt) % nb * nt + (nt - 1 - s % nt), s // (nb * nt)
            return pltpu.make_async_copy(p_hbm.at[:, pl.ds(rt * tr, tr), pl.ds(cg * hc, hc)],
                                         pbuf.at[s % nbuf], sems.at[s % nbuf])

        @pl.when(step == 0)
        def _():
            for s in range(min(nbuf - 1, nsteps)):
                fetch(s).start()

        @pl.when(step + nbuf - 1 < nsteps)
        def _():
            fetch(step + nbuf - 1).start()

        fetch(step).wait()
        p_ref = pbuf.at[step % nbuf]
        q_ref, f_ref, i_ref, g_ref = (p_ref.at[s] for s in range(4))

        @pl.when(t == 0)
        def _():
            dst_scr[...] = jnp.zeros_like(dst_scr)

        @pl.when((b == 0) & (t == 0))
        def _():
            dlb_ref[...] = jnp.zeros_like(dlb_ref)

        @pl.when((hg == 0) & (b == 0) & (t == 0))
        def _():
            dgn_ref[...] = jnp.zeros_like(dgn_ref)

        lb = lb_ref[0:1, :]
        sig, fg = _hgrn_gates(f_ref, lb, a_scr, k_scr, tr)
        gnv = gn_ref[...]
        rr = lax.broadcasted_iota(jnp.int32, (CHUNK, CHUNK), 0)
        cc = lax.broadcasted_iota(jnp.int32, (CHUNK, CHUNK), 1)
        causal = cc <= rr
        rowi = lax.broadcasted_iota(jnp.int32, (CHUNK, HEAD_DIM), 0)

        def chunk(it, carry):
            n = ncl - 1 - it
            rows = _chunk_rows(n)
            for hd0 in range(0, hpg, PHASE_HEADS):
                heads(n, rows, range(hd0, min(hpg, hd0 + PHASE_HEADS)))
            return carry

        def heads(n, rows, ids):
            lanes = [slice(hd * HEAD_DIM, (hd + 1) * HEAD_DIM) for hd in ids]
            hs = []
            for hd, ls in zip(ids, lanes):
                h = {}
                ah, kh = a_scr[rows, ls], k_scr[rows, ls]
                qp = q_ref[rows, ls]
                sq = _sigmoid(qp)
                qh = qp * sq
                h["dsilu_q"] = sq * (1.0 + qp * (1.0 - sq))
                h["vb"] = i_ref[rows, ls].astype(BF16)
                aref, alast = ah[CHUNK // 2 - 1:CHUNK // 2, :], ah[CHUNK - 1:CHUNK, :]
                h["e1"], h["e2"] = jnp.exp(ah - aref), jnp.exp(aref - ah)
                h["e3"], h["e4"] = jnp.exp(ah), jnp.exp(alast - ah)
                h["dec"] = jnp.exp(alast)
                h["q_in"], h["k_in"], h["q_out"], h["k_out"] = qh * h["e1"], kh * h["e2"], qh * h["e3"], kh * h["e4"]
                for nm in ("q_in", "k_in", "q_out", "k_out"):
                    h[nm + "_b"] = h[nm].astype(BF16)
                o = o_ref[rows, ls]
                rstd = lax.rsqrt(jnp.mean(o * o, axis=-1, keepdims=True) + EPS)
                ohat = o * rstd
                gg = g_ref[rows, ls]
                sg = _sigmoid(gg)
                dyv = dy_ref[rows, ls]
                d_on = dyv * (gg * sg)
                dp_ref[3, rows, ls] = (dyv * (ohat * gnv) * (sg * (1.0 + gg * (1.0 - sg)))).astype(BF16)
                h["dgn"] = jnp.sum(d_on * ohat, axis=0, keepdims=True)
                dohat = d_on * gnv
                do = rstd * (dohat - ohat * jnp.mean(dohat * ohat, axis=-1, keepdims=True))
                h["do_b"] = do.astype(BF16)
                h["st_prev"] = st_ref[n, hd]
                h["dst"] = dst_scr[hd]
                hs.append(h)
            for h in hs:
                dst_b = h["dst"].astype(BF16)
                h["scores"] = _dot_nt(h["q_in_b"], h["k_in_b"])
                h["dscores"] = _dot_nt(h["do_b"], h["vb"])
                h["dv_inter"] = _dot_nt(h["k_out_b"], dst_b)
                h["dq_out"] = _dot(h["do_b"], h["st_prev"].astype(BF16))
                h["dk_out"] = _dot(h["vb"], dst_b)
                h["dst_mm"] = _dot_tn(h["do_b"], h["q_out_b"])
            for h in hs:
                scores = jnp.where(causal, h["scores"], 0.0).astype(BF16)
                dscores = jnp.where(causal, h["dscores"], 0.0).astype(BF16)
                h["dv"] = _dot_tn(scores, h["do_b"]) + h["dv_inter"]
                h["dq_in"] = _dot(dscores, h["k_in_b"])
                h["dk_in"] = _dot_tn(dscores, h["q_in_b"])
            dgn = hs[0]["dgn"]
            for h in hs[1:]:
                dgn = dgn + h["dgn"]
            dgn_ref[...] += dgn
            for hd, h, ls in zip(ids, hs, lanes):
                ddec = jnp.sum(h["dst"] * h["st_prev"], axis=0, keepdims=True)
                dst_scr[hd] = h["dst"] * h["dec"] + h["dst_mm"]
                dp_ref[2, rows, ls] = h["dv"].astype(BF16)
                dq = h["dq_in"] * h["e1"] + h["dq_out"] * h["e3"]
                dp_ref[0, rows, ls] = (dq * h["dsilu_q"]).astype(BF16)
                dk_scr[rows, ls] = h["dk_in"] * h["e2"] + h["dk_out"] * h["e4"]
                t_in = h["dq_in"] * h["q_in"] - h["dk_in"] * h["k_in"]
                t_out = h["dk_out"] * h["k_out"]
                da = t_in + h["dq_out"] * h["q_out"] - t_out
                da_ref_row = -jnp.sum(t_in, axis=0, keepdims=True)
                da_last_row = jnp.sum(t_out, axis=0, keepdims=True) + ddec * h["dec"]
                da = da + jnp.where(rowi == CHUNK // 2 - 1, da_ref_row, 0.0) \
                        + jnp.where(rowi == CHUNK - 1, da_last_row, 0.0)
                da_scr[rows, ls] = da

        if ncl <= 2:
            for it in range(ncl):
                chunk(it, 0)
        else:
            lax.fori_loop(0, ncl, chunk, 0)
        g = min(CUM_ROWS, tr)
        tri = _tri_mask(g, reverse=True)
        for rg in range(tr // g):
            rs = slice(rg * g, (rg + 1) * g)
            dlogf = _tri_apply(tri, da_scr[rs, :])
            df = dlogf / fg[rs, :] - dk_scr[rs, :]
            sgr = sig[rs, :]
            dp_ref[1, rs, :] = (df * (1.0 - lb) * (sgr * (1.0 - sgr))).astype(BF16)
            dlb_ref[...] += jnp.sum(df * (1.0 - sgr), axis=0, keepdims=True) * lb_ref[1:2, :]

    blk = pl.BlockSpec((tr, hc), lambda hg, b, t: (b * nt + (nt - 1 - t), hg))
    return _pc(body, name="hgrn_bwd",
               out_shape=[jax.ShapeDtypeStruct((4, m, di), BF16), jax.ShapeDtypeStruct((1, di), F32),
                          jax.ShapeDtypeStruct((1, HEAD_DIM), F32)],
               grid=(nhg, nb, nt),
               in_specs=[ANY, blk, blk,
                         pl.BlockSpec((ncl, hpg, HEAD_DIM, HEAD_DIM),
                                      lambda hg, b, t: (b * nt + (nt - 1 - t), hg, 0, 0)),
                         pl.BlockSpec((2, hc), lambda hg, b, t: (0, hg)),
                         pl.BlockSpec((1, HEAD_DIM), lambda hg, b, t: (0, 0))],
               out_specs=[pl.BlockSpec((4, tr, hc), lambda hg, b, t: (0, b * nt + (nt - 1 - t), hg)),
                          pl.BlockSpec((1, hc), lambda hg, b, t: (0, hg)),
                          pl.BlockSpec((1, HEAD_DIM), lambda hg, b, t: (0, 0))],
               scratch=[pltpu.VMEM((hpg, HEAD_DIM, HEAD_DIM), F32)] + [pltpu.VMEM((tr, hc), F32)] * 4 + [
                   pltpu.VMEM((nbuf, 4, tr, hc), F32), pltpu.SemaphoreType.DMA((nbuf,))],
               sem=("arbitrary", "arbitrary", "arbitrary"), comm=comm)(
                   proj, o_all, dybr, states, lbj, gn)


def _adamw(parts, w, m, v, name):
    r, c = w.shape
    tr = _tile(r, 256)
    npart = len(parts)
    c1 = 1.0 - ADAM_B1 ** ADAM_STEP
    c2 = 1.0 - ADAM_B2 ** ADAM_STEP

    def body(*refs):
        p_refs = refs[:npart]
        _adamw_math(p_refs, *refs[npart:], c1, c2)

    blk = pl.BlockSpec((tr, c), lambda i: (i, 0))
    return _pc(body, name=name, out_shape=[jax.ShapeDtypeStruct((r, c), F32)] * 4, grid=(r // tr,),
               in_specs=[blk] * (npart + 3), out_specs=[blk] * 4, sem=("parallel",))(*parts, w, m, v)


def _adamw_math(p_refs, w_ref, m_ref, v_ref, g_ref, d_ref, nm_ref, nv_ref, c1, c2):
    g = p_refs[0][...].astype(F32)
    for p in p_refs[1:]:
        g = g + p[...].astype(F32)
    nm = ADAM_B1 * m_ref[...] + (1.0 - ADAM_B1) * g
    nv = ADAM_B2 * v_ref[...] + (1.0 - ADAM_B2) * (g * g)
    g_ref[...] = g
    nm_ref[...] = nm
    nv_ref[...] = nv
    d_ref[...] = -ADAM_LR * ((nm / c1) / (jnp.sqrt(nv / c2) + ADAM_EPS) + ADAM_WD * w_ref[...])


def _adamw_small(gathered, ws, ms, vs, name, sums=()):
    n, ns = len(ws), len(sums)
    c1 = 1.0 - ADAM_B1 ** ADAM_STEP
    c2 = 1.0 - ADAM_B2 ** ADAM_STEP

    def total(ref):
        g = ref[0]
        for part in range(1, ref.shape[0]):
            g = g + ref[part]
        return g

    def body(*refs):
        g_in, w_in, m_in, v_in = refs[:n], refs[n:2 * n], refs[2 * n:3 * n], refs[3 * n:4 * n]
        s_in = refs[4 * n:4 * n + ns]
        outs = refs[4 * n + ns:]
        for k in range(n):
            g = total(g_in[k])
            nm = ADAM_B1 * m_in[k][...] + (1.0 - ADAM_B1) * g
            nv = ADAM_B2 * v_in[k][...] + (1.0 - ADAM_B2) * (g * g)
            outs[4 * k][...] = g
            outs[4 * k + 1][...] = -ADAM_LR * ((nm / c1) / (jnp.sqrt(nv / c2) + ADAM_EPS) + ADAM_WD * w_in[k][...])
            outs[4 * k + 2][...] = nm
            outs[4 * k + 3][...] = nv
        for k in range(ns):
            outs[4 * n + k][...] = total(s_in[k])

    out_shape = [jax.ShapeDtypeStruct(w.shape, F32) for w in ws for _ in range(4)]
    out_shape += [jax.ShapeDtypeStruct(s.shape[1:], F32) for s in sums]
    res = _pc(body, name=name, out_shape=out_shape)(*gathered, *ws, *ms, *vs, *sums)
    return [res[4 * k:4 * k + 4] for k in range(n)] + list(res[4 * n:])


def _adamw_blocks(parts, idx, w, m, v, name):
    r, c = w.shape
    tr = _tile(r, 256)
    npart = len(parts)
    c1 = 1.0 - ADAM_B1 ** ADAM_STEP
    c2 = 1.0 - ADAM_B2 ** ADAM_STEP

    def body(idx_ref, *refs):
        _adamw_math(refs[:npart], *refs[npart:], c1, c2)

    def sel(p):
        return pl.BlockSpec((None, tr, c), lambda i, s: (s[p], i, 0))

    blk = pl.BlockSpec((tr, c), lambda i, s: (i, 0))
    gs = pltpu.PrefetchScalarGridSpec(num_scalar_prefetch=1, grid=(r // tr,),
                                      in_specs=[sel(p) for p in range(npart)] + [blk] * 3, out_specs=[blk] * 4)
    return _pc(body, name=name, out_shape=[jax.ShapeDtypeStruct((r, c), F32)] * 4, grid_spec=gs,
               sem=("parallel",))(idx, *parts, w, m, v)


_EARLY = ["a_ln_gain", "a_ln_bias", "a_w_s", "a_b_s", "b_lower_bounds", "b_gn_gain"]


def kernel(x, c, norm_gain, w_ada, b_ada, a_w_in, a_ln_gain, a_ln_bias, a_w_s, a_b_s, a_w_out, b_w_in, b_lower_bounds, b_gn_gain, b_w_out, final_gain, loss_target, m_norm_gain, m_w_ada, m_b_ada, m_a_w_in, m_a_ln_gain, m_a_ln_bias, m_a_w_s, m_a_b_s, m_a_w_out, m_b_w_in, m_b_lower_bounds, m_b_gn_gain, m_b_w_out, m_final_gain, v_norm_gain, v_w_ada, v_b_ada, v_a_w_in, v_a_ln_gain, v_a_ln_bias, v_a_w_s, v_a_b_s, v_a_w_out, v_b_w_in, v_b_lower_bounds, v_b_gn_gain, v_b_w_out, v_final_gain):
    w = dict(norm_gain=norm_gain, w_ada=w_ada, b_ada=b_ada, a_w_in=a_w_in, a_ln_gain=a_ln_gain,
             a_ln_bias=a_ln_bias, a_w_s=a_w_s, a_b_s=a_b_s, a_w_out=a_w_out, b_w_in=b_w_in,
             b_lower_bounds=b_lower_bounds, b_gn_gain=b_gn_gain, b_w_out=b_w_out, final_gain=final_gain)
    mo = dict(norm_gain=m_norm_gain, w_ada=m_w_ada, b_ada=m_b_ada, a_w_in=m_a_w_in, a_ln_gain=m_a_ln_gain,
              a_ln_bias=m_a_ln_bias, a_w_s=m_a_w_s, a_b_s=m_a_b_s, a_w_out=m_a_w_out, b_w_in=m_b_w_in,
              b_lower_bounds=m_b_lower_bounds, b_gn_gain=m_b_gn_gain, b_w_out=m_b_w_out, final_gain=m_final_gain)
    vo = dict(norm_gain=v_norm_gain, w_ada=v_w_ada, b_ada=v_b_ada, a_w_in=v_a_w_in, a_ln_gain=v_a_ln_gain,
              a_ln_bias=v_a_ln_bias, a_w_s=v_a_w_s, a_b_s=v_a_b_s, a_w_out=v_a_w_out, b_w_in=v_b_w_in,
              b_lower_bounds=v_b_lower_bounds, b_gn_gain=v_b_gn_gain, b_w_out=v_b_w_out, final_gain=v_final_gain)

    nb, t_seq, d = x.shape
    m = nb * t_seq
    ncol_ada = w_ada.shape[2]
    xi, yi, ci = lax.axis_index("x"), lax.axis_index("y"), lax.axis_index("c")
    me = 4 * xi + 2 * yi + ci

    c_g, wa_in_g = _all_gather([c, a_w_in[0].astype(BF16)], "gather_c_wa")

    c_all = c_g.reshape(NDEV * nb, d)
    b_cols = lax.dynamic_slice(b_ada, (0, me * ncol_ada), (2, ncol_ada)).reshape(2, 1, ncol_ada)
    mod_part, lbj = _ada_fwd(c_all, w_ada, b_cols, b_lower_bounds)
    mod_all = _all_gather([mod_part], "gather_mod")[0]
    mod_mine = lax.dynamic_slice_in_dim(mod_all, me * nb, nb, axis=2)
    mod_mine = mod_mine.transpose(1, 2, 0, 3).reshape(2, nb, 3, d)
    mod0, mod1 = mod_mine[0], mod_mine[1]

    di = a_w_out.shape[1] * NDEV

    xf = x.reshape(m, d)
    tgt = loss_target.reshape(m, d)
    ng0, ng1 = norm_gain[0:1], norm_gain[1:2]
    ncb = b_w_in.shape[2]
    wb_lo, wb_hi = b_w_in[0][:, :ncb // 2].astype(BF16), b_w_in[0][:, ncb // 2:].astype(BF16)
    h0, h0_t = _prenorm(xf, ng0, mod0, t_seq, "prenorm_a")
    proj_a, half = _mm_in(h0, [wa_in_g], 1, "in_proj_a", comm=_gather_first([a_w_out[0].astype(BF16), wb_lo]))
    bs_t = jnp.pad(a_b_s[0].T, ((0, 0), (0, 128 - SG_GROUPS)))
    ybr_a, (wa_out_g, wb_lo_g, wb_hi_half) = _a_mid_fwd(
        proj_a, a_ln_gain, a_ln_bias, a_w_s[0], bs_t, t_seq, comm=_join(_gather_second(half), _gather_first([wb_hi])))
    wa_out = wa_out_g.reshape(di, d)
    (yout_a, x1), (wb_hi_g, wb_out_half) = _out_proj(
        ybr_a, wa_out, xf, mod0, t_seq, "out_proj_a",
        comm=_join(_gather_second([wb_hi_half]), _gather_first([b_w_out[0].astype(BF16)])))
    wb_in_g = [wb_lo_g, wb_hi_g]
    h1, h1_t = _prenorm(x1, ng1, mod1, t_seq, "prenorm_b")
    proj_b, (wb_out_g,) = _mm_in(h1, wb_in_g, 4, "in_proj_b", comm=_gather_second([wb_out_half]))
    wb_out = wb_out_g.reshape(di, d)
    o_b, ybr_b, states = _hgrn_fwd(proj_b, lbj, b_gn_gain, nb, t_seq)
    yout_b, dx2, loss_part, d_final_gain = _out_proj_loss(ybr_b, wb_out, x1, mod1, final_gain.reshape(1, d), tgt, t_seq)

    rows_out = a_w_out.shape[1]
    dy_b, dgate1, dybr_b = _gate_dybr(dx2, yout_b, mod1, wb_out, t_seq, "dybr_b")
    rs_wb_out = _ReduceScatter(_mm_dw_out(ybr_b, dy_b, "dw_out_b").reshape(NDEV, rows_out, d), "b_w_out")
    (dproj_b, d_lb, d_gn), got = _hgrn_bwd(proj_b, o_b, dybr_b, states, lbj, b_gn_gain, nb, t_seq,
                                           comm=rs_wb_out.swap_core())
    rs_wb_out.after_core(got[0])
    dh1, got = _mm_din(dproj_b, wb_in_g, 4, "dh_b", comm=rs_wb_out.swap_chips())
    rs_wb_out.after_chips(got[0])
    dx1, dss1, dgain1 = _prenorm_bwd(dh1, x1, ng1, mod1, dx2, t_seq, "prenorm_bwd_b")
    rs_wb_in = _ReduceScatter(_mm_dw_in(h1_t, dproj_b, ncb, 4, "dw_in_b"), "b_w_in")

    dy_a, dgate0, dybr_a = _gate_dybr(dx1, yout_a, mod0, wa_out, t_seq, "dybr_a")
    g_wa_out, got = _mm_dw_out(ybr_a, dy_a, "dw_out_a", comm=rs_wb_in.swap_core())
    rs_wb_in.after_core(got[0])
    rs_wa_out = _ReduceScatter(g_wa_out.reshape(NDEV, rows_out, d), "a_w_out")
    (dproj_a, d_lng, d_lnb, d_ws, d_bs_t), got = _a_mid_bwd(
        proj_a, dybr_a, a_ln_gain, a_ln_bias, a_w_s[0], bs_t, t_seq,
        comm=_join(rs_wb_in.swap_chips(), rs_wa_out.swap_core()))
    rs_wb_in.after_chips(got[0])
    rs_wa_out.after_core(got[1])
    early_parts = [d_lng, d_lnb, d_ws.reshape(SG_GROUPS * SG_BLOCK, SG_BLOCK), d_bs_t[:, :SG_GROUPS].T,
                   jnp.concatenate([-d_lb, d_lb], axis=0), d_gn]
    g_wa_in, got = _mm_dw_in(h0_t, dproj_a, wa_in_g.shape[2], 1, "dw_in_a",
                             comm=_join(rs_wa_out.swap_chips(), _gather_first(early_parts)))
    rs_wa_out.after_chips(got[0])
    rs_wa_in = _ReduceScatter(g_wa_in, "a_w_in")
    n_tiles = m // _din_tile(m)
    assert n_tiles >= 2
    first_tiles = max(1, (3 * n_tiles) // 8)
    dh0, got2 = _mm_din(dproj_a, [wa_in_g], 1, "dh_a_first", tiles=(0, first_tiles),
                        comm=_join(rs_wa_in.swap_core(), _gather_second(got[1:])))
    rs_wa_in.after_core(got2[0])
    early_all = got2[1:]
    dh0, got = _mm_din(dproj_a, [wa_in_g], 1, "dh_a_rest", comm=rs_wa_in.swap_chips(),
                       tiles=(first_tiles, n_tiles - first_tiles), prev=dh0)
    rs_wa_in.after_chips(got[0])
    dx0, dss0, dgain0 = _prenorm_bwd(dh0, xf, ng0, mod0, dx1, t_seq, "prenorm_bwd_a")
    grad_x = dx0.reshape(nb, t_seq, d)

    dmod = jnp.stack([jnp.concatenate([dss0, dgate0], axis=1), jnp.concatenate([dss1, dgate1], axis=1)])
    dmod_all, dgain_all, dfinal_all, loss_all = _all_gather(
        [dmod.reshape(2, nb, 3 * d), jnp.concatenate([dgain0, dgain1], axis=0), d_final_gain,
         jnp.broadcast_to(loss_part, (1, 128))], "gather_tail")
    dmod_all = dmod_all.transpose(1, 0, 2, 3).reshape(2, NDEV * nb, 3 * d)
    dmod_cols = lax.dynamic_slice_in_dim(dmod_all, me * ncol_ada, ncol_ada, axis=2)
    g_w_ada, g_b_ada = _ada_bwd(c_all, dmod_cols, dmod_all)

    def small2d(k, t):
        return t[k].reshape(early_parts[_EARLY.index(k)].shape if k in _EARLY else (-1, t[k].shape[-1]))

    res = {}
    sm = _adamw_small(early_all, *[[small2d(k, t) for k in _EARLY] for t in (w, mo, vo)], "adamw_small_early")
    for k, r in zip(_EARLY, sm):
        res[k] = tuple(z.reshape(w[k].shape) for z in r)
    late = ["norm_gain", "final_gain", "b_ada"]
    sm = _adamw_small([dgain_all, dfinal_all, g_b_ada[None]], *[[small2d(k, t) for k in late] for t in (w, mo, vo)],
                      "adamw_small_late", sums=[loss_all])
    for k, r in zip(late, sm):
        res[k] = tuple(z.reshape(w[k].shape) for z in r)
    loss = sm[3][0, 0]
    sh = w_ada.shape
    ra = _adamw([g_w_ada.reshape(sh[0] * sh[1], sh[2])], w_ada.reshape(sh[0] * sh[1], sh[2]),
                mo["w_ada"].reshape(sh[0] * sh[1], sh[2]), vo["w_ada"].reshape(sh[0] * sh[1], sh[2]), "adamw_w_ada")
    res["w_ada"] = tuple(z.reshape(sh) for z in ra)

    for k, rs in (("b_w_out", rs_wb_out), ("b_w_in", rs_wb_in), ("a_w_out", rs_wa_out), ("a_w_in", rs_wa_in)):
        res[k] = tuple(z[None] for z in _adamw_blocks(rs.parts, rs.idx, w[k][0], mo[k][0], vo[k][0], "adamw_" + k))

    order = ["norm_gain", "w_ada", "b_ada", "a_w_in", "a_ln_gain", "a_ln_bias", "a_w_s", "a_b_s", "a_w_out",
             "b_w_in", "b_lower_bounds", "b_gn_gain", "b_w_out", "final_gain"]
    return (loss, grad_x, *[res[k][0] for k in order], *[res[k][1] for k in order],
            *[res[k][2] for k in order], *[res[k][3] for k in order])
```
